```python
import jax, jax.numpy as jnp
from jax import lax
import numpy as np

D_MODEL = 1024
BATCH = 8
SEQ = 2048
DEPTH = 1

GRID_W = 64
N_META = 16
EPS = 1e-6

NA_HEADS = 8
NA_HEAD_DIM = 64
NA_WIDTH = NA_HEADS * NA_HEAD_DIM
NA_WIN_H = 8
NA_WIN_W = 16

HG_HEADS = 4
HG_DK = 128
HG_DV = 128
HG_KDIM = HG_HEADS * HG_DK
HG_VDIM = HG_HEADS * HG_DV
HG_CHUNK = 16

D_FF = 4 * D_MODEL

IN_SPLIT = (NA_WIDTH, NA_WIDTH, NA_WIDTH,
            HG_KDIM, HG_KDIM, HG_KDIM, HG_VDIM, HG_VDIM,
            D_MODEL, D_MODEL)
IN_COLS = sum(IN_SPLIT)

kernel_name = "hybrid_natten_hgrn2_griffin_block"


def rms_norm(x, g):
    xf = x.astype(jnp.float32)
    y = xf * lax.rsqrt(jnp.mean(xf * xf, axis=-1, keepdims=True) + EPS)
    return (y * g.astype(jnp.float32)).astype(x.dtype)


def split_cols(a):
    outs, off = [], 0
    for n in IN_SPLIT:
        outs.append(a[..., off:off + n])
        off += n
    return outs


def neighbourhood_attention(q, k, v, rpb):
    B, L, H, dh = q.shape
    T = L - N_META
    rows = T // GRID_W
    kh = min(NA_WIN_H, rows)
    scale = dh ** -0.5
    qm, km, vm = q[:, :N_META], k[:, :N_META], v[:, :N_META]
    qg = q[:, N_META:].reshape(B, rows, GRID_W, H, dh)
    kg = k[:, N_META:].reshape(B, rows, GRID_W, H, dh)
    vg = v[:, N_META:].reshape(B, rows, GRID_W, H, dh)

    r = jnp.arange(rows)
    row_start = jnp.clip(r - kh // 2, 0, rows - kh)
    row_idx = row_start[:, None] + jnp.arange(kh)[None, :]
    k_blk = kg[:, row_idx]
    v_blk = vg[:, row_idx]

    s_win = jnp.einsum('brchd,brjwhd->bhrcjw', qg, k_blk).astype(jnp.float32) * scale
    c = jnp.arange(GRID_W)
    col_start = jnp.clip(c - NA_WIN_W // 2, 0, GRID_W - NA_WIN_W)
    in_win = (c[None, :] >= col_start[:, None]) & (c[None, :] < col_start[:, None] + NA_WIN_W)
    dr = row_idx - r[:, None]
    dc = jnp.clip(c[None, :] - c[:, None], -(NA_WIN_W - 1), NA_WIN_W - 1)
    bias = rpb.astype(jnp.float32)[:, dr[:, None, :, None] + NA_WIN_H - 1,
                                   dc[None, :, None, :] + NA_WIN_W - 1]
    s_win = jnp.where(in_win[:, None, :], s_win + bias[None], -1e30)

    s_meta = jnp.einsum('brchd,bmhd->bhrcm', qg, km).astype(jnp.float32) * scale
    s = jnp.concatenate([s_win.reshape(B, H, rows, GRID_W, kh * GRID_W), s_meta], axis=-1)
    p = jax.nn.softmax(s, axis=-1).astype(v.dtype)
    p_win = p[..., :kh * GRID_W].reshape(B, H, rows, GRID_W, kh, GRID_W)
    p_meta = p[..., kh * GRID_W:]
    o_grid = (jnp.einsum('bhrcjw,brjwhd->brchd', p_win, v_blk)
              + jnp.einsum('bhrcm,bmhd->brchd', p_meta, vm)).reshape(B, T, H, dh)

    s_mm = jnp.einsum('bmhd,bnhd->bhmn', qm, km).astype(jnp.float32) * scale
    p_mm = jax.nn.softmax(s_mm, axis=-1).astype(v.dtype)
    o_meta = jnp.einsum('bhmn,bnhd->bmhd', p_mm, vm)
    return jnp.concatenate([o_meta, o_grid], axis=1)


def chunk_scan(q, k, v, log_f):
    B, L, H, dk = q.shape
    dv = v.shape[-1]
    n = L // HG_CHUNK

    def to_chunks(a):
        return a.reshape(B, n, HG_CHUNK, H, a.shape[-1]).transpose(1, 0, 3, 2, 4)

    tri = jnp.tril(jnp.ones((HG_CHUNK, HG_CHUNK), dtype=bool))

    def step(S, inp):
        qi, ki, vi, gi = inp
        b = jnp.cumsum(gi, axis=-2)
        o_inter = jnp.einsum('bhtk,bhkv->bhtv', qi * jnp.exp(b), S)
        diff = jnp.where(tri[:, :, None], b[..., :, None, :] - b[..., None, :, :], -jnp.inf)
        A = jnp.einsum('bhtk,bhsk,bhtsk->bhts', qi, ki, jnp.exp(diff))
        o_intra = jnp.einsum('bhts,bhsv->bhtv', A, vi)
        b_last = b[..., -1:, :]
        S_new = (jnp.exp(b_last[..., 0, :])[..., None] * S
                 + jnp.einsum('bhsk,bhsv->bhkv', ki * jnp.exp(b_last - b), vi))
        return S_new, o_inter + o_intra

    S0 = jnp.zeros((B, H, dk, dv), jnp.float32)
    _, o = lax.scan(step, S0, (to_chunks(q), to_chunks(k), to_chunks(v), to_chunks(log_f)))
    return o.transpose(1, 0, 3, 2, 4).reshape(B, L, H, dv)


def hgrn2_branch(q, z_fwd, z_bwd, i, g, lb, gain):
    B, L, _ = q.shape
    dtype = q.dtype

    def heads(a, d):
        return a.astype(jnp.float32).reshape(B, L, HG_HEADS, d)

    qh = jax.nn.silu(heads(q, HG_DK))
    vh = heads(i, HG_DV)

    def gates(z, lb_dir):
        lb_h = lb_dir.reshape(HG_HEADS, HG_DK)
        log_f = jnp.logaddexp(jnp.log(lb_h), jnp.log1p(-lb_h) + jax.nn.log_sigmoid(heads(z, HG_DK)))
        return -jnp.expm1(log_f), log_f

    k_f, lf_f = gates(z_fwd, lb[0])
    k_b, lf_b = gates(z_bwd, lb[1])
    rev = lambda a: jnp.flip(a, axis=1)
    o = chunk_scan(qh, k_f, vh, lf_f) + rev(chunk_scan(rev(qh), rev(k_b), rev(vh), rev(lf_b)))
    o = o * lax.rsqrt(jnp.mean(o * o, axis=-1, keepdims=True) + EPS)
    o = o.reshape(B, L, HG_VDIM) * gain.astype(jnp.float32) * jax.nn.silu(g.astype(jnp.float32))
    return o.astype(dtype)


def _fwd_setup_inputs(seed: int = 0) -> dict:
    key = jax.random.key(seed)
    ks = jax.random.split(key, 14)
    f32 = jnp.float32
    nrm = lambda k, shape, s: jax.random.normal(k, shape, f32) * s
    return {
        "x": nrm(ks[0], (BATCH, SEQ, D_MODEL), 1.0),
        "meta_tokens": nrm(ks[1], (N_META, D_MODEL), 1.0),
        "w_in": nrm(ks[2], (DEPTH, D_MODEL, IN_COLS), D_MODEL ** -0.5),
        "w_na_out": nrm(ks[3], (DEPTH, NA_WIDTH, D_MODEL), NA_WIDTH ** -0.5),
        "w_hg_out": nrm(ks[4], (DEPTH, HG_VDIM, D_MODEL), HG_VDIM ** -0.5),
        "w_o": nrm(ks[5], (DEPTH, D_MODEL, D_MODEL), D_MODEL ** -0.5),
        "w_up": nrm(ks[6], (DEPTH, D_MODEL, D_FF), D_MODEL ** -0.5),
        "w_down": nrm(ks[7], (DEPTH, D_FF, D_MODEL), D_FF ** -0.5),
        "norm_mix": 1.0 + nrm(ks[8], (DEPTH, D_MODEL), 0.05),
        "norm_mlp": 1.0 + nrm(ks[9], (DEPTH, D_MODEL), 0.05),
        "norm_final": 1.0 + nrm(ks[10], (D_MODEL,), 0.05),
        "hg_norm": 1.0 + nrm(ks[11], (DEPTH, HG_VDIM), 0.05),
        "na_rpb": nrm(ks[12], (DEPTH, NA_HEADS, 2 * NA_WIN_H - 1, 2 * NA_WIN_W - 1), 0.1),
        "hg_lb_logits": nrm(ks[13], (2, DEPTH + 1, HG_KDIM), 0.5),
    }


def _fwd_reference(x, meta_tokens, w_in, w_na_out, w_hg_out, w_o, w_up, w_down,
              norm_mix, norm_mlp, norm_final, hg_norm, na_rpb, hg_lb_logits):
    B = x.shape[0]
    h = jnp.concatenate([jnp.broadcast_to(meta_tokens.astype(x.dtype)[None], (B, N_META, D_MODEL)), x], axis=1)
    L = h.shape[1]
    lb_all = jnp.cumsum(jax.nn.softmax(hg_lb_logits.astype(jnp.float32), axis=1), axis=1)
    for l in range(DEPTH):
        a = rms_norm(h, norm_mix[l])
        (q_na, k_na, v_na, q_hg, z_f, z_b, i_hg, g_hg, gate_na, gate_hg) = split_cols(a @ w_in[l])
        hd = lambda t: t.reshape(B, L, NA_HEADS, NA_HEAD_DIM)
        y_na = neighbourhood_attention(hd(q_na), hd(k_na), hd(v_na), na_rpb[l]).reshape(B, L, NA_WIDTH) @ w_na_out[l]
        y_hg = hgrn2_branch(q_hg, z_f, z_b, i_hg, g_hg, lb_all[:, l], hg_norm[l]) @ w_hg_out[l]
        mix = jax.nn.sigmoid(gate_na) * y_na + jax.nn.sigmoid(gate_hg) * y_hg
        h = h + mix @ w_o[l]
        m = rms_norm(h, norm_mlp[l])
        h = h + jnp.square(jax.nn.relu(m @ w_up[l])) @ w_down[l]
    h = rms_norm(h, norm_final)
    return h[:, N_META:]


import jax as _jax
import jax.numpy as _jnp

TWIN_FORMAT = 'train_step'
FWD_PARAMS = ['x', 'meta_tokens', 'w_in', 'w_na_out', 'w_hg_out', 'w_o', 'w_up', 'w_down', 'norm_mix', 'norm_mlp', 'norm_final', 'hg_norm', 'na_rpb', 'hg_lb_logits']
TWIN_WEIGHTS = ['meta_tokens', 'w_in', 'w_na_out', 'w_hg_out', 'w_o', 'w_up', 'w_down', 'norm_mix', 'norm_mlp', 'norm_final', 'hg_norm', 'na_rpb', 'hg_lb_logits']
TWIN_DIFF_INPUT = 'x'
TWIN_INPUTS = ['x', 'meta_tokens', 'w_in', 'w_na_out', 'w_hg_out', 'w_o', 'w_up', 'w_down', 'norm_mix', 'norm_mlp', 'norm_final', 'hg_norm', 'na_rpb', 'hg_lb_logits', 'loss_target', 'm_meta_tokens', 'm_w_in', 'm_w_na_out', 'm_w_hg_out', 'm_w_o', 'm_w_up', 'm_w_down', 'm_norm_mix', 'm_norm_mlp', 'm_norm_final', 'm_hg_norm', 'm_na_rpb', 'm_hg_lb_logits', 'v_meta_tokens', 'v_w_in', 'v_w_na_out', 'v_w_hg_out', 'v_w_o', 'v_w_up', 'v_w_down', 'v_norm_mix', 'v_norm_mlp', 'v_norm_final', 'v_hg_norm', 'v_na_rpb', 'v_hg_lb_logits']
TWIN_OUTPUTS = ['loss', 'grad_x', 'grad_meta_tokens', 'grad_w_in', 'grad_w_na_out', 'grad_w_hg_out', 'grad_w_o', 'grad_w_up', 'grad_w_down', 'grad_norm_mix', 'grad_norm_mlp', 'grad_norm_final', 'grad_hg_norm', 'grad_na_rpb', 'grad_hg_lb_logits', 'delta_meta_tokens', 'delta_w_in', 'delta_w_na_out', 'delta_w_hg_out', 'delta_w_o', 'delta_w_up', 'delta_w_down', 'delta_norm_mix', 'delta_norm_mlp', 'delta_norm_final', 'delta_hg_norm', 'delta_na_rpb', 'delta_hg_lb_logits', 'new_m_meta_tokens', 'new_m_w_in', 'new_m_w_na_out', 'new_m_w_hg_out', 'new_m_w_o', 'new_m_w_up', 'new_m_w_down', 'new_m_norm_mix', 'new_m_norm_mlp', 'new_m_norm_final', 'new_m_hg_norm', 'new_m_na_rpb', 'new_m_hg_lb_logits', 'new_v_meta_tokens', 'new_v_w_in', 'new_v_w_na_out', 'new_v_w_hg_out', 'new_v_w_o', 'new_v_w_up', 'new_v_w_down', 'new_v_norm_mix', 'new_v_norm_mlp', 'new_v_norm_final', 'new_v_hg_norm', 'new_v_na_rpb', 'new_v_hg_lb_logits']
TWIN_LEAF_KINDS = {'loss': 'loss', 'grad_x': 'grad_x', 'grad_meta_tokens': 'grad_w', 'grad_w_in': 'grad_w', 'grad_w_na_out': 'grad_w', 'grad_w_hg_out': 'grad_w', 'grad_w_o': 'grad_w', 'grad_w_up': 'grad_w', 'grad_w_down': 'grad_w', 'grad_norm_mix': 'grad_w', 'grad_norm_mlp': 'grad_w', 'grad_norm_final': 'grad_w', 'grad_hg_norm': 'grad_w', 'grad_na_rpb': 'grad_w', 'grad_hg_lb_logits': 'grad_w', 'delta_meta_tokens': 'delta_w', 'delta_w_in': 'delta_w', 'delta_w_na_out': 'delta_w', 'delta_w_hg_out': 'delta_w', 'delta_w_o': 'delta_w', 'delta_w_up': 'delta_w', 'delta_w_down': 'delta_w', 'delta_norm_mix': 'delta_w', 'delta_norm_mlp': 'delta_w', 'delta_norm_final': 'delta_w', 'delta_hg_norm': 'delta_w', 'delta_na_rpb': 'delta_w', 'delta_hg_lb_logits': 'delta_w', 'new_m_meta_tokens': 'new_m', 'new_m_w_in': 'new_m', 'new_m_w_na_out': 'new_m', 'new_m_w_hg_out': 'new_m', 'new_m_w_o': 'new_m', 'new_m_w_up': 'new_m', 'new_m_w_down': 'new_m', 'new_m_norm_mix': 'new_m', 'new_m_norm_mlp': 'new_m', 'new_m_norm_final': 'new_m', 'new_m_hg_norm': 'new_m', 'new_m_na_rpb': 'new_m', 'new_m_hg_lb_logits': 'new_m', 'new_v_meta_tokens': 'new_v', 'new_v_w_in': 'new_v', 'new_v_w_na_out': 'new_v', 'new_v_w_hg_out': 'new_v', 'new_v_w_o': 'new_v', 'new_v_w_up': 'new_v', 'new_v_w_down': 'new_v', 'new_v_norm_mix': 'new_v', 'new_v_norm_mlp': 'new_v', 'new_v_norm_final': 'new_v', 'new_v_hg_norm': 'new_v', 'new_v_na_rpb': 'new_v', 'new_v_hg_lb_logits': 'new_v'}


def _forward(args):
    return _fwd_reference(*[args[k] for k in FWD_PARAMS])


def _output_shape():
    out = _jax.eval_shape(lambda: _forward(_fwd_setup_inputs(0)))
    return out.shape, out.dtype

N_MICROBATCH = 1
ADAM_LR = 0.001
ADAM_B1 = 0.9
ADAM_B2 = 0.999
ADAM_EPS = 1e-08
ADAM_WD = 0.01
ADAM_STEP = 10
PER_EXAMPLE_BATCH_AXIS = {'x': 0, 'loss_target': 0}
SHARED_INPUTS = []
_WEIGHT_DTYPES = {'meta_tokens': _jnp.float32, 'w_in': _jnp.float32, 'w_na_out': _jnp.float32, 'w_hg_out': _jnp.float32, 'w_o': _jnp.float32, 'w_up': _jnp.float32, 'w_down': _jnp.float32, 'norm_mix': _jnp.float32, 'norm_mlp': _jnp.float32, 'norm_final': _jnp.float32, 'hg_norm': _jnp.float32, 'na_rpb': _jnp.float32, 'hg_lb_logits': _jnp.float32}
MOMENT_SCALE = {'meta_tokens': 2.157747e-03, 'w_in': 2.753172e-02, 'w_na_out': 1.321391e-02, 'w_hg_out': 4.244160e-02, 'w_o': 4.430403e-02, 'w_up': 5.564172e-02, 'w_down': 1.445113e-01, 'norm_mix': 6.914054e-02, 'norm_mlp': 1.061533e-01, 'norm_final': 1.619559e+01, 'hg_norm': 6.353990e-02, 'na_rpb': 5.499314e-03, 'hg_lb_logits': 3.162602e-03}


def _to_microbatches(a, axis):
    t = _jnp.moveaxis(a, axis, 0)
    t = t.reshape((N_MICROBATCH, t.shape[0] // N_MICROBATCH) + t.shape[1:])
    return _jnp.moveaxis(t, 1, axis + 1)


def setup_inputs(seed: int = 0) -> dict:
    inp = _fwd_setup_inputs(seed)
    key = _jax.random.fold_in(_jax.random.key(seed), 7919)
    shape, _ = _output_shape()
    out = dict(inp)
    out["loss_target"] = _jax.random.normal(_jax.random.fold_in(key, 0), shape, _jnp.float32)
    for i, name in enumerate(TWIN_WEIGHTS):
        w = inp[name].astype(_jnp.float32)
        if MOMENT_SCALE is None:
            s = _jnp.sqrt(_jnp.mean(_jnp.square(w)) + 1e-30)
        else:
            s = MOMENT_SCALE[name]
        km, kv = _jax.random.split(_jax.random.fold_in(key, i + 1))
        out[name] = w
        out["m_" + name] = s * _jax.random.normal(km, w.shape, _jnp.float32)
        out["v_" + name] = (s * s) * _jax.random.uniform(kv, w.shape, _jnp.float32, 0.5, 1.5)
    if N_MICROBATCH > 1:
        for name, axis in PER_EXAMPLE_BATCH_AXIS.items():
            out[name] = _to_microbatches(out[name], axis)
    return {'x': out['x'], 'meta_tokens': out['meta_tokens'], 'w_in': out['w_in'], 'w_na_out': out['w_na_out'], 'w_hg_out': out['w_hg_out'], 'w_o': out['w_o'], 'w_up': out['w_up'], 'w_down': out['w_down'], 'norm_mix': out['norm_mix'], 'norm_mlp': out['norm_mlp'], 'norm_final': out['norm_final'], 'hg_norm': out['hg_norm'], 'na_rpb': out['na_rpb'], 'hg_lb_logits': out['hg_lb_logits'], 'loss_target': out['loss_target'], 'm_meta_tokens': out['m_meta_tokens'], 'm_w_in': out['m_w_in'], 'm_w_na_out': out['m_w_na_out'], 'm_w_hg_out': out['m_w_hg_out'], 'm_w_o': out['m_w_o'], 'm_w_up': out['m_w_up'], 'm_w_down': out['m_w_down'], 'm_norm_mix': out['m_norm_mix'], 'm_norm_mlp': out['m_norm_mlp'], 'm_norm_final': out['m_norm_final'], 'm_hg_norm': out['m_hg_norm'], 'm_na_rpb': out['m_na_rpb'], 'm_hg_lb_logits': out['m_hg_lb_logits'], 'v_meta_tokens': out['v_meta_tokens'], 'v_w_in': out['v_w_in'], 'v_w_na_out': out['v_w_na_out'], 'v_w_hg_out': out['v_w_hg_out'], 'v_w_o': out['v_w_o'], 'v_w_up': out['v_w_up'], 'v_w_down': out['v_w_down'], 'v_norm_mix': out['v_norm_mix'], 'v_norm_mlp': out['v_norm_mlp'], 'v_norm_final': out['v_norm_final'], 'v_hg_norm': out['v_hg_norm'], 'v_na_rpb': out['v_na_rpb'], 'v_hg_lb_logits': out['v_hg_lb_logits']}


def _loss(weights, diff, rest, loss_target):
    with _jax.named_scope("forward"):
        args = {**rest, TWIN_DIFF_INPUT: diff, **{k: w.astype(_WEIGHT_DTYPES[k]) for k, w in weights.items()}}
        y = _forward(args)
    with _jax.named_scope("loss_head"):
        err = _jnp.square(y.astype(_jnp.float32) - loss_target)
        return 0.5 * _jnp.sum(_jnp.mean(err, axis=-1)) if err.ndim else 0.5 * err


def _adamw(w, g, m, v):
    m = ADAM_B1 * m + (1.0 - ADAM_B1) * g
    v = ADAM_B2 * v + (1.0 - ADAM_B2) * _jnp.square(g)
    m_hat = m / (1.0 - ADAM_B1 ** ADAM_STEP)
    v_hat = v / (1.0 - ADAM_B2 ** ADAM_STEP)
    delta = -ADAM_LR * (m_hat / (_jnp.sqrt(v_hat) + ADAM_EPS) + ADAM_WD * w)
    return delta, m, v


def reference(x, meta_tokens, w_in, w_na_out, w_hg_out, w_o, w_up, w_down, norm_mix, norm_mlp, norm_final, hg_norm, na_rpb, hg_lb_logits, loss_target, m_meta_tokens, m_w_in, m_w_na_out, m_w_hg_out, m_w_o, m_w_up, m_w_down, m_norm_mix, m_norm_mlp, m_norm_final, m_hg_norm, m_na_rpb, m_hg_lb_logits, v_meta_tokens, v_w_in, v_w_na_out, v_w_hg_out, v_w_o, v_w_up, v_w_down, v_norm_mix, v_norm_mlp, v_norm_final, v_hg_norm, v_na_rpb, v_hg_lb_logits):
    given = dict(x=x, meta_tokens=meta_tokens, w_in=w_in, w_na_out=w_na_out, w_hg_out=w_hg_out, w_o=w_o, w_up=w_up, w_down=w_down, norm_mix=norm_mix, norm_mlp=norm_mlp, norm_final=norm_final, hg_norm=hg_norm, na_rpb=na_rpb, hg_lb_logits=hg_lb_logits, loss_target=loss_target, m_meta_tokens=m_meta_tokens, m_w_in=m_w_in, m_w_na_out=m_w_na_out, m_w_hg_out=m_w_hg_out, m_w_o=m_w_o, m_w_up=m_w_up, m_w_down=m_w_down, m_norm_mix=m_norm_mix, m_norm_mlp=m_norm_mlp, m_norm_final=m_norm_final, m_hg_norm=m_hg_norm, m_na_rpb=m_na_rpb, m_hg_lb_logits=m_hg_lb_logits, v_meta_tokens=v_meta_tokens, v_w_in=v_w_in, v_w_na_out=v_w_na_out, v_w_hg_out=v_w_hg_out, v_w_o=v_w_o, v_w_up=v_w_up, v_w_down=v_w_down, v_norm_mix=v_norm_mix, v_norm_mlp=v_norm_mlp, v_norm_final=v_norm_final, v_hg_norm=v_hg_norm, v_na_rpb=v_na_rpb, v_hg_lb_logits=v_hg_lb_logits)
    weights = {n: given[n] for n in TWIN_WEIGHTS}
    shared = {n: given[n] for n in SHARED_INPUTS}
    per_example = {n: given[n] for n in ['x']}
    grad_fn = _jax.value_and_grad(_loss, argnums=(0, 1))

    def one_microbatch(ex, loss_target):
        ex = dict(ex)
        diff = ex.pop(TWIN_DIFF_INPUT)
        return grad_fn(weights, diff, {**shared, **ex}, loss_target)

    if N_MICROBATCH == 1:
        loss, (grad_w, grad_x) = one_microbatch(per_example, given["loss_target"])
    else:
        def body(carry, xs):
            loss_sum, grad_sum = carry
            l_k, (gw_k, gx_k) = one_microbatch(xs[0], xs[1])
            with _jax.named_scope("update"):
                return (loss_sum + l_k, _jax.tree.map(_jnp.add, grad_sum, gw_k)), gx_k

        init = (_jnp.zeros((), _jnp.float32), _jax.tree.map(_jnp.zeros_like, weights))
        (loss, grad_w), grad_x = _jax.lax.scan(body, init, (per_example, given["loss_target"]))
    with _jax.named_scope("update"):
        delta_w, new_m, new_v = {}, {}, {}
        for n in TWIN_WEIGHTS:
            delta_w[n], new_m[n], new_v[n] = _adamw(weights[n], grad_w[n], given["m_" + n], given["v_" + n])
    return (loss, grad_x, *[grad_w[n] for n in TWIN_WEIGHTS], *[delta_w[n] for n in TWIN_WEIGHTS],
            *[new_m[n] for n in TWIN_WEIGHTS], *[new_v[n] for n in TWIN_WEIGHTS])
```

```python
import functools

import numpy as np
import jax
import jax.numpy as jnp
from jax import lax
from jax.experimental import pallas as pl
from jax.experimental.pallas import tpu as pltpu

F32 = jnp.float32
BF16 = jnp.bfloat16

D = 1024
SEQ = 2048
NM = 16
L = SEQ + NM
T = 2176
NDEV = 8
EPS = 1e-6
GRID_W = 64
ROWS = SEQ // GRID_W
NA_HEADS = 8
NA_DH = 64
NA_SCALE = NA_DH ** -0.5
HG_HEADS = 4
HG_C = 16
NCHUNK = L // HG_C
D_FF = 4096
IN_COLS = 6144
NEG = -1e30

ADAM_LR = 0.001
ADAM_B1 = 0.9
ADAM_B2 = 0.999
ADAM_EPS = 1e-08
ADAM_WD = 0.01
ADAM_STEP = 10

MESH_ID = pl.DeviceIdType.MESH
ANY = pl.BlockSpec(memory_space=pl.ANY)

NN = (((1,), (0,)), ((), ()))
NT = (((1,), (1,)), ((), ()))
TN = (((0,), (0,)), ((), ()))


def _cp(sem=None, vmem_mb=48):
    return pltpu.CompilerParams(dimension_semantics=sem, vmem_limit_bytes=vmem_mb * 1024 * 1024)


def _dot(a, b, dims=NN):
    return lax.dot_general(a, b, dims, preferred_element_type=F32)


def _sds(shape, dtype):
    return jax.ShapeDtypeStruct(shape, dtype)


def _exchange(arrs, scatter, name):
    n = len(arrs)
    out_shapes = []
    for a, sc in zip(arrs, scatter):
        out_shapes.append(_sds(a.shape if sc else (NDEV,) + a.shape, a.dtype))

    def body(*refs):
        ins, outs = refs[:n], refs[n:2 * n]
        send_sems, recv_sems, loc_sems = refs[2 * n:]
        x, y, c = lax.axis_index("x"), lax.axis_index("y"), lax.axis_index("c")
        me = 4 * x + 2 * y + c
        copies = []
        for k in range(n):
            src_me = ins[k].at[me] if scatter[k] else ins[k]
            loc = pltpu.make_async_copy(src_me, outs[k].at[me], loc_sems.at[k])
            loc.start()
            copies.append(loc)
        remote = []
        for k in range(n):
            for m in range(1, NDEV):
                px, py, pc = x ^ (m >> 2), y ^ ((m >> 1) & 1), c ^ (m & 1)
                peer = 4 * px + 2 * py + pc
                src = ins[k].at[peer] if scatter[k] else ins[k]
                cp = pltpu.make_async_remote_copy(
                    src_ref=src, dst_ref=outs[k].at[me], send_sem=send_sems.at[k, m - 1], recv_sem=recv_sems.at[k, m - 1],
                    device_id=(px, py, pc), device_id_type=MESH_ID)
                cp.start()
                remote.append(cp)
        for cp in remote:
            cp.wait_recv()
        for cp in remote:
            cp.wait_send()
        for cp in copies:
            cp.wait()

    return pl.pallas_call(
        body, name=name, out_shape=tuple(out_shapes), in_specs=[ANY] * n, out_specs=tuple([ANY] * n),
        scratch_shapes=[pltpu.SemaphoreType.DMA((n, NDEV - 1)), pltpu.SemaphoreType.DMA((n, NDEV - 1)),
                        pltpu.SemaphoreType.DMA((n,))],
    )(*arrs)


TM_E = 272


def _norm_fwd(h, g, name):
    def body(h_ref, g_ref, o_ref):
        xv = h_ref[...]
        r = lax.rsqrt(jnp.mean(xv * xv, axis=-1, keepdims=True) + EPS)
        o_ref[...] = (xv * r * g_ref[...]).astype(BF16)

    return pl.pallas_call(
        body, name=name, grid=(T // TM_E,),
        in_specs=[pl.BlockSpec((TM_E, D), lambda i: (i, 0)), pl.BlockSpec((1, D), lambda i: (0, 0))],
        out_specs=pl.BlockSpec((TM_E, D), lambda i: (i, 0)), out_shape=_sds((T, D), BF16),
        compiler_params=_cp(("parallel",)))(h, g)


def _norm_bwd(h, g, dn, dres, name):
    def body(h_ref, g_ref, dn_ref, dres_ref, dh_ref, dhb_ref, dg_ref):
        i = pl.program_id(0)
        xv = h_ref[...]
        r = lax.rsqrt(jnp.mean(xv * xv, axis=-1, keepdims=True) + EPS)
        xh = xv * r
        dnv = dn_ref[...].astype(F32)
        dxh = dnv * g_ref[...]
        dh = dres_ref[...] + r * (dxh - xh * jnp.mean(dxh * xh, axis=-1, keepdims=True))
        dh_ref[...] = dh
        dhb_ref[...] = dh.astype(BF16)
        part = jnp.sum(dnv * xh, axis=0, keepdims=True)

        @pl.when(i == 0)
        def _():
            dg_ref[...] = part

        @pl.when(i > 0)
        def _():
            dg_ref[...] += part

    blk = pl.BlockSpec((TM_E, D), lambda i: (i, 0))
    vec = pl.BlockSpec((1, D), lambda i: (0, 0))
    return pl.pallas_call(
        body, name=name, grid=(T // TM_E,), in_specs=[blk, vec, blk, blk], out_specs=(blk, blk, vec),
        out_shape=(_sds((T, D), F32), _sds((T, D), BF16), _sds((1, D), F32)),
        compiler_params=_cp(("arbitrary",)))(h, g, dn, dres)


TM_MM = 1088


def _inproj_fwd(a, w_g):
    nb = w_g.shape[2]

    def body(a_ref, w_ref, o_ref):
        o_ref[...] = _dot(a_ref[...], w_ref[0])

    return pl.pallas_call(
        body, name="inproj_fwd", grid=(T // TM_MM, NDEV),
        in_specs=[pl.BlockSpec((TM_MM, D), lambda i, j: (i, 0)), pl.BlockSpec((1, D, nb), lambda i, j: (j, 0, 0))],
        out_specs=pl.BlockSpec((TM_MM, nb), lambda i, j: (i, j)), out_shape=_sds((T, NDEV * nb), F32),
        compiler_params=_cp(("parallel", "parallel")))(a, w_g)


TM_B = 544


def _inproj_bwd(a, dp, w_g):
    nb = w_g.shape[2]
    ni = T // TM_B

    def body(a_ref, dp_ref, w_ref, dw_ref, da_ref, acc):
        j, i = pl.program_id(0), pl.program_id(1)
        av, dpv = a_ref[...], dp_ref[...]
        part = _dot(av, dpv, TN)

        @pl.when(i == 0)
        def _():
            acc[...] = part

        @pl.when(i > 0)
        def _():
            acc[...] += part

        @pl.when(i == ni - 1)
        def _():
            dw_ref[0] = acc[...].astype(BF16)

        rows = pl.ds(pl.multiple_of(i * TM_B, TM_B), TM_B)
        dav = _dot(dpv, w_ref[0], NT)

        @pl.when(j == 0)
        def _():
            da_ref[rows, :] = dav

        @pl.when(j > 0)
        def _():
            da_ref[rows, :] += dav

    return pl.pallas_call(
        body, name="inproj_bwd", grid=(NDEV, ni),
        in_specs=[pl.BlockSpec((TM_B, D), lambda j, i: (i, 0)), pl.BlockSpec((TM_B, nb), lambda j, i: (i, j)),
                  pl.BlockSpec((1, D, nb), lambda j, i: (j, 0, 0))],
        out_specs=(pl.BlockSpec((1, D, nb), lambda j, i: (j, 0, 0)), pl.BlockSpec((T, D), lambda j, i: (0, 0))),
        out_shape=(_sds((NDEV, D, nb), BF16), _sds((T, D), F32)),
        scratch_shapes=[pltpu.VMEM((D, nb), F32)],
        compiler_params=_cp(("arbitrary", "arbitrary")))(a, dp, w_g)


NA_QB = 256
NA_GROUPS = ROWS // 4
NA_UROWS = 11
NA_KW = NA_UROWS * GRID_W
NA_KU = 768


def _na_row_offset(var, i, j):
    valid = (j < 8, i <= j < i + 8, 3 <= j < NA_UROWS)[var]
    return (j - i + (7, 3, 0)[var]) if valid else None


def _na_bias_table(rpb):
    def body(r_ref, o_ref):
        row = lax.broadcasted_iota(jnp.int32, (GRID_W, 128), 0)
        lane = lax.broadcasted_iota(jnp.int32, (GRID_W, 128), 1)
        w = lane & (GRID_W - 1)
        cs = jnp.clip(row - 8, 0, GRID_W - 16)
        in_win = (w >= cs) & (w < cs + 16)
        neg = jnp.full((GRID_W, 128), NEG, F32)
        tabs = []
        for a in range(15):
            z = jnp.broadcast_to(r_ref[0, a:a + 1, :], (GRID_W, 128))
            for bit in range(6):
                sh = 1 << bit
                z = jnp.where((row & sh) != 0, jnp.roll(z, sh, axis=1), z)
            z = jnp.roll(z, 128 - 15, axis=1)
            z = jnp.where(lane < GRID_W, z, 0.0)
            z = z + jnp.roll(z, GRID_W, axis=1)
            tabs.append(jnp.where(in_win, z, NEG))
        tail = jnp.where(lane < GRID_W + NM, 0.0, NEG)
        for var in range(3):
            for i in range(4):
                for jp in range(NA_KU // 128):
                    halves = []
                    for j in (2 * jp, 2 * jp + 1):
                        a = _na_row_offset(var, i, j) if j < NA_UROWS else None
                        halves.append(tail if j >= NA_UROWS else (neg if a is None else tabs[a]))
                    o_ref[var, 0, i * 64:(i + 1) * 64, jp * 128:(jp + 1) * 128] = jnp.where(lane < GRID_W, halves[0], halves[1])

    rp = jnp.concatenate([rpb, jnp.zeros((NA_HEADS, 15, 128 - 31), F32)], axis=2)
    return pl.pallas_call(
        body, name="na_bias_table", grid=(NA_HEADS,),
        in_specs=[pl.BlockSpec((1, 15, 128), lambda h: (h, 0, 0))],
        out_specs=pl.BlockSpec((3, 1, NA_QB, NA_KU), lambda h: (0, h, 0, 0)),
        out_shape=_sds((3, NA_HEADS, NA_QB, NA_KU), F32), compiler_params=_cp(("parallel",)))(rp)


def _na_var(g):
    return jnp.where(g == 0, 0, jnp.where(g == NA_GROUPS - 1, 2, 1))


def _na_load_window(src_ref, dst, g):
    us = jnp.clip(4 * g - 4, 0, ROWS - NA_UROWS)
    kstart = pl.multiple_of(NM + GRID_W * us, 16)
    dst[0:NA_KW, :] = src_ref[pl.ds(kstart, NA_KW), :].astype(BF16)
    dst[NA_KW:NA_KW + NM, :] = src_ref[0:NM, :].astype(BF16)
    dst[NA_KW + NM:, :] = jnp.zeros((NA_KU - NA_KW - NM, 128), BF16)
    return kstart


def _na_fwd(p_act, bias_tab):
    def body(q_ref, k_ref, v_ref, b_ref, o_ref, lse_ref, ku, vu):
        g = pl.program_id(1)
        _na_load_window(k_ref, ku, g)
        _na_load_window(v_ref, vu, g)
        qstart = pl.multiple_of(NM + NA_QB * g, 16)
        q = q_ref[pl.ds(qstart, NA_QB), :]
        lane = lax.broadcasted_iota(jnp.int32, (NA_QB, 128), 1)
        o_h, lse_h = [], []
        for h in range(2):
            hm = (lane < 64) if h == 0 else (lane >= 64)
            qm = jnp.where(hm, q, 0.0).astype(BF16)
            s = _dot(qm, ku[...], NT) * NA_SCALE + b_ref[0, h]
            m = jnp.max(s, axis=-1, keepdims=True)
            p = jnp.exp(s - m)
            l = jnp.sum(p, axis=-1, keepdims=True)
            o_h.append(_dot(p.astype(BF16), vu[...]) / l)
            lse_h.append(jnp.broadcast_to(m + jnp.log(l), (NA_QB, 128)))
        o_ref[pl.ds(qstart, NA_QB), :] = jnp.where(lane < 64, o_h[0], o_h[1]).astype(BF16)
        lse_ref[0, pl.ds(qstart, NA_QB), :] = jnp.where(lane < 64, lse_h[0], lse_h[1])

        @pl.when(g == 0)
        def _():
            qm_ = q_ref[0:NM, :]
            lane_m = lax.broadcasted_iota(jnp.int32, (NM, 128), 1)
            km, vm = ku[NA_KW:NA_KW + NM, :], vu[NA_KW:NA_KW + NM, :]
            om = []
            for h in range(2):
                hm = (lane_m < 64) if h == 0 else (lane_m >= 64)
                s = _dot(jnp.where(hm, qm_, 0.0).astype(BF16), km, NT) * NA_SCALE
                p = jnp.exp(s - jnp.max(s, axis=-1, keepdims=True))
                l = jnp.sum(p, axis=-1, keepdims=True)
                om.append(_dot(p.astype(BF16), vm) / l)
            o_ref[0:NM, :] = jnp.where(lane_m < 64, om[0], om[1]).astype(BF16)
            o_ref[L:T, :] = jnp.zeros((T - L, 128), BF16)
            lse_ref[0, 0:NM, :] = jnp.zeros((NM, 128), F32)
            lse_ref[0, L:T, :] = jnp.zeros((T - L, 128), F32)

    col = lambda off: pl.BlockSpec((T, 128), lambda hp, g: (0, off + hp))
    return pl.pallas_call(
        body, name="na_fwd", grid=(4, NA_GROUPS),
        in_specs=[col(0), col(4), col(8),
                  pl.BlockSpec((1, 2, NA_QB, NA_KU), lambda hp, g: (_na_var(g), hp, 0, 0))],
        out_specs=(pl.BlockSpec((T, 128), lambda hp, g: (0, hp)), pl.BlockSpec((1, T, 128), lambda hp, g: (hp, 0, 0))),
        out_shape=(_sds((T, 512), BF16), _sds((4, T, 128), F32)),
        scratch_shapes=[pltpu.VMEM((NA_KU, 128), BF16), pltpu.VMEM((NA_KU, 128), BF16)],
        compiler_params=_cp(("parallel", "arbitrary")))(p_act, p_act, p_act, bias_tab)


def _na_bwd(p_act, do, lse, bias_tab):
    def body(q_ref, k_ref, v_ref, do_ref, lse_ref, b_ref, dq_ref, dk_ref, dv_ref, db_ref, ku, vu):
        g = pl.program_id(1)

        @pl.when(g == 0)
        def _():
            dq_ref[...] = jnp.zeros((T, 128), F32)
            dk_ref[...] = jnp.zeros((T, 128), F32)
            dv_ref[...] = jnp.zeros((T, 128), F32)

        kstart = _na_load_window(k_ref, ku, g)
        _na_load_window(v_ref, vu, g)
        qstart = pl.multiple_of(NM + NA_QB * g, 16)
        q = q_ref[pl.ds(qstart, NA_QB), :]
        dov = do_ref[pl.ds(qstart, NA_QB), :]
        lsev = lse_ref[0, pl.ds(qstart, NA_QB), :]
        lane = lax.broadcasted_iota(jnp.int32, (NA_QB, 128), 1)
        first = (g == 0) | (g == 1) | (g == NA_GROUPS - 1)
        dq_h = []
        dku = jnp.zeros((NA_KU, 128), F32)
        dvu = jnp.zeros((NA_KU, 128), F32)
        for h in range(2):
            hm = (lane < 64) if h == 0 else (lane >= 64)
            qm = jnp.where(hm, q, 0.0).astype(BF16)
            dom = jnp.where(hm, dov, 0.0).astype(BF16)
            s = _dot(qm, ku[...], NT) * NA_SCALE + b_ref[0, h]
            p = jnp.exp(s - lsev[:, 64 * h:64 * h + 1])
            dp = _dot(dom, vu[...], NT)
            delta = jnp.sum(p * dp, axis=-1, keepdims=True)
            ds = p * (dp - delta)

            @pl.when(first)
            def _():
                db_ref[0, h] = ds

            @pl.when(jnp.logical_not(first))
            def _():
                db_ref[0, h] += ds

            dsb = (ds * NA_SCALE).astype(BF16)
            dq_h.append(_dot(dsb, ku[...]))
            dku = dku + _dot(dsb, qm, TN)
            dvu = dvu + _dot(p.astype(BF16), dom, TN)
        dq_ref[pl.ds(qstart, NA_QB), :] = jnp.where(lane < 64, dq_h[0], dq_h[1])
        dk_ref[pl.ds(kstart, NA_KW), :] += dku[0:NA_KW]
        dv_ref[pl.ds(kstart, NA_KW), :] += dvu[0:NA_KW]
        dk_ref[0:NM, :] += dku[NA_KW:NA_KW + NM]
        dv_ref[0:NM, :] += dvu[NA_KW:NA_KW + NM]

        @pl.when(g == 0)
        def _():
            qm_ = q_ref[0:NM, :]
            dom_ = do_ref[0:NM, :]
            lane_m = lax.broadcasted_iota(jnp.int32, (NM, 128), 1)
            km, vm = ku[NA_KW:NA_KW + NM, :], vu[NA_KW:NA_KW + NM, :]
            dqs = []
            dkm = jnp.zeros((NM, 128), F32)
            dvm = jnp.zeros((NM, 128), F32)
            for h in range(2):
                hm = (lane_m < 64) if h == 0 else (lane_m >= 64)
                qh = jnp.where(hm, qm_, 0.0).astype(BF16)
                doh = jnp.where(hm, dom_, 0.0).astype(BF16)
                s = _dot(qh, km, NT) * NA_SCALE
                e = jnp.exp(s - jnp.max(s, axis=-1, keepdims=True))
                p = e / jnp.sum(e, axis=-1, keepdims=True)
                dp = _dot(doh, vm, NT)
                ds = p * (dp - jnp.sum(p * dp, axis=-1, keepdims=True))
                dsb = (ds * NA_SCALE).astype(BF16)
                dqs.append(_dot(dsb, km))
                dkm = dkm + _dot(dsb, qh, TN)
                dvm = dvm + _dot(p.astype(BF16), doh, TN)
            dq_ref[0:NM, :] = jnp.where(lane_m < 64, dqs[0], dqs[1])
            dk_ref[0:NM, :] += dkm
            dv_ref[0:NM, :] += dvm

    col = lambda off: pl.BlockSpec((T, 128), lambda hp, g: (0, off + hp))
    ocol = pl.BlockSpec((T, 128), lambda hp, g: (0, hp))
    bspec = pl.BlockSpec((1, 2, NA_QB, NA_KU), lambda hp, g: (_na_var(g), hp, 0, 0))
    return pl.pallas_call(
        body, name="na_bwd", grid=(4, NA_GROUPS),
        in_specs=[col(0), col(4), col(8), ocol, pl.BlockSpec((1, T, 128), lambda hp, g: (hp, 0, 0)), bspec],
        out_specs=(ocol, ocol, ocol, bspec),
        out_shape=(_sds((T, 512), F32), _sds((T, 512), F32), _sds((T, 512), F32), _sds((3, NA_HEADS, NA_QB, NA_KU), F32)),
        scratch_shapes=[pltpu.VMEM((NA_KU, 128), BF16), pltpu.VMEM((NA_KU, 128), BF16)],
        compiler_params=_cp(("parallel", "arbitrary")))(p_act, p_act, p_act, do, lse, bias_tab)


def _na_rpb_reduce(dbias):
    def body(db_ref, o_ref):
        row = lax.broadcasted_iota(jnp.int32, (GRID_W, 128), 0)
        for a in range(15):
            acc = jnp.zeros((GRID_W, GRID_W), F32)
            for var in range(3):
                for i in range(4):
                    for j in range(NA_UROWS):
                        if _na_row_offset(var, i, j) == a:
                            pair = db_ref[var, 0, i * 64:(i + 1) * 64, (j // 2) * 128:(j // 2 + 1) * 128]
                            acc = acc + pair[:, (j % 2) * 64:(j % 2 + 1) * 64]
            z = jnp.concatenate([acc, jnp.zeros((GRID_W, 128 - GRID_W), F32)], axis=1)
            for bit in range(6):
                sh = 1 << bit
                z = jnp.where((row & sh) != 0, jnp.roll(z, 128 - sh, axis=1), z)
            z = jnp.roll(z, 15, axis=1)
            o_ref[0, a:a + 1, :] = jnp.sum(z, axis=0, keepdims=True)

    return pl.pallas_call(
        body, name="na_rpb_reduce", grid=(NA_HEADS,),
        in_specs=[pl.BlockSpec((3, 1, NA_QB, NA_KU), lambda h: (0, h, 0, 0))],
        out_specs=pl.BlockSpec((1, 15, 128), lambda h: (h, 0, 0)), out_shape=_sds((NA_HEADS, 15, 128), F32),
        compiler_params=_cp(("parallel",)))(dbias)


HG_RB = 128
HG_NB = T // HG_RB
HG_SLOTS = HG_NB * 8
HI = lax.Precision.HIGHEST


def _chunk_tri(lower):
    r = lax.broadcasted_iota(jnp.int32, (HG_RB, HG_RB), 0)
    c = lax.broadcasted_iota(jnp.int32, (HG_RB, HG_RB), 1)
    same = (r // HG_C) == (c // HG_C)
    keep = (c <= r) if lower else (c >= r)
    return jnp.where(same & keep, 1.0, 0.0).astype(F32)


def _hg_gate_terms(z, lg):
    dl = lg[0:1, :] - lg[1:2, :]
    log_lb = jax.nn.log_sigmoid(dl)
    log_1mlb = jax.nn.log_sigmoid(-dl)
    yz = log_1mlb + jax.nn.log_sigmoid(z)
    log_f = jnp.logaddexp(log_lb, yz)
    snz = jax.nn.sigmoid(-z)
    k = jnp.exp(log_1mlb) * snz
    w2 = jnp.exp(yz - log_f)
    return log_f, k, snz, w2


def _hg_pre(p_act, logits):
    def body(q_ref, zf_ref, zb_ref, lg_ref, qh_ref, kf_ref, bf_ref, kb_ref, bb_ref):
        qh_ref[...] = jax.nn.silu(q_ref[...])
        lf, kf, _, _ = _hg_gate_terms(zf_ref[...], lg_ref[0])
        kf_ref[...] = kf
        bf_ref[...] = jnp.dot(_chunk_tri(True), lf, precision=HI, preferred_element_type=F32)
        lb_, kb, _, _ = _hg_gate_terms(zb_ref[...], lg_ref[1])
        kb_ref[...] = kb
        bb_ref[...] = jnp.dot(_chunk_tri(False), lb_, precision=HI, preferred_element_type=F32)

    blk = lambda c: pl.BlockSpec((HG_RB, 512), lambda i: (i, c))
    ob = pl.BlockSpec((HG_RB, 512), lambda i: (i, 0))
    return pl.pallas_call(
        body, name="hg_pre", grid=(HG_NB,),
        in_specs=[blk(3), blk(4), blk(5), pl.BlockSpec((2, 2, 512), lambda i: (0, 0, 0))],
        out_specs=(ob,) * 5, out_shape=(_sds((T, 512), F32),) * 5,
        compiler_params=_cp(("parallel",)))(p_act, p_act, p_act, logits)


def _bdot(a, b, ca, cb):
    return lax.dot_general(a.astype(BF16), b.astype(BF16), (((ca,), (cb,)), ((0,), (0,))), preferred_element_type=F32)


def _hg_scan_fwd(qh, k, b, p_act, rev):
    anchor = 0 if rev else HG_C - 1

    def body(q_ref, k_ref, b_ref, v_ref, o_ref, st_ref, dsc):
        def phase_a(blk, _):
            rows = pl.ds(pl.multiple_of(blk * HG_RB, HG_RB), HG_RB)
            b3 = b_ref[rows, :].reshape(8, HG_C, 128)
            k3 = k_ref[rows, :].reshape(8, HG_C, 128)
            v3 = v_ref[rows, :].reshape(8, HG_C, 128)
            bl = b3[:, anchor:anchor + 1, :]
            kt = k3 * jnp.exp(bl - b3)
            st_ref[0, pl.ds(pl.multiple_of(blk * 8, 8), 8)] = _bdot(v3, kt, 1, 1)
            dsc[pl.ds(pl.multiple_of(blk * 8, 8), 8), :] = jnp.exp(bl[:, 0, :])
            return 0

        lax.fori_loop(0, HG_NB, phase_a, 0)

        def phase_b(n, carry):
            c = (NCHUNK - 1 - n) if rev else n
            u = st_ref[0, c]
            st_ref[0, c] = carry
            return carry * dsc[pl.ds(c, 1), :] + u

        lax.fori_loop(0, NCHUNK, phase_b, jnp.zeros((128, 128), F32))
        for c in range(NCHUNK, HG_SLOTS):
            st_ref[0, c] = jnp.zeros((128, 128), F32)

        t_io = lax.broadcasted_iota(jnp.int32, (8, HG_C, 128), 1)
        l_io = lax.broadcasted_iota(jnp.int32, (8, HG_C, HG_C), 2)

        def phase_c(blk, _):
            rows = pl.ds(pl.multiple_of(blk * HG_RB, HG_RB), HG_RB)
            b3 = b_ref[rows, :].reshape(8, HG_C, 128)
            k3 = k_ref[rows, :].reshape(8, HG_C, 128)
            q3 = q_ref[rows, :].reshape(8, HG_C, 128)
            v3 = v_ref[rows, :].reshape(8, HG_C, 128)
            st = st_ref[0, pl.ds(pl.multiple_of(blk * 8, 8), 8)]
            o = _bdot(q3 * jnp.exp(b3), st, 2, 2)
            a = jnp.zeros((8, HG_C, HG_C), F32)
            for s in range(HG_C):
                ok = (t_io <= s) if rev else (t_io >= s)
                f = jnp.exp(jnp.where(ok, b3 - b3[:, s:s + 1, :], NEG))
                col = jnp.sum(q3 * f * k3[:, s:s + 1, :], axis=-1, keepdims=True)
                a = a + jnp.where(l_io == s, col, 0.0)
            o = o + _bdot(a, v3, 2, 1)
            o_ref[rows, :] = o.reshape(HG_RB, 128)
            return 0

        lax.fori_loop(0, HG_NB, phase_c, 0)

    col = pl.BlockSpec((T, 128), lambda h: (0, h))
    return pl.pallas_call(
        body, name="hg_scan_bwd_dir" if rev else "hg_scan_fwd_dir", grid=(HG_HEADS,),
        in_specs=[col, col, col, pl.BlockSpec((T, 128), lambda h: (0, 24 + h))],
        out_specs=(col, pl.BlockSpec((1, HG_SLOTS, 128, 128), lambda h: (h, 0, 0, 0))),
        out_shape=(_sds((T, 512), F32), _sds((HG_HEADS, HG_SLOTS, 128, 128), F32)),
        scratch_shapes=[pltpu.VMEM((HG_SLOTS, 128), F32)],
        compiler_params=_cp(("parallel",), 56))(qh, k, b, p_act)


def _hg_scan_bwd(qh, k, b, p_act, st, do, rev):
    anchor = 0 if rev else HG_C - 1

    def body(q_ref, k_ref, b_ref, v_ref, st_ref, do_ref, dq_ref, dk_ref, db_ref, dv_ref, gst, dsc, dbl):
        def phase_a(blk, _):
            rows = pl.ds(pl.multiple_of(blk * HG_RB, HG_RB), HG_RB)
            b3 = b_ref[rows, :].reshape(8, HG_C, 128)
            q3 = q_ref[rows, :].reshape(8, HG_C, 128)
            do3 = do_ref[rows, :].reshape(8, HG_C, 128)
            gst[pl.ds(pl.multiple_of(blk * 8, 8), 8)] = _bdot(do3, q3 * jnp.exp(b3), 1, 1)
            dsc[pl.ds(pl.multiple_of(blk * 8, 8), 8), :] = jnp.exp(b3[:, anchor, :])
            return 0

        lax.fori_loop(0, HG_NB, phase_a, 0)

        def phase_b(n, carry):
            c = n if rev else (NCHUNK - 1 - n)
            w = gst[c]
            gst[c] = carry
            dcv = dsc[pl.ds(c, 1), :]
            dbl[pl.ds(c, 1), :] = dcv * jnp.sum(st_ref[0, c] * carry, axis=0, keepdims=True)
            return carry * dcv + w

        lax.fori_loop(0, NCHUNK, phase_b, jnp.zeros((128, 128), F32))
        for c in range(NCHUNK, HG_SLOTS):
            gst[c] = jnp.zeros((128, 128), F32)
            dbl[c:c + 1, :] = jnp.zeros((1, 128), F32)

        t_io = lax.broadcasted_iota(jnp.int32, (8, HG_C, 128), 1)
        r_io = lax.broadcasted_iota(jnp.int32, (8, HG_C, HG_C), 1)
        l_io = lax.broadcasted_iota(jnp.int32, (8, HG_C, HG_C), 2)

        def phase_c(blk, _):
            rows = pl.ds(pl.multiple_of(blk * HG_RB, HG_RB), HG_RB)
            cs = pl.ds(pl.multiple_of(blk * 8, 8), 8)
            b3 = b_ref[rows, :].reshape(8, HG_C, 128)
            k3 = k_ref[rows, :].reshape(8, HG_C, 128)
            q3 = q_ref[rows, :].reshape(8, HG_C, 128)
            v3 = v_ref[rows, :].reshape(8, HG_C, 128)
            do3 = do_ref[rows, :].reshape(8, HG_C, 128)
            s_t = st_ref[0, cs]
            g_t = gst[cs]
            bl = b3[:, anchor:anchor + 1, :]
            ekl = jnp.exp(bl - b3)
            kt = k3 * ekl
            dqt = _bdot(do3, s_t, 2, 1)
            dkt = _bdot(v3, g_t, 2, 1)
            dv = _bdot(kt, g_t, 2, 2)
            causal = (l_io >= r_io) if rev else (l_io <= r_io)
            da = jnp.where(causal, _bdot(do3, v3, 2, 2), 0.0)
            causal_t = (l_io <= r_io) if rev else (l_io >= r_io)
            dat = jnp.where(causal_t, _bdot(v3, do3, 2, 2), 0.0)
            dq = dqt * jnp.exp(b3)
            dk = dkt * ekl
            at = jnp.zeros((8, HG_C, HG_C), F32)
            for s in range(HG_C):
                ok = (t_io <= s) if rev else (t_io >= s)
                f = jnp.exp(jnp.where(ok, b3 - b3[:, s:s + 1, :], NEG))
                dq = dq + da[:, :, s:s + 1] * (f * k3[:, s:s + 1, :])
            for t in range(HG_C):
                ok = (t_io >= t) if rev else (t_io <= t)
                e = jnp.exp(jnp.where(ok, b3[:, t:t + 1, :] - b3, NEG))
                eq = e * q3[:, t:t + 1, :]
                dk = dk + dat[:, :, t:t + 1] * eq
                at = at + jnp.where(l_io == t, jnp.sum(eq * k3, axis=-1, keepdims=True), 0.0)
            dv = dv + _bdot(at, do3, 2, 1)
            dbl3 = dbl[cs, :].reshape(8, 1, 128) + jnp.sum(dkt * kt, axis=1, keepdims=True)
            db = q3 * dq - k3 * dk + jnp.where(t_io == anchor, dbl3, 0.0)
            dq_ref[rows, :] = dq.reshape(HG_RB, 128)
            dk_ref[rows, :] = dk.reshape(HG_RB, 128)
            db_ref[rows, :] = db.reshape(HG_RB, 128)
            dv_ref[rows, :] = dv.reshape(HG_RB, 128)
            return 0

        lax.fori_loop(0, HG_NB, phase_c, 0)

    col = pl.BlockSpec((T, 128), lambda h: (0, h))
    return pl.pallas_call(
        body, name="hg_scan_bwd_dir_bwd" if rev else "hg_scan_fwd_dir_bwd", grid=(HG_HEADS,),
        in_specs=[col, col, col, pl.BlockSpec((T, 128), lambda h: (0, 24 + h)),
                  pl.BlockSpec((1, HG_SLOTS, 128, 128), lambda h: (h, 0, 0, 0)), col],
        out_specs=(col,) * 4, out_shape=(_sds((T, 512), F32),) * 4,
        scratch_shapes=[pltpu.VMEM((HG_SLOTS, 128, 128), F32), pltpu.VMEM((HG_SLOTS, 128), F32),
                        pltpu.VMEM((HG_SLOTS, 128), F32)],
        compiler_params=_cp(("parallel",), 56))(qh, k, b, p_act, st, do)


def _row_valid(i, tm):
    r = lax.broadcasted_iota(jnp.int32, (tm, 1), 0) + i * tm
    return r < L


def _hg_post(o_f, o_b, p_act, gain):
    def body(of_ref, ob_ref, g_ref, gain_ref, u_ref):
        o = of_ref[...] + ob_ref[...]
        sg = jax.nn.silu(g_ref[...])
        parts = []
        for h in range(HG_HEADS):
            oh = o[:, 128 * h:128 * (h + 1)]
            parts.append(oh * lax.rsqrt(jnp.mean(oh * oh, axis=-1, keepdims=True) + EPS))
        n = jnp.concatenate(parts, axis=1)
        u = n * gain_ref[...] * sg
        u_ref[...] = jnp.where(_row_valid(pl.program_id(0), TM_E), u, 0.0).astype(BF16)

    blk = pl.BlockSpec((TM_E, 512), lambda i: (i, 0))
    return pl.pallas_call(
        body, name="hg_post", grid=(T // TM_E,),
        in_specs=[blk, blk, pl.BlockSpec((TM_E, 512), lambda i: (i, 7)), pl.BlockSpec((1, 512), lambda i: (0, 0))],
        out_specs=blk, out_shape=_sds((T, 512), BF16), compiler_params=_cp(("parallel",)))(o_f, o_b, p_act, gain)


def _hg_post_bwd(du, o_f, o_b, p_act, gain):
    def body(du_ref, of_ref, ob_ref, g_ref, gain_ref, do_ref, dg_ref, dgain_ref):
        i = pl.program_id(0)
        valid = _row_valid(i, TM_E)
        duv = jnp.where(valid, du_ref[...], 0.0)
        o = of_ref[...] + ob_ref[...]
        gv = g_ref[...]
        sig = jax.nn.sigmoid(gv)
        sg = gv * sig
        gain_v = gain_ref[...]
        dn = duv * gain_v * sg
        do_parts, n_parts = [], []
        for h in range(HG_HEADS):
            sl = slice(128 * h, 128 * (h + 1))
            oh = o[:, sl]
            r = lax.rsqrt(jnp.mean(oh * oh, axis=-1, keepdims=True) + EPS)
            nh = oh * r
            dnh = dn[:, sl]
            do_parts.append(r * (dnh - nh * jnp.mean(dnh * nh, axis=-1, keepdims=True)))
            n_parts.append(nh)
        n = jnp.where(valid, jnp.concatenate(n_parts, axis=1), 0.0)
        do_ref[...] = jnp.where(valid, jnp.concatenate(do_parts, axis=1), 0.0)
        dg_ref[...] = (duv * n * gain_v * (sig * (1.0 + gv * (1.0 - sig)))).astype(BF16)
        part = jnp.sum(duv * n * sg, axis=0, keepdims=True)

        @pl.when(i == 0)
        def _():
            dgain_ref[...] = part

        @pl.when(i > 0)
        def _():
            dgain_ref[...] += part

    blk = pl.BlockSpec((TM_E, 512), lambda i: (i, 0))
    vec = pl.BlockSpec((1, 512), lambda i: (0, 0))
    return pl.pallas_call(
        body, name="hg_post_bwd", grid=(T // TM_E,),
        in_specs=[blk, blk, blk, pl.BlockSpec((TM_E, 512), lambda i: (i, 7)), vec],
        out_specs=(blk, blk, vec), out_shape=(_sds((T, 512), F32), _sds((T, 512), BF16), _sds((1, 512), F32)),
        compiler_params=_cp(("arbitrary",)))(du, o_f, o_b, p_act, gain)


def _hg_pre_bwd(p_act, logits, dq_f, dq_b, dk_f, dk_b, db_f, db_b, dv_f, dv_b):
    def body(q_ref, zf_ref, zb_ref, lg_ref, dqf_ref, dqb_ref, dkf_ref, dkb_ref, dbf_ref, dbb_ref, dvf_ref, dvb_ref,
             dq_ref, dzf_ref, dzb_ref, di_ref, dlg_ref):
        i = pl.program_id(0)
        valid = _row_valid(i, HG_RB)
        qv = q_ref[...]
        sig = jax.nn.sigmoid(qv)
        dq_ref[...] = jnp.where(valid, (dqf_ref[...] + dqb_ref[...]) * (sig * (1.0 + qv * (1.0 - sig))), 0.0).astype(BF16)
        di_ref[...] = jnp.where(valid, dvf_ref[...] + dvb_ref[...], 0.0).astype(BF16)
        for d, (z_ref, dk_r, db_r, dz_ref) in enumerate(((zf_ref, dkf_ref, dbf_ref, dzf_ref), (zb_ref, dkb_ref, dbb_ref, dzb_ref))):
            lg = lg_ref[d]
            dl = lg[0:1, :] - lg[1:2, :]
            lb = jax.nn.sigmoid(dl)
            one_m_lb = jax.nn.sigmoid(-dl)
            log_f, _, snz, w2 = _hg_gate_terms(z_ref[...], lg)
            dbv = jnp.where(valid, db_r[...], 0.0)
            dkv = jnp.where(valid, dk_r[...], 0.0)
            dlf = jnp.dot(_chunk_tri(d == 1), dbv, precision=HI, preferred_element_type=F32)
            sz = 1.0 - snz
            dz_ref[...] = (dlf * w2 * snz - dkv * one_m_lb * sz * snz).astype(BF16)
            dlb = jnp.sum(dlf * snz * jnp.exp(-log_f) - dkv * snz, axis=0, keepdims=True)
            dl0 = dlb * lb * one_m_lb
            part = jnp.concatenate([dl0, -dl0], axis=0)

            @pl.when(i == 0)
            def _():
                dlg_ref[d] = part

            @pl.when(i > 0)
            def _():
                dlg_ref[d] += part

    blk = lambda c: pl.BlockSpec((HG_RB, 512), lambda i: (i, c))
    ob = pl.BlockSpec((HG_RB, 512), lambda i: (i, 0))
    lgs = pl.BlockSpec((2, 2, 512), lambda i: (0, 0, 0))
    return pl.pallas_call(
        body, name="hg_pre_bwd", grid=(HG_NB,),
        in_specs=[blk(3), blk(4), blk(5), lgs] + [ob] * 8,
        out_specs=(ob, ob, ob, ob, lgs),
        out_shape=(_sds((T, 512), BF16),) * 4 + (_sds((2, 2, 512), F32),),
        compiler_params=_cp(("arbitrary",)))(p_act, p_act, p_act, logits, dq_f, dq_b, dk_f, dk_b, db_f, db_b, dv_f, dv_b)


def _mix_fwd(o_na, u_hg, wna_g, whg_g, p_act):
    def body(ona_ref, uhg_ref, wna_ref, whg_ref, gna_ref, ghg_ref, o_ref):
        y_na = _dot(ona_ref[...], wna_ref[0])
        y_hg = _dot(uhg_ref[...], whg_ref[0])
        o_ref[...] = (jax.nn.sigmoid(gna_ref[...]) * y_na + jax.nn.sigmoid(ghg_ref[...]) * y_hg).astype(BF16)

    act = pl.BlockSpec((TM_MM, 512), lambda i, j: (i, 0))
    wsp = pl.BlockSpec((1, 512, 128), lambda i, j: (j, 0, 0))
    return pl.pallas_call(
        body, name="mix_fwd", grid=(T // TM_MM, NDEV),
        in_specs=[act, act, wsp, wsp, pl.BlockSpec((TM_MM, 128), lambda i, j: (i, 32 + j)),
                  pl.BlockSpec((TM_MM, 128), lambda i, j: (i, 40 + j))],
        out_specs=pl.BlockSpec((TM_MM, 128), lambda i, j: (i, j)), out_shape=_sds((T, D), BF16),
        compiler_params=_cp(("parallel", "parallel")))(o_na, u_hg, wna_g, whg_g, p_act, p_act)


def _mix_bwd(o_na, u_hg, wna_g, whg_g, p_act, dmix):
    ni = T // TM_B

    def body(ona_ref, uhg_ref, wna_ref, whg_ref, gna_ref, ghg_ref, dmix_ref,
             dgna_ref, dghg_ref, dwna_ref, dwhg_ref, dona_ref, duhg_ref, acc_na, acc_hg):
        j, i = pl.program_id(0), pl.program_id(1)
        rows = pl.ds(pl.multiple_of(i * TM_B, TM_B), TM_B)
        dm = dmix_ref[...].astype(F32)
        for x_ref, w_ref, g_ref, dg_ref, dx_ref, dw_ref, acc in (
                (ona_ref, wna_ref, gna_ref, dgna_ref, dona_ref, dwna_ref, acc_na),
                (uhg_ref, whg_ref, ghg_ref, dghg_ref, duhg_ref, dwhg_ref, acc_hg)):
            xv = x_ref[...]
            y = _dot(xv, w_ref[0])
            sg = jax.nn.sigmoid(g_ref[...])
            dg_ref[...] = (dm * y * sg * (1.0 - sg)).astype(BF16)
            dy = (dm * sg).astype(BF16)
            part = _dot(xv, dy, TN)
            dxv = _dot(dy, w_ref[0], NT)

            @pl.when(i == 0)
            def _():
                acc[...] = part

            @pl.when(i > 0)
            def _():
                acc[...] += part

            @pl.when(i == ni - 1)
            def _():
                dw_ref[0] = acc[...].astype(BF16)

            @pl.when(j == 0)
            def _():
                dx_ref[rows, :] = dxv

            @pl.when(j > 0)
            def _():
                dx_ref[rows, :] += dxv

    act = pl.BlockSpec((TM_B, 512), lambda j, i: (i, 0))
    wsp = pl.BlockSpec((1, 512, 128), lambda j, i: (j, 0, 0))
    cblk = pl.BlockSpec((TM_B, 128), lambda j, i: (i, j))
    full = pl.BlockSpec((T, 512), lambda j, i: (0, 0))
    return pl.pallas_call(
        body, name="mix_bwd", grid=(NDEV, ni),
        in_specs=[act, act, wsp, wsp, pl.BlockSpec((TM_B, 128), lambda j, i: (i, 32 + j)),
                  pl.BlockSpec((TM_B, 128), lambda j, i: (i, 40 + j)), cblk],
        out_specs=(cblk, cblk, wsp, wsp, full, full),
        out_shape=(_sds((T, D), BF16), _sds((T, D), BF16), _sds((NDEV, 512, 128), BF16), _sds((NDEV, 512, 128), BF16),
                   _sds((T, 512), F32), _sds((T, 512), F32)),
        scratch_shapes=[pltpu.VMEM((512, 128), F32), pltpu.VMEM((512, 128), F32)],
        compiler_params=_cp(("arbitrary", "arbitrary")))(o_na, u_hg, wna_g, whg_g, p_act, p_act, dmix)


def _wo_fwd(mix, w_o, h0, g_mlp):
    def body(mix_ref, w_ref, h0_ref, g_ref, h1_ref, m_ref):
        h1 = h0_ref[...] + _dot(mix_ref[...], w_ref[...])
        h1_ref[...] = h1
        r = lax.rsqrt(jnp.mean(h1 * h1, axis=-1, keepdims=True) + EPS)
        m_ref[...] = (h1 * r * g_ref[...]).astype(BF16)

    blk = pl.BlockSpec((TM_B, D), lambda i: (i, 0))
    return pl.pallas_call(
        body, name="wo_fwd", grid=(T // TM_B,),
        in_specs=[blk, pl.BlockSpec((D, D), lambda i: (0, 0)), blk, pl.BlockSpec((1, D), lambda i: (0, 0))],
        out_specs=(blk, blk), out_shape=(_sds((T, D), F32), _sds((T, D), BF16)),
        compiler_params=_cp(("parallel",)))(mix, w_o, h0, g_mlp)


def _wo_bwd(dh1_b, w_o, mix):
    ni = T // TM_B

    def body(dh_ref, w_ref, mix_ref, dmix_ref, dw_ref, acc):
        i = pl.program_id(0)
        dh = dh_ref[...]
        dmix_ref[...] = _dot(dh, w_ref[...], NT).astype(BF16)
        part = _dot(mix_ref[...], dh, TN)

        @pl.when(i == 0)
        def _():
            acc[...] = part

        @pl.when(i > 0)
        def _():
            acc[...] += part

        @pl.when(i == ni - 1)
        def _():
            dw_ref[...] = acc[...].astype(BF16)

    blk = pl.BlockSpec((TM_B, D), lambda i: (i, 0))
    wsp = pl.BlockSpec((D, D), lambda i: (0, 0))
    return pl.pallas_call(
        body, name="wo_bwd", grid=(ni,), in_specs=[blk, wsp, blk], out_specs=(blk, wsp),
        out_shape=(_sds((T, D), BF16), _sds((D, D), BF16)), scratch_shapes=[pltpu.VMEM((D, D), F32)],
        compiler_params=_cp(("arbitrary",)))(dh1_b, w_o, mix)


FF_B = D_FF // NDEV


def _mlp_fwd(m, wup_g, wdown_g, h1):
    def body(m_ref, wu_ref, wd_ref, h1_ref, h2_ref):
        j = pl.program_id(1)
        up = jnp.maximum(_dot(m_ref[...], wu_ref[0]), 0.0)
        part = _dot((up * up).astype(BF16), wd_ref[0])

        @pl.when(j == 0)
        def _():
            h2_ref[...] = h1_ref[...] + part

        @pl.when(j > 0)
        def _():
            h2_ref[...] += part

    blk = pl.BlockSpec((TM_MM, D), lambda i, j: (i, 0))
    return pl.pallas_call(
        body, name="mlp_fwd", grid=(T // TM_MM, NDEV),
        in_specs=[blk, pl.BlockSpec((1, D, FF_B), lambda i, j: (j, 0, 0)), pl.BlockSpec((1, FF_B, D), lambda i, j: (j, 0, 0)), blk],
        out_specs=blk, out_shape=_sds((T, D), F32),
        compiler_params=_cp(("parallel", "arbitrary")))(m, wup_g, wdown_g, h1)


def _mlp_bwd(m, dh2_b, wup_g, wdown_g):
    ni = T // TM_B

    def body(m_ref, dh_ref, wu_ref, wd_ref, dwu_ref, dwd_ref, dm_ref, acc_u, acc_d):
        j, i = pl.program_id(0), pl.program_id(1)
        rows = pl.ds(pl.multiple_of(i * TM_B, TM_B), TM_B)
        mv, dh = m_ref[...], dh_ref[...]
        r = jnp.maximum(_dot(mv, wu_ref[0]), 0.0)
        act = (r * r).astype(BF16)
        dact = _dot(dh, wd_ref[0], NT)
        dup = (dact * (2.0 * r)).astype(BF16)
        pd = _dot(act, dh, TN)
        pu = _dot(mv, dup, TN)
        dmv = _dot(dup, wu_ref[0], NT)

        @pl.when(i == 0)
        def _():
            acc_u[...] = pu
            acc_d[...] = pd

        @pl.when(i > 0)
        def _():
            acc_u[...] += pu
            acc_d[...] += pd

        @pl.when(i == ni - 1)
        def _():
            dwu_ref[0] = acc_u[...].astype(BF16)
            dwd_ref[0] = acc_d[...].astype(BF16)

        @pl.when(j == 0)
        def _():
            dm_ref[rows, :] = dmv

        @pl.when(j > 0)
        def _():
            dm_ref[rows, :] += dmv

    blk = pl.BlockSpec((TM_B, D), lambda j, i: (i, 0))
    wus = pl.BlockSpec((1, D, FF_B), lambda j, i: (j, 0, 0))
    wds = pl.BlockSpec((1, FF_B, D), lambda j, i: (j, 0, 0))
    return pl.pallas_call(
        body, name="mlp_bwd", grid=(NDEV, ni), in_specs=[blk, blk, wus, wds],
        out_specs=(wus, wds, pl.BlockSpec((T, D), lambda j, i: (0, 0))),
        out_shape=(_sds((NDEV, D, FF_B), BF16), _sds((NDEV, FF_B, D), BF16), _sds((T, D), F32)),
        scratch_shapes=[pltpu.VMEM((D, FF_B), F32), pltpu.VMEM((FF_B, D), F32)],
        compiler_params=_cp(("arbitrary", "arbitrary")))(m, dh2_b, wup_g, wdown_g)


def _loss_head(h2, g_final, tgt):
    def body(h_ref, g_ref, t_ref, loss_ref, dh_ref, dhb_ref, dg_ref):
        i = pl.program_id(0)
        r_io = lax.broadcasted_iota(jnp.int32, (TM_E, 1), 0) + i * TM_E
        valid = (r_io >= NM) & (r_io < L)
        xv = h_ref[...]
        r = lax.rsqrt(jnp.mean(xv * xv, axis=-1, keepdims=True) + EPS)
        xh = xv * r
        gv = g_ref[...]
        err = jnp.where(valid, xh * gv - t_ref[...], 0.0)
        lpart = jnp.broadcast_to(0.5 * jnp.sum(jnp.sum(err * err, axis=-1, keepdims=True) * (1.0 / D), axis=0, keepdims=True), (1, 128))
        dy = err * (1.0 / D)
        dxh = dy * gv
        dh = r * (dxh - xh * jnp.mean(dxh * xh, axis=-1, keepdims=True))
        dh_ref[...] = dh
        dhb_ref[...] = dh.astype(BF16)
        gpart = jnp.sum(dy * xh, axis=0, keepdims=True)

        @pl.when(i == 0)
        def _():
            loss_ref[...] = lpart
            dg_ref[...] = gpart

        @pl.when(i > 0)
        def _():
            loss_ref[...] += lpart
            dg_ref[...] += gpart

    blk = pl.BlockSpec((TM_E, D), lambda i: (i, 0))
    vec = pl.BlockSpec((1, D), lambda i: (0, 0))
    return pl.pallas_call(
        body, name="loss_head", grid=(T // TM_E,), in_specs=[blk, vec, blk],
        out_specs=(pl.BlockSpec((1, 128), lambda i: (0, 0)), blk, blk, vec),
        out_shape=(_sds((1, 128), F32), _sds((T, D), F32), _sds((T, D), BF16), _sds((1, D), F32)),
        compiler_params=_cp(("arbitrary",)))(h2, g_final, tgt)


def _adamw(parts, w, m, v, name):
    rr, cc = w.shape
    tr = rr
    for cand in (256, 128, 64):
        if rr % cand == 0 and rr > cand:
            tr = cand
            break
    c1 = 1.0 - ADAM_B1 ** ADAM_STEP
    c2 = 1.0 - ADAM_B2 ** ADAM_STEP

    def body(p_ref, w_ref, m_ref, v_ref, g_ref, d_ref, nm_ref, nv_ref):
        g = p_ref[0].astype(F32)
        for s in range(1, NDEV):
            g = g + p_ref[s].astype(F32)
        mn = ADAM_B1 * m_ref[...] + (1.0 - ADAM_B1) * g
        vn = ADAM_B2 * v_ref[...] + (1.0 - ADAM_B2) * (g * g)
        g_ref[...] = g
        nm_ref[...] = mn
        nv_ref[...] = vn
        d_ref[...] = -ADAM_LR * ((mn / c1) / (jnp.sqrt(vn / c2) + ADAM_EPS) + ADAM_WD * w_ref[...])

    blk = pl.BlockSpec((tr, cc), lambda i: (i, 0))
    return pl.pallas_call(
        body, name=name, grid=(rr // tr,),
        in_specs=[pl.BlockSpec((NDEV, tr, cc), lambda i: (0, i, 0)), blk, blk, blk],
        out_specs=(blk,) * 4, out_shape=(_sds((rr, cc), F32),) * 4,
        compiler_params=_cp(("parallel",)))(parts, w, m, v)


RPB_N = NA_HEADS * 15 * 31
RPB_PAD = 3840


def _pack_owned(meta_blk, lb_blk):
    return jnp.concatenate([meta_blk, lb_blk.reshape(2, 128)], axis=0)


def _pack_replicated(n_mix, n_mlp, n_final, hg_gain, rpb):
    flat = jnp.concatenate([rpb.reshape(RPB_N), jnp.zeros((RPB_PAD - RPB_N,), F32)])
    return jnp.concatenate([n_mix.reshape(8, 128), n_mlp.reshape(8, 128), n_final.reshape(8, 128), hg_gain.reshape(4, 128),
                            flat.reshape(30, 128)], axis=0)


def _unpack_replicated(a):
    return (a[0:8].reshape(1, D), a[8:16].reshape(1, D), a[16:24].reshape(D), a[24:28].reshape(1, 512),
            a[28:58].reshape(RPB_PAD)[:RPB_N].reshape(1, NA_HEADS, 15, 31))


def kernel(x, meta_tokens, w_in, w_na_out, w_hg_out, w_o, w_up, w_down, norm_mix, norm_mlp, norm_final, hg_norm, na_rpb, hg_lb_logits, loss_target, m_meta_tokens, m_w_in, m_w_na_out, m_w_hg_out, m_w_o, m_w_up, m_w_down, m_norm_mix, m_norm_mlp, m_norm_final, m_hg_norm, m_na_rpb, m_hg_lb_logits, v_meta_tokens, v_w_in, v_w_na_out, v_w_hg_out, v_w_o, v_w_up, v_w_down, v_norm_mix, v_norm_mlp, v_norm_final, v_hg_norm, v_na_rpb, v_hg_lb_logits):
    owned = _pack_owned(meta_tokens, hg_lb_logits)
    win_g, wna_g, whg_g, wo_g, wup_g, wdown_g, owned_g = _exchange(
        [w_in[0].astype(BF16), w_na_out[0].astype(BF16), w_hg_out[0].astype(BF16), w_o[0].astype(BF16),
         w_up[0].astype(BF16), w_down[0].astype(BF16), owned],
        [False] * 7, "gather_params")
    w_o_full = wo_g.reshape(D, D)
    meta_full = jnp.transpose(owned_g[:, 0:NM, :], (1, 0, 2)).reshape(NM, D)
    logits = jnp.transpose(owned_g[:, NM:NM + 2, :].reshape(NDEV, 2, 2, 64), (1, 2, 0, 3)).reshape(2, 2, 512)

    h0 = jnp.concatenate([meta_full, x[0], jnp.zeros((T - L, D), F32)], axis=0)
    tgt = jnp.concatenate([jnp.zeros((NM, D), F32), loss_target[0], jnp.zeros((T - L, D), F32)], axis=0)
    bias_tab = _na_bias_table(na_rpb[0])

    a = _norm_fwd(h0, norm_mix, "norm_mix_fwd")
    p_act = _inproj_fwd(a, win_g)
    o_na, lse = _na_fwd(p_act, bias_tab)
    qh, k_f, b_f, k_b, b_b = _hg_pre(p_act, logits)
    o_f, st_f = _hg_scan_fwd(qh, k_f, b_f, p_act, False)
    o_b, st_b = _hg_scan_fwd(qh, k_b, b_b, p_act, True)
    u_hg = _hg_post(o_f, o_b, p_act, hg_norm)
    mix = _mix_fwd(o_na, u_hg, wna_g, whg_g, p_act)
    h1, m_act = _wo_fwd(mix, w_o_full, h0, norm_mlp)
    h2 = _mlp_fwd(m_act, wup_g, wdown_g, h1)
    loss_part, dh2, dh2_b, d_nfinal = _loss_head(h2, norm_final.reshape(1, D), tgt)

    dwup_p, dwdown_p, dm = _mlp_bwd(m_act, dh2_b, wup_g, wdown_g)
    dh1, dh1_b, d_nmlp = _norm_bwd(h1, norm_mlp, dm, dh2, "norm_mlp_bwd")
    dmix, dwo = _wo_bwd(dh1_b, w_o_full, mix)
    dgna, dghg, dwna_p, dwhg_p, do_na, du_hg = _mix_bwd(o_na, u_hg, wna_g, whg_g, p_act, dmix)
    do_hg, dg_hg, d_gain = _hg_post_bwd(du_hg, o_f, o_b, p_act, hg_norm)
    dq_f, dk_f, db_f, dv_f = _hg_scan_bwd(qh, k_f, b_f, p_act, st_f, do_hg, False)
    dq_b, dk_b, db_b, dv_b = _hg_scan_bwd(qh, k_b, b_b, p_act, st_b, do_hg, True)
    dq_hg, dz_f, dz_b, di_hg, d_logits = _hg_pre_bwd(p_act, logits, dq_f, dq_b, dk_f, dk_b, db_f, db_b, dv_f, dv_b)
    dq_na, dk_na, dv_na, dbias = _na_bwd(p_act, do_na, lse, bias_tab)
    d_rpb = _na_rpb_reduce(dbias)[:, :, :31]
    dp = jnp.concatenate([dq_na.astype(BF16), dk_na.astype(BF16), dv_na.astype(BF16), dq_hg, dz_f, dz_b, di_hg, dg_hg,
                          dgna, dghg], axis=1)
    dwin_p, da = _inproj_bwd(a, dp, win_g)
    dh0, _, d_nmix = _norm_bwd(h0, norm_mix, da, dh1, "norm_mix_bwd")

    d_meta = jnp.transpose(dh0[0:NM].reshape(NM, NDEV, 128), (1, 0, 2))
    d_lg = jnp.transpose(d_logits.reshape(2, 2, NDEV, 64), (2, 0, 1, 3)).reshape(NDEV, 2, 128)
    owned_p = jnp.concatenate([d_meta, d_lg], axis=1)
    repl_p = _pack_replicated(d_nmix, d_nmlp, d_nfinal, d_gain, d_rpb)
    win_r, wna_r, whg_r, wo_r, wup_r, wdown_r, owned_r, repl_r = _exchange(
        [dwin_p, dwna_p, dwhg_p, dwo.reshape(NDEV, D // NDEV, D), dwup_p, dwdown_p, owned_p, repl_p],
        [True] * 7 + [False], "scatter_grads")

    res = {}
    for nm, parts, w, mm, vv in (
            ("w_in", win_r, w_in, m_w_in, v_w_in), ("w_na_out", wna_r, w_na_out, m_w_na_out, v_w_na_out),
            ("w_hg_out", whg_r, w_hg_out, m_w_hg_out, v_w_hg_out), ("w_o", wo_r, w_o, m_w_o, v_w_o),
            ("w_up", wup_r, w_up, m_w_up, v_w_up), ("w_down", wdown_r, w_down, m_w_down, v_w_down)):
        res[nm] = [r[None] for r in _adamw(parts, w[0], mm[0], vv[0], "adamw_" + nm)]
    own = _adamw(owned_r, owned, _pack_owned(m_meta_tokens, m_hg_lb_logits), _pack_owned(v_meta_tokens, v_hg_lb_logits),
                 "adamw_owned_small")
    res["meta_tokens"] = [r[0:NM] for r in own]
    res["hg_lb_logits"] = [r[NM:NM + 2].reshape(2, 2, 64) for r in own]
    rep = _adamw(repl_r, _pack_replicated(norm_mix, norm_mlp, norm_final, hg_norm, na_rpb),
                 _pack_replicated(m_norm_mix, m_norm_mlp, m_norm_final, m_hg_norm, m_na_rpb),
                 _pack_replicated(v_norm_mix, v_norm_mlp, v_norm_final, v_hg_norm, v_na_rpb), "adamw_replicated")
    for q in range(4):
        um = _unpack_replicated(rep[q])
        for nm, val in zip(("norm_mix", "norm_mlp", "norm_final", "hg_norm", "na_rpb"), um):
            res.setdefault(nm, [None] * 4)[q] = val

    loss = lax.psum(loss_part[0, 0], ("x", "y", "c"))
    grad_x = dh0[NM:L][None]
    order = ("meta_tokens", "w_in", "w_na_out", "w_hg_out", "w_o", "w_up", "w_down", "norm_mix", "norm_mlp", "norm_final",
             "hg_norm", "na_rpb", "hg_lb_logits")
    outs = [loss, grad_x]
    for q in range(4):
        outs += [res[nm][q] for nm in order]
    return tuple(outs)
```

```python
import functools

import numpy as np
import jax
import jax.numpy as jnp
from jax import lax
from jax.experimental import pallas as pl
from jax.experimental.pallas import tpu as pltpu

F32 = jnp.float32
BF16 = jnp.bfloat16

D = 1024
SEQ = 2048
NM = 16
L = SEQ + NM
T = 2176
NDEV = 8
EPS = 1e-6
GRID_W = 64
ROWS = SEQ // GRID_W
NA_HEADS = 8
NA_DH = 64
NA_SCALE = NA_DH ** -0.5
HG_HEADS = 4
HG_C = 16
NCHUNK = L // HG_C
D_FF = 4096
IN_COLS = 6144
NEG = -1e30

ADAM_LR = 0.001
ADAM_B1 = 0.9
ADAM_B2 = 0.999
ADAM_EPS = 1e-08
ADAM_WD = 0.01
ADAM_STEP = 10

MESH_ID = pl.DeviceIdType.MESH
ANY = pl.BlockSpec(memory_space=pl.ANY)

NN = (((1,), (0,)), ((), ()))
NT = (((1,), (1,)), ((), ()))
TN = (((0,), (0,)), ((), ()))


def _cp(sem=None, vmem_mb=48):
    return pltpu.CompilerParams(dimension_semantics=sem, vmem_limit_bytes=vmem_mb * 1024 * 1024)


def _dot(a, b, dims=NN):
    return lax.dot_general(a, b, dims, preferred_element_type=F32)


def _sds(shape, dtype):
    return jax.ShapeDtypeStruct(shape, dtype)


HBM = pl.BlockSpec(memory_space=pltpu.HBM)
SEM = pl.BlockSpec(memory_space=pltpu.SEMAPHORE)
EFFECT = pltpu.SideEffectType.DATAFLOW_SIDE_EFFECTING


def _exchange(arrs, scatter, name):
    n = len(arrs)
    out_shapes = []
    for a, sc in zip(arrs, scatter):
        out_shapes.append(_sds(a.shape if sc else (NDEV,) + a.shape, a.dtype))

    def body(*refs):
        ins, outs = refs[:n], refs[n:2 * n]
        send_sems, recv_sems, loc_sems = refs[2 * n:]
        me = 4 * lax.axis_index("x") + 2 * lax.axis_index("y") + lax.axis_index("c")
        copies = []
        for k in range(n):
            src_me = ins[k].at[me] if scatter[k] else ins[k]
            loc = pltpu.make_async_copy(src_me, outs[k].at[me], loc_sems.at[k])
            loc.start()
            copies.append(loc)
        remote = _peer_copies(ins, outs, scatter, send_sems, recv_sems)
        for cp in remote:
            cp.start()
        for cp in remote:
            cp.wait_recv()
        for cp in remote:
            cp.wait_send()
        for cp in copies:
            cp.wait()

    return pl.pallas_call(
        body, name=name, out_shape=tuple(out_shapes), in_specs=[ANY] * n, out_specs=tuple([ANY] * n),
        scratch_shapes=[pltpu.SemaphoreType.DMA((n * (NDEV - 1),)), pltpu.SemaphoreType.DMA((n * (NDEV - 1),)),
                        pltpu.SemaphoreType.DMA((n,))],
    )(*arrs)


def _peer_copies(srcs, lands, scatter, send_sems, recv_sems):
    x, y, c = lax.axis_index("x"), lax.axis_index("y"), lax.axis_index("c")
    me = 4 * x + 2 * y + c
    out = []
    for k in range(len(srcs)):
        for m in range(1, NDEV):
            px, py, pc = x ^ (m >> 2), y ^ ((m >> 1) & 1), c ^ (m & 1)
            src = srcs[k].at[4 * px + 2 * py + pc] if scatter[k] else srcs[k]
            out.append(pltpu.make_async_remote_copy(
                src_ref=src, dst_ref=lands[k].at[me], send_sem=send_sems.at[k * (NDEV - 1) + m - 1],
                recv_sem=recv_sems.at[k * (NDEV - 1) + m - 1],
                device_id=(px, py, pc), device_id_type=MESH_ID))
    return out


def _exchange_start(arrs, scatter, name):
    n = len(arrs)
    me = 4 * lax.axis_index("x") + 2 * lax.axis_index("y") + lax.axis_index("c")
    lands = []
    for a, sc in zip(arrs, scatter):
        own = lax.dynamic_index_in_dim(a, me, 0, keepdims=True) if sc else a[None]
        shape = a.shape if sc else (NDEV,) + a.shape
        lands.append(lax.dynamic_update_index_in_dim(lax.empty(shape, a.dtype), own, me, 0))

    def body(*refs):
        srcs, lnds = refs[:n], refs[n:2 * n]
        send_sems, recv_sems = refs[2 * n], refs[2 * n + 1]
        token = refs[-1]
        for cp in _peer_copies(srcs, lnds, scatter, send_sems, recv_sems):
            cp.start()
        token[...] = jnp.zeros_like(token)

    ops = [pltpu.with_memory_space_constraint(a, pltpu.HBM) for a in list(arrs) + lands]
    res = pl.pallas_call(
        body, name=name,
        out_shape=(pltpu.SemaphoreType.DMA((n * (NDEV - 1),)), pltpu.SemaphoreType.DMA((n * (NDEV - 1),)))
        + tuple(pltpu.HBM(o.shape, o.dtype) for o in ops) + (_sds((8, 128), F32),),
        in_specs=[HBM] * (2 * n), out_specs=(SEM, SEM) + (HBM,) * (2 * n) + (pl.BlockSpec(memory_space=pltpu.VMEM),),
        input_output_aliases={k: 2 + k for k in range(2 * n)},
        compiler_params=pltpu.CompilerParams(has_side_effects=EFFECT),
    )(*ops)
    return res[:-1], res[-1]


def _exchange_wait(handle, scatter, after, name):
    send_sems, recv_sems = handle[0], handle[1]
    bufs = handle[2:]
    n = len(bufs) // 2

    def body(*refs):
        srcs, lnds = refs[:n], refs[n:2 * n]
        for cp in _peer_copies(srcs, lnds, scatter, refs[2 * n], refs[2 * n + 1]):
            cp.wait_send()
            cp.wait_recv()

    res = pl.pallas_call(
        body, name=name, out_shape=tuple(pltpu.HBM(b.shape, b.dtype) for b in bufs),
        in_specs=[HBM] * (2 * n) + [SEM, SEM, ANY], out_specs=(HBM,) * (2 * n),
        input_output_aliases={k: k for k in range(2 * n)},
        compiler_params=pltpu.CompilerParams(has_side_effects=EFFECT),
    )(*bufs, send_sems, recv_sems, after)
    return res[n:]


def _tie(x, token):
    return lax.optimization_barrier((x, token))[0]


TM_E = 272


def _norm_fwd(h, g, name):
    def body(h_ref, g_ref, o_ref):
        xv = h_ref[...]
        r = lax.rsqrt(jnp.mean(xv * xv, axis=-1, keepdims=True) + EPS)
        o_ref[...] = (xv * r * g_ref[...]).astype(BF16)

    return pl.pallas_call(
        body, name=name, grid=(T // TM_E,),
        in_specs=[pl.BlockSpec((TM_E, D), lambda i: (i, 0)), pl.BlockSpec((1, D), lambda i: (0, 0))],
        out_specs=pl.BlockSpec((TM_E, D), lambda i: (i, 0)), out_shape=_sds((T, D), BF16),
        compiler_params=_cp(("parallel",)))(h, g)


def _norm_bwd(h, g, dn, dres, name):
    def body(h_ref, g_ref, dn_ref, dres_ref, dh_ref, dhb_ref, dg_ref):
        i = pl.program_id(0)
        xv = h_ref[...]
        r = lax.rsqrt(jnp.mean(xv * xv, axis=-1, keepdims=True) + EPS)
        xh = xv * r
        dnv = dn_ref[...].astype(F32)
        dxh = dnv * g_ref[...]
        dh = dres_ref[...] + r * (dxh - xh * jnp.mean(dxh * xh, axis=-1, keepdims=True))
        dh_ref[...] = dh
        dhb_ref[...] = dh.astype(BF16)
        part = jnp.sum(dnv * xh, axis=0, keepdims=True)

        @pl.when(i == 0)
        def _():
            dg_ref[...] = part

        @pl.when(i > 0)
        def _():
            dg_ref[...] += part

    blk = pl.BlockSpec((TM_E, D), lambda i: (i, 0))
    vec = pl.BlockSpec((1, D), lambda i: (0, 0))
    return pl.pallas_call(
        body, name=name, grid=(T // TM_E,), in_specs=[blk, vec, blk, blk], out_specs=(blk, blk, vec),
        out_shape=(_sds((T, D), F32), _sds((T, D), BF16), _sds((1, D), F32)),
        compiler_params=_cp(("arbitrary",)))(h, g, dn, dres)


TM_MM = 1088


def _inproj_fwd(a, w_g):
    nb = w_g.shape[2]

    def body(a_ref, w_ref, o_ref):
        o_ref[...] = _dot(a_ref[...], w_ref[0])

    return pl.pallas_call(
        body, name="inproj_fwd", grid=(T // TM_MM, NDEV),
        in_specs=[pl.BlockSpec((TM_MM, D), lambda i, j: (i, 0)), pl.BlockSpec((1, D, nb), lambda i, j: (j, 0, 0))],
        out_specs=pl.BlockSpec((TM_MM, nb), lambda i, j: (i, j)), out_shape=_sds((T, NDEV * nb), F32),
        compiler_params=_cp(("parallel", "parallel")))(a, w_g)


TM_B = 544


def _inproj_bwd(a, dp, w_g):
    nb = w_g.shape[2]
    ni = T // TM_B

    def body(a_ref, dp_ref, w_ref, dw_ref, da_ref, acc):
        j, i = pl.program_id(0), pl.program_id(1)
        av, dpv = a_ref[...], dp_ref[...]
        part = _dot(av, dpv, TN)

        @pl.when(i == 0)
        def _():
            acc[...] = part

        @pl.when(i > 0)
        def _():
            acc[...] += part

        @pl.when(i == ni - 1)
        def _():
            dw_ref[0] = acc[...].astype(BF16)

        rows = pl.ds(pl.multiple_of(i * TM_B, TM_B), TM_B)
        dav = _dot(dpv, w_ref[0], NT)

        @pl.when(j == 0)
        def _():
            da_ref[rows, :] = dav

        @pl.when(j > 0)
        def _():
            da_ref[rows, :] += dav

    return pl.pallas_call(
        body, name="inproj_bwd", grid=(NDEV, ni),
        in_specs=[pl.BlockSpec((TM_B, D), lambda j, i: (i, 0)), pl.BlockSpec((TM_B, nb), lambda j, i: (i, j)),
                  pl.BlockSpec((1, D, nb), lambda j, i: (j, 0, 0))],
        out_specs=(pl.BlockSpec((1, D, nb), lambda j, i: (j, 0, 0)), pl.BlockSpec((T, D), lambda j, i: (0, 0))),
        out_shape=(_sds((NDEV, D, nb), BF16), _sds((T, D), F32)),
        scratch_shapes=[pltpu.VMEM((D, nb), F32)],
        compiler_params=_cp(("arbitrary", "arbitrary")))(a, dp, w_g)


NA_QB = 256
NA_GROUPS = ROWS // 4
NA_UROWS = 11
NA_KW = NA_UROWS * GRID_W
NA_KU = 768


def _na_row_offset(var, i, j):
    valid = (j < 8, i <= j < i + 8, 3 <= j < NA_UROWS)[var]
    return (j - i + (7, 3, 0)[var]) if valid else None


def _na_bias_table(rpb):
    def body(r_ref, o_ref):
        row = lax.broadcasted_iota(jnp.int32, (GRID_W, 128), 0)
        lane = lax.broadcasted_iota(jnp.int32, (GRID_W, 128), 1)
        w = lane & (GRID_W - 1)
        cs = jnp.clip(row - 8, 0, GRID_W - 16)
        in_win = (w >= cs) & (w < cs + 16)
        neg = jnp.full((GRID_W, 128), NEG, F32)
        tabs = []
        for a in range(15):
            z = jnp.broadcast_to(r_ref[0, a:a + 1, :], (GRID_W, 128))
            for bit in range(6):
                sh = 1 << bit
                z = jnp.where((row & sh) != 0, jnp.roll(z, sh, axis=1), z)
            z = jnp.roll(z, 128 - 15, axis=1)
            z = jnp.where(lane < GRID_W, z, 0.0)
            z = z + jnp.roll(z, GRID_W, axis=1)
            tabs.append(jnp.where(in_win, z, NEG))
        tail = jnp.where(lane < GRID_W + NM, 0.0, NEG)
        for var in range(3):
            for i in range(4):
                for jp in range(NA_KU // 128):
                    halves = []
                    for j in (2 * jp, 2 * jp + 1):
                        a = _na_row_offset(var, i, j) if j < NA_UROWS else None
                        halves.append(tail if j >= NA_UROWS else (neg if a is None else tabs[a]))
                    o_ref[var, 0, i * 64:(i + 1) * 64, jp * 128:(jp + 1) * 128] = jnp.where(lane < GRID_W, halves[0], halves[1])

    rp = jnp.concatenate([rpb, jnp.zeros((NA_HEADS, 15, 128 - 31), F32)], axis=2)
    return pl.pallas_call(
        body, name="na_bias_table", grid=(NA_HEADS,),
        in_specs=[pl.BlockSpec((1, 15, 128), lambda h: (h, 0, 0))],
        out_specs=pl.BlockSpec((3, 1, NA_QB, NA_KU), lambda h: (0, h, 0, 0)),
        out_shape=_sds((3, NA_HEADS, NA_QB, NA_KU), F32), compiler_params=_cp(("parallel",)))(rp)


def _na_var(g):
    return jnp.where(g == 0, 0, jnp.where(g == NA_GROUPS - 1, 2, 1))


def _na_load_window(src_ref, dst, g):
    us = jnp.clip(4 * g - 4, 0, ROWS - NA_UROWS)
    kstart = pl.multiple_of(NM + GRID_W * us, 16)
    dst[0:NA_KW, :] = src_ref[pl.ds(kstart, NA_KW), :].astype(BF16)
    dst[NA_KW:NA_KW + NM, :] = src_ref[0:NM, :].astype(BF16)
    dst[NA_KW + NM:, :] = jnp.zeros((NA_KU - NA_KW - NM, 128), BF16)
    return kstart


def _na_fwd(p_act, bias_tab):
    def body(q_ref, k_ref, v_ref, b_ref, o_ref, lse_ref, ku, vu):
        g = pl.program_id(1)
        _na_load_window(k_ref, ku, g)
        _na_load_window(v_ref, vu, g)
        qstart = pl.multiple_of(NM + NA_QB * g, 16)
        q = q_ref[pl.ds(qstart, NA_QB), :]
        lane = lax.broadcasted_iota(jnp.int32, (NA_QB, 128), 1)
        o_h, lse_h = [], []
        for h in range(2):
            hm = (lane < 64) if h == 0 else (lane >= 64)
            qm = jnp.where(hm, q, 0.0).astype(BF16)
            s = _dot(qm, ku[...], NT) * NA_SCALE + b_ref[0, h]
            m = jnp.max(s, axis=-1, keepdims=True)
            p = jnp.exp(s - m)
            l = jnp.sum(p, axis=-1, keepdims=True)
            o_h.append(_dot(p.astype(BF16), vu[...]) / l)
            lse_h.append(jnp.broadcast_to(m + jnp.log(l), (NA_QB, 128)))
        o_ref[pl.ds(qstart, NA_QB), :] = jnp.where(lane < 64, o_h[0], o_h[1]).astype(BF16)
        lse_ref[0, pl.ds(qstart, NA_QB), :] = jnp.where(lane < 64, lse_h[0], lse_h[1])

        @pl.when(g == 0)
        def _():
            qm_ = q_ref[0:NM, :]
            lane_m = lax.broadcasted_iota(jnp.int32, (NM, 128), 1)
            km, vm = ku[NA_KW:NA_KW + NM, :], vu[NA_KW:NA_KW + NM, :]
            om = []
            for h in range(2):
                hm = (lane_m < 64) if h == 0 else (lane_m >= 64)
                s = _dot(jnp.where(hm, qm_, 0.0).astype(BF16), km, NT) * NA_SCALE
                p = jnp.exp(s - jnp.max(s, axis=-1, keepdims=True))
                l = jnp.sum(p, axis=-1, keepdims=True)
                om.append(_dot(p.astype(BF16), vm) / l)
            o_ref[0:NM, :] = jnp.where(lane_m < 64, om[0], om[1]).astype(BF16)
            o_ref[L:T, :] = jnp.zeros((T - L, 128), BF16)
            lse_ref[0, 0:NM, :] = jnp.zeros((NM, 128), F32)
            lse_ref[0, L:T, :] = jnp.zeros((T - L, 128), F32)

    col = lambda off: pl.BlockSpec((T, 128), lambda hp, g: (0, off + hp))
    return pl.pallas_call(
        body, name="na_fwd", grid=(4, NA_GROUPS),
        in_specs=[col(0), col(4), col(8),
                  pl.BlockSpec((1, 2, NA_QB, NA_KU), lambda hp, g: (_na_var(g), hp, 0, 0))],
        out_specs=(pl.BlockSpec((T, 128), lambda hp, g: (0, hp)), pl.BlockSpec((1, T, 128), lambda hp, g: (hp, 0, 0))),
        out_shape=(_sds((T, 512), BF16), _sds((4, T, 128), F32)),
        scratch_shapes=[pltpu.VMEM((NA_KU, 128), BF16), pltpu.VMEM((NA_KU, 128), BF16)],
        compiler_params=_cp(("parallel", "arbitrary")))(p_act, p_act, p_act, bias_tab)


def _na_bwd(p_act, do, lse, bias_tab):
    def body(q_ref, k_ref, v_ref, do_ref, lse_ref, b_ref, dq_ref, dk_ref, dv_ref, db_ref, ku, vu):
        g = pl.program_id(1)

        @pl.when(g == 0)
        def _():
            dq_ref[...] = jnp.zeros((T, 128), F32)
            dk_ref[...] = jnp.zeros((T, 128), F32)
            dv_ref[...] = jnp.zeros((T, 128), F32)

        kstart = _na_load_window(k_ref, ku, g)
        _na_load_window(v_ref, vu, g)
        qstart = pl.multiple_of(NM + NA_QB * g, 16)
        q = q_ref[pl.ds(qstart, NA_QB), :]
        dov = do_ref[pl.ds(qstart, NA_QB), :]
        lsev = lse_ref[0, pl.ds(qstart, NA_QB), :]
        lane = lax.broadcasted_iota(jnp.int32, (NA_QB, 128), 1)
        first = (g == 0) | (g == 1) | (g == NA_GROUPS - 1)
        dq_h = []
        dku = jnp.zeros((NA_KU, 128), F32)
        dvu = jnp.zeros((NA_KU, 128), F32)
        for h in range(2):
            hm = (lane < 64) if h == 0 else (lane >= 64)
            qm = jnp.where(hm, q, 0.0).astype(BF16)
            dom = jnp.where(hm, dov, 0.0).astype(BF16)
            s = _dot(qm, ku[...], NT) * NA_SCALE + b_ref[0, h]
            p = jnp.exp(s - lsev[:, 64 * h:64 * h + 1])
            dp = _dot(dom, vu[...], NT)
            delta = jnp.sum(p * dp, axis=-1, keepdims=True)
            ds = p * (dp - delta)

            @pl.when(first)
            def _():
                db_ref[0, h] = ds

            @pl.when(jnp.logical_not(first))
            def _():
                db_ref[0, h] += ds

            dsb = (ds * NA_SCALE).astype(BF16)
            dq_h.append(_dot(dsb, ku[...]))
            dku = dku + _dot(dsb, qm, TN)
            dvu = dvu + _dot(p.astype(BF16), dom, TN)
        dq_ref[pl.ds(qstart, NA_QB), :] = jnp.where(lane < 64, dq_h[0], dq_h[1])
        dk_ref[pl.ds(kstart, NA_KW), :] += dku[0:NA_KW]
        dv_ref[pl.ds(kstart, NA_KW), :] += dvu[0:NA_KW]
        dk_ref[0:NM, :] += dku[NA_KW:NA_KW + NM]
        dv_ref[0:NM, :] += dvu[NA_KW:NA_KW + NM]

        @pl.when(g == 0)
        def _():
            qm_ = q_ref[0:NM, :]
            dom_ = do_ref[0:NM, :]
            lane_m = lax.broadcasted_iota(jnp.int32, (NM, 128), 1)
            km, vm = ku[NA_KW:NA_KW + NM, :], vu[NA_KW:NA_KW + NM, :]
            dqs = []
            dkm = jnp.zeros((NM, 128), F32)
            dvm = jnp.zeros((NM, 128), F32)
            for h in range(2):
                hm = (lane_m < 64) if h == 0 else (lane_m >= 64)
                qh = jnp.where(hm, qm_, 0.0).astype(BF16)
                doh = jnp.where(hm, dom_, 0.0).astype(BF16)
                s = _dot(qh, km, NT) * NA_SCALE
                e = jnp.exp(s - jnp.max(s, axis=-1, keepdims=True))
                p = e / jnp.sum(e, axis=-1, keepdims=True)
                dp = _dot(doh, vm, NT)
                ds = p * (dp - jnp.sum(p * dp, axis=-1, keepdims=True))
                dsb = (ds * NA_SCALE).astype(BF16)
                dqs.append(_dot(dsb, km))
                dkm = dkm + _dot(dsb, qh, TN)
                dvm = dvm + _dot(p.astype(BF16), doh, TN)
            dq_ref[0:NM, :] = jnp.where(lane_m < 64, dqs[0], dqs[1])
            dk_ref[0:NM, :] += dkm
            dv_ref[0:NM, :] += dvm

    col = lambda off: pl.BlockSpec((T, 128), lambda hp, g: (0, off + hp))
    ocol = pl.BlockSpec((T, 128), lambda hp, g: (0, hp))
    bspec = pl.BlockSpec((1, 2, NA_QB, NA_KU), lambda hp, g: (_na_var(g), hp, 0, 0))
    return pl.pallas_call(
        body, name="na_bwd", grid=(4, NA_GROUPS),
        in_specs=[col(0), col(4), col(8), ocol, pl.BlockSpec((1, T, 128), lambda hp, g: (hp, 0, 0)), bspec],
        out_specs=(ocol, ocol, ocol, bspec),
        out_shape=(_sds((T, 512), F32), _sds((T, 512), F32), _sds((T, 512), F32), _sds((3, NA_HEADS, NA_QB, NA_KU), F32)),
        scratch_shapes=[pltpu.VMEM((NA_KU, 128), BF16), pltpu.VMEM((NA_KU, 128), BF16)],
        compiler_params=_cp(("parallel", "arbitrary")))(p_act, p_act, p_act, do, lse, bias_tab)


def _na_rpb_reduce(dbias):
    def body(db_ref, o_ref):
        row = lax.broadcasted_iota(jnp.int32, (GRID_W, 128), 0)
        for a in range(15):
            acc = jnp.zeros((GRID_W, GRID_W), F32)
            for var in range(3):
                for i in range(4):
                    for j in range(NA_UROWS):
                        if _na_row_offset(var, i, j) == a:
                            pair = db_ref[var, 0, i * 64:(i + 1) * 64, (j // 2) * 128:(j // 2 + 1) * 128]
                            acc = acc + pair[:, (j % 2) * 64:(j % 2 + 1) * 64]
            z = jnp.concatenate([acc, jnp.zeros((GRID_W, 128 - GRID_W), F32)], axis=1)
            for bit in range(6):
                sh = 1 << bit
                z = jnp.where((row & sh) != 0, jnp.roll(z, 128 - sh, axis=1), z)
            z = jnp.roll(z, 15, axis=1)
            o_ref[0, a:a + 1, :] = jnp.sum(z, axis=0, keepdims=True)

    return pl.pallas_call(
        body, name="na_rpb_reduce", grid=(NA_HEADS,),
        in_specs=[pl.BlockSpec((3, 1, NA_QB, NA_KU), lambda h: (0, h, 0, 0))],
        out_specs=pl.BlockSpec((1, 15, 128), lambda h: (h, 0, 0)), out_shape=_sds((NA_HEADS, 15, 128), F32),
        compiler_params=_cp(("parallel",)))(dbias)


HG_RB = 128
HG_NB = T // HG_RB
HG_SLOTS = HG_NB * 8
HI = lax.Precision.HIGHEST


def _chunk_tri(lower):
    r = lax.broadcasted_iota(jnp.int32, (HG_RB, HG_RB), 0)
    c = lax.broadcasted_iota(jnp.int32, (HG_RB, HG_RB), 1)
    same = (r // HG_C) == (c // HG_C)
    keep = (c <= r) if lower else (c >= r)
    return jnp.where(same & keep, 1.0, 0.0).astype(F32)


def _hg_gate_terms(z, lg):
    dl = lg[0:1, :] - lg[1:2, :]
    log_lb = jax.nn.log_sigmoid(dl)
    log_1mlb = jax.nn.log_sigmoid(-dl)
    yz = log_1mlb + jax.nn.log_sigmoid(z)
    log_f = jnp.logaddexp(log_lb, yz)
    snz = jax.nn.sigmoid(-z)
    k = jnp.exp(log_1mlb) * snz
    w2 = jnp.exp(yz - log_f)
    return log_f, k, snz, w2


def _hg_pre(p_act, logits):
    def body(q_ref, zf_ref, zb_ref, lg_ref, qh_ref, kf_ref, bf_ref, kb_ref, bb_ref):
        qh_ref[...] = jax.nn.silu(q_ref[...])
        lf, kf, _, _ = _hg_gate_terms(zf_ref[...], lg_ref[0])
        kf_ref[...] = kf
        bf_ref[...] = jnp.dot(_chunk_tri(True), lf, precision=HI, preferred_element_type=F32)
        lb_, kb, _, _ = _hg_gate_terms(zb_ref[...], lg_ref[1])
        kb_ref[...] = kb
        bb_ref[...] = jnp.dot(_chunk_tri(False), lb_, precision=HI, preferred_element_type=F32)

    blk = lambda c: pl.BlockSpec((HG_RB, 512), lambda i: (i, c))
    ob = pl.BlockSpec((HG_RB, 512), lambda i: (i, 0))
    return pl.pallas_call(
        body, name="hg_pre", grid=(HG_NB,),
        in_specs=[blk(3), blk(4), blk(5), pl.BlockSpec((2, 2, 512), lambda i: (0, 0, 0))],
        out_specs=(ob,) * 5, out_shape=(_sds((T, 512), F32),) * 5,
        compiler_params=_cp(("parallel",)))(p_act, p_act, p_act, logits)


def _bdot(a, b, ca, cb):
    return lax.dot_general(a.astype(BF16), b.astype(BF16), (((ca,), (cb,)), ((0,), (0,))), preferred_element_type=F32)


def _hg_scan_fwd(qh, k, b, p_act, rev):
    anchor = 0 if rev else HG_C - 1

    def body(q_ref, k_ref, b_ref, v_ref, o_ref, st_ref, dsc):
        def phase_a(blk, _):
            rows = pl.ds(pl.multiple_of(blk * HG_RB, HG_RB), HG_RB)
            b3 = b_ref[rows, :].reshape(8, HG_C, 128)
            k3 = k_ref[rows, :].reshape(8, HG_C, 128)
            v3 = v_ref[rows, :].reshape(8, HG_C, 128)
            bl = b3[:, anchor:anchor + 1, :]
            kt = k3 * jnp.exp(bl - b3)
            st_ref[0, pl.ds(pl.multiple_of(blk * 8, 8), 8)] = _bdot(v3, kt, 1, 1)
            dsc[pl.ds(pl.multiple_of(blk * 8, 8), 8), :] = jnp.exp(bl[:, 0, :])
            return 0

        lax.fori_loop(0, HG_NB, phase_a, 0)

        def phase_b(n, carry):
            c = (NCHUNK - 1 - n) if rev else n
            u = st_ref[0, c]
            st_ref[0, c] = carry
            return carry * dsc[pl.ds(c, 1), :] + u

        lax.fori_loop(0, NCHUNK, phase_b, jnp.zeros((128, 128), F32))
        for c in range(NCHUNK, HG_SLOTS):
            st_ref[0, c] = jnp.zeros((128, 128), F32)

        t_io = lax.broadcasted_iota(jnp.int32, (8, HG_C, 128), 1)
        l_io = lax.broadcasted_iota(jnp.int32, (8, HG_C, HG_C), 2)

        def phase_c(blk, _):
            rows = pl.ds(pl.multiple_of(blk * HG_RB, HG_RB), HG_RB)
            b3 = b_ref[rows, :].reshape(8, HG_C, 128)
            k3 = k_ref[rows, :].reshape(8, HG_C, 128)
            q3 = q_ref[rows, :].reshape(8, HG_C, 128)
            v3 = v_ref[rows, :].reshape(8, HG_C, 128)
            st = st_ref[0, pl.ds(pl.multiple_of(blk * 8, 8), 8)]
            o = _bdot(q3 * jnp.exp(b3), st, 2, 2)
            a = jnp.zeros((8, HG_C, HG_C), F32)
            for s in range(HG_C):
                ok = (t_io <= s) if rev else (t_io >= s)
                f = jnp.exp(jnp.where(ok, b3 - b3[:, s:s + 1, :], NEG))
                col = jnp.sum(q3 * f * k3[:, s:s + 1, :], axis=-1, keepdims=True)
                a = a + jnp.where(l_io == s, col, 0.0)
            o = o + _bdot(a, v3, 2, 1)
            o_ref[rows, :] = o.reshape(HG_RB, 128)
            return 0

        lax.fori_loop(0, HG_NB, phase_c, 0)

    col = pl.BlockSpec((T, 128), lambda h: (0, h))
    return pl.pallas_call(
        body, name="hg_scan_bwd_dir" if rev else "hg_scan_fwd_dir", grid=(HG_HEADS,),
        in_specs=[col, col, col, pl.BlockSpec((T, 128), lambda h: (0, 24 + h))],
        out_specs=(col, pl.BlockSpec((1, HG_SLOTS, 128, 128), lambda h: (h, 0, 0, 0))),
        out_shape=(_sds((T, 512), F32), _sds((HG_HEADS, HG_SLOTS, 128, 128), F32)),
        scratch_shapes=[pltpu.VMEM((HG_SLOTS, 128), F32)],
        compiler_params=_cp(("parallel",), 56))(qh, k, b, p_act)


def _hg_scan_bwd(qh, k, b, p_act, st, do, rev):
    anchor = 0 if rev else HG_C - 1

    def body(q_ref, k_ref, b_ref, v_ref, st_ref, do_ref, dq_ref, dk_ref, db_ref, dv_ref, gst, dsc, dbl):
        def phase_a(blk, _):
            rows = pl.ds(pl.multiple_of(blk * HG_RB, HG_RB), HG_RB)
            b3 = b_ref[rows, :].reshape(8, HG_C, 128)
            q3 = q_ref[rows, :].reshape(8, HG_C, 128)
            do3 = do_ref[rows, :].reshape(8, HG_C, 128)
            gst[pl.ds(pl.multiple_of(blk * 8, 8), 8)] = _bdot(do3, q3 * jnp.exp(b3), 1, 1)
            dsc[pl.ds(pl.multiple_of(blk * 8, 8), 8), :] = jnp.exp(b3[:, anchor, :])
            return 0

        lax.fori_loop(0, HG_NB, phase_a, 0)

        def phase_b(n, carry):
            c = n if rev else (NCHUNK - 1 - n)
            w = gst[c]
            gst[c] = carry
            dcv = dsc[pl.ds(c, 1), :]
            dbl[pl.ds(c, 1), :] = dcv * jnp.sum(st_ref[0, c] * carry, axis=0, keepdims=True)
            return carry * dcv + w

        lax.fori_loop(0, NCHUNK, phase_b, jnp.zeros((128, 128), F32))
        for c in range(NCHUNK, HG_SLOTS):
            gst[c] = jnp.zeros((128, 128), F32)
            dbl[c:c + 1, :] = jnp.zeros((1, 128), F32)

        t_io = lax.broadcasted_iota(jnp.int32, (8, HG_C, 128), 1)
        r_io = lax.broadcasted_iota(jnp.int32, (8, HG_C, HG_C), 1)
        l_io = lax.broadcasted_iota(jnp.int32, (8, HG_C, HG_C), 2)

        def phase_c(blk, _):
            rows = pl.ds(pl.multiple_of(blk * HG_RB, HG_RB), HG_RB)
            cs = pl.ds(pl.multiple_of(blk * 8, 8), 8)
            b3 = b_ref[rows, :].reshape(8, HG_C, 128)
            k3 = k_ref[rows, :].reshape(8, HG_C, 128)
            q3 = q_ref[rows, :].reshape(8, HG_C, 128)
            v3 = v_ref[rows, :].reshape(8, HG_C, 128)
            do3 = do_ref[rows, :].reshape(8, HG_C, 128)
            s_t = st_ref[0, cs]
            g_t = gst[cs]
            bl = b3[:, anchor:anchor + 1, :]
            ekl = jnp.exp(bl - b3)
            kt = k3 * ekl
            dqt = _bdot(do3, s_t, 2, 1)
            dkt = _bdot(v3, g_t, 2, 1)
            dv = _bdot(kt, g_t, 2, 2)
            causal = (l_io >= r_io) if rev else (l_io <= r_io)
            da = jnp.where(causal, _bdot(do3, v3, 2, 2), 0.0)
            causal_t = (l_io <= r_io) if rev else (l_io >= r_io)
            dat = jnp.where(causal_t, _bdot(v3, do3, 2, 2), 0.0)
            dq = dqt * jnp.exp(b3)
            dk = dkt * ekl
            at = jnp.zeros((8, HG_C, HG_C), F32)
            for s in range(HG_C):
                ok = (t_io <= s) if rev else (t_io >= s)
                f = jnp.exp(jnp.where(ok, b3 - b3[:, s:s + 1, :], NEG))
                dq = dq + da[:, :, s:s + 1] * (f * k3[:, s:s + 1, :])
            for t in range(HG_C):
                ok = (t_io >= t) if rev else (t_io <= t)
                e = jnp.exp(jnp.where(ok, b3[:, t:t + 1, :] - b3, NEG))
                eq = e * q3[:, t:t + 1, :]
                dk = dk + dat[:, :, t:t + 1] * eq
                at = at + jnp.where(l_io == t, jnp.sum(eq * k3, axis=-1, keepdims=True), 0.0)
            dv = dv + _bdot(at, do3, 2, 1)
            dbl3 = dbl[cs, :].reshape(8, 1, 128) + jnp.sum(dkt * kt, axis=1, keepdims=True)
            db = q3 * dq - k3 * dk + jnp.where(t_io == anchor, dbl3, 0.0)
            dq_ref[rows, :] = dq.reshape(HG_RB, 128)
            dk_ref[rows, :] = dk.reshape(HG_RB, 128)
            db_ref[rows, :] = db.reshape(HG_RB, 128)
            dv_ref[rows, :] = dv.reshape(HG_RB, 128)
            return 0

        lax.fori_loop(0, HG_NB, phase_c, 0)

    col = pl.BlockSpec((T, 128), lambda h: (0, h))
    return pl.pallas_call(
        body, name="hg_scan_bwd_dir_bwd" if rev else "hg_scan_fwd_dir_bwd", grid=(HG_HEADS,),
        in_specs=[col, col, col, pl.BlockSpec((T, 128), lambda h: (0, 24 + h)),
                  pl.BlockSpec((1, HG_SLOTS, 128, 128), lambda h: (h, 0, 0, 0)), col],
        out_specs=(col,) * 4, out_shape=(_sds((T, 512), F32),) * 4,
        scratch_shapes=[pltpu.VMEM((HG_SLOTS, 128, 128), F32), pltpu.VMEM((HG_SLOTS, 128), F32),
                        pltpu.VMEM((HG_SLOTS, 128), F32)],
        compiler_params=_cp(("parallel",), 56))(qh, k, b, p_act, st, do)


def _row_valid(i, tm):
    r = lax.broadcasted_iota(jnp.int32, (tm, 1), 0) + i * tm
    return r < L


def _hg_post(o_f, o_b, p_act, gain):
    def body(of_ref, ob_ref, g_ref, gain_ref, u_ref):
        o = of_ref[...] + ob_ref[...]
        sg = jax.nn.silu(g_ref[...])
        parts = []
        for h in range(HG_HEADS):
            oh = o[:, 128 * h:128 * (h + 1)]
            parts.append(oh * lax.rsqrt(jnp.mean(oh * oh, axis=-1, keepdims=True) + EPS))
        n = jnp.concatenate(parts, axis=1)
        u = n * gain_ref[...] * sg
        u_ref[...] = jnp.where(_row_valid(pl.program_id(0), TM_E), u, 0.0).astype(BF16)

    blk = pl.BlockSpec((TM_E, 512), lambda i: (i, 0))
    return pl.pallas_call(
        body, name="hg_post", grid=(T // TM_E,),
        in_specs=[blk, blk, pl.BlockSpec((TM_E, 512), lambda i: (i, 7)), pl.BlockSpec((1, 512), lambda i: (0, 0))],
        out_specs=blk, out_shape=_sds((T, 512), BF16), compiler_params=_cp(("parallel",)))(o_f, o_b, p_act, gain)


def _hg_post_bwd(du, o_f, o_b, p_act, gain):
    def body(du_ref, of_ref, ob_ref, g_ref, gain_ref, do_ref, dg_ref, dgain_ref):
        i = pl.program_id(0)
        valid = _row_valid(i, TM_E)
        duv = jnp.where(valid, du_ref[...], 0.0)
        o = of_ref[...] + ob_ref[...]
        gv = g_ref[...]
        sig = jax.nn.sigmoid(gv)
        sg = gv * sig
        gain_v = gain_ref[...]
        dn = duv * gain_v * sg
        do_parts, n_parts = [], []
        for h in range(HG_HEADS):
            sl = slice(128 * h, 128 * (h + 1))
            oh = o[:, sl]
            r = lax.rsqrt(jnp.mean(oh * oh, axis=-1, keepdims=True) + EPS)
            nh = oh * r
            dnh = dn[:, sl]
            do_parts.append(r * (dnh - nh * jnp.mean(dnh * nh, axis=-1, keepdims=True)))
            n_parts.append(nh)
        n = jnp.where(valid, jnp.concatenate(n_parts, axis=1), 0.0)
        do_ref[...] = jnp.where(valid, jnp.concatenate(do_parts, axis=1), 0.0)
        dg_ref[...] = (duv * n * gain_v * (sig * (1.0 + gv * (1.0 - sig)))).astype(BF16)
        part = jnp.sum(duv * n * sg, axis=0, keepdims=True)

        @pl.when(i == 0)
        def _():
            dgain_ref[...] = part

        @pl.when(i > 0)
        def _():
            dgain_ref[...] += part

    blk = pl.BlockSpec((TM_E, 512), lambda i: (i, 0))
    vec = pl.BlockSpec((1, 512), lambda i: (0, 0))
    return pl.pallas_call(
        body, name="hg_post_bwd", grid=(T // TM_E,),
        in_specs=[blk, blk, blk, pl.BlockSpec((TM_E, 512), lambda i: (i, 7)), vec],
        out_specs=(blk, blk, vec), out_shape=(_sds((T, 512), F32), _sds((T, 512), BF16), _sds((1, 512), F32)),
        compiler_params=_cp(("arbitrary",)))(du, o_f, o_b, p_act, gain)


def _hg_pre_bwd(p_act, logits, dq_f, dq_b, dk_f, dk_b, db_f, db_b, dv_f, dv_b):
    def body(q_ref, zf_ref, zb_ref, lg_ref, dqf_ref, dqb_ref, dkf_ref, dkb_ref, dbf_ref, dbb_ref, dvf_ref, dvb_ref,
             dq_ref, dzf_ref, dzb_ref, di_ref, dlg_ref):
        i = pl.program_id(0)
        valid = _row_valid(i, HG_RB)
        qv = q_ref[...]
        sig = jax.nn.sigmoid(qv)
        dq_ref[...] = jnp.where(valid, (dqf_ref[...] + dqb_ref[...]) * (sig * (1.0 + qv * (1.0 - sig))), 0.0).astype(BF16)
        di_ref[...] = jnp.where(valid, dvf_ref[...] + dvb_ref[...], 0.0).astype(BF16)
        for d, (z_ref, dk_r, db_r, dz_ref) in enumerate(((zf_ref, dkf_ref, dbf_ref, dzf_ref), (zb_ref, dkb_ref, dbb_ref, dzb_ref))):
            lg = lg_ref[d]
            dl = lg[0:1, :] - lg[1:2, :]
            lb = jax.nn.sigmoid(dl)
            one_m_lb = jax.nn.sigmoid(-dl)
            log_f, _, snz, w2 = _hg_gate_terms(z_ref[...], lg)
            dbv = jnp.where(valid, db_r[...], 0.0)
            dkv = jnp.where(valid, dk_r[...], 0.0)
            dlf = jnp.dot(_chunk_tri(d == 1), dbv, precision=HI, preferred_element_type=F32)
            sz = 1.0 - snz
            dz_ref[...] = (dlf * w2 * snz - dkv * one_m_lb * sz * snz).astype(BF16)
            dlb = jnp.sum(dlf * snz * jnp.exp(-log_f) - dkv * snz, axis=0, keepdims=True)
            dl0 = dlb * lb * one_m_lb
            part = jnp.concatenate([dl0, -dl0], axis=0)

            @pl.when(i == 0)
            def _():
                dlg_ref[d] = part

            @pl.when(i > 0)
            def _():
                dlg_ref[d] += part

    blk = lambda c: pl.BlockSpec((HG_RB, 512), lambda i: (i, c))
    ob = pl.BlockSpec((HG_RB, 512), lambda i: (i, 0))
    lgs = pl.BlockSpec((2, 2, 512), lambda i: (0, 0, 0))
    return pl.pallas_call(
        body, name="hg_pre_bwd", grid=(HG_NB,),
        in_specs=[blk(3), blk(4), blk(5), lgs] + [ob] * 8,
        out_specs=(ob, ob, ob, ob, lgs),
        out_shape=(_sds((T, 512), BF16),) * 4 + (_sds((2, 2, 512), F32),),
        compiler_params=_cp(("arbitrary",)))(p_act, p_act, p_act, logits, dq_f, dq_b, dk_f, dk_b, db_f, db_b, dv_f, dv_b)


def _mix_fwd(o_na, u_hg, wna_g, whg_g, p_act):
    def body(ona_ref, uhg_ref, wna_ref, whg_ref, gna_ref, ghg_ref, o_ref):
        y_na = _dot(ona_ref[...], wna_ref[0])
        y_hg = _dot(uhg_ref[...], whg_ref[0])
        o_ref[...] = (jax.nn.sigmoid(gna_ref[...]) * y_na + jax.nn.sigmoid(ghg_ref[...]) * y_hg).astype(BF16)

    act = pl.BlockSpec((TM_MM, 512), lambda i, j: (i, 0))
    wsp = pl.BlockSpec((1, 512, 128), lambda i, j: (j, 0, 0))
    return pl.pallas_call(
        body, name="mix_fwd", grid=(T // TM_MM, NDEV),
        in_specs=[act, act, wsp, wsp, pl.BlockSpec((TM_MM, 128), lambda i, j: (i, 32 + j)),
                  pl.BlockSpec((TM_MM, 128), lambda i, j: (i, 40 + j))],
        out_specs=pl.BlockSpec((TM_MM, 128), lambda i, j: (i, j)), out_shape=_sds((T, D), BF16),
        compiler_params=_cp(("parallel", "parallel")))(o_na, u_hg, wna_g, whg_g, p_act, p_act)


def _mix_bwd(o_na, u_hg, wna_g, whg_g, p_act, dmix):
    ni = T // TM_B

    def body(ona_ref, uhg_ref, wna_ref, whg_ref, gna_ref, ghg_ref, dmix_ref,
             dgna_ref, dghg_ref, dwna_ref, dwhg_ref, dona_ref, duhg_ref, acc_na, acc_hg):
        j, i = pl.program_id(0), pl.program_id(1)
        rows = pl.ds(pl.multiple_of(i * TM_B, TM_B), TM_B)
        dm = dmix_ref[...].astype(F32)
        for x_ref, w_ref, g_ref, dg_ref, dx_ref, dw_ref, acc in (
                (ona_ref, wna_ref, gna_ref, dgna_ref, dona_ref, dwna_ref, acc_na),
                (uhg_ref, whg_ref, ghg_ref, dghg_ref, duhg_ref, dwhg_ref, acc_hg)):
            xv = x_ref[...]
            y = _dot(xv, w_ref[0])
            sg = jax.nn.sigmoid(g_ref[...])
            dg_ref[...] = (dm * y * sg * (1.0 - sg)).astype(BF16)
            dy = (dm * sg).astype(BF16)
            part = _dot(xv, dy, TN)
            dxv = _dot(dy, w_ref[0], NT)

            @pl.when(i == 0)
            def _():
                acc[...] = part

            @pl.when(i > 0)
            def _():
                acc[...] += part

            @pl.when(i == ni - 1)
            def _():
                dw_ref[0] = acc[...].astype(BF16)

            @pl.when(j == 0)
            def _():
                dx_ref[rows, :] = dxv

            @pl.when(j > 0)
            def _():
                dx_ref[rows, :] += dxv

    act = pl.BlockSpec((TM_B, 512), lambda j, i: (i, 0))
    wsp = pl.BlockSpec((1, 512, 128), lambda j, i: (j, 0, 0))
    cblk = pl.BlockSpec((TM_B, 128), lambda j, i: (i, j))
    full = pl.BlockSpec((T, 512), lambda j, i: (0, 0))
    return pl.pallas_call(
        body, name="mix_bwd", grid=(NDEV, ni),
        in_specs=[act, act, wsp, wsp, pl.BlockSpec((TM_B, 128), lambda j, i: (i, 32 + j)),
                  pl.BlockSpec((TM_B, 128), lambda j, i: (i, 40 + j)), cblk],
        out_specs=(cblk, cblk, wsp, wsp, full, full),
        out_shape=(_sds((T, D), BF16), _sds((T, D), BF16), _sds((NDEV, 512, 128), BF16), _sds((NDEV, 512, 128), BF16),
                   _sds((T, 512), F32), _sds((T, 512), F32)),
        scratch_shapes=[pltpu.VMEM((512, 128), F32), pltpu.VMEM((512, 128), F32)],
        compiler_params=_cp(("arbitrary", "arbitrary")))(o_na, u_hg, wna_g, whg_g, p_act, p_act, dmix)


def _wo_fwd(mix, w_o, h0, g_mlp):
    def body(mix_ref, w_ref, h0_ref, g_ref, h1_ref, m_ref):
        h1 = h0_ref[...] + _dot(mix_ref[...], w_ref[...])
        h1_ref[...] = h1
        r = lax.rsqrt(jnp.mean(h1 * h1, axis=-1, keepdims=True) + EPS)
        m_ref[...] = (h1 * r * g_ref[...]).astype(BF16)

    blk = pl.BlockSpec((TM_B, D), lambda i: (i, 0))
    return pl.pallas_call(
        body, name="wo_fwd", grid=(T // TM_B,),
        in_specs=[blk, pl.BlockSpec((D, D), lambda i: (0, 0)), blk, pl.BlockSpec((1, D), lambda i: (0, 0))],
        out_specs=(blk, blk), out_shape=(_sds((T, D), F32), _sds((T, D), BF16)),
        compiler_params=_cp(("parallel",)))(mix, w_o, h0, g_mlp)


def _wo_bwd(dh1_b, w_o, mix):
    ni = T // TM_B

    def body(dh_ref, w_ref, mix_ref, dmix_ref, dw_ref, acc):
        i = pl.program_id(0)
        dh = dh_ref[...]
        dmix_ref[...] = _dot(dh, w_ref[...], NT).astype(BF16)
        part = _dot(mix_ref[...], dh, TN)

        @pl.when(i == 0)
        def _():
            acc[...] = part

        @pl.when(i > 0)
        def _():
            acc[...] += part

        @pl.when(i == ni - 1)
        def _():
            dw_ref[...] = acc[...].astype(BF16)

    blk = pl.BlockSpec((TM_B, D), lambda i: (i, 0))
    wsp = pl.BlockSpec((D, D), lambda i: (0, 0))
    return pl.pallas_call(
        body, name="wo_bwd", grid=(ni,), in_specs=[blk, wsp, blk], out_specs=(blk, wsp),
        out_shape=(_sds((T, D), BF16), _sds((D, D), BF16)), scratch_shapes=[pltpu.VMEM((D, D), F32)],
        compiler_params=_cp(("arbitrary",)))(dh1_b, w_o, mix)


FF_B = D_FF // NDEV


def _mlp_fwd(m, wup_g, wdown_g, h1):
    def body(m_ref, wu_ref, wd_ref, h1_ref, h2_ref):
        j = pl.program_id(1)
        up = jnp.maximum(_dot(m_ref[...], wu_ref[0]), 0.0)
        part = _dot((up * up).astype(BF16), wd_ref[0])

        @pl.when(j == 0)
        def _():
            h2_ref[...] = h1_ref[...] + part

        @pl.when(j > 0)
        def _():
            h2_ref[...] += part

    blk = pl.BlockSpec((TM_MM, D), lambda i, j: (i, 0))
    return pl.pallas_call(
        body, name="mlp_fwd", grid=(T // TM_MM, NDEV),
        in_specs=[blk, pl.BlockSpec((1, D, FF_B), lambda i, j: (j, 0, 0)), pl.BlockSpec((1, FF_B, D), lambda i, j: (j, 0, 0)), blk],
        out_specs=blk, out_shape=_sds((T, D), F32),
        compiler_params=_cp(("parallel", "arbitrary")))(m, wup_g, wdown_g, h1)


def _mlp_bwd(m, dh2_b, wup_g, wdown_g):
    ni = T // TM_B

    def body(m_ref, dh_ref, wu_ref, wd_ref, dwu_ref, dwd_ref, dm_ref, acc_u, acc_d):
        j, i = pl.program_id(0), pl.program_id(1)
        rows = pl.ds(pl.multiple_of(i * TM_B, TM_B), TM_B)
        mv, dh = m_ref[...], dh_ref[...]
        r = jnp.maximum(_dot(mv, wu_ref[0]), 0.0)
        act = (r * r).astype(BF16)
        dact = _dot(dh, wd_ref[0], NT)
        dup = (dact * (2.0 * r)).astype(BF16)
        pd = _dot(act, dh, TN)
        pu = _dot(mv, dup, TN)
        dmv = _dot(dup, wu_ref[0], NT)

        @pl.when(i == 0)
        def _():
            acc_u[...] = pu
            acc_d[...] = pd

        @pl.when(i > 0)
        def _():
            acc_u[...] += pu
            acc_d[...] += pd

        @pl.when(i == ni - 1)
        def _():
            dwu_ref[0] = acc_u[...].astype(BF16)
            dwd_ref[0] = acc_d[...].astype(BF16)

        @pl.when(j == 0)
        def _():
            dm_ref[rows, :] = dmv

        @pl.when(j > 0)
        def _():
            dm_ref[rows, :] += dmv

    blk = pl.BlockSpec((TM_B, D), lambda j, i: (i, 0))
    wus = pl.BlockSpec((1, D, FF_B), lambda j, i: (j, 0, 0))
    wds = pl.BlockSpec((1, FF_B, D), lambda j, i: (j, 0, 0))
    return pl.pallas_call(
        body, name="mlp_bwd", grid=(NDEV, ni), in_specs=[blk, blk, wus, wds],
        out_specs=(wus, wds, pl.BlockSpec((T, D), lambda j, i: (0, 0))),
        out_shape=(_sds((NDEV, D, FF_B), BF16), _sds((NDEV, FF_B, D), BF16), _sds((T, D), F32)),
        scratch_shapes=[pltpu.VMEM((D, FF_B), F32), pltpu.VMEM((FF_B, D), F32)],
        compiler_params=_cp(("arbitrary", "arbitrary")))(m, dh2_b, wup_g, wdown_g)


def _loss_head(h2, g_final, tgt):
    def body(h_ref, g_ref, t_ref, loss_ref, dh_ref, dhb_ref, dg_ref):
        i = pl.program_id(0)
        r_io = lax.broadcasted_iota(jnp.int32, (TM_E, 1), 0) + i * TM_E
        valid = (r_io >= NM) & (r_io < L)
        xv = h_ref[...]
        r = lax.rsqrt(jnp.mean(xv * xv, axis=-1, keepdims=True) + EPS)
        xh = xv * r
        gv = g_ref[...]
        err = jnp.where(valid, xh * gv - t_ref[...], 0.0)
        lpart = jnp.broadcast_to(0.5 * jnp.sum(jnp.sum(err * err, axis=-1, keepdims=True) * (1.0 / D), axis=0, keepdims=True), (1, 128))
        dy = err * (1.0 / D)
        dxh = dy * gv
        dh = r * (dxh - xh * jnp.mean(dxh * xh, axis=-1, keepdims=True))
        dh_ref[...] = dh
        dhb_ref[...] = dh.astype(BF16)
        gpart = jnp.sum(dy * xh, axis=0, keepdims=True)

        @pl.when(i == 0)
        def _():
            loss_ref[...] = lpart
            dg_ref[...] = gpart

        @pl.when(i > 0)
        def _():
            loss_ref[...] += lpart
            dg_ref[...] += gpart

    blk = pl.BlockSpec((TM_E, D), lambda i: (i, 0))
    vec = pl.BlockSpec((1, D), lambda i: (0, 0))
    return pl.pallas_call(
        body, name="loss_head", grid=(T // TM_E,), in_specs=[blk, vec, blk],
        out_specs=(pl.BlockSpec((1, 128), lambda i: (0, 0)), blk, blk, vec),
        out_shape=(_sds((1, 128), F32), _sds((T, D), F32), _sds((T, D), BF16), _sds((1, D), F32)),
        compiler_params=_cp(("arbitrary",)))(h2, g_final, tgt)


def _adamw(parts, w, m, v, name):
    rr, cc = w.shape
    tr = rr
    for cand in (256, 128, 64):
        if rr % cand == 0 and rr > cand:
            tr = cand
            break
    c1 = 1.0 - ADAM_B1 ** ADAM_STEP
    c2 = 1.0 - ADAM_B2 ** ADAM_STEP

    def body(p_ref, w_ref, m_ref, v_ref, g_ref, d_ref, nm_ref, nv_ref):
        g = p_ref[0].astype(F32)
        for s in range(1, NDEV):
            g = g + p_ref[s].astype(F32)
        mn = ADAM_B1 * m_ref[...] + (1.0 - ADAM_B1) * g
        vn = ADAM_B2 * v_ref[...] + (1.0 - ADAM_B2) * (g * g)
        g_ref[...] = g
        nm_ref[...] = mn
        nv_ref[...] = vn
        d_ref[...] = -ADAM_LR * ((mn / c1) / (jnp.sqrt(vn / c2) + ADAM_EPS) + ADAM_WD * w_ref[...])

    blk = pl.BlockSpec((tr, cc), lambda i: (i, 0))
    return pl.pallas_call(
        body, name=name, grid=(rr // tr,),
        in_specs=[pl.BlockSpec((NDEV, tr, cc), lambda i: (0, i, 0)), blk, blk, blk],
        out_specs=(blk,) * 4, out_shape=(_sds((rr, cc), F32),) * 4,
        compiler_params=_cp(("parallel",)))(parts, w, m, v)


RPB_N = NA_HEADS * 15 * 31
RPB_PAD = 4096
OWN_ROWS = NM + 8


def _pad_rows(a, rows):
    return jnp.pad(a, ((0, rows - a.shape[0]),) + ((0, 0),) * (a.ndim - 1))


def _pack_owned(meta_blk, lb_blk):
    return jnp.concatenate([meta_blk, _pad_rows(lb_blk.reshape(2, 128), 8)], axis=0)


def _pack_replicated(n_mix, n_mlp, n_final, hg_gain, rpb):
    flat = _pad_rows(rpb.reshape(RPB_N), RPB_PAD)
    return jnp.concatenate([n_mix.reshape(8, 128), n_mlp.reshape(8, 128), n_final.reshape(8, 128),
                            _pad_rows(hg_gain.reshape(4, 128), 8), flat.reshape(32, 128)], axis=0)


def _unpack_replicated(a):
    return (a[0:8].reshape(1, D), a[8:16].reshape(1, D), a[16:24].reshape(D), a[24:28].reshape(1, 512),
            a[32:64].reshape(RPB_PAD)[:RPB_N].reshape(1, NA_HEADS, 15, 31))


def kernel(x, meta_tokens, w_in, w_na_out, w_hg_out, w_o, w_up, w_down, norm_mix, norm_mlp, norm_final, hg_norm, na_rpb, hg_lb_logits, loss_target, m_meta_tokens, m_w_in, m_w_na_out, m_w_hg_out, m_w_o, m_w_up, m_w_down, m_norm_mix, m_norm_mlp, m_norm_final, m_hg_norm, m_na_rpb, m_hg_lb_logits, v_meta_tokens, v_w_in, v_w_na_out, v_w_hg_out, v_w_o, v_w_up, v_w_down, v_norm_mix, v_norm_mlp, v_norm_final, v_hg_norm, v_na_rpb, v_hg_lb_logits):
    owned = _pack_owned(meta_tokens, hg_lb_logits)
    win_g, owned_g = _exchange([w_in[0].astype(BF16), owned], [False] * 2, "gather_first")
    later = [_tie(w[0].astype(BF16), owned_g) for w in (w_na_out, w_hg_out, w_o, w_up, w_down)]
    gather_rest, tok = _exchange_start(later, [False] * 5, "gather_rest_start")
    win_g = _tie(win_g, tok)
    meta_full = jnp.transpose(owned_g[:, 0:NM, :], (1, 0, 2)).reshape(NM, D)
    logits = jnp.transpose(owned_g[:, NM:NM + 2, :].reshape(NDEV, 2, 2, 64), (1, 2, 0, 3)).reshape(2, 2, 512)

    h0 = jnp.concatenate([meta_full, x[0], jnp.zeros((T - L, D), F32)], axis=0)
    tgt = jnp.concatenate([jnp.zeros((NM, D), F32), loss_target[0], jnp.zeros((T - L, D), F32)], axis=0)
    bias_tab = _na_bias_table(na_rpb[0])

    a = _norm_fwd(h0, norm_mix, "norm_mix_fwd")
    p_act = _inproj_fwd(a, win_g)
    o_na, lse = _na_fwd(p_act, bias_tab)
    qh, k_f, b_f, k_b, b_b = _hg_pre(p_act, logits)
    o_f, st_f = _hg_scan_fwd(qh, k_f, b_f, p_act, False)
    o_b, st_b = _hg_scan_fwd(qh, k_b, b_b, p_act, True)
    u_hg = _hg_post(o_f, o_b, p_act, hg_norm)
    wna_g, whg_g, wo_g, wup_g, wdown_g = _exchange_wait(gather_rest, [False] * 5, u_hg, "gather_rest_wait")
    w_o_full = wo_g.reshape(D, D)
    mix = _mix_fwd(o_na, u_hg, wna_g, whg_g, p_act)
    h1, m_act = _wo_fwd(mix, w_o_full, h0, norm_mlp)
    h2 = _mlp_fwd(m_act, wup_g, wdown_g, h1)
    loss_part, dh2, dh2_b, d_nfinal = _loss_head(h2, norm_final.reshape(1, D), tgt)

    dwup_p, dwdown_p, dm = _mlp_bwd(m_act, dh2_b, wup_g, wdown_g)
    sc_mlp, tok = _exchange_start([dwup_p, dwdown_p], [True] * 2, "scatter_mlp_start")
    dh1, dh1_b, d_nmlp = _norm_bwd(h1, norm_mlp, _tie(dm, tok), dh2, "norm_mlp_bwd")
    dmix, dwo = _wo_bwd(dh1_b, w_o_full, mix)
    sc_wo, tok = _exchange_start([dwo.reshape(NDEV, D // NDEV, D)], [True], "scatter_wo_start")
    dgna, dghg, dwna_p, dwhg_p, do_na, du_hg = _mix_bwd(o_na, u_hg, wna_g, whg_g, p_act, _tie(dmix, tok))
    sc_br, tok = _exchange_start([dwna_p, dwhg_p], [True] * 2, "scatter_branch_start")
    du_hg = _tie(du_hg, tok)
    do_hg, dg_hg, d_gain = _hg_post_bwd(du_hg, o_f, o_b, p_act, hg_norm)
    dq_f, dk_f, db_f, dv_f = _hg_scan_bwd(qh, k_f, b_f, p_act, st_f, do_hg, False)
    dq_b, dk_b, db_b, dv_b = _hg_scan_bwd(qh, k_b, b_b, p_act, st_b, do_hg, True)
    dq_hg, dz_f, dz_b, di_hg, d_logits = _hg_pre_bwd(p_act, logits, dq_f, dq_b, dk_f, dk_b, db_f, db_b, dv_f, dv_b)
    dq_na, dk_na, dv_na, dbias = _na_bwd(p_act, do_na, lse, bias_tab)
    d_rpb = _na_rpb_reduce(dbias)[:, :, :31]
    dp = jnp.concatenate([dq_na.astype(BF16), dk_na.astype(BF16), dv_na.astype(BF16), dq_hg, dz_f, dz_b, di_hg, dg_hg,
                          dgna, dghg], axis=1)
    dwin_p, da = _inproj_bwd(a, dp, win_g)
    dh0, _, d_nmix = _norm_bwd(h0, norm_mix, da, dh1, "norm_mix_bwd")

    d_meta = jnp.transpose(dh0[0:NM].reshape(NM, NDEV, 128), (1, 0, 2))
    d_lg = jnp.transpose(d_logits.reshape(2, 2, NDEV, 64), (2, 0, 1, 3)).reshape(NDEV, 2, 128)
    owned_p = jnp.concatenate([d_meta, jnp.pad(d_lg, ((0, 0), (0, OWN_ROWS - NM - 2), (0, 0)))], axis=1)
    repl_p = _pack_replicated(d_nmix, d_nmlp, d_nfinal, d_gain, d_rpb)
    win_r, owned_r, repl_r = _exchange([dwin_p, owned_p, repl_p], [True, True, False], "scatter_last")
    wup_r, wdown_r = _exchange_wait(sc_mlp, [True] * 2, win_r, "scatter_mlp_wait")
    (wo_r,) = _exchange_wait(sc_wo, [True], wup_r, "scatter_wo_wait")
    wna_r, whg_r = _exchange_wait(sc_br, [True] * 2, wo_r, "scatter_branch_wait")

    res = {}
    for nm, parts, w, mm, vv in (
            ("w_in", win_r, w_in, m_w_in, v_w_in), ("w_na_out", wna_r, w_na_out, m_w_na_out, v_w_na_out),
            ("w_hg_out", whg_r, w_hg_out, m_w_hg_out, v_w_hg_out), ("w_o", wo_r, w_o, m_w_o, v_w_o),
            ("w_up", wup_r, w_up, m_w_up, v_w_up), ("w_down", wdown_r, w_down, m_w_down, v_w_down)):
        res[nm] = [r[None] for r in _adamw(parts, w[0], mm[0], vv[0], "adamw_" + nm)]
    own = _adamw(owned_r, owned, _pack_owned(m_meta_tokens, m_hg_lb_logits), _pack_owned(v_meta_tokens, v_hg_lb_logits),
                 "adamw_owned_small")
    res["meta_tokens"] = [r[0:NM] for r in own]
    res["hg_lb_logits"] = [r[NM:NM + 2].reshape(2, 2, 64) for r in own]
    rep = _adamw(repl_r, _pack_replicated(norm_mix, norm_mlp, norm_final, hg_norm, na_rpb),
                 _pack_replicated(m_norm_mix, m_norm_mlp, m_norm_final, m_hg_norm, m_na_rpb),
                 _pack_replicated(v_norm_mix, v_norm_mlp, v_norm_final, v_hg_norm, v_na_rpb), "adamw_replicated")
    for q in range(4):
        um = _unpack_replicated(rep[q])
        for nm, val in zip(("norm_mix", "norm_mlp", "norm_final", "hg_norm", "na_rpb"), um):
            res.setdefault(nm, [None] * 4)[q] = val

    loss = lax.psum(loss_part[0, 0], ("x", "y", "c"))
    grad_x = dh0[NM:L][None]
    order = ("meta_tokens", "w_in", "w_na_out", "w_hg_out", "w_o", "w_up", "w_down", "norm_mix", "norm_mlp", "norm_final",
             "hg_norm", "na_rpb", "hg_lb_logits")
    outs = [loss, grad_x]
    for q in range(4):
        outs += [res[nm][q] for nm in order]
    return tuple(outs)
```

```python
import functools

import numpy as np
import jax
import jax.numpy as jnp
from jax import lax
from jax.experimental import pallas as pl
from jax.experimental.pallas import tpu as pltpu

F32 = jnp.float32
BF16 = jnp.bfloat16

D = 1024
SEQ = 2048
NM = 16
L = SEQ + NM
T = 2176
NDEV = 8
EPS = 1e-6
GRID_W = 64
ROWS = SEQ // GRID_W
NA_HEADS = 8
NA_DH = 64
NA_SCALE = NA_DH ** -0.5
HG_HEADS = 4
HG_C = 16
NCHUNK = L // HG_C
D_FF = 4096
IN_COLS = 6144
NEG = -1e30

ADAM_LR = 0.001
ADAM_B1 = 0.9
ADAM_B2 = 0.999
ADAM_EPS = 1e-08
ADAM_WD = 0.01
ADAM_STEP = 10

MESH_ID = pl.DeviceIdType.MESH
ANY = pl.BlockSpec(memory_space=pl.ANY)

NN = (((1,), (0,)), ((), ()))
NT = (((1,), (1,)), ((), ()))
TN = (((0,), (0,)), ((), ()))


def _cp(sem=None, vmem_mb=48):
    return pltpu.CompilerParams(dimension_semantics=sem, vmem_limit_bytes=vmem_mb * 1024 * 1024)


def _dot(a, b, dims=NN):
    return lax.dot_general(a, b, dims, preferred_element_type=F32)


def _sds(shape, dtype):
    return jax.ShapeDtypeStruct(shape, dtype)


HBM = pl.BlockSpec(memory_space=pltpu.HBM)
SEM = pl.BlockSpec(memory_space=pltpu.SEMAPHORE)
EFFECT = pltpu.SideEffectType.DATAFLOW_SIDE_EFFECTING


def _exchange(arrs, scatter, name):
    n = len(arrs)
    out_shapes = []
    for a, sc in zip(arrs, scatter):
        out_shapes.append(_sds(a.shape if sc else (NDEV,) + a.shape, a.dtype))

    def body(*refs):
        ins, outs = refs[:n], refs[n:2 * n]
        send_sems, recv_sems, loc_sems = refs[2 * n:]
        me = 4 * lax.axis_index("x") + 2 * lax.axis_index("y") + lax.axis_index("c")
        copies = []
        for k in range(n):
            src_me = ins[k].at[me] if scatter[k] else ins[k]
            loc = pltpu.make_async_copy(src_me, outs[k].at[me], loc_sems.at[k])
            loc.start()
            copies.append(loc)
        remote = _peer_copies(ins, outs, scatter, send_sems, recv_sems)
        for cp in remote:
            cp.start()
        for cp in remote:
            cp.wait_recv()
        for cp in remote:
            cp.wait_send()
        for cp in copies:
            cp.wait()

    return pl.pallas_call(
        body, name=name, out_shape=tuple(out_shapes), in_specs=[ANY] * n, out_specs=tuple([ANY] * n),
        scratch_shapes=[pltpu.SemaphoreType.DMA((n * (NDEV - 1),)), pltpu.SemaphoreType.DMA((n * (NDEV - 1),)),
                        pltpu.SemaphoreType.DMA((n,))],
    )(*arrs)


def _peer_copies(srcs, lands, scatter, send_sems, recv_sems):
    x, y, c = lax.axis_index("x"), lax.axis_index("y"), lax.axis_index("c")
    me = 4 * x + 2 * y + c
    out = []
    for k in range(len(srcs)):
        for m in range(1, NDEV):
            px, py, pc = x ^ (m >> 2), y ^ ((m >> 1) & 1), c ^ (m & 1)
            src = srcs[k].at[4 * px + 2 * py + pc] if scatter[k] else srcs[k]
            out.append(pltpu.make_async_remote_copy(
                src_ref=src, dst_ref=lands[k].at[me], send_sem=send_sems.at[k * (NDEV - 1) + m - 1],
                recv_sem=recv_sems.at[k * (NDEV - 1) + m - 1],
                device_id=(px, py, pc), device_id_type=MESH_ID))
    return out


def _exchange_start(arrs, scatter, name):
    n = len(arrs)
    me = 4 * lax.axis_index("x") + 2 * lax.axis_index("y") + lax.axis_index("c")
    lands = []
    for a, sc in zip(arrs, scatter):
        own = lax.dynamic_index_in_dim(a, me, 0, keepdims=True) if sc else a[None]
        shape = a.shape if sc else (NDEV,) + a.shape
        lands.append(lax.dynamic_update_index_in_dim(lax.empty(shape, a.dtype), own, me, 0))

    def body(*refs):
        srcs, lnds = refs[:n], refs[n:2 * n]
        send_sems, recv_sems = refs[2 * n], refs[2 * n + 1]
        token = refs[-1]
        for cp in _peer_copies(srcs, lnds, scatter, send_sems, recv_sems):
            cp.start()
        token[...] = jnp.zeros_like(token)

    ops = [pltpu.with_memory_space_constraint(a, pltpu.HBM) for a in list(arrs) + lands]
    res = pl.pallas_call(
        body, name=name,
        out_shape=(pltpu.SemaphoreType.DMA((n * (NDEV - 1),)), pltpu.SemaphoreType.DMA((n * (NDEV - 1),)))
        + tuple(pltpu.HBM(o.shape, o.dtype) for o in ops) + (_sds((8, 128), F32),),
        in_specs=[HBM] * (2 * n), out_specs=(SEM, SEM) + (HBM,) * (2 * n) + (pl.BlockSpec(memory_space=pltpu.VMEM),),
        input_output_aliases={k: 2 + k for k in range(2 * n)},
        compiler_params=pltpu.CompilerParams(has_side_effects=EFFECT),
    )(*ops)
    return res[:-1], res[-1]


def _exchange_wait(handle, scatter, after, name):
    send_sems, recv_sems = handle[0], handle[1]
    bufs = handle[2:]
    n = len(bufs) // 2
    after = list(after)

    def body(*refs):
        srcs, lnds = refs[:n], refs[n:2 * n]
        for cp in _peer_copies(srcs, lnds, scatter, refs[2 * n], refs[2 * n + 1]):
            cp.wait_send()
            cp.wait_recv()

    res = pl.pallas_call(
        body, name=name, out_shape=tuple(pltpu.HBM(b.shape, b.dtype) for b in bufs),
        in_specs=[HBM] * (2 * n) + [SEM, SEM] + [ANY] * len(after), out_specs=(HBM,) * (2 * n),
        input_output_aliases={k: k for k in range(2 * n)},
        compiler_params=pltpu.CompilerParams(has_side_effects=EFFECT),
    )(*bufs, send_sems, recv_sems, *after)
    return res[n:]


def _tie(x, token, name):
    def body(x_ref, t_ref, o_ref):
        del x_ref, t_ref, o_ref

    return pl.pallas_call(body, name=name, out_shape=_sds(x.shape, x.dtype), in_specs=[ANY, ANY], out_specs=ANY,
                          input_output_aliases={0: 0})(x, token)


TM_E = 272


def _norm_fwd(h, g, name):
    def body(h_ref, g_ref, o_ref):
        xv = h_ref[...]
        r = lax.rsqrt(jnp.mean(xv * xv, axis=-1, keepdims=True) + EPS)
        o_ref[...] = (xv * r * g_ref[...]).astype(BF16)

    return pl.pallas_call(
        body, name=name, grid=(T // TM_E,),
        in_specs=[pl.BlockSpec((TM_E, D), lambda i: (i, 0)), pl.BlockSpec((1, D), lambda i: (0, 0))],
        out_specs=pl.BlockSpec((TM_E, D), lambda i: (i, 0)), out_shape=_sds((T, D), BF16),
        compiler_params=_cp(("parallel",)))(h, g)


def _norm_bwd(h, g, dn, dres, name):
    def body(h_ref, g_ref, dn_ref, dres_ref, dh_ref, dhb_ref, dg_ref):
        i = pl.program_id(0)
        xv = h_ref[...]
        r = lax.rsqrt(jnp.mean(xv * xv, axis=-1, keepdims=True) + EPS)
        xh = xv * r
        dnv = dn_ref[...].astype(F32)
        dxh = dnv * g_ref[...]
        dh = dres_ref[...] + r * (dxh - xh * jnp.mean(dxh * xh, axis=-1, keepdims=True))
        dh_ref[...] = dh
        dhb_ref[...] = dh.astype(BF16)
        part = jnp.sum(dnv * xh, axis=0, keepdims=True)

        @pl.when(i == 0)
        def _():
            dg_ref[...] = part

        @pl.when(i > 0)
        def _():
            dg_ref[...] += part

    blk = pl.BlockSpec((TM_E, D), lambda i: (i, 0))
    vec = pl.BlockSpec((1, D), lambda i: (0, 0))
    return pl.pallas_call(
        body, name=name, grid=(T // TM_E,), in_specs=[blk, vec, blk, blk], out_specs=(blk, blk, vec),
        out_shape=(_sds((T, D), F32), _sds((T, D), BF16), _sds((1, D), F32)),
        compiler_params=_cp(("arbitrary",)))(h, g, dn, dres)


TM_MM = 1088


def _inproj_fwd(a, w_g):
    nb = w_g.shape[2]

    def body(a_ref, w_ref, o_ref):
        o_ref[...] = _dot(a_ref[...], w_ref[0])

    return pl.pallas_call(
        body, name="inproj_fwd", grid=(T // TM_MM, NDEV),
        in_specs=[pl.BlockSpec((TM_MM, D), lambda i, j: (i, 0)), pl.BlockSpec((1, D, nb), lambda i, j: (j, 0, 0))],
        out_specs=pl.BlockSpec((TM_MM, nb), lambda i, j: (i, j)), out_shape=_sds((T, NDEV * nb), F32),
        compiler_params=_cp(("parallel", "parallel")))(a, w_g)


TM_B = 544


def _inproj_bwd(a, dp, w_g):
    nb = w_g.shape[2]
    ni = T // TM_B

    def body(a_ref, dp_ref, w_ref, dw_ref, da_ref, acc):
        j, i = pl.program_id(0), pl.program_id(1)
        av, dpv = a_ref[...], dp_ref[...]
        part = _dot(av, dpv, TN)

        @pl.when(i == 0)
        def _():
            acc[...] = part

        @pl.when(i > 0)
        def _():
            acc[...] += part

        @pl.when(i == ni - 1)
        def _():
            dw_ref[0] = acc[...].astype(BF16)

        rows = pl.ds(pl.multiple_of(i * TM_B, TM_B), TM_B)
        dav = _dot(dpv, w_ref[0], NT)

        @pl.when(j == 0)
        def _():
            da_ref[rows, :] = dav

        @pl.when(j > 0)
        def _():
            da_ref[rows, :] += dav

    return pl.pallas_call(
        body, name="inproj_bwd", grid=(NDEV, ni),
        in_specs=[pl.BlockSpec((TM_B, D), lambda j, i: (i, 0)), pl.BlockSpec((TM_B, nb), lambda j, i: (i, j)),
                  pl.BlockSpec((1, D, nb), lambda j, i: (j, 0, 0))],
        out_specs=(pl.BlockSpec((1, D, nb), lambda j, i: (j, 0, 0)), pl.BlockSpec((T, D), lambda j, i: (0, 0))),
        out_shape=(_sds((NDEV, D, nb), BF16), _sds((T, D), F32)),
        scratch_shapes=[pltpu.VMEM((D, nb), F32)],
        compiler_params=_cp(("arbitrary", "arbitrary")))(a, dp, w_g)


NA_QB = 256
NA_GROUPS = ROWS // 4
NA_UROWS = 11
NA_KW = NA_UROWS * GRID_W
NA_KU = 768


def _na_row_offset(var, i, j):
    valid = (j < 8, i <= j < i + 8, 3 <= j < NA_UROWS)[var]
    return (j - i + (7, 3, 0)[var]) if valid else None


def _na_bias_table(rpb):
    def body(r_ref, o_ref):
        row = lax.broadcasted_iota(jnp.int32, (GRID_W, 128), 0)
        lane = lax.broadcasted_iota(jnp.int32, (GRID_W, 128), 1)
        w = lane & (GRID_W - 1)
        cs = jnp.clip(row - 8, 0, GRID_W - 16)
        in_win = (w >= cs) & (w < cs + 16)
        neg = jnp.full((GRID_W, 128), NEG, F32)
        tabs = []
        for a in range(15):
            z = jnp.broadcast_to(r_ref[0, a:a + 1, :], (GRID_W, 128))
            for bit in range(6):
                sh = 1 << bit
                z = jnp.where((row & sh) != 0, jnp.roll(z, sh, axis=1), z)
            z = jnp.roll(z, 128 - 15, axis=1)
            z = jnp.where(lane < GRID_W, z, 0.0)
            z = z + jnp.roll(z, GRID_W, axis=1)
            tabs.append(jnp.where(in_win, z, NEG))
        tail = jnp.where(lane < GRID_W + NM, 0.0, NEG)
        for var in range(3):
            for i in range(4):
                for jp in range(NA_KU // 128):
                    halves = []
                    for j in (2 * jp, 2 * jp + 1):
                        a = _na_row_offset(var, i, j) if j < NA_UROWS else None
                        halves.append(tail if j >= NA_UROWS else (neg if a is None else tabs[a]))
                    o_ref[var, 0, i * 64:(i + 1) * 64, jp * 128:(jp + 1) * 128] = jnp.where(lane < GRID_W, halves[0], halves[1])

    rp = jnp.concatenate([rpb, jnp.zeros((NA_HEADS, 15, 128 - 31), F32)], axis=2)
    return pl.pallas_call(
        body, name="na_bias_table", grid=(NA_HEADS,),
        in_specs=[pl.BlockSpec((1, 15, 128), lambda h: (h, 0, 0))],
        out_specs=pl.BlockSpec((3, 1, NA_QB, NA_KU), lambda h: (0, h, 0, 0)),
        out_shape=_sds((3, NA_HEADS, NA_QB, NA_KU), F32), compiler_params=_cp(("parallel",)))(rp)


def _na_var(g):
    return jnp.where(g == 0, 0, jnp.where(g == NA_GROUPS - 1, 2, 1))


def _na_load_window(src_ref, dst, g):
    us = jnp.clip(4 * g - 4, 0, ROWS - NA_UROWS)
    kstart = pl.multiple_of(NM + GRID_W * us, 16)
    dst[0:NA_KW, :] = src_ref[pl.ds(kstart, NA_KW), :].astype(BF16)
    dst[NA_KW:NA_KW + NM, :] = src_ref[0:NM, :].astype(BF16)
    dst[NA_KW + NM:, :] = jnp.zeros((NA_KU - NA_KW - NM, 128), BF16)
    return kstart


def _na_fwd(p_act, bias_tab):
    def body(q_ref, k_ref, v_ref, b_ref, o_ref, lse_ref, ku, vu):
        g = pl.program_id(1)
        _na_load_window(k_ref, ku, g)
        _na_load_window(v_ref, vu, g)
        qstart = pl.multiple_of(NM + NA_QB * g, 16)
        q = q_ref[pl.ds(qstart, NA_QB), :]
        lane = lax.broadcasted_iota(jnp.int32, (NA_QB, 128), 1)
        o_h, lse_h = [], []
        for h in range(2):
            hm = (lane < 64) if h == 0 else (lane >= 64)
            qm = jnp.where(hm, q, 0.0).astype(BF16)
            s = _dot(qm, ku[...], NT) * NA_SCALE + b_ref[0, h]
            m = jnp.max(s, axis=-1, keepdims=True)
            p = jnp.exp(s - m)
            l = jnp.sum(p, axis=-1, keepdims=True)
            o_h.append(_dot(p.astype(BF16), vu[...]) / l)
            lse_h.append(jnp.broadcast_to(m + jnp.log(l), (NA_QB, 128)))
        o_ref[pl.ds(qstart, NA_QB), :] = jnp.where(lane < 64, o_h[0], o_h[1]).astype(BF16)
        lse_ref[0, pl.ds(qstart, NA_QB), :] = jnp.where(lane < 64, lse_h[0], lse_h[1])

        @pl.when(g == 0)
        def _():
            qm_ = q_ref[0:NM, :]
            lane_m = lax.broadcasted_iota(jnp.int32, (NM, 128), 1)
            km, vm = ku[NA_KW:NA_KW + NM, :], vu[NA_KW:NA_KW + NM, :]
            om = []
            for h in range(2):
                hm = (lane_m < 64) if h == 0 else (lane_m >= 64)
                s = _dot(jnp.where(hm, qm_, 0.0).astype(BF16), km, NT) * NA_SCALE
                p = jnp.exp(s - jnp.max(s, axis=-1, keepdims=True))
                l = jnp.sum(p, axis=-1, keepdims=True)
                om.append(_dot(p.astype(BF16), vm) / l)
            o_ref[0:NM, :] = jnp.where(lane_m < 64, om[0], om[1]).astype(BF16)
            o_ref[L:T, :] = jnp.zeros((T - L, 128), BF16)
            lse_ref[0, 0:NM, :] = jnp.zeros((NM, 128), F32)
            lse_ref[0, L:T, :] = jnp.zeros((T - L, 128), F32)

    col = lambda off: pl.BlockSpec((T, 128), lambda hp, g: (0, off + hp))
    return pl.pallas_call(
        body, name="na_fwd", grid=(4, NA_GROUPS),
        in_specs=[col(0), col(4), col(8),
                  pl.BlockSpec((1, 2, NA_QB, NA_KU), lambda hp, g: (_na_var(g), hp, 0, 0))],
        out_specs=(pl.BlockSpec((T, 128), lambda hp, g: (0, hp)), pl.BlockSpec((1, T, 128), lambda hp, g: (hp, 0, 0))),
        out_shape=(_sds((T, 512), BF16), _sds((4, T, 128), F32)),
        scratch_shapes=[pltpu.VMEM((NA_KU, 128), BF16), pltpu.VMEM((NA_KU, 128), BF16)],
        compiler_params=_cp(("parallel", "arbitrary")))(p_act, p_act, p_act, bias_tab)


def _na_bwd(p_act, do, lse, bias_tab):
    def body(q_ref, k_ref, v_ref, do_ref, lse_ref, b_ref, dq_ref, dk_ref, dv_ref, db_ref, ku, vu):
        g = pl.program_id(1)

        @pl.when(g == 0)
        def _():
            dq_ref[...] = jnp.zeros((T, 128), F32)
            dk_ref[...] = jnp.zeros((T, 128), F32)
            dv_ref[...] = jnp.zeros((T, 128), F32)

        kstart = _na_load_window(k_ref, ku, g)
        _na_load_window(v_ref, vu, g)
        qstart = pl.multiple_of(NM + NA_QB * g, 16)
        q = q_ref[pl.ds(qstart, NA_QB), :]
        dov = do_ref[pl.ds(qstart, NA_QB), :]
        lsev = lse_ref[0, pl.ds(qstart, NA_QB), :]
        lane = lax.broadcasted_iota(jnp.int32, (NA_QB, 128), 1)
        first = (g == 0) | (g == 1) | (g == NA_GROUPS - 1)
        dq_h = []
        dku = jnp.zeros((NA_KU, 128), F32)
        dvu = jnp.zeros((NA_KU, 128), F32)
        for h in range(2):
            hm = (lane < 64) if h == 0 else (lane >= 64)
            qm = jnp.where(hm, q, 0.0).astype(BF16)
            dom = jnp.where(hm, dov, 0.0).astype(BF16)
            s = _dot(qm, ku[...], NT) * NA_SCALE + b_ref[0, h]
            p = jnp.exp(s - lsev[:, 64 * h:64 * h + 1])
            dp = _dot(dom, vu[...], NT)
            delta = jnp.sum(p * dp, axis=-1, keepdims=True)
            ds = p * (dp - delta)

            @pl.when(first)
            def _():
                db_ref[0, h] = ds

            @pl.when(jnp.logical_not(first))
            def _():
                db_ref[0, h] += ds

            dsb = (ds * NA_SCALE).astype(BF16)
            dq_h.append(_dot(dsb, ku[...]))
            dku = dku + _dot(dsb, qm, TN)
            dvu = dvu + _dot(p.astype(BF16), dom, TN)
        dq_ref[pl.ds(qstart, NA_QB), :] = jnp.where(lane < 64, dq_h[0], dq_h[1])
        dk_ref[pl.ds(kstart, NA_KW), :] += dku[0:NA_KW]
        dv_ref[pl.ds(kstart, NA_KW), :] += dvu[0:NA_KW]
        dk_ref[0:NM, :] += dku[NA_KW:NA_KW + NM]
        dv_ref[0:NM, :] += dvu[NA_KW:NA_KW + NM]

        @pl.when(g == 0)
        def _():
            qm_ = q_ref[0:NM, :]
            dom_ = do_ref[0:NM, :]
            lane_m = lax.broadcasted_iota(jnp.int32, (NM, 128), 1)
            km, vm = ku[NA_KW:NA_KW + NM, :], vu[NA_KW:NA_KW + NM, :]
            dqs = []
            dkm = jnp.zeros((NM, 128), F32)
            dvm = jnp.zeros((NM, 128), F32)
            for h in range(2):
                hm = (lane_m < 64) if h == 0 else (lane_m >= 64)
                qh = jnp.where(hm, qm_, 0.0).astype(BF16)
                doh = jnp.where(hm, dom_, 0.0).astype(BF16)
                s = _dot(qh, km, NT) * NA_SCALE
                e = jnp.exp(s - jnp.max(s, axis=-1, keepdims=True))
                p = e / jnp.sum(e, axis=-1, keepdims=True)
                dp = _dot(doh, vm, NT)
                ds = p * (dp - jnp.sum(p * dp, axis=-1, keepdims=True))
                dsb = (ds * NA_SCALE).astype(BF16)
                dqs.append(_dot(dsb, km))
                dkm = dkm + _dot(dsb, qh, TN)
                dvm = dvm + _dot(p.astype(BF16), doh, TN)
            dq_ref[0:NM, :] = jnp.where(lane_m < 64, dqs[0], dqs[1])
            dk_ref[0:NM, :] += dkm
            dv_ref[0:NM, :] += dvm

    col = lambda off: pl.BlockSpec((T, 128), lambda hp, g: (0, off + hp))
    ocol = pl.BlockSpec((T, 128), lambda hp, g: (0, hp))
    bspec = pl.BlockSpec((1, 2, NA_QB, NA_KU), lambda hp, g: (_na_var(g), hp, 0, 0))
    return pl.pallas_call(
        body, name="na_bwd", grid=(4, NA_GROUPS),
        in_specs=[col(0), col(4), col(8), ocol, pl.BlockSpec((1, T, 128), lambda hp, g: (hp, 0, 0)), bspec],
        out_specs=(ocol, ocol, ocol, bspec),
        out_shape=(_sds((T, 512), F32), _sds((T, 512), F32), _sds((T, 512), F32), _sds((3, NA_HEADS, NA_QB, NA_KU), F32)),
        scratch_shapes=[pltpu.VMEM((NA_KU, 128), BF16), pltpu.VMEM((NA_KU, 128), BF16)],
        compiler_params=_cp(("parallel", "arbitrary")))(p_act, p_act, p_act, do, lse, bias_tab)


def _na_rpb_reduce(dbias):
    def body(db_ref, o_ref):
        row = lax.broadcasted_iota(jnp.int32, (GRID_W, 128), 0)
        for a in range(15):
            acc = jnp.zeros((GRID_W, GRID_W), F32)
            for var in range(3):
                for i in range(4):
                    for j in range(NA_UROWS):
                        if _na_row_offset(var, i, j) == a:
                            pair = db_ref[var, 0, i * 64:(i + 1) * 64, (j // 2) * 128:(j // 2 + 1) * 128]
                            acc = acc + pair[:, (j % 2) * 64:(j % 2 + 1) * 64]
            z = jnp.concatenate([acc, jnp.zeros((GRID_W, 128 - GRID_W), F32)], axis=1)
            for bit in range(6):
                sh = 1 << bit
                z = jnp.where((row & sh) != 0, jnp.roll(z, 128 - sh, axis=1), z)
            z = jnp.roll(z, 15, axis=1)
            o_ref[0, a:a + 1, :] = jnp.sum(z, axis=0, keepdims=True)

    return pl.pallas_call(
        body, name="na_rpb_reduce", grid=(NA_HEADS,),
        in_specs=[pl.BlockSpec((3, 1, NA_QB, NA_KU), lambda h: (0, h, 0, 0))],
        out_specs=pl.BlockSpec((1, 15, 128), lambda h: (h, 0, 0)), out_shape=_sds((NA_HEADS, 15, 128), F32),
        compiler_params=_cp(("parallel",)))(dbias)


HG_RB = 128
HG_NB = T // HG_RB
HG_SLOTS = HG_NB * 8
HI = lax.Precision.HIGHEST


def _chunk_tri(lower):
    r = lax.broadcasted_iota(jnp.int32, (HG_RB, HG_RB), 0)
    c = lax.broadcasted_iota(jnp.int32, (HG_RB, HG_RB), 1)
    same = (r // HG_C) == (c // HG_C)
    keep = (c <= r) if lower else (c >= r)
    return jnp.where(same & keep, 1.0, 0.0).astype(F32)


def _hg_gate_terms(z, lg):
    dl = lg[0:1, :] - lg[1:2, :]
    log_lb = jax.nn.log_sigmoid(dl)
    log_1mlb = jax.nn.log_sigmoid(-dl)
    yz = log_1mlb + jax.nn.log_sigmoid(z)
    log_f = jnp.logaddexp(log_lb, yz)
    snz = jax.nn.sigmoid(-z)
    k = jnp.exp(log_1mlb) * snz
    w2 = jnp.exp(yz - log_f)
    return log_f, k, snz, w2


def _hg_pre(p_act, logits):
    def body(q_ref, zf_ref, zb_ref, lg_ref, qh_ref, kf_ref, bf_ref, kb_ref, bb_ref):
        qh_ref[...] = jax.nn.silu(q_ref[...])
        lf, kf, _, _ = _hg_gate_terms(zf_ref[...], lg_ref[0])
        kf_ref[...] = kf
        bf_ref[...] = jnp.dot(_chunk_tri(True), lf, precision=HI, preferred_element_type=F32)
        lb_, kb, _, _ = _hg_gate_terms(zb_ref[...], lg_ref[1])
        kb_ref[...] = kb
        bb_ref[...] = jnp.dot(_chunk_tri(False), lb_, precision=HI, preferred_element_type=F32)

    blk = lambda c: pl.BlockSpec((HG_RB, 512), lambda i: (i, c))
    ob = pl.BlockSpec((HG_RB, 512), lambda i: (i, 0))
    return pl.pallas_call(
        body, name="hg_pre", grid=(HG_NB,),
        in_specs=[blk(3), blk(4), blk(5), pl.BlockSpec((2, 2, 512), lambda i: (0, 0, 0))],
        out_specs=(ob,) * 5, out_shape=(_sds((T, 512), F32),) * 5,
        compiler_params=_cp(("parallel",)))(p_act, p_act, p_act, logits)


def _bdot(a, b, ca, cb):
    return lax.dot_general(a.astype(BF16), b.astype(BF16), (((ca,), (cb,)), ((0,), (0,))), preferred_element_type=F32)


def _hg_scan_fwd(qh, k, b, p_act, rev):
    anchor = 0 if rev else HG_C - 1

    def body(q_ref, k_ref, b_ref, v_ref, o_ref, st_ref, dsc):
        def phase_a(blk, _):
            rows = pl.ds(pl.multiple_of(blk * HG_RB, HG_RB), HG_RB)
            b3 = b_ref[rows, :].reshape(8, HG_C, 128)
            k3 = k_ref[rows, :].reshape(8, HG_C, 128)
            v3 = v_ref[rows, :].reshape(8, HG_C, 128)
            bl = b3[:, anchor:anchor + 1, :]
            kt = k3 * jnp.exp(bl - b3)
            st_ref[0, pl.ds(pl.multiple_of(blk * 8, 8), 8)] = _bdot(v3, kt, 1, 1)
            dsc[pl.ds(pl.multiple_of(blk * 8, 8), 8), :] = jnp.exp(bl[:, 0, :])
            return 0

        lax.fori_loop(0, HG_NB, phase_a, 0)

        def phase_b(n, carry):
            c = (NCHUNK - 1 - n) if rev else n
            u = st_ref[0, c]
            st_ref[0, c] = carry
            return carry * dsc[pl.ds(c, 1), :] + u

        lax.fori_loop(0, NCHUNK, phase_b, jnp.zeros((128, 128), F32))
        for c in range(NCHUNK, HG_SLOTS):
            st_ref[0, c] = jnp.zeros((128, 128), F32)

        t_io = lax.broadcasted_iota(jnp.int32, (8, HG_C, 128), 1)
        l_io = lax.broadcasted_iota(jnp.int32, (8, HG_C, HG_C), 2)

        def phase_c(blk, _):
            rows = pl.ds(pl.multiple_of(blk * HG_RB, HG_RB), HG_RB)
            b3 = b_ref[rows, :].reshape(8, HG_C, 128)
            k3 = k_ref[rows, :].reshape(8, HG_C, 128)
            q3 = q_ref[rows, :].reshape(8, HG_C, 128)
            v3 = v_ref[rows, :].reshape(8, HG_C, 128)
            st = st_ref[0, pl.ds(pl.multiple_of(blk * 8, 8), 8)]
            o = _bdot(q3 * jnp.exp(b3), st, 2, 2)
            a = jnp.zeros((8, HG_C, HG_C), F32)
            for s in range(HG_C):
                ok = (t_io <= s) if rev else (t_io >= s)
                f = jnp.exp(jnp.where(ok, b3 - b3[:, s:s + 1, :], NEG))
                col = jnp.sum(q3 * f * k3[:, s:s + 1, :], axis=-1, keepdims=True)
                a = a + jnp.where(l_io == s, col, 0.0)
            o = o + _bdot(a, v3, 2, 1)
            o_ref[rows, :] = o.reshape(HG_RB, 128)
            return 0

        lax.fori_loop(0, HG_NB, phase_c, 0)

    col = pl.BlockSpec((T, 128), lambda h: (0, h))
    return pl.pallas_call(
        body, name="hg_scan_bwd_dir" if rev else "hg_scan_fwd_dir", grid=(HG_HEADS,),
        in_specs=[col, col, col, pl.BlockSpec((T, 128), lambda h: (0, 24 + h))],
        out_specs=(col, pl.BlockSpec((1, HG_SLOTS, 128, 128), lambda h: (h, 0, 0, 0))),
        out_shape=(_sds((T, 512), F32), _sds((HG_HEADS, HG_SLOTS, 128, 128), F32)),
        scratch_shapes=[pltpu.VMEM((HG_SLOTS, 128), F32)],
        compiler_params=_cp(("parallel",), 56))(qh, k, b, p_act)


def _hg_scan_bwd(qh, k, b, p_act, st, do, rev):
    anchor = 0 if rev else HG_C - 1

    def body(q_ref, k_ref, b_ref, v_ref, st_ref, do_ref, dq_ref, dk_ref, db_ref, dv_ref, gst, dsc, dbl):
        def phase_a(blk, _):
            rows = pl.ds(pl.multiple_of(blk * HG_RB, HG_RB), HG_RB)
            b3 = b_ref[rows, :].reshape(8, HG_C, 128)
            q3 = q_ref[rows, :].reshape(8, HG_C, 128)
            do3 = do_ref[rows, :].reshape(8, HG_C, 128)
            gst[pl.ds(pl.multiple_of(blk * 8, 8), 8)] = _bdot(do3, q3 * jnp.exp(b3), 1, 1)
            dsc[pl.ds(pl.multiple_of(blk * 8, 8), 8), :] = jnp.exp(b3[:, anchor, :])
            return 0

        lax.fori_loop(0, HG_NB, phase_a, 0)

        def phase_b(n, carry):
            c = n if rev else (NCHUNK - 1 - n)
            w = gst[c]
            gst[c] = carry
            dcv = dsc[pl.ds(c, 1), :]
            dbl[pl.ds(c, 1), :] = dcv * jnp.sum(st_ref[0, c] * carry, axis=0, keepdims=True)
            return carry * dcv + w

        lax.fori_loop(0, NCHUNK, phase_b, jnp.zeros((128, 128), F32))
        for c in range(NCHUNK, HG_SLOTS):
            gst[c] = jnp.zeros((128, 128), F32)
            dbl[c:c + 1, :] = jnp.zeros((1, 128), F32)

        t_io = lax.broadcasted_iota(jnp.int32, (8, HG_C, 128), 1)
        r_io = lax.broadcasted_iota(jnp.int32, (8, HG_C, HG_C), 1)
        l_io = lax.broadcasted_iota(jnp.int32, (8, HG_C, HG_C), 2)

        def phase_c(blk, _):
            rows = pl.ds(pl.multiple_of(blk * HG_RB, HG_RB), HG_RB)
            cs = pl.ds(pl.multiple_of(blk * 8, 8), 8)
            b3 = b_ref[rows, :].reshape(8, HG_C, 128)
            k3 = k_ref[rows, :].reshape(8, HG_C, 128)
            q3 = q_ref[rows, :].reshape(8, HG_C, 128)
            v3 = v_ref[rows, :].reshape(8, HG_C, 128)
            do3 = do_ref[rows, :].reshape(8, HG_C, 128)
            s_t = st_ref[0, cs]
            g_t = gst[cs]
            bl = b3[:, anchor:anchor + 1, :]
            ekl = jnp.exp(bl - b3)
            kt = k3 * ekl
            dqt = _bdot(do3, s_t, 2, 1)
            dkt = _bdot(v3, g_t, 2, 1)
            dv = _bdot(kt, g_t, 2, 2)
            causal = (l_io >= r_io) if rev else (l_io <= r_io)
            da = jnp.where(causal, _bdot(do3, v3, 2, 2), 0.0)
            causal_t = (l_io <= r_io) if rev else (l_io >= r_io)
            dat = jnp.where(causal_t, _bdot(v3, do3, 2, 2), 0.0)
            dq = dqt * jnp.exp(b3)
            dk = dkt * ekl
            at = jnp.zeros((8, HG_C, HG_C), F32)
            for s in range(HG_C):
                ok = (t_io <= s) if rev else (t_io >= s)
                f = jnp.exp(jnp.where(ok, b3 - b3[:, s:s + 1, :], NEG))
                dq = dq + da[:, :, s:s + 1] * (f * k3[:, s:s + 1, :])
            for t in range(HG_C):
                ok = (t_io >= t) if rev else (t_io <= t)
                e = jnp.exp(jnp.where(ok, b3[:, t:t + 1, :] - b3, NEG))
                eq = e * q3[:, t:t + 1, :]
                dk = dk + dat[:, :, t:t + 1] * eq
                at = at + jnp.where(l_io == t, jnp.sum(eq * k3, axis=-1, keepdims=True), 0.0)
            dv = dv + _bdot(at, do3, 2, 1)
            dbl3 = dbl[cs, :].reshape(8, 1, 128) + jnp.sum(dkt * kt, axis=1, keepdims=True)
            db = q3 * dq - k3 * dk + jnp.where(t_io == anchor, dbl3, 0.0)
            dq_ref[rows, :] = dq.reshape(HG_RB, 128)
            dk_ref[rows, :] = dk.reshape(HG_RB, 128)
            db_ref[rows, :] = db.reshape(HG_RB, 128)
            dv_ref[rows, :] = dv.reshape(HG_RB, 128)
            return 0

        lax.fori_loop(0, HG_NB, phase_c, 0)

    col = pl.BlockSpec((T, 128), lambda h: (0, h))
    return pl.pallas_call(
        body, name="hg_scan_bwd_dir_bwd" if rev else "hg_scan_fwd_dir_bwd", grid=(HG_HEADS,),
        in_specs=[col, col, col, pl.BlockSpec((T, 128), lambda h: (0, 24 + h)),
                  pl.BlockSpec((1, HG_SLOTS, 128, 128), lambda h: (h, 0, 0, 0)), col],
        out_specs=(col,) * 4, out_shape=(_sds((T, 512), F32),) * 4,
        scratch_shapes=[pltpu.VMEM((HG_SLOTS, 128, 128), F32), pltpu.VMEM((HG_SLOTS, 128), F32),
                        pltpu.VMEM((HG_SLOTS, 128), F32)],
        compiler_params=_cp(("parallel",), 56))(qh, k, b, p_act, st, do)


def _row_valid(i, tm):
    r = lax.broadcasted_iota(jnp.int32, (tm, 1), 0) + i * tm
    return r < L


def _hg_post(o_f, o_b, p_act, gain):
    def body(of_ref, ob_ref, g_ref, gain_ref, u_ref):
        o = of_ref[...] + ob_ref[...]
        sg = jax.nn.silu(g_ref[...])
        parts = []
        for h in range(HG_HEADS):
            oh = o[:, 128 * h:128 * (h + 1)]
            parts.append(oh * lax.rsqrt(jnp.mean(oh * oh, axis=-1, keepdims=True) + EPS))
        n = jnp.concatenate(parts, axis=1)
        u = n * gain_ref[...] * sg
        u_ref[...] = jnp.where(_row_valid(pl.program_id(0), TM_E), u, 0.0).astype(BF16)

    blk = pl.BlockSpec((TM_E, 512), lambda i: (i, 0))
    return pl.pallas_call(
        body, name="hg_post", grid=(T // TM_E,),
        in_specs=[blk, blk, pl.BlockSpec((TM_E, 512), lambda i: (i, 7)), pl.BlockSpec((1, 512), lambda i: (0, 0))],
        out_specs=blk, out_shape=_sds((T, 512), BF16), compiler_params=_cp(("parallel",)))(o_f, o_b, p_act, gain)


def _hg_post_bwd(du, o_f, o_b, p_act, gain):
    def body(du_ref, of_ref, ob_ref, g_ref, gain_ref, do_ref, dg_ref, dgain_ref):
        i = pl.program_id(0)
        valid = _row_valid(i, TM_E)
        duv = jnp.where(valid, du_ref[...], 0.0)
        o = of_ref[...] + ob_ref[...]
        gv = g_ref[...]
        sig = jax.nn.sigmoid(gv)
        sg = gv * sig
        gain_v = gain_ref[...]
        dn = duv * gain_v * sg
        do_parts, n_parts = [], []
        for h in range(HG_HEADS):
            sl = slice(128 * h, 128 * (h + 1))
            oh = o[:, sl]
            r = lax.rsqrt(jnp.mean(oh * oh, axis=-1, keepdims=True) + EPS)
            nh = oh * r
            dnh = dn[:, sl]
            do_parts.append(r * (dnh - nh * jnp.mean(dnh * nh, axis=-1, keepdims=True)))
            n_parts.append(nh)
        n = jnp.where(valid, jnp.concatenate(n_parts, axis=1), 0.0)
        do_ref[...] = jnp.where(valid, jnp.concatenate(do_parts, axis=1), 0.0)
        dg_ref[...] = (duv * n * gain_v * (sig * (1.0 + gv * (1.0 - sig)))).astype(BF16)
        part = jnp.sum(duv * n * sg, axis=0, keepdims=True)

        @pl.when(i == 0)
        def _():
            dgain_ref[...] = part

        @pl.when(i > 0)
        def _():
            dgain_ref[...] += part

    blk = pl.BlockSpec((TM_E, 512), lambda i: (i, 0))
    vec = pl.BlockSpec((1, 512), lambda i: (0, 0))
    return pl.pallas_call(
        body, name="hg_post_bwd", grid=(T // TM_E,),
        in_specs=[blk, blk, blk, pl.BlockSpec((TM_E, 512), lambda i: (i, 7)), vec],
        out_specs=(blk, blk, vec), out_shape=(_sds((T, 512), F32), _sds((T, 512), BF16), _sds((1, 512), F32)),
        compiler_params=_cp(("arbitrary",)))(du, o_f, o_b, p_act, gain)


def _hg_pre_bwd(p_act, logits, dq_f, dq_b, dk_f, dk_b, db_f, db_b, dv_f, dv_b):
    def body(q_ref, zf_ref, zb_ref, lg_ref, dqf_ref, dqb_ref, dkf_ref, dkb_ref, dbf_ref, dbb_ref, dvf_ref, dvb_ref,
             dq_ref, dzf_ref, dzb_ref, di_ref, dlg_ref):
        i = pl.program_id(0)
        valid = _row_valid(i, HG_RB)
        qv = q_ref[...]
        sig = jax.nn.sigmoid(qv)
        dq_ref[...] = jnp.where(valid, (dqf_ref[...] + dqb_ref[...]) * (sig * (1.0 + qv * (1.0 - sig))), 0.0).astype(BF16)
        di_ref[...] = jnp.where(valid, dvf_ref[...] + dvb_ref[...], 0.0).astype(BF16)
        for d, (z_ref, dk_r, db_r, dz_ref) in enumerate(((zf_ref, dkf_ref, dbf_ref, dzf_ref), (zb_ref, dkb_ref, dbb_ref, dzb_ref))):
            lg = lg_ref[d]
            dl = lg[0:1, :] - lg[1:2, :]
            lb = jax.nn.sigmoid(dl)
            one_m_lb = jax.nn.sigmoid(-dl)
            log_f, _, snz, w2 = _hg_gate_terms(z_ref[...], lg)
            dbv = jnp.where(valid, db_r[...], 0.0)
            dkv = jnp.where(valid, dk_r[...], 0.0)
            dlf = jnp.dot(_chunk_tri(d == 1), dbv, precision=HI, preferred_element_type=F32)
            sz = 1.0 - snz
            dz_ref[...] = (dlf * w2 * snz - dkv * one_m_lb * sz * snz).astype(BF16)
            dlb = jnp.sum(dlf * snz * jnp.exp(-log_f) - dkv * snz, axis=0, keepdims=True)
            dl0 = dlb * lb * one_m_lb
            part = jnp.concatenate([dl0, -dl0], axis=0)

            @pl.when(i == 0)
            def _():
                dlg_ref[d] = part

            @pl.when(i > 0)
            def _():
                dlg_ref[d] += part

    blk = lambda c: pl.BlockSpec((HG_RB, 512), lambda i: (i, c))
    ob = pl.BlockSpec((HG_RB, 512), lambda i: (i, 0))
    lgs = pl.BlockSpec((2, 2, 512), lambda i: (0, 0, 0))
    return pl.pallas_call(
        body, name="hg_pre_bwd", grid=(HG_NB,),
        in_specs=[blk(3), blk(4), blk(5), lgs] + [ob] * 8,
        out_specs=(ob, ob, ob, ob, lgs),
        out_shape=(_sds((T, 512), BF16),) * 4 + (_sds((2, 2, 512), F32),),
        compiler_params=_cp(("arbitrary",)))(p_act, p_act, p_act, logits, dq_f, dq_b, dk_f, dk_b, db_f, db_b, dv_f, dv_b)


def _mix_fwd(o_na, u_hg, wna_g, whg_g, p_act):
    def body(ona_ref, uhg_ref, wna_ref, whg_ref, gna_ref, ghg_ref, o_ref):
        y_na = _dot(ona_ref[...], wna_ref[0])
        y_hg = _dot(uhg_ref[...], whg_ref[0])
        o_ref[...] = (jax.nn.sigmoid(gna_ref[...]) * y_na + jax.nn.sigmoid(ghg_ref[...]) * y_hg).astype(BF16)

    act = pl.BlockSpec((TM_MM, 512), lambda i, j: (i, 0))
    wsp = pl.BlockSpec((1, 512, 128), lambda i, j: (j, 0, 0))
    return pl.pallas_call(
        body, name="mix_fwd", grid=(T // TM_MM, NDEV),
        in_specs=[act, act, wsp, wsp, pl.BlockSpec((TM_MM, 128), lambda i, j: (i, 32 + j)),
                  pl.BlockSpec((TM_MM, 128), lambda i, j: (i, 40 + j))],
        out_specs=pl.BlockSpec((TM_MM, 128), lambda i, j: (i, j)), out_shape=_sds((T, D), BF16),
        compiler_params=_cp(("parallel", "parallel")))(o_na, u_hg, wna_g, whg_g, p_act, p_act)


def _mix_bwd(o_na, u_hg, wna_g, whg_g, p_act, dmix):
    ni = T // TM_B

    def body(ona_ref, uhg_ref, wna_ref, whg_ref, gna_ref, ghg_ref, dmix_ref,
             dgna_ref, dghg_ref, dwna_ref, dwhg_ref, dona_ref, duhg_ref, acc_na, acc_hg):
        j, i = pl.program_id(0), pl.program_id(1)
        rows = pl.ds(pl.multiple_of(i * TM_B, TM_B), TM_B)
        dm = dmix_ref[...].astype(F32)
        for x_ref, w_ref, g_ref, dg_ref, dx_ref, dw_ref, acc in (
                (ona_ref, wna_ref, gna_ref, dgna_ref, dona_ref, dwna_ref, acc_na),
                (uhg_ref, whg_ref, ghg_ref, dghg_ref, duhg_ref, dwhg_ref, acc_hg)):
            xv = x_ref[...]
            y = _dot(xv, w_ref[0])
            sg = jax.nn.sigmoid(g_ref[...])
            dg_ref[...] = (dm * y * sg * (1.0 - sg)).astype(BF16)
            dy = (dm * sg).astype(BF16)
            part = _dot(xv, dy, TN)
            dxv = _dot(dy, w_ref[0], NT)

            @pl.when(i == 0)
            def _():
                acc[...] = part

            @pl.when(i > 0)
            def _():
                acc[...] += part

            @pl.when(i == ni - 1)
            def _():
                dw_ref[0] = acc[...].astype(BF16)

            @pl.when(j == 0)
            def _():
                dx_ref[rows, :] = dxv

            @pl.when(j > 0)
            def _():
                dx_ref[rows, :] += dxv

    act = pl.BlockSpec((TM_B, 512), lambda j, i: (i, 0))
    wsp = pl.BlockSpec((1, 512, 128), lambda j, i: (j, 0, 0))
    cblk = pl.BlockSpec((TM_B, 128), lambda j, i: (i, j))
    full = pl.BlockSpec((T, 512), lambda j, i: (0, 0))
    return pl.pallas_call(
        body, name="mix_bwd", grid=(NDEV, ni),
        in_specs=[act, act, wsp, wsp, pl.BlockSpec((TM_B, 128), lambda j, i: (i, 32 + j)),
                  pl.BlockSpec((TM_B, 128), lambda j, i: (i, 40 + j)), cblk],
        out_specs=(cblk, cblk, wsp, wsp, full, full),
        out_shape=(_sds((T, D), BF16), _sds((T, D), BF16), _sds((NDEV, 512, 128), BF16), _sds((NDEV, 512, 128), BF16),
                   _sds((T, 512), F32), _sds((T, 512), F32)),
        scratch_shapes=[pltpu.VMEM((512, 128), F32), pltpu.VMEM((512, 128), F32)],
        compiler_params=_cp(("arbitrary", "arbitrary")))(o_na, u_hg, wna_g, whg_g, p_act, p_act, dmix)


def _wo_fwd(mix, w_o, h0, g_mlp):
    def body(mix_ref, w_ref, h0_ref, g_ref, h1_ref, m_ref):
        h1 = h0_ref[...] + _dot(mix_ref[...], w_ref[...])
        h1_ref[...] = h1
        r = lax.rsqrt(jnp.mean(h1 * h1, axis=-1, keepdims=True) + EPS)
        m_ref[...] = (h1 * r * g_ref[...]).astype(BF16)

    blk = pl.BlockSpec((TM_B, D), lambda i: (i, 0))
    return pl.pallas_call(
        body, name="wo_fwd", grid=(T // TM_B,),
        in_specs=[blk, pl.BlockSpec((D, D), lambda i: (0, 0)), blk, pl.BlockSpec((1, D), lambda i: (0, 0))],
        out_specs=(blk, blk), out_shape=(_sds((T, D), F32), _sds((T, D), BF16)),
        compiler_params=_cp(("parallel",)))(mix, w_o, h0, g_mlp)


def _wo_bwd(dh1_b, w_o, mix):
    ni = T // TM_B

    def body(dh_ref, w_ref, mix_ref, dmix_ref, dw_ref, acc):
        i = pl.program_id(0)
        dh = dh_ref[...]
        dmix_ref[...] = _dot(dh, w_ref[...], NT).astype(BF16)
        part = _dot(mix_ref[...], dh, TN)

        @pl.when(i == 0)
        def _():
            acc[...] = part

        @pl.when(i > 0)
        def _():
            acc[...] += part

        @pl.when(i == ni - 1)
        def _():
            dw_ref[...] = acc[...].astype(BF16)

    blk = pl.BlockSpec((TM_B, D), lambda i: (i, 0))
    wsp = pl.BlockSpec((D, D), lambda i: (0, 0))
    return pl.pallas_call(
        body, name="wo_bwd", grid=(ni,), in_specs=[blk, wsp, blk], out_specs=(blk, wsp),
        out_shape=(_sds((T, D), BF16), _sds((D, D), BF16)), scratch_shapes=[pltpu.VMEM((D, D), F32)],
        compiler_params=_cp(("arbitrary",)))(dh1_b, w_o, mix)


FF_B = D_FF // NDEV


def _mlp_fwd(m, wup_g, wdown_g, h1):
    def body(m_ref, wu_ref, wd_ref, h1_ref, h2_ref):
        j = pl.program_id(1)
        up = jnp.maximum(_dot(m_ref[...], wu_ref[0]), 0.0)
        part = _dot((up * up).astype(BF16), wd_ref[0])

        @pl.when(j == 0)
        def _():
            h2_ref[...] = h1_ref[...] + part

        @pl.when(j > 0)
        def _():
            h2_ref[...] += part

    blk = pl.BlockSpec((TM_MM, D), lambda i, j: (i, 0))
    return pl.pallas_call(
        body, name="mlp_fwd", grid=(T // TM_MM, NDEV),
        in_specs=[blk, pl.BlockSpec((1, D, FF_B), lambda i, j: (j, 0, 0)), pl.BlockSpec((1, FF_B, D), lambda i, j: (j, 0, 0)), blk],
        out_specs=blk, out_shape=_sds((T, D), F32),
        compiler_params=_cp(("parallel", "arbitrary")))(m, wup_g, wdown_g, h1)


def _mlp_bwd(m, dh2_b, wup_g, wdown_g):
    ni = T // TM_B

    def body(m_ref, dh_ref, wu_ref, wd_ref, dwu_ref, dwd_ref, dm_ref, acc_u, acc_d):
        j, i = pl.program_id(0), pl.program_id(1)
        rows = pl.ds(pl.multiple_of(i * TM_B, TM_B), TM_B)
        mv, dh = m_ref[...], dh_ref[...]
        r = jnp.maximum(_dot(mv, wu_ref[0]), 0.0)
        act = (r * r).astype(BF16)
        dact = _dot(dh, wd_ref[0], NT)
        dup = (dact * (2.0 * r)).astype(BF16)
        pd = _dot(act, dh, TN)
        pu = _dot(mv, dup, TN)
        dmv = _dot(dup, wu_ref[0], NT)

        @pl.when(i == 0)
        def _():
            acc_u[...] = pu
            acc_d[...] = pd

        @pl.when(i > 0)
        def _():
            acc_u[...] += pu
            acc_d[...] += pd

        @pl.when(i == ni - 1)
        def _():
            dwu_ref[0] = acc_u[...].astype(BF16)
            dwd_ref[0] = acc_d[...].astype(BF16)

        @pl.when(j == 0)
        def _():
            dm_ref[rows, :] = dmv

        @pl.when(j > 0)
        def _():
            dm_ref[rows, :] += dmv

    blk = pl.BlockSpec((TM_B, D), lambda j, i: (i, 0))
    wus = pl.BlockSpec((1, D, FF_B), lambda j, i: (j, 0, 0))
    wds = pl.BlockSpec((1, FF_B, D), lambda j, i: (j, 0, 0))
    return pl.pallas_call(
        body, name="mlp_bwd", grid=(NDEV, ni), in_specs=[blk, blk, wus, wds],
        out_specs=(wus, wds, pl.BlockSpec((T, D), lambda j, i: (0, 0))),
        out_shape=(_sds((NDEV, D, FF_B), BF16), _sds((NDEV, FF_B, D), BF16), _sds((T, D), F32)),
        scratch_shapes=[pltpu.VMEM((D, FF_B), F32), pltpu.VMEM((FF_B, D), F32)],
        compiler_params=_cp(("arbitrary", "arbitrary")))(m, dh2_b, wup_g, wdown_g)


def _loss_head(h2, g_final, tgt):
    def body(h_ref, g_ref, t_ref, loss_ref, dh_ref, dhb_ref, dg_ref):
        i = pl.program_id(0)
        r_io = lax.broadcasted_iota(jnp.int32, (TM_E, 1), 0) + i * TM_E
        valid = (r_io >= NM) & (r_io < L)
        xv = h_ref[...]
        r = lax.rsqrt(jnp.mean(xv * xv, axis=-1, keepdims=True) + EPS)
        xh = xv * r
        gv = g_ref[...]
        err = jnp.where(valid, xh * gv - t_ref[...], 0.0)
        lpart = jnp.broadcast_to(0.5 * jnp.sum(jnp.sum(err * err, axis=-1, keepdims=True) * (1.0 / D), axis=0, keepdims=True), (1, 128))
        dy = err * (1.0 / D)
        dxh = dy * gv
        dh = r * (dxh - xh * jnp.mean(dxh * xh, axis=-1, keepdims=True))
        dh_ref[...] = dh
        dhb_ref[...] = dh.astype(BF16)
        gpart = jnp.sum(dy * xh, axis=0, keepdims=True)

        @pl.when(i == 0)
        def _():
            loss_ref[...] = lpart
            dg_ref[...] = gpart

        @pl.when(i > 0)
        def _():
            loss_ref[...] += lpart
            dg_ref[...] += gpart

    blk = pl.BlockSpec((TM_E, D), lambda i: (i, 0))
    vec = pl.BlockSpec((1, D), lambda i: (0, 0))
    return pl.pallas_call(
        body, name="loss_head", grid=(T // TM_E,), in_specs=[blk, vec, blk],
        out_specs=(pl.BlockSpec((1, 128), lambda i: (0, 0)), blk, blk, vec),
        out_shape=(_sds((1, 128), F32), _sds((T, D), F32), _sds((T, D), BF16), _sds((1, D), F32)),
        compiler_params=_cp(("arbitrary",)))(h2, g_final, tgt)


def _adamw(parts, w, m, v, name):
    rr, cc = w.shape
    tr = rr
    for cand in (256, 128, 64):
        if rr % cand == 0 and rr > cand:
            tr = cand
            break
    c1 = 1.0 - ADAM_B1 ** ADAM_STEP
    c2 = 1.0 - ADAM_B2 ** ADAM_STEP

    def body(p_ref, w_ref, m_ref, v_ref, g_ref, d_ref, nm_ref, nv_ref):
        g = p_ref[0].astype(F32)
        for s in range(1, NDEV):
            g = g + p_ref[s].astype(F32)
        mn = ADAM_B1 * m_ref[...] + (1.0 - ADAM_B1) * g
        vn = ADAM_B2 * v_ref[...] + (1.0 - ADAM_B2) * (g * g)
        g_ref[...] = g
        nm_ref[...] = mn
        nv_ref[...] = vn
        d_ref[...] = -ADAM_LR * ((mn / c1) / (jnp.sqrt(vn / c2) + ADAM_EPS) + ADAM_WD * w_ref[...])

    blk = pl.BlockSpec((tr, cc), lambda i: (i, 0))
    return pl.pallas_call(
        body, name=name, grid=(rr // tr,),
        in_specs=[pl.BlockSpec((NDEV, tr, cc), lambda i: (0, i, 0)), blk, blk, blk],
        out_specs=(blk,) * 4, out_shape=(_sds((rr, cc), F32),) * 4,
        compiler_params=_cp(("parallel",)))(parts, w, m, v)


RPB_N = NA_HEADS * 15 * 31
RPB_PAD = 4096
OWN_ROWS = NM + 8


def _pad_rows(a, rows):
    return jnp.pad(a, ((0, rows - a.shape[0]),) + ((0, 0),) * (a.ndim - 1))


def _pack_owned(meta_blk, lb_blk):
    return jnp.concatenate([meta_blk, _pad_rows(lb_blk.reshape(2, 128), 8)], axis=0)


def _pack_replicated(n_mix, n_mlp, n_final, hg_gain, rpb):
    flat = _pad_rows(rpb.reshape(RPB_N), RPB_PAD)
    return jnp.concatenate([n_mix.reshape(8, 128), n_mlp.reshape(8, 128), n_final.reshape(8, 128),
                            _pad_rows(hg_gain.reshape(4, 128), 8), flat.reshape(32, 128)], axis=0)


def _unpack_replicated(a):
    return (a[0:8].reshape(1, D), a[8:16].reshape(1, D), a[16:24].reshape(D), a[24:28].reshape(1, 512),
            a[32:64].reshape(RPB_PAD)[:RPB_N].reshape(1, NA_HEADS, 15, 31))


def kernel(x, meta_tokens, w_in, w_na_out, w_hg_out, w_o, w_up, w_down, norm_mix, norm_mlp, norm_final, hg_norm, na_rpb, hg_lb_logits, loss_target, m_meta_tokens, m_w_in, m_w_na_out, m_w_hg_out, m_w_o, m_w_up, m_w_down, m_norm_mix, m_norm_mlp, m_norm_final, m_hg_norm, m_na_rpb, m_hg_lb_logits, v_meta_tokens, v_w_in, v_w_na_out, v_w_hg_out, v_w_o, v_w_up, v_w_down, v_norm_mix, v_norm_mlp, v_norm_final, v_hg_norm, v_na_rpb, v_hg_lb_logits):
    owned = _pack_owned(meta_tokens, hg_lb_logits)
    win_g, owned_g = _exchange([w_in[0].astype(BF16), owned], [False] * 2, "gather_first")
    later = [w[0].astype(BF16) for w in (w_na_out, w_hg_out, w_o, w_up, w_down)]
    later[0] = _tie(later[0], owned_g, "tie_gather_rest")
    gather_rest, tok = _exchange_start(later, [False] * 5, "gather_rest_start")
    win_g = _tie(win_g, tok, "tie_inproj")
    meta_full = jnp.transpose(owned_g[:, 0:NM, :], (1, 0, 2)).reshape(NM, D)
    logits = jnp.transpose(owned_g[:, NM:NM + 2, :].reshape(NDEV, 2, 2, 64), (1, 2, 0, 3)).reshape(2, 2, 512)

    h0 = jnp.concatenate([meta_full, x[0], jnp.zeros((T - L, D), F32)], axis=0)
    tgt = jnp.concatenate([jnp.zeros((NM, D), F32), loss_target[0], jnp.zeros((T - L, D), F32)], axis=0)
    bias_tab = _na_bias_table(na_rpb[0])

    a = _norm_fwd(h0, norm_mix, "norm_mix_fwd")
    p_act = _inproj_fwd(a, win_g)
    o_na, lse = _na_fwd(p_act, bias_tab)
    qh, k_f, b_f, k_b, b_b = _hg_pre(p_act, logits)
    o_f, st_f = _hg_scan_fwd(qh, k_f, b_f, p_act, False)
    o_b, st_b = _hg_scan_fwd(qh, k_b, b_b, p_act, True)
    u_hg = _hg_post(o_f, o_b, p_act, hg_norm)
    wna_g, whg_g, wo_g, wup_g, wdown_g = _exchange_wait(gather_rest, [False] * 5, [u_hg, o_na], "gather_rest_wait")
    w_o_full = wo_g.reshape(D, D)
    mix = _mix_fwd(o_na, u_hg, wna_g, whg_g, p_act)
    h1, m_act = _wo_fwd(mix, w_o_full, h0, norm_mlp)
    h2 = _mlp_fwd(m_act, wup_g, wdown_g, h1)
    loss_part, dh2, dh2_b, d_nfinal = _loss_head(h2, norm_final.reshape(1, D), tgt)

    dwup_p, dwdown_p, dm = _mlp_bwd(m_act, dh2_b, wup_g, wdown_g)
    sc_mlp, tok = _exchange_start([dwup_p, dwdown_p], [True] * 2, "scatter_mlp_start")
    dh1, dh1_b, d_nmlp = _norm_bwd(h1, norm_mlp, _tie(dm, tok, "tie_norm_mlp_bwd"), dh2, "norm_mlp_bwd")
    dmix, dwo = _wo_bwd(dh1_b, w_o_full, mix)
    sc_wo, tok = _exchange_start([dwo.reshape(NDEV, D // NDEV, D)], [True], "scatter_wo_start")
    dgna, dghg, dwna_p, dwhg_p, do_na, du_hg = _mix_bwd(o_na, u_hg, wna_g, whg_g, p_act, _tie(dmix, tok, "tie_mix_bwd"))
    sc_br, tok = _exchange_start([dwna_p, dwhg_p], [True] * 2, "scatter_branch_start")
    du_hg = _tie(du_hg, tok, "tie_hg_post_bwd")
    do_hg, dg_hg, d_gain = _hg_post_bwd(du_hg, o_f, o_b, p_act, hg_norm)
    dq_f, dk_f, db_f, dv_f = _hg_scan_bwd(qh, k_f, b_f, p_act, st_f, do_hg, False)
    dq_b, dk_b, db_b, dv_b = _hg_scan_bwd(qh, k_b, b_b, p_act, st_b, do_hg, True)
    dq_hg, dz_f, dz_b, di_hg, d_logits = _hg_pre_bwd(p_act, logits, dq_f, dq_b, dk_f, dk_b, db_f, db_b, dv_f, dv_b)
    dq_na, dk_na, dv_na, dbias = _na_bwd(p_act, do_na, lse, bias_tab)
    d_rpb = _na_rpb_reduce(dbias)[:, :, :31]
    dp = jnp.concatenate([dq_na.astype(BF16), dk_na.astype(BF16), dv_na.astype(BF16), dq_hg, dz_f, dz_b, di_hg, dg_hg,
                          dgna, dghg], axis=1)
    dwin_p, da = _inproj_bwd(a, dp, win_g)
    dh0, _, d_nmix = _norm_bwd(h0, norm_mix, da, dh1, "norm_mix_bwd")

    d_meta = jnp.transpose(dh0[0:NM].reshape(NM, NDEV, 128), (1, 0, 2))
    d_lg = jnp.transpose(d_logits.reshape(2, 2, NDEV, 64), (2, 0, 1, 3)).reshape(NDEV, 2, 128)
    owned_p = jnp.concatenate([d_meta, jnp.pad(d_lg, ((0, 0), (0, OWN_ROWS - NM - 2), (0, 0)))], axis=1)
    repl_p = _pack_replicated(d_nmix, d_nmlp, d_nfinal, d_gain, d_rpb)
    win_r, owned_r, repl_r = _exchange([dwin_p, owned_p, repl_p], [True, True, False], "scatter_last")
    wup_r, wdown_r = _exchange_wait(sc_mlp, [True] * 2, [win_r], "scatter_mlp_wait")
    (wo_r,) = _exchange_wait(sc_wo, [True], [wup_r], "scatter_wo_wait")
    wna_r, whg_r = _exchange_wait(sc_br, [True] * 2, [wo_r], "scatter_branch_wait")

    res = {}
    for nm, parts, w, mm, vv in (
            ("w_in", win_r, w_in, m_w_in, v_w_in), ("w_na_out", wna_r, w_na_out, m_w_na_out, v_w_na_out),
            ("w_hg_out", whg_r, w_hg_out, m_w_hg_out, v_w_hg_out), ("w_o", wo_r, w_o, m_w_o, v_w_o),
            ("w_up", wup_r, w_up, m_w_up, v_w_up), ("w_down", wdown_r, w_down, m_w_down, v_w_down)):
        res[nm] = [r[None] for r in _adamw(parts, w[0], mm[0], vv[0], "adamw_" + nm)]
    own = _adamw(owned_r, owned, _pack_owned(m_meta_tokens, m_hg_lb_logits), _pack_owned(v_meta_tokens, v_hg_lb_logits),
                 "adamw_owned_small")
    res["meta_tokens"] = [r[0:NM] for r in own]
    res["hg_lb_logits"] = [r[NM:NM + 2].reshape(2, 2, 64) for r in own]
    rep = _adamw(repl_r, _pack_replicated(norm_mix, norm_mlp, norm_final, hg_norm, na_rpb),
                 _pack_replicated(m_norm_mix, m_norm_mlp, m_norm_final, m_hg_norm, m_na_rpb),
                 _pack_replicated(v_norm_mix, v_norm_mlp, v_norm_final, v_hg_norm, v_na_rpb), "adamw_replicated")
    for q in range(4):
        um = _unpack_replicated(rep[q])
        for nm, val in zip(("norm_mix", "norm_mlp", "norm_final", "hg_norm", "na_rpb"), um):
            res.setdefault(nm, [None] * 4)[q] = val

    loss = lax.psum(loss_part[0, 0], ("x", "y", "c"))
    grad_x = dh0[NM:L][None]
    order = ("meta_tokens", "w_in", "w_na_out", "w_hg_out", "w_o", "w_up", "w_down", "norm_mix", "norm_mlp", "norm_final",
             "hg_norm", "na_rpb", "hg_lb_logits")
    outs = [loss, grad_x]
    for q in range(4):
        outs += [res[nm][q] for nm in order]
    return tuple(outs)
```

```python
import functools

import numpy as np
import jax
import jax.numpy as jnp
from jax import lax
from jax.experimental import pallas as pl
from jax.experimental.pallas import tpu as pltpu

F32 = jnp.float32
BF16 = jnp.bfloat16

D = 1024
SEQ = 2048
NM = 16
L = SEQ + NM
T = 2176
NDEV = 8
EPS = 1e-6
GRID_W = 64
ROWS = SEQ // GRID_W
NA_HEADS = 8
NA_DH = 64
NA_SCALE = NA_DH ** -0.5
HG_HEADS = 4
HG_C = 16
NCHUNK = L // HG_C
D_FF = 4096
IN_COLS = 6144
NEG = -1e30

ADAM_LR = 0.001
ADAM_B1 = 0.9
ADAM_B2 = 0.999
ADAM_EPS = 1e-08
ADAM_WD = 0.01
ADAM_STEP = 10

MESH_ID = pl.DeviceIdType.MESH
ANY = pl.BlockSpec(memory_space=pl.ANY)

NN = (((1,), (0,)), ((), ()))
NT = (((1,), (1,)), ((), ()))
TN = (((0,), (0,)), ((), ()))


def _cp(sem=None, vmem_mb=48):
    return pltpu.CompilerParams(dimension_semantics=sem, vmem_limit_bytes=vmem_mb * 1024 * 1024)


def _dot(a, b, dims=NN):
    return lax.dot_general(a, b, dims, preferred_element_type=F32)


def _sds(shape, dtype):
    return jax.ShapeDtypeStruct(shape, dtype)


HBM = pl.BlockSpec(memory_space=pltpu.HBM)
SEM = pl.BlockSpec(memory_space=pltpu.SEMAPHORE)
EFFECT = pltpu.SideEffectType.DATAFLOW_SIDE_EFFECTING


def _exchange(arrs, scatter, name):
    n = len(arrs)
    out_shapes = []
    for a, sc in zip(arrs, scatter):
        out_shapes.append(_sds(a.shape if sc else (NDEV,) + a.shape, a.dtype))

    def body(*refs):
        ins, outs = refs[:n], refs[n:2 * n]
        send_sems, recv_sems, loc_sems = refs[2 * n:]
        me = 4 * lax.axis_index("x") + 2 * lax.axis_index("y") + lax.axis_index("c")
        copies = []
        for k in range(n):
            src_me = ins[k].at[me] if scatter[k] else ins[k]
            loc = pltpu.make_async_copy(src_me, outs[k].at[me], loc_sems.at[k])
            loc.start()
            copies.append(loc)
        remote = _peer_copies(ins, outs, scatter, send_sems, recv_sems)
        for cp in remote:
            cp.start()
        for cp in remote:
            cp.wait_recv()
        for cp in remote:
            cp.wait_send()
        for cp in copies:
            cp.wait()

    return pl.pallas_call(
        body, name=name, out_shape=tuple(out_shapes), in_specs=[ANY] * n, out_specs=tuple([ANY] * n),
        scratch_shapes=[pltpu.SemaphoreType.DMA((n * (NDEV - 1),)), pltpu.SemaphoreType.DMA((n * (NDEV - 1),)),
                        pltpu.SemaphoreType.DMA((n,))],
    )(*arrs)


def _gather_two_level(arrs, name):
    n = len(arrs)

    def body(*refs):
        ins, outs = refs[:n], refs[n:2 * n]
        send_sems, recv_sems, loc_sems = refs[2 * n:]
        x, y, c = lax.axis_index("x"), lax.axis_index("y"), lax.axis_index("c")
        sib = (x, y, 1 - c)
        chips = [(1 - x, y), (x, 1 - y), (1 - x, 1 - y)]

        def slot(k, px, py, pc):
            return outs[k].at[4 * px + 2 * py + pc]

        def copy(k, q, block, to, src=None):
            return pltpu.make_async_remote_copy(
                src_ref=slot(k, *block) if src is None else src, dst_ref=slot(k, *block),
                send_sem=send_sems.at[7 * k + q], recv_sem=recv_sems.at[7 * k + q], device_id=to, device_id_type=MESH_ID)

        mine = [pltpu.make_async_copy(ins[k], slot(k, x, y, c), loc_sems.at[k]) for k in range(n)]
        for cp in mine:
            cp.start()
        first = []
        for k in range(n):
            first.append(copy(k, 0, (x, y, c), sib, src=ins[k]))
            first += [copy(k, 1 + j, (x, y, c), (*chip, c), src=ins[k]) for j, chip in enumerate(chips)]
        for cp in first:
            cp.start()
        passed = []
        for j, chip in enumerate(chips):
            for k in range(n):
                copy(k, 1 + j, (*chip, c), (x, y, c)).wait_recv()
                fw = copy(k, 4 + j, (*chip, c), sib)
                fw.start()
                passed.append(fw)
        for k in range(n):
            copy(k, 0, (x, y, 1 - c), (x, y, c)).wait_recv()
            for j, chip in enumerate(chips):
                copy(k, 4 + j, (*chip, 1 - c), (x, y, c)).wait_recv()
        for cp in first + passed:
            cp.wait_send()
        for cp in mine:
            cp.wait()

    return pl.pallas_call(
        body, name=name, out_shape=tuple(_sds((NDEV,) + a.shape, a.dtype) for a in arrs),
        in_specs=[ANY] * n, out_specs=tuple([ANY] * n),
        scratch_shapes=[pltpu.SemaphoreType.DMA((7 * n,)), pltpu.SemaphoreType.DMA((7 * n,)), pltpu.SemaphoreType.DMA((n,))],
    )(*arrs)


def _peer_copies(srcs, lands, scatter, send_sems, recv_sems):
    x, y, c = lax.axis_index("x"), lax.axis_index("y"), lax.axis_index("c")
    me = 4 * x + 2 * y + c
    out = []
    for k in range(len(srcs)):
        for m in range(1, NDEV):
            px, py, pc = x ^ (m >> 2), y ^ ((m >> 1) & 1), c ^ (m & 1)
            src = srcs[k].at[4 * px + 2 * py + pc] if scatter[k] else srcs[k]
            out.append(pltpu.make_async_remote_copy(
                src_ref=src, dst_ref=lands[k].at[me], send_sem=send_sems.at[k * (NDEV - 1) + m - 1],
                recv_sem=recv_sems.at[k * (NDEV - 1) + m - 1],
                device_id=(px, py, pc), device_id_type=MESH_ID))
    return out


def _exchange_start(arrs, scatter, name):
    n = len(arrs)
    me = 4 * lax.axis_index("x") + 2 * lax.axis_index("y") + lax.axis_index("c")
    lands = []
    for a, sc in zip(arrs, scatter):
        own = lax.dynamic_index_in_dim(a, me, 0, keepdims=True) if sc else a[None]
        shape = a.shape if sc else (NDEV,) + a.shape
        lands.append(lax.dynamic_update_index_in_dim(lax.empty(shape, a.dtype), own, me, 0))

    def body(*refs):
        srcs, lnds = refs[:n], refs[n:2 * n]
        send_sems, recv_sems = refs[2 * n], refs[2 * n + 1]
        token = refs[-1]
        for cp in _peer_copies(srcs, lnds, scatter, send_sems, recv_sems):
            cp.start()
        token[...] = jnp.zeros_like(token)

    ops = [pltpu.with_memory_space_constraint(a, pltpu.HBM) for a in list(arrs) + lands]
    res = pl.pallas_call(
        body, name=name,
        out_shape=(pltpu.SemaphoreType.DMA((n * (NDEV - 1),)), pltpu.SemaphoreType.DMA((n * (NDEV - 1),)))
        + tuple(pltpu.HBM(o.shape, o.dtype) for o in ops) + (_sds((8, 128), F32),),
        in_specs=[HBM] * (2 * n), out_specs=(SEM, SEM) + (HBM,) * (2 * n) + (pl.BlockSpec(memory_space=pltpu.VMEM),),
        input_output_aliases={k: 2 + k for k in range(2 * n)},
        compiler_params=pltpu.CompilerParams(has_side_effects=EFFECT),
    )(*ops)
    return res[:-1], res[-1]


def _exchange_wait(handle, scatter, after, name):
    send_sems, recv_sems = handle[0], handle[1]
    bufs = handle[2:]
    n = len(bufs) // 2
    after = list(after)

    def body(*refs):
        srcs, lnds = refs[:n], refs[n:2 * n]
        for cp in _peer_copies(srcs, lnds, scatter, refs[2 * n], refs[2 * n + 1]):
            cp.wait_send()
            cp.wait_recv()

    res = pl.pallas_call(
        body, name=name, out_shape=tuple(pltpu.HBM(b.shape, b.dtype) for b in bufs),
        in_specs=[HBM] * (2 * n) + [SEM, SEM] + [ANY] * len(after), out_specs=(HBM,) * (2 * n),
        input_output_aliases={k: k for k in range(2 * n)},
        compiler_params=pltpu.CompilerParams(has_side_effects=EFFECT),
    )(*bufs, send_sems, recv_sems, *after)
    return res[n:]


def _tie(x, token, name):
    def body(x_ref, t_ref, o_ref):
        del x_ref, t_ref, o_ref

    return pl.pallas_call(body, name=name, out_shape=_sds(x.shape, x.dtype), in_specs=[ANY, ANY], out_specs=ANY,
                          input_output_aliases={0: 0})(x, token)


TM_E = 272


def _norm_fwd(h, g, name):
    def body(h_ref, g_ref, o_ref):
        xv = h_ref[...]
        r = lax.rsqrt(jnp.mean(xv * xv, axis=-1, keepdims=True) + EPS)
        o_ref[...] = (xv * r * g_ref[...]).astype(BF16)

    return pl.pallas_call(
        body, name=name, grid=(T // TM_E,),
        in_specs=[pl.BlockSpec((TM_E, D), lambda i: (i, 0)), pl.BlockSpec((1, D), lambda i: (0, 0))],
        out_specs=pl.BlockSpec((TM_E, D), lambda i: (i, 0)), out_shape=_sds((T, D), BF16),
        compiler_params=_cp(("parallel",)))(h, g)


def _norm_bwd(h, g, dn, dres, name):
    def body(h_ref, g_ref, dn_ref, dres_ref, dh_ref, dhb_ref, dg_ref):
        i = pl.program_id(0)
        xv = h_ref[...]
        r = lax.rsqrt(jnp.mean(xv * xv, axis=-1, keepdims=True) + EPS)
        xh = xv * r
        dnv = dn_ref[...].astype(F32)
        dxh = dnv * g_ref[...]
        dh = dres_ref[...] + r * (dxh - xh * jnp.mean(dxh * xh, axis=-1, keepdims=True))
        dh_ref[...] = dh
        dhb_ref[...] = dh.astype(BF16)
        part = jnp.sum(dnv * xh, axis=0, keepdims=True)

        @pl.when(i == 0)
        def _():
            dg_ref[...] = part

        @pl.when(i > 0)
        def _():
            dg_ref[...] += part

    blk = pl.BlockSpec((TM_E, D), lambda i: (i, 0))
    vec = pl.BlockSpec((1, D), lambda i: (0, 0))
    return pl.pallas_call(
        body, name=name, grid=(T // TM_E,), in_specs=[blk, vec, blk, blk], out_specs=(blk, blk, vec),
        out_shape=(_sds((T, D), F32), _sds((T, D), BF16), _sds((1, D), F32)),
        compiler_params=_cp(("arbitrary",)))(h, g, dn, dres)


TM_MM = 1088


def _inproj_fwd(a, w_g):
    nb = w_g.shape[2]

    def body(a_ref, w_ref, o_ref):
        o_ref[...] = _dot(a_ref[...], w_ref[0])

    return pl.pallas_call(
        body, name="inproj_fwd", grid=(T // TM_MM, NDEV),
        in_specs=[pl.BlockSpec((TM_MM, D), lambda i, j: (i, 0)), pl.BlockSpec((1, D, nb), lambda i, j: (j, 0, 0))],
        out_specs=pl.BlockSpec((TM_MM, nb), lambda i, j: (i, j)), out_shape=_sds((T, NDEV * nb), F32),
        compiler_params=_cp(("parallel", "parallel")))(a, w_g)


TM_B = 544


def _inproj_bwd(a, dp, w_g):
    nb = w_g.shape[2]
    ni = T // TM_B

    def body(a_ref, dp_ref, w_ref, dw_ref, da_ref, acc):
        j, i = pl.program_id(0), pl.program_id(1)
        av, dpv = a_ref[...], dp_ref[...]
        part = _dot(av, dpv, TN)

        @pl.when(i == 0)
        def _():
            acc[...] = part

        @pl.when(i > 0)
        def _():
            acc[...] += part

        @pl.when(i == ni - 1)
        def _():
            dw_ref[0] = acc[...].astype(BF16)

        rows = pl.ds(pl.multiple_of(i * TM_B, TM_B), TM_B)
        dav = _dot(dpv, w_ref[0], NT)

        @pl.when(j == 0)
        def _():
            da_ref[rows, :] = dav

        @pl.when(j > 0)
        def _():
            da_ref[rows, :] += dav

    return pl.pallas_call(
        body, name="inproj_bwd", grid=(NDEV, ni),
        in_specs=[pl.BlockSpec((TM_B, D), lambda j, i: (i, 0)), pl.BlockSpec((TM_B, nb), lambda j, i: (i, j)),
                  pl.BlockSpec((1, D, nb), lambda j, i: (j, 0, 0))],
        out_specs=(pl.BlockSpec((1, D, nb), lambda j, i: (j, 0, 0)), pl.BlockSpec((T, D), lambda j, i: (0, 0))),
        out_shape=(_sds((NDEV, D, nb), BF16), _sds((T, D), F32)),
        scratch_shapes=[pltpu.VMEM((D, nb), F32)],
        compiler_params=_cp(("arbitrary", "arbitrary")))(a, dp, w_g)


NA_QB = 256
NA_GROUPS = ROWS // 4
NA_UROWS = 11
NA_KW = NA_UROWS * GRID_W
NA_KU = 768


def _na_row_offset(var, i, j):
    valid = (j < 8, i <= j < i + 8, 3 <= j < NA_UROWS)[var]
    return (j - i + (7, 3, 0)[var]) if valid else None


def _na_bias_table(rpb):
    def body(r_ref, o_ref):
        row = lax.broadcasted_iota(jnp.int32, (GRID_W, 128), 0)
        lane = lax.broadcasted_iota(jnp.int32, (GRID_W, 128), 1)
        w = lane & (GRID_W - 1)
        cs = jnp.clip(row - 8, 0, GRID_W - 16)
        in_win = (w >= cs) & (w < cs + 16)
        neg = jnp.full((GRID_W, 128), NEG, F32)
        tabs = []
        for a in range(15):
            z = jnp.broadcast_to(r_ref[0, a:a + 1, :], (GRID_W, 128))
            for bit in range(6):
                sh = 1 << bit
                z = jnp.where((row & sh) != 0, jnp.roll(z, sh, axis=1), z)
            z = jnp.roll(z, 128 - 15, axis=1)
            z = jnp.where(lane < GRID_W, z, 0.0)
            z = z + jnp.roll(z, GRID_W, axis=1)
            tabs.append(jnp.where(in_win, z, NEG))
        tail = jnp.where(lane < GRID_W + NM, 0.0, NEG)
        for var in range(3):
            for i in range(4):
                for jp in range(NA_KU // 128):
                    halves = []
                    for j in (2 * jp, 2 * jp + 1):
                        a = _na_row_offset(var, i, j) if j < NA_UROWS else None
                        halves.append(tail if j >= NA_UROWS else (neg if a is None else tabs[a]))
                    o_ref[var, 0, i * 64:(i + 1) * 64, jp * 128:(jp + 1) * 128] = jnp.where(lane < GRID_W, halves[0], halves[1])

    rp = jnp.concatenate([rpb, jnp.zeros((NA_HEADS, 15, 128 - 31), F32)], axis=2)
    return pl.pallas_call(
        body, name="na_bias_table", grid=(NA_HEADS,),
        in_specs=[pl.BlockSpec((1, 15, 128), lambda h: (h, 0, 0))],
        out_specs=pl.BlockSpec((3, 1, NA_QB, NA_KU), lambda h: (0, h, 0, 0)),
        out_shape=_sds((3, NA_HEADS, NA_QB, NA_KU), F32), compiler_params=_cp(("parallel",)))(rp)


def _na_var(g):
    return jnp.where(g == 0, 0, jnp.where(g == NA_GROUPS - 1, 2, 1))


def _na_load_window(src_ref, dst, g):
    us = jnp.clip(4 * g - 4, 0, ROWS - NA_UROWS)
    kstart = pl.multiple_of(NM + GRID_W * us, 16)
    dst[0:NA_KW, :] = src_ref[pl.ds(kstart, NA_KW), :].astype(BF16)
    dst[NA_KW:NA_KW + NM, :] = src_ref[0:NM, :].astype(BF16)
    dst[NA_KW + NM:, :] = jnp.zeros((NA_KU - NA_KW - NM, 128), BF16)
    return kstart


def _na_fwd(p_act, bias_tab):
    def body(q_ref, k_ref, v_ref, b_ref, o_ref, lse_ref, ku, vu):
        g = pl.program_id(1)
        _na_load_window(k_ref, ku, g)
        _na_load_window(v_ref, vu, g)
        qstart = pl.multiple_of(NM + NA_QB * g, 16)
        q = q_ref[pl.ds(qstart, NA_QB), :]
        lane = lax.broadcasted_iota(jnp.int32, (NA_QB, 128), 1)
        o_h, lse_h = [], []
        for h in range(2):
            hm = (lane < 64) if h == 0 else (lane >= 64)
            qm = jnp.where(hm, q, 0.0).astype(BF16)
            s = _dot(qm, ku[...], NT) * NA_SCALE + b_ref[0, h]
            m = jnp.max(s, axis=-1, keepdims=True)
            p = jnp.exp(s - m)
            l = jnp.sum(p, axis=-1, keepdims=True)
            o_h.append(_dot(p.astype(BF16), vu[...]) / l)
            lse_h.append(jnp.broadcast_to(m + jnp.log(l), (NA_QB, 128)))
        o_ref[pl.ds(qstart, NA_QB), :] = jnp.where(lane < 64, o_h[0], o_h[1]).astype(BF16)
        lse_ref[0, pl.ds(qstart, NA_QB), :] = jnp.where(lane < 64, lse_h[0], lse_h[1])

        @pl.when(g == 0)
        def _():
            qm_ = q_ref[0:NM, :]
            lane_m = lax.broadcasted_iota(jnp.int32, (NM, 128), 1)
            km, vm = ku[NA_KW:NA_KW + NM, :], vu[NA_KW:NA_KW + NM, :]
            om = []
            for h in range(2):
                hm = (lane_m < 64) if h == 0 else (lane_m >= 64)
                s = _dot(jnp.where(hm, qm_, 0.0).astype(BF16), km, NT) * NA_SCALE
                p = jnp.exp(s - jnp.max(s, axis=-1, keepdims=True))
                l = jnp.sum(p, axis=-1, keepdims=True)
                om.append(_dot(p.astype(BF16), vm) / l)
            o_ref[0:NM, :] = jnp.where(lane_m < 64, om[0], om[1]).astype(BF16)
            o_ref[L:T, :] = jnp.zeros((T - L, 128), BF16)
            lse_ref[0, 0:NM, :] = jnp.zeros((NM, 128), F32)
            lse_ref[0, L:T, :] = jnp.zeros((T - L, 128), F32)

    col = lambda off: pl.BlockSpec((T, 128), lambda hp, g: (0, off + hp))
    return pl.pallas_call(
        body, name="na_fwd", grid=(4, NA_GROUPS),
        in_specs=[col(0), col(4), col(8),
                  pl.BlockSpec((1, 2, NA_QB, NA_KU), lambda hp, g: (_na_var(g), hp, 0, 0))],
        out_specs=(pl.BlockSpec((T, 128), lambda hp, g: (0, hp)), pl.BlockSpec((1, T, 128), lambda hp, g: (hp, 0, 0))),
        out_shape=(_sds((T, 512), BF16), _sds((4, T, 128), F32)),
        scratch_shapes=[pltpu.VMEM((NA_KU, 128), BF16), pltpu.VMEM((NA_KU, 128), BF16)],
        compiler_params=_cp(("parallel", "arbitrary")))(p_act, p_act, p_act, bias_tab)


def _na_bwd(p_act, do, lse, bias_tab):
    def body(q_ref, k_ref, v_ref, do_ref, lse_ref, b_ref, dq_ref, dk_ref, dv_ref, db_ref, ku, vu):
        g = pl.program_id(1)

        @pl.when(g == 0)
        def _():
            dq_ref[...] = jnp.zeros((T, 128), F32)
            dk_ref[...] = jnp.zeros((T, 128), F32)
            dv_ref[...] = jnp.zeros((T, 128), F32)

        kstart = _na_load_window(k_ref, ku, g)
        _na_load_window(v_ref, vu, g)
        qstart = pl.multiple_of(NM + NA_QB * g, 16)
        q = q_ref[pl.ds(qstart, NA_QB), :]
        dov = do_ref[pl.ds(qstart, NA_QB), :]
        lsev = lse_ref[0, pl.ds(qstart, NA_QB), :]
        lane = lax.broadcasted_iota(jnp.int32, (NA_QB, 128), 1)
        first = (g == 0) | (g == 1) | (g == NA_GROUPS - 1)
        dq_h = []
        dku = jnp.zeros((NA_KU, 128), F32)
        dvu = jnp.zeros((NA_KU, 128), F32)
        for h in range(2):
            hm = (lane < 64) if h == 0 else (lane >= 64)
            qm = jnp.where(hm, q, 0.0).astype(BF16)
            dom = jnp.where(hm, dov, 0.0).astype(BF16)
            s = _dot(qm, ku[...], NT) * NA_SCALE + b_ref[0, h]
            p = jnp.exp(s - lsev[:, 64 * h:64 * h + 1])
            dp = _dot(dom, vu[...], NT)
            delta = jnp.sum(p * dp, axis=-1, keepdims=True)
            ds = p * (dp - delta)

            @pl.when(first)
            def _():
                db_ref[0, h] = ds

            @pl.when(jnp.logical_not(first))
            def _():
                db_ref[0, h] += ds

            dsb = (ds * NA_SCALE).astype(BF16)
            dq_h.append(_dot(dsb, ku[...]))
            dku = dku + _dot(dsb, qm, TN)
            dvu = dvu + _dot(p.astype(BF16), dom, TN)
        dq_ref[pl.ds(qstart, NA_QB), :] = jnp.where(lane < 64, dq_h[0], dq_h[1])
        dk_ref[pl.ds(kstart, NA_KW), :] += dku[0:NA_KW]
        dv_ref[pl.ds(kstart, NA_KW), :] += dvu[0:NA_KW]
        dk_ref[0:NM, :] += dku[NA_KW:NA_KW + NM]
        dv_ref[0:NM, :] += dvu[NA_KW:NA_KW + NM]

        @pl.when(g == 0)
        def _():
            qm_ = q_ref[0:NM, :]
            dom_ = do_ref[0:NM, :]
            lane_m = lax.broadcasted_iota(jnp.int32, (NM, 128), 1)
            km, vm = ku[NA_KW:NA_KW + NM, :], vu[NA_KW:NA_KW + NM, :]
            dqs = []
            dkm = jnp.zeros((NM, 128), F32)
            dvm = jnp.zeros((NM, 128), F32)
            for h in range(2):
                hm = (lane_m < 64) if h == 0 else (lane_m >= 64)
                qh = jnp.where(hm, qm_, 0.0).astype(BF16)
                doh = jnp.where(hm, dom_, 0.0).astype(BF16)
                s = _dot(qh, km, NT) * NA_SCALE
                e = jnp.exp(s - jnp.max(s, axis=-1, keepdims=True))
                p = e / jnp.sum(e, axis=-1, keepdims=True)
                dp = _dot(doh, vm, NT)
                ds = p * (dp - jnp.sum(p * dp, axis=-1, keepdims=True))
                dsb = (ds * NA_SCALE).astype(BF16)
                dqs.append(_dot(dsb, km))
                dkm = dkm + _dot(dsb, qh, TN)
                dvm = dvm + _dot(p.astype(BF16), doh, TN)
            dq_ref[0:NM, :] = jnp.where(lane_m < 64, dqs[0], dqs[1])
            dk_ref[0:NM, :] += dkm
            dv_ref[0:NM, :] += dvm

    col = lambda off: pl.BlockSpec((T, 128), lambda hp, g: (0, off + hp))
    ocol = pl.BlockSpec((T, 128), lambda hp, g: (0, hp))
    bspec = pl.BlockSpec((1, 2, NA_QB, NA_KU), lambda hp, g: (_na_var(g), hp, 0, 0))
    return pl.pallas_call(
        body, name="na_bwd", grid=(4, NA_GROUPS),
        in_specs=[col(0), col(4), col(8), ocol, pl.BlockSpec((1, T, 128), lambda hp, g: (hp, 0, 0)), bspec],
        out_specs=(ocol, ocol, ocol, bspec),
        out_shape=(_sds((T, 512), F32), _sds((T, 512), F32), _sds((T, 512), F32), _sds((3, NA_HEADS, NA_QB, NA_KU), F32)),
        scratch_shapes=[pltpu.VMEM((NA_KU, 128), BF16), pltpu.VMEM((NA_KU, 128), BF16)],
        compiler_params=_cp(("parallel", "arbitrary")))(p_act, p_act, p_act, do, lse, bias_tab)


def _na_rpb_reduce(dbias):
    def body(db_ref, o_ref):
        row = lax.broadcasted_iota(jnp.int32, (GRID_W, 128), 0)
        for a in range(15):
            acc = jnp.zeros((GRID_W, GRID_W), F32)
            for var in range(3):
                for i in range(4):
                    for j in range(NA_UROWS):
                        if _na_row_offset(var, i, j) == a:
                            pair = db_ref[var, 0, i * 64:(i + 1) * 64, (j // 2) * 128:(j // 2 + 1) * 128]
                            acc = acc + pair[:, (j % 2) * 64:(j % 2 + 1) * 64]
            z = jnp.concatenate([acc, jnp.zeros((GRID_W, 128 - GRID_W), F32)], axis=1)
            for bit in range(6):
                sh = 1 << bit
                z = jnp.where((row & sh) != 0, jnp.roll(z, 128 - sh, axis=1), z)
            z = jnp.roll(z, 15, axis=1)
            o_ref[0, a:a + 1, :] = jnp.sum(z, axis=0, keepdims=True)

    return pl.pallas_call(
        body, name="na_rpb_reduce", grid=(NA_HEADS,),
        in_specs=[pl.BlockSpec((3, 1, NA_QB, NA_KU), lambda h: (0, h, 0, 0))],
        out_specs=pl.BlockSpec((1, 15, 128), lambda h: (h, 0, 0)), out_shape=_sds((NA_HEADS, 15, 128), F32),
        compiler_params=_cp(("parallel",)))(dbias)


HG_RB = 128
HG_NB = T // HG_RB
HG_SLOTS = HG_NB * 8
HI = lax.Precision.HIGHEST


def _chunk_tri(lower):
    r = lax.broadcasted_iota(jnp.int32, (HG_RB, HG_RB), 0)
    c = lax.broadcasted_iota(jnp.int32, (HG_RB, HG_RB), 1)
    same = (r // HG_C) == (c // HG_C)
    keep = (c <= r) if lower else (c >= r)
    return jnp.where(same & keep, 1.0, 0.0).astype(F32)


def _hg_gate_terms(z, lg):
    dl = lg[0:1, :] - lg[1:2, :]
    log_lb = jax.nn.log_sigmoid(dl)
    log_1mlb = jax.nn.log_sigmoid(-dl)
    yz = log_1mlb + jax.nn.log_sigmoid(z)
    log_f = jnp.logaddexp(log_lb, yz)
    snz = jax.nn.sigmoid(-z)
    k = jnp.exp(log_1mlb) * snz
    w2 = jnp.exp(yz - log_f)
    return log_f, k, snz, w2


def _hg_pre(p_act, logits):
    def body(q_ref, zf_ref, zb_ref, lg_ref, qh_ref, kf_ref, bf_ref, kb_ref, bb_ref):
        qh_ref[...] = jax.nn.silu(q_ref[...])
        lf, kf, _, _ = _hg_gate_terms(zf_ref[...], lg_ref[0])
        kf_ref[...] = kf
        bf_ref[...] = jnp.dot(_chunk_tri(True), lf, precision=HI, preferred_element_type=F32)
        lb_, kb, _, _ = _hg_gate_terms(zb_ref[...], lg_ref[1])
        kb_ref[...] = kb
        bb_ref[...] = jnp.dot(_chunk_tri(False), lb_, precision=HI, preferred_element_type=F32)

    blk = lambda c: pl.BlockSpec((HG_RB, 512), lambda i: (i, c))
    ob = pl.BlockSpec((HG_RB, 512), lambda i: (i, 0))
    return pl.pallas_call(
        body, name="hg_pre", grid=(HG_NB,),
        in_specs=[blk(3), blk(4), blk(5), pl.BlockSpec((2, 2, 512), lambda i: (0, 0, 0))],
        out_specs=(ob,) * 5, out_shape=(_sds((T, 512), F32),) * 5,
        compiler_params=_cp(("parallel",)))(p_act, p_act, p_act, logits)


def _bdot(a, b, ca, cb):
    return lax.dot_general(a.astype(BF16), b.astype(BF16), (((ca,), (cb,)), ((0,), (0,))), preferred_element_type=F32)


def _hg_scan_fwd(qh, k, b, p_act, rev):
    anchor = 0 if rev else HG_C - 1

    def body(q_ref, k_ref, b_ref, v_ref, o_ref, st_ref, dsc):
        def phase_a(blk, _):
            rows = pl.ds(pl.multiple_of(blk * HG_RB, HG_RB), HG_RB)
            b3 = b_ref[rows, :].reshape(8, HG_C, 128)
            k3 = k_ref[rows, :].reshape(8, HG_C, 128)
            v3 = v_ref[rows, :].reshape(8, HG_C, 128)
            bl = b3[:, anchor:anchor + 1, :]
            kt = k3 * jnp.exp(bl - b3)
            st_ref[0, pl.ds(pl.multiple_of(blk * 8, 8), 8)] = _bdot(v3, kt, 1, 1)
            dsc[pl.ds(pl.multiple_of(blk * 8, 8), 8), :] = jnp.exp(bl[:, 0, :])
            return 0

        lax.fori_loop(0, HG_NB, phase_a, 0)

        def phase_b(n, carry):
            c = (NCHUNK - 1 - n) if rev else n
            u = st_ref[0, c]
            st_ref[0, c] = carry
            return carry * dsc[pl.ds(c, 1), :] + u

        lax.fori_loop(0, NCHUNK, phase_b, jnp.zeros((128, 128), F32))
        for c in range(NCHUNK, HG_SLOTS):
            st_ref[0, c] = jnp.zeros((128, 128), F32)

        t_io = lax.broadcasted_iota(jnp.int32, (8, HG_C, 128), 1)
        l_io = lax.broadcasted_iota(jnp.int32, (8, HG_C, HG_C), 2)

        def phase_c(blk, _):
            rows = pl.ds(pl.multiple_of(blk * HG_RB, HG_RB), HG_RB)
            b3 = b_ref[rows, :].reshape(8, HG_C, 128)
            k3 = k_ref[rows, :].reshape(8, HG_C, 128)
            q3 = q_ref[rows, :].reshape(8, HG_C, 128)
            v3 = v_ref[rows, :].reshape(8, HG_C, 128)
            st = st_ref[0, pl.ds(pl.multiple_of(blk * 8, 8), 8)]
            o = _bdot(q3 * jnp.exp(b3), st, 2, 2)
            a = jnp.zeros((8, HG_C, HG_C), F32)
            for s in range(HG_C):
                ok = (t_io <= s) if rev else (t_io >= s)
                f = jnp.exp(jnp.where(ok, b3 - b3[:, s:s + 1, :], NEG))
                col = jnp.sum(q3 * f * k3[:, s:s + 1, :], axis=-1, keepdims=True)
                a = a + jnp.where(l_io == s, col, 0.0)
            o = o + _bdot(a, v3, 2, 1)
            o_ref[rows, :] = o.reshape(HG_RB, 128)
            return 0

        lax.fori_loop(0, HG_NB, phase_c, 0)

    col = pl.BlockSpec((T, 128), lambda h: (0, h))
    return pl.pallas_call(
        body, name="hg_scan_bwd_dir" if rev else "hg_scan_fwd_dir", grid=(HG_HEADS,),
        in_specs=[col, col, col, pl.BlockSpec((T, 128), lambda h: (0, 24 + h))],
        out_specs=(col, pl.BlockSpec((1, HG_SLOTS, 128, 128), lambda h: (h, 0, 0, 0))),
        out_shape=(_sds((T, 512), F32), _sds((HG_HEADS, HG_SLOTS, 128, 128), F32)),
        scratch_shapes=[pltpu.VMEM((HG_SLOTS, 128), F32)],
        compiler_params=_cp(("parallel",), 56))(qh, k, b, p_act)


def _hg_scan_bwd(qh, k, b, p_act, st, do, rev):
    anchor = 0 if rev else HG_C - 1

    def body(q_ref, k_ref, b_ref, v_ref, st_ref, do_ref, dq_ref, dk_ref, db_ref, dv_ref, gst, dsc, dbl):
        def phase_a(blk, _):
            rows = pl.ds(pl.multiple_of(blk * HG_RB, HG_RB), HG_RB)
            b3 = b_ref[rows, :].reshape(8, HG_C, 128)
            q3 = q_ref[rows, :].reshape(8, HG_C, 128)
            do3 = do_ref[rows, :].reshape(8, HG_C, 128)
            gst[pl.ds(pl.multiple_of(blk * 8, 8), 8)] = _bdot(do3, q3 * jnp.exp(b3), 1, 1)
            dsc[pl.ds(pl.multiple_of(blk * 8, 8), 8), :] = jnp.exp(b3[:, anchor, :])
            return 0

        lax.fori_loop(0, HG_NB, phase_a, 0)

        def phase_b(n, carry):
            c = n if rev else (NCHUNK - 1 - n)
            w = gst[c]
            gst[c] = carry
            dcv = dsc[pl.ds(c, 1), :]
            dbl[pl.ds(c, 1), :] = dcv * jnp.sum(st_ref[0, c] * carry, axis=0, keepdims=True)
            return carry * dcv + w

        lax.fori_loop(0, NCHUNK, phase_b, jnp.zeros((128, 128), F32))
        for c in range(NCHUNK, HG_SLOTS):
            gst[c] = jnp.zeros((128, 128), F32)
            dbl[c:c + 1, :] = jnp.zeros((1, 128), F32)

        t_io = lax.broadcasted_iota(jnp.int32, (8, HG_C, 128), 1)
        r_io = lax.broadcasted_iota(jnp.int32, (8, HG_C, HG_C), 1)
        l_io = lax.broadcasted_iota(jnp.int32, (8, HG_C, HG_C), 2)

        def phase_c(blk, _):
            rows = pl.ds(pl.multiple_of(blk * HG_RB, HG_RB), HG_RB)
            cs = pl.ds(pl.multiple_of(blk * 8, 8), 8)
            b3 = b_ref[rows, :].reshape(8, HG_C, 128)
            k3 = k_ref[rows, :].reshape(8, HG_C, 128)
            q3 = q_ref[rows, :].reshape(8, HG_C, 128)
            v3 = v_ref[rows, :].reshape(8, HG_C, 128)
            do3 = do_ref[rows, :].reshape(8, HG_C, 128)
            s_t = st_ref[0, cs]
            g_t = gst[cs]
            bl = b3[:, anchor:anchor + 1, :]
            ekl = jnp.exp(bl - b3)
            kt = k3 * ekl
            dqt = _bdot(do3, s_t, 2, 1)
            dkt = _bdot(v3, g_t, 2, 1)
            dv = _bdot(kt, g_t, 2, 2)
            causal = (l_io >= r_io) if rev else (l_io <= r_io)
            da = jnp.where(causal, _bdot(do3, v3, 2, 2), 0.0)
            causal_t = (l_io <= r_io) if rev else (l_io >= r_io)
            dat = jnp.where(causal_t, _bdot(v3, do3, 2, 2), 0.0)
            dq = dqt * jnp.exp(b3)
            dk = dkt * ekl
            at = jnp.zeros((8, HG_C, HG_C), F32)
            for s in range(HG_C):
                ok = (t_io <= s) if rev else (t_io >= s)
                f = jnp.exp(jnp.where(ok, b3 - b3[:, s:s + 1, :], NEG))
                dq = dq + da[:, :, s:s + 1] * (f * k3[:, s:s + 1, :])
            for t in range(HG_C):
                ok = (t_io >= t) if rev else (t_io <= t)
                e = jnp.exp(jnp.where(ok, b3[:, t:t + 1, :] - b3, NEG))
                eq = e * q3[:, t:t + 1, :]
                dk = dk + dat[:, :, t:t + 1] * eq
                at = at + jnp.where(l_io == t, jnp.sum(eq * k3, axis=-1, keepdims=True), 0.0)
            dv = dv + _bdot(at, do3, 2, 1)
            dbl3 = dbl[cs, :].reshape(8, 1, 128) + jnp.sum(dkt * kt, axis=1, keepdims=True)
            db = q3 * dq - k3 * dk + jnp.where(t_io == anchor, dbl3, 0.0)
            dq_ref[rows, :] = dq.reshape(HG_RB, 128)
            dk_ref[rows, :] = dk.reshape(HG_RB, 128)
            db_ref[rows, :] = db.reshape(HG_RB, 128)
            dv_ref[rows, :] = dv.reshape(HG_RB, 128)
            return 0

        lax.fori_loop(0, HG_NB, phase_c, 0)

    col = pl.BlockSpec((T, 128), lambda h: (0, h))
    return pl.pallas_call(
        body, name="hg_scan_bwd_dir_bwd" if rev else "hg_scan_fwd_dir_bwd", grid=(HG_HEADS,),
        in_specs=[col, col, col, pl.BlockSpec((T, 128), lambda h: (0, 24 + h)),
                  pl.BlockSpec((1, HG_SLOTS, 128, 128), lambda h: (h, 0, 0, 0)), col],
        out_specs=(col,) * 4, out_shape=(_sds((T, 512), F32),) * 4,
        scratch_shapes=[pltpu.VMEM((HG_SLOTS, 128, 128), F32), pltpu.VMEM((HG_SLOTS, 128), F32),
                        pltpu.VMEM((HG_SLOTS, 128), F32)],
        compiler_params=_cp(("parallel",), 56))(qh, k, b, p_act, st, do)


def _row_valid(i, tm):
    r = lax.broadcasted_iota(jnp.int32, (tm, 1), 0) + i * tm
    return r < L


def _hg_post(o_f, o_b, p_act, gain):
    def body(of_ref, ob_ref, g_ref, gain_ref, u_ref):
        o = of_ref[...] + ob_ref[...]
        sg = jax.nn.silu(g_ref[...])
        parts = []
        for h in range(HG_HEADS):
            oh = o[:, 128 * h:128 * (h + 1)]
            parts.append(oh * lax.rsqrt(jnp.mean(oh * oh, axis=-1, keepdims=True) + EPS))
        n = jnp.concatenate(parts, axis=1)
        u = n * gain_ref[...] * sg
        u_ref[...] = jnp.where(_row_valid(pl.program_id(0), TM_E), u, 0.0).astype(BF16)

    blk = pl.BlockSpec((TM_E, 512), lambda i: (i, 0))
    return pl.pallas_call(
        body, name="hg_post", grid=(T // TM_E,),
        in_specs=[blk, blk, pl.BlockSpec((TM_E, 512), lambda i: (i, 7)), pl.BlockSpec((1, 512), lambda i: (0, 0))],
        out_specs=blk, out_shape=_sds((T, 512), BF16), compiler_params=_cp(("parallel",)))(o_f, o_b, p_act, gain)


def _hg_post_bwd(du, o_f, o_b, p_act, gain):
    def body(du_ref, of_ref, ob_ref, g_ref, gain_ref, do_ref, dg_ref, dgain_ref):
        i = pl.program_id(0)
        valid = _row_valid(i, TM_E)
        duv = jnp.where(valid, du_ref[...], 0.0)
        o = of_ref[...] + ob_ref[...]
        gv = g_ref[...]
        sig = jax.nn.sigmoid(gv)
        sg = gv * sig
        gain_v = gain_ref[...]
        dn = duv * gain_v * sg
        do_parts, n_parts = [], []
        for h in range(HG_HEADS):
            sl = slice(128 * h, 128 * (h + 1))
            oh = o[:, sl]
            r = lax.rsqrt(jnp.mean(oh * oh, axis=-1, keepdims=True) + EPS)
            nh = oh * r
            dnh = dn[:, sl]
            do_parts.append(r * (dnh - nh * jnp.mean(dnh * nh, axis=-1, keepdims=True)))
            n_parts.append(nh)
        n = jnp.where(valid, jnp.concatenate(n_parts, axis=1), 0.0)
        do_ref[...] = jnp.where(valid, jnp.concatenate(do_parts, axis=1), 0.0)
        dg_ref[...] = (duv * n * gain_v * (sig * (1.0 + gv * (1.0 - sig)))).astype(BF16)
        part = jnp.sum(duv * n * sg, axis=0, keepdims=True)

        @pl.when(i == 0)
        def _():
            dgain_ref[...] = part

        @pl.when(i > 0)
        def _():
            dgain_ref[...] += part

    blk = pl.BlockSpec((TM_E, 512), lambda i: (i, 0))
    vec = pl.BlockSpec((1, 512), lambda i: (0, 0))
    return pl.pallas_call(
        body, name="hg_post_bwd", grid=(T // TM_E,),
        in_specs=[blk, blk, blk, pl.BlockSpec((TM_E, 512), lambda i: (i, 7)), vec],
        out_specs=(blk, blk, vec), out_shape=(_sds((T, 512), F32), _sds((T, 512), BF16), _sds((1, 512), F32)),
        compiler_params=_cp(("arbitrary",)))(du, o_f, o_b, p_act, gain)


def _hg_pre_bwd(p_act, logits, dq_f, dq_b, dk_f, dk_b, db_f, db_b, dv_f, dv_b):
    def body(q_ref, zf_ref, zb_ref, lg_ref, dqf_ref, dqb_ref, dkf_ref, dkb_ref, dbf_ref, dbb_ref, dvf_ref, dvb_ref,
             dq_ref, dzf_ref, dzb_ref, di_ref, dlg_ref):
        i = pl.program_id(0)
        valid = _row_valid(i, HG_RB)
        qv = q_ref[...]
        sig = jax.nn.sigmoid(qv)
        dq_ref[...] = jnp.where(valid, (dqf_ref[...] + dqb_ref[...]) * (sig * (1.0 + qv * (1.0 - sig))), 0.0).astype(BF16)
        di_ref[...] = jnp.where(valid, dvf_ref[...] + dvb_ref[...], 0.0).astype(BF16)
        for d, (z_ref, dk_r, db_r, dz_ref) in enumerate(((zf_ref, dkf_ref, dbf_ref, dzf_ref), (zb_ref, dkb_ref, dbb_ref, dzb_ref))):
            lg = lg_ref[d]
            dl = lg[0:1, :] - lg[1:2, :]
            lb = jax.nn.sigmoid(dl)
            one_m_lb = jax.nn.sigmoid(-dl)
            log_f, _, snz, w2 = _hg_gate_terms(z_ref[...], lg)
            dbv = jnp.where(valid, db_r[...], 0.0)
            dkv = jnp.where(valid, dk_r[...], 0.0)
            dlf = jnp.dot(_chunk_tri(d == 1), dbv, precision=HI, preferred_element_type=F32)
            sz = 1.0 - snz
            dz_ref[...] = (dlf * w2 * snz - dkv * one_m_lb * sz * snz).astype(BF16)
            dlb = jnp.sum(dlf * snz * jnp.exp(-log_f) - dkv * snz, axis=0, keepdims=True)
            dl0 = dlb * lb * one_m_lb
            part = jnp.concatenate([dl0, -dl0], axis=0)

            @pl.when(i == 0)
            def _():
                dlg_ref[d] = part

            @pl.when(i > 0)
            def _():
                dlg_ref[d] += part

    blk = lambda c: pl.BlockSpec((HG_RB, 512), lambda i: (i, c))
    ob = pl.BlockSpec((HG_RB, 512), lambda i: (i, 0))
    lgs = pl.BlockSpec((2, 2, 512), lambda i: (0, 0, 0))
    return pl.pallas_call(
        body, name="hg_pre_bwd", grid=(HG_NB,),
        in_specs=[blk(3), blk(4), blk(5), lgs] + [ob] * 8,
        out_specs=(ob, ob, ob, ob, lgs),
        out_shape=(_sds((T, 512), BF16),) * 4 + (_sds((2, 2, 512), F32),),
        compiler_params=_cp(("arbitrary",)))(p_act, p_act, p_act, logits, dq_f, dq_b, dk_f, dk_b, db_f, db_b, dv_f, dv_b)


def _mix_fwd(o_na, u_hg, wna_g, whg_g, p_act):
    def body(ona_ref, uhg_ref, wna_ref, whg_ref, gna_ref, ghg_ref, o_ref):
        y_na = _dot(ona_ref[...], wna_ref[0])
        y_hg = _dot(uhg_ref[...], whg_ref[0])
        o_ref[...] = (jax.nn.sigmoid(gna_ref[...]) * y_na + jax.nn.sigmoid(ghg_ref[...]) * y_hg).astype(BF16)

    act = pl.BlockSpec((TM_MM, 512), lambda i, j: (i, 0))
    wsp = pl.BlockSpec((1, 512, 128), lambda i, j: (j, 0, 0))
    return pl.pallas_call(
        body, name="mix_fwd", grid=(T // TM_MM, NDEV),
        in_specs=[act, act, wsp, wsp, pl.BlockSpec((TM_MM, 128), lambda i, j: (i, 32 + j)),
                  pl.BlockSpec((TM_MM, 128), lambda i, j: (i, 40 + j))],
        out_specs=pl.BlockSpec((TM_MM, 128), lambda i, j: (i, j)), out_shape=_sds((T, D), BF16),
        compiler_params=_cp(("parallel", "parallel")))(o_na, u_hg, wna_g, whg_g, p_act, p_act)


def _mix_bwd(o_na, u_hg, wna_g, whg_g, p_act, dmix):
    ni = T // TM_B

    def body(ona_ref, uhg_ref, wna_ref, whg_ref, gna_ref, ghg_ref, dmix_ref,
             dgna_ref, dghg_ref, dwna_ref, dwhg_ref, dona_ref, duhg_ref, acc_na, acc_hg):
        j, i = pl.program_id(0), pl.program_id(1)
        rows = pl.ds(pl.multiple_of(i * TM_B, TM_B), TM_B)
        dm = dmix_ref[...].astype(F32)
        for x_ref, w_ref, g_ref, dg_ref, dx_ref, dw_ref, acc in (
                (ona_ref, wna_ref, gna_ref, dgna_ref, dona_ref, dwna_ref, acc_na),
                (uhg_ref, whg_ref, ghg_ref, dghg_ref, duhg_ref, dwhg_ref, acc_hg)):
            xv = x_ref[...]
            y = _dot(xv, w_ref[0])
            sg = jax.nn.sigmoid(g_ref[...])
            dg_ref[...] = (dm * y * sg * (1.0 - sg)).astype(BF16)
            dy = (dm * sg).astype(BF16)
            part = _dot(xv, dy, TN)
            dxv = _dot(dy, w_ref[0], NT)

            @pl.when(i == 0)
            def _():
                acc[...] = part

            @pl.when(i > 0)
            def _():
                acc[...] += part

            @pl.when(i == ni - 1)
            def _():
                dw_ref[0] = acc[...].astype(BF16)

            @pl.when(j == 0)
            def _():
                dx_ref[rows, :] = dxv

            @pl.when(j > 0)
            def _():
                dx_ref[rows, :] += dxv

    act = pl.BlockSpec((TM_B, 512), lambda j, i: (i, 0))
    wsp = pl.BlockSpec((1, 512, 128), lambda j, i: (j, 0, 0))
    cblk = pl.BlockSpec((TM_B, 128), lambda j, i: (i, j))
    full = pl.BlockSpec((T, 512), lambda j, i: (0, 0))
    return pl.pallas_call(
        body, name="mix_bwd", grid=(NDEV, ni),
        in_specs=[act, act, wsp, wsp, pl.BlockSpec((TM_B, 128), lambda j, i: (i, 32 + j)),
                  pl.BlockSpec((TM_B, 128), lambda j, i: (i, 40 + j)), cblk],
        out_specs=(cblk, cblk, wsp, wsp, full, full),
        out_shape=(_sds((T, D), BF16), _sds((T, D), BF16), _sds((NDEV, 512, 128), BF16), _sds((NDEV, 512, 128), BF16),
                   _sds((T, 512), F32), _sds((T, 512), F32)),
        scratch_shapes=[pltpu.VMEM((512, 128), F32), pltpu.VMEM((512, 128), F32)],
        compiler_params=_cp(("arbitrary", "arbitrary")))(o_na, u_hg, wna_g, whg_g, p_act, p_act, dmix)


def _wo_fwd(mix, w_o, h0, g_mlp):
    def body(mix_ref, w_ref, h0_ref, g_ref, h1_ref, m_ref):
        h1 = h0_ref[...] + _dot(mix_ref[...], w_ref[...])
        h1_ref[...] = h1
        r = lax.rsqrt(jnp.mean(h1 * h1, axis=-1, keepdims=True) + EPS)
        m_ref[...] = (h1 * r * g_ref[...]).astype(BF16)

    blk = pl.BlockSpec((TM_B, D), lambda i: (i, 0))
    return pl.pallas_call(
        body, name="wo_fwd", grid=(T // TM_B,),
        in_specs=[blk, pl.BlockSpec((D, D), lambda i: (0, 0)), blk, pl.BlockSpec((1, D), lambda i: (0, 0))],
        out_specs=(blk, blk), out_shape=(_sds((T, D), F32), _sds((T, D), BF16)),
        compiler_params=_cp(("parallel",)))(mix, w_o, h0, g_mlp)


def _wo_bwd(dh1_b, w_o, mix):
    ni = T // TM_B

    def body(dh_ref, w_ref, mix_ref, dmix_ref, dw_ref, acc):
        i = pl.program_id(0)
        dh = dh_ref[...]
        dmix_ref[...] = _dot(dh, w_ref[...], NT).astype(BF16)
        part = _dot(mix_ref[...], dh, TN)

        @pl.when(i == 0)
        def _():
            acc[...] = part

        @pl.when(i > 0)
        def _():
            acc[...] += part

        @pl.when(i == ni - 1)
        def _():
            dw_ref[...] = acc[...].astype(BF16)

    blk = pl.BlockSpec((TM_B, D), lambda i: (i, 0))
    wsp = pl.BlockSpec((D, D), lambda i: (0, 0))
    return pl.pallas_call(
        body, name="wo_bwd", grid=(ni,), in_specs=[blk, wsp, blk], out_specs=(blk, wsp),
        out_shape=(_sds((T, D), BF16), _sds((D, D), BF16)), scratch_shapes=[pltpu.VMEM((D, D), F32)],
        compiler_params=_cp(("arbitrary",)))(dh1_b, w_o, mix)


FF_B = D_FF // NDEV


def _mlp_fwd(m, wup_g, wdown_g, h1):
    def body(m_ref, wu_ref, wd_ref, h1_ref, h2_ref):
        j = pl.program_id(1)
        up = jnp.maximum(_dot(m_ref[...], wu_ref[0]), 0.0)
        part = _dot((up * up).astype(BF16), wd_ref[0])

        @pl.when(j == 0)
        def _():
            h2_ref[...] = h1_ref[...] + part

        @pl.when(j > 0)
        def _():
            h2_ref[...] += part

    blk = pl.BlockSpec((TM_MM, D), lambda i, j: (i, 0))
    return pl.pallas_call(
        body, name="mlp_fwd", grid=(T // TM_MM, NDEV),
        in_specs=[blk, pl.BlockSpec((1, D, FF_B), lambda i, j: (j, 0, 0)), pl.BlockSpec((1, FF_B, D), lambda i, j: (j, 0, 0)), blk],
        out_specs=blk, out_shape=_sds((T, D), F32),
        compiler_params=_cp(("parallel", "arbitrary")))(m, wup_g, wdown_g, h1)


def _mlp_bwd(m, dh2_b, wup_g, wdown_g):
    ni = T // TM_B

    def body(m_ref, dh_ref, wu_ref, wd_ref, dwu_ref, dwd_ref, dm_ref, acc_u, acc_d):
        j, i = pl.program_id(0), pl.program_id(1)
        rows = pl.ds(pl.multiple_of(i * TM_B, TM_B), TM_B)
        mv, dh = m_ref[...], dh_ref[...]
        r = jnp.maximum(_dot(mv, wu_ref[0]), 0.0)
        act = (r * r).astype(BF16)
        dact = _dot(dh, wd_ref[0], NT)
        dup = (dact * (2.0 * r)).astype(BF16)
        pd = _dot(act, dh, TN)
        pu = _dot(mv, dup, TN)
        dmv = _dot(dup, wu_ref[0], NT)

        @pl.when(i == 0)
        def _():
            acc_u[...] = pu
            acc_d[...] = pd

        @pl.when(i > 0)
        def _():
            acc_u[...] += pu
            acc_d[...] += pd

        @pl.when(i == ni - 1)
        def _():
            dwu_ref[0] = acc_u[...].astype(BF16)
            dwd_ref[0] = acc_d[...].astype(BF16)

        @pl.when(j == 0)
        def _():
            dm_ref[rows, :] = dmv

        @pl.when(j > 0)
        def _():
            dm_ref[rows, :] += dmv

    blk = pl.BlockSpec((TM_B, D), lambda j, i: (i, 0))
    wus = pl.BlockSpec((1, D, FF_B), lambda j, i: (j, 0, 0))
    wds = pl.BlockSpec((1, FF_B, D), lambda j, i: (j, 0, 0))
    return pl.pallas_call(
        body, name="mlp_bwd", grid=(NDEV, ni), in_specs=[blk, blk, wus, wds],
        out_specs=(wus, wds, pl.BlockSpec((T, D), lambda j, i: (0, 0))),
        out_shape=(_sds((NDEV, D, FF_B), BF16), _sds((NDEV, FF_B, D), BF16), _sds((T, D), F32)),
        scratch_shapes=[pltpu.VMEM((D, FF_B), F32), pltpu.VMEM((FF_B, D), F32)],
        compiler_params=_cp(("arbitrary", "arbitrary")))(m, dh2_b, wup_g, wdown_g)


def _loss_head(h2, g_final, tgt):
    def body(h_ref, g_ref, t_ref, loss_ref, dh_ref, dhb_ref, dg_ref):
        i = pl.program_id(0)
        r_io = lax.broadcasted_iota(jnp.int32, (TM_E, 1), 0) + i * TM_E
        valid = (r_io >= NM) & (r_io < L)
        xv = h_ref[...]
        r = lax.rsqrt(jnp.mean(xv * xv, axis=-1, keepdims=True) + EPS)
        xh = xv * r
        gv = g_ref[...]
        err = jnp.where(valid, xh * gv - t_ref[...], 0.0)
        lpart = jnp.broadcast_to(0.5 * jnp.sum(jnp.sum(err * err, axis=-1, keepdims=True) * (1.0 / D), axis=0, keepdims=True), (1, 128))
        dy = err * (1.0 / D)
        dxh = dy * gv
        dh = r * (dxh - xh * jnp.mean(dxh * xh, axis=-1, keepdims=True))
        dh_ref[...] = dh
        dhb_ref[...] = dh.astype(BF16)
        gpart = jnp.sum(dy * xh, axis=0, keepdims=True)

        @pl.when(i == 0)
        def _():
            loss_ref[...] = lpart
            dg_ref[...] = gpart

        @pl.when(i > 0)
        def _():
            loss_ref[...] += lpart
            dg_ref[...] += gpart

    blk = pl.BlockSpec((TM_E, D), lambda i: (i, 0))
    vec = pl.BlockSpec((1, D), lambda i: (0, 0))
    return pl.pallas_call(
        body, name="loss_head", grid=(T // TM_E,), in_specs=[blk, vec, blk],
        out_specs=(pl.BlockSpec((1, 128), lambda i: (0, 0)), blk, blk, vec),
        out_shape=(_sds((1, 128), F32), _sds((T, D), F32), _sds((T, D), BF16), _sds((1, D), F32)),
        compiler_params=_cp(("arbitrary",)))(h2, g_final, tgt)


def _adamw(parts, w, m, v, name):
    rr, cc = w.shape
    tr = rr
    for cand in (256, 128, 64):
        if rr % cand == 0 and rr > cand:
            tr = cand
            break
    c1 = 1.0 - ADAM_B1 ** ADAM_STEP
    c2 = 1.0 - ADAM_B2 ** ADAM_STEP

    def body(p_ref, w_ref, m_ref, v_ref, g_ref, d_ref, nm_ref, nv_ref):
        g = p_ref[0].astype(F32)
        for s in range(1, NDEV):
            g = g + p_ref[s].astype(F32)
        mn = ADAM_B1 * m_ref[...] + (1.0 - ADAM_B1) * g
        vn = ADAM_B2 * v_ref[...] + (1.0 - ADAM_B2) * (g * g)
        g_ref[...] = g
        nm_ref[...] = mn
        nv_ref[...] = vn
        d_ref[...] = -ADAM_LR * ((mn / c1) / (jnp.sqrt(vn / c2) + ADAM_EPS) + ADAM_WD * w_ref[...])

    blk = pl.BlockSpec((tr, cc), lambda i: (i, 0))
    return pl.pallas_call(
        body, name=name, grid=(rr // tr,),
        in_specs=[pl.BlockSpec((NDEV, tr, cc), lambda i: (0, i, 0)), blk, blk, blk],
        out_specs=(blk,) * 4, out_shape=(_sds((rr, cc), F32),) * 4,
        compiler_params=_cp(("parallel",)))(parts, w, m, v)


RPB_N = NA_HEADS * 15 * 31
RPB_PAD = 4096
OWN_ROWS = NM + 8


def _pad_rows(a, rows):
    return jnp.pad(a, ((0, rows - a.shape[0]),) + ((0, 0),) * (a.ndim - 1))


def _pack_owned(meta_blk, lb_blk):
    return jnp.concatenate([meta_blk, _pad_rows(lb_blk.reshape(2, 128), 8)], axis=0)


def _pack_replicated(n_mix, n_mlp, n_final, hg_gain, rpb):
    flat = _pad_rows(rpb.reshape(RPB_N), RPB_PAD)
    return jnp.concatenate([n_mix.reshape(8, 128), n_mlp.reshape(8, 128), n_final.reshape(8, 128),
                            _pad_rows(hg_gain.reshape(4, 128), 8), flat.reshape(32, 128)], axis=0)


def _unpack_replicated(a):
    return (a[0:8].reshape(1, D), a[8:16].reshape(1, D), a[16:24].reshape(D), a[24:28].reshape(1, 512),
            a[32:64].reshape(RPB_PAD)[:RPB_N].reshape(1, NA_HEADS, 15, 31))


def kernel(x, meta_tokens, w_in, w_na_out, w_hg_out, w_o, w_up, w_down, norm_mix, norm_mlp, norm_final, hg_norm, na_rpb, hg_lb_logits, loss_target, m_meta_tokens, m_w_in, m_w_na_out, m_w_hg_out, m_w_o, m_w_up, m_w_down, m_norm_mix, m_norm_mlp, m_norm_final, m_hg_norm, m_na_rpb, m_hg_lb_logits, v_meta_tokens, v_w_in, v_w_na_out, v_w_hg_out, v_w_o, v_w_up, v_w_down, v_norm_mix, v_norm_mlp, v_norm_final, v_hg_norm, v_na_rpb, v_hg_lb_logits):
    owned = _pack_owned(meta_tokens, hg_lb_logits)
    win_g, owned_g = _gather_two_level([w_in[0].astype(BF16), owned], "gather_first")
    later = [w[0].astype(BF16) for w in (w_na_out, w_hg_out, w_o, w_up, w_down)]
    later[0] = _tie(later[0], owned_g, "tie_gather_rest")
    gather_rest, tok = _exchange_start(later, [False] * 5, "gather_rest_start")
    win_g = _tie(win_g, tok, "tie_inproj")
    meta_full = jnp.transpose(owned_g[:, 0:NM, :], (1, 0, 2)).reshape(NM, D)
    logits = jnp.transpose(owned_g[:, NM:NM + 2, :].reshape(NDEV, 2, 2, 64), (1, 2, 0, 3)).reshape(2, 2, 512)

    h0 = jnp.concatenate([meta_full, x[0], jnp.zeros((T - L, D), F32)], axis=0)
    tgt = jnp.concatenate([jnp.zeros((NM, D), F32), loss_target[0], jnp.zeros((T - L, D), F32)], axis=0)
    bias_tab = _na_bias_table(na_rpb[0])

    a = _norm_fwd(h0, norm_mix, "norm_mix_fwd")
    p_act = _inproj_fwd(a, win_g)
    o_na, lse = _na_fwd(p_act, bias_tab)
    qh, k_f, b_f, k_b, b_b = _hg_pre(p_act, logits)
    o_f, st_f = _hg_scan_fwd(qh, k_f, b_f, p_act, False)
    o_b, st_b = _hg_scan_fwd(qh, k_b, b_b, p_act, True)
    u_hg = _hg_post(o_f, o_b, p_act, hg_norm)
    wna_g, whg_g, wo_g, wup_g, wdown_g = _exchange_wait(gather_rest, [False] * 5, [u_hg, o_na], "gather_rest_wait")
    w_o_full = wo_g.reshape(D, D)
    mix = _mix_fwd(o_na, u_hg, wna_g, whg_g, p_act)
    h1, m_act = _wo_fwd(mix, w_o_full, h0, norm_mlp)
    h2 = _mlp_fwd(m_act, wup_g, wdown_g, h1)
    loss_part, dh2, dh2_b, d_nfinal = _loss_head(h2, norm_final.reshape(1, D), tgt)

    dwup_p, dwdown_p, dm = _mlp_bwd(m_act, dh2_b, wup_g, wdown_g)
    sc_mlp, tok = _exchange_start([dwup_p, dwdown_p], [True] * 2, "scatter_mlp_start")
    dh1, dh1_b, d_nmlp = _norm_bwd(h1, norm_mlp, _tie(dm, tok, "tie_norm_mlp_bwd"), dh2, "norm_mlp_bwd")
    dmix, dwo = _wo_bwd(dh1_b, w_o_full, mix)
    sc_wo, tok = _exchange_start([dwo.reshape(NDEV, D // NDEV, D)], [True], "scatter_wo_start")
    dgna, dghg, dwna_p, dwhg_p, do_na, du_hg = _mix_bwd(o_na, u_hg, wna_g, whg_g, p_act, _tie(dmix, tok, "tie_mix_bwd"))
    sc_br, tok = _exchange_start([dwna_p, dwhg_p], [True] * 2, "scatter_branch_start")
    du_hg = _tie(du_hg, tok, "tie_hg_post_bwd")
    do_hg, dg_hg, d_gain = _hg_post_bwd(du_hg, o_f, o_b, p_act, hg_norm)
    dq_f, dk_f, db_f, dv_f = _hg_scan_bwd(qh, k_f, b_f, p_act, st_f, do_hg, False)
    dq_b, dk_b, db_b, dv_b = _hg_scan_bwd(qh, k_b, b_b, p_act, st_b, do_hg, True)
    dq_hg, dz_f, dz_b, di_hg, d_logits = _hg_pre_bwd(p_act, logits, dq_f, dq_b, dk_f, dk_b, db_f, db_b, dv_f, dv_b)
    dq_na, dk_na, dv_na, dbias = _na_bwd(p_act, do_na, lse, bias_tab)
    dp = jnp.concatenate([dq_na.astype(BF16), dk_na.astype(BF16), dv_na.astype(BF16), dq_hg, dz_f, dz_b, di_hg, dg_hg,
                          dgna, dghg], axis=1)
    dwin_p, da = _inproj_bwd(a, dp, win_g)
    sc_in, tok = _exchange_start([dwin_p], [True], "scatter_in_start")
    dh0, _, d_nmix = _norm_bwd(h0, norm_mix, _tie(da, tok, "tie_norm_mix_bwd"), dh1, "norm_mix_bwd")
    d_rpb = _na_rpb_reduce(_tie(dbias, tok, "tie_rpb_reduce"))[:, :, :31]

    res = {}

    def update(nm, parts, w, mm, vv):
        res[nm] = [r[None] for r in _adamw(parts, w[0], mm[0], vv[0], "adamw_" + nm)]
        return res[nm][1]

    wup_r, wdown_r = _exchange_wait(sc_mlp, [True] * 2, [dh0, d_rpb], "scatter_mlp_wait")
    update("w_up", wup_r, w_up, m_w_up, v_w_up)
    last = update("w_down", wdown_r, w_down, m_w_down, v_w_down)
    (wo_r,) = _exchange_wait(sc_wo, [True], [last], "scatter_wo_wait")
    last = update("w_o", wo_r, w_o, m_w_o, v_w_o)
    wna_r, whg_r = _exchange_wait(sc_br, [True] * 2, [last], "scatter_branch_wait")
    update("w_na_out", wna_r, w_na_out, m_w_na_out, v_w_na_out)
    last = update("w_hg_out", whg_r, w_hg_out, m_w_hg_out, v_w_hg_out)

    d_meta = jnp.transpose(dh0[0:NM].reshape(NM, NDEV, 128), (1, 0, 2))
    d_lg = jnp.transpose(d_logits.reshape(2, 2, NDEV, 64), (2, 0, 1, 3)).reshape(NDEV, 2, 128)
    owned_p = jnp.concatenate([d_meta, jnp.pad(d_lg, ((0, 0), (0, OWN_ROWS - NM - 2), (0, 0)))], axis=1)
    repl_p = _pack_replicated(d_nmix, d_nmlp, d_nfinal, d_gain, d_rpb)
    owned_r, repl_r = _exchange([_tie(owned_p, last, "tie_scatter_small"), repl_p], [True, False], "scatter_small")
    own = _adamw(owned_r, owned, _pack_owned(m_meta_tokens, m_hg_lb_logits), _pack_owned(v_meta_tokens, v_hg_lb_logits),
                 "adamw_owned_small")
    res["meta_tokens"] = [r[0:NM] for r in own]
    res["hg_lb_logits"] = [r[NM:NM + 2].reshape(2, 2, 64) for r in own]
    rep = _adamw(repl_r, _pack_replicated(norm_mix, norm_mlp, norm_final, hg_norm, na_rpb),
                 _pack_replicated(m_norm_mix, m_norm_mlp, m_norm_final, m_hg_norm, m_na_rpb),
                 _pack_replicated(v_norm_mix, v_norm_mlp, v_norm_final, v_hg_norm, v_na_rpb), "adamw_replicated")
    for q in range(4):
        um = _unpack_replicated(rep[q])
        for nm, val in zip(("norm_mix", "norm_mlp", "norm_final", "hg_norm", "na_rpb"), um):
            res.setdefault(nm, [None] * 4)[q] = val
    (win_r,) = _exchange_wait(sc_in, [True], [rep[1], own[1]], "scatter_in_wait")
    update("w_in", win_r, w_in, m_w_in, v_w_in)

    loss = lax.psum(loss_part[0, 0], ("x", "y", "c"))
    grad_x = dh0[NM:L][None]
    order = ("meta_tokens", "w_in", "w_na_out", "w_hg_out", "w_o", "w_up", "w_down", "norm_mix", "norm_mlp", "norm_final",
             "hg_norm", "na_rpb", "hg_lb_logits")
    outs = [loss, grad_x]
    for q in range(4):
        outs += [res[nm][q] for nm in order]
    return tuple(outs)
```

```python
import functools

import numpy as np
import jax
import jax.numpy as jnp
from jax import lax
from jax.experimental import pallas as pl
from jax.experimental.pallas import tpu as pltpu

F32 = jnp.float32
BF16 = jnp.bfloat16

D = 1024
SEQ = 2048
NM = 16
L = SEQ + NM
T = 2176
NDEV = 8
EPS = 1e-6
GRID_W = 64
ROWS = SEQ // GRID_W
NA_HEADS = 8
NA_DH = 64
NA_SCALE = NA_DH ** -0.5
HG_HEADS = 4
HG_C = 16
NCHUNK = L // HG_C
D_FF = 4096
IN_COLS = 6144
NEG = -1e30

ADAM_LR = 0.001
ADAM_B1 = 0.9
ADAM_B2 = 0.999
ADAM_EPS = 1e-08
ADAM_WD = 0.01
ADAM_STEP = 10

MESH_ID = pl.DeviceIdType.MESH
ANY = pl.BlockSpec(memory_space=pl.ANY)

NN = (((1,), (0,)), ((), ()))
NT = (((1,), (1,)), ((), ()))
TN = (((0,), (0,)), ((), ()))


def _cp(sem=None, vmem_mb=48):
    return pltpu.CompilerParams(dimension_semantics=sem, vmem_limit_bytes=vmem_mb * 1024 * 1024)


def _dot(a, b, dims=NN):
    return lax.dot_general(a, b, dims, preferred_element_type=F32)


def _sds(shape, dtype):
    return jax.ShapeDtypeStruct(shape, dtype)


HBM = pl.BlockSpec(memory_space=pltpu.HBM)
SEM = pl.BlockSpec(memory_space=pltpu.SEMAPHORE)
EFFECT = pltpu.SideEffectType.DATAFLOW_SIDE_EFFECTING


def _exchange(arrs, scatter, name):
    n = len(arrs)
    out_shapes = []
    for a, sc in zip(arrs, scatter):
        out_shapes.append(_sds(a.shape if sc else (NDEV,) + a.shape, a.dtype))

    def body(*refs):
        ins, outs = refs[:n], refs[n:2 * n]
        send_sems, recv_sems, loc_sems = refs[2 * n:]
        me = 4 * lax.axis_index("x") + 2 * lax.axis_index("y") + lax.axis_index("c")
        copies = []
        for k in range(n):
            src_me = ins[k].at[me] if scatter[k] else ins[k]
            loc = pltpu.make_async_copy(src_me, outs[k].at[me], loc_sems.at[k])
            loc.start()
            copies.append(loc)
        remote = _peer_copies(ins, outs, scatter, send_sems, recv_sems)
        for cp in remote:
            cp.start()
        for cp in remote:
            cp.wait_recv()
        for cp in remote:
            cp.wait_send()
        for cp in copies:
            cp.wait()

    return pl.pallas_call(
        body, name=name, out_shape=tuple(out_shapes), in_specs=[ANY] * n, out_specs=tuple([ANY] * n),
        scratch_shapes=[pltpu.SemaphoreType.DMA((n * (NDEV - 1),)), pltpu.SemaphoreType.DMA((n * (NDEV - 1),)),
                        pltpu.SemaphoreType.DMA((n,))],
    )(*arrs)


def _gather_two_level(arrs, name):
    n = len(arrs)

    def body(*refs):
        ins, outs = refs[:n], refs[n:2 * n]
        send_sems, recv_sems, loc_sems = refs[2 * n:]
        x, y, c = lax.axis_index("x"), lax.axis_index("y"), lax.axis_index("c")
        sib = (x, y, 1 - c)
        chips = [(1 - x, y), (x, 1 - y), (1 - x, 1 - y)]

        def slot(k, px, py, pc):
            return outs[k].at[4 * px + 2 * py + pc]

        def copy(k, q, block, to, src=None):
            return pltpu.make_async_remote_copy(
                src_ref=slot(k, *block) if src is None else src, dst_ref=slot(k, *block),
                send_sem=send_sems.at[7 * k + q], recv_sem=recv_sems.at[7 * k + q], device_id=to, device_id_type=MESH_ID)

        mine = [pltpu.make_async_copy(ins[k], slot(k, x, y, c), loc_sems.at[k]) for k in range(n)]
        for cp in mine:
            cp.start()
        first = []
        for k in range(n):
            first.append(copy(k, 0, (x, y, c), sib, src=ins[k]))
            first += [copy(k, 1 + j, (x, y, c), (*chip, c), src=ins[k]) for j, chip in enumerate(chips)]
        for cp in first:
            cp.start()
        passed = []
        for j, chip in enumerate(chips):
            for k in range(n):
                copy(k, 1 + j, (*chip, c), (x, y, c)).wait_recv()
                fw = copy(k, 4 + j, (*chip, c), sib)
                fw.start()
                passed.append(fw)
        for k in range(n):
            copy(k, 0, (x, y, 1 - c), (x, y, c)).wait_recv()
            for j, chip in enumerate(chips):
                copy(k, 4 + j, (*chip, 1 - c), (x, y, c)).wait_recv()
        for cp in first + passed:
            cp.wait_send()
        for cp in mine:
            cp.wait()

    return pl.pallas_call(
        body, name=name, out_shape=tuple(_sds((NDEV,) + a.shape, a.dtype) for a in arrs),
        in_specs=[ANY] * n, out_specs=tuple([ANY] * n),
        scratch_shapes=[pltpu.SemaphoreType.DMA((7 * n,)), pltpu.SemaphoreType.DMA((7 * n,)), pltpu.SemaphoreType.DMA((n,))],
    )(*arrs)


def _peer_copies(srcs, lands, scatter, send_sems, recv_sems):
    x, y, c = lax.axis_index("x"), lax.axis_index("y"), lax.axis_index("c")
    me = 4 * x + 2 * y + c
    out = []
    for k in range(len(srcs)):
        for m in range(1, NDEV):
            px, py, pc = x ^ (m >> 2), y ^ ((m >> 1) & 1), c ^ (m & 1)
            src = srcs[k].at[4 * px + 2 * py + pc] if scatter[k] else srcs[k]
            out.append(pltpu.make_async_remote_copy(
                src_ref=src, dst_ref=lands[k].at[me], send_sem=send_sems.at[k * (NDEV - 1) + m - 1],
                recv_sem=recv_sems.at[k * (NDEV - 1) + m - 1],
                device_id=(px, py, pc), device_id_type=MESH_ID))
    return out


def _exchange_start(arrs, scatter, name):
    n = len(arrs)
    me = 4 * lax.axis_index("x") + 2 * lax.axis_index("y") + lax.axis_index("c")
    lands = []
    for a, sc in zip(arrs, scatter):
        own = lax.dynamic_index_in_dim(a, me, 0, keepdims=True) if sc else a[None]
        shape = a.shape if sc else (NDEV,) + a.shape
        lands.append(lax.dynamic_update_index_in_dim(lax.empty(shape, a.dtype), own, me, 0))

    def body(*refs):
        srcs, lnds = refs[:n], refs[n:2 * n]
        send_sems, recv_sems = refs[2 * n], refs[2 * n + 1]
        token = refs[-1]
        for cp in _peer_copies(srcs, lnds, scatter, send_sems, recv_sems):
            cp.start()
        token[...] = jnp.zeros_like(token)

    ops = [pltpu.with_memory_space_constraint(a, pltpu.HBM) for a in list(arrs) + lands]
    res = pl.pallas_call(
        body, name=name,
        out_shape=(pltpu.SemaphoreType.DMA((n * (NDEV - 1),)), pltpu.SemaphoreType.DMA((n * (NDEV - 1),)))
        + tuple(pltpu.HBM(o.shape, o.dtype) for o in ops) + (_sds((8, 128), F32),),
        in_specs=[HBM] * (2 * n), out_specs=(SEM, SEM) + (HBM,) * (2 * n) + (pl.BlockSpec(memory_space=pltpu.VMEM),),
        input_output_aliases={k: 2 + k for k in range(2 * n)},
        compiler_params=pltpu.CompilerParams(has_side_effects=EFFECT),
    )(*ops)
    return res[:-1], res[-1]


def _exchange_wait(handle, scatter, after, name):
    send_sems, recv_sems = handle[0], handle[1]
    bufs = handle[2:]
    n = len(bufs) // 2
    after = list(after)

    def body(*refs):
        srcs, lnds = refs[:n], refs[n:2 * n]
        for cp in _peer_copies(srcs, lnds, scatter, refs[2 * n], refs[2 * n + 1]):
            cp.wait_send()
            cp.wait_recv()

    res = pl.pallas_call(
        body, name=name, out_shape=tuple(pltpu.HBM(b.shape, b.dtype) for b in bufs),
        in_specs=[HBM] * (2 * n) + [SEM, SEM] + [ANY] * len(after), out_specs=(HBM,) * (2 * n),
        input_output_aliases={k: k for k in range(2 * n)},
        compiler_params=pltpu.CompilerParams(has_side_effects=EFFECT),
    )(*bufs, send_sems, recv_sems, *after)
    return res[n:]


def _tie(x, token, name):
    def body(x_ref, t_ref, o_ref):
        del x_ref, t_ref, o_ref

    return pl.pallas_call(body, name=name, out_shape=_sds(x.shape, x.dtype), in_specs=[ANY, ANY], out_specs=ANY,
                          input_output_aliases={0: 0})(x, token)


TM_E = 272


def _norm_fwd(h, g, name):
    def body(h_ref, g_ref, o_ref):
        xv = h_ref[...]
        r = lax.rsqrt(jnp.mean(xv * xv, axis=-1, keepdims=True) + EPS)
        o_ref[...] = (xv * r * g_ref[...]).astype(BF16)

    return pl.pallas_call(
        body, name=name, grid=(T // TM_E,),
        in_specs=[pl.BlockSpec((TM_E, D), lambda i: (i, 0)), pl.BlockSpec((1, D), lambda i: (0, 0))],
        out_specs=pl.BlockSpec((TM_E, D), lambda i: (i, 0)), out_shape=_sds((T, D), BF16),
        compiler_params=_cp(("parallel",)))(h, g)


def _norm_bwd(h, g, dn, dres, name):
    def body(h_ref, g_ref, dn_ref, dres_ref, dh_ref, dhb_ref, dg_ref):
        i = pl.program_id(0)
        xv = h_ref[...]
        r = lax.rsqrt(jnp.mean(xv * xv, axis=-1, keepdims=True) + EPS)
        xh = xv * r
        dnv = dn_ref[...].astype(F32)
        dxh = dnv * g_ref[...]
        dh = dres_ref[...] + r * (dxh - xh * jnp.mean(dxh * xh, axis=-1, keepdims=True))
        dh_ref[...] = dh
        dhb_ref[...] = dh.astype(BF16)
        part = jnp.sum(dnv * xh, axis=0, keepdims=True)

        @pl.when(i == 0)
        def _():
            dg_ref[...] = part

        @pl.when(i > 0)
        def _():
            dg_ref[...] += part

    blk = pl.BlockSpec((TM_E, D), lambda i: (i, 0))
    vec = pl.BlockSpec((1, D), lambda i: (0, 0))
    return pl.pallas_call(
        body, name=name, grid=(T // TM_E,), in_specs=[blk, vec, blk, blk], out_specs=(blk, blk, vec),
        out_shape=(_sds((T, D), F32), _sds((T, D), BF16), _sds((1, D), F32)),
        compiler_params=_cp(("arbitrary",)))(h, g, dn, dres)


TM_MM = 1088


def _inproj_fwd(a, w_g):
    nb = w_g.shape[2]

    def body(a_ref, w_ref, o_ref):
        o_ref[...] = _dot(a_ref[...], w_ref[0])

    return pl.pallas_call(
        body, name="inproj_fwd", grid=(T // TM_MM, NDEV),
        in_specs=[pl.BlockSpec((TM_MM, D), lambda i, j: (i, 0)), pl.BlockSpec((1, D, nb), lambda i, j: (j, 0, 0))],
        out_specs=pl.BlockSpec((TM_MM, nb), lambda i, j: (i, j)), out_shape=_sds((T, NDEV * nb), F32),
        compiler_params=_cp(("parallel", "parallel")))(a, w_g)


TM_B = 544


def _inproj_bwd(a, dp, w_g):
    nb = w_g.shape[2]
    ni = T // TM_B

    def body(a_ref, dp_ref, w_ref, dw_ref, da_ref, acc):
        j, i = pl.program_id(0), pl.program_id(1)
        av, dpv = a_ref[...], dp_ref[...]
        part = _dot(av, dpv, TN)

        @pl.when(i == 0)
        def _():
            acc[...] = part

        @pl.when(i > 0)
        def _():
            acc[...] += part

        @pl.when(i == ni - 1)
        def _():
            dw_ref[0] = acc[...].astype(BF16)

        rows = pl.ds(pl.multiple_of(i * TM_B, TM_B), TM_B)
        dav = _dot(dpv, w_ref[0], NT)

        @pl.when(j == 0)
        def _():
            da_ref[rows, :] = dav

        @pl.when(j > 0)
        def _():
            da_ref[rows, :] += dav

    return pl.pallas_call(
        body, name="inproj_bwd", grid=(NDEV, ni),
        in_specs=[pl.BlockSpec((TM_B, D), lambda j, i: (i, 0)), pl.BlockSpec((TM_B, nb), lambda j, i: (i, j)),
                  pl.BlockSpec((1, D, nb), lambda j, i: (j, 0, 0))],
        out_specs=(pl.BlockSpec((1, D, nb), lambda j, i: (j, 0, 0)), pl.BlockSpec((T, D), lambda j, i: (0, 0))),
        out_shape=(_sds((NDEV, D, nb), BF16), _sds((T, D), F32)),
        scratch_shapes=[pltpu.VMEM((D, nb), F32)],
        compiler_params=_cp(("arbitrary", "arbitrary")))(a, dp, w_g)


NA_QB = 256
NA_GROUPS = ROWS // 4
NA_UROWS = 11
NA_KW = NA_UROWS * GRID_W
NA_KU = 768


def _na_row_offset(var, i, j):
    valid = (j < 8, i <= j < i + 8, 3 <= j < NA_UROWS)[var]
    return (j - i + (7, 3, 0)[var]) if valid else None


def _na_bias_table(rpb):
    def body(r_ref, o_ref):
        row = lax.broadcasted_iota(jnp.int32, (GRID_W, 128), 0)
        lane = lax.broadcasted_iota(jnp.int32, (GRID_W, 128), 1)
        w = lane & (GRID_W - 1)
        cs = jnp.clip(row - 8, 0, GRID_W - 16)
        in_win = (w >= cs) & (w < cs + 16)
        neg = jnp.full((GRID_W, 128), NEG, F32)
        tabs = []
        for a in range(15):
            z = jnp.broadcast_to(r_ref[0, a:a + 1, :], (GRID_W, 128))
            for bit in range(6):
                sh = 1 << bit
                z = jnp.where((row & sh) != 0, jnp.roll(z, sh, axis=1), z)
            z = jnp.roll(z, 128 - 15, axis=1)
            z = jnp.where(lane < GRID_W, z, 0.0)
            z = z + jnp.roll(z, GRID_W, axis=1)
            tabs.append(jnp.where(in_win, z, NEG))
        tail = jnp.where(lane < GRID_W + NM, 0.0, NEG)
        for var in range(3):
            for i in range(4):
                for jp in range(NA_KU // 128):
                    halves = []
                    for j in (2 * jp, 2 * jp + 1):
                        a = _na_row_offset(var, i, j) if j < NA_UROWS else None
                        halves.append(tail if j >= NA_UROWS else (neg if a is None else tabs[a]))
                    o_ref[var, 0, i * 64:(i + 1) * 64, jp * 128:(jp + 1) * 128] = jnp.where(lane < GRID_W, halves[0], halves[1])

    rp = jnp.concatenate([rpb, jnp.zeros((NA_HEADS, 15, 128 - 31), F32)], axis=2)
    return pl.pallas_call(
        body, name="na_bias_table", grid=(NA_HEADS,),
        in_specs=[pl.BlockSpec((1, 15, 128), lambda h: (h, 0, 0))],
        out_specs=pl.BlockSpec((3, 1, NA_QB, NA_KU), lambda h: (0, h, 0, 0)),
        out_shape=_sds((3, NA_HEADS, NA_QB, NA_KU), F32), compiler_params=_cp(("parallel",)))(rp)


def _na_var(g):
    return jnp.where(g == 0, 0, jnp.where(g == NA_GROUPS - 1, 2, 1))


def _na_load_window(src_ref, dst, g):
    us = jnp.clip(4 * g - 4, 0, ROWS - NA_UROWS)
    kstart = pl.multiple_of(NM + GRID_W * us, 16)
    dst[0:NA_KW, :] = src_ref[pl.ds(kstart, NA_KW), :].astype(BF16)
    dst[NA_KW:NA_KW + NM, :] = src_ref[0:NM, :].astype(BF16)
    dst[NA_KW + NM:, :] = jnp.zeros((NA_KU - NA_KW - NM, 128), BF16)
    return kstart


def _na_fwd(p_act, bias_tab):
    def body(q_ref, k_ref, v_ref, b_ref, o_ref, lse_ref, ku, vu):
        g = pl.program_id(1)
        _na_load_window(k_ref, ku, g)
        _na_load_window(v_ref, vu, g)
        qstart = pl.multiple_of(NM + NA_QB * g, 16)
        q = q_ref[pl.ds(qstart, NA_QB), :]
        lane = lax.broadcasted_iota(jnp.int32, (NA_QB, 128), 1)
        o_h, lse_h = [], []
        for h in range(2):
            hm = (lane < 64) if h == 0 else (lane >= 64)
            qm = jnp.where(hm, q, 0.0).astype(BF16)
            s = _dot(qm, ku[...], NT) * NA_SCALE + b_ref[0, h]
            m = jnp.max(s, axis=-1, keepdims=True)
            p = jnp.exp(s - m)
            l = jnp.sum(p, axis=-1, keepdims=True)
            o_h.append(_dot(p.astype(BF16), vu[...]) / l)
            lse_h.append(jnp.broadcast_to(m + jnp.log(l), (NA_QB, 128)))
        o_ref[pl.ds(qstart, NA_QB), :] = jnp.where(lane < 64, o_h[0], o_h[1]).astype(BF16)
        lse_ref[0, pl.ds(qstart, NA_QB), :] = jnp.where(lane < 64, lse_h[0], lse_h[1])

        @pl.when(g == 0)
        def _():
            qm_ = q_ref[0:NM, :]
            lane_m = lax.broadcasted_iota(jnp.int32, (NM, 128), 1)
            km, vm = ku[NA_KW:NA_KW + NM, :], vu[NA_KW:NA_KW + NM, :]
            om = []
            for h in range(2):
                hm = (lane_m < 64) if h == 0 else (lane_m >= 64)
                s = _dot(jnp.where(hm, qm_, 0.0).astype(BF16), km, NT) * NA_SCALE
                p = jnp.exp(s - jnp.max(s, axis=-1, keepdims=True))
                l = jnp.sum(p, axis=-1, keepdims=True)
                om.append(_dot(p.astype(BF16), vm) / l)
            o_ref[0:NM, :] = jnp.where(lane_m < 64, om[0], om[1]).astype(BF16)
            o_ref[L:T, :] = jnp.zeros((T - L, 128), BF16)
            lse_ref[0, 0:NM, :] = jnp.zeros((NM, 128), F32)
            lse_ref[0, L:T, :] = jnp.zeros((T - L, 128), F32)

    col = lambda off: pl.BlockSpec((T, 128), lambda hp, g: (0, off + hp))
    return pl.pallas_call(
        body, name="na_fwd", grid=(4, NA_GROUPS),
        in_specs=[col(0), col(4), col(8),
                  pl.BlockSpec((1, 2, NA_QB, NA_KU), lambda hp, g: (_na_var(g), hp, 0, 0))],
        out_specs=(pl.BlockSpec((T, 128), lambda hp, g: (0, hp)), pl.BlockSpec((1, T, 128), lambda hp, g: (hp, 0, 0))),
        out_shape=(_sds((T, 512), BF16), _sds((4, T, 128), F32)),
        scratch_shapes=[pltpu.VMEM((NA_KU, 128), BF16), pltpu.VMEM((NA_KU, 128), BF16)],
        compiler_params=_cp(("parallel", "arbitrary")))(p_act, p_act, p_act, bias_tab)


def _na_bwd(p_act, do, lse, bias_tab):
    def body(q_ref, k_ref, v_ref, do_ref, lse_ref, b_ref, dq_ref, dk_ref, dv_ref, db_ref, ku, vu):
        g = pl.program_id(1)

        @pl.when(g == 0)
        def _():
            dq_ref[...] = jnp.zeros((T, 128), F32)
            dk_ref[...] = jnp.zeros((T, 128), F32)
            dv_ref[...] = jnp.zeros((T, 128), F32)

        kstart = _na_load_window(k_ref, ku, g)
        _na_load_window(v_ref, vu, g)
        qstart = pl.multiple_of(NM + NA_QB * g, 16)
        q = q_ref[pl.ds(qstart, NA_QB), :]
        dov = do_ref[pl.ds(qstart, NA_QB), :]
        lsev = lse_ref[0, pl.ds(qstart, NA_QB), :]
        lane = lax.broadcasted_iota(jnp.int32, (NA_QB, 128), 1)
        first = (g == 0) | (g == 1) | (g == NA_GROUPS - 1)
        dq_h = []
        dku = jnp.zeros((NA_KU, 128), F32)
        dvu = jnp.zeros((NA_KU, 128), F32)
        for h in range(2):
            hm = (lane < 64) if h == 0 else (lane >= 64)
            qm = jnp.where(hm, q, 0.0).astype(BF16)
            dom = jnp.where(hm, dov, 0.0).astype(BF16)
            s = _dot(qm, ku[...], NT) * NA_SCALE + b_ref[0, h]
            p = jnp.exp(s - lsev[:, 64 * h:64 * h + 1])
            dp = _dot(dom, vu[...], NT)
            delta = jnp.sum(p * dp, axis=-1, keepdims=True)
            ds = p * (dp - delta)

            @pl.when(first)
            def _():
                db_ref[0, h] = ds

            @pl.when(jnp.logical_not(first))
            def _():
                db_ref[0, h] += ds

            dsb = (ds * NA_SCALE).astype(BF16)
            dq_h.append(_dot(dsb, ku[...]))
            dku = dku + _dot(dsb, qm, TN)
            dvu = dvu + _dot(p.astype(BF16), dom, TN)
        dq_ref[pl.ds(qstart, NA_QB), :] = jnp.where(lane < 64, dq_h[0], dq_h[1])
        dk_ref[pl.ds(kstart, NA_KW), :] += dku[0:NA_KW]
        dv_ref[pl.ds(kstart, NA_KW), :] += dvu[0:NA_KW]
        dk_ref[0:NM, :] += dku[NA_KW:NA_KW + NM]
        dv_ref[0:NM, :] += dvu[NA_KW:NA_KW + NM]

        @pl.when(g == 0)
        def _():
            qm_ = q_ref[0:NM, :]
            dom_ = do_ref[0:NM, :]
            lane_m = lax.broadcasted_iota(jnp.int32, (NM, 128), 1)
            km, vm = ku[NA_KW:NA_KW + NM, :], vu[NA_KW:NA_KW + NM, :]
            dqs = []
            dkm = jnp.zeros((NM, 128), F32)
            dvm = jnp.zeros((NM, 128), F32)
            for h in range(2):
                hm = (lane_m < 64) if h == 0 else (lane_m >= 64)
                qh = jnp.where(hm, qm_, 0.0).astype(BF16)
                doh = jnp.where(hm, dom_, 0.0).astype(BF16)
                s = _dot(qh, km, NT) * NA_SCALE
                e = jnp.exp(s - jnp.max(s, axis=-1, keepdims=True))
                p = e / jnp.sum(e, axis=-1, keepdims=True)
                dp = _dot(doh, vm, NT)
                ds = p * (dp - jnp.sum(p * dp, axis=-1, keepdims=True))
                dsb = (ds * NA_SCALE).astype(BF16)
                dqs.append(_dot(dsb, km))
                dkm = dkm + _dot(dsb, qh, TN)
                dvm = dvm + _dot(p.astype(BF16), doh, TN)
            dq_ref[0:NM, :] = jnp.where(lane_m < 64, dqs[0], dqs[1])
            dk_ref[0:NM, :] += dkm
            dv_ref[0:NM, :] += dvm

    col = lambda off: pl.BlockSpec((T, 128), lambda hp, g: (0, off + hp))
    ocol = pl.BlockSpec((T, 128), lambda hp, g: (0, hp))
    bspec = pl.BlockSpec((1, 2, NA_QB, NA_KU), lambda hp, g: (_na_var(g), hp, 0, 0))
    return pl.pallas_call(
        body, name="na_bwd", grid=(4, NA_GROUPS),
        in_specs=[col(0), col(4), col(8), ocol, pl.BlockSpec((1, T, 128), lambda hp, g: (hp, 0, 0)), bspec],
        out_specs=(ocol, ocol, ocol, bspec),
        out_shape=(_sds((T, 512), F32), _sds((T, 512), F32), _sds((T, 512), F32), _sds((3, NA_HEADS, NA_QB, NA_KU), F32)),
        scratch_shapes=[pltpu.VMEM((NA_KU, 128), BF16), pltpu.VMEM((NA_KU, 128), BF16)],
        compiler_params=_cp(("parallel", "arbitrary")))(p_act, p_act, p_act, do, lse, bias_tab)


def _na_rpb_reduce(dbias):
    def body(db_ref, o_ref):
        row = lax.broadcasted_iota(jnp.int32, (GRID_W, 128), 0)
        for a in range(15):
            acc = jnp.zeros((GRID_W, GRID_W), F32)
            for var in range(3):
                for i in range(4):
                    for j in range(NA_UROWS):
                        if _na_row_offset(var, i, j) == a:
                            pair = db_ref[var, 0, i * 64:(i + 1) * 64, (j // 2) * 128:(j // 2 + 1) * 128]
                            acc = acc + pair[:, (j % 2) * 64:(j % 2 + 1) * 64]
            z = jnp.concatenate([acc, jnp.zeros((GRID_W, 128 - GRID_W), F32)], axis=1)
            for bit in range(6):
                sh = 1 << bit
                z = jnp.where((row & sh) != 0, jnp.roll(z, 128 - sh, axis=1), z)
            z = jnp.roll(z, 15, axis=1)
            o_ref[0, a:a + 1, :] = jnp.sum(z, axis=0, keepdims=True)

    return pl.pallas_call(
        body, name="na_rpb_reduce", grid=(NA_HEADS,),
        in_specs=[pl.BlockSpec((3, 1, NA_QB, NA_KU), lambda h: (0, h, 0, 0))],
        out_specs=pl.BlockSpec((1, 15, 128), lambda h: (h, 0, 0)), out_shape=_sds((NA_HEADS, 15, 128), F32),
        compiler_params=_cp(("parallel",)))(dbias)


HG_RB = 128
HG_NB = T // HG_RB
HG_SLOTS = HG_NB * 8
HI = lax.Precision.HIGHEST
HG_UNROLL = 4


def _chunk_tri(lower):
    r = lax.broadcasted_iota(jnp.int32, (HG_RB, HG_RB), 0)
    c = lax.broadcasted_iota(jnp.int32, (HG_RB, HG_RB), 1)
    same = (r // HG_C) == (c // HG_C)
    keep = (c <= r) if lower else (c >= r)
    return jnp.where(same & keep, 1.0, 0.0).astype(F32)


def _hg_gate_terms(z, lg):
    dl = lg[0:1, :] - lg[1:2, :]
    log_lb = jax.nn.log_sigmoid(dl)
    log_1mlb = jax.nn.log_sigmoid(-dl)
    yz = log_1mlb + jax.nn.log_sigmoid(z)
    log_f = jnp.logaddexp(log_lb, yz)
    snz = jax.nn.sigmoid(-z)
    k = jnp.exp(log_1mlb) * snz
    w2 = jnp.exp(yz - log_f)
    return log_f, k, snz, w2


def _hg_pre(p_act, logits):
    def body(q_ref, zf_ref, zb_ref, lg_ref, qh_ref, kf_ref, bf_ref, kb_ref, bb_ref):
        qh_ref[...] = jax.nn.silu(q_ref[...])
        lf, kf, _, _ = _hg_gate_terms(zf_ref[...], lg_ref[0])
        kf_ref[...] = kf
        bf_ref[...] = jnp.dot(_chunk_tri(True), lf, precision=HI, preferred_element_type=F32)
        lb_, kb, _, _ = _hg_gate_terms(zb_ref[...], lg_ref[1])
        kb_ref[...] = kb
        bb_ref[...] = jnp.dot(_chunk_tri(False), lb_, precision=HI, preferred_element_type=F32)

    blk = lambda c: pl.BlockSpec((HG_RB, 512), lambda i: (i, c))
    ob = pl.BlockSpec((HG_RB, 512), lambda i: (i, 0))
    return pl.pallas_call(
        body, name="hg_pre", grid=(HG_NB,),
        in_specs=[blk(3), blk(4), blk(5), pl.BlockSpec((2, 2, 512), lambda i: (0, 0, 0))],
        out_specs=(ob,) * 5, out_shape=(_sds((T, 512), F32),) * 5,
        compiler_params=_cp(("parallel",)))(p_act, p_act, p_act, logits)


def _bdot(a, b, ca, cb):
    return lax.dot_general(a.astype(BF16), b.astype(BF16), (((ca,), (cb,)), ((0,), (0,))), preferred_element_type=F32)


HG_S = 8
HG_NS = HG_RB // HG_S


def _lane_sums(xs):
    l_io = lax.broadcasted_iota(jnp.int32, (HG_NS, HG_S, HG_S), 2)
    a = jnp.zeros((HG_NS, HG_S, HG_S), F32)
    for j, x in enumerate(xs):
        a = a + jnp.where(l_io == j, jnp.sum(x, axis=-1, keepdims=True), 0.0)
    return a


def _halves(x):
    y = x.reshape(8, 2, HG_S, x.shape[-1])
    return y[:, 0], y[:, 1]


def _join(first, second):
    return jnp.stack([first, second], axis=1).reshape(HG_RB, first.shape[-1])


def _cross_split(rev, b4):
    b_1, b_2 = _halves(b4)
    if rev:
        r = b_2[:, 0:1, :]
        return jnp.exp(b_1 - r), jnp.exp(r - b_2)
    r = b_1[:, HG_S - 1:HG_S, :]
    return jnp.exp(b_2 - r), jnp.exp(r - b_1)


def _hg_scan_fwd(qh, k, b, p_act, rev):
    anchor = 0 if rev else HG_C - 1

    def body(q_ref, k_ref, b_ref, v_ref, o_ref, st_ref, dsc):
        def phase_a(blk, _):
            rows = pl.ds(pl.multiple_of(blk * HG_RB, HG_RB), HG_RB)
            b3 = b_ref[rows, :].reshape(8, HG_C, 128)
            k3 = k_ref[rows, :].reshape(8, HG_C, 128)
            v3 = v_ref[rows, :].reshape(8, HG_C, 128)
            bl = b3[:, anchor:anchor + 1, :]
            kt = k3 * jnp.exp(bl - b3)
            st_ref[0, pl.ds(pl.multiple_of(blk * 8, 8), 8)] = _bdot(v3, kt, 1, 1)
            dsc[pl.ds(pl.multiple_of(blk * 8, 8), 8), :] = jnp.exp(bl[:, 0, :])
            return 0

        lax.fori_loop(0, HG_NB, phase_a, 0, unroll=HG_UNROLL)

        def phase_b(n, carry):
            c = (NCHUNK - 1 - n) if rev else n
            u = st_ref[0, c]
            st_ref[0, c] = carry
            return carry * dsc[pl.ds(c, 1), :] + u

        lax.fori_loop(0, NCHUNK, phase_b, jnp.zeros((128, 128), F32))
        for c in range(NCHUNK, HG_SLOTS):
            st_ref[0, c] = jnp.zeros((128, 128), F32)

        t_io = lax.broadcasted_iota(jnp.int32, (HG_NS, HG_S, 128), 1)

        def phase_c(blk, _):
            rows = pl.ds(pl.multiple_of(blk * HG_RB, HG_RB), HG_RB)
            b4 = b_ref[rows, :].reshape(HG_NS, HG_S, 128)
            k4 = k_ref[rows, :].reshape(HG_NS, HG_S, 128)
            q4 = q_ref[rows, :].reshape(HG_NS, HG_S, 128)
            v4 = v_ref[rows, :].reshape(HG_NS, HG_S, 128)
            st = st_ref[0, pl.ds(pl.multiple_of(blk * 8, 8), 8)]
            o = _bdot((q4 * jnp.exp(b4)).reshape(8, HG_C, 128), st, 2, 2).reshape(HG_RB, 128)
            terms = []
            for s in range(HG_S):
                ok = (t_io <= s) if rev else (t_io >= s)
                f = jnp.exp(jnp.where(ok, b4 - b4[:, s:s + 1, :], NEG))
                terms.append(q4 * f * k4[:, s:s + 1, :])
            o_in = _bdot(_lane_sums(terms), v4, 2, 1)
            wq, wk = _cross_split(rev, b4)
            q_1, q_2 = _halves(q4)
            k_1, k_2 = _halves(k4)
            v_1, v_2 = _halves(v4)
            o_1, o_2 = _halves(o_in)
            if rev:
                o_1 = o_1 + _bdot(_bdot(q_1 * wq, k_2 * wk, 2, 2), v_2, 2, 1)
            else:
                o_2 = o_2 + _bdot(_bdot(q_2 * wq, k_1 * wk, 2, 2), v_1, 2, 1)
            o_ref[rows, :] = o + _join(o_1, o_2)
            return 0

        lax.fori_loop(0, HG_NB, phase_c, 0, unroll=HG_UNROLL)

    col = pl.BlockSpec((T, 128), lambda h: (0, h))
    return pl.pallas_call(
        body, name="hg_scan_bwd_dir" if rev else "hg_scan_fwd_dir", grid=(HG_HEADS,),
        in_specs=[col, col, col, pl.BlockSpec((T, 128), lambda h: (0, 24 + h))],
        out_specs=(col, pl.BlockSpec((1, HG_SLOTS, 128, 128), lambda h: (h, 0, 0, 0))),
        out_shape=(_sds((T, 512), F32), _sds((HG_HEADS, HG_SLOTS, 128, 128), F32)),
        scratch_shapes=[pltpu.VMEM((HG_SLOTS, 128), F32)],
        compiler_params=_cp(("parallel",), 56))(qh, k, b, p_act)


def _hg_scan_bwd(qh, k, b, p_act, st, do, rev):
    anchor = 0 if rev else HG_C - 1

    def body(q_ref, k_ref, b_ref, v_ref, st_ref, do_ref, dq_ref, dk_ref, db_ref, dv_ref, gst, dsc, dbl):
        def phase_a(blk, _):
            rows = pl.ds(pl.multiple_of(blk * HG_RB, HG_RB), HG_RB)
            b3 = b_ref[rows, :].reshape(8, HG_C, 128)
            q3 = q_ref[rows, :].reshape(8, HG_C, 128)
            do3 = do_ref[rows, :].reshape(8, HG_C, 128)
            gst[pl.ds(pl.multiple_of(blk * 8, 8), 8)] = _bdot(do3, q3 * jnp.exp(b3), 1, 1)
            dsc[pl.ds(pl.multiple_of(blk * 8, 8), 8), :] = jnp.exp(b3[:, anchor, :])
            return 0

        lax.fori_loop(0, HG_NB, phase_a, 0, unroll=HG_UNROLL)

        def phase_b(n, carry):
            c = n if rev else (NCHUNK - 1 - n)
            w = gst[c]
            gst[c] = carry
            dcv = dsc[pl.ds(c, 1), :]
            dbl[pl.ds(c, 1), :] = dcv * jnp.sum(st_ref[0, c] * carry, axis=0, keepdims=True)
            return carry * dcv + w

        lax.fori_loop(0, NCHUNK, phase_b, jnp.zeros((128, 128), F32))
        for c in range(NCHUNK, HG_SLOTS):
            gst[c] = jnp.zeros((128, 128), F32)
            dbl[c:c + 1, :] = jnp.zeros((1, 128), F32)

        t_io = lax.broadcasted_iota(jnp.int32, (HG_NS, HG_S, 128), 1)
        t16 = lax.broadcasted_iota(jnp.int32, (8, HG_C, 128), 1)
        r_io = lax.broadcasted_iota(jnp.int32, (HG_NS, HG_S, HG_S), 1)
        l_io = lax.broadcasted_iota(jnp.int32, (HG_NS, HG_S, HG_S), 2)

        def phase_c(blk, _):
            rows = pl.ds(pl.multiple_of(blk * HG_RB, HG_RB), HG_RB)
            cs = pl.ds(pl.multiple_of(blk * 8, 8), 8)
            b4 = b_ref[rows, :].reshape(HG_NS, HG_S, 128)
            k4 = k_ref[rows, :].reshape(HG_NS, HG_S, 128)
            q4 = q_ref[rows, :].reshape(HG_NS, HG_S, 128)
            v4 = v_ref[rows, :].reshape(HG_NS, HG_S, 128)
            do4 = do_ref[rows, :].reshape(HG_NS, HG_S, 128)
            b3, k3, q3 = (z.reshape(8, HG_C, 128) for z in (b4, k4, q4))
            v3, do3 = v4.reshape(8, HG_C, 128), do4.reshape(8, HG_C, 128)
            s_t = st_ref[0, cs]
            g_t = gst[cs]
            bl = b3[:, anchor:anchor + 1, :]
            ekl = jnp.exp(bl - b3)
            kt = k3 * ekl
            dkt = _bdot(v3, g_t, 2, 1)
            dq = (_bdot(do3, s_t, 2, 1) * jnp.exp(b3)).reshape(HG_NS, HG_S, 128)
            dk = (dkt * ekl).reshape(HG_NS, HG_S, 128)
            dv = _bdot(kt, g_t, 2, 2).reshape(HG_NS, HG_S, 128)
            dbl3 = dbl[cs, :].reshape(8, 1, 128) + jnp.sum(dkt * kt, axis=1, keepdims=True)
            causal = (l_io >= r_io) if rev else (l_io <= r_io)
            da = jnp.where(causal, _bdot(do4, v4, 2, 2), 0.0)
            causal_t = (l_io <= r_io) if rev else (l_io >= r_io)
            dat = jnp.where(causal_t, _bdot(v4, do4, 2, 2), 0.0)
            for s in range(HG_S):
                ok = (t_io <= s) if rev else (t_io >= s)
                f = jnp.exp(jnp.where(ok, b4 - b4[:, s:s + 1, :], NEG))
                dq = dq + da[:, :, s:s + 1] * (f * k4[:, s:s + 1, :])
            terms = []
            for t in range(HG_S):
                ok = (t_io >= t) if rev else (t_io <= t)
                e = jnp.exp(jnp.where(ok, b4[:, t:t + 1, :] - b4, NEG))
                eq = e * q4[:, t:t + 1, :]
                dk = dk + dat[:, :, t:t + 1] * eq
                terms.append(eq * k4)
            dv = dv + _bdot(_lane_sums(terms), do4, 2, 1)
            wq, wk = _cross_split(rev, b4)
            pick = (lambda z: _halves(z)) if rev else (lambda z: _halves(z)[::-1])
            (q_q, _), (_, k_k), (_, v_k), (do_q, _) = pick(q4), pick(k4), pick(v4), pick(do4)
            qx, kx = q_q * wq, k_k * wk
            dq_q = _bdot(_bdot(do_q, v_k, 2, 2), kx, 2, 1) * wq
            dk_k = _bdot(_bdot(v_k, do_q, 2, 2), qx, 2, 1) * wk
            dv_k = _bdot(_bdot(kx, qx, 2, 2), do_q, 2, 1)
            zero = jnp.zeros((8, HG_S, 128), F32)
            place_q = (lambda z: _join(z, zero)) if rev else (lambda z: _join(zero, z))
            place_k = (lambda z: _join(zero, z)) if rev else (lambda z: _join(z, zero))
            dq2 = dq.reshape(HG_RB, 128) + place_q(dq_q)
            dk2 = dk.reshape(HG_RB, 128) + place_k(dk_k)
            dv2 = dv.reshape(HG_RB, 128) + place_k(dv_k)
            dq3, dk3 = dq2.reshape(8, HG_C, 128), dk2.reshape(8, HG_C, 128)
            db = q3 * dq3 - k3 * dk3 + jnp.where(t16 == anchor, dbl3, 0.0)
            dq_ref[rows, :] = dq2
            dk_ref[rows, :] = dk2
            db_ref[rows, :] = db.reshape(HG_RB, 128)
            dv_ref[rows, :] = dv2
            return 0

        lax.fori_loop(0, HG_NB, phase_c, 0, unroll=HG_UNROLL)

    col = pl.BlockSpec((T, 128), lambda h: (0, h))
    return pl.pallas_call(
        body, name="hg_scan_bwd_dir_bwd" if rev else "hg_scan_fwd_dir_bwd", grid=(HG_HEADS,),
        in_specs=[col, col, col, pl.BlockSpec((T, 128), lambda h: (0, 24 + h)),
                  pl.BlockSpec((1, HG_SLOTS, 128, 128), lambda h: (h, 0, 0, 0)), col],
        out_specs=(col,) * 4, out_shape=(_sds((T, 512), F32),) * 4,
        scratch_shapes=[pltpu.VMEM((HG_SLOTS, 128, 128), F32), pltpu.VMEM((HG_SLOTS, 128), F32),
                        pltpu.VMEM((HG_SLOTS, 128), F32)],
        compiler_params=_cp(("parallel",), 56))(qh, k, b, p_act, st, do)


def _row_valid(i, tm):
    r = lax.broadcasted_iota(jnp.int32, (tm, 1), 0) + i * tm
    return r < L


def _hg_post(o_f, o_b, p_act, gain):
    def body(of_ref, ob_ref, g_ref, gain_ref, u_ref):
        o = of_ref[...] + ob_ref[...]
        sg = jax.nn.silu(g_ref[...])
        parts = []
        for h in range(HG_HEADS):
            oh = o[:, 128 * h:128 * (h + 1)]
            parts.append(oh * lax.rsqrt(jnp.mean(oh * oh, axis=-1, keepdims=True) + EPS))
        n = jnp.concatenate(parts, axis=1)
        u = n * gain_ref[...] * sg
        u_ref[...] = jnp.where(_row_valid(pl.program_id(0), TM_E), u, 0.0).astype(BF16)

    blk = pl.BlockSpec((TM_E, 512), lambda i: (i, 0))
    return pl.pallas_call(
        body, name="hg_post", grid=(T // TM_E,),
        in_specs=[blk, blk, pl.BlockSpec((TM_E, 512), lambda i: (i, 7)), pl.BlockSpec((1, 512), lambda i: (0, 0))],
        out_specs=blk, out_shape=_sds((T, 512), BF16), compiler_params=_cp(("parallel",)))(o_f, o_b, p_act, gain)


def _hg_post_bwd(du, o_f, o_b, p_act, gain):
    def body(du_ref, of_ref, ob_ref, g_ref, gain_ref, do_ref, dg_ref, dgain_ref):
        i = pl.program_id(0)
        valid = _row_valid(i, TM_E)
        duv = jnp.where(valid, du_ref[...], 0.0)
        o = of_ref[...] + ob_ref[...]
        gv = g_ref[...]
        sig = jax.nn.sigmoid(gv)
        sg = gv * sig
        gain_v = gain_ref[...]
        dn = duv * gain_v * sg
        do_parts, n_parts = [], []
        for h in range(HG_HEADS):
            sl = slice(128 * h, 128 * (h + 1))
            oh = o[:, sl]
            r = lax.rsqrt(jnp.mean(oh * oh, axis=-1, keepdims=True) + EPS)
            nh = oh * r
            dnh = dn[:, sl]
            do_parts.append(r * (dnh - nh * jnp.mean(dnh * nh, axis=-1, keepdims=True)))
            n_parts.append(nh)
        n = jnp.where(valid, jnp.concatenate(n_parts, axis=1), 0.0)
        do_ref[...] = jnp.where(valid, jnp.concatenate(do_parts, axis=1), 0.0)
        dg_ref[...] = (duv * n * gain_v * (sig * (1.0 + gv * (1.0 - sig)))).astype(BF16)
        part = jnp.sum(duv * n * sg, axis=0, keepdims=True)

        @pl.when(i == 0)
        def _():
            dgain_ref[...] = part

        @pl.when(i > 0)
        def _():
            dgain_ref[...] += part

    blk = pl.BlockSpec((TM_E, 512), lambda i: (i, 0))
    vec = pl.BlockSpec((1, 512), lambda i: (0, 0))
    return pl.pallas_call(
        body, name="hg_post_bwd", grid=(T // TM_E,),
        in_specs=[blk, blk, blk, pl.BlockSpec((TM_E, 512), lambda i: (i, 7)), vec],
        out_specs=(blk, blk, vec), out_shape=(_sds((T, 512), F32), _sds((T, 512), BF16), _sds((1, 512), F32)),
        compiler_params=_cp(("arbitrary",)))(du, o_f, o_b, p_act, gain)


def _hg_pre_bwd(p_act, logits, dq_f, dq_b, dk_f, dk_b, db_f, db_b, dv_f, dv_b):
    def body(q_ref, zf_ref, zb_ref, lg_ref, dqf_ref, dqb_ref, dkf_ref, dkb_ref, dbf_ref, dbb_ref, dvf_ref, dvb_ref,
             dq_ref, dzf_ref, dzb_ref, di_ref, dlg_ref):
        i = pl.program_id(0)
        valid = _row_valid(i, HG_RB)
        qv = q_ref[...]
        sig = jax.nn.sigmoid(qv)
        dq_ref[...] = jnp.where(valid, (dqf_ref[...] + dqb_ref[...]) * (sig * (1.0 + qv * (1.0 - sig))), 0.0).astype(BF16)
        di_ref[...] = jnp.where(valid, dvf_ref[...] + dvb_ref[...], 0.0).astype(BF16)
        for d, (z_ref, dk_r, db_r, dz_ref) in enumerate(((zf_ref, dkf_ref, dbf_ref, dzf_ref), (zb_ref, dkb_ref, dbb_ref, dzb_ref))):
            lg = lg_ref[d]
            dl = lg[0:1, :] - lg[1:2, :]
            lb = jax.nn.sigmoid(dl)
            one_m_lb = jax.nn.sigmoid(-dl)
            log_f, _, snz, w2 = _hg_gate_terms(z_ref[...], lg)
            dbv = jnp.where(valid, db_r[...], 0.0)
            dkv = jnp.where(valid, dk_r[...], 0.0)
            dlf = jnp.dot(_chunk_tri(d == 1), dbv, precision=HI, preferred_element_type=F32)
            sz = 1.0 - snz
            dz_ref[...] = (dlf * w2 * snz - dkv * one_m_lb * sz * snz).astype(BF16)
            dlb = jnp.sum(dlf * snz * jnp.exp(-log_f) - dkv * snz, axis=0, keepdims=True)
            dl0 = dlb * lb * one_m_lb
            part = jnp.concatenate([dl0, -dl0], axis=0)

            @pl.when(i == 0)
            def _():
                dlg_ref[d] = part

            @pl.when(i > 0)
            def _():
                dlg_ref[d] += part

    blk = lambda c: pl.BlockSpec((HG_RB, 512), lambda i: (i, c))
    ob = pl.BlockSpec((HG_RB, 512), lambda i: (i, 0))
    lgs = pl.BlockSpec((2, 2, 512), lambda i: (0, 0, 0))
    return pl.pallas_call(
        body, name="hg_pre_bwd", grid=(HG_NB,),
        in_specs=[blk(3), blk(4), blk(5), lgs] + [ob] * 8,
        out_specs=(ob, ob, ob, ob, lgs),
        out_shape=(_sds((T, 512), BF16),) * 4 + (_sds((2, 2, 512), F32),),
        compiler_params=_cp(("arbitrary",)))(p_act, p_act, p_act, logits, dq_f, dq_b, dk_f, dk_b, db_f, db_b, dv_f, dv_b)


def _mix_fwd(o_na, u_hg, wna_g, whg_g, p_act):
    def body(ona_ref, uhg_ref, wna_ref, whg_ref, gna_ref, ghg_ref, o_ref):
        y_na = _dot(ona_ref[...], wna_ref[0])
        y_hg = _dot(uhg_ref[...], whg_ref[0])
        o_ref[...] = (jax.nn.sigmoid(gna_ref[...]) * y_na + jax.nn.sigmoid(ghg_ref[...]) * y_hg).astype(BF16)

    act = pl.BlockSpec((TM_MM, 512), lambda i, j: (i, 0))
    wsp = pl.BlockSpec((1, 512, 128), lambda i, j: (j, 0, 0))
    return pl.pallas_call(
        body, name="mix_fwd", grid=(T // TM_MM, NDEV),
        in_specs=[act, act, wsp, wsp, pl.BlockSpec((TM_MM, 128), lambda i, j: (i, 32 + j)),
                  pl.BlockSpec((TM_MM, 128), lambda i, j: (i, 40 + j))],
        out_specs=pl.BlockSpec((TM_MM, 128), lambda i, j: (i, j)), out_shape=_sds((T, D), BF16),
        compiler_params=_cp(("parallel", "parallel")))(o_na, u_hg, wna_g, whg_g, p_act, p_act)


def _mix_bwd(o_na, u_hg, wna_g, whg_g, p_act, dmix):
    ni = T // TM_B

    def body(ona_ref, uhg_ref, wna_ref, whg_ref, gna_ref, ghg_ref, dmix_ref,
             dgna_ref, dghg_ref, dwna_ref, dwhg_ref, dona_ref, duhg_ref, acc_na, acc_hg):
        j, i = pl.program_id(0), pl.program_id(1)
        rows = pl.ds(pl.multiple_of(i * TM_B, TM_B), TM_B)
        dm = dmix_ref[...].astype(F32)
        for x_ref, w_ref, g_ref, dg_ref, dx_ref, dw_ref, acc in (
                (ona_ref, wna_ref, gna_ref, dgna_ref, dona_ref, dwna_ref, acc_na),
                (uhg_ref, whg_ref, ghg_ref, dghg_ref, duhg_ref, dwhg_ref, acc_hg)):
            xv = x_ref[...]
            y = _dot(xv, w_ref[0])
            sg = jax.nn.sigmoid(g_ref[...])
            dg_ref[...] = (dm * y * sg * (1.0 - sg)).astype(BF16)
            dy = (dm * sg).astype(BF16)
            part = _dot(xv, dy, TN)
            dxv = _dot(dy, w_ref[0], NT)

            @pl.when(i == 0)
            def _():
                acc[...] = part

            @pl.when(i > 0)
            def _():
                acc[...] += part

            @pl.when(i == ni - 1)
            def _():
                dw_ref[0] = acc[...].astype(BF16)

            @pl.when(j == 0)
            def _():
                dx_ref[rows, :] = dxv

            @pl.when(j > 0)
            def _():
                dx_ref[rows, :] += dxv

    act = pl.BlockSpec((TM_B, 512), lambda j, i: (i, 0))
    wsp = pl.BlockSpec((1, 512, 128), lambda j, i: (j, 0, 0))
    cblk = pl.BlockSpec((TM_B, 128), lambda j, i: (i, j))
    full = pl.BlockSpec((T, 512), lambda j, i: (0, 0))
    return pl.pallas_call(
        body, name="mix_bwd", grid=(NDEV, ni),
        in_specs=[act, act, wsp, wsp, pl.BlockSpec((TM_B, 128), lambda j, i: (i, 32 + j)),
                  pl.BlockSpec((TM_B, 128), lambda j, i: (i, 40 + j)), cblk],
        out_specs=(cblk, cblk, wsp, wsp, full, full),
        out_shape=(_sds((T, D), BF16), _sds((T, D), BF16), _sds((NDEV, 512, 128), BF16), _sds((NDEV, 512, 128), BF16),
                   _sds((T, 512), F32), _sds((T, 512), F32)),
        scratch_shapes=[pltpu.VMEM((512, 128), F32), pltpu.VMEM((512, 128), F32)],
        compiler_params=_cp(("arbitrary", "arbitrary")))(o_na, u_hg, wna_g, whg_g, p_act, p_act, dmix)


def _wo_fwd(mix, w_o, h0, g_mlp):
    def body(mix_ref, w_ref, h0_ref, g_ref, h1_ref, m_ref):
        h1 = h0_ref[...] + _dot(mix_ref[...], w_ref[...])
        h1_ref[...] = h1
        r = lax.rsqrt(jnp.mean(h1 * h1, axis=-1, keepdims=True) + EPS)
        m_ref[...] = (h1 * r * g_ref[...]).astype(BF16)

    blk = pl.BlockSpec((TM_B, D), lambda i: (i, 0))
    return pl.pallas_call(
        body, name="wo_fwd", grid=(T // TM_B,),
        in_specs=[blk, pl.BlockSpec((D, D), lambda i: (0, 0)), blk, pl.BlockSpec((1, D), lambda i: (0, 0))],
        out_specs=(blk, blk), out_shape=(_sds((T, D), F32), _sds((T, D), BF16)),
        compiler_params=_cp(("parallel",)))(mix, w_o, h0, g_mlp)


def _wo_bwd(dh1_b, w_o, mix):
    ni = T // TM_B

    def body(dh_ref, w_ref, mix_ref, dmix_ref, dw_ref, acc):
        i = pl.program_id(0)
        dh = dh_ref[...]
        dmix_ref[...] = _dot(dh, w_ref[...], NT).astype(BF16)
        part = _dot(mix_ref[...], dh, TN)

        @pl.when(i == 0)
        def _():
            acc[...] = part

        @pl.when(i > 0)
        def _():
            acc[...] += part

        @pl.when(i == ni - 1)
        def _():
            dw_ref[...] = acc[...].astype(BF16)

    blk = pl.BlockSpec((TM_B, D), lambda i: (i, 0))
    wsp = pl.BlockSpec((D, D), lambda i: (0, 0))
    return pl.pallas_call(
        body, name="wo_bwd", grid=(ni,), in_specs=[blk, wsp, blk], out_specs=(blk, wsp),
        out_shape=(_sds((T, D), BF16), _sds((D, D), BF16)), scratch_shapes=[pltpu.VMEM((D, D), F32)],
        compiler_params=_cp(("arbitrary",)))(dh1_b, w_o, mix)


FF_B = D_FF // NDEV


def _mlp_fwd(m, wup_g, wdown_g, h1):
    def body(m_ref, wu_ref, wd_ref, h1_ref, h2_ref):
        j = pl.program_id(1)
        up = jnp.maximum(_dot(m_ref[...], wu_ref[0]), 0.0)
        part = _dot((up * up).astype(BF16), wd_ref[0])

        @pl.when(j == 0)
        def _():
            h2_ref[...] = h1_ref[...] + part

        @pl.when(j > 0)
        def _():
            h2_ref[...] += part

    blk = pl.BlockSpec((TM_MM, D), lambda i, j: (i, 0))
    return pl.pallas_call(
        body, name="mlp_fwd", grid=(T // TM_MM, NDEV),
        in_specs=[blk, pl.BlockSpec((1, D, FF_B), lambda i, j: (j, 0, 0)), pl.BlockSpec((1, FF_B, D), lambda i, j: (j, 0, 0)), blk],
        out_specs=blk, out_shape=_sds((T, D), F32),
        compiler_params=_cp(("parallel", "arbitrary")))(m, wup_g, wdown_g, h1)


def _mlp_bwd(m, dh2_b, wup_g, wdown_g):
    ni = T // TM_B

    def body(m_ref, dh_ref, wu_ref, wd_ref, dwu_ref, dwd_ref, dm_ref, acc_u, acc_d):
        j, i = pl.program_id(0), pl.program_id(1)
        rows = pl.ds(pl.multiple_of(i * TM_B, TM_B), TM_B)
        mv, dh = m_ref[...], dh_ref[...]
        r = jnp.maximum(_dot(mv, wu_ref[0]), 0.0)
        act = (r * r).astype(BF16)
        dact = _dot(dh, wd_ref[0], NT)
        dup = (dact * (2.0 * r)).astype(BF16)
        pd = _dot(act, dh, TN)
        pu = _dot(mv, dup, TN)
        dmv = _dot(dup, wu_ref[0], NT)

        @pl.when(i == 0)
        def _():
            acc_u[...] = pu
            acc_d[...] = pd

        @pl.when(i > 0)
        def _():
            acc_u[...] += pu
            acc_d[...] += pd

        @pl.when(i == ni - 1)
        def _():
            dwu_ref[0] = acc_u[...].astype(BF16)
            dwd_ref[0] = acc_d[...].astype(BF16)

        @pl.when(j == 0)
        def _():
            dm_ref[rows, :] = dmv

        @pl.when(j > 0)
        def _():
            dm_ref[rows, :] += dmv

    blk = pl.BlockSpec((TM_B, D), lambda j, i: (i, 0))
    wus = pl.BlockSpec((1, D, FF_B), lambda j, i: (j, 0, 0))
    wds = pl.BlockSpec((1, FF_B, D), lambda j, i: (j, 0, 0))
    return pl.pallas_call(
        body, name="mlp_bwd", grid=(NDEV, ni), in_specs=[blk, blk, wus, wds],
        out_specs=(wus, wds, pl.BlockSpec((T, D), lambda j, i: (0, 0))),
        out_shape=(_sds((NDEV, D, FF_B), BF16), _sds((NDEV, FF_B, D), BF16), _sds((T, D), F32)),
        scratch_shapes=[pltpu.VMEM((D, FF_B), F32), pltpu.VMEM((FF_B, D), F32)],
        compiler_params=_cp(("arbitrary", "arbitrary")))(m, dh2_b, wup_g, wdown_g)


def _loss_head(h2, g_final, tgt):
    def body(h_ref, g_ref, t_ref, loss_ref, dh_ref, dhb_ref, dg_ref):
        i = pl.program_id(0)
        r_io = lax.broadcasted_iota(jnp.int32, (TM_E, 1), 0) + i * TM_E
        valid = (r_io >= NM) & (r_io < L)
        xv = h_ref[...]
        r = lax.rsqrt(jnp.mean(xv * xv, axis=-1, keepdims=True) + EPS)
        xh = xv * r
        gv = g_ref[...]
        err = jnp.where(valid, xh * gv - t_ref[...], 0.0)
        lpart = jnp.broadcast_to(0.5 * jnp.sum(jnp.sum(err * err, axis=-1, keepdims=True) * (1.0 / D), axis=0, keepdims=True), (1, 128))
        dy = err * (1.0 / D)
        dxh = dy * gv
        dh = r * (dxh - xh * jnp.mean(dxh * xh, axis=-1, keepdims=True))
        dh_ref[...] = dh
        dhb_ref[...] = dh.astype(BF16)
        gpart = jnp.sum(dy * xh, axis=0, keepdims=True)

        @pl.when(i == 0)
        def _():
            loss_ref[...] = lpart
            dg_ref[...] = gpart

        @pl.when(i > 0)
        def _():
            loss_ref[...] += lpart
            dg_ref[...] += gpart

    blk = pl.BlockSpec((TM_E, D), lambda i: (i, 0))
    vec = pl.BlockSpec((1, D), lambda i: (0, 0))
    return pl.pallas_call(
        body, name="loss_head", grid=(T // TM_E,), in_specs=[blk, vec, blk],
        out_specs=(pl.BlockSpec((1, 128), lambda i: (0, 0)), blk, blk, vec),
        out_shape=(_sds((1, 128), F32), _sds((T, D), F32), _sds((T, D), BF16), _sds((1, D), F32)),
        compiler_params=_cp(("arbitrary",)))(h2, g_final, tgt)


def _adamw(parts, w, m, v, name):
    rr, cc = w.shape
    tr = rr
    for cand in (256, 128, 64):
        if rr % cand == 0 and rr > cand:
            tr = cand
            break
    c1 = 1.0 - ADAM_B1 ** ADAM_STEP
    c2 = 1.0 - ADAM_B2 ** ADAM_STEP

    def body(p_ref, w_ref, m_ref, v_ref, g_ref, d_ref, nm_ref, nv_ref):
        g = p_ref[0].astype(F32)
        for s in range(1, NDEV):
            g = g + p_ref[s].astype(F32)
        mn = ADAM_B1 * m_ref[...] + (1.0 - ADAM_B1) * g
        vn = ADAM_B2 * v_ref[...] + (1.0 - ADAM_B2) * (g * g)
        g_ref[...] = g
        nm_ref[...] = mn
        nv_ref[...] = vn
        d_ref[...] = -ADAM_LR * ((mn / c1) / (jnp.sqrt(vn / c2) + ADAM_EPS) + ADAM_WD * w_ref[...])

    blk = pl.BlockSpec((tr, cc), lambda i: (i, 0))
    return pl.pallas_call(
        body, name=name, grid=(rr // tr,),
        in_specs=[pl.BlockSpec((NDEV, tr, cc), lambda i: (0, i, 0)), blk, blk, blk],
        out_specs=(blk,) * 4, out_shape=(_sds((rr, cc), F32),) * 4,
        compiler_params=_cp(("parallel",)))(parts, w, m, v)


RPB_N = NA_HEADS * 15 * 31
RPB_PAD = 4096
OWN_ROWS = NM + 8


def _pad_rows(a, rows):
    return jnp.pad(a, ((0, rows - a.shape[0]),) + ((0, 0),) * (a.ndim - 1))


def _pack_owned(meta_blk, lb_blk):
    return jnp.concatenate([meta_blk, _pad_rows(lb_blk.reshape(2, 128), 8)], axis=0)


def _pack_replicated(n_mix, n_mlp, n_final, hg_gain, rpb):
    flat = _pad_rows(rpb.reshape(RPB_N), RPB_PAD)
    return jnp.concatenate([n_mix.reshape(8, 128), n_mlp.reshape(8, 128), n_final.reshape(8, 128),
                            _pad_rows(hg_gain.reshape(4, 128), 8), flat.reshape(32, 128)], axis=0)


def _unpack_replicated(a):
    return (a[0:8].reshape(1, D), a[8:16].reshape(1, D), a[16:24].reshape(D), a[24:28].reshape(1, 512),
            a[32:64].reshape(RPB_PAD)[:RPB_N].reshape(1, NA_HEADS, 15, 31))


def kernel(x, meta_tokens, w_in, w_na_out, w_hg_out, w_o, w_up, w_down, norm_mix, norm_mlp, norm_final, hg_norm, na_rpb, hg_lb_logits, loss_target, m_meta_tokens, m_w_in, m_w_na_out, m_w_hg_out, m_w_o, m_w_up, m_w_down, m_norm_mix, m_norm_mlp, m_norm_final, m_hg_norm, m_na_rpb, m_hg_lb_logits, v_meta_tokens, v_w_in, v_w_na_out, v_w_hg_out, v_w_o, v_w_up, v_w_down, v_norm_mix, v_norm_mlp, v_norm_final, v_hg_norm, v_na_rpb, v_hg_lb_logits):
    owned = _pack_owned(meta_tokens, hg_lb_logits)
    win_g, owned_g = _gather_two_level([w_in[0].astype(BF16), owned], "gather_first")
    later = [w[0].astype(BF16) for w in (w_na_out, w_hg_out, w_o, w_up, w_down)]
    later[0] = _tie(later[0], owned_g, "tie_gather_rest")
    gather_rest, tok = _exchange_start(later, [False] * 5, "gather_rest_start")
    win_g = _tie(win_g, tok, "tie_inproj")
    meta_full = jnp.transpose(owned_g[:, 0:NM, :], (1, 0, 2)).reshape(NM, D)
    logits = jnp.transpose(owned_g[:, NM:NM + 2, :].reshape(NDEV, 2, 2, 64), (1, 2, 0, 3)).reshape(2, 2, 512)

    h0 = jnp.concatenate([meta_full, x[0], jnp.zeros((T - L, D), F32)], axis=0)
    tgt = jnp.concatenate([jnp.zeros((NM, D), F32), loss_target[0], jnp.zeros((T - L, D), F32)], axis=0)
    bias_tab = _na_bias_table(na_rpb[0])

    a = _norm_fwd(h0, norm_mix, "norm_mix_fwd")
    p_act = _inproj_fwd(a, win_g)
    o_na, lse = _na_fwd(p_act, bias_tab)
    qh, k_f, b_f, k_b, b_b = _hg_pre(p_act, logits)
    o_f, st_f = _hg_scan_fwd(qh, k_f, b_f, p_act, False)
    o_b, st_b = _hg_scan_fwd(qh, k_b, b_b, p_act, True)
    u_hg = _hg_post(o_f, o_b, p_act, hg_norm)
    wna_g, whg_g, wo_g, wup_g, wdown_g = _exchange_wait(gather_rest, [False] * 5, [u_hg, o_na], "gather_rest_wait")
    w_o_full = wo_g.reshape(D, D)
    mix = _mix_fwd(o_na, u_hg, wna_g, whg_g, p_act)
    h1, m_act = _wo_fwd(mix, w_o_full, h0, norm_mlp)
    h2 = _mlp_fwd(m_act, wup_g, wdown_g, h1)
    loss_part, dh2, dh2_b, d_nfinal = _loss_head(h2, norm_final.reshape(1, D), tgt)

    dwup_p, dwdown_p, dm = _mlp_bwd(m_act, dh2_b, wup_g, wdown_g)
    sc_mlp, tok = _exchange_start([dwup_p, dwdown_p], [True] * 2, "scatter_mlp_start")
    dh1, dh1_b, d_nmlp = _norm_bwd(h1, norm_mlp, _tie(dm, tok, "tie_norm_mlp_bwd"), dh2, "norm_mlp_bwd")
    dmix, dwo = _wo_bwd(dh1_b, w_o_full, mix)
    sc_wo, tok = _exchange_start([dwo.reshape(NDEV, D // NDEV, D)], [True], "scatter_wo_start")
    dgna, dghg, dwna_p, dwhg_p, do_na, du_hg = _mix_bwd(o_na, u_hg, wna_g, whg_g, p_act, _tie(dmix, tok, "tie_mix_bwd"))
    sc_br, tok = _exchange_start([dwna_p, dwhg_p], [True] * 2, "scatter_branch_start")
    du_hg = _tie(du_hg, tok, "tie_hg_post_bwd")
    do_hg, dg_hg, d_gain = _hg_post_bwd(du_hg, o_f, o_b, p_act, hg_norm)
    dq_f, dk_f, db_f, dv_f = _hg_scan_bwd(qh, k_f, b_f, p_act, st_f, do_hg, False)
    dq_b, dk_b, db_b, dv_b = _hg_scan_bwd(qh, k_b, b_b, p_act, st_b, do_hg, True)
    dq_hg, dz_f, dz_b, di_hg, d_logits = _hg_pre_bwd(p_act, logits, dq_f, dq_b, dk_f, dk_b, db_f, db_b, dv_f, dv_b)
    dq_na, dk_na, dv_na, dbias = _na_bwd(p_act, do_na, lse, bias_tab)
    dp = jnp.concatenate([dq_na.astype(BF16), dk_na.astype(BF16), dv_na.astype(BF16), dq_hg, dz_f, dz_b, di_hg, dg_hg,
                          dgna, dghg], axis=1)
    dwin_p, da = _inproj_bwd(a, dp, win_g)
    sc_in, tok = _exchange_start([dwin_p], [True], "scatter_in_start")
    dh0, _, d_nmix = _norm_bwd(h0, norm_mix, _tie(da, tok, "tie_norm_mix_bwd"), dh1, "norm_mix_bwd")
    d_rpb = _na_rpb_reduce(_tie(dbias, tok, "tie_rpb_reduce"))[:, :, :31]

    res = {}

    def update(nm, parts, w, mm, vv):
        res[nm] = [r[None] for r in _adamw(parts, w[0], mm[0], vv[0], "adamw_" + nm)]
        return res[nm][1]

    wup_r, wdown_r = _exchange_wait(sc_mlp, [True] * 2, [dh0, d_rpb], "scatter_mlp_wait")
    update("w_up", wup_r, w_up, m_w_up, v_w_up)
    last = update("w_down", wdown_r, w_down, m_w_down, v_w_down)
    (wo_r,) = _exchange_wait(sc_wo, [True], [last], "scatter_wo_wait")
    last = update("w_o", wo_r, w_o, m_w_o, v_w_o)
    wna_r, whg_r = _exchange_wait(sc_br, [True] * 2, [last], "scatter_branch_wait")
    update("w_na_out", wna_r, w_na_out, m_w_na_out, v_w_na_out)
    last = update("w_hg_out", whg_r, w_hg_out, m_w_hg_out, v_w_hg_out)

    d_meta = jnp.transpose(dh0[0:NM].reshape(NM, NDEV, 128), (1, 0, 2))
    d_lg = jnp.transpose(d_logits.reshape(2, 2, NDEV, 64), (2, 0, 1, 3)).reshape(NDEV, 2, 128)
    owned_p = jnp.concatenate([d_meta, jnp.pad(d_lg, ((0, 0), (0, OWN_ROWS - NM - 2), (0, 0)))], axis=1)
    repl_p = _pack_replicated(d_nmix, d_nmlp, d_nfinal, d_gain, d_rpb)
    owned_r, repl_r = _exchange([_tie(owned_p, last, "tie_scatter_small"), repl_p], [True, False], "scatter_small")
    own = _adamw(owned_r, owned, _pack_owned(m_meta_tokens, m_hg_lb_logits), _pack_owned(v_meta_tokens, v_hg_lb_logits),
                 "adamw_owned_small")
    res["meta_tokens"] = [r[0:NM] for r in own]
    res["hg_lb_logits"] = [r[NM:NM + 2].reshape(2, 2, 64) for r in own]
    rep = _adamw(repl_r, _pack_replicated(norm_mix, norm_mlp, norm_final, hg_norm, na_rpb),
                 _pack_replicated(m_norm_mix, m_norm_mlp, m_norm_final, m_hg_norm, m_na_rpb),
                 _pack_replicated(v_norm_mix, v_norm_mlp, v_norm_final, v_hg_norm, v_na_rpb), "adamw_replicated")
    for q in range(4):
        um = _unpack_replicated(rep[q])
        for nm, val in zip(("norm_mix", "norm_mlp", "norm_final", "hg_norm", "na_rpb"), um):
            res.setdefault(nm, [None] * 4)[q] = val
    (win_r,) = _exchange_wait(sc_in, [True], [rep[1], own[1]], "scatter_in_wait")
    update("w_in", win_r, w_in, m_w_in, v_w_in)

    loss = lax.psum(loss_part[0, 0], ("x", "y", "c"))
    grad_x = dh0[NM:L][None]
    order = ("meta_tokens", "w_in", "w_na_out", "w_hg_out", "w_o", "w_up", "w_down", "norm_mix", "norm_mlp", "norm_final",
             "hg_norm", "na_rpb", "hg_lb_logits")
    outs = [loss, grad_x]
    for q in range(4):
        outs += [res[nm][q] for nm in order]
    return tuple(outs)
```

```python
import functools

import numpy as np
import jax
import jax.numpy as jnp
from jax import lax
from jax.experimental import pallas as pl
from jax.experimental.pallas import tpu as pltpu

F32 = jnp.float32
BF16 = jnp.bfloat16

D = 1024
SEQ = 2048
NM = 16
L = SEQ + NM
T = 2176
NDEV = 8
EPS = 1e-6
GRID_W = 64
ROWS = SEQ // GRID_W
NA_HEADS = 8
NA_DH = 64
NA_SCALE = NA_DH ** -0.5
HG_HEADS = 4
HG_C = 16
NCHUNK = L // HG_C
D_FF = 4096
IN_COLS = 6144
NEG = -1e30

ADAM_LR = 0.001
ADAM_B1 = 0.9
ADAM_B2 = 0.999
ADAM_EPS = 1e-08
ADAM_WD = 0.01
ADAM_STEP = 10

MESH_ID = pl.DeviceIdType.MESH
ANY = pl.BlockSpec(memory_space=pl.ANY)

NN = (((1,), (0,)), ((), ()))
NT = (((1,), (1,)), ((), ()))
TN = (((0,), (0,)), ((), ()))


def _cp(sem=None, vmem_mb=48):
    return pltpu.CompilerParams(dimension_semantics=sem, vmem_limit_bytes=vmem_mb * 1024 * 1024)


def _dot(a, b, dims=NN):
    return lax.dot_general(a, b, dims, preferred_element_type=F32)


def _sds(shape, dtype):
    return jax.ShapeDtypeStruct(shape, dtype)


HBM = pl.BlockSpec(memory_space=pltpu.HBM)
SEM = pl.BlockSpec(memory_space=pltpu.SEMAPHORE)
EFFECT = pltpu.SideEffectType.DATAFLOW_SIDE_EFFECTING


def _exchange(arrs, scatter, name):
    n = len(arrs)
    out_shapes = []
    for a, sc in zip(arrs, scatter):
        out_shapes.append(_sds(a.shape if sc else (NDEV,) + a.shape, a.dtype))

    def body(*refs):
        ins, outs = refs[:n], refs[n:2 * n]
        send_sems, recv_sems, loc_sems = refs[2 * n:]
        me = 4 * lax.axis_index("x") + 2 * lax.axis_index("y") + lax.axis_index("c")
        copies = []
        for k in range(n):
            src_me = ins[k].at[me] if scatter[k] else ins[k]
            loc = pltpu.make_async_copy(src_me, outs[k].at[me], loc_sems.at[k])
            loc.start()
            copies.append(loc)
        remote = _peer_copies(ins, outs, scatter, send_sems, recv_sems)
        for cp in remote:
            cp.start()
        for cp in remote:
            cp.wait_recv()
        for cp in remote:
            cp.wait_send()
        for cp in copies:
            cp.wait()

    return pl.pallas_call(
        body, name=name, out_shape=tuple(out_shapes), in_specs=[ANY] * n, out_specs=tuple([ANY] * n),
        scratch_shapes=[pltpu.SemaphoreType.DMA((n * (NDEV - 1),)), pltpu.SemaphoreType.DMA((n * (NDEV - 1),)),
                        pltpu.SemaphoreType.DMA((n,))],
    )(*arrs)


def _gather_two_level(arrs, name):
    n = len(arrs)

    def body(*refs):
        ins, outs = refs[:n], refs[n:2 * n]
        send_sems, recv_sems, loc_sems = refs[2 * n:]
        x, y, c = lax.axis_index("x"), lax.axis_index("y"), lax.axis_index("c")
        sib = (x, y, 1 - c)
        chips = [(1 - x, y), (x, 1 - y), (1 - x, 1 - y)]

        def slot(k, px, py, pc):
            return outs[k].at[4 * px + 2 * py + pc]

        def copy(k, q, block, to, src=None):
            return pltpu.make_async_remote_copy(
                src_ref=slot(k, *block) if src is None else src, dst_ref=slot(k, *block),
                send_sem=send_sems.at[7 * k + q], recv_sem=recv_sems.at[7 * k + q], device_id=to, device_id_type=MESH_ID)

        mine = [pltpu.make_async_copy(ins[k], slot(k, x, y, c), loc_sems.at[k]) for k in range(n)]
        for cp in mine:
            cp.start()
        first = []
        for k in range(n):
            first.append(copy(k, 0, (x, y, c), sib, src=ins[k]))
            first += [copy(k, 1 + j, (x, y, c), (*chip, c), src=ins[k]) for j, chip in enumerate(chips)]
        for cp in first:
            cp.start()
        passed = []
        for j, chip in enumerate(chips):
            for k in range(n):
                copy(k, 1 + j, (*chip, c), (x, y, c)).wait_recv()
                fw = copy(k, 4 + j, (*chip, c), sib)
                fw.start()
                passed.append(fw)
        for k in range(n):
            copy(k, 0, (x, y, 1 - c), (x, y, c)).wait_recv()
            for j, chip in enumerate(chips):
                copy(k, 4 + j, (*chip, 1 - c), (x, y, c)).wait_recv()
        for cp in first + passed:
            cp.wait_send()
        for cp in mine:
            cp.wait()

    return pl.pallas_call(
        body, name=name, out_shape=tuple(_sds((NDEV,) + a.shape, a.dtype) for a in arrs),
        in_specs=[ANY] * n, out_specs=tuple([ANY] * n),
        scratch_shapes=[pltpu.SemaphoreType.DMA((7 * n,)), pltpu.SemaphoreType.DMA((7 * n,)), pltpu.SemaphoreType.DMA((n,))],
    )(*arrs)


def _peer_copies(srcs, lands, scatter, send_sems, recv_sems):
    x, y, c = lax.axis_index("x"), lax.axis_index("y"), lax.axis_index("c")
    me = 4 * x + 2 * y + c
    out = []
    for k in range(len(srcs)):
        for m in range(1, NDEV):
            px, py, pc = x ^ (m >> 2), y ^ ((m >> 1) & 1), c ^ (m & 1)
            src = srcs[k].at[4 * px + 2 * py + pc] if scatter[k] else srcs[k]
            out.append(pltpu.make_async_remote_copy(
                src_ref=src, dst_ref=lands[k].at[me], send_sem=send_sems.at[k * (NDEV - 1) + m - 1],
                recv_sem=recv_sems.at[k * (NDEV - 1) + m - 1],
                device_id=(px, py, pc), device_id_type=MESH_ID))
    return out


def _exchange_start(arrs, scatter, name):
    n = len(arrs)
    me = 4 * lax.axis_index("x") + 2 * lax.axis_index("y") + lax.axis_index("c")
    lands = []
    for a, sc in zip(arrs, scatter):
        own = lax.dynamic_index_in_dim(a, me, 0, keepdims=True) if sc else a[None]
        shape = a.shape if sc else (NDEV,) + a.shape
        lands.append(lax.dynamic_update_index_in_dim(lax.empty(shape, a.dtype), own, me, 0))

    def body(*refs):
        srcs, lnds = refs[:n], refs[n:2 * n]
        send_sems, recv_sems = refs[2 * n], refs[2 * n + 1]
        token = refs[-1]
        for cp in _peer_copies(srcs, lnds, scatter, send_sems, recv_sems):
            cp.start()
        token[...] = jnp.zeros_like(token)

    ops = [pltpu.with_memory_space_constraint(a, pltpu.HBM) for a in list(arrs) + lands]
    res = pl.pallas_call(
        body, name=name,
        out_shape=(pltpu.SemaphoreType.DMA((n * (NDEV - 1),)), pltpu.SemaphoreType.DMA((n * (NDEV - 1),)))
        + tuple(pltpu.HBM(o.shape, o.dtype) for o in ops) + (_sds((8, 128), F32),),
        in_specs=[HBM] * (2 * n), out_specs=(SEM, SEM) + (HBM,) * (2 * n) + (pl.BlockSpec(memory_space=pltpu.VMEM),),
        input_output_aliases={k: 2 + k for k in range(2 * n)},
        compiler_params=pltpu.CompilerParams(has_side_effects=EFFECT),
    )(*ops)
    return res[:-1], res[-1]


def _exchange_wait(handle, scatter, after, name):
    send_sems, recv_sems = handle[0], handle[1]
    bufs = handle[2:]
    n = len(bufs) // 2
    after = list(after)

    def body(*refs):
        srcs, lnds = refs[:n], refs[n:2 * n]
        for cp in _peer_copies(srcs, lnds, scatter, refs[2 * n], refs[2 * n + 1]):
            cp.wait_send()
            cp.wait_recv()

    res = pl.pallas_call(
        body, name=name, out_shape=tuple(pltpu.HBM(b.shape, b.dtype) for b in bufs),
        in_specs=[HBM] * (2 * n) + [SEM, SEM] + [ANY] * len(after), out_specs=(HBM,) * (2 * n),
        input_output_aliases={k: k for k in range(2 * n)},
        compiler_params=pltpu.CompilerParams(has_side_effects=EFFECT),
    )(*bufs, send_sems, recv_sems, *after)
    return res[n:]


def _tie(x, token, name):
    def body(x_ref, t_ref, o_ref):
        del x_ref, t_ref, o_ref

    return pl.pallas_call(body, name=name, out_shape=_sds(x.shape, x.dtype), in_specs=[ANY, ANY], out_specs=ANY,
                          input_output_aliases={0: 0})(x, token)


TM_E = 272


def _norm_fwd(h, g, name):
    def body(h_ref, g_ref, o_ref):
        xv = h_ref[...]
        r = lax.rsqrt(jnp.mean(xv * xv, axis=-1, keepdims=True) + EPS)
        o_ref[...] = (xv * r * g_ref[...]).astype(BF16)

    return pl.pallas_call(
        body, name=name, grid=(T // TM_E,),
        in_specs=[pl.BlockSpec((TM_E, D), lambda i: (i, 0)), pl.BlockSpec((1, D), lambda i: (0, 0))],
        out_specs=pl.BlockSpec((TM_E, D), lambda i: (i, 0)), out_shape=_sds((T, D), BF16),
        compiler_params=_cp(("parallel",)))(h, g)


def _norm_bwd(h, g, dn, dres, name):
    def body(h_ref, g_ref, dn_ref, dres_ref, dh_ref, dhb_ref, dg_ref):
        i = pl.program_id(0)
        xv = h_ref[...]
        r = lax.rsqrt(jnp.mean(xv * xv, axis=-1, keepdims=True) + EPS)
        xh = xv * r
        dnv = dn_ref[...].astype(F32)
        dxh = dnv * g_ref[...]
        dh = dres_ref[...] + r * (dxh - xh * jnp.mean(dxh * xh, axis=-1, keepdims=True))
        dh_ref[...] = dh
        dhb_ref[...] = dh.astype(BF16)
        part = jnp.sum(dnv * xh, axis=0, keepdims=True)

        @pl.when(i == 0)
        def _():
            dg_ref[...] = part

        @pl.when(i > 0)
        def _():
            dg_ref[...] += part

    blk = pl.BlockSpec((TM_E, D), lambda i: (i, 0))
    vec = pl.BlockSpec((1, D), lambda i: (0, 0))
    return pl.pallas_call(
        body, name=name, grid=(T // TM_E,), in_specs=[blk, vec, blk, blk], out_specs=(blk, blk, vec),
        out_shape=(_sds((T, D), F32), _sds((T, D), BF16), _sds((1, D), F32)),
        compiler_params=_cp(("arbitrary",)))(h, g, dn, dres)


TM_MM = 1088


def _inproj_fwd(a, w_g):
    nb = w_g.shape[2]

    def body(a_ref, w_ref, o_ref):
        o_ref[...] = _dot(a_ref[...], w_ref[0])

    return pl.pallas_call(
        body, name="inproj_fwd", grid=(T // TM_MM, NDEV),
        in_specs=[pl.BlockSpec((TM_MM, D), lambda i, j: (i, 0)), pl.BlockSpec((1, D, nb), lambda i, j: (j, 0, 0))],
        out_specs=pl.BlockSpec((TM_MM, nb), lambda i, j: (i, j)), out_shape=_sds((T, NDEV * nb), F32),
        compiler_params=_cp(("parallel", "parallel")))(a, w_g)


TM_B = 544


def _inproj_bwd_dw(a, dp):
    nb = IN_COLS // NDEV
    ni = T // TM_B

    def body(a_ref, dp_ref, dw_ref, acc):
        i = pl.program_id(1)
        part = _dot(a_ref[...], dp_ref[...], TN)

        @pl.when(i == 0)
        def _():
            acc[...] = part

        @pl.when(i > 0)
        def _():
            acc[...] += part

        @pl.when(i == ni - 1)
        def _():
            dw_ref[0] = acc[...].astype(BF16)

    return pl.pallas_call(
        body, name="inproj_bwd_dw", grid=(NDEV, ni),
        in_specs=[pl.BlockSpec((TM_B, D), lambda j, i: (i, 0)), pl.BlockSpec((TM_B, nb), lambda j, i: (i, j))],
        out_specs=pl.BlockSpec((1, D, nb), lambda j, i: (j, 0, 0)), out_shape=_sds((NDEV, D, nb), BF16),
        scratch_shapes=[pltpu.VMEM((D, nb), F32)],
        compiler_params=_cp(("parallel", "arbitrary")))(a, dp)


def _inproj_bwd_da(dp, w_g):
    nb = w_g.shape[2]

    def body(dp_ref, w_ref, da_ref):
        j = pl.program_id(1)
        dav = _dot(dp_ref[...], w_ref[0], NT)

        @pl.when(j == 0)
        def _():
            da_ref[...] = dav

        @pl.when(j > 0)
        def _():
            da_ref[...] += dav

    return pl.pallas_call(
        body, name="inproj_bwd_da", grid=(T // TM_MM, NDEV),
        in_specs=[pl.BlockSpec((TM_MM, nb), lambda i, j: (i, j)), pl.BlockSpec((1, D, nb), lambda i, j: (j, 0, 0))],
        out_specs=pl.BlockSpec((TM_MM, D), lambda i, j: (i, 0)), out_shape=_sds((T, D), F32),
        compiler_params=_cp(("parallel", "arbitrary")))(dp, w_g)


NA_QB = 256
NA_GROUPS = ROWS // 4
NA_UROWS = 11
NA_KW = NA_UROWS * GRID_W
NA_KU = 768


def _na_row_offset(var, i, j):
    valid = (j < 8, i <= j < i + 8, 3 <= j < NA_UROWS)[var]
    return (j - i + (7, 3, 0)[var]) if valid else None


def _na_bias_table(rpb):
    def body(r_ref, o_ref):
        row = lax.broadcasted_iota(jnp.int32, (GRID_W, 128), 0)
        lane = lax.broadcasted_iota(jnp.int32, (GRID_W, 128), 1)
        w = lane & (GRID_W - 1)
        cs = jnp.clip(row - 8, 0, GRID_W - 16)
        in_win = (w >= cs) & (w < cs + 16)
        neg = jnp.full((GRID_W, 128), NEG, F32)
        tabs = []
        for a in range(15):
            z = jnp.broadcast_to(r_ref[0, a:a + 1, :], (GRID_W, 128))
            for bit in range(6):
                sh = 1 << bit
                z = jnp.where((row & sh) != 0, jnp.roll(z, sh, axis=1), z)
            z = jnp.roll(z, 128 - 15, axis=1)
            z = jnp.where(lane < GRID_W, z, 0.0)
            z = z + jnp.roll(z, GRID_W, axis=1)
            tabs.append(jnp.where(in_win, z, NEG))
        tail = jnp.where(lane < GRID_W + NM, 0.0, NEG)
        for var in range(3):
            for i in range(4):
                for jp in range(NA_KU // 128):
                    halves = []
                    for j in (2 * jp, 2 * jp + 1):
                        a = _na_row_offset(var, i, j) if j < NA_UROWS else None
                        halves.append(tail if j >= NA_UROWS else (neg if a is None else tabs[a]))
                    o_ref[var, 0, i * 64:(i + 1) * 64, jp * 128:(jp + 1) * 128] = jnp.where(lane < GRID_W, halves[0], halves[1])

    rp = jnp.concatenate([rpb, jnp.zeros((NA_HEADS, 15, 128 - 31), F32)], axis=2)
    return pl.pallas_call(
        body, name="na_bias_table", grid=(NA_HEADS,),
        in_specs=[pl.BlockSpec((1, 15, 128), lambda h: (h, 0, 0))],
        out_specs=pl.BlockSpec((3, 1, NA_QB, NA_KU), lambda h: (0, h, 0, 0)),
        out_shape=_sds((3, NA_HEADS, NA_QB, NA_KU), F32), compiler_params=_cp(("parallel",)))(rp)


def _na_var(g):
    return jnp.where(g == 0, 0, jnp.where(g == NA_GROUPS - 1, 2, 1))


def _na_load_window(src_ref, dst, g):
    us = jnp.clip(4 * g - 4, 0, ROWS - NA_UROWS)
    kstart = pl.multiple_of(NM + GRID_W * us, 16)
    dst[0:NA_KW, :] = src_ref[pl.ds(kstart, NA_KW), :].astype(BF16)
    dst[NA_KW:NA_KW + NM, :] = src_ref[0:NM, :].astype(BF16)
    dst[NA_KW + NM:, :] = jnp.zeros((NA_KU - NA_KW - NM, 128), BF16)
    return kstart


def _na_fwd(p_act, bias_tab):
    def body(q_ref, k_ref, v_ref, b_ref, o_ref, lse_ref, ku, vu):
        g = pl.program_id(1)
        _na_load_window(k_ref, ku, g)
        _na_load_window(v_ref, vu, g)
        qstart = pl.multiple_of(NM + NA_QB * g, 16)
        q = q_ref[pl.ds(qstart, NA_QB), :]
        lane = lax.broadcasted_iota(jnp.int32, (NA_QB, 128), 1)
        o_h, lse_h = [], []
        for h in range(2):
            hm = (lane < 64) if h == 0 else (lane >= 64)
            qm = jnp.where(hm, q, 0.0).astype(BF16)
            s = _dot(qm, ku[...], NT) * NA_SCALE + b_ref[0, h]
            m = jnp.max(s, axis=-1, keepdims=True)
            p = jnp.exp(s - m)
            l = jnp.sum(p, axis=-1, keepdims=True)
            o_h.append(_dot(p.astype(BF16), vu[...]) / l)
            lse_h.append(jnp.broadcast_to(m + jnp.log(l), (NA_QB, 128)))
        o_ref[pl.ds(qstart, NA_QB), :] = jnp.where(lane < 64, o_h[0], o_h[1]).astype(BF16)
        lse_ref[0, pl.ds(qstart, NA_QB), :] = jnp.where(lane < 64, lse_h[0], lse_h[1])

        @pl.when(g == 0)
        def _():
            qm_ = q_ref[0:NM, :]
            lane_m = lax.broadcasted_iota(jnp.int32, (NM, 128), 1)
            km, vm = ku[NA_KW:NA_KW + NM, :], vu[NA_KW:NA_KW + NM, :]
            om = []
            for h in range(2):
                hm = (lane_m < 64) if h == 0 else (lane_m >= 64)
                s = _dot(jnp.where(hm, qm_, 0.0).astype(BF16), km, NT) * NA_SCALE
                p = jnp.exp(s - jnp.max(s, axis=-1, keepdims=True))
                l = jnp.sum(p, axis=-1, keepdims=True)
                om.append(_dot(p.astype(BF16), vm) / l)
            o_ref[0:NM, :] = jnp.where(lane_m < 64, om[0], om[1]).astype(BF16)
            o_ref[L:T, :] = jnp.zeros((T - L, 128), BF16)
            lse_ref[0, 0:NM, :] = jnp.zeros((NM, 128), F32)
            lse_ref[0, L:T, :] = jnp.zeros((T - L, 128), F32)

    col = lambda off: pl.BlockSpec((T, 128), lambda hp, g: (0, off + hp))
    return pl.pallas_call(
        body, name="na_fwd", grid=(4, NA_GROUPS),
        in_specs=[col(0), col(4), col(8),
                  pl.BlockSpec((1, 2, NA_QB, NA_KU), lambda hp, g: (_na_var(g), hp, 0, 0))],
        out_specs=(pl.BlockSpec((T, 128), lambda hp, g: (0, hp)), pl.BlockSpec((1, T, 128), lambda hp, g: (hp, 0, 0))),
        out_shape=(_sds((T, 512), BF16), _sds((4, T, 128), F32)),
        scratch_shapes=[pltpu.VMEM((NA_KU, 128), BF16), pltpu.VMEM((NA_KU, 128), BF16)],
        compiler_params=_cp(("parallel", "arbitrary")))(p_act, p_act, p_act, bias_tab)


def _na_bwd(p_act, do, lse, bias_tab):
    def body(q_ref, k_ref, v_ref, do_ref, lse_ref, b_ref, dq_ref, dk_ref, dv_ref, db_ref, ku, vu):
        g = pl.program_id(1)

        @pl.when(g == 0)
        def _():
            dq_ref[...] = jnp.zeros((T, 128), F32)
            dk_ref[...] = jnp.zeros((T, 128), F32)
            dv_ref[...] = jnp.zeros((T, 128), F32)

        kstart = _na_load_window(k_ref, ku, g)
        _na_load_window(v_ref, vu, g)
        qstart = pl.multiple_of(NM + NA_QB * g, 16)
        q = q_ref[pl.ds(qstart, NA_QB), :]
        dov = do_ref[pl.ds(qstart, NA_QB), :]
        lsev = lse_ref[0, pl.ds(qstart, NA_QB), :]
        lane = lax.broadcasted_iota(jnp.int32, (NA_QB, 128), 1)
        first = (g == 0) | (g == 1) | (g == NA_GROUPS - 1)
        dq_h = []
        dku = jnp.zeros((NA_KU, 128), F32)
        dvu = jnp.zeros((NA_KU, 128), F32)
        for h in range(2):
            hm = (lane < 64) if h == 0 else (lane >= 64)
            qm = jnp.where(hm, q, 0.0).astype(BF16)
            dom = jnp.where(hm, dov, 0.0).astype(BF16)
            s = _dot(qm, ku[...], NT) * NA_SCALE + b_ref[0, h]
            p = jnp.exp(s - lsev[:, 64 * h:64 * h + 1])
            dp = _dot(dom, vu[...], NT)
            delta = jnp.sum(p * dp, axis=-1, keepdims=True)
            ds = p * (dp - delta)

            @pl.when(first)
            def _():
                db_ref[0, h] = ds

            @pl.when(jnp.logical_not(first))
            def _():
                db_ref[0, h] += ds

            dsb = (ds * NA_SCALE).astype(BF16)
            dq_h.append(_dot(dsb, ku[...]))
            dku = dku + _dot(dsb, qm, TN)
            dvu = dvu + _dot(p.astype(BF16), dom, TN)
        dq_ref[pl.ds(qstart, NA_QB), :] = jnp.where(lane < 64, dq_h[0], dq_h[1])
        dk_ref[pl.ds(kstart, NA_KW), :] += dku[0:NA_KW]
        dv_ref[pl.ds(kstart, NA_KW), :] += dvu[0:NA_KW]
        dk_ref[0:NM, :] += dku[NA_KW:NA_KW + NM]
        dv_ref[0:NM, :] += dvu[NA_KW:NA_KW + NM]

        @pl.when(g == 0)
        def _():
            qm_ = q_ref[0:NM, :]
            dom_ = do_ref[0:NM, :]
            lane_m = lax.broadcasted_iota(jnp.int32, (NM, 128), 1)
            km, vm = ku[NA_KW:NA_KW + NM, :], vu[NA_KW:NA_KW + NM, :]
            dqs = []
            dkm = jnp.zeros((NM, 128), F32)
            dvm = jnp.zeros((NM, 128), F32)
            for h in range(2):
                hm = (lane_m < 64) if h == 0 else (lane_m >= 64)
                qh = jnp.where(hm, qm_, 0.0).astype(BF16)
                doh = jnp.where(hm, dom_, 0.0).astype(BF16)
                s = _dot(qh, km, NT) * NA_SCALE
                e = jnp.exp(s - jnp.max(s, axis=-1, keepdims=True))
                p = e / jnp.sum(e, axis=-1, keepdims=True)
                dp = _dot(doh, vm, NT)
                ds = p * (dp - jnp.sum(p * dp, axis=-1, keepdims=True))
                dsb = (ds * NA_SCALE).astype(BF16)
                dqs.append(_dot(dsb, km))
                dkm = dkm + _dot(dsb, qh, TN)
                dvm = dvm + _dot(p.astype(BF16), doh, TN)
            dq_ref[0:NM, :] = jnp.where(lane_m < 64, dqs[0], dqs[1])
            dk_ref[0:NM, :] += dkm
            dv_ref[0:NM, :] += dvm

    col = lambda off: pl.BlockSpec((T, 128), lambda hp, g: (0, off + hp))
    ocol = pl.BlockSpec((T, 128), lambda hp, g: (0, hp))
    bspec = pl.BlockSpec((1, 2, NA_QB, NA_KU), lambda hp, g: (_na_var(g), hp, 0, 0))
    return pl.pallas_call(
        body, name="na_bwd", grid=(4, NA_GROUPS),
        in_specs=[col(0), col(4), col(8), ocol, pl.BlockSpec((1, T, 128), lambda hp, g: (hp, 0, 0)), bspec],
        out_specs=(ocol, ocol, ocol, bspec),
        out_shape=(_sds((T, 512), F32), _sds((T, 512), F32), _sds((T, 512), F32), _sds((3, NA_HEADS, NA_QB, NA_KU), F32)),
        scratch_shapes=[pltpu.VMEM((NA_KU, 128), BF16), pltpu.VMEM((NA_KU, 128), BF16)],
        compiler_params=_cp(("parallel", "arbitrary")))(p_act, p_act, p_act, do, lse, bias_tab)


def _na_rpb_reduce(dbias):
    def body(db_ref, o_ref):
        row = lax.broadcasted_iota(jnp.int32, (GRID_W, 128), 0)
        for a in range(15):
            acc = jnp.zeros((GRID_W, GRID_W), F32)
            for var in range(3):
                for i in range(4):
                    for j in range(NA_UROWS):
                        if _na_row_offset(var, i, j) == a:
                            pair = db_ref[var, 0, i * 64:(i + 1) * 64, (j // 2) * 128:(j // 2 + 1) * 128]
                            acc = acc + pair[:, (j % 2) * 64:(j % 2 + 1) * 64]
            z = jnp.concatenate([acc, jnp.zeros((GRID_W, 128 - GRID_W), F32)], axis=1)
            for bit in range(6):
                sh = 1 << bit
                z = jnp.where((row & sh) != 0, jnp.roll(z, 128 - sh, axis=1), z)
            z = jnp.roll(z, 15, axis=1)
            o_ref[0, a:a + 1, :] = jnp.sum(z, axis=0, keepdims=True)

    return pl.pallas_call(
        body, name="na_rpb_reduce", grid=(NA_HEADS,),
        in_specs=[pl.BlockSpec((3, 1, NA_QB, NA_KU), lambda h: (0, h, 0, 0))],
        out_specs=pl.BlockSpec((1, 15, 128), lambda h: (h, 0, 0)), out_shape=_sds((NA_HEADS, 15, 128), F32),
        compiler_params=_cp(("parallel",)))(dbias)


HG_RB = 128
HG_NB = T // HG_RB
HG_SLOTS = HG_NB * 8
HI = lax.Precision.HIGHEST
HG_UNROLL = 4


def _chunk_tri(lower):
    r = lax.broadcasted_iota(jnp.int32, (HG_RB, HG_RB), 0)
    c = lax.broadcasted_iota(jnp.int32, (HG_RB, HG_RB), 1)
    same = (r // HG_C) == (c // HG_C)
    keep = (c <= r) if lower else (c >= r)
    return jnp.where(same & keep, 1.0, 0.0).astype(F32)


def _hg_gate_terms(z, lg):
    dl = lg[0:1, :] - lg[1:2, :]
    log_lb = jax.nn.log_sigmoid(dl)
    log_1mlb = jax.nn.log_sigmoid(-dl)
    yz = log_1mlb + jax.nn.log_sigmoid(z)
    log_f = jnp.logaddexp(log_lb, yz)
    snz = jax.nn.sigmoid(-z)
    k = jnp.exp(log_1mlb) * snz
    w2 = jnp.exp(yz - log_f)
    return log_f, k, snz, w2


def _hg_pre(p_act, logits):
    def body(q_ref, zf_ref, zb_ref, lg_ref, qh_ref, kf_ref, bf_ref, kb_ref, bb_ref):
        qh_ref[...] = jax.nn.silu(q_ref[...])
        lf, kf, _, _ = _hg_gate_terms(zf_ref[...], lg_ref[0])
        kf_ref[...] = kf
        bf_ref[...] = jnp.dot(_chunk_tri(True), lf, precision=HI, preferred_element_type=F32)
        lb_, kb, _, _ = _hg_gate_terms(zb_ref[...], lg_ref[1])
        kb_ref[...] = kb
        bb_ref[...] = jnp.dot(_chunk_tri(False), lb_, precision=HI, preferred_element_type=F32)

    blk = lambda c: pl.BlockSpec((HG_RB, 512), lambda i: (i, c))
    ob = pl.BlockSpec((HG_RB, 512), lambda i: (i, 0))
    return pl.pallas_call(
        body, name="hg_pre", grid=(HG_NB,),
        in_specs=[blk(3), blk(4), blk(5), pl.BlockSpec((2, 2, 512), lambda i: (0, 0, 0))],
        out_specs=(ob,) * 5, out_shape=(_sds((T, 512), F32),) * 5,
        compiler_params=_cp(("parallel",)))(p_act, p_act, p_act, logits)


def _bdot(a, b, ca, cb):
    return lax.dot_general(a.astype(BF16), b.astype(BF16), (((ca,), (cb,)), ((0,), (0,))), preferred_element_type=F32)


HG_S = 8
HG_NS = HG_RB // HG_S


def _lane_sums(xs):
    l_io = lax.broadcasted_iota(jnp.int32, (HG_NS, HG_S, HG_S), 2)
    a = jnp.zeros((HG_NS, HG_S, HG_S), F32)
    for j, x in enumerate(xs):
        a = a + jnp.where(l_io == j, jnp.sum(x, axis=-1, keepdims=True), 0.0)
    return a


def _halves(x):
    y = x.reshape(8, 2, HG_S, x.shape[-1])
    return y[:, 0], y[:, 1]


def _join(first, second):
    return jnp.stack([first, second], axis=1).reshape(HG_RB, first.shape[-1])


def _cross_split(rev, b4):
    b_1, b_2 = _halves(b4)
    if rev:
        r = b_2[:, 0:1, :]
        return jnp.exp(b_1 - r), jnp.exp(r - b_2)
    r = b_1[:, HG_S - 1:HG_S, :]
    return jnp.exp(b_2 - r), jnp.exp(r - b_1)


def _hg_scan_fwd(qh, k, b, p_act, rev):
    anchor = 0 if rev else HG_C - 1

    def body(q_ref, k_ref, b_ref, v_ref, o_ref, st_ref, dsc):
        def phase_a(blk, _):
            rows = pl.ds(pl.multiple_of(blk * HG_RB, HG_RB), HG_RB)
            b3 = b_ref[rows, :].reshape(8, HG_C, 128)
            k3 = k_ref[rows, :].reshape(8, HG_C, 128)
            v3 = v_ref[rows, :].reshape(8, HG_C, 128)
            bl = b3[:, anchor:anchor + 1, :]
            kt = k3 * jnp.exp(bl - b3)
            st_ref[0, pl.ds(pl.multiple_of(blk * 8, 8), 8)] = _bdot(v3, kt, 1, 1)
            dsc[pl.ds(pl.multiple_of(blk * 8, 8), 8), :] = jnp.exp(bl[:, 0, :])
            return 0

        lax.fori_loop(0, HG_NB, phase_a, 0, unroll=HG_UNROLL)

        def phase_b(n, carry):
            c = (NCHUNK - 1 - n) if rev else n
            u = st_ref[0, c]
            st_ref[0, c] = carry
            return carry * dsc[pl.ds(c, 1), :] + u

        lax.fori_loop(0, NCHUNK, phase_b, jnp.zeros((128, 128), F32))
        for c in range(NCHUNK, HG_SLOTS):
            st_ref[0, c] = jnp.zeros((128, 128), F32)

        t_io = lax.broadcasted_iota(jnp.int32, (HG_NS, HG_S, 128), 1)

        def phase_c(blk, _):
            rows = pl.ds(pl.multiple_of(blk * HG_RB, HG_RB), HG_RB)
            b4 = b_ref[rows, :].reshape(HG_NS, HG_S, 128)
            k4 = k_ref[rows, :].reshape(HG_NS, HG_S, 128)
            q4 = q_ref[rows, :].reshape(HG_NS, HG_S, 128)
            v4 = v_ref[rows, :].reshape(HG_NS, HG_S, 128)
            st = st_ref[0, pl.ds(pl.multiple_of(blk * 8, 8), 8)]
            o = _bdot((q4 * jnp.exp(b4)).reshape(8, HG_C, 128), st, 2, 2).reshape(HG_RB, 128)
            terms = []
            for s in range(HG_S):
                ok = (t_io <= s) if rev else (t_io >= s)
                f = jnp.exp(jnp.where(ok, b4 - b4[:, s:s + 1, :], NEG))
                terms.append(q4 * f * k4[:, s:s + 1, :])
            o_in = _bdot(_lane_sums(terms), v4, 2, 1)
            wq, wk = _cross_split(rev, b4)
            q_1, q_2 = _halves(q4)
            k_1, k_2 = _halves(k4)
            v_1, v_2 = _halves(v4)
            o_1, o_2 = _halves(o_in)
            if rev:
                o_1 = o_1 + _bdot(_bdot(q_1 * wq, k_2 * wk, 2, 2), v_2, 2, 1)
            else:
                o_2 = o_2 + _bdot(_bdot(q_2 * wq, k_1 * wk, 2, 2), v_1, 2, 1)
            o_ref[rows, :] = o + _join(o_1, o_2)
            return 0

        lax.fori_loop(0, HG_NB, phase_c, 0, unroll=HG_UNROLL)

    col = pl.BlockSpec((T, 128), lambda h: (0, h))
    return pl.pallas_call(
        body, name="hg_scan_bwd_dir" if rev else "hg_scan_fwd_dir", grid=(HG_HEADS,),
        in_specs=[col, col, col, pl.BlockSpec((T, 128), lambda h: (0, 24 + h))],
        out_specs=(col, pl.BlockSpec((1, HG_SLOTS, 128, 128), lambda h: (h, 0, 0, 0))),
        out_shape=(_sds((T, 512), F32), _sds((HG_HEADS, HG_SLOTS, 128, 128), F32)),
        scratch_shapes=[pltpu.VMEM((HG_SLOTS, 128), F32)],
        compiler_params=_cp(("parallel",), 56))(qh, k, b, p_act)


def _hg_scan_bwd(qh, k, b, p_act, st, do, rev):
    anchor = 0 if rev else HG_C - 1

    def body(q_ref, k_ref, b_ref, v_ref, st_ref, do_ref, dq_ref, dk_ref, db_ref, dv_ref, gst, dsc, dbl):
        def phase_a(blk, _):
            rows = pl.ds(pl.multiple_of(blk * HG_RB, HG_RB), HG_RB)
            b3 = b_ref[rows, :].reshape(8, HG_C, 128)
            q3 = q_ref[rows, :].reshape(8, HG_C, 128)
            do3 = do_ref[rows, :].reshape(8, HG_C, 128)
            gst[pl.ds(pl.multiple_of(blk * 8, 8), 8)] = _bdot(do3, q3 * jnp.exp(b3), 1, 1)
            dsc[pl.ds(pl.multiple_of(blk * 8, 8), 8), :] = jnp.exp(b3[:, anchor, :])
            return 0

        lax.fori_loop(0, HG_NB, phase_a, 0, unroll=HG_UNROLL)

        def phase_b(n, carry):
            c = n if rev else (NCHUNK - 1 - n)
            w = gst[c]
            gst[c] = carry
            dcv = dsc[pl.ds(c, 1), :]
            dbl[pl.ds(c, 1), :] = dcv * jnp.sum(st_ref[0, c] * carry, axis=0, keepdims=True)
            return carry * dcv + w

        lax.fori_loop(0, NCHUNK, phase_b, jnp.zeros((128, 128), F32))
        for c in range(NCHUNK, HG_SLOTS):
            gst[c] = jnp.zeros((128, 128), F32)
            dbl[c:c + 1, :] = jnp.zeros((1, 128), F32)

        t_io = lax.broadcasted_iota(jnp.int32, (HG_NS, HG_S, 128), 1)
        t16 = lax.broadcasted_iota(jnp.int32, (8, HG_C, 128), 1)
        r_io = lax.broadcasted_iota(jnp.int32, (HG_NS, HG_S, HG_S), 1)
        l_io = lax.broadcasted_iota(jnp.int32, (HG_NS, HG_S, HG_S), 2)

        def phase_c(blk, _):
            rows = pl.ds(pl.multiple_of(blk * HG_RB, HG_RB), HG_RB)
            cs = pl.ds(pl.multiple_of(blk * 8, 8), 8)
            b4 = b_ref[rows, :].reshape(HG_NS, HG_S, 128)
            k4 = k_ref[rows, :].reshape(HG_NS, HG_S, 128)
            q4 = q_ref[rows, :].reshape(HG_NS, HG_S, 128)
            v4 = v_ref[rows, :].reshape(HG_NS, HG_S, 128)
            do4 = do_ref[rows, :].reshape(HG_NS, HG_S, 128)
            b3, k3, q3 = (z.reshape(8, HG_C, 128) for z in (b4, k4, q4))
            v3, do3 = v4.reshape(8, HG_C, 128), do4.reshape(8, HG_C, 128)
            s_t = st_ref[0, cs]
            g_t = gst[cs]
            bl = b3[:, anchor:anchor + 1, :]
            ekl = jnp.exp(bl - b3)
            kt = k3 * ekl
            dkt = _bdot(v3, g_t, 2, 1)
            dq = (_bdot(do3, s_t, 2, 1) * jnp.exp(b3)).reshape(HG_NS, HG_S, 128)
            dk = (dkt * ekl).reshape(HG_NS, HG_S, 128)
            dv = _bdot(kt, g_t, 2, 2).reshape(HG_NS, HG_S, 128)
            dbl3 = dbl[cs, :].reshape(8, 1, 128) + jnp.sum(dkt * kt, axis=1, keepdims=True)
            causal = (l_io >= r_io) if rev else (l_io <= r_io)
            da = jnp.where(causal, _bdot(do4, v4, 2, 2), 0.0)
            causal_t = (l_io <= r_io) if rev else (l_io >= r_io)
            dat = jnp.where(causal_t, _bdot(v4, do4, 2, 2), 0.0)
            for s in range(HG_S):
                ok = (t_io <= s) if rev else (t_io >= s)
                f = jnp.exp(jnp.where(ok, b4 - b4[:, s:s + 1, :], NEG))
                dq = dq + da[:, :, s:s + 1] * (f * k4[:, s:s + 1, :])
            terms = []
            for t in range(HG_S):
                ok = (t_io >= t) if rev else (t_io <= t)
                e = jnp.exp(jnp.where(ok, b4[:, t:t + 1, :] - b4, NEG))
                eq = e * q4[:, t:t + 1, :]
                dk = dk + dat[:, :, t:t + 1] * eq
                terms.append(eq * k4)
            dv = dv + _bdot(_lane_sums(terms), do4, 2, 1)
            wq, wk = _cross_split(rev, b4)
            pick = (lambda z: _halves(z)) if rev else (lambda z: _halves(z)[::-1])
            (q_q, _), (_, k_k), (_, v_k), (do_q, _) = pick(q4), pick(k4), pick(v4), pick(do4)
            qx, kx = q_q * wq, k_k * wk
            dq_q = _bdot(_bdot(do_q, v_k, 2, 2), kx, 2, 1) * wq
            dk_k = _bdot(_bdot(v_k, do_q, 2, 2), qx, 2, 1) * wk
            dv_k = _bdot(_bdot(kx, qx, 2, 2), do_q, 2, 1)
            zero = jnp.zeros((8, HG_S, 128), F32)
            place_q = (lambda z: _join(z, zero)) if rev else (lambda z: _join(zero, z))
            place_k = (lambda z: _join(zero, z)) if rev else (lambda z: _join(z, zero))
            dq2 = dq.reshape(HG_RB, 128) + place_q(dq_q)
            dk2 = dk.reshape(HG_RB, 128) + place_k(dk_k)
            dv2 = dv.reshape(HG_RB, 128) + place_k(dv_k)
            dq3, dk3 = dq2.reshape(8, HG_C, 128), dk2.reshape(8, HG_C, 128)
            db = q3 * dq3 - k3 * dk3 + jnp.where(t16 == anchor, dbl3, 0.0)
            dq_ref[rows, :] = dq2
            dk_ref[rows, :] = dk2
            db_ref[rows, :] = db.reshape(HG_RB, 128)
            dv_ref[rows, :] = dv2
            return 0

        lax.fori_loop(0, HG_NB, phase_c, 0, unroll=HG_UNROLL)

    col = pl.BlockSpec((T, 128), lambda h: (0, h))
    return pl.pallas_call(
        body, name="hg_scan_bwd_dir_bwd" if rev else "hg_scan_fwd_dir_bwd", grid=(HG_HEADS,),
        in_specs=[col, col, col, pl.BlockSpec((T, 128), lambda h: (0, 24 + h)),
                  pl.BlockSpec((1, HG_SLOTS, 128, 128), lambda h: (h, 0, 0, 0)), col],
        out_specs=(col,) * 4, out_shape=(_sds((T, 512), F32),) * 4,
        scratch_shapes=[pltpu.VMEM((HG_SLOTS, 128, 128), F32), pltpu.VMEM((HG_SLOTS, 128), F32),
                        pltpu.VMEM((HG_SLOTS, 128), F32)],
        compiler_params=_cp(("parallel",), 56))(qh, k, b, p_act, st, do)


def _row_valid(i, tm):
    r = lax.broadcasted_iota(jnp.int32, (tm, 1), 0) + i * tm
    return r < L


def _hg_post(o_f, o_b, p_act, gain):
    def body(of_ref, ob_ref, g_ref, gain_ref, u_ref):
        o = of_ref[...] + ob_ref[...]
        sg = jax.nn.silu(g_ref[...])
        parts = []
        for h in range(HG_HEADS):
            oh = o[:, 128 * h:128 * (h + 1)]
            parts.append(oh * lax.rsqrt(jnp.mean(oh * oh, axis=-1, keepdims=True) + EPS))
        n = jnp.concatenate(parts, axis=1)
        u = n * gain_ref[...] * sg
        u_ref[...] = jnp.where(_row_valid(pl.program_id(0), TM_E), u, 0.0).astype(BF16)

    blk = pl.BlockSpec((TM_E, 512), lambda i: (i, 0))
    return pl.pallas_call(
        body, name="hg_post", grid=(T // TM_E,),
        in_specs=[blk, blk, pl.BlockSpec((TM_E, 512), lambda i: (i, 7)), pl.BlockSpec((1, 512), lambda i: (0, 0))],
        out_specs=blk, out_shape=_sds((T, 512), BF16), compiler_params=_cp(("parallel",)))(o_f, o_b, p_act, gain)


def _hg_post_bwd(du, o_f, o_b, p_act, gain):
    def body(du_ref, of_ref, ob_ref, g_ref, gain_ref, do_ref, dg_ref, dgain_ref):
        i = pl.program_id(0)
        valid = _row_valid(i, TM_E)
        duv = jnp.where(valid, du_ref[...], 0.0)
        o = of_ref[...] + ob_ref[...]
        gv = g_ref[...]
        sig = jax.nn.sigmoid(gv)
        sg = gv * sig
        gain_v = gain_ref[...]
        dn = duv * gain_v * sg
        do_parts, n_parts = [], []
        for h in range(HG_HEADS):
            sl = slice(128 * h, 128 * (h + 1))
            oh = o[:, sl]
            r = lax.rsqrt(jnp.mean(oh * oh, axis=-1, keepdims=True) + EPS)
            nh = oh * r
            dnh = dn[:, sl]
            do_parts.append(r * (dnh - nh * jnp.mean(dnh * nh, axis=-1, keepdims=True)))
            n_parts.append(nh)
        n = jnp.where(valid, jnp.concatenate(n_parts, axis=1), 0.0)
        do_ref[...] = jnp.where(valid, jnp.concatenate(do_parts, axis=1), 0.0)
        dg_ref[...] = (duv * n * gain_v * (sig * (1.0 + gv * (1.0 - sig)))).astype(BF16)
        part = jnp.sum(duv * n * sg, axis=0, keepdims=True)

        @pl.when(i == 0)
        def _():
            dgain_ref[...] = part

        @pl.when(i > 0)
        def _():
            dgain_ref[...] += part

    blk = pl.BlockSpec((TM_E, 512), lambda i: (i, 0))
    vec = pl.BlockSpec((1, 512), lambda i: (0, 0))
    return pl.pallas_call(
        body, name="hg_post_bwd", grid=(T // TM_E,),
        in_specs=[blk, blk, blk, pl.BlockSpec((TM_E, 512), lambda i: (i, 7)), vec],
        out_specs=(blk, blk, vec), out_shape=(_sds((T, 512), F32), _sds((T, 512), BF16), _sds((1, 512), F32)),
        compiler_params=_cp(("arbitrary",)))(du, o_f, o_b, p_act, gain)


def _hg_pre_bwd(p_act, logits, dq_f, dq_b, dk_f, dk_b, db_f, db_b, dv_f, dv_b):
    def body(q_ref, zf_ref, zb_ref, lg_ref, dqf_ref, dqb_ref, dkf_ref, dkb_ref, dbf_ref, dbb_ref, dvf_ref, dvb_ref,
             dq_ref, dzf_ref, dzb_ref, di_ref, dlg_ref):
        i = pl.program_id(0)
        valid = _row_valid(i, HG_RB)
        qv = q_ref[...]
        sig = jax.nn.sigmoid(qv)
        dq_ref[...] = jnp.where(valid, (dqf_ref[...] + dqb_ref[...]) * (sig * (1.0 + qv * (1.0 - sig))), 0.0).astype(BF16)
        di_ref[...] = jnp.where(valid, dvf_ref[...] + dvb_ref[...], 0.0).astype(BF16)
        for d, (z_ref, dk_r, db_r, dz_ref) in enumerate(((zf_ref, dkf_ref, dbf_ref, dzf_ref), (zb_ref, dkb_ref, dbb_ref, dzb_ref))):
            lg = lg_ref[d]
            dl = lg[0:1, :] - lg[1:2, :]
            lb = jax.nn.sigmoid(dl)
            one_m_lb = jax.nn.sigmoid(-dl)
            log_f, _, snz, w2 = _hg_gate_terms(z_ref[...], lg)
            dbv = jnp.where(valid, db_r[...], 0.0)
            dkv = jnp.where(valid, dk_r[...], 0.0)
            dlf = jnp.dot(_chunk_tri(d == 1), dbv, precision=HI, preferred_element_type=F32)
            sz = 1.0 - snz
            dz_ref[...] = (dlf * w2 * snz - dkv * one_m_lb * sz * snz).astype(BF16)
            dlb = jnp.sum(dlf * snz * jnp.exp(-log_f) - dkv * snz, axis=0, keepdims=True)
            dl0 = dlb * lb * one_m_lb
            part = jnp.concatenate([dl0, -dl0], axis=0)

            @pl.when(i == 0)
            def _():
                dlg_ref[d] = part

            @pl.when(i > 0)
            def _():
                dlg_ref[d] += part

    blk = lambda c: pl.BlockSpec((HG_RB, 512), lambda i: (i, c))
    ob = pl.BlockSpec((HG_RB, 512), lambda i: (i, 0))
    lgs = pl.BlockSpec((2, 2, 512), lambda i: (0, 0, 0))
    return pl.pallas_call(
        body, name="hg_pre_bwd", grid=(HG_NB,),
        in_specs=[blk(3), blk(4), blk(5), lgs] + [ob] * 8,
        out_specs=(ob, ob, ob, ob, lgs),
        out_shape=(_sds((T, 512), BF16),) * 4 + (_sds((2, 2, 512), F32),),
        compiler_params=_cp(("arbitrary",)))(p_act, p_act, p_act, logits, dq_f, dq_b, dk_f, dk_b, db_f, db_b, dv_f, dv_b)


def _mix_fwd(o_na, u_hg, wna_g, whg_g, p_act):
    def body(ona_ref, uhg_ref, wna_ref, whg_ref, gna_ref, ghg_ref, o_ref):
        y_na = _dot(ona_ref[...], wna_ref[0])
        y_hg = _dot(uhg_ref[...], whg_ref[0])
        o_ref[...] = (jax.nn.sigmoid(gna_ref[...]) * y_na + jax.nn.sigmoid(ghg_ref[...]) * y_hg).astype(BF16)

    act = pl.BlockSpec((TM_MM, 512), lambda i, j: (i, 0))
    wsp = pl.BlockSpec((1, 512, 128), lambda i, j: (j, 0, 0))
    return pl.pallas_call(
        body, name="mix_fwd", grid=(T // TM_MM, NDEV),
        in_specs=[act, act, wsp, wsp, pl.BlockSpec((TM_MM, 128), lambda i, j: (i, 32 + j)),
                  pl.BlockSpec((TM_MM, 128), lambda i, j: (i, 40 + j))],
        out_specs=pl.BlockSpec((TM_MM, 128), lambda i, j: (i, j)), out_shape=_sds((T, D), BF16),
        compiler_params=_cp(("parallel", "parallel")))(o_na, u_hg, wna_g, whg_g, p_act, p_act)


def _mix_bwd(o_na, u_hg, wna_g, whg_g, p_act, dmix):
    ni = T // TM_B

    def body(ona_ref, uhg_ref, wna_ref, whg_ref, gna_ref, ghg_ref, dmix_ref,
             dgna_ref, dghg_ref, dwna_ref, dwhg_ref, dona_ref, duhg_ref, acc_na, acc_hg):
        j, i = pl.program_id(0), pl.program_id(1)
        rows = pl.ds(pl.multiple_of(i * TM_B, TM_B), TM_B)
        dm = dmix_ref[...].astype(F32)
        for x_ref, w_ref, g_ref, dg_ref, dx_ref, dw_ref, acc in (
                (ona_ref, wna_ref, gna_ref, dgna_ref, dona_ref, dwna_ref, acc_na),
                (uhg_ref, whg_ref, ghg_ref, dghg_ref, duhg_ref, dwhg_ref, acc_hg)):
            xv = x_ref[...]
            y = _dot(xv, w_ref[0])
            sg = jax.nn.sigmoid(g_ref[...])
            dg_ref[...] = (dm * y * sg * (1.0 - sg)).astype(BF16)
            dy = (dm * sg).astype(BF16)
            part = _dot(xv, dy, TN)
            dxv = _dot(dy, w_ref[0], NT)

            @pl.when(i == 0)
            def _():
                acc[...] = part

            @pl.when(i > 0)
            def _():
                acc[...] += part

            @pl.when(i == ni - 1)
            def _():
                dw_ref[0] = acc[...].astype(BF16)

            @pl.when(j == 0)
            def _():
                dx_ref[rows, :] = dxv

            @pl.when(j > 0)
            def _():
                dx_ref[rows, :] += dxv

    act = pl.BlockSpec((TM_B, 512), lambda j, i: (i, 0))
    wsp = pl.BlockSpec((1, 512, 128), lambda j, i: (j, 0, 0))
    cblk = pl.BlockSpec((TM_B, 128), lambda j, i: (i, j))
    full = pl.BlockSpec((T, 512), lambda j, i: (0, 0))
    return pl.pallas_call(
        body, name="mix_bwd", grid=(NDEV, ni),
        in_specs=[act, act, wsp, wsp, pl.BlockSpec((TM_B, 128), lambda j, i: (i, 32 + j)),
                  pl.BlockSpec((TM_B, 128), lambda j, i: (i, 40 + j)), cblk],
        out_specs=(cblk, cblk, wsp, wsp, full, full),
        out_shape=(_sds((T, D), BF16), _sds((T, D), BF16), _sds((NDEV, 512, 128), BF16), _sds((NDEV, 512, 128), BF16),
                   _sds((T, 512), F32), _sds((T, 512), F32)),
        scratch_shapes=[pltpu.VMEM((512, 128), F32), pltpu.VMEM((512, 128), F32)],
        compiler_params=_cp(("arbitrary", "arbitrary")))(o_na, u_hg, wna_g, whg_g, p_act, p_act, dmix)


def _wo_fwd(mix, w_o, h0, g_mlp):
    def body(mix_ref, w_ref, h0_ref, g_ref, h1_ref, m_ref):
        h1 = h0_ref[...] + _dot(mix_ref[...], w_ref[...])
        h1_ref[...] = h1
        r = lax.rsqrt(jnp.mean(h1 * h1, axis=-1, keepdims=True) + EPS)
        m_ref[...] = (h1 * r * g_ref[...]).astype(BF16)

    blk = pl.BlockSpec((TM_B, D), lambda i: (i, 0))
    return pl.pallas_call(
        body, name="wo_fwd", grid=(T // TM_B,),
        in_specs=[blk, pl.BlockSpec((D, D), lambda i: (0, 0)), blk, pl.BlockSpec((1, D), lambda i: (0, 0))],
        out_specs=(blk, blk), out_shape=(_sds((T, D), F32), _sds((T, D), BF16)),
        compiler_params=_cp(("parallel",)))(mix, w_o, h0, g_mlp)


def _wo_bwd(dh1_b, w_o, mix):
    ni = T // TM_B

    def body(dh_ref, w_ref, mix_ref, dmix_ref, dw_ref, acc):
        i = pl.program_id(0)
        dh = dh_ref[...]
        dmix_ref[...] = _dot(dh, w_ref[...], NT).astype(BF16)
        part = _dot(mix_ref[...], dh, TN)

        @pl.when(i == 0)
        def _():
            acc[...] = part

        @pl.when(i > 0)
        def _():
            acc[...] += part

        @pl.when(i == ni - 1)
        def _():
            dw_ref[...] = acc[...].astype(BF16)

    blk = pl.BlockSpec((TM_B, D), lambda i: (i, 0))
    wsp = pl.BlockSpec((D, D), lambda i: (0, 0))
    return pl.pallas_call(
        body, name="wo_bwd", grid=(ni,), in_specs=[blk, wsp, blk], out_specs=(blk, wsp),
        out_shape=(_sds((T, D), BF16), _sds((D, D), BF16)), scratch_shapes=[pltpu.VMEM((D, D), F32)],
        compiler_params=_cp(("arbitrary",)))(dh1_b, w_o, mix)


FF_B = D_FF // NDEV


def _mlp_fwd(m, wup_g, wdown_g, h1):
    def body(m_ref, wu_ref, wd_ref, h1_ref, h2_ref):
        j = pl.program_id(1)
        up = jnp.maximum(_dot(m_ref[...], wu_ref[0]), 0.0)
        part = _dot((up * up).astype(BF16), wd_ref[0])

        @pl.when(j == 0)
        def _():
            h2_ref[...] = h1_ref[...] + part

        @pl.when(j > 0)
        def _():
            h2_ref[...] += part

    blk = pl.BlockSpec((TM_MM, D), lambda i, j: (i, 0))
    return pl.pallas_call(
        body, name="mlp_fwd", grid=(T // TM_MM, NDEV),
        in_specs=[blk, pl.BlockSpec((1, D, FF_B), lambda i, j: (j, 0, 0)), pl.BlockSpec((1, FF_B, D), lambda i, j: (j, 0, 0)), blk],
        out_specs=blk, out_shape=_sds((T, D), F32),
        compiler_params=_cp(("parallel", "arbitrary")))(m, wup_g, wdown_g, h1)


def _mlp_bwd(m, dh2_b, wup_g, wdown_g):
    ni = T // TM_B

    def body(m_ref, dh_ref, wu_ref, wd_ref, dwu_ref, dwd_ref, dm_ref, acc_u, acc_d):
        j, i = pl.program_id(0), pl.program_id(1)
        rows = pl.ds(pl.multiple_of(i * TM_B, TM_B), TM_B)
        mv, dh = m_ref[...], dh_ref[...]
        r = jnp.maximum(_dot(mv, wu_ref[0]), 0.0)
        act = (r * r).astype(BF16)
        dact = _dot(dh, wd_ref[0], NT)
        dup = (dact * (2.0 * r)).astype(BF16)
        pd = _dot(act, dh, TN)
        pu = _dot(mv, dup, TN)
        dmv = _dot(dup, wu_ref[0], NT)

        @pl.when(i == 0)
        def _():
            acc_u[...] = pu
            acc_d[...] = pd

        @pl.when(i > 0)
        def _():
            acc_u[...] += pu
            acc_d[...] += pd

        @pl.when(i == ni - 1)
        def _():
            dwu_ref[0] = acc_u[...].astype(BF16)
            dwd_ref[0] = acc_d[...].astype(BF16)

        @pl.when(j == 0)
        def _():
            dm_ref[rows, :] = dmv

        @pl.when(j > 0)
        def _():
            dm_ref[rows, :] += dmv

    blk = pl.BlockSpec((TM_B, D), lambda j, i: (i, 0))
    wus = pl.BlockSpec((1, D, FF_B), lambda j, i: (j, 0, 0))
    wds = pl.BlockSpec((1, FF_B, D), lambda j, i: (j, 0, 0))
    return pl.pallas_call(
        body, name="mlp_bwd", grid=(NDEV, ni), in_specs=[blk, blk, wus, wds],
        out_specs=(wus, wds, pl.BlockSpec((T, D), lambda j, i: (0, 0))),
        out_shape=(_sds((NDEV, D, FF_B), BF16), _sds((NDEV, FF_B, D), BF16), _sds((T, D), F32)),
        scratch_shapes=[pltpu.VMEM((D, FF_B), F32), pltpu.VMEM((FF_B, D), F32)],
        compiler_params=_cp(("arbitrary", "arbitrary")))(m, dh2_b, wup_g, wdown_g)


def _loss_head(h2, g_final, tgt):
    def body(h_ref, g_ref, t_ref, loss_ref, dh_ref, dhb_ref, dg_ref):
        i = pl.program_id(0)
        r_io = lax.broadcasted_iota(jnp.int32, (TM_E, 1), 0) + i * TM_E
        valid = (r_io >= NM) & (r_io < L)
        xv = h_ref[...]
        r = lax.rsqrt(jnp.mean(xv * xv, axis=-1, keepdims=True) + EPS)
        xh = xv * r
        gv = g_ref[...]
        err = jnp.where(valid, xh * gv - t_ref[...], 0.0)
        lpart = jnp.broadcast_to(0.5 * jnp.sum(jnp.sum(err * err, axis=-1, keepdims=True) * (1.0 / D), axis=0, keepdims=True), (1, 128))
        dy = err * (1.0 / D)
        dxh = dy * gv
        dh = r * (dxh - xh * jnp.mean(dxh * xh, axis=-1, keepdims=True))
        dh_ref[...] = dh
        dhb_ref[...] = dh.astype(BF16)
        gpart = jnp.sum(dy * xh, axis=0, keepdims=True)

        @pl.when(i == 0)
        def _():
            loss_ref[...] = lpart
            dg_ref[...] = gpart

        @pl.when(i > 0)
        def _():
            loss_ref[...] += lpart
            dg_ref[...] += gpart

    blk = pl.BlockSpec((TM_E, D), lambda i: (i, 0))
    vec = pl.BlockSpec((1, D), lambda i: (0, 0))
    return pl.pallas_call(
        body, name="loss_head", grid=(T // TM_E,), in_specs=[blk, vec, blk],
        out_specs=(pl.BlockSpec((1, 128), lambda i: (0, 0)), blk, blk, vec),
        out_shape=(_sds((1, 128), F32), _sds((T, D), F32), _sds((T, D), BF16), _sds((1, D), F32)),
        compiler_params=_cp(("arbitrary",)))(h2, g_final, tgt)


def _adamw(parts, w, m, v, name):
    rr, cc = w.shape
    tr = rr
    for cand in (256, 128, 64):
        if rr % cand == 0 and rr > cand:
            tr = cand
            break
    c1 = 1.0 - ADAM_B1 ** ADAM_STEP
    c2 = 1.0 - ADAM_B2 ** ADAM_STEP

    def body(p_ref, w_ref, m_ref, v_ref, g_ref, d_ref, nm_ref, nv_ref):
        g = p_ref[0].astype(F32)
        for s in range(1, NDEV):
            g = g + p_ref[s].astype(F32)
        mn = ADAM_B1 * m_ref[...] + (1.0 - ADAM_B1) * g
        vn = ADAM_B2 * v_ref[...] + (1.0 - ADAM_B2) * (g * g)
        g_ref[...] = g
        nm_ref[...] = mn
        nv_ref[...] = vn
        d_ref[...] = -ADAM_LR * ((mn / c1) / (jnp.sqrt(vn / c2) + ADAM_EPS) + ADAM_WD * w_ref[...])

    blk = pl.BlockSpec((tr, cc), lambda i: (i, 0))
    return pl.pallas_call(
        body, name=name, grid=(rr // tr,),
        in_specs=[pl.BlockSpec((NDEV, tr, cc), lambda i: (0, i, 0)), blk, blk, blk],
        out_specs=(blk,) * 4, out_shape=(_sds((rr, cc), F32),) * 4,
        compiler_params=_cp(("parallel",)))(parts, w, m, v)


RPB_N = NA_HEADS * 15 * 31
RPB_PAD = 4096
OWN_ROWS = NM + 8


def _pad_rows(a, rows):
    return jnp.pad(a, ((0, rows - a.shape[0]),) + ((0, 0),) * (a.ndim - 1))


def _pack_owned(meta_blk, lb_blk):
    return jnp.concatenate([meta_blk, _pad_rows(lb_blk.reshape(2, 128), 8)], axis=0)


def _pack_replicated(n_mix, n_mlp, n_final, hg_gain, rpb):
    flat = _pad_rows(rpb.reshape(RPB_N), RPB_PAD)
    return jnp.concatenate([n_mix.reshape(8, 128), n_mlp.reshape(8, 128), n_final.reshape(8, 128),
                            _pad_rows(hg_gain.reshape(4, 128), 8), flat.reshape(32, 128)], axis=0)


def _unpack_replicated(a):
    return (a[0:8].reshape(1, D), a[8:16].reshape(1, D), a[16:24].reshape(D), a[24:28].reshape(1, 512),
            a[32:64].reshape(RPB_PAD)[:RPB_N].reshape(1, NA_HEADS, 15, 31))


def kernel(x, meta_tokens, w_in, w_na_out, w_hg_out, w_o, w_up, w_down, norm_mix, norm_mlp, norm_final, hg_norm, na_rpb, hg_lb_logits, loss_target, m_meta_tokens, m_w_in, m_w_na_out, m_w_hg_out, m_w_o, m_w_up, m_w_down, m_norm_mix, m_norm_mlp, m_norm_final, m_hg_norm, m_na_rpb, m_hg_lb_logits, v_meta_tokens, v_w_in, v_w_na_out, v_w_hg_out, v_w_o, v_w_up, v_w_down, v_norm_mix, v_norm_mlp, v_norm_final, v_hg_norm, v_na_rpb, v_hg_lb_logits):
    owned = _pack_owned(meta_tokens, hg_lb_logits)
    win_g, owned_g = _gather_two_level([w_in[0].astype(BF16), owned], "gather_first")
    later = [w[0].astype(BF16) for w in (w_na_out, w_hg_out, w_o, w_up, w_down)]
    later[0] = _tie(later[0], owned_g, "tie_gather_rest")
    gather_rest, tok = _exchange_start(later, [False] * 5, "gather_rest_start")
    win_g = _tie(win_g, tok, "tie_inproj")
    meta_full = jnp.transpose(owned_g[:, 0:NM, :], (1, 0, 2)).reshape(NM, D)
    logits = jnp.transpose(owned_g[:, NM:NM + 2, :].reshape(NDEV, 2, 2, 64), (1, 2, 0, 3)).reshape(2, 2, 512)

    h0 = jnp.concatenate([meta_full, x[0], jnp.zeros((T - L, D), F32)], axis=0)
    tgt = jnp.concatenate([jnp.zeros((NM, D), F32), loss_target[0], jnp.zeros((T - L, D), F32)], axis=0)
    bias_tab = _na_bias_table(na_rpb[0])

    a = _norm_fwd(h0, norm_mix, "norm_mix_fwd")
    p_act = _inproj_fwd(a, win_g)
    o_na, lse = _na_fwd(p_act, bias_tab)
    qh, k_f, b_f, k_b, b_b = _hg_pre(p_act, logits)
    o_f, st_f = _hg_scan_fwd(qh, k_f, b_f, p_act, False)
    o_b, st_b = _hg_scan_fwd(qh, k_b, b_b, p_act, True)
    u_hg = _hg_post(o_f, o_b, p_act, hg_norm)
    wna_g, whg_g, wo_g, wup_g, wdown_g = _exchange_wait(gather_rest, [False] * 5, [u_hg, o_na], "gather_rest_wait")
    w_o_full = wo_g.reshape(D, D)
    mix = _mix_fwd(o_na, u_hg, wna_g, whg_g, p_act)
    h1, m_act = _wo_fwd(mix, w_o_full, h0, norm_mlp)
    h2 = _mlp_fwd(m_act, wup_g, wdown_g, h1)
    loss_part, dh2, dh2_b, d_nfinal = _loss_head(h2, norm_final.reshape(1, D), tgt)

    dwup_p, dwdown_p, dm = _mlp_bwd(m_act, dh2_b, wup_g, wdown_g)
    sc_mlp, tok = _exchange_start([dwup_p, dwdown_p], [True] * 2, "scatter_mlp_start")
    dh1, dh1_b, d_nmlp = _norm_bwd(h1, norm_mlp, _tie(dm, tok, "tie_norm_mlp_bwd"), dh2, "norm_mlp_bwd")
    dmix, dwo = _wo_bwd(dh1_b, w_o_full, mix)
    sc_wo, tok = _exchange_start([dwo.reshape(NDEV, D // NDEV, D)], [True], "scatter_wo_start")
    dgna, dghg, dwna_p, dwhg_p, do_na, du_hg = _mix_bwd(o_na, u_hg, wna_g, whg_g, p_act, _tie(dmix, tok, "tie_mix_bwd"))
    sc_br, tok = _exchange_start([dwna_p, dwhg_p], [True] * 2, "scatter_branch_start")
    du_hg = _tie(du_hg, tok, "tie_hg_post_bwd")
    do_hg, dg_hg, d_gain = _hg_post_bwd(du_hg, o_f, o_b, p_act, hg_norm)
    dq_f, dk_f, db_f, dv_f = _hg_scan_bwd(qh, k_f, b_f, p_act, st_f, do_hg, False)
    dq_b, dk_b, db_b, dv_b = _hg_scan_bwd(qh, k_b, b_b, p_act, st_b, do_hg, True)
    dq_hg, dz_f, dz_b, di_hg, d_logits = _hg_pre_bwd(p_act, logits, dq_f, dq_b, dk_f, dk_b, db_f, db_b, dv_f, dv_b)
    dq_na, dk_na, dv_na, dbias = _na_bwd(p_act, do_na, lse, bias_tab)
    dp = jnp.concatenate([dq_na.astype(BF16), dk_na.astype(BF16), dv_na.astype(BF16), dq_hg, dz_f, dz_b, di_hg, dg_hg,
                          dgna, dghg], axis=1)
    dwin_p = _inproj_bwd_dw(a, dp)
    sc_in, tok = _exchange_start([dwin_p], [True], "scatter_in_start")
    da = _inproj_bwd_da(_tie(dp, tok, "tie_inproj_bwd_da"), win_g)
    dh0, _, d_nmix = _norm_bwd(h0, norm_mix, da, dh1, "norm_mix_bwd")
    d_rpb = _na_rpb_reduce(_tie(dbias, tok, "tie_rpb_reduce"))[:, :, :31]

    res = {}

    def update(nm, parts, w, mm, vv):
        res[nm] = [r[None] for r in _adamw(parts, w[0], mm[0], vv[0], "adamw_" + nm)]
        return res[nm][1]

    wup_r, wdown_r = _exchange_wait(sc_mlp, [True] * 2, [dh0, d_rpb], "scatter_mlp_wait")
    update("w_up", wup_r, w_up, m_w_up, v_w_up)
    last = update("w_down", wdown_r, w_down, m_w_down, v_w_down)
    (wo_r,) = _exchange_wait(sc_wo, [True], [last], "scatter_wo_wait")
    last = update("w_o", wo_r, w_o, m_w_o, v_w_o)
    wna_r, whg_r = _exchange_wait(sc_br, [True] * 2, [last], "scatter_branch_wait")
    update("w_na_out", wna_r, w_na_out, m_w_na_out, v_w_na_out)
    last = update("w_hg_out", whg_r, w_hg_out, m_w_hg_out, v_w_hg_out)

    d_meta = jnp.transpose(dh0[0:NM].reshape(NM, NDEV, 128), (1, 0, 2))
    d_lg = jnp.transpose(d_logits.reshape(2, 2, NDEV, 64), (2, 0, 1, 3)).reshape(NDEV, 2, 128)
    owned_p = jnp.concatenate([d_meta, jnp.pad(d_lg, ((0, 0), (0, OWN_ROWS - NM - 2), (0, 0)))], axis=1)
    repl_p = _pack_replicated(d_nmix, d_nmlp, d_nfinal, d_gain, d_rpb)
    owned_r, repl_r = _exchange([_tie(owned_p, last, "tie_scatter_small"), repl_p], [True, False], "scatter_small")
    own = _adamw(owned_r, owned, _pack_owned(m_meta_tokens, m_hg_lb_logits), _pack_owned(v_meta_tokens, v_hg_lb_logits),
                 "adamw_owned_small")
    res["meta_tokens"] = [r[0:NM] for r in own]
    res["hg_lb_logits"] = [r[NM:NM + 2].reshape(2, 2, 64) for r in own]
    rep = _adamw(repl_r, _pack_replicated(norm_mix, norm_mlp, norm_final, hg_norm, na_rpb),
                 _pack_replicated(m_norm_mix, m_norm_mlp, m_norm_final, m_hg_norm, m_na_rpb),
                 _pack_replicated(v_norm_mix, v_norm_mlp, v_norm_final, v_hg_norm, v_na_rpb), "adamw_replicated")
    for q in range(4):
        um = _unpack_replicated(rep[q])
        for nm, val in zip(("norm_mix", "norm_mlp", "norm_final", "hg_norm", "na_rpb"), um):
            res.setdefault(nm, [None] * 4)[q] = val
    (win_r,) = _exchange_wait(sc_in, [True], [rep[1], own[1]], "scatter_in_wait")
    update("w_in", win_r, w_in, m_w_in, v_w_in)

    loss = lax.psum(loss_part[0, 0], ("x", "y", "c"))
    grad_x = dh0[NM:L][None]
    order = ("meta_tokens", "w_in", "w_na_out", "w_hg_out", "w_o", "w_up", "w_down", "norm_mix", "norm_mlp", "norm_final",
             "hg_norm", "na_rpb", "hg_lb_logits")
    outs = [loss, grad_x]
    for q in range(4):
        outs += [res[nm][q] for nm in order]
    return tuple(outs)
```

```python
import functools

import numpy as np
import jax
import jax.numpy as jnp
from jax import lax
from jax.experimental import pallas as pl
from jax.experimental.pallas import tpu as pltpu

F32 = jnp.float32
BF16 = jnp.bfloat16

D = 1024
SEQ = 2048
NM = 16
L = SEQ + NM
T = 2176
NDEV = 8
EPS = 1e-6
GRID_W = 64
ROWS = SEQ // GRID_W
NA_HEADS = 8
NA_DH = 64
NA_SCALE = NA_DH ** -0.5
HG_HEADS = 4
HG_C = 16
NCHUNK = L // HG_C
D_FF = 4096
IN_COLS = 6144
NEG = -1e30

ADAM_LR = 0.001
ADAM_B1 = 0.9
ADAM_B2 = 0.999
ADAM_EPS = 1e-08
ADAM_WD = 0.01
ADAM_STEP = 10

MESH_ID = pl.DeviceIdType.MESH
ANY = pl.BlockSpec(memory_space=pl.ANY)

NN = (((1,), (0,)), ((), ()))
NT = (((1,), (1,)), ((), ()))
TN = (((0,), (0,)), ((), ()))


def _cp(sem=None, vmem_mb=48):
    return pltpu.CompilerParams(dimension_semantics=sem, vmem_limit_bytes=vmem_mb * 1024 * 1024)


def _dot(a, b, dims=NN):
    return lax.dot_general(a, b, dims, preferred_element_type=F32)


def _sds(shape, dtype):
    return jax.ShapeDtypeStruct(shape, dtype)


HBM = pl.BlockSpec(memory_space=pltpu.HBM)
SEM = pl.BlockSpec(memory_space=pltpu.SEMAPHORE)
EFFECT = pltpu.SideEffectType.DATAFLOW_SIDE_EFFECTING


def _exchange(arrs, scatter, name):
    n = len(arrs)
    out_shapes = []
    for a, sc in zip(arrs, scatter):
        out_shapes.append(_sds(a.shape if sc else (NDEV,) + a.shape, a.dtype))

    def body(*refs):
        ins, outs = refs[:n], refs[n:2 * n]
        send_sems, recv_sems, loc_sems = refs[2 * n:]
        me = 4 * lax.axis_index("x") + 2 * lax.axis_index("y") + lax.axis_index("c")
        copies = []
        for k in range(n):
            src_me = ins[k].at[me] if scatter[k] else ins[k]
            loc = pltpu.make_async_copy(src_me, outs[k].at[me], loc_sems.at[k])
            loc.start()
            copies.append(loc)
        remote = _peer_copies(ins, outs, scatter, send_sems, recv_sems)
        for cp in remote:
            cp.start()
        for cp in remote:
            cp.wait_recv()
        for cp in remote:
            cp.wait_send()
        for cp in copies:
            cp.wait()

    return pl.pallas_call(
        body, name=name, out_shape=tuple(out_shapes), in_specs=[ANY] * n, out_specs=tuple([ANY] * n),
        scratch_shapes=[pltpu.SemaphoreType.DMA((n * (NDEV - 1),)), pltpu.SemaphoreType.DMA((n * (NDEV - 1),)),
                        pltpu.SemaphoreType.DMA((n,))],
    )(*arrs)


def _gather_two_level(arrs, name):
    n = len(arrs)

    def body(*refs):
        ins, outs = refs[:n], refs[n:2 * n]
        send_sems, recv_sems, loc_sems = refs[2 * n:]
        x, y, c = lax.axis_index("x"), lax.axis_index("y"), lax.axis_index("c")
        sib = (x, y, 1 - c)
        chips = [(1 - x, y), (x, 1 - y), (1 - x, 1 - y)]

        def slot(k, px, py, pc):
            return outs[k].at[4 * px + 2 * py + pc]

        def copy(k, q, block, to, src=None):
            return pltpu.make_async_remote_copy(
                src_ref=slot(k, *block) if src is None else src, dst_ref=slot(k, *block),
                send_sem=send_sems.at[7 * k + q], recv_sem=recv_sems.at[7 * k + q], device_id=to, device_id_type=MESH_ID)

        mine = [pltpu.make_async_copy(ins[k], slot(k, x, y, c), loc_sems.at[k]) for k in range(n)]
        for cp in mine:
            cp.start()
        first = []
        for k in range(n):
            first.append(copy(k, 0, (x, y, c), sib, src=ins[k]))
            first += [copy(k, 1 + j, (x, y, c), (*chip, c), src=ins[k]) for j, chip in enumerate(chips)]
        for cp in first:
            cp.start()
        passed = []
        for j, chip in enumerate(chips):
            for k in range(n):
                copy(k, 1 + j, (*chip, c), (x, y, c)).wait_recv()
                fw = copy(k, 4 + j, (*chip, c), sib)
                fw.start()
                passed.append(fw)
        for k in range(n):
            copy(k, 0, (x, y, 1 - c), (x, y, c)).wait_recv()
            for j, chip in enumerate(chips):
                copy(k, 4 + j, (*chip, 1 - c), (x, y, c)).wait_recv()
        for cp in first + passed:
            cp.wait_send()
        for cp in mine:
            cp.wait()

    return pl.pallas_call(
        body, name=name, out_shape=tuple(_sds((NDEV,) + a.shape, a.dtype) for a in arrs),
        in_specs=[ANY] * n, out_specs=tuple([ANY] * n),
        scratch_shapes=[pltpu.SemaphoreType.DMA((7 * n,)), pltpu.SemaphoreType.DMA((7 * n,)), pltpu.SemaphoreType.DMA((n,))],
    )(*arrs)


def _peer_copies(srcs, lands, scatter, send_sems, recv_sems):
    x, y, c = lax.axis_index("x"), lax.axis_index("y"), lax.axis_index("c")
    me = 4 * x + 2 * y + c
    out = []
    for k in range(len(srcs)):
        for m in range(1, NDEV):
            px, py, pc = x ^ (m >> 2), y ^ ((m >> 1) & 1), c ^ (m & 1)
            src = srcs[k].at[4 * px + 2 * py + pc] if scatter[k] else srcs[k]
            out.append(pltpu.make_async_remote_copy(
                src_ref=src, dst_ref=lands[k].at[me], send_sem=send_sems.at[k * (NDEV - 1) + m - 1],
                recv_sem=recv_sems.at[k * (NDEV - 1) + m - 1],
                device_id=(px, py, pc), device_id_type=MESH_ID))
    return out


def _exchange_start(arrs, scatter, name):
    n = len(arrs)
    me = 4 * lax.axis_index("x") + 2 * lax.axis_index("y") + lax.axis_index("c")
    lands = []
    for a, sc in zip(arrs, scatter):
        own = lax.dynamic_index_in_dim(a, me, 0, keepdims=True) if sc else a[None]
        shape = a.shape if sc else (NDEV,) + a.shape
        lands.append(lax.dynamic_update_index_in_dim(lax.empty(shape, a.dtype), own, me, 0))

    def body(*refs):
        srcs, lnds = refs[:n], refs[n:2 * n]
        send_sems, recv_sems = refs[2 * n], refs[2 * n + 1]
        token = refs[-1]
        for cp in _peer_copies(srcs, lnds, scatter, send_sems, recv_sems):
            cp.start()
        token[...] = jnp.zeros_like(token)

    ops = [pltpu.with_memory_space_constraint(a, pltpu.HBM) for a in list(arrs) + lands]
    res = pl.pallas_call(
        body, name=name,
        out_shape=(pltpu.SemaphoreType.DMA((n * (NDEV - 1),)), pltpu.SemaphoreType.DMA((n * (NDEV - 1),)))
        + tuple(pltpu.HBM(o.shape, o.dtype) for o in ops) + (_sds((8, 128), F32),),
        in_specs=[HBM] * (2 * n), out_specs=(SEM, SEM) + (HBM,) * (2 * n) + (pl.BlockSpec(memory_space=pltpu.VMEM),),
        input_output_aliases={k: 2 + k for k in range(2 * n)},
        compiler_params=pltpu.CompilerParams(has_side_effects=EFFECT),
    )(*ops)
    return res[:-1], res[-1]


def _exchange_wait(handle, scatter, after, name):
    send_sems, recv_sems = handle[0], handle[1]
    bufs = handle[2:]
    n = len(bufs) // 2
    after = list(after)

    def body(*refs):
        srcs, lnds = refs[:n], refs[n:2 * n]
        for cp in _peer_copies(srcs, lnds, scatter, refs[2 * n], refs[2 * n + 1]):
            cp.wait_send()
            cp.wait_recv()

    res = pl.pallas_call(
        body, name=name, out_shape=tuple(pltpu.HBM(b.shape, b.dtype) for b in bufs),
        in_specs=[HBM] * (2 * n) + [SEM, SEM] + [ANY] * len(after), out_specs=(HBM,) * (2 * n),
        input_output_aliases={k: k for k in range(2 * n)},
        compiler_params=pltpu.CompilerParams(has_side_effects=EFFECT),
    )(*bufs, send_sems, recv_sems, *after)
    return res[n:]


def _tie(x, token, name):
    def body(x_ref, t_ref, o_ref):
        del x_ref, t_ref, o_ref

    return pl.pallas_call(body, name=name, out_shape=_sds(x.shape, x.dtype), in_specs=[ANY, ANY], out_specs=ANY,
                          input_output_aliases={0: 0})(x, token)


TM_E = 272


def _norm_fwd(h, g, name):
    def body(h_ref, g_ref, o_ref):
        xv = h_ref[...]
        r = lax.rsqrt(jnp.mean(xv * xv, axis=-1, keepdims=True) + EPS)
        o_ref[...] = (xv * r * g_ref[...]).astype(BF16)

    return pl.pallas_call(
        body, name=name, grid=(T // TM_E,),
        in_specs=[pl.BlockSpec((TM_E, D), lambda i: (i, 0)), pl.BlockSpec((1, D), lambda i: (0, 0))],
        out_specs=pl.BlockSpec((TM_E, D), lambda i: (i, 0)), out_shape=_sds((T, D), BF16),
        compiler_params=_cp(("parallel",)))(h, g)


def _norm_fwd_t(h, g, name):
    def body(h_ref, g_ref, o_ref, ot_ref):
        xv = h_ref[...]
        r = lax.rsqrt(jnp.mean(xv * xv, axis=-1, keepdims=True) + EPS)
        y = xv * r * g_ref[...]
        o_ref[...] = y.astype(BF16)
        ot_ref[...] = y.T.astype(BF16)

    return pl.pallas_call(
        body, name=name, grid=(T // 128,),
        in_specs=[pl.BlockSpec((128, D), lambda i: (i, 0)), pl.BlockSpec((1, D), lambda i: (0, 0))],
        out_specs=(pl.BlockSpec((128, D), lambda i: (i, 0)), pl.BlockSpec((D, 128), lambda i: (0, i))),
        out_shape=(_sds((T, D), BF16), _sds((D, T), BF16)), compiler_params=_cp(("parallel",)))(h, g)


def _norm_bwd(h, g, dn, dres, name):
    def body(h_ref, g_ref, dn_ref, dres_ref, dh_ref, dhb_ref, dg_ref):
        i = pl.program_id(0)
        xv = h_ref[...]
        r = lax.rsqrt(jnp.mean(xv * xv, axis=-1, keepdims=True) + EPS)
        xh = xv * r
        dnv = dn_ref[...].astype(F32)
        dxh = dnv * g_ref[...]
        dh = dres_ref[...] + r * (dxh - xh * jnp.mean(dxh * xh, axis=-1, keepdims=True))
        dh_ref[...] = dh
        dhb_ref[...] = dh.astype(BF16)
        part = jnp.sum(dnv * xh, axis=0, keepdims=True)

        @pl.when(i == 0)
        def _():
            dg_ref[...] = part

        @pl.when(i > 0)
        def _():
            dg_ref[...] += part

    blk = pl.BlockSpec((TM_E, D), lambda i: (i, 0))
    vec = pl.BlockSpec((1, D), lambda i: (0, 0))
    return pl.pallas_call(
        body, name=name, grid=(T // TM_E,), in_specs=[blk, vec, blk, blk], out_specs=(blk, blk, vec),
        out_shape=(_sds((T, D), F32), _sds((T, D), BF16), _sds((1, D), F32)),
        compiler_params=_cp(("arbitrary",)))(h, g, dn, dres)


TM_MM = 1088


def _inproj_fwd(a, w_g):
    nb = w_g.shape[2]

    def body(a_ref, w_ref, o_ref):
        o_ref[...] = _dot(a_ref[...], w_ref[0])

    return pl.pallas_call(
        body, name="inproj_fwd", grid=(T // TM_MM, NDEV),
        in_specs=[pl.BlockSpec((TM_MM, D), lambda i, j: (i, 0)), pl.BlockSpec((1, D, nb), lambda i, j: (j, 0, 0))],
        out_specs=pl.BlockSpec((TM_MM, nb), lambda i, j: (i, j)), out_shape=_sds((T, NDEV * nb), F32),
        compiler_params=_cp(("parallel", "parallel")))(a, w_g)


TM_B = 544


def _inproj_bwd_dw(a_t, dp):
    nb = IN_COLS // NDEV

    def body(at_ref, dp_ref, dw_ref):
        dw_ref[0] = _dot(at_ref[...], dp_ref[...]).astype(BF16)

    return pl.pallas_call(
        body, name="inproj_bwd_dw", grid=(NDEV,),
        in_specs=[pl.BlockSpec((D, T), lambda j: (0, 0)), pl.BlockSpec((T, nb), lambda j: (0, j))],
        out_specs=pl.BlockSpec((1, D, nb), lambda j: (j, 0, 0)), out_shape=_sds((NDEV, D, nb), BF16),
        compiler_params=_cp(("parallel",)))(a_t, dp)


def _inproj_bwd_da(dp, w_g):
    nb = w_g.shape[2]

    def body(dp_ref, w_ref, da_ref):
        j = pl.program_id(1)
        dav = _dot(dp_ref[...], w_ref[0], NT)

        @pl.when(j == 0)
        def _():
            da_ref[...] = dav

        @pl.when(j > 0)
        def _():
            da_ref[...] += dav

    return pl.pallas_call(
        body, name="inproj_bwd_da", grid=(T // TM_MM, NDEV),
        in_specs=[pl.BlockSpec((TM_MM, nb), lambda i, j: (i, j)), pl.BlockSpec((1, D, nb), lambda i, j: (j, 0, 0))],
        out_specs=pl.BlockSpec((TM_MM, D), lambda i, j: (i, 0)), out_shape=_sds((T, D), F32),
        compiler_params=_cp(("parallel", "arbitrary")))(dp, w_g)


NA_QB = 256
NA_GROUPS = ROWS // 4
NA_UROWS = 11
NA_KW = NA_UROWS * GRID_W
NA_KU = 768


def _na_row_offset(var, i, j):
    valid = (j < 8, i <= j < i + 8, 3 <= j < NA_UROWS)[var]
    return (j - i + (7, 3, 0)[var]) if valid else None


def _na_bias_table(rpb):
    def body(r_ref, o_ref):
        row3 = lax.broadcasted_iota(jnp.int32, (15, GRID_W, 128), 1)
        lane3 = lax.broadcasted_iota(jnp.int32, (15, GRID_W, 128), 2)
        w3 = lane3 & (GRID_W - 1)
        cs3 = jnp.clip(row3 - 8, 0, GRID_W - 16)
        lane = lax.broadcasted_iota(jnp.int32, (GRID_W, 128), 1)
        neg = jnp.full((GRID_W, 128), NEG, F32)
        z = jnp.stack([jnp.broadcast_to(r_ref[0, a:a + 1, :], (GRID_W, 128)) for a in range(15)])
        for bit in range(6):
            sh = 1 << bit
            z = jnp.where((row3 & sh) != 0, jnp.roll(z, sh, axis=2), z)
        z = jnp.roll(z, 128 - 15, axis=2)
        z = jnp.where(lane3 < GRID_W, z, 0.0)
        z = z + jnp.roll(z, GRID_W, axis=2)
        tabs = jnp.where((w3 >= cs3) & (w3 < cs3 + 16), z, NEG)
        tail = jnp.where(lane < GRID_W + NM, 0.0, NEG)
        for var in range(3):
            for i in range(4):
                for jp in range(NA_KU // 128):
                    halves = []
                    for j in (2 * jp, 2 * jp + 1):
                        a = _na_row_offset(var, i, j) if j < NA_UROWS else None
                        halves.append(tail if j >= NA_UROWS else (neg if a is None else tabs[a]))
                    o_ref[var, 0, i * 64:(i + 1) * 64, jp * 128:(jp + 1) * 128] = jnp.where(lane < GRID_W, halves[0], halves[1])

    rp = jnp.concatenate([rpb, jnp.zeros((NA_HEADS, 15, 128 - 31), F32)], axis=2)
    return pl.pallas_call(
        body, name="na_bias_table", grid=(NA_HEADS,),
        in_specs=[pl.BlockSpec((1, 15, 128), lambda h: (h, 0, 0))],
        out_specs=pl.BlockSpec((3, 1, NA_QB, NA_KU), lambda h: (0, h, 0, 0)),
        out_shape=_sds((3, NA_HEADS, NA_QB, NA_KU), F32), compiler_params=_cp(("parallel",)))(rp)


def _na_var(g):
    return jnp.where(g == 0, 0, jnp.where(g == NA_GROUPS - 1, 2, 1))


def _na_load_window(src_ref, dst, g):
    us = jnp.clip(4 * g - 4, 0, ROWS - NA_UROWS)
    kstart = pl.multiple_of(NM + GRID_W * us, 16)
    dst[0:NA_KW, :] = src_ref[pl.ds(kstart, NA_KW), :].astype(BF16)
    dst[NA_KW:NA_KW + NM, :] = src_ref[0:NM, :].astype(BF16)
    dst[NA_KW + NM:, :] = jnp.zeros((NA_KU - NA_KW - NM, 128), BF16)
    return kstart


def _na_fwd(p_act, bias_tab):
    def body(q_ref, k_ref, v_ref, b_ref, o_ref, lse_ref, ku, vu):
        g = pl.program_id(1)
        _na_load_window(k_ref, ku, g)
        _na_load_window(v_ref, vu, g)
        qstart = pl.multiple_of(NM + NA_QB * g, 16)
        q = q_ref[pl.ds(qstart, NA_QB), :]
        lane = lax.broadcasted_iota(jnp.int32, (NA_QB, 128), 1)
        o_h, lse_h = [], []
        for h in range(2):
            hm = (lane < 64) if h == 0 else (lane >= 64)
            qm = jnp.where(hm, q, 0.0).astype(BF16)
            s = _dot(qm, ku[...], NT) * NA_SCALE + b_ref[0, h]
            m = jnp.max(s, axis=-1, keepdims=True)
            p = jnp.exp(s - m)
            l = jnp.sum(p, axis=-1, keepdims=True)
            o_h.append(_dot(p.astype(BF16), vu[...]) / l)
            lse_h.append(jnp.broadcast_to(m + jnp.log(l), (NA_QB, 128)))
        o_ref[pl.ds(qstart, NA_QB), :] = jnp.where(lane < 64, o_h[0], o_h[1]).astype(BF16)
        lse_ref[0, pl.ds(qstart, NA_QB), :] = jnp.where(lane < 64, lse_h[0], lse_h[1])

        @pl.when(g == 0)
        def _():
            qm_ = q_ref[0:NM, :]
            lane_m = lax.broadcasted_iota(jnp.int32, (NM, 128), 1)
            km, vm = ku[NA_KW:NA_KW + NM, :], vu[NA_KW:NA_KW + NM, :]
            om = []
            for h in range(2):
                hm = (lane_m < 64) if h == 0 else (lane_m >= 64)
                s = _dot(jnp.where(hm, qm_, 0.0).astype(BF16), km, NT) * NA_SCALE
                p = jnp.exp(s - jnp.max(s, axis=-1, keepdims=True))
                l = jnp.sum(p, axis=-1, keepdims=True)
                om.append(_dot(p.astype(BF16), vm) / l)
            o_ref[0:NM, :] = jnp.where(lane_m < 64, om[0], om[1]).astype(BF16)
            o_ref[L:T, :] = jnp.zeros((T - L, 128), BF16)
            lse_ref[0, 0:NM, :] = jnp.zeros((NM, 128), F32)
            lse_ref[0, L:T, :] = jnp.zeros((T - L, 128), F32)

    col = lambda off: pl.BlockSpec((T, 128), lambda hp, g: (0, off + hp))
    return pl.pallas_call(
        body, name="na_fwd", grid=(4, NA_GROUPS),
        in_specs=[col(0), col(4), col(8),
                  pl.BlockSpec((1, 2, NA_QB, NA_KU), lambda hp, g: (_na_var(g), hp, 0, 0))],
        out_specs=(pl.BlockSpec((T, 128), lambda hp, g: (0, hp)), pl.BlockSpec((1, T, 128), lambda hp, g: (hp, 0, 0))),
        out_shape=(_sds((T, 512), BF16), _sds((4, T, 128), F32)),
        scratch_shapes=[pltpu.VMEM((NA_KU, 128), BF16), pltpu.VMEM((NA_KU, 128), BF16)],
        compiler_params=_cp(("parallel", "arbitrary")))(p_act, p_act, p_act, bias_tab)


def _na_bwd(p_act, do, lse, bias_tab):
    def body(q_ref, k_ref, v_ref, do_ref, lse_ref, b_ref, dq_ref, dk_ref, dv_ref, db_ref, ku, vu):
        g = pl.program_id(1)

        @pl.when(g == 0)
        def _():
            dq_ref[...] = jnp.zeros((T, 128), F32)
            dk_ref[...] = jnp.zeros((T, 128), F32)
            dv_ref[...] = jnp.zeros((T, 128), F32)

        kstart = _na_load_window(k_ref, ku, g)
        _na_load_window(v_ref, vu, g)
        qstart = pl.multiple_of(NM + NA_QB * g, 16)
        q = q_ref[pl.ds(qstart, NA_QB), :]
        dov = do_ref[pl.ds(qstart, NA_QB), :]
        lsev = lse_ref[0, pl.ds(qstart, NA_QB), :]
        lane = lax.broadcasted_iota(jnp.int32, (NA_QB, 128), 1)
        first = (g == 0) | (g == 1) | (g == NA_GROUPS - 1)
        dq_h = []
        dku = jnp.zeros((NA_KU, 128), F32)
        dvu = jnp.zeros((NA_KU, 128), F32)
        for h in range(2):
            hm = (lane < 64) if h == 0 else (lane >= 64)
            qm = jnp.where(hm, q, 0.0).astype(BF16)
            dom = jnp.where(hm, dov, 0.0).astype(BF16)
            s = _dot(qm, ku[...], NT) * NA_SCALE + b_ref[0, h]
            p = jnp.exp(s - lsev[:, 64 * h:64 * h + 1])
            dp = _dot(dom, vu[...], NT)
            delta = jnp.sum(p * dp, axis=-1, keepdims=True)
            ds = p * (dp - delta)

            @pl.when(first)
            def _():
                db_ref[0, h] = ds

            @pl.when(jnp.logical_not(first))
            def _():
                db_ref[0, h] += ds

            dsb = (ds * NA_SCALE).astype(BF16)
            dq_h.append(_dot(dsb, ku[...]))
            dku = dku + _dot(dsb, qm, TN)
            dvu = dvu + _dot(p.astype(BF16), dom, TN)
        dq_ref[pl.ds(qstart, NA_QB), :] = jnp.where(lane < 64, dq_h[0], dq_h[1])
        dk_ref[pl.ds(kstart, NA_KW), :] += dku[0:NA_KW]
        dv_ref[pl.ds(kstart, NA_KW), :] += dvu[0:NA_KW]
        dk_ref[0:NM, :] += dku[NA_KW:NA_KW + NM]
        dv_ref[0:NM, :] += dvu[NA_KW:NA_KW + NM]

        @pl.when(g == 0)
        def _():
            qm_ = q_ref[0:NM, :]
            dom_ = do_ref[0:NM, :]
            lane_m = lax.broadcasted_iota(jnp.int32, (NM, 128), 1)
            km, vm = ku[NA_KW:NA_KW + NM, :], vu[NA_KW:NA_KW + NM, :]
            dqs = []
            dkm = jnp.zeros((NM, 128), F32)
            dvm = jnp.zeros((NM, 128), F32)
            for h in range(2):
                hm = (lane_m < 64) if h == 0 else (lane_m >= 64)
                qh = jnp.where(hm, qm_, 0.0).astype(BF16)
                doh = jnp.where(hm, dom_, 0.0).astype(BF16)
                s = _dot(qh, km, NT) * NA_SCALE
                e = jnp.exp(s - jnp.max(s, axis=-1, keepdims=True))
                p = e / jnp.sum(e, axis=-1, keepdims=True)
                dp = _dot(doh, vm, NT)
                ds = p * (dp - jnp.sum(p * dp, axis=-1, keepdims=True))
                dsb = (ds * NA_SCALE).astype(BF16)
                dqs.append(_dot(dsb, km))
                dkm = dkm + _dot(dsb, qh, TN)
                dvm = dvm + _dot(p.astype(BF16), doh, TN)
            dq_ref[0:NM, :] = jnp.where(lane_m < 64, dqs[0], dqs[1])
            dk_ref[0:NM, :] += dkm
            dv_ref[0:NM, :] += dvm

    col = lambda off: pl.BlockSpec((T, 128), lambda hp, g: (0, off + hp))
    ocol = pl.BlockSpec((T, 128), lambda hp, g: (0, hp))
    bspec = pl.BlockSpec((1, 2, NA_QB, NA_KU), lambda hp, g: (_na_var(g), hp, 0, 0))
    return pl.pallas_call(
        body, name="na_bwd", grid=(4, NA_GROUPS),
        in_specs=[col(0), col(4), col(8), ocol, pl.BlockSpec((1, T, 128), lambda hp, g: (hp, 0, 0)), bspec],
        out_specs=(ocol, ocol, ocol, bspec),
        out_shape=(_sds((T, 512), F32), _sds((T, 512), F32), _sds((T, 512), F32), _sds((3, NA_HEADS, NA_QB, NA_KU), F32)),
        scratch_shapes=[pltpu.VMEM((NA_KU, 128), BF16), pltpu.VMEM((NA_KU, 128), BF16)],
        compiler_params=_cp(("parallel", "arbitrary")))(p_act, p_act, p_act, do, lse, bias_tab)


def _na_rpb_reduce(dbias):
    def body(db_ref, o_ref):
        lane = lax.broadcasted_iota(jnp.int32, (GRID_W, 128), 1)
        row3 = lax.broadcasted_iota(jnp.int32, (15, GRID_W, 128), 1)
        lane3 = lax.broadcasted_iota(jnp.int32, (15, GRID_W, 128), 2)
        accs = []
        for a in range(15):
            acc = jnp.zeros((GRID_W, 128), F32)
            for var in range(3):
                for i in range(4):
                    for j in range(NA_UROWS):
                        if _na_row_offset(var, i, j) == a:
                            pair = db_ref[var, 0, i * 64:(i + 1) * 64, (j // 2) * 128:(j // 2 + 1) * 128]
                            acc = acc + jnp.where((lane < GRID_W) if j % 2 == 0 else (lane >= GRID_W), pair, 0.0)
            accs.append(acc)
        z = jnp.stack(accs)
        z = jnp.where(lane3 < GRID_W, z + jnp.roll(z, GRID_W, axis=2), 0.0)
        for bit in range(6):
            sh = 1 << bit
            z = jnp.where((row3 & sh) != 0, jnp.roll(z, 128 - sh, axis=2), z)
        z = jnp.roll(z, 15, axis=2)
        o_ref[0] = jnp.sum(z, axis=1)

    return pl.pallas_call(
        body, name="na_rpb_reduce", grid=(NA_HEADS,),
        in_specs=[pl.BlockSpec((3, 1, NA_QB, NA_KU), lambda h: (0, h, 0, 0))],
        out_specs=pl.BlockSpec((1, 15, 128), lambda h: (h, 0, 0)), out_shape=_sds((NA_HEADS, 15, 128), F32),
        compiler_params=_cp(("parallel",)))(dbias)


HG_RB = 128
HG_NB = T // HG_RB
HG_SLOTS = HG_NB * 8
HI = lax.Precision.HIGHEST
HG_UNROLL = 4


def _chunk_tri(lower):
    r = lax.broadcasted_iota(jnp.int32, (HG_RB, HG_RB), 0)
    c = lax.broadcasted_iota(jnp.int32, (HG_RB, HG_RB), 1)
    same = (r // HG_C) == (c // HG_C)
    keep = (c <= r) if lower else (c >= r)
    return jnp.where(same & keep, 1.0, 0.0).astype(F32)


def _hg_gate_terms(z, lg):
    dl = lg[0:1, :] - lg[1:2, :]
    log_lb = jax.nn.log_sigmoid(dl)
    log_1mlb = jax.nn.log_sigmoid(-dl)
    yz = log_1mlb + jax.nn.log_sigmoid(z)
    log_f = jnp.logaddexp(log_lb, yz)
    snz = jax.nn.sigmoid(-z)
    k = jnp.exp(log_1mlb) * snz
    w2 = jnp.exp(yz - log_f)
    return log_f, k, snz, w2


def _hg_pre(p_act, logits):
    def body(q_ref, zf_ref, zb_ref, lg_ref, qh_ref, kf_ref, bf_ref, kb_ref, bb_ref):
        qh_ref[...] = jax.nn.silu(q_ref[...])
        lf, kf, _, _ = _hg_gate_terms(zf_ref[...], lg_ref[0])
        kf_ref[...] = kf
        bf_ref[...] = jnp.dot(_chunk_tri(True), lf, precision=HI, preferred_element_type=F32)
        lb_, kb, _, _ = _hg_gate_terms(zb_ref[...], lg_ref[1])
        kb_ref[...] = kb
        bb_ref[...] = jnp.dot(_chunk_tri(False), lb_, precision=HI, preferred_element_type=F32)

    blk = lambda c: pl.BlockSpec((HG_RB, 512), lambda i: (i, c))
    ob = pl.BlockSpec((HG_RB, 512), lambda i: (i, 0))
    return pl.pallas_call(
        body, name="hg_pre", grid=(HG_NB,),
        in_specs=[blk(3), blk(4), blk(5), pl.BlockSpec((2, 2, 512), lambda i: (0, 0, 0))],
        out_specs=(ob,) * 5, out_shape=(_sds((T, 512), F32),) * 5,
        compiler_params=_cp(("parallel",)))(p_act, p_act, p_act, logits)


def _bdot(a, b, ca, cb):
    return lax.dot_general(a.astype(BF16), b.astype(BF16), (((ca,), (cb,)), ((0,), (0,))), preferred_element_type=F32)


HG_S = 8
HG_NS = HG_RB // HG_S


def _lane_sums(xs):
    l_io = lax.broadcasted_iota(jnp.int32, (HG_NS, HG_S, HG_S), 2)
    a = jnp.zeros((HG_NS, HG_S, HG_S), F32)
    for j, x in enumerate(xs):
        a = a + jnp.where(l_io == j, jnp.sum(x, axis=-1, keepdims=True), 0.0)
    return a


def _halves(x):
    y = x.reshape(8, 2, HG_S, x.shape[-1])
    return y[:, 0], y[:, 1]


def _join(first, second):
    return jnp.stack([first, second], axis=1).reshape(HG_RB, first.shape[-1])


def _cross_split(rev, b4):
    b_1, b_2 = _halves(b4)
    if rev:
        r = b_2[:, 0:1, :]
        return jnp.exp(b_1 - r), jnp.exp(r - b_2)
    r = b_1[:, HG_S - 1:HG_S, :]
    return jnp.exp(b_2 - r), jnp.exp(r - b_1)


def _hg_scan_fwd(qh, k, b, p_act, rev):
    anchor = 0 if rev else HG_C - 1

    def body(q_ref, k_ref, b_ref, v_ref, o_ref, st_ref, dsc):
        def phase_a(blk, _):
            rows = pl.ds(pl.multiple_of(blk * HG_RB, HG_RB), HG_RB)
            b3 = b_ref[rows, :].reshape(8, HG_C, 128)
            k3 = k_ref[rows, :].reshape(8, HG_C, 128)
            v3 = v_ref[rows, :].reshape(8, HG_C, 128)
            bl = b3[:, anchor:anchor + 1, :]
            kt = k3 * jnp.exp(bl - b3)
            st_ref[0, pl.ds(pl.multiple_of(blk * 8, 8), 8)] = _bdot(v3, kt, 1, 1)
            dsc[pl.ds(pl.multiple_of(blk * 8, 8), 8), :] = jnp.exp(bl[:, 0, :])
            return 0

        lax.fori_loop(0, HG_NB, phase_a, 0, unroll=HG_UNROLL)

        def phase_b(n, carry):
            c = (NCHUNK - 1 - n) if rev else n
            u = st_ref[0, c]
            st_ref[0, c] = carry
            return carry * dsc[pl.ds(c, 1), :] + u

        lax.fori_loop(0, NCHUNK, phase_b, jnp.zeros((128, 128), F32))
        for c in range(NCHUNK, HG_SLOTS):
            st_ref[0, c] = jnp.zeros((128, 128), F32)

        t_io = lax.broadcasted_iota(jnp.int32, (HG_NS, HG_S, 128), 1)

        def phase_c(blk, _):
            rows = pl.ds(pl.multiple_of(blk * HG_RB, HG_RB), HG_RB)
            b4 = b_ref[rows, :].reshape(HG_NS, HG_S, 128)
            k4 = k_ref[rows, :].reshape(HG_NS, HG_S, 128)
            q4 = q_ref[rows, :].reshape(HG_NS, HG_S, 128)
            v4 = v_ref[rows, :].reshape(HG_NS, HG_S, 128)
            st = st_ref[0, pl.ds(pl.multiple_of(blk * 8, 8), 8)]
            o = _bdot((q4 * jnp.exp(b4)).reshape(8, HG_C, 128), st, 2, 2).reshape(HG_RB, 128)
            terms = []
            for s in range(HG_S):
                ok = (t_io <= s) if rev else (t_io >= s)
                f = jnp.exp(jnp.where(ok, b4 - b4[:, s:s + 1, :], NEG))
                terms.append(q4 * f * k4[:, s:s + 1, :])
            o_in = _bdot(_lane_sums(terms), v4, 2, 1)
            wq, wk = _cross_split(rev, b4)
            q_1, q_2 = _halves(q4)
            k_1, k_2 = _halves(k4)
            v_1, v_2 = _halves(v4)
            o_1, o_2 = _halves(o_in)
            if rev:
                o_1 = o_1 + _bdot(_bdot(q_1 * wq, k_2 * wk, 2, 2), v_2, 2, 1)
            else:
                o_2 = o_2 + _bdot(_bdot(q_2 * wq, k_1 * wk, 2, 2), v_1, 2, 1)
            o_ref[rows, :] = o + _join(o_1, o_2)
            return 0

        lax.fori_loop(0, HG_NB, phase_c, 0, unroll=HG_UNROLL)

    col = pl.BlockSpec((T, 128), lambda h: (0, h))
    return pl.pallas_call(
        body, name="hg_scan_bwd_dir" if rev else "hg_scan_fwd_dir", grid=(HG_HEADS,),
        in_specs=[col, col, col, pl.BlockSpec((T, 128), lambda h: (0, 24 + h))],
        out_specs=(col, pl.BlockSpec((1, HG_SLOTS, 128, 128), lambda h: (h, 0, 0, 0))),
        out_shape=(_sds((T, 512), F32), _sds((HG_HEADS, HG_SLOTS, 128, 128), F32)),
        scratch_shapes=[pltpu.VMEM((HG_SLOTS, 128), F32)],
        compiler_params=_cp(("parallel",), 56))(qh, k, b, p_act)


def _hg_scan_bwd(qh, k, b, p_act, st, do, rev):
    anchor = 0 if rev else HG_C - 1

    def body(q_ref, k_ref, b_ref, v_ref, st_ref, do_ref, dq_ref, dk_ref, db_ref, dv_ref, gst, dsc, dbl):
        def phase_a(blk, _):
            rows = pl.ds(pl.multiple_of(blk * HG_RB, HG_RB), HG_RB)
            b3 = b_ref[rows, :].reshape(8, HG_C, 128)
            q3 = q_ref[rows, :].reshape(8, HG_C, 128)
            do3 = do_ref[rows, :].reshape(8, HG_C, 128)
            gst[pl.ds(pl.multiple_of(blk * 8, 8), 8)] = _bdot(do3, q3 * jnp.exp(b3), 1, 1)
            dsc[pl.ds(pl.multiple_of(blk * 8, 8), 8), :] = jnp.exp(b3[:, anchor, :])
            return 0

        lax.fori_loop(0, HG_NB, phase_a, 0, unroll=HG_UNROLL)

        def phase_b(n, carry):
            c = n if rev else (NCHUNK - 1 - n)
            w = gst[c]
            gst[c] = carry
            dcv = dsc[pl.ds(c, 1), :]
            dbl[pl.ds(c, 1), :] = dcv * jnp.sum(st_ref[0, c] * carry, axis=0, keepdims=True)
            return carry * dcv + w

        lax.fori_loop(0, NCHUNK, phase_b, jnp.zeros((128, 128), F32))
        for c in range(NCHUNK, HG_SLOTS):
            gst[c] = jnp.zeros((128, 128), F32)
            dbl[c:c + 1, :] = jnp.zeros((1, 128), F32)

        t_io = lax.broadcasted_iota(jnp.int32, (HG_NS, HG_S, 128), 1)
        t16 = lax.broadcasted_iota(jnp.int32, (8, HG_C, 128), 1)
        r_io = lax.broadcasted_iota(jnp.int32, (HG_NS, HG_S, HG_S), 1)
        l_io = lax.broadcasted_iota(jnp.int32, (HG_NS, HG_S, HG_S), 2)

        def phase_c(blk, _):
            rows = pl.ds(pl.multiple_of(blk * HG_RB, HG_RB), HG_RB)
            cs = pl.ds(pl.multiple_of(blk * 8, 8), 8)
            b4 = b_ref[rows, :].reshape(HG_NS, HG_S, 128)
            k4 = k_ref[rows, :].reshape(HG_NS, HG_S, 128)
            q4 = q_ref[rows, :].reshape(HG_NS, HG_S, 128)
            v4 = v_ref[rows, :].reshape(HG_NS, HG_S, 128)
            do4 = do_ref[rows, :].reshape(HG_NS, HG_S, 128)
            b3, k3, q3 = (z.reshape(8, HG_C, 128) for z in (b4, k4, q4))
            v3, do3 = v4.reshape(8, HG_C, 128), do4.reshape(8, HG_C, 128)
            s_t = st_ref[0, cs]
            g_t = gst[cs]
            bl = b3[:, anchor:anchor + 1, :]
            ekl = jnp.exp(bl - b3)
            kt = k3 * ekl
            dkt = _bdot(v3, g_t, 2, 1)
            dq = (_bdot(do3, s_t, 2, 1) * jnp.exp(b3)).reshape(HG_NS, HG_S, 128)
            dk = (dkt * ekl).reshape(HG_NS, HG_S, 128)
            dv = _bdot(kt, g_t, 2, 2).reshape(HG_NS, HG_S, 128)
            dbl3 = dbl[cs, :].reshape(8, 1, 128) + jnp.sum(dkt * kt, axis=1, keepdims=True)
            causal = (l_io >= r_io) if rev else (l_io <= r_io)
            da = jnp.where(causal, _bdot(do4, v4, 2, 2), 0.0)
            causal_t = (l_io <= r_io) if rev else (l_io >= r_io)
            dat = jnp.where(causal_t, _bdot(v4, do4, 2, 2), 0.0)
            for s in range(HG_S):
                ok = (t_io <= s) if rev else (t_io >= s)
                f = jnp.exp(jnp.where(ok, b4 - b4[:, s:s + 1, :], NEG))
                dq = dq + da[:, :, s:s + 1] * (f * k4[:, s:s + 1, :])
            terms = []
            for t in range(HG_S):
                ok = (t_io >= t) if rev else (t_io <= t)
                e = jnp.exp(jnp.where(ok, b4[:, t:t + 1, :] - b4, NEG))
                eq = e * q4[:, t:t + 1, :]
                dk = dk + dat[:, :, t:t + 1] * eq
                terms.append(eq * k4)
            dv = dv + _bdot(_lane_sums(terms), do4, 2, 1)
            wq, wk = _cross_split(rev, b4)
            pick = (lambda z: _halves(z)) if rev else (lambda z: _halves(z)[::-1])
            (q_q, _), (_, k_k), (_, v_k), (do_q, _) = pick(q4), pick(k4), pick(v4), pick(do4)
            qx, kx = q_q * wq, k_k * wk
            dq_q = _bdot(_bdot(do_q, v_k, 2, 2), kx, 2, 1) * wq
            dk_k = _bdot(_bdot(v_k, do_q, 2, 2), qx, 2, 1) * wk
            dv_k = _bdot(_bdot(kx, qx, 2, 2), do_q, 2, 1)
            zero = jnp.zeros((8, HG_S, 128), F32)
            place_q = (lambda z: _join(z, zero)) if rev else (lambda z: _join(zero, z))
            place_k = (lambda z: _join(zero, z)) if rev else (lambda z: _join(z, zero))
            dq2 = dq.reshape(HG_RB, 128) + place_q(dq_q)
            dk2 = dk.reshape(HG_RB, 128) + place_k(dk_k)
            dv2 = dv.reshape(HG_RB, 128) + place_k(dv_k)
            dq3, dk3 = dq2.reshape(8, HG_C, 128), dk2.reshape(8, HG_C, 128)
            db = q3 * dq3 - k3 * dk3 + jnp.where(t16 == anchor, dbl3, 0.0)
            dq_ref[rows, :] = dq2
            dk_ref[rows, :] = dk2
            db_ref[rows, :] = db.reshape(HG_RB, 128)
            dv_ref[rows, :] = dv2
            return 0

        lax.fori_loop(0, HG_NB, phase_c, 0, unroll=HG_UNROLL)

    col = pl.BlockSpec((T, 128), lambda h: (0, h))
    return pl.pallas_call(
        body, name="hg_scan_bwd_dir_bwd" if rev else "hg_scan_fwd_dir_bwd", grid=(HG_HEADS,),
        in_specs=[col, col, col, pl.BlockSpec((T, 128), lambda h: (0, 24 + h)),
                  pl.BlockSpec((1, HG_SLOTS, 128, 128), lambda h: (h, 0, 0, 0)), col],
        out_specs=(col,) * 4, out_shape=(_sds((T, 512), F32),) * 4,
        scratch_shapes=[pltpu.VMEM((HG_SLOTS, 128, 128), F32), pltpu.VMEM((HG_SLOTS, 128), F32),
                        pltpu.VMEM((HG_SLOTS, 128), F32)],
        compiler_params=_cp(("parallel",), 56))(qh, k, b, p_act, st, do)


def _row_valid(i, tm):
    r = lax.broadcasted_iota(jnp.int32, (tm, 1), 0) + i * tm
    return r < L


def _hg_post(o_f, o_b, p_act, gain):
    def body(of_ref, ob_ref, g_ref, gain_ref, u_ref):
        o = of_ref[...] + ob_ref[...]
        sg = jax.nn.silu(g_ref[...])
        parts = []
        for h in range(HG_HEADS):
            oh = o[:, 128 * h:128 * (h + 1)]
            parts.append(oh * lax.rsqrt(jnp.mean(oh * oh, axis=-1, keepdims=True) + EPS))
        n = jnp.concatenate(parts, axis=1)
        u = n * gain_ref[...] * sg
        u_ref[...] = jnp.where(_row_valid(pl.program_id(0), TM_E), u, 0.0).astype(BF16)

    blk = pl.BlockSpec((TM_E, 512), lambda i: (i, 0))
    return pl.pallas_call(
        body, name="hg_post", grid=(T // TM_E,),
        in_specs=[blk, blk, pl.BlockSpec((TM_E, 512), lambda i: (i, 7)), pl.BlockSpec((1, 512), lambda i: (0, 0))],
        out_specs=blk, out_shape=_sds((T, 512), BF16), compiler_params=_cp(("parallel",)))(o_f, o_b, p_act, gain)


def _hg_post_bwd(du, o_f, o_b, p_act, gain):
    def body(du_ref, of_ref, ob_ref, g_ref, gain_ref, do_ref, dg_ref, dgain_ref):
        i = pl.program_id(0)
        valid = _row_valid(i, TM_E)
        duv = jnp.where(valid, du_ref[...], 0.0)
        o = of_ref[...] + ob_ref[...]
        gv = g_ref[...]
        sig = jax.nn.sigmoid(gv)
        sg = gv * sig
        gain_v = gain_ref[...]
        dn = duv * gain_v * sg
        do_parts, n_parts = [], []
        for h in range(HG_HEADS):
            sl = slice(128 * h, 128 * (h + 1))
            oh = o[:, sl]
            r = lax.rsqrt(jnp.mean(oh * oh, axis=-1, keepdims=True) + EPS)
            nh = oh * r
            dnh = dn[:, sl]
            do_parts.append(r * (dnh - nh * jnp.mean(dnh * nh, axis=-1, keepdims=True)))
            n_parts.append(nh)
        n = jnp.where(valid, jnp.concatenate(n_parts, axis=1), 0.0)
        do_ref[...] = jnp.where(valid, jnp.concatenate(do_parts, axis=1), 0.0)
        dg_ref[...] = (duv * n * gain_v * (sig * (1.0 + gv * (1.0 - sig)))).astype(BF16)
        part = jnp.sum(duv * n * sg, axis=0, keepdims=True)

        @pl.when(i == 0)
        def _():
            dgain_ref[...] = part

        @pl.when(i > 0)
        def _():
            dgain_ref[...] += part

    blk = pl.BlockSpec((TM_E, 512), lambda i: (i, 0))
    vec = pl.BlockSpec((1, 512), lambda i: (0, 0))
    return pl.pallas_call(
        body, name="hg_post_bwd", grid=(T // TM_E,),
        in_specs=[blk, blk, blk, pl.BlockSpec((TM_E, 512), lambda i: (i, 7)), vec],
        out_specs=(blk, blk, vec), out_shape=(_sds((T, 512), F32), _sds((T, 512), BF16), _sds((1, 512), F32)),
        compiler_params=_cp(("arbitrary",)))(du, o_f, o_b, p_act, gain)


def _hg_pre_bwd(p_act, logits, dq_f, dq_b, dk_f, dk_b, db_f, db_b, dv_f, dv_b):
    def body(q_ref, zf_ref, zb_ref, lg_ref, dqf_ref, dqb_ref, dkf_ref, dkb_ref, dbf_ref, dbb_ref, dvf_ref, dvb_ref,
             dq_ref, dzf_ref, dzb_ref, di_ref, dlg_ref):
        i = pl.program_id(0)
        valid = _row_valid(i, HG_RB)
        qv = q_ref[...]
        sig = jax.nn.sigmoid(qv)
        dq_ref[...] = jnp.where(valid, (dqf_ref[...] + dqb_ref[...]) * (sig * (1.0 + qv * (1.0 - sig))), 0.0).astype(BF16)
        di_ref[...] = jnp.where(valid, dvf_ref[...] + dvb_ref[...], 0.0).astype(BF16)
        for d, (z_ref, dk_r, db_r, dz_ref) in enumerate(((zf_ref, dkf_ref, dbf_ref, dzf_ref), (zb_ref, dkb_ref, dbb_ref, dzb_ref))):
            lg = lg_ref[d]
            dl = lg[0:1, :] - lg[1:2, :]
            lb = jax.nn.sigmoid(dl)
            one_m_lb = jax.nn.sigmoid(-dl)
            log_f, _, snz, w2 = _hg_gate_terms(z_ref[...], lg)
            dbv = jnp.where(valid, db_r[...], 0.0)
            dkv = jnp.where(valid, dk_r[...], 0.0)
            dlf = jnp.dot(_chunk_tri(d == 1), dbv, precision=HI, preferred_element_type=F32)
            sz = 1.0 - snz
            dz_ref[...] = (dlf * w2 * snz - dkv * one_m_lb * sz * snz).astype(BF16)
            dlb = jnp.sum(dlf * snz * jnp.exp(-log_f) - dkv * snz, axis=0, keepdims=True)
            dl0 = dlb * lb * one_m_lb
            part = jnp.concatenate([dl0, -dl0], axis=0)

            @pl.when(i == 0)
            def _():
                dlg_ref[d] = part

            @pl.when(i > 0)
            def _():
                dlg_ref[d] += part

    blk = lambda c: pl.BlockSpec((HG_RB, 512), lambda i: (i, c))
    ob = pl.BlockSpec((HG_RB, 512), lambda i: (i, 0))
    lgs = pl.BlockSpec((2, 2, 512), lambda i: (0, 0, 0))
    return pl.pallas_call(
        body, name="hg_pre_bwd", grid=(HG_NB,),
        in_specs=[blk(3), blk(4), blk(5), lgs] + [ob] * 8,
        out_specs=(ob, ob, ob, ob, lgs),
        out_shape=(_sds((T, 512), BF16),) * 4 + (_sds((2, 2, 512), F32),),
        compiler_params=_cp(("arbitrary",)))(p_act, p_act, p_act, logits, dq_f, dq_b, dk_f, dk_b, db_f, db_b, dv_f, dv_b)


def _mix_fwd(o_na, u_hg, w_na, w_hg, p_act):
    def body(ona_ref, uhg_ref, wna_ref, whg_ref, gna_ref, ghg_ref, o_ref):
        y_na = _dot(ona_ref[...], wna_ref[...])
        y_hg = _dot(uhg_ref[...], whg_ref[...])
        o_ref[...] = (jax.nn.sigmoid(gna_ref[...]) * y_na + jax.nn.sigmoid(ghg_ref[...]) * y_hg).astype(BF16)

    act = pl.BlockSpec((TM_B, 512), lambda i: (i, 0))
    wsp = pl.BlockSpec((512, D), lambda i: (0, 0))
    return pl.pallas_call(
        body, name="mix_fwd", grid=(T // TM_B,),
        in_specs=[act, act, wsp, wsp, pl.BlockSpec((TM_B, D), lambda i: (i, 4)), pl.BlockSpec((TM_B, D), lambda i: (i, 5))],
        out_specs=pl.BlockSpec((TM_B, D), lambda i: (i, 0)), out_shape=_sds((T, D), BF16),
        compiler_params=_cp(("parallel",)))(o_na, u_hg, w_na, w_hg, p_act, p_act)


def _mix_bwd(o_na, u_hg, w_na, w_hg, p_act, dmix):
    ni = T // TM_B

    def body(ona_ref, uhg_ref, wna_ref, whg_ref, gna_ref, ghg_ref, dmix_ref,
             dgna_ref, dghg_ref, dwna_ref, dwhg_ref, dona_ref, duhg_ref, acc_na, acc_hg):
        i = pl.program_id(0)
        dm = dmix_ref[...].astype(F32)
        for x_ref, w_ref, g_ref, dg_ref, dx_ref, dw_ref, acc in (
                (ona_ref, wna_ref, gna_ref, dgna_ref, dona_ref, dwna_ref, acc_na),
                (uhg_ref, whg_ref, ghg_ref, dghg_ref, duhg_ref, dwhg_ref, acc_hg)):
            xv = x_ref[...]
            y = _dot(xv, w_ref[...])
            sg = jax.nn.sigmoid(g_ref[...])
            dg_ref[...] = (dm * y * sg * (1.0 - sg)).astype(BF16)
            dy = (dm * sg).astype(BF16)
            dx_ref[...] = _dot(dy, w_ref[...], NT)
            part = _dot(xv, dy, TN)

            @pl.when(i == 0)
            def _():
                acc[...] = part

            @pl.when(i > 0)
            def _():
                acc[...] += part

            @pl.when(i == ni - 1)
            def _():
                dw_ref[...] = acc[...].astype(BF16)

    act = pl.BlockSpec((TM_B, 512), lambda i: (i, 0))
    wsp = pl.BlockSpec((512, D), lambda i: (0, 0))
    rblk = pl.BlockSpec((TM_B, D), lambda i: (i, 0))
    return pl.pallas_call(
        body, name="mix_bwd", grid=(ni,),
        in_specs=[act, act, wsp, wsp, pl.BlockSpec((TM_B, D), lambda i: (i, 4)), pl.BlockSpec((TM_B, D), lambda i: (i, 5)),
                  rblk],
        out_specs=(rblk, rblk, wsp, wsp, act, act),
        out_shape=(_sds((T, D), BF16), _sds((T, D), BF16), _sds((512, D), BF16), _sds((512, D), BF16),
                   _sds((T, 512), F32), _sds((T, 512), F32)),
        scratch_shapes=[pltpu.VMEM((512, D), F32), pltpu.VMEM((512, D), F32)],
        compiler_params=_cp(("arbitrary",)))(o_na, u_hg, w_na, w_hg, p_act, p_act, dmix)


def _wo_fwd(mix, w_o, h0, g_mlp):
    def body(mix_ref, w_ref, h0_ref, g_ref, h1_ref, m_ref):
        h1 = h0_ref[...] + _dot(mix_ref[...], w_ref[...])
        h1_ref[...] = h1
        r = lax.rsqrt(jnp.mean(h1 * h1, axis=-1, keepdims=True) + EPS)
        m_ref[...] = (h1 * r * g_ref[...]).astype(BF16)

    blk = pl.BlockSpec((TM_B, D), lambda i: (i, 0))
    return pl.pallas_call(
        body, name="wo_fwd", grid=(T // TM_B,),
        in_specs=[blk, pl.BlockSpec((D, D), lambda i: (0, 0)), blk, pl.BlockSpec((1, D), lambda i: (0, 0))],
        out_specs=(blk, blk), out_shape=(_sds((T, D), F32), _sds((T, D), BF16)),
        compiler_params=_cp(("parallel",)))(mix, w_o, h0, g_mlp)


def _wo_bwd(dh1_b, w_o, mix):
    ni = T // TM_B

    def body(dh_ref, w_ref, mix_ref, dmix_ref, dw_ref, acc):
        i = pl.program_id(0)
        dh = dh_ref[...]
        dmix_ref[...] = _dot(dh, w_ref[...], NT).astype(BF16)
        part = _dot(mix_ref[...], dh, TN)

        @pl.when(i == 0)
        def _():
            acc[...] = part

        @pl.when(i > 0)
        def _():
            acc[...] += part

        @pl.when(i == ni - 1)
        def _():
            dw_ref[...] = acc[...].astype(BF16)

    blk = pl.BlockSpec((TM_B, D), lambda i: (i, 0))
    wsp = pl.BlockSpec((D, D), lambda i: (0, 0))
    return pl.pallas_call(
        body, name="wo_bwd", grid=(ni,), in_specs=[blk, wsp, blk], out_specs=(blk, wsp),
        out_shape=(_sds((T, D), BF16), _sds((D, D), BF16)), scratch_shapes=[pltpu.VMEM((D, D), F32)],
        compiler_params=_cp(("arbitrary",)))(dh1_b, w_o, mix)


FF_B = D_FF // NDEV


def _mlp_fwd(m, wup_g, wdown_g, h1):
    def body(m_ref, wu_ref, wd_ref, h1_ref, h2_ref):
        j = pl.program_id(1)
        up = jnp.maximum(_dot(m_ref[...], wu_ref[0]), 0.0)
        part = _dot((up * up).astype(BF16), wd_ref[0])

        @pl.when(j == 0)
        def _():
            h2_ref[...] = h1_ref[...] + part

        @pl.when(j > 0)
        def _():
            h2_ref[...] += part

    blk = pl.BlockSpec((TM_MM, D), lambda i, j: (i, 0))
    return pl.pallas_call(
        body, name="mlp_fwd", grid=(T // TM_MM, NDEV),
        in_specs=[blk, pl.BlockSpec((1, D, FF_B), lambda i, j: (j, 0, 0)), pl.BlockSpec((1, FF_B, D), lambda i, j: (j, 0, 0)), blk],
        out_specs=blk, out_shape=_sds((T, D), F32),
        compiler_params=_cp(("parallel", "arbitrary")))(m, wup_g, wdown_g, h1)


def _mlp_bwd(m, dh2_b, wup_g, wdown_g):
    ni = T // TM_B

    def body(m_ref, dh_ref, wu_ref, wd_ref, dwu_ref, dwd_ref, dm_ref, acc_u, acc_d):
        j, i = pl.program_id(0), pl.program_id(1)
        rows = pl.ds(pl.multiple_of(i * TM_B, TM_B), TM_B)
        mv, dh = m_ref[...], dh_ref[...]
        r = jnp.maximum(_dot(mv, wu_ref[0]), 0.0)
        act = (r * r).astype(BF16)
        dact = _dot(dh, wd_ref[0], NT)
        dup = (dact * (2.0 * r)).astype(BF16)
        pd = _dot(act, dh, TN)
        pu = _dot(mv, dup, TN)
        dmv = _dot(dup, wu_ref[0], NT)

        @pl.when(i == 0)
        def _():
            acc_u[...] = pu
            acc_d[...] = pd

        @pl.when(i > 0)
        def _():
            acc_u[...] += pu
            acc_d[...] += pd

        @pl.when(i == ni - 1)
        def _():
            dwu_ref[0] = acc_u[...].astype(BF16)
            dwd_ref[0] = acc_d[...].astype(BF16)

        @pl.when(j == 0)
        def _():
            dm_ref[rows, :] = dmv

        @pl.when(j > 0)
        def _():
            dm_ref[rows, :] += dmv

    blk = pl.BlockSpec((TM_B, D), lambda j, i: (i, 0))
    wus = pl.BlockSpec((1, D, FF_B), lambda j, i: (j, 0, 0))
    wds = pl.BlockSpec((1, FF_B, D), lambda j, i: (j, 0, 0))
    return pl.pallas_call(
        body, name="mlp_bwd", grid=(NDEV, ni), in_specs=[blk, blk, wus, wds],
        out_specs=(wus, wds, pl.BlockSpec((T, D), lambda j, i: (0, 0))),
        out_shape=(_sds((NDEV, D, FF_B), BF16), _sds((NDEV, FF_B, D), BF16), _sds((T, D), F32)),
        scratch_shapes=[pltpu.VMEM((D, FF_B), F32), pltpu.VMEM((FF_B, D), F32)],
        compiler_params=_cp(("arbitrary", "arbitrary")))(m, dh2_b, wup_g, wdown_g)


def _loss_head(h2, g_final, tgt):
    def body(h_ref, g_ref, t_ref, loss_ref, dh_ref, dhb_ref, dg_ref):
        i = pl.program_id(0)
        r_io = lax.broadcasted_iota(jnp.int32, (TM_E, 1), 0) + i * TM_E
        valid = (r_io >= NM) & (r_io < L)
        xv = h_ref[...]
        r = lax.rsqrt(jnp.mean(xv * xv, axis=-1, keepdims=True) + EPS)
        xh = xv * r
        gv = g_ref[...]
        err = jnp.where(valid, xh * gv - t_ref[...], 0.0)
        lpart = jnp.broadcast_to(0.5 * jnp.sum(jnp.sum(err * err, axis=-1, keepdims=True) * (1.0 / D), axis=0, keepdims=True), (1, 128))
        dy = err * (1.0 / D)
        dxh = dy * gv
        dh = r * (dxh - xh * jnp.mean(dxh * xh, axis=-1, keepdims=True))
        dh_ref[...] = dh
        dhb_ref[...] = dh.astype(BF16)
        gpart = jnp.sum(dy * xh, axis=0, keepdims=True)

        @pl.when(i == 0)
        def _():
            loss_ref[...] = lpart
            dg_ref[...] = gpart

        @pl.when(i > 0)
        def _():
            loss_ref[...] += lpart
            dg_ref[...] += gpart

    blk = pl.BlockSpec((TM_E, D), lambda i: (i, 0))
    vec = pl.BlockSpec((1, D), lambda i: (0, 0))
    return pl.pallas_call(
        body, name="loss_head", grid=(T // TM_E,), in_specs=[blk, vec, blk],
        out_specs=(pl.BlockSpec((1, 128), lambda i: (0, 0)), blk, blk, vec),
        out_shape=(_sds((1, 128), F32), _sds((T, D), F32), _sds((T, D), BF16), _sds((1, D), F32)),
        compiler_params=_cp(("arbitrary",)))(h2, g_final, tgt)


def _adamw(parts, w, m, v, name):
    rr, cc = w.shape
    tr = rr
    for cand in (256, 128, 64):
        if rr % cand == 0 and rr > cand:
            tr = cand
            break
    c1 = 1.0 - ADAM_B1 ** ADAM_STEP
    c2 = 1.0 - ADAM_B2 ** ADAM_STEP

    def body(p_ref, w_ref, m_ref, v_ref, g_ref, d_ref, nm_ref, nv_ref):
        g = p_ref[0].astype(F32)
        for s in range(1, NDEV):
            g = g + p_ref[s].astype(F32)
        mn = ADAM_B1 * m_ref[...] + (1.0 - ADAM_B1) * g
        vn = ADAM_B2 * v_ref[...] + (1.0 - ADAM_B2) * (g * g)
        g_ref[...] = g
        nm_ref[...] = mn
        nv_ref[...] = vn
        d_ref[...] = -ADAM_LR * ((mn / c1) / (jnp.sqrt(vn / c2) + ADAM_EPS) + ADAM_WD * w_ref[...])

    blk = pl.BlockSpec((tr, cc), lambda i: (i, 0))
    return pl.pallas_call(
        body, name=name, grid=(rr // tr,),
        in_specs=[pl.BlockSpec((NDEV, tr, cc), lambda i: (0, i, 0)), blk, blk, blk],
        out_specs=(blk,) * 4, out_shape=(_sds((rr, cc), F32),) * 4,
        compiler_params=_cp(("parallel",)))(parts, w, m, v)


RPB_N = NA_HEADS * 15 * 31
RPB_PAD = 4096
OWN_ROWS = NM + 8


def _pad_rows(a, rows):
    return jnp.pad(a, ((0, rows - a.shape[0]),) + ((0, 0),) * (a.ndim - 1))


def _pack_owned(meta_blk, lb_blk):
    return jnp.concatenate([meta_blk, _pad_rows(lb_blk.reshape(2, 128), 8)], axis=0)


def _pack_replicated(n_mix, n_mlp, n_final, hg_gain, rpb):
    flat = _pad_rows(rpb.reshape(RPB_N), RPB_PAD)
    return jnp.concatenate([n_mix.reshape(8, 128), n_mlp.reshape(8, 128), n_final.reshape(8, 128),
                            _pad_rows(hg_gain.reshape(4, 128), 8), flat.reshape(32, 128)], axis=0)


def _unpack_replicated(a):
    return (a[0:8].reshape(1, D), a[8:16].reshape(1, D), a[16:24].reshape(D), a[24:28].reshape(1, 512),
            a[32:64].reshape(RPB_PAD)[:RPB_N].reshape(1, NA_HEADS, 15, 31))


def kernel(x, meta_tokens, w_in, w_na_out, w_hg_out, w_o, w_up, w_down, norm_mix, norm_mlp, norm_final, hg_norm, na_rpb, hg_lb_logits, loss_target, m_meta_tokens, m_w_in, m_w_na_out, m_w_hg_out, m_w_o, m_w_up, m_w_down, m_norm_mix, m_norm_mlp, m_norm_final, m_hg_norm, m_na_rpb, m_hg_lb_logits, v_meta_tokens, v_w_in, v_w_na_out, v_w_hg_out, v_w_o, v_w_up, v_w_down, v_norm_mix, v_norm_mlp, v_norm_final, v_hg_norm, v_na_rpb, v_hg_lb_logits):
    owned = _pack_owned(meta_tokens, hg_lb_logits)
    win_g, owned_g = _gather_two_level([w_in[0].astype(BF16), owned], "gather_first")
    later = [w[0].astype(BF16) for w in (w_na_out, w_hg_out, w_o, w_up, w_down)]
    later[0] = _tie(later[0], owned_g, "tie_gather_rest")
    gather_rest, tok = _exchange_start(later, [False] * 5, "gather_rest_start")
    win_g = _tie(win_g, tok, "tie_inproj")
    meta_full = jnp.transpose(owned_g[:, 0:NM, :], (1, 0, 2)).reshape(NM, D)
    logits = jnp.transpose(owned_g[:, NM:NM + 2, :].reshape(NDEV, 2, 2, 64), (1, 2, 0, 3)).reshape(2, 2, 512)

    h0 = jnp.concatenate([meta_full, x[0], jnp.zeros((T - L, D), F32)], axis=0)
    tgt = jnp.concatenate([jnp.zeros((NM, D), F32), loss_target[0], jnp.zeros((T - L, D), F32)], axis=0)
    bias_tab = _na_bias_table(na_rpb[0])

    a, a_t = _norm_fwd_t(h0, norm_mix, "norm_mix_fwd")
    p_act = _inproj_fwd(a, win_g)
    o_na, lse = _na_fwd(p_act, bias_tab)
    qh, k_f, b_f, k_b, b_b = _hg_pre(p_act, logits)
    o_f, st_f = _hg_scan_fwd(qh, k_f, b_f, p_act, False)
    o_b, st_b = _hg_scan_fwd(qh, k_b, b_b, p_act, True)
    u_hg = _hg_post(o_f, o_b, p_act, hg_norm)
    wna_g, whg_g, wo_g, wup_g, wdown_g = _exchange_wait(gather_rest, [False] * 5, [u_hg, o_na], "gather_rest_wait")
    w_o_full = wo_g.reshape(D, D)
    w_na_full = jnp.transpose(wna_g, (1, 0, 2)).reshape(512, D)
    w_hg_full = jnp.transpose(whg_g, (1, 0, 2)).reshape(512, D)
    mix = _mix_fwd(o_na, u_hg, w_na_full, w_hg_full, p_act)
    h1, m_act = _wo_fwd(mix, w_o_full, h0, norm_mlp)
    h2 = _mlp_fwd(m_act, wup_g, wdown_g, h1)
    loss_part, dh2, dh2_b, d_nfinal = _loss_head(h2, norm_final.reshape(1, D), tgt)

    dwup_p, dwdown_p, dm = _mlp_bwd(m_act, dh2_b, wup_g, wdown_g)
    sc_mlp, tok = _exchange_start([dwup_p, dwdown_p], [True] * 2, "scatter_mlp_start")
    dh1, dh1_b, d_nmlp = _norm_bwd(h1, norm_mlp, _tie(dm, tok, "tie_norm_mlp_bwd"), dh2, "norm_mlp_bwd")
    dmix, dwo = _wo_bwd(dh1_b, w_o_full, mix)
    sc_wo, tok = _exchange_start([dwo.reshape(NDEV, D // NDEV, D)], [True], "scatter_wo_start")
    dgna, dghg, dwna, dwhg, do_na, du_hg = _mix_bwd(o_na, u_hg, w_na_full, w_hg_full, p_act, _tie(dmix, tok, "tie_mix_bwd"))
    owner_cols = lambda w: jnp.transpose(w.reshape(512, NDEV, D // NDEV), (1, 0, 2))
    sc_br, tok = _exchange_start([owner_cols(dwna), owner_cols(dwhg)], [True] * 2, "scatter_branch_start")
    du_hg = _tie(du_hg, tok, "tie_hg_post_bwd")
    do_hg, dg_hg, d_gain = _hg_post_bwd(du_hg, o_f, o_b, p_act, hg_norm)
    dq_f, dk_f, db_f, dv_f = _hg_scan_bwd(qh, k_f, b_f, p_act, st_f, do_hg, False)
    dq_b, dk_b, db_b, dv_b = _hg_scan_bwd(qh, k_b, b_b, p_act, st_b, do_hg, True)
    dq_hg, dz_f, dz_b, di_hg, d_logits = _hg_pre_bwd(p_act, logits, dq_f, dq_b, dk_f, dk_b, db_f, db_b, dv_f, dv_b)
    dq_na, dk_na, dv_na, dbias = _na_bwd(p_act, do_na, lse, bias_tab)
    dp = jnp.concatenate([dq_na.astype(BF16), dk_na.astype(BF16), dv_na.astype(BF16), dq_hg, dz_f, dz_b, di_hg, dg_hg,
                          dgna, dghg], axis=1)
    dwin_p = _inproj_bwd_dw(a_t, dp)
    sc_in, tok = _exchange_start([dwin_p], [True], "scatter_in_start")
    da = _inproj_bwd_da(_tie(dp, tok, "tie_inproj_bwd_da"), win_g)
    dh0, _, d_nmix = _norm_bwd(h0, norm_mix, da, dh1, "norm_mix_bwd")
    d_rpb = _na_rpb_reduce(_tie(dbias, tok, "tie_rpb_reduce"))[:, :, :31]

    res = {}

    def update(nm, parts, w, mm, vv):
        res[nm] = [r[None] for r in _adamw(parts, w[0], mm[0], vv[0], "adamw_" + nm)]
        return res[nm][1]

    wup_r, wdown_r = _exchange_wait(sc_mlp, [True] * 2, [dh0, d_rpb], "scatter_mlp_wait")
    update("w_up", wup_r, w_up, m_w_up, v_w_up)
    last = update("w_down", wdown_r, w_down, m_w_down, v_w_down)
    (wo_r,) = _exchange_wait(sc_wo, [True], [last], "scatter_wo_wait")
    last = update("w_o", wo_r, w_o, m_w_o, v_w_o)
    wna_r, whg_r = _exchange_wait(sc_br, [True] * 2, [last], "scatter_branch_wait")
    update("w_na_out", wna_r, w_na_out, m_w_na_out, v_w_na_out)
    last = update("w_hg_out", whg_r, w_hg_out, m_w_hg_out, v_w_hg_out)

    d_meta = jnp.transpose(dh0[0:NM].reshape(NM, NDEV, 128), (1, 0, 2))
    d_lg = jnp.transpose(d_logits.reshape(2, 2, NDEV, 64), (2, 0, 1, 3)).reshape(NDEV, 2, 128)
    owned_p = jnp.concatenate([d_meta, jnp.pad(d_lg, ((0, 0), (0, OWN_ROWS - NM - 2), (0, 0)))], axis=1)
    repl_p = _pack_replicated(d_nmix, d_nmlp, d_nfinal, d_gain, d_rpb)
    owned_r, repl_r = _exchange([_tie(owned_p, last, "tie_scatter_small"), repl_p], [True, False], "scatter_small")
    own = _adamw(owned_r, owned, _pack_owned(m_meta_tokens, m_hg_lb_logits), _pack_owned(v_meta_tokens, v_hg_lb_logits),
                 "adamw_owned_small")
    res["meta_tokens"] = [r[0:NM] for r in own]
    res["hg_lb_logits"] = [r[NM:NM + 2].reshape(2, 2, 64) for r in own]
    rep = _adamw(repl_r, _pack_replicated(norm_mix, norm_mlp, norm_final, hg_norm, na_rpb),
                 _pack_replicated(m_norm_mix, m_norm_mlp, m_norm_final, m_hg_norm, m_na_rpb),
                 _pack_replicated(v_norm_mix, v_norm_mlp, v_norm_final, v_hg_norm, v_na_rpb), "adamw_replicated")
    for q in range(4):
        um = _unpack_replicated(rep[q])
        for nm, val in zip(("norm_mix", "norm_mlp", "norm_final", "hg_norm", "na_rpb"), um):
            res.setdefault(nm, [None] * 4)[q] = val
    (win_r,) = _exchange_wait(sc_in, [True], [rep[1], own[1]], "scatter_in_wait")
    update("w_in", win_r, w_in, m_w_in, v_w_in)

    loss = lax.psum(loss_part[0, 0], ("x", "y", "c"))
    grad_x = dh0[NM:L][None]
    order = ("meta_tokens", "w_in", "w_na_out", "w_hg_out", "w_o", "w_up", "w_down", "norm_mix", "norm_mlp", "norm_final",
             "hg_norm", "na_rpb", "hg_lb_logits")
    outs = [loss, grad_x]
    for q in range(4):
        outs += [res[nm][q] for nm in order]
    return tuple(outs)
```

```python
import functools

import numpy as np
import jax
import jax.numpy as jnp
from jax import lax
from jax.experimental import pallas as pl
from jax.experimental.pallas import tpu as pltpu

F32 = jnp.float32
BF16 = jnp.bfloat16

D = 1024
SEQ = 2048
NM = 16
L = SEQ + NM
T = 2176
NDEV = 8
EPS = 1e-6
GRID_W = 64
ROWS = SEQ // GRID_W
NA_HEADS = 8
NA_DH = 64
NA_SCALE = NA_DH ** -0.5
HG_HEADS = 4
HG_C = 16
NCHUNK = L // HG_C
D_FF = 4096
IN_COLS = 6144
NEG = -1e30

ADAM_LR = 0.001
ADAM_B1 = 0.9
ADAM_B2 = 0.999
ADAM_EPS = 1e-08
ADAM_WD = 0.01
ADAM_STEP = 10

MESH_ID = pl.DeviceIdType.MESH
ANY = pl.BlockSpec(memory_space=pl.ANY)

NN = (((1,), (0,)), ((), ()))
NT = (((1,), (1,)), ((), ()))
TN = (((0,), (0,)), ((), ()))


def _cp(sem=None, vmem_mb=48):
    return pltpu.CompilerParams(dimension_semantics=sem, vmem_limit_bytes=vmem_mb * 1024 * 1024)


def _dot(a, b, dims=NN):
    return lax.dot_general(a, b, dims, preferred_element_type=F32)


def _sds(shape, dtype):
    return jax.ShapeDtypeStruct(shape, dtype)


HBM = pl.BlockSpec(memory_space=pltpu.HBM)
SEM = pl.BlockSpec(memory_space=pltpu.SEMAPHORE)
EFFECT = pltpu.SideEffectType.DATAFLOW_SIDE_EFFECTING


def _exchange(arrs, scatter, name):
    n = len(arrs)
    out_shapes = []
    for a, sc in zip(arrs, scatter):
        out_shapes.append(_sds(a.shape if sc else (NDEV,) + a.shape, a.dtype))

    def body(*refs):
        ins, outs = refs[:n], refs[n:2 * n]
        send_sems, recv_sems, loc_sems = refs[2 * n:]
        me = 4 * lax.axis_index("x") + 2 * lax.axis_index("y") + lax.axis_index("c")
        copies = []
        for k in range(n):
            src_me = ins[k].at[me] if scatter[k] else ins[k]
            loc = pltpu.make_async_copy(src_me, outs[k].at[me], loc_sems.at[k])
            loc.start()
            copies.append(loc)
        remote = _peer_copies(ins, outs, scatter, send_sems, recv_sems)
        for cp in remote:
            cp.start()
        for cp in remote:
            cp.wait_recv()
        for cp in remote:
            cp.wait_send()
        for cp in copies:
            cp.wait()

    return pl.pallas_call(
        body, name=name, out_shape=tuple(out_shapes), in_specs=[ANY] * n, out_specs=tuple([ANY] * n),
        scratch_shapes=[pltpu.SemaphoreType.DMA((n * (NDEV - 1),)), pltpu.SemaphoreType.DMA((n * (NDEV - 1),)),
                        pltpu.SemaphoreType.DMA((n,))],
    )(*arrs)


def _gather_two_level(arrs, name):
    n = len(arrs)

    def body(*refs):
        ins, outs = refs[:n], refs[n:2 * n]
        send_sems, recv_sems, loc_sems = refs[2 * n:]
        x, y, c = lax.axis_index("x"), lax.axis_index("y"), lax.axis_index("c")
        sib = (x, y, 1 - c)
        chips = [(1 - x, y), (x, 1 - y), (1 - x, 1 - y)]

        def slot(k, px, py, pc):
            return outs[k].at[4 * px + 2 * py + pc]

        def copy(k, q, block, to, src=None):
            return pltpu.make_async_remote_copy(
                src_ref=slot(k, *block) if src is None else src, dst_ref=slot(k, *block),
                send_sem=send_sems.at[7 * k + q], recv_sem=recv_sems.at[7 * k + q], device_id=to, device_id_type=MESH_ID)

        mine = [pltpu.make_async_copy(ins[k], slot(k, x, y, c), loc_sems.at[k]) for k in range(n)]
        for cp in mine:
            cp.start()
        first = []
        for k in range(n):
            first.append(copy(k, 0, (x, y, c), sib, src=ins[k]))
            first += [copy(k, 1 + j, (x, y, c), (*chip, c), src=ins[k]) for j, chip in enumerate(chips)]
        for cp in first:
            cp.start()
        passed = []
        for j, chip in enumerate(chips):
            for k in range(n):
                copy(k, 1 + j, (*chip, c), (x, y, c)).wait_recv()
                fw = copy(k, 4 + j, (*chip, c), sib)
                fw.start()
                passed.append(fw)
        for k in range(n):
            copy(k, 0, (x, y, 1 - c), (x, y, c)).wait_recv()
            for j, chip in enumerate(chips):
                copy(k, 4 + j, (*chip, 1 - c), (x, y, c)).wait_recv()
        for cp in first + passed:
            cp.wait_send()
        for cp in mine:
            cp.wait()

    return pl.pallas_call(
        body, name=name, out_shape=tuple(_sds((NDEV,) + a.shape, a.dtype) for a in arrs),
        in_specs=[ANY] * n, out_specs=tuple([ANY] * n),
        scratch_shapes=[pltpu.SemaphoreType.DMA((7 * n,)), pltpu.SemaphoreType.DMA((7 * n,)), pltpu.SemaphoreType.DMA((n,))],
    )(*arrs)


def _peer_copies(srcs, lands, scatter, send_sems, recv_sems):
    x, y, c = lax.axis_index("x"), lax.axis_index("y"), lax.axis_index("c")
    me = 4 * x + 2 * y + c
    out = []
    for k in range(len(srcs)):
        for m in range(1, NDEV):
            px, py, pc = x ^ (m >> 2), y ^ ((m >> 1) & 1), c ^ (m & 1)
            src = srcs[k].at[4 * px + 2 * py + pc] if scatter[k] else srcs[k]
            out.append(pltpu.make_async_remote_copy(
                src_ref=src, dst_ref=lands[k].at[me], send_sem=send_sems.at[k * (NDEV - 1) + m - 1],
                recv_sem=recv_sems.at[k * (NDEV - 1) + m - 1],
                device_id=(px, py, pc), device_id_type=MESH_ID))
    return out


def _exchange_start(arrs, scatter, name):
    n = len(arrs)
    me = 4 * lax.axis_index("x") + 2 * lax.axis_index("y") + lax.axis_index("c")
    lands = []
    for a, sc in zip(arrs, scatter):
        own = lax.dynamic_index_in_dim(a, me, 0, keepdims=True) if sc else a[None]
        shape = a.shape if sc else (NDEV,) + a.shape
        lands.append(lax.dynamic_update_index_in_dim(lax.empty(shape, a.dtype), own, me, 0))

    def body(*refs):
        srcs, lnds = refs[:n], refs[n:2 * n]
        send_sems, recv_sems = refs[2 * n], refs[2 * n + 1]
        token = refs[-1]
        for cp in _peer_copies(srcs, lnds, scatter, send_sems, recv_sems):
            cp.start()
        token[...] = jnp.zeros_like(token)

    ops = [pltpu.with_memory_space_constraint(a, pltpu.HBM) for a in list(arrs) + lands]
    res = pl.pallas_call(
        body, name=name,
        out_shape=(pltpu.SemaphoreType.DMA((n * (NDEV - 1),)), pltpu.SemaphoreType.DMA((n * (NDEV - 1),)))
        + tuple(pltpu.HBM(o.shape, o.dtype) for o in ops) + (_sds((8, 128), F32),),
        in_specs=[HBM] * (2 * n), out_specs=(SEM, SEM) + (HBM,) * (2 * n) + (pl.BlockSpec(memory_space=pltpu.VMEM),),
        input_output_aliases={k: 2 + k for k in range(2 * n)},
        compiler_params=pltpu.CompilerParams(has_side_effects=EFFECT),
    )(*ops)
    return res[:-1], res[-1]


def _exchange_wait(handle, scatter, after, name):
    send_sems, recv_sems = handle[0], handle[1]
    bufs = handle[2:]
    n = len(bufs) // 2
    after = list(after)

    def body(*refs):
        srcs, lnds = refs[:n], refs[n:2 * n]
        for cp in _peer_copies(srcs, lnds, scatter, refs[2 * n], refs[2 * n + 1]):
            cp.wait_send()
            cp.wait_recv()

    res = pl.pallas_call(
        body, name=name, out_shape=tuple(pltpu.HBM(b.shape, b.dtype) for b in bufs),
        in_specs=[HBM] * (2 * n) + [SEM, SEM] + [ANY] * len(after), out_specs=(HBM,) * (2 * n),
        input_output_aliases={k: k for k in range(2 * n)},
        compiler_params=pltpu.CompilerParams(has_side_effects=EFFECT),
    )(*bufs, send_sems, recv_sems, *after)
    return res[n:]


def _owner_copies(src, land, owners, send_sems, recv_sems):
    me = 4 * lax.axis_index("x") + 2 * lax.axis_index("y") + lax.axis_index("c")
    out = []
    for k, o in enumerate(owners):
        cp = pltpu.make_async_remote_copy(
            src_ref=src.at[k], dst_ref=land.at[me], send_sem=send_sems.at[k], recv_sem=recv_sems.at[me],
            device_id=(o >> 2, (o >> 1) & 1, o & 1), device_id_type=MESH_ID)
        out.append((cp, me != o))
    return out


def _owner_scatter_start(arr, owners, name):
    me = 4 * lax.axis_index("x") + 2 * lax.axis_index("y") + lax.axis_index("c")
    k_me = jnp.clip(me - owners[0], 0, len(owners) - 1)
    own = lax.dynamic_index_in_dim(arr, k_me, 0, keepdims=True)
    land = lax.dynamic_update_index_in_dim(lax.empty((NDEV,) + arr.shape[1:], arr.dtype), own, me, 0)

    def body(src, lnd, send_sems, recv_sems, src_thru, land_thru, token):
        for cp, mine in _owner_copies(src, lnd, owners, send_sems, recv_sems):
            @pl.when(mine)
            def _():
                cp.start()
        token[...] = jnp.zeros_like(token)

    ops = [pltpu.with_memory_space_constraint(a, pltpu.HBM) for a in (arr, land)]
    res = pl.pallas_call(
        body, name=name,
        out_shape=(pltpu.SemaphoreType.DMA((len(owners),)), pltpu.SemaphoreType.DMA((NDEV,)))
        + tuple(pltpu.HBM(o.shape, o.dtype) for o in ops) + (_sds((8, 128), F32),),
        in_specs=[HBM] * 2, out_specs=(SEM, SEM, HBM, HBM, pl.BlockSpec(memory_space=pltpu.VMEM)),
        input_output_aliases={0: 2, 1: 3}, compiler_params=pltpu.CompilerParams(has_side_effects=EFFECT),
    )(*ops)
    return res[:-1], res[-1]


def _owner_scatter_wait(handle, owners, after, name):
    send_sems, recv_sems, src_thru, land_thru = handle
    after = list(after)

    def body(src, lnd, send_sems, recv_sems, *rest):
        me = 4 * lax.axis_index("x") + 2 * lax.axis_index("y") + lax.axis_index("c")
        for cp, mine in _owner_copies(src, lnd, owners, send_sems, recv_sems):
            @pl.when(mine)
            def _():
                cp.wait_send()
        is_owner = functools.reduce(jnp.logical_or, [me == o for o in owners])
        for s in range(NDEV):
            arrival = pltpu.make_async_remote_copy(
                src_ref=src.at[0], dst_ref=lnd.at[s], send_sem=send_sems.at[0], recv_sem=recv_sems.at[s],
                device_id=(lax.axis_index("x"), lax.axis_index("y"), lax.axis_index("c")), device_id_type=MESH_ID)

            @pl.when(is_owner & (me != s))
            def _():
                arrival.wait_recv()

    res = pl.pallas_call(
        body, name=name, out_shape=(pltpu.HBM(src_thru.shape, src_thru.dtype), pltpu.HBM(land_thru.shape, land_thru.dtype)),
        in_specs=[HBM, HBM, SEM, SEM] + [ANY] * len(after), out_specs=(HBM, HBM), input_output_aliases={0: 0, 1: 1},
        compiler_params=pltpu.CompilerParams(has_side_effects=EFFECT),
    )(src_thru, land_thru, send_sems, recv_sems, *after)
    return res[1]


def _tie(x, token, name):
    def body(x_ref, t_ref, o_ref):
        del x_ref, t_ref, o_ref

    return pl.pallas_call(body, name=name, out_shape=_sds(x.shape, x.dtype), in_specs=[ANY, ANY], out_specs=ANY,
                          input_output_aliases={0: 0})(x, token)


TM_E = 272


def _norm_fwd(h, g, name):
    def body(h_ref, g_ref, o_ref):
        xv = h_ref[...]
        r = lax.rsqrt(jnp.mean(xv * xv, axis=-1, keepdims=True) + EPS)
        o_ref[...] = (xv * r * g_ref[...]).astype(BF16)

    return pl.pallas_call(
        body, name=name, grid=(T // TM_E,),
        in_specs=[pl.BlockSpec((TM_E, D), lambda i: (i, 0)), pl.BlockSpec((1, D), lambda i: (0, 0))],
        out_specs=pl.BlockSpec((TM_E, D), lambda i: (i, 0)), out_shape=_sds((T, D), BF16),
        compiler_params=_cp(("parallel",)))(h, g)


def _norm_fwd_t(h, g, name):
    def body(h_ref, g_ref, o_ref, ot_ref):
        xv = h_ref[...]
        r = lax.rsqrt(jnp.mean(xv * xv, axis=-1, keepdims=True) + EPS)
        y = xv * r * g_ref[...]
        o_ref[...] = y.astype(BF16)
        ot_ref[...] = y.T.astype(BF16)

    return pl.pallas_call(
        body, name=name, grid=(T // 128,),
        in_specs=[pl.BlockSpec((128, D), lambda i: (i, 0)), pl.BlockSpec((1, D), lambda i: (0, 0))],
        out_specs=(pl.BlockSpec((128, D), lambda i: (i, 0)), pl.BlockSpec((D, 128), lambda i: (0, i))),
        out_shape=(_sds((T, D), BF16), _sds((D, T), BF16)), compiler_params=_cp(("parallel",)))(h, g)


def _norm_bwd(h, g, dn, dres, name):
    def body(h_ref, g_ref, dn_ref, dres_ref, dh_ref, dhb_ref, dg_ref):
        i = pl.program_id(0)
        xv = h_ref[...]
        r = lax.rsqrt(jnp.mean(xv * xv, axis=-1, keepdims=True) + EPS)
        xh = xv * r
        dnv = dn_ref[...].astype(F32)
        dxh = dnv * g_ref[...]
        dh = dres_ref[...] + r * (dxh - xh * jnp.mean(dxh * xh, axis=-1, keepdims=True))
        dh_ref[...] = dh
        dhb_ref[...] = dh.astype(BF16)
        part = jnp.sum(dnv * xh, axis=0, keepdims=True)

        @pl.when(i == 0)
        def _():
            dg_ref[...] = part

        @pl.when(i > 0)
        def _():
            dg_ref[...] += part

    blk = pl.BlockSpec((TM_E, D), lambda i: (i, 0))
    vec = pl.BlockSpec((1, D), lambda i: (0, 0))
    return pl.pallas_call(
        body, name=name, grid=(T // TM_E,), in_specs=[blk, vec, blk, blk], out_specs=(blk, blk, vec),
        out_shape=(_sds((T, D), F32), _sds((T, D), BF16), _sds((1, D), F32)),
        compiler_params=_cp(("arbitrary",)))(h, g, dn, dres)


TM_MM = 1088


def _inproj_fwd(a, w_g):
    nb = w_g.shape[2]

    def body(a_ref, w_ref, o_ref):
        o_ref[...] = _dot(a_ref[...], w_ref[0])

    return pl.pallas_call(
        body, name="inproj_fwd", grid=(T // TM_MM, NDEV),
        in_specs=[pl.BlockSpec((TM_MM, D), lambda i, j: (i, 0)), pl.BlockSpec((1, D, nb), lambda i, j: (j, 0, 0))],
        out_specs=pl.BlockSpec((TM_MM, nb), lambda i, j: (i, j)), out_shape=_sds((T, NDEV * nb), F32),
        compiler_params=_cp(("parallel", "parallel")))(a, w_g)


TM_B = 544


W_IN_B = IN_COLS // NDEV
OWNERS_B = (0, 1)
OWNERS_A = (2, 3, 4, 5, 6, 7)


def _inproj_bwd_dw(a_t, dp, name):
    nblk = dp.shape[1] // W_IN_B

    def body(at_ref, dp_ref, dw_ref):
        dw_ref[0] = _dot(at_ref[...], dp_ref[...]).astype(BF16)

    return pl.pallas_call(
        body, name=name, grid=(nblk,),
        in_specs=[pl.BlockSpec((D, T), lambda j: (0, 0)), pl.BlockSpec((T, W_IN_B), lambda j: (0, j))],
        out_specs=pl.BlockSpec((1, D, W_IN_B), lambda j: (j, 0, 0)), out_shape=_sds((nblk, D, W_IN_B), BF16),
        compiler_params=_cp(("parallel",)))(a_t, dp)


def _inproj_bwd_da(dp, w_g, first_blk, da_prev, name):
    nblk = dp.shape[1] // W_IN_B

    def body(dp_ref, w_ref, *rest):
        da_ref = rest[-1]
        j = pl.program_id(1)
        dav = _dot(dp_ref[...], w_ref[0], NT)

        @pl.when(j == 0)
        def _():
            da_ref[...] = dav if da_prev is None else rest[0][...] + dav

        @pl.when(j > 0)
        def _():
            da_ref[...] += dav

    rblk = pl.BlockSpec((TM_MM, D), lambda i, j: (i, 0))
    return pl.pallas_call(
        body, name=name, grid=(T // TM_MM, nblk),
        in_specs=[pl.BlockSpec((TM_MM, W_IN_B), lambda i, j: (i, j)),
                  pl.BlockSpec((1, D, W_IN_B), lambda i, j: (first_blk + j, 0, 0))] + ([] if da_prev is None else [rblk]),
        out_specs=rblk, out_shape=_sds((T, D), F32),
        compiler_params=_cp(("parallel", "arbitrary")))(dp, w_g, *([] if da_prev is None else [da_prev]))


NA_QB = 256
NA_GROUPS = ROWS // 4
NA_UROWS = 11
NA_KW = NA_UROWS * GRID_W
NA_KU = 768


def _na_row_offset(var, i, j):
    valid = (j < 8, i <= j < i + 8, 3 <= j < NA_UROWS)[var]
    return (j - i + (7, 3, 0)[var]) if valid else None


def _na_bias_table(rpb):
    def body(r_ref, o_ref):
        row3 = lax.broadcasted_iota(jnp.int32, (15, GRID_W, 128), 1)
        lane3 = lax.broadcasted_iota(jnp.int32, (15, GRID_W, 128), 2)
        w3 = lane3 & (GRID_W - 1)
        cs3 = jnp.clip(row3 - 8, 0, GRID_W - 16)
        lane = lax.broadcasted_iota(jnp.int32, (GRID_W, 128), 1)
        neg = jnp.full((GRID_W, 128), NEG, F32)
        z = jnp.stack([jnp.broadcast_to(r_ref[0, a:a + 1, :], (GRID_W, 128)) for a in range(15)])
        for bit in range(6):
            sh = 1 << bit
            z = jnp.where((row3 & sh) != 0, jnp.roll(z, sh, axis=2), z)
        z = jnp.roll(z, 128 - 15, axis=2)
        z = jnp.where(lane3 < GRID_W, z, 0.0)
        z = z + jnp.roll(z, GRID_W, axis=2)
        tabs = jnp.where((w3 >= cs3) & (w3 < cs3 + 16), z, NEG)
        tail = jnp.where(lane < GRID_W + NM, 0.0, NEG)
        for var in range(3):
            for i in range(4):
                for jp in range(NA_KU // 128):
                    halves = []
                    for j in (2 * jp, 2 * jp + 1):
                        a = _na_row_offset(var, i, j) if j < NA_UROWS else None
                        halves.append(tail if j >= NA_UROWS else (neg if a is None else tabs[a]))
                    o_ref[var, 0, i * 64:(i + 1) * 64, jp * 128:(jp + 1) * 128] = jnp.where(lane < GRID_W, halves[0], halves[1])

    rp = jnp.concatenate([rpb, jnp.zeros((NA_HEADS, 15, 128 - 31), F32)], axis=2)
    return pl.pallas_call(
        body, name="na_bias_table", grid=(NA_HEADS,),
        in_specs=[pl.BlockSpec((1, 15, 128), lambda h: (h, 0, 0))],
        out_specs=pl.BlockSpec((3, 1, NA_QB, NA_KU), lambda h: (0, h, 0, 0)),
        out_shape=_sds((3, NA_HEADS, NA_QB, NA_KU), F32), compiler_params=_cp(("parallel",)))(rp)


def _na_var(g):
    return jnp.where(g == 0, 0, jnp.where(g == NA_GROUPS - 1, 2, 1))


def _na_load_window(src_ref, dst, g):
    us = jnp.clip(4 * g - 4, 0, ROWS - NA_UROWS)
    kstart = pl.multiple_of(NM + GRID_W * us, 16)
    dst[0:NA_KW, :] = src_ref[pl.ds(kstart, NA_KW), :].astype(BF16)
    dst[NA_KW:NA_KW + NM, :] = src_ref[0:NM, :].astype(BF16)
    dst[NA_KW + NM:, :] = jnp.zeros((NA_KU - NA_KW - NM, 128), BF16)
    return kstart


def _na_fwd(p_act, bias_tab):
    def body(q_ref, k_ref, v_ref, b_ref, o_ref, lse_ref, ku, vu):
        g = pl.program_id(1)
        _na_load_window(k_ref, ku, g)
        _na_load_window(v_ref, vu, g)
        qstart = pl.multiple_of(NM + NA_QB * g, 16)
        q = q_ref[pl.ds(qstart, NA_QB), :]
        lane = lax.broadcasted_iota(jnp.int32, (NA_QB, 128), 1)
        o_h, lse_h = [], []
        for h in range(2):
            hm = (lane < 64) if h == 0 else (lane >= 64)
            qm = jnp.where(hm, q, 0.0).astype(BF16)
            s = _dot(qm, ku[...], NT) * NA_SCALE + b_ref[0, h]
            m = jnp.max(s, axis=-1, keepdims=True)
            p = jnp.exp(s - m)
            l = jnp.sum(p, axis=-1, keepdims=True)
            o_h.append(_dot(p.astype(BF16), vu[...]) / l)
            lse_h.append(jnp.broadcast_to(m + jnp.log(l), (NA_QB, 128)))
        o_ref[pl.ds(qstart, NA_QB), :] = jnp.where(lane < 64, o_h[0], o_h[1]).astype(BF16)
        lse_ref[0, pl.ds(qstart, NA_QB), :] = jnp.where(lane < 64, lse_h[0], lse_h[1])

        @pl.when(g == 0)
        def _():
            qm_ = q_ref[0:NM, :]
            lane_m = lax.broadcasted_iota(jnp.int32, (NM, 128), 1)
            km, vm = ku[NA_KW:NA_KW + NM, :], vu[NA_KW:NA_KW + NM, :]
            om = []
            for h in range(2):
                hm = (lane_m < 64) if h == 0 else (lane_m >= 64)
                s = _dot(jnp.where(hm, qm_, 0.0).astype(BF16), km, NT) * NA_SCALE
                p = jnp.exp(s - jnp.max(s, axis=-1, keepdims=True))
                l = jnp.sum(p, axis=-1, keepdims=True)
                om.append(_dot(p.astype(BF16), vm) / l)
            o_ref[0:NM, :] = jnp.where(lane_m < 64, om[0], om[1]).astype(BF16)
            o_ref[L:T, :] = jnp.zeros((T - L, 128), BF16)
            lse_ref[0, 0:NM, :] = jnp.zeros((NM, 128), F32)
            lse_ref[0, L:T, :] = jnp.zeros((T - L, 128), F32)

    col = lambda off: pl.BlockSpec((T, 128), lambda hp, g: (0, off + hp))
    return pl.pallas_call(
        body, name="na_fwd", grid=(4, NA_GROUPS),
        in_specs=[col(0), col(4), col(8),
                  pl.BlockSpec((1, 2, NA_QB, NA_KU), lambda hp, g: (_na_var(g), hp, 0, 0))],
        out_specs=(pl.BlockSpec((T, 128), lambda hp, g: (0, hp)), pl.BlockSpec((1, T, 128), lambda hp, g: (hp, 0, 0))),
        out_shape=(_sds((T, 512), BF16), _sds((4, T, 128), F32)),
        scratch_shapes=[pltpu.VMEM((NA_KU, 128), BF16), pltpu.VMEM((NA_KU, 128), BF16)],
        compiler_params=_cp(("parallel", "arbitrary")))(p_act, p_act, p_act, bias_tab)


def _na_bwd(p_act, do, lse, bias_tab):
    def body(q_ref, k_ref, v_ref, do_ref, lse_ref, b_ref, dq_ref, dk_ref, dv_ref, db_ref, ku, vu):
        g = pl.program_id(1)

        @pl.when(g == 0)
        def _():
            dq_ref[...] = jnp.zeros((T, 128), F32)
            dk_ref[...] = jnp.zeros((T, 128), F32)
            dv_ref[...] = jnp.zeros((T, 128), F32)

        kstart = _na_load_window(k_ref, ku, g)
        _na_load_window(v_ref, vu, g)
        qstart = pl.multiple_of(NM + NA_QB * g, 16)
        q = q_ref[pl.ds(qstart, NA_QB), :]
        dov = do_ref[pl.ds(qstart, NA_QB), :]
        lsev = lse_ref[0, pl.ds(qstart, NA_QB), :]
        lane = lax.broadcasted_iota(jnp.int32, (NA_QB, 128), 1)
        first = (g == 0) | (g == 1) | (g == NA_GROUPS - 1)
        dq_h = []
        dku = jnp.zeros((NA_KU, 128), F32)
        dvu = jnp.zeros((NA_KU, 128), F32)
        for h in range(2):
            hm = (lane < 64) if h == 0 else (lane >= 64)
            qm = jnp.where(hm, q, 0.0).astype(BF16)
            dom = jnp.where(hm, dov, 0.0).astype(BF16)
            s = _dot(qm, ku[...], NT) * NA_SCALE + b_ref[0, h]
            p = jnp.exp(s - lsev[:, 64 * h:64 * h + 1])
            dp = _dot(dom, vu[...], NT)
            delta = jnp.sum(p * dp, axis=-1, keepdims=True)
            ds = p * (dp - delta)

            @pl.when(first)
            def _():
                db_ref[0, h] = ds

            @pl.when(jnp.logical_not(first))
            def _():
                db_ref[0, h] += ds

            dsb = (ds * NA_SCALE).astype(BF16)
            dq_h.append(_dot(dsb, ku[...]))
            dku = dku + _dot(dsb, qm, TN)
            dvu = dvu + _dot(p.astype(BF16), dom, TN)
        dq_ref[pl.ds(qstart, NA_QB), :] = jnp.where(lane < 64, dq_h[0], dq_h[1])
        dk_ref[pl.ds(kstart, NA_KW), :] += dku[0:NA_KW]
        dv_ref[pl.ds(kstart, NA_KW), :] += dvu[0:NA_KW]
        dk_ref[0:NM, :] += dku[NA_KW:NA_KW + NM]
        dv_ref[0:NM, :] += dvu[NA_KW:NA_KW + NM]

        @pl.when(g == 0)
        def _():
            qm_ = q_ref[0:NM, :]
            dom_ = do_ref[0:NM, :]
            lane_m = lax.broadcasted_iota(jnp.int32, (NM, 128), 1)
            km, vm = ku[NA_KW:NA_KW + NM, :], vu[NA_KW:NA_KW + NM, :]
            dqs = []
            dkm = jnp.zeros((NM, 128), F32)
            dvm = jnp.zeros((NM, 128), F32)
            for h in range(2):
                hm = (lane_m < 64) if h == 0 else (lane_m >= 64)
                qh = jnp.where(hm, qm_, 0.0).astype(BF16)
                doh = jnp.where(hm, dom_, 0.0).astype(BF16)
                s = _dot(qh, km, NT) * NA_SCALE
                e = jnp.exp(s - jnp.max(s, axis=-1, keepdims=True))
                p = e / jnp.sum(e, axis=-1, keepdims=True)
                dp = _dot(doh, vm, NT)
                ds = p * (dp - jnp.sum(p * dp, axis=-1, keepdims=True))
                dsb = (ds * NA_SCALE).astype(BF16)
                dqs.append(_dot(dsb, km))
                dkm = dkm + _dot(dsb, qh, TN)
                dvm = dvm + _dot(p.astype(BF16), doh, TN)
            dq_ref[0:NM, :] = jnp.where(lane_m < 64, dqs[0], dqs[1])
            dk_ref[0:NM, :] += dkm
            dv_ref[0:NM, :] += dvm

    col = lambda off: pl.BlockSpec((T, 128), lambda hp, g: (0, off + hp))
    ocol = pl.BlockSpec((T, 128), lambda hp, g: (0, hp))
    bspec = pl.BlockSpec((1, 2, NA_QB, NA_KU), lambda hp, g: (_na_var(g), hp, 0, 0))
    return pl.pallas_call(
        body, name="na_bwd", grid=(4, NA_GROUPS),
        in_specs=[col(0), col(4), col(8), ocol, pl.BlockSpec((1, T, 128), lambda hp, g: (hp, 0, 0)), bspec],
        out_specs=(ocol, ocol, ocol, bspec),
        out_shape=(_sds((T, 512), F32), _sds((T, 512), F32), _sds((T, 512), F32), _sds((3, NA_HEADS, NA_QB, NA_KU), F32)),
        scratch_shapes=[pltpu.VMEM((NA_KU, 128), BF16), pltpu.VMEM((NA_KU, 128), BF16)],
        compiler_params=_cp(("parallel", "arbitrary")))(p_act, p_act, p_act, do, lse, bias_tab)


def _na_rpb_reduce(dbias):
    def body(db_ref, o_ref):
        lane = lax.broadcasted_iota(jnp.int32, (GRID_W, 128), 1)
        row3 = lax.broadcasted_iota(jnp.int32, (15, GRID_W, 128), 1)
        lane3 = lax.broadcasted_iota(jnp.int32, (15, GRID_W, 128), 2)
        accs = []
        for a in range(15):
            acc = jnp.zeros((GRID_W, 128), F32)
            for var in range(3):
                for i in range(4):
                    for j in range(NA_UROWS):
                        if _na_row_offset(var, i, j) == a:
                            pair = db_ref[var, 0, i * 64:(i + 1) * 64, (j // 2) * 128:(j // 2 + 1) * 128]
                            acc = acc + jnp.where((lane < GRID_W) if j % 2 == 0 else (lane >= GRID_W), pair, 0.0)
            accs.append(acc)
        z = jnp.stack(accs)
        z = jnp.where(lane3 < GRID_W, z + jnp.roll(z, GRID_W, axis=2), 0.0)
        for bit in range(6):
            sh = 1 << bit
            z = jnp.where((row3 & sh) != 0, jnp.roll(z, 128 - sh, axis=2), z)
        z = jnp.roll(z, 15, axis=2)
        o_ref[0] = jnp.sum(z, axis=1)

    return pl.pallas_call(
        body, name="na_rpb_reduce", grid=(NA_HEADS,),
        in_specs=[pl.BlockSpec((3, 1, NA_QB, NA_KU), lambda h: (0, h, 0, 0))],
        out_specs=pl.BlockSpec((1, 15, 128), lambda h: (h, 0, 0)), out_shape=_sds((NA_HEADS, 15, 128), F32),
        compiler_params=_cp(("parallel",)))(dbias)


HG_RB = 128
HG_NB = T // HG_RB
HG_SLOTS = HG_NB * 8
HI = lax.Precision.HIGHEST
HG_UNROLL = 4


def _chunk_tri(lower):
    r = lax.broadcasted_iota(jnp.int32, (HG_RB, HG_RB), 0)
    c = lax.broadcasted_iota(jnp.int32, (HG_RB, HG_RB), 1)
    same = (r // HG_C) == (c // HG_C)
    keep = (c <= r) if lower else (c >= r)
    return jnp.where(same & keep, 1.0, 0.0).astype(F32)


def _hg_gate_terms(z, lg):
    dl = lg[0:1, :] - lg[1:2, :]
    log_lb = jax.nn.log_sigmoid(dl)
    log_1mlb = jax.nn.log_sigmoid(-dl)
    yz = log_1mlb + jax.nn.log_sigmoid(z)
    log_f = jnp.logaddexp(log_lb, yz)
    snz = jax.nn.sigmoid(-z)
    k = jnp.exp(log_1mlb) * snz
    w2 = jnp.exp(yz - log_f)
    return log_f, k, snz, w2


def _hg_pre(p_act, logits):
    def body(q_ref, zf_ref, zb_ref, lg_ref, qh_ref, kf_ref, bf_ref, kb_ref, bb_ref):
        qh_ref[...] = jax.nn.silu(q_ref[...])
        lf, kf, _, _ = _hg_gate_terms(zf_ref[...], lg_ref[0])
        kf_ref[...] = kf
        bf_ref[...] = jnp.dot(_chunk_tri(True), lf, precision=HI, preferred_element_type=F32)
        lb_, kb, _, _ = _hg_gate_terms(zb_ref[...], lg_ref[1])
        kb_ref[...] = kb
        bb_ref[...] = jnp.dot(_chunk_tri(False), lb_, precision=HI, preferred_element_type=F32)

    blk = lambda c: pl.BlockSpec((HG_RB, 512), lambda i: (i, c))
    ob = pl.BlockSpec((HG_RB, 512), lambda i: (i, 0))
    return pl.pallas_call(
        body, name="hg_pre", grid=(HG_NB,),
        in_specs=[blk(3), blk(4), blk(5), pl.BlockSpec((2, 2, 512), lambda i: (0, 0, 0))],
        out_specs=(ob,) * 5, out_shape=(_sds((T, 512), F32),) * 5,
        compiler_params=_cp(("parallel",)))(p_act, p_act, p_act, logits)


def _bdot(a, b, ca, cb):
    return lax.dot_general(a.astype(BF16), b.astype(BF16), (((ca,), (cb,)), ((0,), (0,))), preferred_element_type=F32)


HG_S = 8
HG_NS = HG_RB // HG_S


def _lane_sums(xs):
    l_io = lax.broadcasted_iota(jnp.int32, (HG_NS, HG_S, HG_S), 2)
    a = jnp.zeros((HG_NS, HG_S, HG_S), F32)
    for j, x in enumerate(xs):
        a = a + jnp.where(l_io == j, jnp.sum(x, axis=-1, keepdims=True), 0.0)
    return a


def _halves(x):
    y = x.reshape(8, 2, HG_S, x.shape[-1])
    return y[:, 0], y[:, 1]


def _join(first, second):
    return jnp.stack([first, second], axis=1).reshape(HG_RB, first.shape[-1])


def _cross_split(rev, b4):
    b_1, b_2 = _halves(b4)
    if rev:
        r = b_2[:, 0:1, :]
        return jnp.exp(b_1 - r), jnp.exp(r - b_2)
    r = b_1[:, HG_S - 1:HG_S, :]
    return jnp.exp(b_2 - r), jnp.exp(r - b_1)


def _hg_scan_fwd(qh, k, b, p_act, rev):
    anchor = 0 if rev else HG_C - 1

    def body(q_ref, k_ref, b_ref, v_ref, o_ref, st_ref, dsc):
        def phase_a(blk, _):
            rows = pl.ds(pl.multiple_of(blk * HG_RB, HG_RB), HG_RB)
            b3 = b_ref[rows, :].reshape(8, HG_C, 128)
            k3 = k_ref[rows, :].reshape(8, HG_C, 128)
            v3 = v_ref[rows, :].reshape(8, HG_C, 128)
            bl = b3[:, anchor:anchor + 1, :]
            kt = k3 * jnp.exp(bl - b3)
            st_ref[0, pl.ds(pl.multiple_of(blk * 8, 8), 8)] = _bdot(v3, kt, 1, 1)
            dsc[pl.ds(pl.multiple_of(blk * 8, 8), 8), :] = jnp.exp(bl[:, 0, :])
            return 0

        lax.fori_loop(0, HG_NB, phase_a, 0, unroll=HG_UNROLL)

        def phase_b(n, carry):
            c = (NCHUNK - 1 - n) if rev else n
            u = st_ref[0, c]
            st_ref[0, c] = carry
            return carry * dsc[pl.ds(c, 1), :] + u

        lax.fori_loop(0, NCHUNK, phase_b, jnp.zeros((128, 128), F32))
        for c in range(NCHUNK, HG_SLOTS):
            st_ref[0, c] = jnp.zeros((128, 128), F32)

        t_io = lax.broadcasted_iota(jnp.int32, (HG_NS, HG_S, 128), 1)

        def phase_c(blk, _):
            rows = pl.ds(pl.multiple_of(blk * HG_RB, HG_RB), HG_RB)
            b4 = b_ref[rows, :].reshape(HG_NS, HG_S, 128)
            k4 = k_ref[rows, :].reshape(HG_NS, HG_S, 128)
            q4 = q_ref[rows, :].reshape(HG_NS, HG_S, 128)
            v4 = v_ref[rows, :].reshape(HG_NS, HG_S, 128)
            st = st_ref[0, pl.ds(pl.multiple_of(blk * 8, 8), 8)]
            o = _bdot((q4 * jnp.exp(b4)).reshape(8, HG_C, 128), st, 2, 2).reshape(HG_RB, 128)
            terms = []
            for s in range(HG_S):
                ok = (t_io <= s) if rev else (t_io >= s)
                f = jnp.exp(jnp.where(ok, b4 - b4[:, s:s + 1, :], NEG))
                terms.append(q4 * f * k4[:, s:s + 1, :])
            o_in = _bdot(_lane_sums(terms), v4, 2, 1)
            wq, wk = _cross_split(rev, b4)
            q_1, q_2 = _halves(q4)
            k_1, k_2 = _halves(k4)
            v_1, v_2 = _halves(v4)
            o_1, o_2 = _halves(o_in)
            if rev:
                o_1 = o_1 + _bdot(_bdot(q_1 * wq, k_2 * wk, 2, 2), v_2, 2, 1)
            else:
                o_2 = o_2 + _bdot(_bdot(q_2 * wq, k_1 * wk, 2, 2), v_1, 2, 1)
            o_ref[rows, :] = o + _join(o_1, o_2)
            return 0

        lax.fori_loop(0, HG_NB, phase_c, 0, unroll=HG_UNROLL)

    col = pl.BlockSpec((T, 128), lambda h: (0, h))
    return pl.pallas_call(
        body, name="hg_scan_bwd_dir" if rev else "hg_scan_fwd_dir", grid=(HG_HEADS,),
        in_specs=[col, col, col, pl.BlockSpec((T, 128), lambda h: (0, 24 + h))],
        out_specs=(col, pl.BlockSpec((1, HG_SLOTS, 128, 128), lambda h: (h, 0, 0, 0))),
        out_shape=(_sds((T, 512), F32), _sds((HG_HEADS, HG_SLOTS, 128, 128), F32)),
        scratch_shapes=[pltpu.VMEM((HG_SLOTS, 128), F32)],
        compiler_params=_cp(("parallel",), 56))(qh, k, b, p_act)


def _hg_scan_bwd(qh, k, b, p_act, st, do, rev):
    anchor = 0 if rev else HG_C - 1

    def body(q_ref, k_ref, b_ref, v_ref, st_ref, do_ref, dq_ref, dk_ref, db_ref, dv_ref, gst, dsc, dbl):
        def phase_a(blk, _):
            rows = pl.ds(pl.multiple_of(blk * HG_RB, HG_RB), HG_RB)
            b3 = b_ref[rows, :].reshape(8, HG_C, 128)
            q3 = q_ref[rows, :].reshape(8, HG_C, 128)
            do3 = do_ref[rows, :].reshape(8, HG_C, 128)
            gst[pl.ds(pl.multiple_of(blk * 8, 8), 8)] = _bdot(do3, q3 * jnp.exp(b3), 1, 1)
            dsc[pl.ds(pl.multiple_of(blk * 8, 8), 8), :] = jnp.exp(b3[:, anchor, :])
            return 0

        lax.fori_loop(0, HG_NB, phase_a, 0, unroll=HG_UNROLL)

        def phase_b(n, carry):
            c = n if rev else (NCHUNK - 1 - n)
            w = gst[c]
            gst[c] = carry
            dcv = dsc[pl.ds(c, 1), :]
            dbl[pl.ds(c, 1), :] = dcv * jnp.sum(st_ref[0, c] * carry, axis=0, keepdims=True)
            return carry * dcv + w

        lax.fori_loop(0, NCHUNK, phase_b, jnp.zeros((128, 128), F32))
        for c in range(NCHUNK, HG_SLOTS):
            gst[c] = jnp.zeros((128, 128), F32)
            dbl[c:c + 1, :] = jnp.zeros((1, 128), F32)

        t_io = lax.broadcasted_iota(jnp.int32, (HG_NS, HG_S, 128), 1)
        t16 = lax.broadcasted_iota(jnp.int32, (8, HG_C, 128), 1)
        r_io = lax.broadcasted_iota(jnp.int32, (HG_NS, HG_S, HG_S), 1)
        l_io = lax.broadcasted_iota(jnp.int32, (HG_NS, HG_S, HG_S), 2)

        def phase_c(blk, _):
            rows = pl.ds(pl.multiple_of(blk * HG_RB, HG_RB), HG_RB)
            cs = pl.ds(pl.multiple_of(blk * 8, 8), 8)
            b4 = b_ref[rows, :].reshape(HG_NS, HG_S, 128)
            k4 = k_ref[rows, :].reshape(HG_NS, HG_S, 128)
            q4 = q_ref[rows, :].reshape(HG_NS, HG_S, 128)
            v4 = v_ref[rows, :].reshape(HG_NS, HG_S, 128)
            do4 = do_ref[rows, :].reshape(HG_NS, HG_S, 128)
            b3, k3, q3 = (z.reshape(8, HG_C, 128) for z in (b4, k4, q4))
            v3, do3 = v4.reshape(8, HG_C, 128), do4.reshape(8, HG_C, 128)
            s_t = st_ref[0, cs]
            g_t = gst[cs]
            bl = b3[:, anchor:anchor + 1, :]
            ekl = jnp.exp(bl - b3)
            kt = k3 * ekl
            dkt = _bdot(v3, g_t, 2, 1)
            dq = (_bdot(do3, s_t, 2, 1) * jnp.exp(b3)).reshape(HG_NS, HG_S, 128)
            dk = (dkt * ekl).reshape(HG_NS, HG_S, 128)
            dv = _bdot(kt, g_t, 2, 2).reshape(HG_NS, HG_S, 128)
            dbl3 = dbl[cs, :].reshape(8, 1, 128) + jnp.sum(dkt * kt, axis=1, keepdims=True)
            causal = (l_io >= r_io) if rev else (l_io <= r_io)
            da = jnp.where(causal, _bdot(do4, v4, 2, 2), 0.0)
            causal_t = (l_io <= r_io) if rev else (l_io >= r_io)
            dat = jnp.where(causal_t, _bdot(v4, do4, 2, 2), 0.0)
            for s in range(HG_S):
                ok = (t_io <= s) if rev else (t_io >= s)
                f = jnp.exp(jnp.where(ok, b4 - b4[:, s:s + 1, :], NEG))
                dq = dq + da[:, :, s:s + 1] * (f * k4[:, s:s + 1, :])
            terms = []
            for t in range(HG_S):
                ok = (t_io >= t) if rev else (t_io <= t)
                e = jnp.exp(jnp.where(ok, b4[:, t:t + 1, :] - b4, NEG))
                eq = e * q4[:, t:t + 1, :]
                dk = dk + dat[:, :, t:t + 1] * eq
                terms.append(eq * k4)
            dv = dv + _bdot(_lane_sums(terms), do4, 2, 1)
            wq, wk = _cross_split(rev, b4)
            pick = (lambda z: _halves(z)) if rev else (lambda z: _halves(z)[::-1])
            (q_q, _), (_, k_k), (_, v_k), (do_q, _) = pick(q4), pick(k4), pick(v4), pick(do4)
            qx, kx = q_q * wq, k_k * wk
            dq_q = _bdot(_bdot(do_q, v_k, 2, 2), kx, 2, 1) * wq
            dk_k = _bdot(_bdot(v_k, do_q, 2, 2), qx, 2, 1) * wk
            dv_k = _bdot(_bdot(kx, qx, 2, 2), do_q, 2, 1)
            zero = jnp.zeros((8, HG_S, 128), F32)
            place_q = (lambda z: _join(z, zero)) if rev else (lambda z: _join(zero, z))
            place_k = (lambda z: _join(zero, z)) if rev else (lambda z: _join(z, zero))
            dq2 = dq.reshape(HG_RB, 128) + place_q(dq_q)
            dk2 = dk.reshape(HG_RB, 128) + place_k(dk_k)
            dv2 = dv.reshape(HG_RB, 128) + place_k(dv_k)
            dq3, dk3 = dq2.reshape(8, HG_C, 128), dk2.reshape(8, HG_C, 128)
            db = q3 * dq3 - k3 * dk3 + jnp.where(t16 == anchor, dbl3, 0.0)
            dq_ref[rows, :] = dq2
            dk_ref[rows, :] = dk2
            db_ref[rows, :] = db.reshape(HG_RB, 128)
            dv_ref[rows, :] = dv2
            return 0

        lax.fori_loop(0, HG_NB, phase_c, 0, unroll=HG_UNROLL)

    col = pl.BlockSpec((T, 128), lambda h: (0, h))
    return pl.pallas_call(
        body, name="hg_scan_bwd_dir_bwd" if rev else "hg_scan_fwd_dir_bwd", grid=(HG_HEADS,),
        in_specs=[col, col, col, pl.BlockSpec((T, 128), lambda h: (0, 24 + h)),
                  pl.BlockSpec((1, HG_SLOTS, 128, 128), lambda h: (h, 0, 0, 0)), col],
        out_specs=(col,) * 4, out_shape=(_sds((T, 512), F32),) * 4,
        scratch_shapes=[pltpu.VMEM((HG_SLOTS, 128, 128), F32), pltpu.VMEM((HG_SLOTS, 128), F32),
                        pltpu.VMEM((HG_SLOTS, 128), F32)],
        compiler_params=_cp(("parallel",), 56))(qh, k, b, p_act, st, do)


def _row_valid(i, tm):
    r = lax.broadcasted_iota(jnp.int32, (tm, 1), 0) + i * tm
    return r < L


def _hg_post(o_f, o_b, p_act, gain):
    def body(of_ref, ob_ref, g_ref, gain_ref, u_ref):
        o = of_ref[...] + ob_ref[...]
        sg = jax.nn.silu(g_ref[...])
        parts = []
        for h in range(HG_HEADS):
            oh = o[:, 128 * h:128 * (h + 1)]
            parts.append(oh * lax.rsqrt(jnp.mean(oh * oh, axis=-1, keepdims=True) + EPS))
        n = jnp.concatenate(parts, axis=1)
        u = n * gain_ref[...] * sg
        u_ref[...] = jnp.where(_row_valid(pl.program_id(0), TM_E), u, 0.0).astype(BF16)

    blk = pl.BlockSpec((TM_E, 512), lambda i: (i, 0))
    return pl.pallas_call(
        body, name="hg_post", grid=(T // TM_E,),
        in_specs=[blk, blk, pl.BlockSpec((TM_E, 512), lambda i: (i, 7)), pl.BlockSpec((1, 512), lambda i: (0, 0))],
        out_specs=blk, out_shape=_sds((T, 512), BF16), compiler_params=_cp(("parallel",)))(o_f, o_b, p_act, gain)


def _hg_post_bwd(du, o_f, o_b, p_act, gain):
    def body(du_ref, of_ref, ob_ref, g_ref, gain_ref, do_ref, dg_ref, dgain_ref):
        i = pl.program_id(0)
        valid = _row_valid(i, TM_E)
        duv = jnp.where(valid, du_ref[...], 0.0)
        o = of_ref[...] + ob_ref[...]
        gv = g_ref[...]
        sig = jax.nn.sigmoid(gv)
        sg = gv * sig
        gain_v = gain_ref[...]
        dn = duv * gain_v * sg
        do_parts, n_parts = [], []
        for h in range(HG_HEADS):
            sl = slice(128 * h, 128 * (h + 1))
            oh = o[:, sl]
            r = lax.rsqrt(jnp.mean(oh * oh, axis=-1, keepdims=True) + EPS)
            nh = oh * r
            dnh = dn[:, sl]
            do_parts.append(r * (dnh - nh * jnp.mean(dnh * nh, axis=-1, keepdims=True)))
            n_parts.append(nh)
        n = jnp.where(valid, jnp.concatenate(n_parts, axis=1), 0.0)
        do_ref[...] = jnp.where(valid, jnp.concatenate(do_parts, axis=1), 0.0)
        dg_ref[...] = (duv * n * gain_v * (sig * (1.0 + gv * (1.0 - sig)))).astype(BF16)
        part = jnp.sum(duv * n * sg, axis=0, keepdims=True)

        @pl.when(i == 0)
        def _():
            dgain_ref[...] = part

        @pl.when(i > 0)
        def _():
            dgain_ref[...] += part

    blk = pl.BlockSpec((TM_E, 512), lambda i: (i, 0))
    vec = pl.BlockSpec((1, 512), lambda i: (0, 0))
    return pl.pallas_call(
        body, name="hg_post_bwd", grid=(T // TM_E,),
        in_specs=[blk, blk, blk, pl.BlockSpec((TM_E, 512), lambda i: (i, 7)), vec],
        out_specs=(blk, blk, vec), out_shape=(_sds((T, 512), F32), _sds((T, 512), BF16), _sds((1, 512), F32)),
        compiler_params=_cp(("arbitrary",)))(du, o_f, o_b, p_act, gain)


def _hg_pre_bwd(p_act, logits, dq_f, dq_b, dk_f, dk_b, db_f, db_b, dv_f, dv_b):
    def body(q_ref, zf_ref, zb_ref, lg_ref, dqf_ref, dqb_ref, dkf_ref, dkb_ref, dbf_ref, dbb_ref, dvf_ref, dvb_ref,
             dq_ref, dzf_ref, dzb_ref, di_ref, dlg_ref):
        i = pl.program_id(0)
        valid = _row_valid(i, HG_RB)
        qv = q_ref[...]
        sig = jax.nn.sigmoid(qv)
        dq_ref[...] = jnp.where(valid, (dqf_ref[...] + dqb_ref[...]) * (sig * (1.0 + qv * (1.0 - sig))), 0.0).astype(BF16)
        di_ref[...] = jnp.where(valid, dvf_ref[...] + dvb_ref[...], 0.0).astype(BF16)
        for d, (z_ref, dk_r, db_r, dz_ref) in enumerate(((zf_ref, dkf_ref, dbf_ref, dzf_ref), (zb_ref, dkb_ref, dbb_ref, dzb_ref))):
            lg = lg_ref[d]
            dl = lg[0:1, :] - lg[1:2, :]
            lb = jax.nn.sigmoid(dl)
            one_m_lb = jax.nn.sigmoid(-dl)
            log_f, _, snz, w2 = _hg_gate_terms(z_ref[...], lg)
            dbv = jnp.where(valid, db_r[...], 0.0)
            dkv = jnp.where(valid, dk_r[...], 0.0)
            dlf = jnp.dot(_chunk_tri(d == 1), dbv, precision=HI, preferred_element_type=F32)
            sz = 1.0 - snz
            dz_ref[...] = (dlf * w2 * snz - dkv * one_m_lb * sz * snz).astype(BF16)
            dlb = jnp.sum(dlf * snz * jnp.exp(-log_f) - dkv * snz, axis=0, keepdims=True)
            dl0 = dlb * lb * one_m_lb
            part = jnp.concatenate([dl0, -dl0], axis=0)

            @pl.when(i == 0)
            def _():
                dlg_ref[d] = part

            @pl.when(i > 0)
            def _():
                dlg_ref[d] += part

    blk = lambda c: pl.BlockSpec((HG_RB, 512), lambda i: (i, c))
    ob = pl.BlockSpec((HG_RB, 512), lambda i: (i, 0))
    lgs = pl.BlockSpec((2, 2, 512), lambda i: (0, 0, 0))
    return pl.pallas_call(
        body, name="hg_pre_bwd", grid=(HG_NB,),
        in_specs=[blk(3), blk(4), blk(5), lgs] + [ob] * 8,
        out_specs=(ob, ob, ob, ob, lgs),
        out_shape=(_sds((T, 512), BF16),) * 4 + (_sds((2, 2, 512), F32),),
        compiler_params=_cp(("arbitrary",)))(p_act, p_act, p_act, logits, dq_f, dq_b, dk_f, dk_b, db_f, db_b, dv_f, dv_b)


def _mix_fwd(o_na, u_hg, w_na, w_hg, p_act):
    def body(ona_ref, uhg_ref, wna_ref, whg_ref, gna_ref, ghg_ref, o_ref):
        y_na = _dot(ona_ref[...], wna_ref[...])
        y_hg = _dot(uhg_ref[...], whg_ref[...])
        o_ref[...] = (jax.nn.sigmoid(gna_ref[...]) * y_na + jax.nn.sigmoid(ghg_ref[...]) * y_hg).astype(BF16)

    act = pl.BlockSpec((TM_B, 512), lambda i: (i, 0))
    wsp = pl.BlockSpec((512, D), lambda i: (0, 0))
    return pl.pallas_call(
        body, name="mix_fwd", grid=(T // TM_B,),
        in_specs=[act, act, wsp, wsp, pl.BlockSpec((TM_B, D), lambda i: (i, 4)), pl.BlockSpec((TM_B, D), lambda i: (i, 5))],
        out_specs=pl.BlockSpec((TM_B, D), lambda i: (i, 0)), out_shape=_sds((T, D), BF16),
        compiler_params=_cp(("parallel",)))(o_na, u_hg, w_na, w_hg, p_act, p_act)


def _mix_bwd(o_na, u_hg, w_na, w_hg, p_act, dmix):
    ni = T // TM_B

    def body(ona_ref, uhg_ref, wna_ref, whg_ref, gna_ref, ghg_ref, dmix_ref,
             dgna_ref, dghg_ref, dwna_ref, dwhg_ref, dona_ref, duhg_ref, acc_na, acc_hg):
        i = pl.program_id(0)
        dm = dmix_ref[...].astype(F32)
        for x_ref, w_ref, g_ref, dg_ref, dx_ref, dw_ref, acc in (
                (ona_ref, wna_ref, gna_ref, dgna_ref, dona_ref, dwna_ref, acc_na),
                (uhg_ref, whg_ref, ghg_ref, dghg_ref, duhg_ref, dwhg_ref, acc_hg)):
            xv = x_ref[...]
            y = _dot(xv, w_ref[...])
            sg = jax.nn.sigmoid(g_ref[...])
            dg_ref[...] = (dm * y * sg * (1.0 - sg)).astype(BF16)
            dy = (dm * sg).astype(BF16)
            dx_ref[...] = _dot(dy, w_ref[...], NT)
            part = _dot(xv, dy, TN)

            @pl.when(i == 0)
            def _():
                acc[...] = part

            @pl.when(i > 0)
            def _():
                acc[...] += part

            @pl.when(i == ni - 1)
            def _():
                dw_ref[...] = acc[...].astype(BF16)

    act = pl.BlockSpec((TM_B, 512), lambda i: (i, 0))
    wsp = pl.BlockSpec((512, D), lambda i: (0, 0))
    rblk = pl.BlockSpec((TM_B, D), lambda i: (i, 0))
    return pl.pallas_call(
        body, name="mix_bwd", grid=(ni,),
        in_specs=[act, act, wsp, wsp, pl.BlockSpec((TM_B, D), lambda i: (i, 4)), pl.BlockSpec((TM_B, D), lambda i: (i, 5)),
                  rblk],
        out_specs=(rblk, rblk, wsp, wsp, act, act),
        out_shape=(_sds((T, D), BF16), _sds((T, D), BF16), _sds((512, D), BF16), _sds((512, D), BF16),
                   _sds((T, 512), F32), _sds((T, 512), F32)),
        scratch_shapes=[pltpu.VMEM((512, D), F32), pltpu.VMEM((512, D), F32)],
        compiler_params=_cp(("arbitrary",)))(o_na, u_hg, w_na, w_hg, p_act, p_act, dmix)


def _wo_fwd(mix, w_o, h0, g_mlp):
    def body(mix_ref, w_ref, h0_ref, g_ref, h1_ref, m_ref):
        h1 = h0_ref[...] + _dot(mix_ref[...], w_ref[...])
        h1_ref[...] = h1
        r = lax.rsqrt(jnp.mean(h1 * h1, axis=-1, keepdims=True) + EPS)
        m_ref[...] = (h1 * r * g_ref[...]).astype(BF16)

    blk = pl.BlockSpec((TM_B, D), lambda i: (i, 0))
    return pl.pallas_call(
        body, name="wo_fwd", grid=(T // TM_B,),
        in_specs=[blk, pl.BlockSpec((D, D), lambda i: (0, 0)), blk, pl.BlockSpec((1, D), lambda i: (0, 0))],
        out_specs=(blk, blk), out_shape=(_sds((T, D), F32), _sds((T, D), BF16)),
        compiler_params=_cp(("parallel",)))(mix, w_o, h0, g_mlp)


def _wo_bwd(dh1_b, w_o, mix):
    ni = T // TM_B

    def body(dh_ref, w_ref, mix_ref, dmix_ref, dw_ref, acc):
        i = pl.program_id(0)
        dh = dh_ref[...]
        dmix_ref[...] = _dot(dh, w_ref[...], NT).astype(BF16)
        part = _dot(mix_ref[...], dh, TN)

        @pl.when(i == 0)
        def _():
            acc[...] = part

        @pl.when(i > 0)
        def _():
            acc[...] += part

        @pl.when(i == ni - 1)
        def _():
            dw_ref[...] = acc[...].astype(BF16)

    blk = pl.BlockSpec((TM_B, D), lambda i: (i, 0))
    wsp = pl.BlockSpec((D, D), lambda i: (0, 0))
    return pl.pallas_call(
        body, name="wo_bwd", grid=(ni,), in_specs=[blk, wsp, blk], out_specs=(blk, wsp),
        out_shape=(_sds((T, D), BF16), _sds((D, D), BF16)), scratch_shapes=[pltpu.VMEM((D, D), F32)],
        compiler_params=_cp(("arbitrary",)))(dh1_b, w_o, mix)


FF_B = D_FF // NDEV


def _mlp_fwd(m, wup_g, wdown_g, h1):
    def body(m_ref, wu_ref, wd_ref, h1_ref, h2_ref):
        j = pl.program_id(1)
        up = jnp.maximum(_dot(m_ref[...], wu_ref[0]), 0.0)
        part = _dot((up * up).astype(BF16), wd_ref[0])

        @pl.when(j == 0)
        def _():
            h2_ref[...] = h1_ref[...] + part

        @pl.when(j > 0)
        def _():
            h2_ref[...] += part

    blk = pl.BlockSpec((TM_MM, D), lambda i, j: (i, 0))
    return pl.pallas_call(
        body, name="mlp_fwd", grid=(T // TM_MM, NDEV),
        in_specs=[blk, pl.BlockSpec((1, D, FF_B), lambda i, j: (j, 0, 0)), pl.BlockSpec((1, FF_B, D), lambda i, j: (j, 0, 0)), blk],
        out_specs=blk, out_shape=_sds((T, D), F32),
        compiler_params=_cp(("parallel", "arbitrary")))(m, wup_g, wdown_g, h1)


def _mlp_bwd(m, dh2_b, wup_g, wdown_g):
    ni = T // TM_B

    def body(m_ref, dh_ref, wu_ref, wd_ref, dwu_ref, dwd_ref, dm_ref, acc_u, acc_d):
        j, i = pl.program_id(0), pl.program_id(1)
        rows = pl.ds(pl.multiple_of(i * TM_B, TM_B), TM_B)
        mv, dh = m_ref[...], dh_ref[...]
        r = jnp.maximum(_dot(mv, wu_ref[0]), 0.0)
        act = (r * r).astype(BF16)
        dact = _dot(dh, wd_ref[0], NT)
        dup = (dact * (2.0 * r)).astype(BF16)
        pd = _dot(act, dh, TN)
        pu = _dot(mv, dup, TN)
        dmv = _dot(dup, wu_ref[0], NT)

        @pl.when(i == 0)
        def _():
            acc_u[...] = pu
            acc_d[...] = pd

        @pl.when(i > 0)
        def _():
            acc_u[...] += pu
            acc_d[...] += pd

        @pl.when(i == ni - 1)
        def _():
            dwu_ref[0] = acc_u[...].astype(BF16)
            dwd_ref[0] = acc_d[...].astype(BF16)

        @pl.when(j == 0)
        def _():
            dm_ref[rows, :] = dmv

        @pl.when(j > 0)
        def _():
            dm_ref[rows, :] += dmv

    blk = pl.BlockSpec((TM_B, D), lambda j, i: (i, 0))
    wus = pl.BlockSpec((1, D, FF_B), lambda j, i: (j, 0, 0))
    wds = pl.BlockSpec((1, FF_B, D), lambda j, i: (j, 0, 0))
    return pl.pallas_call(
        body, name="mlp_bwd", grid=(NDEV, ni), in_specs=[blk, blk, wus, wds],
        out_specs=(wus, wds, pl.BlockSpec((T, D), lambda j, i: (0, 0))),
        out_shape=(_sds((NDEV, D, FF_B), BF16), _sds((NDEV, FF_B, D), BF16), _sds((T, D), F32)),
        scratch_shapes=[pltpu.VMEM((D, FF_B), F32), pltpu.VMEM((FF_B, D), F32)],
        compiler_params=_cp(("arbitrary", "arbitrary")))(m, dh2_b, wup_g, wdown_g)


def _loss_head(h2, g_final, tgt):
    def body(h_ref, g_ref, t_ref, loss_ref, dh_ref, dhb_ref, dg_ref):
        i = pl.program_id(0)
        r_io = lax.broadcasted_iota(jnp.int32, (TM_E, 1), 0) + i * TM_E
        valid = (r_io >= NM) & (r_io < L)
        xv = h_ref[...]
        r = lax.rsqrt(jnp.mean(xv * xv, axis=-1, keepdims=True) + EPS)
        xh = xv * r
        gv = g_ref[...]
        err = jnp.where(valid, xh * gv - t_ref[...], 0.0)
        lpart = jnp.broadcast_to(0.5 * jnp.sum(jnp.sum(err * err, axis=-1, keepdims=True) * (1.0 / D), axis=0, keepdims=True), (1, 128))
        dy = err * (1.0 / D)
        dxh = dy * gv
        dh = r * (dxh - xh * jnp.mean(dxh * xh, axis=-1, keepdims=True))
        dh_ref[...] = dh
        dhb_ref[...] = dh.astype(BF16)
        gpart = jnp.sum(dy * xh, axis=0, keepdims=True)

        @pl.when(i == 0)
        def _():
            loss_ref[...] = lpart
            dg_ref[...] = gpart

        @pl.when(i > 0)
        def _():
            loss_ref[...] += lpart
            dg_ref[...] += gpart

    blk = pl.BlockSpec((TM_E, D), lambda i: (i, 0))
    vec = pl.BlockSpec((1, D), lambda i: (0, 0))
    return pl.pallas_call(
        body, name="loss_head", grid=(T // TM_E,), in_specs=[blk, vec, blk],
        out_specs=(pl.BlockSpec((1, 128), lambda i: (0, 0)), blk, blk, vec),
        out_shape=(_sds((1, 128), F32), _sds((T, D), F32), _sds((T, D), BF16), _sds((1, D), F32)),
        compiler_params=_cp(("arbitrary",)))(h2, g_final, tgt)


def _adamw(parts, w, m, v, name, alt_parts=None, alt_owners=()):
    rr, cc = w.shape
    tr = rr
    for cand in (256, 128, 64):
        if rr % cand == 0 and rr > cand:
            tr = cand
            break
    c1 = 1.0 - ADAM_B1 ** ADAM_STEP
    c2 = 1.0 - ADAM_B2 ** ADAM_STEP

    def body(p_ref, *rest):
        w_ref, m_ref, v_ref, g_ref, d_ref, nm_ref, nv_ref = rest[-7:]
        if alt_parts is None:
            part = lambda s: p_ref[s].astype(F32)
        else:
            me = 4 * lax.axis_index("x") + 2 * lax.axis_index("y") + lax.axis_index("c")
            use_alt = functools.reduce(jnp.logical_or, [me == o for o in alt_owners])
            part = lambda s: jnp.where(use_alt, rest[0][s], p_ref[s]).astype(F32)
        g = part(0)
        for s in range(1, NDEV):
            g = g + part(s)
        mn = ADAM_B1 * m_ref[...] + (1.0 - ADAM_B1) * g
        vn = ADAM_B2 * v_ref[...] + (1.0 - ADAM_B2) * (g * g)
        g_ref[...] = g
        nm_ref[...] = mn
        nv_ref[...] = vn
        d_ref[...] = -ADAM_LR * ((mn / c1) / (jnp.sqrt(vn / c2) + ADAM_EPS) + ADAM_WD * w_ref[...])

    blk = pl.BlockSpec((tr, cc), lambda i: (i, 0))
    pblk = pl.BlockSpec((NDEV, tr, cc), lambda i: (0, i, 0))
    lands = [parts] if alt_parts is None else [parts, alt_parts]
    return pl.pallas_call(
        body, name=name, grid=(rr // tr,),
        in_specs=[pblk] * len(lands) + [blk, blk, blk],
        out_specs=(blk,) * 4, out_shape=(_sds((rr, cc), F32),) * 4,
        compiler_params=_cp(("parallel",)))(*lands, w, m, v)


RPB_N = NA_HEADS * 15 * 31
RPB_PAD = 4096
OWN_ROWS = NM + 8


def _pad_rows(a, rows):
    return jnp.pad(a, ((0, rows - a.shape[0]),) + ((0, 0),) * (a.ndim - 1))


def _pack_owned(meta_blk, lb_blk):
    return jnp.concatenate([meta_blk, _pad_rows(lb_blk.reshape(2, 128), 8)], axis=0)


def _pack_replicated(n_mix, n_mlp, n_final, hg_gain, rpb):
    flat = _pad_rows(rpb.reshape(RPB_N), RPB_PAD)
    return jnp.concatenate([n_mix.reshape(8, 128), n_mlp.reshape(8, 128), n_final.reshape(8, 128),
                            _pad_rows(hg_gain.reshape(4, 128), 8), flat.reshape(32, 128)], axis=0)


def _unpack_replicated(a):
    return (a[0:8].reshape(1, D), a[8:16].reshape(1, D), a[16:24].reshape(D), a[24:28].reshape(1, 512),
            a[32:64].reshape(RPB_PAD)[:RPB_N].reshape(1, NA_HEADS, 15, 31))


def kernel(x, meta_tokens, w_in, w_na_out, w_hg_out, w_o, w_up, w_down, norm_mix, norm_mlp, norm_final, hg_norm, na_rpb, hg_lb_logits, loss_target, m_meta_tokens, m_w_in, m_w_na_out, m_w_hg_out, m_w_o, m_w_up, m_w_down, m_norm_mix, m_norm_mlp, m_norm_final, m_hg_norm, m_na_rpb, m_hg_lb_logits, v_meta_tokens, v_w_in, v_w_na_out, v_w_hg_out, v_w_o, v_w_up, v_w_down, v_norm_mix, v_norm_mlp, v_norm_final, v_hg_norm, v_na_rpb, v_hg_lb_logits):
    owned = _pack_owned(meta_tokens, hg_lb_logits)
    win_g, owned_g = _gather_two_level([w_in[0].astype(BF16), owned], "gather_first")
    later = [w[0].astype(BF16) for w in (w_na_out, w_hg_out, w_o, w_up, w_down)]
    later[0] = _tie(later[0], owned_g, "tie_gather_rest")
    gather_rest, tok = _exchange_start(later, [False] * 5, "gather_rest_start")
    win_g = _tie(win_g, tok, "tie_inproj")
    meta_full = jnp.transpose(owned_g[:, 0:NM, :], (1, 0, 2)).reshape(NM, D)
    logits = jnp.transpose(owned_g[:, NM:NM + 2, :].reshape(NDEV, 2, 2, 64), (1, 2, 0, 3)).reshape(2, 2, 512)

    h0 = jnp.concatenate([meta_full, x[0], jnp.zeros((T - L, D), F32)], axis=0)
    tgt = jnp.concatenate([jnp.zeros((NM, D), F32), loss_target[0], jnp.zeros((T - L, D), F32)], axis=0)
    bias_tab = _na_bias_table(na_rpb[0])

    a, a_t = _norm_fwd_t(h0, norm_mix, "norm_mix_fwd")
    p_act = _inproj_fwd(a, win_g)
    o_na, lse = _na_fwd(p_act, bias_tab)
    qh, k_f, b_f, k_b, b_b = _hg_pre(p_act, logits)
    o_f, st_f = _hg_scan_fwd(qh, k_f, b_f, p_act, False)
    o_b, st_b = _hg_scan_fwd(qh, k_b, b_b, p_act, True)
    u_hg = _hg_post(o_f, o_b, p_act, hg_norm)
    wna_g, whg_g, wo_g, wup_g, wdown_g = _exchange_wait(gather_rest, [False] * 5, [u_hg, o_na], "gather_rest_wait")
    w_o_full = wo_g.reshape(D, D)
    w_na_full = jnp.transpose(wna_g, (1, 0, 2)).reshape(512, D)
    w_hg_full = jnp.transpose(whg_g, (1, 0, 2)).reshape(512, D)
    mix = _mix_fwd(o_na, u_hg, w_na_full, w_hg_full, p_act)
    h1, m_act = _wo_fwd(mix, w_o_full, h0, norm_mlp)
    h2 = _mlp_fwd(m_act, wup_g, wdown_g, h1)
    loss_part, dh2, dh2_b, d_nfinal = _loss_head(h2, norm_final.reshape(1, D), tgt)

    dwup_p, dwdown_p, dm = _mlp_bwd(m_act, dh2_b, wup_g, wdown_g)
    sc_mlp, tok = _exchange_start([dwup_p, dwdown_p], [True] * 2, "scatter_mlp_start")
    dh1, dh1_b, d_nmlp = _norm_bwd(h1, norm_mlp, _tie(dm, tok, "tie_norm_mlp_bwd"), dh2, "norm_mlp_bwd")
    dmix, dwo = _wo_bwd(dh1_b, w_o_full, mix)
    sc_wo, tok = _exchange_start([dwo.reshape(NDEV, D // NDEV, D)], [True], "scatter_wo_start")
    dgna, dghg, dwna, dwhg, do_na, du_hg = _mix_bwd(o_na, u_hg, w_na_full, w_hg_full, p_act, _tie(dmix, tok, "tie_mix_bwd"))
    owner_cols = lambda w: jnp.transpose(w.reshape(512, NDEV, D // NDEV), (1, 0, 2))
    sc_br, tok = _exchange_start([owner_cols(dwna), owner_cols(dwhg)], [True] * 2, "scatter_branch_start")
    du_hg = _tie(du_hg, tok, "tie_hg_post_bwd")
    do_hg, dg_hg, d_gain = _hg_post_bwd(du_hg, o_f, o_b, p_act, hg_norm)
    dq_f, dk_f, db_f, dv_f = _hg_scan_bwd(qh, k_f, b_f, p_act, st_f, do_hg, False)
    dq_b, dk_b, db_b, dv_b = _hg_scan_bwd(qh, k_b, b_b, p_act, st_b, do_hg, True)
    dq_hg, dz_f, dz_b, di_hg, d_logits = _hg_pre_bwd(p_act, logits, dq_f, dq_b, dk_f, dk_b, db_f, db_b, dv_f, dv_b)
    dp_late = jnp.concatenate([dq_hg, dz_f, dz_b, di_hg, dg_hg, dgna, dghg], axis=1)
    sc_in_a, tok = _owner_scatter_start(_inproj_bwd_dw(a_t, dp_late, "inproj_bwd_dw_a"), OWNERS_A, "scatter_in_a_start")
    da = _inproj_bwd_da(_tie(dp_late, tok, "tie_inproj_bwd_da_a"), win_g, OWNERS_A[0], None, "inproj_bwd_da_a")
    dq_na, dk_na, dv_na, dbias = _na_bwd(p_act, _tie(do_na, da, "tie_na_bwd"), lse, bias_tab)
    dp_early = jnp.concatenate([dq_na.astype(BF16), dk_na.astype(BF16), dv_na.astype(BF16)], axis=1)
    sc_in_b, tok = _owner_scatter_start(_inproj_bwd_dw(a_t, dp_early, "inproj_bwd_dw_b"), OWNERS_B, "scatter_in_b_start")
    da = _inproj_bwd_da(_tie(dp_early, tok, "tie_inproj_bwd_da_b"), win_g, OWNERS_B[0], da, "inproj_bwd_da_b")
    dh0, _, d_nmix = _norm_bwd(h0, norm_mix, da, dh1, "norm_mix_bwd")
    d_rpb = _na_rpb_reduce(_tie(dbias, tok, "tie_rpb_reduce"))[:, :, :31]

    res = {}

    def update(nm, parts, w, mm, vv):
        res[nm] = [r[None] for r in _adamw(parts, w[0], mm[0], vv[0], "adamw_" + nm)]
        return res[nm][1]

    wup_r, wdown_r = _exchange_wait(sc_mlp, [True] * 2, [dh0, d_rpb], "scatter_mlp_wait")
    update("w_up", wup_r, w_up, m_w_up, v_w_up)
    last = update("w_down", wdown_r, w_down, m_w_down, v_w_down)
    (wo_r,) = _exchange_wait(sc_wo, [True], [last], "scatter_wo_wait")
    last = update("w_o", wo_r, w_o, m_w_o, v_w_o)
    wna_r, whg_r = _exchange_wait(sc_br, [True] * 2, [last], "scatter_branch_wait")
    update("w_na_out", wna_r, w_na_out, m_w_na_out, v_w_na_out)
    last = update("w_hg_out", whg_r, w_hg_out, m_w_hg_out, v_w_hg_out)

    d_meta = jnp.transpose(dh0[0:NM].reshape(NM, NDEV, 128), (1, 0, 2))
    d_lg = jnp.transpose(d_logits.reshape(2, 2, NDEV, 64), (2, 0, 1, 3)).reshape(NDEV, 2, 128)
    owned_p = jnp.concatenate([d_meta, jnp.pad(d_lg, ((0, 0), (0, OWN_ROWS - NM - 2), (0, 0)))], axis=1)
    repl_p = _pack_replicated(d_nmix, d_nmlp, d_nfinal, d_gain, d_rpb)
    owned_r, repl_r = _exchange([_tie(owned_p, last, "tie_scatter_small"), repl_p], [True, False], "scatter_small")
    own = _adamw(owned_r, owned, _pack_owned(m_meta_tokens, m_hg_lb_logits), _pack_owned(v_meta_tokens, v_hg_lb_logits),
                 "adamw_owned_small")
    res["meta_tokens"] = [r[0:NM] for r in own]
    res["hg_lb_logits"] = [r[NM:NM + 2].reshape(2, 2, 64) for r in own]
    rep = _adamw(repl_r, _pack_replicated(norm_mix, norm_mlp, norm_final, hg_norm, na_rpb),
                 _pack_replicated(m_norm_mix, m_norm_mlp, m_norm_final, m_hg_norm, m_na_rpb),
                 _pack_replicated(v_norm_mix, v_norm_mlp, v_norm_final, v_hg_norm, v_na_rpb), "adamw_replicated")
    for q in range(4):
        um = _unpack_replicated(rep[q])
        for nm, val in zip(("norm_mix", "norm_mlp", "norm_final", "hg_norm", "na_rpb"), um):
            res.setdefault(nm, [None] * 4)[q] = val
    win_ra = _owner_scatter_wait(sc_in_a, OWNERS_A, [rep[1], own[1]], "scatter_in_a_wait")
    win_rb = _owner_scatter_wait(sc_in_b, OWNERS_B, [win_ra], "scatter_in_b_wait")
    res["w_in"] = [r[None] for r in _adamw(win_ra, w_in[0], m_w_in[0], v_w_in[0], "adamw_w_in", win_rb, OWNERS_B)]

    loss = lax.psum(loss_part[0, 0], ("x", "y", "c"))
    grad_x = dh0[NM:L][None]
    order = ("meta_tokens", "w_in", "w_na_out", "w_hg_out", "w_o", "w_up", "w_down", "norm_mix", "norm_mlp", "norm_final",
             "hg_norm", "na_rpb", "hg_lb_logits")
    outs = [loss, grad_x]
    for q in range(4):
        outs += [res[nm][q] for nm in order]
    return tuple(outs)
```

```python
import functools

import numpy as np
import jax
import jax.numpy as jnp
from jax import lax
from jax.experimental import pallas as pl
from jax.experimental.pallas import tpu as pltpu

F32 = jnp.float32
BF16 = jnp.bfloat16

D = 1024
SEQ = 2048
NM = 16
L = SEQ + NM
T = 2176
NDEV = 8
EPS = 1e-6
GRID_W = 64
ROWS = SEQ // GRID_W
NA_HEADS = 8
NA_DH = 64
NA_SCALE = NA_DH ** -0.5
HG_HEADS = 4
HG_C = 16
NCHUNK = L // HG_C
D_FF = 4096
IN_COLS = 6144
NEG = -1e30

ADAM_LR = 0.001
ADAM_B1 = 0.9
ADAM_B2 = 0.999
ADAM_EPS = 1e-08
ADAM_WD = 0.01
ADAM_STEP = 10

MESH_ID = pl.DeviceIdType.MESH
ANY = pl.BlockSpec(memory_space=pl.ANY)

NN = (((1,), (0,)), ((), ()))
NT = (((1,), (1,)), ((), ()))
TN = (((0,), (0,)), ((), ()))


def _cp(sem=None, vmem_mb=48):
    return pltpu.CompilerParams(dimension_semantics=sem, vmem_limit_bytes=vmem_mb * 1024 * 1024)


def _dot(a, b, dims=NN):
    return lax.dot_general(a, b, dims, preferred_element_type=F32)


def _sds(shape, dtype):
    return jax.ShapeDtypeStruct(shape, dtype)


HBM = pl.BlockSpec(memory_space=pltpu.HBM)
SEM = pl.BlockSpec(memory_space=pltpu.SEMAPHORE)
EFFECT = pltpu.SideEffectType.DATAFLOW_SIDE_EFFECTING


def _exchange(arrs, scatter, name):
    n = len(arrs)
    out_shapes = []
    for a, sc in zip(arrs, scatter):
        out_shapes.append(_sds(a.shape if sc else (NDEV,) + a.shape, a.dtype))

    def body(*refs):
        ins, outs = refs[:n], refs[n:2 * n]
        send_sems, recv_sems, loc_sems = refs[2 * n:]
        me = 4 * lax.axis_index("x") + 2 * lax.axis_index("y") + lax.axis_index("c")
        copies = []
        for k in range(n):
            src_me = ins[k].at[me] if scatter[k] else ins[k]
            loc = pltpu.make_async_copy(src_me, outs[k].at[me], loc_sems.at[k])
            loc.start()
            copies.append(loc)
        remote = _peer_copies(ins, outs, scatter, send_sems, recv_sems)
        for cp in remote:
            cp.start()
        for cp in remote:
            cp.wait_recv()
        for cp in remote:
            cp.wait_send()
        for cp in copies:
            cp.wait()

    return pl.pallas_call(
        body, name=name, out_shape=tuple(out_shapes), in_specs=[ANY] * n, out_specs=tuple([ANY] * n),
        scratch_shapes=[pltpu.SemaphoreType.DMA((n * (NDEV - 1),)), pltpu.SemaphoreType.DMA((n * (NDEV - 1),)),
                        pltpu.SemaphoreType.DMA((n,))],
    )(*arrs)


def _gather_two_level(arrs, name):
    n = len(arrs)

    def body(*refs):
        ins, outs = refs[:n], refs[n:2 * n]
        send_sems, recv_sems, loc_sems = refs[2 * n:]
        x, y, c = lax.axis_index("x"), lax.axis_index("y"), lax.axis_index("c")
        sib = (x, y, 1 - c)
        chips = [(1 - x, y), (x, 1 - y), (1 - x, 1 - y)]

        def slot(k, px, py, pc):
            return outs[k].at[4 * px + 2 * py + pc]

        def copy(k, q, block, to, src=None):
            return pltpu.make_async_remote_copy(
                src_ref=slot(k, *block) if src is None else src, dst_ref=slot(k, *block),
                send_sem=send_sems.at[7 * k + q], recv_sem=recv_sems.at[7 * k + q], device_id=to, device_id_type=MESH_ID)

        mine = [pltpu.make_async_copy(ins[k], slot(k, x, y, c), loc_sems.at[k]) for k in range(n)]
        for cp in mine:
            cp.start()
        first = []
        for k in range(n):
            first.append(copy(k, 0, (x, y, c), sib, src=ins[k]))
            first += [copy(k, 1 + j, (x, y, c), (*chip, c), src=ins[k]) for j, chip in enumerate(chips)]
        for cp in first:
            cp.start()
        passed = []
        for j, chip in enumerate(chips):
            for k in range(n):
                copy(k, 1 + j, (*chip, c), (x, y, c)).wait_recv()
                fw = copy(k, 4 + j, (*chip, c), sib)
                fw.start()
                passed.append(fw)
        for k in range(n):
            copy(k, 0, (x, y, 1 - c), (x, y, c)).wait_recv()
            for j, chip in enumerate(chips):
                copy(k, 4 + j, (*chip, 1 - c), (x, y, c)).wait_recv()
        for cp in first + passed:
            cp.wait_send()
        for cp in mine:
            cp.wait()

    return pl.pallas_call(
        body, name=name, out_shape=tuple(_sds((NDEV,) + a.shape, a.dtype) for a in arrs),
        in_specs=[ANY] * n, out_specs=tuple([ANY] * n),
        scratch_shapes=[pltpu.SemaphoreType.DMA((7 * n,)), pltpu.SemaphoreType.DMA((7 * n,)), pltpu.SemaphoreType.DMA((n,))],
    )(*arrs)


def _peer_copies(srcs, lands, scatter, send_sems, recv_sems):
    x, y, c = lax.axis_index("x"), lax.axis_index("y"), lax.axis_index("c")
    me = 4 * x + 2 * y + c
    out = []
    for k in range(len(srcs)):
        for m in range(1, NDEV):
            px, py, pc = x ^ (m >> 2), y ^ ((m >> 1) & 1), c ^ (m & 1)
            src = srcs[k].at[4 * px + 2 * py + pc] if scatter[k] else srcs[k]
            out.append(pltpu.make_async_remote_copy(
                src_ref=src, dst_ref=lands[k].at[me], send_sem=send_sems.at[k * (NDEV - 1) + m - 1],
                recv_sem=recv_sems.at[k * (NDEV - 1) + m - 1],
                device_id=(px, py, pc), device_id_type=MESH_ID))
    return out


def _exchange_start(arrs, scatter, name):
    n = len(arrs)
    me = 4 * lax.axis_index("x") + 2 * lax.axis_index("y") + lax.axis_index("c")
    lands = []
    for a, sc in zip(arrs, scatter):
        own = lax.dynamic_index_in_dim(a, me, 0, keepdims=True) if sc else a[None]
        shape = a.shape if sc else (NDEV,) + a.shape
        lands.append(lax.dynamic_update_index_in_dim(lax.empty(shape, a.dtype), own, me, 0))

    def body(*refs):
        srcs, lnds = refs[:n], refs[n:2 * n]
        send_sems, recv_sems = refs[2 * n], refs[2 * n + 1]
        token = refs[-1]
        for cp in _peer_copies(srcs, lnds, scatter, send_sems, recv_sems):
            cp.start()
        token[...] = jnp.zeros_like(token)

    ops = [pltpu.with_memory_space_constraint(a, pltpu.HBM) for a in list(arrs) + lands]
    res = pl.pallas_call(
        body, name=name,
        out_shape=(pltpu.SemaphoreType.DMA((n * (NDEV - 1),)), pltpu.SemaphoreType.DMA((n * (NDEV - 1),)))
        + tuple(pltpu.HBM(o.shape, o.dtype) for o in ops) + (_sds((8, 128), F32),),
        in_specs=[HBM] * (2 * n), out_specs=(SEM, SEM) + (HBM,) * (2 * n) + (pl.BlockSpec(memory_space=pltpu.VMEM),),
        input_output_aliases={k: 2 + k for k in range(2 * n)},
        compiler_params=pltpu.CompilerParams(has_side_effects=EFFECT),
    )(*ops)
    return res[:-1], res[-1]


def _exchange_wait(handle, scatter, after, name):
    send_sems, recv_sems = handle[0], handle[1]
    bufs = handle[2:]
    n = len(bufs) // 2
    after = list(after)

    def body(*refs):
        srcs, lnds = refs[:n], refs[n:2 * n]
        for cp in _peer_copies(srcs, lnds, scatter, refs[2 * n], refs[2 * n + 1]):
            cp.wait_send()
            cp.wait_recv()

    res = pl.pallas_call(
        body, name=name, out_shape=tuple(pltpu.HBM(b.shape, b.dtype) for b in bufs),
        in_specs=[HBM] * (2 * n) + [SEM, SEM] + [ANY] * len(after), out_specs=(HBM,) * (2 * n),
        input_output_aliases={k: k for k in range(2 * n)},
        compiler_params=pltpu.CompilerParams(has_side_effects=EFFECT),
    )(*bufs, send_sems, recv_sems, *after)
    return res[n:]


def _tie(x, token, name):
    def body(x_ref, t_ref, o_ref):
        del x_ref, t_ref, o_ref

    return pl.pallas_call(body, name=name, out_shape=_sds(x.shape, x.dtype), in_specs=[ANY, ANY], out_specs=ANY,
                          input_output_aliases={0: 0})(x, token)


TM_E = 272


def _norm_fwd_t(h, g, name):
    def body(h_ref, g_ref, o_ref, ot_ref):
        xv = h_ref[...]
        r = lax.rsqrt(jnp.mean(xv * xv, axis=-1, keepdims=True) + EPS)
        y = xv * r * g_ref[...]
        o_ref[...] = y.astype(BF16)
        ot_ref[...] = y.T.astype(BF16)

    return pl.pallas_call(
        body, name=name, grid=(T // 128,),
        in_specs=[pl.BlockSpec((128, D), lambda i: (i, 0)), pl.BlockSpec((1, D), lambda i: (0, 0))],
        out_specs=(pl.BlockSpec((128, D), lambda i: (i, 0)), pl.BlockSpec((D, 128), lambda i: (0, i))),
        out_shape=(_sds((T, D), BF16), _sds((D, T), BF16)), compiler_params=_cp(("parallel",)))(h, g)


def _norm_bwd_rows(xv, gv, dnv, dres):
    r = lax.rsqrt(jnp.mean(xv * xv, axis=-1, keepdims=True) + EPS)
    xh = xv * r
    dxh = dnv * gv
    dx = dres + r * (dxh - xh * jnp.mean(dxh * xh, axis=-1, keepdims=True))
    return dx, jnp.sum(dnv * xh, axis=0, keepdims=True)


TM_MM = 1088


def _inproj_fwd(a, w_g):
    nb = w_g.shape[2]

    def body(a_ref, w_ref, o_ref):
        o_ref[...] = _dot(a_ref[...], w_ref[0])

    return pl.pallas_call(
        body, name="inproj_fwd", grid=(T // TM_MM, NDEV),
        in_specs=[pl.BlockSpec((TM_MM, D), lambda i, j: (i, 0)), pl.BlockSpec((1, D, nb), lambda i, j: (j, 0, 0))],
        out_specs=pl.BlockSpec((TM_MM, nb), lambda i, j: (i, j)), out_shape=_sds((T, NDEV * nb), F32),
        compiler_params=_cp(("parallel", "parallel")))(a, w_g)


TM_B = 544


W_IN_B = IN_COLS // NDEV


def _inproj_bwd_dw(a_t, dp, name):
    nblk = dp.shape[1] // W_IN_B

    def body(at_ref, dp_ref, dw_ref):
        dw_ref[0] = _dot(at_ref[...], dp_ref[...]).astype(BF16)

    return pl.pallas_call(
        body, name=name, grid=(nblk,),
        in_specs=[pl.BlockSpec((D, T), lambda j: (0, 0)), pl.BlockSpec((T, W_IN_B), lambda j: (0, j))],
        out_specs=pl.BlockSpec((1, D, W_IN_B), lambda j: (j, 0, 0)), out_shape=_sds((nblk, D, W_IN_B), BF16),
        compiler_params=_cp(("parallel",)))(a_t, dp)


def _inproj_bwd_da(dp, w_g, h0, g_mix, dh1):
    nsub = TM_MM // TM_E

    def body(dp_ref, w_ref, h0_ref, g_ref, dres_ref, dh0_ref, dg_ref, da):
        i, j = pl.program_id(0), pl.program_id(1)
        dav = _dot(dp_ref[...], w_ref[0], NT)

        @pl.when(j == 0)
        def _():
            da[...] = dav

        @pl.when(j > 0)
        def _():
            da[...] += dav

        @pl.when(j == NDEV - 1)
        def _():
            gsum = jnp.zeros((1, D), F32)
            for s in range(nsub):
                sub = slice(s * TM_E, (s + 1) * TM_E)
                dx, gpart = _norm_bwd_rows(h0_ref[sub, :], g_ref[...], da[sub, :], dres_ref[sub, :])
                dh0_ref[sub, :] = dx
                gsum = gsum + gpart

            @pl.when(i == 0)
            def _():
                dg_ref[...] = gsum

            @pl.when(i > 0)
            def _():
                dg_ref[...] += gsum

    rblk = pl.BlockSpec((TM_MM, D), lambda i, j: (i, 0))
    vec = pl.BlockSpec((1, D), lambda i, j: (0, 0))
    return pl.pallas_call(
        body, name="inproj_bwd_da", grid=(T // TM_MM, NDEV),
        in_specs=[pl.BlockSpec((TM_MM, W_IN_B), lambda i, j: (i, j)), pl.BlockSpec((1, D, W_IN_B), lambda i, j: (j, 0, 0)),
                  rblk, vec, rblk],
        out_specs=(rblk, vec), out_shape=(_sds((T, D), F32), _sds((1, D), F32)),
        scratch_shapes=[pltpu.VMEM((TM_MM, D), F32)],
        compiler_params=_cp(("arbitrary", "arbitrary"), 56))(dp, w_g, h0, g_mix, dh1)


NA_QB = 256
NA_GROUPS = ROWS // 4
NA_UROWS = 11
NA_KW = NA_UROWS * GRID_W
NA_KU = 768


def _na_row_offset(var, i, j):
    valid = (j < 8, i <= j < i + 8, 3 <= j < NA_UROWS)[var]
    return (j - i + (7, 3, 0)[var]) if valid else None


def _na_bias_table(rpb):
    def body(r_ref, o_ref):
        row3 = lax.broadcasted_iota(jnp.int32, (15, GRID_W, 128), 1)
        lane3 = lax.broadcasted_iota(jnp.int32, (15, GRID_W, 128), 2)
        w3 = lane3 & (GRID_W - 1)
        cs3 = jnp.clip(row3 - 8, 0, GRID_W - 16)
        lane = lax.broadcasted_iota(jnp.int32, (GRID_W, 128), 1)
        neg = jnp.full((GRID_W, 128), NEG, F32)
        z = jnp.stack([jnp.broadcast_to(r_ref[0, a:a + 1, :], (GRID_W, 128)) for a in range(15)])
        for bit in range(6):
            sh = 1 << bit
            z = jnp.where((row3 & sh) != 0, jnp.roll(z, sh, axis=2), z)
        z = jnp.roll(z, 128 - 15, axis=2)
        z = jnp.where(lane3 < GRID_W, z, 0.0)
        z = z + jnp.roll(z, GRID_W, axis=2)
        tabs = jnp.where((w3 >= cs3) & (w3 < cs3 + 16), z, NEG)
        tail = jnp.where(lane < GRID_W + NM, 0.0, NEG)
        for var in range(3):
            for i in range(4):
                for jp in range(NA_KU // 128):
                    halves = []
                    for j in (2 * jp, 2 * jp + 1):
                        a = _na_row_offset(var, i, j) if j < NA_UROWS else None
                        halves.append(tail if j >= NA_UROWS else (neg if a is None else tabs[a]))
                    o_ref[var, 0, i * 64:(i + 1) * 64, jp * 128:(jp + 1) * 128] = jnp.where(lane < GRID_W, halves[0], halves[1])

    rp = jnp.concatenate([rpb, jnp.zeros((NA_HEADS, 15, 128 - 31), F32)], axis=2)
    return pl.pallas_call(
        body, name="na_bias_table", grid=(NA_HEADS,),
        in_specs=[pl.BlockSpec((1, 15, 128), lambda h: (h, 0, 0))],
        out_specs=pl.BlockSpec((3, 1, NA_QB, NA_KU), lambda h: (0, h, 0, 0)),
        out_shape=_sds((3, NA_HEADS, NA_QB, NA_KU), F32), compiler_params=_cp(("parallel",)))(rp)


def _na_var(g):
    return jnp.where(g == 0, 0, jnp.where(g == NA_GROUPS - 1, 2, 1))


def _na_load_window(src_ref, dst, g):
    us = jnp.clip(4 * g - 4, 0, ROWS - NA_UROWS)
    kstart = pl.multiple_of(NM + GRID_W * us, 16)
    dst[0:NA_KW, :] = src_ref[pl.ds(kstart, NA_KW), :].astype(BF16)
    dst[NA_KW:NA_KW + NM, :] = src_ref[0:NM, :].astype(BF16)
    dst[NA_KW + NM:, :] = jnp.zeros((NA_KU - NA_KW - NM, 128), BF16)
    return kstart


def _na_fwd(p_act, bias_tab):
    def body(q_ref, k_ref, v_ref, b_ref, o_ref, lse_ref, ku, vu):
        g = pl.program_id(1)
        _na_load_window(k_ref, ku, g)
        _na_load_window(v_ref, vu, g)
        qstart = pl.multiple_of(NM + NA_QB * g, 16)
        q = q_ref[pl.ds(qstart, NA_QB), :]
        lane = lax.broadcasted_iota(jnp.int32, (NA_QB, 128), 1)
        o_h, lse_h = [], []
        for h in range(2):
            hm = (lane < 64) if h == 0 else (lane >= 64)
            qm = jnp.where(hm, q, 0.0).astype(BF16)
            s = _dot(qm, ku[...], NT) * NA_SCALE + b_ref[0, h]
            m = jnp.max(s, axis=-1, keepdims=True)
            p = jnp.exp(s - m)
            l = jnp.sum(p, axis=-1, keepdims=True)
            o_h.append(_dot(p.astype(BF16), vu[...]) / l)
            lse_h.append(jnp.broadcast_to(m + jnp.log(l), (NA_QB, 128)))
        o_ref[pl.ds(qstart, NA_QB), :] = jnp.where(lane < 64, o_h[0], o_h[1]).astype(BF16)
        lse_ref[0, pl.ds(qstart, NA_QB), :] = jnp.where(lane < 64, lse_h[0], lse_h[1])

        @pl.when(g == 0)
        def _():
            qm_ = q_ref[0:NM, :]
            lane_m = lax.broadcasted_iota(jnp.int32, (NM, 128), 1)
            km, vm = ku[NA_KW:NA_KW + NM, :], vu[NA_KW:NA_KW + NM, :]
            om = []
            for h in range(2):
                hm = (lane_m < 64) if h == 0 else (lane_m >= 64)
                s = _dot(jnp.where(hm, qm_, 0.0).astype(BF16), km, NT) * NA_SCALE
                p = jnp.exp(s - jnp.max(s, axis=-1, keepdims=True))
                l = jnp.sum(p, axis=-1, keepdims=True)
                om.append(_dot(p.astype(BF16), vm) / l)
            o_ref[0:NM, :] = jnp.where(lane_m < 64, om[0], om[1]).astype(BF16)
            o_ref[L:T, :] = jnp.zeros((T - L, 128), BF16)
            lse_ref[0, 0:NM, :] = jnp.zeros((NM, 128), F32)
            lse_ref[0, L:T, :] = jnp.zeros((T - L, 128), F32)

    col = lambda off: pl.BlockSpec((T, 128), lambda hp, g: (0, off + hp))
    return pl.pallas_call(
        body, name="na_fwd", grid=(4, NA_GROUPS),
        in_specs=[col(0), col(4), col(8),
                  pl.BlockSpec((1, 2, NA_QB, NA_KU), lambda hp, g: (_na_var(g), hp, 0, 0))],
        out_specs=(pl.BlockSpec((T, 128), lambda hp, g: (0, hp)), pl.BlockSpec((1, T, 128), lambda hp, g: (hp, 0, 0))),
        out_shape=(_sds((T, 512), BF16), _sds((4, T, 128), F32)),
        scratch_shapes=[pltpu.VMEM((NA_KU, 128), BF16), pltpu.VMEM((NA_KU, 128), BF16)],
        compiler_params=_cp(("parallel", "arbitrary")))(p_act, p_act, p_act, bias_tab)


def _na_bwd(p_act, do, lse, bias_tab):
    def body(q_ref, k_ref, v_ref, do_ref, lse_ref, b_ref, dq_ref, dk_ref, dv_ref, db_ref, ku, vu):
        g = pl.program_id(1)

        @pl.when(g == 0)
        def _():
            dq_ref[...] = jnp.zeros((T, 128), F32)
            dk_ref[...] = jnp.zeros((T, 128), F32)
            dv_ref[...] = jnp.zeros((T, 128), F32)

        kstart = _na_load_window(k_ref, ku, g)
        _na_load_window(v_ref, vu, g)
        qstart = pl.multiple_of(NM + NA_QB * g, 16)
        q = q_ref[pl.ds(qstart, NA_QB), :]
        dov = do_ref[pl.ds(qstart, NA_QB), :]
        lsev = lse_ref[0, pl.ds(qstart, NA_QB), :]
        lane = lax.broadcasted_iota(jnp.int32, (NA_QB, 128), 1)
        first = (g == 0) | (g == 1) | (g == NA_GROUPS - 1)
        dq_h = []
        dku = jnp.zeros((NA_KU, 128), F32)
        dvu = jnp.zeros((NA_KU, 128), F32)
        for h in range(2):
            hm = (lane < 64) if h == 0 else (lane >= 64)
            qm = jnp.where(hm, q, 0.0).astype(BF16)
            dom = jnp.where(hm, dov, 0.0).astype(BF16)
            s = _dot(qm, ku[...], NT) * NA_SCALE + b_ref[0, h]
            p = jnp.exp(s - lsev[:, 64 * h:64 * h + 1])
            dp = _dot(dom, vu[...], NT)
            delta = jnp.sum(p * dp, axis=-1, keepdims=True)
            ds = p * (dp - delta)

            @pl.when(first)
            def _():
                db_ref[0, h] = ds

            @pl.when(jnp.logical_not(first))
            def _():
                db_ref[0, h] += ds

            dsb = (ds * NA_SCALE).astype(BF16)
            dq_h.append(_dot(dsb, ku[...]))
            dku = dku + _dot(dsb, qm, TN)
            dvu = dvu + _dot(p.astype(BF16), dom, TN)
        dq_ref[pl.ds(qstart, NA_QB), :] = jnp.where(lane < 64, dq_h[0], dq_h[1])
        dk_ref[pl.ds(kstart, NA_KW), :] += dku[0:NA_KW]
        dv_ref[pl.ds(kstart, NA_KW), :] += dvu[0:NA_KW]
        dk_ref[0:NM, :] += dku[NA_KW:NA_KW + NM]
        dv_ref[0:NM, :] += dvu[NA_KW:NA_KW + NM]

        @pl.when(g == 0)
        def _():
            qm_ = q_ref[0:NM, :]
            dom_ = do_ref[0:NM, :]
            lane_m = lax.broadcasted_iota(jnp.int32, (NM, 128), 1)
            km, vm = ku[NA_KW:NA_KW + NM, :], vu[NA_KW:NA_KW + NM, :]
            dqs = []
            dkm = jnp.zeros((NM, 128), F32)
            dvm = jnp.zeros((NM, 128), F32)
            for h in range(2):
                hm = (lane_m < 64) if h == 0 else (lane_m >= 64)
                qh = jnp.where(hm, qm_, 0.0).astype(BF16)
                doh = jnp.where(hm, dom_, 0.0).astype(BF16)
                s = _dot(qh, km, NT) * NA_SCALE
                e = jnp.exp(s - jnp.max(s, axis=-1, keepdims=True))
                p = e / jnp.sum(e, axis=-1, keepdims=True)
                dp = _dot(doh, vm, NT)
                ds = p * (dp - jnp.sum(p * dp, axis=-1, keepdims=True))
                dsb = (ds * NA_SCALE).astype(BF16)
                dqs.append(_dot(dsb, km))
                dkm = dkm + _dot(dsb, qh, TN)
                dvm = dvm + _dot(p.astype(BF16), doh, TN)
            dq_ref[0:NM, :] = jnp.where(lane_m < 64, dqs[0], dqs[1])
            dk_ref[0:NM, :] += dkm
            dv_ref[0:NM, :] += dvm

    col = lambda off: pl.BlockSpec((T, 128), lambda hp, g: (0, off + hp))
    ocol = pl.BlockSpec((T, 128), lambda hp, g: (0, hp))
    bspec = pl.BlockSpec((1, 2, NA_QB, NA_KU), lambda hp, g: (_na_var(g), hp, 0, 0))
    return pl.pallas_call(
        body, name="na_bwd", grid=(4, NA_GROUPS),
        in_specs=[col(0), col(4), col(8), ocol, pl.BlockSpec((1, T, 128), lambda hp, g: (hp, 0, 0)), bspec],
        out_specs=(ocol, ocol, ocol, bspec),
        out_shape=(_sds((T, 512), F32), _sds((T, 512), F32), _sds((T, 512), F32), _sds((3, NA_HEADS, NA_QB, NA_KU), F32)),
        scratch_shapes=[pltpu.VMEM((NA_KU, 128), BF16), pltpu.VMEM((NA_KU, 128), BF16)],
        compiler_params=_cp(("parallel", "arbitrary")))(p_act, p_act, p_act, do, lse, bias_tab)


def _na_rpb_reduce(dbias):
    def body(db_ref, o_ref):
        lane = lax.broadcasted_iota(jnp.int32, (GRID_W, 128), 1)
        row3 = lax.broadcasted_iota(jnp.int32, (15, GRID_W, 128), 1)
        lane3 = lax.broadcasted_iota(jnp.int32, (15, GRID_W, 128), 2)
        accs = []
        for a in range(15):
            acc = jnp.zeros((GRID_W, 128), F32)
            for var in range(3):
                for i in range(4):
                    for j in range(NA_UROWS):
                        if _na_row_offset(var, i, j) == a:
                            pair = db_ref[var, 0, i * 64:(i + 1) * 64, (j // 2) * 128:(j // 2 + 1) * 128]
                            acc = acc + jnp.where((lane < GRID_W) if j % 2 == 0 else (lane >= GRID_W), pair, 0.0)
            accs.append(acc)
        z = jnp.stack(accs)
        z = jnp.where(lane3 < GRID_W, z + jnp.roll(z, GRID_W, axis=2), 0.0)
        for bit in range(6):
            sh = 1 << bit
            z = jnp.where((row3 & sh) != 0, jnp.roll(z, 128 - sh, axis=2), z)
        z = jnp.roll(z, 15, axis=2)
        o_ref[0] = jnp.sum(z, axis=1)

    return pl.pallas_call(
        body, name="na_rpb_reduce", grid=(NA_HEADS,),
        in_specs=[pl.BlockSpec((3, 1, NA_QB, NA_KU), lambda h: (0, h, 0, 0))],
        out_specs=pl.BlockSpec((1, 15, 128), lambda h: (h, 0, 0)), out_shape=_sds((NA_HEADS, 15, 128), F32),
        compiler_params=_cp(("parallel",)))(dbias)


HG_RB = 128
HG_NB = T // HG_RB
HG_SLOTS = HG_NB * 8
HI = lax.Precision.HIGHEST
HG_UNROLL = 4


def _chunk_tri(lower):
    r = lax.broadcasted_iota(jnp.int32, (HG_RB, HG_RB), 0)
    c = lax.broadcasted_iota(jnp.int32, (HG_RB, HG_RB), 1)
    same = (r // HG_C) == (c // HG_C)
    keep = (c <= r) if lower else (c >= r)
    return jnp.where(same & keep, 1.0, 0.0).astype(F32)


def _hg_gate_terms(z, lg):
    dl = lg[0:1, :] - lg[1:2, :]
    log_lb = jax.nn.log_sigmoid(dl)
    log_1mlb = jax.nn.log_sigmoid(-dl)
    yz = log_1mlb + jax.nn.log_sigmoid(z)
    log_f = jnp.logaddexp(log_lb, yz)
    snz = jax.nn.sigmoid(-z)
    k = jnp.exp(log_1mlb) * snz
    w2 = jnp.exp(yz - log_f)
    return log_f, k, snz, w2


def _hg_pre(p_act, logits):
    def body(q_ref, zf_ref, zb_ref, lg_ref, qh_ref, kf_ref, bf_ref, kb_ref, bb_ref):
        qh_ref[...] = jax.nn.silu(q_ref[...])
        lf, kf, _, _ = _hg_gate_terms(zf_ref[...], lg_ref[0])
        kf_ref[...] = kf
        bf_ref[...] = jnp.dot(_chunk_tri(True), lf, precision=HI, preferred_element_type=F32)
        lb_, kb, _, _ = _hg_gate_terms(zb_ref[...], lg_ref[1])
        kb_ref[...] = kb
        bb_ref[...] = jnp.dot(_chunk_tri(False), lb_, precision=HI, preferred_element_type=F32)

    blk = lambda c: pl.BlockSpec((HG_RB, 512), lambda i: (i, c))
    ob = pl.BlockSpec((HG_RB, 512), lambda i: (i, 0))
    return pl.pallas_call(
        body, name="hg_pre", grid=(HG_NB,),
        in_specs=[blk(3), blk(4), blk(5), pl.BlockSpec((2, 2, 512), lambda i: (0, 0, 0))],
        out_specs=(ob,) * 5, out_shape=(_sds((T, 512), F32),) * 5,
        compiler_params=_cp(("parallel",)))(p_act, p_act, p_act, logits)


def _bdot(a, b, ca, cb):
    return lax.dot_general(a.astype(BF16), b.astype(BF16), (((ca,), (cb,)), ((0,), (0,))), preferred_element_type=F32)


HG_S = 8
HG_NS = HG_RB // HG_S


def _lane_sums(xs):
    l_io = lax.broadcasted_iota(jnp.int32, (HG_NS, HG_S, HG_S), 2)
    a = jnp.zeros((HG_NS, HG_S, HG_S), F32)
    for j, x in enumerate(xs):
        a = a + jnp.where(l_io == j, jnp.sum(x, axis=-1, keepdims=True), 0.0)
    return a


def _halves(x):
    y = x.reshape(8, 2, HG_S, x.shape[-1])
    return y[:, 0], y[:, 1]


def _join(first, second):
    return jnp.stack([first, second], axis=1).reshape(HG_RB, first.shape[-1])


def _cross_split(rev, b4):
    b_1, b_2 = _halves(b4)
    if rev:
        r = b_2[:, 0:1, :]
        return jnp.exp(b_1 - r), jnp.exp(r - b_2)
    r = b_1[:, HG_S - 1:HG_S, :]
    return jnp.exp(b_2 - r), jnp.exp(r - b_1)


def _hg_scan_fwd(qh, k, b, p_act, rev):
    anchor = 0 if rev else HG_C - 1

    def body(q_ref, k_ref, b_ref, v_ref, o_ref, st_ref, dsc):
        def phase_a(blk, _):
            rows = pl.ds(pl.multiple_of(blk * HG_RB, HG_RB), HG_RB)
            b3 = b_ref[rows, :].reshape(8, HG_C, 128)
            k3 = k_ref[rows, :].reshape(8, HG_C, 128)
            v3 = v_ref[rows, :].reshape(8, HG_C, 128)
            bl = b3[:, anchor:anchor + 1, :]
            kt = k3 * jnp.exp(bl - b3)
            st_ref[0, pl.ds(pl.multiple_of(blk * 8, 8), 8)] = _bdot(v3, kt, 1, 1)
            dsc[pl.ds(pl.multiple_of(blk * 8, 8), 8), :] = jnp.exp(bl[:, 0, :])
            return 0

        lax.fori_loop(0, HG_NB, phase_a, 0, unroll=HG_UNROLL)

        def phase_b(n, carry):
            c = (NCHUNK - 1 - n) if rev else n
            u = st_ref[0, c]
            st_ref[0, c] = carry
            return carry * dsc[pl.ds(c, 1), :] + u

        lax.fori_loop(0, NCHUNK, phase_b, jnp.zeros((128, 128), F32))
        for c in range(NCHUNK, HG_SLOTS):
            st_ref[0, c] = jnp.zeros((128, 128), F32)

        t_io = lax.broadcasted_iota(jnp.int32, (HG_NS, HG_S, 128), 1)

        def phase_c(blk, _):
            rows = pl.ds(pl.multiple_of(blk * HG_RB, HG_RB), HG_RB)
            b4 = b_ref[rows, :].reshape(HG_NS, HG_S, 128)
            k4 = k_ref[rows, :].reshape(HG_NS, HG_S, 128)
            q4 = q_ref[rows, :].reshape(HG_NS, HG_S, 128)
            v4 = v_ref[rows, :].reshape(HG_NS, HG_S, 128)
            st = st_ref[0, pl.ds(pl.multiple_of(blk * 8, 8), 8)]
            o = _bdot((q4 * jnp.exp(b4)).reshape(8, HG_C, 128), st, 2, 2).reshape(HG_RB, 128)
            terms = []
            for s in range(HG_S):
                ok = (t_io <= s) if rev else (t_io >= s)
                f = jnp.exp(jnp.where(ok, b4 - b4[:, s:s + 1, :], NEG))
                terms.append(q4 * f * k4[:, s:s + 1, :])
            o_in = _bdot(_lane_sums(terms), v4, 2, 1)
            wq, wk = _cross_split(rev, b4)
            q_1, q_2 = _halves(q4)
            k_1, k_2 = _halves(k4)
            v_1, v_2 = _halves(v4)
            o_1, o_2 = _halves(o_in)
            if rev:
                o_1 = o_1 + _bdot(_bdot(q_1 * wq, k_2 * wk, 2, 2), v_2, 2, 1)
            else:
                o_2 = o_2 + _bdot(_bdot(q_2 * wq, k_1 * wk, 2, 2), v_1, 2, 1)
            o_ref[rows, :] = o + _join(o_1, o_2)
            return 0

        lax.fori_loop(0, HG_NB, phase_c, 0, unroll=HG_UNROLL)

    col = pl.BlockSpec((T, 128), lambda h: (0, h))
    return pl.pallas_call(
        body, name="hg_scan_bwd_dir" if rev else "hg_scan_fwd_dir", grid=(HG_HEADS,),
        in_specs=[col, col, col, pl.BlockSpec((T, 128), lambda h: (0, 24 + h))],
        out_specs=(col, pl.BlockSpec((1, HG_SLOTS, 128, 128), lambda h: (h, 0, 0, 0))),
        out_shape=(_sds((T, 512), F32), _sds((HG_HEADS, HG_SLOTS, 128, 128), F32)),
        scratch_shapes=[pltpu.VMEM((HG_SLOTS, 128), F32)],
        compiler_params=_cp(("parallel",), 56))(qh, k, b, p_act)


def _hg_scan_bwd(qh, k, b, p_act, st, do, rev):
    anchor = 0 if rev else HG_C - 1

    def body(q_ref, k_ref, b_ref, v_ref, st_ref, do_ref, dq_ref, dk_ref, db_ref, dv_ref, gst, dsc, dbl):
        def phase_a(blk, _):
            rows = pl.ds(pl.multiple_of(blk * HG_RB, HG_RB), HG_RB)
            b3 = b_ref[rows, :].reshape(8, HG_C, 128)
            q3 = q_ref[rows, :].reshape(8, HG_C, 128)
            do3 = do_ref[rows, :].reshape(8, HG_C, 128)
            gst[pl.ds(pl.multiple_of(blk * 8, 8), 8)] = _bdot(do3, q3 * jnp.exp(b3), 1, 1)
            dsc[pl.ds(pl.multiple_of(blk * 8, 8), 8), :] = jnp.exp(b3[:, anchor, :])
            return 0

        lax.fori_loop(0, HG_NB, phase_a, 0, unroll=HG_UNROLL)

        def phase_b(n, carry):
            c = n if rev else (NCHUNK - 1 - n)
            w = gst[c]
            gst[c] = carry
            dcv = dsc[pl.ds(c, 1), :]
            dbl[pl.ds(c, 1), :] = dcv * jnp.sum(st_ref[0, c] * carry, axis=0, keepdims=True)
            return carry * dcv + w

        lax.fori_loop(0, NCHUNK, phase_b, jnp.zeros((128, 128), F32))
        for c in range(NCHUNK, HG_SLOTS):
            gst[c] = jnp.zeros((128, 128), F32)
            dbl[c:c + 1, :] = jnp.zeros((1, 128), F32)

        t_io = lax.broadcasted_iota(jnp.int32, (HG_NS, HG_S, 128), 1)
        t16 = lax.broadcasted_iota(jnp.int32, (8, HG_C, 128), 1)
        r_io = lax.broadcasted_iota(jnp.int32, (HG_NS, HG_S, HG_S), 1)
        l_io = lax.broadcasted_iota(jnp.int32, (HG_NS, HG_S, HG_S), 2)

        def phase_c(blk, _):
            rows = pl.ds(pl.multiple_of(blk * HG_RB, HG_RB), HG_RB)
            cs = pl.ds(pl.multiple_of(blk * 8, 8), 8)
            b4 = b_ref[rows, :].reshape(HG_NS, HG_S, 128)
            k4 = k_ref[rows, :].reshape(HG_NS, HG_S, 128)
            q4 = q_ref[rows, :].reshape(HG_NS, HG_S, 128)
            v4 = v_ref[rows, :].reshape(HG_NS, HG_S, 128)
            do4 = do_ref[rows, :].reshape(HG_NS, HG_S, 128)
            b3, k3, q3 = (z.reshape(8, HG_C, 128) for z in (b4, k4, q4))
            v3, do3 = v4.reshape(8, HG_C, 128), do4.reshape(8, HG_C, 128)
            s_t = st_ref[0, cs]
            g_t = gst[cs]
            bl = b3[:, anchor:anchor + 1, :]
            ekl = jnp.exp(bl - b3)
            kt = k3 * ekl
            dkt = _bdot(v3, g_t, 2, 1)
            dq = (_bdot(do3, s_t, 2, 1) * jnp.exp(b3)).reshape(HG_NS, HG_S, 128)
            dk = (dkt * ekl).reshape(HG_NS, HG_S, 128)
            dv = _bdot(kt, g_t, 2, 2).reshape(HG_NS, HG_S, 128)
            dbl3 = dbl[cs, :].reshape(8, 1, 128) + jnp.sum(dkt * kt, axis=1, keepdims=True)
            causal = (l_io >= r_io) if rev else (l_io <= r_io)
            da = jnp.where(causal, _bdot(do4, v4, 2, 2), 0.0)
            causal_t = (l_io <= r_io) if rev else (l_io >= r_io)
            dat = jnp.where(causal_t, _bdot(v4, do4, 2, 2), 0.0)
            for s in range(HG_S):
                ok = (t_io <= s) if rev else (t_io >= s)
                f = jnp.exp(jnp.where(ok, b4 - b4[:, s:s + 1, :], NEG))
                dq = dq + da[:, :, s:s + 1] * (f * k4[:, s:s + 1, :])
            terms = []
            for t in range(HG_S):
                ok = (t_io >= t) if rev else (t_io <= t)
                e = jnp.exp(jnp.where(ok, b4[:, t:t + 1, :] - b4, NEG))
                eq = e * q4[:, t:t + 1, :]
                dk = dk + dat[:, :, t:t + 1] * eq
                terms.append(eq * k4)
            dv = dv + _bdot(_lane_sums(terms), do4, 2, 1)
            wq, wk = _cross_split(rev, b4)
            pick = (lambda z: _halves(z)) if rev else (lambda z: _halves(z)[::-1])
            (q_q, _), (_, k_k), (_, v_k), (do_q, _) = pick(q4), pick(k4), pick(v4), pick(do4)
            qx, kx = q_q * wq, k_k * wk
            dq_q = _bdot(_bdot(do_q, v_k, 2, 2), kx, 2, 1) * wq
            dk_k = _bdot(_bdot(v_k, do_q, 2, 2), qx, 2, 1) * wk
            dv_k = _bdot(_bdot(kx, qx, 2, 2), do_q, 2, 1)
            zero = jnp.zeros((8, HG_S, 128), F32)
            place_q = (lambda z: _join(z, zero)) if rev else (lambda z: _join(zero, z))
            place_k = (lambda z: _join(zero, z)) if rev else (lambda z: _join(z, zero))
            dq2 = dq.reshape(HG_RB, 128) + place_q(dq_q)
            dk2 = dk.reshape(HG_RB, 128) + place_k(dk_k)
            dv2 = dv.reshape(HG_RB, 128) + place_k(dv_k)
            dq3, dk3 = dq2.reshape(8, HG_C, 128), dk2.reshape(8, HG_C, 128)
            db = q3 * dq3 - k3 * dk3 + jnp.where(t16 == anchor, dbl3, 0.0)
            dq_ref[rows, :] = dq2
            dk_ref[rows, :] = dk2
            db_ref[rows, :] = db.reshape(HG_RB, 128)
            dv_ref[rows, :] = dv2
            return 0

        lax.fori_loop(0, HG_NB, phase_c, 0, unroll=HG_UNROLL)

    col = pl.BlockSpec((T, 128), lambda h: (0, h))
    return pl.pallas_call(
        body, name="hg_scan_bwd_dir_bwd" if rev else "hg_scan_fwd_dir_bwd", grid=(HG_HEADS,),
        in_specs=[col, col, col, pl.BlockSpec((T, 128), lambda h: (0, 24 + h)),
                  pl.BlockSpec((1, HG_SLOTS, 128, 128), lambda h: (h, 0, 0, 0)), col],
        out_specs=(col,) * 4, out_shape=(_sds((T, 512), F32),) * 4,
        scratch_shapes=[pltpu.VMEM((HG_SLOTS, 128, 128), F32), pltpu.VMEM((HG_SLOTS, 128), F32),
                        pltpu.VMEM((HG_SLOTS, 128), F32)],
        compiler_params=_cp(("parallel",), 56))(qh, k, b, p_act, st, do)


def _row_valid(i, tm):
    r = lax.broadcasted_iota(jnp.int32, (tm, 1), 0) + i * tm
    return r < L


def _hg_post(o_f, o_b, p_act, gain):
    def body(of_ref, ob_ref, g_ref, gain_ref, u_ref):
        o = of_ref[...] + ob_ref[...]
        sg = jax.nn.silu(g_ref[...])
        parts = []
        for h in range(HG_HEADS):
            oh = o[:, 128 * h:128 * (h + 1)]
            parts.append(oh * lax.rsqrt(jnp.mean(oh * oh, axis=-1, keepdims=True) + EPS))
        n = jnp.concatenate(parts, axis=1)
        u = n * gain_ref[...] * sg
        u_ref[...] = jnp.where(_row_valid(pl.program_id(0), TM_E), u, 0.0).astype(BF16)

    blk = pl.BlockSpec((TM_E, 512), lambda i: (i, 0))
    return pl.pallas_call(
        body, name="hg_post", grid=(T // TM_E,),
        in_specs=[blk, blk, pl.BlockSpec((TM_E, 512), lambda i: (i, 7)), pl.BlockSpec((1, 512), lambda i: (0, 0))],
        out_specs=blk, out_shape=_sds((T, 512), BF16), compiler_params=_cp(("parallel",)))(o_f, o_b, p_act, gain)


def _hg_post_bwd(du, o_f, o_b, p_act, gain):
    def body(du_ref, of_ref, ob_ref, g_ref, gain_ref, do_ref, dg_ref, dgain_ref):
        i = pl.program_id(0)
        valid = _row_valid(i, TM_E)
        duv = jnp.where(valid, du_ref[...], 0.0)
        o = of_ref[...] + ob_ref[...]
        gv = g_ref[...]
        sig = jax.nn.sigmoid(gv)
        sg = gv * sig
        gain_v = gain_ref[...]
        dn = duv * gain_v * sg
        do_parts, n_parts = [], []
        for h in range(HG_HEADS):
            sl = slice(128 * h, 128 * (h + 1))
            oh = o[:, sl]
            r = lax.rsqrt(jnp.mean(oh * oh, axis=-1, keepdims=True) + EPS)
            nh = oh * r
            dnh = dn[:, sl]
            do_parts.append(r * (dnh - nh * jnp.mean(dnh * nh, axis=-1, keepdims=True)))
            n_parts.append(nh)
        n = jnp.where(valid, jnp.concatenate(n_parts, axis=1), 0.0)
        do_ref[...] = jnp.where(valid, jnp.concatenate(do_parts, axis=1), 0.0)
        dg_ref[...] = (duv * n * gain_v * (sig * (1.0 + gv * (1.0 - sig)))).astype(BF16)
        part = jnp.sum(duv * n * sg, axis=0, keepdims=True)

        @pl.when(i == 0)
        def _():
            dgain_ref[...] = part

        @pl.when(i > 0)
        def _():
            dgain_ref[...] += part

    blk = pl.BlockSpec((TM_E, 512), lambda i: (i, 0))
    vec = pl.BlockSpec((1, 512), lambda i: (0, 0))
    return pl.pallas_call(
        body, name="hg_post_bwd", grid=(T // TM_E,),
        in_specs=[blk, blk, blk, pl.BlockSpec((TM_E, 512), lambda i: (i, 7)), vec],
        out_specs=(blk, blk, vec), out_shape=(_sds((T, 512), F32), _sds((T, 512), BF16), _sds((1, 512), F32)),
        compiler_params=_cp(("arbitrary",)))(du, o_f, o_b, p_act, gain)


def _hg_pre_bwd(p_act, logits, dq_f, dq_b, dk_f, dk_b, db_f, db_b, dv_f, dv_b):
    def body(q_ref, zf_ref, zb_ref, lg_ref, dqf_ref, dqb_ref, dkf_ref, dkb_ref, dbf_ref, dbb_ref, dvf_ref, dvb_ref,
             dq_ref, dzf_ref, dzb_ref, di_ref, dlg_ref):
        i = pl.program_id(0)
        valid = _row_valid(i, HG_RB)
        qv = q_ref[...]
        sig = jax.nn.sigmoid(qv)
        dq_ref[...] = jnp.where(valid, (dqf_ref[...] + dqb_ref[...]) * (sig * (1.0 + qv * (1.0 - sig))), 0.0).astype(BF16)
        di_ref[...] = jnp.where(valid, dvf_ref[...] + dvb_ref[...], 0.0).astype(BF16)
        for d, (z_ref, dk_r, db_r, dz_ref) in enumerate(((zf_ref, dkf_ref, dbf_ref, dzf_ref), (zb_ref, dkb_ref, dbb_ref, dzb_ref))):
            lg = lg_ref[d]
            dl = lg[0:1, :] - lg[1:2, :]
            lb = jax.nn.sigmoid(dl)
            one_m_lb = jax.nn.sigmoid(-dl)
            log_f, _, snz, w2 = _hg_gate_terms(z_ref[...], lg)
            dbv = jnp.where(valid, db_r[...], 0.0)
            dkv = jnp.where(valid, dk_r[...], 0.0)
            dlf = jnp.dot(_chunk_tri(d == 1), dbv, precision=HI, preferred_element_type=F32)
            sz = 1.0 - snz
            dz_ref[...] = (dlf * w2 * snz - dkv * one_m_lb * sz * snz).astype(BF16)
            dlb = jnp.sum(dlf * snz * jnp.exp(-log_f) - dkv * snz, axis=0, keepdims=True)
            dl0 = dlb * lb * one_m_lb
            part = jnp.concatenate([dl0, -dl0], axis=0)

            @pl.when(i == 0)
            def _():
                dlg_ref[d] = part

            @pl.when(i > 0)
            def _():
                dlg_ref[d] += part

    blk = lambda c: pl.BlockSpec((HG_RB, 512), lambda i: (i, c))
    ob = pl.BlockSpec((HG_RB, 512), lambda i: (i, 0))
    lgs = pl.BlockSpec((2, 2, 512), lambda i: (0, 0, 0))
    return pl.pallas_call(
        body, name="hg_pre_bwd", grid=(HG_NB,),
        in_specs=[blk(3), blk(4), blk(5), lgs] + [ob] * 8,
        out_specs=(ob, ob, ob, ob, lgs),
        out_shape=(_sds((T, 512), BF16),) * 4 + (_sds((2, 2, 512), F32),),
        compiler_params=_cp(("arbitrary",)))(p_act, p_act, p_act, logits, dq_f, dq_b, dk_f, dk_b, db_f, db_b, dv_f, dv_b)


def _mix_fwd(o_na, u_hg, w_na, w_hg, p_act):
    def body(ona_ref, uhg_ref, wna_ref, whg_ref, gna_ref, ghg_ref, o_ref):
        y_na = _dot(ona_ref[...], wna_ref[...])
        y_hg = _dot(uhg_ref[...], whg_ref[...])
        o_ref[...] = (jax.nn.sigmoid(gna_ref[...]) * y_na + jax.nn.sigmoid(ghg_ref[...]) * y_hg).astype(BF16)

    act = pl.BlockSpec((TM_B, 512), lambda i: (i, 0))
    wsp = pl.BlockSpec((512, D), lambda i: (0, 0))
    return pl.pallas_call(
        body, name="mix_fwd", grid=(T // TM_B,),
        in_specs=[act, act, wsp, wsp, pl.BlockSpec((TM_B, D), lambda i: (i, 4)), pl.BlockSpec((TM_B, D), lambda i: (i, 5))],
        out_specs=pl.BlockSpec((TM_B, D), lambda i: (i, 0)), out_shape=_sds((T, D), BF16),
        compiler_params=_cp(("parallel",)))(o_na, u_hg, w_na, w_hg, p_act, p_act)


def _mix_bwd(o_na, u_hg, w_na, w_hg, p_act, dmix):
    ni = T // TM_B

    def body(ona_ref, uhg_ref, wna_ref, whg_ref, gna_ref, ghg_ref, dmix_ref,
             dgna_ref, dghg_ref, dwna_ref, dwhg_ref, dona_ref, duhg_ref, acc_na, acc_hg):
        i = pl.program_id(0)
        dm = dmix_ref[...].astype(F32)
        for x_ref, w_ref, g_ref, dg_ref, dx_ref, dw_ref, acc in (
                (ona_ref, wna_ref, gna_ref, dgna_ref, dona_ref, dwna_ref, acc_na),
                (uhg_ref, whg_ref, ghg_ref, dghg_ref, duhg_ref, dwhg_ref, acc_hg)):
            xv = x_ref[...]
            y = _dot(xv, w_ref[...])
            sg = jax.nn.sigmoid(g_ref[...])
            dg_ref[...] = (dm * y * sg * (1.0 - sg)).astype(BF16)
            dy = (dm * sg).astype(BF16)
            dx_ref[...] = _dot(dy, w_ref[...], NT)
            part = _dot(xv, dy, TN)

            @pl.when(i == 0)
            def _():
                acc[...] = part

            @pl.when(i > 0)
            def _():
                acc[...] += part

            @pl.when(i == ni - 1)
            def _():
                dw_ref[...] = acc[...].astype(BF16)

    act = pl.BlockSpec((TM_B, 512), lambda i: (i, 0))
    wsp = pl.BlockSpec((512, D), lambda i: (0, 0))
    rblk = pl.BlockSpec((TM_B, D), lambda i: (i, 0))
    return pl.pallas_call(
        body, name="mix_bwd", grid=(ni,),
        in_specs=[act, act, wsp, wsp, pl.BlockSpec((TM_B, D), lambda i: (i, 4)), pl.BlockSpec((TM_B, D), lambda i: (i, 5)),
                  rblk],
        out_specs=(rblk, rblk, wsp, wsp, act, act),
        out_shape=(_sds((T, D), BF16), _sds((T, D), BF16), _sds((512, D), BF16), _sds((512, D), BF16),
                   _sds((T, 512), F32), _sds((T, 512), F32)),
        scratch_shapes=[pltpu.VMEM((512, D), F32), pltpu.VMEM((512, D), F32)],
        compiler_params=_cp(("arbitrary",)))(o_na, u_hg, w_na, w_hg, p_act, p_act, dmix)


def _wo_fwd(mix, w_o, h0, g_mlp):
    def body(mix_ref, w_ref, h0_ref, g_ref, h1_ref, m_ref):
        h1 = h0_ref[...] + _dot(mix_ref[...], w_ref[...])
        h1_ref[...] = h1
        r = lax.rsqrt(jnp.mean(h1 * h1, axis=-1, keepdims=True) + EPS)
        m_ref[...] = (h1 * r * g_ref[...]).astype(BF16)

    blk = pl.BlockSpec((TM_B, D), lambda i: (i, 0))
    return pl.pallas_call(
        body, name="wo_fwd", grid=(T // TM_B,),
        in_specs=[blk, pl.BlockSpec((D, D), lambda i: (0, 0)), blk, pl.BlockSpec((1, D), lambda i: (0, 0))],
        out_specs=(blk, blk), out_shape=(_sds((T, D), F32), _sds((T, D), BF16)),
        compiler_params=_cp(("parallel",)))(mix, w_o, h0, g_mlp)


def _wo_bwd(dh1_b, w_o, mix):
    ni = T // TM_B

    def body(dh_ref, w_ref, mix_ref, dmix_ref, dw_ref, acc):
        i = pl.program_id(0)
        dh = dh_ref[...]
        dmix_ref[...] = _dot(dh, w_ref[...], NT).astype(BF16)
        part = _dot(mix_ref[...], dh, TN)

        @pl.when(i == 0)
        def _():
            acc[...] = part

        @pl.when(i > 0)
        def _():
            acc[...] += part

        @pl.when(i == ni - 1)
        def _():
            dw_ref[...] = acc[...].astype(BF16)

    blk = pl.BlockSpec((TM_B, D), lambda i: (i, 0))
    wsp = pl.BlockSpec((D, D), lambda i: (0, 0))
    return pl.pallas_call(
        body, name="wo_bwd", grid=(ni,), in_specs=[blk, wsp, blk], out_specs=(blk, wsp),
        out_shape=(_sds((T, D), BF16), _sds((D, D), BF16)), scratch_shapes=[pltpu.VMEM((D, D), F32)],
        compiler_params=_cp(("arbitrary",)))(dh1_b, w_o, mix)


FF_B = D_FF // NDEV


def _loss_rows(xv, gv, tv, row0):
    r_io = lax.broadcasted_iota(jnp.int32, (xv.shape[0], 1), 0) + row0
    valid = (r_io >= NM) & (r_io < L)
    r = lax.rsqrt(jnp.mean(xv * xv, axis=-1, keepdims=True) + EPS)
    xh = xv * r
    err = jnp.where(valid, xh * gv - tv, 0.0)
    lpart = 0.5 * jnp.sum(jnp.sum(err * err, axis=-1, keepdims=True) * (1.0 / D), axis=0, keepdims=True)
    dy = err * (1.0 / D)
    dxh = dy * gv
    dh = r * (dxh - xh * jnp.mean(dxh * xh, axis=-1, keepdims=True))
    return lpart, dh, jnp.sum(dy * xh, axis=0, keepdims=True)


def _mlp_fwd_loss(m, wup_g, wdown_g, h1, g_final, tgt):
    nsub = TM_MM // TM_E

    def body(m_ref, wu_ref, wd_ref, h1_ref, g_ref, t_ref, loss_ref, dh_ref, dhb_ref, dg_ref, h2):
        i, j = pl.program_id(0), pl.program_id(1)
        up = jnp.maximum(_dot(m_ref[...], wu_ref[0]), 0.0)
        part = _dot((up * up).astype(BF16), wd_ref[0])

        @pl.when(j == 0)
        def _():
            h2[...] = h1_ref[...] + part

        @pl.when(j > 0)
        def _():
            h2[...] += part

        @pl.when(j == NDEV - 1)
        def _():
            lsum = jnp.zeros((1, 1), F32)
            gsum = jnp.zeros((1, D), F32)
            for s in range(nsub):
                rows = slice(s * TM_E, (s + 1) * TM_E)
                lpart, dh, gpart = _loss_rows(h2[rows, :], g_ref[...], t_ref[rows, :], i * TM_MM + s * TM_E)
                dh_ref[rows, :] = dh
                dhb_ref[rows, :] = dh.astype(BF16)
                lsum = lsum + lpart
                gsum = gsum + gpart
            lsum = jnp.broadcast_to(lsum, (1, 128))

            @pl.when(i == 0)
            def _():
                loss_ref[...] = lsum
                dg_ref[...] = gsum

            @pl.when(i > 0)
            def _():
                loss_ref[...] += lsum
                dg_ref[...] += gsum

    blk = pl.BlockSpec((TM_MM, D), lambda i, j: (i, 0))
    vec = pl.BlockSpec((1, D), lambda i, j: (0, 0))
    return pl.pallas_call(
        body, name="mlp_fwd_loss", grid=(T // TM_MM, NDEV),
        in_specs=[blk, pl.BlockSpec((1, D, FF_B), lambda i, j: (j, 0, 0)), pl.BlockSpec((1, FF_B, D), lambda i, j: (j, 0, 0)),
                  blk, vec, blk],
        out_specs=(pl.BlockSpec((1, 128), lambda i, j: (0, 0)), blk, blk, vec),
        out_shape=(_sds((1, 128), F32), _sds((T, D), F32), _sds((T, D), BF16), _sds((1, D), F32)),
        scratch_shapes=[pltpu.VMEM((TM_MM, D), F32)],
        compiler_params=_cp(("arbitrary", "arbitrary"), 56))(m, wup_g, wdown_g, h1, g_final, tgt)


def _mlp_bwd(m, dh2_b, wup_g, wdown_g, h1, g_mlp, dh2):
    ni = T // TM_B
    nsub = TM_B // TM_E

    def body(m_ref, dh_ref, wu_ref, wd_ref, h1_ref, g_ref, dres_ref, dwu_ref, dwd_ref, dh1_ref, dh1b_ref, dg_ref,
             dm_ref, acc_u, acc_d):
        j, i = pl.program_id(0), pl.program_id(1)
        rows = pl.ds(pl.multiple_of(i * TM_B, TM_B), TM_B)
        mv, dh = m_ref[...], dh_ref[...]
        r = jnp.maximum(_dot(mv, wu_ref[0]), 0.0)
        act = (r * r).astype(BF16)
        dact = _dot(dh, wd_ref[0], NT)
        dup = (dact * (2.0 * r)).astype(BF16)
        pd = _dot(act, dh, TN)
        pu = _dot(mv, dup, TN)
        dmv = _dot(dup, wu_ref[0], NT)

        @pl.when(i == 0)
        def _():
            acc_u[...] = pu
            acc_d[...] = pd

        @pl.when(i > 0)
        def _():
            acc_u[...] += pu
            acc_d[...] += pd

        @pl.when(i == ni - 1)
        def _():
            dwu_ref[0] = acc_u[...].astype(BF16)
            dwd_ref[0] = acc_d[...].astype(BF16)

        @pl.when(j == 0)
        def _():
            dm_ref[rows, :] = dmv

        @pl.when(j > 0)
        def _():
            dm_ref[rows, :] += dmv

        @pl.when(j == NDEV - 1)
        def _():
            gsum = jnp.zeros((1, D), F32)
            for s in range(nsub):
                sub = slice(s * TM_E, (s + 1) * TM_E)
                dm_rows = dm_ref[pl.ds(pl.multiple_of(i * TM_B + s * TM_E, TM_E), TM_E), :]
                dx, gpart = _norm_bwd_rows(h1_ref[sub, :], g_ref[...], dm_rows, dres_ref[sub, :])
                dh1_ref[sub, :] = dx
                dh1b_ref[sub, :] = dx.astype(BF16)
                gsum = gsum + gpart

            @pl.when(i == 0)
            def _():
                dg_ref[...] = gsum

            @pl.when(i > 0)
            def _():
                dg_ref[...] += gsum

    blk = pl.BlockSpec((TM_B, D), lambda j, i: (i, 0))
    late = pl.BlockSpec((TM_B, D), lambda j, i: (jnp.where(j == NDEV - 1, i, 0), 0))
    vec = pl.BlockSpec((1, D), lambda j, i: (0, 0))
    wus = pl.BlockSpec((1, D, FF_B), lambda j, i: (j, 0, 0))
    wds = pl.BlockSpec((1, FF_B, D), lambda j, i: (j, 0, 0))
    return pl.pallas_call(
        body, name="mlp_bwd", grid=(NDEV, ni), in_specs=[blk, blk, wus, wds, late, vec, late],
        out_specs=(wus, wds, late, late, vec),
        out_shape=(_sds((NDEV, D, FF_B), BF16), _sds((NDEV, FF_B, D), BF16), _sds((T, D), F32), _sds((T, D), BF16),
                   _sds((1, D), F32)),
        scratch_shapes=[pltpu.VMEM((T, D), F32), pltpu.VMEM((D, FF_B), F32), pltpu.VMEM((FF_B, D), F32)],
        compiler_params=_cp(("arbitrary", "arbitrary"), 56))(m, dh2_b, wup_g, wdown_g, h1, g_mlp, dh2)


def _adamw(parts, w, m, v, name):
    rr, cc = w.shape
    tr = rr
    for cand in (256, 128, 64):
        if rr % cand == 0 and rr > cand:
            tr = cand
            break
    c1 = 1.0 - ADAM_B1 ** ADAM_STEP
    c2 = 1.0 - ADAM_B2 ** ADAM_STEP

    def body(p_ref, w_ref, m_ref, v_ref, g_ref, d_ref, nm_ref, nv_ref):
        g = p_ref[0].astype(F32)
        for s in range(1, NDEV):
            g = g + p_ref[s].astype(F32)
        mn = ADAM_B1 * m_ref[...] + (1.0 - ADAM_B1) * g
        vn = ADAM_B2 * v_ref[...] + (1.0 - ADAM_B2) * (g * g)
        g_ref[...] = g
        nm_ref[...] = mn
        nv_ref[...] = vn
        d_ref[...] = -ADAM_LR * ((mn / c1) / (jnp.sqrt(vn / c2) + ADAM_EPS) + ADAM_WD * w_ref[...])

    blk = pl.BlockSpec((tr, cc), lambda i: (i, 0))
    return pl.pallas_call(
        body, name=name, grid=(rr // tr,),
        in_specs=[pl.BlockSpec((NDEV, tr, cc), lambda i: (0, i, 0)), blk, blk, blk],
        out_specs=(blk,) * 4, out_shape=(_sds((rr, cc), F32),) * 4,
        compiler_params=_cp(("parallel",)))(parts, w, m, v)


RPB_N = NA_HEADS * 15 * 31
RPB_PAD = 4096
OWN_ROWS = NM + 8


def _pad_rows(a, rows):
    return jnp.pad(a, ((0, rows - a.shape[0]),) + ((0, 0),) * (a.ndim - 1))


def _pack_owned(meta_blk, lb_blk):
    return jnp.concatenate([meta_blk, _pad_rows(lb_blk.reshape(2, 128), 8)], axis=0)


def _pack_replicated(n_mix, n_mlp, n_final, hg_gain, rpb):
    flat = _pad_rows(rpb.reshape(RPB_N), RPB_PAD)
    return jnp.concatenate([n_mix.reshape(8, 128), n_mlp.reshape(8, 128), n_final.reshape(8, 128),
                            _pad_rows(hg_gain.reshape(4, 128), 8), flat.reshape(32, 128)], axis=0)


def _unpack_replicated(a):
    return (a[0:8].reshape(1, D), a[8:16].reshape(1, D), a[16:24].reshape(D), a[24:28].reshape(1, 512),
            a[32:64].reshape(RPB_PAD)[:RPB_N].reshape(1, NA_HEADS, 15, 31))


def kernel(x, meta_tokens, w_in, w_na_out, w_hg_out, w_o, w_up, w_down, norm_mix, norm_mlp, norm_final, hg_norm, na_rpb, hg_lb_logits, loss_target, m_meta_tokens, m_w_in, m_w_na_out, m_w_hg_out, m_w_o, m_w_up, m_w_down, m_norm_mix, m_norm_mlp, m_norm_final, m_hg_norm, m_na_rpb, m_hg_lb_logits, v_meta_tokens, v_w_in, v_w_na_out, v_w_hg_out, v_w_o, v_w_up, v_w_down, v_norm_mix, v_norm_mlp, v_norm_final, v_hg_norm, v_na_rpb, v_hg_lb_logits):
    owned = _pack_owned(meta_tokens, hg_lb_logits)
    win_g, owned_g = _gather_two_level([w_in[0].astype(BF16), owned], "gather_first")
    later = [w[0].astype(BF16) for w in (w_na_out, w_hg_out, w_o, w_up, w_down)]
    later[0] = _tie(later[0], owned_g, "tie_gather_rest")
    gather_rest, tok = _exchange_start(later, [False] * 5, "gather_rest_start")
    win_g = _tie(win_g, tok, "tie_inproj")
    meta_full = jnp.transpose(owned_g[:, 0:NM, :], (1, 0, 2)).reshape(NM, D)
    logits = jnp.transpose(owned_g[:, NM:NM + 2, :].reshape(NDEV, 2, 2, 64), (1, 2, 0, 3)).reshape(2, 2, 512)

    h0 = jnp.concatenate([meta_full, x[0], jnp.zeros((T - L, D), F32)], axis=0)
    tgt = jnp.concatenate([jnp.zeros((NM, D), F32), loss_target[0], jnp.zeros((T - L, D), F32)], axis=0)
    bias_tab = _na_bias_table(na_rpb[0])

    a, a_t = _norm_fwd_t(h0, norm_mix, "norm_mix_fwd")
    p_act = _inproj_fwd(a, win_g)
    o_na, lse = _na_fwd(p_act, bias_tab)
    qh, k_f, b_f, k_b, b_b = _hg_pre(p_act, logits)
    o_f, st_f = _hg_scan_fwd(qh, k_f, b_f, p_act, False)
    o_b, st_b = _hg_scan_fwd(qh, k_b, b_b, p_act, True)
    u_hg = _hg_post(o_f, o_b, p_act, hg_norm)
    wna_g, whg_g, wo_g, wup_g, wdown_g = _exchange_wait(gather_rest, [False] * 5, [u_hg, o_na], "gather_rest_wait")
    w_o_full = wo_g.reshape(D, D)
    w_na_full = jnp.transpose(wna_g, (1, 0, 2)).reshape(512, D)
    w_hg_full = jnp.transpose(whg_g, (1, 0, 2)).reshape(512, D)
    mix = _mix_fwd(o_na, u_hg, w_na_full, w_hg_full, p_act)
    h1, m_act = _wo_fwd(mix, w_o_full, h0, norm_mlp)
    loss_part, dh2, dh2_b, d_nfinal = _mlp_fwd_loss(m_act, wup_g, wdown_g, h1, norm_final.reshape(1, D), tgt)

    dwup_p, dwdown_p, dh1, dh1_b, d_nmlp = _mlp_bwd(m_act, dh2_b, wup_g, wdown_g, h1, norm_mlp, dh2)
    sc_mlp, tok = _exchange_start([dwup_p, dwdown_p], [True] * 2, "scatter_mlp_start")
    dmix, dwo = _wo_bwd(_tie(dh1_b, tok, "tie_wo_bwd"), w_o_full, mix)
    sc_wo, tok = _exchange_start([dwo.reshape(NDEV, D // NDEV, D)], [True], "scatter_wo_start")
    dgna, dghg, dwna, dwhg, do_na, du_hg = _mix_bwd(o_na, u_hg, w_na_full, w_hg_full, p_act, _tie(dmix, tok, "tie_mix_bwd"))
    owner_cols = lambda w: jnp.transpose(w.reshape(512, NDEV, D // NDEV), (1, 0, 2))
    sc_br, tok = _exchange_start([owner_cols(dwna), owner_cols(dwhg)], [True] * 2, "scatter_branch_start")
    du_hg = _tie(du_hg, tok, "tie_hg_post_bwd")
    do_hg, dg_hg, d_gain = _hg_post_bwd(du_hg, o_f, o_b, p_act, hg_norm)
    dq_f, dk_f, db_f, dv_f = _hg_scan_bwd(qh, k_f, b_f, p_act, st_f, do_hg, False)
    dq_b, dk_b, db_b, dv_b = _hg_scan_bwd(qh, k_b, b_b, p_act, st_b, do_hg, True)
    dq_hg, dz_f, dz_b, di_hg, d_logits = _hg_pre_bwd(p_act, logits, dq_f, dq_b, dk_f, dk_b, db_f, db_b, dv_f, dv_b)
    dq_na, dk_na, dv_na, dbias = _na_bwd(p_act, do_na, lse, bias_tab)
    dp = jnp.concatenate([dq_na.astype(BF16), dk_na.astype(BF16), dv_na.astype(BF16), dq_hg, dz_f, dz_b, di_hg, dg_hg,
                          dgna, dghg], axis=1)
    dwin_p = _inproj_bwd_dw(a_t, dp, "inproj_bwd_dw")
    sc_in, tok = _exchange_start([dwin_p], [True], "scatter_in_start")
    dh0, d_nmix = _inproj_bwd_da(_tie(dp, tok, "tie_inproj_bwd_da"), win_g, h0, norm_mix, dh1)
    d_rpb = _na_rpb_reduce(_tie(dbias, tok, "tie_rpb_reduce"))[:, :, :31]

    res = {}

    def update(nm, parts, w, mm, vv):
        res[nm] = [r[None] for r in _adamw(parts, w[0], mm[0], vv[0], "adamw_" + nm)]
        return res[nm][1]

    wup_r, wdown_r = _exchange_wait(sc_mlp, [True] * 2, [dh0, d_rpb], "scatter_mlp_wait")
    update("w_up", wup_r, w_up, m_w_up, v_w_up)
    last = update("w_down", wdown_r, w_down, m_w_down, v_w_down)
    (wo_r,) = _exchange_wait(sc_wo, [True], [last], "scatter_wo_wait")
    last = update("w_o", wo_r, w_o, m_w_o, v_w_o)
    wna_r, whg_r = _exchange_wait(sc_br, [True] * 2, [last], "scatter_branch_wait")
    update("w_na_out", wna_r, w_na_out, m_w_na_out, v_w_na_out)
    last = update("w_hg_out", whg_r, w_hg_out, m_w_hg_out, v_w_hg_out)

    d_meta = jnp.transpose(dh0[0:NM].reshape(NM, NDEV, 128), (1, 0, 2))
    d_lg = jnp.transpose(d_logits.reshape(2, 2, NDEV, 64), (2, 0, 1, 3)).reshape(NDEV, 2, 128)
    owned_p = jnp.concatenate([d_meta, jnp.pad(d_lg, ((0, 0), (0, OWN_ROWS - NM - 2), (0, 0)))], axis=1)
    repl_p = _pack_replicated(d_nmix, d_nmlp, d_nfinal, d_gain, d_rpb)
    owned_r, repl_r = _exchange([_tie(owned_p, last, "tie_scatter_small"), repl_p], [True, False], "scatter_small")
    own = _adamw(owned_r, owned, _pack_owned(m_meta_tokens, m_hg_lb_logits), _pack_owned(v_meta_tokens, v_hg_lb_logits),
                 "adamw_owned_small")
    res["meta_tokens"] = [r[0:NM] for r in own]
    res["hg_lb_logits"] = [r[NM:NM + 2].reshape(2, 2, 64) for r in own]
    rep = _adamw(repl_r, _pack_replicated(norm_mix, norm_mlp, norm_final, hg_norm, na_rpb),
                 _pack_replicated(m_norm_mix, m_norm_mlp, m_norm_final, m_hg_norm, m_na_rpb),
                 _pack_replicated(v_norm_mix, v_norm_mlp, v_norm_final, v_hg_norm, v_na_rpb), "adamw_replicated")
    for q in range(4):
        um = _unpack_replicated(rep[q])
        for nm, val in zip(("norm_mix", "norm_mlp", "norm_final", "hg_norm", "na_rpb"), um):
            res.setdefault(nm, [None] * 4)[q] = val
    (win_r,) = _exchange_wait(sc_in, [True], [rep[1], own[1]], "scatter_in_wait")
    update("w_in", win_r, w_in, m_w_in, v_w_in)

    loss = lax.psum(loss_part[0, 0], ("x", "y", "c"))
    grad_x = dh0[NM:L][None]
    order = ("meta_tokens", "w_in", "w_na_out", "w_hg_out", "w_o", "w_up", "w_down", "norm_mix", "norm_mlp", "norm_final",
             "hg_norm", "na_rpb", "hg_lb_logits")
    outs = [loss, grad_x]
    for q in range(4):
        outs += [res[nm][q] for nm in order]
    return tuple(outs)
```

```python
import functools

import numpy as np
import jax
import jax.numpy as jnp
from jax import lax
from jax.experimental import pallas as pl
from jax.experimental.pallas import tpu as pltpu

F32 = jnp.float32
BF16 = jnp.bfloat16

D = 1024
SEQ = 2048
NM = 16
L = SEQ + NM
T = 2176
NDEV = 8
EPS = 1e-6
GRID_W = 64
ROWS = SEQ // GRID_W
NA_HEADS = 8
NA_DH = 64
NA_SCALE = NA_DH ** -0.5
HG_HEADS = 4
HG_C = 16
NCHUNK = L // HG_C
D_FF = 4096
IN_COLS = 6144
NEG = -1e30

ADAM_LR = 0.001
ADAM_B1 = 0.9
ADAM_B2 = 0.999
ADAM_EPS = 1e-08
ADAM_WD = 0.01
ADAM_STEP = 10

MESH_ID = pl.DeviceIdType.MESH
ANY = pl.BlockSpec(memory_space=pl.ANY)

NN = (((1,), (0,)), ((), ()))
NT = (((1,), (1,)), ((), ()))
TN = (((0,), (0,)), ((), ()))


def _cp(sem=None, vmem_mb=48):
    return pltpu.CompilerParams(dimension_semantics=sem, vmem_limit_bytes=vmem_mb * 1024 * 1024)


def _dot(a, b, dims=NN):
    return lax.dot_general(a, b, dims, preferred_element_type=F32)


def _sds(shape, dtype):
    return jax.ShapeDtypeStruct(shape, dtype)


HBM = pl.BlockSpec(memory_space=pltpu.HBM)
SEM = pl.BlockSpec(memory_space=pltpu.SEMAPHORE)
EFFECT = pltpu.SideEffectType.DATAFLOW_SIDE_EFFECTING


def _exchange(arrs, scatter, name):
    n = len(arrs)
    out_shapes = []
    for a, sc in zip(arrs, scatter):
        out_shapes.append(_sds(a.shape if sc else (NDEV,) + a.shape, a.dtype))

    def body(*refs):
        ins, outs = refs[:n], refs[n:2 * n]
        send_sems, recv_sems, loc_sems = refs[2 * n:]
        me = 4 * lax.axis_index("x") + 2 * lax.axis_index("y") + lax.axis_index("c")
        copies = []
        for k in range(n):
            src_me = ins[k].at[me] if scatter[k] else ins[k]
            loc = pltpu.make_async_copy(src_me, outs[k].at[me], loc_sems.at[k])
            loc.start()
            copies.append(loc)
        remote = _peer_copies(ins, outs, scatter, send_sems, recv_sems)
        for cp in remote:
            cp.start()
        for cp in remote:
            cp.wait_recv()
        for cp in remote:
            cp.wait_send()
        for cp in copies:
            cp.wait()

    return pl.pallas_call(
        body, name=name, out_shape=tuple(out_shapes), in_specs=[ANY] * n, out_specs=tuple([ANY] * n),
        scratch_shapes=[pltpu.SemaphoreType.DMA((n * (NDEV - 1),)), pltpu.SemaphoreType.DMA((n * (NDEV - 1),)),
                        pltpu.SemaphoreType.DMA((n,))],
    )(*arrs)


def _forward_to_sibling(bufs, name):
    n = len(bufs)

    def body(*refs):
        ins, outs = refs[:n], refs[n:2 * n]
        send_sems, recv_sems = refs[2 * n:]
        x, y, c = lax.axis_index("x"), lax.axis_index("y"), lax.axis_index("c")
        copies = []
        for k in range(n):
            for j, (cx, cy) in enumerate(((1 - x, y), (x, 1 - y), (1 - x, 1 - y))):
                slot = 4 * cx + 2 * cy + c
                copies.append(pltpu.make_async_remote_copy(
                    src_ref=ins[k].at[slot], dst_ref=outs[k].at[slot], send_sem=send_sems.at[3 * k + j],
                    recv_sem=recv_sems.at[3 * k + j], device_id=(x, y, 1 - c), device_id_type=MESH_ID))
        for cp in copies:
            cp.start()
        for cp in copies:
            cp.wait_recv()
        for cp in copies:
            cp.wait_send()

    return pl.pallas_call(
        body, name=name, out_shape=tuple(_sds(b.shape, b.dtype) for b in bufs), in_specs=[ANY] * n,
        out_specs=tuple([ANY] * n), input_output_aliases={k: k for k in range(n)},
        scratch_shapes=[pltpu.SemaphoreType.DMA((3 * n,)), pltpu.SemaphoreType.DMA((3 * n,))],
    )(*bufs)


ALL_PEERS = tuple(range(1, NDEV))
SAME_CORE_AND_SIBLING = (1, 2, 4, 6)


def _peer_copies(srcs, lands, scatter, send_sems, recv_sems, masks=ALL_PEERS):
    x, y, c = lax.axis_index("x"), lax.axis_index("y"), lax.axis_index("c")
    me = 4 * x + 2 * y + c
    out = []
    for k in range(len(srcs)):
        for m in masks:
            px, py, pc = x ^ (m >> 2), y ^ ((m >> 1) & 1), c ^ (m & 1)
            src = srcs[k].at[4 * px + 2 * py + pc] if scatter[k] else srcs[k]
            out.append(pltpu.make_async_remote_copy(
                src_ref=src, dst_ref=lands[k].at[me], send_sem=send_sems.at[k * (NDEV - 1) + m - 1],
                recv_sem=recv_sems.at[k * (NDEV - 1) + m - 1],
                device_id=(px, py, pc), device_id_type=MESH_ID))
    return out


def _exchange_start(arrs, scatter, name, masks=ALL_PEERS):
    n = len(arrs)
    me = 4 * lax.axis_index("x") + 2 * lax.axis_index("y") + lax.axis_index("c")
    lands = []
    for a, sc in zip(arrs, scatter):
        own = lax.dynamic_index_in_dim(a, me, 0, keepdims=True) if sc else a[None]
        shape = a.shape if sc else (NDEV,) + a.shape
        lands.append(lax.dynamic_update_index_in_dim(lax.empty(shape, a.dtype), own, me, 0))

    def body(*refs):
        srcs, lnds = refs[:n], refs[n:2 * n]
        send_sems, recv_sems = refs[2 * n], refs[2 * n + 1]
        token = refs[-1]
        for cp in _peer_copies(srcs, lnds, scatter, send_sems, recv_sems, masks):
            cp.start()
        token[...] = jnp.zeros_like(token)

    ops = [pltpu.with_memory_space_constraint(a, pltpu.HBM) for a in list(arrs) + lands]
    res = pl.pallas_call(
        body, name=name,
        out_shape=(pltpu.SemaphoreType.DMA((n * (NDEV - 1),)), pltpu.SemaphoreType.DMA((n * (NDEV - 1),)))
        + tuple(pltpu.HBM(o.shape, o.dtype) for o in ops) + (_sds((8, 128), F32),),
        in_specs=[HBM] * (2 * n), out_specs=(SEM, SEM) + (HBM,) * (2 * n) + (pl.BlockSpec(memory_space=pltpu.VMEM),),
        input_output_aliases={k: 2 + k for k in range(2 * n)},
        compiler_params=pltpu.CompilerParams(has_side_effects=EFFECT),
    )(*ops)
    return res[:-1], res[-1]


def _exchange_wait(handle, scatter, after, name, masks=ALL_PEERS):
    send_sems, recv_sems = handle[0], handle[1]
    bufs = handle[2:]
    n = len(bufs) // 2
    after = list(after)

    def body(*refs):
        srcs, lnds = refs[:n], refs[n:2 * n]
        for cp in _peer_copies(srcs, lnds, scatter, refs[2 * n], refs[2 * n + 1], masks):
            cp.wait_send()
            cp.wait_recv()

    res = pl.pallas_call(
        body, name=name, out_shape=tuple(pltpu.HBM(b.shape, b.dtype) for b in bufs),
        in_specs=[HBM] * (2 * n) + [SEM, SEM] + [ANY] * len(after), out_specs=(HBM,) * (2 * n),
        input_output_aliases={k: k for k in range(2 * n)},
        compiler_params=pltpu.CompilerParams(has_side_effects=EFFECT),
    )(*bufs, send_sems, recv_sems, *after)
    return res[n:]


def _tie(x, token, name):
    def body(x_ref, t_ref, o_ref):
        del x_ref, t_ref, o_ref

    return pl.pallas_call(body, name=name, out_shape=_sds(x.shape, x.dtype), in_specs=[ANY, ANY], out_specs=ANY,
                          input_output_aliases={0: 0})(x, token)


TM_E = 272


def _norm_fwd_t(h, g, name):
    def body(h_ref, g_ref, o_ref, ot_ref):
        xv = h_ref[...]
        r = lax.rsqrt(jnp.mean(xv * xv, axis=-1, keepdims=True) + EPS)
        y = xv * r * g_ref[...]
        o_ref[...] = y.astype(BF16)
        ot_ref[...] = y.T.astype(BF16)

    return pl.pallas_call(
        body, name=name, grid=(T // 128,),
        in_specs=[pl.BlockSpec((128, D), lambda i: (i, 0)), pl.BlockSpec((1, D), lambda i: (0, 0))],
        out_specs=(pl.BlockSpec((128, D), lambda i: (i, 0)), pl.BlockSpec((D, 128), lambda i: (0, i))),
        out_shape=(_sds((T, D), BF16), _sds((D, T), BF16)), compiler_params=_cp(("parallel",)))(h, g)


def _norm_bwd_rows(xv, gv, dnv, dres):
    r = lax.rsqrt(jnp.mean(xv * xv, axis=-1, keepdims=True) + EPS)
    xh = xv * r
    dxh = dnv * gv
    dx = dres + r * (dxh - xh * jnp.mean(dxh * xh, axis=-1, keepdims=True))
    return dx, jnp.sum(dnv * xh, axis=0, keepdims=True)


TM_MM = 1088


def _inproj_fwd(a, w_g):
    nb = w_g.shape[2]

    def body(a_ref, w_ref, o_ref):
        o_ref[...] = _dot(a_ref[...], w_ref[0])

    return pl.pallas_call(
        body, name="inproj_fwd", grid=(T // TM_MM, NDEV),
        in_specs=[pl.BlockSpec((TM_MM, D), lambda i, j: (i, 0)), pl.BlockSpec((1, D, nb), lambda i, j: (j, 0, 0))],
        out_specs=pl.BlockSpec((TM_MM, nb), lambda i, j: (i, j)), out_shape=_sds((T, NDEV * nb), F32),
        compiler_params=_cp(("parallel", "parallel")))(a, w_g)


TM_B = 544


W_IN_B = IN_COLS // NDEV


def _inproj_bwd_dw(a_t, dp, name):
    nblk = dp.shape[1] // W_IN_B

    def body(at_ref, dp_ref, dw_ref):
        dw_ref[0] = _dot(at_ref[...], dp_ref[...]).astype(BF16)

    return pl.pallas_call(
        body, name=name, grid=(nblk,),
        in_specs=[pl.BlockSpec((D, T), lambda j: (0, 0)), pl.BlockSpec((T, W_IN_B), lambda j: (0, j))],
        out_specs=pl.BlockSpec((1, D, W_IN_B), lambda j: (j, 0, 0)), out_shape=_sds((nblk, D, W_IN_B), BF16),
        compiler_params=_cp(("parallel",)))(a_t, dp)


def _inproj_bwd_da(dp, w_g, h0, g_mix, dh1):
    nsub = TM_MM // TM_E

    def body(dp_ref, w_ref, h0_ref, g_ref, dres_ref, dh0_ref, dg_ref, da):
        i, j = pl.program_id(0), pl.program_id(1)
        dav = _dot(dp_ref[...], w_ref[0], NT)

        @pl.when(j == 0)
        def _():
            da[...] = dav

        @pl.when(j > 0)
        def _():
            da[...] += dav

        @pl.when(j == NDEV - 1)
        def _():
            gsum = jnp.zeros((1, D), F32)
            for s in range(nsub):
                sub = slice(s * TM_E, (s + 1) * TM_E)
                dx, gpart = _norm_bwd_rows(h0_ref[sub, :], g_ref[...], da[sub, :], dres_ref[sub, :])
                dh0_ref[sub, :] = dx
                gsum = gsum + gpart

            @pl.when(i == 0)
            def _():
                dg_ref[...] = gsum

            @pl.when(i > 0)
            def _():
                dg_ref[...] += gsum

    rblk = pl.BlockSpec((TM_MM, D), lambda i, j: (i, 0))
    vec = pl.BlockSpec((1, D), lambda i, j: (0, 0))
    return pl.pallas_call(
        body, name="inproj_bwd_da", grid=(T // TM_MM, NDEV),
        in_specs=[pl.BlockSpec((TM_MM, W_IN_B), lambda i, j: (i, j)), pl.BlockSpec((1, D, W_IN_B), lambda i, j: (j, 0, 0)),
                  rblk, vec, rblk],
        out_specs=(rblk, vec), out_shape=(_sds((T, D), F32), _sds((1, D), F32)),
        scratch_shapes=[pltpu.VMEM((TM_MM, D), F32)],
        compiler_params=_cp(("arbitrary", "arbitrary"), 56))(dp, w_g, h0, g_mix, dh1)


NA_QB = 256
NA_GROUPS = ROWS // 4
NA_UROWS = 11
NA_KW = NA_UROWS * GRID_W
NA_KU = 768


def _na_row_offset(var, i, j):
    valid = (j < 8, i <= j < i + 8, 3 <= j < NA_UROWS)[var]
    return (j - i + (7, 3, 0)[var]) if valid else None


def _na_bias_table(rp):
    def body(r_ref, o_ref):
        row3 = lax.broadcasted_iota(jnp.int32, (15, GRID_W, 128), 1)
        lane3 = lax.broadcasted_iota(jnp.int32, (15, GRID_W, 128), 2)
        w3 = lane3 & (GRID_W - 1)
        cs3 = jnp.clip(row3 - 8, 0, GRID_W - 16)
        lane = lax.broadcasted_iota(jnp.int32, (GRID_W, 128), 1)
        neg = jnp.full((GRID_W, 128), NEG, F32)
        z = jnp.stack([jnp.broadcast_to(r_ref[0, a:a + 1, :], (GRID_W, 128)) for a in range(15)])
        for bit in range(6):
            sh = 1 << bit
            z = jnp.where((row3 & sh) != 0, jnp.roll(z, sh, axis=2), z)
        z = jnp.roll(z, 128 - 15, axis=2)
        z = jnp.where(lane3 < GRID_W, z, 0.0)
        z = z + jnp.roll(z, GRID_W, axis=2)
        tabs = jnp.where((w3 >= cs3) & (w3 < cs3 + 16), z, NEG)
        tail = jnp.where(lane < GRID_W + NM, 0.0, NEG)
        for var in range(3):
            for i in range(4):
                for jp in range(NA_KU // 128):
                    halves = []
                    for j in (2 * jp, 2 * jp + 1):
                        a = _na_row_offset(var, i, j) if j < NA_UROWS else None
                        halves.append(tail if j >= NA_UROWS else (neg if a is None else tabs[a]))
                    o_ref[var, 0, i * 64:(i + 1) * 64, jp * 128:(jp + 1) * 128] = jnp.where(lane < GRID_W, halves[0], halves[1])

    return pl.pallas_call(
        body, name="na_bias_table", grid=(NA_HEADS,),
        in_specs=[pl.BlockSpec((1, 15, 128), lambda h: (h, 0, 0))],
        out_specs=pl.BlockSpec((3, 1, NA_QB, NA_KU), lambda h: (0, h, 0, 0)),
        out_shape=_sds((3, NA_HEADS, NA_QB, NA_KU), F32), compiler_params=_cp(("parallel",)))(rp)


def _na_var(g):
    return jnp.where(g == 0, 0, jnp.where(g == NA_GROUPS - 1, 2, 1))


def _na_load_window(src_ref, dst, g):
    us = jnp.clip(4 * g - 4, 0, ROWS - NA_UROWS)
    kstart = pl.multiple_of(NM + GRID_W * us, 16)
    dst[0:NA_KW, :] = src_ref[pl.ds(kstart, NA_KW), :].astype(BF16)
    dst[NA_KW:NA_KW + NM, :] = src_ref[0:NM, :].astype(BF16)
    dst[NA_KW + NM:, :] = jnp.zeros((NA_KU - NA_KW - NM, 128), BF16)
    return kstart


def _na_fwd(p_act, bias_tab):
    def body(q_ref, k_ref, v_ref, b_ref, o_ref, lse_ref, ku, vu):
        g = pl.program_id(1)
        _na_load_window(k_ref, ku, g)
        _na_load_window(v_ref, vu, g)
        qstart = pl.multiple_of(NM + NA_QB * g, 16)
        q = q_ref[pl.ds(qstart, NA_QB), :]
        lane = lax.broadcasted_iota(jnp.int32, (NA_QB, 128), 1)
        o_h, lse_h = [], []
        for h in range(2):
            hm = (lane < 64) if h == 0 else (lane >= 64)
            qm = jnp.where(hm, q, 0.0).astype(BF16)
            s = _dot(qm, ku[...], NT) * NA_SCALE + b_ref[0, h]
            m = jnp.max(s, axis=-1, keepdims=True)
            p = jnp.exp(s - m)
            l = jnp.sum(p, axis=-1, keepdims=True)
            o_h.append(_dot(p.astype(BF16), vu[...]) / l)
            lse_h.append(jnp.broadcast_to(m + jnp.log(l), (NA_QB, 128)))
        o_ref[pl.ds(qstart, NA_QB), :] = jnp.where(lane < 64, o_h[0], o_h[1]).astype(BF16)
        lse_ref[0, pl.ds(qstart, NA_QB), :] = jnp.where(lane < 64, lse_h[0], lse_h[1])

        @pl.when(g == 0)
        def _():
            qm_ = q_ref[0:NM, :]
            lane_m = lax.broadcasted_iota(jnp.int32, (NM, 128), 1)
            km, vm = ku[NA_KW:NA_KW + NM, :], vu[NA_KW:NA_KW + NM, :]
            om = []
            for h in range(2):
                hm = (lane_m < 64) if h == 0 else (lane_m >= 64)
                s = _dot(jnp.where(hm, qm_, 0.0).astype(BF16), km, NT) * NA_SCALE
                p = jnp.exp(s - jnp.max(s, axis=-1, keepdims=True))
                l = jnp.sum(p, axis=-1, keepdims=True)
                om.append(_dot(p.astype(BF16), vm) / l)
            o_ref[0:NM, :] = jnp.where(lane_m < 64, om[0], om[1]).astype(BF16)
            o_ref[L:T, :] = jnp.zeros((T - L, 128), BF16)
            lse_ref[0, 0:NM, :] = jnp.zeros((NM, 128), F32)
            lse_ref[0, L:T, :] = jnp.zeros((T - L, 128), F32)

    col = lambda off: pl.BlockSpec((T, 128), lambda hp, g: (0, off + hp))
    return pl.pallas_call(
        body, name="na_fwd", grid=(4, NA_GROUPS),
        in_specs=[col(0), col(4), col(8),
                  pl.BlockSpec((1, 2, NA_QB, NA_KU), lambda hp, g: (_na_var(g), hp, 0, 0))],
        out_specs=(pl.BlockSpec((T, 128), lambda hp, g: (0, hp)), pl.BlockSpec((1, T, 128), lambda hp, g: (hp, 0, 0))),
        out_shape=(_sds((T, 512), BF16), _sds((4, T, 128), F32)),
        scratch_shapes=[pltpu.VMEM((NA_KU, 128), BF16), pltpu.VMEM((NA_KU, 128), BF16)],
        compiler_params=_cp(("parallel", "arbitrary")))(p_act, p_act, p_act, bias_tab)


def _na_bwd(p_act, do, lse, bias_tab):
    def body(q_ref, k_ref, v_ref, do_ref, lse_ref, b_ref, dq_ref, dk_ref, dv_ref, db_ref, ku, vu):
        g = pl.program_id(1)

        @pl.when(g == 0)
        def _():
            dq_ref[...] = jnp.zeros((T, 128), F32)
            dk_ref[...] = jnp.zeros((T, 128), F32)
            dv_ref[...] = jnp.zeros((T, 128), F32)

        kstart = _na_load_window(k_ref, ku, g)
        _na_load_window(v_ref, vu, g)
        qstart = pl.multiple_of(NM + NA_QB * g, 16)
        q = q_ref[pl.ds(qstart, NA_QB), :]
        dov = do_ref[pl.ds(qstart, NA_QB), :]
        lsev = lse_ref[0, pl.ds(qstart, NA_QB), :]
        lane = lax.broadcasted_iota(jnp.int32, (NA_QB, 128), 1)
        first = (g == 0) | (g == 1) | (g == NA_GROUPS - 1)
        dq_h = []
        dku = jnp.zeros((NA_KU, 128), F32)
        dvu = jnp.zeros((NA_KU, 128), F32)
        for h in range(2):
            hm = (lane < 64) if h == 0 else (lane >= 64)
            qm = jnp.where(hm, q, 0.0).astype(BF16)
            dom = jnp.where(hm, dov, 0.0).astype(BF16)
            s = _dot(qm, ku[...], NT) * NA_SCALE + b_ref[0, h]
            p = jnp.exp(s - lsev[:, 64 * h:64 * h + 1])
            dp = _dot(dom, vu[...], NT)
            delta = jnp.sum(p * dp, axis=-1, keepdims=True)
            ds = p * (dp - delta)

            @pl.when(first)
            def _():
                db_ref[0, h] = ds

            @pl.when(jnp.logical_not(first))
            def _():
                db_ref[0, h] += ds

            dsb = (ds * NA_SCALE).astype(BF16)
            dq_h.append(_dot(dsb, ku[...]))
            dku = dku + _dot(dsb, qm, TN)
            dvu = dvu + _dot(p.astype(BF16), dom, TN)
        dq_ref[pl.ds(qstart, NA_QB), :] = jnp.where(lane < 64, dq_h[0], dq_h[1])
        dk_ref[pl.ds(kstart, NA_KW), :] += dku[0:NA_KW]
        dv_ref[pl.ds(kstart, NA_KW), :] += dvu[0:NA_KW]
        dk_ref[0:NM, :] += dku[NA_KW:NA_KW + NM]
        dv_ref[0:NM, :] += dvu[NA_KW:NA_KW + NM]

        @pl.when(g == 0)
        def _():
            qm_ = q_ref[0:NM, :]
            dom_ = do_ref[0:NM, :]
            lane_m = lax.broadcasted_iota(jnp.int32, (NM, 128), 1)
            km, vm = ku[NA_KW:NA_KW + NM, :], vu[NA_KW:NA_KW + NM, :]
            dqs = []
            dkm = jnp.zeros((NM, 128), F32)
            dvm = jnp.zeros((NM, 128), F32)
            for h in range(2):
                hm = (lane_m < 64) if h == 0 else (lane_m >= 64)
                qh = jnp.where(hm, qm_, 0.0).astype(BF16)
                doh = jnp.where(hm, dom_, 0.0).astype(BF16)
                s = _dot(qh, km, NT) * NA_SCALE
                e = jnp.exp(s - jnp.max(s, axis=-1, keepdims=True))
                p = e / jnp.sum(e, axis=-1, keepdims=True)
                dp = _dot(doh, vm, NT)
                ds = p * (dp - jnp.sum(p * dp, axis=-1, keepdims=True))
                dsb = (ds * NA_SCALE).astype(BF16)
                dqs.append(_dot(dsb, km))
                dkm = dkm + _dot(dsb, qh, TN)
                dvm = dvm + _dot(p.astype(BF16), doh, TN)
            dq_ref[0:NM, :] = jnp.where(lane_m < 64, dqs[0], dqs[1])
            dk_ref[0:NM, :] += dkm
            dv_ref[0:NM, :] += dvm

    col = lambda off: pl.BlockSpec((T, 128), lambda hp, g: (0, off + hp))
    ocol = pl.BlockSpec((T, 128), lambda hp, g: (0, hp))
    bspec = pl.BlockSpec((1, 2, NA_QB, NA_KU), lambda hp, g: (_na_var(g), hp, 0, 0))
    return pl.pallas_call(
        body, name="na_bwd", grid=(4, NA_GROUPS),
        in_specs=[col(0), col(4), col(8), ocol, pl.BlockSpec((1, T, 128), lambda hp, g: (hp, 0, 0)), bspec],
        out_specs=(ocol, ocol, ocol, bspec),
        out_shape=(_sds((T, 512), F32), _sds((T, 512), F32), _sds((T, 512), F32), _sds((3, NA_HEADS, NA_QB, NA_KU), F32)),
        scratch_shapes=[pltpu.VMEM((NA_KU, 128), BF16), pltpu.VMEM((NA_KU, 128), BF16)],
        compiler_params=_cp(("parallel", "arbitrary")))(p_act, p_act, p_act, do, lse, bias_tab)


def _na_rpb_reduce(dbias):
    def body(db_ref, o_ref):
        lane = lax.broadcasted_iota(jnp.int32, (GRID_W, 128), 1)
        row3 = lax.broadcasted_iota(jnp.int32, (15, GRID_W, 128), 1)
        lane3 = lax.broadcasted_iota(jnp.int32, (15, GRID_W, 128), 2)
        accs = []
        for a in range(15):
            acc = jnp.zeros((GRID_W, 128), F32)
            for var in range(3):
                for i in range(4):
                    for j in range(NA_UROWS):
                        if _na_row_offset(var, i, j) == a:
                            pair = db_ref[var, 0, i * 64:(i + 1) * 64, (j // 2) * 128:(j // 2 + 1) * 128]
                            acc = acc + jnp.where((lane < GRID_W) if j % 2 == 0 else (lane >= GRID_W), pair, 0.0)
            accs.append(acc)
        z = jnp.stack(accs)
        z = jnp.where(lane3 < GRID_W, z + jnp.roll(z, GRID_W, axis=2), 0.0)
        for bit in range(6):
            sh = 1 << bit
            z = jnp.where((row3 & sh) != 0, jnp.roll(z, 128 - sh, axis=2), z)
        z = jnp.roll(z, 15, axis=2)
        o_ref[0] = jnp.sum(z, axis=1)

    return pl.pallas_call(
        body, name="na_rpb_reduce", grid=(NA_HEADS,),
        in_specs=[pl.BlockSpec((3, 1, NA_QB, NA_KU), lambda h: (0, h, 0, 0))],
        out_specs=pl.BlockSpec((1, 15, 128), lambda h: (h, 0, 0)), out_shape=_sds((NA_HEADS, 15, 128), F32),
        compiler_params=_cp(("parallel",)))(dbias)


HG_RB = 128
HG_NB = T // HG_RB
HG_SLOTS = HG_NB * 8
HI = lax.Precision.HIGHEST
HG_UNROLL = 4


def _chunk_tri(lower):
    r = lax.broadcasted_iota(jnp.int32, (HG_RB, HG_RB), 0)
    c = lax.broadcasted_iota(jnp.int32, (HG_RB, HG_RB), 1)
    same = (r // HG_C) == (c // HG_C)
    keep = (c <= r) if lower else (c >= r)
    return jnp.where(same & keep, 1.0, 0.0).astype(F32)


def _hg_gate_terms(z, lg):
    dl = lg[0:1, :] - lg[1:2, :]
    log_lb = jax.nn.log_sigmoid(dl)
    log_1mlb = jax.nn.log_sigmoid(-dl)
    yz = log_1mlb + jax.nn.log_sigmoid(z)
    log_f = jnp.logaddexp(log_lb, yz)
    snz = jax.nn.sigmoid(-z)
    k = jnp.exp(log_1mlb) * snz
    w2 = jnp.exp(yz - log_f)
    return log_f, k, snz, w2


def _hg_pre(p_act, logits):
    def body(q_ref, zf_ref, zb_ref, lg_ref, qh_ref, kf_ref, bf_ref, kb_ref, bb_ref):
        qh_ref[...] = jax.nn.silu(q_ref[...])
        lf, kf, _, _ = _hg_gate_terms(zf_ref[...], lg_ref[0])
        kf_ref[...] = kf
        bf_ref[...] = jnp.dot(_chunk_tri(True), lf, precision=HI, preferred_element_type=F32)
        lb_, kb, _, _ = _hg_gate_terms(zb_ref[...], lg_ref[1])
        kb_ref[...] = kb
        bb_ref[...] = jnp.dot(_chunk_tri(False), lb_, precision=HI, preferred_element_type=F32)

    blk = lambda c: pl.BlockSpec((HG_RB, 512), lambda i: (i, c))
    ob = pl.BlockSpec((HG_RB, 512), lambda i: (i, 0))
    return pl.pallas_call(
        body, name="hg_pre", grid=(HG_NB,),
        in_specs=[blk(3), blk(4), blk(5), pl.BlockSpec((2, 2, 512), lambda i: (0, 0, 0))],
        out_specs=(ob,) * 5, out_shape=(_sds((T, 512), F32),) * 5,
        compiler_params=_cp(("parallel",)))(p_act, p_act, p_act, logits)


def _bdot(a, b, ca, cb):
    return lax.dot_general(a.astype(BF16), b.astype(BF16), (((ca,), (cb,)), ((0,), (0,))), preferred_element_type=F32)


HG_S = 8
HG_NS = HG_RB // HG_S


def _lane_sums(xs):
    l_io = lax.broadcasted_iota(jnp.int32, (HG_NS, HG_S, HG_S), 2)
    a = jnp.zeros((HG_NS, HG_S, HG_S), F32)
    for j, x in enumerate(xs):
        a = a + jnp.where(l_io == j, jnp.sum(x, axis=-1, keepdims=True), 0.0)
    return a


def _halves(x):
    y = x.reshape(8, 2, HG_S, x.shape[-1])
    return y[:, 0], y[:, 1]


def _join(first, second):
    return jnp.stack([first, second], axis=1).reshape(HG_RB, first.shape[-1])


def _cross_split(rev, b4):
    b_1, b_2 = _halves(b4)
    if rev:
        r = b_2[:, 0:1, :]
        return jnp.exp(b_1 - r), jnp.exp(r - b_2)
    r = b_1[:, HG_S - 1:HG_S, :]
    return jnp.exp(b_2 - r), jnp.exp(r - b_1)


def _hg_scan_fwd(qh, k, b, p_act, rev):
    anchor = 0 if rev else HG_C - 1

    def body(q_ref, k_ref, b_ref, v_ref, o_ref, st_ref, dsc):
        def phase_a(blk, _):
            rows = pl.ds(pl.multiple_of(blk * HG_RB, HG_RB), HG_RB)
            b3 = b_ref[rows, :].reshape(8, HG_C, 128)
            k3 = k_ref[rows, :].reshape(8, HG_C, 128)
            v3 = v_ref[rows, :].reshape(8, HG_C, 128)
            bl = b3[:, anchor:anchor + 1, :]
            kt = k3 * jnp.exp(bl - b3)
            st_ref[0, pl.ds(pl.multiple_of(blk * 8, 8), 8)] = _bdot(v3, kt, 1, 1)
            dsc[pl.ds(pl.multiple_of(blk * 8, 8), 8), :] = jnp.exp(bl[:, 0, :])
            return 0

        lax.fori_loop(0, HG_NB, phase_a, 0, unroll=HG_UNROLL)

        def phase_b(n, carry):
            c = (NCHUNK - 1 - n) if rev else n
            u = st_ref[0, c]
            st_ref[0, c] = carry
            return carry * dsc[pl.ds(c, 1), :] + u

        lax.fori_loop(0, NCHUNK // 3, lambda n3, s: phase_b(3 * n3 + 2, phase_b(3 * n3 + 1, phase_b(3 * n3, s))),
                      jnp.zeros((128, 128), F32))
        for c in range(NCHUNK, HG_SLOTS):
            st_ref[0, c] = jnp.zeros((128, 128), F32)

        t_io = lax.broadcasted_iota(jnp.int32, (HG_NS, HG_S, 128), 1)

        def phase_c(blk, _):
            rows = pl.ds(pl.multiple_of(blk * HG_RB, HG_RB), HG_RB)
            b4 = b_ref[rows, :].reshape(HG_NS, HG_S, 128)
            k4 = k_ref[rows, :].reshape(HG_NS, HG_S, 128)
            q4 = q_ref[rows, :].reshape(HG_NS, HG_S, 128)
            v4 = v_ref[rows, :].reshape(HG_NS, HG_S, 128)
            st = st_ref[0, pl.ds(pl.multiple_of(blk * 8, 8), 8)]
            o = _bdot((q4 * jnp.exp(b4)).reshape(8, HG_C, 128), st, 2, 2).reshape(HG_RB, 128)
            terms = []
            for s in range(HG_S):
                ok = (t_io <= s) if rev else (t_io >= s)
                f = jnp.exp(jnp.where(ok, b4 - b4[:, s:s + 1, :], NEG))
                terms.append(q4 * f * k4[:, s:s + 1, :])
            o_in = _bdot(_lane_sums(terms), v4, 2, 1)
            wq, wk = _cross_split(rev, b4)
            q_1, q_2 = _halves(q4)
            k_1, k_2 = _halves(k4)
            v_1, v_2 = _halves(v4)
            o_1, o_2 = _halves(o_in)
            if rev:
                o_1 = o_1 + _bdot(_bdot(q_1 * wq, k_2 * wk, 2, 2), v_2, 2, 1)
            else:
                o_2 = o_2 + _bdot(_bdot(q_2 * wq, k_1 * wk, 2, 2), v_1, 2, 1)
            o_ref[rows, :] = o + _join(o_1, o_2)
            return 0

        lax.fori_loop(0, HG_NB, phase_c, 0, unroll=HG_UNROLL)

    col = pl.BlockSpec((T, 128), lambda h: (0, h))
    return pl.pallas_call(
        body, name="hg_scan_bwd_dir" if rev else "hg_scan_fwd_dir", grid=(HG_HEADS,),
        in_specs=[col, col, col, pl.BlockSpec((T, 128), lambda h: (0, 24 + h))],
        out_specs=(col, pl.BlockSpec((1, HG_SLOTS, 128, 128), lambda h: (h, 0, 0, 0))),
        out_shape=(_sds((T, 512), F32), _sds((HG_HEADS, HG_SLOTS, 128, 128), F32)),
        scratch_shapes=[pltpu.VMEM((HG_SLOTS, 128), F32)],
        compiler_params=_cp(("parallel",), 56))(qh, k, b, p_act)


def _hg_scan_bwd(qh, k, b, p_act, st, do, rev):
    anchor = 0 if rev else HG_C - 1

    def body(q_ref, k_ref, b_ref, v_ref, st_ref, do_ref, dq_ref, dk_ref, db_ref, dv_ref, gst, dsc, dbl):
        def phase_a(blk, _):
            rows = pl.ds(pl.multiple_of(blk * HG_RB, HG_RB), HG_RB)
            b3 = b_ref[rows, :].reshape(8, HG_C, 128)
            q3 = q_ref[rows, :].reshape(8, HG_C, 128)
            do3 = do_ref[rows, :].reshape(8, HG_C, 128)
            gst[pl.ds(pl.multiple_of(blk * 8, 8), 8)] = _bdot(do3, q3 * jnp.exp(b3), 1, 1)
            dsc[pl.ds(pl.multiple_of(blk * 8, 8), 8), :] = jnp.exp(b3[:, anchor, :])
            return 0

        lax.fori_loop(0, HG_NB, phase_a, 0, unroll=HG_UNROLL)

        def phase_b(n, carry):
            c = n if rev else (NCHUNK - 1 - n)
            w = gst[c]
            gst[c] = carry
            dcv = dsc[pl.ds(c, 1), :]
            dbl[pl.ds(c, 1), :] = dcv * jnp.sum(st_ref[0, c] * carry, axis=0, keepdims=True)
            return carry * dcv + w

        lax.fori_loop(0, NCHUNK // 3, lambda n3, s: phase_b(3 * n3 + 2, phase_b(3 * n3 + 1, phase_b(3 * n3, s))),
                      jnp.zeros((128, 128), F32))
        for c in range(NCHUNK, HG_SLOTS):
            gst[c] = jnp.zeros((128, 128), F32)
            dbl[c:c + 1, :] = jnp.zeros((1, 128), F32)

        t_io = lax.broadcasted_iota(jnp.int32, (HG_NS, HG_S, 128), 1)
        t16 = lax.broadcasted_iota(jnp.int32, (8, HG_C, 128), 1)
        r_io = lax.broadcasted_iota(jnp.int32, (HG_NS, HG_S, HG_S), 1)
        l_io = lax.broadcasted_iota(jnp.int32, (HG_NS, HG_S, HG_S), 2)

        def phase_c(blk, _):
            rows = pl.ds(pl.multiple_of(blk * HG_RB, HG_RB), HG_RB)
            cs = pl.ds(pl.multiple_of(blk * 8, 8), 8)
            b4 = b_ref[rows, :].reshape(HG_NS, HG_S, 128)
            k4 = k_ref[rows, :].reshape(HG_NS, HG_S, 128)
            q4 = q_ref[rows, :].reshape(HG_NS, HG_S, 128)
            v4 = v_ref[rows, :].reshape(HG_NS, HG_S, 128)
            do4 = do_ref[rows, :].reshape(HG_NS, HG_S, 128)
            b3, k3, q3 = (z.reshape(8, HG_C, 128) for z in (b4, k4, q4))
            v3, do3 = v4.reshape(8, HG_C, 128), do4.reshape(8, HG_C, 128)
            s_t = st_ref[0, cs]
            g_t = gst[cs]
            bl = b3[:, anchor:anchor + 1, :]
            ekl = jnp.exp(bl - b3)
            kt = k3 * ekl
            dkt = _bdot(v3, g_t, 2, 1)
            dq = (_bdot(do3, s_t, 2, 1) * jnp.exp(b3)).reshape(HG_NS, HG_S, 128)
            dk = (dkt * ekl).reshape(HG_NS, HG_S, 128)
            dv = _bdot(kt, g_t, 2, 2).reshape(HG_NS, HG_S, 128)
            dbl3 = dbl[cs, :].reshape(8, 1, 128) + jnp.sum(dkt * kt, axis=1, keepdims=True)
            causal = (l_io >= r_io) if rev else (l_io <= r_io)
            da = jnp.where(causal, _bdot(do4, v4, 2, 2), 0.0)
            causal_t = (l_io <= r_io) if rev else (l_io >= r_io)
            dat = jnp.where(causal_t, _bdot(v4, do4, 2, 2), 0.0)
            for s in range(HG_S):
                ok = (t_io <= s) if rev else (t_io >= s)
                f = jnp.exp(jnp.where(ok, b4 - b4[:, s:s + 1, :], NEG))
                dq = dq + da[:, :, s:s + 1] * (f * k4[:, s:s + 1, :])
            terms = []
            for t in range(HG_S):
                ok = (t_io >= t) if rev else (t_io <= t)
                e = jnp.exp(jnp.where(ok, b4[:, t:t + 1, :] - b4, NEG))
                eq = e * q4[:, t:t + 1, :]
                dk = dk + dat[:, :, t:t + 1] * eq
                terms.append(eq * k4)
            dv = dv + _bdot(_lane_sums(terms), do4, 2, 1)
            wq, wk = _cross_split(rev, b4)
            pick = (lambda z: _halves(z)) if rev else (lambda z: _halves(z)[::-1])
            (q_q, _), (_, k_k), (_, v_k), (do_q, _) = pick(q4), pick(k4), pick(v4), pick(do4)
            qx, kx = q_q * wq, k_k * wk
            dq_q = _bdot(_bdot(do_q, v_k, 2, 2), kx, 2, 1) * wq
            dk_k = _bdot(_bdot(v_k, do_q, 2, 2), qx, 2, 1) * wk
            dv_k = _bdot(_bdot(kx, qx, 2, 2), do_q, 2, 1)
            zero = jnp.zeros((8, HG_S, 128), F32)
            place_q = (lambda z: _join(z, zero)) if rev else (lambda z: _join(zero, z))
            place_k = (lambda z: _join(zero, z)) if rev else (lambda z: _join(z, zero))
            dq2 = dq.reshape(HG_RB, 128) + place_q(dq_q)
            dk2 = dk.reshape(HG_RB, 128) + place_k(dk_k)
            dv2 = dv.reshape(HG_RB, 128) + place_k(dv_k)
            dq3, dk3 = dq2.reshape(8, HG_C, 128), dk2.reshape(8, HG_C, 128)
            db = q3 * dq3 - k3 * dk3 + jnp.where(t16 == anchor, dbl3, 0.0)
            dq_ref[rows, :] = dq2
            dk_ref[rows, :] = dk2
            db_ref[rows, :] = db.reshape(HG_RB, 128)
            dv_ref[rows, :] = dv2
            return 0

        lax.fori_loop(0, HG_NB, phase_c, 0, unroll=HG_UNROLL)

    col = pl.BlockSpec((T, 128), lambda h: (0, h))
    return pl.pallas_call(
        body, name="hg_scan_bwd_dir_bwd" if rev else "hg_scan_fwd_dir_bwd", grid=(HG_HEADS,),
        in_specs=[col, col, col, pl.BlockSpec((T, 128), lambda h: (0, 24 + h)),
                  pl.BlockSpec((1, HG_SLOTS, 128, 128), lambda h: (h, 0, 0, 0)), col],
        out_specs=(col,) * 4, out_shape=(_sds((T, 512), F32),) * 4,
        scratch_shapes=[pltpu.VMEM((HG_SLOTS, 128, 128), F32), pltpu.VMEM((HG_SLOTS, 128), F32),
                        pltpu.VMEM((HG_SLOTS, 128), F32)],
        compiler_params=_cp(("parallel",), 56))(qh, k, b, p_act, st, do)


def _row_valid(i, tm):
    r = lax.broadcasted_iota(jnp.int32, (tm, 1), 0) + i * tm
    return r < L


def _hg_post_rows(o, gv, gain_v, valid):
    parts = []
    for h in range(HG_HEADS):
        oh = o[:, 128 * h:128 * (h + 1)]
        parts.append(oh * lax.rsqrt(jnp.mean(oh * oh, axis=-1, keepdims=True) + EPS))
    return jnp.where(valid, jnp.concatenate(parts, axis=1) * gain_v * jax.nn.silu(gv), 0.0)


def _hg_post_bwd_rows(du, o, gv, gain_v, valid):
    duv = jnp.where(valid, du, 0.0)
    sig = jax.nn.sigmoid(gv)
    sg = gv * sig
    dn = duv * gain_v * sg
    do_parts, n_parts = [], []
    for h in range(HG_HEADS):
        sl = slice(128 * h, 128 * (h + 1))
        oh = o[:, sl]
        r = lax.rsqrt(jnp.mean(oh * oh, axis=-1, keepdims=True) + EPS)
        nh = oh * r
        dnh = dn[:, sl]
        do_parts.append(r * (dnh - nh * jnp.mean(dnh * nh, axis=-1, keepdims=True)))
        n_parts.append(nh)
    n = jnp.where(valid, jnp.concatenate(n_parts, axis=1), 0.0)
    do = jnp.where(valid, jnp.concatenate(do_parts, axis=1), 0.0)
    dg = duv * n * gain_v * (sig * (1.0 + gv * (1.0 - sig)))
    return do, dg, jnp.sum(duv * n * sg, axis=0, keepdims=True)


def _hg_pre_bwd(p_act, logits, dq_f, dq_b, dk_f, dk_b, db_f, db_b, dv_f, dv_b):
    def body(q_ref, zf_ref, zb_ref, lg_ref, dqf_ref, dqb_ref, dkf_ref, dkb_ref, dbf_ref, dbb_ref, dvf_ref, dvb_ref,
             dq_ref, dzf_ref, dzb_ref, di_ref, dlg_ref):
        i = pl.program_id(0)
        valid = _row_valid(i, HG_RB)
        qv = q_ref[...]
        sig = jax.nn.sigmoid(qv)
        dq_ref[...] = jnp.where(valid, (dqf_ref[...] + dqb_ref[...]) * (sig * (1.0 + qv * (1.0 - sig))), 0.0).astype(BF16)
        di_ref[...] = jnp.where(valid, dvf_ref[...] + dvb_ref[...], 0.0).astype(BF16)
        for d, (z_ref, dk_r, db_r, dz_ref) in enumerate(((zf_ref, dkf_ref, dbf_ref, dzf_ref), (zb_ref, dkb_ref, dbb_ref, dzb_ref))):
            lg = lg_ref[d]
            dl = lg[0:1, :] - lg[1:2, :]
            lb = jax.nn.sigmoid(dl)
            one_m_lb = jax.nn.sigmoid(-dl)
            log_f, _, snz, w2 = _hg_gate_terms(z_ref[...], lg)
            dbv = jnp.where(valid, db_r[...], 0.0)
            dkv = jnp.where(valid, dk_r[...], 0.0)
            dlf = jnp.dot(_chunk_tri(d == 1), dbv, precision=HI, preferred_element_type=F32)
            sz = 1.0 - snz
            dz_ref[...] = (dlf * w2 * snz - dkv * one_m_lb * sz * snz).astype(BF16)
            dlb = jnp.sum(dlf * snz * jnp.exp(-log_f) - dkv * snz, axis=0, keepdims=True)
            dl0 = dlb * lb * one_m_lb
            part = jnp.concatenate([dl0, -dl0], axis=0)

            @pl.when(i == 0)
            def _():
                dlg_ref[d] = part

            @pl.when(i > 0)
            def _():
                dlg_ref[d] += part

    blk = lambda c: pl.BlockSpec((HG_RB, 512), lambda i: (i, c))
    ob = pl.BlockSpec((HG_RB, 512), lambda i: (i, 0))
    lgs = pl.BlockSpec((2, 2, 512), lambda i: (0, 0, 0))
    return pl.pallas_call(
        body, name="hg_pre_bwd", grid=(HG_NB,),
        in_specs=[blk(3), blk(4), blk(5), lgs] + [ob] * 8,
        out_specs=(ob, ob, ob, ob, lgs),
        out_shape=(_sds((T, 512), BF16),) * 4 + (_sds((2, 2, 512), F32),),
        compiler_params=_cp(("arbitrary",)))(p_act, p_act, p_act, logits, dq_f, dq_b, dk_f, dk_b, db_f, db_b, dv_f, dv_b)


def _mix_fwd(o_na, o_f, o_b, gain, w_na, w_hg, p_act):
    def body(ona_ref, of_ref, ob_ref, g_ref, gain_ref, wna_ref, whg_ref, gna_ref, ghg_ref, o_ref, u_ref):
        u = _hg_post_rows(of_ref[...] + ob_ref[...], g_ref[...], gain_ref[...], _row_valid(pl.program_id(0), TM_B)).astype(BF16)
        u_ref[...] = u
        y_na = _dot(ona_ref[...], wna_ref[...])
        y_hg = _dot(u, whg_ref[...])
        o_ref[...] = (jax.nn.sigmoid(gna_ref[...]) * y_na + jax.nn.sigmoid(ghg_ref[...]) * y_hg).astype(BF16)

    act = pl.BlockSpec((TM_B, 512), lambda i: (i, 0))
    wsp = pl.BlockSpec((512, D), lambda i: (0, 0))
    return pl.pallas_call(
        body, name="mix_fwd", grid=(T // TM_B,),
        in_specs=[act, act, act, pl.BlockSpec((TM_B, 512), lambda i: (i, 7)), pl.BlockSpec((1, 512), lambda i: (0, 0)),
                  wsp, wsp, pl.BlockSpec((TM_B, D), lambda i: (i, 4)), pl.BlockSpec((TM_B, D), lambda i: (i, 5))],
        out_specs=(pl.BlockSpec((TM_B, D), lambda i: (i, 0)), act), out_shape=(_sds((T, D), BF16), _sds((T, 512), BF16)),
        compiler_params=_cp(("parallel",)))(o_na, o_f, o_b, p_act, gain, w_na, w_hg, p_act, p_act)


def _mix_bwd(o_na, u_hg, o_f, o_b, gain, w_na, w_hg, p_act, dmix):
    ni = T // TM_B

    def body(ona_ref, uhg_ref, of_ref, ob_ref, g_ref, gain_ref, wna_ref, whg_ref, gna_ref, ghg_ref, dmix_ref,
             dgna_ref, dghg_ref, dwna_ref, dwhg_ref, dona_ref, do_ref, dg_ref, dgain_ref, acc_na, acc_hg):
        i = pl.program_id(0)
        dm = dmix_ref[...].astype(F32)
        dxs = []
        for x_ref, w_ref, gt_ref, dgt_ref, dw_ref, acc in (
                (ona_ref, wna_ref, gna_ref, dgna_ref, dwna_ref, acc_na), (uhg_ref, whg_ref, ghg_ref, dghg_ref, dwhg_ref, acc_hg)):
            xv = x_ref[...]
            y = _dot(xv, w_ref[...])
            sg = jax.nn.sigmoid(gt_ref[...])
            dgt_ref[...] = (dm * y * sg * (1.0 - sg)).astype(BF16)
            dy = (dm * sg).astype(BF16)
            dxs.append(_dot(dy, w_ref[...], NT))
            part = _dot(xv, dy, TN)

            @pl.when(i == 0)
            def _():
                acc[...] = part

            @pl.when(i > 0)
            def _():
                acc[...] += part

            @pl.when(i == ni - 1)
            def _():
                dw_ref[...] = acc[...].astype(BF16)

        dona_ref[...] = dxs[0]
        do, dg, gpart = _hg_post_bwd_rows(dxs[1], of_ref[...] + ob_ref[...], g_ref[...], gain_ref[...], _row_valid(i, TM_B))
        do_ref[...] = do
        dg_ref[...] = dg.astype(BF16)

        @pl.when(i == 0)
        def _():
            dgain_ref[...] = gpart

        @pl.when(i > 0)
        def _():
            dgain_ref[...] += gpart

    act = pl.BlockSpec((TM_B, 512), lambda i: (i, 0))
    wsp = pl.BlockSpec((512, D), lambda i: (0, 0))
    rblk = pl.BlockSpec((TM_B, D), lambda i: (i, 0))
    vec = pl.BlockSpec((1, 512), lambda i: (0, 0))
    return pl.pallas_call(
        body, name="mix_bwd", grid=(ni,),
        in_specs=[act, act, act, act, pl.BlockSpec((TM_B, 512), lambda i: (i, 7)), vec, wsp, wsp,
                  pl.BlockSpec((TM_B, D), lambda i: (i, 4)), pl.BlockSpec((TM_B, D), lambda i: (i, 5)), rblk],
        out_specs=(rblk, rblk, wsp, wsp, act, act, act, vec),
        out_shape=(_sds((T, D), BF16), _sds((T, D), BF16), _sds((512, D), BF16), _sds((512, D), BF16),
                   _sds((T, 512), F32), _sds((T, 512), F32), _sds((T, 512), BF16), _sds((1, 512), F32)),
        scratch_shapes=[pltpu.VMEM((512, D), F32), pltpu.VMEM((512, D), F32)],
        compiler_params=_cp(("arbitrary",)))(o_na, u_hg, o_f, o_b, p_act, gain, w_na, w_hg, p_act, p_act, dmix)


def _wo_fwd(mix, w_o, h0, g_mlp):
    def body(mix_ref, w_ref, h0_ref, g_ref, h1_ref, m_ref):
        h1 = h0_ref[...] + _dot(mix_ref[...], w_ref[...])
        h1_ref[...] = h1
        r = lax.rsqrt(jnp.mean(h1 * h1, axis=-1, keepdims=True) + EPS)
        m_ref[...] = (h1 * r * g_ref[...]).astype(BF16)

    blk = pl.BlockSpec((TM_B, D), lambda i: (i, 0))
    return pl.pallas_call(
        body, name="wo_fwd", grid=(T // TM_B,),
        in_specs=[blk, pl.BlockSpec((D, D), lambda i: (0, 0)), blk, pl.BlockSpec((1, D), lambda i: (0, 0))],
        out_specs=(blk, blk), out_shape=(_sds((T, D), F32), _sds((T, D), BF16)),
        compiler_params=_cp(("parallel",)))(mix, w_o, h0, g_mlp)


def _wo_bwd(dh1_b, w_o, mix):
    ni = T // TM_B

    def body(dh_ref, w_ref, mix_ref, dmix_ref, dw_ref, acc):
        i = pl.program_id(0)
        dh = dh_ref[...]
        dmix_ref[...] = _dot(dh, w_ref[...], NT).astype(BF16)
        part = _dot(mix_ref[...], dh, TN)

        @pl.when(i == 0)
        def _():
            acc[...] = part

        @pl.when(i > 0)
        def _():
            acc[...] += part

        @pl.when(i == ni - 1)
        def _():
            dw_ref[...] = acc[...].astype(BF16)

    blk = pl.BlockSpec((TM_B, D), lambda i: (i, 0))
    wsp = pl.BlockSpec((D, D), lambda i: (0, 0))
    return pl.pallas_call(
        body, name="wo_bwd", grid=(ni,), in_specs=[blk, wsp, blk], out_specs=(blk, wsp),
        out_shape=(_sds((T, D), BF16), _sds((D, D), BF16)), scratch_shapes=[pltpu.VMEM((D, D), F32)],
        compiler_params=_cp(("arbitrary",)))(dh1_b, w_o, mix)


FF_B = D_FF // NDEV


def _loss_rows(xv, gv, tv, row0):
    r_io = lax.broadcasted_iota(jnp.int32, (xv.shape[0], 1), 0) + row0
    valid = (r_io >= NM) & (r_io < L)
    r = lax.rsqrt(jnp.mean(xv * xv, axis=-1, keepdims=True) + EPS)
    xh = xv * r
    err = jnp.where(valid, xh * gv - tv, 0.0)
    lpart = 0.5 * jnp.sum(jnp.sum(err * err, axis=-1, keepdims=True) * (1.0 / D), axis=0, keepdims=True)
    dy = err * (1.0 / D)
    dxh = dy * gv
    dh = r * (dxh - xh * jnp.mean(dxh * xh, axis=-1, keepdims=True))
    return lpart, dh, jnp.sum(dy * xh, axis=0, keepdims=True)


def _mlp_fwd_loss(m, wup_g, wdown_g, h1, g_final, tgt):
    nsub = TM_MM // TM_E

    def body(m_ref, wu_ref, wd_ref, h1_ref, g_ref, t_ref, loss_ref, dh_ref, dhb_ref, dg_ref, h2):
        i, j = pl.program_id(0), pl.program_id(1)
        up = jnp.maximum(_dot(m_ref[...], wu_ref[0]), 0.0)
        part = _dot((up * up).astype(BF16), wd_ref[0])

        @pl.when(j == 0)
        def _():
            h2[...] = h1_ref[...] + part

        @pl.when(j > 0)
        def _():
            h2[...] += part

        @pl.when(j == NDEV - 1)
        def _():
            lsum = jnp.zeros((1, 1), F32)
            gsum = jnp.zeros((1, D), F32)
            for s in range(nsub):
                rows = slice(s * TM_E, (s + 1) * TM_E)
                lpart, dh, gpart = _loss_rows(h2[rows, :], g_ref[...], t_ref[rows, :], i * TM_MM + s * TM_E)
                dh_ref[rows, :] = dh
                dhb_ref[rows, :] = dh.astype(BF16)
                lsum = lsum + lpart
                gsum = gsum + gpart
            lsum = jnp.broadcast_to(lsum, (1, 128))

            @pl.when(i == 0)
            def _():
                loss_ref[...] = lsum
                dg_ref[...] = gsum

            @pl.when(i > 0)
            def _():
                loss_ref[...] += lsum
                dg_ref[...] += gsum

    blk = pl.BlockSpec((TM_MM, D), lambda i, j: (i, 0))
    vec = pl.BlockSpec((1, D), lambda i, j: (0, 0))
    return pl.pallas_call(
        body, name="mlp_fwd_loss", grid=(T // TM_MM, NDEV),
        in_specs=[blk, pl.BlockSpec((1, D, FF_B), lambda i, j: (j, 0, 0)), pl.BlockSpec((1, FF_B, D), lambda i, j: (j, 0, 0)),
                  blk, vec, blk],
        out_specs=(pl.BlockSpec((1, 128), lambda i, j: (0, 0)), blk, blk, vec),
        out_shape=(_sds((1, 128), F32), _sds((T, D), F32), _sds((T, D), BF16), _sds((1, D), F32)),
        scratch_shapes=[pltpu.VMEM((TM_MM, D), F32)],
        compiler_params=_cp(("arbitrary", "arbitrary"), 56))(m, wup_g, wdown_g, h1, g_final, tgt)


def _mlp_bwd(m, dh2_b, wup_g, wdown_g, h1, g_mlp, dh2):
    ni = T // TM_B
    nsub = TM_B // TM_E

    def body(m_ref, dh_ref, wu_ref, wd_ref, h1_ref, g_ref, dres_ref, dwu_ref, dwd_ref, dh1_ref, dh1b_ref, dg_ref,
             dm_ref, acc_u, acc_d):
        j, i = pl.program_id(0), pl.program_id(1)
        rows = pl.ds(pl.multiple_of(i * TM_B, TM_B), TM_B)
        mv, dh = m_ref[...], dh_ref[...]
        r = jnp.maximum(_dot(mv, wu_ref[0]), 0.0)
        act = (r * r).astype(BF16)
        dact = _dot(dh, wd_ref[0], NT)
        dup = (dact * (2.0 * r)).astype(BF16)
        pd = _dot(act, dh, TN)
        pu = _dot(mv, dup, TN)
        dmv = _dot(dup, wu_ref[0], NT)

        @pl.when(i == 0)
        def _():
            acc_u[...] = pu
            acc_d[...] = pd

        @pl.when(i > 0)
        def _():
            acc_u[...] += pu
            acc_d[...] += pd

        @pl.when(i == ni - 1)
        def _():
            dwu_ref[0] = acc_u[...].astype(BF16)
            dwd_ref[0] = acc_d[...].astype(BF16)

        @pl.when(j == 0)
        def _():
            dm_ref[rows, :] = dmv

        @pl.when(j > 0)
        def _():
            dm_ref[rows, :] += dmv

        @pl.when(j == NDEV - 1)
        def _():
            gsum = jnp.zeros((1, D), F32)
            for s in range(nsub):
                sub = slice(s * TM_E, (s + 1) * TM_E)
                dm_rows = dm_ref[pl.ds(pl.multiple_of(i * TM_B + s * TM_E, TM_E), TM_E), :]
                dx, gpart = _norm_bwd_rows(h1_ref[sub, :], g_ref[...], dm_rows, dres_ref[sub, :])
                dh1_ref[sub, :] = dx
                dh1b_ref[sub, :] = dx.astype(BF16)
                gsum = gsum + gpart

            @pl.when(i == 0)
            def _():
                dg_ref[...] = gsum

            @pl.when(i > 0)
            def _():
                dg_ref[...] += gsum

    blk = pl.BlockSpec((TM_B, D), lambda j, i: (i, 0))
    late = pl.BlockSpec((TM_B, D), lambda j, i: (jnp.where(j == NDEV - 1, i, 0), 0))
    vec = pl.BlockSpec((1, D), lambda j, i: (0, 0))
    wus = pl.BlockSpec((1, D, FF_B), lambda j, i: (j, 0, 0))
    wds = pl.BlockSpec((1, FF_B, D), lambda j, i: (j, 0, 0))
    return pl.pallas_call(
        body, name="mlp_bwd", grid=(NDEV, ni), in_specs=[blk, blk, wus, wds, late, vec, late],
        out_specs=(wus, wds, late, late, vec),
        out_shape=(_sds((NDEV, D, FF_B), BF16), _sds((NDEV, FF_B, D), BF16), _sds((T, D), F32), _sds((T, D), BF16),
                   _sds((1, D), F32)),
        scratch_shapes=[pltpu.VMEM((T, D), F32), pltpu.VMEM((D, FF_B), F32), pltpu.VMEM((FF_B, D), F32)],
        compiler_params=_cp(("arbitrary", "arbitrary"), 56))(m, dh2_b, wup_g, wdown_g, h1, g_mlp, dh2)


def _adamw(parts, w, m, v, name):
    rr, cc = w.shape
    tr = rr
    for cand in (256, 128, 64):
        if rr % cand == 0 and rr > cand:
            tr = cand
            break
    c1 = 1.0 - ADAM_B1 ** ADAM_STEP
    c2 = 1.0 - ADAM_B2 ** ADAM_STEP

    def body(p_ref, w_ref, m_ref, v_ref, g_ref, d_ref, nm_ref, nv_ref):
        g = p_ref[0].astype(F32)
        for s in range(1, NDEV):
            g = g + p_ref[s].astype(F32)
        mn = ADAM_B1 * m_ref[...] + (1.0 - ADAM_B1) * g
        vn = ADAM_B2 * v_ref[...] + (1.0 - ADAM_B2) * (g * g)
        g_ref[...] = g
        nm_ref[...] = mn
        nv_ref[...] = vn
        d_ref[...] = -ADAM_LR * ((mn / c1) / (jnp.sqrt(vn / c2) + ADAM_EPS) + ADAM_WD * w_ref[...])

    blk = pl.BlockSpec((tr, cc), lambda i: (i, 0))
    return pl.pallas_call(
        body, name=name, grid=(rr // tr,),
        in_specs=[pl.BlockSpec((NDEV, tr, cc), lambda i: (0, i, 0)), blk, blk, blk],
        out_specs=(blk,) * 4, out_shape=(_sds((rr, cc), F32),) * 4,
        compiler_params=_cp(("parallel",)))(parts, w, m, v)


RPB_N = NA_HEADS * 15 * 31
RPB_PAD = 4096
OWN_ROWS = NM + 8


def _pad_rows(a, rows):
    return jnp.pad(a, ((0, rows - a.shape[0]),) + ((0, 0),) * (a.ndim - 1))


def _pack_owned(meta_blk, lb_blk):
    return jnp.concatenate([meta_blk, _pad_rows(lb_blk.reshape(2, 128), 8)], axis=0)


def _pack_replicated(n_mix, n_mlp, n_final, hg_gain, rpb):
    flat = _pad_rows(rpb.reshape(RPB_N), RPB_PAD)
    return jnp.concatenate([n_mix.reshape(8, 128), n_mlp.reshape(8, 128), n_final.reshape(8, 128),
                            _pad_rows(hg_gain.reshape(4, 128), 8), flat.reshape(32, 128)], axis=0)


def _unpack_replicated(a):
    return (a[0:8].reshape(1, D), a[8:16].reshape(1, D), a[16:24].reshape(D), a[24:28].reshape(1, 512),
            a[32:64].reshape(RPB_PAD)[:RPB_N].reshape(1, NA_HEADS, 15, 31))


def kernel(x, meta_tokens, w_in, w_na_out, w_hg_out, w_o, w_up, w_down, norm_mix, norm_mlp, norm_final, hg_norm, na_rpb, hg_lb_logits, loss_target, m_meta_tokens, m_w_in, m_w_na_out, m_w_hg_out, m_w_o, m_w_up, m_w_down, m_norm_mix, m_norm_mlp, m_norm_final, m_hg_norm, m_na_rpb, m_hg_lb_logits, v_meta_tokens, v_w_in, v_w_na_out, v_w_hg_out, v_w_o, v_w_up, v_w_down, v_norm_mix, v_norm_mlp, v_norm_final, v_hg_norm, v_na_rpb, v_hg_lb_logits):
    owned = _pack_owned(meta_tokens, hg_lb_logits)
    first, tok = _exchange_start([w_in[0].astype(BF16), owned], [False] * 2, "gather_first_start", SAME_CORE_AND_SIBLING)
    bias_tab = _na_bias_table(_tie(jnp.pad(na_rpb[0], ((0, 0), (0, 0), (0, 128 - 31))), tok, "tie_bias_table"))
    first = _exchange_wait(first, [False] * 2, [bias_tab], "gather_first_wait", SAME_CORE_AND_SIBLING)
    win_g, owned_g = _forward_to_sibling(first, "gather_first_forward")
    later = [w[0].astype(BF16) for w in (w_na_out, w_hg_out, w_o, w_up, w_down)]
    later[0] = _tie(later[0], owned_g, "tie_gather_rest")
    gather_rest, tok = _exchange_start(later, [False] * 5, "gather_rest_start")
    win_g = _tie(win_g, tok, "tie_inproj")
    meta_full = jnp.transpose(owned_g[:, 0:NM, :], (1, 0, 2)).reshape(NM, D)
    logits = jnp.transpose(owned_g[:, NM:NM + 2, :].reshape(NDEV, 2, 2, 64), (1, 2, 0, 3)).reshape(2, 2, 512)

    h0 = jnp.concatenate([meta_full, x[0], jnp.zeros((T - L, D), F32)], axis=0)
    tgt = jnp.concatenate([jnp.zeros((NM, D), F32), loss_target[0], jnp.zeros((T - L, D), F32)], axis=0)

    a, a_t = _norm_fwd_t(h0, norm_mix, "norm_mix_fwd")
    p_act = _inproj_fwd(a, win_g)
    o_na, lse = _na_fwd(p_act, bias_tab)
    qh, k_f, b_f, k_b, b_b = _hg_pre(p_act, logits)
    o_f, st_f = _hg_scan_fwd(qh, k_f, b_f, p_act, False)
    o_b, st_b = _hg_scan_fwd(qh, k_b, b_b, p_act, True)
    wna_g, whg_g, wo_g, wup_g, wdown_g = _exchange_wait(gather_rest, [False] * 5, [o_f, o_b, o_na], "gather_rest_wait")
    w_o_full = wo_g.reshape(D, D)
    w_na_full = jnp.transpose(wna_g, (1, 0, 2)).reshape(512, D)
    w_hg_full = jnp.transpose(whg_g, (1, 0, 2)).reshape(512, D)
    mix, u_hg = _mix_fwd(o_na, o_f, o_b, hg_norm, w_na_full, w_hg_full, p_act)
    h1, m_act = _wo_fwd(mix, w_o_full, h0, norm_mlp)
    loss_part, dh2, dh2_b, d_nfinal = _mlp_fwd_loss(m_act, wup_g, wdown_g, h1, norm_final.reshape(1, D), tgt)

    dwup_p, dwdown_p, dh1, dh1_b, d_nmlp = _mlp_bwd(m_act, dh2_b, wup_g, wdown_g, h1, norm_mlp, dh2)
    sc_mlp, tok = _exchange_start([dwup_p, dwdown_p], [True] * 2, "scatter_mlp_start")
    dmix, dwo = _wo_bwd(_tie(dh1_b, tok, "tie_wo_bwd"), w_o_full, mix)
    sc_wo, tok = _exchange_start([dwo.reshape(NDEV, D // NDEV, D)], [True], "scatter_wo_start")
    dgna, dghg, dwna, dwhg, do_na, do_hg, dg_hg, d_gain = _mix_bwd(
        o_na, u_hg, o_f, o_b, hg_norm, w_na_full, w_hg_full, p_act, _tie(dmix, tok, "tie_mix_bwd"))
    owner_cols = lambda w: jnp.transpose(w.reshape(512, NDEV, D // NDEV), (1, 0, 2))
    sc_br, tok = _exchange_start([owner_cols(dwna), owner_cols(dwhg)], [True] * 2, "scatter_branch_start")
    do_hg = _tie(do_hg, tok, "tie_hg_scan_bwd")
    dq_f, dk_f, db_f, dv_f = _hg_scan_bwd(qh, k_f, b_f, p_act, st_f, do_hg, False)
    dq_b, dk_b, db_b, dv_b = _hg_scan_bwd(qh, k_b, b_b, p_act, st_b, do_hg, True)
    dq_hg, dz_f, dz_b, di_hg, d_logits = _hg_pre_bwd(p_act, logits, dq_f, dq_b, dk_f, dk_b, db_f, db_b, dv_f, dv_b)
    dq_na, dk_na, dv_na, dbias = _na_bwd(p_act, do_na, lse, bias_tab)
    dp = jnp.concatenate([dq_na.astype(BF16), dk_na.astype(BF16), dv_na.astype(BF16), dq_hg, dz_f, dz_b, di_hg, dg_hg,
                          dgna, dghg], axis=1)
    dwin_p = _inproj_bwd_dw(a_t, dp, "inproj_bwd_dw")
    sc_in, tok = _exchange_start([dwin_p], [True], "scatter_in_start")
    dh0, d_nmix = _inproj_bwd_da(_tie(dp, tok, "tie_inproj_bwd_da"), win_g, h0, norm_mix, dh1)
    d_rpb = _na_rpb_reduce(_tie(dbias, tok, "tie_rpb_reduce"))[:, :, :31]

    res = {}

    def update(nm, parts, w, mm, vv):
        res[nm] = [r[None] for r in _adamw(parts, w[0], mm[0], vv[0], "adamw_" + nm)]
        return res[nm][1]

    wup_r, wdown_r = _exchange_wait(sc_mlp, [True] * 2, [dh0, d_rpb], "scatter_mlp_wait")
    update("w_up", wup_r, w_up, m_w_up, v_w_up)
    last = update("w_down", wdown_r, w_down, m_w_down, v_w_down)
    (wo_r,) = _exchange_wait(sc_wo, [True], [last], "scatter_wo_wait")
    last = update("w_o", wo_r, w_o, m_w_o, v_w_o)
    wna_r, whg_r = _exchange_wait(sc_br, [True] * 2, [last], "scatter_branch_wait")
    update("w_na_out", wna_r, w_na_out, m_w_na_out, v_w_na_out)
    last = update("w_hg_out", whg_r, w_hg_out, m_w_hg_out, v_w_hg_out)

    d_meta = jnp.transpose(dh0[0:NM].reshape(NM, NDEV, 128), (1, 0, 2))
    d_lg = jnp.transpose(d_logits.reshape(2, 2, NDEV, 64), (2, 0, 1, 3)).reshape(NDEV, 2, 128)
    owned_p = jnp.concatenate([d_meta, jnp.pad(d_lg, ((0, 0), (0, OWN_ROWS - NM - 2), (0, 0)))], axis=1)
    repl_p = _pack_replicated(d_nmix, d_nmlp, d_nfinal, d_gain, d_rpb)
    owned_r, repl_r = _exchange([_tie(owned_p, last, "tie_scatter_small"), repl_p], [True, False], "scatter_small")
    own = _adamw(owned_r, owned, _pack_owned(m_meta_tokens, m_hg_lb_logits), _pack_owned(v_meta_tokens, v_hg_lb_logits),
                 "adamw_owned_small")
    res["meta_tokens"] = [r[0:NM] for r in own]
    res["hg_lb_logits"] = [r[NM:NM + 2].reshape(2, 2, 64) for r in own]
    rep = _adamw(repl_r, _pack_replicated(norm_mix, norm_mlp, norm_final, hg_norm, na_rpb),
                 _pack_replicated(m_norm_mix, m_norm_mlp, m_norm_final, m_hg_norm, m_na_rpb),
                 _pack_replicated(v_norm_mix, v_norm_mlp, v_norm_final, v_hg_norm, v_na_rpb), "adamw_replicated")
    for q in range(4):
        um = _unpack_replicated(rep[q])
        for nm, val in zip(("norm_mix", "norm_mlp", "norm_final", "hg_norm", "na_rpb"), um):
            res.setdefault(nm, [None] * 4)[q] = val
    (win_r,) = _exchange_wait(sc_in, [True], [rep[1], own[1]], "scatter_in_wait")
    update("w_in", win_r, w_in, m_w_in, v_w_in)

    loss = lax.psum(loss_part[0, 0], ("x", "y", "c"))
    grad_x = dh0[NM:L][None]
    order = ("meta_tokens", "w_in", "w_na_out", "w_hg_out", "w_o", "w_up", "w_down", "norm_mix", "norm_mlp", "norm_final",
             "hg_norm", "na_rpb", "hg_lb_logits")
    outs = [loss, grad_x]
    for q in range(4):
        outs += [res[nm][q] for nm in order]
    return tuple(outs)
```

```python
import functools

import numpy as np
import jax
import jax.numpy as jnp
from jax import lax
from jax.experimental import pallas as pl
from jax.experimental.pallas import tpu as pltpu

F32 = jnp.float32
BF16 = jnp.bfloat16

D = 1024
SEQ = 2048
NM = 16
L = SEQ + NM
T = 2176
NDEV = 8
EPS = 1e-6
GRID_W = 64
ROWS = SEQ // GRID_W
NA_HEADS = 8
NA_DH = 64
NA_SCALE = NA_DH ** -0.5
HG_HEADS = 4
HG_C = 16
NCHUNK = L // HG_C
D_FF = 4096
IN_COLS = 6144
NEG = -1e30

ADAM_LR = 0.001
ADAM_B1 = 0.9
ADAM_B2 = 0.999
ADAM_EPS = 1e-08
ADAM_WD = 0.01
ADAM_STEP = 10

MESH_ID = pl.DeviceIdType.MESH
ANY = pl.BlockSpec(memory_space=pl.ANY)

NN = (((1,), (0,)), ((), ()))
NT = (((1,), (1,)), ((), ()))
TN = (((0,), (0,)), ((), ()))


def _cp(sem=None, vmem_mb=48):
    return pltpu.CompilerParams(dimension_semantics=sem, vmem_limit_bytes=vmem_mb * 1024 * 1024)


def _dot(a, b, dims=NN):
    return lax.dot_general(a, b, dims, preferred_element_type=F32)


def _sds(shape, dtype):
    return jax.ShapeDtypeStruct(shape, dtype)


HBM = pl.BlockSpec(memory_space=pltpu.HBM)
SEM = pl.BlockSpec(memory_space=pltpu.SEMAPHORE)
EFFECT = pltpu.SideEffectType.DATAFLOW_SIDE_EFFECTING


def _exchange(arrs, scatter, name):
    n = len(arrs)
    out_shapes = []
    for a, sc in zip(arrs, scatter):
        out_shapes.append(_sds(a.shape if sc else (NDEV,) + a.shape, a.dtype))

    def body(*refs):
        ins, outs = refs[:n], refs[n:2 * n]
        send_sems, recv_sems, loc_sems = refs[2 * n:]
        me = 4 * lax.axis_index("x") + 2 * lax.axis_index("y") + lax.axis_index("c")
        copies = []
        for k in range(n):
            src_me = ins[k].at[me] if scatter[k] else ins[k]
            loc = pltpu.make_async_copy(src_me, outs[k].at[me], loc_sems.at[k])
            loc.start()
            copies.append(loc)
        remote = _peer_copies(ins, outs, scatter, send_sems, recv_sems)
        for cp in remote:
            cp.start()
        for cp in remote:
            cp.wait_recv()
        for cp in remote:
            cp.wait_send()
        for cp in copies:
            cp.wait()

    return pl.pallas_call(
        body, name=name, out_shape=tuple(out_shapes), in_specs=[ANY] * n, out_specs=tuple([ANY] * n),
        scratch_shapes=[pltpu.SemaphoreType.DMA((n * (NDEV - 1),)), pltpu.SemaphoreType.DMA((n * (NDEV - 1),)),
                        pltpu.SemaphoreType.DMA((n,))],
    )(*arrs)


def _forward_to_sibling(bufs, name):
    n = len(bufs)

    def body(*refs):
        ins, outs = refs[:n], refs[n:2 * n]
        send_sems, recv_sems = refs[2 * n:]
        x, y, c = lax.axis_index("x"), lax.axis_index("y"), lax.axis_index("c")
        copies = []
        for k in range(n):
            for j, (cx, cy) in enumerate(((1 - x, y), (x, 1 - y), (1 - x, 1 - y))):
                slot = 4 * cx + 2 * cy + c
                copies.append(pltpu.make_async_remote_copy(
                    src_ref=ins[k].at[slot], dst_ref=outs[k].at[slot], send_sem=send_sems.at[3 * k + j],
                    recv_sem=recv_sems.at[3 * k + j], device_id=(x, y, 1 - c), device_id_type=MESH_ID))
        for cp in copies:
            cp.start()
        for cp in copies:
            cp.wait_recv()
        for cp in copies:
            cp.wait_send()

    return pl.pallas_call(
        body, name=name, out_shape=tuple(_sds(b.shape, b.dtype) for b in bufs), in_specs=[ANY] * n,
        out_specs=tuple([ANY] * n), input_output_aliases={k: k for k in range(n)},
        scratch_shapes=[pltpu.SemaphoreType.DMA((3 * n,)), pltpu.SemaphoreType.DMA((3 * n,))],
    )(*bufs)


ALL_PEERS = tuple(range(1, NDEV))
SAME_CORE_AND_SIBLING = (1, 2, 4, 6)


def _peer_copies(srcs, lands, scatter, send_sems, recv_sems, masks=ALL_PEERS):
    x, y, c = lax.axis_index("x"), lax.axis_index("y"), lax.axis_index("c")
    me = 4 * x + 2 * y + c
    out = []
    for k in range(len(srcs)):
        for m in masks:
            px, py, pc = x ^ (m >> 2), y ^ ((m >> 1) & 1), c ^ (m & 1)
            src = srcs[k].at[4 * px + 2 * py + pc] if scatter[k] else srcs[k]
            out.append(pltpu.make_async_remote_copy(
                src_ref=src, dst_ref=lands[k].at[me], send_sem=send_sems.at[k * (NDEV - 1) + m - 1],
                recv_sem=recv_sems.at[k * (NDEV - 1) + m - 1],
                device_id=(px, py, pc), device_id_type=MESH_ID))
    return out


def _exchange_start(arrs, scatter, name, masks=ALL_PEERS):
    n = len(arrs)
    me = 4 * lax.axis_index("x") + 2 * lax.axis_index("y") + lax.axis_index("c")
    lands = []
    for a, sc in zip(arrs, scatter):
        own = lax.dynamic_index_in_dim(a, me, 0, keepdims=True) if sc else a[None]
        shape = a.shape if sc else (NDEV,) + a.shape
        lands.append(lax.dynamic_update_index_in_dim(lax.empty(shape, a.dtype), own, me, 0))

    def body(*refs):
        srcs, lnds = refs[:n], refs[n:2 * n]
        send_sems, recv_sems = refs[2 * n], refs[2 * n + 1]
        token = refs[-1]
        for cp in _peer_copies(srcs, lnds, scatter, send_sems, recv_sems, masks):
            cp.start()
        token[...] = jnp.zeros_like(token)

    ops = [pltpu.with_memory_space_constraint(a, pltpu.HBM) for a in list(arrs) + lands]
    res = pl.pallas_call(
        body, name=name,
        out_shape=(pltpu.SemaphoreType.DMA((n * (NDEV - 1),)), pltpu.SemaphoreType.DMA((n * (NDEV - 1),)))
        + tuple(pltpu.HBM(o.shape, o.dtype) for o in ops) + (_sds((8, 128), F32),),
        in_specs=[HBM] * (2 * n), out_specs=(SEM, SEM) + (HBM,) * (2 * n) + (pl.BlockSpec(memory_space=pltpu.VMEM),),
        input_output_aliases={k: 2 + k for k in range(2 * n)},
        compiler_params=pltpu.CompilerParams(has_side_effects=EFFECT),
    )(*ops)
    return res[:-1], res[-1]


def _exchange_wait(handle, scatter, after, name, masks=ALL_PEERS):
    send_sems, recv_sems = handle[0], handle[1]
    bufs = handle[2:]
    n = len(bufs) // 2
    after = list(after)

    def body(*refs):
        srcs, lnds = refs[:n], refs[n:2 * n]
        for cp in _peer_copies(srcs, lnds, scatter, refs[2 * n], refs[2 * n + 1], masks):
            cp.wait_send()
            cp.wait_recv()

    res = pl.pallas_call(
        body, name=name, out_shape=tuple(pltpu.HBM(b.shape, b.dtype) for b in bufs),
        in_specs=[HBM] * (2 * n) + [SEM, SEM] + [ANY] * len(after), out_specs=(HBM,) * (2 * n),
        input_output_aliases={k: k for k in range(2 * n)},
        compiler_params=pltpu.CompilerParams(has_side_effects=EFFECT),
    )(*bufs, send_sems, recv_sems, *after)
    return res[n:]


def _tie(x, token, name):
    def body(x_ref, t_ref, o_ref):
        del x_ref, t_ref, o_ref

    return pl.pallas_call(body, name=name, out_shape=_sds(x.shape, x.dtype), in_specs=[ANY, ANY], out_specs=ANY,
                          input_output_aliases={0: 0})(x, token)


TM_E = 272


def _norm_fwd_t(h, g, name):
    def body(h_ref, g_ref, o_ref, ot_ref):
        xv = h_ref[...]
        r = lax.rsqrt(jnp.mean(xv * xv, axis=-1, keepdims=True) + EPS)
        y = xv * r * g_ref[...]
        o_ref[...] = y.astype(BF16)
        ot_ref[...] = y.T.astype(BF16)

    return pl.pallas_call(
        body, name=name, grid=(T // 128,),
        in_specs=[pl.BlockSpec((128, D), lambda i: (i, 0)), pl.BlockSpec((1, D), lambda i: (0, 0))],
        out_specs=(pl.BlockSpec((128, D), lambda i: (i, 0)), pl.BlockSpec((D, 128), lambda i: (0, i))),
        out_shape=(_sds((T, D), BF16), _sds((D, T), BF16)), compiler_params=_cp(("parallel",)))(h, g)


def _norm_bwd_rows(xv, gv, dnv, dres):
    r = lax.rsqrt(jnp.mean(xv * xv, axis=-1, keepdims=True) + EPS)
    xh = xv * r
    dxh = dnv * gv
    dx = dres + r * (dxh - xh * jnp.mean(dxh * xh, axis=-1, keepdims=True))
    return dx, jnp.sum(dnv * xh, axis=0, keepdims=True)


TM_MM = 1088


def _inproj_fwd(a, w_g):
    nb = w_g.shape[2]

    def body(a_ref, w_ref, o_ref):
        o_ref[...] = _dot(a_ref[...], w_ref[0])

    return pl.pallas_call(
        body, name="inproj_fwd", grid=(T // TM_MM, NDEV),
        in_specs=[pl.BlockSpec((TM_MM, D), lambda i, j: (i, 0)), pl.BlockSpec((1, D, nb), lambda i, j: (j, 0, 0))],
        out_specs=pl.BlockSpec((TM_MM, nb), lambda i, j: (i, j)), out_shape=_sds((T, NDEV * nb), F32),
        compiler_params=_cp(("parallel", "parallel")))(a, w_g)


TM_B = 544


W_IN_B = IN_COLS // NDEV


DP_SUB = 256
DP_NSUB = IN_COLS // DP_SUB


def _piece_ranges(pieces):
    out, s0 = [], 0
    for p in pieces:
        out.append((s0, p.shape[1] // DP_SUB))
        s0 += p.shape[1] // DP_SUB
    assert s0 == DP_NSUB
    return out


def _piece_spec(rows, rng, row_index):
    s0, cnt = rng
    return pl.BlockSpec((rows, DP_SUB), lambda *g: (row_index(*g), jnp.clip(g[-1] - s0, 0, cnt - 1)))


def _inproj_bwd_dw(a_t, pieces):
    ranges = _piece_ranges(pieces)

    def body(at_ref, *refs):
        dw_ref = refs[-1]
        s = pl.program_id(0)
        for p_ref, (s0, cnt) in zip(refs[:-1], ranges):
            @pl.when((s >= s0) & (s < s0 + cnt))
            def _():
                dw_ref[0] = _dot(at_ref[...], p_ref[...].astype(BF16)).astype(BF16)

    return pl.pallas_call(
        body, name="inproj_bwd_dw", grid=(DP_NSUB,),
        in_specs=[pl.BlockSpec((D, T), lambda s: (0, 0))] + [_piece_spec(T, r, lambda s: 0) for r in ranges],
        out_specs=pl.BlockSpec((1, D, DP_SUB), lambda s: (s // 3, 0, s % 3)), out_shape=_sds((NDEV, D, W_IN_B), BF16),
        compiler_params=_cp(("parallel",), 56))(a_t, *pieces)


def _inproj_bwd_da(pieces, w_g, h0, g_mix, dh1):
    ranges = _piece_ranges(pieces)
    nsub = TM_B // TM_E

    def body(*refs):
        np_ = len(pieces)
        w_ref, h0_ref, g_ref, dres_ref, dh0_ref, dg_ref, da = refs[np_:]
        i, s = pl.program_id(0), pl.program_id(1)
        for p_ref, (s0, cnt) in zip(refs[:np_], ranges):
            @pl.when((s >= s0) & (s < s0 + cnt))
            def _():
                dav = _dot(p_ref[...].astype(BF16), w_ref[0], NT)

                @pl.when(s == 0)
                def _():
                    da[...] = dav

                @pl.when(s > 0)
                def _():
                    da[...] += dav

        @pl.when(s == DP_NSUB - 1)
        def _():
            gsum = jnp.zeros((1, D), F32)
            for q in range(nsub):
                sub = slice(q * TM_E, (q + 1) * TM_E)
                dx, gpart = _norm_bwd_rows(h0_ref[sub, :], g_ref[...], da[sub, :], dres_ref[sub, :])
                dh0_ref[sub, :] = dx
                gsum = gsum + gpart

            @pl.when(i == 0)
            def _():
                dg_ref[...] = gsum

            @pl.when(i > 0)
            def _():
                dg_ref[...] += gsum

    rblk = pl.BlockSpec((TM_B, D), lambda i, s: (i, 0))
    vec = pl.BlockSpec((1, D), lambda i, s: (0, 0))
    return pl.pallas_call(
        body, name="inproj_bwd_da", grid=(T // TM_B, DP_NSUB),
        in_specs=[_piece_spec(TM_B, r, lambda i, s: i) for r in ranges]
        + [pl.BlockSpec((1, D, DP_SUB), lambda i, s: (s // 3, 0, s % 3)), rblk, vec, rblk],
        out_specs=(rblk, vec), out_shape=(_sds((T, D), F32), _sds((1, D), F32)),
        scratch_shapes=[pltpu.VMEM((TM_B, D), F32)],
        compiler_params=_cp(("arbitrary", "arbitrary"), 56))(*pieces, w_g, h0, g_mix, dh1)


NA_QB = 256
NA_GROUPS = ROWS // 4
NA_UROWS = 11
NA_KW = NA_UROWS * GRID_W
NA_KU = 768


def _na_row_offset(var, i, j):
    valid = (j < 8, i <= j < i + 8, 3 <= j < NA_UROWS)[var]
    return (j - i + (7, 3, 0)[var]) if valid else None


def _na_bias_table(rp):
    def body(r_ref, o_ref):
        row3 = lax.broadcasted_iota(jnp.int32, (15, GRID_W, 128), 1)
        lane3 = lax.broadcasted_iota(jnp.int32, (15, GRID_W, 128), 2)
        w3 = lane3 & (GRID_W - 1)
        cs3 = jnp.clip(row3 - 8, 0, GRID_W - 16)
        lane = lax.broadcasted_iota(jnp.int32, (GRID_W, 128), 1)
        neg = jnp.full((GRID_W, 128), NEG, F32)
        z = jnp.stack([jnp.broadcast_to(r_ref[0, a:a + 1, :], (GRID_W, 128)) for a in range(15)])
        for bit in range(6):
            sh = 1 << bit
            z = jnp.where((row3 & sh) != 0, jnp.roll(z, sh, axis=2), z)
        z = jnp.roll(z, 128 - 15, axis=2)
        z = jnp.where(lane3 < GRID_W, z, 0.0)
        z = z + jnp.roll(z, GRID_W, axis=2)
        tabs = jnp.where((w3 >= cs3) & (w3 < cs3 + 16), z, NEG)
        tail = jnp.where(lane < GRID_W + NM, 0.0, NEG)
        for var in range(3):
            for i in range(4):
                for jp in range(NA_KU // 128):
                    halves = []
                    for j in (2 * jp, 2 * jp + 1):
                        a = _na_row_offset(var, i, j) if j < NA_UROWS else None
                        halves.append(tail if j >= NA_UROWS else (neg if a is None else tabs[a]))
                    o_ref[var, 0, i * 64:(i + 1) * 64, jp * 128:(jp + 1) * 128] = jnp.where(lane < GRID_W, halves[0], halves[1])

    return pl.pallas_call(
        body, name="na_bias_table", grid=(NA_HEADS,),
        in_specs=[pl.BlockSpec((1, 15, 128), lambda h: (h, 0, 0))],
        out_specs=pl.BlockSpec((3, 1, NA_QB, NA_KU), lambda h: (0, h, 0, 0)),
        out_shape=_sds((3, NA_HEADS, NA_QB, NA_KU), F32), compiler_params=_cp(("parallel",)))(rp)


def _na_var(g):
    return jnp.where(g == 0, 0, jnp.where(g == NA_GROUPS - 1, 2, 1))


def _na_load_window(src_ref, dst, g):
    us = jnp.clip(4 * g - 4, 0, ROWS - NA_UROWS)
    kstart = pl.multiple_of(NM + GRID_W * us, 16)
    dst[0:NA_KW, :] = src_ref[pl.ds(kstart, NA_KW), :].astype(BF16)
    dst[NA_KW:NA_KW + NM, :] = src_ref[0:NM, :].astype(BF16)
    dst[NA_KW + NM:, :] = jnp.zeros((NA_KU - NA_KW - NM, 128), BF16)
    return kstart


def _na_fwd(p_act, bias_tab):
    def body(q_ref, k_ref, v_ref, b_ref, o_ref, lse_ref, ku, vu):
        g = pl.program_id(1)
        _na_load_window(k_ref, ku, g)
        _na_load_window(v_ref, vu, g)
        qstart = pl.multiple_of(NM + NA_QB * g, 16)
        q = q_ref[pl.ds(qstart, NA_QB), :]
        lane = lax.broadcasted_iota(jnp.int32, (NA_QB, 128), 1)
        o_h, lse_h = [], []
        for h in range(2):
            hm = (lane < 64) if h == 0 else (lane >= 64)
            qm = jnp.where(hm, q, 0.0).astype(BF16)
            s = _dot(qm, ku[...], NT) * NA_SCALE + b_ref[0, h]
            m = jnp.max(s, axis=-1, keepdims=True)
            p = jnp.exp(s - m)
            l = jnp.sum(p, axis=-1, keepdims=True)
            o_h.append(_dot(p.astype(BF16), vu[...]) / l)
            lse_h.append(jnp.broadcast_to(m + jnp.log(l), (NA_QB, 128)))
        o_ref[pl.ds(qstart, NA_QB), :] = jnp.where(lane < 64, o_h[0], o_h[1]).astype(BF16)
        lse_ref[0, pl.ds(qstart, NA_QB), :] = jnp.where(lane < 64, lse_h[0], lse_h[1])

        @pl.when(g == 0)
        def _():
            qm_ = q_ref[0:NM, :]
            lane_m = lax.broadcasted_iota(jnp.int32, (NM, 128), 1)
            km, vm = ku[NA_KW:NA_KW + NM, :], vu[NA_KW:NA_KW + NM, :]
            om = []
            for h in range(2):
                hm = (lane_m < 64) if h == 0 else (lane_m >= 64)
                s = _dot(jnp.where(hm, qm_, 0.0).astype(BF16), km, NT) * NA_SCALE
                p = jnp.exp(s - jnp.max(s, axis=-1, keepdims=True))
                l = jnp.sum(p, axis=-1, keepdims=True)
                om.append(_dot(p.astype(BF16), vm) / l)
            o_ref[0:NM, :] = jnp.where(lane_m < 64, om[0], om[1]).astype(BF16)
            o_ref[L:T, :] = jnp.zeros((T - L, 128), BF16)
            lse_ref[0, 0:NM, :] = jnp.zeros((NM, 128), F32)
            lse_ref[0, L:T, :] = jnp.zeros((T - L, 128), F32)

    col = lambda off: pl.BlockSpec((T, 128), lambda hp, g: (0, off + hp))
    return pl.pallas_call(
        body, name="na_fwd", grid=(4, NA_GROUPS),
        in_specs=[col(0), col(4), col(8),
                  pl.BlockSpec((1, 2, NA_QB, NA_KU), lambda hp, g: (_na_var(g), hp, 0, 0))],
        out_specs=(pl.BlockSpec((T, 128), lambda hp, g: (0, hp)), pl.BlockSpec((1, T, 128), lambda hp, g: (hp, 0, 0))),
        out_shape=(_sds((T, 512), BF16), _sds((4, T, 128), F32)),
        scratch_shapes=[pltpu.VMEM((NA_KU, 128), BF16), pltpu.VMEM((NA_KU, 128), BF16)],
        compiler_params=_cp(("parallel", "arbitrary")))(p_act, p_act, p_act, bias_tab)


def _na_bwd(p_act, do, lse, bias_tab):
    def body(q_ref, k_ref, v_ref, do_ref, lse_ref, b_ref, dq_ref, dk_ref, dv_ref, db_ref, ku, vu):
        g = pl.program_id(1)

        @pl.when(g == 0)
        def _():
            dq_ref[...] = jnp.zeros((T, 128), F32)
            dk_ref[...] = jnp.zeros((T, 128), F32)
            dv_ref[...] = jnp.zeros((T, 128), F32)

        kstart = _na_load_window(k_ref, ku, g)
        _na_load_window(v_ref, vu, g)
        qstart = pl.multiple_of(NM + NA_QB * g, 16)
        q = q_ref[pl.ds(qstart, NA_QB), :]
        dov = do_ref[pl.ds(qstart, NA_QB), :]
        lsev = lse_ref[0, pl.ds(qstart, NA_QB), :]
        lane = lax.broadcasted_iota(jnp.int32, (NA_QB, 128), 1)
        first = (g == 0) | (g == 1) | (g == NA_GROUPS - 1)
        dq_h = []
        dku = jnp.zeros((NA_KU, 128), F32)
        dvu = jnp.zeros((NA_KU, 128), F32)
        for h in range(2):
            hm = (lane < 64) if h == 0 else (lane >= 64)
            qm = jnp.where(hm, q, 0.0).astype(BF16)
            dom = jnp.where(hm, dov, 0.0).astype(BF16)
            s = _dot(qm, ku[...], NT) * NA_SCALE + b_ref[0, h]
            p = jnp.exp(s - lsev[:, 64 * h:64 * h + 1])
            dp = _dot(dom, vu[...], NT)
            delta = jnp.sum(p * dp, axis=-1, keepdims=True)
            ds = p * (dp - delta)

            @pl.when(first)
            def _():
                db_ref[0, h] = ds

            @pl.when(jnp.logical_not(first))
            def _():
                db_ref[0, h] += ds

            dsb = (ds * NA_SCALE).astype(BF16)
            dq_h.append(_dot(dsb, ku[...]))
            dku = dku + _dot(dsb, qm, TN)
            dvu = dvu + _dot(p.astype(BF16), dom, TN)
        dq_ref[pl.ds(qstart, NA_QB), :] = jnp.where(lane < 64, dq_h[0], dq_h[1])
        dk_ref[pl.ds(kstart, NA_KW), :] += dku[0:NA_KW]
        dv_ref[pl.ds(kstart, NA_KW), :] += dvu[0:NA_KW]
        dk_ref[0:NM, :] += dku[NA_KW:NA_KW + NM]
        dv_ref[0:NM, :] += dvu[NA_KW:NA_KW + NM]

        @pl.when(g == 0)
        def _():
            qm_ = q_ref[0:NM, :]
            dom_ = do_ref[0:NM, :]
            lane_m = lax.broadcasted_iota(jnp.int32, (NM, 128), 1)
            km, vm = ku[NA_KW:NA_KW + NM, :], vu[NA_KW:NA_KW + NM, :]
            dqs = []
            dkm = jnp.zeros((NM, 128), F32)
            dvm = jnp.zeros((NM, 128), F32)
            for h in range(2):
                hm = (lane_m < 64) if h == 0 else (lane_m >= 64)
                qh = jnp.where(hm, qm_, 0.0).astype(BF16)
                doh = jnp.where(hm, dom_, 0.0).astype(BF16)
                s = _dot(qh, km, NT) * NA_SCALE
                e = jnp.exp(s - jnp.max(s, axis=-1, keepdims=True))
                p = e / jnp.sum(e, axis=-1, keepdims=True)
                dp = _dot(doh, vm, NT)
                ds = p * (dp - jnp.sum(p * dp, axis=-1, keepdims=True))
                dsb = (ds * NA_SCALE).astype(BF16)
                dqs.append(_dot(dsb, km))
                dkm = dkm + _dot(dsb, qh, TN)
                dvm = dvm + _dot(p.astype(BF16), doh, TN)
            dq_ref[0:NM, :] = jnp.where(lane_m < 64, dqs[0], dqs[1])
            dk_ref[0:NM, :] += dkm
            dv_ref[0:NM, :] += dvm

    col = lambda off: pl.BlockSpec((T, 128), lambda hp, g: (0, off + hp))
    ocol = pl.BlockSpec((T, 128), lambda hp, g: (0, hp))
    bspec = pl.BlockSpec((1, 2, NA_QB, NA_KU), lambda hp, g: (_na_var(g), hp, 0, 0))
    return pl.pallas_call(
        body, name="na_bwd", grid=(4, NA_GROUPS),
        in_specs=[col(0), col(4), col(8), ocol, pl.BlockSpec((1, T, 128), lambda hp, g: (hp, 0, 0)), bspec],
        out_specs=(ocol, ocol, ocol, bspec),
        out_shape=(_sds((T, 512), F32), _sds((T, 512), F32), _sds((T, 512), F32), _sds((3, NA_HEADS, NA_QB, NA_KU), F32)),
        scratch_shapes=[pltpu.VMEM((NA_KU, 128), BF16), pltpu.VMEM((NA_KU, 128), BF16)],
        compiler_params=_cp(("parallel", "arbitrary")))(p_act, p_act, p_act, do, lse, bias_tab)


def _na_rpb_reduce(dbias):
    def body(db_ref, o_ref):
        lane = lax.broadcasted_iota(jnp.int32, (GRID_W, 128), 1)
        row3 = lax.broadcasted_iota(jnp.int32, (15, GRID_W, 128), 1)
        lane3 = lax.broadcasted_iota(jnp.int32, (15, GRID_W, 128), 2)
        accs = []
        for a in range(15):
            acc = jnp.zeros((GRID_W, 128), F32)
            for var in range(3):
                for i in range(4):
                    for j in range(NA_UROWS):
                        if _na_row_offset(var, i, j) == a:
                            pair = db_ref[var, 0, i * 64:(i + 1) * 64, (j // 2) * 128:(j // 2 + 1) * 128]
                            acc = acc + jnp.where((lane < GRID_W) if j % 2 == 0 else (lane >= GRID_W), pair, 0.0)
            accs.append(acc)
        z = jnp.stack(accs)
        z = jnp.where(lane3 < GRID_W, z + jnp.roll(z, GRID_W, axis=2), 0.0)
        for bit in range(6):
            sh = 1 << bit
            z = jnp.where((row3 & sh) != 0, jnp.roll(z, 128 - sh, axis=2), z)
        z = jnp.roll(z, 15, axis=2)
        o_ref[0] = jnp.sum(z, axis=1)

    return pl.pallas_call(
        body, name="na_rpb_reduce", grid=(NA_HEADS,),
        in_specs=[pl.BlockSpec((3, 1, NA_QB, NA_KU), lambda h: (0, h, 0, 0))],
        out_specs=pl.BlockSpec((1, 15, 128), lambda h: (h, 0, 0)), out_shape=_sds((NA_HEADS, 15, 128), F32),
        compiler_params=_cp(("parallel",)))(dbias)


HG_RB = 128
HG_NB = T // HG_RB
HG_SLOTS = HG_NB * 8
HI = lax.Precision.HIGHEST
HG_UNROLL = 4


def _chunk_tri(lower):
    r = lax.broadcasted_iota(jnp.int32, (HG_RB, HG_RB), 0)
    c = lax.broadcasted_iota(jnp.int32, (HG_RB, HG_RB), 1)
    same = (r // HG_C) == (c // HG_C)
    keep = (c <= r) if lower else (c >= r)
    return jnp.where(same & keep, 1.0, 0.0).astype(F32)


def _hg_gate_terms(z, lg):
    dl = lg[0:1, :] - lg[1:2, :]
    log_lb = jax.nn.log_sigmoid(dl)
    log_1mlb = jax.nn.log_sigmoid(-dl)
    yz = log_1mlb + jax.nn.log_sigmoid(z)
    log_f = jnp.logaddexp(log_lb, yz)
    snz = jax.nn.sigmoid(-z)
    k = jnp.exp(log_1mlb) * snz
    w2 = jnp.exp(yz - log_f)
    return log_f, k, snz, w2


def _hg_pre(p_act, logits):
    def body(q_ref, zf_ref, zb_ref, lg_ref, qh_ref, kf_ref, bf_ref, kb_ref, bb_ref):
        qh_ref[...] = jax.nn.silu(q_ref[...])
        lf, kf, _, _ = _hg_gate_terms(zf_ref[...], lg_ref[0])
        kf_ref[...] = kf
        bf_ref[...] = jnp.dot(_chunk_tri(True), lf, precision=HI, preferred_element_type=F32)
        lb_, kb, _, _ = _hg_gate_terms(zb_ref[...], lg_ref[1])
        kb_ref[...] = kb
        bb_ref[...] = jnp.dot(_chunk_tri(False), lb_, precision=HI, preferred_element_type=F32)

    blk = lambda c: pl.BlockSpec((HG_RB, 512), lambda i: (i, c))
    ob = pl.BlockSpec((HG_RB, 512), lambda i: (i, 0))
    return pl.pallas_call(
        body, name="hg_pre", grid=(HG_NB,),
        in_specs=[blk(3), blk(4), blk(5), pl.BlockSpec((2, 2, 512), lambda i: (0, 0, 0))],
        out_specs=(ob,) * 5, out_shape=(_sds((T, 512), F32),) * 5,
        compiler_params=_cp(("parallel",)))(p_act, p_act, p_act, logits)


def _bdot(a, b, ca, cb):
    return lax.dot_general(a.astype(BF16), b.astype(BF16), (((ca,), (cb,)), ((0,), (0,))), preferred_element_type=F32)


HG_S = 8
HG_NS = HG_RB // HG_S


def _lane_sums(xs):
    l_io = lax.broadcasted_iota(jnp.int32, (HG_NS, HG_S, HG_S), 2)
    a = jnp.zeros((HG_NS, HG_S, HG_S), F32)
    for j, x in enumerate(xs):
        a = a + jnp.where(l_io == j, jnp.sum(x, axis=-1, keepdims=True), 0.0)
    return a


def _halves(x):
    y = x.reshape(8, 2, HG_S, x.shape[-1])
    return y[:, 0], y[:, 1]


def _join(first, second):
    return jnp.stack([first, second], axis=1).reshape(HG_RB, first.shape[-1])


def _cross_split(rev, b4):
    b_1, b_2 = _halves(b4)
    if rev:
        r = b_2[:, 0:1, :]
        return jnp.exp(b_1 - r), jnp.exp(r - b_2)
    r = b_1[:, HG_S - 1:HG_S, :]
    return jnp.exp(b_2 - r), jnp.exp(r - b_1)


def _hg_scan_fwd(qh, k, b, p_act, rev):
    anchor = 0 if rev else HG_C - 1

    def body(q_ref, k_ref, b_ref, v_ref, o_ref, st_ref, dsc):
        def phase_a(blk, _):
            rows = pl.ds(pl.multiple_of(blk * HG_RB, HG_RB), HG_RB)
            b3 = b_ref[rows, :].reshape(8, HG_C, 128)
            k3 = k_ref[rows, :].reshape(8, HG_C, 128)
            v3 = v_ref[rows, :].reshape(8, HG_C, 128)
            bl = b3[:, anchor:anchor + 1, :]
            kt = k3 * jnp.exp(bl - b3)
            st_ref[0, pl.ds(pl.multiple_of(blk * 8, 8), 8)] = _bdot(v3, kt, 1, 1)
            dsc[pl.ds(pl.multiple_of(blk * 8, 8), 8), :] = jnp.exp(bl[:, 0, :])
            return 0

        lax.fori_loop(0, HG_NB, phase_a, 0, unroll=HG_UNROLL)

        def phase_b(n, carry):
            c = (NCHUNK - 1 - n) if rev else n
            u = st_ref[0, c]
            st_ref[0, c] = carry
            return carry * dsc[pl.ds(c, 1), :] + u

        lax.fori_loop(0, NCHUNK // 3, lambda n3, s: phase_b(3 * n3 + 2, phase_b(3 * n3 + 1, phase_b(3 * n3, s))),
                      jnp.zeros((128, 128), F32))
        for c in range(NCHUNK, HG_SLOTS):
            st_ref[0, c] = jnp.zeros((128, 128), F32)

        t_io = lax.broadcasted_iota(jnp.int32, (HG_NS, HG_S, 128), 1)

        def phase_c(blk, _):
            rows = pl.ds(pl.multiple_of(blk * HG_RB, HG_RB), HG_RB)
            b4 = b_ref[rows, :].reshape(HG_NS, HG_S, 128)
            k4 = k_ref[rows, :].reshape(HG_NS, HG_S, 128)
            q4 = q_ref[rows, :].reshape(HG_NS, HG_S, 128)
            v4 = v_ref[rows, :].reshape(HG_NS, HG_S, 128)
            st = st_ref[0, pl.ds(pl.multiple_of(blk * 8, 8), 8)]
            o = _bdot((q4 * jnp.exp(b4)).reshape(8, HG_C, 128), st, 2, 2).reshape(HG_RB, 128)
            terms = []
            for s in range(HG_S):
                ok = (t_io <= s) if rev else (t_io >= s)
                f = jnp.exp(jnp.where(ok, b4 - b4[:, s:s + 1, :], NEG))
                terms.append(q4 * f * k4[:, s:s + 1, :])
            o_in = _bdot(_lane_sums(terms), v4, 2, 1)
            wq, wk = _cross_split(rev, b4)
            q_1, q_2 = _halves(q4)
            k_1, k_2 = _halves(k4)
            v_1, v_2 = _halves(v4)
            o_1, o_2 = _halves(o_in)
            if rev:
                o_1 = o_1 + _bdot(_bdot(q_1 * wq, k_2 * wk, 2, 2), v_2, 2, 1)
            else:
                o_2 = o_2 + _bdot(_bdot(q_2 * wq, k_1 * wk, 2, 2), v_1, 2, 1)
            o_ref[rows, :] = o + _join(o_1, o_2)
            return 0

        lax.fori_loop(0, HG_NB, phase_c, 0, unroll=HG_UNROLL)

    col = pl.BlockSpec((T, 128), lambda h: (0, h))
    return pl.pallas_call(
        body, name="hg_scan_bwd_dir" if rev else "hg_scan_fwd_dir", grid=(HG_HEADS,),
        in_specs=[col, col, col, pl.BlockSpec((T, 128), lambda h: (0, 24 + h))],
        out_specs=(col, pl.BlockSpec((1, HG_SLOTS, 128, 128), lambda h: (h, 0, 0, 0))),
        out_shape=(_sds((T, 512), F32), _sds((HG_HEADS, HG_SLOTS, 128, 128), F32)),
        scratch_shapes=[pltpu.VMEM((HG_SLOTS, 128), F32)],
        compiler_params=_cp(("parallel",), 56))(qh, k, b, p_act)


def _hg_scan_bwd(qh, k, b, p_act, st, do, rev):
    anchor = 0 if rev else HG_C - 1

    def body(q_ref, k_ref, b_ref, v_ref, st_ref, do_ref, dq_ref, dk_ref, db_ref, dv_ref, gst, dsc, dbl):
        def phase_a(blk, _):
            rows = pl.ds(pl.multiple_of(blk * HG_RB, HG_RB), HG_RB)
            b3 = b_ref[rows, :].reshape(8, HG_C, 128)
            q3 = q_ref[rows, :].reshape(8, HG_C, 128)
            do3 = do_ref[rows, :].reshape(8, HG_C, 128)
            gst[pl.ds(pl.multiple_of(blk * 8, 8), 8)] = _bdot(do3, q3 * jnp.exp(b3), 1, 1)
            dsc[pl.ds(pl.multiple_of(blk * 8, 8), 8), :] = jnp.exp(b3[:, anchor, :])
            return 0

        lax.fori_loop(0, HG_NB, phase_a, 0, unroll=HG_UNROLL)

        def phase_b(n, carry):
            c = n if rev else (NCHUNK - 1 - n)
            w = gst[c]
            gst[c] = carry
            dcv = dsc[pl.ds(c, 1), :]
            dbl[pl.ds(c, 1), :] = dcv * jnp.sum(st_ref[0, c] * carry, axis=0, keepdims=True)
            return carry * dcv + w

        lax.fori_loop(0, NCHUNK // 3, lambda n3, s: phase_b(3 * n3 + 2, phase_b(3 * n3 + 1, phase_b(3 * n3, s))),
                      jnp.zeros((128, 128), F32))
        for c in range(NCHUNK, HG_SLOTS):
            gst[c] = jnp.zeros((128, 128), F32)
            dbl[c:c + 1, :] = jnp.zeros((1, 128), F32)

        t_io = lax.broadcasted_iota(jnp.int32, (HG_NS, HG_S, 128), 1)
        t16 = lax.broadcasted_iota(jnp.int32, (8, HG_C, 128), 1)
        r_io = lax.broadcasted_iota(jnp.int32, (HG_NS, HG_S, HG_S), 1)
        l_io = lax.broadcasted_iota(jnp.int32, (HG_NS, HG_S, HG_S), 2)

        def phase_c(blk, _):
            rows = pl.ds(pl.multiple_of(blk * HG_RB, HG_RB), HG_RB)
            cs = pl.ds(pl.multiple_of(blk * 8, 8), 8)
            b4 = b_ref[rows, :].reshape(HG_NS, HG_S, 128)
            k4 = k_ref[rows, :].reshape(HG_NS, HG_S, 128)
            q4 = q_ref[rows, :].reshape(HG_NS, HG_S, 128)
            v4 = v_ref[rows, :].reshape(HG_NS, HG_S, 128)
            do4 = do_ref[rows, :].reshape(HG_NS, HG_S, 128)
            b3, k3, q3 = (z.reshape(8, HG_C, 128) for z in (b4, k4, q4))
            v3, do3 = v4.reshape(8, HG_C, 128), do4.reshape(8, HG_C, 128)
            s_t = st_ref[0, cs]
            g_t = gst[cs]
            bl = b3[:, anchor:anchor + 1, :]
            ekl = jnp.exp(bl - b3)
            kt = k3 * ekl
            dkt = _bdot(v3, g_t, 2, 1)
            dq = (_bdot(do3, s_t, 2, 1) * jnp.exp(b3)).reshape(HG_NS, HG_S, 128)
            dk = (dkt * ekl).reshape(HG_NS, HG_S, 128)
            dv = _bdot(kt, g_t, 2, 2).reshape(HG_NS, HG_S, 128)
            dbl3 = dbl[cs, :].reshape(8, 1, 128) + jnp.sum(dkt * kt, axis=1, keepdims=True)
            causal = (l_io >= r_io) if rev else (l_io <= r_io)
            da = jnp.where(causal, _bdot(do4, v4, 2, 2), 0.0)
            causal_t = (l_io <= r_io) if rev else (l_io >= r_io)
            dat = jnp.where(causal_t, _bdot(v4, do4, 2, 2), 0.0)
            for s in range(HG_S):
                ok = (t_io <= s) if rev else (t_io >= s)
                f = jnp.exp(jnp.where(ok, b4 - b4[:, s:s + 1, :], NEG))
                dq = dq + da[:, :, s:s + 1] * (f * k4[:, s:s + 1, :])
            terms = []
            for t in range(HG_S):
                ok = (t_io >= t) if rev else (t_io <= t)
                e = jnp.exp(jnp.where(ok, b4[:, t:t + 1, :] - b4, NEG))
                eq = e * q4[:, t:t + 1, :]
                dk = dk + dat[:, :, t:t + 1] * eq
                terms.append(eq * k4)
            dv = dv + _bdot(_lane_sums(terms), do4, 2, 1)
            wq, wk = _cross_split(rev, b4)
            pick = (lambda z: _halves(z)) if rev else (lambda z: _halves(z)[::-1])
            (q_q, _), (_, k_k), (_, v_k), (do_q, _) = pick(q4), pick(k4), pick(v4), pick(do4)
            qx, kx = q_q * wq, k_k * wk
            dq_q = _bdot(_bdot(do_q, v_k, 2, 2), kx, 2, 1) * wq
            dk_k = _bdot(_bdot(v_k, do_q, 2, 2), qx, 2, 1) * wk
            dv_k = _bdot(_bdot(kx, qx, 2, 2), do_q, 2, 1)
            zero = jnp.zeros((8, HG_S, 128), F32)
            place_q = (lambda z: _join(z, zero)) if rev else (lambda z: _join(zero, z))
            place_k = (lambda z: _join(zero, z)) if rev else (lambda z: _join(z, zero))
            dq2 = dq.reshape(HG_RB, 128) + place_q(dq_q)
            dk2 = dk.reshape(HG_RB, 128) + place_k(dk_k)
            dv2 = dv.reshape(HG_RB, 128) + place_k(dv_k)
            dq3, dk3 = dq2.reshape(8, HG_C, 128), dk2.reshape(8, HG_C, 128)
            db = q3 * dq3 - k3 * dk3 + jnp.where(t16 == anchor, dbl3, 0.0)
            dq_ref[rows, :] = dq2
            dk_ref[rows, :] = dk2
            db_ref[rows, :] = db.reshape(HG_RB, 128)
            dv_ref[rows, :] = dv2
            return 0

        lax.fori_loop(0, HG_NB, phase_c, 0, unroll=HG_UNROLL)

    col = pl.BlockSpec((T, 128), lambda h: (0, h))
    return pl.pallas_call(
        body, name="hg_scan_bwd_dir_bwd" if rev else "hg_scan_fwd_dir_bwd", grid=(HG_HEADS,),
        in_specs=[col, col, col, pl.BlockSpec((T, 128), lambda h: (0, 24 + h)),
                  pl.BlockSpec((1, HG_SLOTS, 128, 128), lambda h: (h, 0, 0, 0)), col],
        out_specs=(col,) * 4, out_shape=(_sds((T, 512), F32),) * 4,
        scratch_shapes=[pltpu.VMEM((HG_SLOTS, 128, 128), F32), pltpu.VMEM((HG_SLOTS, 128), F32),
                        pltpu.VMEM((HG_SLOTS, 128), F32)],
        compiler_params=_cp(("parallel",), 56))(qh, k, b, p_act, st, do)


def _row_valid(i, tm):
    r = lax.broadcasted_iota(jnp.int32, (tm, 1), 0) + i * tm
    return r < L


def _hg_post_rows(o, gv, gain_v, valid):
    parts = []
    for h in range(HG_HEADS):
        oh = o[:, 128 * h:128 * (h + 1)]
        parts.append(oh * lax.rsqrt(jnp.mean(oh * oh, axis=-1, keepdims=True) + EPS))
    return jnp.where(valid, jnp.concatenate(parts, axis=1) * gain_v * jax.nn.silu(gv), 0.0)


def _hg_post_bwd_rows(du, o, gv, gain_v, valid):
    duv = jnp.where(valid, du, 0.0)
    sig = jax.nn.sigmoid(gv)
    sg = gv * sig
    dn = duv * gain_v * sg
    do_parts, n_parts = [], []
    for h in range(HG_HEADS):
        sl = slice(128 * h, 128 * (h + 1))
        oh = o[:, sl]
        r = lax.rsqrt(jnp.mean(oh * oh, axis=-1, keepdims=True) + EPS)
        nh = oh * r
        dnh = dn[:, sl]
        do_parts.append(r * (dnh - nh * jnp.mean(dnh * nh, axis=-1, keepdims=True)))
        n_parts.append(nh)
    n = jnp.where(valid, jnp.concatenate(n_parts, axis=1), 0.0)
    do = jnp.where(valid, jnp.concatenate(do_parts, axis=1), 0.0)
    dg = duv * n * gain_v * (sig * (1.0 + gv * (1.0 - sig)))
    return do, dg, jnp.sum(duv * n * sg, axis=0, keepdims=True)


def _hg_pre_bwd(p_act, logits, dq_f, dq_b, dk_f, dk_b, db_f, db_b, dv_f, dv_b):
    def body(q_ref, zf_ref, zb_ref, lg_ref, dqf_ref, dqb_ref, dkf_ref, dkb_ref, dbf_ref, dbb_ref, dvf_ref, dvb_ref,
             dq_ref, dzf_ref, dzb_ref, di_ref, dlg_ref):
        i = pl.program_id(0)
        valid = _row_valid(i, HG_RB)
        qv = q_ref[...]
        sig = jax.nn.sigmoid(qv)
        dq_ref[...] = jnp.where(valid, (dqf_ref[...] + dqb_ref[...]) * (sig * (1.0 + qv * (1.0 - sig))), 0.0).astype(BF16)
        di_ref[...] = jnp.where(valid, dvf_ref[...] + dvb_ref[...], 0.0).astype(BF16)
        for d, (z_ref, dk_r, db_r, dz_ref) in enumerate(((zf_ref, dkf_ref, dbf_ref, dzf_ref), (zb_ref, dkb_ref, dbb_ref, dzb_ref))):
            lg = lg_ref[d]
            dl = lg[0:1, :] - lg[1:2, :]
            lb = jax.nn.sigmoid(dl)
            one_m_lb = jax.nn.sigmoid(-dl)
            log_f, _, snz, w2 = _hg_gate_terms(z_ref[...], lg)
            dbv = jnp.where(valid, db_r[...], 0.0)
            dkv = jnp.where(valid, dk_r[...], 0.0)
            dlf = jnp.dot(_chunk_tri(d == 1), dbv, precision=HI, preferred_element_type=F32)
            sz = 1.0 - snz
            dz_ref[...] = (dlf * w2 * snz - dkv * one_m_lb * sz * snz).astype(BF16)
            dlb = jnp.sum(dlf * snz * jnp.exp(-log_f) - dkv * snz, axis=0, keepdims=True)
            dl0 = dlb * lb * one_m_lb
            part = jnp.concatenate([dl0, -dl0], axis=0)

            @pl.when(i == 0)
            def _():
                dlg_ref[d] = part

            @pl.when(i > 0)
            def _():
                dlg_ref[d] += part

    blk = lambda c: pl.BlockSpec((HG_RB, 512), lambda i: (i, c))
    ob = pl.BlockSpec((HG_RB, 512), lambda i: (i, 0))
    lgs = pl.BlockSpec((2, 2, 512), lambda i: (0, 0, 0))
    return pl.pallas_call(
        body, name="hg_pre_bwd", grid=(HG_NB,),
        in_specs=[blk(3), blk(4), blk(5), lgs] + [ob] * 8,
        out_specs=(ob, ob, ob, ob, lgs),
        out_shape=(_sds((T, 512), BF16),) * 4 + (_sds((2, 2, 512), F32),),
        compiler_params=_cp(("arbitrary",)))(p_act, p_act, p_act, logits, dq_f, dq_b, dk_f, dk_b, db_f, db_b, dv_f, dv_b)


def _mix_fwd(o_na, o_f, o_b, gain, w_na, w_hg, p_act):
    def body(ona_ref, of_ref, ob_ref, g_ref, gain_ref, wna_ref, whg_ref, gna_ref, ghg_ref, o_ref, u_ref):
        u = _hg_post_rows(of_ref[...] + ob_ref[...], g_ref[...], gain_ref[...], _row_valid(pl.program_id(0), TM_B)).astype(BF16)
        u_ref[...] = u
        y_na = _dot(ona_ref[...], wna_ref[...])
        y_hg = _dot(u, whg_ref[...])
        o_ref[...] = (jax.nn.sigmoid(gna_ref[...]) * y_na + jax.nn.sigmoid(ghg_ref[...]) * y_hg).astype(BF16)

    act = pl.BlockSpec((TM_B, 512), lambda i: (i, 0))
    wsp = pl.BlockSpec((512, D), lambda i: (0, 0))
    return pl.pallas_call(
        body, name="mix_fwd", grid=(T // TM_B,),
        in_specs=[act, act, act, pl.BlockSpec((TM_B, 512), lambda i: (i, 7)), pl.BlockSpec((1, 512), lambda i: (0, 0)),
                  wsp, wsp, pl.BlockSpec((TM_B, D), lambda i: (i, 4)), pl.BlockSpec((TM_B, D), lambda i: (i, 5))],
        out_specs=(pl.BlockSpec((TM_B, D), lambda i: (i, 0)), act), out_shape=(_sds((T, D), BF16), _sds((T, 512), BF16)),
        compiler_params=_cp(("parallel",)))(o_na, o_f, o_b, p_act, gain, w_na, w_hg, p_act, p_act)


def _mix_bwd(o_na, u_hg, o_f, o_b, gain, w_na, w_hg, p_act, dmix):
    ni = T // TM_B

    def body(ona_ref, uhg_ref, of_ref, ob_ref, g_ref, gain_ref, wna_ref, whg_ref, gna_ref, ghg_ref, dmix_ref,
             dgna_ref, dghg_ref, dwna_ref, dwhg_ref, dona_ref, do_ref, dg_ref, dgain_ref, acc_na, acc_hg):
        i = pl.program_id(0)
        dm = dmix_ref[...].astype(F32)
        dxs = []
        for x_ref, w_ref, gt_ref, dgt_ref, dw_ref, acc in (
                (ona_ref, wna_ref, gna_ref, dgna_ref, dwna_ref, acc_na), (uhg_ref, whg_ref, ghg_ref, dghg_ref, dwhg_ref, acc_hg)):
            xv = x_ref[...]
            y = _dot(xv, w_ref[...])
            sg = jax.nn.sigmoid(gt_ref[...])
            dgt_ref[...] = (dm * y * sg * (1.0 - sg)).astype(BF16)
            dy = (dm * sg).astype(BF16)
            dxs.append(_dot(dy, w_ref[...], NT))
            part = _dot(xv, dy, TN)

            @pl.when(i == 0)
            def _():
                acc[...] = part

            @pl.when(i > 0)
            def _():
                acc[...] += part

            @pl.when(i == ni - 1)
            def _():
                dw_ref[...] = acc[...].astype(BF16)

        dona_ref[...] = dxs[0]
        do, dg, gpart = _hg_post_bwd_rows(dxs[1], of_ref[...] + ob_ref[...], g_ref[...], gain_ref[...], _row_valid(i, TM_B))
        do_ref[...] = do
        dg_ref[...] = dg.astype(BF16)

        @pl.when(i == 0)
        def _():
            dgain_ref[...] = gpart

        @pl.when(i > 0)
        def _():
            dgain_ref[...] += gpart

    act = pl.BlockSpec((TM_B, 512), lambda i: (i, 0))
    wsp = pl.BlockSpec((512, D), lambda i: (0, 0))
    rblk = pl.BlockSpec((TM_B, D), lambda i: (i, 0))
    vec = pl.BlockSpec((1, 512), lambda i: (0, 0))
    return pl.pallas_call(
        body, name="mix_bwd", grid=(ni,),
        in_specs=[act, act, act, act, pl.BlockSpec((TM_B, 512), lambda i: (i, 7)), vec, wsp, wsp,
                  pl.BlockSpec((TM_B, D), lambda i: (i, 4)), pl.BlockSpec((TM_B, D), lambda i: (i, 5)), rblk],
        out_specs=(rblk, rblk, wsp, wsp, act, act, act, vec),
        out_shape=(_sds((T, D), BF16), _sds((T, D), BF16), _sds((512, D), BF16), _sds((512, D), BF16),
                   _sds((T, 512), F32), _sds((T, 512), F32), _sds((T, 512), BF16), _sds((1, 512), F32)),
        scratch_shapes=[pltpu.VMEM((512, D), F32), pltpu.VMEM((512, D), F32)],
        compiler_params=_cp(("arbitrary",)))(o_na, u_hg, o_f, o_b, p_act, gain, w_na, w_hg, p_act, p_act, dmix)


def _wo_fwd(mix, w_o, h0, g_mlp):
    def body(mix_ref, w_ref, h0_ref, g_ref, h1_ref, m_ref):
        h1 = h0_ref[...] + _dot(mix_ref[...], w_ref[...])
        h1_ref[...] = h1
        r = lax.rsqrt(jnp.mean(h1 * h1, axis=-1, keepdims=True) + EPS)
        m_ref[...] = (h1 * r * g_ref[...]).astype(BF16)

    blk = pl.BlockSpec((TM_B, D), lambda i: (i, 0))
    return pl.pallas_call(
        body, name="wo_fwd", grid=(T // TM_B,),
        in_specs=[blk, pl.BlockSpec((D, D), lambda i: (0, 0)), blk, pl.BlockSpec((1, D), lambda i: (0, 0))],
        out_specs=(blk, blk), out_shape=(_sds((T, D), F32), _sds((T, D), BF16)),
        compiler_params=_cp(("parallel",)))(mix, w_o, h0, g_mlp)


def _wo_bwd(dh1_b, w_o, mix):
    ni = T // TM_B

    def body(dh_ref, w_ref, mix_ref, dmix_ref, dw_ref, acc):
        i = pl.program_id(0)
        dh = dh_ref[...]
        dmix_ref[...] = _dot(dh, w_ref[...], NT).astype(BF16)
        part = _dot(mix_ref[...], dh, TN)

        @pl.when(i == 0)
        def _():
            acc[...] = part

        @pl.when(i > 0)
        def _():
            acc[...] += part

        @pl.when(i == ni - 1)
        def _():
            dw_ref[...] = acc[...].astype(BF16)

    blk = pl.BlockSpec((TM_B, D), lambda i: (i, 0))
    wsp = pl.BlockSpec((D, D), lambda i: (0, 0))
    return pl.pallas_call(
        body, name="wo_bwd", grid=(ni,), in_specs=[blk, wsp, blk], out_specs=(blk, wsp),
        out_shape=(_sds((T, D), BF16), _sds((D, D), BF16)), scratch_shapes=[pltpu.VMEM((D, D), F32)],
        compiler_params=_cp(("arbitrary",)))(dh1_b, w_o, mix)


FF_B = D_FF // NDEV


def _loss_rows(xv, gv, tv, row0):
    r_io = lax.broadcasted_iota(jnp.int32, (xv.shape[0], 1), 0) + row0
    valid = (r_io >= NM) & (r_io < L)
    r = lax.rsqrt(jnp.mean(xv * xv, axis=-1, keepdims=True) + EPS)
    xh = xv * r
    err = jnp.where(valid, xh * gv - tv, 0.0)
    lpart = 0.5 * jnp.sum(jnp.sum(err * err, axis=-1, keepdims=True) * (1.0 / D), axis=0, keepdims=True)
    dy = err * (1.0 / D)
    dxh = dy * gv
    dh = r * (dxh - xh * jnp.mean(dxh * xh, axis=-1, keepdims=True))
    return lpart, dh, jnp.sum(dy * xh, axis=0, keepdims=True)


def _mlp_fwd_loss(m, wup_g, wdown_g, h1, g_final, tgt):
    nsub = TM_MM // TM_E

    def body(m_ref, wu_ref, wd_ref, h1_ref, g_ref, t_ref, loss_ref, dh_ref, dhb_ref, dg_ref, h2):
        i, j = pl.program_id(0), pl.program_id(1)
        up = jnp.maximum(_dot(m_ref[...], wu_ref[0]), 0.0)
        part = _dot((up * up).astype(BF16), wd_ref[0])

        @pl.when(j == 0)
        def _():
            h2[...] = h1_ref[...] + part

        @pl.when(j > 0)
        def _():
            h2[...] += part

        @pl.when(j == NDEV - 1)
        def _():
            lsum = jnp.zeros((1, 1), F32)
            gsum = jnp.zeros((1, D), F32)
            for s in range(nsub):
                rows = slice(s * TM_E, (s + 1) * TM_E)
                lpart, dh, gpart = _loss_rows(h2[rows, :], g_ref[...], t_ref[rows, :], i * TM_MM + s * TM_E)
                dh_ref[rows, :] = dh
                dhb_ref[rows, :] = dh.astype(BF16)
                lsum = lsum + lpart
                gsum = gsum + gpart
            lsum = jnp.broadcast_to(lsum, (1, 128))

            @pl.when(i == 0)
            def _():
                loss_ref[...] = lsum
                dg_ref[...] = gsum

            @pl.when(i > 0)
            def _():
                loss_ref[...] += lsum
                dg_ref[...] += gsum

    blk = pl.BlockSpec((TM_MM, D), lambda i, j: (i, 0))
    vec = pl.BlockSpec((1, D), lambda i, j: (0, 0))
    return pl.pallas_call(
        body, name="mlp_fwd_loss", grid=(T // TM_MM, NDEV),
        in_specs=[blk, pl.BlockSpec((1, D, FF_B), lambda i, j: (j, 0, 0)), pl.BlockSpec((1, FF_B, D), lambda i, j: (j, 0, 0)),
                  blk, vec, blk],
        out_specs=(pl.BlockSpec((1, 128), lambda i, j: (0, 0)), blk, blk, vec),
        out_shape=(_sds((1, 128), F32), _sds((T, D), F32), _sds((T, D), BF16), _sds((1, D), F32)),
        scratch_shapes=[pltpu.VMEM((TM_MM, D), F32)],
        compiler_params=_cp(("arbitrary", "arbitrary"), 56))(m, wup_g, wdown_g, h1, g_final, tgt)


def _mlp_bwd(m, dh2_b, wup_g, wdown_g, h1, g_mlp, dh2):
    ni = T // TM_B
    nsub = TM_B // TM_E

    def body(m_ref, dh_ref, wu_ref, wd_ref, h1_ref, g_ref, dres_ref, dwu_ref, dwd_ref, dh1_ref, dh1b_ref, dg_ref,
             dm_ref, acc_u, acc_d):
        j, i = pl.program_id(0), pl.program_id(1)
        rows = pl.ds(pl.multiple_of(i * TM_B, TM_B), TM_B)
        mv, dh = m_ref[...], dh_ref[...]
        r = jnp.maximum(_dot(mv, wu_ref[0]), 0.0)
        act = (r * r).astype(BF16)
        dact = _dot(dh, wd_ref[0], NT)
        dup = (dact * (2.0 * r)).astype(BF16)
        pd = _dot(act, dh, TN)
        pu = _dot(mv, dup, TN)
        dmv = _dot(dup, wu_ref[0], NT)

        @pl.when(i == 0)
        def _():
            acc_u[...] = pu
            acc_d[...] = pd

        @pl.when(i > 0)
        def _():
            acc_u[...] += pu
            acc_d[...] += pd

        @pl.when(i == ni - 1)
        def _():
            dwu_ref[0] = acc_u[...].astype(BF16)
            dwd_ref[0] = acc_d[...].astype(BF16)

        @pl.when(j == 0)
        def _():
            dm_ref[rows, :] = dmv

        @pl.when(j > 0)
        def _():
            dm_ref[rows, :] += dmv

        @pl.when(j == NDEV - 1)
        def _():
            gsum = jnp.zeros((1, D), F32)
            for s in range(nsub):
                sub = slice(s * TM_E, (s + 1) * TM_E)
                dm_rows = dm_ref[pl.ds(pl.multiple_of(i * TM_B + s * TM_E, TM_E), TM_E), :]
                dx, gpart = _norm_bwd_rows(h1_ref[sub, :], g_ref[...], dm_rows, dres_ref[sub, :])
                dh1_ref[sub, :] = dx
                dh1b_ref[sub, :] = dx.astype(BF16)
                gsum = gsum + gpart

            @pl.when(i == 0)
            def _():
                dg_ref[...] = gsum

            @pl.when(i > 0)
            def _():
                dg_ref[...] += gsum

    blk = pl.BlockSpec((TM_B, D), lambda j, i: (i, 0))
    late = pl.BlockSpec((TM_B, D), lambda j, i: (jnp.where(j == NDEV - 1, i, 0), 0))
    vec = pl.BlockSpec((1, D), lambda j, i: (0, 0))
    wus = pl.BlockSpec((1, D, FF_B), lambda j, i: (j, 0, 0))
    wds = pl.BlockSpec((1, FF_B, D), lambda j, i: (j, 0, 0))
    return pl.pallas_call(
        body, name="mlp_bwd", grid=(NDEV, ni), in_specs=[blk, blk, wus, wds, late, vec, late],
        out_specs=(wus, wds, late, late, vec),
        out_shape=(_sds((NDEV, D, FF_B), BF16), _sds((NDEV, FF_B, D), BF16), _sds((T, D), F32), _sds((T, D), BF16),
                   _sds((1, D), F32)),
        scratch_shapes=[pltpu.VMEM((T, D), F32), pltpu.VMEM((D, FF_B), F32), pltpu.VMEM((FF_B, D), F32)],
        compiler_params=_cp(("arbitrary", "arbitrary"), 56))(m, dh2_b, wup_g, wdown_g, h1, g_mlp, dh2)


def _adamw(parts, w, m, v, name):
    rr, cc = w.shape
    tr = rr
    for cand in (256, 128, 64):
        if rr % cand == 0 and rr > cand:
            tr = cand
            break
    c1 = 1.0 - ADAM_B1 ** ADAM_STEP
    c2 = 1.0 - ADAM_B2 ** ADAM_STEP

    def body(p_ref, w_ref, m_ref, v_ref, g_ref, d_ref, nm_ref, nv_ref):
        g = p_ref[0].astype(F32)
        for s in range(1, NDEV):
            g = g + p_ref[s].astype(F32)
        mn = ADAM_B1 * m_ref[...] + (1.0 - ADAM_B1) * g
        vn = ADAM_B2 * v_ref[...] + (1.0 - ADAM_B2) * (g * g)
        g_ref[...] = g
        nm_ref[...] = mn
        nv_ref[...] = vn
        d_ref[...] = -ADAM_LR * ((mn / c1) / (jnp.sqrt(vn / c2) + ADAM_EPS) + ADAM_WD * w_ref[...])

    blk = pl.BlockSpec((tr, cc), lambda i: (i, 0))
    return pl.pallas_call(
        body, name=name, grid=(rr // tr,),
        in_specs=[pl.BlockSpec((NDEV, tr, cc), lambda i: (0, i, 0)), blk, blk, blk],
        out_specs=(blk,) * 4, out_shape=(_sds((rr, cc), F32),) * 4,
        compiler_params=_cp(("parallel",)))(parts, w, m, v)


RPB_N = NA_HEADS * 15 * 31
RPB_PAD = 4096
OWN_ROWS = NM + 8


def _pad_rows(a, rows):
    return jnp.pad(a, ((0, rows - a.shape[0]),) + ((0, 0),) * (a.ndim - 1))


def _pack_owned(meta_blk, lb_blk):
    return jnp.concatenate([meta_blk, _pad_rows(lb_blk.reshape(2, 128), 8)], axis=0)


LOSS_ROW = 28


def _pack_replicated(n_mix, n_mlp, n_final, hg_gain, rpb, loss_row=None):
    flat = _pad_rows(rpb.reshape(RPB_N), RPB_PAD)
    gain8 = _pad_rows(hg_gain.reshape(4, 128), 8)
    if loss_row is not None:
        gain8 = gain8 + jnp.pad(loss_row, ((LOSS_ROW - 24, 31 - LOSS_ROW), (0, 0)))
    return jnp.concatenate([n_mix.reshape(8, 128), n_mlp.reshape(8, 128), n_final.reshape(8, 128), gain8,
                            flat.reshape(32, 128)], axis=0)


def _unpack_replicated(a):
    return (a[0:8].reshape(1, D), a[8:16].reshape(1, D), a[16:24].reshape(D), a[24:28].reshape(1, 512),
            a[32:64].reshape(RPB_PAD)[:RPB_N].reshape(1, NA_HEADS, 15, 31))


def kernel(x, meta_tokens, w_in, w_na_out, w_hg_out, w_o, w_up, w_down, norm_mix, norm_mlp, norm_final, hg_norm, na_rpb, hg_lb_logits, loss_target, m_meta_tokens, m_w_in, m_w_na_out, m_w_hg_out, m_w_o, m_w_up, m_w_down, m_norm_mix, m_norm_mlp, m_norm_final, m_hg_norm, m_na_rpb, m_hg_lb_logits, v_meta_tokens, v_w_in, v_w_na_out, v_w_hg_out, v_w_o, v_w_up, v_w_down, v_norm_mix, v_norm_mlp, v_norm_final, v_hg_norm, v_na_rpb, v_hg_lb_logits):
    owned = _pack_owned(meta_tokens, hg_lb_logits)
    first, tok = _exchange_start([w_in[0].astype(BF16), owned], [False] * 2, "gather_first_start", SAME_CORE_AND_SIBLING)
    bias_tab = _na_bias_table(_tie(jnp.pad(na_rpb[0], ((0, 0), (0, 0), (0, 128 - 31))), tok, "tie_bias_table"))
    first = _exchange_wait(first, [False] * 2, [bias_tab], "gather_first_wait", SAME_CORE_AND_SIBLING)
    win_g, owned_g = _forward_to_sibling(first, "gather_first_forward")
    later = [w[0].astype(BF16) for w in (w_na_out, w_hg_out, w_o, w_up, w_down)]
    later[0] = _tie(later[0], owned_g, "tie_gather_rest")
    gather_rest, tok = _exchange_start(later, [False] * 5, "gather_rest_start")
    win_g = _tie(win_g, tok, "tie_inproj")
    meta_full = jnp.transpose(owned_g[:, 0:NM, :], (1, 0, 2)).reshape(NM, D)
    logits = jnp.transpose(owned_g[:, NM:NM + 2, :].reshape(NDEV, 2, 2, 64), (1, 2, 0, 3)).reshape(2, 2, 512)

    h0 = jnp.concatenate([meta_full, x[0], jnp.zeros((T - L, D), F32)], axis=0)
    tgt = jnp.concatenate([jnp.zeros((NM, D), F32), loss_target[0], jnp.zeros((T - L, D), F32)], axis=0)

    a, a_t = _norm_fwd_t(h0, norm_mix, "norm_mix_fwd")
    p_act = _inproj_fwd(a, win_g)
    o_na, lse = _na_fwd(p_act, bias_tab)
    qh, k_f, b_f, k_b, b_b = _hg_pre(p_act, logits)
    o_f, st_f = _hg_scan_fwd(qh, k_f, b_f, p_act, False)
    o_b, st_b = _hg_scan_fwd(qh, k_b, b_b, p_act, True)
    wna_g, whg_g, wo_g, wup_g, wdown_g = _exchange_wait(gather_rest, [False] * 5, [o_f, o_b, o_na], "gather_rest_wait")
    w_o_full = wo_g.reshape(D, D)
    w_na_full = jnp.transpose(wna_g, (1, 0, 2)).reshape(512, D)
    w_hg_full = jnp.transpose(whg_g, (1, 0, 2)).reshape(512, D)
    mix, u_hg = _mix_fwd(o_na, o_f, o_b, hg_norm, w_na_full, w_hg_full, p_act)
    h1, m_act = _wo_fwd(mix, w_o_full, h0, norm_mlp)
    loss_part, dh2, dh2_b, d_nfinal = _mlp_fwd_loss(m_act, wup_g, wdown_g, h1, norm_final.reshape(1, D), tgt)

    dwup_p, dwdown_p, dh1, dh1_b, d_nmlp = _mlp_bwd(m_act, dh2_b, wup_g, wdown_g, h1, norm_mlp, dh2)
    sc_mlp, tok = _exchange_start([dwup_p, dwdown_p], [True] * 2, "scatter_mlp_start")
    dmix, dwo = _wo_bwd(_tie(dh1_b, tok, "tie_wo_bwd"), w_o_full, mix)
    sc_wo, tok = _exchange_start([dwo.reshape(NDEV, D // NDEV, D)], [True], "scatter_wo_start")
    dgna, dghg, dwna, dwhg, do_na, do_hg, dg_hg, d_gain = _mix_bwd(
        o_na, u_hg, o_f, o_b, hg_norm, w_na_full, w_hg_full, p_act, _tie(dmix, tok, "tie_mix_bwd"))
    owner_cols = lambda w: jnp.transpose(w.reshape(512, NDEV, D // NDEV), (1, 0, 2))
    sc_br, tok = _exchange_start([owner_cols(dwna), owner_cols(dwhg)], [True] * 2, "scatter_branch_start")
    do_hg = _tie(do_hg, tok, "tie_hg_scan_bwd")
    dq_f, dk_f, db_f, dv_f = _hg_scan_bwd(qh, k_f, b_f, p_act, st_f, do_hg, False)
    dq_b, dk_b, db_b, dv_b = _hg_scan_bwd(qh, k_b, b_b, p_act, st_b, do_hg, True)
    dq_hg, dz_f, dz_b, di_hg, d_logits = _hg_pre_bwd(p_act, logits, dq_f, dq_b, dk_f, dk_b, db_f, db_b, dv_f, dv_b)
    dq_na, dk_na, dv_na, dbias = _na_bwd(p_act, do_na, lse, bias_tab)
    dp_pieces = [dq_na, dk_na, dv_na, dq_hg, dz_f, dz_b, di_hg, dg_hg, dgna, dghg]
    dwin_p = _inproj_bwd_dw(a_t, dp_pieces)
    sc_in, tok = _exchange_start([dwin_p], [True], "scatter_in_start")
    dp_pieces[0] = _tie(dq_na, tok, "tie_inproj_bwd_da")
    dh0, d_nmix = _inproj_bwd_da(dp_pieces, win_g, h0, norm_mix, dh1)
    d_rpb = _na_rpb_reduce(_tie(dbias, tok, "tie_rpb_reduce"))[:, :, :31]

    res = {}

    def update(nm, parts, w, mm, vv):
        res[nm] = [r[None] for r in _adamw(parts, w[0], mm[0], vv[0], "adamw_" + nm)]
        return res[nm][1]

    wup_r, wdown_r = _exchange_wait(sc_mlp, [True] * 2, [dh0, d_rpb], "scatter_mlp_wait")
    update("w_up", wup_r, w_up, m_w_up, v_w_up)
    last = update("w_down", wdown_r, w_down, m_w_down, v_w_down)
    (wo_r,) = _exchange_wait(sc_wo, [True], [last], "scatter_wo_wait")
    last = update("w_o", wo_r, w_o, m_w_o, v_w_o)
    wna_r, whg_r = _exchange_wait(sc_br, [True] * 2, [last], "scatter_branch_wait")
    update("w_na_out", wna_r, w_na_out, m_w_na_out, v_w_na_out)
    last = update("w_hg_out", whg_r, w_hg_out, m_w_hg_out, v_w_hg_out)

    d_meta = jnp.transpose(dh0[0:NM].reshape(NM, NDEV, 128), (1, 0, 2))
    d_lg = jnp.transpose(d_logits.reshape(2, 2, NDEV, 64), (2, 0, 1, 3)).reshape(NDEV, 2, 128)
    owned_p = jnp.concatenate([d_meta, jnp.pad(d_lg, ((0, 0), (0, OWN_ROWS - NM - 2), (0, 0)))], axis=1)
    repl_p = _pack_replicated(d_nmix, d_nmlp, d_nfinal, d_gain, d_rpb, loss_part)
    owned_r, repl_r = _exchange([_tie(owned_p, last, "tie_scatter_small"), repl_p], [True, False], "scatter_small")
    own = _adamw(owned_r, owned, _pack_owned(m_meta_tokens, m_hg_lb_logits), _pack_owned(v_meta_tokens, v_hg_lb_logits),
                 "adamw_owned_small")
    res["meta_tokens"] = [r[0:NM] for r in own]
    res["hg_lb_logits"] = [r[NM:NM + 2].reshape(2, 2, 64) for r in own]
    rep = _adamw(repl_r, _pack_replicated(norm_mix, norm_mlp, norm_final, hg_norm, na_rpb),
                 _pack_replicated(m_norm_mix, m_norm_mlp, m_norm_final, m_hg_norm, m_na_rpb),
                 _pack_replicated(v_norm_mix, v_norm_mlp, v_norm_final, v_hg_norm, v_na_rpb), "adamw_replicated")
    for q in range(4):
        um = _unpack_replicated(rep[q])
        for nm, val in zip(("norm_mix", "norm_mlp", "norm_final", "hg_norm", "na_rpb"), um):
            res.setdefault(nm, [None] * 4)[q] = val
    (win_r,) = _exchange_wait(sc_in, [True], [rep[1], own[1]], "scatter_in_wait")
    update("w_in", win_r, w_in, m_w_in, v_w_in)

    loss = jnp.sum(repl_r[:, LOSS_ROW, 0])
    grad_x = dh0[NM:L][None]
    order = ("meta_tokens", "w_in", "w_na_out", "w_hg_out", "w_o", "w_up", "w_down", "norm_mix", "norm_mlp", "norm_final",
             "hg_norm", "na_rpb", "hg_lb_logits")
    outs = [loss, grad_x]
    for q in range(4):
        outs += [res[nm][q] for nm in order]
    return tuple(outs)
```

```python
import functools

import numpy as np
import jax
import jax.numpy as jnp
from jax import lax
from jax.experimental import pallas as pl
from jax.experimental.pallas import tpu as pltpu

F32 = jnp.float32
BF16 = jnp.bfloat16

D = 1024
SEQ = 2048
NM = 16
L = SEQ + NM
T = 2176
NDEV = 8
EPS = 1e-6
GRID_W = 64
ROWS = SEQ // GRID_W
NA_HEADS = 8
NA_DH = 64
NA_SCALE = NA_DH ** -0.5
HG_HEADS = 4
HG_C = 16
NCHUNK = L // HG_C
D_FF = 4096
IN_COLS = 6144
NEG = -1e30

ADAM_LR = 0.001
ADAM_B1 = 0.9
ADAM_B2 = 0.999
ADAM_EPS = 1e-08
ADAM_WD = 0.01
ADAM_STEP = 10

MESH_ID = pl.DeviceIdType.MESH
ANY = pl.BlockSpec(memory_space=pl.ANY)

NN = (((1,), (0,)), ((), ()))
NT = (((1,), (1,)), ((), ()))
TN = (((0,), (0,)), ((), ()))


def _cp(sem=None, vmem_mb=48):
    return pltpu.CompilerParams(dimension_semantics=sem, vmem_limit_bytes=vmem_mb * 1024 * 1024)


def _dot(a, b, dims=NN):
    return lax.dot_general(a, b, dims, preferred_element_type=F32)


def _sds(shape, dtype):
    return jax.ShapeDtypeStruct(shape, dtype)


HBM = pl.BlockSpec(memory_space=pltpu.HBM)
SEM = pl.BlockSpec(memory_space=pltpu.SEMAPHORE)
EFFECT = pltpu.SideEffectType.DATAFLOW_SIDE_EFFECTING


def _exchange(arrs, scatter, name):
    n = len(arrs)
    out_shapes = []
    for a, sc in zip(arrs, scatter):
        out_shapes.append(_sds(a.shape if sc else (NDEV,) + a.shape, a.dtype))

    def body(*refs):
        ins, outs = refs[:n], refs[n:2 * n]
        send_sems, recv_sems, loc_sems = refs[2 * n:]
        me = 4 * lax.axis_index("x") + 2 * lax.axis_index("y") + lax.axis_index("c")
        copies = []
        for k in range(n):
            src_me = ins[k].at[me] if scatter[k] else ins[k]
            loc = pltpu.make_async_copy(src_me, outs[k].at[me], loc_sems.at[k])
            loc.start()
            copies.append(loc)
        remote = _peer_copies(ins, outs, scatter, send_sems, recv_sems)
        for cp in remote:
            cp.start()
        for cp in remote:
            cp.wait_recv()
        for cp in remote:
            cp.wait_send()
        for cp in copies:
            cp.wait()

    return pl.pallas_call(
        body, name=name, out_shape=tuple(out_shapes), in_specs=[ANY] * n, out_specs=tuple([ANY] * n),
        scratch_shapes=[pltpu.SemaphoreType.DMA((n * (NDEV - 1),)), pltpu.SemaphoreType.DMA((n * (NDEV - 1),)),
                        pltpu.SemaphoreType.DMA((n,))],
    )(*arrs)


def _forward_to_sibling(bufs, name):
    n = len(bufs)

    def body(*refs):
        ins, outs = refs[:n], refs[n:2 * n]
        send_sems, recv_sems = refs[2 * n:]
        x, y, c = lax.axis_index("x"), lax.axis_index("y"), lax.axis_index("c")
        copies = []
        for k in range(n):
            for j, (cx, cy) in enumerate(((1 - x, y), (x, 1 - y), (1 - x, 1 - y))):
                slot = 4 * cx + 2 * cy + c
                copies.append(pltpu.make_async_remote_copy(
                    src_ref=ins[k].at[slot], dst_ref=outs[k].at[slot], send_sem=send_sems.at[3 * k + j],
                    recv_sem=recv_sems.at[3 * k + j], device_id=(x, y, 1 - c), device_id_type=MESH_ID))
        for cp in copies:
            cp.start()
        for cp in copies:
            cp.wait_recv()
        for cp in copies:
            cp.wait_send()

    return pl.pallas_call(
        body, name=name, out_shape=tuple(_sds(b.shape, b.dtype) for b in bufs), in_specs=[ANY] * n,
        out_specs=tuple([ANY] * n), input_output_aliases={k: k for k in range(n)},
        scratch_shapes=[pltpu.SemaphoreType.DMA((3 * n,)), pltpu.SemaphoreType.DMA((3 * n,))],
    )(*bufs)


ALL_PEERS = tuple(range(1, NDEV))
SAME_CORE_AND_SIBLING = (1, 2, 4, 6)


def _peer_copies(srcs, lands, scatter, send_sems, recv_sems, masks=ALL_PEERS):
    x, y, c = lax.axis_index("x"), lax.axis_index("y"), lax.axis_index("c")
    me = 4 * x + 2 * y + c
    out = []
    for k in range(len(srcs)):
        for m in masks:
            px, py, pc = x ^ (m >> 2), y ^ ((m >> 1) & 1), c ^ (m & 1)
            src = srcs[k].at[4 * px + 2 * py + pc] if scatter[k] else srcs[k]
            out.append(pltpu.make_async_remote_copy(
                src_ref=src, dst_ref=lands[k].at[me], send_sem=send_sems.at[k * (NDEV - 1) + m - 1],
                recv_sem=recv_sems.at[k * (NDEV - 1) + m - 1],
                device_id=(px, py, pc), device_id_type=MESH_ID))
    return out


def _exchange_start(arrs, scatter, name, masks=ALL_PEERS):
    n = len(arrs)
    me = 4 * lax.axis_index("x") + 2 * lax.axis_index("y") + lax.axis_index("c")
    lands = []
    for a, sc in zip(arrs, scatter):
        own = lax.dynamic_index_in_dim(a, me, 0, keepdims=True) if sc else a[None]
        shape = a.shape if sc else (NDEV,) + a.shape
        lands.append(lax.dynamic_update_index_in_dim(lax.empty(shape, a.dtype), own, me, 0))

    def body(*refs):
        srcs, lnds = refs[:n], refs[n:2 * n]
        send_sems, recv_sems = refs[2 * n], refs[2 * n + 1]
        token = refs[-1]
        for cp in _peer_copies(srcs, lnds, scatter, send_sems, recv_sems, masks):
            cp.start()
        token[...] = jnp.zeros_like(token)

    ops = [pltpu.with_memory_space_constraint(a, pltpu.HBM) for a in list(arrs) + lands]
    res = pl.pallas_call(
        body, name=name,
        out_shape=(pltpu.SemaphoreType.DMA((n * (NDEV - 1),)), pltpu.SemaphoreType.DMA((n * (NDEV - 1),)))
        + tuple(pltpu.HBM(o.shape, o.dtype) for o in ops) + (_sds((8, 128), F32),),
        in_specs=[HBM] * (2 * n), out_specs=(SEM, SEM) + (HBM,) * (2 * n) + (pl.BlockSpec(memory_space=pltpu.VMEM),),
        input_output_aliases={k: 2 + k for k in range(2 * n)},
        compiler_params=pltpu.CompilerParams(has_side_effects=EFFECT),
    )(*ops)
    return res[:-1], res[-1]


def _exchange_wait(handle, scatter, after, name, masks=ALL_PEERS):
    send_sems, recv_sems = handle[0], handle[1]
    bufs = handle[2:]
    n = len(bufs) // 2
    after = list(after)

    def body(*refs):
        srcs, lnds = refs[:n], refs[n:2 * n]
        for cp in _peer_copies(srcs, lnds, scatter, refs[2 * n], refs[2 * n + 1], masks):
            cp.wait_send()
            cp.wait_recv()

    res = pl.pallas_call(
        body, name=name, out_shape=tuple(pltpu.HBM(b.shape, b.dtype) for b in bufs),
        in_specs=[HBM] * (2 * n) + [SEM, SEM] + [ANY] * len(after), out_specs=(HBM,) * (2 * n),
        input_output_aliases={k: k for k in range(2 * n)},
        compiler_params=pltpu.CompilerParams(has_side_effects=EFFECT),
    )(*bufs, send_sems, recv_sems, *after)
    return res[n:]


def _tie(x, token, name):
    def body(x_ref, t_ref, o_ref):
        del x_ref, t_ref, o_ref

    return pl.pallas_call(body, name=name, out_shape=_sds(x.shape, x.dtype), in_specs=[ANY, ANY], out_specs=ANY,
                          input_output_aliases={0: 0})(x, token)


TM_E = 272


def _norm_fwd_t(h, g, name):
    def body(h_ref, g_ref, o_ref, ot_ref):
        xv = h_ref[...]
        r = lax.rsqrt(jnp.mean(xv * xv, axis=-1, keepdims=True) + EPS)
        y = xv * r * g_ref[...]
        o_ref[...] = y.astype(BF16)
        ot_ref[...] = y.T.astype(BF16)

    return pl.pallas_call(
        body, name=name, grid=(T // 128,),
        in_specs=[pl.BlockSpec((128, D), lambda i: (i, 0)), pl.BlockSpec((1, D), lambda i: (0, 0))],
        out_specs=(pl.BlockSpec((128, D), lambda i: (i, 0)), pl.BlockSpec((D, 128), lambda i: (0, i))),
        out_shape=(_sds((T, D), BF16), _sds((D, T), BF16)), compiler_params=_cp(("parallel",)))(h, g)


def _norm_bwd_rows(xv, gv, dnv, dres):
    r = lax.rsqrt(jnp.mean(xv * xv, axis=-1, keepdims=True) + EPS)
    xh = xv * r
    dxh = dnv * gv
    dx = dres + r * (dxh - xh * jnp.mean(dxh * xh, axis=-1, keepdims=True))
    return dx, jnp.sum(dnv * xh, axis=0, keepdims=True)


TM_MM = 1088


def _inproj_fwd(a, w_g):
    nb = w_g.shape[2]

    def body(a_ref, w_ref, o_ref):
        o_ref[...] = _dot(a_ref[...], w_ref[0])

    return pl.pallas_call(
        body, name="inproj_fwd", grid=(T // TM_MM, NDEV),
        in_specs=[pl.BlockSpec((TM_MM, D), lambda i, j: (i, 0)), pl.BlockSpec((1, D, nb), lambda i, j: (j, 0, 0))],
        out_specs=pl.BlockSpec((TM_MM, nb), lambda i, j: (i, j)), out_shape=_sds((T, NDEV * nb), F32),
        compiler_params=_cp(("parallel", "parallel")))(a, w_g)


TM_B = 544


W_IN_B = IN_COLS // NDEV


def _inproj_bwd_dw(a_t, dp):
    def body(at_ref, dp_ref, dw_ref):
        dw_ref[0] = _dot(at_ref[...], dp_ref[...]).astype(BF16)

    return pl.pallas_call(
        body, name="inproj_bwd_dw", grid=(NDEV,),
        in_specs=[pl.BlockSpec((D, T), lambda j: (0, 0)), pl.BlockSpec((T, W_IN_B), lambda j: (0, j))],
        out_specs=pl.BlockSpec((1, D, W_IN_B), lambda j: (j, 0, 0)), out_shape=_sds((NDEV, D, W_IN_B), BF16),
        compiler_params=_cp(("parallel",)))(a_t, dp)


def _inproj_bwd_da(dp, w_g, h0, g_mix, dh1):
    nsub = TM_MM // TM_E

    def body(dp_ref, w_ref, h0_ref, g_ref, dres_ref, dh0_ref, dg_ref, da):
        i, j = pl.program_id(0), pl.program_id(1)
        dav = _dot(dp_ref[...], w_ref[0], NT)

        @pl.when(j == 0)
        def _():
            da[...] = dav

        @pl.when(j > 0)
        def _():
            da[...] += dav

        @pl.when(j == NDEV - 1)
        def _():
            gsum = jnp.zeros((1, D), F32)
            for s in range(nsub):
                sub = slice(s * TM_E, (s + 1) * TM_E)
                dx, gpart = _norm_bwd_rows(h0_ref[sub, :], g_ref[...], da[sub, :], dres_ref[sub, :])
                dh0_ref[sub, :] = dx
                gsum = gsum + gpart

            @pl.when(i == 0)
            def _():
                dg_ref[...] = gsum

            @pl.when(i > 0)
            def _():
                dg_ref[...] += gsum

    rblk = pl.BlockSpec((TM_MM, D), lambda i, j: (i, 0))
    vec = pl.BlockSpec((1, D), lambda i, j: (0, 0))
    return pl.pallas_call(
        body, name="inproj_bwd_da", grid=(T // TM_MM, NDEV),
        in_specs=[pl.BlockSpec((TM_MM, W_IN_B), lambda i, j: (i, j)), pl.BlockSpec((1, D, W_IN_B), lambda i, j: (j, 0, 0)),
                  rblk, vec, rblk],
        out_specs=(rblk, vec), out_shape=(_sds((T, D), F32), _sds((1, D), F32)),
        scratch_shapes=[pltpu.VMEM((TM_MM, D), F32)],
        compiler_params=_cp(("arbitrary", "arbitrary"), 56))(dp, w_g, h0, g_mix, dh1)


NA_QB = 256
NA_GROUPS = ROWS // 4
NA_UROWS = 11
NA_KW = NA_UROWS * GRID_W
NA_KU = 768


def _na_row_offset(var, i, j):
    valid = (j < 8, i <= j < i + 8, 3 <= j < NA_UROWS)[var]
    return (j - i + (7, 3, 0)[var]) if valid else None


def _na_bias_table(rp):
    def body(r_ref, o_ref):
        row3 = lax.broadcasted_iota(jnp.int32, (15, GRID_W, 128), 1)
        lane3 = lax.broadcasted_iota(jnp.int32, (15, GRID_W, 128), 2)
        w3 = lane3 & (GRID_W - 1)
        cs3 = jnp.clip(row3 - 8, 0, GRID_W - 16)
        lane = lax.broadcasted_iota(jnp.int32, (GRID_W, 128), 1)
        neg = jnp.full((GRID_W, 128), NEG, F32)
        z = jnp.stack([jnp.broadcast_to(r_ref[0, a:a + 1, :], (GRID_W, 128)) for a in range(15)])
        for bit in range(6):
            sh = 1 << bit
            z = jnp.where((row3 & sh) != 0, jnp.roll(z, sh, axis=2), z)
        z = jnp.roll(z, 128 - 15, axis=2)
        z = jnp.where(lane3 < GRID_W, z, 0.0)
        z = z + jnp.roll(z, GRID_W, axis=2)
        tabs = jnp.where((w3 >= cs3) & (w3 < cs3 + 16), z, NEG)
        tail = jnp.where(lane < GRID_W + NM, 0.0, NEG)
        for var in range(3):
            for i in range(4):
                for jp in range(NA_KU // 128):
                    halves = []
                    for j in (2 * jp, 2 * jp + 1):
                        a = _na_row_offset(var, i, j) if j < NA_UROWS else None
                        halves.append(tail if j >= NA_UROWS else (neg if a is None else tabs[a]))
                    o_ref[var, 0, i * 64:(i + 1) * 64, jp * 128:(jp + 1) * 128] = jnp.where(lane < GRID_W, halves[0], halves[1])

    return pl.pallas_call(
        body, name="na_bias_table", grid=(NA_HEADS,),
        in_specs=[pl.BlockSpec((1, 15, 128), lambda h: (h, 0, 0))],
        out_specs=pl.BlockSpec((3, 1, NA_QB, NA_KU), lambda h: (0, h, 0, 0)),
        out_shape=_sds((3, NA_HEADS, NA_QB, NA_KU), F32), compiler_params=_cp(("parallel",)))(rp)


def _na_var(g):
    return jnp.where(g == 0, 0, jnp.where(g == NA_GROUPS - 1, 2, 1))


def _na_load_window(src_ref, dst, g):
    us = jnp.clip(4 * g - 4, 0, ROWS - NA_UROWS)
    kstart = pl.multiple_of(NM + GRID_W * us, 16)
    dst[0:NA_KW, :] = src_ref[pl.ds(kstart, NA_KW), :].astype(BF16)
    dst[NA_KW:NA_KW + NM, :] = src_ref[0:NM, :].astype(BF16)
    dst[NA_KW + NM:, :] = jnp.zeros((NA_KU - NA_KW - NM, 128), BF16)
    return kstart


def _na_fwd(p_act, bias_tab):
    def body(q_ref, k_ref, v_ref, b_ref, o_ref, lse_ref, ku, vu):
        g = pl.program_id(1)
        _na_load_window(k_ref, ku, g)
        _na_load_window(v_ref, vu, g)
        qstart = pl.multiple_of(NM + NA_QB * g, 16)
        q = q_ref[pl.ds(qstart, NA_QB), :]
        lane = lax.broadcasted_iota(jnp.int32, (NA_QB, 128), 1)
        o_h, lse_h = [], []
        for h in range(2):
            hm = (lane < 64) if h == 0 else (lane >= 64)
            qm = jnp.where(hm, q, 0.0).astype(BF16)
            s = _dot(qm, ku[...], NT) * NA_SCALE + b_ref[0, h]
            m = jnp.max(s, axis=-1, keepdims=True)
            p = jnp.exp(s - m)
            l = jnp.sum(p, axis=-1, keepdims=True)
            o_h.append(_dot(p.astype(BF16), vu[...]) / l)
            lse_h.append(jnp.broadcast_to(m + jnp.log(l), (NA_QB, 128)))
        o_ref[pl.ds(qstart, NA_QB), :] = jnp.where(lane < 64, o_h[0], o_h[1]).astype(BF16)
        lse_ref[0, pl.ds(qstart, NA_QB), :] = jnp.where(lane < 64, lse_h[0], lse_h[1])

        @pl.when(g == 0)
        def _():
            qm_ = q_ref[0:NM, :]
            lane_m = lax.broadcasted_iota(jnp.int32, (NM, 128), 1)
            km, vm = ku[NA_KW:NA_KW + NM, :], vu[NA_KW:NA_KW + NM, :]
            om = []
            for h in range(2):
                hm = (lane_m < 64) if h == 0 else (lane_m >= 64)
                s = _dot(jnp.where(hm, qm_, 0.0).astype(BF16), km, NT) * NA_SCALE
                p = jnp.exp(s - jnp.max(s, axis=-1, keepdims=True))
                l = jnp.sum(p, axis=-1, keepdims=True)
                om.append(_dot(p.astype(BF16), vm) / l)
            o_ref[0:NM, :] = jnp.where(lane_m < 64, om[0], om[1]).astype(BF16)
            o_ref[L:T, :] = jnp.zeros((T - L, 128), BF16)
            lse_ref[0, 0:NM, :] = jnp.zeros((NM, 128), F32)
            lse_ref[0, L:T, :] = jnp.zeros((T - L, 128), F32)

    col = lambda off: pl.BlockSpec((T, 128), lambda hp, g: (0, off + hp))
    return pl.pallas_call(
        body, name="na_fwd", grid=(4, NA_GROUPS),
        in_specs=[col(0), col(4), col(8),
                  pl.BlockSpec((1, 2, NA_QB, NA_KU), lambda hp, g: (_na_var(g), hp, 0, 0))],
        out_specs=(pl.BlockSpec((T, 128), lambda hp, g: (0, hp)), pl.BlockSpec((1, T, 128), lambda hp, g: (hp, 0, 0))),
        out_shape=(_sds((T, 512), BF16), _sds((4, T, 128), F32)),
        scratch_shapes=[pltpu.VMEM((NA_KU, 128), BF16), pltpu.VMEM((NA_KU, 128), BF16)],
        compiler_params=_cp(("parallel", "arbitrary")))(p_act, p_act, p_act, bias_tab)


def _na_bwd(p_act, do, lse, bias_tab):
    def body(q_ref, k_ref, v_ref, do_ref, lse_ref, b_ref, dq_ref, dk_ref, dv_ref, db_ref, ku, vu):
        g = pl.program_id(1)

        @pl.when(g == 0)
        def _():
            dq_ref[...] = jnp.zeros((T, 128), F32)
            dk_ref[...] = jnp.zeros((T, 128), F32)
            dv_ref[...] = jnp.zeros((T, 128), F32)

        kstart = _na_load_window(k_ref, ku, g)
        _na_load_window(v_ref, vu, g)
        qstart = pl.multiple_of(NM + NA_QB * g, 16)
        q = q_ref[pl.ds(qstart, NA_QB), :]
        dov = do_ref[pl.ds(qstart, NA_QB), :]
        lsev = lse_ref[0, pl.ds(qstart, NA_QB), :]
        lane = lax.broadcasted_iota(jnp.int32, (NA_QB, 128), 1)
        first = (g == 0) | (g == 1) | (g == NA_GROUPS - 1)
        dq_h = []
        dku = jnp.zeros((NA_KU, 128), F32)
        dvu = jnp.zeros((NA_KU, 128), F32)
        for h in range(2):
            hm = (lane < 64) if h == 0 else (lane >= 64)
            qm = jnp.where(hm, q, 0.0).astype(BF16)
            dom = jnp.where(hm, dov, 0.0).astype(BF16)
            s = _dot(qm, ku[...], NT) * NA_SCALE + b_ref[0, h]
            p = jnp.exp(s - lsev[:, 64 * h:64 * h + 1])
            dp = _dot(dom, vu[...], NT)
            delta = jnp.sum(p * dp, axis=-1, keepdims=True)
            ds = p * (dp - delta)

            @pl.when(first)
            def _():
                db_ref[0, h] = ds

            @pl.when(jnp.logical_not(first))
            def _():
                db_ref[0, h] += ds

            dsb = (ds * NA_SCALE).astype(BF16)
            dq_h.append(_dot(dsb, ku[...]))
            dku = dku + _dot(dsb, qm, TN)
            dvu = dvu + _dot(p.astype(BF16), dom, TN)
        dq_ref[pl.ds(qstart, NA_QB), :] = jnp.where(lane < 64, dq_h[0], dq_h[1])
        dk_ref[pl.ds(kstart, NA_KW), :] += dku[0:NA_KW]
        dv_ref[pl.ds(kstart, NA_KW), :] += dvu[0:NA_KW]
        dk_ref[0:NM, :] += dku[NA_KW:NA_KW + NM]
        dv_ref[0:NM, :] += dvu[NA_KW:NA_KW + NM]

        @pl.when(g == 0)
        def _():
            qm_ = q_ref[0:NM, :]
            dom_ = do_ref[0:NM, :]
            lane_m = lax.broadcasted_iota(jnp.int32, (NM, 128), 1)
            km, vm = ku[NA_KW:NA_KW + NM, :], vu[NA_KW:NA_KW + NM, :]
            dqs = []
            dkm = jnp.zeros((NM, 128), F32)
            dvm = jnp.zeros((NM, 128), F32)
            for h in range(2):
                hm = (lane_m < 64) if h == 0 else (lane_m >= 64)
                qh = jnp.where(hm, qm_, 0.0).astype(BF16)
                doh = jnp.where(hm, dom_, 0.0).astype(BF16)
                s = _dot(qh, km, NT) * NA_SCALE
                e = jnp.exp(s - jnp.max(s, axis=-1, keepdims=True))
                p = e / jnp.sum(e, axis=-1, keepdims=True)
                dp = _dot(doh, vm, NT)
                ds = p * (dp - jnp.sum(p * dp, axis=-1, keepdims=True))
                dsb = (ds * NA_SCALE).astype(BF16)
                dqs.append(_dot(dsb, km))
                dkm = dkm + _dot(dsb, qh, TN)
                dvm = dvm + _dot(p.astype(BF16), doh, TN)
            dq_ref[0:NM, :] = jnp.where(lane_m < 64, dqs[0], dqs[1])
            dk_ref[0:NM, :] += dkm
            dv_ref[0:NM, :] += dvm

    col = lambda off: pl.BlockSpec((T, 128), lambda hp, g: (0, off + hp))
    ocol = pl.BlockSpec((T, 128), lambda hp, g: (0, hp))
    bspec = pl.BlockSpec((1, 2, NA_QB, NA_KU), lambda hp, g: (_na_var(g), hp, 0, 0))
    return pl.pallas_call(
        body, name="na_bwd", grid=(4, NA_GROUPS),
        in_specs=[col(0), col(4), col(8), ocol, pl.BlockSpec((1, T, 128), lambda hp, g: (hp, 0, 0)), bspec],
        out_specs=(ocol, ocol, ocol, bspec),
        out_shape=(_sds((T, 512), F32), _sds((T, 512), F32), _sds((T, 512), F32), _sds((3, NA_HEADS, NA_QB, NA_KU), F32)),
        scratch_shapes=[pltpu.VMEM((NA_KU, 128), BF16), pltpu.VMEM((NA_KU, 128), BF16)],
        compiler_params=_cp(("parallel", "arbitrary")))(p_act, p_act, p_act, do, lse, bias_tab)


def _na_rpb_reduce(dbias):
    def body(db_ref, o_ref):
        lane = lax.broadcasted_iota(jnp.int32, (GRID_W, 128), 1)
        row3 = lax.broadcasted_iota(jnp.int32, (15, GRID_W, 128), 1)
        lane3 = lax.broadcasted_iota(jnp.int32, (15, GRID_W, 128), 2)
        accs = []
        for a in range(15):
            acc = jnp.zeros((GRID_W, 128), F32)
            for var in range(3):
                for i in range(4):
                    for j in range(NA_UROWS):
                        if _na_row_offset(var, i, j) == a:
                            pair = db_ref[var, 0, i * 64:(i + 1) * 64, (j // 2) * 128:(j // 2 + 1) * 128]
                            acc = acc + jnp.where((lane < GRID_W) if j % 2 == 0 else (lane >= GRID_W), pair, 0.0)
            accs.append(acc)
        z = jnp.stack(accs)
        z = jnp.where(lane3 < GRID_W, z + jnp.roll(z, GRID_W, axis=2), 0.0)
        for bit in range(6):
            sh = 1 << bit
            z = jnp.where((row3 & sh) != 0, jnp.roll(z, 128 - sh, axis=2), z)
        z = jnp.roll(z, 15, axis=2)
        o_ref[0] = jnp.sum(z, axis=1)

    return pl.pallas_call(
        body, name="na_rpb_reduce", grid=(NA_HEADS,),
        in_specs=[pl.BlockSpec((3, 1, NA_QB, NA_KU), lambda h: (0, h, 0, 0))],
        out_specs=pl.BlockSpec((1, 15, 128), lambda h: (h, 0, 0)), out_shape=_sds((NA_HEADS, 15, 128), F32),
        compiler_params=_cp(("parallel",)))(dbias)


HG_RB = 128
HG_NB = T // HG_RB
HG_SLOTS = HG_NB * 8
HI = lax.Precision.HIGHEST
HG_UNROLL = 4


def _chunk_tri(lower):
    r = lax.broadcasted_iota(jnp.int32, (HG_RB, HG_RB), 0)
    c = lax.broadcasted_iota(jnp.int32, (HG_RB, HG_RB), 1)
    same = (r // HG_C) == (c // HG_C)
    keep = (c <= r) if lower else (c >= r)
    return jnp.where(same & keep, 1.0, 0.0).astype(F32)


def _hg_gate_terms(z, lg):
    dl = lg[0:1, :] - lg[1:2, :]
    log_lb = jax.nn.log_sigmoid(dl)
    log_1mlb = jax.nn.log_sigmoid(-dl)
    yz = log_1mlb + jax.nn.log_sigmoid(z)
    log_f = jnp.logaddexp(log_lb, yz)
    snz = jax.nn.sigmoid(-z)
    k = jnp.exp(log_1mlb) * snz
    w2 = jnp.exp(yz - log_f)
    return log_f, k, snz, w2


def _hg_pre(p_act, logits):
    def body(q_ref, zf_ref, zb_ref, lg_ref, qh_ref, kf_ref, bf_ref, kb_ref, bb_ref):
        qh_ref[...] = jax.nn.silu(q_ref[...])
        lf, kf, _, _ = _hg_gate_terms(zf_ref[...], lg_ref[0])
        kf_ref[...] = kf
        bf_ref[...] = jnp.dot(_chunk_tri(True), lf, precision=HI, preferred_element_type=F32)
        lb_, kb, _, _ = _hg_gate_terms(zb_ref[...], lg_ref[1])
        kb_ref[...] = kb
        bb_ref[...] = jnp.dot(_chunk_tri(False), lb_, precision=HI, preferred_element_type=F32)

    blk = lambda c: pl.BlockSpec((HG_RB, 512), lambda i: (i, c))
    ob = pl.BlockSpec((HG_RB, 512), lambda i: (i, 0))
    return pl.pallas_call(
        body, name="hg_pre", grid=(HG_NB,),
        in_specs=[blk(3), blk(4), blk(5), pl.BlockSpec((2, 2, 512), lambda i: (0, 0, 0))],
        out_specs=(ob,) * 5, out_shape=(_sds((T, 512), F32),) * 5,
        compiler_params=_cp(("parallel",)))(p_act, p_act, p_act, logits)


def _bdot(a, b, ca, cb):
    return lax.dot_general(a.astype(BF16), b.astype(BF16), (((ca,), (cb,)), ((0,), (0,))), preferred_element_type=F32)


HG_S = 8
HG_NS = HG_RB // HG_S


def _lane_sums(xs):
    l_io = lax.broadcasted_iota(jnp.int32, (HG_NS, HG_S, HG_S), 2)
    a = jnp.zeros((HG_NS, HG_S, HG_S), F32)
    for j, x in enumerate(xs):
        a = a + jnp.where(l_io == j, jnp.sum(x, axis=-1, keepdims=True), 0.0)
    return a


def _halves(x):
    y = x.reshape(8, 2, HG_S, x.shape[-1])
    return y[:, 0], y[:, 1]


def _join(first, second):
    return jnp.stack([first, second], axis=1).reshape(HG_RB, first.shape[-1])


def _cross_split(rev, b4):
    b_1, b_2 = _halves(b4)
    if rev:
        r = b_2[:, 0:1, :]
        return jnp.exp(b_1 - r), jnp.exp(r - b_2)
    r = b_1[:, HG_S - 1:HG_S, :]
    return jnp.exp(b_2 - r), jnp.exp(r - b_1)


def _hg_scan_fwd(qh, k, b, p_act, rev):
    anchor = 0 if rev else HG_C - 1

    def body(q_ref, k_ref, b_ref, v_ref, o_ref, st_ref, dsc):
        def phase_a(blk, _):
            rows = pl.ds(pl.multiple_of(blk * HG_RB, HG_RB), HG_RB)
            b3 = b_ref[rows, :].reshape(8, HG_C, 128)
            k3 = k_ref[rows, :].reshape(8, HG_C, 128)
            v3 = v_ref[rows, :].reshape(8, HG_C, 128)
            bl = b3[:, anchor:anchor + 1, :]
            kt = k3 * jnp.exp(bl - b3)
            st_ref[0, pl.ds(pl.multiple_of(blk * 8, 8), 8)] = _bdot(v3, kt, 1, 1)
            dsc[pl.ds(pl.multiple_of(blk * 8, 8), 8), :] = jnp.exp(bl[:, 0, :])
            return 0

        lax.fori_loop(0, HG_NB, phase_a, 0, unroll=HG_UNROLL)

        def phase_b(n, carry):
            c = (NCHUNK - 1 - n) if rev else n
            u = st_ref[0, c]
            st_ref[0, c] = carry
            return carry * dsc[pl.ds(c, 1), :] + u

        lax.fori_loop(0, NCHUNK // 3, lambda n3, s: phase_b(3 * n3 + 2, phase_b(3 * n3 + 1, phase_b(3 * n3, s))),
                      jnp.zeros((128, 128), F32))
        for c in range(NCHUNK, HG_SLOTS):
            st_ref[0, c] = jnp.zeros((128, 128), F32)

        t_io = lax.broadcasted_iota(jnp.int32, (HG_NS, HG_S, 128), 1)

        def phase_c(blk, _):
            rows = pl.ds(pl.multiple_of(blk * HG_RB, HG_RB), HG_RB)
            b4 = b_ref[rows, :].reshape(HG_NS, HG_S, 128)
            k4 = k_ref[rows, :].reshape(HG_NS, HG_S, 128)
            q4 = q_ref[rows, :].reshape(HG_NS, HG_S, 128)
            v4 = v_ref[rows, :].reshape(HG_NS, HG_S, 128)
            st = st_ref[0, pl.ds(pl.multiple_of(blk * 8, 8), 8)]
            o = _bdot((q4 * jnp.exp(b4)).reshape(8, HG_C, 128), st, 2, 2).reshape(HG_RB, 128)
            terms = []
            for s in range(HG_S):
                ok = (t_io <= s) if rev else (t_io >= s)
                f = jnp.exp(jnp.where(ok, b4 - b4[:, s:s + 1, :], NEG))
                terms.append(q4 * f * k4[:, s:s + 1, :])
            o_in = _bdot(_lane_sums(terms), v4, 2, 1)
            wq, wk = _cross_split(rev, b4)
            q_1, q_2 = _halves(q4)
            k_1, k_2 = _halves(k4)
            v_1, v_2 = _halves(v4)
            o_1, o_2 = _halves(o_in)
            if rev:
                o_1 = o_1 + _bdot(_bdot(q_1 * wq, k_2 * wk, 2, 2), v_2, 2, 1)
            else:
                o_2 = o_2 + _bdot(_bdot(q_2 * wq, k_1 * wk, 2, 2), v_1, 2, 1)
            o_ref[rows, :] = o + _join(o_1, o_2)
            return 0

        lax.fori_loop(0, HG_NB, phase_c, 0, unroll=HG_UNROLL)

    col = pl.BlockSpec((T, 128), lambda h: (0, h))
    return pl.pallas_call(
        body, name="hg_scan_bwd_dir" if rev else "hg_scan_fwd_dir", grid=(HG_HEADS,),
        in_specs=[col, col, col, pl.BlockSpec((T, 128), lambda h: (0, 24 + h))],
        out_specs=(col, pl.BlockSpec((1, HG_SLOTS, 128, 128), lambda h: (h, 0, 0, 0))),
        out_shape=(_sds((T, 512), F32), _sds((HG_HEADS, HG_SLOTS, 128, 128), F32)),
        scratch_shapes=[pltpu.VMEM((HG_SLOTS, 128), F32)],
        compiler_params=_cp(("parallel",), 56))(qh, k, b, p_act)


def _hg_scan_bwd(qh, k, b, p_act, st, do, rev):
    anchor = 0 if rev else HG_C - 1

    def body(q_ref, k_ref, b_ref, v_ref, st_ref, do_ref, dq_ref, dk_ref, db_ref, dv_ref, gst, dsc, dbl):
        def phase_a(blk, _):
            rows = pl.ds(pl.multiple_of(blk * HG_RB, HG_RB), HG_RB)
            b3 = b_ref[rows, :].reshape(8, HG_C, 128)
            q3 = q_ref[rows, :].reshape(8, HG_C, 128)
            do3 = do_ref[rows, :].reshape(8, HG_C, 128)
            gst[pl.ds(pl.multiple_of(blk * 8, 8), 8)] = _bdot(do3, q3 * jnp.exp(b3), 1, 1)
            dsc[pl.ds(pl.multiple_of(blk * 8, 8), 8), :] = jnp.exp(b3[:, anchor, :])
            return 0

        lax.fori_loop(0, HG_NB, phase_a, 0, unroll=HG_UNROLL)

        def phase_b(n, carry):
            c = n if rev else (NCHUNK - 1 - n)
            w = gst[c]
            gst[c] = carry
            dcv = dsc[pl.ds(c, 1), :]
            dbl[pl.ds(c, 1), :] = dcv * jnp.sum(st_ref[0, c] * carry, axis=0, keepdims=True)
            return carry * dcv + w

        lax.fori_loop(0, NCHUNK // 3, lambda n3, s: phase_b(3 * n3 + 2, phase_b(3 * n3 + 1, phase_b(3 * n3, s))),
                      jnp.zeros((128, 128), F32))
        for c in range(NCHUNK, HG_SLOTS):
            gst[c] = jnp.zeros((128, 128), F32)
            dbl[c:c + 1, :] = jnp.zeros((1, 128), F32)

        t_io = lax.broadcasted_iota(jnp.int32, (HG_NS, HG_S, 128), 1)
        t16 = lax.broadcasted_iota(jnp.int32, (8, HG_C, 128), 1)
        r_io = lax.broadcasted_iota(jnp.int32, (HG_NS, HG_S, HG_S), 1)
        l_io = lax.broadcasted_iota(jnp.int32, (HG_NS, HG_S, HG_S), 2)

        def phase_c(blk, _):
            rows = pl.ds(pl.multiple_of(blk * HG_RB, HG_RB), HG_RB)
            cs = pl.ds(pl.multiple_of(blk * 8, 8), 8)
            b4 = b_ref[rows, :].reshape(HG_NS, HG_S, 128)
            k4 = k_ref[rows, :].reshape(HG_NS, HG_S, 128)
            q4 = q_ref[rows, :].reshape(HG_NS, HG_S, 128)
            v4 = v_ref[rows, :].reshape(HG_NS, HG_S, 128)
            do4 = do_ref[rows, :].reshape(HG_NS, HG_S, 128)
            b3, k3, q3 = (z.reshape(8, HG_C, 128) for z in (b4, k4, q4))
            v3, do3 = v4.reshape(8, HG_C, 128), do4.reshape(8, HG_C, 128)
            s_t = st_ref[0, cs]
            g_t = gst[cs]
            bl = b3[:, anchor:anchor + 1, :]
            ekl = jnp.exp(bl - b3)
            kt = k3 * ekl
            dkt = _bdot(v3, g_t, 2, 1)
            dq = (_bdot(do3, s_t, 2, 1) * jnp.exp(b3)).reshape(HG_NS, HG_S, 128)
            dk = (dkt * ekl).reshape(HG_NS, HG_S, 128)
            dv = _bdot(kt, g_t, 2, 2).reshape(HG_NS, HG_S, 128)
            dbl3 = dbl[cs, :].reshape(8, 1, 128) + jnp.sum(dkt * kt, axis=1, keepdims=True)
            causal = (l_io >= r_io) if rev else (l_io <= r_io)
            da = jnp.where(causal, _bdot(do4, v4, 2, 2), 0.0)
            causal_t = (l_io <= r_io) if rev else (l_io >= r_io)
            dat = jnp.where(causal_t, _bdot(v4, do4, 2, 2), 0.0)
            for s in range(HG_S):
                ok = (t_io <= s) if rev else (t_io >= s)
                f = jnp.exp(jnp.where(ok, b4 - b4[:, s:s + 1, :], NEG))
                dq = dq + da[:, :, s:s + 1] * (f * k4[:, s:s + 1, :])
            terms = []
            for t in range(HG_S):
                ok = (t_io >= t) if rev else (t_io <= t)
                e = jnp.exp(jnp.where(ok, b4[:, t:t + 1, :] - b4, NEG))
                eq = e * q4[:, t:t + 1, :]
                dk = dk + dat[:, :, t:t + 1] * eq
                terms.append(eq * k4)
            dv = dv + _bdot(_lane_sums(terms), do4, 2, 1)
            wq, wk = _cross_split(rev, b4)
            pick = (lambda z: _halves(z)) if rev else (lambda z: _halves(z)[::-1])
            (q_q, _), (_, k_k), (_, v_k), (do_q, _) = pick(q4), pick(k4), pick(v4), pick(do4)
            qx, kx = q_q * wq, k_k * wk
            dq_q = _bdot(_bdot(do_q, v_k, 2, 2), kx, 2, 1) * wq
            dk_k = _bdot(_bdot(v_k, do_q, 2, 2), qx, 2, 1) * wk
            dv_k = _bdot(_bdot(kx, qx, 2, 2), do_q, 2, 1)
            zero = jnp.zeros((8, HG_S, 128), F32)
            place_q = (lambda z: _join(z, zero)) if rev else (lambda z: _join(zero, z))
            place_k = (lambda z: _join(zero, z)) if rev else (lambda z: _join(z, zero))
            dq2 = dq.reshape(HG_RB, 128) + place_q(dq_q)
            dk2 = dk.reshape(HG_RB, 128) + place_k(dk_k)
            dv2 = dv.reshape(HG_RB, 128) + place_k(dv_k)
            dq3, dk3 = dq2.reshape(8, HG_C, 128), dk2.reshape(8, HG_C, 128)
            db = q3 * dq3 - k3 * dk3 + jnp.where(t16 == anchor, dbl3, 0.0)
            dq_ref[rows, :] = dq2
            dk_ref[rows, :] = dk2
            db_ref[rows, :] = db.reshape(HG_RB, 128)
            dv_ref[rows, :] = dv2
            return 0

        lax.fori_loop(0, HG_NB, phase_c, 0, unroll=HG_UNROLL)

    col = pl.BlockSpec((T, 128), lambda h: (0, h))
    return pl.pallas_call(
        body, name="hg_scan_bwd_dir_bwd" if rev else "hg_scan_fwd_dir_bwd", grid=(HG_HEADS,),
        in_specs=[col, col, col, pl.BlockSpec((T, 128), lambda h: (0, 24 + h)),
                  pl.BlockSpec((1, HG_SLOTS, 128, 128), lambda h: (h, 0, 0, 0)), col],
        out_specs=(col,) * 4, out_shape=(_sds((T, 512), F32),) * 4,
        scratch_shapes=[pltpu.VMEM((HG_SLOTS, 128, 128), F32), pltpu.VMEM((HG_SLOTS, 128), F32),
                        pltpu.VMEM((HG_SLOTS, 128), F32)],
        compiler_params=_cp(("parallel",), 56))(qh, k, b, p_act, st, do)


def _row_valid(i, tm):
    r = lax.broadcasted_iota(jnp.int32, (tm, 1), 0) + i * tm
    return r < L


def _hg_post_rows(o, gv, gain_v, valid):
    parts = []
    for h in range(HG_HEADS):
        oh = o[:, 128 * h:128 * (h + 1)]
        parts.append(oh * lax.rsqrt(jnp.mean(oh * oh, axis=-1, keepdims=True) + EPS))
    return jnp.where(valid, jnp.concatenate(parts, axis=1) * gain_v * jax.nn.silu(gv), 0.0)


def _hg_post_bwd_rows(du, o, gv, gain_v, valid):
    duv = jnp.where(valid, du, 0.0)
    sig = jax.nn.sigmoid(gv)
    sg = gv * sig
    dn = duv * gain_v * sg
    do_parts, n_parts = [], []
    for h in range(HG_HEADS):
        sl = slice(128 * h, 128 * (h + 1))
        oh = o[:, sl]
        r = lax.rsqrt(jnp.mean(oh * oh, axis=-1, keepdims=True) + EPS)
        nh = oh * r
        dnh = dn[:, sl]
        do_parts.append(r * (dnh - nh * jnp.mean(dnh * nh, axis=-1, keepdims=True)))
        n_parts.append(nh)
    n = jnp.where(valid, jnp.concatenate(n_parts, axis=1), 0.0)
    do = jnp.where(valid, jnp.concatenate(do_parts, axis=1), 0.0)
    dg = duv * n * gain_v * (sig * (1.0 + gv * (1.0 - sig)))
    return do, dg, jnp.sum(duv * n * sg, axis=0, keepdims=True)


def _hg_pre_bwd(p_act, logits, dq_f, dq_b, dk_f, dk_b, db_f, db_b, dv_f, dv_b):
    def body(q_ref, zf_ref, zb_ref, lg_ref, dqf_ref, dqb_ref, dkf_ref, dkb_ref, dbf_ref, dbb_ref, dvf_ref, dvb_ref,
             dq_ref, dzf_ref, dzb_ref, di_ref, dlg_ref):
        i = pl.program_id(0)
        valid = _row_valid(i, HG_RB)
        qv = q_ref[...]
        sig = jax.nn.sigmoid(qv)
        dq_ref[...] = jnp.where(valid, (dqf_ref[...] + dqb_ref[...]) * (sig * (1.0 + qv * (1.0 - sig))), 0.0).astype(BF16)
        di_ref[...] = jnp.where(valid, dvf_ref[...] + dvb_ref[...], 0.0).astype(BF16)
        for d, (z_ref, dk_r, db_r, dz_ref) in enumerate(((zf_ref, dkf_ref, dbf_ref, dzf_ref), (zb_ref, dkb_ref, dbb_ref, dzb_ref))):
            lg = lg_ref[d]
            dl = lg[0:1, :] - lg[1:2, :]
            lb = jax.nn.sigmoid(dl)
            one_m_lb = jax.nn.sigmoid(-dl)
            log_f, _, snz, w2 = _hg_gate_terms(z_ref[...], lg)
            dbv = jnp.where(valid, db_r[...], 0.0)
            dkv = jnp.where(valid, dk_r[...], 0.0)
            dlf = jnp.dot(_chunk_tri(d == 1), dbv, precision=HI, preferred_element_type=F32)
            sz = 1.0 - snz
            dz_ref[...] = (dlf * w2 * snz - dkv * one_m_lb * sz * snz).astype(BF16)
            dlb = jnp.sum(dlf * snz * jnp.exp(-log_f) - dkv * snz, axis=0, keepdims=True)
            dl0 = dlb * lb * one_m_lb
            part = jnp.concatenate([dl0, -dl0], axis=0)

            @pl.when(i == 0)
            def _():
                dlg_ref[d] = part

            @pl.when(i > 0)
            def _():
                dlg_ref[d] += part

    blk = lambda c: pl.BlockSpec((HG_RB, 512), lambda i: (i, c))
    ob = pl.BlockSpec((HG_RB, 512), lambda i: (i, 0))
    lgs = pl.BlockSpec((2, 2, 512), lambda i: (0, 0, 0))
    return pl.pallas_call(
        body, name="hg_pre_bwd", grid=(HG_NB,),
        in_specs=[blk(3), blk(4), blk(5), lgs] + [ob] * 8,
        out_specs=(ob, ob, ob, ob, lgs),
        out_shape=(_sds((T, 512), BF16),) * 4 + (_sds((2, 2, 512), F32),),
        compiler_params=_cp(("arbitrary",)))(p_act, p_act, p_act, logits, dq_f, dq_b, dk_f, dk_b, db_f, db_b, dv_f, dv_b)


def _mix_fwd(o_na, o_f, o_b, gain, w_na, w_hg, p_act):
    def body(ona_ref, of_ref, ob_ref, g_ref, gain_ref, wna_ref, whg_ref, gna_ref, ghg_ref, o_ref, u_ref):
        u = _hg_post_rows(of_ref[...] + ob_ref[...], g_ref[...], gain_ref[...], _row_valid(pl.program_id(0), TM_B)).astype(BF16)
        u_ref[...] = u
        y_na = _dot(ona_ref[...], wna_ref[...])
        y_hg = _dot(u, whg_ref[...])
        o_ref[...] = (jax.nn.sigmoid(gna_ref[...]) * y_na + jax.nn.sigmoid(ghg_ref[...]) * y_hg).astype(BF16)

    act = pl.BlockSpec((TM_B, 512), lambda i: (i, 0))
    wsp = pl.BlockSpec((512, D), lambda i: (0, 0))
    return pl.pallas_call(
        body, name="mix_fwd", grid=(T // TM_B,),
        in_specs=[act, act, act, pl.BlockSpec((TM_B, 512), lambda i: (i, 7)), pl.BlockSpec((1, 512), lambda i: (0, 0)),
                  wsp, wsp, pl.BlockSpec((TM_B, D), lambda i: (i, 4)), pl.BlockSpec((TM_B, D), lambda i: (i, 5))],
        out_specs=(pl.BlockSpec((TM_B, D), lambda i: (i, 0)), act), out_shape=(_sds((T, D), BF16), _sds((T, 512), BF16)),
        compiler_params=_cp(("parallel",)))(o_na, o_f, o_b, p_act, gain, w_na, w_hg, p_act, p_act)


def _mix_bwd(o_na, u_hg, o_f, o_b, gain, w_na, w_hg, p_act, dmix):
    ni = T // TM_B

    def body(ona_ref, uhg_ref, of_ref, ob_ref, g_ref, gain_ref, wna_ref, whg_ref, gna_ref, ghg_ref, dmix_ref,
             dgna_ref, dghg_ref, dwna_ref, dwhg_ref, dona_ref, do_ref, dg_ref, dgain_ref, acc_na, acc_hg):
        i = pl.program_id(0)
        dm = dmix_ref[...].astype(F32)
        dxs = []
        for x_ref, w_ref, gt_ref, dgt_ref, dw_ref, acc in (
                (ona_ref, wna_ref, gna_ref, dgna_ref, dwna_ref, acc_na), (uhg_ref, whg_ref, ghg_ref, dghg_ref, dwhg_ref, acc_hg)):
            xv = x_ref[...]
            y = _dot(xv, w_ref[...])
            sg = jax.nn.sigmoid(gt_ref[...])
            dgt_ref[...] = (dm * y * sg * (1.0 - sg)).astype(BF16)
            dy = (dm * sg).astype(BF16)
            dxs.append(_dot(dy, w_ref[...], NT))
            part = _dot(xv, dy, TN)

            @pl.when(i == 0)
            def _():
                acc[...] = part

            @pl.when(i > 0)
            def _():
                acc[...] += part

            @pl.when(i == ni - 1)
            def _():
                dw_ref[...] = acc[...].astype(BF16)

        dona_ref[...] = dxs[0]
        do, dg, gpart = _hg_post_bwd_rows(dxs[1], of_ref[...] + ob_ref[...], g_ref[...], gain_ref[...], _row_valid(i, TM_B))
        do_ref[...] = do
        dg_ref[...] = dg.astype(BF16)

        @pl.when(i == 0)
        def _():
            dgain_ref[...] = gpart

        @pl.when(i > 0)
        def _():
            dgain_ref[...] += gpart

    act = pl.BlockSpec((TM_B, 512), lambda i: (i, 0))
    wsp = pl.BlockSpec((512, D), lambda i: (0, 0))
    rblk = pl.BlockSpec((TM_B, D), lambda i: (i, 0))
    vec = pl.BlockSpec((1, 512), lambda i: (0, 0))
    return pl.pallas_call(
        body, name="mix_bwd", grid=(ni,),
        in_specs=[act, act, act, act, pl.BlockSpec((TM_B, 512), lambda i: (i, 7)), vec, wsp, wsp,
                  pl.BlockSpec((TM_B, D), lambda i: (i, 4)), pl.BlockSpec((TM_B, D), lambda i: (i, 5)), rblk],
        out_specs=(rblk, rblk, wsp, wsp, act, act, act, vec),
        out_shape=(_sds((T, D), BF16), _sds((T, D), BF16), _sds((512, D), BF16), _sds((512, D), BF16),
                   _sds((T, 512), F32), _sds((T, 512), F32), _sds((T, 512), BF16), _sds((1, 512), F32)),
        scratch_shapes=[pltpu.VMEM((512, D), F32), pltpu.VMEM((512, D), F32)],
        compiler_params=_cp(("arbitrary",)))(o_na, u_hg, o_f, o_b, p_act, gain, w_na, w_hg, p_act, p_act, dmix)


def _wo_fwd(mix, w_o, h0, g_mlp):
    def body(mix_ref, w_ref, h0_ref, g_ref, h1_ref, m_ref):
        h1 = h0_ref[...] + _dot(mix_ref[...], w_ref[...])
        h1_ref[...] = h1
        r = lax.rsqrt(jnp.mean(h1 * h1, axis=-1, keepdims=True) + EPS)
        m_ref[...] = (h1 * r * g_ref[...]).astype(BF16)

    blk = pl.BlockSpec((TM_B, D), lambda i: (i, 0))
    return pl.pallas_call(
        body, name="wo_fwd", grid=(T // TM_B,),
        in_specs=[blk, pl.BlockSpec((D, D), lambda i: (0, 0)), blk, pl.BlockSpec((1, D), lambda i: (0, 0))],
        out_specs=(blk, blk), out_shape=(_sds((T, D), F32), _sds((T, D), BF16)),
        compiler_params=_cp(("parallel",)))(mix, w_o, h0, g_mlp)


def _wo_bwd(dh1_b, w_o, mix):
    ni = T // TM_B

    def body(dh_ref, w_ref, mix_ref, dmix_ref, dw_ref, acc):
        i = pl.program_id(0)
        dh = dh_ref[...]
        dmix_ref[...] = _dot(dh, w_ref[...], NT).astype(BF16)
        part = _dot(mix_ref[...], dh, TN)

        @pl.when(i == 0)
        def _():
            acc[...] = part

        @pl.when(i > 0)
        def _():
            acc[...] += part

        @pl.when(i == ni - 1)
        def _():
            dw_ref[...] = acc[...].astype(BF16)

    blk = pl.BlockSpec((TM_B, D), lambda i: (i, 0))
    wsp = pl.BlockSpec((D, D), lambda i: (0, 0))
    return pl.pallas_call(
        body, name="wo_bwd", grid=(ni,), in_specs=[blk, wsp, blk], out_specs=(blk, wsp),
        out_shape=(_sds((T, D), BF16), _sds((D, D), BF16)), scratch_shapes=[pltpu.VMEM((D, D), F32)],
        compiler_params=_cp(("arbitrary",)))(dh1_b, w_o, mix)


FF_B = D_FF // NDEV


def _loss_rows(xv, gv, tv, row0):
    r_io = lax.broadcasted_iota(jnp.int32, (xv.shape[0], 1), 0) + row0
    valid = (r_io >= NM) & (r_io < L)
    r = lax.rsqrt(jnp.mean(xv * xv, axis=-1, keepdims=True) + EPS)
    xh = xv * r
    err = jnp.where(valid, xh * gv - tv, 0.0)
    lpart = 0.5 * jnp.sum(jnp.sum(err * err, axis=-1, keepdims=True) * (1.0 / D), axis=0, keepdims=True)
    dy = err * (1.0 / D)
    dxh = dy * gv
    dh = r * (dxh - xh * jnp.mean(dxh * xh, axis=-1, keepdims=True))
    return lpart, dh, jnp.sum(dy * xh, axis=0, keepdims=True)


def _mlp_fwd_loss(m, wup_g, wdown_g, h1, g_final, tgt):
    nsub = TM_MM // TM_E

    def body(m_ref, wu_ref, wd_ref, h1_ref, g_ref, t_ref, loss_ref, dh_ref, dhb_ref, dg_ref, h2):
        i, j = pl.program_id(0), pl.program_id(1)
        up = jnp.maximum(_dot(m_ref[...], wu_ref[0]), 0.0)
        part = _dot((up * up).astype(BF16), wd_ref[0])

        @pl.when(j == 0)
        def _():
            h2[...] = h1_ref[...] + part

        @pl.when(j > 0)
        def _():
            h2[...] += part

        @pl.when(j == NDEV - 1)
        def _():
            lsum = jnp.zeros((1, 1), F32)
            gsum = jnp.zeros((1, D), F32)
            for s in range(nsub):
                rows = slice(s * TM_E, (s + 1) * TM_E)
                lpart, dh, gpart = _loss_rows(h2[rows, :], g_ref[...], t_ref[rows, :], i * TM_MM + s * TM_E)
                dh_ref[rows, :] = dh
                dhb_ref[rows, :] = dh.astype(BF16)
                lsum = lsum + lpart
                gsum = gsum + gpart
            lsum = jnp.broadcast_to(lsum, (1, 128))

            @pl.when(i == 0)
            def _():
                loss_ref[...] = lsum
                dg_ref[...] = gsum

            @pl.when(i > 0)
            def _():
                loss_ref[...] += lsum
                dg_ref[...] += gsum

    blk = pl.BlockSpec((TM_MM, D), lambda i, j: (i, 0))
    vec = pl.BlockSpec((1, D), lambda i, j: (0, 0))
    return pl.pallas_call(
        body, name="mlp_fwd_loss", grid=(T // TM_MM, NDEV),
        in_specs=[blk, pl.BlockSpec((1, D, FF_B), lambda i, j: (j, 0, 0)), pl.BlockSpec((1, FF_B, D), lambda i, j: (j, 0, 0)),
                  blk, vec, blk],
        out_specs=(pl.BlockSpec((1, 128), lambda i, j: (0, 0)), blk, blk, vec),
        out_shape=(_sds((1, 128), F32), _sds((T, D), F32), _sds((T, D), BF16), _sds((1, D), F32)),
        scratch_shapes=[pltpu.VMEM((TM_MM, D), F32)],
        compiler_params=_cp(("arbitrary", "arbitrary"), 56))(m, wup_g, wdown_g, h1, g_final, tgt)


def _mlp_bwd(m, dh2_b, wup_g, wdown_g, h1, g_mlp, dh2):
    ni = T // TM_B
    nsub = TM_B // TM_E

    def body(m_ref, dh_ref, wu_ref, wd_ref, h1_ref, g_ref, dres_ref, dwu_ref, dwd_ref, dh1_ref, dh1b_ref, dg_ref,
             dm_ref, acc_u, acc_d):
        j, i = pl.program_id(0), pl.program_id(1)
        rows = pl.ds(pl.multiple_of(i * TM_B, TM_B), TM_B)
        mv, dh = m_ref[...], dh_ref[...]
        r = jnp.maximum(_dot(mv, wu_ref[0]), 0.0)
        act = (r * r).astype(BF16)
        dact = _dot(dh, wd_ref[0], NT)
        dup = (dact * (2.0 * r)).astype(BF16)
        pd = _dot(act, dh, TN)
        pu = _dot(mv, dup, TN)
        dmv = _dot(dup, wu_ref[0], NT)

        @pl.when(i == 0)
        def _():
            acc_u[...] = pu
            acc_d[...] = pd

        @pl.when(i > 0)
        def _():
            acc_u[...] += pu
            acc_d[...] += pd

        @pl.when(i == ni - 1)
        def _():
            dwu_ref[0] = acc_u[...].astype(BF16)
            dwd_ref[0] = acc_d[...].astype(BF16)

        @pl.when(j == 0)
        def _():
            dm_ref[rows, :] = dmv

        @pl.when(j > 0)
        def _():
            dm_ref[rows, :] += dmv

        @pl.when(j == NDEV - 1)
        def _():
            gsum = jnp.zeros((1, D), F32)
            for s in range(nsub):
                sub = slice(s * TM_E, (s + 1) * TM_E)
                dm_rows = dm_ref[pl.ds(pl.multiple_of(i * TM_B + s * TM_E, TM_E), TM_E), :]
                dx, gpart = _norm_bwd_rows(h1_ref[sub, :], g_ref[...], dm_rows, dres_ref[sub, :])
                dh1_ref[sub, :] = dx
                dh1b_ref[sub, :] = dx.astype(BF16)
                gsum = gsum + gpart

            @pl.when(i == 0)
            def _():
                dg_ref[...] = gsum

            @pl.when(i > 0)
            def _():
                dg_ref[...] += gsum

    blk = pl.BlockSpec((TM_B, D), lambda j, i: (i, 0))
    late = pl.BlockSpec((TM_B, D), lambda j, i: (jnp.where(j == NDEV - 1, i, 0), 0))
    vec = pl.BlockSpec((1, D), lambda j, i: (0, 0))
    wus = pl.BlockSpec((1, D, FF_B), lambda j, i: (j, 0, 0))
    wds = pl.BlockSpec((1, FF_B, D), lambda j, i: (j, 0, 0))
    return pl.pallas_call(
        body, name="mlp_bwd", grid=(NDEV, ni), in_specs=[blk, blk, wus, wds, late, vec, late],
        out_specs=(wus, wds, late, late, vec),
        out_shape=(_sds((NDEV, D, FF_B), BF16), _sds((NDEV, FF_B, D), BF16), _sds((T, D), F32), _sds((T, D), BF16),
                   _sds((1, D), F32)),
        scratch_shapes=[pltpu.VMEM((T, D), F32), pltpu.VMEM((D, FF_B), F32), pltpu.VMEM((FF_B, D), F32)],
        compiler_params=_cp(("arbitrary", "arbitrary"), 56))(m, dh2_b, wup_g, wdown_g, h1, g_mlp, dh2)


def _adamw(parts, w, m, v, name):
    rr, cc = w.shape
    tr = rr
    for cand in (256, 128, 64):
        if rr % cand == 0 and rr > cand:
            tr = cand
            break
    c1 = 1.0 - ADAM_B1 ** ADAM_STEP
    c2 = 1.0 - ADAM_B2 ** ADAM_STEP

    def body(p_ref, w_ref, m_ref, v_ref, g_ref, d_ref, nm_ref, nv_ref):
        g = p_ref[0].astype(F32)
        for s in range(1, NDEV):
            g = g + p_ref[s].astype(F32)
        mn = ADAM_B1 * m_ref[...] + (1.0 - ADAM_B1) * g
        vn = ADAM_B2 * v_ref[...] + (1.0 - ADAM_B2) * (g * g)
        g_ref[...] = g
        nm_ref[...] = mn
        nv_ref[...] = vn
        d_ref[...] = -ADAM_LR * ((mn / c1) / (jnp.sqrt(vn / c2) + ADAM_EPS) + ADAM_WD * w_ref[...])

    blk = pl.BlockSpec((tr, cc), lambda i: (i, 0))
    return pl.pallas_call(
        body, name=name, grid=(rr // tr,),
        in_specs=[pl.BlockSpec((NDEV, tr, cc), lambda i: (0, i, 0)), blk, blk, blk],
        out_specs=(blk,) * 4, out_shape=(_sds((rr, cc), F32),) * 4,
        compiler_params=_cp(("parallel",)))(parts, w, m, v)


RPB_N = NA_HEADS * 15 * 31
RPB_PAD = 4096
OWN_ROWS = NM + 8


def _pad_rows(a, rows):
    return jnp.pad(a, ((0, rows - a.shape[0]),) + ((0, 0),) * (a.ndim - 1))


def _pack_owned(meta_blk, lb_blk):
    return jnp.concatenate([meta_blk, _pad_rows(lb_blk.reshape(2, 128), 8)], axis=0)


LOSS_ROW = 28


def _pack_replicated(n_mix, n_mlp, n_final, hg_gain, rpb, loss_row=None):
    flat = _pad_rows(rpb.reshape(RPB_N), RPB_PAD)
    gain8 = _pad_rows(hg_gain.reshape(4, 128), 8)
    if loss_row is not None:
        gain8 = gain8 + jnp.pad(loss_row, ((LOSS_ROW - 24, 31 - LOSS_ROW), (0, 0)))
    return jnp.concatenate([n_mix.reshape(8, 128), n_mlp.reshape(8, 128), n_final.reshape(8, 128), gain8,
                            flat.reshape(32, 128)], axis=0)


def _unpack_replicated(a):
    return (a[0:8].reshape(1, D), a[8:16].reshape(1, D), a[16:24].reshape(D), a[24:28].reshape(1, 512),
            a[32:64].reshape(RPB_PAD)[:RPB_N].reshape(1, NA_HEADS, 15, 31))


def kernel(x, meta_tokens, w_in, w_na_out, w_hg_out, w_o, w_up, w_down, norm_mix, norm_mlp, norm_final, hg_norm, na_rpb, hg_lb_logits, loss_target, m_meta_tokens, m_w_in, m_w_na_out, m_w_hg_out, m_w_o, m_w_up, m_w_down, m_norm_mix, m_norm_mlp, m_norm_final, m_hg_norm, m_na_rpb, m_hg_lb_logits, v_meta_tokens, v_w_in, v_w_na_out, v_w_hg_out, v_w_o, v_w_up, v_w_down, v_norm_mix, v_norm_mlp, v_norm_final, v_hg_norm, v_na_rpb, v_hg_lb_logits):
    owned = _pack_owned(meta_tokens, hg_lb_logits)
    first, tok = _exchange_start([w_in[0].astype(BF16), owned], [False] * 2, "gather_first_start", SAME_CORE_AND_SIBLING)
    bias_tab = _na_bias_table(_tie(jnp.pad(na_rpb[0], ((0, 0), (0, 0), (0, 128 - 31))), tok, "tie_bias_table"))
    first = _exchange_wait(first, [False] * 2, [bias_tab], "gather_first_wait", SAME_CORE_AND_SIBLING)
    win_g, owned_g = _forward_to_sibling(first, "gather_first_forward")
    later = [w[0].astype(BF16) for w in (w_na_out, w_hg_out, w_o, w_up, w_down)]
    later[0] = _tie(later[0], owned_g, "tie_gather_rest")
    gather_rest, tok = _exchange_start(later, [False] * 5, "gather_rest_start")
    win_g = _tie(win_g, tok, "tie_inproj")
    meta_full = jnp.transpose(owned_g[:, 0:NM, :], (1, 0, 2)).reshape(NM, D)
    logits = jnp.transpose(owned_g[:, NM:NM + 2, :].reshape(NDEV, 2, 2, 64), (1, 2, 0, 3)).reshape(2, 2, 512)

    h0 = jnp.concatenate([meta_full, x[0], jnp.zeros((T - L, D), F32)], axis=0)
    tgt = jnp.concatenate([jnp.zeros((NM, D), F32), loss_target[0], jnp.zeros((T - L, D), F32)], axis=0)

    a, a_t = _norm_fwd_t(h0, norm_mix, "norm_mix_fwd")
    p_act = _inproj_fwd(a, win_g)
    o_na, lse = _na_fwd(p_act, bias_tab)
    qh, k_f, b_f, k_b, b_b = _hg_pre(p_act, logits)
    o_f, st_f = _hg_scan_fwd(qh, k_f, b_f, p_act, False)
    o_b, st_b = _hg_scan_fwd(qh, k_b, b_b, p_act, True)
    wna_g, whg_g, wo_g, wup_g, wdown_g = _exchange_wait(gather_rest, [False] * 5, [o_f, o_b, o_na], "gather_rest_wait")
    w_o_full = wo_g.reshape(D, D)
    w_na_full = jnp.transpose(wna_g, (1, 0, 2)).reshape(512, D)
    w_hg_full = jnp.transpose(whg_g, (1, 0, 2)).reshape(512, D)
    mix, u_hg = _mix_fwd(o_na, o_f, o_b, hg_norm, w_na_full, w_hg_full, p_act)
    h1, m_act = _wo_fwd(mix, w_o_full, h0, norm_mlp)
    loss_part, dh2, dh2_b, d_nfinal = _mlp_fwd_loss(m_act, wup_g, wdown_g, h1, norm_final.reshape(1, D), tgt)

    dwup_p, dwdown_p, dh1, dh1_b, d_nmlp = _mlp_bwd(m_act, dh2_b, wup_g, wdown_g, h1, norm_mlp, dh2)
    sc_mlp, tok = _exchange_start([dwup_p, dwdown_p], [True] * 2, "scatter_mlp_start")
    dmix, dwo = _wo_bwd(_tie(dh1_b, tok, "tie_wo_bwd"), w_o_full, mix)
    sc_wo, tok = _exchange_start([dwo.reshape(NDEV, D // NDEV, D)], [True], "scatter_wo_start")
    dgna, dghg, dwna, dwhg, do_na, do_hg, dg_hg, d_gain = _mix_bwd(
        o_na, u_hg, o_f, o_b, hg_norm, w_na_full, w_hg_full, p_act, _tie(dmix, tok, "tie_mix_bwd"))
    owner_cols = lambda w: jnp.transpose(w.reshape(512, NDEV, D // NDEV), (1, 0, 2))
    sc_br, tok = _exchange_start([owner_cols(dwna), owner_cols(dwhg)], [True] * 2, "scatter_branch_start")
    do_hg = _tie(do_hg, tok, "tie_hg_scan_bwd")
    dq_f, dk_f, db_f, dv_f = _hg_scan_bwd(qh, k_f, b_f, p_act, st_f, do_hg, False)
    dq_b, dk_b, db_b, dv_b = _hg_scan_bwd(qh, k_b, b_b, p_act, st_b, do_hg, True)
    dq_hg, dz_f, dz_b, di_hg, d_logits = _hg_pre_bwd(p_act, logits, dq_f, dq_b, dk_f, dk_b, db_f, db_b, dv_f, dv_b)
    dq_na, dk_na, dv_na, dbias = _na_bwd(p_act, do_na, lse, bias_tab)
    dp = jnp.concatenate([dq_na.astype(BF16), dk_na.astype(BF16), dv_na.astype(BF16), dq_hg, dz_f, dz_b, di_hg, dg_hg,
                          dgna, dghg], axis=1)
    dwin_p = _inproj_bwd_dw(a_t, dp)
    sc_in, tok = _exchange_start([dwin_p], [True], "scatter_in_start")
    dh0, d_nmix = _inproj_bwd_da(_tie(dp, tok, "tie_inproj_bwd_da"), win_g, h0, norm_mix, dh1)
    d_rpb = _na_rpb_reduce(_tie(dbias, tok, "tie_rpb_reduce"))[:, :, :31]

    res = {}

    def update(nm, parts, w, mm, vv):
        res[nm] = [r[None] for r in _adamw(parts, w[0], mm[0], vv[0], "adamw_" + nm)]
        return res[nm][1]

    wup_r, wdown_r = _exchange_wait(sc_mlp, [True] * 2, [dh0, d_rpb], "scatter_mlp_wait")
    update("w_up", wup_r, w_up, m_w_up, v_w_up)
    last = update("w_down", wdown_r, w_down, m_w_down, v_w_down)
    (wo_r,) = _exchange_wait(sc_wo, [True], [last], "scatter_wo_wait")
    last = update("w_o", wo_r, w_o, m_w_o, v_w_o)
    wna_r, whg_r = _exchange_wait(sc_br, [True] * 2, [last], "scatter_branch_wait")
    update("w_na_out", wna_r, w_na_out, m_w_na_out, v_w_na_out)
    last = update("w_hg_out", whg_r, w_hg_out, m_w_hg_out, v_w_hg_out)

    d_meta = jnp.transpose(dh0[0:NM].reshape(NM, NDEV, 128), (1, 0, 2))
    d_lg = jnp.transpose(d_logits.reshape(2, 2, NDEV, 64), (2, 0, 1, 3)).reshape(NDEV, 2, 128)
    owned_p = jnp.concatenate([d_meta, jnp.pad(d_lg, ((0, 0), (0, OWN_ROWS - NM - 2), (0, 0)))], axis=1)
    repl_p = _pack_replicated(d_nmix, d_nmlp, d_nfinal, d_gain, d_rpb, loss_part)
    owned_r, repl_r = _exchange([_tie(owned_p, last, "tie_scatter_small"), repl_p], [True, False], "scatter_small")
    own = _adamw(owned_r, owned, _pack_owned(m_meta_tokens, m_hg_lb_logits), _pack_owned(v_meta_tokens, v_hg_lb_logits),
                 "adamw_owned_small")
    res["meta_tokens"] = [r[0:NM] for r in own]
    res["hg_lb_logits"] = [r[NM:NM + 2].reshape(2, 2, 64) for r in own]
    rep = _adamw(repl_r, _pack_replicated(norm_mix, norm_mlp, norm_final, hg_norm, na_rpb),
                 _pack_replicated(m_norm_mix, m_norm_mlp, m_norm_final, m_hg_norm, m_na_rpb),
                 _pack_replicated(v_norm_mix, v_norm_mlp, v_norm_final, v_hg_norm, v_na_rpb), "adamw_replicated")
    for q in range(4):
        um = _unpack_replicated(rep[q])
        for nm, val in zip(("norm_mix", "norm_mlp", "norm_final", "hg_norm", "na_rpb"), um):
            res.setdefault(nm, [None] * 4)[q] = val
    (win_r,) = _exchange_wait(sc_in, [True], [rep[1], own[1]], "scatter_in_wait")
    update("w_in", win_r, w_in, m_w_in, v_w_in)

    loss = jnp.sum(repl_r[:, LOSS_ROW, 0])
    grad_x = dh0[NM:L][None]
    order = ("meta_tokens", "w_in", "w_na_out", "w_hg_out", "w_o", "w_up", "w_down", "norm_mix", "norm_mlp", "norm_final",
             "hg_norm", "na_rpb", "hg_lb_logits")
    outs = [loss, grad_x]
    for q in range(4):
        outs += [res[nm][q] for nm in order]
    return tuple(outs)
```

```python
import functools

import numpy as np
import jax
import jax.numpy as jnp
from jax import lax
from jax.experimental import pallas as pl
from jax.experimental.pallas import tpu as pltpu

F32 = jnp.float32
BF16 = jnp.bfloat16

D = 1024
SEQ = 2048
NM = 16
L = SEQ + NM
T = 2176
NDEV = 8
EPS = 1e-6
GRID_W = 64
ROWS = SEQ // GRID_W
NA_HEADS = 8
NA_DH = 64
NA_SCALE = NA_DH ** -0.5
HG_HEADS = 4
HG_C = 16
NCHUNK = L // HG_C
D_FF = 4096
IN_COLS = 6144
NEG = -1e30

ADAM_LR = 0.001
ADAM_B1 = 0.9
ADAM_B2 = 0.999
ADAM_EPS = 1e-08
ADAM_WD = 0.01
ADAM_STEP = 10

MESH_ID = pl.DeviceIdType.MESH
ANY = pl.BlockSpec(memory_space=pl.ANY)

NN = (((1,), (0,)), ((), ()))
NT = (((1,), (1,)), ((), ()))
TN = (((0,), (0,)), ((), ()))


def _cp(sem=None, vmem_mb=48):
    return pltpu.CompilerParams(dimension_semantics=sem, vmem_limit_bytes=vmem_mb * 1024 * 1024)


def _dot(a, b, dims=NN):
    return lax.dot_general(a, b, dims, preferred_element_type=F32)


def _sds(shape, dtype):
    return jax.ShapeDtypeStruct(shape, dtype)


HBM = pl.BlockSpec(memory_space=pltpu.HBM)
SEM = pl.BlockSpec(memory_space=pltpu.SEMAPHORE)
EFFECT = pltpu.SideEffectType.DATAFLOW_SIDE_EFFECTING


def _exchange(arrs, scatter, name, after=()):
    n = len(arrs)
    after = list(after)
    out_shapes = []
    for a, sc in zip(arrs, scatter):
        out_shapes.append(_sds(a.shape if sc else (NDEV,) + a.shape, a.dtype))

    def body(*refs):
        ins, outs = refs[:n], refs[n + len(after):2 * n + len(after)]
        send_sems, recv_sems, loc_sems = refs[2 * n + len(after):]
        me = 4 * lax.axis_index("x") + 2 * lax.axis_index("y") + lax.axis_index("c")
        copies = []
        for k in range(n):
            src_me = ins[k].at[me] if scatter[k] else ins[k]
            loc = pltpu.make_async_copy(src_me, outs[k].at[me], loc_sems.at[k])
            loc.start()
            copies.append(loc)
        remote = _peer_copies(ins, outs, scatter, send_sems, recv_sems)
        for cp in remote:
            cp.start()
        for cp in remote:
            cp.wait_recv()
        for cp in remote:
            cp.wait_send()
        for cp in copies:
            cp.wait()

    return pl.pallas_call(
        body, name=name, out_shape=tuple(out_shapes), in_specs=[ANY] * (n + len(after)), out_specs=tuple([ANY] * n),
        scratch_shapes=[pltpu.SemaphoreType.DMA((n * (NDEV - 1),)), pltpu.SemaphoreType.DMA((n * (NDEV - 1),)),
                        pltpu.SemaphoreType.DMA((n,))],
    )(*arrs, *after)


def _forward_to_sibling(bufs, name):
    n = len(bufs)

    def body(*refs):
        ins, outs = refs[:n], refs[n:2 * n]
        send_sems, recv_sems = refs[2 * n:]
        x, y, c = lax.axis_index("x"), lax.axis_index("y"), lax.axis_index("c")
        copies = []
        for k in range(n):
            for j, (cx, cy) in enumerate(((1 - x, y), (x, 1 - y), (1 - x, 1 - y))):
                slot = 4 * cx + 2 * cy + c
                copies.append(pltpu.make_async_remote_copy(
                    src_ref=ins[k].at[slot], dst_ref=outs[k].at[slot], send_sem=send_sems.at[3 * k + j],
                    recv_sem=recv_sems.at[3 * k + j], device_id=(x, y, 1 - c), device_id_type=MESH_ID))
        for cp in copies:
            cp.start()
        for cp in copies:
            cp.wait_recv()
        for cp in copies:
            cp.wait_send()

    return pl.pallas_call(
        body, name=name, out_shape=tuple(_sds(b.shape, b.dtype) for b in bufs), in_specs=[ANY] * n,
        out_specs=tuple([ANY] * n), input_output_aliases={k: k for k in range(n)},
        scratch_shapes=[pltpu.SemaphoreType.DMA((3 * n,)), pltpu.SemaphoreType.DMA((3 * n,))],
    )(*bufs)


ALL_PEERS = tuple(range(1, NDEV))
SAME_CORE_AND_SIBLING = (1, 2, 4, 6)


def _peer_copies(srcs, lands, scatter, send_sems, recv_sems, masks=ALL_PEERS):
    x, y, c = lax.axis_index("x"), lax.axis_index("y"), lax.axis_index("c")
    me = 4 * x + 2 * y + c
    out = []
    for k in range(len(srcs)):
        for m in masks:
            px, py, pc = x ^ (m >> 2), y ^ ((m >> 1) & 1), c ^ (m & 1)
            src = srcs[k].at[4 * px + 2 * py + pc] if scatter[k] else srcs[k]
            out.append(pltpu.make_async_remote_copy(
                src_ref=src, dst_ref=lands[k].at[me], send_sem=send_sems.at[k * (NDEV - 1) + m - 1],
                recv_sem=recv_sems.at[k * (NDEV - 1) + m - 1],
                device_id=(px, py, pc), device_id_type=MESH_ID))
    return out


def _exchange_start(arrs, scatter, name, masks=ALL_PEERS):
    n = len(arrs)
    me = 4 * lax.axis_index("x") + 2 * lax.axis_index("y") + lax.axis_index("c")
    lands = []
    for a, sc in zip(arrs, scatter):
        own = lax.dynamic_index_in_dim(a, me, 0, keepdims=True) if sc else a[None]
        shape = a.shape if sc else (NDEV,) + a.shape
        lands.append(lax.dynamic_update_index_in_dim(lax.empty(shape, a.dtype), own, me, 0))

    def body(*refs):
        srcs, lnds = refs[:n], refs[n:2 * n]
        send_sems, recv_sems = refs[2 * n], refs[2 * n + 1]
        token = refs[-1]
        for cp in _peer_copies(srcs, lnds, scatter, send_sems, recv_sems, masks):
            cp.start()
        token[...] = jnp.zeros_like(token)

    ops = [pltpu.with_memory_space_constraint(a, pltpu.HBM) for a in list(arrs) + lands]
    res = pl.pallas_call(
        body, name=name,
        out_shape=(pltpu.SemaphoreType.DMA((n * (NDEV - 1),)), pltpu.SemaphoreType.DMA((n * (NDEV - 1),)))
        + tuple(pltpu.HBM(o.shape, o.dtype) for o in ops) + (_sds((8, 128), F32),),
        in_specs=[HBM] * (2 * n), out_specs=(SEM, SEM) + (HBM,) * (2 * n) + (pl.BlockSpec(memory_space=pltpu.VMEM),),
        input_output_aliases={k: 2 + k for k in range(2 * n)},
        compiler_params=pltpu.CompilerParams(has_side_effects=EFFECT),
    )(*ops)
    return res[:-1], res[-1]


def _exchange_wait(handle, scatter, after, name, masks=ALL_PEERS):
    send_sems, recv_sems = handle[0], handle[1]
    bufs = handle[2:]
    n = len(bufs) // 2
    after = list(after)

    def body(*refs):
        srcs, lnds = refs[:n], refs[n:2 * n]
        for cp in _peer_copies(srcs, lnds, scatter, refs[2 * n], refs[2 * n + 1], masks):
            cp.wait_send()
            cp.wait_recv()

    res = pl.pallas_call(
        body, name=name, out_shape=tuple(pltpu.HBM(b.shape, b.dtype) for b in bufs),
        in_specs=[HBM] * (2 * n) + [SEM, SEM] + [ANY] * len(after), out_specs=(HBM,) * (2 * n),
        input_output_aliases={k: k for k in range(2 * n)},
        compiler_params=pltpu.CompilerParams(has_side_effects=EFFECT),
    )(*bufs, send_sems, recv_sems, *after)
    return res[n:]


def _tie(x, token, name):
    def body(x_ref, t_ref, o_ref):
        del x_ref, t_ref, o_ref

    return pl.pallas_call(body, name=name, out_shape=_sds(x.shape, x.dtype), in_specs=[ANY, ANY], out_specs=ANY,
                          input_output_aliases={0: 0})(x, token)


TM_E = 272


def _norm_fwd_t(h, g, name):
    def body(h_ref, g_ref, o_ref, ot_ref):
        xv = h_ref[...]
        r = lax.rsqrt(jnp.mean(xv * xv, axis=-1, keepdims=True) + EPS)
        y = xv * r * g_ref[...]
        o_ref[...] = y.astype(BF16)
        ot_ref[...] = y.T.astype(BF16)

    return pl.pallas_call(
        body, name=name, grid=(T // 128,),
        in_specs=[pl.BlockSpec((128, D), lambda i: (i, 0)), pl.BlockSpec((1, D), lambda i: (0, 0))],
        out_specs=(pl.BlockSpec((128, D), lambda i: (i, 0)), pl.BlockSpec((D, 128), lambda i: (0, i))),
        out_shape=(_sds((T, D), BF16), _sds((D, T), BF16)), compiler_params=_cp(("parallel",)))(h, g)


def _norm_bwd_rows(xv, gv, dnv, dres):
    r = lax.rsqrt(jnp.mean(xv * xv, axis=-1, keepdims=True) + EPS)
    xh = xv * r
    dxh = dnv * gv
    dx = dres + r * (dxh - xh * jnp.mean(dxh * xh, axis=-1, keepdims=True))
    return dx, jnp.sum(dnv * xh, axis=0, keepdims=True)


TM_MM = 1088


def _inproj_fwd(a, w_g):
    nb = w_g.shape[2]

    def body(a_ref, w_ref, o_ref):
        o_ref[...] = _dot(a_ref[...], w_ref[0])

    return pl.pallas_call(
        body, name="inproj_fwd", grid=(T // TM_MM, NDEV),
        in_specs=[pl.BlockSpec((TM_MM, D), lambda i, j: (i, 0)), pl.BlockSpec((1, D, nb), lambda i, j: (j, 0, 0))],
        out_specs=pl.BlockSpec((TM_MM, nb), lambda i, j: (i, j)), out_shape=_sds((T, NDEV * nb), F32),
        compiler_params=_cp(("parallel", "parallel")))(a, w_g)


TM_B = 544


W_IN_B = IN_COLS // NDEV


def _inproj_bwd_dw(a_t, dp):
    def body(at_ref, dp_ref, dw_ref):
        dw_ref[0] = _dot(at_ref[...], dp_ref[...]).astype(BF16)

    return pl.pallas_call(
        body, name="inproj_bwd_dw", grid=(NDEV,),
        in_specs=[pl.BlockSpec((D, T), lambda j: (0, 0)), pl.BlockSpec((T, W_IN_B), lambda j: (0, j))],
        out_specs=pl.BlockSpec((1, D, W_IN_B), lambda j: (j, 0, 0)), out_shape=_sds((NDEV, D, W_IN_B), BF16),
        compiler_params=_cp(("parallel",)))(a_t, dp)


def _inproj_bwd_da(dp, w_g, h0, g_mix, dh1):
    nsub = TM_MM // TM_E

    def body(dp_ref, w_ref, h0_ref, g_ref, dres_ref, dh0_ref, dg_ref, da):
        i, j = pl.program_id(0), pl.program_id(1)
        dav = _dot(dp_ref[...], w_ref[0], NT)

        @pl.when(j == 0)
        def _():
            da[...] = dav

        @pl.when(j > 0)
        def _():
            da[...] += dav

        @pl.when(j == NDEV - 1)
        def _():
            gsum = jnp.zeros((1, D), F32)
            for s in range(nsub):
                sub = slice(s * TM_E, (s + 1) * TM_E)
                dx, gpart = _norm_bwd_rows(h0_ref[sub, :], g_ref[...], da[sub, :], dres_ref[sub, :])
                dh0_ref[sub, :] = dx
                gsum = gsum + gpart

            @pl.when(i == 0)
            def _():
                dg_ref[...] = gsum

            @pl.when(i > 0)
            def _():
                dg_ref[...] += gsum

    rblk = pl.BlockSpec((TM_MM, D), lambda i, j: (i, 0))
    vec = pl.BlockSpec((1, D), lambda i, j: (0, 0))
    return pl.pallas_call(
        body, name="inproj_bwd_da", grid=(T // TM_MM, NDEV),
        in_specs=[pl.BlockSpec((TM_MM, W_IN_B), lambda i, j: (i, j)), pl.BlockSpec((1, D, W_IN_B), lambda i, j: (j, 0, 0)),
                  rblk, vec, rblk],
        out_specs=(rblk, vec), out_shape=(_sds((T, D), F32), _sds((1, D), F32)),
        scratch_shapes=[pltpu.VMEM((TM_MM, D), F32)],
        compiler_params=_cp(("arbitrary", "arbitrary"), 56))(dp, w_g, h0, g_mix, dh1)


NA_QB = 256
NA_GROUPS = ROWS // 4
NA_UROWS = 11
NA_KW = NA_UROWS * GRID_W
NA_KU = 768


def _na_row_offset(var, i, j):
    valid = (j < 8, i <= j < i + 8, 3 <= j < NA_UROWS)[var]
    return (j - i + (7, 3, 0)[var]) if valid else None


def _na_bias_table(rp):
    def body(r_ref, o_ref):
        row3 = lax.broadcasted_iota(jnp.int32, (15, GRID_W, 128), 1)
        lane3 = lax.broadcasted_iota(jnp.int32, (15, GRID_W, 128), 2)
        w3 = lane3 & (GRID_W - 1)
        cs3 = jnp.clip(row3 - 8, 0, GRID_W - 16)
        lane = lax.broadcasted_iota(jnp.int32, (GRID_W, 128), 1)
        neg = jnp.full((GRID_W, 128), NEG, F32)
        z = jnp.stack([jnp.broadcast_to(r_ref[0, a:a + 1, :], (GRID_W, 128)) for a in range(15)])
        for bit in range(6):
            sh = 1 << bit
            z = jnp.where((row3 & sh) != 0, jnp.roll(z, sh, axis=2), z)
        z = jnp.roll(z, 128 - 15, axis=2)
        z = jnp.where(lane3 < GRID_W, z, 0.0)
        z = z + jnp.roll(z, GRID_W, axis=2)
        tabs = jnp.where((w3 >= cs3) & (w3 < cs3 + 16), z, NEG)
        tail = jnp.where(lane < GRID_W + NM, 0.0, NEG)
        for var in range(3):
            for i in range(4):
                for jp in range(NA_KU // 128):
                    halves = []
                    for j in (2 * jp, 2 * jp + 1):
                        a = _na_row_offset(var, i, j) if j < NA_UROWS else None
                        halves.append(tail if j >= NA_UROWS else (neg if a is None else tabs[a]))
                    o_ref[var, 0, i * 64:(i + 1) * 64, jp * 128:(jp + 1) * 128] = jnp.where(lane < GRID_W, halves[0], halves[1])

    return pl.pallas_call(
        body, name="na_bias_table", grid=(NA_HEADS,),
        in_specs=[pl.BlockSpec((1, 15, 128), lambda h: (h, 0, 0))],
        out_specs=pl.BlockSpec((3, 1, NA_QB, NA_KU), lambda h: (0, h, 0, 0)),
        out_shape=_sds((3, NA_HEADS, NA_QB, NA_KU), F32), compiler_params=_cp(("parallel",)))(rp)


def _na_var(g):
    return jnp.where(g == 0, 0, jnp.where(g == NA_GROUPS - 1, 2, 1))


def _na_load_window(src_ref, dst, g):
    us = jnp.clip(4 * g - 4, 0, ROWS - NA_UROWS)
    kstart = pl.multiple_of(NM + GRID_W * us, 16)
    dst[0:NA_KW, :] = src_ref[pl.ds(kstart, NA_KW), :].astype(BF16)
    dst[NA_KW:NA_KW + NM, :] = src_ref[0:NM, :].astype(BF16)
    dst[NA_KW + NM:, :] = jnp.zeros((NA_KU - NA_KW - NM, 128), BF16)
    return kstart


def _na_fwd(p_act, bias_tab):
    def body(q_ref, k_ref, v_ref, b_ref, o_ref, lse_ref, ku, vu):
        g = pl.program_id(1)
        _na_load_window(k_ref, ku, g)
        _na_load_window(v_ref, vu, g)
        qstart = pl.multiple_of(NM + NA_QB * g, 16)
        q = q_ref[pl.ds(qstart, NA_QB), :]
        lane = lax.broadcasted_iota(jnp.int32, (NA_QB, 128), 1)
        o_h, lse_h = [], []
        for h in range(2):
            hm = (lane < 64) if h == 0 else (lane >= 64)
            qm = jnp.where(hm, q, 0.0).astype(BF16)
            s = _dot(qm, ku[...], NT) * NA_SCALE + b_ref[0, h]
            m = jnp.max(s, axis=-1, keepdims=True)
            p = jnp.exp(s - m)
            l = jnp.sum(p, axis=-1, keepdims=True)
            o_h.append(_dot(p.astype(BF16), vu[...]) / l)
            lse_h.append(jnp.broadcast_to(m + jnp.log(l), (NA_QB, 128)))
        o_ref[pl.ds(qstart, NA_QB), :] = jnp.where(lane < 64, o_h[0], o_h[1]).astype(BF16)
        lse_ref[0, pl.ds(qstart, NA_QB), :] = jnp.where(lane < 64, lse_h[0], lse_h[1])

        @pl.when(g == 0)
        def _():
            qm_ = q_ref[0:NM, :]
            lane_m = lax.broadcasted_iota(jnp.int32, (NM, 128), 1)
            km, vm = ku[NA_KW:NA_KW + NM, :], vu[NA_KW:NA_KW + NM, :]
            om = []
            for h in range(2):
                hm = (lane_m < 64) if h == 0 else (lane_m >= 64)
                s = _dot(jnp.where(hm, qm_, 0.0).astype(BF16), km, NT) * NA_SCALE
                p = jnp.exp(s - jnp.max(s, axis=-1, keepdims=True))
                l = jnp.sum(p, axis=-1, keepdims=True)
                om.append(_dot(p.astype(BF16), vm) / l)
            o_ref[0:NM, :] = jnp.where(lane_m < 64, om[0], om[1]).astype(BF16)
            o_ref[L:T, :] = jnp.zeros((T - L, 128), BF16)
            lse_ref[0, 0:NM, :] = jnp.zeros((NM, 128), F32)
            lse_ref[0, L:T, :] = jnp.zeros((T - L, 128), F32)

    col = lambda off: pl.BlockSpec((T, 128), lambda hp, g: (0, off + hp))
    return pl.pallas_call(
        body, name="na_fwd", grid=(4, NA_GROUPS),
        in_specs=[col(0), col(4), col(8),
                  pl.BlockSpec((1, 2, NA_QB, NA_KU), lambda hp, g: (_na_var(g), hp, 0, 0))],
        out_specs=(pl.BlockSpec((T, 128), lambda hp, g: (0, hp)), pl.BlockSpec((1, T, 128), lambda hp, g: (hp, 0, 0))),
        out_shape=(_sds((T, 512), BF16), _sds((4, T, 128), F32)),
        scratch_shapes=[pltpu.VMEM((NA_KU, 128), BF16), pltpu.VMEM((NA_KU, 128), BF16)],
        compiler_params=_cp(("parallel", "arbitrary")))(p_act, p_act, p_act, bias_tab)


def _na_bwd(p_act, do, lse, bias_tab):
    def body(q_ref, k_ref, v_ref, do_ref, lse_ref, b_ref, dq_ref, dk_ref, dv_ref, db_ref, ku, vu):
        g = pl.program_id(1)

        @pl.when(g == 0)
        def _():
            dq_ref[...] = jnp.zeros((T, 128), F32)
            dk_ref[...] = jnp.zeros((T, 128), F32)
            dv_ref[...] = jnp.zeros((T, 128), F32)

        kstart = _na_load_window(k_ref, ku, g)
        _na_load_window(v_ref, vu, g)
        qstart = pl.multiple_of(NM + NA_QB * g, 16)
        q = q_ref[pl.ds(qstart, NA_QB), :]
        dov = do_ref[pl.ds(qstart, NA_QB), :]
        lsev = lse_ref[0, pl.ds(qstart, NA_QB), :]
        lane = lax.broadcasted_iota(jnp.int32, (NA_QB, 128), 1)
        first = (g == 0) | (g == 1) | (g == NA_GROUPS - 1)
        dq_h = []
        dku = jnp.zeros((NA_KU, 128), F32)
        dvu = jnp.zeros((NA_KU, 128), F32)
        for h in range(2):
            hm = (lane < 64) if h == 0 else (lane >= 64)
            qm = jnp.where(hm, q, 0.0).astype(BF16)
            dom = jnp.where(hm, dov, 0.0).astype(BF16)
            s = _dot(qm, ku[...], NT) * NA_SCALE + b_ref[0, h]
            p = jnp.exp(s - lsev[:, 64 * h:64 * h + 1])
            dp = _dot(dom, vu[...], NT)
            delta = jnp.sum(p * dp, axis=-1, keepdims=True)
            ds = p * (dp - delta)

            @pl.when(first)
            def _():
                db_ref[0, h] = ds

            @pl.when(jnp.logical_not(first))
            def _():
                db_ref[0, h] += ds

            dsb = (ds * NA_SCALE).astype(BF16)
            dq_h.append(_dot(dsb, ku[...]))
            dku = dku + _dot(dsb, qm, TN)
            dvu = dvu + _dot(p.astype(BF16), dom, TN)
        dq_ref[pl.ds(qstart, NA_QB), :] = jnp.where(lane < 64, dq_h[0], dq_h[1])
        dk_ref[pl.ds(kstart, NA_KW), :] += dku[0:NA_KW]
        dv_ref[pl.ds(kstart, NA_KW), :] += dvu[0:NA_KW]
        dk_ref[0:NM, :] += dku[NA_KW:NA_KW + NM]
        dv_ref[0:NM, :] += dvu[NA_KW:NA_KW + NM]

        @pl.when(g == 0)
        def _():
            qm_ = q_ref[0:NM, :]
            dom_ = do_ref[0:NM, :]
            lane_m = lax.broadcasted_iota(jnp.int32, (NM, 128), 1)
            km, vm = ku[NA_KW:NA_KW + NM, :], vu[NA_KW:NA_KW + NM, :]
            dqs = []
            dkm = jnp.zeros((NM, 128), F32)
            dvm = jnp.zeros((NM, 128), F32)
            for h in range(2):
                hm = (lane_m < 64) if h == 0 else (lane_m >= 64)
                qh = jnp.where(hm, qm_, 0.0).astype(BF16)
                doh = jnp.where(hm, dom_, 0.0).astype(BF16)
                s = _dot(qh, km, NT) * NA_SCALE
                e = jnp.exp(s - jnp.max(s, axis=-1, keepdims=True))
                p = e / jnp.sum(e, axis=-1, keepdims=True)
                dp = _dot(doh, vm, NT)
                ds = p * (dp - jnp.sum(p * dp, axis=-1, keepdims=True))
                dsb = (ds * NA_SCALE).astype(BF16)
                dqs.append(_dot(dsb, km))
                dkm = dkm + _dot(dsb, qh, TN)
                dvm = dvm + _dot(p.astype(BF16), doh, TN)
            dq_ref[0:NM, :] = jnp.where(lane_m < 64, dqs[0], dqs[1])
            dk_ref[0:NM, :] += dkm
            dv_ref[0:NM, :] += dvm

    col = lambda off: pl.BlockSpec((T, 128), lambda hp, g: (0, off + hp))
    ocol = pl.BlockSpec((T, 128), lambda hp, g: (0, hp))
    bspec = pl.BlockSpec((1, 2, NA_QB, NA_KU), lambda hp, g: (_na_var(g), hp, 0, 0))
    return pl.pallas_call(
        body, name="na_bwd", grid=(4, NA_GROUPS),
        in_specs=[col(0), col(4), col(8), ocol, pl.BlockSpec((1, T, 128), lambda hp, g: (hp, 0, 0)), bspec],
        out_specs=(ocol, ocol, ocol, bspec),
        out_shape=(_sds((T, 512), F32), _sds((T, 512), F32), _sds((T, 512), F32), _sds((3, NA_HEADS, NA_QB, NA_KU), F32)),
        scratch_shapes=[pltpu.VMEM((NA_KU, 128), BF16), pltpu.VMEM((NA_KU, 128), BF16)],
        compiler_params=_cp(("parallel", "arbitrary")))(p_act, p_act, p_act, do, lse, bias_tab)


def _na_rpb_reduce(dbias):
    def body(db_ref, o_ref):
        lane = lax.broadcasted_iota(jnp.int32, (GRID_W, 128), 1)
        row3 = lax.broadcasted_iota(jnp.int32, (15, GRID_W, 128), 1)
        lane3 = lax.broadcasted_iota(jnp.int32, (15, GRID_W, 128), 2)
        accs = []
        for a in range(15):
            acc = jnp.zeros((GRID_W, 128), F32)
            for var in range(3):
                for i in range(4):
                    for j in range(NA_UROWS):
                        if _na_row_offset(var, i, j) == a:
                            pair = db_ref[var, 0, i * 64:(i + 1) * 64, (j // 2) * 128:(j // 2 + 1) * 128]
                            acc = acc + jnp.where((lane < GRID_W) if j % 2 == 0 else (lane >= GRID_W), pair, 0.0)
            accs.append(acc)
        z = jnp.stack(accs)
        z = jnp.where(lane3 < GRID_W, z + jnp.roll(z, GRID_W, axis=2), 0.0)
        for bit in range(6):
            sh = 1 << bit
            z = jnp.where((row3 & sh) != 0, jnp.roll(z, 128 - sh, axis=2), z)
        z = jnp.roll(z, 15, axis=2)
        o_ref[0] = jnp.sum(z, axis=1)

    return pl.pallas_call(
        body, name="na_rpb_reduce", grid=(NA_HEADS,),
        in_specs=[pl.BlockSpec((3, 1, NA_QB, NA_KU), lambda h: (0, h, 0, 0))],
        out_specs=pl.BlockSpec((1, 15, 128), lambda h: (h, 0, 0)), out_shape=_sds((NA_HEADS, 15, 128), F32),
        compiler_params=_cp(("parallel",)))(dbias)


HG_RB = 128
HG_NB = T // HG_RB
HG_SLOTS = HG_NB * 8
HI = lax.Precision.HIGHEST
HG_UNROLL = 4


def _chunk_tri(lower):
    r = lax.broadcasted_iota(jnp.int32, (HG_RB, HG_RB), 0)
    c = lax.broadcasted_iota(jnp.int32, (HG_RB, HG_RB), 1)
    same = (r // HG_C) == (c // HG_C)
    keep = (c <= r) if lower else (c >= r)
    return jnp.where(same & keep, 1.0, 0.0).astype(F32)


def _hg_gate_terms(z, lg):
    dl = lg[0:1, :] - lg[1:2, :]
    log_lb = jax.nn.log_sigmoid(dl)
    log_1mlb = jax.nn.log_sigmoid(-dl)
    yz = log_1mlb + jax.nn.log_sigmoid(z)
    log_f = jnp.logaddexp(log_lb, yz)
    snz = jax.nn.sigmoid(-z)
    k = jnp.exp(log_1mlb) * snz
    w2 = jnp.exp(yz - log_f)
    return log_f, k, snz, w2


def _hg_pre(p_act, logits):
    def body(q_ref, zf_ref, zb_ref, lg_ref, qh_ref, kf_ref, bf_ref, kb_ref, bb_ref):
        qh_ref[...] = jax.nn.silu(q_ref[...])
        lf, kf, _, _ = _hg_gate_terms(zf_ref[...], lg_ref[0])
        kf_ref[...] = kf
        bf_ref[...] = jnp.dot(_chunk_tri(True), lf, precision=HI, preferred_element_type=F32)
        lb_, kb, _, _ = _hg_gate_terms(zb_ref[...], lg_ref[1])
        kb_ref[...] = kb
        bb_ref[...] = jnp.dot(_chunk_tri(False), lb_, precision=HI, preferred_element_type=F32)

    blk = lambda c: pl.BlockSpec((HG_RB, 512), lambda i: (i, c))
    ob = pl.BlockSpec((HG_RB, 512), lambda i: (i, 0))
    return pl.pallas_call(
        body, name="hg_pre", grid=(HG_NB,),
        in_specs=[blk(3), blk(4), blk(5), pl.BlockSpec((2, 2, 512), lambda i: (0, 0, 0))],
        out_specs=(ob,) * 5, out_shape=(_sds((T, 512), F32),) * 5,
        compiler_params=_cp(("parallel",)))(p_act, p_act, p_act, logits)


def _bdot(a, b, ca, cb):
    return lax.dot_general(a.astype(BF16), b.astype(BF16), (((ca,), (cb,)), ((0,), (0,))), preferred_element_type=F32)


HG_S = 8
HG_NS = HG_RB // HG_S


def _lane_sums(xs):
    l_io = lax.broadcasted_iota(jnp.int32, (HG_NS, HG_S, HG_S), 2)
    a = jnp.zeros((HG_NS, HG_S, HG_S), F32)
    for j, x in enumerate(xs):
        a = a + jnp.where(l_io == j, jnp.sum(x, axis=-1, keepdims=True), 0.0)
    return a


def _halves(x):
    y = x.reshape(8, 2, HG_S, x.shape[-1])
    return y[:, 0], y[:, 1]


def _join(first, second):
    return jnp.stack([first, second], axis=1).reshape(HG_RB, first.shape[-1])


def _cross_split(rev, b4):
    b_1, b_2 = _halves(b4)
    if rev:
        r = b_2[:, 0:1, :]
        return jnp.exp(b_1 - r), jnp.exp(r - b_2)
    r = b_1[:, HG_S - 1:HG_S, :]
    return jnp.exp(b_2 - r), jnp.exp(r - b_1)


def _hg_scan_fwd(qh, k, b, p_act, rev):
    anchor = 0 if rev else HG_C - 1

    def body(q_ref, k_ref, b_ref, v_ref, o_ref, st_ref, dsc):
        def phase_a(blk, _):
            rows = pl.ds(pl.multiple_of(blk * HG_RB, HG_RB), HG_RB)
            b3 = b_ref[rows, :].reshape(8, HG_C, 128)
            k3 = k_ref[rows, :].reshape(8, HG_C, 128)
            v3 = v_ref[rows, :].reshape(8, HG_C, 128)
            bl = b3[:, anchor:anchor + 1, :]
            kt = k3 * jnp.exp(bl - b3)
            st_ref[0, pl.ds(pl.multiple_of(blk * 8, 8), 8)] = _bdot(v3, kt, 1, 1)
            dsc[pl.ds(pl.multiple_of(blk * 8, 8), 8), :] = jnp.exp(bl[:, 0, :])
            return 0

        lax.fori_loop(0, HG_NB, phase_a, 0, unroll=HG_UNROLL)

        def phase_b(n, carry):
            c = (NCHUNK - 1 - n) if rev else n
            u = st_ref[0, c]
            st_ref[0, c] = carry
            return carry * dsc[pl.ds(c, 1), :] + u

        lax.fori_loop(0, NCHUNK // 3, lambda n3, s: phase_b(3 * n3 + 2, phase_b(3 * n3 + 1, phase_b(3 * n3, s))),
                      jnp.zeros((128, 128), F32))
        for c in range(NCHUNK, HG_SLOTS):
            st_ref[0, c] = jnp.zeros((128, 128), F32)

        t_io = lax.broadcasted_iota(jnp.int32, (HG_NS, HG_S, 128), 1)

        def phase_c(blk, _):
            rows = pl.ds(pl.multiple_of(blk * HG_RB, HG_RB), HG_RB)
            b4 = b_ref[rows, :].reshape(HG_NS, HG_S, 128)
            k4 = k_ref[rows, :].reshape(HG_NS, HG_S, 128)
            q4 = q_ref[rows, :].reshape(HG_NS, HG_S, 128)
            v4 = v_ref[rows, :].reshape(HG_NS, HG_S, 128)
            st = st_ref[0, pl.ds(pl.multiple_of(blk * 8, 8), 8)]
            o = _bdot((q4 * jnp.exp(b4)).reshape(8, HG_C, 128), st, 2, 2).reshape(HG_RB, 128)
            terms = []
            for s in range(HG_S):
                ok = (t_io <= s) if rev else (t_io >= s)
                f = jnp.exp(jnp.where(ok, b4 - b4[:, s:s + 1, :], NEG))
                terms.append(q4 * f * k4[:, s:s + 1, :])
            o_in = _bdot(_lane_sums(terms), v4, 2, 1)
            wq, wk = _cross_split(rev, b4)
            q_1, q_2 = _halves(q4)
            k_1, k_2 = _halves(k4)
            v_1, v_2 = _halves(v4)
            o_1, o_2 = _halves(o_in)
            if rev:
                o_1 = o_1 + _bdot(_bdot(q_1 * wq, k_2 * wk, 2, 2), v_2, 2, 1)
            else:
                o_2 = o_2 + _bdot(_bdot(q_2 * wq, k_1 * wk, 2, 2), v_1, 2, 1)
            o_ref[rows, :] = o + _join(o_1, o_2)
            return 0

        lax.fori_loop(0, HG_NB, phase_c, 0, unroll=HG_UNROLL)

    col = pl.BlockSpec((T, 128), lambda h: (0, h))
    return pl.pallas_call(
        body, name="hg_scan_bwd_dir" if rev else "hg_scan_fwd_dir", grid=(HG_HEADS,),
        in_specs=[col, col, col, pl.BlockSpec((T, 128), lambda h: (0, 24 + h))],
        out_specs=(col, pl.BlockSpec((1, HG_SLOTS, 128, 128), lambda h: (h, 0, 0, 0))),
        out_shape=(_sds((T, 512), F32), _sds((HG_HEADS, HG_SLOTS, 128, 128), F32)),
        scratch_shapes=[pltpu.VMEM((HG_SLOTS, 128), F32)],
        compiler_params=_cp(("parallel",), 56))(qh, k, b, p_act)


def _hg_scan_bwd(qh, k, b, p_act, st, do, rev):
    anchor = 0 if rev else HG_C - 1

    def body(q_ref, k_ref, b_ref, v_ref, st_ref, do_ref, dq_ref, dk_ref, db_ref, dv_ref, gst, dsc, dbl):
        def phase_a(blk, _):
            rows = pl.ds(pl.multiple_of(blk * HG_RB, HG_RB), HG_RB)
            b3 = b_ref[rows, :].reshape(8, HG_C, 128)
            q3 = q_ref[rows, :].reshape(8, HG_C, 128)
            do3 = do_ref[rows, :].reshape(8, HG_C, 128)
            gst[pl.ds(pl.multiple_of(blk * 8, 8), 8)] = _bdot(do3, q3 * jnp.exp(b3), 1, 1)
            dsc[pl.ds(pl.multiple_of(blk * 8, 8), 8), :] = jnp.exp(b3[:, anchor, :])
            return 0

        lax.fori_loop(0, HG_NB, phase_a, 0, unroll=HG_UNROLL)

        def phase_b(n, carry):
            c = n if rev else (NCHUNK - 1 - n)
            w = gst[c]
            gst[c] = carry
            dcv = dsc[pl.ds(c, 1), :]
            dbl[pl.ds(c, 1), :] = dcv * jnp.sum(st_ref[0, c] * carry, axis=0, keepdims=True)
            return carry * dcv + w

        lax.fori_loop(0, NCHUNK // 3, lambda n3, s: phase_b(3 * n3 + 2, phase_b(3 * n3 + 1, phase_b(3 * n3, s))),
                      jnp.zeros((128, 128), F32))
        for c in range(NCHUNK, HG_SLOTS):
            gst[c] = jnp.zeros((128, 128), F32)
            dbl[c:c + 1, :] = jnp.zeros((1, 128), F32)

        t_io = lax.broadcasted_iota(jnp.int32, (HG_NS, HG_S, 128), 1)
        t16 = lax.broadcasted_iota(jnp.int32, (8, HG_C, 128), 1)
        r_io = lax.broadcasted_iota(jnp.int32, (HG_NS, HG_S, HG_S), 1)
        l_io = lax.broadcasted_iota(jnp.int32, (HG_NS, HG_S, HG_S), 2)

        def phase_c(blk, _):
            rows = pl.ds(pl.multiple_of(blk * HG_RB, HG_RB), HG_RB)
            cs = pl.ds(pl.multiple_of(blk * 8, 8), 8)
            b4 = b_ref[rows, :].reshape(HG_NS, HG_S, 128)
            k4 = k_ref[rows, :].reshape(HG_NS, HG_S, 128)
            q4 = q_ref[rows, :].reshape(HG_NS, HG_S, 128)
            v4 = v_ref[rows, :].reshape(HG_NS, HG_S, 128)
            do4 = do_ref[rows, :].reshape(HG_NS, HG_S, 128)
            b3, k3, q3 = (z.reshape(8, HG_C, 128) for z in (b4, k4, q4))
            v3, do3 = v4.reshape(8, HG_C, 128), do4.reshape(8, HG_C, 128)
            s_t = st_ref[0, cs]
            g_t = gst[cs]
            bl = b3[:, anchor:anchor + 1, :]
            ekl = jnp.exp(bl - b3)
            kt = k3 * ekl
            dkt = _bdot(v3, g_t, 2, 1)
            dq = (_bdot(do3, s_t, 2, 1) * jnp.exp(b3)).reshape(HG_NS, HG_S, 128)
            dk = (dkt * ekl).reshape(HG_NS, HG_S, 128)
            dv = _bdot(kt, g_t, 2, 2).reshape(HG_NS, HG_S, 128)
            dbl3 = dbl[cs, :].reshape(8, 1, 128) + jnp.sum(dkt * kt, axis=1, keepdims=True)
            causal = (l_io >= r_io) if rev else (l_io <= r_io)
            da = jnp.where(causal, _bdot(do4, v4, 2, 2), 0.0)
            causal_t = (l_io <= r_io) if rev else (l_io >= r_io)
            dat = jnp.where(causal_t, _bdot(v4, do4, 2, 2), 0.0)
            for s in range(HG_S):
                ok = (t_io <= s) if rev else (t_io >= s)
                f = jnp.exp(jnp.where(ok, b4 - b4[:, s:s + 1, :], NEG))
                dq = dq + da[:, :, s:s + 1] * (f * k4[:, s:s + 1, :])
            terms = []
            for t in range(HG_S):
                ok = (t_io >= t) if rev else (t_io <= t)
                e = jnp.exp(jnp.where(ok, b4[:, t:t + 1, :] - b4, NEG))
                eq = e * q4[:, t:t + 1, :]
                dk = dk + dat[:, :, t:t + 1] * eq
                terms.append(eq * k4)
            dv = dv + _bdot(_lane_sums(terms), do4, 2, 1)
            wq, wk = _cross_split(rev, b4)
            pick = (lambda z: _halves(z)) if rev else (lambda z: _halves(z)[::-1])
            (q_q, _), (_, k_k), (_, v_k), (do_q, _) = pick(q4), pick(k4), pick(v4), pick(do4)
            qx, kx = q_q * wq, k_k * wk
            dq_q = _bdot(_bdot(do_q, v_k, 2, 2), kx, 2, 1) * wq
            dk_k = _bdot(_bdot(v_k, do_q, 2, 2), qx, 2, 1) * wk
            dv_k = _bdot(_bdot(kx, qx, 2, 2), do_q, 2, 1)
            zero = jnp.zeros((8, HG_S, 128), F32)
            place_q = (lambda z: _join(z, zero)) if rev else (lambda z: _join(zero, z))
            place_k = (lambda z: _join(zero, z)) if rev else (lambda z: _join(z, zero))
            dq2 = dq.reshape(HG_RB, 128) + place_q(dq_q)
            dk2 = dk.reshape(HG_RB, 128) + place_k(dk_k)
            dv2 = dv.reshape(HG_RB, 128) + place_k(dv_k)
            dq3, dk3 = dq2.reshape(8, HG_C, 128), dk2.reshape(8, HG_C, 128)
            db = q3 * dq3 - k3 * dk3 + jnp.where(t16 == anchor, dbl3, 0.0)
            dq_ref[rows, :] = dq2
            dk_ref[rows, :] = dk2
            db_ref[rows, :] = db.reshape(HG_RB, 128)
            dv_ref[rows, :] = dv2
            return 0

        lax.fori_loop(0, HG_NB, phase_c, 0, unroll=HG_UNROLL)

    col = pl.BlockSpec((T, 128), lambda h: (0, h))
    return pl.pallas_call(
        body, name="hg_scan_bwd_dir_bwd" if rev else "hg_scan_fwd_dir_bwd", grid=(HG_HEADS,),
        in_specs=[col, col, col, pl.BlockSpec((T, 128), lambda h: (0, 24 + h)),
                  pl.BlockSpec((1, HG_SLOTS, 128, 128), lambda h: (h, 0, 0, 0)), col],
        out_specs=(col,) * 4, out_shape=(_sds((T, 512), F32),) * 4,
        scratch_shapes=[pltpu.VMEM((HG_SLOTS, 128, 128), F32), pltpu.VMEM((HG_SLOTS, 128), F32),
                        pltpu.VMEM((HG_SLOTS, 128), F32)],
        compiler_params=_cp(("parallel",), 56))(qh, k, b, p_act, st, do)


def _row_valid(i, tm):
    r = lax.broadcasted_iota(jnp.int32, (tm, 1), 0) + i * tm
    return r < L


def _hg_post_rows(o, gv, gain_v, valid):
    parts = []
    for h in range(HG_HEADS):
        oh = o[:, 128 * h:128 * (h + 1)]
        parts.append(oh * lax.rsqrt(jnp.mean(oh * oh, axis=-1, keepdims=True) + EPS))
    return jnp.where(valid, jnp.concatenate(parts, axis=1) * gain_v * jax.nn.silu(gv), 0.0)


def _hg_post_bwd_rows(du, o, gv, gain_v, valid):
    duv = jnp.where(valid, du, 0.0)
    sig = jax.nn.sigmoid(gv)
    sg = gv * sig
    dn = duv * gain_v * sg
    do_parts, n_parts = [], []
    for h in range(HG_HEADS):
        sl = slice(128 * h, 128 * (h + 1))
        oh = o[:, sl]
        r = lax.rsqrt(jnp.mean(oh * oh, axis=-1, keepdims=True) + EPS)
        nh = oh * r
        dnh = dn[:, sl]
        do_parts.append(r * (dnh - nh * jnp.mean(dnh * nh, axis=-1, keepdims=True)))
        n_parts.append(nh)
    n = jnp.where(valid, jnp.concatenate(n_parts, axis=1), 0.0)
    do = jnp.where(valid, jnp.concatenate(do_parts, axis=1), 0.0)
    dg = duv * n * gain_v * (sig * (1.0 + gv * (1.0 - sig)))
    return do, dg, jnp.sum(duv * n * sg, axis=0, keepdims=True)


def _hg_pre_bwd(p_act, logits, dq_f, dq_b, dk_f, dk_b, db_f, db_b, dv_f, dv_b):
    def body(q_ref, zf_ref, zb_ref, lg_ref, dqf_ref, dqb_ref, dkf_ref, dkb_ref, dbf_ref, dbb_ref, dvf_ref, dvb_ref,
             dq_ref, dzf_ref, dzb_ref, di_ref, dlg_ref):
        i = pl.program_id(0)
        valid = _row_valid(i, HG_RB)
        qv = q_ref[...]
        sig = jax.nn.sigmoid(qv)
        dq_ref[...] = jnp.where(valid, (dqf_ref[...] + dqb_ref[...]) * (sig * (1.0 + qv * (1.0 - sig))), 0.0).astype(BF16)
        di_ref[...] = jnp.where(valid, dvf_ref[...] + dvb_ref[...], 0.0).astype(BF16)
        for d, (z_ref, dk_r, db_r, dz_ref) in enumerate(((zf_ref, dkf_ref, dbf_ref, dzf_ref), (zb_ref, dkb_ref, dbb_ref, dzb_ref))):
            lg = lg_ref[d]
            dl = lg[0:1, :] - lg[1:2, :]
            lb = jax.nn.sigmoid(dl)
            one_m_lb = jax.nn.sigmoid(-dl)
            log_f, _, snz, w2 = _hg_gate_terms(z_ref[...], lg)
            dbv = jnp.where(valid, db_r[...], 0.0)
            dkv = jnp.where(valid, dk_r[...], 0.0)
            dlf = jnp.dot(_chunk_tri(d == 1), dbv, precision=HI, preferred_element_type=F32)
            sz = 1.0 - snz
            dz_ref[...] = (dlf * w2 * snz - dkv * one_m_lb * sz * snz).astype(BF16)
            dlb = jnp.sum(dlf * snz * jnp.exp(-log_f) - dkv * snz, axis=0, keepdims=True)
            dl0 = dlb * lb * one_m_lb
            part = jnp.concatenate([dl0, -dl0], axis=0)

            @pl.when(i == 0)
            def _():
                dlg_ref[d] = part

            @pl.when(i > 0)
            def _():
                dlg_ref[d] += part

    blk = lambda c: pl.BlockSpec((HG_RB, 512), lambda i: (i, c))
    ob = pl.BlockSpec((HG_RB, 512), lambda i: (i, 0))
    lgs = pl.BlockSpec((2, 2, 512), lambda i: (0, 0, 0))
    return pl.pallas_call(
        body, name="hg_pre_bwd", grid=(HG_NB,),
        in_specs=[blk(3), blk(4), blk(5), lgs] + [ob] * 8,
        out_specs=(ob, ob, ob, ob, lgs),
        out_shape=(_sds((T, 512), BF16),) * 4 + (_sds((2, 2, 512), F32),),
        compiler_params=_cp(("arbitrary",)))(p_act, p_act, p_act, logits, dq_f, dq_b, dk_f, dk_b, db_f, db_b, dv_f, dv_b)


def _mix_fwd(o_na, o_f, o_b, gain, w_na, w_hg, p_act):
    def body(ona_ref, of_ref, ob_ref, g_ref, gain_ref, wna_ref, whg_ref, gna_ref, ghg_ref, o_ref, u_ref):
        u = _hg_post_rows(of_ref[...] + ob_ref[...], g_ref[...], gain_ref[...], _row_valid(pl.program_id(0), TM_B)).astype(BF16)
        u_ref[...] = u
        y_na = _dot(ona_ref[...], wna_ref[...])
        y_hg = _dot(u, whg_ref[...])
        o_ref[...] = (jax.nn.sigmoid(gna_ref[...]) * y_na + jax.nn.sigmoid(ghg_ref[...]) * y_hg).astype(BF16)

    act = pl.BlockSpec((TM_B, 512), lambda i: (i, 0))
    wsp = pl.BlockSpec((512, D), lambda i: (0, 0))
    return pl.pallas_call(
        body, name="mix_fwd", grid=(T // TM_B,),
        in_specs=[act, act, act, pl.BlockSpec((TM_B, 512), lambda i: (i, 7)), pl.BlockSpec((1, 512), lambda i: (0, 0)),
                  wsp, wsp, pl.BlockSpec((TM_B, D), lambda i: (i, 4)), pl.BlockSpec((TM_B, D), lambda i: (i, 5))],
        out_specs=(pl.BlockSpec((TM_B, D), lambda i: (i, 0)), act), out_shape=(_sds((T, D), BF16), _sds((T, 512), BF16)),
        compiler_params=_cp(("parallel",)))(o_na, o_f, o_b, p_act, gain, w_na, w_hg, p_act, p_act)


def _mix_bwd(o_na, u_hg, o_f, o_b, gain, w_na, w_hg, p_act, dmix):
    ni = T // TM_B

    def body(ona_ref, uhg_ref, of_ref, ob_ref, g_ref, gain_ref, wna_ref, whg_ref, gna_ref, ghg_ref, dmix_ref,
             dgna_ref, dghg_ref, dwna_ref, dwhg_ref, dona_ref, do_ref, dg_ref, dgain_ref, acc_na, acc_hg):
        i = pl.program_id(0)
        dm = dmix_ref[...].astype(F32)
        dxs = []
        for x_ref, w_ref, gt_ref, dgt_ref, dw_ref, acc in (
                (ona_ref, wna_ref, gna_ref, dgna_ref, dwna_ref, acc_na), (uhg_ref, whg_ref, ghg_ref, dghg_ref, dwhg_ref, acc_hg)):
            xv = x_ref[...]
            y = _dot(xv, w_ref[...])
            sg = jax.nn.sigmoid(gt_ref[...])
            dgt_ref[...] = (dm * y * sg * (1.0 - sg)).astype(BF16)
            dy = (dm * sg).astype(BF16)
            dxs.append(_dot(dy, w_ref[...], NT))
            part = _dot(xv, dy, TN)

            @pl.when(i == 0)
            def _():
                acc[...] = part

            @pl.when(i > 0)
            def _():
                acc[...] += part

            @pl.when(i == ni - 1)
            def _():
                dw_ref[...] = acc[...].astype(BF16)

        dona_ref[...] = dxs[0]
        do, dg, gpart = _hg_post_bwd_rows(dxs[1], of_ref[...] + ob_ref[...], g_ref[...], gain_ref[...], _row_valid(i, TM_B))
        do_ref[...] = do
        dg_ref[...] = dg.astype(BF16)

        @pl.when(i == 0)
        def _():
            dgain_ref[...] = gpart

        @pl.when(i > 0)
        def _():
            dgain_ref[...] += gpart

    act = pl.BlockSpec((TM_B, 512), lambda i: (i, 0))
    wsp = pl.BlockSpec((512, D), lambda i: (0, 0))
    rblk = pl.BlockSpec((TM_B, D), lambda i: (i, 0))
    vec = pl.BlockSpec((1, 512), lambda i: (0, 0))
    return pl.pallas_call(
        body, name="mix_bwd", grid=(ni,),
        in_specs=[act, act, act, act, pl.BlockSpec((TM_B, 512), lambda i: (i, 7)), vec, wsp, wsp,
                  pl.BlockSpec((TM_B, D), lambda i: (i, 4)), pl.BlockSpec((TM_B, D), lambda i: (i, 5)), rblk],
        out_specs=(rblk, rblk, wsp, wsp, act, act, act, vec),
        out_shape=(_sds((T, D), BF16), _sds((T, D), BF16), _sds((512, D), BF16), _sds((512, D), BF16),
                   _sds((T, 512), F32), _sds((T, 512), F32), _sds((T, 512), BF16), _sds((1, 512), F32)),
        scratch_shapes=[pltpu.VMEM((512, D), F32), pltpu.VMEM((512, D), F32)],
        compiler_params=_cp(("arbitrary",)))(o_na, u_hg, o_f, o_b, p_act, gain, w_na, w_hg, p_act, p_act, dmix)


def _wo_fwd(mix, w_o, h0, g_mlp):
    def body(mix_ref, w_ref, h0_ref, g_ref, h1_ref, m_ref):
        h1 = h0_ref[...] + _dot(mix_ref[...], w_ref[...])
        h1_ref[...] = h1
        r = lax.rsqrt(jnp.mean(h1 * h1, axis=-1, keepdims=True) + EPS)
        m_ref[...] = (h1 * r * g_ref[...]).astype(BF16)

    blk = pl.BlockSpec((TM_B, D), lambda i: (i, 0))
    return pl.pallas_call(
        body, name="wo_fwd", grid=(T // TM_B,),
        in_specs=[blk, pl.BlockSpec((D, D), lambda i: (0, 0)), blk, pl.BlockSpec((1, D), lambda i: (0, 0))],
        out_specs=(blk, blk), out_shape=(_sds((T, D), F32), _sds((T, D), BF16)),
        compiler_params=_cp(("parallel",)))(mix, w_o, h0, g_mlp)


def _wo_bwd(dh1_b, w_o, mix):
    ni = T // TM_B

    def body(dh_ref, w_ref, mix_ref, dmix_ref, dw_ref, acc):
        i = pl.program_id(0)
        dh = dh_ref[...]
        dmix_ref[...] = _dot(dh, w_ref[...], NT).astype(BF16)
        part = _dot(mix_ref[...], dh, TN)

        @pl.when(i == 0)
        def _():
            acc[...] = part

        @pl.when(i > 0)
        def _():
            acc[...] += part

        @pl.when(i == ni - 1)
        def _():
            dw_ref[...] = acc[...].astype(BF16)

    blk = pl.BlockSpec((TM_B, D), lambda i: (i, 0))
    wsp = pl.BlockSpec((D, D), lambda i: (0, 0))
    return pl.pallas_call(
        body, name="wo_bwd", grid=(ni,), in_specs=[blk, wsp, blk], out_specs=(blk, wsp),
        out_shape=(_sds((T, D), BF16), _sds((D, D), BF16)), scratch_shapes=[pltpu.VMEM((D, D), F32)],
        compiler_params=_cp(("arbitrary",)))(dh1_b, w_o, mix)


FF_B = D_FF // NDEV


def _loss_rows(xv, gv, tv, row0):
    r_io = lax.broadcasted_iota(jnp.int32, (xv.shape[0], 1), 0) + row0
    valid = (r_io >= NM) & (r_io < L)
    r = lax.rsqrt(jnp.mean(xv * xv, axis=-1, keepdims=True) + EPS)
    xh = xv * r
    err = jnp.where(valid, xh * gv - tv, 0.0)
    lpart = 0.5 * jnp.sum(jnp.sum(err * err, axis=-1, keepdims=True) * (1.0 / D), axis=0, keepdims=True)
    dy = err * (1.0 / D)
    dxh = dy * gv
    dh = r * (dxh - xh * jnp.mean(dxh * xh, axis=-1, keepdims=True))
    return lpart, dh, jnp.sum(dy * xh, axis=0, keepdims=True)


def _mlp_fwd_loss(m, wup_g, wdown_g, h1, g_final, tgt):
    nsub = TM_MM // TM_E

    def body(m_ref, wu_ref, wd_ref, h1_ref, g_ref, t_ref, loss_ref, dh_ref, dhb_ref, dg_ref, h2):
        i, j = pl.program_id(0), pl.program_id(1)
        up = jnp.maximum(_dot(m_ref[...], wu_ref[0]), 0.0)
        part = _dot((up * up).astype(BF16), wd_ref[0])

        @pl.when(j == 0)
        def _():
            h2[...] = h1_ref[...] + part

        @pl.when(j > 0)
        def _():
            h2[...] += part

        @pl.when(j == NDEV - 1)
        def _():
            lsum = jnp.zeros((1, 1), F32)
            gsum = jnp.zeros((1, D), F32)
            for s in range(nsub):
                rows = slice(s * TM_E, (s + 1) * TM_E)
                lpart, dh, gpart = _loss_rows(h2[rows, :], g_ref[...], t_ref[rows, :], i * TM_MM + s * TM_E)
                dh_ref[rows, :] = dh
                dhb_ref[rows, :] = dh.astype(BF16)
                lsum = lsum + lpart
                gsum = gsum + gpart
            lsum = jnp.broadcast_to(lsum, (1, 128))

            @pl.when(i == 0)
            def _():
                loss_ref[...] = lsum
                dg_ref[...] = gsum

            @pl.when(i > 0)
            def _():
                loss_ref[...] += lsum
                dg_ref[...] += gsum

    blk = pl.BlockSpec((TM_MM, D), lambda i, j: (i, 0))
    vec = pl.BlockSpec((1, D), lambda i, j: (0, 0))
    return pl.pallas_call(
        body, name="mlp_fwd_loss", grid=(T // TM_MM, NDEV),
        in_specs=[blk, pl.BlockSpec((1, D, FF_B), lambda i, j: (j, 0, 0)), pl.BlockSpec((1, FF_B, D), lambda i, j: (j, 0, 0)),
                  blk, vec, blk],
        out_specs=(pl.BlockSpec((1, 128), lambda i, j: (0, 0)), blk, blk, vec),
        out_shape=(_sds((1, 128), F32), _sds((T, D), F32), _sds((T, D), BF16), _sds((1, D), F32)),
        scratch_shapes=[pltpu.VMEM((TM_MM, D), F32)],
        compiler_params=_cp(("arbitrary", "arbitrary"), 56))(m, wup_g, wdown_g, h1, g_final, tgt)


def _mlp_bwd(m, dh2_b, wup_g, wdown_g, h1, g_mlp, dh2):
    ni = T // TM_B
    nsub = TM_B // TM_E

    def body(m_ref, dh_ref, wu_ref, wd_ref, h1_ref, g_ref, dres_ref, dwu_ref, dwd_ref, dh1_ref, dh1b_ref, dg_ref,
             dm_ref, acc_u, acc_d):
        j, i = pl.program_id(0), pl.program_id(1)
        rows = pl.ds(pl.multiple_of(i * TM_B, TM_B), TM_B)
        mv, dh = m_ref[...], dh_ref[...]
        r = jnp.maximum(_dot(mv, wu_ref[0]), 0.0)
        act = (r * r).astype(BF16)
        dact = _dot(dh, wd_ref[0], NT)
        dup = (dact * (2.0 * r)).astype(BF16)
        pd = _dot(act, dh, TN)
        pu = _dot(mv, dup, TN)
        dmv = _dot(dup, wu_ref[0], NT)

        @pl.when(i == 0)
        def _():
            acc_u[...] = pu
            acc_d[...] = pd

        @pl.when(i > 0)
        def _():
            acc_u[...] += pu
            acc_d[...] += pd

        @pl.when(i == ni - 1)
        def _():
            dwu_ref[0] = acc_u[...].astype(BF16)
            dwd_ref[0] = acc_d[...].astype(BF16)

        @pl.when(j == 0)
        def _():
            dm_ref[rows, :] = dmv

        @pl.when(j > 0)
        def _():
            dm_ref[rows, :] += dmv

        @pl.when(j == NDEV - 1)
        def _():
            gsum = jnp.zeros((1, D), F32)
            for s in range(nsub):
                sub = slice(s * TM_E, (s + 1) * TM_E)
                dm_rows = dm_ref[pl.ds(pl.multiple_of(i * TM_B + s * TM_E, TM_E), TM_E), :]
                dx, gpart = _norm_bwd_rows(h1_ref[sub, :], g_ref[...], dm_rows, dres_ref[sub, :])
                dh1_ref[sub, :] = dx
                dh1b_ref[sub, :] = dx.astype(BF16)
                gsum = gsum + gpart

            @pl.when(i == 0)
            def _():
                dg_ref[...] = gsum

            @pl.when(i > 0)
            def _():
                dg_ref[...] += gsum

    blk = pl.BlockSpec((TM_B, D), lambda j, i: (i, 0))
    late = pl.BlockSpec((TM_B, D), lambda j, i: (jnp.where(j == NDEV - 1, i, 0), 0))
    vec = pl.BlockSpec((1, D), lambda j, i: (0, 0))
    wus = pl.BlockSpec((1, D, FF_B), lambda j, i: (j, 0, 0))
    wds = pl.BlockSpec((1, FF_B, D), lambda j, i: (j, 0, 0))
    return pl.pallas_call(
        body, name="mlp_bwd", grid=(NDEV, ni), in_specs=[blk, blk, wus, wds, late, vec, late],
        out_specs=(wus, wds, late, late, vec),
        out_shape=(_sds((NDEV, D, FF_B), BF16), _sds((NDEV, FF_B, D), BF16), _sds((T, D), F32), _sds((T, D), BF16),
                   _sds((1, D), F32)),
        scratch_shapes=[pltpu.VMEM((T, D), F32), pltpu.VMEM((D, FF_B), F32), pltpu.VMEM((FF_B, D), F32)],
        compiler_params=_cp(("arbitrary", "arbitrary"), 56))(m, dh2_b, wup_g, wdown_g, h1, g_mlp, dh2)


def _adamw(parts, w, m, v, name):
    rr, cc = w.shape
    tr = rr
    for cand in (256, 128, 64):
        if rr % cand == 0 and rr > cand:
            tr = cand
            break
    c1 = 1.0 - ADAM_B1 ** ADAM_STEP
    c2 = 1.0 - ADAM_B2 ** ADAM_STEP

    def body(p_ref, w_ref, m_ref, v_ref, g_ref, d_ref, nm_ref, nv_ref):
        g = p_ref[0].astype(F32)
        for s in range(1, NDEV):
            g = g + p_ref[s].astype(F32)
        mn = ADAM_B1 * m_ref[...] + (1.0 - ADAM_B1) * g
        vn = ADAM_B2 * v_ref[...] + (1.0 - ADAM_B2) * (g * g)
        g_ref[...] = g
        nm_ref[...] = mn
        nv_ref[...] = vn
        d_ref[...] = -ADAM_LR * ((mn / c1) / (jnp.sqrt(vn / c2) + ADAM_EPS) + ADAM_WD * w_ref[...])

    blk = pl.BlockSpec((tr, cc), lambda i: (i, 0))
    return pl.pallas_call(
        body, name=name, grid=(rr // tr,),
        in_specs=[pl.BlockSpec((NDEV, tr, cc), lambda i: (0, i, 0)), blk, blk, blk],
        out_specs=(blk,) * 4, out_shape=(_sds((rr, cc), F32),) * 4,
        compiler_params=_cp(("parallel",)))(parts, w, m, v)


RPB_N = NA_HEADS * 15 * 31
RPB_PAD = 4096
OWN_ROWS = NM + 8


def _pad_rows(a, rows):
    return jnp.pad(a, ((0, rows - a.shape[0]),) + ((0, 0),) * (a.ndim - 1))


def _pack_owned(meta_blk, lb_blk):
    return jnp.concatenate([meta_blk, _pad_rows(lb_blk.reshape(2, 128), 8)], axis=0)


LOSS_ROW = 28


def _pack_replicated(n_mix, n_mlp, n_final, hg_gain, rpb, loss_row=None):
    flat = _pad_rows(rpb.reshape(RPB_N), RPB_PAD)
    gain8 = _pad_rows(hg_gain.reshape(4, 128), 8)
    if loss_row is not None:
        gain8 = gain8 + jnp.pad(loss_row, ((LOSS_ROW - 24, 31 - LOSS_ROW), (0, 0)))
    return jnp.concatenate([n_mix.reshape(8, 128), n_mlp.reshape(8, 128), n_final.reshape(8, 128), gain8,
                            flat.reshape(32, 128)], axis=0)


def _unpack_replicated(a):
    return (a[0:8].reshape(1, D), a[8:16].reshape(1, D), a[16:24].reshape(D), a[24:28].reshape(1, 512),
            a[32:64].reshape(RPB_PAD)[:RPB_N].reshape(1, NA_HEADS, 15, 31))


def kernel(x, meta_tokens, w_in, w_na_out, w_hg_out, w_o, w_up, w_down, norm_mix, norm_mlp, norm_final, hg_norm, na_rpb, hg_lb_logits, loss_target, m_meta_tokens, m_w_in, m_w_na_out, m_w_hg_out, m_w_o, m_w_up, m_w_down, m_norm_mix, m_norm_mlp, m_norm_final, m_hg_norm, m_na_rpb, m_hg_lb_logits, v_meta_tokens, v_w_in, v_w_na_out, v_w_hg_out, v_w_o, v_w_up, v_w_down, v_norm_mix, v_norm_mlp, v_norm_final, v_hg_norm, v_na_rpb, v_hg_lb_logits):
    owned = _pack_owned(meta_tokens, hg_lb_logits)
    first, tok = _exchange_start([w_in[0].astype(BF16), owned], [False] * 2, "gather_first_start", SAME_CORE_AND_SIBLING)
    bias_tab = _na_bias_table(_tie(jnp.pad(na_rpb[0], ((0, 0), (0, 0), (0, 128 - 31))), tok, "tie_bias_table"))
    first = _exchange_wait(first, [False] * 2, [bias_tab], "gather_first_wait", SAME_CORE_AND_SIBLING)
    win_g, owned_g = _forward_to_sibling(first, "gather_first_forward")
    later = [w[0].astype(BF16) for w in (w_na_out, w_hg_out, w_o, w_up, w_down)]
    later[0] = _tie(later[0], owned_g, "tie_gather_rest")
    gather_rest, tok = _exchange_start(later, [False] * 5, "gather_rest_start")
    win_g = _tie(win_g, tok, "tie_inproj")
    meta_full = jnp.transpose(owned_g[:, 0:NM, :], (1, 0, 2)).reshape(NM, D)
    logits = jnp.transpose(owned_g[:, NM:NM + 2, :].reshape(NDEV, 2, 2, 64), (1, 2, 0, 3)).reshape(2, 2, 512)

    h0 = jnp.concatenate([meta_full, x[0], jnp.zeros((T - L, D), F32)], axis=0)
    tgt = jnp.concatenate([jnp.zeros((NM, D), F32), loss_target[0], jnp.zeros((T - L, D), F32)], axis=0)

    a, a_t = _norm_fwd_t(h0, norm_mix, "norm_mix_fwd")
    p_act = _inproj_fwd(a, win_g)
    o_na, lse = _na_fwd(p_act, bias_tab)
    qh, k_f, b_f, k_b, b_b = _hg_pre(p_act, logits)
    o_f, st_f = _hg_scan_fwd(qh, k_f, b_f, p_act, False)
    o_b, st_b = _hg_scan_fwd(qh, k_b, b_b, p_act, True)
    wna_g, whg_g, wo_g, wup_g, wdown_g = _exchange_wait(gather_rest, [False] * 5, [o_f, o_b, o_na], "gather_rest_wait")
    w_o_full = wo_g.reshape(D, D)
    w_na_full = jnp.transpose(wna_g, (1, 0, 2)).reshape(512, D)
    w_hg_full = jnp.transpose(whg_g, (1, 0, 2)).reshape(512, D)
    mix, u_hg = _mix_fwd(o_na, o_f, o_b, hg_norm, w_na_full, w_hg_full, p_act)
    h1, m_act = _wo_fwd(mix, w_o_full, h0, norm_mlp)
    loss_part, dh2, dh2_b, d_nfinal = _mlp_fwd_loss(m_act, wup_g, wdown_g, h1, norm_final.reshape(1, D), tgt)

    dwup_p, dwdown_p, dh1, dh1_b, d_nmlp = _mlp_bwd(m_act, dh2_b, wup_g, wdown_g, h1, norm_mlp, dh2)
    sc_mlp, tok = _exchange_start([dwup_p, dwdown_p], [True] * 2, "scatter_mlp_start")
    dmix, dwo = _wo_bwd(_tie(dh1_b, tok, "tie_wo_bwd"), w_o_full, mix)
    sc_wo, tok = _exchange_start([dwo.reshape(NDEV, D // NDEV, D)], [True], "scatter_wo_start")
    dgna, dghg, dwna, dwhg, do_na, do_hg, dg_hg, d_gain = _mix_bwd(
        o_na, u_hg, o_f, o_b, hg_norm, w_na_full, w_hg_full, p_act, _tie(dmix, tok, "tie_mix_bwd"))
    owner_cols = lambda w: jnp.transpose(w.reshape(512, NDEV, D // NDEV), (1, 0, 2))
    sc_br, tok = _exchange_start([owner_cols(dwna), owner_cols(dwhg)], [True] * 2, "scatter_branch_start")
    do_hg = _tie(do_hg, tok, "tie_hg_scan_bwd")
    dq_f, dk_f, db_f, dv_f = _hg_scan_bwd(qh, k_f, b_f, p_act, st_f, do_hg, False)
    dq_b, dk_b, db_b, dv_b = _hg_scan_bwd(qh, k_b, b_b, p_act, st_b, do_hg, True)
    dq_hg, dz_f, dz_b, di_hg, d_logits = _hg_pre_bwd(p_act, logits, dq_f, dq_b, dk_f, dk_b, db_f, db_b, dv_f, dv_b)
    dq_na, dk_na, dv_na, dbias = _na_bwd(p_act, do_na, lse, bias_tab)
    dp = jnp.concatenate([dq_na.astype(BF16), dk_na.astype(BF16), dv_na.astype(BF16), dq_hg, dz_f, dz_b, di_hg, dg_hg,
                          dgna, dghg], axis=1)
    dwin_p = _inproj_bwd_dw(a_t, dp)
    sc_in, tok = _exchange_start([dwin_p], [True], "scatter_in_start")
    dh0, d_nmix = _inproj_bwd_da(_tie(dp, tok, "tie_inproj_bwd_da"), win_g, h0, norm_mix, dh1)
    d_rpb = _na_rpb_reduce(_tie(dbias, tok, "tie_rpb_reduce"))[:, :, :31]

    res = {}

    def update(nm, parts, w, mm, vv):
        res[nm] = [r[None] for r in _adamw(parts, w[0], mm[0], vv[0], "adamw_" + nm)]
        return res[nm][1]

    wup_r, wdown_r = _exchange_wait(sc_mlp, [True] * 2, [dh0, d_rpb], "scatter_mlp_wait")
    update("w_up", wup_r, w_up, m_w_up, v_w_up)
    last = update("w_down", wdown_r, w_down, m_w_down, v_w_down)
    (wo_r,) = _exchange_wait(sc_wo, [True], [last], "scatter_wo_wait")
    last = update("w_o", wo_r, w_o, m_w_o, v_w_o)
    wna_r, whg_r = _exchange_wait(sc_br, [True] * 2, [last], "scatter_branch_wait")
    update("w_na_out", wna_r, w_na_out, m_w_na_out, v_w_na_out)
    last = update("w_hg_out", whg_r, w_hg_out, m_w_hg_out, v_w_hg_out)

    d_meta = jnp.transpose(dh0[0:NM].reshape(NM, NDEV, 128), (1, 0, 2))
    d_lg = jnp.transpose(d_logits.reshape(2, 2, NDEV, 64), (2, 0, 1, 3)).reshape(NDEV, 2, 128)
    owned_p = jnp.concatenate([d_meta, jnp.pad(d_lg, ((0, 0), (0, OWN_ROWS - NM - 2), (0, 0)))], axis=1)
    repl_p = _pack_replicated(d_nmix, d_nmlp, d_nfinal, d_gain, d_rpb, loss_part)
    grad_x = dh0[NM:L][None]
    done_first = [grad_x] + [res[nm][0] for nm in ("w_up", "w_down", "w_o", "w_na_out", "w_hg_out")]
    owned_r, repl_r = _exchange([owned_p, repl_p], [True, False], "scatter_small", done_first)
    own = _adamw(owned_r, owned, _pack_owned(m_meta_tokens, m_hg_lb_logits), _pack_owned(v_meta_tokens, v_hg_lb_logits),
                 "adamw_owned_small")
    res["meta_tokens"] = [r[0:NM] for r in own]
    res["hg_lb_logits"] = [r[NM:NM + 2].reshape(2, 2, 64) for r in own]
    rep = _adamw(repl_r, _pack_replicated(norm_mix, norm_mlp, norm_final, hg_norm, na_rpb),
                 _pack_replicated(m_norm_mix, m_norm_mlp, m_norm_final, m_hg_norm, m_na_rpb),
                 _pack_replicated(v_norm_mix, v_norm_mlp, v_norm_final, v_hg_norm, v_na_rpb), "adamw_replicated")
    for q in range(4):
        um = _unpack_replicated(rep[q])
        for nm, val in zip(("norm_mix", "norm_mlp", "norm_final", "hg_norm", "na_rpb"), um):
            res.setdefault(nm, [None] * 4)[q] = val
    (win_r,) = _exchange_wait(sc_in, [True], [rep[1], own[1]], "scatter_in_wait")
    update("w_in", win_r, w_in, m_w_in, v_w_in)

    loss = jnp.sum(repl_r[:, LOSS_ROW, 0])
    order = ("meta_tokens", "w_in", "w_na_out", "w_hg_out", "w_o", "w_up", "w_down", "norm_mix", "norm_mlp", "norm_final",
             "hg_norm", "na_rpb", "hg_lb_logits")
    outs = [loss, grad_x]
    for q in range(4):
        outs += [res[nm][q] for nm in order]
    return tuple(outs)
```

```python
import functools

import numpy as np
import jax
import jax.numpy as jnp
from jax import lax
from jax.experimental import pallas as pl
from jax.experimental.pallas import tpu as pltpu

F32 = jnp.float32
BF16 = jnp.bfloat16

D = 1024
SEQ = 2048
NM = 16
L = SEQ + NM
T = 2176
NDEV = 8
EPS = 1e-6
GRID_W = 64
ROWS = SEQ // GRID_W
NA_HEADS = 8
NA_DH = 64
NA_SCALE = NA_DH ** -0.5
HG_HEADS = 4
HG_C = 16
NCHUNK = L // HG_C
D_FF = 4096
IN_COLS = 6144
NEG = -1e30

ADAM_LR = 0.001
ADAM_B1 = 0.9
ADAM_B2 = 0.999
ADAM_EPS = 1e-08
ADAM_WD = 0.01
ADAM_STEP = 10

MESH_ID = pl.DeviceIdType.MESH
ANY = pl.BlockSpec(memory_space=pl.ANY)

NN = (((1,), (0,)), ((), ()))
NT = (((1,), (1,)), ((), ()))
TN = (((0,), (0,)), ((), ()))


def _cp(sem=None, vmem_mb=48):
    return pltpu.CompilerParams(dimension_semantics=sem, vmem_limit_bytes=vmem_mb * 1024 * 1024)


def _dot(a, b, dims=NN):
    return lax.dot_general(a, b, dims, preferred_element_type=F32)


def _sds(shape, dtype):
    return jax.ShapeDtypeStruct(shape, dtype)


HBM = pl.BlockSpec(memory_space=pltpu.HBM)
SEM = pl.BlockSpec(memory_space=pltpu.SEMAPHORE)
EFFECT = pltpu.SideEffectType.DATAFLOW_SIDE_EFFECTING


def _exchange(arrs, scatter, name, after=()):
    n = len(arrs)
    after = list(after)
    out_shapes = []
    for a, sc in zip(arrs, scatter):
        out_shapes.append(_sds(a.shape if sc else (NDEV,) + a.shape, a.dtype))

    def body(*refs):
        ins, outs = refs[:n], refs[n + len(after):2 * n + len(after)]
        send_sems, recv_sems, loc_sems = refs[2 * n + len(after):]
        me = 4 * lax.axis_index("x") + 2 * lax.axis_index("y") + lax.axis_index("c")
        copies = []
        for k in range(n):
            src_me = ins[k].at[me] if scatter[k] else ins[k]
            loc = pltpu.make_async_copy(src_me, outs[k].at[me], loc_sems.at[k])
            loc.start()
            copies.append(loc)
        remote = _peer_copies(ins, outs, scatter, send_sems, recv_sems)
        for cp in remote:
            cp.start()
        for cp in remote:
            cp.wait_recv()
        for cp in remote:
            cp.wait_send()
        for cp in copies:
            cp.wait()

    return pl.pallas_call(
        body, name=name, out_shape=tuple(out_shapes), in_specs=[ANY] * (n + len(after)), out_specs=tuple([ANY] * n),
        scratch_shapes=[pltpu.SemaphoreType.DMA((n * (NDEV - 1),)), pltpu.SemaphoreType.DMA((n * (NDEV - 1),)),
                        pltpu.SemaphoreType.DMA((n,))],
    )(*arrs, *after)


def _forward_to_sibling(bufs, name):
    n = len(bufs)

    def body(*refs):
        ins, outs = refs[:n], refs[n:2 * n]
        send_sems, recv_sems = refs[2 * n:]
        x, y, c = lax.axis_index("x"), lax.axis_index("y"), lax.axis_index("c")
        copies = []
        for k in range(n):
            for j, (cx, cy) in enumerate(((1 - x, y), (x, 1 - y), (1 - x, 1 - y))):
                slot = 4 * cx + 2 * cy + c
                copies.append(pltpu.make_async_remote_copy(
                    src_ref=ins[k].at[slot], dst_ref=outs[k].at[slot], send_sem=send_sems.at[3 * k + j],
                    recv_sem=recv_sems.at[3 * k + j], device_id=(x, y, 1 - c), device_id_type=MESH_ID))
        for cp in copies:
            cp.start()
        for cp in copies:
            cp.wait_recv()
        for cp in copies:
            cp.wait_send()

    return pl.pallas_call(
        body, name=name, out_shape=tuple(_sds(b.shape, b.dtype) for b in bufs), in_specs=[ANY] * n,
        out_specs=tuple([ANY] * n), input_output_aliases={k: k for k in range(n)},
        scratch_shapes=[pltpu.SemaphoreType.DMA((3 * n,)), pltpu.SemaphoreType.DMA((3 * n,))],
    )(*bufs)


ALL_PEERS = tuple(range(1, NDEV))
SAME_CORE_AND_SIBLING = (1, 2, 4, 6)


def _peer_copies(srcs, lands, scatter, send_sems, recv_sems, masks=ALL_PEERS):
    x, y, c = lax.axis_index("x"), lax.axis_index("y"), lax.axis_index("c")
    me = 4 * x + 2 * y + c
    out = []
    for k in range(len(srcs)):
        for m in masks:
            px, py, pc = x ^ (m >> 2), y ^ ((m >> 1) & 1), c ^ (m & 1)
            src = srcs[k].at[4 * px + 2 * py + pc] if scatter[k] else srcs[k]
            out.append(pltpu.make_async_remote_copy(
                src_ref=src, dst_ref=lands[k].at[me], send_sem=send_sems.at[k * (NDEV - 1) + m - 1],
                recv_sem=recv_sems.at[k * (NDEV - 1) + m - 1],
                device_id=(px, py, pc), device_id_type=MESH_ID))
    return out


def _exchange_start(arrs, scatter, name, masks=ALL_PEERS):
    n = len(arrs)
    me = 4 * lax.axis_index("x") + 2 * lax.axis_index("y") + lax.axis_index("c")
    lands = []
    for a, sc in zip(arrs, scatter):
        own = lax.dynamic_index_in_dim(a, me, 0, keepdims=True) if sc else a[None]
        shape = a.shape if sc else (NDEV,) + a.shape
        lands.append(lax.dynamic_update_index_in_dim(lax.empty(shape, a.dtype), own, me, 0))

    def body(*refs):
        srcs, lnds = refs[:n], refs[n:2 * n]
        send_sems, recv_sems = refs[2 * n], refs[2 * n + 1]
        token = refs[-1]
        for cp in _peer_copies(srcs, lnds, scatter, send_sems, recv_sems, masks):
            cp.start()
        token[...] = jnp.zeros_like(token)

    ops = [pltpu.with_memory_space_constraint(a, pltpu.HBM) for a in list(arrs) + lands]
    res = pl.pallas_call(
        body, name=name,
        out_shape=(pltpu.SemaphoreType.DMA((n * (NDEV - 1),)), pltpu.SemaphoreType.DMA((n * (NDEV - 1),)))
        + tuple(pltpu.HBM(o.shape, o.dtype) for o in ops) + (_sds((8, 128), F32),),
        in_specs=[HBM] * (2 * n), out_specs=(SEM, SEM) + (HBM,) * (2 * n) + (pl.BlockSpec(memory_space=pltpu.VMEM),),
        input_output_aliases={k: 2 + k for k in range(2 * n)},
        compiler_params=pltpu.CompilerParams(has_side_effects=EFFECT),
    )(*ops)
    return res[:-1], res[-1]


def _exchange_wait(handle, scatter, after, name, masks=ALL_PEERS):
    send_sems, recv_sems = handle[0], handle[1]
    bufs = handle[2:]
    n = len(bufs) // 2
    after = list(after)

    def body(*refs):
        srcs, lnds = refs[:n], refs[n:2 * n]
        for cp in _peer_copies(srcs, lnds, scatter, refs[2 * n], refs[2 * n + 1], masks):
            cp.wait_send()
            cp.wait_recv()

    res = pl.pallas_call(
        body, name=name, out_shape=tuple(pltpu.HBM(b.shape, b.dtype) for b in bufs),
        in_specs=[HBM] * (2 * n) + [SEM, SEM] + [ANY] * len(after), out_specs=(HBM,) * (2 * n),
        input_output_aliases={k: k for k in range(2 * n)},
        compiler_params=pltpu.CompilerParams(has_side_effects=EFFECT),
    )(*bufs, send_sems, recv_sems, *after)
    return res[n:]


def _tie(x, token, name):
    def body(x_ref, t_ref, o_ref):
        del x_ref, t_ref, o_ref

    return pl.pallas_call(body, name=name, out_shape=_sds(x.shape, x.dtype), in_specs=[ANY, ANY], out_specs=ANY,
                          input_output_aliases={0: 0})(x, token)


TM_E = 272


def _norm_fwd_t(h, g, name):
    def body(h_ref, g_ref, o_ref, ot_ref):
        xv = h_ref[...]
        r = lax.rsqrt(jnp.mean(xv * xv, axis=-1, keepdims=True) + EPS)
        y = xv * r * g_ref[...]
        o_ref[...] = y.astype(BF16)
        ot_ref[...] = y.T.astype(BF16)

    return pl.pallas_call(
        body, name=name, grid=(T // 128,),
        in_specs=[pl.BlockSpec((128, D), lambda i: (i, 0)), pl.BlockSpec((1, D), lambda i: (0, 0))],
        out_specs=(pl.BlockSpec((128, D), lambda i: (i, 0)), pl.BlockSpec((D, 128), lambda i: (0, i))),
        out_shape=(_sds((T, D), BF16), _sds((D, T), BF16)), compiler_params=_cp(("parallel",)))(h, g)


def _norm_bwd_rows(xv, gv, dnv, dres):
    r = lax.rsqrt(jnp.mean(xv * xv, axis=-1, keepdims=True) + EPS)
    xh = xv * r
    dxh = dnv * gv
    dx = dres + r * (dxh - xh * jnp.mean(dxh * xh, axis=-1, keepdims=True))
    return dx, jnp.sum(dnv * xh, axis=0, keepdims=True)


TM_MM = 1088


def _inproj_fwd(a, w_g):
    nb = w_g.shape[2]

    def body(a_ref, w_ref, o_ref):
        o_ref[...] = _dot(a_ref[...], w_ref[0])

    return pl.pallas_call(
        body, name="inproj_fwd", grid=(T // TM_MM, NDEV),
        in_specs=[pl.BlockSpec((TM_MM, D), lambda i, j: (i, 0)), pl.BlockSpec((1, D, nb), lambda i, j: (j, 0, 0))],
        out_specs=pl.BlockSpec((TM_MM, nb), lambda i, j: (i, j)), out_shape=_sds((T, NDEV * nb), F32),
        compiler_params=_cp(("parallel", "parallel")))(a, w_g)


TM_B = 544


W_IN_B = IN_COLS // NDEV


def _inproj_bwd_dw(a_t, dp):
    def body(at_ref, dp_ref, dw_ref):
        dw_ref[0] = _dot(at_ref[...], dp_ref[...]).astype(BF16)

    return pl.pallas_call(
        body, name="inproj_bwd_dw", grid=(NDEV,),
        in_specs=[pl.BlockSpec((D, T), lambda j: (0, 0)), pl.BlockSpec((T, W_IN_B), lambda j: (0, j))],
        out_specs=pl.BlockSpec((1, D, W_IN_B), lambda j: (j, 0, 0)), out_shape=_sds((NDEV, D, W_IN_B), BF16),
        compiler_params=_cp(("parallel",)))(a_t, dp)


def _inproj_bwd_da(dp, w_g, h0, g_mix, dh1):
    nsub = TM_MM // TM_E

    def body(dp_ref, w_ref, h0_ref, g_ref, dres_ref, dh0_ref, dg_ref, da):
        i, j = pl.program_id(0), pl.program_id(1)
        dav = _dot(dp_ref[...], w_ref[0], NT)

        @pl.when(j == 0)
        def _():
            da[...] = dav

        @pl.when(j > 0)
        def _():
            da[...] += dav

        @pl.when(j == NDEV - 1)
        def _():
            gsum = jnp.zeros((1, D), F32)
            for s in range(nsub):
                sub = slice(s * TM_E, (s + 1) * TM_E)
                dx, gpart = _norm_bwd_rows(h0_ref[sub, :], g_ref[...], da[sub, :], dres_ref[sub, :])
                dh0_ref[sub, :] = dx
                gsum = gsum + gpart

            @pl.when(i == 0)
            def _():
                dg_ref[...] = gsum

            @pl.when(i > 0)
            def _():
                dg_ref[...] += gsum

    rblk = pl.BlockSpec((TM_MM, D), lambda i, j: (i, 0))
    vec = pl.BlockSpec((1, D), lambda i, j: (0, 0))
    return pl.pallas_call(
        body, name="inproj_bwd_da", grid=(T // TM_MM, NDEV),
        in_specs=[pl.BlockSpec((TM_MM, W_IN_B), lambda i, j: (i, j)), pl.BlockSpec((1, D, W_IN_B), lambda i, j: (j, 0, 0)),
                  rblk, vec, rblk],
        out_specs=(rblk, vec), out_shape=(_sds((T, D), F32), _sds((1, D), F32)),
        scratch_shapes=[pltpu.VMEM((TM_MM, D), F32)],
        compiler_params=_cp(("arbitrary", "arbitrary"), 56))(dp, w_g, h0, g_mix, dh1)


NA_QB = 256
NA_GROUPS = ROWS // 4
NA_UROWS = 11
NA_KW = NA_UROWS * GRID_W
NA_KU = 768


def _na_row_offset(var, i, j):
    valid = (j < 8, i <= j < i + 8, 3 <= j < NA_UROWS)[var]
    return (j - i + (7, 3, 0)[var]) if valid else None


def _na_bias_table(rp):
    def body(r_ref, o_ref):
        row3 = lax.broadcasted_iota(jnp.int32, (15, GRID_W, 128), 1)
        lane3 = lax.broadcasted_iota(jnp.int32, (15, GRID_W, 128), 2)
        w3 = lane3 & (GRID_W - 1)
        cs3 = jnp.clip(row3 - 8, 0, GRID_W - 16)
        lane = lax.broadcasted_iota(jnp.int32, (GRID_W, 128), 1)
        neg = jnp.full((GRID_W, 128), NEG, F32)
        z = jnp.stack([jnp.broadcast_to(r_ref[0, a:a + 1, :], (GRID_W, 128)) for a in range(15)])
        for bit in range(6):
            sh = 1 << bit
            z = jnp.where((row3 & sh) != 0, jnp.roll(z, sh, axis=2), z)
        z = jnp.roll(z, 128 - 15, axis=2)
        z = jnp.where(lane3 < GRID_W, z, 0.0)
        z = z + jnp.roll(z, GRID_W, axis=2)
        tabs = jnp.where((w3 >= cs3) & (w3 < cs3 + 16), z, NEG)
        tail = jnp.where(lane < GRID_W + NM, 0.0, NEG)
        for var in range(3):
            for i in range(4):
                for jp in range(NA_KU // 128):
                    halves = []
                    for j in (2 * jp, 2 * jp + 1):
                        a = _na_row_offset(var, i, j) if j < NA_UROWS else None
                        halves.append(tail if j >= NA_UROWS else (neg if a is None else tabs[a]))
                    o_ref[var, 0, i * 64:(i + 1) * 64, jp * 128:(jp + 1) * 128] = jnp.where(lane < GRID_W, halves[0], halves[1])

    return pl.pallas_call(
        body, name="na_bias_table", grid=(NA_HEADS,),
        in_specs=[pl.BlockSpec((1, 15, 128), lambda h: (h, 0, 0))],
        out_specs=pl.BlockSpec((3, 1, NA_QB, NA_KU), lambda h: (0, h, 0, 0)),
        out_shape=_sds((3, NA_HEADS, NA_QB, NA_KU), F32), compiler_params=_cp(("parallel",)))(rp)


def _na_var(g):
    return jnp.where(g == 0, 0, jnp.where(g == NA_GROUPS - 1, 2, 1))


def _na_load_window(src_ref, dst, g):
    us = jnp.clip(4 * g - 4, 0, ROWS - NA_UROWS)
    kstart = pl.multiple_of(NM + GRID_W * us, 16)
    dst[0:NA_KW, :] = src_ref[pl.ds(kstart, NA_KW), :].astype(BF16)
    dst[NA_KW:NA_KW + NM, :] = src_ref[0:NM, :].astype(BF16)
    dst[NA_KW + NM:, :] = jnp.zeros((NA_KU - NA_KW - NM, 128), BF16)
    return kstart


def _na_fwd(p_act, bias_tab):
    def body(q_ref, k_ref, v_ref, b_ref, o_ref, lse_ref, ku, vu):
        g = pl.program_id(1)
        _na_load_window(k_ref, ku, g)
        _na_load_window(v_ref, vu, g)
        qstart = pl.multiple_of(NM + NA_QB * g, 16)
        q = q_ref[pl.ds(qstart, NA_QB), :]
        lane = lax.broadcasted_iota(jnp.int32, (NA_QB, 128), 1)
        o_h, lse_h = [], []
        for h in range(2):
            hm = (lane < 64) if h == 0 else (lane >= 64)
            qm = (jnp.where(hm, q, 0.0) * NA_SCALE).astype(BF16)
            s = _dot(qm, ku[...], NT) + b_ref[0, h]
            m = jnp.max(s, axis=-1, keepdims=True)
            p = jnp.exp(s - m)
            l = jnp.sum(p, axis=-1, keepdims=True)
            o_h.append(_dot(p.astype(BF16), vu[...]) / l)
            lse_h.append(jnp.broadcast_to(m + jnp.log(l), (NA_QB, 128)))
        o_ref[pl.ds(qstart, NA_QB), :] = jnp.where(lane < 64, o_h[0], o_h[1]).astype(BF16)
        lse_ref[0, pl.ds(qstart, NA_QB), :] = jnp.where(lane < 64, lse_h[0], lse_h[1])

        @pl.when(g == 0)
        def _():
            qm_ = q_ref[0:NM, :]
            lane_m = lax.broadcasted_iota(jnp.int32, (NM, 128), 1)
            km, vm = ku[NA_KW:NA_KW + NM, :], vu[NA_KW:NA_KW + NM, :]
            om = []
            for h in range(2):
                hm = (lane_m < 64) if h == 0 else (lane_m >= 64)
                s = _dot(jnp.where(hm, qm_, 0.0).astype(BF16), km, NT) * NA_SCALE
                p = jnp.exp(s - jnp.max(s, axis=-1, keepdims=True))
                l = jnp.sum(p, axis=-1, keepdims=True)
                om.append(_dot(p.astype(BF16), vm) / l)
            o_ref[0:NM, :] = jnp.where(lane_m < 64, om[0], om[1]).astype(BF16)
            o_ref[L:T, :] = jnp.zeros((T - L, 128), BF16)
            lse_ref[0, 0:NM, :] = jnp.zeros((NM, 128), F32)
            lse_ref[0, L:T, :] = jnp.zeros((T - L, 128), F32)

    col = lambda off: pl.BlockSpec((T, 128), lambda hp, g: (0, off + hp))
    return pl.pallas_call(
        body, name="na_fwd", grid=(4, NA_GROUPS),
        in_specs=[col(0), col(4), col(8),
                  pl.BlockSpec((1, 2, NA_QB, NA_KU), lambda hp, g: (_na_var(g), hp, 0, 0))],
        out_specs=(pl.BlockSpec((T, 128), lambda hp, g: (0, hp)), pl.BlockSpec((1, T, 128), lambda hp, g: (hp, 0, 0))),
        out_shape=(_sds((T, 512), BF16), _sds((4, T, 128), F32)),
        scratch_shapes=[pltpu.VMEM((NA_KU, 128), BF16), pltpu.VMEM((NA_KU, 128), BF16)],
        compiler_params=_cp(("parallel", "arbitrary")))(p_act, p_act, p_act, bias_tab)


def _na_bwd(p_act, do, lse, bias_tab):
    def body(q_ref, k_ref, v_ref, do_ref, lse_ref, b_ref, dq_ref, dk_ref, dv_ref, db_ref, ku, vu):
        g = pl.program_id(1)

        @pl.when(g == 0)
        def _():
            dq_ref[...] = jnp.zeros((T, 128), F32)
            dk_ref[...] = jnp.zeros((T, 128), F32)
            dv_ref[...] = jnp.zeros((T, 128), F32)

        kstart = _na_load_window(k_ref, ku, g)
        _na_load_window(v_ref, vu, g)
        qstart = pl.multiple_of(NM + NA_QB * g, 16)
        q = q_ref[pl.ds(qstart, NA_QB), :]
        dov = do_ref[pl.ds(qstart, NA_QB), :]
        lsev = lse_ref[0, pl.ds(qstart, NA_QB), :]
        lane = lax.broadcasted_iota(jnp.int32, (NA_QB, 128), 1)
        first = (g == 0) | (g == 1) | (g == NA_GROUPS - 1)
        dq_h = []
        dku = jnp.zeros((NA_KU, 128), F32)
        dvu = jnp.zeros((NA_KU, 128), F32)
        for h in range(2):
            hm = (lane < 64) if h == 0 else (lane >= 64)
            qm = (jnp.where(hm, q, 0.0) * NA_SCALE).astype(BF16)
            dom = jnp.where(hm, dov, 0.0).astype(BF16)
            s = _dot(qm, ku[...], NT) + b_ref[0, h]
            p = jnp.exp(s - lsev[:, 64 * h:64 * h + 1])
            dp = _dot(dom, vu[...], NT)
            delta = jnp.sum(p * dp, axis=-1, keepdims=True)
            ds = p * (dp - delta)

            @pl.when(first)
            def _():
                db_ref[0, h] = ds

            @pl.when(jnp.logical_not(first))
            def _():
                db_ref[0, h] += ds

            dsb = ds.astype(BF16)
            dq_h.append(_dot(dsb, ku[...]) * NA_SCALE)
            dku = dku + _dot(dsb, qm, TN)
            dvu = dvu + _dot(p.astype(BF16), dom, TN)
        dq_ref[pl.ds(qstart, NA_QB), :] = jnp.where(lane < 64, dq_h[0], dq_h[1])
        dk_ref[pl.ds(kstart, NA_KW), :] += dku[0:NA_KW]
        dv_ref[pl.ds(kstart, NA_KW), :] += dvu[0:NA_KW]
        dk_ref[0:NM, :] += dku[NA_KW:NA_KW + NM]
        dv_ref[0:NM, :] += dvu[NA_KW:NA_KW + NM]

        @pl.when(g == 0)
        def _():
            qm_ = q_ref[0:NM, :]
            dom_ = do_ref[0:NM, :]
            lane_m = lax.broadcasted_iota(jnp.int32, (NM, 128), 1)
            km, vm = ku[NA_KW:NA_KW + NM, :], vu[NA_KW:NA_KW + NM, :]
            dqs = []
            dkm = jnp.zeros((NM, 128), F32)
            dvm = jnp.zeros((NM, 128), F32)
            for h in range(2):
                hm = (lane_m < 64) if h == 0 else (lane_m >= 64)
                qh = jnp.where(hm, qm_, 0.0).astype(BF16)
                doh = jnp.where(hm, dom_, 0.0).astype(BF16)
                s = _dot(qh, km, NT) * NA_SCALE
                e = jnp.exp(s - jnp.max(s, axis=-1, keepdims=True))
                p = e / jnp.sum(e, axis=-1, keepdims=True)
                dp = _dot(doh, vm, NT)
                ds = p * (dp - jnp.sum(p * dp, axis=-1, keepdims=True))
                dsb = (ds * NA_SCALE).astype(BF16)
                dqs.append(_dot(dsb, km))
                dkm = dkm + _dot(dsb, qh, TN)
                dvm = dvm + _dot(p.astype(BF16), doh, TN)
            dq_ref[0:NM, :] = jnp.where(lane_m < 64, dqs[0], dqs[1])
            dk_ref[0:NM, :] += dkm
            dv_ref[0:NM, :] += dvm

    col = lambda off: pl.BlockSpec((T, 128), lambda hp, g: (0, off + hp))
    ocol = pl.BlockSpec((T, 128), lambda hp, g: (0, hp))
    bspec = pl.BlockSpec((1, 2, NA_QB, NA_KU), lambda hp, g: (_na_var(g), hp, 0, 0))
    return pl.pallas_call(
        body, name="na_bwd", grid=(4, NA_GROUPS),
        in_specs=[col(0), col(4), col(8), ocol, pl.BlockSpec((1, T, 128), lambda hp, g: (hp, 0, 0)), bspec],
        out_specs=(ocol, ocol, ocol, bspec),
        out_shape=(_sds((T, 512), F32), _sds((T, 512), F32), _sds((T, 512), F32), _sds((3, NA_HEADS, NA_QB, NA_KU), F32)),
        scratch_shapes=[pltpu.VMEM((NA_KU, 128), BF16), pltpu.VMEM((NA_KU, 128), BF16)],
        compiler_params=_cp(("parallel", "arbitrary")))(p_act, p_act, p_act, do, lse, bias_tab)


def _na_rpb_reduce(dbias):
    def body(db_ref, o_ref):
        lane = lax.broadcasted_iota(jnp.int32, (GRID_W, 128), 1)
        row3 = lax.broadcasted_iota(jnp.int32, (15, GRID_W, 128), 1)
        lane3 = lax.broadcasted_iota(jnp.int32, (15, GRID_W, 128), 2)
        accs = []
        for a in range(15):
            acc = jnp.zeros((GRID_W, 128), F32)
            for var in range(3):
                for i in range(4):
                    for j in range(NA_UROWS):
                        if _na_row_offset(var, i, j) == a:
                            pair = db_ref[var, 0, i * 64:(i + 1) * 64, (j // 2) * 128:(j // 2 + 1) * 128]
                            acc = acc + jnp.where((lane < GRID_W) if j % 2 == 0 else (lane >= GRID_W), pair, 0.0)
            accs.append(acc)
        z = jnp.stack(accs)
        z = jnp.where(lane3 < GRID_W, z + jnp.roll(z, GRID_W, axis=2), 0.0)
        for bit in range(6):
            sh = 1 << bit
            z = jnp.where((row3 & sh) != 0, jnp.roll(z, 128 - sh, axis=2), z)
        z = jnp.roll(z, 15, axis=2)
        o_ref[0] = jnp.sum(z, axis=1)

    return pl.pallas_call(
        body, name="na_rpb_reduce", grid=(NA_HEADS,),
        in_specs=[pl.BlockSpec((3, 1, NA_QB, NA_KU), lambda h: (0, h, 0, 0))],
        out_specs=pl.BlockSpec((1, 15, 128), lambda h: (h, 0, 0)), out_shape=_sds((NA_HEADS, 15, 128), F32),
        compiler_params=_cp(("parallel",)))(dbias)


HG_RB = 128
HG_NB = T // HG_RB
HG_SLOTS = HG_NB * 8
HI = lax.Precision.HIGHEST
HG_UNROLL = 4


def _chunk_tri(lower):
    r = lax.broadcasted_iota(jnp.int32, (HG_RB, HG_RB), 0)
    c = lax.broadcasted_iota(jnp.int32, (HG_RB, HG_RB), 1)
    same = (r // HG_C) == (c // HG_C)
    keep = (c <= r) if lower else (c >= r)
    return jnp.where(same & keep, 1.0, 0.0).astype(F32)


def _hg_gate_terms(z, lg):
    dl = lg[0:1, :] - lg[1:2, :]
    log_lb = jax.nn.log_sigmoid(dl)
    log_1mlb = jax.nn.log_sigmoid(-dl)
    yz = log_1mlb + jax.nn.log_sigmoid(z)
    log_f = jnp.logaddexp(log_lb, yz)
    snz = jax.nn.sigmoid(-z)
    k = jnp.exp(log_1mlb) * snz
    w2 = jnp.exp(yz - log_f)
    return log_f, k, snz, w2


def _hg_pre(p_act, logits):
    def body(q_ref, zf_ref, zb_ref, lg_ref, qh_ref, kf_ref, bf_ref, kb_ref, bb_ref):
        qh_ref[...] = jax.nn.silu(q_ref[...])
        lf, kf, _, _ = _hg_gate_terms(zf_ref[...], lg_ref[0])
        kf_ref[...] = kf
        bf_ref[...] = jnp.dot(_chunk_tri(True), lf, precision=HI, preferred_element_type=F32)
        lb_, kb, _, _ = _hg_gate_terms(zb_ref[...], lg_ref[1])
        kb_ref[...] = kb
        bb_ref[...] = jnp.dot(_chunk_tri(False), lb_, precision=HI, preferred_element_type=F32)

    blk = lambda c: pl.BlockSpec((HG_RB, 512), lambda i: (i, c))
    ob = pl.BlockSpec((HG_RB, 512), lambda i: (i, 0))
    return pl.pallas_call(
        body, name="hg_pre", grid=(HG_NB,),
        in_specs=[blk(3), blk(4), blk(5), pl.BlockSpec((2, 2, 512), lambda i: (0, 0, 0))],
        out_specs=(ob,) * 5, out_shape=(_sds((T, 512), F32),) * 5,
        compiler_params=_cp(("parallel",)))(p_act, p_act, p_act, logits)


def _bdot(a, b, ca, cb):
    return lax.dot_general(a.astype(BF16), b.astype(BF16), (((ca,), (cb,)), ((0,), (0,))), preferred_element_type=F32)


HG_S = 8
HG_NS = HG_RB // HG_S


def _lane_sums(xs):
    l_io = lax.broadcasted_iota(jnp.int32, (HG_NS, HG_S, HG_S), 2)
    a = jnp.zeros((HG_NS, HG_S, HG_S), F32)
    for j, x in enumerate(xs):
        a = a + jnp.where(l_io == j, jnp.sum(x, axis=-1, keepdims=True), 0.0)
    return a


def _halves(x):
    y = x.reshape(8, 2, HG_S, x.shape[-1])
    return y[:, 0], y[:, 1]


def _join(first, second):
    return jnp.stack([first, second], axis=1).reshape(HG_RB, first.shape[-1])


def _cross_split(rev, b4):
    b_1, b_2 = _halves(b4)
    if rev:
        r = b_2[:, 0:1, :]
        return jnp.exp(b_1 - r), jnp.exp(r - b_2)
    r = b_1[:, HG_S - 1:HG_S, :]
    return jnp.exp(b_2 - r), jnp.exp(r - b_1)


def _hg_scan_fwd(qh, k, b, p_act, rev):
    anchor = 0 if rev else HG_C - 1

    def body(q_ref, k_ref, b_ref, v_ref, o_ref, st_ref, dsc):
        def phase_a(blk, _):
            rows = pl.ds(pl.multiple_of(blk * HG_RB, HG_RB), HG_RB)
            b3 = b_ref[rows, :].reshape(8, HG_C, 128)
            k3 = k_ref[rows, :].reshape(8, HG_C, 128)
            v3 = v_ref[rows, :].reshape(8, HG_C, 128)
            bl = b3[:, anchor:anchor + 1, :]
            kt = k3 * jnp.exp(bl - b3)
            st_ref[0, pl.ds(pl.multiple_of(blk * 8, 8), 8)] = _bdot(v3, kt, 1, 1)
            dsc[pl.ds(pl.multiple_of(blk * 8, 8), 8), :] = jnp.exp(bl[:, 0, :])
            return 0

        lax.fori_loop(0, HG_NB, phase_a, 0, unroll=HG_UNROLL)

        def phase_b(n, carry):
            c = (NCHUNK - 1 - n) if rev else n
            u = st_ref[0, c]
            st_ref[0, c] = carry
            return carry * dsc[pl.ds(c, 1), :] + u

        lax.fori_loop(0, NCHUNK // 3, lambda n3, s: phase_b(3 * n3 + 2, phase_b(3 * n3 + 1, phase_b(3 * n3, s))),
                      jnp.zeros((128, 128), F32))
        for c in range(NCHUNK, HG_SLOTS):
            st_ref[0, c] = jnp.zeros((128, 128), F32)

        t_io = lax.broadcasted_iota(jnp.int32, (HG_NS, HG_S, 128), 1)

        def phase_c(blk, _):
            rows = pl.ds(pl.multiple_of(blk * HG_RB, HG_RB), HG_RB)
            b4 = b_ref[rows, :].reshape(HG_NS, HG_S, 128)
            k4 = k_ref[rows, :].reshape(HG_NS, HG_S, 128)
            q4 = q_ref[rows, :].reshape(HG_NS, HG_S, 128)
            v4 = v_ref[rows, :].reshape(HG_NS, HG_S, 128)
            st = st_ref[0, pl.ds(pl.multiple_of(blk * 8, 8), 8)]
            o = _bdot((q4 * jnp.exp(b4)).reshape(8, HG_C, 128), st, 2, 2).reshape(HG_RB, 128)
            terms = []
            for s in range(HG_S):
                ok = (t_io <= s) if rev else (t_io >= s)
                f = jnp.exp(jnp.where(ok, b4 - b4[:, s:s + 1, :], NEG))
                terms.append(q4 * f * k4[:, s:s + 1, :])
            o_in = _bdot(_lane_sums(terms), v4, 2, 1)
            wq, wk = _cross_split(rev, b4)
            q_1, q_2 = _halves(q4)
            k_1, k_2 = _halves(k4)
            v_1, v_2 = _halves(v4)
            o_1, o_2 = _halves(o_in)
            if rev:
                o_1 = o_1 + _bdot(_bdot(q_1 * wq, k_2 * wk, 2, 2), v_2, 2, 1)
            else:
                o_2 = o_2 + _bdot(_bdot(q_2 * wq, k_1 * wk, 2, 2), v_1, 2, 1)
            o_ref[rows, :] = o + _join(o_1, o_2)
            return 0

        lax.fori_loop(0, HG_NB, phase_c, 0, unroll=HG_UNROLL)

    col = pl.BlockSpec((T, 128), lambda h: (0, h))
    return pl.pallas_call(
        body, name="hg_scan_bwd_dir" if rev else "hg_scan_fwd_dir", grid=(HG_HEADS,),
        in_specs=[col, col, col, pl.BlockSpec((T, 128), lambda h: (0, 24 + h))],
        out_specs=(col, pl.BlockSpec((1, HG_SLOTS, 128, 128), lambda h: (h, 0, 0, 0))),
        out_shape=(_sds((T, 512), F32), _sds((HG_HEADS, HG_SLOTS, 128, 128), F32)),
        scratch_shapes=[pltpu.VMEM((HG_SLOTS, 128), F32)],
        compiler_params=_cp(("parallel",), 56))(qh, k, b, p_act)


def _hg_scan_bwd(qh, k, b, p_act, st, do, rev):
    anchor = 0 if rev else HG_C - 1

    def body(q_ref, k_ref, b_ref, v_ref, st_ref, do_ref, dq_ref, dk_ref, db_ref, dv_ref, gst, dsc, dbl):
        def phase_a(blk, _):
            rows = pl.ds(pl.multiple_of(blk * HG_RB, HG_RB), HG_RB)
            b3 = b_ref[rows, :].reshape(8, HG_C, 128)
            q3 = q_ref[rows, :].reshape(8, HG_C, 128)
            do3 = do_ref[rows, :].reshape(8, HG_C, 128)
            gst[pl.ds(pl.multiple_of(blk * 8, 8), 8)] = _bdot(do3, q3 * jnp.exp(b3), 1, 1)
            dsc[pl.ds(pl.multiple_of(blk * 8, 8), 8), :] = jnp.exp(b3[:, anchor, :])
            return 0

        lax.fori_loop(0, HG_NB, phase_a, 0, unroll=HG_UNROLL)

        def phase_b(n, carry):
            c = n if rev else (NCHUNK - 1 - n)
            w = gst[c]
            gst[c] = carry
            dcv = dsc[pl.ds(c, 1), :]
            dbl[pl.ds(c, 1), :] = dcv * jnp.sum(st_ref[0, c] * carry, axis=0, keepdims=True)
            return carry * dcv + w

        lax.fori_loop(0, NCHUNK // 3, lambda n3, s: phase_b(3 * n3 + 2, phase_b(3 * n3 + 1, phase_b(3 * n3, s))),
                      jnp.zeros((128, 128), F32))
        for c in range(NCHUNK, HG_SLOTS):
            gst[c] = jnp.zeros((128, 128), F32)
            dbl[c:c + 1, :] = jnp.zeros((1, 128), F32)

        t_io = lax.broadcasted_iota(jnp.int32, (HG_NS, HG_S, 128), 1)
        t16 = lax.broadcasted_iota(jnp.int32, (8, HG_C, 128), 1)
        r_io = lax.broadcasted_iota(jnp.int32, (HG_NS, HG_S, HG_S), 1)
        l_io = lax.broadcasted_iota(jnp.int32, (HG_NS, HG_S, HG_S), 2)

        def phase_c(blk, _):
            rows = pl.ds(pl.multiple_of(blk * HG_RB, HG_RB), HG_RB)
            cs = pl.ds(pl.multiple_of(blk * 8, 8), 8)
            b4 = b_ref[rows, :].reshape(HG_NS, HG_S, 128)
            k4 = k_ref[rows, :].reshape(HG_NS, HG_S, 128)
            q4 = q_ref[rows, :].reshape(HG_NS, HG_S, 128)
            v4 = v_ref[rows, :].reshape(HG_NS, HG_S, 128)
            do4 = do_ref[rows, :].reshape(HG_NS, HG_S, 128)
            b3, k3, q3 = (z.reshape(8, HG_C, 128) for z in (b4, k4, q4))
            v3, do3 = v4.reshape(8, HG_C, 128), do4.reshape(8, HG_C, 128)
            s_t = st_ref[0, cs]
            g_t = gst[cs]
            bl = b3[:, anchor:anchor + 1, :]
            ekl = jnp.exp(bl - b3)
            kt = k3 * ekl
            dkt = _bdot(v3, g_t, 2, 1)
            dq = (_bdot(do3, s_t, 2, 1) * jnp.exp(b3)).reshape(HG_NS, HG_S, 128)
            dk = (dkt * ekl).reshape(HG_NS, HG_S, 128)
            dv = _bdot(kt, g_t, 2, 2).reshape(HG_NS, HG_S, 128)
            dbl3 = dbl[cs, :].reshape(8, 1, 128) + jnp.sum(dkt * kt, axis=1, keepdims=True)
            causal = (l_io >= r_io) if rev else (l_io <= r_io)
            da = jnp.where(causal, _bdot(do4, v4, 2, 2), 0.0)
            causal_t = (l_io <= r_io) if rev else (l_io >= r_io)
            dat = jnp.where(causal_t, _bdot(v4, do4, 2, 2), 0.0)
            for s in range(HG_S):
                ok = (t_io <= s) if rev else (t_io >= s)
                f = jnp.exp(jnp.where(ok, b4 - b4[:, s:s + 1, :], NEG))
                dq = dq + da[:, :, s:s + 1] * (f * k4[:, s:s + 1, :])
            terms = []
            for t in range(HG_S):
                ok = (t_io >= t) if rev else (t_io <= t)
                e = jnp.exp(jnp.where(ok, b4[:, t:t + 1, :] - b4, NEG))
                eq = e * q4[:, t:t + 1, :]
                dk = dk + dat[:, :, t:t + 1] * eq
                terms.append(eq * k4)
            dv = dv + _bdot(_lane_sums(terms), do4, 2, 1)
            wq, wk = _cross_split(rev, b4)
            pick = (lambda z: _halves(z)) if rev else (lambda z: _halves(z)[::-1])
            (q_q, _), (_, k_k), (_, v_k), (do_q, _) = pick(q4), pick(k4), pick(v4), pick(do4)
            qx, kx = q_q * wq, k_k * wk
            dq_q = _bdot(_bdot(do_q, v_k, 2, 2), kx, 2, 1) * wq
            dk_k = _bdot(_bdot(v_k, do_q, 2, 2), qx, 2, 1) * wk
            dv_k = _bdot(_bdot(kx, qx, 2, 2), do_q, 2, 1)
            zero = jnp.zeros((8, HG_S, 128), F32)
            place_q = (lambda z: _join(z, zero)) if rev else (lambda z: _join(zero, z))
            place_k = (lambda z: _join(zero, z)) if rev else (lambda z: _join(z, zero))
            dq2 = dq.reshape(HG_RB, 128) + place_q(dq_q)
            dk2 = dk.reshape(HG_RB, 128) + place_k(dk_k)
            dv2 = dv.reshape(HG_RB, 128) + place_k(dv_k)
            dq3, dk3 = dq2.reshape(8, HG_C, 128), dk2.reshape(8, HG_C, 128)
            db = q3 * dq3 - k3 * dk3 + jnp.where(t16 == anchor, dbl3, 0.0)
            dq_ref[rows, :] = dq2
            dk_ref[rows, :] = dk2
            db_ref[rows, :] = db.reshape(HG_RB, 128)
            dv_ref[rows, :] = dv2
            return 0

        lax.fori_loop(0, HG_NB, phase_c, 0, unroll=HG_UNROLL)

    col = pl.BlockSpec((T, 128), lambda h: (0, h))
    return pl.pallas_call(
        body, name="hg_scan_bwd_dir_bwd" if rev else "hg_scan_fwd_dir_bwd", grid=(HG_HEADS,),
        in_specs=[col, col, col, pl.BlockSpec((T, 128), lambda h: (0, 24 + h)),
                  pl.BlockSpec((1, HG_SLOTS, 128, 128), lambda h: (h, 0, 0, 0)), col],
        out_specs=(col,) * 4, out_shape=(_sds((T, 512), F32),) * 4,
        scratch_shapes=[pltpu.VMEM((HG_SLOTS, 128, 128), F32), pltpu.VMEM((HG_SLOTS, 128), F32),
                        pltpu.VMEM((HG_SLOTS, 128), F32)],
        compiler_params=_cp(("parallel",), 56))(qh, k, b, p_act, st, do)


def _row_valid(i, tm):
    r = lax.broadcasted_iota(jnp.int32, (tm, 1), 0) + i * tm
    return r < L


def _hg_post_rows(o, gv, gain_v, valid):
    parts = []
    for h in range(HG_HEADS):
        oh = o[:, 128 * h:128 * (h + 1)]
        parts.append(oh * lax.rsqrt(jnp.mean(oh * oh, axis=-1, keepdims=True) + EPS))
    return jnp.where(valid, jnp.concatenate(parts, axis=1) * gain_v * jax.nn.silu(gv), 0.0)


def _hg_post_bwd_rows(du, o, gv, gain_v, valid):
    duv = jnp.where(valid, du, 0.0)
    sig = jax.nn.sigmoid(gv)
    sg = gv * sig
    dn = duv * gain_v * sg
    do_parts, n_parts = [], []
    for h in range(HG_HEADS):
        sl = slice(128 * h, 128 * (h + 1))
        oh = o[:, sl]
        r = lax.rsqrt(jnp.mean(oh * oh, axis=-1, keepdims=True) + EPS)
        nh = oh * r
        dnh = dn[:, sl]
        do_parts.append(r * (dnh - nh * jnp.mean(dnh * nh, axis=-1, keepdims=True)))
        n_parts.append(nh)
    n = jnp.where(valid, jnp.concatenate(n_parts, axis=1), 0.0)
    do = jnp.where(valid, jnp.concatenate(do_parts, axis=1), 0.0)
    dg = duv * n * gain_v * (sig * (1.0 + gv * (1.0 - sig)))
    return do, dg, jnp.sum(duv * n * sg, axis=0, keepdims=True)


def _hg_pre_bwd(p_act, logits, dq_f, dq_b, dk_f, dk_b, db_f, db_b, dv_f, dv_b):
    def body(q_ref, zf_ref, zb_ref, lg_ref, dqf_ref, dqb_ref, dkf_ref, dkb_ref, dbf_ref, dbb_ref, dvf_ref, dvb_ref,
             dq_ref, dzf_ref, dzb_ref, di_ref, dlg_ref):
        i = pl.program_id(0)
        valid = _row_valid(i, HG_RB)
        qv = q_ref[...]
        sig = jax.nn.sigmoid(qv)
        dq_ref[...] = jnp.where(valid, (dqf_ref[...] + dqb_ref[...]) * (sig * (1.0 + qv * (1.0 - sig))), 0.0).astype(BF16)
        di_ref[...] = jnp.where(valid, dvf_ref[...] + dvb_ref[...], 0.0).astype(BF16)
        for d, (z_ref, dk_r, db_r, dz_ref) in enumerate(((zf_ref, dkf_ref, dbf_ref, dzf_ref), (zb_ref, dkb_ref, dbb_ref, dzb_ref))):
            lg = lg_ref[d]
            dl = lg[0:1, :] - lg[1:2, :]
            lb = jax.nn.sigmoid(dl)
            one_m_lb = jax.nn.sigmoid(-dl)
            log_f, _, snz, w2 = _hg_gate_terms(z_ref[...], lg)
            dbv = jnp.where(valid, db_r[...], 0.0)
            dkv = jnp.where(valid, dk_r[...], 0.0)
            dlf = jnp.dot(_chunk_tri(d == 1), dbv, precision=HI, preferred_element_type=F32)
            sz = 1.0 - snz
            dz_ref[...] = (dlf * w2 * snz - dkv * one_m_lb * sz * snz).astype(BF16)
            dlb = jnp.sum(dlf * snz * jnp.exp(-log_f) - dkv * snz, axis=0, keepdims=True)
            dl0 = dlb * lb * one_m_lb
            part = jnp.concatenate([dl0, -dl0], axis=0)

            @pl.when(i == 0)
            def _():
                dlg_ref[d] = part

            @pl.when(i > 0)
            def _():
                dlg_ref[d] += part

    blk = lambda c: pl.BlockSpec((HG_RB, 512), lambda i: (i, c))
    ob = pl.BlockSpec((HG_RB, 512), lambda i: (i, 0))
    lgs = pl.BlockSpec((2, 2, 512), lambda i: (0, 0, 0))
    return pl.pallas_call(
        body, name="hg_pre_bwd", grid=(HG_NB,),
        in_specs=[blk(3), blk(4), blk(5), lgs] + [ob] * 8,
        out_specs=(ob, ob, ob, ob, lgs),
        out_shape=(_sds((T, 512), BF16),) * 4 + (_sds((2, 2, 512), F32),),
        compiler_params=_cp(("arbitrary",)))(p_act, p_act, p_act, logits, dq_f, dq_b, dk_f, dk_b, db_f, db_b, dv_f, dv_b)


def _mix_fwd(o_na, o_f, o_b, gain, w_na, w_hg, p_act):
    def body(ona_ref, of_ref, ob_ref, g_ref, gain_ref, wna_ref, whg_ref, gna_ref, ghg_ref, o_ref, u_ref):
        u = _hg_post_rows(of_ref[...] + ob_ref[...], g_ref[...], gain_ref[...], _row_valid(pl.program_id(0), TM_B)).astype(BF16)
        u_ref[...] = u
        y_na = _dot(ona_ref[...], wna_ref[...])
        y_hg = _dot(u, whg_ref[...])
        o_ref[...] = (jax.nn.sigmoid(gna_ref[...]) * y_na + jax.nn.sigmoid(ghg_ref[...]) * y_hg).astype(BF16)

    act = pl.BlockSpec((TM_B, 512), lambda i: (i, 0))
    wsp = pl.BlockSpec((512, D), lambda i: (0, 0))
    return pl.pallas_call(
        body, name="mix_fwd", grid=(T // TM_B,),
        in_specs=[act, act, act, pl.BlockSpec((TM_B, 512), lambda i: (i, 7)), pl.BlockSpec((1, 512), lambda i: (0, 0)),
                  wsp, wsp, pl.BlockSpec((TM_B, D), lambda i: (i, 4)), pl.BlockSpec((TM_B, D), lambda i: (i, 5))],
        out_specs=(pl.BlockSpec((TM_B, D), lambda i: (i, 0)), act), out_shape=(_sds((T, D), BF16), _sds((T, 512), BF16)),
        compiler_params=_cp(("parallel",)))(o_na, o_f, o_b, p_act, gain, w_na, w_hg, p_act, p_act)


def _mix_bwd(o_na, u_hg, o_f, o_b, gain, w_na, w_hg, p_act, dmix):
    ni = T // TM_B

    def body(ona_ref, uhg_ref, of_ref, ob_ref, g_ref, gain_ref, wna_ref, whg_ref, gna_ref, ghg_ref, dmix_ref,
             dgna_ref, dghg_ref, dwna_ref, dwhg_ref, dona_ref, do_ref, dg_ref, dgain_ref, acc_na, acc_hg):
        i = pl.program_id(0)
        dm = dmix_ref[...].astype(F32)
        dxs = []
        for x_ref, w_ref, gt_ref, dgt_ref, dw_ref, acc in (
                (ona_ref, wna_ref, gna_ref, dgna_ref, dwna_ref, acc_na), (uhg_ref, whg_ref, ghg_ref, dghg_ref, dwhg_ref, acc_hg)):
            xv = x_ref[...]
            y = _dot(xv, w_ref[...])
            sg = jax.nn.sigmoid(gt_ref[...])
            dgt_ref[...] = (dm * y * sg * (1.0 - sg)).astype(BF16)
            dy = (dm * sg).astype(BF16)
            dxs.append(_dot(dy, w_ref[...], NT))
            part = _dot(xv, dy, TN)

            @pl.when(i == 0)
            def _():
                acc[...] = part

            @pl.when(i > 0)
            def _():
                acc[...] += part

            @pl.when(i == ni - 1)
            def _():
                dw_ref[...] = acc[...].astype(BF16)

        dona_ref[...] = dxs[0]
        do, dg, gpart = _hg_post_bwd_rows(dxs[1], of_ref[...] + ob_ref[...], g_ref[...], gain_ref[...], _row_valid(i, TM_B))
        do_ref[...] = do
        dg_ref[...] = dg.astype(BF16)

        @pl.when(i == 0)
        def _():
            dgain_ref[...] = gpart

        @pl.when(i > 0)
        def _():
            dgain_ref[...] += gpart

    act = pl.BlockSpec((TM_B, 512), lambda i: (i, 0))
    wsp = pl.BlockSpec((512, D), lambda i: (0, 0))
    rblk = pl.BlockSpec((TM_B, D), lambda i: (i, 0))
    vec = pl.BlockSpec((1, 512), lambda i: (0, 0))
    return pl.pallas_call(
        body, name="mix_bwd", grid=(ni,),
        in_specs=[act, act, act, act, pl.BlockSpec((TM_B, 512), lambda i: (i, 7)), vec, wsp, wsp,
                  pl.BlockSpec((TM_B, D), lambda i: (i, 4)), pl.BlockSpec((TM_B, D), lambda i: (i, 5)), rblk],
        out_specs=(rblk, rblk, wsp, wsp, act, act, act, vec),
        out_shape=(_sds((T, D), BF16), _sds((T, D), BF16), _sds((512, D), BF16), _sds((512, D), BF16),
                   _sds((T, 512), F32), _sds((T, 512), F32), _sds((T, 512), BF16), _sds((1, 512), F32)),
        scratch_shapes=[pltpu.VMEM((512, D), F32), pltpu.VMEM((512, D), F32)],
        compiler_params=_cp(("arbitrary",)))(o_na, u_hg, o_f, o_b, p_act, gain, w_na, w_hg, p_act, p_act, dmix)


def _wo_fwd(mix, w_o, h0, g_mlp):
    def body(mix_ref, w_ref, h0_ref, g_ref, h1_ref, m_ref):
        h1 = h0_ref[...] + _dot(mix_ref[...], w_ref[...])
        h1_ref[...] = h1
        r = lax.rsqrt(jnp.mean(h1 * h1, axis=-1, keepdims=True) + EPS)
        m_ref[...] = (h1 * r * g_ref[...]).astype(BF16)

    blk = pl.BlockSpec((TM_B, D), lambda i: (i, 0))
    return pl.pallas_call(
        body, name="wo_fwd", grid=(T // TM_B,),
        in_specs=[blk, pl.BlockSpec((D, D), lambda i: (0, 0)), blk, pl.BlockSpec((1, D), lambda i: (0, 0))],
        out_specs=(blk, blk), out_shape=(_sds((T, D), F32), _sds((T, D), BF16)),
        compiler_params=_cp(("parallel",)))(mix, w_o, h0, g_mlp)


def _wo_bwd(dh1_b, w_o, mix):
    ni = T // TM_B

    def body(dh_ref, w_ref, mix_ref, dmix_ref, dw_ref, acc):
        i = pl.program_id(0)
        dh = dh_ref[...]
        dmix_ref[...] = _dot(dh, w_ref[...], NT).astype(BF16)
        part = _dot(mix_ref[...], dh, TN)

        @pl.when(i == 0)
        def _():
            acc[...] = part

        @pl.when(i > 0)
        def _():
            acc[...] += part

        @pl.when(i == ni - 1)
        def _():
            dw_ref[...] = acc[...].astype(BF16)

    blk = pl.BlockSpec((TM_B, D), lambda i: (i, 0))
    wsp = pl.BlockSpec((D, D), lambda i: (0, 0))
    return pl.pallas_call(
        body, name="wo_bwd", grid=(ni,), in_specs=[blk, wsp, blk], out_specs=(blk, wsp),
        out_shape=(_sds((T, D), BF16), _sds((D, D), BF16)), scratch_shapes=[pltpu.VMEM((D, D), F32)],
        compiler_params=_cp(("arbitrary",)))(dh1_b, w_o, mix)


FF_B = D_FF // NDEV


def _loss_rows(xv, gv, tv, row0):
    r_io = lax.broadcasted_iota(jnp.int32, (xv.shape[0], 1), 0) + row0
    valid = (r_io >= NM) & (r_io < L)
    r = lax.rsqrt(jnp.mean(xv * xv, axis=-1, keepdims=True) + EPS)
    xh = xv * r
    err = jnp.where(valid, xh * gv - tv, 0.0)
    lpart = 0.5 * jnp.sum(jnp.sum(err * err, axis=-1, keepdims=True) * (1.0 / D), axis=0, keepdims=True)
    dy = err * (1.0 / D)
    dxh = dy * gv
    dh = r * (dxh - xh * jnp.mean(dxh * xh, axis=-1, keepdims=True))
    return lpart, dh, jnp.sum(dy * xh, axis=0, keepdims=True)


def _mlp_fwd_loss(m, wup_g, wdown_g, h1, g_final, tgt):
    nsub = TM_MM // TM_E

    def body(m_ref, wu_ref, wd_ref, h1_ref, g_ref, t_ref, loss_ref, dh_ref, dhb_ref, dg_ref, h2):
        i, j = pl.program_id(0), pl.program_id(1)
        up = jnp.maximum(_dot(m_ref[...], wu_ref[0]), 0.0)
        part = _dot((up * up).astype(BF16), wd_ref[0])

        @pl.when(j == 0)
        def _():
            h2[...] = h1_ref[...] + part

        @pl.when(j > 0)
        def _():
            h2[...] += part

        @pl.when(j == NDEV - 1)
        def _():
            lsum = jnp.zeros((1, 1), F32)
            gsum = jnp.zeros((1, D), F32)
            for s in range(nsub):
                rows = slice(s * TM_E, (s + 1) * TM_E)
                lpart, dh, gpart = _loss_rows(h2[rows, :], g_ref[...], t_ref[rows, :], i * TM_MM + s * TM_E)
                dh_ref[rows, :] = dh
                dhb_ref[rows, :] = dh.astype(BF16)
                lsum = lsum + lpart
                gsum = gsum + gpart
            lsum = jnp.broadcast_to(lsum, (1, 128))

            @pl.when(i == 0)
            def _():
                loss_ref[...] = lsum
                dg_ref[...] = gsum

            @pl.when(i > 0)
            def _():
                loss_ref[...] += lsum
                dg_ref[...] += gsum

    blk = pl.BlockSpec((TM_MM, D), lambda i, j: (i, 0))
    vec = pl.BlockSpec((1, D), lambda i, j: (0, 0))
    return pl.pallas_call(
        body, name="mlp_fwd_loss", grid=(T // TM_MM, NDEV),
        in_specs=[blk, pl.BlockSpec((1, D, FF_B), lambda i, j: (j, 0, 0)), pl.BlockSpec((1, FF_B, D), lambda i, j: (j, 0, 0)),
                  blk, vec, blk],
        out_specs=(pl.BlockSpec((1, 128), lambda i, j: (0, 0)), blk, blk, vec),
        out_shape=(_sds((1, 128), F32), _sds((T, D), F32), _sds((T, D), BF16), _sds((1, D), F32)),
        scratch_shapes=[pltpu.VMEM((TM_MM, D), F32)],
        compiler_params=_cp(("arbitrary", "arbitrary"), 56))(m, wup_g, wdown_g, h1, g_final, tgt)


def _mlp_bwd(m, dh2_b, wup_g, wdown_g, h1, g_mlp, dh2):
    ni = T // TM_B
    nsub = TM_B // TM_E

    def body(m_ref, dh_ref, wu_ref, wd_ref, h1_ref, g_ref, dres_ref, dwu_ref, dwd_ref, dh1_ref, dh1b_ref, dg_ref,
             dm_ref, acc_u, acc_d):
        j, i = pl.program_id(0), pl.program_id(1)
        rows = pl.ds(pl.multiple_of(i * TM_B, TM_B), TM_B)
        mv, dh = m_ref[...], dh_ref[...]
        r = jnp.maximum(_dot(mv, wu_ref[0]), 0.0)
        act = (r * r).astype(BF16)
        dact = _dot(dh, wd_ref[0], NT)
        dup = (dact * (2.0 * r)).astype(BF16)
        pd = _dot(act, dh, TN)
        pu = _dot(mv, dup, TN)
        dmv = _dot(dup, wu_ref[0], NT)

        @pl.when(i == 0)
        def _():
            acc_u[...] = pu
            acc_d[...] = pd

        @pl.when(i > 0)
        def _():
            acc_u[...] += pu
            acc_d[...] += pd

        @pl.when(i == ni - 1)
        def _():
            dwu_ref[0] = acc_u[...].astype(BF16)
            dwd_ref[0] = acc_d[...].astype(BF16)

        @pl.when(j == 0)
        def _():
            dm_ref[rows, :] = dmv

        @pl.when(j > 0)
        def _():
            dm_ref[rows, :] += dmv

        @pl.when(j == NDEV - 1)
        def _():
            gsum = jnp.zeros((1, D), F32)
            for s in range(nsub):
                sub = slice(s * TM_E, (s + 1) * TM_E)
                dm_rows = dm_ref[pl.ds(pl.multiple_of(i * TM_B + s * TM_E, TM_E), TM_E), :]
                dx, gpart = _norm_bwd_rows(h1_ref[sub, :], g_ref[...], dm_rows, dres_ref[sub, :])
                dh1_ref[sub, :] = dx
                dh1b_ref[sub, :] = dx.astype(BF16)
                gsum = gsum + gpart

            @pl.when(i == 0)
            def _():
                dg_ref[...] = gsum

            @pl.when(i > 0)
            def _():
                dg_ref[...] += gsum

    blk = pl.BlockSpec((TM_B, D), lambda j, i: (i, 0))
    late = pl.BlockSpec((TM_B, D), lambda j, i: (jnp.where(j == NDEV - 1, i, 0), 0))
    vec = pl.BlockSpec((1, D), lambda j, i: (0, 0))
    wus = pl.BlockSpec((1, D, FF_B), lambda j, i: (j, 0, 0))
    wds = pl.BlockSpec((1, FF_B, D), lambda j, i: (j, 0, 0))
    return pl.pallas_call(
        body, name="mlp_bwd", grid=(NDEV, ni), in_specs=[blk, blk, wus, wds, late, vec, late],
        out_specs=(wus, wds, late, late, vec),
        out_shape=(_sds((NDEV, D, FF_B), BF16), _sds((NDEV, FF_B, D), BF16), _sds((T, D), F32), _sds((T, D), BF16),
                   _sds((1, D), F32)),
        scratch_shapes=[pltpu.VMEM((T, D), F32), pltpu.VMEM((D, FF_B), F32), pltpu.VMEM((FF_B, D), F32)],
        compiler_params=_cp(("arbitrary", "arbitrary"), 56))(m, dh2_b, wup_g, wdown_g, h1, g_mlp, dh2)


def _adamw(parts, w, m, v, name):
    rr, cc = w.shape
    tr = rr
    for cand in (256, 128, 64):
        if rr % cand == 0 and rr > cand:
            tr = cand
            break
    c1 = 1.0 - ADAM_B1 ** ADAM_STEP
    c2 = 1.0 - ADAM_B2 ** ADAM_STEP

    def body(p_ref, w_ref, m_ref, v_ref, g_ref, d_ref, nm_ref, nv_ref):
        g = p_ref[0].astype(F32)
        for s in range(1, NDEV):
            g = g + p_ref[s].astype(F32)
        mn = ADAM_B1 * m_ref[...] + (1.0 - ADAM_B1) * g
        vn = ADAM_B2 * v_ref[...] + (1.0 - ADAM_B2) * (g * g)
        g_ref[...] = g
        nm_ref[...] = mn
        nv_ref[...] = vn
        d_ref[...] = -ADAM_LR * ((mn / c1) / (jnp.sqrt(vn / c2) + ADAM_EPS) + ADAM_WD * w_ref[...])

    blk = pl.BlockSpec((tr, cc), lambda i: (i, 0))
    return pl.pallas_call(
        body, name=name, grid=(rr // tr,),
        in_specs=[pl.BlockSpec((NDEV, tr, cc), lambda i: (0, i, 0)), blk, blk, blk],
        out_specs=(blk,) * 4, out_shape=(_sds((rr, cc), F32),) * 4,
        compiler_params=_cp(("parallel",)))(parts, w, m, v)


RPB_N = NA_HEADS * 15 * 31
RPB_PAD = 4096
OWN_ROWS = NM + 8


def _pad_rows(a, rows):
    return jnp.pad(a, ((0, rows - a.shape[0]),) + ((0, 0),) * (a.ndim - 1))


def _pack_owned(meta_blk, lb_blk):
    return jnp.concatenate([meta_blk, _pad_rows(lb_blk.reshape(2, 128), 8)], axis=0)


LOSS_ROW = 28


def _pack_replicated(n_mix, n_mlp, n_final, hg_gain, rpb, loss_row=None):
    flat = _pad_rows(rpb.reshape(RPB_N), RPB_PAD)
    gain8 = _pad_rows(hg_gain.reshape(4, 128), 8)
    if loss_row is not None:
        gain8 = gain8 + jnp.pad(loss_row, ((LOSS_ROW - 24, 31 - LOSS_ROW), (0, 0)))
    return jnp.concatenate([n_mix.reshape(8, 128), n_mlp.reshape(8, 128), n_final.reshape(8, 128), gain8,
                            flat.reshape(32, 128)], axis=0)


def _unpack_replicated(a):
    return (a[0:8].reshape(1, D), a[8:16].reshape(1, D), a[16:24].reshape(D), a[24:28].reshape(1, 512),
            a[32:64].reshape(RPB_PAD)[:RPB_N].reshape(1, NA_HEADS, 15, 31))


def kernel(x, meta_tokens, w_in, w_na_out, w_hg_out, w_o, w_up, w_down, norm_mix, norm_mlp, norm_final, hg_norm, na_rpb, hg_lb_logits, loss_target, m_meta_tokens, m_w_in, m_w_na_out, m_w_hg_out, m_w_o, m_w_up, m_w_down, m_norm_mix, m_norm_mlp, m_norm_final, m_hg_norm, m_na_rpb, m_hg_lb_logits, v_meta_tokens, v_w_in, v_w_na_out, v_w_hg_out, v_w_o, v_w_up, v_w_down, v_norm_mix, v_norm_mlp, v_norm_final, v_hg_norm, v_na_rpb, v_hg_lb_logits):
    owned = _pack_owned(meta_tokens, hg_lb_logits)
    first, tok = _exchange_start([w_in[0].astype(BF16), owned], [False] * 2, "gather_first_start", SAME_CORE_AND_SIBLING)
    bias_tab = _na_bias_table(_tie(jnp.pad(na_rpb[0], ((0, 0), (0, 0), (0, 128 - 31))), tok, "tie_bias_table"))
    later = [w[0].astype(BF16) for w in (w_na_out, w_hg_out, w_o, w_up, w_down)]
    h0_rows = jnp.concatenate([jnp.zeros((NM, D), F32), x[0], jnp.zeros((T - L, D), F32)], axis=0)
    tgt = jnp.concatenate([jnp.zeros((NM, D), F32), loss_target[0], jnp.zeros((T - L, D), F32)], axis=0)
    first = _exchange_wait(first, [False] * 2, [bias_tab, h0_rows, tgt] + later, "gather_first_wait", SAME_CORE_AND_SIBLING)
    win_g, owned_g = _forward_to_sibling(first, "gather_first_forward")
    later[0] = _tie(later[0], owned_g, "tie_gather_rest")
    gather_rest, tok = _exchange_start(later, [False] * 5, "gather_rest_start")
    win_g = _tie(win_g, tok, "tie_inproj")
    meta_full = jnp.transpose(owned_g[:, 0:NM, :], (1, 0, 2)).reshape(NM, D)
    logits = jnp.transpose(owned_g[:, NM:NM + 2, :].reshape(NDEV, 2, 2, 64), (1, 2, 0, 3)).reshape(2, 2, 512)

    h0 = lax.dynamic_update_slice(h0_rows, meta_full, (0, 0))

    a, a_t = _norm_fwd_t(h0, norm_mix, "norm_mix_fwd")
    p_act = _inproj_fwd(a, win_g)
    o_na, lse = _na_fwd(p_act, bias_tab)
    qh, k_f, b_f, k_b, b_b = _hg_pre(p_act, logits)
    o_f, st_f = _hg_scan_fwd(qh, k_f, b_f, p_act, False)
    o_b, st_b = _hg_scan_fwd(qh, k_b, b_b, p_act, True)
    wna_g, whg_g, wo_g, wup_g, wdown_g = _exchange_wait(gather_rest, [False] * 5, [o_f, o_b, o_na], "gather_rest_wait")
    w_o_full = wo_g.reshape(D, D)
    w_na_full = jnp.transpose(wna_g, (1, 0, 2)).reshape(512, D)
    w_hg_full = jnp.transpose(whg_g, (1, 0, 2)).reshape(512, D)
    mix, u_hg = _mix_fwd(o_na, o_f, o_b, hg_norm, w_na_full, w_hg_full, p_act)
    h1, m_act = _wo_fwd(mix, w_o_full, h0, norm_mlp)
    loss_part, dh2, dh2_b, d_nfinal = _mlp_fwd_loss(m_act, wup_g, wdown_g, h1, norm_final.reshape(1, D), tgt)

    dwup_p, dwdown_p, dh1, dh1_b, d_nmlp = _mlp_bwd(m_act, dh2_b, wup_g, wdown_g, h1, norm_mlp, dh2)
    sc_mlp, tok = _exchange_start([dwup_p, dwdown_p], [True] * 2, "scatter_mlp_start")
    dmix, dwo = _wo_bwd(_tie(dh1_b, tok, "tie_wo_bwd"), w_o_full, mix)
    sc_wo, tok = _exchange_start([dwo.reshape(NDEV, D // NDEV, D)], [True], "scatter_wo_start")
    dgna, dghg, dwna, dwhg, do_na, do_hg, dg_hg, d_gain = _mix_bwd(
        o_na, u_hg, o_f, o_b, hg_norm, w_na_full, w_hg_full, p_act, _tie(dmix, tok, "tie_mix_bwd"))
    owner_cols = lambda w: jnp.transpose(w.reshape(512, NDEV, D // NDEV), (1, 0, 2))
    sc_br, tok = _exchange_start([owner_cols(dwna), owner_cols(dwhg)], [True] * 2, "scatter_branch_start")
    do_hg = _tie(do_hg, tok, "tie_hg_scan_bwd")
    dq_f, dk_f, db_f, dv_f = _hg_scan_bwd(qh, k_f, b_f, p_act, st_f, do_hg, False)
    dq_b, dk_b, db_b, dv_b = _hg_scan_bwd(qh, k_b, b_b, p_act, st_b, do_hg, True)
    dq_hg, dz_f, dz_b, di_hg, d_logits = _hg_pre_bwd(p_act, logits, dq_f, dq_b, dk_f, dk_b, db_f, db_b, dv_f, dv_b)
    dq_na, dk_na, dv_na, dbias = _na_bwd(p_act, do_na, lse, bias_tab)
    dp = jnp.concatenate([dq_na.astype(BF16), dk_na.astype(BF16), dv_na.astype(BF16), dq_hg, dz_f, dz_b, di_hg, dg_hg,
                          dgna, dghg], axis=1)
    dwin_p = _inproj_bwd_dw(a_t, dp)
    sc_in, tok = _exchange_start([dwin_p], [True], "scatter_in_start")
    dh0, d_nmix = _inproj_bwd_da(_tie(dp, tok, "tie_inproj_bwd_da"), win_g, h0, norm_mix, dh1)
    d_rpb = _na_rpb_reduce(_tie(dbias, tok, "tie_rpb_reduce"))[:, :, :31]

    res = {}

    def update(nm, parts, w, mm, vv):
        res[nm] = [r[None] for r in _adamw(parts, w[0], mm[0], vv[0], "adamw_" + nm)]
        return res[nm][1]

    wup_r, wdown_r = _exchange_wait(sc_mlp, [True] * 2, [dh0, d_rpb], "scatter_mlp_wait")
    update("w_up", wup_r, w_up, m_w_up, v_w_up)
    last = update("w_down", wdown_r, w_down, m_w_down, v_w_down)
    (wo_r,) = _exchange_wait(sc_wo, [True], [last], "scatter_wo_wait")
    last = update("w_o", wo_r, w_o, m_w_o, v_w_o)
    wna_r, whg_r = _exchange_wait(sc_br, [True] * 2, [last], "scatter_branch_wait")
    update("w_na_out", wna_r, w_na_out, m_w_na_out, v_w_na_out)
    last = update("w_hg_out", whg_r, w_hg_out, m_w_hg_out, v_w_hg_out)

    d_meta = jnp.transpose(dh0[0:NM].reshape(NM, NDEV, 128), (1, 0, 2))
    d_lg = jnp.transpose(d_logits.reshape(2, 2, NDEV, 64), (2, 0, 1, 3)).reshape(NDEV, 2, 128)
    owned_p = jnp.concatenate([d_meta, jnp.pad(d_lg, ((0, 0), (0, OWN_ROWS - NM - 2), (0, 0)))], axis=1)
    repl_p = _pack_replicated(d_nmix, d_nmlp, d_nfinal, d_gain, d_rpb, loss_part)
    grad_x = dh0[NM:L][None]
    done_first = [grad_x] + [res[nm][0] for nm in ("w_up", "w_down", "w_o", "w_na_out", "w_hg_out")]
    owned_r, repl_r = _exchange([owned_p, repl_p], [True, False], "scatter_small", done_first)
    own = _adamw(owned_r, owned, _pack_owned(m_meta_tokens, m_hg_lb_logits), _pack_owned(v_meta_tokens, v_hg_lb_logits),
                 "adamw_owned_small")
    res["meta_tokens"] = [r[0:NM] for r in own]
    res["hg_lb_logits"] = [r[NM:NM + 2].reshape(2, 2, 64) for r in own]
    rep = _adamw(repl_r, _pack_replicated(norm_mix, norm_mlp, norm_final, hg_norm, na_rpb),
                 _pack_replicated(m_norm_mix, m_norm_mlp, m_norm_final, m_hg_norm, m_na_rpb),
                 _pack_replicated(v_norm_mix, v_norm_mlp, v_norm_final, v_hg_norm, v_na_rpb), "adamw_replicated")
    for q in range(4):
        um = _unpack_replicated(rep[q])
        for nm, val in zip(("norm_mix", "norm_mlp", "norm_final", "hg_norm", "na_rpb"), um):
            res.setdefault(nm, [None] * 4)[q] = val
    (win_r,) = _exchange_wait(sc_in, [True], [rep[1], own[1]], "scatter_in_wait")
    update("w_in", win_r, w_in, m_w_in, v_w_in)

    loss = jnp.sum(repl_r[:, LOSS_ROW, 0])
    order = ("meta_tokens", "w_in", "w_na_out", "w_hg_out", "w_o", "w_up", "w_down", "norm_mix", "norm_mlp", "norm_final",
             "hg_norm", "na_rpb", "hg_lb_logits")
    outs = [loss, grad_x]
    for q in range(4):
        outs += [res[nm][q] for nm in order]
    return tuple(outs)
```

```python
import functools

import numpy as np
import jax
import jax.numpy as jnp
from jax import lax
from jax.experimental import pallas as pl
from jax.experimental.pallas import tpu as pltpu

F32 = jnp.float32
BF16 = jnp.bfloat16

D = 1024
SEQ = 2048
NM = 16
L = SEQ + NM
T = 2176
NDEV = 8
EPS = 1e-6
GRID_W = 64
ROWS = SEQ // GRID_W
NA_HEADS = 8
NA_DH = 64
NA_SCALE = NA_DH ** -0.5
HG_HEADS = 4
HG_C = 16
NCHUNK = L // HG_C
D_FF = 4096
IN_COLS = 6144
NEG = -1e30

ADAM_LR = 0.001
ADAM_B1 = 0.9
ADAM_B2 = 0.999
ADAM_EPS = 1e-08
ADAM_WD = 0.01
ADAM_STEP = 10

MESH_ID = pl.DeviceIdType.MESH
ANY = pl.BlockSpec(memory_space=pl.ANY)

NN = (((1,), (0,)), ((), ()))
NT = (((1,), (1,)), ((), ()))
TN = (((0,), (0,)), ((), ()))


def _cp(sem=None, vmem_mb=48):
    return pltpu.CompilerParams(dimension_semantics=sem, vmem_limit_bytes=vmem_mb * 1024 * 1024)


def _dot(a, b, dims=NN):
    return lax.dot_general(a, b, dims, preferred_element_type=F32)


def _sds(shape, dtype):
    return jax.ShapeDtypeStruct(shape, dtype)


HBM = pl.BlockSpec(memory_space=pltpu.HBM)
SEM = pl.BlockSpec(memory_space=pltpu.SEMAPHORE)
EFFECT = pltpu.SideEffectType.DATAFLOW_SIDE_EFFECTING


def _exchange(arrs, scatter, name, after=()):
    n = len(arrs)
    after = list(after)
    out_shapes = []
    for a, sc in zip(arrs, scatter):
        out_shapes.append(_sds(a.shape if sc else (NDEV,) + a.shape, a.dtype))

    def body(*refs):
        ins, outs = refs[:n], refs[n + len(after):2 * n + len(after)]
        send_sems, recv_sems, loc_sems = refs[2 * n + len(after):]
        me = 4 * lax.axis_index("x") + 2 * lax.axis_index("y") + lax.axis_index("c")
        copies = []
        for k in range(n):
            src_me = ins[k].at[me] if scatter[k] else ins[k]
            loc = pltpu.make_async_copy(src_me, outs[k].at[me], loc_sems.at[k])
            loc.start()
            copies.append(loc)
        remote = _peer_copies(ins, outs, scatter, send_sems, recv_sems)
        for cp in remote:
            cp.start()
        for cp in remote:
            cp.wait_recv()
        for cp in remote:
            cp.wait_send()
        for cp in copies:
            cp.wait()

    return pl.pallas_call(
        body, name=name, out_shape=tuple(out_shapes), in_specs=[ANY] * (n + len(after)), out_specs=tuple([ANY] * n),
        scratch_shapes=[pltpu.SemaphoreType.DMA((n * (NDEV - 1),)), pltpu.SemaphoreType.DMA((n * (NDEV - 1),)),
                        pltpu.SemaphoreType.DMA((n,))],
    )(*arrs, *after)


def _forward_to_sibling(bufs, name):
    n = len(bufs)

    def body(*refs):
        ins, outs = refs[:n], refs[n:2 * n]
        send_sems, recv_sems = refs[2 * n:]
        x, y, c = lax.axis_index("x"), lax.axis_index("y"), lax.axis_index("c")
        copies = []
        for k in range(n):
            for j, (cx, cy) in enumerate(((1 - x, y), (x, 1 - y), (1 - x, 1 - y))):
                slot = 4 * cx + 2 * cy + c
                copies.append(pltpu.make_async_remote_copy(
                    src_ref=ins[k].at[slot], dst_ref=outs[k].at[slot], send_sem=send_sems.at[3 * k + j],
                    recv_sem=recv_sems.at[3 * k + j], device_id=(x, y, 1 - c), device_id_type=MESH_ID))
        for cp in copies:
            cp.start()
        for cp in copies:
            cp.wait_recv()
        for cp in copies:
            cp.wait_send()

    return pl.pallas_call(
        body, name=name, out_shape=tuple(_sds(b.shape, b.dtype) for b in bufs), in_specs=[ANY] * n,
        out_specs=tuple([ANY] * n), input_output_aliases={k: k for k in range(n)},
        scratch_shapes=[pltpu.SemaphoreType.DMA((3 * n,)), pltpu.SemaphoreType.DMA((3 * n,))],
    )(*bufs)


ALL_PEERS = tuple(range(1, NDEV))
SAME_CORE_AND_SIBLING = (1, 2, 4, 6)


def _peer_copies(srcs, lands, scatter, send_sems, recv_sems, masks=ALL_PEERS):
    x, y, c = lax.axis_index("x"), lax.axis_index("y"), lax.axis_index("c")
    me = 4 * x + 2 * y + c
    out = []
    for k in range(len(srcs)):
        for m in masks:
            px, py, pc = x ^ (m >> 2), y ^ ((m >> 1) & 1), c ^ (m & 1)
            src = srcs[k].at[4 * px + 2 * py + pc] if scatter[k] else srcs[k]
            out.append(pltpu.make_async_remote_copy(
                src_ref=src, dst_ref=lands[k].at[me], send_sem=send_sems.at[k * (NDEV - 1) + m - 1],
                recv_sem=recv_sems.at[k * (NDEV - 1) + m - 1],
                device_id=(px, py, pc), device_id_type=MESH_ID))
    return out


def _exchange_start(arrs, scatter, name, masks=ALL_PEERS):
    n = len(arrs)
    me = 4 * lax.axis_index("x") + 2 * lax.axis_index("y") + lax.axis_index("c")
    lands = []
    for a, sc in zip(arrs, scatter):
        own = lax.dynamic_index_in_dim(a, me, 0, keepdims=True) if sc else a[None]
        shape = a.shape if sc else (NDEV,) + a.shape
        lands.append(lax.dynamic_update_index_in_dim(lax.empty(shape, a.dtype), own, me, 0))

    def body(*refs):
        srcs, lnds = refs[:n], refs[n:2 * n]
        send_sems, recv_sems = refs[2 * n], refs[2 * n + 1]
        token = refs[-1]
        for cp in _peer_copies(srcs, lnds, scatter, send_sems, recv_sems, masks):
            cp.start()
        token[...] = jnp.zeros_like(token)

    ops = [pltpu.with_memory_space_constraint(a, pltpu.HBM) for a in list(arrs) + lands]
    res = pl.pallas_call(
        body, name=name,
        out_shape=(pltpu.SemaphoreType.DMA((n * (NDEV - 1),)), pltpu.SemaphoreType.DMA((n * (NDEV - 1),)))
        + tuple(pltpu.HBM(o.shape, o.dtype) for o in ops) + (_sds((8, 128), F32),),
        in_specs=[HBM] * (2 * n), out_specs=(SEM, SEM) + (HBM,) * (2 * n) + (pl.BlockSpec(memory_space=pltpu.VMEM),),
        input_output_aliases={k: 2 + k for k in range(2 * n)},
        compiler_params=pltpu.CompilerParams(has_side_effects=EFFECT),
    )(*ops)
    return res[:-1], res[-1]


def _exchange_wait(handle, scatter, after, name, masks=ALL_PEERS):
    send_sems, recv_sems = handle[0], handle[1]
    bufs = handle[2:]
    n = len(bufs) // 2
    after = list(after)

    def body(*refs):
        srcs, lnds = refs[:n], refs[n:2 * n]
        for cp in _peer_copies(srcs, lnds, scatter, refs[2 * n], refs[2 * n + 1], masks):
            cp.wait_send()
            cp.wait_recv()

    res = pl.pallas_call(
        body, name=name, out_shape=tuple(pltpu.HBM(b.shape, b.dtype) for b in bufs),
        in_specs=[HBM] * (2 * n) + [SEM, SEM] + [ANY] * len(after), out_specs=(HBM,) * (2 * n),
        input_output_aliases={k: k for k in range(2 * n)},
        compiler_params=pltpu.CompilerParams(has_side_effects=EFFECT),
    )(*bufs, send_sems, recv_sems, *after)
    return res[n:]


def _tie(x, token, name):
    def body(x_ref, t_ref, o_ref):
        del x_ref, t_ref, o_ref

    return pl.pallas_call(body, name=name, out_shape=_sds(x.shape, x.dtype), in_specs=[ANY, ANY], out_specs=ANY,
                          input_output_aliases={0: 0})(x, token)


TM_E = 272


def _norm_fwd_t(h, g, name):
    def body(h_ref, g_ref, o_ref, ot_ref):
        xv = h_ref[...]
        r = lax.rsqrt(jnp.mean(xv * xv, axis=-1, keepdims=True) + EPS)
        y = xv * r * g_ref[...]
        o_ref[...] = y.astype(BF16)
        ot_ref[...] = y.T.astype(BF16)

    return pl.pallas_call(
        body, name=name, grid=(T // 128,),
        in_specs=[pl.BlockSpec((128, D), lambda i: (i, 0)), pl.BlockSpec((1, D), lambda i: (0, 0))],
        out_specs=(pl.BlockSpec((128, D), lambda i: (i, 0)), pl.BlockSpec((D, 128), lambda i: (0, i))),
        out_shape=(_sds((T, D), BF16), _sds((D, T), BF16)), compiler_params=_cp(("parallel",)))(h, g)


def _norm_bwd_rows(xv, gv, dnv, dres):
    r = lax.rsqrt(jnp.mean(xv * xv, axis=-1, keepdims=True) + EPS)
    xh = xv * r
    dxh = dnv * gv
    dx = dres + r * (dxh - xh * jnp.mean(dxh * xh, axis=-1, keepdims=True))
    return dx, jnp.sum(dnv * xh, axis=0, keepdims=True)


TM_MM = 1088


def _inproj_fwd(a, w_g):
    nb = w_g.shape[2]

    def body(a_ref, w_ref, o_ref):
        o_ref[...] = _dot(a_ref[...], w_ref[0])

    return pl.pallas_call(
        body, name="inproj_fwd", grid=(T // TM_MM, NDEV),
        in_specs=[pl.BlockSpec((TM_MM, D), lambda i, j: (i, 0)), pl.BlockSpec((1, D, nb), lambda i, j: (j, 0, 0))],
        out_specs=pl.BlockSpec((TM_MM, nb), lambda i, j: (i, j)), out_shape=_sds((T, NDEV * nb), F32),
        compiler_params=_cp(("parallel", "parallel")))(a, w_g)


TM_B = 544


W_IN_B = IN_COLS // NDEV


def _inproj_bwd_dw(a_t, dp):
    def body(at_ref, dp_ref, dw_ref):
        dw_ref[0] = _dot(at_ref[...], dp_ref[...]).astype(BF16)

    return pl.pallas_call(
        body, name="inproj_bwd_dw", grid=(NDEV,),
        in_specs=[pl.BlockSpec((D, T), lambda j: (0, 0)), pl.BlockSpec((T, W_IN_B), lambda j: (0, j))],
        out_specs=pl.BlockSpec((1, D, W_IN_B), lambda j: (j, 0, 0)), out_shape=_sds((NDEV, D, W_IN_B), BF16),
        compiler_params=_cp(("parallel",)))(a_t, dp)


def _inproj_bwd_da(dp, w_g, h0, g_mix, dh1):
    nsub = TM_MM // TM_E

    def body(dp_ref, w_ref, h0_ref, g_ref, dres_ref, dh0_ref, dg_ref, da):
        i, j = pl.program_id(0), pl.program_id(1)
        dav = _dot(dp_ref[...], w_ref[0], NT)

        @pl.when(j == 0)
        def _():
            da[...] = dav

        @pl.when(j > 0)
        def _():
            da[...] += dav

        @pl.when(j == NDEV - 1)
        def _():
            gsum = jnp.zeros((1, D), F32)
            for s in range(nsub):
                sub = slice(s * TM_E, (s + 1) * TM_E)
                dx, gpart = _norm_bwd_rows(h0_ref[sub, :], g_ref[...], da[sub, :], dres_ref[sub, :])
                dh0_ref[sub, :] = dx
                gsum = gsum + gpart

            @pl.when(i == 0)
            def _():
                dg_ref[...] = gsum

            @pl.when(i > 0)
            def _():
                dg_ref[...] += gsum

    rblk = pl.BlockSpec((TM_MM, D), lambda i, j: (i, 0))
    vec = pl.BlockSpec((1, D), lambda i, j: (0, 0))
    return pl.pallas_call(
        body, name="inproj_bwd_da", grid=(T // TM_MM, NDEV),
        in_specs=[pl.BlockSpec((TM_MM, W_IN_B), lambda i, j: (i, j)), pl.BlockSpec((1, D, W_IN_B), lambda i, j: (j, 0, 0)),
                  rblk, vec, rblk],
        out_specs=(rblk, vec), out_shape=(_sds((T, D), F32), _sds((1, D), F32)),
        scratch_shapes=[pltpu.VMEM((TM_MM, D), F32)],
        compiler_params=_cp(("arbitrary", "arbitrary"), 56))(dp, w_g, h0, g_mix, dh1)


NA_QB = 256
NA_GROUPS = ROWS // 4
NA_UROWS = 11
NA_KW = NA_UROWS * GRID_W
NA_KU = 768


def _na_row_offset(var, i, j):
    valid = (j < 8, i <= j < i + 8, 3 <= j < NA_UROWS)[var]
    return (j - i + (7, 3, 0)[var]) if valid else None


def _na_bias_table(rp):
    def body(r_ref, o_ref):
        row3 = lax.broadcasted_iota(jnp.int32, (15, GRID_W, 128), 1)
        lane3 = lax.broadcasted_iota(jnp.int32, (15, GRID_W, 128), 2)
        w3 = lane3 & (GRID_W - 1)
        cs3 = jnp.clip(row3 - 8, 0, GRID_W - 16)
        lane = lax.broadcasted_iota(jnp.int32, (GRID_W, 128), 1)
        neg = jnp.full((GRID_W, 128), NEG, F32)
        z = jnp.stack([jnp.broadcast_to(r_ref[0, a:a + 1, :], (GRID_W, 128)) for a in range(15)])
        for bit in range(6):
            sh = 1 << bit
            z = jnp.where((row3 & sh) != 0, jnp.roll(z, sh, axis=2), z)
        z = jnp.roll(z, 128 - 15, axis=2)
        z = jnp.where(lane3 < GRID_W, z, 0.0)
        z = z + jnp.roll(z, GRID_W, axis=2)
        tabs = jnp.where((w3 >= cs3) & (w3 < cs3 + 16), z, NEG)
        tail = jnp.where(lane < GRID_W + NM, 0.0, NEG)
        for var in range(3):
            for i in range(4):
                for jp in range(NA_KU // 128):
                    halves = []
                    for j in (2 * jp, 2 * jp + 1):
                        a = _na_row_offset(var, i, j) if j < NA_UROWS else None
                        halves.append(tail if j >= NA_UROWS else (neg if a is None else tabs[a]))
                    o_ref[var, 0, i * 64:(i + 1) * 64, jp * 128:(jp + 1) * 128] = jnp.where(lane < GRID_W, halves[0], halves[1])

    return pl.pallas_call(
        body, name="na_bias_table", grid=(NA_HEADS,),
        in_specs=[pl.BlockSpec((1, 15, 128), lambda h: (h, 0, 0))],
        out_specs=pl.BlockSpec((3, 1, NA_QB, NA_KU), lambda h: (0, h, 0, 0)),
        out_shape=_sds((3, NA_HEADS, NA_QB, NA_KU), F32), compiler_params=_cp(("parallel",)))(rp)


def _na_var(g):
    return jnp.where(g == 0, 0, jnp.where(g == NA_GROUPS - 1, 2, 1))


def _na_load_window(src_ref, dst, g):
    us = jnp.clip(4 * g - 4, 0, ROWS - NA_UROWS)
    kstart = pl.multiple_of(NM + GRID_W * us, 16)
    dst[0:NA_KW, :] = src_ref[pl.ds(kstart, NA_KW), :].astype(BF16)
    dst[NA_KW:NA_KW + NM, :] = src_ref[0:NM, :].astype(BF16)
    dst[NA_KW + NM:, :] = jnp.zeros((NA_KU - NA_KW - NM, 128), BF16)
    return kstart


def _na_fwd(p_act, bias_tab):
    def body(q_ref, k_ref, v_ref, b_ref, o_ref, lse_ref, ku, vu):
        g = pl.program_id(1)
        _na_load_window(k_ref, ku, g)
        _na_load_window(v_ref, vu, g)
        qstart = pl.multiple_of(NM + NA_QB * g, 16)
        q = q_ref[pl.ds(qstart, NA_QB), :]
        lane = lax.broadcasted_iota(jnp.int32, (NA_QB, 128), 1)
        o_h, lse_h = [], []
        for h in range(2):
            hm = (lane < 64) if h == 0 else (lane >= 64)
            qm = (jnp.where(hm, q, 0.0) * NA_SCALE).astype(BF16)
            s = _dot(qm, ku[...], NT) + b_ref[0, h]
            m = jnp.max(s, axis=-1, keepdims=True)
            p = jnp.exp(s - m)
            l = jnp.sum(p, axis=-1, keepdims=True)
            o_h.append(_dot(p.astype(BF16), vu[...]) / l)
            lse_h.append(jnp.broadcast_to(m + jnp.log(l), (NA_QB, 128)))
        o_ref[pl.ds(qstart, NA_QB), :] = jnp.where(lane < 64, o_h[0], o_h[1]).astype(BF16)
        lse_ref[0, pl.ds(qstart, NA_QB), :] = jnp.where(lane < 64, lse_h[0], lse_h[1])

        @pl.when(g == 0)
        def _():
            qm_ = q_ref[0:NM, :]
            lane_m = lax.broadcasted_iota(jnp.int32, (NM, 128), 1)
            km, vm = ku[NA_KW:NA_KW + NM, :], vu[NA_KW:NA_KW + NM, :]
            om = []
            for h in range(2):
                hm = (lane_m < 64) if h == 0 else (lane_m >= 64)
                s = _dot(jnp.where(hm, qm_, 0.0).astype(BF16), km, NT) * NA_SCALE
                p = jnp.exp(s - jnp.max(s, axis=-1, keepdims=True))
                l = jnp.sum(p, axis=-1, keepdims=True)
                om.append(_dot(p.astype(BF16), vm) / l)
            o_ref[0:NM, :] = jnp.where(lane_m < 64, om[0], om[1]).astype(BF16)
            o_ref[L:T, :] = jnp.zeros((T - L, 128), BF16)
            lse_ref[0, 0:NM, :] = jnp.zeros((NM, 128), F32)
            lse_ref[0, L:T, :] = jnp.zeros((T - L, 128), F32)

    col = lambda off: pl.BlockSpec((T, 128), lambda hp, g: (0, off + hp))
    return pl.pallas_call(
        body, name="na_fwd", grid=(4, NA_GROUPS),
        in_specs=[col(0), col(4), col(8),
                  pl.BlockSpec((1, 2, NA_QB, NA_KU), lambda hp, g: (_na_var(g), hp, 0, 0))],
        out_specs=(pl.BlockSpec((T, 128), lambda hp, g: (0, hp)), pl.BlockSpec((1, T, 128), lambda hp, g: (hp, 0, 0))),
        out_shape=(_sds((T, 512), BF16), _sds((4, T, 128), F32)),
        scratch_shapes=[pltpu.VMEM((NA_KU, 128), BF16), pltpu.VMEM((NA_KU, 128), BF16)],
        compiler_params=_cp(("parallel", "arbitrary")))(p_act, p_act, p_act, bias_tab)


def _na_bwd(p_act, do, lse, bias_tab):
    def body(q_ref, k_ref, v_ref, do_ref, lse_ref, b_ref, dq_ref, dk_ref, dv_ref, db_ref, ku, vu):
        g = pl.program_id(1)

        @pl.when(g == 0)
        def _():
            dq_ref[...] = jnp.zeros((T, 128), F32)
            dk_ref[...] = jnp.zeros((T, 128), F32)
            dv_ref[...] = jnp.zeros((T, 128), F32)

        kstart = _na_load_window(k_ref, ku, g)
        _na_load_window(v_ref, vu, g)
        qstart = pl.multiple_of(NM + NA_QB * g, 16)
        q = q_ref[pl.ds(qstart, NA_QB), :]
        dov = do_ref[pl.ds(qstart, NA_QB), :]
        lsev = lse_ref[0, pl.ds(qstart, NA_QB), :]
        lane = lax.broadcasted_iota(jnp.int32, (NA_QB, 128), 1)
        first = (g == 0) | (g == 1) | (g == NA_GROUPS - 1)
        dq_h = []
        dku = jnp.zeros((NA_KU, 128), F32)
        dvu = jnp.zeros((NA_KU, 128), F32)
        for h in range(2):
            hm = (lane < 64) if h == 0 else (lane >= 64)
            qm = (jnp.where(hm, q, 0.0) * NA_SCALE).astype(BF16)
            dom = jnp.where(hm, dov, 0.0).astype(BF16)
            s = _dot(qm, ku[...], NT) + b_ref[0, h]
            p = jnp.exp(s - lsev[:, 64 * h:64 * h + 1])
            dp = _dot(dom, vu[...], NT)
            delta = jnp.sum(p * dp, axis=-1, keepdims=True)
            ds = p * (dp - delta)

            @pl.when(first)
            def _():
                db_ref[0, h] = ds

            @pl.when(jnp.logical_not(first))
            def _():
                db_ref[0, h] += ds

            dsb = ds.astype(BF16)
            dq_h.append(_dot(dsb, ku[...]) * NA_SCALE)
            dku = dku + _dot(dsb, qm, TN)
            dvu = dvu + _dot(p.astype(BF16), dom, TN)
        dq_ref[pl.ds(qstart, NA_QB), :] = jnp.where(lane < 64, dq_h[0], dq_h[1])
        dk_ref[pl.ds(kstart, NA_KW), :] += dku[0:NA_KW]
        dv_ref[pl.ds(kstart, NA_KW), :] += dvu[0:NA_KW]
        dk_ref[0:NM, :] += dku[NA_KW:NA_KW + NM]
        dv_ref[0:NM, :] += dvu[NA_KW:NA_KW + NM]

        @pl.when(g == 0)
        def _():
            qm_ = q_ref[0:NM, :]
            dom_ = do_ref[0:NM, :]
            lane_m = lax.broadcasted_iota(jnp.int32, (NM, 128), 1)
            km, vm = ku[NA_KW:NA_KW + NM, :], vu[NA_KW:NA_KW + NM, :]
            dqs = []
            dkm = jnp.zeros((NM, 128), F32)
            dvm = jnp.zeros((NM, 128), F32)
            for h in range(2):
                hm = (lane_m < 64) if h == 0 else (lane_m >= 64)
                qh = jnp.where(hm, qm_, 0.0).astype(BF16)
                doh = jnp.where(hm, dom_, 0.0).astype(BF16)
                s = _dot(qh, km, NT) * NA_SCALE
                e = jnp.exp(s - jnp.max(s, axis=-1, keepdims=True))
                p = e / jnp.sum(e, axis=-1, keepdims=True)
                dp = _dot(doh, vm, NT)
                ds = p * (dp - jnp.sum(p * dp, axis=-1, keepdims=True))
                dsb = (ds * NA_SCALE).astype(BF16)
                dqs.append(_dot(dsb, km))
                dkm = dkm + _dot(dsb, qh, TN)
                dvm = dvm + _dot(p.astype(BF16), doh, TN)
            dq_ref[0:NM, :] = jnp.where(lane_m < 64, dqs[0], dqs[1])
            dk_ref[0:NM, :] += dkm
            dv_ref[0:NM, :] += dvm

    col = lambda off: pl.BlockSpec((T, 128), lambda hp, g: (0, off + hp))
    ocol = pl.BlockSpec((T, 128), lambda hp, g: (0, hp))
    bspec = pl.BlockSpec((1, 2, NA_QB, NA_KU), lambda hp, g: (_na_var(g), hp, 0, 0))
    return pl.pallas_call(
        body, name="na_bwd", grid=(4, NA_GROUPS),
        in_specs=[col(0), col(4), col(8), ocol, pl.BlockSpec((1, T, 128), lambda hp, g: (hp, 0, 0)), bspec],
        out_specs=(ocol, ocol, ocol, bspec),
        out_shape=(_sds((T, 512), F32), _sds((T, 512), F32), _sds((T, 512), F32), _sds((3, NA_HEADS, NA_QB, NA_KU), F32)),
        scratch_shapes=[pltpu.VMEM((NA_KU, 128), BF16), pltpu.VMEM((NA_KU, 128), BF16)],
        compiler_params=_cp(("parallel", "arbitrary")))(p_act, p_act, p_act, do, lse, bias_tab)


def _na_rpb_reduce(dbias):
    def body(db_ref, o_ref):
        lane = lax.broadcasted_iota(jnp.int32, (GRID_W, 128), 1)
        row3 = lax.broadcasted_iota(jnp.int32, (15, GRID_W, 128), 1)
        lane3 = lax.broadcasted_iota(jnp.int32, (15, GRID_W, 128), 2)
        accs = []
        for a in range(15):
            acc = jnp.zeros((GRID_W, 128), F32)
            for var in range(3):
                for i in range(4):
                    for j in range(NA_UROWS):
                        if _na_row_offset(var, i, j) == a:
                            pair = db_ref[var, 0, i * 64:(i + 1) * 64, (j // 2) * 128:(j // 2 + 1) * 128]
                            acc = acc + jnp.where((lane < GRID_W) if j % 2 == 0 else (lane >= GRID_W), pair, 0.0)
            accs.append(acc)
        z = jnp.stack(accs)
        z = jnp.where(lane3 < GRID_W, z + jnp.roll(z, GRID_W, axis=2), 0.0)
        for bit in range(6):
            sh = 1 << bit
            z = jnp.where((row3 & sh) != 0, jnp.roll(z, 128 - sh, axis=2), z)
        z = jnp.roll(z, 15, axis=2)
        o_ref[0] = jnp.sum(z, axis=1)

    return pl.pallas_call(
        body, name="na_rpb_reduce", grid=(NA_HEADS,),
        in_specs=[pl.BlockSpec((3, 1, NA_QB, NA_KU), lambda h: (0, h, 0, 0))],
        out_specs=pl.BlockSpec((1, 15, 128), lambda h: (h, 0, 0)), out_shape=_sds((NA_HEADS, 15, 128), F32),
        compiler_params=_cp(("parallel",)))(dbias)


HG_RB = 128
HG_NB = T // HG_RB
HG_SLOTS = HG_NB * 8
HI = lax.Precision.HIGHEST
HG_UNROLL = 4


def _chunk_tri(lower):
    r = lax.broadcasted_iota(jnp.int32, (HG_RB, HG_RB), 0)
    c = lax.broadcasted_iota(jnp.int32, (HG_RB, HG_RB), 1)
    same = (r // HG_C) == (c // HG_C)
    keep = (c <= r) if lower else (c >= r)
    return jnp.where(same & keep, 1.0, 0.0).astype(F32)


def _hg_gate_terms(z, lg):
    dl = lg[0:1, :] - lg[1:2, :]
    log_lb = jax.nn.log_sigmoid(dl)
    log_1mlb = jax.nn.log_sigmoid(-dl)
    yz = log_1mlb + jax.nn.log_sigmoid(z)
    log_f = jnp.logaddexp(log_lb, yz)
    snz = jax.nn.sigmoid(-z)
    k = jnp.exp(log_1mlb) * snz
    w2 = jnp.exp(yz - log_f)
    return log_f, k, snz, w2


def _hg_pre(p_act, logits):
    def body(q_ref, zf_ref, zb_ref, lg_ref, qh_ref, kf_ref, bf_ref, kb_ref, bb_ref):
        qh_ref[...] = jax.nn.silu(q_ref[...])
        lf, kf, _, _ = _hg_gate_terms(zf_ref[...], lg_ref[0])
        kf_ref[...] = kf
        bf_ref[...] = jnp.dot(_chunk_tri(True), lf, precision=HI, preferred_element_type=F32)
        lb_, kb, _, _ = _hg_gate_terms(zb_ref[...], lg_ref[1])
        kb_ref[...] = kb
        bb_ref[...] = jnp.dot(_chunk_tri(False), lb_, precision=HI, preferred_element_type=F32)

    blk = lambda c: pl.BlockSpec((HG_RB, 512), lambda i: (i, c))
    ob = pl.BlockSpec((HG_RB, 512), lambda i: (i, 0))
    return pl.pallas_call(
        body, name="hg_pre", grid=(HG_NB,),
        in_specs=[blk(3), blk(4), blk(5), pl.BlockSpec((2, 2, 512), lambda i: (0, 0, 0))],
        out_specs=(ob,) * 5, out_shape=(_sds((T, 512), F32),) * 5,
        compiler_params=_cp(("parallel",)))(p_act, p_act, p_act, logits)


def _bdot(a, b, ca, cb):
    return lax.dot_general(a.astype(BF16), b.astype(BF16), (((ca,), (cb,)), ((0,), (0,))), preferred_element_type=F32)


HG_S = 8
HG_NS = HG_RB // HG_S


def _lane_sums(xs):
    l_io = lax.broadcasted_iota(jnp.int32, (HG_NS, HG_S, HG_S), 2)
    a = jnp.zeros((HG_NS, HG_S, HG_S), F32)
    for j, x in enumerate(xs):
        a = a + jnp.where(l_io == j, jnp.sum(x, axis=-1, keepdims=True), 0.0)
    return a


def _halves(x):
    y = x.reshape(8, 2, HG_S, x.shape[-1])
    return y[:, 0], y[:, 1]


def _join(first, second):
    return jnp.stack([first, second], axis=1).reshape(HG_RB, first.shape[-1])


def _cross_split(rev, b4):
    b_1, b_2 = _halves(b4)
    if rev:
        r = b_2[:, 0:1, :]
        return jnp.exp(b_1 - r), jnp.exp(r - b_2)
    r = b_1[:, HG_S - 1:HG_S, :]
    return jnp.exp(b_2 - r), jnp.exp(r - b_1)


def _hg_scan_fwd(qh, k, b, p_act, rev):
    anchor = 0 if rev else HG_C - 1

    def body(q_ref, k_ref, b_ref, v_ref, o_ref, st_ref, dsc):
        def phase_a(blk, _):
            rows = pl.ds(pl.multiple_of(blk * HG_RB, HG_RB), HG_RB)
            b3 = b_ref[rows, :].reshape(8, HG_C, 128)
            k3 = k_ref[rows, :].reshape(8, HG_C, 128)
            v3 = v_ref[rows, :].reshape(8, HG_C, 128)
            bl = b3[:, anchor:anchor + 1, :]
            kt = k3 * jnp.exp(bl - b3)
            st_ref[0, pl.ds(pl.multiple_of(blk * 8, 8), 8)] = _bdot(v3, kt, 1, 1)
            dsc[pl.ds(pl.multiple_of(blk * 8, 8), 8), :] = jnp.exp(bl[:, 0, :])
            return 0

        lax.fori_loop(0, HG_NB, phase_a, 0, unroll=HG_UNROLL)

        def phase_b(n, carry):
            c = (NCHUNK - 1 - n) if rev else n
            u = st_ref[0, c]
            st_ref[0, c] = carry
            return carry * dsc[pl.ds(c, 1), :] + u

        lax.fori_loop(0, NCHUNK // 3, lambda n3, s: phase_b(3 * n3 + 2, phase_b(3 * n3 + 1, phase_b(3 * n3, s))),
                      jnp.zeros((128, 128), F32))
        for c in range(NCHUNK, HG_SLOTS):
            st_ref[0, c] = jnp.zeros((128, 128), F32)

        t_io = lax.broadcasted_iota(jnp.int32, (HG_NS, HG_S, 128), 1)

        def phase_c(blk, _):
            rows = pl.ds(pl.multiple_of(blk * HG_RB, HG_RB), HG_RB)
            b4 = b_ref[rows, :].reshape(HG_NS, HG_S, 128)
            k4 = k_ref[rows, :].reshape(HG_NS, HG_S, 128)
            q4 = q_ref[rows, :].reshape(HG_NS, HG_S, 128)
            v4 = v_ref[rows, :].reshape(HG_NS, HG_S, 128)
            st = st_ref[0, pl.ds(pl.multiple_of(blk * 8, 8), 8)]
            o = _bdot((q4 * jnp.exp(b4)).reshape(8, HG_C, 128), st, 2, 2).reshape(HG_RB, 128)
            terms = []
            for s in range(HG_S):
                ok = (t_io <= s) if rev else (t_io >= s)
                f = jnp.exp(jnp.where(ok, b4 - b4[:, s:s + 1, :], NEG))
                terms.append(q4 * f * k4[:, s:s + 1, :])
            o_in = _bdot(_lane_sums(terms), v4, 2, 1)
            wq, wk = _cross_split(rev, b4)
            q_1, q_2 = _halves(q4)
            k_1, k_2 = _halves(k4)
            v_1, v_2 = _halves(v4)
            o_1, o_2 = _halves(o_in)
            if rev:
                o_1 = o_1 + _bdot(_bdot(q_1 * wq, k_2 * wk, 2, 2), v_2, 2, 1)
            else:
                o_2 = o_2 + _bdot(_bdot(q_2 * wq, k_1 * wk, 2, 2), v_1, 2, 1)
            o_ref[rows, :] = o + _join(o_1, o_2)
            return 0

        lax.fori_loop(0, HG_NB, phase_c, 0, unroll=HG_UNROLL)

    col = pl.BlockSpec((T, 128), lambda h: (0, h))
    return pl.pallas_call(
        body, name="hg_scan_bwd_dir" if rev else "hg_scan_fwd_dir", grid=(HG_HEADS,),
        in_specs=[col, col, col, pl.BlockSpec((T, 128), lambda h: (0, 24 + h))],
        out_specs=(col, pl.BlockSpec((1, HG_SLOTS, 128, 128), lambda h: (h, 0, 0, 0))),
        out_shape=(_sds((T, 512), F32), _sds((HG_HEADS, HG_SLOTS, 128, 128), F32)),
        scratch_shapes=[pltpu.VMEM((HG_SLOTS, 128), F32)],
        compiler_params=_cp(("parallel",), 56))(qh, k, b, p_act)


def _hg_scan_bwd(qh, k, b, p_act, st, do, rev):
    anchor = 0 if rev else HG_C - 1

    def body(q_ref, k_ref, b_ref, v_ref, st_ref, do_ref, dq_ref, dk_ref, db_ref, dv_ref, gst, dsc, dbl):
        def phase_a(blk, _):
            rows = pl.ds(pl.multiple_of(blk * HG_RB, HG_RB), HG_RB)
            b3 = b_ref[rows, :].reshape(8, HG_C, 128)
            q3 = q_ref[rows, :].reshape(8, HG_C, 128)
            do3 = do_ref[rows, :].reshape(8, HG_C, 128)
            gst[pl.ds(pl.multiple_of(blk * 8, 8), 8)] = _bdot(do3, q3 * jnp.exp(b3), 1, 1)
            dsc[pl.ds(pl.multiple_of(blk * 8, 8), 8), :] = jnp.exp(b3[:, anchor, :])
            return 0

        lax.fori_loop(0, HG_NB, phase_a, 0, unroll=HG_UNROLL)

        def phase_b(n, carry):
            c = n if rev else (NCHUNK - 1 - n)
            w = gst[c]
            gst[c] = carry
            dcv = dsc[pl.ds(c, 1), :]
            dbl[pl.ds(c, 1), :] = dcv * jnp.sum(st_ref[0, c] * carry, axis=0, keepdims=True)
            return carry * dcv + w

        lax.fori_loop(0, NCHUNK // 3, lambda n3, s: phase_b(3 * n3 + 2, phase_b(3 * n3 + 1, phase_b(3 * n3, s))),
                      jnp.zeros((128, 128), F32))
        for c in range(NCHUNK, HG_SLOTS):
            gst[c] = jnp.zeros((128, 128), F32)
            dbl[c:c + 1, :] = jnp.zeros((1, 128), F32)

        t_io = lax.broadcasted_iota(jnp.int32, (HG_NS, HG_S, 128), 1)
        t16 = lax.broadcasted_iota(jnp.int32, (8, HG_C, 128), 1)
        r_io = lax.broadcasted_iota(jnp.int32, (HG_NS, HG_S, HG_S), 1)
        l_io = lax.broadcasted_iota(jnp.int32, (HG_NS, HG_S, HG_S), 2)

        def phase_c(blk, _):
            rows = pl.ds(pl.multiple_of(blk * HG_RB, HG_RB), HG_RB)
            cs = pl.ds(pl.multiple_of(blk * 8, 8), 8)
            b4 = b_ref[rows, :].reshape(HG_NS, HG_S, 128)
            k4 = k_ref[rows, :].reshape(HG_NS, HG_S, 128)
            q4 = q_ref[rows, :].reshape(HG_NS, HG_S, 128)
            v4 = v_ref[rows, :].reshape(HG_NS, HG_S, 128)
            do4 = do_ref[rows, :].reshape(HG_NS, HG_S, 128)
            b3, k3, q3 = (z.reshape(8, HG_C, 128) for z in (b4, k4, q4))
            v3, do3 = v4.reshape(8, HG_C, 128), do4.reshape(8, HG_C, 128)
            s_t = st_ref[0, cs]
            g_t = gst[cs]
            bl = b3[:, anchor:anchor + 1, :]
            ekl = jnp.exp(bl - b3)
            kt = k3 * ekl
            dkt = _bdot(v3, g_t, 2, 1)
            dq = (_bdot(do3, s_t, 2, 1) * jnp.exp(b3)).reshape(HG_NS, HG_S, 128)
            dk = (dkt * ekl).reshape(HG_NS, HG_S, 128)
            dv = _bdot(kt, g_t, 2, 2).reshape(HG_NS, HG_S, 128)
            dbl3 = dbl[cs, :].reshape(8, 1, 128) + jnp.sum(dkt * kt, axis=1, keepdims=True)
            causal = (l_io >= r_io) if rev else (l_io <= r_io)
            da = jnp.where(causal, _bdot(do4, v4, 2, 2), 0.0)
            causal_t = (l_io <= r_io) if rev else (l_io >= r_io)
            dat = jnp.where(causal_t, _bdot(v4, do4, 2, 2), 0.0)
            for s in range(HG_S):
                ok = (t_io <= s) if rev else (t_io >= s)
                f = jnp.exp(jnp.where(ok, b4 - b4[:, s:s + 1, :], NEG))
                dq = dq + da[:, :, s:s + 1] * (f * k4[:, s:s + 1, :])
            terms = []
            for t in range(HG_S):
                ok = (t_io >= t) if rev else (t_io <= t)
                e = jnp.exp(jnp.where(ok, b4[:, t:t + 1, :] - b4, NEG))
                eq = e * q4[:, t:t + 1, :]
                dk = dk + dat[:, :, t:t + 1] * eq
                terms.append(eq * k4)
            dv = dv + _bdot(_lane_sums(terms), do4, 2, 1)
            wq, wk = _cross_split(rev, b4)
            pick = (lambda z: _halves(z)) if rev else (lambda z: _halves(z)[::-1])
            (q_q, _), (_, k_k), (_, v_k), (do_q, _) = pick(q4), pick(k4), pick(v4), pick(do4)
            qx, kx = q_q * wq, k_k * wk
            dq_q = _bdot(_bdot(do_q, v_k, 2, 2), kx, 2, 1) * wq
            dk_k = _bdot(_bdot(v_k, do_q, 2, 2), qx, 2, 1) * wk
            dv_k = _bdot(_bdot(kx, qx, 2, 2), do_q, 2, 1)
            zero = jnp.zeros((8, HG_S, 128), F32)
            place_q = (lambda z: _join(z, zero)) if rev else (lambda z: _join(zero, z))
            place_k = (lambda z: _join(zero, z)) if rev else (lambda z: _join(z, zero))
            dq2 = dq.reshape(HG_RB, 128) + place_q(dq_q)
            dk2 = dk.reshape(HG_RB, 128) + place_k(dk_k)
            dv2 = dv.reshape(HG_RB, 128) + place_k(dv_k)
            dq3, dk3 = dq2.reshape(8, HG_C, 128), dk2.reshape(8, HG_C, 128)
            db = q3 * dq3 - k3 * dk3 + jnp.where(t16 == anchor, dbl3, 0.0)
            dq_ref[rows, :] = dq2
            dk_ref[rows, :] = dk2
            db_ref[rows, :] = db.reshape(HG_RB, 128)
            dv_ref[rows, :] = dv2
            return 0

        lax.fori_loop(0, HG_NB, phase_c, 0, unroll=HG_UNROLL)

    col = pl.BlockSpec((T, 128), lambda h: (0, h))
    return pl.pallas_call(
        body, name="hg_scan_bwd_dir_bwd" if rev else "hg_scan_fwd_dir_bwd", grid=(HG_HEADS,),
        in_specs=[col, col, col, pl.BlockSpec((T, 128), lambda h: (0, 24 + h)),
                  pl.BlockSpec((1, HG_SLOTS, 128, 128), lambda h: (h, 0, 0, 0)), col],
        out_specs=(col,) * 4, out_shape=(_sds((T, 512), F32),) * 4,
        scratch_shapes=[pltpu.VMEM((HG_SLOTS, 128, 128), F32), pltpu.VMEM((HG_SLOTS, 128), F32),
                        pltpu.VMEM((HG_SLOTS, 128), F32)],
        compiler_params=_cp(("parallel",), 56))(qh, k, b, p_act, st, do)


def _row_valid(i, tm):
    r = lax.broadcasted_iota(jnp.int32, (tm, 1), 0) + i * tm
    return r < L


def _hg_post_rows(o, gv, gain_v, valid):
    parts = []
    for h in range(HG_HEADS):
        oh = o[:, 128 * h:128 * (h + 1)]
        parts.append(oh * lax.rsqrt(jnp.mean(oh * oh, axis=-1, keepdims=True) + EPS))
    return jnp.where(valid, jnp.concatenate(parts, axis=1) * gain_v * jax.nn.silu(gv), 0.0)


def _hg_post_bwd_rows(du, o, gv, gain_v, valid):
    duv = jnp.where(valid, du, 0.0)
    sig = jax.nn.sigmoid(gv)
    sg = gv * sig
    dn = duv * gain_v * sg
    do_parts, n_parts = [], []
    for h in range(HG_HEADS):
        sl = slice(128 * h, 128 * (h + 1))
        oh = o[:, sl]
        r = lax.rsqrt(jnp.mean(oh * oh, axis=-1, keepdims=True) + EPS)
        nh = oh * r
        dnh = dn[:, sl]
        do_parts.append(r * (dnh - nh * jnp.mean(dnh * nh, axis=-1, keepdims=True)))
        n_parts.append(nh)
    n = jnp.where(valid, jnp.concatenate(n_parts, axis=1), 0.0)
    do = jnp.where(valid, jnp.concatenate(do_parts, axis=1), 0.0)
    dg = duv * n * gain_v * (sig * (1.0 + gv * (1.0 - sig)))
    return do, dg, jnp.sum(duv * n * sg, axis=0, keepdims=True)


def _hg_pre_bwd(p_act, logits, dq_f, dq_b, dk_f, dk_b, db_f, db_b, dv_f, dv_b):
    def body(q_ref, zf_ref, zb_ref, lg_ref, dqf_ref, dqb_ref, dkf_ref, dkb_ref, dbf_ref, dbb_ref, dvf_ref, dvb_ref,
             dq_ref, dzf_ref, dzb_ref, di_ref, dlg_ref):
        i = pl.program_id(0)
        valid = _row_valid(i, HG_RB)
        qv = q_ref[...]
        sig = jax.nn.sigmoid(qv)
        dq_ref[...] = jnp.where(valid, (dqf_ref[...] + dqb_ref[...]) * (sig * (1.0 + qv * (1.0 - sig))), 0.0).astype(BF16)
        di_ref[...] = jnp.where(valid, dvf_ref[...] + dvb_ref[...], 0.0).astype(BF16)
        for d, (z_ref, dk_r, db_r, dz_ref) in enumerate(((zf_ref, dkf_ref, dbf_ref, dzf_ref), (zb_ref, dkb_ref, dbb_ref, dzb_ref))):
            lg = lg_ref[d]
            dl = lg[0:1, :] - lg[1:2, :]
            lb = jax.nn.sigmoid(dl)
            one_m_lb = jax.nn.sigmoid(-dl)
            log_f, _, snz, w2 = _hg_gate_terms(z_ref[...], lg)
            dbv = jnp.where(valid, db_r[...], 0.0)
            dkv = jnp.where(valid, dk_r[...], 0.0)
            dlf = jnp.dot(_chunk_tri(d == 1), dbv, precision=HI, preferred_element_type=F32)
            sz = 1.0 - snz
            dz_ref[...] = (dlf * w2 * snz - dkv * one_m_lb * sz * snz).astype(BF16)
            dlb = jnp.sum(dlf * snz * jnp.exp(-log_f) - dkv * snz, axis=0, keepdims=True)
            dl0 = dlb * lb * one_m_lb
            part = jnp.concatenate([dl0, -dl0], axis=0)

            @pl.when(i == 0)
            def _():
                dlg_ref[d] = part

            @pl.when(i > 0)
            def _():
                dlg_ref[d] += part

    blk = lambda c: pl.BlockSpec((HG_RB, 512), lambda i: (i, c))
    ob = pl.BlockSpec((HG_RB, 512), lambda i: (i, 0))
    lgs = pl.BlockSpec((2, 2, 512), lambda i: (0, 0, 0))
    return pl.pallas_call(
        body, name="hg_pre_bwd", grid=(HG_NB,),
        in_specs=[blk(3), blk(4), blk(5), lgs] + [ob] * 8,
        out_specs=(ob, ob, ob, ob, lgs),
        out_shape=(_sds((T, 512), BF16),) * 4 + (_sds((2, 2, 512), F32),),
        compiler_params=_cp(("arbitrary",)))(p_act, p_act, p_act, logits, dq_f, dq_b, dk_f, dk_b, db_f, db_b, dv_f, dv_b)


def _mix_fwd(o_na, o_f, o_b, gain, w_na, w_hg, p_act):
    def body(ona_ref, of_ref, ob_ref, g_ref, gain_ref, wna_ref, whg_ref, gna_ref, ghg_ref, o_ref, u_ref):
        u = _hg_post_rows(of_ref[...] + ob_ref[...], g_ref[...], gain_ref[...], _row_valid(pl.program_id(0), TM_B)).astype(BF16)
        u_ref[...] = u
        y_na = _dot(ona_ref[...], wna_ref[...])
        y_hg = _dot(u, whg_ref[...])
        o_ref[...] = (jax.nn.sigmoid(gna_ref[...]) * y_na + jax.nn.sigmoid(ghg_ref[...]) * y_hg).astype(BF16)

    act = pl.BlockSpec((TM_B, 512), lambda i: (i, 0))
    wsp = pl.BlockSpec((512, D), lambda i: (0, 0))
    return pl.pallas_call(
        body, name="mix_fwd", grid=(T // TM_B,),
        in_specs=[act, act, act, pl.BlockSpec((TM_B, 512), lambda i: (i, 7)), pl.BlockSpec((1, 512), lambda i: (0, 0)),
                  wsp, wsp, pl.BlockSpec((TM_B, D), lambda i: (i, 4)), pl.BlockSpec((TM_B, D), lambda i: (i, 5))],
        out_specs=(pl.BlockSpec((TM_B, D), lambda i: (i, 0)), act), out_shape=(_sds((T, D), BF16), _sds((T, 512), BF16)),
        compiler_params=_cp(("parallel",)))(o_na, o_f, o_b, p_act, gain, w_na, w_hg, p_act, p_act)


def _mix_bwd(o_na, u_hg, o_f, o_b, gain, w_na, w_hg, p_act, dmix):
    ni = T // TM_B

    def body(ona_ref, uhg_ref, of_ref, ob_ref, g_ref, gain_ref, wna_ref, whg_ref, gna_ref, ghg_ref, dmix_ref,
             dgna_ref, dghg_ref, dwna_ref, dwhg_ref, dona_ref, do_ref, dg_ref, dgain_ref, acc_na, acc_hg):
        i = pl.program_id(0)
        dm = dmix_ref[...].astype(F32)
        dxs = []
        for x_ref, w_ref, gt_ref, dgt_ref, dw_ref, acc in (
                (ona_ref, wna_ref, gna_ref, dgna_ref, dwna_ref, acc_na), (uhg_ref, whg_ref, ghg_ref, dghg_ref, dwhg_ref, acc_hg)):
            xv = x_ref[...]
            y = _dot(xv, w_ref[...])
            sg = jax.nn.sigmoid(gt_ref[...])
            dgt_ref[...] = (dm * y * sg * (1.0 - sg)).astype(BF16)
            dy = (dm * sg).astype(BF16)
            dxs.append(_dot(dy, w_ref[...], NT))
            part = _dot(xv, dy, TN)

            @pl.when(i == 0)
            def _():
                acc[...] = part

            @pl.when(i > 0)
            def _():
                acc[...] += part

            @pl.when(i == ni - 1)
            def _():
                dw_ref[...] = acc[...].astype(BF16)

        dona_ref[...] = dxs[0]
        do, dg, gpart = _hg_post_bwd_rows(dxs[1], of_ref[...] + ob_ref[...], g_ref[...], gain_ref[...], _row_valid(i, TM_B))
        do_ref[...] = do
        dg_ref[...] = dg.astype(BF16)

        @pl.when(i == 0)
        def _():
            dgain_ref[...] = gpart

        @pl.when(i > 0)
        def _():
            dgain_ref[...] += gpart

    act = pl.BlockSpec((TM_B, 512), lambda i: (i, 0))
    wsp = pl.BlockSpec((512, D), lambda i: (0, 0))
    rblk = pl.BlockSpec((TM_B, D), lambda i: (i, 0))
    vec = pl.BlockSpec((1, 512), lambda i: (0, 0))
    return pl.pallas_call(
        body, name="mix_bwd", grid=(ni,),
        in_specs=[act, act, act, act, pl.BlockSpec((TM_B, 512), lambda i: (i, 7)), vec, wsp, wsp,
                  pl.BlockSpec((TM_B, D), lambda i: (i, 4)), pl.BlockSpec((TM_B, D), lambda i: (i, 5)), rblk],
        out_specs=(rblk, rblk, wsp, wsp, act, act, act, vec),
        out_shape=(_sds((T, D), BF16), _sds((T, D), BF16), _sds((512, D), BF16), _sds((512, D), BF16),
                   _sds((T, 512), F32), _sds((T, 512), F32), _sds((T, 512), BF16), _sds((1, 512), F32)),
        scratch_shapes=[pltpu.VMEM((512, D), F32), pltpu.VMEM((512, D), F32)],
        compiler_params=_cp(("arbitrary",)))(o_na, u_hg, o_f, o_b, p_act, gain, w_na, w_hg, p_act, p_act, dmix)


def _wo_fwd(mix, w_o, h0, g_mlp):
    def body(mix_ref, w_ref, h0_ref, g_ref, h1_ref, m_ref):
        h1 = h0_ref[...] + _dot(mix_ref[...], w_ref[...])
        h1_ref[...] = h1
        r = lax.rsqrt(jnp.mean(h1 * h1, axis=-1, keepdims=True) + EPS)
        m_ref[...] = (h1 * r * g_ref[...]).astype(BF16)

    blk = pl.BlockSpec((TM_B, D), lambda i: (i, 0))
    return pl.pallas_call(
        body, name="wo_fwd", grid=(T // TM_B,),
        in_specs=[blk, pl.BlockSpec((D, D), lambda i: (0, 0)), blk, pl.BlockSpec((1, D), lambda i: (0, 0))],
        out_specs=(blk, blk), out_shape=(_sds((T, D), F32), _sds((T, D), BF16)),
        compiler_params=_cp(("parallel",)))(mix, w_o, h0, g_mlp)


def _wo_bwd(dh1_b, w_o, mix):
    ni = T // TM_B

    def body(dh_ref, w_ref, mix_ref, dmix_ref, dw_ref, acc):
        i = pl.program_id(0)
        dh = dh_ref[...]
        dmix_ref[...] = _dot(dh, w_ref[...], NT).astype(BF16)
        part = _dot(mix_ref[...], dh, TN)

        @pl.when(i == 0)
        def _():
            acc[...] = part

        @pl.when(i > 0)
        def _():
            acc[...] += part

        @pl.when(i == ni - 1)
        def _():
            dw_ref[...] = acc[...].astype(BF16)

    blk = pl.BlockSpec((TM_B, D), lambda i: (i, 0))
    wsp = pl.BlockSpec((D, D), lambda i: (0, 0))
    return pl.pallas_call(
        body, name="wo_bwd", grid=(ni,), in_specs=[blk, wsp, blk], out_specs=(blk, wsp),
        out_shape=(_sds((T, D), BF16), _sds((D, D), BF16)), scratch_shapes=[pltpu.VMEM((D, D), F32)],
        compiler_params=_cp(("arbitrary",)))(dh1_b, w_o, mix)


FF_B = D_FF // NDEV


def _loss_rows(xv, gv, tv, row0):
    r_io = lax.broadcasted_iota(jnp.int32, (xv.shape[0], 1), 0) + row0
    valid = (r_io >= NM) & (r_io < L)
    r = lax.rsqrt(jnp.mean(xv * xv, axis=-1, keepdims=True) + EPS)
    xh = xv * r
    err = jnp.where(valid, xh * gv - tv, 0.0)
    lpart = 0.5 * jnp.sum(jnp.sum(err * err, axis=-1, keepdims=True) * (1.0 / D), axis=0, keepdims=True)
    dy = err * (1.0 / D)
    dxh = dy * gv
    dh = r * (dxh - xh * jnp.mean(dxh * xh, axis=-1, keepdims=True))
    return lpart, dh, jnp.sum(dy * xh, axis=0, keepdims=True)


def _mlp_fwd_loss(m, wup_g, wdown_g, h1, g_final, tgt):
    nsub = TM_MM // TM_E

    def body(m_ref, wu_ref, wd_ref, h1_ref, g_ref, t_ref, loss_ref, dh_ref, dhb_ref, dg_ref, h2):
        i, j = pl.program_id(0), pl.program_id(1)
        up = jnp.maximum(_dot(m_ref[...], wu_ref[0]), 0.0)
        part = _dot((up * up).astype(BF16), wd_ref[0])

        @pl.when(j == 0)
        def _():
            h2[...] = h1_ref[...] + part

        @pl.when(j > 0)
        def _():
            h2[...] += part

        @pl.when(j == NDEV - 1)
        def _():
            lsum = jnp.zeros((1, 1), F32)
            gsum = jnp.zeros((1, D), F32)
            for s in range(nsub):
                rows = slice(s * TM_E, (s + 1) * TM_E)
                lpart, dh, gpart = _loss_rows(h2[rows, :], g_ref[...], t_ref[rows, :], i * TM_MM + s * TM_E)
                dh_ref[rows, :] = dh
                dhb_ref[rows, :] = dh.astype(BF16)
                lsum = lsum + lpart
                gsum = gsum + gpart
            lsum = jnp.broadcast_to(lsum, (1, 128))

            @pl.when(i == 0)
            def _():
                loss_ref[...] = lsum
                dg_ref[...] = gsum

            @pl.when(i > 0)
            def _():
                loss_ref[...] += lsum
                dg_ref[...] += gsum

    blk = pl.BlockSpec((TM_MM, D), lambda i, j: (i, 0))
    vec = pl.BlockSpec((1, D), lambda i, j: (0, 0))
    return pl.pallas_call(
        body, name="mlp_fwd_loss", grid=(T // TM_MM, NDEV),
        in_specs=[blk, pl.BlockSpec((1, D, FF_B), lambda i, j: (j, 0, 0)), pl.BlockSpec((1, FF_B, D), lambda i, j: (j, 0, 0)),
                  blk, vec, blk],
        out_specs=(pl.BlockSpec((1, 128), lambda i, j: (0, 0)), blk, blk, vec),
        out_shape=(_sds((1, 128), F32), _sds((T, D), F32), _sds((T, D), BF16), _sds((1, D), F32)),
        scratch_shapes=[pltpu.VMEM((TM_MM, D), F32)],
        compiler_params=_cp(("arbitrary", "arbitrary"), 56))(m, wup_g, wdown_g, h1, g_final, tgt)


def _mlp_bwd(m, dh2_b, wup_g, wdown_g, h1, g_mlp, dh2):
    ni = T // TM_B
    nsub = TM_B // TM_E

    def body(m_ref, dh_ref, wu_ref, wd_ref, h1_ref, g_ref, dres_ref, dwu_ref, dwd_ref, dh1_ref, dh1b_ref, dg_ref,
             dm_ref, acc_u, acc_d):
        j, i = pl.program_id(0), pl.program_id(1)
        rows = pl.ds(pl.multiple_of(i * TM_B, TM_B), TM_B)
        mv, dh = m_ref[...], dh_ref[...]
        r = jnp.maximum(_dot(mv, wu_ref[0]), 0.0)
        act = (r * r).astype(BF16)
        dact = _dot(dh, wd_ref[0], NT)
        dup = (dact * (2.0 * r)).astype(BF16)
        pd = _dot(act, dh, TN)
        pu = _dot(mv, dup, TN)
        dmv = _dot(dup, wu_ref[0], NT)

        @pl.when(i == 0)
        def _():
            acc_u[...] = pu
            acc_d[...] = pd

        @pl.when(i > 0)
        def _():
            acc_u[...] += pu
            acc_d[...] += pd

        @pl.when(i == ni - 1)
        def _():
            dwu_ref[0] = acc_u[...].astype(BF16)
            dwd_ref[0] = acc_d[...].astype(BF16)

        @pl.when(j == 0)
        def _():
            dm_ref[rows, :] = dmv

        @pl.when(j > 0)
        def _():
            dm_ref[rows, :] += dmv

        @pl.when(j == NDEV - 1)
        def _():
            gsum = jnp.zeros((1, D), F32)
            for s in range(nsub):
                sub = slice(s * TM_E, (s + 1) * TM_E)
                dm_rows = dm_ref[pl.ds(pl.multiple_of(i * TM_B + s * TM_E, TM_E), TM_E), :]
                dx, gpart = _norm_bwd_rows(h1_ref[sub, :], g_ref[...], dm_rows, dres_ref[sub, :])
                dh1_ref[sub, :] = dx
                dh1b_ref[sub, :] = dx.astype(BF16)
                gsum = gsum + gpart

            @pl.when(i == 0)
            def _():
                dg_ref[...] = gsum

            @pl.when(i > 0)
            def _():
                dg_ref[...] += gsum

    blk = pl.BlockSpec((TM_B, D), lambda j, i: (i, 0))
    late = pl.BlockSpec((TM_B, D), lambda j, i: (jnp.where(j == NDEV - 1, i, 0), 0))
    vec = pl.BlockSpec((1, D), lambda j, i: (0, 0))
    wus = pl.BlockSpec((1, D, FF_B), lambda j, i: (j, 0, 0))
    wds = pl.BlockSpec((1, FF_B, D), lambda j, i: (j, 0, 0))
    return pl.pallas_call(
        body, name="mlp_bwd", grid=(NDEV, ni), in_specs=[blk, blk, wus, wds, late, vec, late],
        out_specs=(wus, wds, late, late, vec),
        out_shape=(_sds((NDEV, D, FF_B), BF16), _sds((NDEV, FF_B, D), BF16), _sds((T, D), F32), _sds((T, D), BF16),
                   _sds((1, D), F32)),
        scratch_shapes=[pltpu.VMEM((T, D), F32), pltpu.VMEM((D, FF_B), F32), pltpu.VMEM((FF_B, D), F32)],
        compiler_params=_cp(("arbitrary", "arbitrary"), 56))(m, dh2_b, wup_g, wdown_g, h1, g_mlp, dh2)


def _adamw(parts, w, m, v, name):
    rr, cc = w.shape
    tr = rr
    for cand in (256, 128, 64):
        if rr % cand == 0 and rr > cand:
            tr = cand
            break
    c1 = 1.0 - ADAM_B1 ** ADAM_STEP
    c2 = 1.0 - ADAM_B2 ** ADAM_STEP

    def body(p_ref, w_ref, m_ref, v_ref, g_ref, d_ref, nm_ref, nv_ref):
        g = p_ref[0].astype(F32)
        for s in range(1, NDEV):
            g = g + p_ref[s].astype(F32)
        mn = ADAM_B1 * m_ref[...] + (1.0 - ADAM_B1) * g
        vn = ADAM_B2 * v_ref[...] + (1.0 - ADAM_B2) * (g * g)
        g_ref[...] = g
        nm_ref[...] = mn
        nv_ref[...] = vn
        d_ref[...] = -ADAM_LR * ((mn / c1) / (jnp.sqrt(vn / c2) + ADAM_EPS) + ADAM_WD * w_ref[...])

    blk = pl.BlockSpec((tr, cc), lambda i: (i, 0))
    return pl.pallas_call(
        body, name=name, grid=(rr // tr,),
        in_specs=[pl.BlockSpec((NDEV, tr, cc), lambda i: (0, i, 0)), blk, blk, blk],
        out_specs=(blk,) * 4, out_shape=(_sds((rr, cc), F32),) * 4,
        compiler_params=_cp(("parallel",)))(parts, w, m, v)


RPB_N = NA_HEADS * 15 * 31
RPB_PAD = 4096
OWN_ROWS = NM + 8


def _pad_rows(a, rows):
    return jnp.pad(a, ((0, rows - a.shape[0]),) + ((0, 0),) * (a.ndim - 1))


def _pack_owned(meta_blk, lb_blk):
    return jnp.concatenate([meta_blk, _pad_rows(lb_blk.reshape(2, 128), 8)], axis=0)


LOSS_ROW = 28


def _pack_replicated(n_mix, n_mlp, n_final, hg_gain, rpb, loss_row=None):
    flat = _pad_rows(rpb.reshape(RPB_N), RPB_PAD)
    gain8 = _pad_rows(hg_gain.reshape(4, 128), 8)
    if loss_row is not None:
        gain8 = gain8 + jnp.pad(loss_row, ((LOSS_ROW - 24, 31 - LOSS_ROW), (0, 0)))
    return jnp.concatenate([n_mix.reshape(8, 128), n_mlp.reshape(8, 128), n_final.reshape(8, 128), gain8,
                            flat.reshape(32, 128)], axis=0)


def _unpack_replicated(a):
    return (a[0:8].reshape(1, D), a[8:16].reshape(1, D), a[16:24].reshape(D), a[24:28].reshape(1, 512),
            a[32:64].reshape(RPB_PAD)[:RPB_N].reshape(1, NA_HEADS, 15, 31))


def kernel(x, meta_tokens, w_in, w_na_out, w_hg_out, w_o, w_up, w_down, norm_mix, norm_mlp, norm_final, hg_norm, na_rpb, hg_lb_logits, loss_target, m_meta_tokens, m_w_in, m_w_na_out, m_w_hg_out, m_w_o, m_w_up, m_w_down, m_norm_mix, m_norm_mlp, m_norm_final, m_hg_norm, m_na_rpb, m_hg_lb_logits, v_meta_tokens, v_w_in, v_w_na_out, v_w_hg_out, v_w_o, v_w_up, v_w_down, v_norm_mix, v_norm_mlp, v_norm_final, v_hg_norm, v_na_rpb, v_hg_lb_logits):
    owned = _pack_owned(meta_tokens, hg_lb_logits)
    first, tok = _exchange_start([w_in[0].astype(BF16), owned], [False] * 2, "gather_first_start", SAME_CORE_AND_SIBLING)
    bias_tab = _na_bias_table(_tie(jnp.pad(na_rpb[0], ((0, 0), (0, 0), (0, 128 - 31))), tok, "tie_bias_table"))
    later = [w[0].astype(BF16) for w in (w_na_out, w_hg_out, w_o, w_up, w_down)]
    lead = jnp.zeros((NM, D), F32) + tok[0, 0]
    h0_rows = jnp.concatenate([lead, x[0], jnp.zeros((T - L, D), F32)], axis=0)
    tgt = jnp.concatenate([lead, loss_target[0], jnp.zeros((T - L, D), F32)], axis=0)
    first = _exchange_wait(first, [False] * 2, [bias_tab, h0_rows, tgt] + later, "gather_first_wait", SAME_CORE_AND_SIBLING)
    win_g, owned_g = _forward_to_sibling(first, "gather_first_forward")
    later[0] = _tie(later[0], owned_g, "tie_gather_rest")
    gather_rest, tok = _exchange_start(later, [False] * 5, "gather_rest_start")
    win_g = _tie(win_g, tok, "tie_inproj")
    meta_full = jnp.transpose(owned_g[:, 0:NM, :], (1, 0, 2)).reshape(NM, D)
    logits = jnp.transpose(owned_g[:, NM:NM + 2, :].reshape(NDEV, 2, 2, 64), (1, 2, 0, 3)).reshape(2, 2, 512)

    h0 = lax.dynamic_update_slice(h0_rows, meta_full, (0, 0))

    a, a_t = _norm_fwd_t(h0, norm_mix, "norm_mix_fwd")
    p_act = _inproj_fwd(a, win_g)
    o_na, lse = _na_fwd(p_act, bias_tab)
    qh, k_f, b_f, k_b, b_b = _hg_pre(p_act, logits)
    o_f, st_f = _hg_scan_fwd(qh, k_f, b_f, p_act, False)
    o_b, st_b = _hg_scan_fwd(qh, k_b, b_b, p_act, True)
    wna_g, whg_g, wo_g, wup_g, wdown_g = _exchange_wait(gather_rest, [False] * 5, [o_f, o_b, o_na], "gather_rest_wait")
    w_o_full = wo_g.reshape(D, D)
    w_na_full = jnp.transpose(wna_g, (1, 0, 2)).reshape(512, D)
    w_hg_full = jnp.transpose(whg_g, (1, 0, 2)).reshape(512, D)
    mix, u_hg = _mix_fwd(o_na, o_f, o_b, hg_norm, w_na_full, w_hg_full, p_act)
    h1, m_act = _wo_fwd(mix, w_o_full, h0, norm_mlp)
    loss_part, dh2, dh2_b, d_nfinal = _mlp_fwd_loss(m_act, wup_g, wdown_g, h1, norm_final.reshape(1, D), tgt)

    dwup_p, dwdown_p, dh1, dh1_b, d_nmlp = _mlp_bwd(m_act, dh2_b, wup_g, wdown_g, h1, norm_mlp, dh2)
    sc_mlp, tok = _exchange_start([dwup_p, dwdown_p], [True] * 2, "scatter_mlp_start")
    dmix, dwo = _wo_bwd(_tie(dh1_b, tok, "tie_wo_bwd"), w_o_full, mix)
    sc_wo, tok = _exchange_start([dwo.reshape(NDEV, D // NDEV, D)], [True], "scatter_wo_start")
    dgna, dghg, dwna, dwhg, do_na, do_hg, dg_hg, d_gain = _mix_bwd(
        o_na, u_hg, o_f, o_b, hg_norm, w_na_full, w_hg_full, p_act, _tie(dmix, tok, "tie_mix_bwd"))
    owner_cols = lambda w: jnp.transpose(w.reshape(512, NDEV, D // NDEV), (1, 0, 2))
    sc_br, tok = _exchange_start([owner_cols(dwna), owner_cols(dwhg)], [True] * 2, "scatter_branch_start")
    do_hg = _tie(do_hg, tok, "tie_hg_scan_bwd")
    dq_f, dk_f, db_f, dv_f = _hg_scan_bwd(qh, k_f, b_f, p_act, st_f, do_hg, False)
    dq_b, dk_b, db_b, dv_b = _hg_scan_bwd(qh, k_b, b_b, p_act, st_b, do_hg, True)
    dq_hg, dz_f, dz_b, di_hg, d_logits = _hg_pre_bwd(p_act, logits, dq_f, dq_b, dk_f, dk_b, db_f, db_b, dv_f, dv_b)
    dq_na, dk_na, dv_na, dbias = _na_bwd(p_act, do_na, lse, bias_tab)
    dp = jnp.concatenate([dq_na.astype(BF16), dk_na.astype(BF16), dv_na.astype(BF16), dq_hg, dz_f, dz_b, di_hg, dg_hg,
                          dgna, dghg], axis=1)
    dwin_p = _inproj_bwd_dw(a_t, dp)
    sc_in, tok = _exchange_start([dwin_p], [True], "scatter_in_start")
    dh0, d_nmix = _inproj_bwd_da(_tie(dp, tok, "tie_inproj_bwd_da"), win_g, h0, norm_mix, dh1)
    d_rpb = _na_rpb_reduce(_tie(dbias, tok, "tie_rpb_reduce"))[:, :, :31]

    res = {}

    def update(nm, parts, w, mm, vv):
        res[nm] = [r[None] for r in _adamw(parts, w[0], mm[0], vv[0], "adamw_" + nm)]
        return res[nm][1]

    wup_r, wdown_r = _exchange_wait(sc_mlp, [True] * 2, [dh0, d_rpb], "scatter_mlp_wait")
    update("w_up", wup_r, w_up, m_w_up, v_w_up)
    last = update("w_down", wdown_r, w_down, m_w_down, v_w_down)
    (wo_r,) = _exchange_wait(sc_wo, [True], [last], "scatter_wo_wait")
    last = update("w_o", wo_r, w_o, m_w_o, v_w_o)
    wna_r, whg_r = _exchange_wait(sc_br, [True] * 2, [last], "scatter_branch_wait")
    update("w_na_out", wna_r, w_na_out, m_w_na_out, v_w_na_out)
    last = update("w_hg_out", whg_r, w_hg_out, m_w_hg_out, v_w_hg_out)

    d_meta = jnp.transpose(dh0[0:NM].reshape(NM, NDEV, 128), (1, 0, 2))
    d_lg = jnp.transpose(d_logits.reshape(2, 2, NDEV, 64), (2, 0, 1, 3)).reshape(NDEV, 2, 128)
    owned_p = jnp.concatenate([d_meta, jnp.pad(d_lg, ((0, 0), (0, OWN_ROWS - NM - 2), (0, 0)))], axis=1)
    repl_p = _pack_replicated(d_nmix, d_nmlp, d_nfinal, d_gain, d_rpb, loss_part)
    grad_x = dh0[NM:L][None]
    done_first = [grad_x] + [res[nm][0] for nm in ("w_up", "w_down", "w_o", "w_na_out", "w_hg_out")]
    owned_r, repl_r = _exchange([owned_p, repl_p], [True, False], "scatter_small", done_first)
    own = _adamw(owned_r, owned, _pack_owned(m_meta_tokens, m_hg_lb_logits), _pack_owned(v_meta_tokens, v_hg_lb_logits),
                 "adamw_owned_small")
    res["meta_tokens"] = [r[0:NM] for r in own]
    res["hg_lb_logits"] = [r[NM:NM + 2].reshape(2, 2, 64) for r in own]
    rep = _adamw(repl_r, _pack_replicated(norm_mix, norm_mlp, norm_final, hg_norm, na_rpb),
                 _pack_replicated(m_norm_mix, m_norm_mlp, m_norm_final, m_hg_norm, m_na_rpb),
                 _pack_replicated(v_norm_mix, v_norm_mlp, v_norm_final, v_hg_norm, v_na_rpb), "adamw_replicated")
    for q in range(4):
        um = _unpack_replicated(rep[q])
        for nm, val in zip(("norm_mix", "norm_mlp", "norm_final", "hg_norm", "na_rpb"), um):
            res.setdefault(nm, [None] * 4)[q] = val
    (win_r,) = _exchange_wait(sc_in, [True], [rep[1], own[1]], "scatter_in_wait")
    update("w_in", win_r, w_in, m_w_in, v_w_in)

    loss = jnp.sum(repl_r[:, LOSS_ROW, 0])
    order = ("meta_tokens", "w_in", "w_na_out", "w_hg_out", "w_o", "w_up", "w_down", "norm_mix", "norm_mlp", "norm_final",
             "hg_norm", "na_rpb", "hg_lb_logits")
    outs = [loss, grad_x]
    for q in range(4):
        outs += [res[nm][q] for nm in order]
    return tuple(outs)
```

```python
import functools

import numpy as np
import jax
import jax.numpy as jnp
from jax import lax
from jax.experimental import pallas as pl
from jax.experimental.pallas import tpu as pltpu

F32 = jnp.float32
BF16 = jnp.bfloat16

D = 1024
SEQ = 2048
NM = 16
L = SEQ + NM
T = 2176
NDEV = 8
EPS = 1e-6
GRID_W = 64
ROWS = SEQ // GRID_W
NA_HEADS = 8
NA_DH = 64
NA_SCALE = NA_DH ** -0.5
HG_HEADS = 4
HG_C = 16
NCHUNK = L // HG_C
D_FF = 4096
IN_COLS = 6144
NEG = -1e30

ADAM_LR = 0.001
ADAM_B1 = 0.9
ADAM_B2 = 0.999
ADAM_EPS = 1e-08
ADAM_WD = 0.01
ADAM_STEP = 10

MESH_ID = pl.DeviceIdType.MESH
ANY = pl.BlockSpec(memory_space=pl.ANY)

NN = (((1,), (0,)), ((), ()))
NT = (((1,), (1,)), ((), ()))
TN = (((0,), (0,)), ((), ()))


def _cp(sem=None, vmem_mb=48):
    return pltpu.CompilerParams(dimension_semantics=sem, vmem_limit_bytes=vmem_mb * 1024 * 1024)


def _dot(a, b, dims=NN):
    return lax.dot_general(a, b, dims, preferred_element_type=F32)


def _sds(shape, dtype):
    return jax.ShapeDtypeStruct(shape, dtype)


HBM = pl.BlockSpec(memory_space=pltpu.HBM)
SEM = pl.BlockSpec(memory_space=pltpu.SEMAPHORE)
EFFECT = pltpu.SideEffectType.DATAFLOW_SIDE_EFFECTING


def _exchange(arrs, scatter, name, after=()):
    n = len(arrs)
    after = list(after)
    out_shapes = []
    for a, sc in zip(arrs, scatter):
        out_shapes.append(_sds(a.shape if sc else (NDEV,) + a.shape, a.dtype))

    def body(*refs):
        ins, outs = refs[:n], refs[n + len(after):2 * n + len(after)]
        send_sems, recv_sems, loc_sems = refs[2 * n + len(after):]
        me = 4 * lax.axis_index("x") + 2 * lax.axis_index("y") + lax.axis_index("c")
        copies = []
        for k in range(n):
            src_me = ins[k].at[me] if scatter[k] else ins[k]
            loc = pltpu.make_async_copy(src_me, outs[k].at[me], loc_sems.at[k])
            loc.start()
            copies.append(loc)
        remote = _peer_copies(ins, outs, scatter, send_sems, recv_sems)
        for cp in remote:
            cp.start()
        for cp in remote:
            cp.wait_recv()
        for cp in remote:
            cp.wait_send()
        for cp in copies:
            cp.wait()

    return pl.pallas_call(
        body, name=name, out_shape=tuple(out_shapes), in_specs=[ANY] * (n + len(after)), out_specs=tuple([ANY] * n),
        scratch_shapes=[pltpu.SemaphoreType.DMA((n * (NDEV - 1),)), pltpu.SemaphoreType.DMA((n * (NDEV - 1),)),
                        pltpu.SemaphoreType.DMA((n,))],
    )(*arrs, *after)


def _forward_to_sibling(bufs, name):
    n = len(bufs)

    def body(*refs):
        ins, outs = refs[:n], refs[n:2 * n]
        send_sems, recv_sems = refs[2 * n:]
        x, y, c = lax.axis_index("x"), lax.axis_index("y"), lax.axis_index("c")
        copies = []
        for k in range(n):
            for j, (cx, cy) in enumerate(((1 - x, y), (x, 1 - y), (1 - x, 1 - y))):
                slot = 4 * cx + 2 * cy + c
                copies.append(pltpu.make_async_remote_copy(
                    src_ref=ins[k].at[slot], dst_ref=outs[k].at[slot], send_sem=send_sems.at[3 * k + j],
                    recv_sem=recv_sems.at[3 * k + j], device_id=(x, y, 1 - c), device_id_type=MESH_ID))
        for cp in copies:
            cp.start()
        for cp in copies:
            cp.wait_recv()
        for cp in copies:
            cp.wait_send()

    return pl.pallas_call(
        body, name=name, out_shape=tuple(_sds(b.shape, b.dtype) for b in bufs), in_specs=[ANY] * n,
        out_specs=tuple([ANY] * n), input_output_aliases={k: k for k in range(n)},
        scratch_shapes=[pltpu.SemaphoreType.DMA((3 * n,)), pltpu.SemaphoreType.DMA((3 * n,))],
    )(*bufs)


ALL_PEERS = tuple(range(1, NDEV))
SAME_CORE_AND_SIBLING = (1, 2, 4, 6)


def _peer_copies(srcs, lands, scatter, send_sems, recv_sems, masks=ALL_PEERS):
    x, y, c = lax.axis_index("x"), lax.axis_index("y"), lax.axis_index("c")
    me = 4 * x + 2 * y + c
    out = []
    for k in range(len(srcs)):
        for m in masks:
            px, py, pc = x ^ (m >> 2), y ^ ((m >> 1) & 1), c ^ (m & 1)
            src = srcs[k].at[4 * px + 2 * py + pc] if scatter[k] else srcs[k]
            out.append(pltpu.make_async_remote_copy(
                src_ref=src, dst_ref=lands[k].at[me], send_sem=send_sems.at[k * (NDEV - 1) + m - 1],
                recv_sem=recv_sems.at[k * (NDEV - 1) + m - 1],
                device_id=(px, py, pc), device_id_type=MESH_ID))
    return out


def _exchange_start(arrs, scatter, name, masks=ALL_PEERS):
    n = len(arrs)
    me = 4 * lax.axis_index("x") + 2 * lax.axis_index("y") + lax.axis_index("c")
    lands = []
    for a, sc in zip(arrs, scatter):
        own = lax.dynamic_index_in_dim(a, me, 0, keepdims=True) if sc else a[None]
        shape = a.shape if sc else (NDEV,) + a.shape
        lands.append(lax.dynamic_update_index_in_dim(lax.empty(shape, a.dtype), own, me, 0))

    def body(*refs):
        srcs, lnds = refs[:n], refs[n:2 * n]
        send_sems, recv_sems = refs[2 * n], refs[2 * n + 1]
        token = refs[-1]
        for cp in _peer_copies(srcs, lnds, scatter, send_sems, recv_sems, masks):
            cp.start()
        token[...] = jnp.zeros_like(token)

    ops = [pltpu.with_memory_space_constraint(a, pltpu.HBM) for a in list(arrs) + lands]
    res = pl.pallas_call(
        body, name=name,
        out_shape=(pltpu.SemaphoreType.DMA((n * (NDEV - 1),)), pltpu.SemaphoreType.DMA((n * (NDEV - 1),)))
        + tuple(pltpu.HBM(o.shape, o.dtype) for o in ops) + (_sds((8, 128), F32),),
        in_specs=[HBM] * (2 * n), out_specs=(SEM, SEM) + (HBM,) * (2 * n) + (pl.BlockSpec(memory_space=pltpu.VMEM),),
        input_output_aliases={k: 2 + k for k in range(2 * n)},
        compiler_params=pltpu.CompilerParams(has_side_effects=EFFECT),
    )(*ops)
    return res[:-1], res[-1]


def _exchange_wait(handle, scatter, after, name, masks=ALL_PEERS, which=None):
    send_sems, recv_sems = handle[0], handle[1]
    bufs = handle[2:]
    n = len(bufs) // 2
    after = list(after)
    per = len(masks)

    def body(*refs):
        srcs, lnds = refs[:n], refs[n:2 * n]
        copies = _peer_copies(srcs, lnds, scatter, refs[2 * n], refs[2 * n + 1], masks)
        for k in (range(n) if which is None else which):
            for cp in copies[k * per:(k + 1) * per]:
                cp.wait_send()
                cp.wait_recv()

    res = pl.pallas_call(
        body, name=name, out_shape=tuple(pltpu.HBM(b.shape, b.dtype) for b in bufs),
        in_specs=[HBM] * (2 * n) + [SEM, SEM] + [ANY] * len(after), out_specs=(HBM,) * (2 * n),
        input_output_aliases={k: k for k in range(2 * n)},
        compiler_params=pltpu.CompilerParams(has_side_effects=EFFECT),
    )(*bufs, send_sems, recv_sems, *after)
    return res[n:] if which is None else (res[n:], (send_sems, recv_sems) + tuple(res))


def _tie(x, token, name):
    def body(x_ref, t_ref, o_ref):
        del x_ref, t_ref, o_ref

    return pl.pallas_call(body, name=name, out_shape=_sds(x.shape, x.dtype), in_specs=[ANY, ANY], out_specs=ANY,
                          input_output_aliases={0: 0})(x, token)


TM_E = 272


def _norm_fwd_t(h, g, name):
    def body(h_ref, g_ref, o_ref, ot_ref):
        xv = h_ref[...]
        r = lax.rsqrt(jnp.mean(xv * xv, axis=-1, keepdims=True) + EPS)
        y = xv * r * g_ref[...]
        o_ref[...] = y.astype(BF16)
        ot_ref[...] = y.T.astype(BF16)

    return pl.pallas_call(
        body, name=name, grid=(T // 128,),
        in_specs=[pl.BlockSpec((128, D), lambda i: (i, 0)), pl.BlockSpec((1, D), lambda i: (0, 0))],
        out_specs=(pl.BlockSpec((128, D), lambda i: (i, 0)), pl.BlockSpec((D, 128), lambda i: (0, i))),
        out_shape=(_sds((T, D), BF16), _sds((D, T), BF16)), compiler_params=_cp(("parallel",)))(h, g)


def _norm_bwd_rows(xv, gv, dnv, dres):
    r = lax.rsqrt(jnp.mean(xv * xv, axis=-1, keepdims=True) + EPS)
    xh = xv * r
    dxh = dnv * gv
    dx = dres + r * (dxh - xh * jnp.mean(dxh * xh, axis=-1, keepdims=True))
    return dx, jnp.sum(dnv * xh, axis=0, keepdims=True)


TM_MM = 1088


def _inproj_fwd(a, w_g):
    nb = w_g.shape[2]

    def body(a_ref, w_ref, o_ref):
        o_ref[...] = _dot(a_ref[...], w_ref[0])

    return pl.pallas_call(
        body, name="inproj_fwd", grid=(T // TM_MM, NDEV),
        in_specs=[pl.BlockSpec((TM_MM, D), lambda i, j: (i, 0)), pl.BlockSpec((1, D, nb), lambda i, j: (j, 0, 0))],
        out_specs=pl.BlockSpec((TM_MM, nb), lambda i, j: (i, j)), out_shape=_sds((T, NDEV * nb), F32),
        compiler_params=_cp(("parallel", "parallel")))(a, w_g)


TM_B = 544


W_IN_B = IN_COLS // NDEV


def _inproj_bwd_dw(a_t, dp):
    def body(at_ref, dp_ref, dw_ref):
        dw_ref[0] = _dot(at_ref[...], dp_ref[...]).astype(BF16)

    return pl.pallas_call(
        body, name="inproj_bwd_dw", grid=(NDEV,),
        in_specs=[pl.BlockSpec((D, T), lambda j: (0, 0)), pl.BlockSpec((T, W_IN_B), lambda j: (0, j))],
        out_specs=pl.BlockSpec((1, D, W_IN_B), lambda j: (j, 0, 0)), out_shape=_sds((NDEV, D, W_IN_B), BF16),
        compiler_params=_cp(("parallel",)))(a_t, dp)


def _inproj_bwd_da(dp, w_g, h0, g_mix, dh1):
    nsub = TM_MM // TM_E

    def body(dp_ref, w_ref, h0_ref, g_ref, dres_ref, dh0_ref, dg_ref, da):
        i, j = pl.program_id(0), pl.program_id(1)
        dav = _dot(dp_ref[...], w_ref[0], NT)

        @pl.when(j == 0)
        def _():
            da[...] = dav

        @pl.when(j > 0)
        def _():
            da[...] += dav

        @pl.when(j == NDEV - 1)
        def _():
            gsum = jnp.zeros((1, D), F32)
            for s in range(nsub):
                sub = slice(s * TM_E, (s + 1) * TM_E)
                dx, gpart = _norm_bwd_rows(h0_ref[sub, :], g_ref[...], da[sub, :], dres_ref[sub, :])
                dh0_ref[sub, :] = dx
                gsum = gsum + gpart

            @pl.when(i == 0)
            def _():
                dg_ref[...] = gsum

            @pl.when(i > 0)
            def _():
                dg_ref[...] += gsum

    rblk = pl.BlockSpec((TM_MM, D), lambda i, j: (i, 0))
    vec = pl.BlockSpec((1, D), lambda i, j: (0, 0))
    return pl.pallas_call(
        body, name="inproj_bwd_da", grid=(T // TM_MM, NDEV),
        in_specs=[pl.BlockSpec((TM_MM, W_IN_B), lambda i, j: (i, j)), pl.BlockSpec((1, D, W_IN_B), lambda i, j: (j, 0, 0)),
                  rblk, vec, rblk],
        out_specs=(rblk, vec), out_shape=(_sds((T, D), F32), _sds((1, D), F32)),
        scratch_shapes=[pltpu.VMEM((TM_MM, D), F32)],
        compiler_params=_cp(("arbitrary", "arbitrary"), 56))(dp, w_g, h0, g_mix, dh1)


NA_QB = 256
NA_GROUPS = ROWS // 4
NA_UROWS = 11
NA_KW = NA_UROWS * GRID_W
NA_KU = 768


def _na_row_offset(var, i, j):
    valid = (j < 8, i <= j < i + 8, 3 <= j < NA_UROWS)[var]
    return (j - i + (7, 3, 0)[var]) if valid else None


def _na_bias_table(rp):
    def body(r_ref, o_ref):
        row3 = lax.broadcasted_iota(jnp.int32, (15, GRID_W, 128), 1)
        lane3 = lax.broadcasted_iota(jnp.int32, (15, GRID_W, 128), 2)
        w3 = lane3 & (GRID_W - 1)
        cs3 = jnp.clip(row3 - 8, 0, GRID_W - 16)
        lane = lax.broadcasted_iota(jnp.int32, (GRID_W, 128), 1)
        neg = jnp.full((GRID_W, 128), NEG, F32)
        z = jnp.stack([jnp.broadcast_to(r_ref[0, a:a + 1, :], (GRID_W, 128)) for a in range(15)])
        for bit in range(6):
            sh = 1 << bit
            z = jnp.where((row3 & sh) != 0, jnp.roll(z, sh, axis=2), z)
        z = jnp.roll(z, 128 - 15, axis=2)
        z = jnp.where(lane3 < GRID_W, z, 0.0)
        z = z + jnp.roll(z, GRID_W, axis=2)
        tabs = jnp.where((w3 >= cs3) & (w3 < cs3 + 16), z, NEG)
        tail = jnp.where(lane < GRID_W + NM, 0.0, NEG)
        for var in range(3):
            for i in range(4):
                for jp in range(NA_KU // 128):
                    halves = []
                    for j in (2 * jp, 2 * jp + 1):
                        a = _na_row_offset(var, i, j) if j < NA_UROWS else None
                        halves.append(tail if j >= NA_UROWS else (neg if a is None else tabs[a]))
                    o_ref[var, 0, i * 64:(i + 1) * 64, jp * 128:(jp + 1) * 128] = jnp.where(lane < GRID_W, halves[0], halves[1])

    return pl.pallas_call(
        body, name="na_bias_table", grid=(NA_HEADS,),
        in_specs=[pl.BlockSpec((1, 15, 128), lambda h: (h, 0, 0))],
        out_specs=pl.BlockSpec((3, 1, NA_QB, NA_KU), lambda h: (0, h, 0, 0)),
        out_shape=_sds((3, NA_HEADS, NA_QB, NA_KU), F32), compiler_params=_cp(("parallel",)))(rp)


def _na_var(g):
    return jnp.where(g == 0, 0, jnp.where(g == NA_GROUPS - 1, 2, 1))


def _na_load_window(src_ref, dst, g):
    us = jnp.clip(4 * g - 4, 0, ROWS - NA_UROWS)
    kstart = pl.multiple_of(NM + GRID_W * us, 16)
    dst[0:NA_KW, :] = src_ref[pl.ds(kstart, NA_KW), :].astype(BF16)
    dst[NA_KW:NA_KW + NM, :] = src_ref[0:NM, :].astype(BF16)
    dst[NA_KW + NM:, :] = jnp.zeros((NA_KU - NA_KW - NM, 128), BF16)
    return kstart


def _na_fwd(p_act, bias_tab):
    def body(q_ref, k_ref, v_ref, b_ref, o_ref, lse_ref, ku, vu):
        g = pl.program_id(1)
        _na_load_window(k_ref, ku, g)
        _na_load_window(v_ref, vu, g)
        qstart = pl.multiple_of(NM + NA_QB * g, 16)
        q = q_ref[pl.ds(qstart, NA_QB), :]
        lane = lax.broadcasted_iota(jnp.int32, (NA_QB, 128), 1)
        o_h, lse_h = [], []
        for h in range(2):
            hm = (lane < 64) if h == 0 else (lane >= 64)
            qm = (jnp.where(hm, q, 0.0) * NA_SCALE).astype(BF16)
            s = _dot(qm, ku[...], NT) + b_ref[0, h]
            m = jnp.max(s, axis=-1, keepdims=True)
            p = jnp.exp(s - m)
            l = jnp.sum(p, axis=-1, keepdims=True)
            o_h.append(_dot(p.astype(BF16), vu[...]) / l)
            lse_h.append(jnp.broadcast_to(m + jnp.log(l), (NA_QB, 128)))
        o_ref[pl.ds(qstart, NA_QB), :] = jnp.where(lane < 64, o_h[0], o_h[1]).astype(BF16)
        lse_ref[0, pl.ds(qstart, NA_QB), :] = jnp.where(lane < 64, lse_h[0], lse_h[1])

        @pl.when(g == 0)
        def _():
            qm_ = q_ref[0:NM, :]
            lane_m = lax.broadcasted_iota(jnp.int32, (NM, 128), 1)
            km, vm = ku[NA_KW:NA_KW + NM, :], vu[NA_KW:NA_KW + NM, :]
            om = []
            for h in range(2):
                hm = (lane_m < 64) if h == 0 else (lane_m >= 64)
                s = _dot(jnp.where(hm, qm_, 0.0).astype(BF16), km, NT) * NA_SCALE
                p = jnp.exp(s - jnp.max(s, axis=-1, keepdims=True))
                l = jnp.sum(p, axis=-1, keepdims=True)
                om.append(_dot(p.astype(BF16), vm) / l)
            o_ref[0:NM, :] = jnp.where(lane_m < 64, om[0], om[1]).astype(BF16)
            o_ref[L:T, :] = jnp.zeros((T - L, 128), BF16)
            lse_ref[0, 0:NM, :] = jnp.zeros((NM, 128), F32)
            lse_ref[0, L:T, :] = jnp.zeros((T - L, 128), F32)

    col = lambda off: pl.BlockSpec((T, 128), lambda hp, g: (0, off + hp))
    return pl.pallas_call(
        body, name="na_fwd", grid=(4, NA_GROUPS),
        in_specs=[col(0), col(4), col(8),
                  pl.BlockSpec((1, 2, NA_QB, NA_KU), lambda hp, g: (_na_var(g), hp, 0, 0))],
        out_specs=(pl.BlockSpec((T, 128), lambda hp, g: (0, hp)), pl.BlockSpec((1, T, 128), lambda hp, g: (hp, 0, 0))),
        out_shape=(_sds((T, 512), BF16), _sds((4, T, 128), F32)),
        scratch_shapes=[pltpu.VMEM((NA_KU, 128), BF16), pltpu.VMEM((NA_KU, 128), BF16)],
        compiler_params=_cp(("parallel", "arbitrary")))(p_act, p_act, p_act, bias_tab)


def _na_bwd(p_act, do, lse, bias_tab):
    def body(q_ref, k_ref, v_ref, do_ref, lse_ref, b_ref, dq_ref, dk_ref, dv_ref, db_ref, ku, vu):
        g = pl.program_id(1)

        @pl.when(g == 0)
        def _():
            dq_ref[...] = jnp.zeros((T, 128), F32)
            dk_ref[...] = jnp.zeros((T, 128), F32)
            dv_ref[...] = jnp.zeros((T, 128), F32)

        kstart = _na_load_window(k_ref, ku, g)
        _na_load_window(v_ref, vu, g)
        qstart = pl.multiple_of(NM + NA_QB * g, 16)
        q = q_ref[pl.ds(qstart, NA_QB), :]
        dov = do_ref[pl.ds(qstart, NA_QB), :]
        lsev = lse_ref[0, pl.ds(qstart, NA_QB), :]
        lane = lax.broadcasted_iota(jnp.int32, (NA_QB, 128), 1)
        first = (g == 0) | (g == 1) | (g == NA_GROUPS - 1)
        dq_h = []
        dku = jnp.zeros((NA_KU, 128), F32)
        dvu = jnp.zeros((NA_KU, 128), F32)
        for h in range(2):
            hm = (lane < 64) if h == 0 else (lane >= 64)
            qm = (jnp.where(hm, q, 0.0) * NA_SCALE).astype(BF16)
            dom = jnp.where(hm, dov, 0.0).astype(BF16)
            s = _dot(qm, ku[...], NT) + b_ref[0, h]
            p = jnp.exp(s - lsev[:, 64 * h:64 * h + 1])
            dp = _dot(dom, vu[...], NT)
            delta = jnp.sum(p * dp, axis=-1, keepdims=True)
            ds = p * (dp - delta)

            @pl.when(first)
            def _():
                db_ref[0, h] = ds

            @pl.when(jnp.logical_not(first))
            def _():
                db_ref[0, h] += ds

            dsb = ds.astype(BF16)
            dq_h.append(_dot(dsb, ku[...]) * NA_SCALE)
            dku = dku + _dot(dsb, qm, TN)
            dvu = dvu + _dot(p.astype(BF16), dom, TN)
        dq_ref[pl.ds(qstart, NA_QB), :] = jnp.where(lane < 64, dq_h[0], dq_h[1])
        dk_ref[pl.ds(kstart, NA_KW), :] += dku[0:NA_KW]
        dv_ref[pl.ds(kstart, NA_KW), :] += dvu[0:NA_KW]
        dk_ref[0:NM, :] += dku[NA_KW:NA_KW + NM]
        dv_ref[0:NM, :] += dvu[NA_KW:NA_KW + NM]

        @pl.when(g == 0)
        def _():
            qm_ = q_ref[0:NM, :]
            dom_ = do_ref[0:NM, :]
            lane_m = lax.broadcasted_iota(jnp.int32, (NM, 128), 1)
            km, vm = ku[NA_KW:NA_KW + NM, :], vu[NA_KW:NA_KW + NM, :]
            dqs = []
            dkm = jnp.zeros((NM, 128), F32)
            dvm = jnp.zeros((NM, 128), F32)
            for h in range(2):
                hm = (lane_m < 64) if h == 0 else (lane_m >= 64)
                qh = jnp.where(hm, qm_, 0.0).astype(BF16)
                doh = jnp.where(hm, dom_, 0.0).astype(BF16)
                s = _dot(qh, km, NT) * NA_SCALE
                e = jnp.exp(s - jnp.max(s, axis=-1, keepdims=True))
                p = e / jnp.sum(e, axis=-1, keepdims=True)
                dp = _dot(doh, vm, NT)
                ds = p * (dp - jnp.sum(p * dp, axis=-1, keepdims=True))
                dsb = (ds * NA_SCALE).astype(BF16)
                dqs.append(_dot(dsb, km))
                dkm = dkm + _dot(dsb, qh, TN)
                dvm = dvm + _dot(p.astype(BF16), doh, TN)
            dq_ref[0:NM, :] = jnp.where(lane_m < 64, dqs[0], dqs[1])
            dk_ref[0:NM, :] += dkm
            dv_ref[0:NM, :] += dvm

    col = lambda off: pl.BlockSpec((T, 128), lambda hp, g: (0, off + hp))
    ocol = pl.BlockSpec((T, 128), lambda hp, g: (0, hp))
    bspec = pl.BlockSpec((1, 2, NA_QB, NA_KU), lambda hp, g: (_na_var(g), hp, 0, 0))
    return pl.pallas_call(
        body, name="na_bwd", grid=(4, NA_GROUPS),
        in_specs=[col(0), col(4), col(8), ocol, pl.BlockSpec((1, T, 128), lambda hp, g: (hp, 0, 0)), bspec],
        out_specs=(ocol, ocol, ocol, bspec),
        out_shape=(_sds((T, 512), F32), _sds((T, 512), F32), _sds((T, 512), F32), _sds((3, NA_HEADS, NA_QB, NA_KU), F32)),
        scratch_shapes=[pltpu.VMEM((NA_KU, 128), BF16), pltpu.VMEM((NA_KU, 128), BF16)],
        compiler_params=_cp(("parallel", "arbitrary")))(p_act, p_act, p_act, do, lse, bias_tab)


def _na_rpb_reduce(dbias):
    def body(db_ref, o_ref):
        lane = lax.broadcasted_iota(jnp.int32, (GRID_W, 128), 1)
        row3 = lax.broadcasted_iota(jnp.int32, (15, GRID_W, 128), 1)
        lane3 = lax.broadcasted_iota(jnp.int32, (15, GRID_W, 128), 2)
        accs = []
        for a in range(15):
            acc = jnp.zeros((GRID_W, 128), F32)
            for var in range(3):
                for i in range(4):
                    for j in range(NA_UROWS):
                        if _na_row_offset(var, i, j) == a:
                            pair = db_ref[var, 0, i * 64:(i + 1) * 64, (j // 2) * 128:(j // 2 + 1) * 128]
                            acc = acc + jnp.where((lane < GRID_W) if j % 2 == 0 else (lane >= GRID_W), pair, 0.0)
            accs.append(acc)
        z = jnp.stack(accs)
        z = jnp.where(lane3 < GRID_W, z + jnp.roll(z, GRID_W, axis=2), 0.0)
        for bit in range(6):
            sh = 1 << bit
            z = jnp.where((row3 & sh) != 0, jnp.roll(z, 128 - sh, axis=2), z)
        z = jnp.roll(z, 15, axis=2)
        o_ref[0] = jnp.sum(z, axis=1)

    return pl.pallas_call(
        body, name="na_rpb_reduce", grid=(NA_HEADS,),
        in_specs=[pl.BlockSpec((3, 1, NA_QB, NA_KU), lambda h: (0, h, 0, 0))],
        out_specs=pl.BlockSpec((1, 15, 128), lambda h: (h, 0, 0)), out_shape=_sds((NA_HEADS, 15, 128), F32),
        compiler_params=_cp(("parallel",)))(dbias)


HG_RB = 128
HG_NB = T // HG_RB
HG_SLOTS = HG_NB * 8
HI = lax.Precision.HIGHEST
HG_UNROLL = 4


def _chunk_tri(lower):
    r = lax.broadcasted_iota(jnp.int32, (HG_RB, HG_RB), 0)
    c = lax.broadcasted_iota(jnp.int32, (HG_RB, HG_RB), 1)
    same = (r // HG_C) == (c // HG_C)
    keep = (c <= r) if lower else (c >= r)
    return jnp.where(same & keep, 1.0, 0.0).astype(F32)


def _hg_gate_terms(z, lg):
    dl = lg[0:1, :] - lg[1:2, :]
    log_lb = jax.nn.log_sigmoid(dl)
    log_1mlb = jax.nn.log_sigmoid(-dl)
    yz = log_1mlb + jax.nn.log_sigmoid(z)
    log_f = jnp.logaddexp(log_lb, yz)
    snz = jax.nn.sigmoid(-z)
    k = jnp.exp(log_1mlb) * snz
    w2 = jnp.exp(yz - log_f)
    return log_f, k, snz, w2


def _hg_pre(p_act, logits):
    def body(q_ref, zf_ref, zb_ref, lg_ref, qh_ref, kf_ref, bf_ref, kb_ref, bb_ref):
        qh_ref[...] = jax.nn.silu(q_ref[...])
        lf, kf, _, _ = _hg_gate_terms(zf_ref[...], lg_ref[0])
        kf_ref[...] = kf
        bf_ref[...] = jnp.dot(_chunk_tri(True), lf, precision=HI, preferred_element_type=F32)
        lb_, kb, _, _ = _hg_gate_terms(zb_ref[...], lg_ref[1])
        kb_ref[...] = kb
        bb_ref[...] = jnp.dot(_chunk_tri(False), lb_, precision=HI, preferred_element_type=F32)

    blk = lambda c: pl.BlockSpec((HG_RB, 512), lambda i: (i, c))
    ob = pl.BlockSpec((HG_RB, 512), lambda i: (i, 0))
    return pl.pallas_call(
        body, name="hg_pre", grid=(HG_NB,),
        in_specs=[blk(3), blk(4), blk(5), pl.BlockSpec((2, 2, 512), lambda i: (0, 0, 0))],
        out_specs=(ob,) * 5, out_shape=(_sds((T, 512), F32),) * 5,
        compiler_params=_cp(("parallel",)))(p_act, p_act, p_act, logits)


def _bdot(a, b, ca, cb):
    return lax.dot_general(a.astype(BF16), b.astype(BF16), (((ca,), (cb,)), ((0,), (0,))), preferred_element_type=F32)


HG_S = 8
HG_NS = HG_RB // HG_S


def _lane_sums(xs):
    l_io = lax.broadcasted_iota(jnp.int32, (HG_NS, HG_S, HG_S), 2)
    a = jnp.zeros((HG_NS, HG_S, HG_S), F32)
    for j, x in enumerate(xs):
        a = a + jnp.where(l_io == j, jnp.sum(x, axis=-1, keepdims=True), 0.0)
    return a


def _halves(x):
    y = x.reshape(8, 2, HG_S, x.shape[-1])
    return y[:, 0], y[:, 1]


def _join(first, second):
    return jnp.stack([first, second], axis=1).reshape(HG_RB, first.shape[-1])


def _cross_split(rev, b4):
    b_1, b_2 = _halves(b4)
    if rev:
        r = b_2[:, 0:1, :]
        return jnp.exp(b_1 - r), jnp.exp(r - b_2)
    r = b_1[:, HG_S - 1:HG_S, :]
    return jnp.exp(b_2 - r), jnp.exp(r - b_1)


def _hg_scan_fwd(qh, k, b, p_act, rev):
    anchor = 0 if rev else HG_C - 1

    def body(q_ref, k_ref, b_ref, v_ref, o_ref, st_ref, dsc):
        def phase_a(blk, _):
            rows = pl.ds(pl.multiple_of(blk * HG_RB, HG_RB), HG_RB)
            b3 = b_ref[rows, :].reshape(8, HG_C, 128)
            k3 = k_ref[rows, :].reshape(8, HG_C, 128)
            v3 = v_ref[rows, :].reshape(8, HG_C, 128)
            bl = b3[:, anchor:anchor + 1, :]
            kt = k3 * jnp.exp(bl - b3)
            st_ref[0, pl.ds(pl.multiple_of(blk * 8, 8), 8)] = _bdot(v3, kt, 1, 1)
            dsc[pl.ds(pl.multiple_of(blk * 8, 8), 8), :] = jnp.exp(bl[:, 0, :])
            return 0

        lax.fori_loop(0, HG_NB, phase_a, 0, unroll=HG_UNROLL)

        def phase_b(n, carry):
            c = (NCHUNK - 1 - n) if rev else n
            u = st_ref[0, c]
            st_ref[0, c] = carry
            return carry * dsc[pl.ds(c, 1), :] + u

        lax.fori_loop(0, NCHUNK // 3, lambda n3, s: phase_b(3 * n3 + 2, phase_b(3 * n3 + 1, phase_b(3 * n3, s))),
                      jnp.zeros((128, 128), F32))
        for c in range(NCHUNK, HG_SLOTS):
            st_ref[0, c] = jnp.zeros((128, 128), F32)

        t_io = lax.broadcasted_iota(jnp.int32, (HG_NS, HG_S, 128), 1)

        def phase_c(blk, _):
            rows = pl.ds(pl.multiple_of(blk * HG_RB, HG_RB), HG_RB)
            b4 = b_ref[rows, :].reshape(HG_NS, HG_S, 128)
            k4 = k_ref[rows, :].reshape(HG_NS, HG_S, 128)
            q4 = q_ref[rows, :].reshape(HG_NS, HG_S, 128)
            v4 = v_ref[rows, :].reshape(HG_NS, HG_S, 128)
            st = st_ref[0, pl.ds(pl.multiple_of(blk * 8, 8), 8)]
            o = _bdot((q4 * jnp.exp(b4)).reshape(8, HG_C, 128), st, 2, 2).reshape(HG_RB, 128)
            terms = []
            for s in range(HG_S):
                ok = (t_io <= s) if rev else (t_io >= s)
                f = jnp.exp(jnp.where(ok, b4 - b4[:, s:s + 1, :], NEG))
                terms.append(q4 * f * k4[:, s:s + 1, :])
            o_in = _bdot(_lane_sums(terms), v4, 2, 1)
            wq, wk = _cross_split(rev, b4)
            q_1, q_2 = _halves(q4)
            k_1, k_2 = _halves(k4)
            v_1, v_2 = _halves(v4)
            o_1, o_2 = _halves(o_in)
            if rev:
                o_1 = o_1 + _bdot(_bdot(q_1 * wq, k_2 * wk, 2, 2), v_2, 2, 1)
            else:
                o_2 = o_2 + _bdot(_bdot(q_2 * wq, k_1 * wk, 2, 2), v_1, 2, 1)
            o_ref[rows, :] = o + _join(o_1, o_2)
            return 0

        lax.fori_loop(0, HG_NB, phase_c, 0, unroll=HG_UNROLL)

    col = pl.BlockSpec((T, 128), lambda h: (0, h))
    return pl.pallas_call(
        body, name="hg_scan_bwd_dir" if rev else "hg_scan_fwd_dir", grid=(HG_HEADS,),
        in_specs=[col, col, col, pl.BlockSpec((T, 128), lambda h: (0, 24 + h))],
        out_specs=(col, pl.BlockSpec((1, HG_SLOTS, 128, 128), lambda h: (h, 0, 0, 0))),
        out_shape=(_sds((T, 512), F32), _sds((HG_HEADS, HG_SLOTS, 128, 128), F32)),
        scratch_shapes=[pltpu.VMEM((HG_SLOTS, 128), F32)],
        compiler_params=_cp(("parallel",), 56))(qh, k, b, p_act)


def _hg_scan_bwd(qh, k, b, p_act, st, do, rev):
    anchor = 0 if rev else HG_C - 1

    def body(q_ref, k_ref, b_ref, v_ref, st_ref, do_ref, dq_ref, dk_ref, db_ref, dv_ref, gst, dsc, dbl):
        def phase_a(blk, _):
            rows = pl.ds(pl.multiple_of(blk * HG_RB, HG_RB), HG_RB)
            b3 = b_ref[rows, :].reshape(8, HG_C, 128)
            q3 = q_ref[rows, :].reshape(8, HG_C, 128)
            do3 = do_ref[rows, :].reshape(8, HG_C, 128)
            gst[pl.ds(pl.multiple_of(blk * 8, 8), 8)] = _bdot(do3, q3 * jnp.exp(b3), 1, 1)
            dsc[pl.ds(pl.multiple_of(blk * 8, 8), 8), :] = jnp.exp(b3[:, anchor, :])
            return 0

        lax.fori_loop(0, HG_NB, phase_a, 0, unroll=HG_UNROLL)

        def phase_b(n, carry):
            c = n if rev else (NCHUNK - 1 - n)
            w = gst[c]
            gst[c] = carry
            dcv = dsc[pl.ds(c, 1), :]
            dbl[pl.ds(c, 1), :] = dcv * jnp.sum(st_ref[0, c] * carry, axis=0, keepdims=True)
            return carry * dcv + w

        lax.fori_loop(0, NCHUNK // 3, lambda n3, s: phase_b(3 * n3 + 2, phase_b(3 * n3 + 1, phase_b(3 * n3, s))),
                      jnp.zeros((128, 128), F32))
        for c in range(NCHUNK, HG_SLOTS):
            gst[c] = jnp.zeros((128, 128), F32)
            dbl[c:c + 1, :] = jnp.zeros((1, 128), F32)

        t_io = lax.broadcasted_iota(jnp.int32, (HG_NS, HG_S, 128), 1)
        t16 = lax.broadcasted_iota(jnp.int32, (8, HG_C, 128), 1)
        r_io = lax.broadcasted_iota(jnp.int32, (HG_NS, HG_S, HG_S), 1)
        l_io = lax.broadcasted_iota(jnp.int32, (HG_NS, HG_S, HG_S), 2)

        def phase_c(blk, _):
            rows = pl.ds(pl.multiple_of(blk * HG_RB, HG_RB), HG_RB)
            cs = pl.ds(pl.multiple_of(blk * 8, 8), 8)
            b4 = b_ref[rows, :].reshape(HG_NS, HG_S, 128)
            k4 = k_ref[rows, :].reshape(HG_NS, HG_S, 128)
            q4 = q_ref[rows, :].reshape(HG_NS, HG_S, 128)
            v4 = v_ref[rows, :].reshape(HG_NS, HG_S, 128)
            do4 = do_ref[rows, :].reshape(HG_NS, HG_S, 128)
            b3, k3, q3 = (z.reshape(8, HG_C, 128) for z in (b4, k4, q4))
            v3, do3 = v4.reshape(8, HG_C, 128), do4.reshape(8, HG_C, 128)
            s_t = st_ref[0, cs]
            g_t = gst[cs]
            bl = b3[:, anchor:anchor + 1, :]
            ekl = jnp.exp(bl - b3)
            kt = k3 * ekl
            dkt = _bdot(v3, g_t, 2, 1)
            dq = (_bdot(do3, s_t, 2, 1) * jnp.exp(b3)).reshape(HG_NS, HG_S, 128)
            dk = (dkt * ekl).reshape(HG_NS, HG_S, 128)
            dv = _bdot(kt, g_t, 2, 2).reshape(HG_NS, HG_S, 128)
            dbl3 = dbl[cs, :].reshape(8, 1, 128) + jnp.sum(dkt * kt, axis=1, keepdims=True)
            causal = (l_io >= r_io) if rev else (l_io <= r_io)
            da = jnp.where(causal, _bdot(do4, v4, 2, 2), 0.0)
            causal_t = (l_io <= r_io) if rev else (l_io >= r_io)
            dat = jnp.where(causal_t, _bdot(v4, do4, 2, 2), 0.0)
            for s in range(HG_S):
                ok = (t_io <= s) if rev else (t_io >= s)
                f = jnp.exp(jnp.where(ok, b4 - b4[:, s:s + 1, :], NEG))
                dq = dq + da[:, :, s:s + 1] * (f * k4[:, s:s + 1, :])
            terms = []
            for t in range(HG_S):
                ok = (t_io >= t) if rev else (t_io <= t)
                e = jnp.exp(jnp.where(ok, b4[:, t:t + 1, :] - b4, NEG))
                eq = e * q4[:, t:t + 1, :]
                dk = dk + dat[:, :, t:t + 1] * eq
                terms.append(eq * k4)
            dv = dv + _bdot(_lane_sums(terms), do4, 2, 1)
            wq, wk = _cross_split(rev, b4)
            pick = (lambda z: _halves(z)) if rev else (lambda z: _halves(z)[::-1])
            (q_q, _), (_, k_k), (_, v_k), (do_q, _) = pick(q4), pick(k4), pick(v4), pick(do4)
            qx, kx = q_q * wq, k_k * wk
            dq_q = _bdot(_bdot(do_q, v_k, 2, 2), kx, 2, 1) * wq
            dk_k = _bdot(_bdot(v_k, do_q, 2, 2), qx, 2, 1) * wk
            dv_k = _bdot(_bdot(kx, qx, 2, 2), do_q, 2, 1)
            zero = jnp.zeros((8, HG_S, 128), F32)
            place_q = (lambda z: _join(z, zero)) if rev else (lambda z: _join(zero, z))
            place_k = (lambda z: _join(zero, z)) if rev else (lambda z: _join(z, zero))
            dq2 = dq.reshape(HG_RB, 128) + place_q(dq_q)
            dk2 = dk.reshape(HG_RB, 128) + place_k(dk_k)
            dv2 = dv.reshape(HG_RB, 128) + place_k(dv_k)
            dq3, dk3 = dq2.reshape(8, HG_C, 128), dk2.reshape(8, HG_C, 128)
            db = q3 * dq3 - k3 * dk3 + jnp.where(t16 == anchor, dbl3, 0.0)
            dq_ref[rows, :] = dq2
            dk_ref[rows, :] = dk2
            db_ref[rows, :] = db.reshape(HG_RB, 128)
            dv_ref[rows, :] = dv2
            return 0

        lax.fori_loop(0, HG_NB, phase_c, 0, unroll=HG_UNROLL)

    col = pl.BlockSpec((T, 128), lambda h: (0, h))
    return pl.pallas_call(
        body, name="hg_scan_bwd_dir_bwd" if rev else "hg_scan_fwd_dir_bwd", grid=(HG_HEADS,),
        in_specs=[col, col, col, pl.BlockSpec((T, 128), lambda h: (0, 24 + h)),
                  pl.BlockSpec((1, HG_SLOTS, 128, 128), lambda h: (h, 0, 0, 0)), col],
        out_specs=(col,) * 4, out_shape=(_sds((T, 512), F32),) * 4,
        scratch_shapes=[pltpu.VMEM((HG_SLOTS, 128, 128), F32), pltpu.VMEM((HG_SLOTS, 128), F32),
                        pltpu.VMEM((HG_SLOTS, 128), F32)],
        compiler_params=_cp(("parallel",), 56))(qh, k, b, p_act, st, do)


def _row_valid(i, tm):
    r = lax.broadcasted_iota(jnp.int32, (tm, 1), 0) + i * tm
    return r < L


def _hg_post_rows(o, gv, gain_v, valid):
    parts = []
    for h in range(HG_HEADS):
        oh = o[:, 128 * h:128 * (h + 1)]
        parts.append(oh * lax.rsqrt(jnp.mean(oh * oh, axis=-1, keepdims=True) + EPS))
    return jnp.where(valid, jnp.concatenate(parts, axis=1) * gain_v * jax.nn.silu(gv), 0.0)


def _hg_post_bwd_rows(du, o, gv, gain_v, valid):
    duv = jnp.where(valid, du, 0.0)
    sig = jax.nn.sigmoid(gv)
    sg = gv * sig
    dn = duv * gain_v * sg
    do_parts, n_parts = [], []
    for h in range(HG_HEADS):
        sl = slice(128 * h, 128 * (h + 1))
        oh = o[:, sl]
        r = lax.rsqrt(jnp.mean(oh * oh, axis=-1, keepdims=True) + EPS)
        nh = oh * r
        dnh = dn[:, sl]
        do_parts.append(r * (dnh - nh * jnp.mean(dnh * nh, axis=-1, keepdims=True)))
        n_parts.append(nh)
    n = jnp.where(valid, jnp.concatenate(n_parts, axis=1), 0.0)
    do = jnp.where(valid, jnp.concatenate(do_parts, axis=1), 0.0)
    dg = duv * n * gain_v * (sig * (1.0 + gv * (1.0 - sig)))
    return do, dg, jnp.sum(duv * n * sg, axis=0, keepdims=True)


def _hg_pre_bwd(p_act, logits, dq_f, dq_b, dk_f, dk_b, db_f, db_b, dv_f, dv_b):
    def body(q_ref, zf_ref, zb_ref, lg_ref, dqf_ref, dqb_ref, dkf_ref, dkb_ref, dbf_ref, dbb_ref, dvf_ref, dvb_ref,
             dq_ref, dzf_ref, dzb_ref, di_ref, dlg_ref):
        i = pl.program_id(0)
        valid = _row_valid(i, HG_RB)
        qv = q_ref[...]
        sig = jax.nn.sigmoid(qv)
        dq_ref[...] = jnp.where(valid, (dqf_ref[...] + dqb_ref[...]) * (sig * (1.0 + qv * (1.0 - sig))), 0.0).astype(BF16)
        di_ref[...] = jnp.where(valid, dvf_ref[...] + dvb_ref[...], 0.0).astype(BF16)
        for d, (z_ref, dk_r, db_r, dz_ref) in enumerate(((zf_ref, dkf_ref, dbf_ref, dzf_ref), (zb_ref, dkb_ref, dbb_ref, dzb_ref))):
            lg = lg_ref[d]
            dl = lg[0:1, :] - lg[1:2, :]
            lb = jax.nn.sigmoid(dl)
            one_m_lb = jax.nn.sigmoid(-dl)
            log_f, _, snz, w2 = _hg_gate_terms(z_ref[...], lg)
            dbv = jnp.where(valid, db_r[...], 0.0)
            dkv = jnp.where(valid, dk_r[...], 0.0)
            dlf = jnp.dot(_chunk_tri(d == 1), dbv, precision=HI, preferred_element_type=F32)
            sz = 1.0 - snz
            dz_ref[...] = (dlf * w2 * snz - dkv * one_m_lb * sz * snz).astype(BF16)
            dlb = jnp.sum(dlf * snz * jnp.exp(-log_f) - dkv * snz, axis=0, keepdims=True)
            dl0 = dlb * lb * one_m_lb
            part = jnp.concatenate([dl0, -dl0], axis=0)

            @pl.when(i == 0)
            def _():
                dlg_ref[d] = part

            @pl.when(i > 0)
            def _():
                dlg_ref[d] += part

    blk = lambda c: pl.BlockSpec((HG_RB, 512), lambda i: (i, c))
    ob = pl.BlockSpec((HG_RB, 512), lambda i: (i, 0))
    lgs = pl.BlockSpec((2, 2, 512), lambda i: (0, 0, 0))
    return pl.pallas_call(
        body, name="hg_pre_bwd", grid=(HG_NB,),
        in_specs=[blk(3), blk(4), blk(5), lgs] + [ob] * 8,
        out_specs=(ob, ob, ob, ob, lgs),
        out_shape=(_sds((T, 512), BF16),) * 4 + (_sds((2, 2, 512), F32),),
        compiler_params=_cp(("arbitrary",)))(p_act, p_act, p_act, logits, dq_f, dq_b, dk_f, dk_b, db_f, db_b, dv_f, dv_b)


def _mix_fwd(o_na, o_f, o_b, gain, w_na, w_hg, p_act):
    def body(ona_ref, of_ref, ob_ref, g_ref, gain_ref, wna_ref, whg_ref, gna_ref, ghg_ref, o_ref, u_ref):
        u = _hg_post_rows(of_ref[...] + ob_ref[...], g_ref[...], gain_ref[...], _row_valid(pl.program_id(0), TM_B)).astype(BF16)
        u_ref[...] = u
        y_na = _dot(ona_ref[...], wna_ref[...])
        y_hg = _dot(u, whg_ref[...])
        o_ref[...] = (jax.nn.sigmoid(gna_ref[...]) * y_na + jax.nn.sigmoid(ghg_ref[...]) * y_hg).astype(BF16)

    act = pl.BlockSpec((TM_B, 512), lambda i: (i, 0))
    wsp = pl.BlockSpec((512, D), lambda i: (0, 0))
    return pl.pallas_call(
        body, name="mix_fwd", grid=(T // TM_B,),
        in_specs=[act, act, act, pl.BlockSpec((TM_B, 512), lambda i: (i, 7)), pl.BlockSpec((1, 512), lambda i: (0, 0)),
                  wsp, wsp, pl.BlockSpec((TM_B, D), lambda i: (i, 4)), pl.BlockSpec((TM_B, D), lambda i: (i, 5))],
        out_specs=(pl.BlockSpec((TM_B, D), lambda i: (i, 0)), act), out_shape=(_sds((T, D), BF16), _sds((T, 512), BF16)),
        compiler_params=_cp(("parallel",)))(o_na, o_f, o_b, p_act, gain, w_na, w_hg, p_act, p_act)


def _mix_bwd(o_na, u_hg, o_f, o_b, gain, w_na, w_hg, p_act, dmix):
    ni = T // TM_B

    def body(ona_ref, uhg_ref, of_ref, ob_ref, g_ref, gain_ref, wna_ref, whg_ref, gna_ref, ghg_ref, dmix_ref,
             dgna_ref, dghg_ref, dwna_ref, dwhg_ref, dona_ref, do_ref, dg_ref, dgain_ref, acc_na, acc_hg):
        i = pl.program_id(0)
        dm = dmix_ref[...].astype(F32)
        dxs = []
        for x_ref, w_ref, gt_ref, dgt_ref, dw_ref, acc in (
                (ona_ref, wna_ref, gna_ref, dgna_ref, dwna_ref, acc_na), (uhg_ref, whg_ref, ghg_ref, dghg_ref, dwhg_ref, acc_hg)):
            xv = x_ref[...]
            y = _dot(xv, w_ref[...])
            sg = jax.nn.sigmoid(gt_ref[...])
            dgt_ref[...] = (dm * y * sg * (1.0 - sg)).astype(BF16)
            dy = (dm * sg).astype(BF16)
            dxs.append(_dot(dy, w_ref[...], NT))
            part = _dot(xv, dy, TN)

            @pl.when(i == 0)
            def _():
                acc[...] = part

            @pl.when(i > 0)
            def _():
                acc[...] += part

            @pl.when(i == ni - 1)
            def _():
                dw_ref[...] = acc[...].astype(BF16)

        dona_ref[...] = dxs[0]
        do, dg, gpart = _hg_post_bwd_rows(dxs[1], of_ref[...] + ob_ref[...], g_ref[...], gain_ref[...], _row_valid(i, TM_B))
        do_ref[...] = do
        dg_ref[...] = dg.astype(BF16)

        @pl.when(i == 0)
        def _():
            dgain_ref[...] = gpart

        @pl.when(i > 0)
        def _():
            dgain_ref[...] += gpart

    act = pl.BlockSpec((TM_B, 512), lambda i: (i, 0))
    wsp = pl.BlockSpec((512, D), lambda i: (0, 0))
    rblk = pl.BlockSpec((TM_B, D), lambda i: (i, 0))
    vec = pl.BlockSpec((1, 512), lambda i: (0, 0))
    return pl.pallas_call(
        body, name="mix_bwd", grid=(ni,),
        in_specs=[act, act, act, act, pl.BlockSpec((TM_B, 512), lambda i: (i, 7)), vec, wsp, wsp,
                  pl.BlockSpec((TM_B, D), lambda i: (i, 4)), pl.BlockSpec((TM_B, D), lambda i: (i, 5)), rblk],
        out_specs=(rblk, rblk, wsp, wsp, act, act, act, vec),
        out_shape=(_sds((T, D), BF16), _sds((T, D), BF16), _sds((512, D), BF16), _sds((512, D), BF16),
                   _sds((T, 512), F32), _sds((T, 512), F32), _sds((T, 512), BF16), _sds((1, 512), F32)),
        scratch_shapes=[pltpu.VMEM((512, D), F32), pltpu.VMEM((512, D), F32)],
        compiler_params=_cp(("arbitrary",)))(o_na, u_hg, o_f, o_b, p_act, gain, w_na, w_hg, p_act, p_act, dmix)


def _wo_fwd(mix, w_o, h0, g_mlp):
    def body(mix_ref, w_ref, h0_ref, g_ref, h1_ref, m_ref):
        h1 = h0_ref[...] + _dot(mix_ref[...], w_ref[...])
        h1_ref[...] = h1
        r = lax.rsqrt(jnp.mean(h1 * h1, axis=-1, keepdims=True) + EPS)
        m_ref[...] = (h1 * r * g_ref[...]).astype(BF16)

    blk = pl.BlockSpec((TM_B, D), lambda i: (i, 0))
    return pl.pallas_call(
        body, name="wo_fwd", grid=(T // TM_B,),
        in_specs=[blk, pl.BlockSpec((D, D), lambda i: (0, 0)), blk, pl.BlockSpec((1, D), lambda i: (0, 0))],
        out_specs=(blk, blk), out_shape=(_sds((T, D), F32), _sds((T, D), BF16)),
        compiler_params=_cp(("parallel",)))(mix, w_o, h0, g_mlp)


def _wo_bwd(dh1_b, w_o, mix):
    ni = T // TM_B

    def body(dh_ref, w_ref, mix_ref, dmix_ref, dw_ref, acc):
        i = pl.program_id(0)
        dh = dh_ref[...]
        dmix_ref[...] = _dot(dh, w_ref[...], NT).astype(BF16)
        part = _dot(mix_ref[...], dh, TN)

        @pl.when(i == 0)
        def _():
            acc[...] = part

        @pl.when(i > 0)
        def _():
            acc[...] += part

        @pl.when(i == ni - 1)
        def _():
            dw_ref[...] = acc[...].astype(BF16)

    blk = pl.BlockSpec((TM_B, D), lambda i: (i, 0))
    wsp = pl.BlockSpec((D, D), lambda i: (0, 0))
    return pl.pallas_call(
        body, name="wo_bwd", grid=(ni,), in_specs=[blk, wsp, blk], out_specs=(blk, wsp),
        out_shape=(_sds((T, D), BF16), _sds((D, D), BF16)), scratch_shapes=[pltpu.VMEM((D, D), F32)],
        compiler_params=_cp(("arbitrary",)))(dh1_b, w_o, mix)


FF_B = D_FF // NDEV


def _loss_rows(xv, gv, tv, row0):
    r_io = lax.broadcasted_iota(jnp.int32, (xv.shape[0], 1), 0) + row0
    valid = (r_io >= NM) & (r_io < L)
    r = lax.rsqrt(jnp.mean(xv * xv, axis=-1, keepdims=True) + EPS)
    xh = xv * r
    err = jnp.where(valid, xh * gv - tv, 0.0)
    lpart = 0.5 * jnp.sum(jnp.sum(err * err, axis=-1, keepdims=True) * (1.0 / D), axis=0, keepdims=True)
    dy = err * (1.0 / D)
    dxh = dy * gv
    dh = r * (dxh - xh * jnp.mean(dxh * xh, axis=-1, keepdims=True))
    return lpart, dh, jnp.sum(dy * xh, axis=0, keepdims=True)


def _mlp_fwd_loss(m, wup_g, wdown_g, h1, g_final, tgt):
    nsub = TM_MM // TM_E

    def body(m_ref, wu_ref, wd_ref, h1_ref, g_ref, t_ref, loss_ref, dh_ref, dhb_ref, dg_ref, h2):
        i, j = pl.program_id(0), pl.program_id(1)
        up = jnp.maximum(_dot(m_ref[...], wu_ref[0]), 0.0)
        part = _dot((up * up).astype(BF16), wd_ref[0])

        @pl.when(j == 0)
        def _():
            h2[...] = h1_ref[...] + part

        @pl.when(j > 0)
        def _():
            h2[...] += part

        @pl.when(j == NDEV - 1)
        def _():
            lsum = jnp.zeros((1, 1), F32)
            gsum = jnp.zeros((1, D), F32)
            for s in range(nsub):
                rows = slice(s * TM_E, (s + 1) * TM_E)
                lpart, dh, gpart = _loss_rows(h2[rows, :], g_ref[...], t_ref[rows, :], i * TM_MM + s * TM_E)
                dh_ref[rows, :] = dh
                dhb_ref[rows, :] = dh.astype(BF16)
                lsum = lsum + lpart
                gsum = gsum + gpart
            lsum = jnp.broadcast_to(lsum, (1, 128))

            @pl.when(i == 0)
            def _():
                loss_ref[...] = lsum
                dg_ref[...] = gsum

            @pl.when(i > 0)
            def _():
                loss_ref[...] += lsum
                dg_ref[...] += gsum

    blk = pl.BlockSpec((TM_MM, D), lambda i, j: (i, 0))
    vec = pl.BlockSpec((1, D), lambda i, j: (0, 0))
    return pl.pallas_call(
        body, name="mlp_fwd_loss", grid=(T // TM_MM, NDEV),
        in_specs=[blk, pl.BlockSpec((1, D, FF_B), lambda i, j: (j, 0, 0)), pl.BlockSpec((1, FF_B, D), lambda i, j: (j, 0, 0)),
                  blk, vec, blk],
        out_specs=(pl.BlockSpec((1, 128), lambda i, j: (0, 0)), blk, blk, vec),
        out_shape=(_sds((1, 128), F32), _sds((T, D), F32), _sds((T, D), BF16), _sds((1, D), F32)),
        scratch_shapes=[pltpu.VMEM((TM_MM, D), F32)],
        compiler_params=_cp(("arbitrary", "arbitrary"), 56))(m, wup_g, wdown_g, h1, g_final, tgt)


def _mlp_bwd(m, dh2_b, wup_g, wdown_g, h1, g_mlp, dh2):
    ni = T // TM_B
    nsub = TM_B // TM_E

    def body(m_ref, dh_ref, wu_ref, wd_ref, h1_ref, g_ref, dres_ref, dwu_ref, dwd_ref, dh1_ref, dh1b_ref, dg_ref,
             dm_ref, acc_u, acc_d):
        j, i = pl.program_id(0), pl.program_id(1)
        rows = pl.ds(pl.multiple_of(i * TM_B, TM_B), TM_B)
        mv, dh = m_ref[...], dh_ref[...]
        r = jnp.maximum(_dot(mv, wu_ref[0]), 0.0)
        act = (r * r).astype(BF16)
        dact = _dot(dh, wd_ref[0], NT)
        dup = (dact * (2.0 * r)).astype(BF16)
        pd = _dot(act, dh, TN)
        pu = _dot(mv, dup, TN)
        dmv = _dot(dup, wu_ref[0], NT)

        @pl.when(i == 0)
        def _():
            acc_u[...] = pu
            acc_d[...] = pd

        @pl.when(i > 0)
        def _():
            acc_u[...] += pu
            acc_d[...] += pd

        @pl.when(i == ni - 1)
        def _():
            dwu_ref[0] = acc_u[...].astype(BF16)
            dwd_ref[0] = acc_d[...].astype(BF16)

        @pl.when(j == 0)
        def _():
            dm_ref[rows, :] = dmv

        @pl.when(j > 0)
        def _():
            dm_ref[rows, :] += dmv

        @pl.when(j == NDEV - 1)
        def _():
            gsum = jnp.zeros((1, D), F32)
            for s in range(nsub):
                sub = slice(s * TM_E, (s + 1) * TM_E)
                dm_rows = dm_ref[pl.ds(pl.multiple_of(i * TM_B + s * TM_E, TM_E), TM_E), :]
                dx, gpart = _norm_bwd_rows(h1_ref[sub, :], g_ref[...], dm_rows, dres_ref[sub, :])
                dh1_ref[sub, :] = dx
                dh1b_ref[sub, :] = dx.astype(BF16)
                gsum = gsum + gpart

            @pl.when(i == 0)
            def _():
                dg_ref[...] = gsum

            @pl.when(i > 0)
            def _():
                dg_ref[...] += gsum

    blk = pl.BlockSpec((TM_B, D), lambda j, i: (i, 0))
    late = pl.BlockSpec((TM_B, D), lambda j, i: (jnp.where(j == NDEV - 1, i, 0), 0))
    vec = pl.BlockSpec((1, D), lambda j, i: (0, 0))
    wus = pl.BlockSpec((1, D, FF_B), lambda j, i: (j, 0, 0))
    wds = pl.BlockSpec((1, FF_B, D), lambda j, i: (j, 0, 0))
    return pl.pallas_call(
        body, name="mlp_bwd", grid=(NDEV, ni), in_specs=[blk, blk, wus, wds, late, vec, late],
        out_specs=(wus, wds, late, late, vec),
        out_shape=(_sds((NDEV, D, FF_B), BF16), _sds((NDEV, FF_B, D), BF16), _sds((T, D), F32), _sds((T, D), BF16),
                   _sds((1, D), F32)),
        scratch_shapes=[pltpu.VMEM((T, D), F32), pltpu.VMEM((D, FF_B), F32), pltpu.VMEM((FF_B, D), F32)],
        compiler_params=_cp(("arbitrary", "arbitrary"), 56))(m, dh2_b, wup_g, wdown_g, h1, g_mlp, dh2)


def _adamw(parts, w, m, v, name):
    rr, cc = w.shape
    tr = rr
    for cand in (256, 128, 64):
        if rr % cand == 0 and rr > cand:
            tr = cand
            break
    c1 = 1.0 - ADAM_B1 ** ADAM_STEP
    c2 = 1.0 - ADAM_B2 ** ADAM_STEP

    def body(p_ref, w_ref, m_ref, v_ref, g_ref, d_ref, nm_ref, nv_ref):
        g = p_ref[0].astype(F32)
        for s in range(1, NDEV):
            g = g + p_ref[s].astype(F32)
        mn = ADAM_B1 * m_ref[...] + (1.0 - ADAM_B1) * g
        vn = ADAM_B2 * v_ref[...] + (1.0 - ADAM_B2) * (g * g)
        g_ref[...] = g
        nm_ref[...] = mn
        nv_ref[...] = vn
        d_ref[...] = -ADAM_LR * ((mn / c1) / (jnp.sqrt(vn / c2) + ADAM_EPS) + ADAM_WD * w_ref[...])

    blk = pl.BlockSpec((tr, cc), lambda i: (i, 0))
    return pl.pallas_call(
        body, name=name, grid=(rr // tr,),
        in_specs=[pl.BlockSpec((NDEV, tr, cc), lambda i: (0, i, 0)), blk, blk, blk],
        out_specs=(blk,) * 4, out_shape=(_sds((rr, cc), F32),) * 4,
        compiler_params=_cp(("parallel",)))(parts, w, m, v)


RPB_N = NA_HEADS * 15 * 31
RPB_PAD = 4096
OWN_ROWS = NM + 8


def _pad_rows(a, rows):
    return jnp.pad(a, ((0, rows - a.shape[0]),) + ((0, 0),) * (a.ndim - 1))


def _pack_owned(meta_blk, lb_blk):
    return jnp.concatenate([meta_blk, _pad_rows(lb_blk.reshape(2, 128), 8)], axis=0)


LOSS_ROW = 28


def _pack_replicated(n_mix, n_mlp, n_final, hg_gain, rpb, loss_row=None):
    flat = _pad_rows(rpb.reshape(RPB_N), RPB_PAD)
    gain8 = _pad_rows(hg_gain.reshape(4, 128), 8)
    if loss_row is not None:
        gain8 = gain8 + jnp.pad(loss_row, ((LOSS_ROW - 24, 31 - LOSS_ROW), (0, 0)))
    return jnp.concatenate([n_mix.reshape(8, 128), n_mlp.reshape(8, 128), n_final.reshape(8, 128), gain8,
                            flat.reshape(32, 128)], axis=0)


def _unpack_replicated(a):
    return (a[0:8].reshape(1, D), a[8:16].reshape(1, D), a[16:24].reshape(D), a[24:28].reshape(1, 512),
            a[32:64].reshape(RPB_PAD)[:RPB_N].reshape(1, NA_HEADS, 15, 31))


def kernel(x, meta_tokens, w_in, w_na_out, w_hg_out, w_o, w_up, w_down, norm_mix, norm_mlp, norm_final, hg_norm, na_rpb, hg_lb_logits, loss_target, m_meta_tokens, m_w_in, m_w_na_out, m_w_hg_out, m_w_o, m_w_up, m_w_down, m_norm_mix, m_norm_mlp, m_norm_final, m_hg_norm, m_na_rpb, m_hg_lb_logits, v_meta_tokens, v_w_in, v_w_na_out, v_w_hg_out, v_w_o, v_w_up, v_w_down, v_norm_mix, v_norm_mlp, v_norm_final, v_hg_norm, v_na_rpb, v_hg_lb_logits):
    owned = _pack_owned(meta_tokens, hg_lb_logits)
    first, tok = _exchange_start([w_in[0].astype(BF16), owned], [False] * 2, "gather_first_start", SAME_CORE_AND_SIBLING)
    bias_tab = _na_bias_table(_tie(jnp.pad(na_rpb[0], ((0, 0), (0, 0), (0, 128 - 31))), tok, "tie_bias_table"))
    later = [w[0].astype(BF16) for w in (w_na_out, w_hg_out, w_o, w_up, w_down)]
    lead = jnp.zeros((NM, D), F32) + tok[0, 0]
    h0_rows = jnp.concatenate([lead, x[0], jnp.zeros((T - L, D), F32)], axis=0)
    tgt = jnp.concatenate([lead, loss_target[0], jnp.zeros((T - L, D), F32)], axis=0)
    first = _exchange_wait(first, [False] * 2, [bias_tab, h0_rows, tgt] + later, "gather_first_wait", SAME_CORE_AND_SIBLING)
    win_g, owned_g = _forward_to_sibling(first, "gather_first_forward")
    later[0] = _tie(later[0], owned_g, "tie_gather_rest")
    gather_rest, tok = _exchange_start(later, [False] * 5, "gather_rest_start")
    win_g = _tie(win_g, tok, "tie_inproj")
    meta_full = jnp.transpose(owned_g[:, 0:NM, :], (1, 0, 2)).reshape(NM, D)
    logits = jnp.transpose(owned_g[:, NM:NM + 2, :].reshape(NDEV, 2, 2, 64), (1, 2, 0, 3)).reshape(2, 2, 512)

    h0 = lax.dynamic_update_slice(h0_rows, meta_full, (0, 0))

    a, a_t = _norm_fwd_t(h0, norm_mix, "norm_mix_fwd")
    p_act = _inproj_fwd(a, win_g)
    o_na, lse = _na_fwd(p_act, bias_tab)
    qh, k_f, b_f, k_b, b_b = _hg_pre(p_act, logits)
    o_f, st_f = _hg_scan_fwd(qh, k_f, b_f, p_act, False)
    o_b, st_b = _hg_scan_fwd(qh, k_b, b_b, p_act, True)
    (wna_g, whg_g, wo_g, _, _), gather_rest = _exchange_wait(
        gather_rest, [False] * 5, [o_f, o_b, o_na], "gather_rest_wait_a", which=(0, 1, 2))
    w_na_full = jnp.transpose(wna_g, (1, 0, 2)).reshape(512, D)
    w_hg_full = jnp.transpose(whg_g, (1, 0, 2)).reshape(512, D)
    mix, u_hg = _mix_fwd(o_na, o_f, o_b, hg_norm, w_na_full, w_hg_full, p_act)
    h1, m_act = _wo_fwd(mix, wo_g.reshape(D, D), h0, norm_mlp)
    (_, _, wo_g, wup_g, wdown_g), _ = _exchange_wait(gather_rest, [False] * 5, [m_act], "gather_rest_wait_b", which=(3, 4))
    w_o_full = wo_g.reshape(D, D)
    loss_part, dh2, dh2_b, d_nfinal = _mlp_fwd_loss(m_act, wup_g, wdown_g, h1, norm_final.reshape(1, D), tgt)

    dwup_p, dwdown_p, dh1, dh1_b, d_nmlp = _mlp_bwd(m_act, dh2_b, wup_g, wdown_g, h1, norm_mlp, dh2)
    sc_mlp, tok = _exchange_start([dwup_p, dwdown_p], [True] * 2, "scatter_mlp_start")
    dmix, dwo = _wo_bwd(_tie(dh1_b, tok, "tie_wo_bwd"), w_o_full, mix)
    sc_wo, tok = _exchange_start([dwo.reshape(NDEV, D // NDEV, D)], [True], "scatter_wo_start")
    dgna, dghg, dwna, dwhg, do_na, do_hg, dg_hg, d_gain = _mix_bwd(
        o_na, u_hg, o_f, o_b, hg_norm, w_na_full, w_hg_full, p_act, _tie(dmix, tok, "tie_mix_bwd"))
    owner_cols = lambda w: jnp.transpose(w.reshape(512, NDEV, D // NDEV), (1, 0, 2))
    sc_br, tok = _exchange_start([owner_cols(dwna), owner_cols(dwhg)], [True] * 2, "scatter_branch_start")
    do_hg = _tie(do_hg, tok, "tie_hg_scan_bwd")
    dq_f, dk_f, db_f, dv_f = _hg_scan_bwd(qh, k_f, b_f, p_act, st_f, do_hg, False)
    dq_b, dk_b, db_b, dv_b = _hg_scan_bwd(qh, k_b, b_b, p_act, st_b, do_hg, True)
    dq_hg, dz_f, dz_b, di_hg, d_logits = _hg_pre_bwd(p_act, logits, dq_f, dq_b, dk_f, dk_b, db_f, db_b, dv_f, dv_b)
    dq_na, dk_na, dv_na, dbias = _na_bwd(p_act, do_na, lse, bias_tab)
    dp = jnp.concatenate([dq_na.astype(BF16), dk_na.astype(BF16), dv_na.astype(BF16), dq_hg, dz_f, dz_b, di_hg, dg_hg,
                          dgna, dghg], axis=1)
    dwin_p = _inproj_bwd_dw(a_t, dp)
    sc_in, tok = _exchange_start([dwin_p], [True], "scatter_in_start")
    dh0, d_nmix = _inproj_bwd_da(_tie(dp, tok, "tie_inproj_bwd_da"), win_g, h0, norm_mix, dh1)
    d_rpb = _na_rpb_reduce(_tie(dbias, tok, "tie_rpb_reduce"))[:, :, :31]

    res = {}

    def update(nm, parts, w, mm, vv):
        res[nm] = [r[None] for r in _adamw(parts, w[0], mm[0], vv[0], "adamw_" + nm)]
        return res[nm][1]

    wup_r, wdown_r = _exchange_wait(sc_mlp, [True] * 2, [dh0, d_rpb], "scatter_mlp_wait")
    update("w_up", wup_r, w_up, m_w_up, v_w_up)
    last = update("w_down", wdown_r, w_down, m_w_down, v_w_down)
    (wo_r,) = _exchange_wait(sc_wo, [True], [last], "scatter_wo_wait")
    last = update("w_o", wo_r, w_o, m_w_o, v_w_o)
    wna_r, whg_r = _exchange_wait(sc_br, [True] * 2, [last], "scatter_branch_wait")
    update("w_na_out", wna_r, w_na_out, m_w_na_out, v_w_na_out)
    last = update("w_hg_out", whg_r, w_hg_out, m_w_hg_out, v_w_hg_out)

    d_meta = jnp.transpose(dh0[0:NM].reshape(NM, NDEV, 128), (1, 0, 2))
    d_lg = jnp.transpose(d_logits.reshape(2, 2, NDEV, 64), (2, 0, 1, 3)).reshape(NDEV, 2, 128)
    owned_p = jnp.concatenate([d_meta, jnp.pad(d_lg, ((0, 0), (0, OWN_ROWS - NM - 2), (0, 0)))], axis=1)
    repl_p = _pack_replicated(d_nmix, d_nmlp, d_nfinal, d_gain, d_rpb, loss_part)
    grad_x = dh0[NM:L][None]
    done_first = [grad_x] + [res[nm][0] for nm in ("w_up", "w_down", "w_o", "w_na_out", "w_hg_out")]
    owned_r, repl_r = _exchange([owned_p, repl_p], [True, False], "scatter_small", done_first)
    own = _adamw(owned_r, owned, _pack_owned(m_meta_tokens, m_hg_lb_logits), _pack_owned(v_meta_tokens, v_hg_lb_logits),
                 "adamw_owned_small")
    res["meta_tokens"] = [r[0:NM] for r in own]
    res["hg_lb_logits"] = [r[NM:NM + 2].reshape(2, 2, 64) for r in own]
    rep = _adamw(repl_r, _pack_replicated(norm_mix, norm_mlp, norm_final, hg_norm, na_rpb),
                 _pack_replicated(m_norm_mix, m_norm_mlp, m_norm_final, m_hg_norm, m_na_rpb),
                 _pack_replicated(v_norm_mix, v_norm_mlp, v_norm_final, v_hg_norm, v_na_rpb), "adamw_replicated")
    for q in range(4):
        um = _unpack_replicated(rep[q])
        for nm, val in zip(("norm_mix", "norm_mlp", "norm_final", "hg_norm", "na_rpb"), um):
            res.setdefault(nm, [None] * 4)[q] = val
    (win_r,) = _exchange_wait(sc_in, [True], [rep[1], own[1]], "scatter_in_wait")
    update("w_in", win_r, w_in, m_w_in, v_w_in)

    loss = jnp.sum(repl_r[:, LOSS_ROW, 0])
    order = ("meta_tokens", "w_in", "w_na_out", "w_hg_out", "w_o", "w_up", "w_down", "norm_mix", "norm_mlp", "norm_final",
             "hg_norm", "na_rpb", "hg_lb_logits")
    outs = [loss, grad_x]
    for q in range(4):
        outs += [res[nm][q] for nm in order]
    return tuple(outs)
```

```python
import functools

import numpy as np
import jax
import jax.numpy as jnp
from jax import lax
from jax.experimental import pallas as pl
from jax.experimental.pallas import tpu as pltpu

F32 = jnp.float32
BF16 = jnp.bfloat16

D = 1024
SEQ = 2048
NM = 16
L = SEQ + NM
T = 2176
NDEV = 8
EPS = 1e-6
GRID_W = 64
ROWS = SEQ // GRID_W
NA_HEADS = 8
NA_DH = 64
NA_SCALE = NA_DH ** -0.5
HG_HEADS = 4
HG_C = 16
NCHUNK = L // HG_C
D_FF = 4096
IN_COLS = 6144
NEG = -1e30

ADAM_LR = 0.001
ADAM_B1 = 0.9
ADAM_B2 = 0.999
ADAM_EPS = 1e-08
ADAM_WD = 0.01
ADAM_STEP = 10

MESH_ID = pl.DeviceIdType.MESH
ANY = pl.BlockSpec(memory_space=pl.ANY)

NN = (((1,), (0,)), ((), ()))
NT = (((1,), (1,)), ((), ()))
TN = (((0,), (0,)), ((), ()))


def _cp(sem=None, vmem_mb=48):
    return pltpu.CompilerParams(dimension_semantics=sem, vmem_limit_bytes=vmem_mb * 1024 * 1024)


def _dot(a, b, dims=NN):
    return lax.dot_general(a, b, dims, preferred_element_type=F32)


def _sds(shape, dtype):
    return jax.ShapeDtypeStruct(shape, dtype)


HBM = pl.BlockSpec(memory_space=pltpu.HBM)
SEM = pl.BlockSpec(memory_space=pltpu.SEMAPHORE)
EFFECT = pltpu.SideEffectType.DATAFLOW_SIDE_EFFECTING


def _exchange(arrs, scatter, name, after=()):
    n = len(arrs)
    after = list(after)
    out_shapes = []
    for a, sc in zip(arrs, scatter):
        out_shapes.append(_sds(a.shape if sc else (NDEV,) + a.shape, a.dtype))

    def body(*refs):
        ins, outs = refs[:n], refs[n + len(after):2 * n + len(after)]
        send_sems, recv_sems, loc_sems = refs[2 * n + len(after):]
        me = 4 * lax.axis_index("x") + 2 * lax.axis_index("y") + lax.axis_index("c")
        copies = []
        for k in range(n):
            src_me = ins[k].at[me] if scatter[k] else ins[k]
            loc = pltpu.make_async_copy(src_me, outs[k].at[me], loc_sems.at[k])
            loc.start()
            copies.append(loc)
        remote = _peer_copies(ins, outs, scatter, send_sems, recv_sems)
        for cp in remote:
            cp.start()
        for cp in remote:
            cp.wait_recv()
        for cp in remote:
            cp.wait_send()
        for cp in copies:
            cp.wait()

    return pl.pallas_call(
        body, name=name, out_shape=tuple(out_shapes), in_specs=[ANY] * (n + len(after)), out_specs=tuple([ANY] * n),
        scratch_shapes=[pltpu.SemaphoreType.DMA((n * (NDEV - 1),)), pltpu.SemaphoreType.DMA((n * (NDEV - 1),)),
                        pltpu.SemaphoreType.DMA((n,))],
    )(*arrs, *after)


def _forward_to_sibling(bufs, name):
    n = len(bufs)

    def body(*refs):
        ins, outs = refs[:n], refs[n:2 * n]
        send_sems, recv_sems = refs[2 * n:]
        x, y, c = lax.axis_index("x"), lax.axis_index("y"), lax.axis_index("c")
        copies = []
        for k in range(n):
            for j, (cx, cy) in enumerate(((1 - x, y), (x, 1 - y), (1 - x, 1 - y))):
                slot = 4 * cx + 2 * cy + c
                copies.append(pltpu.make_async_remote_copy(
                    src_ref=ins[k].at[slot], dst_ref=outs[k].at[slot], send_sem=send_sems.at[3 * k + j],
                    recv_sem=recv_sems.at[3 * k + j], device_id=(x, y, 1 - c), device_id_type=MESH_ID))
        for cp in copies:
            cp.start()
        for cp in copies:
            cp.wait_recv()
        for cp in copies:
            cp.wait_send()

    return pl.pallas_call(
        body, name=name, out_shape=tuple(_sds(b.shape, b.dtype) for b in bufs), in_specs=[ANY] * n,
        out_specs=tuple([ANY] * n), input_output_aliases={k: k for k in range(n)},
        scratch_shapes=[pltpu.SemaphoreType.DMA((3 * n,)), pltpu.SemaphoreType.DMA((3 * n,))],
    )(*bufs)


ALL_PEERS = tuple(range(1, NDEV))
SAME_CORE_AND_SIBLING = (1, 2, 4, 6)


def _peer_copies(srcs, lands, scatter, send_sems, recv_sems, masks=ALL_PEERS):
    x, y, c = lax.axis_index("x"), lax.axis_index("y"), lax.axis_index("c")
    me = 4 * x + 2 * y + c
    out = []
    for k in range(len(srcs)):
        for m in masks:
            px, py, pc = x ^ (m >> 2), y ^ ((m >> 1) & 1), c ^ (m & 1)
            src = srcs[k].at[4 * px + 2 * py + pc] if scatter[k] else srcs[k]
            out.append(pltpu.make_async_remote_copy(
                src_ref=src, dst_ref=lands[k].at[me], send_sem=send_sems.at[k * (NDEV - 1) + m - 1],
                recv_sem=recv_sems.at[k * (NDEV - 1) + m - 1],
                device_id=(px, py, pc), device_id_type=MESH_ID))
    return out


def _exchange_start(arrs, scatter, name, masks=ALL_PEERS):
    n = len(arrs)
    me = 4 * lax.axis_index("x") + 2 * lax.axis_index("y") + lax.axis_index("c")
    lands = []
    for a, sc in zip(arrs, scatter):
        own = lax.dynamic_index_in_dim(a, me, 0, keepdims=True) if sc else a[None]
        shape = a.shape if sc else (NDEV,) + a.shape
        lands.append(lax.dynamic_update_index_in_dim(lax.empty(shape, a.dtype), own, me, 0))

    def body(*refs):
        srcs, lnds = refs[:n], refs[n:2 * n]
        send_sems, recv_sems = refs[2 * n], refs[2 * n + 1]
        token = refs[-1]
        for cp in _peer_copies(srcs, lnds, scatter, send_sems, recv_sems, masks):
            cp.start()
        token[...] = jnp.zeros_like(token)

    ops = [pltpu.with_memory_space_constraint(a, pltpu.HBM) for a in list(arrs) + lands]
    res = pl.pallas_call(
        body, name=name,
        out_shape=(pltpu.SemaphoreType.DMA((n * (NDEV - 1),)), pltpu.SemaphoreType.DMA((n * (NDEV - 1),)))
        + tuple(pltpu.HBM(o.shape, o.dtype) for o in ops) + (_sds((8, 128), F32),),
        in_specs=[HBM] * (2 * n), out_specs=(SEM, SEM) + (HBM,) * (2 * n) + (pl.BlockSpec(memory_space=pltpu.VMEM),),
        input_output_aliases={k: 2 + k for k in range(2 * n)},
        compiler_params=pltpu.CompilerParams(has_side_effects=EFFECT),
    )(*ops)
    return res[:-1], res[-1]


def _exchange_wait(handle, scatter, after, name, masks=ALL_PEERS, which=None):
    send_sems, recv_sems = handle[0], handle[1]
    bufs = handle[2:]
    n = len(bufs) // 2
    after = list(after)
    per = len(masks)

    def body(*refs):
        srcs, lnds = refs[:n], refs[n:2 * n]
        copies = _peer_copies(srcs, lnds, scatter, refs[2 * n], refs[2 * n + 1], masks)
        for k in (range(n) if which is None else which):
            for cp in copies[k * per:(k + 1) * per]:
                cp.wait_send()
                cp.wait_recv()

    res = pl.pallas_call(
        body, name=name, out_shape=tuple(pltpu.HBM(b.shape, b.dtype) for b in bufs),
        in_specs=[HBM] * (2 * n) + [SEM, SEM] + [ANY] * len(after), out_specs=(HBM,) * (2 * n),
        input_output_aliases={k: k for k in range(2 * n)},
        compiler_params=pltpu.CompilerParams(has_side_effects=EFFECT),
    )(*bufs, send_sems, recv_sems, *after)
    return res[n:] if which is None else (res[n:], (send_sems, recv_sems) + tuple(res))


def _tie(x, token, name):
    def body(x_ref, t_ref, o_ref):
        del x_ref, t_ref, o_ref

    return pl.pallas_call(body, name=name, out_shape=_sds(x.shape, x.dtype), in_specs=[ANY, ANY], out_specs=ANY,
                          input_output_aliases={0: 0})(x, token)


TM_E = 272


def _norm_fwd_t(h, g, name):
    def body(h_ref, g_ref, o_ref, ot_ref):
        xv = h_ref[...]
        r = lax.rsqrt(jnp.mean(xv * xv, axis=-1, keepdims=True) + EPS)
        y = xv * r * g_ref[...]
        o_ref[...] = y.astype(BF16)
        ot_ref[...] = y.T.astype(BF16)

    return pl.pallas_call(
        body, name=name, grid=(T // 128,),
        in_specs=[pl.BlockSpec((128, D), lambda i: (i, 0)), pl.BlockSpec((1, D), lambda i: (0, 0))],
        out_specs=(pl.BlockSpec((128, D), lambda i: (i, 0)), pl.BlockSpec((D, 128), lambda i: (0, i))),
        out_shape=(_sds((T, D), BF16), _sds((D, T), BF16)), compiler_params=_cp(("parallel",)))(h, g)


def _norm_bwd_rows(xv, gv, dnv, dres):
    r = lax.rsqrt(jnp.mean(xv * xv, axis=-1, keepdims=True) + EPS)
    xh = xv * r
    dxh = dnv * gv
    dx = dres + r * (dxh - xh * jnp.mean(dxh * xh, axis=-1, keepdims=True))
    return dx, jnp.sum(dnv * xh, axis=0, keepdims=True)


TM_MM = 1088


def _inproj_fwd(a, w_g):
    nb = w_g.shape[2]

    def body(a_ref, w_ref, o_ref):
        o_ref[...] = _dot(a_ref[...], w_ref[0])

    return pl.pallas_call(
        body, name="inproj_fwd", grid=(T // TM_MM, NDEV),
        in_specs=[pl.BlockSpec((TM_MM, D), lambda i, j: (i, 0)), pl.BlockSpec((1, D, nb), lambda i, j: (j, 0, 0))],
        out_specs=pl.BlockSpec((TM_MM, nb), lambda i, j: (i, j)), out_shape=_sds((T, NDEV * nb), F32),
        compiler_params=_cp(("parallel", "parallel")))(a, w_g)


TM_B = 544


W_IN_B = IN_COLS // NDEV


NA_BLKS = 1536 // W_IN_B


def _dp_specs(rows, row_index):
    return [pl.BlockSpec((rows, W_IN_B), lambda *g: (row_index(*g), jnp.minimum(g[-1], NA_BLKS - 1))),
            pl.BlockSpec((rows, W_IN_B), lambda *g: (row_index(*g), jnp.maximum(g[-1] - NA_BLKS, 0)))]


def _inproj_bwd_dw(a_t, dp_na, dp_rest):
    def body(at_ref, na_ref, rest_ref, dw_ref):
        j = pl.program_id(0)

        @pl.when(j < NA_BLKS)
        def _():
            dw_ref[0] = _dot(at_ref[...], na_ref[...]).astype(BF16)

        @pl.when(j >= NA_BLKS)
        def _():
            dw_ref[0] = _dot(at_ref[...], rest_ref[...]).astype(BF16)

    return pl.pallas_call(
        body, name="inproj_bwd_dw", grid=(NDEV,),
        in_specs=[pl.BlockSpec((D, T), lambda j: (0, 0))] + _dp_specs(T, lambda j: 0),
        out_specs=pl.BlockSpec((1, D, W_IN_B), lambda j: (j, 0, 0)), out_shape=_sds((NDEV, D, W_IN_B), BF16),
        compiler_params=_cp(("parallel",)))(a_t, dp_na, dp_rest)


def _inproj_bwd_da(dp_na, dp_rest, w_g, h0, g_mix, dh1):
    nsub = TM_MM // TM_E

    def body(na_ref, rest_ref, w_ref, h0_ref, g_ref, dres_ref, dh0_ref, dg_ref, da):
        i, j = pl.program_id(0), pl.program_id(1)
        dpv = jnp.where(j < NA_BLKS, na_ref[...], rest_ref[...])
        dav = _dot(dpv, w_ref[0], NT)

        @pl.when(j == 0)
        def _():
            da[...] = dav

        @pl.when(j > 0)
        def _():
            da[...] += dav

        @pl.when(j == NDEV - 1)
        def _():
            gsum = jnp.zeros((1, D), F32)
            for s in range(nsub):
                sub = slice(s * TM_E, (s + 1) * TM_E)
                dx, gpart = _norm_bwd_rows(h0_ref[sub, :], g_ref[...], da[sub, :], dres_ref[sub, :])
                dh0_ref[sub, :] = dx
                gsum = gsum + gpart

            @pl.when(i == 0)
            def _():
                dg_ref[...] = gsum

            @pl.when(i > 0)
            def _():
                dg_ref[...] += gsum

    rblk = pl.BlockSpec((TM_MM, D), lambda i, j: (i, 0))
    vec = pl.BlockSpec((1, D), lambda i, j: (0, 0))
    return pl.pallas_call(
        body, name="inproj_bwd_da", grid=(T // TM_MM, NDEV),
        in_specs=_dp_specs(TM_MM, lambda i, j: i) + [pl.BlockSpec((1, D, W_IN_B), lambda i, j: (j, 0, 0)), rblk, vec, rblk],
        out_specs=(rblk, vec), out_shape=(_sds((T, D), F32), _sds((1, D), F32)),
        scratch_shapes=[pltpu.VMEM((TM_MM, D), F32)],
        compiler_params=_cp(("arbitrary", "arbitrary"), 56))(dp_na, dp_rest, w_g, h0, g_mix, dh1)


NA_QB = 256
NA_GROUPS = ROWS // 4
NA_UROWS = 11
NA_KW = NA_UROWS * GRID_W
NA_KU = 768


def _na_row_offset(var, i, j):
    valid = (j < 8, i <= j < i + 8, 3 <= j < NA_UROWS)[var]
    return (j - i + (7, 3, 0)[var]) if valid else None


def _na_bias_table(rp):
    def body(r_ref, o_ref):
        row3 = lax.broadcasted_iota(jnp.int32, (15, GRID_W, 128), 1)
        lane3 = lax.broadcasted_iota(jnp.int32, (15, GRID_W, 128), 2)
        w3 = lane3 & (GRID_W - 1)
        cs3 = jnp.clip(row3 - 8, 0, GRID_W - 16)
        lane = lax.broadcasted_iota(jnp.int32, (GRID_W, 128), 1)
        neg = jnp.full((GRID_W, 128), NEG, F32)
        z = jnp.stack([jnp.broadcast_to(r_ref[0, a:a + 1, :], (GRID_W, 128)) for a in range(15)])
        for bit in range(6):
            sh = 1 << bit
            z = jnp.where((row3 & sh) != 0, jnp.roll(z, sh, axis=2), z)
        z = jnp.roll(z, 128 - 15, axis=2)
        z = jnp.where(lane3 < GRID_W, z, 0.0)
        z = z + jnp.roll(z, GRID_W, axis=2)
        tabs = jnp.where((w3 >= cs3) & (w3 < cs3 + 16), z, NEG)
        tail = jnp.where(lane < GRID_W + NM, 0.0, NEG)
        for var in range(3):
            for i in range(4):
                for jp in range(NA_KU // 128):
                    halves = []
                    for j in (2 * jp, 2 * jp + 1):
                        a = _na_row_offset(var, i, j) if j < NA_UROWS else None
                        halves.append(tail if j >= NA_UROWS else (neg if a is None else tabs[a]))
                    o_ref[var, 0, i * 64:(i + 1) * 64, jp * 128:(jp + 1) * 128] = jnp.where(lane < GRID_W, halves[0], halves[1])

    return pl.pallas_call(
        body, name="na_bias_table", grid=(NA_HEADS,),
        in_specs=[pl.BlockSpec((1, 15, 128), lambda h: (h, 0, 0))],
        out_specs=pl.BlockSpec((3, 1, NA_QB, NA_KU), lambda h: (0, h, 0, 0)),
        out_shape=_sds((3, NA_HEADS, NA_QB, NA_KU), F32), compiler_params=_cp(("parallel",)))(rp)


def _na_var(g):
    return jnp.where(g == 0, 0, jnp.where(g == NA_GROUPS - 1, 2, 1))


def _na_load_window(src_ref, dst, g):
    us = jnp.clip(4 * g - 4, 0, ROWS - NA_UROWS)
    kstart = pl.multiple_of(NM + GRID_W * us, 16)
    dst[0:NA_KW, :] = src_ref[pl.ds(kstart, NA_KW), :].astype(BF16)
    dst[NA_KW:NA_KW + NM, :] = src_ref[0:NM, :].astype(BF16)
    dst[NA_KW + NM:, :] = jnp.zeros((NA_KU - NA_KW - NM, 128), BF16)
    return kstart


def _na_fwd(p_act, bias_tab):
    def body(q_ref, k_ref, v_ref, b_ref, o_ref, lse_ref, ku, vu):
        g = pl.program_id(1)
        _na_load_window(k_ref, ku, g)
        _na_load_window(v_ref, vu, g)
        qstart = pl.multiple_of(NM + NA_QB * g, 16)
        q = q_ref[pl.ds(qstart, NA_QB), :]
        lane = lax.broadcasted_iota(jnp.int32, (NA_QB, 128), 1)
        o_h, lse_h = [], []
        for h in range(2):
            hm = (lane < 64) if h == 0 else (lane >= 64)
            qm = (jnp.where(hm, q, 0.0) * NA_SCALE).astype(BF16)
            s = _dot(qm, ku[...], NT) + b_ref[0, h]
            m = jnp.max(s, axis=-1, keepdims=True)
            p = jnp.exp(s - m)
            l = jnp.sum(p, axis=-1, keepdims=True)
            o_h.append(_dot(p.astype(BF16), vu[...]) / l)
            lse_h.append(jnp.broadcast_to(m + jnp.log(l), (NA_QB, 128)))
        o_ref[pl.ds(qstart, NA_QB), :] = jnp.where(lane < 64, o_h[0], o_h[1]).astype(BF16)
        lse_ref[0, pl.ds(qstart, NA_QB), :] = jnp.where(lane < 64, lse_h[0], lse_h[1])

        @pl.when(g == 0)
        def _():
            qm_ = q_ref[0:NM, :]
            lane_m = lax.broadcasted_iota(jnp.int32, (NM, 128), 1)
            km, vm = ku[NA_KW:NA_KW + NM, :], vu[NA_KW:NA_KW + NM, :]
            om = []
            for h in range(2):
                hm = (lane_m < 64) if h == 0 else (lane_m >= 64)
                s = _dot(jnp.where(hm, qm_, 0.0).astype(BF16), km, NT) * NA_SCALE
                p = jnp.exp(s - jnp.max(s, axis=-1, keepdims=True))
                l = jnp.sum(p, axis=-1, keepdims=True)
                om.append(_dot(p.astype(BF16), vm) / l)
            o_ref[0:NM, :] = jnp.where(lane_m < 64, om[0], om[1]).astype(BF16)
            o_ref[L:T, :] = jnp.zeros((T - L, 128), BF16)
            lse_ref[0, 0:NM, :] = jnp.zeros((NM, 128), F32)
            lse_ref[0, L:T, :] = jnp.zeros((T - L, 128), F32)

    col = lambda off: pl.BlockSpec((T, 128), lambda hp, g: (0, off + hp))
    return pl.pallas_call(
        body, name="na_fwd", grid=(4, NA_GROUPS),
        in_specs=[col(0), col(4), col(8),
                  pl.BlockSpec((1, 2, NA_QB, NA_KU), lambda hp, g: (_na_var(g), hp, 0, 0))],
        out_specs=(pl.BlockSpec((T, 128), lambda hp, g: (0, hp)), pl.BlockSpec((1, T, 128), lambda hp, g: (hp, 0, 0))),
        out_shape=(_sds((T, 512), BF16), _sds((4, T, 128), F32)),
        scratch_shapes=[pltpu.VMEM((NA_KU, 128), BF16), pltpu.VMEM((NA_KU, 128), BF16)],
        compiler_params=_cp(("parallel", "arbitrary")))(p_act, p_act, p_act, bias_tab)


def _na_bwd(p_act, do, lse, bias_tab):
    def body(q_ref, k_ref, v_ref, do_ref, lse_ref, b_ref, dq_ref, dk_ref, dv_ref, db_ref, ku, vu):
        g = pl.program_id(1)

        @pl.when(g == 0)
        def _():
            dq_ref[...] = jnp.zeros((T, 128), F32)
            dk_ref[...] = jnp.zeros((T, 128), F32)
            dv_ref[...] = jnp.zeros((T, 128), F32)

        kstart = _na_load_window(k_ref, ku, g)
        _na_load_window(v_ref, vu, g)
        qstart = pl.multiple_of(NM + NA_QB * g, 16)
        q = q_ref[pl.ds(qstart, NA_QB), :]
        dov = do_ref[pl.ds(qstart, NA_QB), :]
        lsev = lse_ref[0, pl.ds(qstart, NA_QB), :]
        lane = lax.broadcasted_iota(jnp.int32, (NA_QB, 128), 1)
        first = (g == 0) | (g == 1) | (g == NA_GROUPS - 1)
        dq_h = []
        dku = jnp.zeros((NA_KU, 128), F32)
        dvu = jnp.zeros((NA_KU, 128), F32)
        for h in range(2):
            hm = (lane < 64) if h == 0 else (lane >= 64)
            qm = (jnp.where(hm, q, 0.0) * NA_SCALE).astype(BF16)
            dom = jnp.where(hm, dov, 0.0).astype(BF16)
            s = _dot(qm, ku[...], NT) + b_ref[0, h]
            p = jnp.exp(s - lsev[:, 64 * h:64 * h + 1])
            dp = _dot(dom, vu[...], NT)
            delta = jnp.sum(p * dp, axis=-1, keepdims=True)
            ds = p * (dp - delta)

            @pl.when(first)
            def _():
                db_ref[0, h] = ds

            @pl.when(jnp.logical_not(first))
            def _():
                db_ref[0, h] += ds

            dsb = ds.astype(BF16)
            dq_h.append(_dot(dsb, ku[...]) * NA_SCALE)
            dku = dku + _dot(dsb, qm, TN)
            dvu = dvu + _dot(p.astype(BF16), dom, TN)
        dq_ref[pl.ds(qstart, NA_QB), :] = jnp.where(lane < 64, dq_h[0], dq_h[1])
        dk_ref[pl.ds(kstart, NA_KW), :] += dku[0:NA_KW]
        dv_ref[pl.ds(kstart, NA_KW), :] += dvu[0:NA_KW]
        dk_ref[0:NM, :] += dku[NA_KW:NA_KW + NM]
        dv_ref[0:NM, :] += dvu[NA_KW:NA_KW + NM]

        @pl.when(g == 0)
        def _():
            qm_ = q_ref[0:NM, :]
            dom_ = do_ref[0:NM, :]
            lane_m = lax.broadcasted_iota(jnp.int32, (NM, 128), 1)
            km, vm = ku[NA_KW:NA_KW + NM, :], vu[NA_KW:NA_KW + NM, :]
            dqs = []
            dkm = jnp.zeros((NM, 128), F32)
            dvm = jnp.zeros((NM, 128), F32)
            for h in range(2):
                hm = (lane_m < 64) if h == 0 else (lane_m >= 64)
                qh = jnp.where(hm, qm_, 0.0).astype(BF16)
                doh = jnp.where(hm, dom_, 0.0).astype(BF16)
                s = _dot(qh, km, NT) * NA_SCALE
                e = jnp.exp(s - jnp.max(s, axis=-1, keepdims=True))
                p = e / jnp.sum(e, axis=-1, keepdims=True)
                dp = _dot(doh, vm, NT)
                ds = p * (dp - jnp.sum(p * dp, axis=-1, keepdims=True))
                dsb = (ds * NA_SCALE).astype(BF16)
                dqs.append(_dot(dsb, km))
                dkm = dkm + _dot(dsb, qh, TN)
                dvm = dvm + _dot(p.astype(BF16), doh, TN)
            dq_ref[0:NM, :] = jnp.where(lane_m < 64, dqs[0], dqs[1])
            dk_ref[0:NM, :] += dkm
            dv_ref[0:NM, :] += dvm

    col = lambda off: pl.BlockSpec((T, 128), lambda hp, g: (0, off + hp))
    ocol = pl.BlockSpec((T, 128), lambda hp, g: (0, hp))
    bspec = pl.BlockSpec((1, 2, NA_QB, NA_KU), lambda hp, g: (_na_var(g), hp, 0, 0))
    return pl.pallas_call(
        body, name="na_bwd", grid=(4, NA_GROUPS),
        in_specs=[col(0), col(4), col(8), ocol, pl.BlockSpec((1, T, 128), lambda hp, g: (hp, 0, 0)), bspec],
        out_specs=(ocol, ocol, ocol, bspec),
        out_shape=(_sds((T, 512), F32), _sds((T, 512), F32), _sds((T, 512), F32), _sds((3, NA_HEADS, NA_QB, NA_KU), F32)),
        scratch_shapes=[pltpu.VMEM((NA_KU, 128), BF16), pltpu.VMEM((NA_KU, 128), BF16)],
        compiler_params=_cp(("parallel", "arbitrary")))(p_act, p_act, p_act, do, lse, bias_tab)


def _na_rpb_reduce(dbias):
    def body(db_ref, o_ref):
        lane = lax.broadcasted_iota(jnp.int32, (GRID_W, 128), 1)
        row3 = lax.broadcasted_iota(jnp.int32, (15, GRID_W, 128), 1)
        lane3 = lax.broadcasted_iota(jnp.int32, (15, GRID_W, 128), 2)
        accs = []
        for a in range(15):
            acc = jnp.zeros((GRID_W, 128), F32)
            for var in range(3):
                for i in range(4):
                    for j in range(NA_UROWS):
                        if _na_row_offset(var, i, j) == a:
                            pair = db_ref[var, 0, i * 64:(i + 1) * 64, (j // 2) * 128:(j // 2 + 1) * 128]
                            acc = acc + jnp.where((lane < GRID_W) if j % 2 == 0 else (lane >= GRID_W), pair, 0.0)
            accs.append(acc)
        z = jnp.stack(accs)
        z = jnp.where(lane3 < GRID_W, z + jnp.roll(z, GRID_W, axis=2), 0.0)
        for bit in range(6):
            sh = 1 << bit
            z = jnp.where((row3 & sh) != 0, jnp.roll(z, 128 - sh, axis=2), z)
        z = jnp.roll(z, 15, axis=2)
        o_ref[0] = jnp.sum(z, axis=1)

    return pl.pallas_call(
        body, name="na_rpb_reduce", grid=(NA_HEADS,),
        in_specs=[pl.BlockSpec((3, 1, NA_QB, NA_KU), lambda h: (0, h, 0, 0))],
        out_specs=pl.BlockSpec((1, 15, 128), lambda h: (h, 0, 0)), out_shape=_sds((NA_HEADS, 15, 128), F32),
        compiler_params=_cp(("parallel",)))(dbias)


HG_RB = 128
HG_NB = T // HG_RB
HG_SLOTS = HG_NB * 8
HI = lax.Precision.HIGHEST
HG_UNROLL = 4


def _chunk_tri(lower):
    r = lax.broadcasted_iota(jnp.int32, (HG_RB, HG_RB), 0)
    c = lax.broadcasted_iota(jnp.int32, (HG_RB, HG_RB), 1)
    same = (r // HG_C) == (c // HG_C)
    keep = (c <= r) if lower else (c >= r)
    return jnp.where(same & keep, 1.0, 0.0).astype(F32)


def _hg_gate_terms(z, lg):
    dl = lg[0:1, :] - lg[1:2, :]
    log_lb = jax.nn.log_sigmoid(dl)
    log_1mlb = jax.nn.log_sigmoid(-dl)
    yz = log_1mlb + jax.nn.log_sigmoid(z)
    log_f = jnp.logaddexp(log_lb, yz)
    snz = jax.nn.sigmoid(-z)
    k = jnp.exp(log_1mlb) * snz
    w2 = jnp.exp(yz - log_f)
    return log_f, k, snz, w2


def _hg_pre(p_act, logits):
    def body(q_ref, zf_ref, zb_ref, lg_ref, qh_ref, kf_ref, bf_ref, kb_ref, bb_ref):
        qh_ref[...] = jax.nn.silu(q_ref[...])
        lf, kf, _, _ = _hg_gate_terms(zf_ref[...], lg_ref[0])
        kf_ref[...] = kf
        bf_ref[...] = jnp.dot(_chunk_tri(True), lf, precision=HI, preferred_element_type=F32)
        lb_, kb, _, _ = _hg_gate_terms(zb_ref[...], lg_ref[1])
        kb_ref[...] = kb
        bb_ref[...] = jnp.dot(_chunk_tri(False), lb_, precision=HI, preferred_element_type=F32)

    blk = lambda c: pl.BlockSpec((HG_RB, 512), lambda i: (i, c))
    ob = pl.BlockSpec((HG_RB, 512), lambda i: (i, 0))
    return pl.pallas_call(
        body, name="hg_pre", grid=(HG_NB,),
        in_specs=[blk(3), blk(4), blk(5), pl.BlockSpec((2, 2, 512), lambda i: (0, 0, 0))],
        out_specs=(ob,) * 5, out_shape=(_sds((T, 512), F32),) * 5,
        compiler_params=_cp(("parallel",)))(p_act, p_act, p_act, logits)


def _bdot(a, b, ca, cb):
    return lax.dot_general(a.astype(BF16), b.astype(BF16), (((ca,), (cb,)), ((0,), (0,))), preferred_element_type=F32)


HG_S = 8
HG_NS = HG_RB // HG_S


def _lane_sums(xs):
    l_io = lax.broadcasted_iota(jnp.int32, (HG_NS, HG_S, HG_S), 2)
    a = jnp.zeros((HG_NS, HG_S, HG_S), F32)
    for j, x in enumerate(xs):
        a = a + jnp.where(l_io == j, jnp.sum(x, axis=-1, keepdims=True), 0.0)
    return a


def _halves(x):
    y = x.reshape(8, 2, HG_S, x.shape[-1])
    return y[:, 0], y[:, 1]


def _join(first, second):
    return jnp.stack([first, second], axis=1).reshape(HG_RB, first.shape[-1])


def _cross_split(rev, b4):
    b_1, b_2 = _halves(b4)
    if rev:
        r = b_2[:, 0:1, :]
        return jnp.exp(b_1 - r), jnp.exp(r - b_2)
    r = b_1[:, HG_S - 1:HG_S, :]
    return jnp.exp(b_2 - r), jnp.exp(r - b_1)


def _hg_scan_fwd(qh, k, b, p_act, rev):
    anchor = 0 if rev else HG_C - 1

    def body(q_ref, k_ref, b_ref, v_ref, o_ref, st_ref, dsc):
        def phase_a(blk, _):
            rows = pl.ds(pl.multiple_of(blk * HG_RB, HG_RB), HG_RB)
            b3 = b_ref[rows, :].reshape(8, HG_C, 128)
            k3 = k_ref[rows, :].reshape(8, HG_C, 128)
            v3 = v_ref[rows, :].reshape(8, HG_C, 128)
            bl = b3[:, anchor:anchor + 1, :]
            kt = k3 * jnp.exp(bl - b3)
            st_ref[0, pl.ds(pl.multiple_of(blk * 8, 8), 8)] = _bdot(v3, kt, 1, 1)
            dsc[pl.ds(pl.multiple_of(blk * 8, 8), 8), :] = jnp.exp(bl[:, 0, :])
            return 0

        lax.fori_loop(0, HG_NB, phase_a, 0, unroll=HG_UNROLL)

        def phase_b(n, carry):
            c = (NCHUNK - 1 - n) if rev else n
            u = st_ref[0, c]
            st_ref[0, c] = carry
            return carry * dsc[pl.ds(c, 1), :] + u

        lax.fori_loop(0, NCHUNK // 3, lambda n3, s: phase_b(3 * n3 + 2, phase_b(3 * n3 + 1, phase_b(3 * n3, s))),
                      jnp.zeros((128, 128), F32))
        for c in range(NCHUNK, HG_SLOTS):
            st_ref[0, c] = jnp.zeros((128, 128), F32)

        t_io = lax.broadcasted_iota(jnp.int32, (HG_NS, HG_S, 128), 1)

        def phase_c(blk, _):
            rows = pl.ds(pl.multiple_of(blk * HG_RB, HG_RB), HG_RB)
            b4 = b_ref[rows, :].reshape(HG_NS, HG_S, 128)
            k4 = k_ref[rows, :].reshape(HG_NS, HG_S, 128)
            q4 = q_ref[rows, :].reshape(HG_NS, HG_S, 128)
            v4 = v_ref[rows, :].reshape(HG_NS, HG_S, 128)
            st = st_ref[0, pl.ds(pl.multiple_of(blk * 8, 8), 8)]
            o = _bdot((q4 * jnp.exp(b4)).reshape(8, HG_C, 128), st, 2, 2).reshape(HG_RB, 128)
            terms = []
            for s in range(HG_S):
                ok = (t_io <= s) if rev else (t_io >= s)
                f = jnp.exp(jnp.where(ok, b4 - b4[:, s:s + 1, :], NEG))
                terms.append(q4 * f * k4[:, s:s + 1, :])
            o_in = _bdot(_lane_sums(terms), v4, 2, 1)
            wq, wk = _cross_split(rev, b4)
            q_1, q_2 = _halves(q4)
            k_1, k_2 = _halves(k4)
            v_1, v_2 = _halves(v4)
            o_1, o_2 = _halves(o_in)
            if rev:
                o_1 = o_1 + _bdot(_bdot(q_1 * wq, k_2 * wk, 2, 2), v_2, 2, 1)
            else:
                o_2 = o_2 + _bdot(_bdot(q_2 * wq, k_1 * wk, 2, 2), v_1, 2, 1)
            o_ref[rows, :] = o + _join(o_1, o_2)
            return 0

        lax.fori_loop(0, HG_NB, phase_c, 0, unroll=HG_UNROLL)

    col = pl.BlockSpec((T, 128), lambda h: (0, h))
    return pl.pallas_call(
        body, name="hg_scan_bwd_dir" if rev else "hg_scan_fwd_dir", grid=(HG_HEADS,),
        in_specs=[col, col, col, pl.BlockSpec((T, 128), lambda h: (0, 24 + h))],
        out_specs=(col, pl.BlockSpec((1, HG_SLOTS, 128, 128), lambda h: (h, 0, 0, 0))),
        out_shape=(_sds((T, 512), F32), _sds((HG_HEADS, HG_SLOTS, 128, 128), F32)),
        scratch_shapes=[pltpu.VMEM((HG_SLOTS, 128), F32)],
        compiler_params=_cp(("parallel",), 56))(qh, k, b, p_act)


def _hg_scan_bwd(qh, k, b, p_act, st, do, rev):
    anchor = 0 if rev else HG_C - 1

    def body(q_ref, k_ref, b_ref, v_ref, st_ref, do_ref, dq_ref, dk_ref, db_ref, dv_ref, gst, dsc, dbl):
        def phase_a(blk, _):
            rows = pl.ds(pl.multiple_of(blk * HG_RB, HG_RB), HG_RB)
            b3 = b_ref[rows, :].reshape(8, HG_C, 128)
            q3 = q_ref[rows, :].reshape(8, HG_C, 128)
            do3 = do_ref[rows, :].reshape(8, HG_C, 128)
            gst[pl.ds(pl.multiple_of(blk * 8, 8), 8)] = _bdot(do3, q3 * jnp.exp(b3), 1, 1)
            dsc[pl.ds(pl.multiple_of(blk * 8, 8), 8), :] = jnp.exp(b3[:, anchor, :])
            return 0

        lax.fori_loop(0, HG_NB, phase_a, 0, unroll=HG_UNROLL)

        def phase_b(n, carry):
            c = n if rev else (NCHUNK - 1 - n)
            w = gst[c]
            gst[c] = carry
            dcv = dsc[pl.ds(c, 1), :]
            dbl[pl.ds(c, 1), :] = dcv * jnp.sum(st_ref[0, c] * carry, axis=0, keepdims=True)
            return carry * dcv + w

        lax.fori_loop(0, NCHUNK // 3, lambda n3, s: phase_b(3 * n3 + 2, phase_b(3 * n3 + 1, phase_b(3 * n3, s))),
                      jnp.zeros((128, 128), F32))
        for c in range(NCHUNK, HG_SLOTS):
            gst[c] = jnp.zeros((128, 128), F32)
            dbl[c:c + 1, :] = jnp.zeros((1, 128), F32)

        t_io = lax.broadcasted_iota(jnp.int32, (HG_NS, HG_S, 128), 1)
        t16 = lax.broadcasted_iota(jnp.int32, (8, HG_C, 128), 1)
        r_io = lax.broadcasted_iota(jnp.int32, (HG_NS, HG_S, HG_S), 1)
        l_io = lax.broadcasted_iota(jnp.int32, (HG_NS, HG_S, HG_S), 2)

        def phase_c(blk, _):
            rows = pl.ds(pl.multiple_of(blk * HG_RB, HG_RB), HG_RB)
            cs = pl.ds(pl.multiple_of(blk * 8, 8), 8)
            b4 = b_ref[rows, :].reshape(HG_NS, HG_S, 128)
            k4 = k_ref[rows, :].reshape(HG_NS, HG_S, 128)
            q4 = q_ref[rows, :].reshape(HG_NS, HG_S, 128)
            v4 = v_ref[rows, :].reshape(HG_NS, HG_S, 128)
            do4 = do_ref[rows, :].reshape(HG_NS, HG_S, 128)
            b3, k3, q3 = (z.reshape(8, HG_C, 128) for z in (b4, k4, q4))
            v3, do3 = v4.reshape(8, HG_C, 128), do4.reshape(8, HG_C, 128)
            s_t = st_ref[0, cs]
            g_t = gst[cs]
            bl = b3[:, anchor:anchor + 1, :]
            ekl = jnp.exp(bl - b3)
            kt = k3 * ekl
            dkt = _bdot(v3, g_t, 2, 1)
            dq = (_bdot(do3, s_t, 2, 1) * jnp.exp(b3)).reshape(HG_NS, HG_S, 128)
            dk = (dkt * ekl).reshape(HG_NS, HG_S, 128)
            dv = _bdot(kt, g_t, 2, 2).reshape(HG_NS, HG_S, 128)
            dbl3 = dbl[cs, :].reshape(8, 1, 128) + jnp.sum(dkt * kt, axis=1, keepdims=True)
            causal = (l_io >= r_io) if rev else (l_io <= r_io)
            da = jnp.where(causal, _bdot(do4, v4, 2, 2), 0.0)
            causal_t = (l_io <= r_io) if rev else (l_io >= r_io)
            dat = jnp.where(causal_t, _bdot(v4, do4, 2, 2), 0.0)
            for s in range(HG_S):
                ok = (t_io <= s) if rev else (t_io >= s)
                f = jnp.exp(jnp.where(ok, b4 - b4[:, s:s + 1, :], NEG))
                dq = dq + da[:, :, s:s + 1] * (f * k4[:, s:s + 1, :])
            terms = []
            for t in range(HG_S):
                ok = (t_io >= t) if rev else (t_io <= t)
                e = jnp.exp(jnp.where(ok, b4[:, t:t + 1, :] - b4, NEG))
                eq = e * q4[:, t:t + 1, :]
                dk = dk + dat[:, :, t:t + 1] * eq
                terms.append(eq * k4)
            dv = dv + _bdot(_lane_sums(terms), do4, 2, 1)
            wq, wk = _cross_split(rev, b4)
            pick = (lambda z: _halves(z)) if rev else (lambda z: _halves(z)[::-1])
            (q_q, _), (_, k_k), (_, v_k), (do_q, _) = pick(q4), pick(k4), pick(v4), pick(do4)
            qx, kx = q_q * wq, k_k * wk
            dq_q = _bdot(_bdot(do_q, v_k, 2, 2), kx, 2, 1) * wq
            dk_k = _bdot(_bdot(v_k, do_q, 2, 2), qx, 2, 1) * wk
            dv_k = _bdot(_bdot(kx, qx, 2, 2), do_q, 2, 1)
            zero = jnp.zeros((8, HG_S, 128), F32)
            place_q = (lambda z: _join(z, zero)) if rev else (lambda z: _join(zero, z))
            place_k = (lambda z: _join(zero, z)) if rev else (lambda z: _join(z, zero))
            dq2 = dq.reshape(HG_RB, 128) + place_q(dq_q)
            dk2 = dk.reshape(HG_RB, 128) + place_k(dk_k)
            dv2 = dv.reshape(HG_RB, 128) + place_k(dv_k)
            dq3, dk3 = dq2.reshape(8, HG_C, 128), dk2.reshape(8, HG_C, 128)
            db = q3 * dq3 - k3 * dk3 + jnp.where(t16 == anchor, dbl3, 0.0)
            dq_ref[rows, :] = dq2
            dk_ref[rows, :] = dk2
            db_ref[rows, :] = db.reshape(HG_RB, 128)
            dv_ref[rows, :] = dv2
            return 0

        lax.fori_loop(0, HG_NB, phase_c, 0, unroll=HG_UNROLL)

    col = pl.BlockSpec((T, 128), lambda h: (0, h))
    return pl.pallas_call(
        body, name="hg_scan_bwd_dir_bwd" if rev else "hg_scan_fwd_dir_bwd", grid=(HG_HEADS,),
        in_specs=[col, col, col, pl.BlockSpec((T, 128), lambda h: (0, 24 + h)),
                  pl.BlockSpec((1, HG_SLOTS, 128, 128), lambda h: (h, 0, 0, 0)), col],
        out_specs=(col,) * 4, out_shape=(_sds((T, 512), F32),) * 4,
        scratch_shapes=[pltpu.VMEM((HG_SLOTS, 128, 128), F32), pltpu.VMEM((HG_SLOTS, 128), F32),
                        pltpu.VMEM((HG_SLOTS, 128), F32)],
        compiler_params=_cp(("parallel",), 56))(qh, k, b, p_act, st, do)


def _row_valid(i, tm):
    r = lax.broadcasted_iota(jnp.int32, (tm, 1), 0) + i * tm
    return r < L


def _hg_post_rows(o, gv, gain_v, valid):
    parts = []
    for h in range(HG_HEADS):
        oh = o[:, 128 * h:128 * (h + 1)]
        parts.append(oh * lax.rsqrt(jnp.mean(oh * oh, axis=-1, keepdims=True) + EPS))
    return jnp.where(valid, jnp.concatenate(parts, axis=1) * gain_v * jax.nn.silu(gv), 0.0)


def _hg_post_bwd_rows(du, o, gv, gain_v, valid):
    duv = jnp.where(valid, du, 0.0)
    sig = jax.nn.sigmoid(gv)
    sg = gv * sig
    dn = duv * gain_v * sg
    do_parts, n_parts = [], []
    for h in range(HG_HEADS):
        sl = slice(128 * h, 128 * (h + 1))
        oh = o[:, sl]
        r = lax.rsqrt(jnp.mean(oh * oh, axis=-1, keepdims=True) + EPS)
        nh = oh * r
        dnh = dn[:, sl]
        do_parts.append(r * (dnh - nh * jnp.mean(dnh * nh, axis=-1, keepdims=True)))
        n_parts.append(nh)
    n = jnp.where(valid, jnp.concatenate(n_parts, axis=1), 0.0)
    do = jnp.where(valid, jnp.concatenate(do_parts, axis=1), 0.0)
    dg = duv * n * gain_v * (sig * (1.0 + gv * (1.0 - sig)))
    return do, dg, jnp.sum(duv * n * sg, axis=0, keepdims=True)


def _hg_pre_bwd(p_act, logits, dq_f, dq_b, dk_f, dk_b, db_f, db_b, dv_f, dv_b, dp_rest):
    def body(q_ref, zf_ref, zb_ref, lg_ref, dqf_ref, dqb_ref, dkf_ref, dkb_ref, dbf_ref, dbb_ref, dvf_ref, dvb_ref, _,
             dp_ref, dlg_ref):
        dq_ref, dzf_ref, dzb_ref, di_ref = (dp_ref.at[:, 512 * c:512 * (c + 1)] for c in range(4))
        i = pl.program_id(0)
        valid = _row_valid(i, HG_RB)
        qv = q_ref[...]
        sig = jax.nn.sigmoid(qv)
        dq_ref[...] = jnp.where(valid, (dqf_ref[...] + dqb_ref[...]) * (sig * (1.0 + qv * (1.0 - sig))), 0.0).astype(BF16)
        di_ref[...] = jnp.where(valid, dvf_ref[...] + dvb_ref[...], 0.0).astype(BF16)
        for d, (z_ref, dk_r, db_r, dz_ref) in enumerate(((zf_ref, dkf_ref, dbf_ref, dzf_ref), (zb_ref, dkb_ref, dbb_ref, dzb_ref))):
            lg = lg_ref[d]
            dl = lg[0:1, :] - lg[1:2, :]
            lb = jax.nn.sigmoid(dl)
            one_m_lb = jax.nn.sigmoid(-dl)
            log_f, _, snz, w2 = _hg_gate_terms(z_ref[...], lg)
            dbv = jnp.where(valid, db_r[...], 0.0)
            dkv = jnp.where(valid, dk_r[...], 0.0)
            dlf = jnp.dot(_chunk_tri(d == 1), dbv, precision=HI, preferred_element_type=F32)
            sz = 1.0 - snz
            dz_ref[...] = (dlf * w2 * snz - dkv * one_m_lb * sz * snz).astype(BF16)
            dlb = jnp.sum(dlf * snz * jnp.exp(-log_f) - dkv * snz, axis=0, keepdims=True)
            dl0 = dlb * lb * one_m_lb
            part = jnp.concatenate([dl0, -dl0], axis=0)

            @pl.when(i == 0)
            def _():
                dlg_ref[d] = part

            @pl.when(i > 0)
            def _():
                dlg_ref[d] += part

    blk = lambda c: pl.BlockSpec((HG_RB, 512), lambda i: (i, c))
    ob = pl.BlockSpec((HG_RB, 512), lambda i: (i, 0))
    lgs = pl.BlockSpec((2, 2, 512), lambda i: (0, 0, 0))
    return pl.pallas_call(
        body, name="hg_pre_bwd", grid=(HG_NB,),
        in_specs=[blk(3), blk(4), blk(5), lgs] + [ob] * 8 + [ANY],
        out_specs=(pl.BlockSpec((HG_RB, 2048), lambda i: (i, 0)), lgs),
        out_shape=(_sds(dp_rest.shape, BF16), _sds((2, 2, 512), F32)), input_output_aliases={12: 0},
        compiler_params=_cp(("arbitrary",)))(p_act, p_act, p_act, logits, dq_f, dq_b, dk_f, dk_b, db_f, db_b, dv_f, dv_b,
                                             dp_rest)


def _mix_fwd(o_na, o_f, o_b, gain, w_na, w_hg, p_act):
    def body(ona_ref, of_ref, ob_ref, g_ref, gain_ref, wna_ref, whg_ref, gna_ref, ghg_ref, o_ref, u_ref):
        u = _hg_post_rows(of_ref[...] + ob_ref[...], g_ref[...], gain_ref[...], _row_valid(pl.program_id(0), TM_B)).astype(BF16)
        u_ref[...] = u
        y_na = _dot(ona_ref[...], wna_ref[...])
        y_hg = _dot(u, whg_ref[...])
        o_ref[...] = (jax.nn.sigmoid(gna_ref[...]) * y_na + jax.nn.sigmoid(ghg_ref[...]) * y_hg).astype(BF16)

    act = pl.BlockSpec((TM_B, 512), lambda i: (i, 0))
    wsp = pl.BlockSpec((512, D), lambda i: (0, 0))
    return pl.pallas_call(
        body, name="mix_fwd", grid=(T // TM_B,),
        in_specs=[act, act, act, pl.BlockSpec((TM_B, 512), lambda i: (i, 7)), pl.BlockSpec((1, 512), lambda i: (0, 0)),
                  wsp, wsp, pl.BlockSpec((TM_B, D), lambda i: (i, 4)), pl.BlockSpec((TM_B, D), lambda i: (i, 5))],
        out_specs=(pl.BlockSpec((TM_B, D), lambda i: (i, 0)), act), out_shape=(_sds((T, D), BF16), _sds((T, 512), BF16)),
        compiler_params=_cp(("parallel",)))(o_na, o_f, o_b, p_act, gain, w_na, w_hg, p_act, p_act)


DP_REST = IN_COLS - 1536


def _mix_bwd(o_na, u_hg, o_f, o_b, gain, w_na, w_hg, p_act, dmix):
    ni = T // TM_B

    def body(ona_ref, uhg_ref, of_ref, ob_ref, g_ref, gain_ref, wna_ref, whg_ref, gna_ref, ghg_ref, dmix_ref,
             dp_ref, dwna_ref, dwhg_ref, dona_ref, do_ref, dgain_ref, acc_na, acc_hg):
        i = pl.program_id(0)
        dg_ref, dgna_ref, dghg_ref = dp_ref.at[:, 2048:2560], dp_ref.at[:, 2560:3584], dp_ref.at[:, 3584:4608]
        dm = dmix_ref[...].astype(F32)
        dxs = []
        for x_ref, w_ref, gt_ref, dgt_ref, dw_ref, acc in (
                (ona_ref, wna_ref, gna_ref, dgna_ref, dwna_ref, acc_na), (uhg_ref, whg_ref, ghg_ref, dghg_ref, dwhg_ref, acc_hg)):
            xv = x_ref[...]
            y = _dot(xv, w_ref[...])
            sg = jax.nn.sigmoid(gt_ref[...])
            dgt_ref[...] = (dm * y * sg * (1.0 - sg)).astype(BF16)
            dy = (dm * sg).astype(BF16)
            dxs.append(_dot(dy, w_ref[...], NT))
            part = _dot(xv, dy, TN)

            @pl.when(i == 0)
            def _():
                acc[...] = part

            @pl.when(i > 0)
            def _():
                acc[...] += part

            @pl.when(i == ni - 1)
            def _():
                dw_ref[...] = acc[...].astype(BF16)

        dona_ref[...] = dxs[0]
        do, dg, gpart = _hg_post_bwd_rows(dxs[1], of_ref[...] + ob_ref[...], g_ref[...], gain_ref[...], _row_valid(i, TM_B))
        do_ref[...] = do
        dg_ref[...] = dg.astype(BF16)

        @pl.when(i == 0)
        def _():
            dgain_ref[...] = gpart

        @pl.when(i > 0)
        def _():
            dgain_ref[...] += gpart

    act = pl.BlockSpec((TM_B, 512), lambda i: (i, 0))
    wsp = pl.BlockSpec((512, D), lambda i: (0, 0))
    rblk = pl.BlockSpec((TM_B, D), lambda i: (i, 0))
    vec = pl.BlockSpec((1, 512), lambda i: (0, 0))
    return pl.pallas_call(
        body, name="mix_bwd", grid=(ni,),
        in_specs=[act, act, act, act, pl.BlockSpec((TM_B, 512), lambda i: (i, 7)), vec, wsp, wsp,
                  pl.BlockSpec((TM_B, D), lambda i: (i, 4)), pl.BlockSpec((TM_B, D), lambda i: (i, 5)), rblk],
        out_specs=(pl.BlockSpec((TM_B, DP_REST), lambda i: (i, 0)), wsp, wsp, act, act, vec),
        out_shape=(_sds((T, DP_REST), BF16), _sds((512, D), BF16), _sds((512, D), BF16),
                   _sds((T, 512), F32), _sds((T, 512), F32), _sds((1, 512), F32)),
        scratch_shapes=[pltpu.VMEM((512, D), F32), pltpu.VMEM((512, D), F32)],
        compiler_params=_cp(("arbitrary",)))(o_na, u_hg, o_f, o_b, p_act, gain, w_na, w_hg, p_act, p_act, dmix)


def _wo_fwd(mix, w_o, h0, g_mlp):
    def body(mix_ref, w_ref, h0_ref, g_ref, h1_ref, m_ref):
        h1 = h0_ref[...] + _dot(mix_ref[...], w_ref[...])
        h1_ref[...] = h1
        r = lax.rsqrt(jnp.mean(h1 * h1, axis=-1, keepdims=True) + EPS)
        m_ref[...] = (h1 * r * g_ref[...]).astype(BF16)

    blk = pl.BlockSpec((TM_B, D), lambda i: (i, 0))
    return pl.pallas_call(
        body, name="wo_fwd", grid=(T // TM_B,),
        in_specs=[blk, pl.BlockSpec((D, D), lambda i: (0, 0)), blk, pl.BlockSpec((1, D), lambda i: (0, 0))],
        out_specs=(blk, blk), out_shape=(_sds((T, D), F32), _sds((T, D), BF16)),
        compiler_params=_cp(("parallel",)))(mix, w_o, h0, g_mlp)


def _wo_bwd(dh1_b, w_o, mix):
    ni = T // TM_B

    def body(dh_ref, w_ref, mix_ref, dmix_ref, dw_ref, acc):
        i = pl.program_id(0)
        dh = dh_ref[...]
        dmix_ref[...] = _dot(dh, w_ref[...], NT).astype(BF16)
        part = _dot(mix_ref[...], dh, TN)

        @pl.when(i == 0)
        def _():
            acc[...] = part

        @pl.when(i > 0)
        def _():
            acc[...] += part

        @pl.when(i == ni - 1)
        def _():
            dw_ref[...] = acc[...].astype(BF16)

    blk = pl.BlockSpec((TM_B, D), lambda i: (i, 0))
    wsp = pl.BlockSpec((D, D), lambda i: (0, 0))
    return pl.pallas_call(
        body, name="wo_bwd", grid=(ni,), in_specs=[blk, wsp, blk], out_specs=(blk, wsp),
        out_shape=(_sds((T, D), BF16), _sds((D, D), BF16)), scratch_shapes=[pltpu.VMEM((D, D), F32)],
        compiler_params=_cp(("arbitrary",)))(dh1_b, w_o, mix)


FF_B = D_FF // NDEV


def _loss_rows(xv, gv, tv, row0):
    r_io = lax.broadcasted_iota(jnp.int32, (xv.shape[0], 1), 0) + row0
    valid = (r_io >= NM) & (r_io < L)
    r = lax.rsqrt(jnp.mean(xv * xv, axis=-1, keepdims=True) + EPS)
    xh = xv * r
    err = jnp.where(valid, xh * gv - tv, 0.0)
    lpart = 0.5 * jnp.sum(jnp.sum(err * err, axis=-1, keepdims=True) * (1.0 / D), axis=0, keepdims=True)
    dy = err * (1.0 / D)
    dxh = dy * gv
    dh = r * (dxh - xh * jnp.mean(dxh * xh, axis=-1, keepdims=True))
    return lpart, dh, jnp.sum(dy * xh, axis=0, keepdims=True)


def _mlp_fwd_loss(m, wup_g, wdown_g, h1, g_final, tgt):
    nsub = TM_MM // TM_E

    def body(m_ref, wu_ref, wd_ref, h1_ref, g_ref, t_ref, loss_ref, dh_ref, dhb_ref, dg_ref, h2):
        i, j = pl.program_id(0), pl.program_id(1)
        up = jnp.maximum(_dot(m_ref[...], wu_ref[0]), 0.0)
        part = _dot((up * up).astype(BF16), wd_ref[0])

        @pl.when(j == 0)
        def _():
            h2[...] = h1_ref[...] + part

        @pl.when(j > 0)
        def _():
            h2[...] += part

        @pl.when(j == NDEV - 1)
        def _():
            lsum = jnp.zeros((1, 1), F32)
            gsum = jnp.zeros((1, D), F32)
            for s in range(nsub):
                rows = slice(s * TM_E, (s + 1) * TM_E)
                lpart, dh, gpart = _loss_rows(h2[rows, :], g_ref[...], t_ref[rows, :], i * TM_MM + s * TM_E)
                dh_ref[rows, :] = dh
                dhb_ref[rows, :] = dh.astype(BF16)
                lsum = lsum + lpart
                gsum = gsum + gpart
            lsum = jnp.broadcast_to(lsum, (1, 128))

            @pl.when(i == 0)
            def _():
                loss_ref[...] = lsum
                dg_ref[...] = gsum

            @pl.when(i > 0)
            def _():
                loss_ref[...] += lsum
                dg_ref[...] += gsum

    blk = pl.BlockSpec((TM_MM, D), lambda i, j: (i, 0))
    vec = pl.BlockSpec((1, D), lambda i, j: (0, 0))
    return pl.pallas_call(
        body, name="mlp_fwd_loss", grid=(T // TM_MM, NDEV),
        in_specs=[blk, pl.BlockSpec((1, D, FF_B), lambda i, j: (j, 0, 0)), pl.BlockSpec((1, FF_B, D), lambda i, j: (j, 0, 0)),
                  blk, vec, blk],
        out_specs=(pl.BlockSpec((1, 128), lambda i, j: (0, 0)), blk, blk, vec),
        out_shape=(_sds((1, 128), F32), _sds((T, D), F32), _sds((T, D), BF16), _sds((1, D), F32)),
        scratch_shapes=[pltpu.VMEM((TM_MM, D), F32)],
        compiler_params=_cp(("arbitrary", "arbitrary"), 56))(m, wup_g, wdown_g, h1, g_final, tgt)


def _mlp_bwd(m, dh2_b, wup_g, wdown_g, h1, g_mlp, dh2):
    ni = T // TM_B
    nsub = TM_B // TM_E

    def body(m_ref, dh_ref, wu_ref, wd_ref, h1_ref, g_ref, dres_ref, dwu_ref, dwd_ref, dh1_ref, dh1b_ref, dg_ref,
             dm_ref, acc_u, acc_d):
        j, i = pl.program_id(0), pl.program_id(1)
        rows = pl.ds(pl.multiple_of(i * TM_B, TM_B), TM_B)
        mv, dh = m_ref[...], dh_ref[...]
        r = jnp.maximum(_dot(mv, wu_ref[0]), 0.0)
        act = (r * r).astype(BF16)
        dact = _dot(dh, wd_ref[0], NT)
        dup = (dact * (2.0 * r)).astype(BF16)
        pd = _dot(act, dh, TN)
        pu = _dot(mv, dup, TN)
        dmv = _dot(dup, wu_ref[0], NT)

        @pl.when(i == 0)
        def _():
            acc_u[...] = pu
            acc_d[...] = pd

        @pl.when(i > 0)
        def _():
            acc_u[...] += pu
            acc_d[...] += pd

        @pl.when(i == ni - 1)
        def _():
            dwu_ref[0] = acc_u[...].astype(BF16)
            dwd_ref[0] = acc_d[...].astype(BF16)

        @pl.when(j == 0)
        def _():
            dm_ref[rows, :] = dmv

        @pl.when(j > 0)
        def _():
            dm_ref[rows, :] += dmv

        @pl.when(j == NDEV - 1)
        def _():
            gsum = jnp.zeros((1, D), F32)
            for s in range(nsub):
                sub = slice(s * TM_E, (s + 1) * TM_E)
                dm_rows = dm_ref[pl.ds(pl.multiple_of(i * TM_B + s * TM_E, TM_E), TM_E), :]
                dx, gpart = _norm_bwd_rows(h1_ref[sub, :], g_ref[...], dm_rows, dres_ref[sub, :])
                dh1_ref[sub, :] = dx
                dh1b_ref[sub, :] = dx.astype(BF16)
                gsum = gsum + gpart

            @pl.when(i == 0)
            def _():
                dg_ref[...] = gsum

            @pl.when(i > 0)
            def _():
                dg_ref[...] += gsum

    blk = pl.BlockSpec((TM_B, D), lambda j, i: (i, 0))
    late = pl.BlockSpec((TM_B, D), lambda j, i: (jnp.where(j == NDEV - 1, i, 0), 0))
    vec = pl.BlockSpec((1, D), lambda j, i: (0, 0))
    wus = pl.BlockSpec((1, D, FF_B), lambda j, i: (j, 0, 0))
    wds = pl.BlockSpec((1, FF_B, D), lambda j, i: (j, 0, 0))
    return pl.pallas_call(
        body, name="mlp_bwd", grid=(NDEV, ni), in_specs=[blk, blk, wus, wds, late, vec, late],
        out_specs=(wus, wds, late, late, vec),
        out_shape=(_sds((NDEV, D, FF_B), BF16), _sds((NDEV, FF_B, D), BF16), _sds((T, D), F32), _sds((T, D), BF16),
                   _sds((1, D), F32)),
        scratch_shapes=[pltpu.VMEM((T, D), F32), pltpu.VMEM((D, FF_B), F32), pltpu.VMEM((FF_B, D), F32)],
        compiler_params=_cp(("arbitrary", "arbitrary"), 56))(m, dh2_b, wup_g, wdown_g, h1, g_mlp, dh2)


def _adamw(parts, w, m, v, name):
    rr, cc = w.shape
    tr = rr
    for cand in (256, 128, 64):
        if rr % cand == 0 and rr > cand:
            tr = cand
            break
    c1 = 1.0 - ADAM_B1 ** ADAM_STEP
    c2 = 1.0 - ADAM_B2 ** ADAM_STEP

    def body(p_ref, w_ref, m_ref, v_ref, g_ref, d_ref, nm_ref, nv_ref):
        g = p_ref[0].astype(F32)
        for s in range(1, NDEV):
            g = g + p_ref[s].astype(F32)
        mn = ADAM_B1 * m_ref[...] + (1.0 - ADAM_B1) * g
        vn = ADAM_B2 * v_ref[...] + (1.0 - ADAM_B2) * (g * g)
        g_ref[...] = g
        nm_ref[...] = mn
        nv_ref[...] = vn
        d_ref[...] = -ADAM_LR * ((mn / c1) / (jnp.sqrt(vn / c2) + ADAM_EPS) + ADAM_WD * w_ref[...])

    blk = pl.BlockSpec((tr, cc), lambda i: (i, 0))
    return pl.pallas_call(
        body, name=name, grid=(rr // tr,),
        in_specs=[pl.BlockSpec((NDEV, tr, cc), lambda i: (0, i, 0)), blk, blk, blk],
        out_specs=(blk,) * 4, out_shape=(_sds((rr, cc), F32),) * 4,
        compiler_params=_cp(("parallel",)))(parts, w, m, v)


RPB_N = NA_HEADS * 15 * 31
RPB_PAD = 4096
OWN_ROWS = NM + 8


def _pad_rows(a, rows):
    return jnp.pad(a, ((0, rows - a.shape[0]),) + ((0, 0),) * (a.ndim - 1))


def _pack_owned(meta_blk, lb_blk):
    return jnp.concatenate([meta_blk, _pad_rows(lb_blk.reshape(2, 128), 8)], axis=0)


LOSS_ROW = 28


def _pack_replicated(n_mix, n_mlp, n_final, hg_gain, rpb, loss_row=None):
    flat = _pad_rows(rpb.reshape(RPB_N), RPB_PAD)
    gain8 = _pad_rows(hg_gain.reshape(4, 128), 8)
    if loss_row is not None:
        gain8 = gain8 + jnp.pad(loss_row, ((LOSS_ROW - 24, 31 - LOSS_ROW), (0, 0)))
    return jnp.concatenate([n_mix.reshape(8, 128), n_mlp.reshape(8, 128), n_final.reshape(8, 128), gain8,
                            flat.reshape(32, 128)], axis=0)


def _unpack_replicated(a):
    return (a[0:8].reshape(1, D), a[8:16].reshape(1, D), a[16:24].reshape(D), a[24:28].reshape(1, 512),
            a[32:64].reshape(RPB_PAD)[:RPB_N].reshape(1, NA_HEADS, 15, 31))


def kernel(x, meta_tokens, w_in, w_na_out, w_hg_out, w_o, w_up, w_down, norm_mix, norm_mlp, norm_final, hg_norm, na_rpb, hg_lb_logits, loss_target, m_meta_tokens, m_w_in, m_w_na_out, m_w_hg_out, m_w_o, m_w_up, m_w_down, m_norm_mix, m_norm_mlp, m_norm_final, m_hg_norm, m_na_rpb, m_hg_lb_logits, v_meta_tokens, v_w_in, v_w_na_out, v_w_hg_out, v_w_o, v_w_up, v_w_down, v_norm_mix, v_norm_mlp, v_norm_final, v_hg_norm, v_na_rpb, v_hg_lb_logits):
    owned = _pack_owned(meta_tokens, hg_lb_logits)
    first, tok = _exchange_start([w_in[0].astype(BF16), owned], [False] * 2, "gather_first_start", SAME_CORE_AND_SIBLING)
    bias_tab = _na_bias_table(_tie(jnp.pad(na_rpb[0], ((0, 0), (0, 0), (0, 128 - 31))), tok, "tie_bias_table"))
    later = [w[0].astype(BF16) for w in (w_na_out, w_hg_out, w_o, w_up, w_down)]
    lead = jnp.zeros((NM, D), F32) + tok[0, 0]
    h0_rows = jnp.concatenate([lead, x[0], jnp.zeros((T - L, D), F32)], axis=0)
    tgt = jnp.concatenate([lead, loss_target[0], jnp.zeros((T - L, D), F32)], axis=0)
    first = _exchange_wait(first, [False] * 2, [bias_tab, h0_rows, tgt] + later, "gather_first_wait", SAME_CORE_AND_SIBLING)
    win_g, owned_g = _forward_to_sibling(first, "gather_first_forward")
    later[0] = _tie(later[0], owned_g, "tie_gather_rest")
    gather_rest, tok = _exchange_start(later, [False] * 5, "gather_rest_start")
    win_g = _tie(win_g, tok, "tie_inproj")
    meta_full = jnp.transpose(owned_g[:, 0:NM, :], (1, 0, 2)).reshape(NM, D)
    logits = jnp.transpose(owned_g[:, NM:NM + 2, :].reshape(NDEV, 2, 2, 64), (1, 2, 0, 3)).reshape(2, 2, 512)

    h0 = lax.dynamic_update_slice(h0_rows, meta_full, (0, 0))

    a, a_t = _norm_fwd_t(h0, norm_mix, "norm_mix_fwd")
    p_act = _inproj_fwd(a, win_g)
    o_na, lse = _na_fwd(p_act, bias_tab)
    qh, k_f, b_f, k_b, b_b = _hg_pre(p_act, logits)
    o_f, st_f = _hg_scan_fwd(qh, k_f, b_f, p_act, False)
    o_b, st_b = _hg_scan_fwd(qh, k_b, b_b, p_act, True)
    (wna_g, whg_g, wo_g, _, _), gather_rest = _exchange_wait(
        gather_rest, [False] * 5, [o_f, o_b, o_na], "gather_rest_wait_a", which=(0, 1, 2))
    w_na_full = jnp.transpose(wna_g, (1, 0, 2)).reshape(512, D)
    w_hg_full = jnp.transpose(whg_g, (1, 0, 2)).reshape(512, D)
    mix, u_hg = _mix_fwd(o_na, o_f, o_b, hg_norm, w_na_full, w_hg_full, p_act)
    h1, m_act = _wo_fwd(mix, wo_g.reshape(D, D), h0, norm_mlp)
    (_, _, wo_g, wup_g, wdown_g), _ = _exchange_wait(gather_rest, [False] * 5, [m_act], "gather_rest_wait_b", which=(3, 4))
    w_o_full = wo_g.reshape(D, D)
    loss_part, dh2, dh2_b, d_nfinal = _mlp_fwd_loss(m_act, wup_g, wdown_g, h1, norm_final.reshape(1, D), tgt)

    dwup_p, dwdown_p, dh1, dh1_b, d_nmlp = _mlp_bwd(m_act, dh2_b, wup_g, wdown_g, h1, norm_mlp, dh2)
    sc_mlp, tok = _exchange_start([dwup_p, dwdown_p], [True] * 2, "scatter_mlp_start")
    dmix, dwo = _wo_bwd(_tie(dh1_b, tok, "tie_wo_bwd"), w_o_full, mix)
    sc_wo, tok = _exchange_start([dwo.reshape(NDEV, D // NDEV, D)], [True], "scatter_wo_start")
    dp_rest, dwna, dwhg, do_na, do_hg, d_gain = _mix_bwd(
        o_na, u_hg, o_f, o_b, hg_norm, w_na_full, w_hg_full, p_act, _tie(dmix, tok, "tie_mix_bwd"))
    owner_cols = lambda w: jnp.transpose(w.reshape(512, NDEV, D // NDEV), (1, 0, 2))
    sc_br, tok = _exchange_start([owner_cols(dwna), owner_cols(dwhg)], [True] * 2, "scatter_branch_start")
    do_hg = _tie(do_hg, tok, "tie_hg_scan_bwd")
    dq_f, dk_f, db_f, dv_f = _hg_scan_bwd(qh, k_f, b_f, p_act, st_f, do_hg, False)
    dq_b, dk_b, db_b, dv_b = _hg_scan_bwd(qh, k_b, b_b, p_act, st_b, do_hg, True)
    dp_rest, d_logits = _hg_pre_bwd(p_act, logits, dq_f, dq_b, dk_f, dk_b, db_f, db_b, dv_f, dv_b, dp_rest)
    dq_na, dk_na, dv_na, dbias = _na_bwd(p_act, do_na, lse, bias_tab)
    dp_na = jnp.concatenate([dq_na.astype(BF16), dk_na.astype(BF16), dv_na.astype(BF16)], axis=1)
    dwin_p = _inproj_bwd_dw(a_t, dp_na, dp_rest)
    sc_in, tok = _exchange_start([dwin_p], [True], "scatter_in_start")
    dh0, d_nmix = _inproj_bwd_da(_tie(dp_na, tok, "tie_inproj_bwd_da"), dp_rest, win_g, h0, norm_mix, dh1)
    d_rpb = _na_rpb_reduce(_tie(dbias, tok, "tie_rpb_reduce"))[:, :, :31]

    res = {}

    def update(nm, parts, w, mm, vv):
        res[nm] = [r[None] for r in _adamw(parts, w[0], mm[0], vv[0], "adamw_" + nm)]
        return res[nm][1]

    wup_r, wdown_r = _exchange_wait(sc_mlp, [True] * 2, [dh0, d_rpb], "scatter_mlp_wait")
    update("w_up", wup_r, w_up, m_w_up, v_w_up)
    last = update("w_down", wdown_r, w_down, m_w_down, v_w_down)
    (wo_r,) = _exchange_wait(sc_wo, [True], [last], "scatter_wo_wait")
    last = update("w_o", wo_r, w_o, m_w_o, v_w_o)
    wna_r, whg_r = _exchange_wait(sc_br, [True] * 2, [last], "scatter_branch_wait")
    update("w_na_out", wna_r, w_na_out, m_w_na_out, v_w_na_out)
    last = update("w_hg_out", whg_r, w_hg_out, m_w_hg_out, v_w_hg_out)

    d_meta = jnp.transpose(dh0[0:NM].reshape(NM, NDEV, 128), (1, 0, 2))
    d_lg = jnp.transpose(d_logits.reshape(2, 2, NDEV, 64), (2, 0, 1, 3)).reshape(NDEV, 2, 128)
    owned_p = jnp.concatenate([d_meta, jnp.pad(d_lg, ((0, 0), (0, OWN_ROWS - NM - 2), (0, 0)))], axis=1)
    repl_p = _pack_replicated(d_nmix, d_nmlp, d_nfinal, d_gain, d_rpb, loss_part)
    grad_x = dh0[NM:L][None]
    done_first = [grad_x] + [res[nm][0] for nm in ("w_up", "w_down", "w_o", "w_na_out", "w_hg_out")]
    owned_r, repl_r = _exchange([owned_p, repl_p], [True, False], "scatter_small", done_first)
    own = _adamw(owned_r, owned, _pack_owned(m_meta_tokens, m_hg_lb_logits), _pack_owned(v_meta_tokens, v_hg_lb_logits),
                 "adamw_owned_small")
    res["meta_tokens"] = [r[0:NM] for r in own]
    res["hg_lb_logits"] = [r[NM:NM + 2].reshape(2, 2, 64) for r in own]
    rep = _adamw(repl_r, _pack_replicated(norm_mix, norm_mlp, norm_final, hg_norm, na_rpb),
                 _pack_replicated(m_norm_mix, m_norm_mlp, m_norm_final, m_hg_norm, m_na_rpb),
                 _pack_replicated(v_norm_mix, v_norm_mlp, v_norm_final, v_hg_norm, v_na_rpb), "adamw_replicated")
    for q in range(4):
        um = _unpack_replicated(rep[q])
        for nm, val in zip(("norm_mix", "norm_mlp", "norm_final", "hg_norm", "na_rpb"), um):
            res.setdefault(nm, [None] * 4)[q] = val
    (win_r,) = _exchange_wait(sc_in, [True], [rep[1], own[1]], "scatter_in_wait")
    update("w_in", win_r, w_in, m_w_in, v_w_in)

    loss = jnp.sum(repl_r[:, LOSS_ROW, 0])
    order = ("meta_tokens", "w_in", "w_na_out", "w_hg_out", "w_o", "w_up", "w_down", "norm_mix", "norm_mlp", "norm_final",
             "hg_norm", "na_rpb", "hg_lb_logits")
    outs = [loss, grad_x]
    for q in range(4):
        outs += [res[nm][q] for nm in order]
    return tuple(outs)
```

```python
import functools

import numpy as np
import jax
import jax.numpy as jnp
from jax import lax
from jax.experimental import pallas as pl
from jax.experimental.pallas import tpu as pltpu

F32 = jnp.float32
BF16 = jnp.bfloat16

D = 1024
SEQ = 2048
NM = 16
L = SEQ + NM
T = 2176
NDEV = 8
EPS = 1e-6
GRID_W = 64
ROWS = SEQ // GRID_W
NA_HEADS = 8
NA_DH = 64
NA_SCALE = NA_DH ** -0.5
HG_HEADS = 4
HG_C = 16
NCHUNK = L // HG_C
D_FF = 4096
IN_COLS = 6144
NEG = -1e30

ADAM_LR = 0.001
ADAM_B1 = 0.9
ADAM_B2 = 0.999
ADAM_EPS = 1e-08
ADAM_WD = 0.01
ADAM_STEP = 10

MESH_ID = pl.DeviceIdType.MESH
ANY = pl.BlockSpec(memory_space=pl.ANY)

NN = (((1,), (0,)), ((), ()))
NT = (((1,), (1,)), ((), ()))
TN = (((0,), (0,)), ((), ()))


def _cp(sem=None, vmem_mb=48):
    return pltpu.CompilerParams(dimension_semantics=sem, vmem_limit_bytes=vmem_mb * 1024 * 1024)


def _dot(a, b, dims=NN):
    return lax.dot_general(a, b, dims, preferred_element_type=F32)


def _sds(shape, dtype):
    return jax.ShapeDtypeStruct(shape, dtype)


HBM = pl.BlockSpec(memory_space=pltpu.HBM)
SEM = pl.BlockSpec(memory_space=pltpu.SEMAPHORE)
EFFECT = pltpu.SideEffectType.DATAFLOW_SIDE_EFFECTING


def _exchange(arrs, scatter, name, after=()):
    n = len(arrs)
    after = list(after)
    out_shapes = []
    for a, sc in zip(arrs, scatter):
        out_shapes.append(_sds(a.shape if sc else (NDEV,) + a.shape, a.dtype))

    def body(*refs):
        ins, outs = refs[:n], refs[n + len(after):2 * n + len(after)]
        send_sems, recv_sems, loc_sems = refs[2 * n + len(after):]
        me = 4 * lax.axis_index("x") + 2 * lax.axis_index("y") + lax.axis_index("c")
        copies = []
        for k in range(n):
            src_me = ins[k].at[me] if scatter[k] else ins[k]
            loc = pltpu.make_async_copy(src_me, outs[k].at[me], loc_sems.at[k])
            loc.start()
            copies.append(loc)
        remote = sum(_peer_copies(ins, outs, scatter, send_sems, recv_sems), [])
        for cp in remote:
            cp.start()
        for cp in remote:
            cp.wait_recv()
        for cp in remote:
            cp.wait_send()
        for cp in copies:
            cp.wait()

    return pl.pallas_call(
        body, name=name, out_shape=tuple(out_shapes), in_specs=[ANY] * (n + len(after)), out_specs=tuple([ANY] * n),
        scratch_shapes=[pltpu.SemaphoreType.DMA((n * (NDEV - 1),)), pltpu.SemaphoreType.DMA((n * (NDEV - 1),)),
                        pltpu.SemaphoreType.DMA((n,))],
    )(*arrs, *after)


def _forward_to_sibling(bufs, name):
    n = len(bufs)

    def body(*refs):
        ins, outs = refs[:n], refs[n:2 * n]
        send_sems, recv_sems = refs[2 * n:]
        x, y, c = lax.axis_index("x"), lax.axis_index("y"), lax.axis_index("c")
        copies = []
        for k in range(n):
            for j, (cx, cy) in enumerate(((1 - x, y), (x, 1 - y), (1 - x, 1 - y))):
                slot = 4 * cx + 2 * cy + c
                copies.append(pltpu.make_async_remote_copy(
                    src_ref=ins[k].at[slot], dst_ref=outs[k].at[slot], send_sem=send_sems.at[3 * k + j],
                    recv_sem=recv_sems.at[3 * k + j], device_id=(x, y, 1 - c), device_id_type=MESH_ID))
        for cp in copies:
            cp.start()
        for cp in copies:
            cp.wait_recv()
        for cp in copies:
            cp.wait_send()

    return pl.pallas_call(
        body, name=name, out_shape=tuple(_sds(b.shape, b.dtype) for b in bufs), in_specs=[ANY] * n,
        out_specs=tuple([ANY] * n), input_output_aliases={k: k for k in range(n)},
        scratch_shapes=[pltpu.SemaphoreType.DMA((3 * n,)), pltpu.SemaphoreType.DMA((3 * n,))],
    )(*bufs)


ALL_PEERS = tuple(range(1, NDEV))
SAME_CORE_AND_SIBLING = (1, 2, 4, 6)


def _peer_copies(srcs, lands, scatter, send_sems, recv_sems, masks=ALL_PEERS):
    x, y, c = lax.axis_index("x"), lax.axis_index("y"), lax.axis_index("c")
    me = 4 * x + 2 * y + c
    out = []
    for k in range(len(srcs)):
        out.append([])
        for m in (masks[k] if isinstance(masks[0], tuple) else masks):
            px, py, pc = x ^ (m >> 2), y ^ ((m >> 1) & 1), c ^ (m & 1)
            src = srcs[k].at[4 * px + 2 * py + pc] if scatter[k] else srcs[k]
            out[k].append(pltpu.make_async_remote_copy(
                src_ref=src, dst_ref=lands[k].at[me], send_sem=send_sems.at[k * (NDEV - 1) + m - 1],
                recv_sem=recv_sems.at[k * (NDEV - 1) + m - 1],
                device_id=(px, py, pc), device_id_type=MESH_ID))
    return out


def _exchange_start(arrs, scatter, name, masks=ALL_PEERS):
    n = len(arrs)
    me = 4 * lax.axis_index("x") + 2 * lax.axis_index("y") + lax.axis_index("c")
    lands = []
    for a, sc in zip(arrs, scatter):
        own = lax.dynamic_index_in_dim(a, me, 0, keepdims=True) if sc else a[None]
        shape = a.shape if sc else (NDEV,) + a.shape
        lands.append(lax.dynamic_update_index_in_dim(lax.empty(shape, a.dtype), own, me, 0))

    def body(*refs):
        srcs, lnds = refs[:n], refs[n:2 * n]
        send_sems, recv_sems = refs[2 * n], refs[2 * n + 1]
        token = refs[-1]
        for cp in sum(_peer_copies(srcs, lnds, scatter, send_sems, recv_sems, masks), []):
            cp.start()
        token[...] = jnp.zeros_like(token)

    ops = [pltpu.with_memory_space_constraint(a, pltpu.HBM) for a in list(arrs) + lands]
    res = pl.pallas_call(
        body, name=name,
        out_shape=(pltpu.SemaphoreType.DMA((n * (NDEV - 1),)), pltpu.SemaphoreType.DMA((n * (NDEV - 1),)))
        + tuple(pltpu.HBM(o.shape, o.dtype) for o in ops) + (_sds((8, 128), F32),),
        in_specs=[HBM] * (2 * n), out_specs=(SEM, SEM) + (HBM,) * (2 * n) + (pl.BlockSpec(memory_space=pltpu.VMEM),),
        input_output_aliases={k: 2 + k for k in range(2 * n)},
        compiler_params=pltpu.CompilerParams(has_side_effects=EFFECT),
    )(*ops)
    return res[:-1], res[-1]


def _exchange_wait(handle, scatter, after, name, masks=ALL_PEERS, which=None):
    send_sems, recv_sems = handle[0], handle[1]
    bufs = handle[2:]
    n = len(bufs) // 2
    after = list(after)

    def body(*refs):
        srcs, lnds = refs[:n], refs[n:2 * n]
        copies = _peer_copies(srcs, lnds, scatter, refs[2 * n], refs[2 * n + 1], masks)
        for k in (range(n) if which is None else which):
            for cp in copies[k]:
                cp.wait_send()
                cp.wait_recv()

    res = pl.pallas_call(
        body, name=name, out_shape=tuple(pltpu.HBM(b.shape, b.dtype) for b in bufs),
        in_specs=[HBM] * (2 * n) + [SEM, SEM] + [ANY] * len(after), out_specs=(HBM,) * (2 * n),
        input_output_aliases={k: k for k in range(2 * n)},
        compiler_params=pltpu.CompilerParams(has_side_effects=EFFECT),
    )(*bufs, send_sems, recv_sems, *after)
    return res[n:] if which is None else (res[n:], (send_sems, recv_sems) + tuple(res))


def _tie(x, token, name):
    def body(x_ref, t_ref, o_ref):
        del x_ref, t_ref, o_ref

    return pl.pallas_call(body, name=name, out_shape=_sds(x.shape, x.dtype), in_specs=[ANY, ANY], out_specs=ANY,
                          input_output_aliases={0: 0})(x, token)


TM_E = 272


def _norm_fwd_t(h, g, name):
    def body(h_ref, g_ref, o_ref, ot_ref):
        xv = h_ref[...]
        r = lax.rsqrt(jnp.mean(xv * xv, axis=-1, keepdims=True) + EPS)
        y = xv * r * g_ref[...]
        o_ref[...] = y.astype(BF16)
        ot_ref[...] = y.T.astype(BF16)

    return pl.pallas_call(
        body, name=name, grid=(T // 128,),
        in_specs=[pl.BlockSpec((128, D), lambda i: (i, 0)), pl.BlockSpec((1, D), lambda i: (0, 0))],
        out_specs=(pl.BlockSpec((128, D), lambda i: (i, 0)), pl.BlockSpec((D, 128), lambda i: (0, i))),
        out_shape=(_sds((T, D), BF16), _sds((D, T), BF16)), compiler_params=_cp(("parallel",)))(h, g)


def _norm_bwd_rows(xv, gv, dnv, dres):
    r = lax.rsqrt(jnp.mean(xv * xv, axis=-1, keepdims=True) + EPS)
    xh = xv * r
    dxh = dnv * gv
    dx = dres + r * (dxh - xh * jnp.mean(dxh * xh, axis=-1, keepdims=True))
    return dx, jnp.sum(dnv * xh, axis=0, keepdims=True)


TM_MM = 1088


def _inproj_fwd(a, w_g):
    nb = w_g.shape[2]

    def body(a_ref, w_ref, o_ref):
        o_ref[...] = _dot(a_ref[...], w_ref[0])

    return pl.pallas_call(
        body, name="inproj_fwd", grid=(T // TM_MM, NDEV),
        in_specs=[pl.BlockSpec((TM_MM, D), lambda i, j: (i, 0)), pl.BlockSpec((1, D, nb), lambda i, j: (j, 0, 0))],
        out_specs=pl.BlockSpec((TM_MM, nb), lambda i, j: (i, j)), out_shape=_sds((T, NDEV * nb), F32),
        compiler_params=_cp(("parallel", "parallel")))(a, w_g)


TM_B = 544


W_IN_B = IN_COLS // NDEV


NA_BLKS = 1536 // W_IN_B


def _dp_specs(rows, row_index):
    return [pl.BlockSpec((rows, W_IN_B), lambda *g: (row_index(*g), jnp.minimum(g[-1], NA_BLKS - 1))),
            pl.BlockSpec((rows, W_IN_B), lambda *g: (row_index(*g), jnp.maximum(g[-1] - NA_BLKS, 0)))]


def _inproj_bwd_dw(a_t, dp_na, dp_rest):
    def body(at_ref, na_ref, rest_ref, dw_ref):
        j = pl.program_id(0)

        @pl.when(j < NA_BLKS)
        def _():
            dw_ref[0] = _dot(at_ref[...], na_ref[...]).astype(BF16)

        @pl.when(j >= NA_BLKS)
        def _():
            dw_ref[0] = _dot(at_ref[...], rest_ref[...]).astype(BF16)

    return pl.pallas_call(
        body, name="inproj_bwd_dw", grid=(NDEV,),
        in_specs=[pl.BlockSpec((D, T), lambda j: (0, 0))] + _dp_specs(T, lambda j: 0),
        out_specs=pl.BlockSpec((1, D, W_IN_B), lambda j: (j, 0, 0)), out_shape=_sds((NDEV, D, W_IN_B), BF16),
        compiler_params=_cp(("parallel",)))(a_t, dp_na, dp_rest)


def _inproj_bwd_da(dp_na, dp_rest, w_g, h0, g_mix, dh1):
    nsub = TM_MM // TM_E

    def body(na_ref, rest_ref, w_ref, h0_ref, g_ref, dres_ref, dh0_ref, dg_ref, da):
        i, j = pl.program_id(0), pl.program_id(1)
        dpv = jnp.where(j < NA_BLKS, na_ref[...], rest_ref[...])
        dav = _dot(dpv, w_ref[0], NT)

        @pl.when(j == 0)
        def _():
            da[...] = dav

        @pl.when(j > 0)
        def _():
            da[...] += dav

        @pl.when(j == NDEV - 1)
        def _():
            gsum = jnp.zeros((1, D), F32)
            for s in range(nsub):
                sub = slice(s * TM_E, (s + 1) * TM_E)
                dx, gpart = _norm_bwd_rows(h0_ref[sub, :], g_ref[...], da[sub, :], dres_ref[sub, :])
                dh0_ref[sub, :] = dx
                gsum = gsum + gpart

            @pl.when(i == 0)
            def _():
                dg_ref[...] = gsum

            @pl.when(i > 0)
            def _():
                dg_ref[...] += gsum

    rblk = pl.BlockSpec((TM_MM, D), lambda i, j: (i, 0))
    vec = pl.BlockSpec((1, D), lambda i, j: (0, 0))
    return pl.pallas_call(
        body, name="inproj_bwd_da", grid=(T // TM_MM, NDEV),
        in_specs=_dp_specs(TM_MM, lambda i, j: i) + [pl.BlockSpec((1, D, W_IN_B), lambda i, j: (j, 0, 0)), rblk, vec, rblk],
        out_specs=(rblk, vec), out_shape=(_sds((T, D), F32), _sds((1, D), F32)),
        scratch_shapes=[pltpu.VMEM((TM_MM, D), F32)],
        compiler_params=_cp(("arbitrary", "arbitrary"), 56))(dp_na, dp_rest, w_g, h0, g_mix, dh1)


NA_QB = 256
NA_GROUPS = ROWS // 4
NA_UROWS = 11
NA_KW = NA_UROWS * GRID_W
NA_KU = 768


def _na_row_offset(var, i, j):
    valid = (j < 8, i <= j < i + 8, 3 <= j < NA_UROWS)[var]
    return (j - i + (7, 3, 0)[var]) if valid else None


def _na_bias_table(rp):
    def body(r_ref, o_ref):
        row3 = lax.broadcasted_iota(jnp.int32, (15, GRID_W, 128), 1)
        lane3 = lax.broadcasted_iota(jnp.int32, (15, GRID_W, 128), 2)
        w3 = lane3 & (GRID_W - 1)
        cs3 = jnp.clip(row3 - 8, 0, GRID_W - 16)
        lane = lax.broadcasted_iota(jnp.int32, (GRID_W, 128), 1)
        neg = jnp.full((GRID_W, 128), NEG, F32)
        z = jnp.stack([jnp.broadcast_to(r_ref[0, a:a + 1, :], (GRID_W, 128)) for a in range(15)])
        for bit in range(6):
            sh = 1 << bit
            z = jnp.where((row3 & sh) != 0, jnp.roll(z, sh, axis=2), z)
        z = jnp.roll(z, 128 - 15, axis=2)
        z = jnp.where(lane3 < GRID_W, z, 0.0)
        z = z + jnp.roll(z, GRID_W, axis=2)
        tabs = jnp.where((w3 >= cs3) & (w3 < cs3 + 16), z, NEG)
        tail = jnp.where(lane < GRID_W + NM, 0.0, NEG)
        for var in range(3):
            for i in range(4):
                for jp in range(NA_KU // 128):
                    halves = []
                    for j in (2 * jp, 2 * jp + 1):
                        a = _na_row_offset(var, i, j) if j < NA_UROWS else None
                        halves.append(tail if j >= NA_UROWS else (neg if a is None else tabs[a]))
                    o_ref[var, 0, i * 64:(i + 1) * 64, jp * 128:(jp + 1) * 128] = jnp.where(lane < GRID_W, halves[0], halves[1])

    return pl.pallas_call(
        body, name="na_bias_table", grid=(NA_HEADS,),
        in_specs=[pl.BlockSpec((1, 15, 128), lambda h: (h, 0, 0))],
        out_specs=pl.BlockSpec((3, 1, NA_QB, NA_KU), lambda h: (0, h, 0, 0)),
        out_shape=_sds((3, NA_HEADS, NA_QB, NA_KU), F32), compiler_params=_cp(("parallel",)))(rp)


def _na_var(g):
    return jnp.where(g == 0, 0, jnp.where(g == NA_GROUPS - 1, 2, 1))


def _na_load_window(src_ref, dst, g):
    us = jnp.clip(4 * g - 4, 0, ROWS - NA_UROWS)
    kstart = pl.multiple_of(NM + GRID_W * us, 16)
    dst[0:NA_KW, :] = src_ref[pl.ds(kstart, NA_KW), :].astype(BF16)
    dst[NA_KW:NA_KW + NM, :] = src_ref[0:NM, :].astype(BF16)
    dst[NA_KW + NM:, :] = jnp.zeros((NA_KU - NA_KW - NM, 128), BF16)
    return kstart


def _na_fwd(p_act, bias_tab):
    def body(q_ref, k_ref, v_ref, b_ref, o_ref, lse_ref, ku, vu):
        g = pl.program_id(1)
        _na_load_window(k_ref, ku, g)
        _na_load_window(v_ref, vu, g)
        qstart = pl.multiple_of(NM + NA_QB * g, 16)
        q = q_ref[pl.ds(qstart, NA_QB), :]
        lane = lax.broadcasted_iota(jnp.int32, (NA_QB, 128), 1)
        o_h, lse_h = [], []
        for h in range(2):
            hm = (lane < 64) if h == 0 else (lane >= 64)
            qm = (jnp.where(hm, q, 0.0) * NA_SCALE).astype(BF16)
            s = _dot(qm, ku[...], NT) + b_ref[0, h]
            m = jnp.max(s, axis=-1, keepdims=True)
            p = jnp.exp(s - m)
            l = jnp.sum(p, axis=-1, keepdims=True)
            o_h.append(_dot(p.astype(BF16), vu[...]) / l)
            lse_h.append(jnp.broadcast_to(m + jnp.log(l), (NA_QB, 128)))
        o_ref[pl.ds(qstart, NA_QB), :] = jnp.where(lane < 64, o_h[0], o_h[1]).astype(BF16)
        lse_ref[0, pl.ds(qstart, NA_QB), :] = jnp.where(lane < 64, lse_h[0], lse_h[1])

        @pl.when(g == 0)
        def _():
            qm_ = q_ref[0:NM, :]
            lane_m = lax.broadcasted_iota(jnp.int32, (NM, 128), 1)
            km, vm = ku[NA_KW:NA_KW + NM, :], vu[NA_KW:NA_KW + NM, :]
            om = []
            for h in range(2):
                hm = (lane_m < 64) if h == 0 else (lane_m >= 64)
                s = _dot(jnp.where(hm, qm_, 0.0).astype(BF16), km, NT) * NA_SCALE
                p = jnp.exp(s - jnp.max(s, axis=-1, keepdims=True))
                l = jnp.sum(p, axis=-1, keepdims=True)
                om.append(_dot(p.astype(BF16), vm) / l)
            o_ref[0:NM, :] = jnp.where(lane_m < 64, om[0], om[1]).astype(BF16)
            o_ref[L:T, :] = jnp.zeros((T - L, 128), BF16)
            lse_ref[0, 0:NM, :] = jnp.zeros((NM, 128), F32)
            lse_ref[0, L:T, :] = jnp.zeros((T - L, 128), F32)

    col = lambda off: pl.BlockSpec((T, 128), lambda hp, g: (0, off + hp))
    return pl.pallas_call(
        body, name="na_fwd", grid=(4, NA_GROUPS),
        in_specs=[col(0), col(4), col(8),
                  pl.BlockSpec((1, 2, NA_QB, NA_KU), lambda hp, g: (_na_var(g), hp, 0, 0))],
        out_specs=(pl.BlockSpec((T, 128), lambda hp, g: (0, hp)), pl.BlockSpec((1, T, 128), lambda hp, g: (hp, 0, 0))),
        out_shape=(_sds((T, 512), BF16), _sds((4, T, 128), F32)),
        scratch_shapes=[pltpu.VMEM((NA_KU, 128), BF16), pltpu.VMEM((NA_KU, 128), BF16)],
        compiler_params=_cp(("parallel", "arbitrary")))(p_act, p_act, p_act, bias_tab)


def _na_bwd(p_act, do, lse, bias_tab):
    def body(q_ref, k_ref, v_ref, do_ref, lse_ref, b_ref, dq_ref, dk_ref, dv_ref, db_ref, ku, vu):
        g = pl.program_id(1)

        @pl.when(g == 0)
        def _():
            dq_ref[...] = jnp.zeros((T, 128), F32)
            dk_ref[...] = jnp.zeros((T, 128), F32)
            dv_ref[...] = jnp.zeros((T, 128), F32)

        kstart = _na_load_window(k_ref, ku, g)
        _na_load_window(v_ref, vu, g)
        qstart = pl.multiple_of(NM + NA_QB * g, 16)
        q = q_ref[pl.ds(qstart, NA_QB), :]
        dov = do_ref[pl.ds(qstart, NA_QB), :]
        lsev = lse_ref[0, pl.ds(qstart, NA_QB), :]
        lane = lax.broadcasted_iota(jnp.int32, (NA_QB, 128), 1)
        first = (g == 0) | (g == 1) | (g == NA_GROUPS - 1)
        dq_h = []
        dku = jnp.zeros((NA_KU, 128), F32)
        dvu = jnp.zeros((NA_KU, 128), F32)
        for h in range(2):
            hm = (lane < 64) if h == 0 else (lane >= 64)
            qm = (jnp.where(hm, q, 0.0) * NA_SCALE).astype(BF16)
            dom = jnp.where(hm, dov, 0.0).astype(BF16)
            s = _dot(qm, ku[...], NT) + b_ref[0, h]
            p = jnp.exp(s - lsev[:, 64 * h:64 * h + 1])
            dp = _dot(dom, vu[...], NT)
            delta = jnp.sum(p * dp, axis=-1, keepdims=True)
            ds = p * (dp - delta)

            @pl.when(first)
            def _():
                db_ref[0, h] = ds

            @pl.when(jnp.logical_not(first))
            def _():
                db_ref[0, h] += ds

            dsb = ds.astype(BF16)
            dq_h.append(_dot(dsb, ku[...]) * NA_SCALE)
            dku = dku + _dot(dsb, qm, TN)
            dvu = dvu + _dot(p.astype(BF16), dom, TN)
        dq_ref[pl.ds(qstart, NA_QB), :] = jnp.where(lane < 64, dq_h[0], dq_h[1])
        dk_ref[pl.ds(kstart, NA_KW), :] += dku[0:NA_KW]
        dv_ref[pl.ds(kstart, NA_KW), :] += dvu[0:NA_KW]
        dk_ref[0:NM, :] += dku[NA_KW:NA_KW + NM]
        dv_ref[0:NM, :] += dvu[NA_KW:NA_KW + NM]

        @pl.when(g == 0)
        def _():
            qm_ = q_ref[0:NM, :]
            dom_ = do_ref[0:NM, :]
            lane_m = lax.broadcasted_iota(jnp.int32, (NM, 128), 1)
            km, vm = ku[NA_KW:NA_KW + NM, :], vu[NA_KW:NA_KW + NM, :]
            dqs = []
            dkm = jnp.zeros((NM, 128), F32)
            dvm = jnp.zeros((NM, 128), F32)
            for h in range(2):
                hm = (lane_m < 64) if h == 0 else (lane_m >= 64)
                qh = jnp.where(hm, qm_, 0.0).astype(BF16)
                doh = jnp.where(hm, dom_, 0.0).astype(BF16)
                s = _dot(qh, km, NT) * NA_SCALE
                e = jnp.exp(s - jnp.max(s, axis=-1, keepdims=True))
                p = e / jnp.sum(e, axis=-1, keepdims=True)
                dp = _dot(doh, vm, NT)
                ds = p * (dp - jnp.sum(p * dp, axis=-1, keepdims=True))
                dsb = (ds * NA_SCALE).astype(BF16)
                dqs.append(_dot(dsb, km))
                dkm = dkm + _dot(dsb, qh, TN)
                dvm = dvm + _dot(p.astype(BF16), doh, TN)
            dq_ref[0:NM, :] = jnp.where(lane_m < 64, dqs[0], dqs[1])
            dk_ref[0:NM, :] += dkm
            dv_ref[0:NM, :] += dvm

    col = lambda off: pl.BlockSpec((T, 128), lambda hp, g: (0, off + hp))
    ocol = pl.BlockSpec((T, 128), lambda hp, g: (0, hp))
    bspec = pl.BlockSpec((1, 2, NA_QB, NA_KU), lambda hp, g: (_na_var(g), hp, 0, 0))
    return pl.pallas_call(
        body, name="na_bwd", grid=(4, NA_GROUPS),
        in_specs=[col(0), col(4), col(8), ocol, pl.BlockSpec((1, T, 128), lambda hp, g: (hp, 0, 0)), bspec],
        out_specs=(ocol, ocol, ocol, bspec),
        out_shape=(_sds((T, 512), F32), _sds((T, 512), F32), _sds((T, 512), F32), _sds((3, NA_HEADS, NA_QB, NA_KU), F32)),
        scratch_shapes=[pltpu.VMEM((NA_KU, 128), BF16), pltpu.VMEM((NA_KU, 128), BF16)],
        compiler_params=_cp(("parallel", "arbitrary")))(p_act, p_act, p_act, do, lse, bias_tab)


def _na_rpb_reduce(dbias):
    def body(db_ref, o_ref):
        lane = lax.broadcasted_iota(jnp.int32, (GRID_W, 128), 1)
        row3 = lax.broadcasted_iota(jnp.int32, (15, GRID_W, 128), 1)
        lane3 = lax.broadcasted_iota(jnp.int32, (15, GRID_W, 128), 2)
        accs = []
        for a in range(15):
            acc = jnp.zeros((GRID_W, 128), F32)
            for var in range(3):
                for i in range(4):
                    for j in range(NA_UROWS):
                        if _na_row_offset(var, i, j) == a:
                            pair = db_ref[var, 0, i * 64:(i + 1) * 64, (j // 2) * 128:(j // 2 + 1) * 128]
                            acc = acc + jnp.where((lane < GRID_W) if j % 2 == 0 else (lane >= GRID_W), pair, 0.0)
            accs.append(acc)
        z = jnp.stack(accs)
        z = jnp.where(lane3 < GRID_W, z + jnp.roll(z, GRID_W, axis=2), 0.0)
        for bit in range(6):
            sh = 1 << bit
            z = jnp.where((row3 & sh) != 0, jnp.roll(z, 128 - sh, axis=2), z)
        z = jnp.roll(z, 15, axis=2)
        o_ref[0] = jnp.sum(z, axis=1)

    return pl.pallas_call(
        body, name="na_rpb_reduce", grid=(NA_HEADS,),
        in_specs=[pl.BlockSpec((3, 1, NA_QB, NA_KU), lambda h: (0, h, 0, 0))],
        out_specs=pl.BlockSpec((1, 15, 128), lambda h: (h, 0, 0)), out_shape=_sds((NA_HEADS, 15, 128), F32),
        compiler_params=_cp(("parallel",)))(dbias)


HG_RB = 128
HG_NB = T // HG_RB
HG_SLOTS = HG_NB * 8
HI = lax.Precision.HIGHEST
HG_UNROLL = 4


def _chunk_tri(lower):
    r = lax.broadcasted_iota(jnp.int32, (HG_RB, HG_RB), 0)
    c = lax.broadcasted_iota(jnp.int32, (HG_RB, HG_RB), 1)
    same = (r // HG_C) == (c // HG_C)
    keep = (c <= r) if lower else (c >= r)
    return jnp.where(same & keep, 1.0, 0.0).astype(F32)


def _hg_gate_terms(z, lg):
    dl = lg[0:1, :] - lg[1:2, :]
    log_lb = jax.nn.log_sigmoid(dl)
    log_1mlb = jax.nn.log_sigmoid(-dl)
    yz = log_1mlb + jax.nn.log_sigmoid(z)
    log_f = jnp.logaddexp(log_lb, yz)
    snz = jax.nn.sigmoid(-z)
    k = jnp.exp(log_1mlb) * snz
    w2 = jnp.exp(yz - log_f)
    return log_f, k, snz, w2


def _hg_pre(p_act, logits):
    def body(q_ref, zf_ref, zb_ref, lg_ref, qh_ref, kf_ref, bf_ref, kb_ref, bb_ref):
        qh_ref[...] = jax.nn.silu(q_ref[...])
        lf, kf, _, _ = _hg_gate_terms(zf_ref[...], lg_ref[0])
        kf_ref[...] = kf
        bf_ref[...] = jnp.dot(_chunk_tri(True), lf, precision=HI, preferred_element_type=F32)
        lb_, kb, _, _ = _hg_gate_terms(zb_ref[...], lg_ref[1])
        kb_ref[...] = kb
        bb_ref[...] = jnp.dot(_chunk_tri(False), lb_, precision=HI, preferred_element_type=F32)

    blk = lambda c: pl.BlockSpec((HG_RB, 512), lambda i: (i, c))
    ob = pl.BlockSpec((HG_RB, 512), lambda i: (i, 0))
    return pl.pallas_call(
        body, name="hg_pre", grid=(HG_NB,),
        in_specs=[blk(3), blk(4), blk(5), pl.BlockSpec((2, 2, 512), lambda i: (0, 0, 0))],
        out_specs=(ob,) * 5, out_shape=(_sds((T, 512), F32),) * 5,
        compiler_params=_cp(("parallel",)))(p_act, p_act, p_act, logits)


def _bdot(a, b, ca, cb):
    return lax.dot_general(a.astype(BF16), b.astype(BF16), (((ca,), (cb,)), ((0,), (0,))), preferred_element_type=F32)


HG_S = 8
HG_NS = HG_RB // HG_S


def _lane_sums(xs):
    l_io = lax.broadcasted_iota(jnp.int32, (HG_NS, HG_S, HG_S), 2)
    a = jnp.zeros((HG_NS, HG_S, HG_S), F32)
    for j, x in enumerate(xs):
        a = a + jnp.where(l_io == j, jnp.sum(x, axis=-1, keepdims=True), 0.0)
    return a


def _halves(x):
    y = x.reshape(8, 2, HG_S, x.shape[-1])
    return y[:, 0], y[:, 1]


def _join(first, second):
    return jnp.stack([first, second], axis=1).reshape(HG_RB, first.shape[-1])


def _cross_split(rev, b4):
    b_1, b_2 = _halves(b4)
    if rev:
        r = b_2[:, 0:1, :]
        return jnp.exp(b_1 - r), jnp.exp(r - b_2)
    r = b_1[:, HG_S - 1:HG_S, :]
    return jnp.exp(b_2 - r), jnp.exp(r - b_1)


def _hg_scan_fwd(qh, k, b, p_act, rev):
    anchor = 0 if rev else HG_C - 1

    def body(q_ref, k_ref, b_ref, v_ref, o_ref, st_ref, dsc):
        def phase_a(blk, _):
            rows = pl.ds(pl.multiple_of(blk * HG_RB, HG_RB), HG_RB)
            b3 = b_ref[rows, :].reshape(8, HG_C, 128)
            k3 = k_ref[rows, :].reshape(8, HG_C, 128)
            v3 = v_ref[rows, :].reshape(8, HG_C, 128)
            bl = b3[:, anchor:anchor + 1, :]
            kt = k3 * jnp.exp(bl - b3)
            st_ref[0, pl.ds(pl.multiple_of(blk * 8, 8), 8)] = _bdot(v3, kt, 1, 1)
            dsc[pl.ds(pl.multiple_of(blk * 8, 8), 8), :] = jnp.exp(bl[:, 0, :])
            return 0

        lax.fori_loop(0, HG_NB, phase_a, 0, unroll=HG_UNROLL)

        def phase_b(n, carry):
            c = (NCHUNK - 1 - n) if rev else n
            u = st_ref[0, c]
            st_ref[0, c] = carry
            return carry * dsc[pl.ds(c, 1), :] + u

        lax.fori_loop(0, NCHUNK // 3, lambda n3, s: phase_b(3 * n3 + 2, phase_b(3 * n3 + 1, phase_b(3 * n3, s))),
                      jnp.zeros((128, 128), F32))
        for c in range(NCHUNK, HG_SLOTS):
            st_ref[0, c] = jnp.zeros((128, 128), F32)

        t_io = lax.broadcasted_iota(jnp.int32, (HG_NS, HG_S, 128), 1)

        def phase_c(blk, _):
            rows = pl.ds(pl.multiple_of(blk * HG_RB, HG_RB), HG_RB)
            b4 = b_ref[rows, :].reshape(HG_NS, HG_S, 128)
            k4 = k_ref[rows, :].reshape(HG_NS, HG_S, 128)
            q4 = q_ref[rows, :].reshape(HG_NS, HG_S, 128)
            v4 = v_ref[rows, :].reshape(HG_NS, HG_S, 128)
            st = st_ref[0, pl.ds(pl.multiple_of(blk * 8, 8), 8)]
            o = _bdot((q4 * jnp.exp(b4)).reshape(8, HG_C, 128), st, 2, 2).reshape(HG_RB, 128)
            terms = []
            for s in range(HG_S):
                ok = (t_io <= s) if rev else (t_io >= s)
                f = jnp.exp(jnp.where(ok, b4 - b4[:, s:s + 1, :], NEG))
                terms.append(q4 * f * k4[:, s:s + 1, :])
            o_in = _bdot(_lane_sums(terms), v4, 2, 1)
            wq, wk = _cross_split(rev, b4)
            q_1, q_2 = _halves(q4)
            k_1, k_2 = _halves(k4)
            v_1, v_2 = _halves(v4)
            o_1, o_2 = _halves(o_in)
            if rev:
                o_1 = o_1 + _bdot(_bdot(q_1 * wq, k_2 * wk, 2, 2), v_2, 2, 1)
            else:
                o_2 = o_2 + _bdot(_bdot(q_2 * wq, k_1 * wk, 2, 2), v_1, 2, 1)
            o_ref[rows, :] = o + _join(o_1, o_2)
            return 0

        lax.fori_loop(0, HG_NB, phase_c, 0, unroll=HG_UNROLL)

    col = pl.BlockSpec((T, 128), lambda h: (0, h))
    return pl.pallas_call(
        body, name="hg_scan_bwd_dir" if rev else "hg_scan_fwd_dir", grid=(HG_HEADS,),
        in_specs=[col, col, col, pl.BlockSpec((T, 128), lambda h: (0, 24 + h))],
        out_specs=(col, pl.BlockSpec((1, HG_SLOTS, 128, 128), lambda h: (h, 0, 0, 0))),
        out_shape=(_sds((T, 512), F32), _sds((HG_HEADS, HG_SLOTS, 128, 128), F32)),
        scratch_shapes=[pltpu.VMEM((HG_SLOTS, 128), F32)],
        compiler_params=_cp(("parallel",), 56))(qh, k, b, p_act)


def _hg_scan_bwd(qh, k, b, p_act, st, do, rev):
    anchor = 0 if rev else HG_C - 1

    def body(q_ref, k_ref, b_ref, v_ref, st_ref, do_ref, dq_ref, dk_ref, db_ref, dv_ref, gst, dsc, dbl):
        def phase_a(blk, _):
            rows = pl.ds(pl.multiple_of(blk * HG_RB, HG_RB), HG_RB)
            b3 = b_ref[rows, :].reshape(8, HG_C, 128)
            q3 = q_ref[rows, :].reshape(8, HG_C, 128)
            do3 = do_ref[rows, :].reshape(8, HG_C, 128)
            gst[pl.ds(pl.multiple_of(blk * 8, 8), 8)] = _bdot(do3, q3 * jnp.exp(b3), 1, 1)
            dsc[pl.ds(pl.multiple_of(blk * 8, 8), 8), :] = jnp.exp(b3[:, anchor, :])
            return 0

        lax.fori_loop(0, HG_NB, phase_a, 0, unroll=HG_UNROLL)

        def phase_b(n, carry):
            c = n if rev else (NCHUNK - 1 - n)
            w = gst[c]
            gst[c] = carry
            dcv = dsc[pl.ds(c, 1), :]
            dbl[pl.ds(c, 1), :] = dcv * jnp.sum(st_ref[0, c] * carry, axis=0, keepdims=True)
            return carry * dcv + w

        lax.fori_loop(0, NCHUNK // 3, lambda n3, s: phase_b(3 * n3 + 2, phase_b(3 * n3 + 1, phase_b(3 * n3, s))),
                      jnp.zeros((128, 128), F32))
        for c in range(NCHUNK, HG_SLOTS):
            gst[c] = jnp.zeros((128, 128), F32)
            dbl[c:c + 1, :] = jnp.zeros((1, 128), F32)

        t_io = lax.broadcasted_iota(jnp.int32, (HG_NS, HG_S, 128), 1)
        t16 = lax.broadcasted_iota(jnp.int32, (8, HG_C, 128), 1)
        r_io = lax.broadcasted_iota(jnp.int32, (HG_NS, HG_S, HG_S), 1)
        l_io = lax.broadcasted_iota(jnp.int32, (HG_NS, HG_S, HG_S), 2)

        def phase_c(blk, _):
            rows = pl.ds(pl.multiple_of(blk * HG_RB, HG_RB), HG_RB)
            cs = pl.ds(pl.multiple_of(blk * 8, 8), 8)
            b4 = b_ref[rows, :].reshape(HG_NS, HG_S, 128)
            k4 = k_ref[rows, :].reshape(HG_NS, HG_S, 128)
            q4 = q_ref[rows, :].reshape(HG_NS, HG_S, 128)
            v4 = v_ref[rows, :].reshape(HG_NS, HG_S, 128)
            do4 = do_ref[rows, :].reshape(HG_NS, HG_S, 128)
            b3, k3, q3 = (z.reshape(8, HG_C, 128) for z in (b4, k4, q4))
            v3, do3 = v4.reshape(8, HG_C, 128), do4.reshape(8, HG_C, 128)
            s_t = st_ref[0, cs]
            g_t = gst[cs]
            bl = b3[:, anchor:anchor + 1, :]
            ekl = jnp.exp(bl - b3)
            kt = k3 * ekl
            dkt = _bdot(v3, g_t, 2, 1)
            dq = (_bdot(do3, s_t, 2, 1) * jnp.exp(b3)).reshape(HG_NS, HG_S, 128)
            dk = (dkt * ekl).reshape(HG_NS, HG_S, 128)
            dv = _bdot(kt, g_t, 2, 2).reshape(HG_NS, HG_S, 128)
            dbl3 = dbl[cs, :].reshape(8, 1, 128) + jnp.sum(dkt * kt, axis=1, keepdims=True)
            causal = (l_io >= r_io) if rev else (l_io <= r_io)
            da = jnp.where(causal, _bdot(do4, v4, 2, 2), 0.0)
            causal_t = (l_io <= r_io) if rev else (l_io >= r_io)
            dat = jnp.where(causal_t, _bdot(v4, do4, 2, 2), 0.0)
            for s in range(HG_S):
                ok = (t_io <= s) if rev else (t_io >= s)
                f = jnp.exp(jnp.where(ok, b4 - b4[:, s:s + 1, :], NEG))
                dq = dq + da[:, :, s:s + 1] * (f * k4[:, s:s + 1, :])
            terms = []
            for t in range(HG_S):
                ok = (t_io >= t) if rev else (t_io <= t)
                e = jnp.exp(jnp.where(ok, b4[:, t:t + 1, :] - b4, NEG))
                eq = e * q4[:, t:t + 1, :]
                dk = dk + dat[:, :, t:t + 1] * eq
                terms.append(eq * k4)
            dv = dv + _bdot(_lane_sums(terms), do4, 2, 1)
            wq, wk = _cross_split(rev, b4)
            pick = (lambda z: _halves(z)) if rev else (lambda z: _halves(z)[::-1])
            (q_q, _), (_, k_k), (_, v_k), (do_q, _) = pick(q4), pick(k4), pick(v4), pick(do4)
            qx, kx = q_q * wq, k_k * wk
            dq_q = _bdot(_bdot(do_q, v_k, 2, 2), kx, 2, 1) * wq
            dk_k = _bdot(_bdot(v_k, do_q, 2, 2), qx, 2, 1) * wk
            dv_k = _bdot(_bdot(kx, qx, 2, 2), do_q, 2, 1)
            zero = jnp.zeros((8, HG_S, 128), F32)
            place_q = (lambda z: _join(z, zero)) if rev else (lambda z: _join(zero, z))
            place_k = (lambda z: _join(zero, z)) if rev else (lambda z: _join(z, zero))
            dq2 = dq.reshape(HG_RB, 128) + place_q(dq_q)
            dk2 = dk.reshape(HG_RB, 128) + place_k(dk_k)
            dv2 = dv.reshape(HG_RB, 128) + place_k(dv_k)
            dq3, dk3 = dq2.reshape(8, HG_C, 128), dk2.reshape(8, HG_C, 128)
            db = q3 * dq3 - k3 * dk3 + jnp.where(t16 == anchor, dbl3, 0.0)
            dq_ref[rows, :] = dq2
            dk_ref[rows, :] = dk2
            db_ref[rows, :] = db.reshape(HG_RB, 128)
            dv_ref[rows, :] = dv2
            return 0

        lax.fori_loop(0, HG_NB, phase_c, 0, unroll=HG_UNROLL)

    col = pl.BlockSpec((T, 128), lambda h: (0, h))
    return pl.pallas_call(
        body, name="hg_scan_bwd_dir_bwd" if rev else "hg_scan_fwd_dir_bwd", grid=(HG_HEADS,),
        in_specs=[col, col, col, pl.BlockSpec((T, 128), lambda h: (0, 24 + h)),
                  pl.BlockSpec((1, HG_SLOTS, 128, 128), lambda h: (h, 0, 0, 0)), col],
        out_specs=(col,) * 4, out_shape=(_sds((T, 512), F32),) * 4,
        scratch_shapes=[pltpu.VMEM((HG_SLOTS, 128, 128), F32), pltpu.VMEM((HG_SLOTS, 128), F32),
                        pltpu.VMEM((HG_SLOTS, 128), F32)],
        compiler_params=_cp(("parallel",), 56))(qh, k, b, p_act, st, do)


def _row_valid(i, tm):
    r = lax.broadcasted_iota(jnp.int32, (tm, 1), 0) + i * tm
    return r < L


def _hg_post_rows(o, gv, gain_v, valid):
    parts = []
    for h in range(HG_HEADS):
        oh = o[:, 128 * h:128 * (h + 1)]
        parts.append(oh * lax.rsqrt(jnp.mean(oh * oh, axis=-1, keepdims=True) + EPS))
    return jnp.where(valid, jnp.concatenate(parts, axis=1) * gain_v * jax.nn.silu(gv), 0.0)


def _hg_post_bwd_rows(du, o, gv, gain_v, valid):
    duv = jnp.where(valid, du, 0.0)
    sig = jax.nn.sigmoid(gv)
    sg = gv * sig
    dn = duv * gain_v * sg
    do_parts, n_parts = [], []
    for h in range(HG_HEADS):
        sl = slice(128 * h, 128 * (h + 1))
        oh = o[:, sl]
        r = lax.rsqrt(jnp.mean(oh * oh, axis=-1, keepdims=True) + EPS)
        nh = oh * r
        dnh = dn[:, sl]
        do_parts.append(r * (dnh - nh * jnp.mean(dnh * nh, axis=-1, keepdims=True)))
        n_parts.append(nh)
    n = jnp.where(valid, jnp.concatenate(n_parts, axis=1), 0.0)
    do = jnp.where(valid, jnp.concatenate(do_parts, axis=1), 0.0)
    dg = duv * n * gain_v * (sig * (1.0 + gv * (1.0 - sig)))
    return do, dg, jnp.sum(duv * n * sg, axis=0, keepdims=True)


def _hg_pre_bwd(p_act, logits, dq_f, dq_b, dk_f, dk_b, db_f, db_b, dv_f, dv_b, dp_rest):
    def body(q_ref, zf_ref, zb_ref, lg_ref, dqf_ref, dqb_ref, dkf_ref, dkb_ref, dbf_ref, dbb_ref, dvf_ref, dvb_ref, _,
             dp_ref, dlg_ref):
        dq_ref, dzf_ref, dzb_ref, di_ref = (dp_ref.at[:, 512 * c:512 * (c + 1)] for c in range(4))
        i = pl.program_id(0)
        valid = _row_valid(i, HG_RB)
        qv = q_ref[...]
        sig = jax.nn.sigmoid(qv)
        dq_ref[...] = jnp.where(valid, (dqf_ref[...] + dqb_ref[...]) * (sig * (1.0 + qv * (1.0 - sig))), 0.0).astype(BF16)
        di_ref[...] = jnp.where(valid, dvf_ref[...] + dvb_ref[...], 0.0).astype(BF16)
        for d, (z_ref, dk_r, db_r, dz_ref) in enumerate(((zf_ref, dkf_ref, dbf_ref, dzf_ref), (zb_ref, dkb_ref, dbb_ref, dzb_ref))):
            lg = lg_ref[d]
            dl = lg[0:1, :] - lg[1:2, :]
            lb = jax.nn.sigmoid(dl)
            one_m_lb = jax.nn.sigmoid(-dl)
            log_f, _, snz, w2 = _hg_gate_terms(z_ref[...], lg)
            dbv = jnp.where(valid, db_r[...], 0.0)
            dkv = jnp.where(valid, dk_r[...], 0.0)
            dlf = jnp.dot(_chunk_tri(d == 1), dbv, precision=HI, preferred_element_type=F32)
            sz = 1.0 - snz
            dz_ref[...] = (dlf * w2 * snz - dkv * one_m_lb * sz * snz).astype(BF16)
            dlb = jnp.sum(dlf * snz * jnp.exp(-log_f) - dkv * snz, axis=0, keepdims=True)
            dl0 = dlb * lb * one_m_lb
            part = jnp.concatenate([dl0, -dl0], axis=0)

            @pl.when(i == 0)
            def _():
                dlg_ref[d] = part

            @pl.when(i > 0)
            def _():
                dlg_ref[d] += part

    blk = lambda c: pl.BlockSpec((HG_RB, 512), lambda i: (i, c))
    ob = pl.BlockSpec((HG_RB, 512), lambda i: (i, 0))
    lgs = pl.BlockSpec((2, 2, 512), lambda i: (0, 0, 0))
    return pl.pallas_call(
        body, name="hg_pre_bwd", grid=(HG_NB,),
        in_specs=[blk(3), blk(4), blk(5), lgs] + [ob] * 8 + [ANY],
        out_specs=(pl.BlockSpec((HG_RB, 2048), lambda i: (i, 0)), lgs),
        out_shape=(_sds(dp_rest.shape, BF16), _sds((2, 2, 512), F32)), input_output_aliases={12: 0},
        compiler_params=_cp(("arbitrary",)))(p_act, p_act, p_act, logits, dq_f, dq_b, dk_f, dk_b, db_f, db_b, dv_f, dv_b,
                                             dp_rest)


def _mix_fwd(o_na, o_f, o_b, gain, w_na, w_hg, p_act):
    def body(ona_ref, of_ref, ob_ref, g_ref, gain_ref, wna_ref, whg_ref, gna_ref, ghg_ref, o_ref, u_ref):
        u = _hg_post_rows(of_ref[...] + ob_ref[...], g_ref[...], gain_ref[...], _row_valid(pl.program_id(0), TM_B)).astype(BF16)
        u_ref[...] = u
        y_na = _dot(ona_ref[...], wna_ref[...])
        y_hg = _dot(u, whg_ref[...])
        o_ref[...] = (jax.nn.sigmoid(gna_ref[...]) * y_na + jax.nn.sigmoid(ghg_ref[...]) * y_hg).astype(BF16)

    act = pl.BlockSpec((TM_B, 512), lambda i: (i, 0))
    wsp = pl.BlockSpec((512, D), lambda i: (0, 0))
    return pl.pallas_call(
        body, name="mix_fwd", grid=(T // TM_B,),
        in_specs=[act, act, act, pl.BlockSpec((TM_B, 512), lambda i: (i, 7)), pl.BlockSpec((1, 512), lambda i: (0, 0)),
                  wsp, wsp, pl.BlockSpec((TM_B, D), lambda i: (i, 4)), pl.BlockSpec((TM_B, D), lambda i: (i, 5))],
        out_specs=(pl.BlockSpec((TM_B, D), lambda i: (i, 0)), act), out_shape=(_sds((T, D), BF16), _sds((T, 512), BF16)),
        compiler_params=_cp(("parallel",)))(o_na, o_f, o_b, p_act, gain, w_na, w_hg, p_act, p_act)


DP_REST = IN_COLS - 1536


def _mix_bwd(o_na, u_hg, o_f, o_b, gain, w_na, w_hg, p_act, dmix):
    ni = T // TM_B

    def body(ona_ref, uhg_ref, of_ref, ob_ref, g_ref, gain_ref, wna_ref, whg_ref, gna_ref, ghg_ref, dmix_ref,
             dp_ref, dwna_ref, dwhg_ref, dona_ref, do_ref, dgain_ref, acc_na, acc_hg):
        i = pl.program_id(0)
        dg_ref, dgna_ref, dghg_ref = dp_ref.at[:, 2048:2560], dp_ref.at[:, 2560:3584], dp_ref.at[:, 3584:4608]
        dm = dmix_ref[...].astype(F32)
        dxs = []
        for x_ref, w_ref, gt_ref, dgt_ref, dw_ref, acc in (
                (ona_ref, wna_ref, gna_ref, dgna_ref, dwna_ref, acc_na), (uhg_ref, whg_ref, ghg_ref, dghg_ref, dwhg_ref, acc_hg)):
            xv = x_ref[...]
            y = _dot(xv, w_ref[...])
            sg = jax.nn.sigmoid(gt_ref[...])
            dgt_ref[...] = (dm * y * sg * (1.0 - sg)).astype(BF16)
            dy = (dm * sg).astype(BF16)
            dxs.append(_dot(dy, w_ref[...], NT))
            part = _dot(xv, dy, TN)

            @pl.when(i == 0)
            def _():
                acc[...] = part

            @pl.when(i > 0)
            def _():
                acc[...] += part

            @pl.when(i == ni - 1)
            def _():
                dw_ref[...] = acc[...].astype(BF16)

        dona_ref[...] = dxs[0]
        do, dg, gpart = _hg_post_bwd_rows(dxs[1], of_ref[...] + ob_ref[...], g_ref[...], gain_ref[...], _row_valid(i, TM_B))
        do_ref[...] = do
        dg_ref[...] = dg.astype(BF16)

        @pl.when(i == 0)
        def _():
            dgain_ref[...] = gpart

        @pl.when(i > 0)
        def _():
            dgain_ref[...] += gpart

    act = pl.BlockSpec((TM_B, 512), lambda i: (i, 0))
    wsp = pl.BlockSpec((512, D), lambda i: (0, 0))
    rblk = pl.BlockSpec((TM_B, D), lambda i: (i, 0))
    vec = pl.BlockSpec((1, 512), lambda i: (0, 0))
    return pl.pallas_call(
        body, name="mix_bwd", grid=(ni,),
        in_specs=[act, act, act, act, pl.BlockSpec((TM_B, 512), lambda i: (i, 7)), vec, wsp, wsp,
                  pl.BlockSpec((TM_B, D), lambda i: (i, 4)), pl.BlockSpec((TM_B, D), lambda i: (i, 5)), rblk],
        out_specs=(pl.BlockSpec((TM_B, DP_REST), lambda i: (i, 0)), wsp, wsp, act, act, vec),
        out_shape=(_sds((T, DP_REST), BF16), _sds((512, D), BF16), _sds((512, D), BF16),
                   _sds((T, 512), F32), _sds((T, 512), F32), _sds((1, 512), F32)),
        scratch_shapes=[pltpu.VMEM((512, D), F32), pltpu.VMEM((512, D), F32)],
        compiler_params=_cp(("arbitrary",)))(o_na, u_hg, o_f, o_b, p_act, gain, w_na, w_hg, p_act, p_act, dmix)


def _wo_fwd(mix, w_o, h0, g_mlp):
    def body(mix_ref, w_ref, h0_ref, g_ref, h1_ref, m_ref):
        h1 = h0_ref[...] + _dot(mix_ref[...], w_ref[...])
        h1_ref[...] = h1
        r = lax.rsqrt(jnp.mean(h1 * h1, axis=-1, keepdims=True) + EPS)
        m_ref[...] = (h1 * r * g_ref[...]).astype(BF16)

    blk = pl.BlockSpec((TM_B, D), lambda i: (i, 0))
    return pl.pallas_call(
        body, name="wo_fwd", grid=(T // TM_B,),
        in_specs=[blk, pl.BlockSpec((D, D), lambda i: (0, 0)), blk, pl.BlockSpec((1, D), lambda i: (0, 0))],
        out_specs=(blk, blk), out_shape=(_sds((T, D), F32), _sds((T, D), BF16)),
        compiler_params=_cp(("parallel",)))(mix, w_o, h0, g_mlp)


def _wo_bwd(dh1_b, w_o, mix):
    ni = T // TM_B

    def body(dh_ref, w_ref, mix_ref, dmix_ref, dw_ref, acc):
        i = pl.program_id(0)
        dh = dh_ref[...]
        dmix_ref[...] = _dot(dh, w_ref[...], NT).astype(BF16)
        part = _dot(mix_ref[...], dh, TN)

        @pl.when(i == 0)
        def _():
            acc[...] = part

        @pl.when(i > 0)
        def _():
            acc[...] += part

        @pl.when(i == ni - 1)
        def _():
            dw_ref[...] = acc[...].astype(BF16)

    blk = pl.BlockSpec((TM_B, D), lambda i: (i, 0))
    wsp = pl.BlockSpec((D, D), lambda i: (0, 0))
    return pl.pallas_call(
        body, name="wo_bwd", grid=(ni,), in_specs=[blk, wsp, blk], out_specs=(blk, wsp),
        out_shape=(_sds((T, D), BF16), _sds((D, D), BF16)), scratch_shapes=[pltpu.VMEM((D, D), F32)],
        compiler_params=_cp(("arbitrary",)))(dh1_b, w_o, mix)


FF_B = D_FF // NDEV


def _loss_rows(xv, gv, tv, row0):
    r_io = lax.broadcasted_iota(jnp.int32, (xv.shape[0], 1), 0) + row0
    valid = (r_io >= NM) & (r_io < L)
    r = lax.rsqrt(jnp.mean(xv * xv, axis=-1, keepdims=True) + EPS)
    xh = xv * r
    err = jnp.where(valid, xh * gv - tv, 0.0)
    lpart = 0.5 * jnp.sum(jnp.sum(err * err, axis=-1, keepdims=True) * (1.0 / D), axis=0, keepdims=True)
    dy = err * (1.0 / D)
    dxh = dy * gv
    dh = r * (dxh - xh * jnp.mean(dxh * xh, axis=-1, keepdims=True))
    return lpart, dh, jnp.sum(dy * xh, axis=0, keepdims=True)


def _mlp_fwd_loss(m, wup_g, wdown_g, h1, g_final, tgt):
    nsub = TM_MM // TM_E

    def body(m_ref, wu_ref, wd_ref, h1_ref, g_ref, t_ref, loss_ref, dh_ref, dhb_ref, dg_ref, h2):
        i, j = pl.program_id(0), pl.program_id(1)
        up = jnp.maximum(_dot(m_ref[...], wu_ref[0]), 0.0)
        part = _dot((up * up).astype(BF16), wd_ref[0])

        @pl.when(j == 0)
        def _():
            h2[...] = h1_ref[...] + part

        @pl.when(j > 0)
        def _():
            h2[...] += part

        @pl.when(j == NDEV - 1)
        def _():
            lsum = jnp.zeros((1, 1), F32)
            gsum = jnp.zeros((1, D), F32)
            for s in range(nsub):
                rows = slice(s * TM_E, (s + 1) * TM_E)
                lpart, dh, gpart = _loss_rows(h2[rows, :], g_ref[...], t_ref[rows, :], i * TM_MM + s * TM_E)
                dh_ref[rows, :] = dh
                dhb_ref[rows, :] = dh.astype(BF16)
                lsum = lsum + lpart
                gsum = gsum + gpart
            lsum = jnp.broadcast_to(lsum, (1, 128))

            @pl.when(i == 0)
            def _():
                loss_ref[...] = lsum
                dg_ref[...] = gsum

            @pl.when(i > 0)
            def _():
                loss_ref[...] += lsum
                dg_ref[...] += gsum

    blk = pl.BlockSpec((TM_MM, D), lambda i, j: (i, 0))
    vec = pl.BlockSpec((1, D), lambda i, j: (0, 0))
    return pl.pallas_call(
        body, name="mlp_fwd_loss", grid=(T // TM_MM, NDEV),
        in_specs=[blk, pl.BlockSpec((1, D, FF_B), lambda i, j: (j, 0, 0)), pl.BlockSpec((1, FF_B, D), lambda i, j: (j, 0, 0)),
                  blk, vec, blk],
        out_specs=(pl.BlockSpec((1, 128), lambda i, j: (0, 0)), blk, blk, vec),
        out_shape=(_sds((1, 128), F32), _sds((T, D), F32), _sds((T, D), BF16), _sds((1, D), F32)),
        scratch_shapes=[pltpu.VMEM((TM_MM, D), F32)],
        compiler_params=_cp(("arbitrary", "arbitrary"), 56))(m, wup_g, wdown_g, h1, g_final, tgt)


def _mlp_bwd(m, dh2_b, wup_g, wdown_g, h1, g_mlp, dh2):
    ni = T // TM_B
    nsub = TM_B // TM_E

    def body(m_ref, dh_ref, wu_ref, wd_ref, h1_ref, g_ref, dres_ref, dwu_ref, dwd_ref, dh1_ref, dh1b_ref, dg_ref,
             dm_ref, acc_u, acc_d):
        j, i = pl.program_id(0), pl.program_id(1)
        rows = pl.ds(pl.multiple_of(i * TM_B, TM_B), TM_B)
        mv, dh = m_ref[...], dh_ref[...]
        r = jnp.maximum(_dot(mv, wu_ref[0]), 0.0)
        act = (r * r).astype(BF16)
        dact = _dot(dh, wd_ref[0], NT)
        dup = (dact * (2.0 * r)).astype(BF16)
        pd = _dot(act, dh, TN)
        pu = _dot(mv, dup, TN)
        dmv = _dot(dup, wu_ref[0], NT)

        @pl.when(i == 0)
        def _():
            acc_u[...] = pu
            acc_d[...] = pd

        @pl.when(i > 0)
        def _():
            acc_u[...] += pu
            acc_d[...] += pd

        @pl.when(i == ni - 1)
        def _():
            dwu_ref[0] = acc_u[...].astype(BF16)
            dwd_ref[0] = acc_d[...].astype(BF16)

        @pl.when(j == 0)
        def _():
            dm_ref[rows, :] = dmv

        @pl.when(j > 0)
        def _():
            dm_ref[rows, :] += dmv

        @pl.when(j == NDEV - 1)
        def _():
            gsum = jnp.zeros((1, D), F32)
            for s in range(nsub):
                sub = slice(s * TM_E, (s + 1) * TM_E)
                dm_rows = dm_ref[pl.ds(pl.multiple_of(i * TM_B + s * TM_E, TM_E), TM_E), :]
                dx, gpart = _norm_bwd_rows(h1_ref[sub, :], g_ref[...], dm_rows, dres_ref[sub, :])
                dh1_ref[sub, :] = dx
                dh1b_ref[sub, :] = dx.astype(BF16)
                gsum = gsum + gpart

            @pl.when(i == 0)
            def _():
                dg_ref[...] = gsum

            @pl.when(i > 0)
            def _():
                dg_ref[...] += gsum

    blk = pl.BlockSpec((TM_B, D), lambda j, i: (i, 0))
    late = pl.BlockSpec((TM_B, D), lambda j, i: (jnp.where(j == NDEV - 1, i, 0), 0))
    vec = pl.BlockSpec((1, D), lambda j, i: (0, 0))
    wus = pl.BlockSpec((1, D, FF_B), lambda j, i: (j, 0, 0))
    wds = pl.BlockSpec((1, FF_B, D), lambda j, i: (j, 0, 0))
    return pl.pallas_call(
        body, name="mlp_bwd", grid=(NDEV, ni), in_specs=[blk, blk, wus, wds, late, vec, late],
        out_specs=(wus, wds, late, late, vec),
        out_shape=(_sds((NDEV, D, FF_B), BF16), _sds((NDEV, FF_B, D), BF16), _sds((T, D), F32), _sds((T, D), BF16),
                   _sds((1, D), F32)),
        scratch_shapes=[pltpu.VMEM((T, D), F32), pltpu.VMEM((D, FF_B), F32), pltpu.VMEM((FF_B, D), F32)],
        compiler_params=_cp(("arbitrary", "arbitrary"), 56))(m, dh2_b, wup_g, wdown_g, h1, g_mlp, dh2)


def _adamw(parts, w, m, v, name):
    rr, cc = w.shape
    tr = rr
    for cand in (256, 128, 64):
        if rr % cand == 0 and rr > cand:
            tr = cand
            break
    c1 = 1.0 - ADAM_B1 ** ADAM_STEP
    c2 = 1.0 - ADAM_B2 ** ADAM_STEP

    def body(p_ref, w_ref, m_ref, v_ref, g_ref, d_ref, nm_ref, nv_ref):
        g = p_ref[0].astype(F32)
        for s in range(1, NDEV):
            g = g + p_ref[s].astype(F32)
        mn = ADAM_B1 * m_ref[...] + (1.0 - ADAM_B1) * g
        vn = ADAM_B2 * v_ref[...] + (1.0 - ADAM_B2) * (g * g)
        g_ref[...] = g
        nm_ref[...] = mn
        nv_ref[...] = vn
        d_ref[...] = -ADAM_LR * ((mn / c1) / (jnp.sqrt(vn / c2) + ADAM_EPS) + ADAM_WD * w_ref[...])

    blk = pl.BlockSpec((tr, cc), lambda i: (i, 0))
    return pl.pallas_call(
        body, name=name, grid=(rr // tr,),
        in_specs=[pl.BlockSpec((NDEV, tr, cc), lambda i: (0, i, 0)), blk, blk, blk],
        out_specs=(blk,) * 4, out_shape=(_sds((rr, cc), F32),) * 4,
        compiler_params=_cp(("parallel",)))(parts, w, m, v)


RPB_N = NA_HEADS * 15 * 31
RPB_PAD = 4096
OWN_ROWS = NM + 8


def _pad_rows(a, rows):
    return jnp.pad(a, ((0, rows - a.shape[0]),) + ((0, 0),) * (a.ndim - 1))


def _pack_owned(meta_blk, lb_blk):
    return jnp.concatenate([meta_blk, _pad_rows(lb_blk.reshape(2, 128), 8)], axis=0)


LOSS_ROW = 28


def _pack_replicated(n_mix, n_mlp, n_final, hg_gain, rpb, loss_row=None):
    flat = _pad_rows(rpb.reshape(RPB_N), RPB_PAD)
    gain8 = _pad_rows(hg_gain.reshape(4, 128), 8)
    if loss_row is not None:
        gain8 = gain8 + jnp.pad(loss_row, ((LOSS_ROW - 24, 31 - LOSS_ROW), (0, 0)))
    return jnp.concatenate([n_mix.reshape(8, 128), n_mlp.reshape(8, 128), n_final.reshape(8, 128), gain8,
                            flat.reshape(32, 128)], axis=0)


def _unpack_replicated(a):
    return (a[0:8].reshape(1, D), a[8:16].reshape(1, D), a[16:24].reshape(D), a[24:28].reshape(1, 512),
            a[32:64].reshape(RPB_PAD)[:RPB_N].reshape(1, NA_HEADS, 15, 31))


def kernel(x, meta_tokens, w_in, w_na_out, w_hg_out, w_o, w_up, w_down, norm_mix, norm_mlp, norm_final, hg_norm, na_rpb, hg_lb_logits, loss_target, m_meta_tokens, m_w_in, m_w_na_out, m_w_hg_out, m_w_o, m_w_up, m_w_down, m_norm_mix, m_norm_mlp, m_norm_final, m_hg_norm, m_na_rpb, m_hg_lb_logits, v_meta_tokens, v_w_in, v_w_na_out, v_w_hg_out, v_w_o, v_w_up, v_w_down, v_norm_mix, v_norm_mlp, v_norm_final, v_hg_norm, v_na_rpb, v_hg_lb_logits):
    owned = _pack_owned(meta_tokens, hg_lb_logits)
    first_masks = (SAME_CORE_AND_SIBLING, ALL_PEERS)
    first, tok = _exchange_start([w_in[0].astype(BF16), owned], [False] * 2, "gather_first_start", first_masks)
    bias_tab = _na_bias_table(_tie(jnp.pad(na_rpb[0], ((0, 0), (0, 0), (0, 128 - 31))), tok, "tie_bias_table"))
    later = [w[0].astype(BF16) for w in (w_na_out, w_hg_out, w_o, w_up, w_down)]
    lead = jnp.zeros((NM, D), F32) + tok[0, 0]
    h0_rows = jnp.concatenate([lead, x[0], jnp.zeros((T - L, D), F32)], axis=0)
    tgt = jnp.concatenate([lead, loss_target[0], jnp.zeros((T - L, D), F32)], axis=0)
    (_, owned_g), first = _exchange_wait(first, [False] * 2, [bias_tab, h0_rows], "gather_small_wait", first_masks, which=(1,))
    meta_full = jnp.transpose(owned_g[:, 0:NM, :], (1, 0, 2)).reshape(NM, D)
    logits = jnp.transpose(owned_g[:, NM:NM + 2, :].reshape(NDEV, 2, 2, 64), (1, 2, 0, 3)).reshape(2, 2, 512)
    h0 = lax.dynamic_update_slice(h0_rows, meta_full, (0, 0))
    a, a_t = _norm_fwd_t(h0, norm_mix, "norm_mix_fwd")
    (win_l, _), _ = _exchange_wait(first, [False] * 2, [a, logits, tgt] + later, "gather_first_wait", first_masks, which=(0,))
    (win_g,) = _forward_to_sibling([win_l], "gather_first_forward")
    later[0] = _tie(later[0], win_g, "tie_gather_rest")
    gather_rest, tok = _exchange_start(later, [False] * 5, "gather_rest_start")
    win_g = _tie(win_g, tok, "tie_inproj")

    p_act = _inproj_fwd(a, win_g)
    o_na, lse = _na_fwd(p_act, bias_tab)
    qh, k_f, b_f, k_b, b_b = _hg_pre(p_act, logits)
    o_f, st_f = _hg_scan_fwd(qh, k_f, b_f, p_act, False)
    o_b, st_b = _hg_scan_fwd(qh, k_b, b_b, p_act, True)
    (wna_g, whg_g, wo_g, _, _), gather_rest = _exchange_wait(
        gather_rest, [False] * 5, [o_f, o_b, o_na], "gather_rest_wait_a", which=(0, 1, 2))
    w_na_full = jnp.transpose(wna_g, (1, 0, 2)).reshape(512, D)
    w_hg_full = jnp.transpose(whg_g, (1, 0, 2)).reshape(512, D)
    mix, u_hg = _mix_fwd(o_na, o_f, o_b, hg_norm, w_na_full, w_hg_full, p_act)
    h1, m_act = _wo_fwd(mix, wo_g.reshape(D, D), h0, norm_mlp)
    (_, _, wo_g, wup_g, wdown_g), _ = _exchange_wait(gather_rest, [False] * 5, [m_act], "gather_rest_wait_b", which=(3, 4))
    w_o_full = wo_g.reshape(D, D)
    loss_part, dh2, dh2_b, d_nfinal = _mlp_fwd_loss(m_act, wup_g, wdown_g, h1, norm_final.reshape(1, D), tgt)

    dwup_p, dwdown_p, dh1, dh1_b, d_nmlp = _mlp_bwd(m_act, dh2_b, wup_g, wdown_g, h1, norm_mlp, dh2)
    sc_mlp, tok = _exchange_start([dwup_p, dwdown_p], [True] * 2, "scatter_mlp_start")
    dmix, dwo = _wo_bwd(_tie(dh1_b, tok, "tie_wo_bwd"), w_o_full, mix)
    sc_wo, tok = _exchange_start([dwo.reshape(NDEV, D // NDEV, D)], [True], "scatter_wo_start")
    dp_rest, dwna, dwhg, do_na, do_hg, d_gain = _mix_bwd(
        o_na, u_hg, o_f, o_b, hg_norm, w_na_full, w_hg_full, p_act, _tie(dmix, tok, "tie_mix_bwd"))
    owner_cols = lambda w: jnp.transpose(w.reshape(512, NDEV, D // NDEV), (1, 0, 2))
    sc_br, tok = _exchange_start([owner_cols(dwna), owner_cols(dwhg)], [True] * 2, "scatter_branch_start")
    do_hg = _tie(do_hg, tok, "tie_hg_scan_bwd")
    dq_f, dk_f, db_f, dv_f = _hg_scan_bwd(qh, k_f, b_f, p_act, st_f, do_hg, False)
    dq_b, dk_b, db_b, dv_b = _hg_scan_bwd(qh, k_b, b_b, p_act, st_b, do_hg, True)
    dp_rest, d_logits = _hg_pre_bwd(p_act, logits, dq_f, dq_b, dk_f, dk_b, db_f, db_b, dv_f, dv_b, dp_rest)
    dq_na, dk_na, dv_na, dbias = _na_bwd(p_act, do_na, lse, bias_tab)
    dp_na = jnp.concatenate([dq_na.astype(BF16), dk_na.astype(BF16), dv_na.astype(BF16)], axis=1)
    dwin_p = _inproj_bwd_dw(a_t, dp_na, dp_rest)
    sc_in, tok = _exchange_start([dwin_p], [True], "scatter_in_start")
    dh0, d_nmix = _inproj_bwd_da(_tie(dp_na, tok, "tie_inproj_bwd_da"), dp_rest, win_g, h0, norm_mix, dh1)
    d_rpb = _na_rpb_reduce(_tie(dbias, tok, "tie_rpb_reduce"))[:, :, :31]

    res = {}

    def update(nm, parts, w, mm, vv):
        res[nm] = [r[None] for r in _adamw(parts, w[0], mm[0], vv[0], "adamw_" + nm)]
        return res[nm][1]

    wup_r, wdown_r = _exchange_wait(sc_mlp, [True] * 2, [dh0, d_rpb], "scatter_mlp_wait")
    update("w_up", wup_r, w_up, m_w_up, v_w_up)
    last = update("w_down", wdown_r, w_down, m_w_down, v_w_down)
    (wo_r,) = _exchange_wait(sc_wo, [True], [last], "scatter_wo_wait")
    last = update("w_o", wo_r, w_o, m_w_o, v_w_o)
    wna_r, whg_r = _exchange_wait(sc_br, [True] * 2, [last], "scatter_branch_wait")
    update("w_na_out", wna_r, w_na_out, m_w_na_out, v_w_na_out)
    last = update("w_hg_out", whg_r, w_hg_out, m_w_hg_out, v_w_hg_out)

    d_meta = jnp.transpose(dh0[0:NM].reshape(NM, NDEV, 128), (1, 0, 2))
    d_lg = jnp.transpose(d_logits.reshape(2, 2, NDEV, 64), (2, 0, 1, 3)).reshape(NDEV, 2, 128)
    owned_p = jnp.concatenate([d_meta, jnp.pad(d_lg, ((0, 0), (0, OWN_ROWS - NM - 2), (0, 0)))], axis=1)
    repl_p = _pack_replicated(d_nmix, d_nmlp, d_nfinal, d_gain, d_rpb, loss_part)
    grad_x = dh0[NM:L][None]
    done_first = [grad_x] + [res[nm][0] for nm in ("w_up", "w_down", "w_o", "w_na_out", "w_hg_out")]
    owned_r, repl_r = _exchange([owned_p, repl_p], [True, False], "scatter_small", done_first)
    own = _adamw(owned_r, owned, _pack_owned(m_meta_tokens, m_hg_lb_logits), _pack_owned(v_meta_tokens, v_hg_lb_logits),
                 "adamw_owned_small")
    res["meta_tokens"] = [r[0:NM] for r in own]
    res["hg_lb_logits"] = [r[NM:NM + 2].reshape(2, 2, 64) for r in own]
    rep = _adamw(repl_r, _pack_replicated(norm_mix, norm_mlp, norm_final, hg_norm, na_rpb),
                 _pack_replicated(m_norm_mix, m_norm_mlp, m_norm_final, m_hg_norm, m_na_rpb),
                 _pack_replicated(v_norm_mix, v_norm_mlp, v_norm_final, v_hg_norm, v_na_rpb), "adamw_replicated")
    for q in range(4):
        um = _unpack_replicated(rep[q])
        for nm, val in zip(("norm_mix", "norm_mlp", "norm_final", "hg_norm", "na_rpb"), um):
            res.setdefault(nm, [None] * 4)[q] = val
    (win_r,) = _exchange_wait(sc_in, [True], [rep[1], own[1]], "scatter_in_wait")
    update("w_in", win_r, w_in, m_w_in, v_w_in)

    loss = jnp.sum(repl_r[:, LOSS_ROW, 0])
    order = ("meta_tokens", "w_in", "w_na_out", "w_hg_out", "w_o", "w_up", "w_down", "norm_mix", "norm_mlp", "norm_final",
             "hg_norm", "na_rpb", "hg_lb_logits")
    outs = [loss, grad_x]
    for q in range(4):
        outs += [res[nm][q] for nm in order]
    return tuple(outs)
```

```python
import functools

import numpy as np
import jax
import jax.numpy as jnp
from jax import lax
from jax.experimental import pallas as pl
from jax.experimental.pallas import tpu as pltpu

F32 = jnp.float32
BF16 = jnp.bfloat16

D = 1024
SEQ = 2048
NM = 16
L = SEQ + NM
T = 2176
NDEV = 8
EPS = 1e-6
GRID_W = 64
ROWS = SEQ // GRID_W
NA_HEADS = 8
NA_DH = 64
NA_SCALE = NA_DH ** -0.5
HG_HEADS = 4
HG_C = 16
NCHUNK = L // HG_C
D_FF = 4096
IN_COLS = 6144
NEG = -1e30

ADAM_LR = 0.001
ADAM_B1 = 0.9
ADAM_B2 = 0.999
ADAM_EPS = 1e-08
ADAM_WD = 0.01
ADAM_STEP = 10

MESH_ID = pl.DeviceIdType.MESH
ANY = pl.BlockSpec(memory_space=pl.ANY)

NN = (((1,), (0,)), ((), ()))
NT = (((1,), (1,)), ((), ()))
TN = (((0,), (0,)), ((), ()))


def _cp(sem=None, vmem_mb=48):
    return pltpu.CompilerParams(dimension_semantics=sem, vmem_limit_bytes=vmem_mb * 1024 * 1024)


def _dot(a, b, dims=NN):
    return lax.dot_general(a, b, dims, preferred_element_type=F32)


def _sds(shape, dtype):
    return jax.ShapeDtypeStruct(shape, dtype)


HBM = pl.BlockSpec(memory_space=pltpu.HBM)
SEM = pl.BlockSpec(memory_space=pltpu.SEMAPHORE)
EFFECT = pltpu.SideEffectType.DATAFLOW_SIDE_EFFECTING


def _exchange(arrs, scatter, name, after=()):
    n = len(arrs)
    after = list(after)
    out_shapes = []
    for a, sc in zip(arrs, scatter):
        out_shapes.append(_sds(a.shape if sc else (NDEV,) + a.shape, a.dtype))

    def body(*refs):
        ins, outs = refs[:n], refs[n + len(after):2 * n + len(after)]
        send_sems, recv_sems, loc_sems = refs[2 * n + len(after):]
        me = 4 * lax.axis_index("x") + 2 * lax.axis_index("y") + lax.axis_index("c")
        copies = []
        for k in range(n):
            src_me = ins[k].at[me] if scatter[k] else ins[k]
            loc = pltpu.make_async_copy(src_me, outs[k].at[me], loc_sems.at[k])
            loc.start()
            copies.append(loc)
        remote = sum(_peer_copies(ins, outs, scatter, send_sems, recv_sems), [])
        for cp in remote:
            cp.start()
        for cp in remote:
            cp.wait_recv()
        for cp in remote:
            cp.wait_send()
        for cp in copies:
            cp.wait()

    return pl.pallas_call(
        body, name=name, out_shape=tuple(out_shapes), in_specs=[ANY] * (n + len(after)), out_specs=tuple([ANY] * n),
        scratch_shapes=[pltpu.SemaphoreType.DMA((n * (NDEV - 1),)), pltpu.SemaphoreType.DMA((n * (NDEV - 1),)),
                        pltpu.SemaphoreType.DMA((n,))],
    )(*arrs, *after)


def _forward_to_sibling(bufs, name):
    n = len(bufs)

    def body(*refs):
        ins, outs = refs[:n], refs[n:2 * n]
        send_sems, recv_sems = refs[2 * n:]
        x, y, c = lax.axis_index("x"), lax.axis_index("y"), lax.axis_index("c")
        copies = []
        for k in range(n):
            for j, (cx, cy) in enumerate(((1 - x, y), (x, 1 - y), (1 - x, 1 - y))):
                slot = 4 * cx + 2 * cy + c
                copies.append(pltpu.make_async_remote_copy(
                    src_ref=ins[k].at[slot], dst_ref=outs[k].at[slot], send_sem=send_sems.at[3 * k + j],
                    recv_sem=recv_sems.at[3 * k + j], device_id=(x, y, 1 - c), device_id_type=MESH_ID))
        for cp in copies:
            cp.start()
        for cp in copies:
            cp.wait_recv()
        for cp in copies:
            cp.wait_send()

    return pl.pallas_call(
        body, name=name, out_shape=tuple(_sds(b.shape, b.dtype) for b in bufs), in_specs=[ANY] * n,
        out_specs=tuple([ANY] * n), input_output_aliases={k: k for k in range(n)},
        scratch_shapes=[pltpu.SemaphoreType.DMA((3 * n,)), pltpu.SemaphoreType.DMA((3 * n,))],
    )(*bufs)


ALL_PEERS = tuple(range(1, NDEV))
SAME_CORE_AND_SIBLING = (1, 2, 4, 6)


def _peer_copies(srcs, lands, scatter, send_sems, recv_sems, masks=ALL_PEERS):
    x, y, c = lax.axis_index("x"), lax.axis_index("y"), lax.axis_index("c")
    me = 4 * x + 2 * y + c
    out = []
    for k in range(len(srcs)):
        out.append([])
        for m in (masks[k] if isinstance(masks[0], tuple) else masks):
            px, py, pc = x ^ (m >> 2), y ^ ((m >> 1) & 1), c ^ (m & 1)
            src = srcs[k].at[4 * px + 2 * py + pc] if scatter[k] else srcs[k]
            out[k].append(pltpu.make_async_remote_copy(
                src_ref=src, dst_ref=lands[k].at[me], send_sem=send_sems.at[k * (NDEV - 1) + m - 1],
                recv_sem=recv_sems.at[k * (NDEV - 1) + m - 1],
                device_id=(px, py, pc), device_id_type=MESH_ID))
    return out


def _exchange_start(arrs, scatter, name, masks=ALL_PEERS):
    n = len(arrs)
    me = 4 * lax.axis_index("x") + 2 * lax.axis_index("y") + lax.axis_index("c")
    lands = []
    for a, sc in zip(arrs, scatter):
        own = lax.dynamic_index_in_dim(a, me, 0, keepdims=True) if sc else a[None]
        shape = a.shape if sc else (NDEV,) + a.shape
        lands.append(lax.dynamic_update_index_in_dim(lax.empty(shape, a.dtype), own, me, 0))

    def body(*refs):
        srcs, lnds = refs[:n], refs[n:2 * n]
        send_sems, recv_sems = refs[2 * n], refs[2 * n + 1]
        token = refs[-1]
        for cp in sum(_peer_copies(srcs, lnds, scatter, send_sems, recv_sems, masks), []):
            cp.start()
        token[...] = jnp.zeros_like(token)

    ops = [pltpu.with_memory_space_constraint(a, pltpu.HBM) for a in list(arrs) + lands]
    res = pl.pallas_call(
        body, name=name,
        out_shape=(pltpu.SemaphoreType.DMA((n * (NDEV - 1),)), pltpu.SemaphoreType.DMA((n * (NDEV - 1),)))
        + tuple(pltpu.HBM(o.shape, o.dtype) for o in ops) + (_sds((8, 128), F32),),
        in_specs=[HBM] * (2 * n), out_specs=(SEM, SEM) + (HBM,) * (2 * n) + (pl.BlockSpec(memory_space=pltpu.VMEM),),
        input_output_aliases={k: 2 + k for k in range(2 * n)},
        compiler_params=pltpu.CompilerParams(has_side_effects=EFFECT),
    )(*ops)
    return res[:-1], res[-1]


def _exchange_wait(handle, scatter, after, name, masks=ALL_PEERS, which=None):
    send_sems, recv_sems = handle[0], handle[1]
    bufs = handle[2:]
    n = len(bufs) // 2
    after = list(after)

    def body(*refs):
        srcs, lnds = refs[:n], refs[n:2 * n]
        copies = _peer_copies(srcs, lnds, scatter, refs[2 * n], refs[2 * n + 1], masks)
        for k in (range(n) if which is None else which):
            for cp in copies[k]:
                cp.wait_send()
                cp.wait_recv()

    res = pl.pallas_call(
        body, name=name, out_shape=tuple(pltpu.HBM(b.shape, b.dtype) for b in bufs),
        in_specs=[HBM] * (2 * n) + [SEM, SEM] + [ANY] * len(after), out_specs=(HBM,) * (2 * n),
        input_output_aliases={k: k for k in range(2 * n)},
        compiler_params=pltpu.CompilerParams(has_side_effects=EFFECT),
    )(*bufs, send_sems, recv_sems, *after)
    return res[n:] if which is None else (res[n:], (send_sems, recv_sems) + tuple(res))


def _tie(x, token, name):
    def body(x_ref, t_ref, o_ref):
        del x_ref, t_ref, o_ref

    return pl.pallas_call(body, name=name, out_shape=_sds(x.shape, x.dtype), in_specs=[ANY, ANY], out_specs=ANY,
                          input_output_aliases={0: 0})(x, token)


TM_E = 272


def _norm_fwd_t(h, g, name):
    def body(h_ref, g_ref, o_ref, ot_ref):
        xv = h_ref[...]
        r = lax.rsqrt(jnp.mean(xv * xv, axis=-1, keepdims=True) + EPS)
        y = xv * r * g_ref[...]
        o_ref[...] = y.astype(BF16)
        ot_ref[...] = y.T.astype(BF16)

    return pl.pallas_call(
        body, name=name, grid=(T // 128,),
        in_specs=[pl.BlockSpec((128, D), lambda i: (i, 0)), pl.BlockSpec((1, D), lambda i: (0, 0))],
        out_specs=(pl.BlockSpec((128, D), lambda i: (i, 0)), pl.BlockSpec((D, 128), lambda i: (0, i))),
        out_shape=(_sds((T, D), BF16), _sds((D, T), BF16)), compiler_params=_cp(("parallel",)))(h, g)


def _norm_bwd_rows(xv, gv, dnv, dres):
    r = lax.rsqrt(jnp.mean(xv * xv, axis=-1, keepdims=True) + EPS)
    xh = xv * r
    dxh = dnv * gv
    dx = dres + r * (dxh - xh * jnp.mean(dxh * xh, axis=-1, keepdims=True))
    return dx, jnp.sum(dnv * xh, axis=0, keepdims=True)


TM_MM = 1088


def _inproj_fwd(a, w_g):
    nb = w_g.shape[2]

    def body(a_ref, w_ref, o_ref):
        o_ref[...] = _dot(a_ref[...], w_ref[0])

    return pl.pallas_call(
        body, name="inproj_fwd", grid=(T // TM_MM, NDEV),
        in_specs=[pl.BlockSpec((TM_MM, D), lambda i, j: (i, 0)), pl.BlockSpec((1, D, nb), lambda i, j: (j, 0, 0))],
        out_specs=pl.BlockSpec((TM_MM, nb), lambda i, j: (i, j)), out_shape=_sds((T, NDEV * nb), F32),
        compiler_params=_cp(("parallel", "parallel")))(a, w_g)


TM_B = 544


W_IN_B = IN_COLS // NDEV


NA_BLKS = 1536 // W_IN_B


def _dp_specs(rows, row_index):
    return [pl.BlockSpec((rows, W_IN_B), lambda *g: (row_index(*g), jnp.minimum(g[-1], NA_BLKS - 1))),
            pl.BlockSpec((rows, W_IN_B), lambda *g: (row_index(*g), jnp.maximum(g[-1] - NA_BLKS, 0)))]


def _inproj_bwd_dw(a_t, dp_na, dp_rest):
    def body(at_ref, na_ref, rest_ref, dw_ref):
        j = pl.program_id(0)

        @pl.when(j < NA_BLKS)
        def _():
            dw_ref[0] = _dot(at_ref[...], na_ref[...]).astype(BF16)

        @pl.when(j >= NA_BLKS)
        def _():
            dw_ref[0] = _dot(at_ref[...], rest_ref[...]).astype(BF16)

    return pl.pallas_call(
        body, name="inproj_bwd_dw", grid=(NDEV,),
        in_specs=[pl.BlockSpec((D, T), lambda j: (0, 0))] + _dp_specs(T, lambda j: 0),
        out_specs=pl.BlockSpec((1, D, W_IN_B), lambda j: (j, 0, 0)), out_shape=_sds((NDEV, D, W_IN_B), BF16),
        compiler_params=_cp(("parallel",)))(a_t, dp_na, dp_rest)


def _inproj_bwd_da(dp_na, dp_rest, w_g, h0, g_mix, dh1):
    nsub = TM_MM // TM_E

    def body(na_ref, rest_ref, w_ref, h0_ref, g_ref, dres_ref, dh0_ref, dg_ref, da):
        i, j = pl.program_id(0), pl.program_id(1)
        dpv = jnp.where(j < NA_BLKS, na_ref[...], rest_ref[...])
        dav = _dot(dpv, w_ref[0], NT)

        @pl.when(j == 0)
        def _():
            da[...] = dav

        @pl.when(j > 0)
        def _():
            da[...] += dav

        @pl.when(j == NDEV - 1)
        def _():
            gsum = jnp.zeros((1, D), F32)
            for s in range(nsub):
                sub = slice(s * TM_E, (s + 1) * TM_E)
                dx, gpart = _norm_bwd_rows(h0_ref[sub, :], g_ref[...], da[sub, :], dres_ref[sub, :])
                dh0_ref[sub, :] = dx
                gsum = gsum + gpart

            @pl.when(i == 0)
            def _():
                dg_ref[...] = gsum

            @pl.when(i > 0)
            def _():
                dg_ref[...] += gsum

    rblk = pl.BlockSpec((TM_MM, D), lambda i, j: (i, 0))
    vec = pl.BlockSpec((1, D), lambda i, j: (0, 0))
    return pl.pallas_call(
        body, name="inproj_bwd_da", grid=(T // TM_MM, NDEV),
        in_specs=_dp_specs(TM_MM, lambda i, j: i) + [pl.BlockSpec((1, D, W_IN_B), lambda i, j: (j, 0, 0)), rblk, vec, rblk],
        out_specs=(rblk, vec), out_shape=(_sds((T, D), F32), _sds((1, D), F32)),
        scratch_shapes=[pltpu.VMEM((TM_MM, D), F32)],
        compiler_params=_cp(("arbitrary", "arbitrary"), 56))(dp_na, dp_rest, w_g, h0, g_mix, dh1)


NA_QB = 256
NA_GROUPS = ROWS // 4
NA_UROWS = 11
NA_KW = NA_UROWS * GRID_W
NA_KU = 768


def _na_row_offset(var, i, j):
    valid = (j < 8, i <= j < i + 8, 3 <= j < NA_UROWS)[var]
    return (j - i + (7, 3, 0)[var]) if valid else None


def _na_bias_table(rp):
    def body(r_ref, o_ref):
        row3 = lax.broadcasted_iota(jnp.int32, (15, GRID_W, 128), 1)
        lane3 = lax.broadcasted_iota(jnp.int32, (15, GRID_W, 128), 2)
        w3 = lane3 & (GRID_W - 1)
        cs3 = jnp.clip(row3 - 8, 0, GRID_W - 16)
        lane = lax.broadcasted_iota(jnp.int32, (GRID_W, 128), 1)
        neg = jnp.full((GRID_W, 128), NEG, F32)
        z = jnp.stack([jnp.broadcast_to(r_ref[0, a:a + 1, :], (GRID_W, 128)) for a in range(15)])
        for bit in range(6):
            sh = 1 << bit
            z = jnp.where((row3 & sh) != 0, jnp.roll(z, sh, axis=2), z)
        z = jnp.roll(z, 128 - 15, axis=2)
        z = jnp.where(lane3 < GRID_W, z, 0.0)
        z = z + jnp.roll(z, GRID_W, axis=2)
        tabs = jnp.where((w3 >= cs3) & (w3 < cs3 + 16), z, NEG)
        tail = jnp.where(lane < GRID_W + NM, 0.0, NEG)
        for var in range(3):
            for i in range(4):
                for jp in range(NA_KU // 128):
                    halves = []
                    for j in (2 * jp, 2 * jp + 1):
                        a = _na_row_offset(var, i, j) if j < NA_UROWS else None
                        halves.append(tail if j >= NA_UROWS else (neg if a is None else tabs[a]))
                    o_ref[var, 0, i * 64:(i + 1) * 64, jp * 128:(jp + 1) * 128] = jnp.where(lane < GRID_W, halves[0], halves[1])

    return pl.pallas_call(
        body, name="na_bias_table", grid=(NA_HEADS,),
        in_specs=[pl.BlockSpec((1, 15, 128), lambda h: (h, 0, 0))],
        out_specs=pl.BlockSpec((3, 1, NA_QB, NA_KU), lambda h: (0, h, 0, 0)),
        out_shape=_sds((3, NA_HEADS, NA_QB, NA_KU), F32), compiler_params=_cp(("parallel",)))(rp)


def _na_var(g):
    return jnp.where(g == 0, 0, jnp.where(g == NA_GROUPS - 1, 2, 1))


def _na_load_window(src_ref, dst, g):
    us = jnp.clip(4 * g - 4, 0, ROWS - NA_UROWS)
    kstart = pl.multiple_of(NM + GRID_W * us, 16)
    dst[0:NA_KW, :] = src_ref[pl.ds(kstart, NA_KW), :].astype(BF16)
    dst[NA_KW:NA_KW + NM, :] = src_ref[0:NM, :].astype(BF16)
    dst[NA_KW + NM:, :] = jnp.zeros((NA_KU - NA_KW - NM, 128), BF16)
    return kstart


def _na_fwd(p_act, bias_tab):
    def body(q_ref, k_ref, v_ref, b_ref, o_ref, lse_ref, ku, vu):
        g = pl.program_id(1)
        _na_load_window(k_ref, ku, g)
        _na_load_window(v_ref, vu, g)
        qstart = pl.multiple_of(NM + NA_QB * g, 16)
        q = q_ref[pl.ds(qstart, NA_QB), :]
        lane = lax.broadcasted_iota(jnp.int32, (NA_QB, 128), 1)
        o_h, lse_h = [], []
        for h in range(2):
            hm = (lane < 64) if h == 0 else (lane >= 64)
            qm = (jnp.where(hm, q, 0.0) * NA_SCALE).astype(BF16)
            s = _dot(qm, ku[...], NT) + b_ref[0, h]
            m = jnp.max(s, axis=-1, keepdims=True)
            p = jnp.exp(s - m)
            l = jnp.sum(p, axis=-1, keepdims=True)
            o_h.append(_dot(p.astype(BF16), vu[...]) / l)
            lse_h.append(jnp.broadcast_to(m + jnp.log(l), (NA_QB, 128)))
        o_ref[pl.ds(qstart, NA_QB), :] = jnp.where(lane < 64, o_h[0], o_h[1]).astype(BF16)
        lse_ref[0, pl.ds(qstart, NA_QB), :] = jnp.where(lane < 64, lse_h[0], lse_h[1])

        @pl.when(g == 0)
        def _():
            qm_ = q_ref[0:NM, :]
            lane_m = lax.broadcasted_iota(jnp.int32, (NM, 128), 1)
            km, vm = ku[NA_KW:NA_KW + NM, :], vu[NA_KW:NA_KW + NM, :]
            om = []
            for h in range(2):
                hm = (lane_m < 64) if h == 0 else (lane_m >= 64)
                s = _dot(jnp.where(hm, qm_, 0.0).astype(BF16), km, NT) * NA_SCALE
                p = jnp.exp(s - jnp.max(s, axis=-1, keepdims=True))
                l = jnp.sum(p, axis=-1, keepdims=True)
                om.append(_dot(p.astype(BF16), vm) / l)
            o_ref[0:NM, :] = jnp.where(lane_m < 64, om[0], om[1]).astype(BF16)
            o_ref[L:T, :] = jnp.zeros((T - L, 128), BF16)
            lse_ref[0, 0:NM, :] = jnp.zeros((NM, 128), F32)
            lse_ref[0, L:T, :] = jnp.zeros((T - L, 128), F32)

    col = lambda off: pl.BlockSpec((T, 128), lambda hp, g: (0, off + hp))
    return pl.pallas_call(
        body, name="na_fwd", grid=(4, NA_GROUPS),
        in_specs=[col(0), col(4), col(8),
                  pl.BlockSpec((1, 2, NA_QB, NA_KU), lambda hp, g: (_na_var(g), hp, 0, 0))],
        out_specs=(pl.BlockSpec((T, 128), lambda hp, g: (0, hp)), pl.BlockSpec((1, T, 128), lambda hp, g: (hp, 0, 0))),
        out_shape=(_sds((T, 512), BF16), _sds((4, T, 128), F32)),
        scratch_shapes=[pltpu.VMEM((NA_KU, 128), BF16), pltpu.VMEM((NA_KU, 128), BF16)],
        compiler_params=_cp(("parallel", "arbitrary")))(p_act, p_act, p_act, bias_tab)


def _na_bwd(p_act, do, lse, bias_tab):
    def body(q_ref, k_ref, v_ref, do_ref, lse_ref, b_ref, dq_ref, dk_ref, dv_ref, db_ref, ku, vu):
        g = pl.program_id(1)

        @pl.when(g == 0)
        def _():
            dq_ref[...] = jnp.zeros((T, 128), F32)
            dk_ref[...] = jnp.zeros((T, 128), F32)
            dv_ref[...] = jnp.zeros((T, 128), F32)

        kstart = _na_load_window(k_ref, ku, g)
        _na_load_window(v_ref, vu, g)
        qstart = pl.multiple_of(NM + NA_QB * g, 16)
        q = q_ref[pl.ds(qstart, NA_QB), :]
        dov = do_ref[pl.ds(qstart, NA_QB), :]
        lsev = lse_ref[0, pl.ds(qstart, NA_QB), :]
        lane = lax.broadcasted_iota(jnp.int32, (NA_QB, 128), 1)
        first = (g == 0) | (g == 1) | (g == NA_GROUPS - 1)
        dq_h = []
        dku = jnp.zeros((NA_KU, 128), F32)
        dvu = jnp.zeros((NA_KU, 128), F32)
        for h in range(2):
            hm = (lane < 64) if h == 0 else (lane >= 64)
            qm = (jnp.where(hm, q, 0.0) * NA_SCALE).astype(BF16)
            dom = jnp.where(hm, dov, 0.0).astype(BF16)
            s = _dot(qm, ku[...], NT) + b_ref[0, h]
            p = jnp.exp(s - lsev[:, 64 * h:64 * h + 1])
            dp = _dot(dom, vu[...], NT)
            delta = jnp.sum(p * dp, axis=-1, keepdims=True)
            ds = p * (dp - delta)

            @pl.when(first)
            def _():
                db_ref[0, h] = ds

            @pl.when(jnp.logical_not(first))
            def _():
                db_ref[0, h] += ds

            dsb = ds.astype(BF16)
            dq_h.append(_dot(dsb, ku[...]) * NA_SCALE)
            dku = dku + _dot(dsb, qm, TN)
            dvu = dvu + _dot(p.astype(BF16), dom, TN)
        dq_ref[pl.ds(qstart, NA_QB), :] = jnp.where(lane < 64, dq_h[0], dq_h[1])
        dk_ref[pl.ds(kstart, NA_KW), :] += dku[0:NA_KW]
        dv_ref[pl.ds(kstart, NA_KW), :] += dvu[0:NA_KW]
        dk_ref[0:NM, :] += dku[NA_KW:NA_KW + NM]
        dv_ref[0:NM, :] += dvu[NA_KW:NA_KW + NM]

        @pl.when(g == 0)
        def _():
            qm_ = q_ref[0:NM, :]
            dom_ = do_ref[0:NM, :]
            lane_m = lax.broadcasted_iota(jnp.int32, (NM, 128), 1)
            km, vm = ku[NA_KW:NA_KW + NM, :], vu[NA_KW:NA_KW + NM, :]
            dqs = []
            dkm = jnp.zeros((NM, 128), F32)
            dvm = jnp.zeros((NM, 128), F32)
            for h in range(2):
                hm = (lane_m < 64) if h == 0 else (lane_m >= 64)
                qh = jnp.where(hm, qm_, 0.0).astype(BF16)
                doh = jnp.where(hm, dom_, 0.0).astype(BF16)
                s = _dot(qh, km, NT) * NA_SCALE
                e = jnp.exp(s - jnp.max(s, axis=-1, keepdims=True))
                p = e / jnp.sum(e, axis=-1, keepdims=True)
                dp = _dot(doh, vm, NT)
                ds = p * (dp - jnp.sum(p * dp, axis=-1, keepdims=True))
                dsb = (ds * NA_SCALE).astype(BF16)
                dqs.append(_dot(dsb, km))
                dkm = dkm + _dot(dsb, qh, TN)
                dvm = dvm + _dot(p.astype(BF16), doh, TN)
            dq_ref[0:NM, :] = jnp.where(lane_m < 64, dqs[0], dqs[1])
            dk_ref[0:NM, :] += dkm
            dv_ref[0:NM, :] += dvm

    col = lambda off: pl.BlockSpec((T, 128), lambda hp, g: (0, off + hp))
    ocol = pl.BlockSpec((T, 128), lambda hp, g: (0, hp))
    bspec = pl.BlockSpec((1, 2, NA_QB, NA_KU), lambda hp, g: (_na_var(g), hp, 0, 0))
    return pl.pallas_call(
        body, name="na_bwd", grid=(4, NA_GROUPS),
        in_specs=[col(0), col(4), col(8), ocol, pl.BlockSpec((1, T, 128), lambda hp, g: (hp, 0, 0)), bspec],
        out_specs=(ocol, ocol, ocol, bspec),
        out_shape=(_sds((T, 512), F32), _sds((T, 512), F32), _sds((T, 512), F32), _sds((3, NA_HEADS, NA_QB, NA_KU), F32)),
        scratch_shapes=[pltpu.VMEM((NA_KU, 128), BF16), pltpu.VMEM((NA_KU, 128), BF16)],
        compiler_params=_cp(("parallel", "arbitrary")))(p_act, p_act, p_act, do, lse, bias_tab)


def _na_rpb_reduce(dbias):
    def body(db_ref, o_ref):
        lane = lax.broadcasted_iota(jnp.int32, (GRID_W, 128), 1)
        row3 = lax.broadcasted_iota(jnp.int32, (15, GRID_W, 128), 1)
        lane3 = lax.broadcasted_iota(jnp.int32, (15, GRID_W, 128), 2)
        accs = []
        for a in range(15):
            acc = jnp.zeros((GRID_W, 128), F32)
            for var in range(3):
                for i in range(4):
                    for j in range(NA_UROWS):
                        if _na_row_offset(var, i, j) == a:
                            pair = db_ref[var, 0, i * 64:(i + 1) * 64, (j // 2) * 128:(j // 2 + 1) * 128]
                            acc = acc + jnp.where((lane < GRID_W) if j % 2 == 0 else (lane >= GRID_W), pair, 0.0)
            accs.append(acc)
        z = jnp.stack(accs)
        z = jnp.where(lane3 < GRID_W, z + jnp.roll(z, GRID_W, axis=2), 0.0)
        for bit in range(6):
            sh = 1 << bit
            z = jnp.where((row3 & sh) != 0, jnp.roll(z, 128 - sh, axis=2), z)
        z = jnp.roll(z, 15, axis=2)
        o_ref[0] = jnp.sum(z, axis=1)

    return pl.pallas_call(
        body, name="na_rpb_reduce", grid=(NA_HEADS,),
        in_specs=[pl.BlockSpec((3, 1, NA_QB, NA_KU), lambda h: (0, h, 0, 0))],
        out_specs=pl.BlockSpec((1, 15, 128), lambda h: (h, 0, 0)), out_shape=_sds((NA_HEADS, 15, 128), F32),
        compiler_params=_cp(("parallel",)))(dbias)


HG_RB = 128
HG_NB = T // HG_RB
HG_SLOTS = HG_NB * 8
HI = lax.Precision.HIGHEST
HG_UNROLL = 4


def _chunk_tri(lower):
    r = lax.broadcasted_iota(jnp.int32, (HG_RB, HG_RB), 0)
    c = lax.broadcasted_iota(jnp.int32, (HG_RB, HG_RB), 1)
    same = (r // HG_C) == (c // HG_C)
    keep = (c <= r) if lower else (c >= r)
    return jnp.where(same & keep, 1.0, 0.0).astype(F32)


def _hg_gate_terms(z, lg):
    dl = lg[0:1, :] - lg[1:2, :]
    log_lb = jax.nn.log_sigmoid(dl)
    log_1mlb = jax.nn.log_sigmoid(-dl)
    yz = log_1mlb + jax.nn.log_sigmoid(z)
    log_f = jnp.logaddexp(log_lb, yz)
    snz = jax.nn.sigmoid(-z)
    k = jnp.exp(log_1mlb) * snz
    w2 = jnp.exp(yz - log_f)
    return log_f, k, snz, w2


def _hg_pre(p_act, logits):
    def body(q_ref, zf_ref, zb_ref, lg_ref, qh_ref, kf_ref, bf_ref, kb_ref, bb_ref):
        qh_ref[...] = jax.nn.silu(q_ref[...])
        lf, kf, _, _ = _hg_gate_terms(zf_ref[...], lg_ref[0])
        kf_ref[...] = kf
        bf_ref[...] = jnp.dot(_chunk_tri(True), lf, precision=HI, preferred_element_type=F32)
        lb_, kb, _, _ = _hg_gate_terms(zb_ref[...], lg_ref[1])
        kb_ref[...] = kb
        bb_ref[...] = jnp.dot(_chunk_tri(False), lb_, precision=HI, preferred_element_type=F32)

    blk = lambda c: pl.BlockSpec((HG_RB, 512), lambda i: (i, c))
    ob = pl.BlockSpec((HG_RB, 512), lambda i: (i, 0))
    return pl.pallas_call(
        body, name="hg_pre", grid=(HG_NB,),
        in_specs=[blk(3), blk(4), blk(5), pl.BlockSpec((2, 2, 512), lambda i: (0, 0, 0))],
        out_specs=(ob,) * 5, out_shape=(_sds((T, 512), F32),) * 5,
        compiler_params=_cp(("parallel",)))(p_act, p_act, p_act, logits)


def _bdot(a, b, ca, cb):
    return lax.dot_general(a.astype(BF16), b.astype(BF16), (((ca,), (cb,)), ((0,), (0,))), preferred_element_type=F32)


HG_S = 8
HG_NS = HG_RB // HG_S


def _lane_sums(xs):
    l_io = lax.broadcasted_iota(jnp.int32, (HG_NS, HG_S, HG_S), 2)
    a = jnp.zeros((HG_NS, HG_S, HG_S), F32)
    for j, x in enumerate(xs):
        a = a + jnp.where(l_io == j, jnp.sum(x, axis=-1, keepdims=True), 0.0)
    return a


def _halves(x):
    y = x.reshape(8, 2, HG_S, x.shape[-1])
    return y[:, 0], y[:, 1]


def _join(first, second):
    return jnp.stack([first, second], axis=1).reshape(HG_RB, first.shape[-1])


def _cross_split(rev, b4):
    b_1, b_2 = _halves(b4)
    if rev:
        r = b_2[:, 0:1, :]
        return jnp.exp(b_1 - r), jnp.exp(r - b_2)
    r = b_1[:, HG_S - 1:HG_S, :]
    return jnp.exp(b_2 - r), jnp.exp(r - b_1)


def _hg_scan_fwd(qh, k, b, p_act, rev):
    anchor = 0 if rev else HG_C - 1

    def body(q_ref, k_ref, b_ref, v_ref, o_ref, st_ref, dsc):
        def phase_a(blk, _):
            rows = pl.ds(pl.multiple_of(blk * HG_RB, HG_RB), HG_RB)
            b3 = b_ref[rows, :].reshape(8, HG_C, 128)
            k3 = k_ref[rows, :].reshape(8, HG_C, 128)
            v3 = v_ref[rows, :].reshape(8, HG_C, 128)
            bl = b3[:, anchor:anchor + 1, :]
            kt = k3 * jnp.exp(bl - b3)
            st_ref[0, pl.ds(pl.multiple_of(blk * 8, 8), 8)] = _bdot(v3, kt, 1, 1)
            dsc[pl.ds(pl.multiple_of(blk * 8, 8), 8), :] = jnp.exp(bl[:, 0, :])
            return 0

        lax.fori_loop(0, HG_NB, phase_a, 0, unroll=HG_UNROLL)

        def phase_b(n, carry):
            c = (NCHUNK - 1 - n) if rev else n
            u = st_ref[0, c]
            st_ref[0, c] = carry
            return carry * dsc[pl.ds(c, 1), :] + u

        lax.fori_loop(0, NCHUNK // 3, lambda n3, s: phase_b(3 * n3 + 2, phase_b(3 * n3 + 1, phase_b(3 * n3, s))),
                      jnp.zeros((128, 128), F32))
        for c in range(NCHUNK, HG_SLOTS):
            st_ref[0, c] = jnp.zeros((128, 128), F32)

        t_io = lax.broadcasted_iota(jnp.int32, (HG_NS, HG_S, 128), 1)

        def phase_c(blk, _):
            rows = pl.ds(pl.multiple_of(blk * HG_RB, HG_RB), HG_RB)
            b4 = b_ref[rows, :].reshape(HG_NS, HG_S, 128)
            k4 = k_ref[rows, :].reshape(HG_NS, HG_S, 128)
            q4 = q_ref[rows, :].reshape(HG_NS, HG_S, 128)
            v4 = v_ref[rows, :].reshape(HG_NS, HG_S, 128)
            st = st_ref[0, pl.ds(pl.multiple_of(blk * 8, 8), 8)]
            o = _bdot((q4 * jnp.exp(b4)).reshape(8, HG_C, 128), st, 2, 2).reshape(HG_RB, 128)
            terms = []
            for s in range(HG_S):
                ok = (t_io <= s) if rev else (t_io >= s)
                f = jnp.exp(jnp.where(ok, b4 - b4[:, s:s + 1, :], NEG))
                terms.append(q4 * f * k4[:, s:s + 1, :])
            o_in = _bdot(_lane_sums(terms), v4, 2, 1)
            wq, wk = _cross_split(rev, b4)
            q_1, q_2 = _halves(q4)
            k_1, k_2 = _halves(k4)
            v_1, v_2 = _halves(v4)
            o_1, o_2 = _halves(o_in)
            if rev:
                o_1 = o_1 + _bdot(_bdot(q_1 * wq, k_2 * wk, 2, 2), v_2, 2, 1)
            else:
                o_2 = o_2 + _bdot(_bdot(q_2 * wq, k_1 * wk, 2, 2), v_1, 2, 1)
            o_ref[rows, :] = o + _join(o_1, o_2)
            return 0

        lax.fori_loop(0, HG_NB, phase_c, 0, unroll=HG_UNROLL)

    col = pl.BlockSpec((T, 128), lambda h: (0, h))
    return pl.pallas_call(
        body, name="hg_scan_bwd_dir" if rev else "hg_scan_fwd_dir", grid=(HG_HEADS,),
        in_specs=[col, col, col, pl.BlockSpec((T, 128), lambda h: (0, 24 + h))],
        out_specs=(col, pl.BlockSpec((1, HG_SLOTS, 128, 128), lambda h: (h, 0, 0, 0))),
        out_shape=(_sds((T, 512), F32), _sds((HG_HEADS, HG_SLOTS, 128, 128), F32)),
        scratch_shapes=[pltpu.VMEM((HG_SLOTS, 128), F32)],
        compiler_params=_cp(("parallel",), 56))(qh, k, b, p_act)


def _hg_scan_bwd(qh, k, b, p_act, st, do, rev):
    anchor = 0 if rev else HG_C - 1

    def body(q_ref, k_ref, b_ref, v_ref, st_ref, do_ref, dq_ref, dk_ref, db_ref, dv_ref, gst, dsc, dbl):
        def phase_a(blk, _):
            rows = pl.ds(pl.multiple_of(blk * HG_RB, HG_RB), HG_RB)
            b3 = b_ref[rows, :].reshape(8, HG_C, 128)
            q3 = q_ref[rows, :].reshape(8, HG_C, 128)
            do3 = do_ref[rows, :].reshape(8, HG_C, 128)
            gst[pl.ds(pl.multiple_of(blk * 8, 8), 8)] = _bdot(do3, q3 * jnp.exp(b3), 1, 1)
            dsc[pl.ds(pl.multiple_of(blk * 8, 8), 8), :] = jnp.exp(b3[:, anchor, :])
            return 0

        lax.fori_loop(0, HG_NB, phase_a, 0, unroll=HG_UNROLL)

        def phase_b(n, carry):
            c = n if rev else (NCHUNK - 1 - n)
            w = gst[c]
            gst[c] = carry
            dcv = dsc[pl.ds(c, 1), :]
            dbl[pl.ds(c, 1), :] = dcv * jnp.sum(st_ref[0, c] * carry, axis=0, keepdims=True)
            return carry * dcv + w

        lax.fori_loop(0, NCHUNK // 3, lambda n3, s: phase_b(3 * n3 + 2, phase_b(3 * n3 + 1, phase_b(3 * n3, s))),
                      jnp.zeros((128, 128), F32))
        for c in range(NCHUNK, HG_SLOTS):
            gst[c] = jnp.zeros((128, 128), F32)
            dbl[c:c + 1, :] = jnp.zeros((1, 128), F32)

        t_io = lax.broadcasted_iota(jnp.int32, (HG_NS, HG_S, 128), 1)
        t16 = lax.broadcasted_iota(jnp.int32, (8, HG_C, 128), 1)
        r_io = lax.broadcasted_iota(jnp.int32, (HG_NS, HG_S, HG_S), 1)
        l_io = lax.broadcasted_iota(jnp.int32, (HG_NS, HG_S, HG_S), 2)

        def phase_c(blk, _):
            rows = pl.ds(pl.multiple_of(blk * HG_RB, HG_RB), HG_RB)
            cs = pl.ds(pl.multiple_of(blk * 8, 8), 8)
            b4 = b_ref[rows, :].reshape(HG_NS, HG_S, 128)
            k4 = k_ref[rows, :].reshape(HG_NS, HG_S, 128)
            q4 = q_ref[rows, :].reshape(HG_NS, HG_S, 128)
            v4 = v_ref[rows, :].reshape(HG_NS, HG_S, 128)
            do4 = do_ref[rows, :].reshape(HG_NS, HG_S, 128)
            b3, k3, q3 = (z.reshape(8, HG_C, 128) for z in (b4, k4, q4))
            v3, do3 = v4.reshape(8, HG_C, 128), do4.reshape(8, HG_C, 128)
            s_t = st_ref[0, cs]
            g_t = gst[cs]
            bl = b3[:, anchor:anchor + 1, :]
            ekl = jnp.exp(bl - b3)
            kt = k3 * ekl
            dkt = _bdot(v3, g_t, 2, 1)
            dq = (_bdot(do3, s_t, 2, 1) * jnp.exp(b3)).reshape(HG_NS, HG_S, 128)
            dk = (dkt * ekl).reshape(HG_NS, HG_S, 128)
            dv = _bdot(kt, g_t, 2, 2).reshape(HG_NS, HG_S, 128)
            dbl3 = dbl[cs, :].reshape(8, 1, 128) + jnp.sum(dkt * kt, axis=1, keepdims=True)
            causal = (l_io >= r_io) if rev else (l_io <= r_io)
            da = jnp.where(causal, _bdot(do4, v4, 2, 2), 0.0)
            causal_t = (l_io <= r_io) if rev else (l_io >= r_io)
            dat = jnp.where(causal_t, _bdot(v4, do4, 2, 2), 0.0)
            for s in range(HG_S):
                ok = (t_io <= s) if rev else (t_io >= s)
                f = jnp.exp(jnp.where(ok, b4 - b4[:, s:s + 1, :], NEG))
                dq = dq + da[:, :, s:s + 1] * (f * k4[:, s:s + 1, :])
            terms = []
            for t in range(HG_S):
                ok = (t_io >= t) if rev else (t_io <= t)
                e = jnp.exp(jnp.where(ok, b4[:, t:t + 1, :] - b4, NEG))
                eq = e * q4[:, t:t + 1, :]
                dk = dk + dat[:, :, t:t + 1] * eq
                terms.append(eq * k4)
            dv = dv + _bdot(_lane_sums(terms), do4, 2, 1)
            wq, wk = _cross_split(rev, b4)
            pick = (lambda z: _halves(z)) if rev else (lambda z: _halves(z)[::-1])
            (q_q, _), (_, k_k), (_, v_k), (do_q, _) = pick(q4), pick(k4), pick(v4), pick(do4)
            qx, kx = q_q * wq, k_k * wk
            dq_q = _bdot(_bdot(do_q, v_k, 2, 2), kx, 2, 1) * wq
            dk_k = _bdot(_bdot(v_k, do_q, 2, 2), qx, 2, 1) * wk
            dv_k = _bdot(_bdot(kx, qx, 2, 2), do_q, 2, 1)
            zero = jnp.zeros((8, HG_S, 128), F32)
            place_q = (lambda z: _join(z, zero)) if rev else (lambda z: _join(zero, z))
            place_k = (lambda z: _join(zero, z)) if rev else (lambda z: _join(z, zero))
            dq2 = dq.reshape(HG_RB, 128) + place_q(dq_q)
            dk2 = dk.reshape(HG_RB, 128) + place_k(dk_k)
            dv2 = dv.reshape(HG_RB, 128) + place_k(dv_k)
            dq3, dk3 = dq2.reshape(8, HG_C, 128), dk2.reshape(8, HG_C, 128)
            db = q3 * dq3 - k3 * dk3 + jnp.where(t16 == anchor, dbl3, 0.0)
            dq_ref[rows, :] = dq2
            dk_ref[rows, :] = dk2
            db_ref[rows, :] = db.reshape(HG_RB, 128)
            dv_ref[rows, :] = dv2
            return 0

        lax.fori_loop(0, HG_NB, phase_c, 0, unroll=HG_UNROLL)

    col = pl.BlockSpec((T, 128), lambda h: (0, h))
    return pl.pallas_call(
        body, name="hg_scan_bwd_dir_bwd" if rev else "hg_scan_fwd_dir_bwd", grid=(HG_HEADS,),
        in_specs=[col, col, col, pl.BlockSpec((T, 128), lambda h: (0, 24 + h)),
                  pl.BlockSpec((1, HG_SLOTS, 128, 128), lambda h: (h, 0, 0, 0)), col],
        out_specs=(col,) * 4, out_shape=(_sds((T, 512), F32),) * 4,
        scratch_shapes=[pltpu.VMEM((HG_SLOTS, 128, 128), F32), pltpu.VMEM((HG_SLOTS, 128), F32),
                        pltpu.VMEM((HG_SLOTS, 128), F32)],
        compiler_params=_cp(("parallel",), 56))(qh, k, b, p_act, st, do)


def _row_valid(i, tm):
    r = lax.broadcasted_iota(jnp.int32, (tm, 1), 0) + i * tm
    return r < L


def _hg_post_rows(o, gv, gain_v, valid):
    parts = []
    for h in range(HG_HEADS):
        oh = o[:, 128 * h:128 * (h + 1)]
        parts.append(oh * lax.rsqrt(jnp.mean(oh * oh, axis=-1, keepdims=True) + EPS))
    return jnp.where(valid, jnp.concatenate(parts, axis=1) * gain_v * jax.nn.silu(gv), 0.0)


def _hg_post_bwd_rows(du, o, gv, gain_v, valid):
    duv = jnp.where(valid, du, 0.0)
    sig = jax.nn.sigmoid(gv)
    sg = gv * sig
    dn = duv * gain_v * sg
    do_parts, n_parts = [], []
    for h in range(HG_HEADS):
        sl = slice(128 * h, 128 * (h + 1))
        oh = o[:, sl]
        r = lax.rsqrt(jnp.mean(oh * oh, axis=-1, keepdims=True) + EPS)
        nh = oh * r
        dnh = dn[:, sl]
        do_parts.append(r * (dnh - nh * jnp.mean(dnh * nh, axis=-1, keepdims=True)))
        n_parts.append(nh)
    n = jnp.where(valid, jnp.concatenate(n_parts, axis=1), 0.0)
    do = jnp.where(valid, jnp.concatenate(do_parts, axis=1), 0.0)
    dg = duv * n * gain_v * (sig * (1.0 + gv * (1.0 - sig)))
    return do, dg, jnp.sum(duv * n * sg, axis=0, keepdims=True)


def _hg_pre_bwd(p_act, logits, dq_f, dq_b, dk_f, dk_b, db_f, db_b, dv_f, dv_b, dp_rest):
    def body(q_ref, zf_ref, zb_ref, lg_ref, dqf_ref, dqb_ref, dkf_ref, dkb_ref, dbf_ref, dbb_ref, dvf_ref, dvb_ref, _,
             dp_ref, dlg_ref):
        dq_ref, dzf_ref, dzb_ref, di_ref = (dp_ref.at[:, 512 * c:512 * (c + 1)] for c in range(4))
        i = pl.program_id(0)
        valid = _row_valid(i, HG_RB)
        qv = q_ref[...]
        sig = jax.nn.sigmoid(qv)
        dq_ref[...] = jnp.where(valid, (dqf_ref[...] + dqb_ref[...]) * (sig * (1.0 + qv * (1.0 - sig))), 0.0).astype(BF16)
        di_ref[...] = jnp.where(valid, dvf_ref[...] + dvb_ref[...], 0.0).astype(BF16)
        for d, (z_ref, dk_r, db_r, dz_ref) in enumerate(((zf_ref, dkf_ref, dbf_ref, dzf_ref), (zb_ref, dkb_ref, dbb_ref, dzb_ref))):
            lg = lg_ref[d]
            dl = lg[0:1, :] - lg[1:2, :]
            lb = jax.nn.sigmoid(dl)
            one_m_lb = jax.nn.sigmoid(-dl)
            log_f, _, snz, w2 = _hg_gate_terms(z_ref[...], lg)
            dbv = jnp.where(valid, db_r[...], 0.0)
            dkv = jnp.where(valid, dk_r[...], 0.0)
            dlf = jnp.dot(_chunk_tri(d == 1), dbv, precision=HI, preferred_element_type=F32)
            sz = 1.0 - snz
            dz_ref[...] = (dlf * w2 * snz - dkv * one_m_lb * sz * snz).astype(BF16)
            dlb = jnp.sum(dlf * snz * jnp.exp(-log_f) - dkv * snz, axis=0, keepdims=True)
            dl0 = dlb * lb * one_m_lb
            part = jnp.concatenate([dl0, -dl0], axis=0)

            @pl.when(i == 0)
            def _():
                dlg_ref[d] = part

            @pl.when(i > 0)
            def _():
                dlg_ref[d] += part

    blk = lambda c: pl.BlockSpec((HG_RB, 512), lambda i: (i, c))
    ob = pl.BlockSpec((HG_RB, 512), lambda i: (i, 0))
    lgs = pl.BlockSpec((2, 2, 512), lambda i: (0, 0, 0))
    return pl.pallas_call(
        body, name="hg_pre_bwd", grid=(HG_NB,),
        in_specs=[blk(3), blk(4), blk(5), lgs] + [ob] * 8 + [ANY],
        out_specs=(pl.BlockSpec((HG_RB, 2048), lambda i: (i, 0)), lgs),
        out_shape=(_sds(dp_rest.shape, BF16), _sds((2, 2, 512), F32)), input_output_aliases={12: 0},
        compiler_params=_cp(("arbitrary",)))(p_act, p_act, p_act, logits, dq_f, dq_b, dk_f, dk_b, db_f, db_b, dv_f, dv_b,
                                             dp_rest)


def _mix_fwd(o_na, o_f, o_b, gain, w_na, w_hg, p_act):
    def body(ona_ref, of_ref, ob_ref, g_ref, gain_ref, wna_ref, whg_ref, gna_ref, ghg_ref, o_ref, u_ref):
        u = _hg_post_rows(of_ref[...] + ob_ref[...], g_ref[...], gain_ref[...], _row_valid(pl.program_id(0), TM_B)).astype(BF16)
        u_ref[...] = u
        y_na = _dot(ona_ref[...], wna_ref[...])
        y_hg = _dot(u, whg_ref[...])
        o_ref[...] = (jax.nn.sigmoid(gna_ref[...]) * y_na + jax.nn.sigmoid(ghg_ref[...]) * y_hg).astype(BF16)

    act = pl.BlockSpec((TM_B, 512), lambda i: (i, 0))
    wsp = pl.BlockSpec((512, D), lambda i: (0, 0))
    return pl.pallas_call(
        body, name="mix_fwd", grid=(T // TM_B,),
        in_specs=[act, act, act, pl.BlockSpec((TM_B, 512), lambda i: (i, 7)), pl.BlockSpec((1, 512), lambda i: (0, 0)),
                  wsp, wsp, pl.BlockSpec((TM_B, D), lambda i: (i, 4)), pl.BlockSpec((TM_B, D), lambda i: (i, 5))],
        out_specs=(pl.BlockSpec((TM_B, D), lambda i: (i, 0)), act), out_shape=(_sds((T, D), BF16), _sds((T, 512), BF16)),
        compiler_params=_cp(("parallel",)))(o_na, o_f, o_b, p_act, gain, w_na, w_hg, p_act, p_act)


DP_REST = IN_COLS - 1536


def _mix_bwd(o_na, u_hg, o_f, o_b, gain, w_na, w_hg, p_act, dmix):
    ni = T // TM_B

    def body(ona_ref, uhg_ref, of_ref, ob_ref, g_ref, gain_ref, wna_ref, whg_ref, gna_ref, ghg_ref, dmix_ref,
             dp_ref, dwna_ref, dwhg_ref, dona_ref, do_ref, dgain_ref, acc_na, acc_hg):
        i = pl.program_id(0)
        dg_ref, dgna_ref, dghg_ref = dp_ref.at[:, 2048:2560], dp_ref.at[:, 2560:3584], dp_ref.at[:, 3584:4608]
        dm = dmix_ref[...].astype(F32)
        dxs = []
        for x_ref, w_ref, gt_ref, dgt_ref, dw_ref, acc in (
                (ona_ref, wna_ref, gna_ref, dgna_ref, dwna_ref, acc_na), (uhg_ref, whg_ref, ghg_ref, dghg_ref, dwhg_ref, acc_hg)):
            xv = x_ref[...]
            y = _dot(xv, w_ref[...])
            sg = jax.nn.sigmoid(gt_ref[...])
            dgt_ref[...] = (dm * y * sg * (1.0 - sg)).astype(BF16)
            dy = (dm * sg).astype(BF16)
            dxs.append(_dot(dy, w_ref[...], NT))
            part = _dot(xv, dy, TN)

            @pl.when(i == 0)
            def _():
                acc[...] = part

            @pl.when(i > 0)
            def _():
                acc[...] += part

            @pl.when(i == ni - 1)
            def _():
                dw_ref[...] = acc[...].astype(BF16)

        dona_ref[...] = dxs[0]
        do, dg, gpart = _hg_post_bwd_rows(dxs[1], of_ref[...] + ob_ref[...], g_ref[...], gain_ref[...], _row_valid(i, TM_B))
        do_ref[...] = do
        dg_ref[...] = dg.astype(BF16)

        @pl.when(i == 0)
        def _():
            dgain_ref[...] = gpart

        @pl.when(i > 0)
        def _():
            dgain_ref[...] += gpart

    act = pl.BlockSpec((TM_B, 512), lambda i: (i, 0))
    wsp = pl.BlockSpec((512, D), lambda i: (0, 0))
    rblk = pl.BlockSpec((TM_B, D), lambda i: (i, 0))
    vec = pl.BlockSpec((1, 512), lambda i: (0, 0))
    return pl.pallas_call(
        body, name="mix_bwd", grid=(ni,),
        in_specs=[act, act, act, act, pl.BlockSpec((TM_B, 512), lambda i: (i, 7)), vec, wsp, wsp,
                  pl.BlockSpec((TM_B, D), lambda i: (i, 4)), pl.BlockSpec((TM_B, D), lambda i: (i, 5)), rblk],
        out_specs=(pl.BlockSpec((TM_B, DP_REST), lambda i: (i, 0)), wsp, wsp, act, act, vec),
        out_shape=(_sds((T, DP_REST), BF16), _sds((512, D), BF16), _sds((512, D), BF16),
                   _sds((T, 512), F32), _sds((T, 512), F32), _sds((1, 512), F32)),
        scratch_shapes=[pltpu.VMEM((512, D), F32), pltpu.VMEM((512, D), F32)],
        compiler_params=_cp(("arbitrary",)))(o_na, u_hg, o_f, o_b, p_act, gain, w_na, w_hg, p_act, p_act, dmix)


def _wo_fwd(mix, w_o, h0, g_mlp):
    def body(mix_ref, w_ref, h0_ref, g_ref, h1_ref, m_ref):
        h1 = h0_ref[...] + _dot(mix_ref[...], w_ref[...])
        h1_ref[...] = h1
        r = lax.rsqrt(jnp.mean(h1 * h1, axis=-1, keepdims=True) + EPS)
        m_ref[...] = (h1 * r * g_ref[...]).astype(BF16)

    blk = pl.BlockSpec((TM_B, D), lambda i: (i, 0))
    return pl.pallas_call(
        body, name="wo_fwd", grid=(T // TM_B,),
        in_specs=[blk, pl.BlockSpec((D, D), lambda i: (0, 0)), blk, pl.BlockSpec((1, D), lambda i: (0, 0))],
        out_specs=(blk, blk), out_shape=(_sds((T, D), F32), _sds((T, D), BF16)),
        compiler_params=_cp(("parallel",)))(mix, w_o, h0, g_mlp)


def _wo_bwd(dh1_b, w_o, mix):
    ni = T // TM_B

    def body(dh_ref, w_ref, mix_ref, dmix_ref, dw_ref, acc):
        i = pl.program_id(0)
        dh = dh_ref[...]
        dmix_ref[...] = _dot(dh, w_ref[...], NT).astype(BF16)
        part = _dot(mix_ref[...], dh, TN)

        @pl.when(i == 0)
        def _():
            acc[...] = part

        @pl.when(i > 0)
        def _():
            acc[...] += part

        @pl.when(i == ni - 1)
        def _():
            dw_ref[...] = acc[...].astype(BF16)

    blk = pl.BlockSpec((TM_B, D), lambda i: (i, 0))
    wsp = pl.BlockSpec((D, D), lambda i: (0, 0))
    return pl.pallas_call(
        body, name="wo_bwd", grid=(ni,), in_specs=[blk, wsp, blk], out_specs=(blk, wsp),
        out_shape=(_sds((T, D), BF16), _sds((D, D), BF16)), scratch_shapes=[pltpu.VMEM((D, D), F32)],
        compiler_params=_cp(("arbitrary",)))(dh1_b, w_o, mix)


FF_B = D_FF // NDEV


def _loss_rows(xv, gv, tv, row0):
    r_io = lax.broadcasted_iota(jnp.int32, (xv.shape[0], 1), 0) + row0
    valid = (r_io >= NM) & (r_io < L)
    r = lax.rsqrt(jnp.mean(xv * xv, axis=-1, keepdims=True) + EPS)
    xh = xv * r
    err = jnp.where(valid, xh * gv - tv, 0.0)
    lpart = 0.5 * jnp.sum(jnp.sum(err * err, axis=-1, keepdims=True) * (1.0 / D), axis=0, keepdims=True)
    dy = err * (1.0 / D)
    dxh = dy * gv
    dh = r * (dxh - xh * jnp.mean(dxh * xh, axis=-1, keepdims=True))
    return lpart, dh, jnp.sum(dy * xh, axis=0, keepdims=True)


def _mlp_fwd_loss(m, wup_g, wdown_g, h1, g_final, tgt):
    nsub = TM_MM // TM_E

    def body(m_ref, wu_ref, wd_ref, h1_ref, g_ref, t_ref, loss_ref, dh_ref, dhb_ref, dg_ref, h2):
        i, j = pl.program_id(0), pl.program_id(1)
        up = jnp.maximum(_dot(m_ref[...], wu_ref[0]), 0.0)
        part = _dot((up * up).astype(BF16), wd_ref[0])

        @pl.when(j == 0)
        def _():
            h2[...] = h1_ref[...] + part

        @pl.when(j > 0)
        def _():
            h2[...] += part

        @pl.when(j == NDEV - 1)
        def _():
            lsum = jnp.zeros((1, 1), F32)
            gsum = jnp.zeros((1, D), F32)
            for s in range(nsub):
                rows = slice(s * TM_E, (s + 1) * TM_E)
                lpart, dh, gpart = _loss_rows(h2[rows, :], g_ref[...], t_ref[rows, :], i * TM_MM + s * TM_E)
                dh_ref[rows, :] = dh
                dhb_ref[rows, :] = dh.astype(BF16)
                lsum = lsum + lpart
                gsum = gsum + gpart
            lsum = jnp.broadcast_to(lsum, (1, 128))

            @pl.when(i == 0)
            def _():
                loss_ref[...] = lsum
                dg_ref[...] = gsum

            @pl.when(i > 0)
            def _():
                loss_ref[...] += lsum
                dg_ref[...] += gsum

    blk = pl.BlockSpec((TM_MM, D), lambda i, j: (i, 0))
    vec = pl.BlockSpec((1, D), lambda i, j: (0, 0))
    return pl.pallas_call(
        body, name="mlp_fwd_loss", grid=(T // TM_MM, NDEV),
        in_specs=[blk, pl.BlockSpec((1, D, FF_B), lambda i, j: (j, 0, 0)), pl.BlockSpec((1, FF_B, D), lambda i, j: (j, 0, 0)),
                  blk, vec, blk],
        out_specs=(pl.BlockSpec((1, 128), lambda i, j: (0, 0)), blk, blk, vec),
        out_shape=(_sds((1, 128), F32), _sds((T, D), F32), _sds((T, D), BF16), _sds((1, D), F32)),
        scratch_shapes=[pltpu.VMEM((TM_MM, D), F32)],
        compiler_params=_cp(("arbitrary", "arbitrary"), 56))(m, wup_g, wdown_g, h1, g_final, tgt)


def _mlp_bwd(m, dh2_b, wup_g, wdown_g, h1, g_mlp, dh2):
    ni = T // TM_B
    nsub = TM_B // TM_E

    def body(m_ref, dh_ref, wu_ref, wd_ref, h1_ref, g_ref, dres_ref, dwu_ref, dwd_ref, dh1_ref, dh1b_ref, dg_ref,
             dm_ref, acc_u, acc_d):
        j, i = pl.program_id(0), pl.program_id(1)
        rows = pl.ds(pl.multiple_of(i * TM_B, TM_B), TM_B)
        mv, dh = m_ref[...], dh_ref[...]
        r = jnp.maximum(_dot(mv, wu_ref[0]), 0.0)
        act = (r * r).astype(BF16)
        dact = _dot(dh, wd_ref[0], NT)
        dup = (dact * (2.0 * r)).astype(BF16)
        pd = _dot(act, dh, TN)
        pu = _dot(mv, dup, TN)
        dmv = _dot(dup, wu_ref[0], NT)

        @pl.when(i == 0)
        def _():
            acc_u[...] = pu
            acc_d[...] = pd

        @pl.when(i > 0)
        def _():
            acc_u[...] += pu
            acc_d[...] += pd

        @pl.when(i == ni - 1)
        def _():
            dwu_ref[0] = acc_u[...].astype(BF16)
            dwd_ref[0] = acc_d[...].astype(BF16)

        @pl.when(j == 0)
        def _():
            dm_ref[rows, :] = dmv

        @pl.when(j > 0)
        def _():
            dm_ref[rows, :] += dmv

        @pl.when(j == NDEV - 1)
        def _():
            gsum = jnp.zeros((1, D), F32)
            for s in range(nsub):
                sub = slice(s * TM_E, (s + 1) * TM_E)
                dm_rows = dm_ref[pl.ds(pl.multiple_of(i * TM_B + s * TM_E, TM_E), TM_E), :]
                dx, gpart = _norm_bwd_rows(h1_ref[sub, :], g_ref[...], dm_rows, dres_ref[sub, :])
                dh1_ref[sub, :] = dx
                dh1b_ref[sub, :] = dx.astype(BF16)
                gsum = gsum + gpart

            @pl.when(i == 0)
            def _():
                dg_ref[...] = gsum

            @pl.when(i > 0)
            def _():
                dg_ref[...] += gsum

    blk = pl.BlockSpec((TM_B, D), lambda j, i: (i, 0))
    late = pl.BlockSpec((TM_B, D), lambda j, i: (jnp.where(j == NDEV - 1, i, 0), 0))
    vec = pl.BlockSpec((1, D), lambda j, i: (0, 0))
    wus = pl.BlockSpec((1, D, FF_B), lambda j, i: (j, 0, 0))
    wds = pl.BlockSpec((1, FF_B, D), lambda j, i: (j, 0, 0))
    return pl.pallas_call(
        body, name="mlp_bwd", grid=(NDEV, ni), in_specs=[blk, blk, wus, wds, late, vec, late],
        out_specs=(wus, wds, late, late, vec),
        out_shape=(_sds((NDEV, D, FF_B), BF16), _sds((NDEV, FF_B, D), BF16), _sds((T, D), F32), _sds((T, D), BF16),
                   _sds((1, D), F32)),
        scratch_shapes=[pltpu.VMEM((T, D), F32), pltpu.VMEM((D, FF_B), F32), pltpu.VMEM((FF_B, D), F32)],
        compiler_params=_cp(("arbitrary", "arbitrary"), 56))(m, dh2_b, wup_g, wdown_g, h1, g_mlp, dh2)


def _adamw(parts, w, m, v, name):
    rr, cc = w.shape
    tr = rr
    for cand in (256, 128, 64):
        if rr % cand == 0 and rr > cand:
            tr = cand
            break
    c1 = 1.0 - ADAM_B1 ** ADAM_STEP
    c2 = 1.0 - ADAM_B2 ** ADAM_STEP

    def body(p_ref, w_ref, m_ref, v_ref, g_ref, d_ref, nm_ref, nv_ref):
        g = p_ref[0].astype(F32)
        for s in range(1, NDEV):
            g = g + p_ref[s].astype(F32)
        mn = ADAM_B1 * m_ref[...] + (1.0 - ADAM_B1) * g
        vn = ADAM_B2 * v_ref[...] + (1.0 - ADAM_B2) * (g * g)
        g_ref[...] = g
        nm_ref[...] = mn
        nv_ref[...] = vn
        d_ref[...] = -ADAM_LR * ((mn / c1) / (jnp.sqrt(vn / c2) + ADAM_EPS) + ADAM_WD * w_ref[...])

    blk = pl.BlockSpec((tr, cc), lambda i: (i, 0))
    return pl.pallas_call(
        body, name=name, grid=(rr // tr,),
        in_specs=[pl.BlockSpec((NDEV, tr, cc), lambda i: (0, i, 0)), blk, blk, blk],
        out_specs=(blk,) * 4, out_shape=(_sds((rr, cc), F32),) * 4,
        compiler_params=_cp(("parallel",)))(parts, w, m, v)


RPB_N = NA_HEADS * 15 * 31
RPB_PAD = 4096
OWN_ROWS = NM + 8


def _pad_rows(a, rows):
    return jnp.pad(a, ((0, rows - a.shape[0]),) + ((0, 0),) * (a.ndim - 1))


def _pack_owned(meta_blk, lb_blk):
    return jnp.concatenate([meta_blk, _pad_rows(lb_blk.reshape(2, 128), 8)], axis=0)


LOSS_ROW = 28


def _pack_replicated(n_mix, n_mlp, n_final, hg_gain, rpb, loss_row=None):
    flat = _pad_rows(rpb.reshape(RPB_N), RPB_PAD)
    gain8 = _pad_rows(hg_gain.reshape(4, 128), 8)
    if loss_row is not None:
        gain8 = gain8 + jnp.pad(loss_row, ((LOSS_ROW - 24, 31 - LOSS_ROW), (0, 0)))
    return jnp.concatenate([n_mix.reshape(8, 128), n_mlp.reshape(8, 128), n_final.reshape(8, 128), gain8,
                            flat.reshape(32, 128)], axis=0)


def _unpack_replicated(a):
    return (a[0:8].reshape(1, D), a[8:16].reshape(1, D), a[16:24].reshape(D), a[24:28].reshape(1, 512),
            a[32:64].reshape(RPB_PAD)[:RPB_N].reshape(1, NA_HEADS, 15, 31))


def kernel(x, meta_tokens, w_in, w_na_out, w_hg_out, w_o, w_up, w_down, norm_mix, norm_mlp, norm_final, hg_norm, na_rpb, hg_lb_logits, loss_target, m_meta_tokens, m_w_in, m_w_na_out, m_w_hg_out, m_w_o, m_w_up, m_w_down, m_norm_mix, m_norm_mlp, m_norm_final, m_hg_norm, m_na_rpb, m_hg_lb_logits, v_meta_tokens, v_w_in, v_w_na_out, v_w_hg_out, v_w_o, v_w_up, v_w_down, v_norm_mix, v_norm_mlp, v_norm_final, v_hg_norm, v_na_rpb, v_hg_lb_logits):
    owned = _pack_owned(meta_tokens, hg_lb_logits)
    first_masks = (ALL_PEERS, SAME_CORE_AND_SIBLING)
    first, tok = _exchange_start([owned, w_in[0].astype(BF16)], [False] * 2, "gather_first_start", first_masks)
    bias_tab = _na_bias_table(_tie(jnp.pad(na_rpb[0], ((0, 0), (0, 0), (0, 128 - 31))), tok, "tie_bias_table"))
    later = [w[0].astype(BF16) for w in (w_na_out, w_hg_out, w_o, w_up, w_down)]
    lead = jnp.zeros((NM, D), F32) + tok[0, 0]
    h0_rows = jnp.concatenate([lead, x[0], jnp.zeros((T - L, D), F32)], axis=0)
    tgt = jnp.concatenate([lead, loss_target[0], jnp.zeros((T - L, D), F32)], axis=0)
    (owned_g, _), first = _exchange_wait(first, [False] * 2, [h0_rows], "gather_small_wait", first_masks, which=(0,))
    meta_full = jnp.transpose(owned_g[:, 0:NM, :], (1, 0, 2)).reshape(NM, D)
    logits = jnp.transpose(owned_g[:, NM:NM + 2, :].reshape(NDEV, 2, 2, 64), (1, 2, 0, 3)).reshape(2, 2, 512)
    h0 = lax.dynamic_update_slice(h0_rows, meta_full, (0, 0))
    a, a_t = _norm_fwd_t(h0, norm_mix, "norm_mix_fwd")
    (_, win_l), _ = _exchange_wait(first, [False] * 2, [a, logits, tgt, bias_tab] + later, "gather_first_wait", first_masks,
                                   which=(1,))
    (win_g,) = _forward_to_sibling([win_l], "gather_first_forward")
    later[0] = _tie(later[0], win_g, "tie_gather_rest")
    gather_rest, tok = _exchange_start(later, [False] * 5, "gather_rest_start")
    win_g = _tie(win_g, tok, "tie_inproj")

    p_act = _inproj_fwd(a, win_g)
    o_na, lse = _na_fwd(p_act, bias_tab)
    qh, k_f, b_f, k_b, b_b = _hg_pre(p_act, logits)
    o_f, st_f = _hg_scan_fwd(qh, k_f, b_f, p_act, False)
    o_b, st_b = _hg_scan_fwd(qh, k_b, b_b, p_act, True)
    (wna_g, whg_g, wo_g, _, _), gather_rest = _exchange_wait(
        gather_rest, [False] * 5, [o_f, o_b, o_na], "gather_rest_wait_a", which=(0, 1, 2))
    w_na_full = jnp.transpose(wna_g, (1, 0, 2)).reshape(512, D)
    w_hg_full = jnp.transpose(whg_g, (1, 0, 2)).reshape(512, D)
    mix, u_hg = _mix_fwd(o_na, o_f, o_b, hg_norm, w_na_full, w_hg_full, p_act)
    h1, m_act = _wo_fwd(mix, wo_g.reshape(D, D), h0, norm_mlp)
    (_, _, wo_g, wup_g, wdown_g), _ = _exchange_wait(gather_rest, [False] * 5, [m_act], "gather_rest_wait_b", which=(3, 4))
    w_o_full = wo_g.reshape(D, D)
    loss_part, dh2, dh2_b, d_nfinal = _mlp_fwd_loss(m_act, wup_g, wdown_g, h1, norm_final.reshape(1, D), tgt)

    dwup_p, dwdown_p, dh1, dh1_b, d_nmlp = _mlp_bwd(m_act, dh2_b, wup_g, wdown_g, h1, norm_mlp, dh2)
    sc_mlp, tok = _exchange_start([dwup_p, dwdown_p], [True] * 2, "scatter_mlp_start")
    dmix, dwo = _wo_bwd(_tie(dh1_b, tok, "tie_wo_bwd"), w_o_full, mix)
    sc_wo, tok = _exchange_start([dwo.reshape(NDEV, D // NDEV, D)], [True], "scatter_wo_start")
    dp_rest, dwna, dwhg, do_na, do_hg, d_gain = _mix_bwd(
        o_na, u_hg, o_f, o_b, hg_norm, w_na_full, w_hg_full, p_act, _tie(dmix, tok, "tie_mix_bwd"))
    owner_cols = lambda w: jnp.transpose(w.reshape(512, NDEV, D // NDEV), (1, 0, 2))
    sc_br, tok = _exchange_start([owner_cols(dwna), owner_cols(dwhg)], [True] * 2, "scatter_branch_start")
    do_hg = _tie(do_hg, tok, "tie_hg_scan_bwd")
    dq_f, dk_f, db_f, dv_f = _hg_scan_bwd(qh, k_f, b_f, p_act, st_f, do_hg, False)
    dq_b, dk_b, db_b, dv_b = _hg_scan_bwd(qh, k_b, b_b, p_act, st_b, do_hg, True)
    dp_rest, d_logits = _hg_pre_bwd(p_act, logits, dq_f, dq_b, dk_f, dk_b, db_f, db_b, dv_f, dv_b, dp_rest)
    dq_na, dk_na, dv_na, dbias = _na_bwd(p_act, do_na, lse, bias_tab)
    dp_na = jnp.concatenate([dq_na.astype(BF16), dk_na.astype(BF16), dv_na.astype(BF16)], axis=1)
    dwin_p = _inproj_bwd_dw(a_t, dp_na, dp_rest)
    sc_in, tok = _exchange_start([dwin_p], [True], "scatter_in_start")
    dh0, d_nmix = _inproj_bwd_da(_tie(dp_na, tok, "tie_inproj_bwd_da"), dp_rest, win_g, h0, norm_mix, dh1)
    d_rpb = _na_rpb_reduce(_tie(dbias, tok, "tie_rpb_reduce"))[:, :, :31]

    res = {}

    def update(nm, parts, w, mm, vv):
        res[nm] = [r[None] for r in _adamw(parts, w[0], mm[0], vv[0], "adamw_" + nm)]
        return res[nm][1]

    wup_r, wdown_r = _exchange_wait(sc_mlp, [True] * 2, [dh0, d_rpb], "scatter_mlp_wait")
    update("w_up", wup_r, w_up, m_w_up, v_w_up)
    last = update("w_down", wdown_r, w_down, m_w_down, v_w_down)
    (wo_r,) = _exchange_wait(sc_wo, [True], [last], "scatter_wo_wait")
    last = update("w_o", wo_r, w_o, m_w_o, v_w_o)
    wna_r, whg_r = _exchange_wait(sc_br, [True] * 2, [last], "scatter_branch_wait")
    update("w_na_out", wna_r, w_na_out, m_w_na_out, v_w_na_out)
    last = update("w_hg_out", whg_r, w_hg_out, m_w_hg_out, v_w_hg_out)

    d_meta = jnp.transpose(dh0[0:NM].reshape(NM, NDEV, 128), (1, 0, 2))
    d_lg = jnp.transpose(d_logits.reshape(2, 2, NDEV, 64), (2, 0, 1, 3)).reshape(NDEV, 2, 128)
    owned_p = jnp.concatenate([d_meta, jnp.pad(d_lg, ((0, 0), (0, OWN_ROWS - NM - 2), (0, 0)))], axis=1)
    repl_p = _pack_replicated(d_nmix, d_nmlp, d_nfinal, d_gain, d_rpb, loss_part)
    grad_x = dh0[NM:L][None]
    done_first = [grad_x] + [res[nm][0] for nm in ("w_up", "w_down", "w_o", "w_na_out", "w_hg_out")]
    owned_r, repl_r = _exchange([owned_p, repl_p], [True, False], "scatter_small", done_first)
    own = _adamw(owned_r, owned, _pack_owned(m_meta_tokens, m_hg_lb_logits), _pack_owned(v_meta_tokens, v_hg_lb_logits),
                 "adamw_owned_small")
    res["meta_tokens"] = [r[0:NM] for r in own]
    res["hg_lb_logits"] = [r[NM:NM + 2].reshape(2, 2, 64) for r in own]
    rep = _adamw(repl_r, _pack_replicated(norm_mix, norm_mlp, norm_final, hg_norm, na_rpb),
                 _pack_replicated(m_norm_mix, m_norm_mlp, m_norm_final, m_hg_norm, m_na_rpb),
                 _pack_replicated(v_norm_mix, v_norm_mlp, v_norm_final, v_hg_norm, v_na_rpb), "adamw_replicated")
    for q in range(4):
        um = _unpack_replicated(rep[q])
        for nm, val in zip(("norm_mix", "norm_mlp", "norm_final", "hg_norm", "na_rpb"), um):
            res.setdefault(nm, [None] * 4)[q] = val
    (win_r,) = _exchange_wait(sc_in, [True], [rep[1], own[1]], "scatter_in_wait")
    update("w_in", win_r, w_in, m_w_in, v_w_in)

    loss = jnp.sum(repl_r[:, LOSS_ROW, 0])
    order = ("meta_tokens", "w_in", "w_na_out", "w_hg_out", "w_o", "w_up", "w_down", "norm_mix", "norm_mlp", "norm_final",
             "hg_norm", "na_rpb", "hg_lb_logits")
    outs = [loss, grad_x]
    for q in range(4):
        outs += [res[nm][q] for nm in order]
    return tuple(outs)
```

```python
import functools

import numpy as np
import jax
import jax.numpy as jnp
from jax import lax
from jax.experimental import pallas as pl
from jax.experimental.pallas import tpu as pltpu

F32 = jnp.float32
BF16 = jnp.bfloat16

D = 1024
SEQ = 2048
NM = 16
L = SEQ + NM
T = 2176
NDEV = 8
EPS = 1e-6
GRID_W = 64
ROWS = SEQ // GRID_W
NA_HEADS = 8
NA_DH = 64
NA_SCALE = NA_DH ** -0.5
HG_HEADS = 4
HG_C = 16
NCHUNK = L // HG_C
D_FF = 4096
IN_COLS = 6144
NEG = -1e30

ADAM_LR = 0.001
ADAM_B1 = 0.9
ADAM_B2 = 0.999
ADAM_EPS = 1e-08
ADAM_WD = 0.01
ADAM_STEP = 10

MESH_ID = pl.DeviceIdType.MESH
ANY = pl.BlockSpec(memory_space=pl.ANY)

NN = (((1,), (0,)), ((), ()))
NT = (((1,), (1,)), ((), ()))
TN = (((0,), (0,)), ((), ()))


def _cp(sem=None, vmem_mb=48):
    return pltpu.CompilerParams(dimension_semantics=sem, vmem_limit_bytes=vmem_mb * 1024 * 1024)


def _dot(a, b, dims=NN):
    return lax.dot_general(a, b, dims, preferred_element_type=F32)


def _sds(shape, dtype):
    return jax.ShapeDtypeStruct(shape, dtype)


HBM = pl.BlockSpec(memory_space=pltpu.HBM)
SEM = pl.BlockSpec(memory_space=pltpu.SEMAPHORE)
EFFECT = pltpu.SideEffectType.DATAFLOW_SIDE_EFFECTING


def _exchange(arrs, scatter, name, after=()):
    n = len(arrs)
    after = list(after)
    out_shapes = []
    for a, sc in zip(arrs, scatter):
        out_shapes.append(_sds(a.shape if sc else (NDEV,) + a.shape, a.dtype))

    def body(*refs):
        ins, outs = refs[:n], refs[n + len(after):2 * n + len(after)]
        send_sems, recv_sems, loc_sems = refs[2 * n + len(after):]
        me = 4 * lax.axis_index("x") + 2 * lax.axis_index("y") + lax.axis_index("c")
        copies = []
        for k in range(n):
            src_me = ins[k].at[me] if scatter[k] else ins[k]
            loc = pltpu.make_async_copy(src_me, outs[k].at[me], loc_sems.at[k])
            loc.start()
            copies.append(loc)
        remote = sum(_peer_copies(ins, outs, scatter, send_sems, recv_sems), [])
        for cp in remote:
            cp.start()
        for cp in remote:
            cp.wait_recv()
        for cp in remote:
            cp.wait_send()
        for cp in copies:
            cp.wait()

    return pl.pallas_call(
        body, name=name, out_shape=tuple(out_shapes), in_specs=[ANY] * (n + len(after)), out_specs=tuple([ANY] * n),
        scratch_shapes=[pltpu.SemaphoreType.DMA((n * (NDEV - 1),)), pltpu.SemaphoreType.DMA((n * (NDEV - 1),)),
                        pltpu.SemaphoreType.DMA((n,))],
    )(*arrs, *after)


def _forward_to_sibling(bufs, name):
    n = len(bufs)

    def body(*refs):
        ins, outs = refs[:n], refs[n:2 * n]
        send_sems, recv_sems = refs[2 * n:]
        x, y, c = lax.axis_index("x"), lax.axis_index("y"), lax.axis_index("c")
        copies = []
        for k in range(n):
            for j, (cx, cy) in enumerate(((1 - x, y), (x, 1 - y), (1 - x, 1 - y))):
                slot = 4 * cx + 2 * cy + c
                copies.append(pltpu.make_async_remote_copy(
                    src_ref=ins[k].at[slot], dst_ref=outs[k].at[slot], send_sem=send_sems.at[3 * k + j],
                    recv_sem=recv_sems.at[3 * k + j], device_id=(x, y, 1 - c), device_id_type=MESH_ID))
        for cp in copies:
            cp.start()
        for cp in copies:
            cp.wait_recv()
        for cp in copies:
            cp.wait_send()

    return pl.pallas_call(
        body, name=name, out_shape=tuple(_sds(b.shape, b.dtype) for b in bufs), in_specs=[ANY] * n,
        out_specs=tuple([ANY] * n), input_output_aliases={k: k for k in range(n)},
        scratch_shapes=[pltpu.SemaphoreType.DMA((3 * n,)), pltpu.SemaphoreType.DMA((3 * n,))],
    )(*bufs)


ALL_PEERS = tuple(range(1, NDEV))
SAME_CORE_AND_SIBLING = (1, 2, 4, 6)
SAME_CORE = (2, 4, 6)
BY_CHIP = "by chip"


def _sibling_swap(x, name):
    def body(x_ref, o_ref, send_sem, recv_sem):
        sib = (lax.axis_index("x"), lax.axis_index("y"), 1 - lax.axis_index("c"))
        cp = pltpu.make_async_remote_copy(src_ref=x_ref, dst_ref=o_ref, send_sem=send_sem, recv_sem=recv_sem,
                                          device_id=sib, device_id_type=MESH_ID)
        cp.start()
        cp.wait()

    return pl.pallas_call(
        body, name=name, out_shape=_sds(x.shape, x.dtype), in_specs=[ANY], out_specs=ANY,
        scratch_shapes=[pltpu.SemaphoreType.DMA(()), pltpu.SemaphoreType.DMA(())])(x)


def _add_bf16(a, b, name):
    n, rr, cc = a.shape

    def body(a_ref, b_ref, o_ref):
        o_ref[...] = (a_ref[...].astype(F32) + b_ref[...].astype(F32)).astype(BF16)

    blk = pl.BlockSpec((1, rr // 2, cc), lambda i, j: (i, j, 0))
    return pl.pallas_call(body, name=name, grid=(n, 2), in_specs=[blk, blk], out_specs=blk, out_shape=_sds(a.shape, BF16),
                          compiler_params=_cp(("parallel", "parallel")))(a, b)


def _peer_copies(srcs, lands, scatter, send_sems, recv_sems, masks=ALL_PEERS):
    x, y, c = lax.axis_index("x"), lax.axis_index("y"), lax.axis_index("c")
    me = 4 * x + 2 * y + c
    out = []
    for k in range(len(srcs)):
        out.append([])
        for m in (masks[k] if isinstance(masks[0], tuple) else masks):
            px, py, pc = x ^ (m >> 2), y ^ ((m >> 1) & 1), c ^ (m & 1)
            if scatter[k] == BY_CHIP:
                src, slot = srcs[k].at[2 * px + py], 2 * x + y
            else:
                src, slot = (srcs[k].at[4 * px + 2 * py + pc] if scatter[k] else srcs[k]), me
            out[k].append(pltpu.make_async_remote_copy(
                src_ref=src, dst_ref=lands[k].at[slot], send_sem=send_sems.at[k * (NDEV - 1) + m - 1],
                recv_sem=recv_sems.at[k * (NDEV - 1) + m - 1],
                device_id=(px, py, pc), device_id_type=MESH_ID))
    return out


def _exchange_start(arrs, scatter, name, masks=ALL_PEERS):
    n = len(arrs)
    me = 4 * lax.axis_index("x") + 2 * lax.axis_index("y") + lax.axis_index("c")
    lands = []
    for a, sc in zip(arrs, scatter):
        slot = (2 * lax.axis_index("x") + lax.axis_index("y")) if sc == BY_CHIP else me
        own = lax.dynamic_index_in_dim(a, slot, 0, keepdims=True) if sc else a[None]
        shape = a.shape if sc else (NDEV,) + a.shape
        lands.append(lax.dynamic_update_index_in_dim(lax.empty(shape, a.dtype), own, slot, 0))

    def body(*refs):
        srcs, lnds = refs[:n], refs[n:2 * n]
        send_sems, recv_sems = refs[2 * n], refs[2 * n + 1]
        token = refs[-1]
        for cp in sum(_peer_copies(srcs, lnds, scatter, send_sems, recv_sems, masks), []):
            cp.start()
        token[...] = jnp.zeros_like(token)

    ops = [pltpu.with_memory_space_constraint(a, pltpu.HBM) for a in list(arrs) + lands]
    res = pl.pallas_call(
        body, name=name,
        out_shape=(pltpu.SemaphoreType.DMA((n * (NDEV - 1),)), pltpu.SemaphoreType.DMA((n * (NDEV - 1),)))
        + tuple(pltpu.HBM(o.shape, o.dtype) for o in ops) + (_sds((8, 128), F32),),
        in_specs=[HBM] * (2 * n), out_specs=(SEM, SEM) + (HBM,) * (2 * n) + (pl.BlockSpec(memory_space=pltpu.VMEM),),
        input_output_aliases={k: 2 + k for k in range(2 * n)},
        compiler_params=pltpu.CompilerParams(has_side_effects=EFFECT),
    )(*ops)
    return res[:-1], res[-1]


def _exchange_wait(handle, scatter, after, name, masks=ALL_PEERS, which=None):
    send_sems, recv_sems = handle[0], handle[1]
    bufs = handle[2:]
    n = len(bufs) // 2
    after = list(after)

    def body(*refs):
        srcs, lnds = refs[:n], refs[n:2 * n]
        copies = _peer_copies(srcs, lnds, scatter, refs[2 * n], refs[2 * n + 1], masks)
        for k in (range(n) if which is None else which):
            for cp in copies[k]:
                cp.wait_send()
                cp.wait_recv()

    res = pl.pallas_call(
        body, name=name, out_shape=tuple(pltpu.HBM(b.shape, b.dtype) for b in bufs),
        in_specs=[HBM] * (2 * n) + [SEM, SEM] + [ANY] * len(after), out_specs=(HBM,) * (2 * n),
        input_output_aliases={k: k for k in range(2 * n)},
        compiler_params=pltpu.CompilerParams(has_side_effects=EFFECT),
    )(*bufs, send_sems, recv_sems, *after)
    return res[n:] if which is None else (res[n:], (send_sems, recv_sems) + tuple(res))


def _tie(x, token, name):
    def body(x_ref, t_ref, o_ref):
        del x_ref, t_ref, o_ref

    return pl.pallas_call(body, name=name, out_shape=_sds(x.shape, x.dtype), in_specs=[ANY, ANY], out_specs=ANY,
                          input_output_aliases={0: 0})(x, token)


TM_E = 272


def _norm_fwd_t(h, g, name):
    def body(h_ref, g_ref, o_ref, ot_ref):
        xv = h_ref[...]
        r = lax.rsqrt(jnp.mean(xv * xv, axis=-1, keepdims=True) + EPS)
        y = xv * r * g_ref[...]
        o_ref[...] = y.astype(BF16)
        ot_ref[...] = y.T.astype(BF16)

    return pl.pallas_call(
        body, name=name, grid=(T // 128,),
        in_specs=[pl.BlockSpec((128, D), lambda i: (i, 0)), pl.BlockSpec((1, D), lambda i: (0, 0))],
        out_specs=(pl.BlockSpec((128, D), lambda i: (i, 0)), pl.BlockSpec((D, 128), lambda i: (0, i))),
        out_shape=(_sds((T, D), BF16), _sds((D, T), BF16)), compiler_params=_cp(("parallel",)))(h, g)


def _norm_bwd_rows(xv, gv, dnv, dres):
    r = lax.rsqrt(jnp.mean(xv * xv, axis=-1, keepdims=True) + EPS)
    xh = xv * r
    dxh = dnv * gv
    dx = dres + r * (dxh - xh * jnp.mean(dxh * xh, axis=-1, keepdims=True))
    return dx, jnp.sum(dnv * xh, axis=0, keepdims=True)


TM_MM = 1088


def _inproj_fwd(a, w_g):
    nb = w_g.shape[2]

    def body(a_ref, w_ref, o_ref):
        o_ref[...] = _dot(a_ref[...], w_ref[0])

    return pl.pallas_call(
        body, name="inproj_fwd", grid=(T // TM_MM, NDEV),
        in_specs=[pl.BlockSpec((TM_MM, D), lambda i, j: (i, 0)), pl.BlockSpec((1, D, nb), lambda i, j: (j, 0, 0))],
        out_specs=pl.BlockSpec((TM_MM, nb), lambda i, j: (i, j)), out_shape=_sds((T, NDEV * nb), F32),
        compiler_params=_cp(("parallel", "parallel")))(a, w_g)


TM_B = 544


W_IN_B = IN_COLS // NDEV


NA_BLKS = 1536 // W_IN_B


def _dp_specs(rows, row_index):
    return [pl.BlockSpec((rows, W_IN_B), lambda *g: (row_index(*g), jnp.minimum(g[-1], NA_BLKS - 1))),
            pl.BlockSpec((rows, W_IN_B), lambda *g: (row_index(*g), jnp.maximum(g[-1] - NA_BLKS, 0)))]


def _inproj_bwd_dw(a_t, dp_na, dp_rest):
    def body(at_ref, na_ref, rest_ref, dw_ref):
        j = pl.program_id(0)

        @pl.when(j < NA_BLKS)
        def _():
            dw_ref[0] = _dot(at_ref[...], na_ref[...]).astype(BF16)

        @pl.when(j >= NA_BLKS)
        def _():
            dw_ref[0] = _dot(at_ref[...], rest_ref[...]).astype(BF16)

    return pl.pallas_call(
        body, name="inproj_bwd_dw", grid=(NDEV,),
        in_specs=[pl.BlockSpec((D, T), lambda j: (0, 0))] + _dp_specs(T, lambda j: 0),
        out_specs=pl.BlockSpec((1, D, W_IN_B), lambda j: (j, 0, 0)), out_shape=_sds((NDEV, D, W_IN_B), BF16),
        compiler_params=_cp(("parallel",)))(a_t, dp_na, dp_rest)


def _inproj_bwd_da(dp_na, dp_rest, w_g, h0, g_mix, dh1):
    nsub = TM_MM // TM_E

    def body(na_ref, rest_ref, w_ref, h0_ref, g_ref, dres_ref, dh0_ref, dg_ref, da):
        i, j = pl.program_id(0), pl.program_id(1)
        dpv = jnp.where(j < NA_BLKS, na_ref[...], rest_ref[...])
        dav = _dot(dpv, w_ref[0], NT)

        @pl.when(j == 0)
        def _():
            da[...] = dav

        @pl.when(j > 0)
        def _():
            da[...] += dav

        @pl.when(j == NDEV - 1)
        def _():
            gsum = jnp.zeros((1, D), F32)
            for s in range(nsub):
                sub = slice(s * TM_E, (s + 1) * TM_E)
                dx, gpart = _norm_bwd_rows(h0_ref[sub, :], g_ref[...], da[sub, :], dres_ref[sub, :])
                dh0_ref[sub, :] = dx
                gsum = gsum + gpart

            @pl.when(i == 0)
            def _():
                dg_ref[...] = gsum

            @pl.when(i > 0)
            def _():
                dg_ref[...] += gsum

    rblk = pl.BlockSpec((TM_MM, D), lambda i, j: (i, 0))
    vec = pl.BlockSpec((1, D), lambda i, j: (0, 0))
    return pl.pallas_call(
        body, name="inproj_bwd_da", grid=(T // TM_MM, NDEV),
        in_specs=_dp_specs(TM_MM, lambda i, j: i) + [pl.BlockSpec((1, D, W_IN_B), lambda i, j: (j, 0, 0)), rblk, vec, rblk],
        out_specs=(rblk, vec), out_shape=(_sds((T, D), F32), _sds((1, D), F32)),
        scratch_shapes=[pltpu.VMEM((TM_MM, D), F32)],
        compiler_params=_cp(("arbitrary", "arbitrary"), 56))(dp_na, dp_rest, w_g, h0, g_mix, dh1)


NA_QB = 256
NA_RT = NA_QB
NA_GROUPS = ROWS // 4
NA_UROWS = 11
NA_KW = NA_UROWS * GRID_W
NA_KU = 768


def _na_row_offset(var, i, j):
    valid = (j < 8, i <= j < i + 8, 3 <= j < NA_UROWS)[var]
    return (j - i + (7, 3, 0)[var]) if valid else None


def _na_bias_table(rp):
    def body(r_ref, o_ref):
        row3 = lax.broadcasted_iota(jnp.int32, (15, GRID_W, 128), 1)
        lane3 = lax.broadcasted_iota(jnp.int32, (15, GRID_W, 128), 2)
        w3 = lane3 & (GRID_W - 1)
        cs3 = jnp.clip(row3 - 8, 0, GRID_W - 16)
        lane = lax.broadcasted_iota(jnp.int32, (GRID_W, 128), 1)
        neg = jnp.full((GRID_W, 128), NEG, F32)
        z = jnp.stack([jnp.broadcast_to(r_ref[0, a:a + 1, :], (GRID_W, 128)) for a in range(15)])
        for bit in range(6):
            sh = 1 << bit
            z = jnp.where((row3 & sh) != 0, jnp.roll(z, sh, axis=2), z)
        z = jnp.roll(z, 128 - 15, axis=2)
        z = jnp.where(lane3 < GRID_W, z, 0.0)
        z = z + jnp.roll(z, GRID_W, axis=2)
        tabs = jnp.where((w3 >= cs3) & (w3 < cs3 + 16), z, NEG)
        tail = jnp.where(lane < GRID_W + NM, 0.0, NEG)
        for var in range(3):
            for i in range(4):
                for jp in range(NA_KU // 128):
                    halves = []
                    for j in (2 * jp, 2 * jp + 1):
                        a = _na_row_offset(var, i, j) if j < NA_UROWS else None
                        halves.append(tail if j >= NA_UROWS else (neg if a is None else tabs[a]))
                    o_ref[var, 0, i * 64:(i + 1) * 64, jp * 128:(jp + 1) * 128] = jnp.where(lane < GRID_W, halves[0], halves[1])

    return pl.pallas_call(
        body, name="na_bias_table", grid=(NA_HEADS,),
        in_specs=[pl.BlockSpec((1, 15, 128), lambda h: (h, 0, 0))],
        out_specs=pl.BlockSpec((3, 1, NA_QB, NA_KU), lambda h: (0, h, 0, 0)),
        out_shape=_sds((3, NA_HEADS, NA_QB, NA_KU), F32), compiler_params=_cp(("parallel",)))(rp)


def _na_var(g):
    return jnp.where(g == 0, 0, jnp.where(g == NA_GROUPS - 1, 2, 1))


def _na_load_window(src_ref, dst, g):
    us = jnp.clip(4 * g - 4, 0, ROWS - NA_UROWS)
    kstart = pl.multiple_of(NM + GRID_W * us, 16)
    dst[0:NA_KW, :] = src_ref[pl.ds(kstart, NA_KW), :].astype(BF16)
    dst[NA_KW:NA_KW + NM, :] = src_ref[0:NM, :].astype(BF16)
    dst[NA_KW + NM:, :] = jnp.zeros((NA_KU - NA_KW - NM, 128), BF16)
    return kstart


def _na_fwd(p_act, bias_tab):
    def body(q_ref, k_ref, v_ref, b_ref, o_ref, lse_ref, ku, vu):
        g = pl.program_id(1)
        _na_load_window(k_ref, ku, g)
        _na_load_window(v_ref, vu, g)
        lane = lax.broadcasted_iota(jnp.int32, (NA_RT, 128), 1)
        for rt in range(NA_QB // NA_RT):
            rows = pl.ds(pl.multiple_of(NM + NA_QB * g + NA_RT * rt, 16), NA_RT)
            tile = slice(NA_RT * rt, NA_RT * (rt + 1))
            q = q_ref[rows, :]
            o_h, lse_h = [], []
            for h in range(2):
                hm = (lane < 64) if h == 0 else (lane >= 64)
                qm = (jnp.where(hm, q, 0.0) * NA_SCALE).astype(BF16)
                s = _dot(qm, ku[...], NT) + b_ref[0, h, tile, :]
                m = jnp.max(s, axis=-1, keepdims=True)
                p = jnp.exp(s - m)
                l = jnp.sum(p, axis=-1, keepdims=True)
                o_h.append(_dot(p.astype(BF16), vu[...]) / l)
                lse_h.append(jnp.broadcast_to(m + jnp.log(l), (NA_RT, 128)))
            o_ref[rows, :] = jnp.where(lane < 64, o_h[0], o_h[1]).astype(BF16)
            lse_ref[0, rows, :] = jnp.where(lane < 64, lse_h[0], lse_h[1])

        @pl.when(g == 0)
        def _():
            qm_ = q_ref[0:NM, :]
            lane_m = lax.broadcasted_iota(jnp.int32, (NM, 128), 1)
            km, vm = ku[NA_KW:NA_KW + NM, :], vu[NA_KW:NA_KW + NM, :]
            om = []
            for h in range(2):
                hm = (lane_m < 64) if h == 0 else (lane_m >= 64)
                s = _dot(jnp.where(hm, qm_, 0.0).astype(BF16), km, NT) * NA_SCALE
                p = jnp.exp(s - jnp.max(s, axis=-1, keepdims=True))
                l = jnp.sum(p, axis=-1, keepdims=True)
                om.append(_dot(p.astype(BF16), vm) / l)
            o_ref[0:NM, :] = jnp.where(lane_m < 64, om[0], om[1]).astype(BF16)
            o_ref[L:T, :] = jnp.zeros((T - L, 128), BF16)
            lse_ref[0, 0:NM, :] = jnp.zeros((NM, 128), F32)
            lse_ref[0, L:T, :] = jnp.zeros((T - L, 128), F32)

    col = lambda off: pl.BlockSpec((T, 128), lambda hp, g: (0, off + hp))
    return pl.pallas_call(
        body, name="na_fwd", grid=(4, NA_GROUPS),
        in_specs=[col(0), col(4), col(8),
                  pl.BlockSpec((1, 2, NA_QB, NA_KU), lambda hp, g: (_na_var(g), hp, 0, 0))],
        out_specs=(pl.BlockSpec((T, 128), lambda hp, g: (0, hp)), pl.BlockSpec((1, T, 128), lambda hp, g: (hp, 0, 0))),
        out_shape=(_sds((T, 512), BF16), _sds((4, T, 128), F32)),
        scratch_shapes=[pltpu.VMEM((NA_KU, 128), BF16), pltpu.VMEM((NA_KU, 128), BF16)],
        compiler_params=_cp(("parallel", "arbitrary")))(p_act, p_act, p_act, bias_tab)


def _na_bwd(p_act, do, lse, bias_tab):
    def body(q_ref, k_ref, v_ref, do_ref, lse_ref, b_ref, dq_ref, dk_ref, dv_ref, db_ref, ku, vu):
        g = pl.program_id(1)

        @pl.when(g == 0)
        def _():
            dq_ref[...] = jnp.zeros((T, 128), F32)
            dk_ref[...] = jnp.zeros((T, 128), F32)
            dv_ref[...] = jnp.zeros((T, 128), F32)

        kstart = _na_load_window(k_ref, ku, g)
        _na_load_window(v_ref, vu, g)
        lane = lax.broadcasted_iota(jnp.int32, (NA_RT, 128), 1)
        first = (g == 0) | (g == 1) | (g == NA_GROUPS - 1)
        dku = jnp.zeros((NA_KU, 128), F32)
        dvu = jnp.zeros((NA_KU, 128), F32)
        for rt in range(NA_QB // NA_RT):
            rows = pl.ds(pl.multiple_of(NM + NA_QB * g + NA_RT * rt, 16), NA_RT)
            tile = slice(NA_RT * rt, NA_RT * (rt + 1))
            q = q_ref[rows, :]
            dov = do_ref[rows, :]
            lsev = lse_ref[0, rows, :]
            dq_h, ds_h = [], []
            for h in range(2):
                hm = (lane < 64) if h == 0 else (lane >= 64)
                qm = (jnp.where(hm, q, 0.0) * NA_SCALE).astype(BF16)
                dom = jnp.where(hm, dov, 0.0).astype(BF16)
                s = _dot(qm, ku[...], NT) + b_ref[0, h, tile, :]
                p = jnp.exp(s - lsev[:, 64 * h:64 * h + 1])
                dp = _dot(dom, vu[...], NT)
                delta = jnp.sum(p * dp, axis=-1, keepdims=True)
                ds = p * (dp - delta)
                ds_h.append(ds)
                dsb = ds.astype(BF16)
                dq_h.append(_dot(dsb, ku[...]) * NA_SCALE)
                dku = dku + _dot(dsb, qm, TN)
                dvu = dvu + _dot(p.astype(BF16), dom, TN)
            dq_ref[rows, :] = jnp.where(lane < 64, dq_h[0], dq_h[1])

            @pl.when(first)
            def _():
                for h in range(2):
                    db_ref[0, h, tile, :] = ds_h[h]

            @pl.when(jnp.logical_not(first))
            def _():
                for h in range(2):
                    db_ref[0, h, tile, :] += ds_h[h]
        dk_ref[pl.ds(kstart, NA_KW), :] += dku[0:NA_KW]
        dv_ref[pl.ds(kstart, NA_KW), :] += dvu[0:NA_KW]
        dk_ref[0:NM, :] += dku[NA_KW:NA_KW + NM]
        dv_ref[0:NM, :] += dvu[NA_KW:NA_KW + NM]

        @pl.when(g == 0)
        def _():
            qm_ = q_ref[0:NM, :]
            dom_ = do_ref[0:NM, :]
            lane_m = lax.broadcasted_iota(jnp.int32, (NM, 128), 1)
            km, vm = ku[NA_KW:NA_KW + NM, :], vu[NA_KW:NA_KW + NM, :]
            dqs = []
            dkm = jnp.zeros((NM, 128), F32)
            dvm = jnp.zeros((NM, 128), F32)
            for h in range(2):
                hm = (lane_m < 64) if h == 0 else (lane_m >= 64)
                qh = jnp.where(hm, qm_, 0.0).astype(BF16)
                doh = jnp.where(hm, dom_, 0.0).astype(BF16)
                s = _dot(qh, km, NT) * NA_SCALE
                e = jnp.exp(s - jnp.max(s, axis=-1, keepdims=True))
                p = e / jnp.sum(e, axis=-1, keepdims=True)
                dp = _dot(doh, vm, NT)
                ds = p * (dp - jnp.sum(p * dp, axis=-1, keepdims=True))
                dsb = (ds * NA_SCALE).astype(BF16)
                dqs.append(_dot(dsb, km))
                dkm = dkm + _dot(dsb, qh, TN)
                dvm = dvm + _dot(p.astype(BF16), doh, TN)
            dq_ref[0:NM, :] = jnp.where(lane_m < 64, dqs[0], dqs[1])
            dk_ref[0:NM, :] += dkm
            dv_ref[0:NM, :] += dvm

    col = lambda off: pl.BlockSpec((T, 128), lambda hp, g: (0, off + hp))
    ocol = pl.BlockSpec((T, 128), lambda hp, g: (0, hp))
    bspec = pl.BlockSpec((1, 2, NA_QB, NA_KU), lambda hp, g: (_na_var(g), hp, 0, 0))
    return pl.pallas_call(
        body, name="na_bwd", grid=(4, NA_GROUPS),
        in_specs=[col(0), col(4), col(8), ocol, pl.BlockSpec((1, T, 128), lambda hp, g: (hp, 0, 0)), bspec],
        out_specs=(ocol, ocol, ocol, bspec),
        out_shape=(_sds((T, 512), F32), _sds((T, 512), F32), _sds((T, 512), F32), _sds((3, NA_HEADS, NA_QB, NA_KU), F32)),
        scratch_shapes=[pltpu.VMEM((NA_KU, 128), BF16), pltpu.VMEM((NA_KU, 128), BF16)],
        compiler_params=_cp(("parallel", "arbitrary")))(p_act, p_act, p_act, do, lse, bias_tab)


def _na_rpb_reduce(dbias):
    def body(db_ref, o_ref):
        lane = lax.broadcasted_iota(jnp.int32, (GRID_W, 128), 1)
        row3 = lax.broadcasted_iota(jnp.int32, (15, GRID_W, 128), 1)
        lane3 = lax.broadcasted_iota(jnp.int32, (15, GRID_W, 128), 2)
        accs = []
        for a in range(15):
            acc = jnp.zeros((GRID_W, 128), F32)
            for var in range(3):
                for i in range(4):
                    for j in range(NA_UROWS):
                        if _na_row_offset(var, i, j) == a:
                            pair = db_ref[var, 0, i * 64:(i + 1) * 64, (j // 2) * 128:(j // 2 + 1) * 128]
                            acc = acc + jnp.where((lane < GRID_W) if j % 2 == 0 else (lane >= GRID_W), pair, 0.0)
            accs.append(acc)
        z = jnp.stack(accs)
        z = jnp.where(lane3 < GRID_W, z + jnp.roll(z, GRID_W, axis=2), 0.0)
        for bit in range(6):
            sh = 1 << bit
            z = jnp.where((row3 & sh) != 0, jnp.roll(z, 128 - sh, axis=2), z)
        z = jnp.roll(z, 15, axis=2)
        o_ref[0] = jnp.sum(z, axis=1)

    return pl.pallas_call(
        body, name="na_rpb_reduce", grid=(NA_HEADS,),
        in_specs=[pl.BlockSpec((3, 1, NA_QB, NA_KU), lambda h: (0, h, 0, 0))],
        out_specs=pl.BlockSpec((1, 15, 128), lambda h: (h, 0, 0)), out_shape=_sds((NA_HEADS, 15, 128), F32),
        compiler_params=_cp(("parallel",)))(dbias)


HG_RB = 128
HG_NB = T // HG_RB
HG_SLOTS = HG_NB * 8
HI = lax.Precision.HIGHEST
HG_UNROLL = 4


def _chunk_tri(lower):
    r = lax.broadcasted_iota(jnp.int32, (HG_RB, HG_RB), 0)
    c = lax.broadcasted_iota(jnp.int32, (HG_RB, HG_RB), 1)
    same = (r // HG_C) == (c // HG_C)
    keep = (c <= r) if lower else (c >= r)
    return jnp.where(same & keep, 1.0, 0.0).astype(F32)


def _hg_gate_terms(z, lg):
    dl = lg[0:1, :] - lg[1:2, :]
    log_lb = jax.nn.log_sigmoid(dl)
    log_1mlb = jax.nn.log_sigmoid(-dl)
    yz = log_1mlb + jax.nn.log_sigmoid(z)
    log_f = jnp.logaddexp(log_lb, yz)
    snz = jax.nn.sigmoid(-z)
    k = jnp.exp(log_1mlb) * snz
    w2 = jnp.exp(yz - log_f)
    return log_f, k, snz, w2


def _hg_pre(p_act, logits):
    def body(q_ref, zf_ref, zb_ref, lg_ref, qh_ref, kf_ref, bf_ref, kb_ref, bb_ref):
        qh_ref[...] = jax.nn.silu(q_ref[...])
        lf, kf, _, _ = _hg_gate_terms(zf_ref[...], lg_ref[0])
        kf_ref[...] = kf
        bf_ref[...] = jnp.dot(_chunk_tri(True), lf, precision=HI, preferred_element_type=F32)
        lb_, kb, _, _ = _hg_gate_terms(zb_ref[...], lg_ref[1])
        kb_ref[...] = kb
        bb_ref[...] = jnp.dot(_chunk_tri(False), lb_, precision=HI, preferred_element_type=F32)

    blk = lambda c: pl.BlockSpec((HG_RB, 512), lambda i: (i, c))
    ob = pl.BlockSpec((HG_RB, 512), lambda i: (i, 0))
    return pl.pallas_call(
        body, name="hg_pre", grid=(HG_NB,),
        in_specs=[blk(3), blk(4), blk(5), pl.BlockSpec((2, 2, 512), lambda i: (0, 0, 0))],
        out_specs=(ob,) * 5, out_shape=(_sds((T, 512), F32),) * 5,
        compiler_params=_cp(("parallel",)))(p_act, p_act, p_act, logits)


def _bdot(a, b, ca, cb):
    return lax.dot_general(a.astype(BF16), b.astype(BF16), (((ca,), (cb,)), ((0,), (0,))), preferred_element_type=F32)


HG_S = 8
HG_NS = HG_RB // HG_S


def _lane_sums(xs):
    l_io = lax.broadcasted_iota(jnp.int32, (HG_NS, HG_S, HG_S), 2)
    a = jnp.zeros((HG_NS, HG_S, HG_S), F32)
    for j, x in enumerate(xs):
        a = a + jnp.where(l_io == j, jnp.sum(x, axis=-1, keepdims=True), 0.0)
    return a


def _halves(x):
    y = x.reshape(8, 2, HG_S, x.shape[-1])
    return y[:, 0], y[:, 1]


def _join(first, second):
    return jnp.stack([first, second], axis=1).reshape(HG_RB, first.shape[-1])


def _cross_split(rev, b4):
    b_1, b_2 = _halves(b4)
    if rev:
        r = b_2[:, 0:1, :]
        return jnp.exp(b_1 - r), jnp.exp(r - b_2)
    r = b_1[:, HG_S - 1:HG_S, :]
    return jnp.exp(b_2 - r), jnp.exp(r - b_1)


def _hg_scan_fwd(qh, k, b, p_act, rev):
    anchor = 0 if rev else HG_C - 1

    def body(q_ref, k_ref, b_ref, v_ref, o_ref, st_ref, dsc):
        def phase_a(blk, _):
            rows = pl.ds(pl.multiple_of(blk * HG_RB, HG_RB), HG_RB)
            b3 = b_ref[rows, :].reshape(8, HG_C, 128)
            k3 = k_ref[rows, :].reshape(8, HG_C, 128)
            v3 = v_ref[rows, :].reshape(8, HG_C, 128)
            bl = b3[:, anchor:anchor + 1, :]
            kt = k3 * jnp.exp(bl - b3)
            st_ref[0, pl.ds(pl.multiple_of(blk * 8, 8), 8)] = _bdot(v3, kt, 1, 1)
            dsc[pl.ds(pl.multiple_of(blk * 8, 8), 8), :] = jnp.exp(bl[:, 0, :])
            return 0

        lax.fori_loop(0, HG_NB, phase_a, 0, unroll=HG_UNROLL)

        def phase_b(n, carry):
            c = (NCHUNK - 1 - n) if rev else n
            u = st_ref[0, c]
            st_ref[0, c] = carry
            return carry * dsc[pl.ds(c, 1), :] + u

        lax.fori_loop(0, NCHUNK // 3, lambda n3, s: phase_b(3 * n3 + 2, phase_b(3 * n3 + 1, phase_b(3 * n3, s))),
                      jnp.zeros((128, 128), F32))
        for c in range(NCHUNK, HG_SLOTS):
            st_ref[0, c] = jnp.zeros((128, 128), F32)

        t_io = lax.broadcasted_iota(jnp.int32, (HG_NS, HG_S, 128), 1)

        def phase_c(blk, _):
            rows = pl.ds(pl.multiple_of(blk * HG_RB, HG_RB), HG_RB)
            b4 = b_ref[rows, :].reshape(HG_NS, HG_S, 128)
            k4 = k_ref[rows, :].reshape(HG_NS, HG_S, 128)
            q4 = q_ref[rows, :].reshape(HG_NS, HG_S, 128)
            v4 = v_ref[rows, :].reshape(HG_NS, HG_S, 128)
            st = st_ref[0, pl.ds(pl.multiple_of(blk * 8, 8), 8)]
            o = _bdot((q4 * jnp.exp(b4)).reshape(8, HG_C, 128), st, 2, 2).reshape(HG_RB, 128)
            terms = []
            for s in range(HG_S):
                ok = (t_io <= s) if rev else (t_io >= s)
                f = jnp.exp(jnp.where(ok, b4 - b4[:, s:s + 1, :], NEG))
                terms.append(q4 * f * k4[:, s:s + 1, :])
            o_in = _bdot(_lane_sums(terms), v4, 2, 1)
            wq, wk = _cross_split(rev, b4)
            q_1, q_2 = _halves(q4)
            k_1, k_2 = _halves(k4)
            v_1, v_2 = _halves(v4)
            o_1, o_2 = _halves(o_in)
            if rev:
                o_1 = o_1 + _bdot(_bdot(q_1 * wq, k_2 * wk, 2, 2), v_2, 2, 1)
            else:
                o_2 = o_2 + _bdot(_bdot(q_2 * wq, k_1 * wk, 2, 2), v_1, 2, 1)
            o_ref[rows, :] = o + _join(o_1, o_2)
            return 0

        lax.fori_loop(0, HG_NB, phase_c, 0, unroll=HG_UNROLL)

    col = pl.BlockSpec((T, 128), lambda h: (0, h))
    return pl.pallas_call(
        body, name="hg_scan_bwd_dir" if rev else "hg_scan_fwd_dir", grid=(HG_HEADS,),
        in_specs=[col, col, col, pl.BlockSpec((T, 128), lambda h: (0, 24 + h))],
        out_specs=(col, pl.BlockSpec((1, HG_SLOTS, 128, 128), lambda h: (h, 0, 0, 0))),
        out_shape=(_sds((T, 512), F32), _sds((HG_HEADS, HG_SLOTS, 128, 128), F32)),
        scratch_shapes=[pltpu.VMEM((HG_SLOTS, 128), F32)],
        compiler_params=_cp(("parallel",), 56))(qh, k, b, p_act)


def _hg_scan_bwd(qh, k, b, p_act, st, do, rev):
    anchor = 0 if rev else HG_C - 1

    def body(q_ref, k_ref, b_ref, v_ref, st_ref, do_ref, dq_ref, dk_ref, db_ref, dv_ref, gst, dsc, dbl):
        def phase_a(blk, _):
            rows = pl.ds(pl.multiple_of(blk * HG_RB, HG_RB), HG_RB)
            b3 = b_ref[rows, :].reshape(8, HG_C, 128)
            q3 = q_ref[rows, :].reshape(8, HG_C, 128)
            do3 = do_ref[rows, :].reshape(8, HG_C, 128)
            gst[pl.ds(pl.multiple_of(blk * 8, 8), 8)] = _bdot(do3, q3 * jnp.exp(b3), 1, 1)
            dsc[pl.ds(pl.multiple_of(blk * 8, 8), 8), :] = jnp.exp(b3[:, anchor, :])
            return 0

        lax.fori_loop(0, HG_NB, phase_a, 0, unroll=HG_UNROLL)

        def phase_b(n, carry):
            c = n if rev else (NCHUNK - 1 - n)
            w = gst[c]
            gst[c] = carry
            dcv = dsc[pl.ds(c, 1), :]
            dbl[pl.ds(c, 1), :] = dcv * jnp.sum(st_ref[0, c] * carry, axis=0, keepdims=True)
            return carry * dcv + w

        lax.fori_loop(0, NCHUNK // 3, lambda n3, s: phase_b(3 * n3 + 2, phase_b(3 * n3 + 1, phase_b(3 * n3, s))),
                      jnp.zeros((128, 128), F32))
        for c in range(NCHUNK, HG_SLOTS):
            gst[c] = jnp.zeros((128, 128), F32)
            dbl[c:c + 1, :] = jnp.zeros((1, 128), F32)

        t_io = lax.broadcasted_iota(jnp.int32, (HG_NS, HG_S, 128), 1)
        t16 = lax.broadcasted_iota(jnp.int32, (8, HG_C, 128), 1)
        r_io = lax.broadcasted_iota(jnp.int32, (HG_NS, HG_S, HG_S), 1)
        l_io = lax.broadcasted_iota(jnp.int32, (HG_NS, HG_S, HG_S), 2)

        def phase_c(blk, _):
            rows = pl.ds(pl.multiple_of(blk * HG_RB, HG_RB), HG_RB)
            cs = pl.ds(pl.multiple_of(blk * 8, 8), 8)
            b4 = b_ref[rows, :].reshape(HG_NS, HG_S, 128)
            k4 = k_ref[rows, :].reshape(HG_NS, HG_S, 128)
            q4 = q_ref[rows, :].reshape(HG_NS, HG_S, 128)
            v4 = v_ref[rows, :].reshape(HG_NS, HG_S, 128)
            do4 = do_ref[rows, :].reshape(HG_NS, HG_S, 128)
            b3, k3, q3 = (z.reshape(8, HG_C, 128) for z in (b4, k4, q4))
            v3, do3 = v4.reshape(8, HG_C, 128), do4.reshape(8, HG_C, 128)
            s_t = st_ref[0, cs]
            g_t = gst[cs]
            bl = b3[:, anchor:anchor + 1, :]
            ekl = jnp.exp(bl - b3)
            kt = k3 * ekl
            dkt = _bdot(v3, g_t, 2, 1)
            dq = (_bdot(do3, s_t, 2, 1) * jnp.exp(b3)).reshape(HG_NS, HG_S, 128)
            dk = (dkt * ekl).reshape(HG_NS, HG_S, 128)
            dv = _bdot(kt, g_t, 2, 2).reshape(HG_NS, HG_S, 128)
            dbl3 = dbl[cs, :].reshape(8, 1, 128) + jnp.sum(dkt * kt, axis=1, keepdims=True)
            causal = (l_io >= r_io) if rev else (l_io <= r_io)
            da = jnp.where(causal, _bdot(do4, v4, 2, 2), 0.0)
            causal_t = (l_io <= r_io) if rev else (l_io >= r_io)
            dat = jnp.where(causal_t, _bdot(v4, do4, 2, 2), 0.0)
            for s in range(HG_S):
                ok = (t_io <= s) if rev else (t_io >= s)
                f = jnp.exp(jnp.where(ok, b4 - b4[:, s:s + 1, :], NEG))
                dq = dq + da[:, :, s:s + 1] * (f * k4[:, s:s + 1, :])
            terms = []
            for t in range(HG_S):
                ok = (t_io >= t) if rev else (t_io <= t)
                e = jnp.exp(jnp.where(ok, b4[:, t:t + 1, :] - b4, NEG))
                eq = e * q4[:, t:t + 1, :]
                dk = dk + dat[:, :, t:t + 1] * eq
                terms.append(eq * k4)
            dv = dv + _bdot(_lane_sums(terms), do4, 2, 1)
            wq, wk = _cross_split(rev, b4)
            pick = (lambda z: _halves(z)) if rev else (lambda z: _halves(z)[::-1])
            (q_q, _), (_, k_k), (_, v_k), (do_q, _) = pick(q4), pick(k4), pick(v4), pick(do4)
            qx, kx = q_q * wq, k_k * wk
            dq_q = _bdot(_bdot(do_q, v_k, 2, 2), kx, 2, 1) * wq
            dk_k = _bdot(_bdot(v_k, do_q, 2, 2), qx, 2, 1) * wk
            dv_k = _bdot(_bdot(kx, qx, 2, 2), do_q, 2, 1)
            zero = jnp.zeros((8, HG_S, 128), F32)
            place_q = (lambda z: _join(z, zero)) if rev else (lambda z: _join(zero, z))
            place_k = (lambda z: _join(zero, z)) if rev else (lambda z: _join(z, zero))
            dq2 = dq.reshape(HG_RB, 128) + place_q(dq_q)
            dk2 = dk.reshape(HG_RB, 128) + place_k(dk_k)
            dv2 = dv.reshape(HG_RB, 128) + place_k(dv_k)
            dq3, dk3 = dq2.reshape(8, HG_C, 128), dk2.reshape(8, HG_C, 128)
            db = q3 * dq3 - k3 * dk3 + jnp.where(t16 == anchor, dbl3, 0.0)
            dq_ref[rows, :] = dq2
            dk_ref[rows, :] = dk2
            db_ref[rows, :] = db.reshape(HG_RB, 128)
            dv_ref[rows, :] = dv2
            return 0

        lax.fori_loop(0, HG_NB, phase_c, 0, unroll=HG_UNROLL)

    col = pl.BlockSpec((T, 128), lambda h: (0, h))
    return pl.pallas_call(
        body, name="hg_scan_bwd_dir_bwd" if rev else "hg_scan_fwd_dir_bwd", grid=(HG_HEADS,),
        in_specs=[col, col, col, pl.BlockSpec((T, 128), lambda h: (0, 24 + h)),
                  pl.BlockSpec((1, HG_SLOTS, 128, 128), lambda h: (h, 0, 0, 0)), col],
        out_specs=(col,) * 4, out_shape=(_sds((T, 512), F32),) * 4,
        scratch_shapes=[pltpu.VMEM((HG_SLOTS, 128, 128), F32), pltpu.VMEM((HG_SLOTS, 128), F32),
                        pltpu.VMEM((HG_SLOTS, 128), F32)],
        compiler_params=_cp(("parallel",), 56))(qh, k, b, p_act, st, do)


def _row_valid(i, tm):
    r = lax.broadcasted_iota(jnp.int32, (tm, 1), 0) + i * tm
    return r < L


def _hg_post_rows(o, gv, gain_v, valid):
    parts = []
    for h in range(HG_HEADS):
        oh = o[:, 128 * h:128 * (h + 1)]
        parts.append(oh * lax.rsqrt(jnp.mean(oh * oh, axis=-1, keepdims=True) + EPS))
    return jnp.where(valid, jnp.concatenate(parts, axis=1) * gain_v * jax.nn.silu(gv), 0.0)


def _hg_post_bwd_rows(du, o, gv, gain_v, valid):
    duv = jnp.where(valid, du, 0.0)
    sig = jax.nn.sigmoid(gv)
    sg = gv * sig
    dn = duv * gain_v * sg
    do_parts, n_parts = [], []
    for h in range(HG_HEADS):
        sl = slice(128 * h, 128 * (h + 1))
        oh = o[:, sl]
        r = lax.rsqrt(jnp.mean(oh * oh, axis=-1, keepdims=True) + EPS)
        nh = oh * r
        dnh = dn[:, sl]
        do_parts.append(r * (dnh - nh * jnp.mean(dnh * nh, axis=-1, keepdims=True)))
        n_parts.append(nh)
    n = jnp.where(valid, jnp.concatenate(n_parts, axis=1), 0.0)
    do = jnp.where(valid, jnp.concatenate(do_parts, axis=1), 0.0)
    dg = duv * n * gain_v * (sig * (1.0 + gv * (1.0 - sig)))
    return do, dg, jnp.sum(duv * n * sg, axis=0, keepdims=True)


def _hg_pre_bwd(p_act, logits, dq_f, dq_b, dk_f, dk_b, db_f, db_b, dv_f, dv_b, dp_rest):
    def body(q_ref, zf_ref, zb_ref, lg_ref, dqf_ref, dqb_ref, dkf_ref, dkb_ref, dbf_ref, dbb_ref, dvf_ref, dvb_ref, _,
             dp_ref, dlg_ref):
        dq_ref, dzf_ref, dzb_ref, di_ref = (dp_ref.at[:, 512 * c:512 * (c + 1)] for c in range(4))
        i = pl.program_id(0)
        valid = _row_valid(i, HG_RB)
        qv = q_ref[...]
        sig = jax.nn.sigmoid(qv)
        dq_ref[...] = jnp.where(valid, (dqf_ref[...] + dqb_ref[...]) * (sig * (1.0 + qv * (1.0 - sig))), 0.0).astype(BF16)
        di_ref[...] = jnp.where(valid, dvf_ref[...] + dvb_ref[...], 0.0).astype(BF16)
        for d, (z_ref, dk_r, db_r, dz_ref) in enumerate(((zf_ref, dkf_ref, dbf_ref, dzf_ref), (zb_ref, dkb_ref, dbb_ref, dzb_ref))):
            lg = lg_ref[d]
            dl = lg[0:1, :] - lg[1:2, :]
            lb = jax.nn.sigmoid(dl)
            one_m_lb = jax.nn.sigmoid(-dl)
            log_f, _, snz, w2 = _hg_gate_terms(z_ref[...], lg)
            dbv = jnp.where(valid, db_r[...], 0.0)
            dkv = jnp.where(valid, dk_r[...], 0.0)
            dlf = jnp.dot(_chunk_tri(d == 1), dbv, precision=HI, preferred_element_type=F32)
            sz = 1.0 - snz
            dz_ref[...] = (dlf * w2 * snz - dkv * one_m_lb * sz * snz).astype(BF16)
            dlb = jnp.sum(dlf * snz * jnp.exp(-log_f) - dkv * snz, axis=0, keepdims=True)
            dl0 = dlb * lb * one_m_lb
            part = jnp.concatenate([dl0, -dl0], axis=0)

            @pl.when(i == 0)
            def _():
                dlg_ref[d] = part

            @pl.when(i > 0)
            def _():
                dlg_ref[d] += part

    blk = lambda c: pl.BlockSpec((HG_RB, 512), lambda i: (i, c))
    ob = pl.BlockSpec((HG_RB, 512), lambda i: (i, 0))
    lgs = pl.BlockSpec((2, 2, 512), lambda i: (0, 0, 0))
    return pl.pallas_call(
        body, name="hg_pre_bwd", grid=(HG_NB,),
        in_specs=[blk(3), blk(4), blk(5), lgs] + [ob] * 8 + [ANY],
        out_specs=(pl.BlockSpec((HG_RB, 2048), lambda i: (i, 0)), lgs),
        out_shape=(_sds(dp_rest.shape, BF16), _sds((2, 2, 512), F32)), input_output_aliases={12: 0},
        compiler_params=_cp(("arbitrary",)))(p_act, p_act, p_act, logits, dq_f, dq_b, dk_f, dk_b, db_f, db_b, dv_f, dv_b,
                                             dp_rest)


def _mix_fwd(o_na, o_f, o_b, gain, w_na, w_hg, p_act):
    def body(ona_ref, of_ref, ob_ref, g_ref, gain_ref, wna_ref, whg_ref, gna_ref, ghg_ref, o_ref, u_ref):
        u = _hg_post_rows(of_ref[...] + ob_ref[...], g_ref[...], gain_ref[...], _row_valid(pl.program_id(0), TM_B)).astype(BF16)
        u_ref[...] = u
        y_na = _dot(ona_ref[...], wna_ref[...])
        y_hg = _dot(u, whg_ref[...])
        o_ref[...] = (jax.nn.sigmoid(gna_ref[...]) * y_na + jax.nn.sigmoid(ghg_ref[...]) * y_hg).astype(BF16)

    act = pl.BlockSpec((TM_B, 512), lambda i: (i, 0))
    wsp = pl.BlockSpec((512, D), lambda i: (0, 0))
    return pl.pallas_call(
        body, name="mix_fwd", grid=(T // TM_B,),
        in_specs=[act, act, act, pl.BlockSpec((TM_B, 512), lambda i: (i, 7)), pl.BlockSpec((1, 512), lambda i: (0, 0)),
                  wsp, wsp, pl.BlockSpec((TM_B, D), lambda i: (i, 4)), pl.BlockSpec((TM_B, D), lambda i: (i, 5))],
        out_specs=(pl.BlockSpec((TM_B, D), lambda i: (i, 0)), act), out_shape=(_sds((T, D), BF16), _sds((T, 512), BF16)),
        compiler_params=_cp(("parallel",)))(o_na, o_f, o_b, p_act, gain, w_na, w_hg, p_act, p_act)


DP_REST = IN_COLS - 1536


def _mix_bwd(o_na, u_hg, o_f, o_b, gain, w_na, w_hg, p_act, dmix):
    ni = T // TM_B

    def body(ona_ref, uhg_ref, of_ref, ob_ref, g_ref, gain_ref, wna_ref, whg_ref, gna_ref, ghg_ref, dmix_ref,
             dp_ref, dwna_ref, dwhg_ref, dona_ref, do_ref, dgain_ref, acc_na, acc_hg):
        i = pl.program_id(0)
        dg_ref, dgna_ref, dghg_ref = dp_ref.at[:, 2048:2560], dp_ref.at[:, 2560:3584], dp_ref.at[:, 3584:4608]
        dm = dmix_ref[...].astype(F32)
        dxs = []
        for x_ref, w_ref, gt_ref, dgt_ref, dw_ref, acc in (
                (ona_ref, wna_ref, gna_ref, dgna_ref, dwna_ref, acc_na), (uhg_ref, whg_ref, ghg_ref, dghg_ref, dwhg_ref, acc_hg)):
            xv = x_ref[...]
            y = _dot(xv, w_ref[...])
            sg = jax.nn.sigmoid(gt_ref[...])
            dgt_ref[...] = (dm * y * sg * (1.0 - sg)).astype(BF16)
            dy = (dm * sg).astype(BF16)
            dxs.append(_dot(dy, w_ref[...], NT))
            part = _dot(xv, dy, TN)

            @pl.when(i == 0)
            def _():
                acc[...] = part

            @pl.when(i > 0)
            def _():
                acc[...] += part

            @pl.when(i == ni - 1)
            def _():
                dw_ref[...] = acc[...].astype(BF16)

        dona_ref[...] = dxs[0]
        do, dg, gpart = _hg_post_bwd_rows(dxs[1], of_ref[...] + ob_ref[...], g_ref[...], gain_ref[...], _row_valid(i, TM_B))
        do_ref[...] = do
        dg_ref[...] = dg.astype(BF16)

        @pl.when(i == 0)
        def _():
            dgain_ref[...] = gpart

        @pl.when(i > 0)
        def _():
            dgain_ref[...] += gpart

    act = pl.BlockSpec((TM_B, 512), lambda i: (i, 0))
    wsp = pl.BlockSpec((512, D), lambda i: (0, 0))
    rblk = pl.BlockSpec((TM_B, D), lambda i: (i, 0))
    vec = pl.BlockSpec((1, 512), lambda i: (0, 0))
    return pl.pallas_call(
        body, name="mix_bwd", grid=(ni,),
        in_specs=[act, act, act, act, pl.BlockSpec((TM_B, 512), lambda i: (i, 7)), vec, wsp, wsp,
                  pl.BlockSpec((TM_B, D), lambda i: (i, 4)), pl.BlockSpec((TM_B, D), lambda i: (i, 5)), rblk],
        out_specs=(pl.BlockSpec((TM_B, DP_REST), lambda i: (i, 0)), wsp, wsp, act, act, vec),
        out_shape=(_sds((T, DP_REST), BF16), _sds((512, D), BF16), _sds((512, D), BF16),
                   _sds((T, 512), F32), _sds((T, 512), F32), _sds((1, 512), F32)),
        scratch_shapes=[pltpu.VMEM((512, D), F32), pltpu.VMEM((512, D), F32)],
        compiler_params=_cp(("arbitrary",)))(o_na, u_hg, o_f, o_b, p_act, gain, w_na, w_hg, p_act, p_act, dmix)


def _wo_fwd(mix, w_o, h0, g_mlp):
    def body(mix_ref, w_ref, h0_ref, g_ref, h1_ref, m_ref):
        h1 = h0_ref[...] + _dot(mix_ref[...], w_ref[...])
        h1_ref[...] = h1
        r = lax.rsqrt(jnp.mean(h1 * h1, axis=-1, keepdims=True) + EPS)
        m_ref[...] = (h1 * r * g_ref[...]).astype(BF16)

    blk = pl.BlockSpec((TM_B, D), lambda i: (i, 0))
    return pl.pallas_call(
        body, name="wo_fwd", grid=(T // TM_B,),
        in_specs=[blk, pl.BlockSpec((D, D), lambda i: (0, 0)), blk, pl.BlockSpec((1, D), lambda i: (0, 0))],
        out_specs=(blk, blk), out_shape=(_sds((T, D), F32), _sds((T, D), BF16)),
        compiler_params=_cp(("parallel",)))(mix, w_o, h0, g_mlp)


def _wo_bwd(dh1_b, w_o, mix):
    ni = T // TM_B

    def body(dh_ref, w_ref, mix_ref, dmix_ref, dw_ref, acc):
        i = pl.program_id(0)
        dh = dh_ref[...]
        dmix_ref[...] = _dot(dh, w_ref[...], NT).astype(BF16)
        part = _dot(mix_ref[...], dh, TN)

        @pl.when(i == 0)
        def _():
            acc[...] = part

        @pl.when(i > 0)
        def _():
            acc[...] += part

        @pl.when(i == ni - 1)
        def _():
            dw_ref[...] = acc[...].astype(BF16)

    blk = pl.BlockSpec((TM_B, D), lambda i: (i, 0))
    wsp = pl.BlockSpec((D, D), lambda i: (0, 0))
    return pl.pallas_call(
        body, name="wo_bwd", grid=(ni,), in_specs=[blk, wsp, blk], out_specs=(blk, wsp),
        out_shape=(_sds((T, D), BF16), _sds((D, D), BF16)), scratch_shapes=[pltpu.VMEM((D, D), F32)],
        compiler_params=_cp(("arbitrary",)))(dh1_b, w_o, mix)


FF_B = D_FF // NDEV


def _loss_rows(xv, gv, tv, row0):
    r_io = lax.broadcasted_iota(jnp.int32, (xv.shape[0], 1), 0) + row0
    valid = (r_io >= NM) & (r_io < L)
    r = lax.rsqrt(jnp.mean(xv * xv, axis=-1, keepdims=True) + EPS)
    xh = xv * r
    err = jnp.where(valid, xh * gv - tv, 0.0)
    lpart = 0.5 * jnp.sum(jnp.sum(err * err, axis=-1, keepdims=True) * (1.0 / D), axis=0, keepdims=True)
    dy = err * (1.0 / D)
    dxh = dy * gv
    dh = r * (dxh - xh * jnp.mean(dxh * xh, axis=-1, keepdims=True))
    return lpart, dh, jnp.sum(dy * xh, axis=0, keepdims=True)


def _mlp_fwd_loss(m, wup_g, wdown_g, h1, g_final, tgt):
    nsub = TM_MM // TM_E

    def body(m_ref, wu_ref, wd_ref, h1_ref, g_ref, t_ref, loss_ref, dh_ref, dhb_ref, dg_ref, h2):
        i, j = pl.program_id(0), pl.program_id(1)
        up = jnp.maximum(_dot(m_ref[...], wu_ref[0]), 0.0)
        part = _dot((up * up).astype(BF16), wd_ref[0])

        @pl.when(j == 0)
        def _():
            h2[...] = h1_ref[...] + part

        @pl.when(j > 0)
        def _():
            h2[...] += part

        @pl.when(j == NDEV - 1)
        def _():
            lsum = jnp.zeros((1, 1), F32)
            gsum = jnp.zeros((1, D), F32)
            for s in range(nsub):
                rows = slice(s * TM_E, (s + 1) * TM_E)
                lpart, dh, gpart = _loss_rows(h2[rows, :], g_ref[...], t_ref[rows, :], i * TM_MM + s * TM_E)
                dh_ref[rows, :] = dh
                dhb_ref[rows, :] = dh.astype(BF16)
                lsum = lsum + lpart
                gsum = gsum + gpart
            lsum = jnp.broadcast_to(lsum, (1, 128))

            @pl.when(i == 0)
            def _():
                loss_ref[...] = lsum
                dg_ref[...] = gsum

            @pl.when(i > 0)
            def _():
                loss_ref[...] += lsum
                dg_ref[...] += gsum

    blk = pl.BlockSpec((TM_MM, D), lambda i, j: (i, 0))
    vec = pl.BlockSpec((1, D), lambda i, j: (0, 0))
    return pl.pallas_call(
        body, name="mlp_fwd_loss", grid=(T // TM_MM, NDEV),
        in_specs=[blk, pl.BlockSpec((1, D, FF_B), lambda i, j: (j, 0, 0)), pl.BlockSpec((1, FF_B, D), lambda i, j: (j, 0, 0)),
                  blk, vec, blk],
        out_specs=(pl.BlockSpec((1, 128), lambda i, j: (0, 0)), blk, blk, vec),
        out_shape=(_sds((1, 128), F32), _sds((T, D), F32), _sds((T, D), BF16), _sds((1, D), F32)),
        scratch_shapes=[pltpu.VMEM((TM_MM, D), F32)],
        compiler_params=_cp(("arbitrary", "arbitrary"), 56))(m, wup_g, wdown_g, h1, g_final, tgt)


def _mlp_bwd(m, dh2_b, wup_g, wdown_g, h1, g_mlp, dh2):
    ni = T // TM_B
    nsub = TM_B // TM_E

    def body(m_ref, dh_ref, wu_ref, wd_ref, h1_ref, g_ref, dres_ref, dwu_ref, dwd_ref, dh1_ref, dh1b_ref, dg_ref,
             dm_ref, acc_u, acc_d):
        j, i = pl.program_id(0), pl.program_id(1)
        rows = pl.ds(pl.multiple_of(i * TM_B, TM_B), TM_B)
        mv, dh = m_ref[...], dh_ref[...]
        r = jnp.maximum(_dot(mv, wu_ref[0]), 0.0)
        act = (r * r).astype(BF16)
        dact = _dot(dh, wd_ref[0], NT)
        dup = (dact * (2.0 * r)).astype(BF16)
        pd = _dot(act, dh, TN)
        pu = _dot(mv, dup, TN)
        dmv = _dot(dup, wu_ref[0], NT)

        @pl.when(i == 0)
        def _():
            acc_u[...] = pu
            acc_d[...] = pd

        @pl.when(i > 0)
        def _():
            acc_u[...] += pu
            acc_d[...] += pd

        @pl.when(i == ni - 1)
        def _():
            dwu_ref[0] = acc_u[...].astype(BF16)
            dwd_ref[0] = acc_d[...].astype(BF16)

        @pl.when(j == 0)
        def _():
            dm_ref[rows, :] = dmv

        @pl.when(j > 0)
        def _():
            dm_ref[rows, :] += dmv

        @pl.when(j == NDEV - 1)
        def _():
            gsum = jnp.zeros((1, D), F32)
            for s in range(nsub):
                sub = slice(s * TM_E, (s + 1) * TM_E)
                dm_rows = dm_ref[pl.ds(pl.multiple_of(i * TM_B + s * TM_E, TM_E), TM_E), :]
                dx, gpart = _norm_bwd_rows(h1_ref[sub, :], g_ref[...], dm_rows, dres_ref[sub, :])
                dh1_ref[sub, :] = dx
                dh1b_ref[sub, :] = dx.astype(BF16)
                gsum = gsum + gpart

            @pl.when(i == 0)
            def _():
                dg_ref[...] = gsum

            @pl.when(i > 0)
            def _():
                dg_ref[...] += gsum

    blk = pl.BlockSpec((TM_B, D), lambda j, i: (i, 0))
    late = pl.BlockSpec((TM_B, D), lambda j, i: (jnp.where(j == NDEV - 1, i, 0), 0))
    vec = pl.BlockSpec((1, D), lambda j, i: (0, 0))
    wus = pl.BlockSpec((1, D, FF_B), lambda j, i: (j, 0, 0))
    wds = pl.BlockSpec((1, FF_B, D), lambda j, i: (j, 0, 0))
    return pl.pallas_call(
        body, name="mlp_bwd", grid=(NDEV, ni), in_specs=[blk, blk, wus, wds, late, vec, late],
        out_specs=(wus, wds, late, late, vec),
        out_shape=(_sds((NDEV, D, FF_B), BF16), _sds((NDEV, FF_B, D), BF16), _sds((T, D), F32), _sds((T, D), BF16),
                   _sds((1, D), F32)),
        scratch_shapes=[pltpu.VMEM((T, D), F32), pltpu.VMEM((D, FF_B), F32), pltpu.VMEM((FF_B, D), F32)],
        compiler_params=_cp(("arbitrary", "arbitrary"), 56))(m, dh2_b, wup_g, wdown_g, h1, g_mlp, dh2)


def _adamw(parts, w, m, v, name):
    rr, cc = w.shape
    nslot = parts.shape[0]
    tr = rr
    for cand in (256, 128, 64):
        if rr % cand == 0 and rr > cand:
            tr = cand
            break
    c1 = 1.0 - ADAM_B1 ** ADAM_STEP
    c2 = 1.0 - ADAM_B2 ** ADAM_STEP

    def body(p_ref, w_ref, m_ref, v_ref, g_ref, d_ref, nm_ref, nv_ref):
        g = p_ref[0].astype(F32)
        for s in range(1, nslot):
            g = g + p_ref[s].astype(F32)
        mn = ADAM_B1 * m_ref[...] + (1.0 - ADAM_B1) * g
        vn = ADAM_B2 * v_ref[...] + (1.0 - ADAM_B2) * (g * g)
        g_ref[...] = g
        nm_ref[...] = mn
        nv_ref[...] = vn
        d_ref[...] = -ADAM_LR * ((mn / c1) / (jnp.sqrt(vn / c2) + ADAM_EPS) + ADAM_WD * w_ref[...])

    blk = pl.BlockSpec((tr, cc), lambda i: (i, 0))
    return pl.pallas_call(
        body, name=name, grid=(rr // tr,),
        in_specs=[pl.BlockSpec((nslot, tr, cc), lambda i: (0, i, 0)), blk, blk, blk],
        out_specs=(blk,) * 4, out_shape=(_sds((rr, cc), F32),) * 4,
        compiler_params=_cp(("parallel",)))(parts, w, m, v)


RPB_N = NA_HEADS * 15 * 31
RPB_PAD = 4096
OWN_ROWS = NM + 8


def _pad_rows(a, rows):
    return jnp.pad(a, ((0, rows - a.shape[0]),) + ((0, 0),) * (a.ndim - 1))


def _pack_owned(meta_blk, lb_blk):
    return jnp.concatenate([meta_blk, _pad_rows(lb_blk.reshape(2, 128), 8)], axis=0)


LOSS_ROW = 28


def _pack_replicated(n_mix, n_mlp, n_final, hg_gain, rpb, loss_row=None):
    flat = _pad_rows(rpb.reshape(RPB_N), RPB_PAD)
    gain8 = _pad_rows(hg_gain.reshape(4, 128), 8)
    if loss_row is not None:
        gain8 = gain8 + jnp.pad(loss_row, ((LOSS_ROW - 24, 31 - LOSS_ROW), (0, 0)))
    return jnp.concatenate([n_mix.reshape(8, 128), n_mlp.reshape(8, 128), n_final.reshape(8, 128), gain8,
                            flat.reshape(32, 128)], axis=0)


def _unpack_replicated(a):
    return (a[0:8].reshape(1, D), a[8:16].reshape(1, D), a[16:24].reshape(D), a[24:28].reshape(1, 512),
            a[32:64].reshape(RPB_PAD)[:RPB_N].reshape(1, NA_HEADS, 15, 31))


def kernel(x, meta_tokens, w_in, w_na_out, w_hg_out, w_o, w_up, w_down, norm_mix, norm_mlp, norm_final, hg_norm, na_rpb, hg_lb_logits, loss_target, m_meta_tokens, m_w_in, m_w_na_out, m_w_hg_out, m_w_o, m_w_up, m_w_down, m_norm_mix, m_norm_mlp, m_norm_final, m_hg_norm, m_na_rpb, m_hg_lb_logits, v_meta_tokens, v_w_in, v_w_na_out, v_w_hg_out, v_w_o, v_w_up, v_w_down, v_norm_mix, v_norm_mlp, v_norm_final, v_hg_norm, v_na_rpb, v_hg_lb_logits):
    owned = _pack_owned(meta_tokens, hg_lb_logits)
    first_masks = (ALL_PEERS, SAME_CORE_AND_SIBLING)
    first, tok = _exchange_start([owned, w_in[0].astype(BF16)], [False] * 2, "gather_first_start", first_masks)
    bias_tab = _na_bias_table(_tie(jnp.pad(na_rpb[0], ((0, 0), (0, 0), (0, 128 - 31))), tok, "tie_bias_table"))
    later = [w[0].astype(BF16) for w in (w_na_out, w_hg_out, w_o, w_up, w_down)]
    lead = jnp.zeros((NM, D), F32) + tok[0, 0]
    h0_rows = jnp.concatenate([lead, x[0], jnp.zeros((T - L, D), F32)], axis=0)
    tgt = jnp.concatenate([lead, loss_target[0], jnp.zeros((T - L, D), F32)], axis=0)
    (owned_g, _), first = _exchange_wait(first, [False] * 2, [h0_rows], "gather_small_wait", first_masks, which=(0,))
    meta_full = jnp.transpose(owned_g[:, 0:NM, :], (1, 0, 2)).reshape(NM, D)
    logits = jnp.transpose(owned_g[:, NM:NM + 2, :].reshape(NDEV, 2, 2, 64), (1, 2, 0, 3)).reshape(2, 2, 512)
    h0 = lax.dynamic_update_slice(h0_rows, meta_full, (0, 0))
    a, a_t = _norm_fwd_t(h0, norm_mix, "norm_mix_fwd")
    (_, win_l), _ = _exchange_wait(first, [False] * 2, [a, logits, tgt, bias_tab] + later, "gather_first_wait", first_masks,
                                   which=(1,))
    (win_g,) = _forward_to_sibling([win_l], "gather_first_forward")
    later[0] = _tie(later[0], win_g, "tie_gather_rest")
    gather_rest, tok = _exchange_start(later, [False] * 5, "gather_rest_start")
    win_g = _tie(win_g, tok, "tie_inproj")

    p_act = _inproj_fwd(a, win_g)
    o_na, lse = _na_fwd(p_act, bias_tab)
    qh, k_f, b_f, k_b, b_b = _hg_pre(p_act, logits)
    o_f, st_f = _hg_scan_fwd(qh, k_f, b_f, p_act, False)
    o_b, st_b = _hg_scan_fwd(qh, k_b, b_b, p_act, True)
    (wna_g, whg_g, wo_g, _, _), gather_rest = _exchange_wait(
        gather_rest, [False] * 5, [o_f, o_b, o_na], "gather_rest_wait_a", which=(0, 1, 2))
    w_na_full = jnp.transpose(wna_g, (1, 0, 2)).reshape(512, D)
    w_hg_full = jnp.transpose(whg_g, (1, 0, 2)).reshape(512, D)
    mix, u_hg = _mix_fwd(o_na, o_f, o_b, hg_norm, w_na_full, w_hg_full, p_act)
    h1, m_act = _wo_fwd(mix, wo_g.reshape(D, D), h0, norm_mlp)
    (_, _, wo_g, wup_g, wdown_g), _ = _exchange_wait(gather_rest, [False] * 5, [m_act], "gather_rest_wait_b", which=(3, 4))
    w_o_full = wo_g.reshape(D, D)
    loss_part, dh2, dh2_b, d_nfinal = _mlp_fwd_loss(m_act, wup_g, wdown_g, h1, norm_final.reshape(1, D), tgt)

    dwup_p, dwdown_p, dh1, dh1_b, d_nmlp = _mlp_bwd(m_act, dh2_b, wup_g, wdown_g, h1, norm_mlp, dh2)
    sc_mlp, tok = _exchange_start([dwup_p, dwdown_p], [True] * 2, "scatter_mlp_start")
    dmix, dwo = _wo_bwd(_tie(dh1_b, tok, "tie_wo_bwd"), w_o_full, mix)
    sc_wo, tok = _exchange_start([dwo.reshape(NDEV, D // NDEV, D)], [True], "scatter_wo_start")
    dp_rest, dwna, dwhg, do_na, do_hg, d_gain = _mix_bwd(
        o_na, u_hg, o_f, o_b, hg_norm, w_na_full, w_hg_full, p_act, _tie(dmix, tok, "tie_mix_bwd"))
    owner_cols = lambda w: jnp.transpose(w.reshape(512, NDEV, D // NDEV), (1, 0, 2))
    sc_br, tok = _exchange_start([owner_cols(dwna), owner_cols(dwhg)], [True] * 2, "scatter_branch_start")
    do_hg = _tie(do_hg, tok, "tie_hg_scan_bwd")
    dq_f, dk_f, db_f, dv_f = _hg_scan_bwd(qh, k_f, b_f, p_act, st_f, do_hg, False)
    dq_b, dk_b, db_b, dv_b = _hg_scan_bwd(qh, k_b, b_b, p_act, st_b, do_hg, True)
    dp_rest, d_logits = _hg_pre_bwd(p_act, logits, dq_f, dq_b, dk_f, dk_b, db_f, db_b, dv_f, dv_b, dp_rest)
    dq_na, dk_na, dv_na, dbias = _na_bwd(p_act, do_na, lse, bias_tab)
    dp_na = jnp.concatenate([dq_na.astype(BF16), dk_na.astype(BF16), dv_na.astype(BF16)], axis=1)
    dwin_p = _inproj_bwd_dw(a_t, dp_na, dp_rest)
    by_core = dwin_p.reshape(NDEV // 2, 2, D, W_IN_B)
    core = lax.axis_index("c")
    mine = lax.dynamic_index_in_dim(by_core, core, 1, keepdims=False)
    theirs = lax.dynamic_index_in_dim(by_core, 1 - core, 1, keepdims=False)
    chip_sum = _add_bf16(mine, _sibling_swap(theirs, "pair_swap_in"), "pair_add_in")
    sc_in, tok = _exchange_start([chip_sum], [BY_CHIP], "scatter_in_start", SAME_CORE)
    dh0, d_nmix = _inproj_bwd_da(_tie(dp_na, tok, "tie_inproj_bwd_da"), dp_rest, win_g, h0, norm_mix, dh1)
    d_rpb = _na_rpb_reduce(_tie(dbias, tok, "tie_rpb_reduce"))[:, :, :31]

    res = {}

    def update(nm, parts, w, mm, vv):
        res[nm] = [r[None] for r in _adamw(parts, w[0], mm[0], vv[0], "adamw_" + nm)]
        return res[nm][1]

    wup_r, wdown_r = _exchange_wait(sc_mlp, [True] * 2, [dh0, d_rpb], "scatter_mlp_wait")
    update("w_up", wup_r, w_up, m_w_up, v_w_up)
    last = update("w_down", wdown_r, w_down, m_w_down, v_w_down)
    (wo_r,) = _exchange_wait(sc_wo, [True], [last], "scatter_wo_wait")
    last = update("w_o", wo_r, w_o, m_w_o, v_w_o)
    wna_r, whg_r = _exchange_wait(sc_br, [True] * 2, [last], "scatter_branch_wait")
    update("w_na_out", wna_r, w_na_out, m_w_na_out, v_w_na_out)
    last = update("w_hg_out", whg_r, w_hg_out, m_w_hg_out, v_w_hg_out)

    d_meta = jnp.transpose(dh0[0:NM].reshape(NM, NDEV, 128), (1, 0, 2))
    d_lg = jnp.transpose(d_logits.reshape(2, 2, NDEV, 64), (2, 0, 1, 3)).reshape(NDEV, 2, 128)
    owned_p = jnp.concatenate([d_meta, jnp.pad(d_lg, ((0, 0), (0, OWN_ROWS - NM - 2), (0, 0)))], axis=1)
    repl_p = _pack_replicated(d_nmix, d_nmlp, d_nfinal, d_gain, d_rpb, loss_part)
    grad_x = dh0[NM:L][None]
    done_first = [grad_x] + [res[nm][0] for nm in ("w_up", "w_down", "w_o", "w_na_out", "w_hg_out")]
    owned_r, repl_r = _exchange([owned_p, repl_p], [True, False], "scatter_small", done_first)
    own = _adamw(owned_r, owned, _pack_owned(m_meta_tokens, m_hg_lb_logits), _pack_owned(v_meta_tokens, v_hg_lb_logits),
                 "adamw_owned_small")
    res["meta_tokens"] = [r[0:NM] for r in own]
    res["hg_lb_logits"] = [r[NM:NM + 2].reshape(2, 2, 64) for r in own]
    rep = _adamw(repl_r, _pack_replicated(norm_mix, norm_mlp, norm_final, hg_norm, na_rpb),
                 _pack_replicated(m_norm_mix, m_norm_mlp, m_norm_final, m_hg_norm, m_na_rpb),
                 _pack_replicated(v_norm_mix, v_norm_mlp, v_norm_final, v_hg_norm, v_na_rpb), "adamw_replicated")
    for q in range(4):
        um = _unpack_replicated(rep[q])
        for nm, val in zip(("norm_mix", "norm_mlp", "norm_final", "hg_norm", "na_rpb"), um):
            res.setdefault(nm, [None] * 4)[q] = val
    (win_r,) = _exchange_wait(sc_in, [BY_CHIP], [rep[1], own[1]], "scatter_in_wait", SAME_CORE)
    update("w_in", win_r, w_in, m_w_in, v_w_in)

    loss = jnp.sum(repl_r[:, LOSS_ROW, 0])
    order = ("meta_tokens", "w_in", "w_na_out", "w_hg_out", "w_o", "w_up", "w_down", "norm_mix", "norm_mlp", "norm_final",
             "hg_norm", "na_rpb", "hg_lb_logits")
    outs = [loss, grad_x]
    for q in range(4):
        outs += [res[nm][q] for nm in order]
    return tuple(outs)
```

```python
import functools

import numpy as np
import jax
import jax.numpy as jnp
from jax import lax
from jax.experimental import pallas as pl
from jax.experimental.pallas import tpu as pltpu

F32 = jnp.float32
BF16 = jnp.bfloat16

D = 1024
SEQ = 2048
NM = 16
L = SEQ + NM
T = 2176
NDEV = 8
EPS = 1e-6
GRID_W = 64
ROWS = SEQ // GRID_W
NA_HEADS = 8
NA_DH = 64
NA_SCALE = NA_DH ** -0.5
HG_HEADS = 4
HG_C = 16
NCHUNK = L // HG_C
D_FF = 4096
IN_COLS = 6144
NEG = -1e30

ADAM_LR = 0.001
ADAM_B1 = 0.9
ADAM_B2 = 0.999
ADAM_EPS = 1e-08
ADAM_WD = 0.01
ADAM_STEP = 10

MESH_ID = pl.DeviceIdType.MESH
ANY = pl.BlockSpec(memory_space=pl.ANY)

NN = (((1,), (0,)), ((), ()))
NT = (((1,), (1,)), ((), ()))
TN = (((0,), (0,)), ((), ()))


def _cp(sem=None, vmem_mb=48):
    return pltpu.CompilerParams(dimension_semantics=sem, vmem_limit_bytes=vmem_mb * 1024 * 1024)


def _dot(a, b, dims=NN):
    return lax.dot_general(a, b, dims, preferred_element_type=F32)


def _sds(shape, dtype):
    return jax.ShapeDtypeStruct(shape, dtype)


HBM = pl.BlockSpec(memory_space=pltpu.HBM)
SEM = pl.BlockSpec(memory_space=pltpu.SEMAPHORE)
EFFECT = pltpu.SideEffectType.DATAFLOW_SIDE_EFFECTING


def _exchange(arrs, scatter, name, after=()):
    n = len(arrs)
    after = list(after)
    out_shapes = []
    for a, sc in zip(arrs, scatter):
        out_shapes.append(_sds(a.shape if sc else (NDEV,) + a.shape, a.dtype))

    def body(*refs):
        ins, outs = refs[:n], refs[n + len(after):2 * n + len(after)]
        send_sems, recv_sems, loc_sems = refs[2 * n + len(after):]
        me = 4 * lax.axis_index("x") + 2 * lax.axis_index("y") + lax.axis_index("c")
        copies = []
        for k in range(n):
            src_me = ins[k].at[me] if scatter[k] else ins[k]
            loc = pltpu.make_async_copy(src_me, outs[k].at[me], loc_sems.at[k])
            loc.start()
            copies.append(loc)
        remote = sum(_peer_copies(ins, outs, scatter, send_sems, recv_sems), [])
        for cp in remote:
            cp.start()
        for cp in remote:
            cp.wait_recv()
        for cp in remote:
            cp.wait_send()
        for cp in copies:
            cp.wait()

    return pl.pallas_call(
        body, name=name, out_shape=tuple(out_shapes), in_specs=[ANY] * (n + len(after)), out_specs=tuple([ANY] * n),
        scratch_shapes=[pltpu.SemaphoreType.DMA((n * (NDEV - 1),)), pltpu.SemaphoreType.DMA((n * (NDEV - 1),)),
                        pltpu.SemaphoreType.DMA((n,))],
    )(*arrs, *after)


def _forward_to_sibling(bufs, name):
    n = len(bufs)

    def body(*refs):
        ins, outs = refs[:n], refs[n:2 * n]
        send_sems, recv_sems = refs[2 * n:]
        x, y, c = lax.axis_index("x"), lax.axis_index("y"), lax.axis_index("c")
        copies = []
        for k in range(n):
            for j, (cx, cy) in enumerate(((1 - x, y), (x, 1 - y), (1 - x, 1 - y))):
                slot = 4 * cx + 2 * cy + c
                copies.append(pltpu.make_async_remote_copy(
                    src_ref=ins[k].at[slot], dst_ref=outs[k].at[slot], send_sem=send_sems.at[3 * k + j],
                    recv_sem=recv_sems.at[3 * k + j], device_id=(x, y, 1 - c), device_id_type=MESH_ID))
        for cp in copies:
            cp.start()
        for cp in copies:
            cp.wait_recv()
        for cp in copies:
            cp.wait_send()

    return pl.pallas_call(
        body, name=name, out_shape=tuple(_sds(b.shape, b.dtype) for b in bufs), in_specs=[ANY] * n,
        out_specs=tuple([ANY] * n), input_output_aliases={k: k for k in range(n)},
        scratch_shapes=[pltpu.SemaphoreType.DMA((3 * n,)), pltpu.SemaphoreType.DMA((3 * n,))],
    )(*bufs)


ALL_PEERS = tuple(range(1, NDEV))
SAME_CORE_AND_SIBLING = (1, 2, 4, 6)


def _peer_copies(srcs, lands, scatter, send_sems, recv_sems, masks=ALL_PEERS):
    x, y, c = lax.axis_index("x"), lax.axis_index("y"), lax.axis_index("c")
    me = 4 * x + 2 * y + c
    out = []
    for k in range(len(srcs)):
        out.append([])
        for m in (masks[k] if isinstance(masks[0], tuple) else masks):
            px, py, pc = x ^ (m >> 2), y ^ ((m >> 1) & 1), c ^ (m & 1)
            src = srcs[k].at[4 * px + 2 * py + pc] if scatter[k] else srcs[k]
            out[k].append(pltpu.make_async_remote_copy(
                src_ref=src, dst_ref=lands[k].at[me], send_sem=send_sems.at[k * (NDEV - 1) + m - 1],
                recv_sem=recv_sems.at[k * (NDEV - 1) + m - 1],
                device_id=(px, py, pc), device_id_type=MESH_ID))
    return out


def _exchange_start(arrs, scatter, name, masks=ALL_PEERS):
    n = len(arrs)
    me = 4 * lax.axis_index("x") + 2 * lax.axis_index("y") + lax.axis_index("c")
    lands = []
    for a, sc in zip(arrs, scatter):
        own = lax.dynamic_index_in_dim(a, me, 0, keepdims=True) if sc else a[None]
        shape = a.shape if sc else (NDEV,) + a.shape
        lands.append(lax.dynamic_update_index_in_dim(lax.empty(shape, a.dtype), own, me, 0))

    def body(*refs):
        srcs, lnds = refs[:n], refs[n:2 * n]
        send_sems, recv_sems = refs[2 * n], refs[2 * n + 1]
        token = refs[-1]
        for cp in sum(_peer_copies(srcs, lnds, scatter, send_sems, recv_sems, masks), []):
            cp.start()
        token[...] = jnp.zeros_like(token)

    ops = [pltpu.with_memory_space_constraint(a, pltpu.HBM) for a in list(arrs) + lands]
    res = pl.pallas_call(
        body, name=name,
        out_shape=(pltpu.SemaphoreType.DMA((n * (NDEV - 1),)), pltpu.SemaphoreType.DMA((n * (NDEV - 1),)))
        + tuple(pltpu.HBM(o.shape, o.dtype) for o in ops) + (_sds((8, 128), F32),),
        in_specs=[HBM] * (2 * n), out_specs=(SEM, SEM) + (HBM,) * (2 * n) + (pl.BlockSpec(memory_space=pltpu.VMEM),),
        input_output_aliases={k: 2 + k for k in range(2 * n)},
        compiler_params=pltpu.CompilerParams(has_side_effects=EFFECT),
    )(*ops)
    return res[:-1], res[-1]


def _exchange_wait(handle, scatter, after, name, masks=ALL_PEERS, which=None):
    send_sems, recv_sems = handle[0], handle[1]
    bufs = handle[2:]
    n = len(bufs) // 2
    after = list(after)

    def body(*refs):
        srcs, lnds = refs[:n], refs[n:2 * n]
        copies = _peer_copies(srcs, lnds, scatter, refs[2 * n], refs[2 * n + 1], masks)
        for k in (range(n) if which is None else which):
            for cp in copies[k]:
                cp.wait_send()
                cp.wait_recv()

    res = pl.pallas_call(
        body, name=name, out_shape=tuple(pltpu.HBM(b.shape, b.dtype) for b in bufs),
        in_specs=[HBM] * (2 * n) + [SEM, SEM] + [ANY] * len(after), out_specs=(HBM,) * (2 * n),
        input_output_aliases={k: k for k in range(2 * n)},
        compiler_params=pltpu.CompilerParams(has_side_effects=EFFECT),
    )(*bufs, send_sems, recv_sems, *after)
    return res[n:] if which is None else (res[n:], (send_sems, recv_sems) + tuple(res))


def _tie(x, token, name):
    def body(x_ref, t_ref, o_ref):
        del x_ref, t_ref, o_ref

    return pl.pallas_call(body, name=name, out_shape=_sds(x.shape, x.dtype), in_specs=[ANY, ANY], out_specs=ANY,
                          input_output_aliases={0: 0})(x, token)


TM_E = 272


def _norm_fwd_t(h, g, name):
    def body(h_ref, g_ref, o_ref, ot_ref):
        xv = h_ref[...]
        r = lax.rsqrt(jnp.mean(xv * xv, axis=-1, keepdims=True) + EPS)
        y = xv * r * g_ref[...]
        o_ref[...] = y.astype(BF16)
        ot_ref[...] = y.T.astype(BF16)

    return pl.pallas_call(
        body, name=name, grid=(T // 128,),
        in_specs=[pl.BlockSpec((128, D), lambda i: (i, 0)), pl.BlockSpec((1, D), lambda i: (0, 0))],
        out_specs=(pl.BlockSpec((128, D), lambda i: (i, 0)), pl.BlockSpec((D, 128), lambda i: (0, i))),
        out_shape=(_sds((T, D), BF16), _sds((D, T), BF16)), compiler_params=_cp(("parallel",)))(h, g)


def _norm_bwd_rows(xv, gv, dnv, dres):
    r = lax.rsqrt(jnp.mean(xv * xv, axis=-1, keepdims=True) + EPS)
    xh = xv * r
    dxh = dnv * gv
    dx = dres + r * (dxh - xh * jnp.mean(dxh * xh, axis=-1, keepdims=True))
    return dx, jnp.sum(dnv * xh, axis=0, keepdims=True)


TM_MM = 1088


def _inproj_fwd(a, w_g):
    nb = w_g.shape[2]

    def body(a_ref, w_ref, o_ref):
        o_ref[...] = _dot(a_ref[...], w_ref[0])

    return pl.pallas_call(
        body, name="inproj_fwd", grid=(T // TM_MM, NDEV),
        in_specs=[pl.BlockSpec((TM_MM, D), lambda i, j: (i, 0)), pl.BlockSpec((1, D, nb), lambda i, j: (j, 0, 0))],
        out_specs=pl.BlockSpec((TM_MM, nb), lambda i, j: (i, j)), out_shape=_sds((T, NDEV * nb), F32),
        compiler_params=_cp(("parallel", "parallel")))(a, w_g)


TM_B = 544


W_IN_B = IN_COLS // NDEV


NA_BLKS = 1536 // W_IN_B


def _dp_specs(rows, row_index):
    return [pl.BlockSpec((rows, W_IN_B), lambda *g: (row_index(*g), jnp.minimum(g[-1], NA_BLKS - 1))),
            pl.BlockSpec((rows, W_IN_B), lambda *g: (row_index(*g), jnp.maximum(g[-1] - NA_BLKS, 0)))]


def _inproj_bwd_dw(a_t, dp_na, dp_rest):
    def body(at_ref, na_ref, rest_ref, dw_ref):
        j = pl.program_id(0)

        @pl.when(j < NA_BLKS)
        def _():
            dw_ref[0] = _dot(at_ref[...], na_ref[...]).astype(BF16)

        @pl.when(j >= NA_BLKS)
        def _():
            dw_ref[0] = _dot(at_ref[...], rest_ref[...]).astype(BF16)

    return pl.pallas_call(
        body, name="inproj_bwd_dw", grid=(NDEV,),
        in_specs=[pl.BlockSpec((D, T), lambda j: (0, 0))] + _dp_specs(T, lambda j: 0),
        out_specs=pl.BlockSpec((1, D, W_IN_B), lambda j: (j, 0, 0)), out_shape=_sds((NDEV, D, W_IN_B), BF16),
        compiler_params=_cp(("parallel",)))(a_t, dp_na, dp_rest)


def _inproj_bwd_da(dp_na, dp_rest, w_g, h0, g_mix, dh1):
    nsub = TM_MM // TM_E

    def body(na_ref, rest_ref, w_ref, h0_ref, g_ref, dres_ref, dh0_ref, dg_ref, da):
        i, j = pl.program_id(0), pl.program_id(1)
        dpv = jnp.where(j < NA_BLKS, na_ref[...], rest_ref[...])
        dav = _dot(dpv, w_ref[0], NT)

        @pl.when(j == 0)
        def _():
            da[...] = dav

        @pl.when(j > 0)
        def _():
            da[...] += dav

        @pl.when(j == NDEV - 1)
        def _():
            gsum = jnp.zeros((1, D), F32)
            for s in range(nsub):
                sub = slice(s * TM_E, (s + 1) * TM_E)
                dx, gpart = _norm_bwd_rows(h0_ref[sub, :], g_ref[...], da[sub, :], dres_ref[sub, :])
                dh0_ref[sub, :] = dx
                gsum = gsum + gpart

            @pl.when(i == 0)
            def _():
                dg_ref[...] = gsum

            @pl.when(i > 0)
            def _():
                dg_ref[...] += gsum

    rblk = pl.BlockSpec((TM_MM, D), lambda i, j: (i, 0))
    vec = pl.BlockSpec((1, D), lambda i, j: (0, 0))
    return pl.pallas_call(
        body, name="inproj_bwd_da", grid=(T // TM_MM, NDEV),
        in_specs=_dp_specs(TM_MM, lambda i, j: i) + [pl.BlockSpec((1, D, W_IN_B), lambda i, j: (j, 0, 0)), rblk, vec, rblk],
        out_specs=(rblk, vec), out_shape=(_sds((T, D), F32), _sds((1, D), F32)),
        scratch_shapes=[pltpu.VMEM((TM_MM, D), F32)],
        compiler_params=_cp(("arbitrary", "arbitrary"), 56))(dp_na, dp_rest, w_g, h0, g_mix, dh1)


NA_QB = 256
NA_RT = NA_QB
NA_GROUPS = ROWS // 4
NA_UROWS = 11
NA_KW = NA_UROWS * GRID_W
NA_KU = 768


def _na_row_offset(var, i, j):
    valid = (j < 8, i <= j < i + 8, 3 <= j < NA_UROWS)[var]
    return (j - i + (7, 3, 0)[var]) if valid else None


def _na_bias_table(rp):
    def body(r_ref, o_ref):
        row3 = lax.broadcasted_iota(jnp.int32, (15, GRID_W, 128), 1)
        lane3 = lax.broadcasted_iota(jnp.int32, (15, GRID_W, 128), 2)
        w3 = lane3 & (GRID_W - 1)
        cs3 = jnp.clip(row3 - 8, 0, GRID_W - 16)
        lane = lax.broadcasted_iota(jnp.int32, (GRID_W, 128), 1)
        neg = jnp.full((GRID_W, 128), NEG, F32)
        z = jnp.stack([jnp.broadcast_to(r_ref[0, a:a + 1, :], (GRID_W, 128)) for a in range(15)])
        for bit in range(6):
            sh = 1 << bit
            z = jnp.where((row3 & sh) != 0, jnp.roll(z, sh, axis=2), z)
        z = jnp.roll(z, 128 - 15, axis=2)
        z = jnp.where(lane3 < GRID_W, z, 0.0)
        z = z + jnp.roll(z, GRID_W, axis=2)
        tabs = jnp.where((w3 >= cs3) & (w3 < cs3 + 16), z, NEG)
        tail = jnp.where(lane < GRID_W + NM, 0.0, NEG)
        for var in range(3):
            for i in range(4):
                for jp in range(NA_KU // 128):
                    halves = []
                    for j in (2 * jp, 2 * jp + 1):
                        a = _na_row_offset(var, i, j) if j < NA_UROWS else None
                        halves.append(tail if j >= NA_UROWS else (neg if a is None else tabs[a]))
                    o_ref[var, 0, i * 64:(i + 1) * 64, jp * 128:(jp + 1) * 128] = jnp.where(lane < GRID_W, halves[0], halves[1])

    return pl.pallas_call(
        body, name="na_bias_table", grid=(NA_HEADS,),
        in_specs=[pl.BlockSpec((1, 15, 128), lambda h: (h, 0, 0))],
        out_specs=pl.BlockSpec((3, 1, NA_QB, NA_KU), lambda h: (0, h, 0, 0)),
        out_shape=_sds((3, NA_HEADS, NA_QB, NA_KU), F32), compiler_params=_cp(("parallel",)))(rp)


def _na_var(g):
    return jnp.where(g == 0, 0, jnp.where(g == NA_GROUPS - 1, 2, 1))


def _na_load_window(src_ref, dst, g):
    us = jnp.clip(4 * g - 4, 0, ROWS - NA_UROWS)
    kstart = pl.multiple_of(NM + GRID_W * us, 16)
    dst[0:NA_KW, :] = src_ref[pl.ds(kstart, NA_KW), :].astype(BF16)
    dst[NA_KW:NA_KW + NM, :] = src_ref[0:NM, :].astype(BF16)
    dst[NA_KW + NM:, :] = jnp.zeros((NA_KU - NA_KW - NM, 128), BF16)
    return kstart


def _na_fwd(p_act, bias_tab):
    def body(q_ref, k_ref, v_ref, b_ref, o_ref, lse_ref, ku, vu):
        g = pl.program_id(1)
        _na_load_window(k_ref, ku, g)
        _na_load_window(v_ref, vu, g)
        lane = lax.broadcasted_iota(jnp.int32, (NA_RT, 128), 1)
        for rt in range(NA_QB // NA_RT):
            rows = pl.ds(pl.multiple_of(NM + NA_QB * g + NA_RT * rt, 16), NA_RT)
            tile = slice(NA_RT * rt, NA_RT * (rt + 1))
            q = q_ref[rows, :]
            o_h, lse_h = [], []
            for h in range(2):
                hm = (lane < 64) if h == 0 else (lane >= 64)
                qm = (jnp.where(hm, q, 0.0) * NA_SCALE).astype(BF16)
                s = _dot(qm, ku[...], NT) + b_ref[0, h, tile, :]
                m = jnp.max(s, axis=-1, keepdims=True)
                p = jnp.exp(s - m)
                l = jnp.sum(p, axis=-1, keepdims=True)
                o_h.append(_dot(p.astype(BF16), vu[...]) / l)
                lse_h.append(jnp.broadcast_to(m + jnp.log(l), (NA_RT, 128)))
            o_ref[rows, :] = jnp.where(lane < 64, o_h[0], o_h[1]).astype(BF16)
            lse_ref[0, rows, :] = jnp.where(lane < 64, lse_h[0], lse_h[1])

        @pl.when(g == 0)
        def _():
            qm_ = q_ref[0:NM, :]
            lane_m = lax.broadcasted_iota(jnp.int32, (NM, 128), 1)
            km, vm = ku[NA_KW:NA_KW + NM, :], vu[NA_KW:NA_KW + NM, :]
            om = []
            for h in range(2):
                hm = (lane_m < 64) if h == 0 else (lane_m >= 64)
                s = _dot(jnp.where(hm, qm_, 0.0).astype(BF16), km, NT) * NA_SCALE
                p = jnp.exp(s - jnp.max(s, axis=-1, keepdims=True))
                l = jnp.sum(p, axis=-1, keepdims=True)
                om.append(_dot(p.astype(BF16), vm) / l)
            o_ref[0:NM, :] = jnp.where(lane_m < 64, om[0], om[1]).astype(BF16)
            o_ref[L:T, :] = jnp.zeros((T - L, 128), BF16)
            lse_ref[0, 0:NM, :] = jnp.zeros((NM, 128), F32)
            lse_ref[0, L:T, :] = jnp.zeros((T - L, 128), F32)

    col = lambda off: pl.BlockSpec((T, 128), lambda hp, g: (0, off + hp))
    return pl.pallas_call(
        body, name="na_fwd", grid=(4, NA_GROUPS),
        in_specs=[col(0), col(4), col(8),
                  pl.BlockSpec((1, 2, NA_QB, NA_KU), lambda hp, g: (_na_var(g), hp, 0, 0))],
        out_specs=(pl.BlockSpec((T, 128), lambda hp, g: (0, hp)), pl.BlockSpec((1, T, 128), lambda hp, g: (hp, 0, 0))),
        out_shape=(_sds((T, 512), BF16), _sds((4, T, 128), F32)),
        scratch_shapes=[pltpu.VMEM((NA_KU, 128), BF16), pltpu.VMEM((NA_KU, 128), BF16)],
        compiler_params=_cp(("parallel", "arbitrary")))(p_act, p_act, p_act, bias_tab)


def _na_bwd(p_act, do, lse, bias_tab):
    def body(q_ref, k_ref, v_ref, do_ref, lse_ref, b_ref, dq_ref, dk_ref, dv_ref, db_ref, ku, vu):
        g = pl.program_id(1)

        @pl.when(g == 0)
        def _():
            dq_ref[...] = jnp.zeros((T, 128), F32)
            dk_ref[...] = jnp.zeros((T, 128), F32)
            dv_ref[...] = jnp.zeros((T, 128), F32)

        kstart = _na_load_window(k_ref, ku, g)
        _na_load_window(v_ref, vu, g)
        lane = lax.broadcasted_iota(jnp.int32, (NA_RT, 128), 1)
        first = (g == 0) | (g == 1) | (g == NA_GROUPS - 1)
        dku = jnp.zeros((NA_KU, 128), F32)
        dvu = jnp.zeros((NA_KU, 128), F32)
        for rt in range(NA_QB // NA_RT):
            rows = pl.ds(pl.multiple_of(NM + NA_QB * g + NA_RT * rt, 16), NA_RT)
            tile = slice(NA_RT * rt, NA_RT * (rt + 1))
            q = q_ref[rows, :]
            dov = do_ref[rows, :]
            lsev = lse_ref[0, rows, :]
            dq_h, ds_h = [], []
            for h in range(2):
                hm = (lane < 64) if h == 0 else (lane >= 64)
                qm = (jnp.where(hm, q, 0.0) * NA_SCALE).astype(BF16)
                dom = jnp.where(hm, dov, 0.0).astype(BF16)
                s = _dot(qm, ku[...], NT) + b_ref[0, h, tile, :]
                p = jnp.exp(s - lsev[:, 64 * h:64 * h + 1])
                dp = _dot(dom, vu[...], NT)
                delta = jnp.sum(p * dp, axis=-1, keepdims=True)
                ds = p * (dp - delta)
                ds_h.append(ds)
                dsb = ds.astype(BF16)
                dq_h.append(_dot(dsb, ku[...]) * NA_SCALE)
                dku = dku + _dot(dsb, qm, TN)
                dvu = dvu + _dot(p.astype(BF16), dom, TN)
            dq_ref[rows, :] = jnp.where(lane < 64, dq_h[0], dq_h[1])

            @pl.when(first)
            def _():
                for h in range(2):
                    db_ref[0, h, tile, :] = ds_h[h]

            @pl.when(jnp.logical_not(first))
            def _():
                for h in range(2):
                    db_ref[0, h, tile, :] += ds_h[h]
        dk_ref[pl.ds(kstart, NA_KW), :] += dku[0:NA_KW]
        dv_ref[pl.ds(kstart, NA_KW), :] += dvu[0:NA_KW]
        dk_ref[0:NM, :] += dku[NA_KW:NA_KW + NM]
        dv_ref[0:NM, :] += dvu[NA_KW:NA_KW + NM]

        @pl.when(g == 0)
        def _():
            qm_ = q_ref[0:NM, :]
            dom_ = do_ref[0:NM, :]
            lane_m = lax.broadcasted_iota(jnp.int32, (NM, 128), 1)
            km, vm = ku[NA_KW:NA_KW + NM, :], vu[NA_KW:NA_KW + NM, :]
            dqs = []
            dkm = jnp.zeros((NM, 128), F32)
            dvm = jnp.zeros((NM, 128), F32)
            for h in range(2):
                hm = (lane_m < 64) if h == 0 else (lane_m >= 64)
                qh = jnp.where(hm, qm_, 0.0).astype(BF16)
                doh = jnp.where(hm, dom_, 0.0).astype(BF16)
                s = _dot(qh, km, NT) * NA_SCALE
                e = jnp.exp(s - jnp.max(s, axis=-1, keepdims=True))
                p = e / jnp.sum(e, axis=-1, keepdims=True)
                dp = _dot(doh, vm, NT)
                ds = p * (dp - jnp.sum(p * dp, axis=-1, keepdims=True))
                dsb = (ds * NA_SCALE).astype(BF16)
                dqs.append(_dot(dsb, km))
                dkm = dkm + _dot(dsb, qh, TN)
                dvm = dvm + _dot(p.astype(BF16), doh, TN)
            dq_ref[0:NM, :] = jnp.where(lane_m < 64, dqs[0], dqs[1])
            dk_ref[0:NM, :] += dkm
            dv_ref[0:NM, :] += dvm

    col = lambda off: pl.BlockSpec((T, 128), lambda hp, g: (0, off + hp))
    ocol = pl.BlockSpec((T, 128), lambda hp, g: (0, hp))
    bspec = pl.BlockSpec((1, 2, NA_QB, NA_KU), lambda hp, g: (_na_var(g), hp, 0, 0))
    return pl.pallas_call(
        body, name="na_bwd", grid=(4, NA_GROUPS),
        in_specs=[col(0), col(4), col(8), ocol, pl.BlockSpec((1, T, 128), lambda hp, g: (hp, 0, 0)), bspec],
        out_specs=(ocol, ocol, ocol, bspec),
        out_shape=(_sds((T, 512), F32), _sds((T, 512), F32), _sds((T, 512), F32), _sds((3, NA_HEADS, NA_QB, NA_KU), F32)),
        scratch_shapes=[pltpu.VMEM((NA_KU, 128), BF16), pltpu.VMEM((NA_KU, 128), BF16)],
        compiler_params=_cp(("parallel", "arbitrary")))(p_act, p_act, p_act, do, lse, bias_tab)


def _na_rpb_reduce(dbias):
    def body(db_ref, o_ref):
        lane = lax.broadcasted_iota(jnp.int32, (GRID_W, 128), 1)
        row3 = lax.broadcasted_iota(jnp.int32, (15, GRID_W, 128), 1)
        lane3 = lax.broadcasted_iota(jnp.int32, (15, GRID_W, 128), 2)
        accs = []
        for a in range(15):
            acc = jnp.zeros((GRID_W, 128), F32)
            for var in range(3):
                for i in range(4):
                    for j in range(NA_UROWS):
                        if _na_row_offset(var, i, j) == a:
                            pair = db_ref[var, 0, i * 64:(i + 1) * 64, (j // 2) * 128:(j // 2 + 1) * 128]
                            acc = acc + jnp.where((lane < GRID_W) if j % 2 == 0 else (lane >= GRID_W), pair, 0.0)
            accs.append(acc)
        z = jnp.stack(accs)
        z = jnp.where(lane3 < GRID_W, z + jnp.roll(z, GRID_W, axis=2), 0.0)
        for bit in range(6):
            sh = 1 << bit
            z = jnp.where((row3 & sh) != 0, jnp.roll(z, 128 - sh, axis=2), z)
        z = jnp.roll(z, 15, axis=2)
        o_ref[0] = jnp.sum(z, axis=1)

    return pl.pallas_call(
        body, name="na_rpb_reduce", grid=(NA_HEADS,),
        in_specs=[pl.BlockSpec((3, 1, NA_QB, NA_KU), lambda h: (0, h, 0, 0))],
        out_specs=pl.BlockSpec((1, 15, 128), lambda h: (h, 0, 0)), out_shape=_sds((NA_HEADS, 15, 128), F32),
        compiler_params=_cp(("parallel",)))(dbias)


HG_RB = 128
HG_NB = T // HG_RB
HG_SLOTS = HG_NB * 8
HI = lax.Precision.HIGHEST
HG_UNROLL = 4
HG_UNROLL_WIDE = 8


def _chunk_tri(lower):
    r = lax.broadcasted_iota(jnp.int32, (HG_RB, HG_RB), 0)
    c = lax.broadcasted_iota(jnp.int32, (HG_RB, HG_RB), 1)
    same = (r // HG_C) == (c // HG_C)
    keep = (c <= r) if lower else (c >= r)
    return jnp.where(same & keep, 1.0, 0.0).astype(F32)


def _hg_gate_terms(z, lg):
    dl = lg[0:1, :] - lg[1:2, :]
    log_lb = jax.nn.log_sigmoid(dl)
    log_1mlb = jax.nn.log_sigmoid(-dl)
    yz = log_1mlb + jax.nn.log_sigmoid(z)
    log_f = jnp.logaddexp(log_lb, yz)
    snz = jax.nn.sigmoid(-z)
    k = jnp.exp(log_1mlb) * snz
    w2 = jnp.exp(yz - log_f)
    return log_f, k, snz, w2


def _hg_pre(p_act, logits):
    def body(q_ref, zf_ref, zb_ref, lg_ref, qh_ref, kf_ref, bf_ref, kb_ref, bb_ref):
        qh_ref[...] = jax.nn.silu(q_ref[...])
        lf, kf, _, _ = _hg_gate_terms(zf_ref[...], lg_ref[0])
        kf_ref[...] = kf
        bf_ref[...] = jnp.dot(_chunk_tri(True), lf, precision=HI, preferred_element_type=F32)
        lb_, kb, _, _ = _hg_gate_terms(zb_ref[...], lg_ref[1])
        kb_ref[...] = kb
        bb_ref[...] = jnp.dot(_chunk_tri(False), lb_, precision=HI, preferred_element_type=F32)

    blk = lambda c: pl.BlockSpec((HG_RB, 512), lambda i: (i, c))
    ob = pl.BlockSpec((HG_RB, 512), lambda i: (i, 0))
    return pl.pallas_call(
        body, name="hg_pre", grid=(HG_NB,),
        in_specs=[blk(3), blk(4), blk(5), pl.BlockSpec((2, 2, 512), lambda i: (0, 0, 0))],
        out_specs=(ob,) * 5, out_shape=(_sds((T, 512), F32),) * 5,
        compiler_params=_cp(("parallel",)))(p_act, p_act, p_act, logits)


def _bdot(a, b, ca, cb):
    return lax.dot_general(a.astype(BF16), b.astype(BF16), (((ca,), (cb,)), ((0,), (0,))), preferred_element_type=F32)


HG_S = 8
HG_NS = HG_RB // HG_S


def _lane_sums(xs):
    l_io = lax.broadcasted_iota(jnp.int32, (HG_NS, HG_S, HG_S), 2)
    a = jnp.zeros((HG_NS, HG_S, HG_S), F32)
    for j, x in enumerate(xs):
        a = a + jnp.where(l_io == j, jnp.sum(x, axis=-1, keepdims=True), 0.0)
    return a


def _halves(x):
    y = x.reshape(8, 2, HG_S, x.shape[-1])
    return y[:, 0], y[:, 1]


def _join(first, second):
    return jnp.stack([first, second], axis=1).reshape(HG_RB, first.shape[-1])


def _cross_split(rev, b4):
    b_1, b_2 = _halves(b4)
    if rev:
        r = b_2[:, 0:1, :]
        return jnp.exp(b_1 - r), jnp.exp(r - b_2)
    r = b_1[:, HG_S - 1:HG_S, :]
    return jnp.exp(b_2 - r), jnp.exp(r - b_1)


def _hg_scan_fwd(qh, k, b, p_act, rev):
    anchor = 0 if rev else HG_C - 1

    def body(q_ref, k_ref, b_ref, v_ref, o_ref, st_ref, dsc):
        def phase_a(blk, _):
            rows = pl.ds(pl.multiple_of(blk * HG_RB, HG_RB), HG_RB)
            b3 = b_ref[rows, :].reshape(8, HG_C, 128)
            k3 = k_ref[rows, :].reshape(8, HG_C, 128)
            v3 = v_ref[rows, :].reshape(8, HG_C, 128)
            bl = b3[:, anchor:anchor + 1, :]
            kt = k3 * jnp.exp(bl - b3)
            st_ref[0, pl.ds(pl.multiple_of(blk * 8, 8), 8)] = _bdot(v3, kt, 1, 1)
            dsc[pl.ds(pl.multiple_of(blk * 8, 8), 8), :] = jnp.exp(bl[:, 0, :])
            return 0

        lax.fori_loop(0, HG_NB, phase_a, 0, unroll=HG_UNROLL_WIDE)

        def phase_b(n, carry):
            c = (NCHUNK - 1 - n) if rev else n
            u = st_ref[0, c]
            st_ref[0, c] = carry
            return carry * dsc[pl.ds(c, 1), :] + u

        lax.fori_loop(0, NCHUNK // 3, lambda n3, s: phase_b(3 * n3 + 2, phase_b(3 * n3 + 1, phase_b(3 * n3, s))),
                      jnp.zeros((128, 128), F32))
        for c in range(NCHUNK, HG_SLOTS):
            st_ref[0, c] = jnp.zeros((128, 128), F32)

        t_io = lax.broadcasted_iota(jnp.int32, (HG_NS, HG_S, 128), 1)

        def phase_c(blk, _):
            rows = pl.ds(pl.multiple_of(blk * HG_RB, HG_RB), HG_RB)
            b4 = b_ref[rows, :].reshape(HG_NS, HG_S, 128)
            k4 = k_ref[rows, :].reshape(HG_NS, HG_S, 128)
            q4 = q_ref[rows, :].reshape(HG_NS, HG_S, 128)
            v4 = v_ref[rows, :].reshape(HG_NS, HG_S, 128)
            st = st_ref[0, pl.ds(pl.multiple_of(blk * 8, 8), 8)]
            o = _bdot((q4 * jnp.exp(b4)).reshape(8, HG_C, 128), st, 2, 2).reshape(HG_RB, 128)
            terms = []
            for s in range(HG_S):
                ok = (t_io <= s) if rev else (t_io >= s)
                f = jnp.exp(jnp.where(ok, b4 - b4[:, s:s + 1, :], NEG))
                terms.append(q4 * f * k4[:, s:s + 1, :])
            o_in = _bdot(_lane_sums(terms), v4, 2, 1)
            wq, wk = _cross_split(rev, b4)
            q_1, q_2 = _halves(q4)
            k_1, k_2 = _halves(k4)
            v_1, v_2 = _halves(v4)
            o_1, o_2 = _halves(o_in)
            if rev:
                o_1 = o_1 + _bdot(_bdot(q_1 * wq, k_2 * wk, 2, 2), v_2, 2, 1)
            else:
                o_2 = o_2 + _bdot(_bdot(q_2 * wq, k_1 * wk, 2, 2), v_1, 2, 1)
            o_ref[rows, :] = o + _join(o_1, o_2)
            return 0

        lax.fori_loop(0, HG_NB, phase_c, 0, unroll=HG_UNROLL_WIDE)

    col = pl.BlockSpec((T, 128), lambda h: (0, h))
    return pl.pallas_call(
        body, name="hg_scan_bwd_dir" if rev else "hg_scan_fwd_dir", grid=(HG_HEADS,),
        in_specs=[col, col, col, pl.BlockSpec((T, 128), lambda h: (0, 24 + h))],
        out_specs=(col, pl.BlockSpec((1, HG_SLOTS, 128, 128), lambda h: (h, 0, 0, 0))),
        out_shape=(_sds((T, 512), F32), _sds((HG_HEADS, HG_SLOTS, 128, 128), F32)),
        scratch_shapes=[pltpu.VMEM((HG_SLOTS, 128), F32)],
        compiler_params=_cp(("parallel",), 56))(qh, k, b, p_act)


def _hg_scan_bwd(qh, k, b, p_act, st, do, rev):
    anchor = 0 if rev else HG_C - 1

    def body(q_ref, k_ref, b_ref, v_ref, st_ref, do_ref, dq_ref, dk_ref, db_ref, dv_ref, gst, dsc, dbl):
        def phase_a(blk, _):
            rows = pl.ds(pl.multiple_of(blk * HG_RB, HG_RB), HG_RB)
            b3 = b_ref[rows, :].reshape(8, HG_C, 128)
            q3 = q_ref[rows, :].reshape(8, HG_C, 128)
            do3 = do_ref[rows, :].reshape(8, HG_C, 128)
            gst[pl.ds(pl.multiple_of(blk * 8, 8), 8)] = _bdot(do3, q3 * jnp.exp(b3), 1, 1)
            dsc[pl.ds(pl.multiple_of(blk * 8, 8), 8), :] = jnp.exp(b3[:, anchor, :])
            return 0

        lax.fori_loop(0, HG_NB, phase_a, 0, unroll=HG_UNROLL_WIDE)

        def phase_b(n, carry):
            c = n if rev else (NCHUNK - 1 - n)
            w = gst[c]
            gst[c] = carry
            dcv = dsc[pl.ds(c, 1), :]
            dbl[pl.ds(c, 1), :] = dcv * jnp.sum(st_ref[0, c] * carry, axis=0, keepdims=True)
            return carry * dcv + w

        lax.fori_loop(0, NCHUNK // 3, lambda n3, s: phase_b(3 * n3 + 2, phase_b(3 * n3 + 1, phase_b(3 * n3, s))),
                      jnp.zeros((128, 128), F32))
        for c in range(NCHUNK, HG_SLOTS):
            gst[c] = jnp.zeros((128, 128), F32)
            dbl[c:c + 1, :] = jnp.zeros((1, 128), F32)

        t_io = lax.broadcasted_iota(jnp.int32, (HG_NS, HG_S, 128), 1)
        t16 = lax.broadcasted_iota(jnp.int32, (8, HG_C, 128), 1)
        r_io = lax.broadcasted_iota(jnp.int32, (HG_NS, HG_S, HG_S), 1)
        l_io = lax.broadcasted_iota(jnp.int32, (HG_NS, HG_S, HG_S), 2)

        def phase_c(blk, _):
            rows = pl.ds(pl.multiple_of(blk * HG_RB, HG_RB), HG_RB)
            cs = pl.ds(pl.multiple_of(blk * 8, 8), 8)
            b4 = b_ref[rows, :].reshape(HG_NS, HG_S, 128)
            k4 = k_ref[rows, :].reshape(HG_NS, HG_S, 128)
            q4 = q_ref[rows, :].reshape(HG_NS, HG_S, 128)
            v4 = v_ref[rows, :].reshape(HG_NS, HG_S, 128)
            do4 = do_ref[rows, :].reshape(HG_NS, HG_S, 128)
            b3, k3, q3 = (z.reshape(8, HG_C, 128) for z in (b4, k4, q4))
            v3, do3 = v4.reshape(8, HG_C, 128), do4.reshape(8, HG_C, 128)
            s_t = st_ref[0, cs]
            g_t = gst[cs]
            bl = b3[:, anchor:anchor + 1, :]
            ekl = jnp.exp(bl - b3)
            kt = k3 * ekl
            dkt = _bdot(v3, g_t, 2, 1)
            dq = (_bdot(do3, s_t, 2, 1) * jnp.exp(b3)).reshape(HG_NS, HG_S, 128)
            dk = (dkt * ekl).reshape(HG_NS, HG_S, 128)
            dv = _bdot(kt, g_t, 2, 2).reshape(HG_NS, HG_S, 128)
            dbl3 = dbl[cs, :].reshape(8, 1, 128) + jnp.sum(dkt * kt, axis=1, keepdims=True)
            causal = (l_io >= r_io) if rev else (l_io <= r_io)
            da = jnp.where(causal, _bdot(do4, v4, 2, 2), 0.0)
            causal_t = (l_io <= r_io) if rev else (l_io >= r_io)
            dat = jnp.where(causal_t, _bdot(v4, do4, 2, 2), 0.0)
            for s in range(HG_S):
                ok = (t_io <= s) if rev else (t_io >= s)
                f = jnp.exp(jnp.where(ok, b4 - b4[:, s:s + 1, :], NEG))
                dq = dq + da[:, :, s:s + 1] * (f * k4[:, s:s + 1, :])
            terms = []
            for t in range(HG_S):
                ok = (t_io >= t) if rev else (t_io <= t)
                e = jnp.exp(jnp.where(ok, b4[:, t:t + 1, :] - b4, NEG))
                eq = e * q4[:, t:t + 1, :]
                dk = dk + dat[:, :, t:t + 1] * eq
                terms.append(eq * k4)
            dv = dv + _bdot(_lane_sums(terms), do4, 2, 1)
            wq, wk = _cross_split(rev, b4)
            pick = (lambda z: _halves(z)) if rev else (lambda z: _halves(z)[::-1])
            (q_q, _), (_, k_k), (_, v_k), (do_q, _) = pick(q4), pick(k4), pick(v4), pick(do4)
            qx, kx = q_q * wq, k_k * wk
            dq_q = _bdot(_bdot(do_q, v_k, 2, 2), kx, 2, 1) * wq
            dk_k = _bdot(_bdot(v_k, do_q, 2, 2), qx, 2, 1) * wk
            dv_k = _bdot(_bdot(kx, qx, 2, 2), do_q, 2, 1)
            zero = jnp.zeros((8, HG_S, 128), F32)
            place_q = (lambda z: _join(z, zero)) if rev else (lambda z: _join(zero, z))
            place_k = (lambda z: _join(zero, z)) if rev else (lambda z: _join(z, zero))
            dq2 = dq.reshape(HG_RB, 128) + place_q(dq_q)
            dk2 = dk.reshape(HG_RB, 128) + place_k(dk_k)
            dv2 = dv.reshape(HG_RB, 128) + place_k(dv_k)
            dq3, dk3 = dq2.reshape(8, HG_C, 128), dk2.reshape(8, HG_C, 128)
            db = q3 * dq3 - k3 * dk3 + jnp.where(t16 == anchor, dbl3, 0.0)
            dq_ref[rows, :] = dq2
            dk_ref[rows, :] = dk2
            db_ref[rows, :] = db.reshape(HG_RB, 128)
            dv_ref[rows, :] = dv2
            return 0

        lax.fori_loop(0, HG_NB, phase_c, 0, unroll=HG_UNROLL)

    col = pl.BlockSpec((T, 128), lambda h: (0, h))
    return pl.pallas_call(
        body, name="hg_scan_bwd_dir_bwd" if rev else "hg_scan_fwd_dir_bwd", grid=(HG_HEADS,),
        in_specs=[col, col, col, pl.BlockSpec((T, 128), lambda h: (0, 24 + h)),
                  pl.BlockSpec((1, HG_SLOTS, 128, 128), lambda h: (h, 0, 0, 0)), col],
        out_specs=(col,) * 4, out_shape=(_sds((T, 512), F32),) * 4,
        scratch_shapes=[pltpu.VMEM((HG_SLOTS, 128, 128), F32), pltpu.VMEM((HG_SLOTS, 128), F32),
                        pltpu.VMEM((HG_SLOTS, 128), F32)],
        compiler_params=_cp(("parallel",), 56))(qh, k, b, p_act, st, do)


def _row_valid(i, tm):
    r = lax.broadcasted_iota(jnp.int32, (tm, 1), 0) + i * tm
    return r < L


def _hg_post_rows(o, gv, gain_v, valid):
    parts = []
    for h in range(HG_HEADS):
        oh = o[:, 128 * h:128 * (h + 1)]
        parts.append(oh * lax.rsqrt(jnp.mean(oh * oh, axis=-1, keepdims=True) + EPS))
    return jnp.where(valid, jnp.concatenate(parts, axis=1) * gain_v * jax.nn.silu(gv), 0.0)


def _hg_post_bwd_rows(du, o, gv, gain_v, valid):
    duv = jnp.where(valid, du, 0.0)
    sig = jax.nn.sigmoid(gv)
    sg = gv * sig
    dn = duv * gain_v * sg
    do_parts, n_parts = [], []
    for h in range(HG_HEADS):
        sl = slice(128 * h, 128 * (h + 1))
        oh = o[:, sl]
        r = lax.rsqrt(jnp.mean(oh * oh, axis=-1, keepdims=True) + EPS)
        nh = oh * r
        dnh = dn[:, sl]
        do_parts.append(r * (dnh - nh * jnp.mean(dnh * nh, axis=-1, keepdims=True)))
        n_parts.append(nh)
    n = jnp.where(valid, jnp.concatenate(n_parts, axis=1), 0.0)
    do = jnp.where(valid, jnp.concatenate(do_parts, axis=1), 0.0)
    dg = duv * n * gain_v * (sig * (1.0 + gv * (1.0 - sig)))
    return do, dg, jnp.sum(duv * n * sg, axis=0, keepdims=True)


def _hg_pre_bwd(p_act, logits, dq_f, dq_b, dk_f, dk_b, db_f, db_b, dv_f, dv_b, dp_rest):
    def body(q_ref, zf_ref, zb_ref, lg_ref, dqf_ref, dqb_ref, dkf_ref, dkb_ref, dbf_ref, dbb_ref, dvf_ref, dvb_ref, _,
             dp_ref, dlg_ref):
        dq_ref, dzf_ref, dzb_ref, di_ref = (dp_ref.at[:, 512 * c:512 * (c + 1)] for c in range(4))
        i = pl.program_id(0)
        valid = _row_valid(i, HG_RB)
        qv = q_ref[...]
        sig = jax.nn.sigmoid(qv)
        dq_ref[...] = jnp.where(valid, (dqf_ref[...] + dqb_ref[...]) * (sig * (1.0 + qv * (1.0 - sig))), 0.0).astype(BF16)
        di_ref[...] = jnp.where(valid, dvf_ref[...] + dvb_ref[...], 0.0).astype(BF16)
        for d, (z_ref, dk_r, db_r, dz_ref) in enumerate(((zf_ref, dkf_ref, dbf_ref, dzf_ref), (zb_ref, dkb_ref, dbb_ref, dzb_ref))):
            lg = lg_ref[d]
            dl = lg[0:1, :] - lg[1:2, :]
            lb = jax.nn.sigmoid(dl)
            one_m_lb = jax.nn.sigmoid(-dl)
            log_f, _, snz, w2 = _hg_gate_terms(z_ref[...], lg)
            dbv = jnp.where(valid, db_r[...], 0.0)
            dkv = jnp.where(valid, dk_r[...], 0.0)
            dlf = jnp.dot(_chunk_tri(d == 1), dbv, precision=HI, preferred_element_type=F32)
            sz = 1.0 - snz
            dz_ref[...] = (dlf * w2 * snz - dkv * one_m_lb * sz * snz).astype(BF16)
            dlb = jnp.sum(dlf * snz * jnp.exp(-log_f) - dkv * snz, axis=0, keepdims=True)
            dl0 = dlb * lb * one_m_lb
            part = jnp.concatenate([dl0, -dl0], axis=0)

            @pl.when(i == 0)
            def _():
                dlg_ref[d] = part

            @pl.when(i > 0)
            def _():
                dlg_ref[d] += part

    blk = lambda c: pl.BlockSpec((HG_RB, 512), lambda i: (i, c))
    ob = pl.BlockSpec((HG_RB, 512), lambda i: (i, 0))
    lgs = pl.BlockSpec((2, 2, 512), lambda i: (0, 0, 0))
    return pl.pallas_call(
        body, name="hg_pre_bwd", grid=(HG_NB,),
        in_specs=[blk(3), blk(4), blk(5), lgs] + [ob] * 8 + [ANY],
        out_specs=(pl.BlockSpec((HG_RB, 2048), lambda i: (i, 0)), lgs),
        out_shape=(_sds(dp_rest.shape, BF16), _sds((2, 2, 512), F32)), input_output_aliases={12: 0},
        compiler_params=_cp(("arbitrary",)))(p_act, p_act, p_act, logits, dq_f, dq_b, dk_f, dk_b, db_f, db_b, dv_f, dv_b,
                                             dp_rest)


def _mix_fwd(o_na, o_f, o_b, gain, w_na, w_hg, p_act):
    def body(ona_ref, of_ref, ob_ref, g_ref, gain_ref, wna_ref, whg_ref, gna_ref, ghg_ref, o_ref, u_ref):
        u = _hg_post_rows(of_ref[...] + ob_ref[...], g_ref[...], gain_ref[...], _row_valid(pl.program_id(0), TM_B)).astype(BF16)
        u_ref[...] = u
        y_na = _dot(ona_ref[...], wna_ref[...])
        y_hg = _dot(u, whg_ref[...])
        o_ref[...] = (jax.nn.sigmoid(gna_ref[...]) * y_na + jax.nn.sigmoid(ghg_ref[...]) * y_hg).astype(BF16)

    act = pl.BlockSpec((TM_B, 512), lambda i: (i, 0))
    wsp = pl.BlockSpec((512, D), lambda i: (0, 0))
    return pl.pallas_call(
        body, name="mix_fwd", grid=(T // TM_B,),
        in_specs=[act, act, act, pl.BlockSpec((TM_B, 512), lambda i: (i, 7)), pl.BlockSpec((1, 512), lambda i: (0, 0)),
                  wsp, wsp, pl.BlockSpec((TM_B, D), lambda i: (i, 4)), pl.BlockSpec((TM_B, D), lambda i: (i, 5))],
        out_specs=(pl.BlockSpec((TM_B, D), lambda i: (i, 0)), act), out_shape=(_sds((T, D), BF16), _sds((T, 512), BF16)),
        compiler_params=_cp(("parallel",)))(o_na, o_f, o_b, p_act, gain, w_na, w_hg, p_act, p_act)


DP_REST = IN_COLS - 1536


def _mix_bwd(o_na, u_hg, o_f, o_b, gain, w_na, w_hg, p_act, dmix):
    ni = T // TM_B

    def body(ona_ref, uhg_ref, of_ref, ob_ref, g_ref, gain_ref, wna_ref, whg_ref, gna_ref, ghg_ref, dmix_ref,
             dp_ref, dwna_ref, dwhg_ref, dona_ref, do_ref, dgain_ref, acc_na, acc_hg):
        i = pl.program_id(0)
        dg_ref, dgna_ref, dghg_ref = dp_ref.at[:, 2048:2560], dp_ref.at[:, 2560:3584], dp_ref.at[:, 3584:4608]
        dm = dmix_ref[...].astype(F32)
        dxs = []
        for x_ref, w_ref, gt_ref, dgt_ref, dw_ref, acc in (
                (ona_ref, wna_ref, gna_ref, dgna_ref, dwna_ref, acc_na), (uhg_ref, whg_ref, ghg_ref, dghg_ref, dwhg_ref, acc_hg)):
            xv = x_ref[...]
            y = _dot(xv, w_ref[...])
            sg = jax.nn.sigmoid(gt_ref[...])
            dgt_ref[...] = (dm * y * sg * (1.0 - sg)).astype(BF16)
            dy = (dm * sg).astype(BF16)
            dxs.append(_dot(dy, w_ref[...], NT))
            part = _dot(xv, dy, TN)

            @pl.when(i == 0)
            def _():
                acc[...] = part

            @pl.when(i > 0)
            def _():
                acc[...] += part

            @pl.when(i == ni - 1)
            def _():
                dw_ref[...] = acc[...].astype(BF16)

        dona_ref[...] = dxs[0]
        do, dg, gpart = _hg_post_bwd_rows(dxs[1], of_ref[...] + ob_ref[...], g_ref[...], gain_ref[...], _row_valid(i, TM_B))
        do_ref[...] = do
        dg_ref[...] = dg.astype(BF16)

        @pl.when(i == 0)
        def _():
            dgain_ref[...] = gpart

        @pl.when(i > 0)
        def _():
            dgain_ref[...] += gpart

    act = pl.BlockSpec((TM_B, 512), lambda i: (i, 0))
    wsp = pl.BlockSpec((512, D), lambda i: (0, 0))
    rblk = pl.BlockSpec((TM_B, D), lambda i: (i, 0))
    vec = pl.BlockSpec((1, 512), lambda i: (0, 0))
    return pl.pallas_call(
        body, name="mix_bwd", grid=(ni,),
        in_specs=[act, act, act, act, pl.BlockSpec((TM_B, 512), lambda i: (i, 7)), vec, wsp, wsp,
                  pl.BlockSpec((TM_B, D), lambda i: (i, 4)), pl.BlockSpec((TM_B, D), lambda i: (i, 5)), rblk],
        out_specs=(pl.BlockSpec((TM_B, DP_REST), lambda i: (i, 0)), wsp, wsp, act, act, vec),
        out_shape=(_sds((T, DP_REST), BF16), _sds((512, D), BF16), _sds((512, D), BF16),
                   _sds((T, 512), F32), _sds((T, 512), F32), _sds((1, 512), F32)),
        scratch_shapes=[pltpu.VMEM((512, D), F32), pltpu.VMEM((512, D), F32)],
        compiler_params=_cp(("arbitrary",)))(o_na, u_hg, o_f, o_b, p_act, gain, w_na, w_hg, p_act, p_act, dmix)


def _wo_fwd(mix, w_o, h0, g_mlp):
    def body(mix_ref, w_ref, h0_ref, g_ref, h1_ref, m_ref):
        h1 = h0_ref[...] + _dot(mix_ref[...], w_ref[...])
        h1_ref[...] = h1
        r = lax.rsqrt(jnp.mean(h1 * h1, axis=-1, keepdims=True) + EPS)
        m_ref[...] = (h1 * r * g_ref[...]).astype(BF16)

    blk = pl.BlockSpec((TM_B, D), lambda i: (i, 0))
    return pl.pallas_call(
        body, name="wo_fwd", grid=(T // TM_B,),
        in_specs=[blk, pl.BlockSpec((D, D), lambda i: (0, 0)), blk, pl.BlockSpec((1, D), lambda i: (0, 0))],
        out_specs=(blk, blk), out_shape=(_sds((T, D), F32), _sds((T, D), BF16)),
        compiler_params=_cp(("parallel",)))(mix, w_o, h0, g_mlp)


def _wo_bwd(dh1_b, w_o, mix):
    ni = T // TM_B

    def body(dh_ref, w_ref, mix_ref, dmix_ref, dw_ref, acc):
        i = pl.program_id(0)
        dh = dh_ref[...]
        dmix_ref[...] = _dot(dh, w_ref[...], NT).astype(BF16)
        part = _dot(mix_ref[...], dh, TN)

        @pl.when(i == 0)
        def _():
            acc[...] = part

        @pl.when(i > 0)
        def _():
            acc[...] += part

        @pl.when(i == ni - 1)
        def _():
            dw_ref[...] = acc[...].astype(BF16)

    blk = pl.BlockSpec((TM_B, D), lambda i: (i, 0))
    wsp = pl.BlockSpec((D, D), lambda i: (0, 0))
    return pl.pallas_call(
        body, name="wo_bwd", grid=(ni,), in_specs=[blk, wsp, blk], out_specs=(blk, wsp),
        out_shape=(_sds((T, D), BF16), _sds((D, D), BF16)), scratch_shapes=[pltpu.VMEM((D, D), F32)],
        compiler_params=_cp(("arbitrary",)))(dh1_b, w_o, mix)


FF_B = D_FF // NDEV


def _loss_rows(xv, gv, tv, row0):
    r_io = lax.broadcasted_iota(jnp.int32, (xv.shape[0], 1), 0) + row0
    valid = (r_io >= NM) & (r_io < L)
    r = lax.rsqrt(jnp.mean(xv * xv, axis=-1, keepdims=True) + EPS)
    xh = xv * r
    err = jnp.where(valid, xh * gv - tv, 0.0)
    lpart = 0.5 * jnp.sum(jnp.sum(err * err, axis=-1, keepdims=True) * (1.0 / D), axis=0, keepdims=True)
    dy = err * (1.0 / D)
    dxh = dy * gv
    dh = r * (dxh - xh * jnp.mean(dxh * xh, axis=-1, keepdims=True))
    return lpart, dh, jnp.sum(dy * xh, axis=0, keepdims=True)


def _mlp_fwd_loss(m, wup_g, wdown_g, h1, g_final, tgt):
    nsub = TM_MM // TM_E

    def body(m_ref, wu_ref, wd_ref, h1_ref, g_ref, t_ref, loss_ref, dh_ref, dhb_ref, dg_ref, h2):
        i, j = pl.program_id(0), pl.program_id(1)
        up = jnp.maximum(_dot(m_ref[...], wu_ref[0]), 0.0)
        part = _dot((up * up).astype(BF16), wd_ref[0])

        @pl.when(j == 0)
        def _():
            h2[...] = h1_ref[...] + part

        @pl.when(j > 0)
        def _():
            h2[...] += part

        @pl.when(j == NDEV - 1)
        def _():
            lsum = jnp.zeros((1, 1), F32)
            gsum = jnp.zeros((1, D), F32)
            for s in range(nsub):
                rows = slice(s * TM_E, (s + 1) * TM_E)
                lpart, dh, gpart = _loss_rows(h2[rows, :], g_ref[...], t_ref[rows, :], i * TM_MM + s * TM_E)
                dh_ref[rows, :] = dh
                dhb_ref[rows, :] = dh.astype(BF16)
                lsum = lsum + lpart
                gsum = gsum + gpart
            lsum = jnp.broadcast_to(lsum, (1, 128))

            @pl.when(i == 0)
            def _():
                loss_ref[...] = lsum
                dg_ref[...] = gsum

            @pl.when(i > 0)
            def _():
                loss_ref[...] += lsum
                dg_ref[...] += gsum

    blk = pl.BlockSpec((TM_MM, D), lambda i, j: (i, 0))
    vec = pl.BlockSpec((1, D), lambda i, j: (0, 0))
    return pl.pallas_call(
        body, name="mlp_fwd_loss", grid=(T // TM_MM, NDEV),
        in_specs=[blk, pl.BlockSpec((1, D, FF_B), lambda i, j: (j, 0, 0)), pl.BlockSpec((1, FF_B, D), lambda i, j: (j, 0, 0)),
                  blk, vec, blk],
        out_specs=(pl.BlockSpec((1, 128), lambda i, j: (0, 0)), blk, blk, vec),
        out_shape=(_sds((1, 128), F32), _sds((T, D), F32), _sds((T, D), BF16), _sds((1, D), F32)),
        scratch_shapes=[pltpu.VMEM((TM_MM, D), F32)],
        compiler_params=_cp(("arbitrary", "arbitrary"), 56))(m, wup_g, wdown_g, h1, g_final, tgt)


def _mlp_bwd(m, dh2_b, wup_g, wdown_g, h1, g_mlp, dh2):
    ni = T // TM_B
    nsub = TM_B // TM_E

    def body(m_ref, dh_ref, wu_ref, wd_ref, h1_ref, g_ref, dres_ref, dwu_ref, dwd_ref, dh1_ref, dh1b_ref, dg_ref,
             dm_ref, acc_u, acc_d):
        j, i = pl.program_id(0), pl.program_id(1)
        rows = pl.ds(pl.multiple_of(i * TM_B, TM_B), TM_B)
        mv, dh = m_ref[...], dh_ref[...]
        r = jnp.maximum(_dot(mv, wu_ref[0]), 0.0)
        act = (r * r).astype(BF16)
        dact = _dot(dh, wd_ref[0], NT)
        dup = (dact * (2.0 * r)).astype(BF16)
        pd = _dot(act, dh, TN)
        pu = _dot(mv, dup, TN)
        dmv = _dot(dup, wu_ref[0], NT)

        @pl.when(i == 0)
        def _():
            acc_u[...] = pu
            acc_d[...] = pd

        @pl.when(i > 0)
        def _():
            acc_u[...] += pu
            acc_d[...] += pd

        @pl.when(i == ni - 1)
        def _():
            dwu_ref[0] = acc_u[...].astype(BF16)
            dwd_ref[0] = acc_d[...].astype(BF16)

        @pl.when(j == 0)
        def _():
            dm_ref[rows, :] = dmv

        @pl.when(j > 0)
        def _():
            dm_ref[rows, :] += dmv

        @pl.when(j == NDEV - 1)
        def _():
            gsum = jnp.zeros((1, D), F32)
            for s in range(nsub):
                sub = slice(s * TM_E, (s + 1) * TM_E)
                dm_rows = dm_ref[pl.ds(pl.multiple_of(i * TM_B + s * TM_E, TM_E), TM_E), :]
                dx, gpart = _norm_bwd_rows(h1_ref[sub, :], g_ref[...], dm_rows, dres_ref[sub, :])
                dh1_ref[sub, :] = dx
                dh1b_ref[sub, :] = dx.astype(BF16)
                gsum = gsum + gpart

            @pl.when(i == 0)
            def _():
                dg_ref[...] = gsum

            @pl.when(i > 0)
            def _():
                dg_ref[...] += gsum

    blk = pl.BlockSpec((TM_B, D), lambda j, i: (i, 0))
    late = pl.BlockSpec((TM_B, D), lambda j, i: (jnp.where(j == NDEV - 1, i, 0), 0))
    vec = pl.BlockSpec((1, D), lambda j, i: (0, 0))
    wus = pl.BlockSpec((1, D, FF_B), lambda j, i: (j, 0, 0))
    wds = pl.BlockSpec((1, FF_B, D), lambda j, i: (j, 0, 0))
    return pl.pallas_call(
        body, name="mlp_bwd", grid=(NDEV, ni), in_specs=[blk, blk, wus, wds, late, vec, late],
        out_specs=(wus, wds, late, late, vec),
        out_shape=(_sds((NDEV, D, FF_B), BF16), _sds((NDEV, FF_B, D), BF16), _sds((T, D), F32), _sds((T, D), BF16),
                   _sds((1, D), F32)),
        scratch_shapes=[pltpu.VMEM((T, D), F32), pltpu.VMEM((D, FF_B), F32), pltpu.VMEM((FF_B, D), F32)],
        compiler_params=_cp(("arbitrary", "arbitrary"), 56))(m, dh2_b, wup_g, wdown_g, h1, g_mlp, dh2)


def _adamw(parts, w, m, v, name):
    rr, cc = w.shape
    nslot = parts.shape[0]
    tr = rr
    for cand in (256, 128, 64):
        if rr % cand == 0 and rr > cand:
            tr = cand
            break
    c1 = 1.0 - ADAM_B1 ** ADAM_STEP
    c2 = 1.0 - ADAM_B2 ** ADAM_STEP

    def body(p_ref, w_ref, m_ref, v_ref, g_ref, d_ref, nm_ref, nv_ref):
        g = p_ref[0].astype(F32)
        for s in range(1, nslot):
            g = g + p_ref[s].astype(F32)
        mn = ADAM_B1 * m_ref[...] + (1.0 - ADAM_B1) * g
        vn = ADAM_B2 * v_ref[...] + (1.0 - ADAM_B2) * (g * g)
        g_ref[...] = g
        nm_ref[...] = mn
        nv_ref[...] = vn
        d_ref[...] = -ADAM_LR * ((mn / c1) / (jnp.sqrt(vn / c2) + ADAM_EPS) + ADAM_WD * w_ref[...])

    blk = pl.BlockSpec((tr, cc), lambda i: (i, 0))
    return pl.pallas_call(
        body, name=name, grid=(rr // tr,),
        in_specs=[pl.BlockSpec((nslot, tr, cc), lambda i: (0, i, 0)), blk, blk, blk],
        out_specs=(blk,) * 4, out_shape=(_sds((rr, cc), F32),) * 4,
        compiler_params=_cp(("parallel",)))(parts, w, m, v)


RPB_N = NA_HEADS * 15 * 31
RPB_PAD = 4096
OWN_ROWS = NM + 8


def _pad_rows(a, rows):
    return jnp.pad(a, ((0, rows - a.shape[0]),) + ((0, 0),) * (a.ndim - 1))


def _pack_owned(meta_blk, lb_blk):
    return jnp.concatenate([meta_blk, _pad_rows(lb_blk.reshape(2, 128), 8)], axis=0)


LOSS_ROW = 28


def _pack_replicated(n_mix, n_mlp, n_final, hg_gain, rpb, loss_row=None):
    flat = _pad_rows(rpb.reshape(RPB_N), RPB_PAD)
    gain8 = _pad_rows(hg_gain.reshape(4, 128), 8)
    if loss_row is not None:
        gain8 = gain8 + jnp.pad(loss_row, ((LOSS_ROW - 24, 31 - LOSS_ROW), (0, 0)))
    return jnp.concatenate([n_mix.reshape(8, 128), n_mlp.reshape(8, 128), n_final.reshape(8, 128), gain8,
                            flat.reshape(32, 128)], axis=0)


def _unpack_replicated(a):
    return (a[0:8].reshape(1, D), a[8:16].reshape(1, D), a[16:24].reshape(D), a[24:28].reshape(1, 512),
            a[32:64].reshape(RPB_PAD)[:RPB_N].reshape(1, NA_HEADS, 15, 31))


def kernel(x, meta_tokens, w_in, w_na_out, w_hg_out, w_o, w_up, w_down, norm_mix, norm_mlp, norm_final, hg_norm, na_rpb, hg_lb_logits, loss_target, m_meta_tokens, m_w_in, m_w_na_out, m_w_hg_out, m_w_o, m_w_up, m_w_down, m_norm_mix, m_norm_mlp, m_norm_final, m_hg_norm, m_na_rpb, m_hg_lb_logits, v_meta_tokens, v_w_in, v_w_na_out, v_w_hg_out, v_w_o, v_w_up, v_w_down, v_norm_mix, v_norm_mlp, v_norm_final, v_hg_norm, v_na_rpb, v_hg_lb_logits):
    owned = _pack_owned(meta_tokens, hg_lb_logits)
    first_masks = (ALL_PEERS, SAME_CORE_AND_SIBLING)
    first, tok = _exchange_start([owned, w_in[0].astype(BF16)], [False] * 2, "gather_first_start", first_masks)
    bias_tab = _na_bias_table(_tie(jnp.pad(na_rpb[0], ((0, 0), (0, 0), (0, 128 - 31))), tok, "tie_bias_table"))
    later = [w[0].astype(BF16) for w in (w_na_out, w_hg_out, w_o, w_up, w_down)]
    lead = jnp.zeros((NM, D), F32) + tok[0, 0]
    h0_rows = jnp.concatenate([lead, x[0], jnp.zeros((T - L, D), F32)], axis=0)
    tgt = jnp.concatenate([lead, loss_target[0], jnp.zeros((T - L, D), F32)], axis=0)
    (owned_g, _), first = _exchange_wait(first, [False] * 2, [h0_rows], "gather_small_wait", first_masks, which=(0,))
    meta_full = jnp.transpose(owned_g[:, 0:NM, :], (1, 0, 2)).reshape(NM, D)
    logits = jnp.transpose(owned_g[:, NM:NM + 2, :].reshape(NDEV, 2, 2, 64), (1, 2, 0, 3)).reshape(2, 2, 512)
    h0 = lax.dynamic_update_slice(h0_rows, meta_full, (0, 0))
    a, a_t = _norm_fwd_t(h0, norm_mix, "norm_mix_fwd")
    (_, win_l), _ = _exchange_wait(first, [False] * 2, [a, logits, tgt, bias_tab] + later, "gather_first_wait", first_masks,
                                   which=(1,))
    (win_g,) = _forward_to_sibling([win_l], "gather_first_forward")
    later[0] = _tie(later[0], win_g, "tie_gather_rest")
    gather_rest, tok = _exchange_start(later, [False] * 5, "gather_rest_start")
    win_g = _tie(win_g, tok, "tie_inproj")

    p_act = _inproj_fwd(a, win_g)
    o_na, lse = _na_fwd(p_act, bias_tab)
    qh, k_f, b_f, k_b, b_b = _hg_pre(p_act, logits)
    o_f, st_f = _hg_scan_fwd(qh, k_f, b_f, p_act, False)
    o_b, st_b = _hg_scan_fwd(qh, k_b, b_b, p_act, True)
    (wna_g, whg_g, wo_g, _, _), gather_rest = _exchange_wait(
        gather_rest, [False] * 5, [o_f, o_b, o_na], "gather_rest_wait_a", which=(0, 1, 2))
    w_na_full = jnp.transpose(wna_g, (1, 0, 2)).reshape(512, D)
    w_hg_full = jnp.transpose(whg_g, (1, 0, 2)).reshape(512, D)
    mix, u_hg = _mix_fwd(o_na, o_f, o_b, hg_norm, w_na_full, w_hg_full, p_act)
    h1, m_act = _wo_fwd(mix, wo_g.reshape(D, D), h0, norm_mlp)
    (_, _, wo_g, wup_g, wdown_g), _ = _exchange_wait(gather_rest, [False] * 5, [m_act], "gather_rest_wait_b", which=(3, 4))
    w_o_full = wo_g.reshape(D, D)
    loss_part, dh2, dh2_b, d_nfinal = _mlp_fwd_loss(m_act, wup_g, wdown_g, h1, norm_final.reshape(1, D), tgt)

    dwup_p, dwdown_p, dh1, dh1_b, d_nmlp = _mlp_bwd(m_act, dh2_b, wup_g, wdown_g, h1, norm_mlp, dh2)
    sc_mlp, tok = _exchange_start([dwup_p, dwdown_p], [True] * 2, "scatter_mlp_start")
    dmix, dwo = _wo_bwd(_tie(dh1_b, tok, "tie_wo_bwd"), w_o_full, mix)
    sc_wo, tok = _exchange_start([dwo.reshape(NDEV, D // NDEV, D)], [True], "scatter_wo_start")
    dp_rest, dwna, dwhg, do_na, do_hg, d_gain = _mix_bwd(
        o_na, u_hg, o_f, o_b, hg_norm, w_na_full, w_hg_full, p_act, _tie(dmix, tok, "tie_mix_bwd"))
    owner_cols = lambda w: jnp.transpose(w.reshape(512, NDEV, D // NDEV), (1, 0, 2))
    sc_br, tok = _exchange_start([owner_cols(dwna), owner_cols(dwhg)], [True] * 2, "scatter_branch_start")
    do_hg = _tie(do_hg, tok, "tie_hg_scan_bwd")
    dq_f, dk_f, db_f, dv_f = _hg_scan_bwd(qh, k_f, b_f, p_act, st_f, do_hg, False)
    dq_b, dk_b, db_b, dv_b = _hg_scan_bwd(qh, k_b, b_b, p_act, st_b, do_hg, True)
    dp_rest, d_logits = _hg_pre_bwd(p_act, logits, dq_f, dq_b, dk_f, dk_b, db_f, db_b, dv_f, dv_b, dp_rest)
    dq_na, dk_na, dv_na, dbias = _na_bwd(p_act, do_na, lse, bias_tab)
    dp_na = jnp.concatenate([dq_na.astype(BF16), dk_na.astype(BF16), dv_na.astype(BF16)], axis=1)
    dwin_p = _inproj_bwd_dw(a_t, dp_na, dp_rest)
    sc_in, tok = _exchange_start([dwin_p], [True], "scatter_in_start")
    dh0, d_nmix = _inproj_bwd_da(_tie(dp_na, tok, "tie_inproj_bwd_da"), dp_rest, win_g, h0, norm_mix, dh1)
    d_rpb = _na_rpb_reduce(_tie(dbias, tok, "tie_rpb_reduce"))[:, :, :31]

    res = {}

    def update(nm, parts, w, mm, vv):
        res[nm] = [r[None] for r in _adamw(parts, w[0], mm[0], vv[0], "adamw_" + nm)]
        return res[nm][1]

    wup_r, wdown_r = _exchange_wait(sc_mlp, [True] * 2, [dh0, d_rpb], "scatter_mlp_wait")
    update("w_up", wup_r, w_up, m_w_up, v_w_up)
    last = update("w_down", wdown_r, w_down, m_w_down, v_w_down)
    (wo_r,) = _exchange_wait(sc_wo, [True], [last], "scatter_wo_wait")
    last = update("w_o", wo_r, w_o, m_w_o, v_w_o)
    wna_r, whg_r = _exchange_wait(sc_br, [True] * 2, [last], "scatter_branch_wait")
    update("w_na_out", wna_r, w_na_out, m_w_na_out, v_w_na_out)
    last = update("w_hg_out", whg_r, w_hg_out, m_w_hg_out, v_w_hg_out)

    d_meta = jnp.transpose(dh0[0:NM].reshape(NM, NDEV, 128), (1, 0, 2))
    d_lg = jnp.transpose(d_logits.reshape(2, 2, NDEV, 64), (2, 0, 1, 3)).reshape(NDEV, 2, 128)
    owned_p = jnp.concatenate([d_meta, jnp.pad(d_lg, ((0, 0), (0, OWN_ROWS - NM - 2), (0, 0)))], axis=1)
    repl_p = _pack_replicated(d_nmix, d_nmlp, d_nfinal, d_gain, d_rpb, loss_part)
    grad_x = dh0[NM:L][None]
    done_first = [grad_x] + [res[nm][0] for nm in ("w_up", "w_down", "w_o", "w_na_out", "w_hg_out")]
    owned_r, repl_r = _exchange([owned_p, repl_p], [True, False], "scatter_small", done_first)
    own = _adamw(owned_r, owned, _pack_owned(m_meta_tokens, m_hg_lb_logits), _pack_owned(v_meta_tokens, v_hg_lb_logits),
                 "adamw_owned_small")
    res["meta_tokens"] = [r[0:NM] for r in own]
    res["hg_lb_logits"] = [r[NM:NM + 2].reshape(2, 2, 64) for r in own]
    rep = _adamw(repl_r, _pack_replicated(norm_mix, norm_mlp, norm_final, hg_norm, na_rpb),
                 _pack_replicated(m_norm_mix, m_norm_mlp, m_norm_final, m_hg_norm, m_na_rpb),
                 _pack_replicated(v_norm_mix, v_norm_mlp, v_norm_final, v_hg_norm, v_na_rpb), "adamw_replicated")
    for q in range(4):
        um = _unpack_replicated(rep[q])
        for nm, val in zip(("norm_mix", "norm_mlp", "norm_final", "hg_norm", "na_rpb"), um):
            res.setdefault(nm, [None] * 4)[q] = val
    (win_r,) = _exchange_wait(sc_in, [True], [rep[1], own[1]], "scatter_in_wait")
    update("w_in", win_r, w_in, m_w_in, v_w_in)

    loss = jnp.sum(repl_r[:, LOSS_ROW, 0])
    order = ("meta_tokens", "w_in", "w_na_out", "w_hg_out", "w_o", "w_up", "w_down", "norm_mix", "norm_mlp", "norm_final",
             "hg_norm", "na_rpb", "hg_lb_logits")
    outs = [loss, grad_x]
    for q in range(4):
        outs += [res[nm][q] for nm in order]
    return tuple(outs)
```

```python
import functools

import numpy as np
import jax
import jax.numpy as jnp
from jax import lax
from jax.experimental import pallas as pl
from jax.experimental.pallas import tpu as pltpu

F32 = jnp.float32
BF16 = jnp.bfloat16

D = 1024
SEQ = 2048
NM = 16
L = SEQ + NM
T = 2176
NDEV = 8
EPS = 1e-6
GRID_W = 64
ROWS = SEQ // GRID_W
NA_HEADS = 8
NA_DH = 64
NA_SCALE = NA_DH ** -0.5
HG_HEADS = 4
HG_C = 16
NCHUNK = L // HG_C
D_FF = 4096
IN_COLS = 6144
NEG = -1e30

ADAM_LR = 0.001
ADAM_B1 = 0.9
ADAM_B2 = 0.999
ADAM_EPS = 1e-08
ADAM_WD = 0.01
ADAM_STEP = 10

MESH_ID = pl.DeviceIdType.MESH
ANY = pl.BlockSpec(memory_space=pl.ANY)

NN = (((1,), (0,)), ((), ()))
NT = (((1,), (1,)), ((), ()))
TN = (((0,), (0,)), ((), ()))


def _cp(sem=None, vmem_mb=48):
    return pltpu.CompilerParams(dimension_semantics=sem, vmem_limit_bytes=vmem_mb * 1024 * 1024)


def _dot(a, b, dims=NN):
    return lax.dot_general(a, b, dims, preferred_element_type=F32)


def _sds(shape, dtype):
    return jax.ShapeDtypeStruct(shape, dtype)


HBM = pl.BlockSpec(memory_space=pltpu.HBM)
SEM = pl.BlockSpec(memory_space=pltpu.SEMAPHORE)
EFFECT = pltpu.SideEffectType.DATAFLOW_SIDE_EFFECTING


def _exchange(arrs, scatter, name, after=()):
    n = len(arrs)
    after = list(after)
    out_shapes = []
    for a, sc in zip(arrs, scatter):
        out_shapes.append(_sds(a.shape if sc else (NDEV,) + a.shape, a.dtype))

    def body(*refs):
        ins, outs = refs[:n], refs[n + len(after):2 * n + len(after)]
        send_sems, recv_sems, loc_sems = refs[2 * n + len(after):]
        me = 4 * lax.axis_index("x") + 2 * lax.axis_index("y") + lax.axis_index("c")
        copies = []
        for k in range(n):
            src_me = ins[k].at[me] if scatter[k] else ins[k]
            loc = pltpu.make_async_copy(src_me, outs[k].at[me], loc_sems.at[k])
            loc.start()
            copies.append(loc)
        remote = sum(_peer_copies(ins, outs, scatter, send_sems, recv_sems), [])
        for cp in remote:
            cp.start()
        for cp in remote:
            cp.wait_recv()
        for cp in remote:
            cp.wait_send()
        for cp in copies:
            cp.wait()

    return pl.pallas_call(
        body, name=name, out_shape=tuple(out_shapes), in_specs=[ANY] * (n + len(after)), out_specs=tuple([ANY] * n),
        scratch_shapes=[pltpu.SemaphoreType.DMA((n * (NDEV - 1),)), pltpu.SemaphoreType.DMA((n * (NDEV - 1),)),
                        pltpu.SemaphoreType.DMA((n,))],
    )(*arrs, *after)


def _forward_to_sibling(bufs, name):
    n = len(bufs)

    def body(*refs):
        ins, outs = refs[:n], refs[n:2 * n]
        send_sems, recv_sems = refs[2 * n:]
        x, y, c = lax.axis_index("x"), lax.axis_index("y"), lax.axis_index("c")
        copies = []
        for k in range(n):
            for j, (cx, cy) in enumerate(((1 - x, y), (x, 1 - y), (1 - x, 1 - y))):
                slot = 4 * cx + 2 * cy + c
                copies.append(pltpu.make_async_remote_copy(
                    src_ref=ins[k].at[slot], dst_ref=outs[k].at[slot], send_sem=send_sems.at[3 * k + j],
                    recv_sem=recv_sems.at[3 * k + j], device_id=(x, y, 1 - c), device_id_type=MESH_ID))
        for cp in copies:
            cp.start()
        for cp in copies:
            cp.wait_recv()
        for cp in copies:
            cp.wait_send()

    return pl.pallas_call(
        body, name=name, out_shape=tuple(_sds(b.shape, b.dtype) for b in bufs), in_specs=[ANY] * n,
        out_specs=tuple([ANY] * n), input_output_aliases={k: k for k in range(n)},
        scratch_shapes=[pltpu.SemaphoreType.DMA((3 * n,)), pltpu.SemaphoreType.DMA((3 * n,))],
    )(*bufs)


ALL_PEERS = tuple(range(1, NDEV))
SAME_CORE_AND_SIBLING = (1, 2, 4, 6)


def _peer_copies(srcs, lands, scatter, send_sems, recv_sems, masks=ALL_PEERS):
    x, y, c = lax.axis_index("x"), lax.axis_index("y"), lax.axis_index("c")
    me = 4 * x + 2 * y + c
    out = []
    for k in range(len(srcs)):
        out.append([])
        for m in (masks[k] if isinstance(masks[0], tuple) else masks):
            px, py, pc = x ^ (m >> 2), y ^ ((m >> 1) & 1), c ^ (m & 1)
            src = srcs[k].at[4 * px + 2 * py + pc] if scatter[k] else srcs[k]
            out[k].append(pltpu.make_async_remote_copy(
                src_ref=src, dst_ref=lands[k].at[me], send_sem=send_sems.at[k * (NDEV - 1) + m - 1],
                recv_sem=recv_sems.at[k * (NDEV - 1) + m - 1],
                device_id=(px, py, pc), device_id_type=MESH_ID))
    return out


def _exchange_start(arrs, scatter, name, masks=ALL_PEERS):
    n = len(arrs)
    me = 4 * lax.axis_index("x") + 2 * lax.axis_index("y") + lax.axis_index("c")
    lands = []
    for a, sc in zip(arrs, scatter):
        own = lax.dynamic_index_in_dim(a, me, 0, keepdims=True) if sc else a[None]
        shape = a.shape if sc else (NDEV,) + a.shape
        lands.append(lax.dynamic_update_index_in_dim(lax.empty(shape, a.dtype), own, me, 0))

    def body(*refs):
        srcs, lnds = refs[:n], refs[n:2 * n]
        send_sems, recv_sems = refs[2 * n], refs[2 * n + 1]
        token = refs[-1]
        for cp in sum(_peer_copies(srcs, lnds, scatter, send_sems, recv_sems, masks), []):
            cp.start()
        token[...] = jnp.zeros_like(token)

    ops = [pltpu.with_memory_space_constraint(a, pltpu.HBM) for a in list(arrs) + lands]
    res = pl.pallas_call(
        body, name=name,
        out_shape=(pltpu.SemaphoreType.DMA((n * (NDEV - 1),)), pltpu.SemaphoreType.DMA((n * (NDEV - 1),)))
        + tuple(pltpu.HBM(o.shape, o.dtype) for o in ops) + (_sds((8, 128), F32),),
        in_specs=[HBM] * (2 * n), out_specs=(SEM, SEM) + (HBM,) * (2 * n) + (pl.BlockSpec(memory_space=pltpu.VMEM),),
        input_output_aliases={k: 2 + k for k in range(2 * n)},
        compiler_params=pltpu.CompilerParams(has_side_effects=EFFECT),
    )(*ops)
    return res[:-1], res[-1]


def _exchange_wait(handle, scatter, after, name, masks=ALL_PEERS, which=None):
    send_sems, recv_sems = handle[0], handle[1]
    bufs = handle[2:]
    n = len(bufs) // 2
    after = list(after)

    def body(*refs):
        srcs, lnds = refs[:n], refs[n:2 * n]
        copies = _peer_copies(srcs, lnds, scatter, refs[2 * n], refs[2 * n + 1], masks)
        for k in (range(n) if which is None else which):
            for cp in copies[k]:
                cp.wait_send()
                cp.wait_recv()

    res = pl.pallas_call(
        body, name=name, out_shape=tuple(pltpu.HBM(b.shape, b.dtype) for b in bufs),
        in_specs=[HBM] * (2 * n) + [SEM, SEM] + [ANY] * len(after), out_specs=(HBM,) * (2 * n),
        input_output_aliases={k: k for k in range(2 * n)},
        compiler_params=pltpu.CompilerParams(has_side_effects=EFFECT),
    )(*bufs, send_sems, recv_sems, *after)
    return res[n:] if which is None else (res[n:], (send_sems, recv_sems) + tuple(res))


def _tie(x, token, name):
    def body(x_ref, t_ref, o_ref):
        del x_ref, t_ref, o_ref

    return pl.pallas_call(body, name=name, out_shape=_sds(x.shape, x.dtype), in_specs=[ANY, ANY], out_specs=ANY,
                          input_output_aliases={0: 0})(x, token)


TM_E = 272


def _norm_fwd_t(h, g, name):
    def body(h_ref, g_ref, o_ref, ot_ref):
        xv = h_ref[...]
        r = lax.rsqrt(jnp.mean(xv * xv, axis=-1, keepdims=True) + EPS)
        y = xv * r * g_ref[...]
        o_ref[...] = y.astype(BF16)
        ot_ref[...] = y.T.astype(BF16)

    return pl.pallas_call(
        body, name=name, grid=(T // 128,),
        in_specs=[pl.BlockSpec((128, D), lambda i: (i, 0)), pl.BlockSpec((1, D), lambda i: (0, 0))],
        out_specs=(pl.BlockSpec((128, D), lambda i: (i, 0)), pl.BlockSpec((D, 128), lambda i: (0, i))),
        out_shape=(_sds((T, D), BF16), _sds((D, T), BF16)), compiler_params=_cp(("parallel",)))(h, g)


def _norm_bwd_rows(xv, gv, dnv, dres):
    r = lax.rsqrt(jnp.mean(xv * xv, axis=-1, keepdims=True) + EPS)
    xh = xv * r
    dxh = dnv * gv
    dx = dres + r * (dxh - xh * jnp.mean(dxh * xh, axis=-1, keepdims=True))
    return dx, jnp.sum(dnv * xh, axis=0, keepdims=True)


TM_MM = 1088


def _inproj_fwd(a, w_g):
    nb = w_g.shape[2]

    def body(a_ref, w_ref, o_ref):
        o_ref[...] = _dot(a_ref[...], w_ref[0])

    return pl.pallas_call(
        body, name="inproj_fwd", grid=(T // TM_MM, NDEV),
        in_specs=[pl.BlockSpec((TM_MM, D), lambda i, j: (i, 0)), pl.BlockSpec((1, D, nb), lambda i, j: (j, 0, 0))],
        out_specs=pl.BlockSpec((TM_MM, nb), lambda i, j: (i, j)), out_shape=_sds((T, NDEV * nb), F32),
        compiler_params=_cp(("parallel", "parallel")))(a, w_g)


TM_B = 544


W_IN_B = IN_COLS // NDEV


NA_BLKS = 1536 // W_IN_B


def _dp_specs(rows, row_index):
    return [pl.BlockSpec((rows, W_IN_B), lambda *g: (row_index(*g), jnp.minimum(g[-1], NA_BLKS - 1))),
            pl.BlockSpec((rows, W_IN_B), lambda *g: (row_index(*g), jnp.maximum(g[-1] - NA_BLKS, 0)))]


def _inproj_bwd_dw(a_t, dp_na, dp_rest):
    def body(at_ref, na_ref, rest_ref, dw_ref):
        j = pl.program_id(0)

        @pl.when(j < NA_BLKS)
        def _():
            dw_ref[0] = _dot(at_ref[...], na_ref[...]).astype(BF16)

        @pl.when(j >= NA_BLKS)
        def _():
            dw_ref[0] = _dot(at_ref[...], rest_ref[...]).astype(BF16)

    return pl.pallas_call(
        body, name="inproj_bwd_dw", grid=(NDEV,),
        in_specs=[pl.BlockSpec((D, T), lambda j: (0, 0))] + _dp_specs(T, lambda j: 0),
        out_specs=pl.BlockSpec((1, D, W_IN_B), lambda j: (j, 0, 0)), out_shape=_sds((NDEV, D, W_IN_B), BF16),
        compiler_params=_cp(("parallel",)))(a_t, dp_na, dp_rest)


def _inproj_bwd_da(dp_na, dp_rest, w_g, h0, g_mix, dh1):
    nsub = TM_MM // TM_E

    def body(na_ref, rest_ref, w_ref, h0_ref, g_ref, dres_ref, dh0_ref, dg_ref, da):
        i, j = pl.program_id(0), pl.program_id(1)
        dpv = jnp.where(j < NA_BLKS, na_ref[...], rest_ref[...])
        dav = _dot(dpv, w_ref[0], NT)

        @pl.when(j == 0)
        def _():
            da[...] = dav

        @pl.when(j > 0)
        def _():
            da[...] += dav

        @pl.when(j == NDEV - 1)
        def _():
            gsum = jnp.zeros((1, D), F32)
            for s in range(nsub):
                sub = slice(s * TM_E, (s + 1) * TM_E)
                dx, gpart = _norm_bwd_rows(h0_ref[sub, :], g_ref[...], da[sub, :], dres_ref[sub, :])
                dh0_ref[sub, :] = dx
                gsum = gsum + gpart

            @pl.when(i == 0)
            def _():
                dg_ref[...] = gsum

            @pl.when(i > 0)
            def _():
                dg_ref[...] += gsum

    rblk = pl.BlockSpec((TM_MM, D), lambda i, j: (i, 0))
    vec = pl.BlockSpec((1, D), lambda i, j: (0, 0))
    return pl.pallas_call(
        body, name="inproj_bwd_da", grid=(T // TM_MM, NDEV),
        in_specs=_dp_specs(TM_MM, lambda i, j: i) + [pl.BlockSpec((1, D, W_IN_B), lambda i, j: (j, 0, 0)), rblk, vec, rblk],
        out_specs=(rblk, vec), out_shape=(_sds((T, D), F32), _sds((1, D), F32)),
        scratch_shapes=[pltpu.VMEM((TM_MM, D), F32)],
        compiler_params=_cp(("arbitrary", "arbitrary"), 56))(dp_na, dp_rest, w_g, h0, g_mix, dh1)


NA_QB = 256
NA_GROUPS = ROWS // 4
NA_UROWS = 11
NA_KW = NA_UROWS * GRID_W
NA_KU = 768


def _na_row_offset(var, i, j):
    valid = (j < 8, i <= j < i + 8, 3 <= j < NA_UROWS)[var]
    return (j - i + (7, 3, 0)[var]) if valid else None


def _na_bias_table(rp):
    def body(r_ref, o_ref):
        row3 = lax.broadcasted_iota(jnp.int32, (15, GRID_W, 128), 1)
        lane3 = lax.broadcasted_iota(jnp.int32, (15, GRID_W, 128), 2)
        w3 = lane3 & (GRID_W - 1)
        cs3 = jnp.clip(row3 - 8, 0, GRID_W - 16)
        lane = lax.broadcasted_iota(jnp.int32, (GRID_W, 128), 1)
        neg = jnp.full((GRID_W, 128), NEG, F32)
        z = jnp.stack([jnp.broadcast_to(r_ref[0, a:a + 1, :], (GRID_W, 128)) for a in range(15)])
        for bit in range(6):
            sh = 1 << bit
            z = jnp.where((row3 & sh) != 0, jnp.roll(z, sh, axis=2), z)
        z = jnp.roll(z, 128 - 15, axis=2)
        z = jnp.where(lane3 < GRID_W, z, 0.0)
        z = z + jnp.roll(z, GRID_W, axis=2)
        tabs = jnp.where((w3 >= cs3) & (w3 < cs3 + 16), z, NEG)
        tail = jnp.where(lane < GRID_W + NM, 0.0, NEG)
        for var in range(3):
            for i in range(4):
                for jp in range(NA_KU // 128):
                    halves = []
                    for j in (2 * jp, 2 * jp + 1):
                        a = _na_row_offset(var, i, j) if j < NA_UROWS else None
                        halves.append(tail if j >= NA_UROWS else (neg if a is None else tabs[a]))
                    o_ref[var, 0, i * 64:(i + 1) * 64, jp * 128:(jp + 1) * 128] = jnp.where(lane < GRID_W, halves[0], halves[1])

    return pl.pallas_call(
        body, name="na_bias_table", grid=(NA_HEADS,),
        in_specs=[pl.BlockSpec((1, 15, 128), lambda h: (h, 0, 0))],
        out_specs=pl.BlockSpec((3, 1, NA_QB, NA_KU), lambda h: (0, h, 0, 0)),
        out_shape=_sds((3, NA_HEADS, NA_QB, NA_KU), F32), compiler_params=_cp(("parallel",)))(rp)


def _na_var(g):
    return jnp.where(g == 0, 0, jnp.where(g == NA_GROUPS - 1, 2, 1))


def _na_load_window(src_ref, dst, g):
    us = jnp.clip(4 * g - 4, 0, ROWS - NA_UROWS)
    kstart = pl.multiple_of(NM + GRID_W * us, 16)
    dst[0:NA_KW, :] = src_ref[pl.ds(kstart, NA_KW), :].astype(BF16)
    dst[NA_KW:NA_KW + NM, :] = src_ref[0:NM, :].astype(BF16)
    dst[NA_KW + NM:, :] = jnp.zeros((NA_KU - NA_KW - NM, 128), BF16)
    return kstart


def _na_fwd(p_act, bias_tab):
    def body(q_ref, k_ref, v_ref, b_ref, o_ref, lse_ref, ku, vu):
        g = pl.program_id(1)
        _na_load_window(k_ref, ku, g)
        _na_load_window(v_ref, vu, g)
        qstart = pl.multiple_of(NM + NA_QB * g, 16)
        q = q_ref[pl.ds(qstart, NA_QB), :]
        lane = lax.broadcasted_iota(jnp.int32, (NA_QB, 128), 1)
        o_h, lse_h = [], []
        for h in range(2):
            hm = (lane < 64) if h == 0 else (lane >= 64)
            qm = (jnp.where(hm, q, 0.0) * NA_SCALE).astype(BF16)
            s = _dot(qm, ku[...], NT) + b_ref[0, h]
            m = jnp.max(s, axis=-1, keepdims=True)
            p = jnp.exp(s - m)
            l = jnp.sum(p, axis=-1, keepdims=True)
            o_h.append(_dot(p.astype(BF16), vu[...]) / l)
            lse_h.append(jnp.broadcast_to(m + jnp.log(l), (NA_QB, 128)))
        o_ref[pl.ds(qstart, NA_QB), :] = jnp.where(lane < 64, o_h[0], o_h[1]).astype(BF16)
        lse_ref[0, pl.ds(qstart, NA_QB), :] = jnp.where(lane < 64, lse_h[0], lse_h[1])

        @pl.when(g == 0)
        def _():
            qm_ = q_ref[0:NM, :]
            lane_m = lax.broadcasted_iota(jnp.int32, (NM, 128), 1)
            km, vm = ku[NA_KW:NA_KW + NM, :], vu[NA_KW:NA_KW + NM, :]
            om = []
            for h in range(2):
                hm = (lane_m < 64) if h == 0 else (lane_m >= 64)
                s = _dot(jnp.where(hm, qm_, 0.0).astype(BF16), km, NT) * NA_SCALE
                p = jnp.exp(s - jnp.max(s, axis=-1, keepdims=True))
                l = jnp.sum(p, axis=-1, keepdims=True)
                om.append(_dot(p.astype(BF16), vm) / l)
            o_ref[0:NM, :] = jnp.where(lane_m < 64, om[0], om[1]).astype(BF16)
            o_ref[L:T, :] = jnp.zeros((T - L, 128), BF16)
            lse_ref[0, 0:NM, :] = jnp.zeros((NM, 128), F32)
            lse_ref[0, L:T, :] = jnp.zeros((T - L, 128), F32)

    col = lambda off: pl.BlockSpec((T, 128), lambda hp, g: (0, off + hp))
    return pl.pallas_call(
        body, name="na_fwd", grid=(4, NA_GROUPS),
        in_specs=[col(0), col(4), col(8),
                  pl.BlockSpec((1, 2, NA_QB, NA_KU), lambda hp, g: (_na_var(g), hp, 0, 0))],
        out_specs=(pl.BlockSpec((T, 128), lambda hp, g: (0, hp)), pl.BlockSpec((1, T, 128), lambda hp, g: (hp, 0, 0))),
        out_shape=(_sds((T, 512), BF16), _sds((4, T, 128), F32)),
        scratch_shapes=[pltpu.VMEM((NA_KU, 128), BF16), pltpu.VMEM((NA_KU, 128), BF16)],
        compiler_params=_cp(("parallel", "arbitrary")))(p_act, p_act, p_act, bias_tab)


def _na_bwd(p_act, do, lse, bias_tab):
    def body(q_ref, k_ref, v_ref, do_ref, lse_ref, b_ref, dq_ref, dk_ref, dv_ref, db_ref, ku, vu):
        g = pl.program_id(1)

        @pl.when(g == 0)
        def _():
            dq_ref[...] = jnp.zeros((T, 128), F32)
            dk_ref[...] = jnp.zeros((T, 128), F32)
            dv_ref[...] = jnp.zeros((T, 128), F32)

        kstart = _na_load_window(k_ref, ku, g)
        _na_load_window(v_ref, vu, g)
        qstart = pl.multiple_of(NM + NA_QB * g, 16)
        q = q_ref[pl.ds(qstart, NA_QB), :]
        dov = do_ref[pl.ds(qstart, NA_QB), :]
        lsev = lse_ref[0, pl.ds(qstart, NA_QB), :]
        lane = lax.broadcasted_iota(jnp.int32, (NA_QB, 128), 1)
        first = (g == 0) | (g == 1) | (g == NA_GROUPS - 1)
        dq_h = []
        dku = jnp.zeros((NA_KU, 128), F32)
        dvu = jnp.zeros((NA_KU, 128), F32)
        for h in range(2):
            hm = (lane < 64) if h == 0 else (lane >= 64)
            qm = (jnp.where(hm, q, 0.0) * NA_SCALE).astype(BF16)
            dom = jnp.where(hm, dov, 0.0).astype(BF16)
            s = _dot(qm, ku[...], NT) + b_ref[0, h]
            p = jnp.exp(s - lsev[:, 64 * h:64 * h + 1])
            dp = _dot(dom, vu[...], NT)
            delta = jnp.sum(p * dp, axis=-1, keepdims=True)
            ds = p * (dp - delta)

            @pl.when(first)
            def _():
                db_ref[0, h] = ds

            @pl.when(jnp.logical_not(first))
            def _():
                db_ref[0, h] += ds

            dsb = ds.astype(BF16)
            dq_h.append(_dot(dsb, ku[...]) * NA_SCALE)
            dku = dku + _dot(dsb, qm, TN)
            dvu = dvu + _dot(p.astype(BF16), dom, TN)
        dq_ref[pl.ds(qstart, NA_QB), :] = jnp.where(lane < 64, dq_h[0], dq_h[1])
        dk_ref[pl.ds(kstart, NA_KW), :] += dku[0:NA_KW]
        dv_ref[pl.ds(kstart, NA_KW), :] += dvu[0:NA_KW]
        dk_ref[0:NM, :] += dku[NA_KW:NA_KW + NM]
        dv_ref[0:NM, :] += dvu[NA_KW:NA_KW + NM]

        @pl.when(g == 0)
        def _():
            qm_ = q_ref[0:NM, :]
            dom_ = do_ref[0:NM, :]
            lane_m = lax.broadcasted_iota(jnp.int32, (NM, 128), 1)
            km, vm = ku[NA_KW:NA_KW + NM, :], vu[NA_KW:NA_KW + NM, :]
            dqs = []
            dkm = jnp.zeros((NM, 128), F32)
            dvm = jnp.zeros((NM, 128), F32)
            for h in range(2):
                hm = (lane_m < 64) if h == 0 else (lane_m >= 64)
                qh = jnp.where(hm, qm_, 0.0).astype(BF16)
                doh = jnp.where(hm, dom_, 0.0).astype(BF16)
                s = _dot(qh, km, NT) * NA_SCALE
                e = jnp.exp(s - jnp.max(s, axis=-1, keepdims=True))
                p = e / jnp.sum(e, axis=-1, keepdims=True)
                dp = _dot(doh, vm, NT)
                ds = p * (dp - jnp.sum(p * dp, axis=-1, keepdims=True))
                dsb = (ds * NA_SCALE).astype(BF16)
                dqs.append(_dot(dsb, km))
                dkm = dkm + _dot(dsb, qh, TN)
                dvm = dvm + _dot(p.astype(BF16), doh, TN)
            dq_ref[0:NM, :] = jnp.where(lane_m < 64, dqs[0], dqs[1])
            dk_ref[0:NM, :] += dkm
            dv_ref[0:NM, :] += dvm

    col = lambda off: pl.BlockSpec((T, 128), lambda hp, g: (0, off + hp))
    ocol = pl.BlockSpec((T, 128), lambda hp, g: (0, hp))
    bspec = pl.BlockSpec((1, 2, NA_QB, NA_KU), lambda hp, g: (_na_var(g), hp, 0, 0))
    return pl.pallas_call(
        body, name="na_bwd", grid=(4, NA_GROUPS),
        in_specs=[col(0), col(4), col(8), ocol, pl.BlockSpec((1, T, 128), lambda hp, g: (hp, 0, 0)), bspec],
        out_specs=(ocol, ocol, ocol, bspec),
        out_shape=(_sds((T, 512), F32), _sds((T, 512), F32), _sds((T, 512), F32), _sds((3, NA_HEADS, NA_QB, NA_KU), F32)),
        scratch_shapes=[pltpu.VMEM((NA_KU, 128), BF16), pltpu.VMEM((NA_KU, 128), BF16)],
        compiler_params=_cp(("parallel", "arbitrary")))(p_act, p_act, p_act, do, lse, bias_tab)


def _na_rpb_reduce(dbias):
    def body(db_ref, o_ref):
        lane = lax.broadcasted_iota(jnp.int32, (GRID_W, 128), 1)
        row3 = lax.broadcasted_iota(jnp.int32, (15, GRID_W, 128), 1)
        lane3 = lax.broadcasted_iota(jnp.int32, (15, GRID_W, 128), 2)
        accs = []
        for a in range(15):
            acc = jnp.zeros((GRID_W, 128), F32)
            for var in range(3):
                for i in range(4):
                    for j in range(NA_UROWS):
                        if _na_row_offset(var, i, j) == a:
                            pair = db_ref[var, 0, i * 64:(i + 1) * 64, (j // 2) * 128:(j // 2 + 1) * 128]
                            acc = acc + jnp.where((lane < GRID_W) if j % 2 == 0 else (lane >= GRID_W), pair, 0.0)
            accs.append(acc)
        z = jnp.stack(accs)
        z = jnp.where(lane3 < GRID_W, z + jnp.roll(z, GRID_W, axis=2), 0.0)
        for bit in range(6):
            sh = 1 << bit
            z = jnp.where((row3 & sh) != 0, jnp.roll(z, 128 - sh, axis=2), z)
        z = jnp.roll(z, 15, axis=2)
        o_ref[0] = jnp.sum(z, axis=1)

    return pl.pallas_call(
        body, name="na_rpb_reduce", grid=(NA_HEADS,),
        in_specs=[pl.BlockSpec((3, 1, NA_QB, NA_KU), lambda h: (0, h, 0, 0))],
        out_specs=pl.BlockSpec((1, 15, 128), lambda h: (h, 0, 0)), out_shape=_sds((NA_HEADS, 15, 128), F32),
        compiler_params=_cp(("parallel",)))(dbias)


HG_RB = 128
HG_NB = T // HG_RB
HG_SLOTS = HG_NB * 8
HI = lax.Precision.HIGHEST
HG_UNROLL = 4
HG_UNROLL_WIDE = 8


def _chunk_tri(lower):
    r = lax.broadcasted_iota(jnp.int32, (HG_RB, HG_RB), 0)
    c = lax.broadcasted_iota(jnp.int32, (HG_RB, HG_RB), 1)
    same = (r // HG_C) == (c // HG_C)
    keep = (c <= r) if lower else (c >= r)
    return jnp.where(same & keep, 1.0, 0.0).astype(F32)


def _hg_gate_terms(z, lg):
    dl = lg[0:1, :] - lg[1:2, :]
    log_lb = jax.nn.log_sigmoid(dl)
    log_1mlb = jax.nn.log_sigmoid(-dl)
    yz = log_1mlb + jax.nn.log_sigmoid(z)
    log_f = jnp.logaddexp(log_lb, yz)
    snz = jax.nn.sigmoid(-z)
    k = jnp.exp(log_1mlb) * snz
    w2 = jnp.exp(yz - log_f)
    return log_f, k, snz, w2


def _hg_pre(p_act, logits):
    def body(q_ref, zf_ref, zb_ref, lg_ref, qh_ref, kf_ref, bf_ref, kb_ref, bb_ref):
        qh_ref[...] = jax.nn.silu(q_ref[...])
        lf, kf, _, _ = _hg_gate_terms(zf_ref[...], lg_ref[0])
        kf_ref[...] = kf
        bf_ref[...] = jnp.dot(_chunk_tri(True), lf, precision=HI, preferred_element_type=F32)
        lb_, kb, _, _ = _hg_gate_terms(zb_ref[...], lg_ref[1])
        kb_ref[...] = kb
        bb_ref[...] = jnp.dot(_chunk_tri(False), lb_, precision=HI, preferred_element_type=F32)

    blk = lambda c: pl.BlockSpec((HG_RB, 512), lambda i: (i, c))
    ob = pl.BlockSpec((HG_RB, 512), lambda i: (i, 0))
    return pl.pallas_call(
        body, name="hg_pre", grid=(HG_NB,),
        in_specs=[blk(3), blk(4), blk(5), pl.BlockSpec((2, 2, 512), lambda i: (0, 0, 0))],
        out_specs=(ob,) * 5, out_shape=(_sds((T, 512), F32),) * 5,
        compiler_params=_cp(("parallel",)))(p_act, p_act, p_act, logits)


def _bdot(a, b, ca, cb):
    return lax.dot_general(a.astype(BF16), b.astype(BF16), (((ca,), (cb,)), ((0,), (0,))), preferred_element_type=F32)


HG_S = 8
HG_NS = HG_RB // HG_S


def _lane_sums(xs):
    l_io = lax.broadcasted_iota(jnp.int32, (HG_NS, HG_S, HG_S), 2)
    a = jnp.zeros((HG_NS, HG_S, HG_S), F32)
    for j, x in enumerate(xs):
        a = a + jnp.where(l_io == j, jnp.sum(x, axis=-1, keepdims=True), 0.0)
    return a


def _halves(x):
    y = x.reshape(8, 2, HG_S, x.shape[-1])
    return y[:, 0], y[:, 1]


def _join(first, second):
    return jnp.stack([first, second], axis=1).reshape(HG_RB, first.shape[-1])


def _cross_split(rev, b4):
    b_1, b_2 = _halves(b4)
    if rev:
        r = b_2[:, 0:1, :]
        return jnp.exp(b_1 - r), jnp.exp(r - b_2)
    r = b_1[:, HG_S - 1:HG_S, :]
    return jnp.exp(b_2 - r), jnp.exp(r - b_1)


def _hg_scan_fwd(qh, k, b, p_act, rev):
    anchor = 0 if rev else HG_C - 1

    def body(q_ref, k_ref, b_ref, v_ref, o_ref, st_ref, dsc):
        def phase_a(blk, _):
            rows = pl.ds(pl.multiple_of(blk * HG_RB, HG_RB), HG_RB)
            b3 = b_ref[rows, :].reshape(8, HG_C, 128)
            k3 = k_ref[rows, :].reshape(8, HG_C, 128)
            v3 = v_ref[rows, :].reshape(8, HG_C, 128)
            bl = b3[:, anchor:anchor + 1, :]
            kt = k3 * jnp.exp(bl - b3)
            st_ref[0, pl.ds(pl.multiple_of(blk * 8, 8), 8)] = _bdot(v3, kt, 1, 1)
            dsc[pl.ds(pl.multiple_of(blk * 8, 8), 8), :] = jnp.exp(bl[:, 0, :])
            return 0

        lax.fori_loop(0, HG_NB, phase_a, 0, unroll=HG_UNROLL_WIDE)

        def phase_b(n, carry):
            c = (NCHUNK - 1 - n) if rev else n
            u = st_ref[0, c]
            st_ref[0, c] = carry
            return carry * dsc[pl.ds(c, 1), :] + u

        lax.fori_loop(0, NCHUNK // 3, lambda n3, s: phase_b(3 * n3 + 2, phase_b(3 * n3 + 1, phase_b(3 * n3, s))),
                      jnp.zeros((128, 128), F32))
        for c in range(NCHUNK, HG_SLOTS):
            st_ref[0, c] = jnp.zeros((128, 128), F32)

        t_io = lax.broadcasted_iota(jnp.int32, (HG_NS, HG_S, 128), 1)

        def phase_c(blk, _):
            rows = pl.ds(pl.multiple_of(blk * HG_RB, HG_RB), HG_RB)
            b4 = b_ref[rows, :].reshape(HG_NS, HG_S, 128)
            k4 = k_ref[rows, :].reshape(HG_NS, HG_S, 128)
            q4 = q_ref[rows, :].reshape(HG_NS, HG_S, 128)
            v4 = v_ref[rows, :].reshape(HG_NS, HG_S, 128)
            st = st_ref[0, pl.ds(pl.multiple_of(blk * 8, 8), 8)]
            o = _bdot((q4 * jnp.exp(b4)).reshape(8, HG_C, 128), st, 2, 2).reshape(HG_RB, 128)
            terms = []
            for s in range(HG_S):
                ok = (t_io <= s) if rev else (t_io >= s)
                f = jnp.exp(jnp.where(ok, b4 - b4[:, s:s + 1, :], NEG))
                terms.append(q4 * f * k4[:, s:s + 1, :])
            o_in = _bdot(_lane_sums(terms), v4, 2, 1)
            wq, wk = _cross_split(rev, b4)
            q_1, q_2 = _halves(q4)
            k_1, k_2 = _halves(k4)
            v_1, v_2 = _halves(v4)
            o_1, o_2 = _halves(o_in)
            if rev:
                o_1 = o_1 + _bdot(_bdot(q_1 * wq, k_2 * wk, 2, 2), v_2, 2, 1)
            else:
                o_2 = o_2 + _bdot(_bdot(q_2 * wq, k_1 * wk, 2, 2), v_1, 2, 1)
            o_ref[rows, :] = o + _join(o_1, o_2)
            return 0

        lax.fori_loop(0, HG_NB, phase_c, 0, unroll=HG_UNROLL_WIDE)

    col = pl.BlockSpec((T, 128), lambda h: (0, h))
    return pl.pallas_call(
        body, name="hg_scan_bwd_dir" if rev else "hg_scan_fwd_dir", grid=(HG_HEADS,),
        in_specs=[col, col, col, pl.BlockSpec((T, 128), lambda h: (0, 24 + h))],
        out_specs=(col, pl.BlockSpec((1, HG_SLOTS, 128, 128), lambda h: (h, 0, 0, 0))),
        out_shape=(_sds((T, 512), F32), _sds((HG_HEADS, HG_SLOTS, 128, 128), F32)),
        scratch_shapes=[pltpu.VMEM((HG_SLOTS, 128), F32)],
        compiler_params=_cp(("parallel",), 56))(qh, k, b, p_act)


def _hg_scan_bwd(qh, k, b, p_act, st, do, rev):
    anchor = 0 if rev else HG_C - 1

    def body(q_ref, k_ref, b_ref, v_ref, st_ref, do_ref, dq_ref, dk_ref, db_ref, dv_ref, gst, dsc, dbl):
        def phase_a(blk, _):
            rows = pl.ds(pl.multiple_of(blk * HG_RB, HG_RB), HG_RB)
            b3 = b_ref[rows, :].reshape(8, HG_C, 128)
            q3 = q_ref[rows, :].reshape(8, HG_C, 128)
            do3 = do_ref[rows, :].reshape(8, HG_C, 128)
            gst[pl.ds(pl.multiple_of(blk * 8, 8), 8)] = _bdot(do3, q3 * jnp.exp(b3), 1, 1)
            dsc[pl.ds(pl.multiple_of(blk * 8, 8), 8), :] = jnp.exp(b3[:, anchor, :])
            return 0

        lax.fori_loop(0, HG_NB, phase_a, 0, unroll=HG_UNROLL_WIDE)

        def phase_b(n, carry):
            c = n if rev else (NCHUNK - 1 - n)
            w = gst[c]
            gst[c] = carry
            dcv = dsc[pl.ds(c, 1), :]
            dbl[pl.ds(c, 1), :] = dcv * jnp.sum(st_ref[0, c] * carry, axis=0, keepdims=True)
            return carry * dcv + w

        lax.fori_loop(0, NCHUNK // 3, lambda n3, s: phase_b(3 * n3 + 2, phase_b(3 * n3 + 1, phase_b(3 * n3, s))),
                      jnp.zeros((128, 128), F32))
        for c in range(NCHUNK, HG_SLOTS):
            gst[c] = jnp.zeros((128, 128), F32)
            dbl[c:c + 1, :] = jnp.zeros((1, 128), F32)

        t_io = lax.broadcasted_iota(jnp.int32, (HG_NS, HG_S, 128), 1)
        t16 = lax.broadcasted_iota(jnp.int32, (8, HG_C, 128), 1)
        r_io = lax.broadcasted_iota(jnp.int32, (HG_NS, HG_S, HG_S), 1)
        l_io = lax.broadcasted_iota(jnp.int32, (HG_NS, HG_S, HG_S), 2)

        def phase_c(blk, _):
            rows = pl.ds(pl.multiple_of(blk * HG_RB, HG_RB), HG_RB)
            cs = pl.ds(pl.multiple_of(blk * 8, 8), 8)
            b4 = b_ref[rows, :].reshape(HG_NS, HG_S, 128)
            k4 = k_ref[rows, :].reshape(HG_NS, HG_S, 128)
            q4 = q_ref[rows, :].reshape(HG_NS, HG_S, 128)
            v4 = v_ref[rows, :].reshape(HG_NS, HG_S, 128)
            do4 = do_ref[rows, :].reshape(HG_NS, HG_S, 128)
            b3, k3, q3 = (z.reshape(8, HG_C, 128) for z in (b4, k4, q4))
            v3, do3 = v4.reshape(8, HG_C, 128), do4.reshape(8, HG_C, 128)
            s_t = st_ref[0, cs]
            g_t = gst[cs]
            bl = b3[:, anchor:anchor + 1, :]
            ekl = jnp.exp(bl - b3)
            kt = k3 * ekl
            dkt = _bdot(v3, g_t, 2, 1)
            dq = (_bdot(do3, s_t, 2, 1) * jnp.exp(b3)).reshape(HG_NS, HG_S, 128)
            dk = (dkt * ekl).reshape(HG_NS, HG_S, 128)
            dv = _bdot(kt, g_t, 2, 2).reshape(HG_NS, HG_S, 128)
            dbl3 = dbl[cs, :].reshape(8, 1, 128) + jnp.sum(dkt * kt, axis=1, keepdims=True)
            causal = (l_io >= r_io) if rev else (l_io <= r_io)
            da = jnp.where(causal, _bdot(do4, v4, 2, 2), 0.0)
            causal_t = (l_io <= r_io) if rev else (l_io >= r_io)
            dat = jnp.where(causal_t, _bdot(v4, do4, 2, 2), 0.0)
            for s in range(HG_S):
                ok = (t_io <= s) if rev else (t_io >= s)
                f = jnp.exp(jnp.where(ok, b4 - b4[:, s:s + 1, :], NEG))
                dq = dq + da[:, :, s:s + 1] * (f * k4[:, s:s + 1, :])
            terms = []
            for t in range(HG_S):
                ok = (t_io >= t) if rev else (t_io <= t)
                e = jnp.exp(jnp.where(ok, b4[:, t:t + 1, :] - b4, NEG))
                eq = e * q4[:, t:t + 1, :]
                dk = dk + dat[:, :, t:t + 1] * eq
                terms.append(eq * k4)
            dv = dv + _bdot(_lane_sums(terms), do4, 2, 1)
            wq, wk = _cross_split(rev, b4)
            pick = (lambda z: _halves(z)) if rev else (lambda z: _halves(z)[::-1])
            (q_q, _), (_, k_k), (_, v_k), (do_q, _) = pick(q4), pick(k4), pick(v4), pick(do4)
            qx, kx = q_q * wq, k_k * wk
            dq_q = _bdot(_bdot(do_q, v_k, 2, 2), kx, 2, 1) * wq
            dk_k = _bdot(_bdot(v_k, do_q, 2, 2), qx, 2, 1) * wk
            dv_k = _bdot(_bdot(kx, qx, 2, 2), do_q, 2, 1)
            zero = jnp.zeros((8, HG_S, 128), F32)
            place_q = (lambda z: _join(z, zero)) if rev else (lambda z: _join(zero, z))
            place_k = (lambda z: _join(zero, z)) if rev else (lambda z: _join(z, zero))
            dq2 = dq.reshape(HG_RB, 128) + place_q(dq_q)
            dk2 = dk.reshape(HG_RB, 128) + place_k(dk_k)
            dv2 = dv.reshape(HG_RB, 128) + place_k(dv_k)
            dq3, dk3 = dq2.reshape(8, HG_C, 128), dk2.reshape(8, HG_C, 128)
            db = q3 * dq3 - k3 * dk3 + jnp.where(t16 == anchor, dbl3, 0.0)
            dq_ref[rows, :] = dq2
            dk_ref[rows, :] = dk2
            db_ref[rows, :] = db.reshape(HG_RB, 128)
            dv_ref[rows, :] = dv2
            return 0

        lax.fori_loop(0, HG_NB, phase_c, 0, unroll=HG_UNROLL)

    col = pl.BlockSpec((T, 128), lambda h: (0, h))
    return pl.pallas_call(
        body, name="hg_scan_bwd_dir_bwd" if rev else "hg_scan_fwd_dir_bwd", grid=(HG_HEADS,),
        in_specs=[col, col, col, pl.BlockSpec((T, 128), lambda h: (0, 24 + h)),
                  pl.BlockSpec((1, HG_SLOTS, 128, 128), lambda h: (h, 0, 0, 0)), col],
        out_specs=(col,) * 4, out_shape=(_sds((T, 512), F32),) * 4,
        scratch_shapes=[pltpu.VMEM((HG_SLOTS, 128, 128), F32), pltpu.VMEM((HG_SLOTS, 128), F32),
                        pltpu.VMEM((HG_SLOTS, 128), F32)],
        compiler_params=_cp(("parallel",), 56))(qh, k, b, p_act, st, do)


def _row_valid(i, tm):
    r = lax.broadcasted_iota(jnp.int32, (tm, 1), 0) + i * tm
    return r < L


def _hg_post_rows(o, gv, gain_v, valid):
    parts = []
    for h in range(HG_HEADS):
        oh = o[:, 128 * h:128 * (h + 1)]
        parts.append(oh * lax.rsqrt(jnp.mean(oh * oh, axis=-1, keepdims=True) + EPS))
    return jnp.where(valid, jnp.concatenate(parts, axis=1) * gain_v * jax.nn.silu(gv), 0.0)


def _hg_post_bwd_rows(du, o, gv, gain_v, valid):
    duv = jnp.where(valid, du, 0.0)
    sig = jax.nn.sigmoid(gv)
    sg = gv * sig
    dn = duv * gain_v * sg
    do_parts, n_parts = [], []
    for h in range(HG_HEADS):
        sl = slice(128 * h, 128 * (h + 1))
        oh = o[:, sl]
        r = lax.rsqrt(jnp.mean(oh * oh, axis=-1, keepdims=True) + EPS)
        nh = oh * r
        dnh = dn[:, sl]
        do_parts.append(r * (dnh - nh * jnp.mean(dnh * nh, axis=-1, keepdims=True)))
        n_parts.append(nh)
    n = jnp.where(valid, jnp.concatenate(n_parts, axis=1), 0.0)
    do = jnp.where(valid, jnp.concatenate(do_parts, axis=1), 0.0)
    dg = duv * n * gain_v * (sig * (1.0 + gv * (1.0 - sig)))
    return do, dg, jnp.sum(duv * n * sg, axis=0, keepdims=True)


def _hg_pre_bwd(p_act, logits, dq_f, dq_b, dk_f, dk_b, db_f, db_b, dv_f, dv_b, dp_rest):
    def body(q_ref, zf_ref, zb_ref, lg_ref, dqf_ref, dqb_ref, dkf_ref, dkb_ref, dbf_ref, dbb_ref, dvf_ref, dvb_ref, _,
             dp_ref, dlg_ref):
        dq_ref, dzf_ref, dzb_ref, di_ref = (dp_ref.at[:, 512 * c:512 * (c + 1)] for c in range(4))
        i = pl.program_id(0)
        valid = _row_valid(i, HG_RB)
        qv = q_ref[...]
        sig = jax.nn.sigmoid(qv)
        dq_ref[...] = jnp.where(valid, (dqf_ref[...] + dqb_ref[...]) * (sig * (1.0 + qv * (1.0 - sig))), 0.0).astype(BF16)
        di_ref[...] = jnp.where(valid, dvf_ref[...] + dvb_ref[...], 0.0).astype(BF16)
        for d, (z_ref, dk_r, db_r, dz_ref) in enumerate(((zf_ref, dkf_ref, dbf_ref, dzf_ref), (zb_ref, dkb_ref, dbb_ref, dzb_ref))):
            lg = lg_ref[d]
            dl = lg[0:1, :] - lg[1:2, :]
            lb = jax.nn.sigmoid(dl)
            one_m_lb = jax.nn.sigmoid(-dl)
            log_f, _, snz, w2 = _hg_gate_terms(z_ref[...], lg)
            dbv = jnp.where(valid, db_r[...], 0.0)
            dkv = jnp.where(valid, dk_r[...], 0.0)
            dlf = jnp.dot(_chunk_tri(d == 1), dbv, precision=HI, preferred_element_type=F32)
            sz = 1.0 - snz
            dz_ref[...] = (dlf * w2 * snz - dkv * one_m_lb * sz * snz).astype(BF16)
            dlb = jnp.sum(dlf * snz * jnp.exp(-log_f) - dkv * snz, axis=0, keepdims=True)
            dl0 = dlb * lb * one_m_lb
            part = jnp.concatenate([dl0, -dl0], axis=0)

            @pl.when(i == 0)
            def _():
                dlg_ref[d] = part

            @pl.when(i > 0)
            def _():
                dlg_ref[d] += part

    blk = lambda c: pl.BlockSpec((HG_RB, 512), lambda i: (i, c))
    ob = pl.BlockSpec((HG_RB, 512), lambda i: (i, 0))
    lgs = pl.BlockSpec((2, 2, 512), lambda i: (0, 0, 0))
    return pl.pallas_call(
        body, name="hg_pre_bwd", grid=(HG_NB,),
        in_specs=[blk(3), blk(4), blk(5), lgs] + [ob] * 8 + [ANY],
        out_specs=(pl.BlockSpec((HG_RB, 2048), lambda i: (i, 0)), lgs),
        out_shape=(_sds(dp_rest.shape, BF16), _sds((2, 2, 512), F32)), input_output_aliases={12: 0},
        compiler_params=_cp(("arbitrary",)))(p_act, p_act, p_act, logits, dq_f, dq_b, dk_f, dk_b, db_f, db_b, dv_f, dv_b,
                                             dp_rest)


def _mix_fwd(o_na, o_f, o_b, gain, w_na, w_hg, p_act):
    def body(ona_ref, of_ref, ob_ref, g_ref, gain_ref, wna_ref, whg_ref, gna_ref, ghg_ref, o_ref, u_ref):
        u = _hg_post_rows(of_ref[...] + ob_ref[...], g_ref[...], gain_ref[...], _row_valid(pl.program_id(0), TM_B)).astype(BF16)
        u_ref[...] = u
        y_na = _dot(ona_ref[...], wna_ref[...])
        y_hg = _dot(u, whg_ref[...])
        o_ref[...] = (jax.nn.sigmoid(gna_ref[...]) * y_na + jax.nn.sigmoid(ghg_ref[...]) * y_hg).astype(BF16)

    act = pl.BlockSpec((TM_B, 512), lambda i: (i, 0))
    wsp = pl.BlockSpec((512, D), lambda i: (0, 0))
    return pl.pallas_call(
        body, name="mix_fwd", grid=(T // TM_B,),
        in_specs=[act, act, act, pl.BlockSpec((TM_B, 512), lambda i: (i, 7)), pl.BlockSpec((1, 512), lambda i: (0, 0)),
                  wsp, wsp, pl.BlockSpec((TM_B, D), lambda i: (i, 4)), pl.BlockSpec((TM_B, D), lambda i: (i, 5))],
        out_specs=(pl.BlockSpec((TM_B, D), lambda i: (i, 0)), act), out_shape=(_sds((T, D), BF16), _sds((T, 512), BF16)),
        compiler_params=_cp(("parallel",)))(o_na, o_f, o_b, p_act, gain, w_na, w_hg, p_act, p_act)


DP_REST = IN_COLS - 1536


def _mix_bwd(o_na, u_hg, o_f, o_b, gain, w_na, w_hg, p_act, dmix):
    ni = T // TM_B

    def body(ona_ref, uhg_ref, of_ref, ob_ref, g_ref, gain_ref, wna_ref, whg_ref, gna_ref, ghg_ref, dmix_ref,
             dp_ref, dwna_ref, dwhg_ref, dona_ref, do_ref, dgain_ref, acc_na, acc_hg):
        i = pl.program_id(0)
        dg_ref, dgna_ref, dghg_ref = dp_ref.at[:, 2048:2560], dp_ref.at[:, 2560:3584], dp_ref.at[:, 3584:4608]
        dm = dmix_ref[...].astype(F32)
        dxs = []
        for x_ref, w_ref, gt_ref, dgt_ref, dw_ref, acc in (
                (ona_ref, wna_ref, gna_ref, dgna_ref, dwna_ref, acc_na), (uhg_ref, whg_ref, ghg_ref, dghg_ref, dwhg_ref, acc_hg)):
            xv = x_ref[...]
            y = _dot(xv, w_ref[...])
            sg = jax.nn.sigmoid(gt_ref[...])
            dgt_ref[...] = (dm * y * sg * (1.0 - sg)).astype(BF16)
            dy = (dm * sg).astype(BF16)
            dxs.append(_dot(dy, w_ref[...], NT))
            part = _dot(xv, dy, TN)

            @pl.when(i == 0)
            def _():
                acc[...] = part

            @pl.when(i > 0)
            def _():
                acc[...] += part

            @pl.when(i == ni - 1)
            def _():
                dw_ref[...] = acc[...].astype(BF16)

        dona_ref[...] = dxs[0]
        do, dg, gpart = _hg_post_bwd_rows(dxs[1], of_ref[...] + ob_ref[...], g_ref[...], gain_ref[...], _row_valid(i, TM_B))
        do_ref[...] = do
        dg_ref[...] = dg.astype(BF16)

        @pl.when(i == 0)
        def _():
            dgain_ref[...] = gpart

        @pl.when(i > 0)
        def _():
            dgain_ref[...] += gpart

    act = pl.BlockSpec((TM_B, 512), lambda i: (i, 0))
    wsp = pl.BlockSpec((512, D), lambda i: (0, 0))
    rblk = pl.BlockSpec((TM_B, D), lambda i: (i, 0))
    vec = pl.BlockSpec((1, 512), lambda i: (0, 0))
    return pl.pallas_call(
        body, name="mix_bwd", grid=(ni,),
        in_specs=[act, act, act, act, pl.BlockSpec((TM_B, 512), lambda i: (i, 7)), vec, wsp, wsp,
                  pl.BlockSpec((TM_B, D), lambda i: (i, 4)), pl.BlockSpec((TM_B, D), lambda i: (i, 5)), rblk],
        out_specs=(pl.BlockSpec((TM_B, DP_REST), lambda i: (i, 0)), wsp, wsp, act, act, vec),
        out_shape=(_sds((T, DP_REST), BF16), _sds((512, D), BF16), _sds((512, D), BF16),
                   _sds((T, 512), F32), _sds((T, 512), F32), _sds((1, 512), F32)),
        scratch_shapes=[pltpu.VMEM((512, D), F32), pltpu.VMEM((512, D), F32)],
        compiler_params=_cp(("arbitrary",)))(o_na, u_hg, o_f, o_b, p_act, gain, w_na, w_hg, p_act, p_act, dmix)


def _wo_fwd(mix, w_o, h0, g_mlp):
    def body(mix_ref, w_ref, h0_ref, g_ref, h1_ref, m_ref):
        h1 = h0_ref[...] + _dot(mix_ref[...], w_ref[...])
        h1_ref[...] = h1
        r = lax.rsqrt(jnp.mean(h1 * h1, axis=-1, keepdims=True) + EPS)
        m_ref[...] = (h1 * r * g_ref[...]).astype(BF16)

    blk = pl.BlockSpec((TM_B, D), lambda i: (i, 0))
    return pl.pallas_call(
        body, name="wo_fwd", grid=(T // TM_B,),
        in_specs=[blk, pl.BlockSpec((D, D), lambda i: (0, 0)), blk, pl.BlockSpec((1, D), lambda i: (0, 0))],
        out_specs=(blk, blk), out_shape=(_sds((T, D), F32), _sds((T, D), BF16)),
        compiler_params=_cp(("parallel",)))(mix, w_o, h0, g_mlp)


def _wo_bwd(dh1_b, w_o, mix):
    ni = T // TM_B

    def body(dh_ref, w_ref, mix_ref, dmix_ref, dw_ref, acc):
        i = pl.program_id(0)
        dh = dh_ref[...]
        dmix_ref[...] = _dot(dh, w_ref[...], NT).astype(BF16)
        part = _dot(mix_ref[...], dh, TN)

        @pl.when(i == 0)
        def _():
            acc[...] = part

        @pl.when(i > 0)
        def _():
            acc[...] += part

        @pl.when(i == ni - 1)
        def _():
            dw_ref[...] = acc[...].astype(BF16)

    blk = pl.BlockSpec((TM_B, D), lambda i: (i, 0))
    wsp = pl.BlockSpec((D, D), lambda i: (0, 0))
    return pl.pallas_call(
        body, name="wo_bwd", grid=(ni,), in_specs=[blk, wsp, blk], out_specs=(blk, wsp),
        out_shape=(_sds((T, D), BF16), _sds((D, D), BF16)), scratch_shapes=[pltpu.VMEM((D, D), F32)],
        compiler_params=_cp(("arbitrary",)))(dh1_b, w_o, mix)


FF_B = D_FF // NDEV


def _loss_rows(xv, gv, tv, row0):
    r_io = lax.broadcasted_iota(jnp.int32, (xv.shape[0], 1), 0) + row0
    valid = (r_io >= NM) & (r_io < L)
    r = lax.rsqrt(jnp.mean(xv * xv, axis=-1, keepdims=True) + EPS)
    xh = xv * r
    err = jnp.where(valid, xh * gv - tv, 0.0)
    lpart = 0.5 * jnp.sum(jnp.sum(err * err, axis=-1, keepdims=True) * (1.0 / D), axis=0, keepdims=True)
    dy = err * (1.0 / D)
    dxh = dy * gv
    dh = r * (dxh - xh * jnp.mean(dxh * xh, axis=-1, keepdims=True))
    return lpart, dh, jnp.sum(dy * xh, axis=0, keepdims=True)


def _mlp_fwd_loss(m, wup_g, wdown_g, h1, g_final, tgt):
    nsub = TM_MM // TM_E

    def body(m_ref, wu_ref, wd_ref, h1_ref, g_ref, t_ref, loss_ref, dh_ref, dhb_ref, dg_ref, h2):
        i, j = pl.program_id(0), pl.program_id(1)
        up = jnp.maximum(_dot(m_ref[...], wu_ref[0]), 0.0)
        part = _dot((up * up).astype(BF16), wd_ref[0])

        @pl.when(j == 0)
        def _():
            h2[...] = h1_ref[...] + part

        @pl.when(j > 0)
        def _():
            h2[...] += part

        @pl.when(j == NDEV - 1)
        def _():
            lsum = jnp.zeros((1, 1), F32)
            gsum = jnp.zeros((1, D), F32)
            for s in range(nsub):
                rows = slice(s * TM_E, (s + 1) * TM_E)
                lpart, dh, gpart = _loss_rows(h2[rows, :], g_ref[...], t_ref[rows, :], i * TM_MM + s * TM_E)
                dh_ref[rows, :] = dh
                dhb_ref[rows, :] = dh.astype(BF16)
                lsum = lsum + lpart
                gsum = gsum + gpart
            lsum = jnp.broadcast_to(lsum, (1, 128))

            @pl.when(i == 0)
            def _():
                loss_ref[...] = lsum
                dg_ref[...] = gsum

            @pl.when(i > 0)
            def _():
                loss_ref[...] += lsum
                dg_ref[...] += gsum

    blk = pl.BlockSpec((TM_MM, D), lambda i, j: (i, 0))
    vec = pl.BlockSpec((1, D), lambda i, j: (0, 0))
    return pl.pallas_call(
        body, name="mlp_fwd_loss", grid=(T // TM_MM, NDEV),
        in_specs=[blk, pl.BlockSpec((1, D, FF_B), lambda i, j: (j, 0, 0)), pl.BlockSpec((1, FF_B, D), lambda i, j: (j, 0, 0)),
                  blk, vec, blk],
        out_specs=(pl.BlockSpec((1, 128), lambda i, j: (0, 0)), blk, blk, vec),
        out_shape=(_sds((1, 128), F32), _sds((T, D), F32), _sds((T, D), BF16), _sds((1, D), F32)),
        scratch_shapes=[pltpu.VMEM((TM_MM, D), F32)],
        compiler_params=_cp(("arbitrary", "arbitrary"), 56))(m, wup_g, wdown_g, h1, g_final, tgt)


def _mlp_bwd(m, dh2_b, wup_g, wdown_g, h1, g_mlp, dh2):
    ni = T // TM_B
    nsub = TM_B // TM_E

    def body(m_ref, dh_ref, wu_ref, wd_ref, h1_ref, g_ref, dres_ref, dwu_ref, dwd_ref, dh1_ref, dh1b_ref, dg_ref,
             dm_ref, acc_u, acc_d):
        j, i = pl.program_id(0), pl.program_id(1)
        rows = pl.ds(pl.multiple_of(i * TM_B, TM_B), TM_B)
        mv, dh = m_ref[...], dh_ref[...]
        r = jnp.maximum(_dot(mv, wu_ref[0]), 0.0)
        act = (r * r).astype(BF16)
        dact = _dot(dh, wd_ref[0], NT)
        dup = (dact * (2.0 * r)).astype(BF16)
        pd = _dot(act, dh, TN)
        pu = _dot(mv, dup, TN)
        dmv = _dot(dup, wu_ref[0], NT)

        @pl.when(i == 0)
        def _():
            acc_u[...] = pu
            acc_d[...] = pd

        @pl.when(i > 0)
        def _():
            acc_u[...] += pu
            acc_d[...] += pd

        @pl.when(i == ni - 1)
        def _():
            dwu_ref[0] = acc_u[...].astype(BF16)
            dwd_ref[0] = acc_d[...].astype(BF16)

        @pl.when(j == 0)
        def _():
            dm_ref[rows, :] = dmv

        @pl.when(j > 0)
        def _():
            dm_ref[rows, :] += dmv

        @pl.when(j == NDEV - 1)
        def _():
            gsum = jnp.zeros((1, D), F32)
            for s in range(nsub):
                sub = slice(s * TM_E, (s + 1) * TM_E)
                dm_rows = dm_ref[pl.ds(pl.multiple_of(i * TM_B + s * TM_E, TM_E), TM_E), :]
                dx, gpart = _norm_bwd_rows(h1_ref[sub, :], g_ref[...], dm_rows, dres_ref[sub, :])
                dh1_ref[sub, :] = dx
                dh1b_ref[sub, :] = dx.astype(BF16)
                gsum = gsum + gpart

            @pl.when(i == 0)
            def _():
                dg_ref[...] = gsum

            @pl.when(i > 0)
            def _():
                dg_ref[...] += gsum

    blk = pl.BlockSpec((TM_B, D), lambda j, i: (i, 0))
    late = pl.BlockSpec((TM_B, D), lambda j, i: (jnp.where(j == NDEV - 1, i, 0), 0))
    vec = pl.BlockSpec((1, D), lambda j, i: (0, 0))
    wus = pl.BlockSpec((1, D, FF_B), lambda j, i: (j, 0, 0))
    wds = pl.BlockSpec((1, FF_B, D), lambda j, i: (j, 0, 0))
    return pl.pallas_call(
        body, name="mlp_bwd", grid=(NDEV, ni), in_specs=[blk, blk, wus, wds, late, vec, late],
        out_specs=(wus, wds, late, late, vec),
        out_shape=(_sds((NDEV, D, FF_B), BF16), _sds((NDEV, FF_B, D), BF16), _sds((T, D), F32), _sds((T, D), BF16),
                   _sds((1, D), F32)),
        scratch_shapes=[pltpu.VMEM((T, D), F32), pltpu.VMEM((D, FF_B), F32), pltpu.VMEM((FF_B, D), F32)],
        compiler_params=_cp(("arbitrary", "arbitrary"), 56))(m, dh2_b, wup_g, wdown_g, h1, g_mlp, dh2)


def _adamw(parts, w, m, v, name):
    rr, cc = w.shape
    nslot = parts.shape[0]
    tr = rr
    for cand in (256, 128, 64):
        if rr % cand == 0 and rr > cand:
            tr = cand
            break
    c1 = 1.0 - ADAM_B1 ** ADAM_STEP
    c2 = 1.0 - ADAM_B2 ** ADAM_STEP

    def body(p_ref, w_ref, m_ref, v_ref, g_ref, d_ref, nm_ref, nv_ref):
        g = p_ref[0].astype(F32)
        for s in range(1, nslot):
            g = g + p_ref[s].astype(F32)
        mn = ADAM_B1 * m_ref[...] + (1.0 - ADAM_B1) * g
        vn = ADAM_B2 * v_ref[...] + (1.0 - ADAM_B2) * (g * g)
        g_ref[...] = g
        nm_ref[...] = mn
        nv_ref[...] = vn
        d_ref[...] = -ADAM_LR * ((mn / c1) / (jnp.sqrt(vn / c2) + ADAM_EPS) + ADAM_WD * w_ref[...])

    blk = pl.BlockSpec((tr, cc), lambda i: (i, 0))
    return pl.pallas_call(
        body, name=name, grid=(rr // tr,),
        in_specs=[pl.BlockSpec((nslot, tr, cc), lambda i: (0, i, 0)), blk, blk, blk],
        out_specs=(blk,) * 4, out_shape=(_sds((rr, cc), F32),) * 4,
        compiler_params=_cp(("parallel",)))(parts, w, m, v)


RPB_N = NA_HEADS * 15 * 31
RPB_PAD = 4096
OWN_ROWS = NM + 8


def _pad_rows(a, rows):
    return jnp.pad(a, ((0, rows - a.shape[0]),) + ((0, 0),) * (a.ndim - 1))


def _pack_owned(meta_blk, lb_blk):
    return jnp.concatenate([meta_blk, _pad_rows(lb_blk.reshape(2, 128), 8)], axis=0)


LOSS_ROW = 28


def _pack_replicated(n_mix, n_mlp, n_final, hg_gain, rpb, loss_row=None):
    flat = _pad_rows(rpb.reshape(RPB_N), RPB_PAD)
    gain8 = _pad_rows(hg_gain.reshape(4, 128), 8)
    if loss_row is not None:
        gain8 = gain8 + jnp.pad(loss_row, ((LOSS_ROW - 24, 31 - LOSS_ROW), (0, 0)))
    return jnp.concatenate([n_mix.reshape(8, 128), n_mlp.reshape(8, 128), n_final.reshape(8, 128), gain8,
                            flat.reshape(32, 128)], axis=0)


def _unpack_replicated(a):
    return (a[0:8].reshape(1, D), a[8:16].reshape(1, D), a[16:24].reshape(D), a[24:28].reshape(1, 512),
            a[32:64].reshape(RPB_PAD)[:RPB_N].reshape(1, NA_HEADS, 15, 31))


def kernel(x, meta_tokens, w_in, w_na_out, w_hg_out, w_o, w_up, w_down, norm_mix, norm_mlp, norm_final, hg_norm, na_rpb, hg_lb_logits, loss_target, m_meta_tokens, m_w_in, m_w_na_out, m_w_hg_out, m_w_o, m_w_up, m_w_down, m_norm_mix, m_norm_mlp, m_norm_final, m_hg_norm, m_na_rpb, m_hg_lb_logits, v_meta_tokens, v_w_in, v_w_na_out, v_w_hg_out, v_w_o, v_w_up, v_w_down, v_norm_mix, v_norm_mlp, v_norm_final, v_hg_norm, v_na_rpb, v_hg_lb_logits):
    owned = _pack_owned(meta_tokens, hg_lb_logits)
    first_masks = (ALL_PEERS, SAME_CORE_AND_SIBLING)
    first, tok = _exchange_start([owned, w_in[0].astype(BF16)], [False] * 2, "gather_first_start", first_masks)
    bias_tab = _na_bias_table(_tie(jnp.pad(na_rpb[0], ((0, 0), (0, 0), (0, 128 - 31))), tok, "tie_bias_table"))
    later = [w[0].astype(BF16) for w in (w_na_out, w_hg_out, w_o, w_up, w_down)]
    lead = jnp.zeros((NM, D), F32) + tok[0, 0]
    h0_rows = jnp.concatenate([lead, x[0], jnp.zeros((T - L, D), F32)], axis=0)
    tgt = jnp.concatenate([lead, loss_target[0], jnp.zeros((T - L, D), F32)], axis=0)
    (owned_g, _), first = _exchange_wait(first, [False] * 2, [h0_rows], "gather_small_wait", first_masks, which=(0,))
    meta_full = jnp.transpose(owned_g[:, 0:NM, :], (1, 0, 2)).reshape(NM, D)
    logits = jnp.transpose(owned_g[:, NM:NM + 2, :].reshape(NDEV, 2, 2, 64), (1, 2, 0, 3)).reshape(2, 2, 512)
    h0 = lax.dynamic_update_slice(h0_rows, meta_full, (0, 0))
    a, a_t = _norm_fwd_t(h0, norm_mix, "norm_mix_fwd")
    (_, win_l), _ = _exchange_wait(first, [False] * 2, [a, logits, tgt, bias_tab] + later, "gather_first_wait", first_masks,
                                   which=(1,))
    (win_g,) = _forward_to_sibling([win_l], "gather_first_forward")
    later[0] = _tie(later[0], win_g, "tie_gather_rest")
    gather_rest, tok = _exchange_start(later, [False] * 5, "gather_rest_start")
    win_g = _tie(win_g, tok, "tie_inproj")

    p_act = _inproj_fwd(a, win_g)
    o_na, lse = _na_fwd(p_act, bias_tab)
    qh, k_f, b_f, k_b, b_b = _hg_pre(p_act, logits)
    o_f, st_f = _hg_scan_fwd(qh, k_f, b_f, p_act, False)
    o_b, st_b = _hg_scan_fwd(qh, k_b, b_b, p_act, True)
    (wna_g, whg_g, wo_g, _, _), gather_rest = _exchange_wait(
        gather_rest, [False] * 5, [o_f, o_b, o_na], "gather_rest_wait_a", which=(0, 1, 2))
    w_na_full = jnp.transpose(wna_g, (1, 0, 2)).reshape(512, D)
    w_hg_full = jnp.transpose(whg_g, (1, 0, 2)).reshape(512, D)
    mix, u_hg = _mix_fwd(o_na, o_f, o_b, hg_norm, w_na_full, w_hg_full, p_act)
    h1, m_act = _wo_fwd(mix, wo_g.reshape(D, D), h0, norm_mlp)
    (_, _, wo_g, wup_g, wdown_g), _ = _exchange_wait(gather_rest, [False] * 5, [m_act], "gather_rest_wait_b", which=(3, 4))
    w_o_full = wo_g.reshape(D, D)
    loss_part, dh2, dh2_b, d_nfinal = _mlp_fwd_loss(m_act, wup_g, wdown_g, h1, norm_final.reshape(1, D), tgt)

    dwup_p, dwdown_p, dh1, dh1_b, d_nmlp = _mlp_bwd(m_act, dh2_b, wup_g, wdown_g, h1, norm_mlp, dh2)
    sc_mlp, tok = _exchange_start([dwup_p, dwdown_p], [True] * 2, "scatter_mlp_start")
    dmix, dwo = _wo_bwd(_tie(dh1_b, tok, "tie_wo_bwd"), w_o_full, mix)
    sc_wo, tok = _exchange_start([dwo.reshape(NDEV, D // NDEV, D)], [True], "scatter_wo_start")
    dp_rest, dwna, dwhg, do_na, do_hg, d_gain = _mix_bwd(
        o_na, u_hg, o_f, o_b, hg_norm, w_na_full, w_hg_full, p_act, _tie(dmix, tok, "tie_mix_bwd"))
    owner_cols = lambda w: jnp.transpose(w.reshape(512, NDEV, D // NDEV), (1, 0, 2))
    sc_br, tok = _exchange_start([owner_cols(dwna), owner_cols(dwhg)], [True] * 2, "scatter_branch_start")
    do_hg = _tie(do_hg, tok, "tie_hg_scan_bwd")
    dq_f, dk_f, db_f, dv_f = _hg_scan_bwd(qh, k_f, b_f, p_act, st_f, do_hg, False)
    dq_b, dk_b, db_b, dv_b = _hg_scan_bwd(qh, k_b, b_b, p_act, st_b, do_hg, True)
    dp_rest, d_logits = _hg_pre_bwd(p_act, logits, dq_f, dq_b, dk_f, dk_b, db_f, db_b, dv_f, dv_b, dp_rest)
    dq_na, dk_na, dv_na, dbias = _na_bwd(p_act, do_na, lse, bias_tab)
    dp_na = jnp.concatenate([dq_na.astype(BF16), dk_na.astype(BF16), dv_na.astype(BF16)], axis=1)
    dwin_p = _inproj_bwd_dw(a_t, dp_na, dp_rest)
    sc_in, tok = _exchange_start([dwin_p], [True], "scatter_in_start")
    dh0, d_nmix = _inproj_bwd_da(_tie(dp_na, tok, "tie_inproj_bwd_da"), dp_rest, win_g, h0, norm_mix, dh1)
    d_rpb = _na_rpb_reduce(_tie(dbias, tok, "tie_rpb_reduce"))[:, :, :31]

    res = {}

    def update(nm, parts, w, mm, vv):
        res[nm] = [r[None] for r in _adamw(parts, w[0], mm[0], vv[0], "adamw_" + nm)]
        return res[nm][1]

    wup_r, wdown_r = _exchange_wait(sc_mlp, [True] * 2, [dh0, d_rpb], "scatter_mlp_wait")
    update("w_up", wup_r, w_up, m_w_up, v_w_up)
    last = update("w_down", wdown_r, w_down, m_w_down, v_w_down)
    (wo_r,) = _exchange_wait(sc_wo, [True], [last], "scatter_wo_wait")
    last = update("w_o", wo_r, w_o, m_w_o, v_w_o)
    wna_r, whg_r = _exchange_wait(sc_br, [True] * 2, [last], "scatter_branch_wait")
    update("w_na_out", wna_r, w_na_out, m_w_na_out, v_w_na_out)
    last = update("w_hg_out", whg_r, w_hg_out, m_w_hg_out, v_w_hg_out)

    d_meta = jnp.transpose(dh0[0:NM].reshape(NM, NDEV, 128), (1, 0, 2))
    d_lg = jnp.transpose(d_logits.reshape(2, 2, NDEV, 64), (2, 0, 1, 3)).reshape(NDEV, 2, 128)
    owned_p = jnp.concatenate([d_meta, jnp.pad(d_lg, ((0, 0), (0, OWN_ROWS - NM - 2), (0, 0)))], axis=1)
    repl_p = _pack_replicated(d_nmix, d_nmlp, d_nfinal, d_gain, d_rpb, loss_part)
    grad_x = dh0[NM:L][None]
    done_first = [grad_x] + [res[nm][0] for nm in ("w_up", "w_down", "w_o", "w_na_out", "w_hg_out")]
    owned_r, repl_r = _exchange([owned_p, repl_p], [True, False], "scatter_small", done_first)
    own = _adamw(owned_r, owned, _pack_owned(m_meta_tokens, m_hg_lb_logits), _pack_owned(v_meta_tokens, v_hg_lb_logits),
                 "adamw_owned_small")
    res["meta_tokens"] = [r[0:NM] for r in own]
    res["hg_lb_logits"] = [r[NM:NM + 2].reshape(2, 2, 64) for r in own]
    rep = _adamw(repl_r, _pack_replicated(norm_mix, norm_mlp, norm_final, hg_norm, na_rpb),
                 _pack_replicated(m_norm_mix, m_norm_mlp, m_norm_final, m_hg_norm, m_na_rpb),
                 _pack_replicated(v_norm_mix, v_norm_mlp, v_norm_final, v_hg_norm, v_na_rpb), "adamw_replicated")
    for q in range(4):
        um = _unpack_replicated(rep[q])
        for nm, val in zip(("norm_mix", "norm_mlp", "norm_final", "hg_norm", "na_rpb"), um):
            res.setdefault(nm, [None] * 4)[q] = val
    (win_r,) = _exchange_wait(sc_in, [True], [rep[1], own[1]], "scatter_in_wait")
    update("w_in", win_r, w_in, m_w_in, v_w_in)

    loss = jnp.sum(repl_r[:, LOSS_ROW, 0])
    order = ("meta_tokens", "w_in", "w_na_out", "w_hg_out", "w_o", "w_up", "w_down", "norm_mix", "norm_mlp", "norm_final",
             "hg_norm", "na_rpb", "hg_lb_logits")
    outs = [loss, grad_x]
    for q in range(4):
        outs += [res[nm][q] for nm in order]
    return tuple(outs)
```

```python
import functools

import numpy as np
import jax
import jax.numpy as jnp
from jax import lax
from jax.experimental import pallas as pl
from jax.experimental.pallas import tpu as pltpu

F32 = jnp.float32
BF16 = jnp.bfloat16

D = 1024
SEQ = 2048
NM = 16
L = SEQ + NM
T = 2176
NDEV = 8
EPS = 1e-6
GRID_W = 64
ROWS = SEQ // GRID_W
NA_HEADS = 8
NA_DH = 64
NA_SCALE = NA_DH ** -0.5
HG_HEADS = 4
HG_C = 16
NCHUNK = L // HG_C
D_FF = 4096
IN_COLS = 6144
NEG = -1e30

ADAM_LR = 0.001
ADAM_B1 = 0.9
ADAM_B2 = 0.999
ADAM_EPS = 1e-08
ADAM_WD = 0.01
ADAM_STEP = 10

MESH_ID = pl.DeviceIdType.MESH
ANY = pl.BlockSpec(memory_space=pl.ANY)

NN = (((1,), (0,)), ((), ()))
NT = (((1,), (1,)), ((), ()))
TN = (((0,), (0,)), ((), ()))


def _cp(sem=None, vmem_mb=48):
    return pltpu.CompilerParams(dimension_semantics=sem, vmem_limit_bytes=vmem_mb * 1024 * 1024)


def _dot(a, b, dims=NN):
    return lax.dot_general(a, b, dims, preferred_element_type=F32)


def _sds(shape, dtype):
    return jax.ShapeDtypeStruct(shape, dtype)


HBM = pl.BlockSpec(memory_space=pltpu.HBM)
SEM = pl.BlockSpec(memory_space=pltpu.SEMAPHORE)
EFFECT = pltpu.SideEffectType.DATAFLOW_SIDE_EFFECTING


def _exchange(arrs, scatter, name, after=()):
    n = len(arrs)
    after = list(after)
    out_shapes = []
    for a, sc in zip(arrs, scatter):
        out_shapes.append(_sds(a.shape if sc else (NDEV,) + a.shape, a.dtype))

    def body(*refs):
        ins, outs = refs[:n], refs[n + len(after):2 * n + len(after)]
        send_sems, recv_sems, loc_sems = refs[2 * n + len(after):]
        me = 4 * lax.axis_index("x") + 2 * lax.axis_index("y") + lax.axis_index("c")
        copies = []
        for k in range(n):
            src_me = ins[k].at[me] if scatter[k] else ins[k]
            loc = pltpu.make_async_copy(src_me, outs[k].at[me], loc_sems.at[k])
            loc.start()
            copies.append(loc)
        remote = sum(_peer_copies(ins, outs, scatter, send_sems, recv_sems), [])
        for cp in remote:
            cp.start()
        for cp in remote:
            cp.wait_recv()
        for cp in remote:
            cp.wait_send()
        for cp in copies:
            cp.wait()

    return pl.pallas_call(
        body, name=name, out_shape=tuple(out_shapes), in_specs=[ANY] * (n + len(after)), out_specs=tuple([ANY] * n),
        scratch_shapes=[pltpu.SemaphoreType.DMA((n * (NDEV - 1),)), pltpu.SemaphoreType.DMA((n * (NDEV - 1),)),
                        pltpu.SemaphoreType.DMA((n,))],
    )(*arrs, *after)


def _forward_to_sibling(bufs, name):
    n = len(bufs)

    def body(*refs):
        ins, outs = refs[:n], refs[n:2 * n]
        send_sems, recv_sems = refs[2 * n:]
        x, y, c = lax.axis_index("x"), lax.axis_index("y"), lax.axis_index("c")
        copies = []
        for k in range(n):
            for j, (cx, cy) in enumerate(((1 - x, y), (x, 1 - y), (1 - x, 1 - y))):
                slot = 4 * cx + 2 * cy + c
                copies.append(pltpu.make_async_remote_copy(
                    src_ref=ins[k].at[slot], dst_ref=outs[k].at[slot], send_sem=send_sems.at[3 * k + j],
                    recv_sem=recv_sems.at[3 * k + j], device_id=(x, y, 1 - c), device_id_type=MESH_ID))
        for cp in copies:
            cp.start()
        for cp in copies:
            cp.wait_recv()
        for cp in copies:
            cp.wait_send()

    return pl.pallas_call(
        body, name=name, out_shape=tuple(_sds(b.shape, b.dtype) for b in bufs), in_specs=[ANY] * n,
        out_specs=tuple([ANY] * n), input_output_aliases={k: k for k in range(n)},
        scratch_shapes=[pltpu.SemaphoreType.DMA((3 * n,)), pltpu.SemaphoreType.DMA((3 * n,))],
    )(*bufs)


ALL_PEERS = tuple(range(1, NDEV))
SAME_CORE_AND_SIBLING = (1, 2, 4, 6)
ALL_BUT_FAR_OTHER_CORE = (1, 2, 3, 4, 5, 6)


def _sibling_swap(x, name):
    def body(x_ref, o_ref, send_sem, recv_sem):
        sib = (lax.axis_index("x"), lax.axis_index("y"), 1 - lax.axis_index("c"))
        cp = pltpu.make_async_remote_copy(src_ref=x_ref, dst_ref=o_ref, send_sem=send_sem, recv_sem=recv_sem,
                                          device_id=sib, device_id_type=MESH_ID)
        cp.start()
        cp.wait()

    return pl.pallas_call(
        body, name=name, out_shape=_sds(x.shape, x.dtype), in_specs=[ANY], out_specs=ANY,
        scratch_shapes=[pltpu.SemaphoreType.DMA(()), pltpu.SemaphoreType.DMA(())])(x)


def _add_bf16(a, b, name):
    n, rr, cc = a.shape

    def body(a_ref, b_ref, o_ref):
        o_ref[...] = (a_ref[...].astype(F32) + b_ref[...].astype(F32)).astype(BF16)

    blk = pl.BlockSpec((1, rr // 2, cc), lambda i, j: (i, j, 0))
    return pl.pallas_call(body, name=name, grid=(n, 2), in_specs=[blk, blk], out_specs=blk, out_shape=_sds(a.shape, BF16),
                          compiler_params=_cp(("parallel", "parallel")))(a, b)


def _peer_copies(srcs, lands, scatter, send_sems, recv_sems, masks=ALL_PEERS):
    x, y, c = lax.axis_index("x"), lax.axis_index("y"), lax.axis_index("c")
    me = 4 * x + 2 * y + c
    out = []
    for k in range(len(srcs)):
        out.append([])
        for m in (masks[k] if isinstance(masks[0], tuple) else masks):
            px, py, pc = x ^ (m >> 2), y ^ ((m >> 1) & 1), c ^ (m & 1)
            src = srcs[k].at[4 * px + 2 * py + pc] if scatter[k] else srcs[k]
            out[k].append(pltpu.make_async_remote_copy(
                src_ref=src, dst_ref=lands[k].at[me], send_sem=send_sems.at[k * (NDEV - 1) + m - 1],
                recv_sem=recv_sems.at[k * (NDEV - 1) + m - 1],
                device_id=(px, py, pc), device_id_type=MESH_ID))
    return out


def _exchange_start(arrs, scatter, name, masks=ALL_PEERS, absent=None):
    n = len(arrs)
    me = 4 * lax.axis_index("x") + 2 * lax.axis_index("y") + lax.axis_index("c")
    lands = []
    for a, sc in zip(arrs, scatter):
        own = lax.dynamic_index_in_dim(a, me, 0, keepdims=True) if sc else a[None]
        shape = a.shape if sc else (NDEV,) + a.shape
        land = lax.dynamic_update_index_in_dim(lax.empty(shape, a.dtype), own, me, 0)
        if absent is not None:
            land = lax.dynamic_update_index_in_dim(land, jnp.zeros_like(own), absent, 0)
        lands.append(land)

    def body(*refs):
        srcs, lnds = refs[:n], refs[n:2 * n]
        send_sems, recv_sems = refs[2 * n], refs[2 * n + 1]
        token = refs[-1]
        for cp in sum(_peer_copies(srcs, lnds, scatter, send_sems, recv_sems, masks), []):
            cp.start()
        token[...] = jnp.zeros_like(token)

    ops = [pltpu.with_memory_space_constraint(a, pltpu.HBM) for a in list(arrs) + lands]
    res = pl.pallas_call(
        body, name=name,
        out_shape=(pltpu.SemaphoreType.DMA((n * (NDEV - 1),)), pltpu.SemaphoreType.DMA((n * (NDEV - 1),)))
        + tuple(pltpu.HBM(o.shape, o.dtype) for o in ops) + (_sds((8, 128), F32),),
        in_specs=[HBM] * (2 * n), out_specs=(SEM, SEM) + (HBM,) * (2 * n) + (pl.BlockSpec(memory_space=pltpu.VMEM),),
        input_output_aliases={k: 2 + k for k in range(2 * n)},
        compiler_params=pltpu.CompilerParams(has_side_effects=EFFECT),
    )(*ops)
    return res[:-1], res[-1]


def _exchange_wait(handle, scatter, after, name, masks=ALL_PEERS, which=None):
    send_sems, recv_sems = handle[0], handle[1]
    bufs = handle[2:]
    n = len(bufs) // 2
    after = list(after)

    def body(*refs):
        srcs, lnds = refs[:n], refs[n:2 * n]
        copies = _peer_copies(srcs, lnds, scatter, refs[2 * n], refs[2 * n + 1], masks)
        for k in (range(n) if which is None else which):
            for cp in copies[k]:
                cp.wait_send()
                cp.wait_recv()

    res = pl.pallas_call(
        body, name=name, out_shape=tuple(pltpu.HBM(b.shape, b.dtype) for b in bufs),
        in_specs=[HBM] * (2 * n) + [SEM, SEM] + [ANY] * len(after), out_specs=(HBM,) * (2 * n),
        input_output_aliases={k: k for k in range(2 * n)},
        compiler_params=pltpu.CompilerParams(has_side_effects=EFFECT),
    )(*bufs, send_sems, recv_sems, *after)
    return res[n:] if which is None else (res[n:], (send_sems, recv_sems) + tuple(res))


def _tie(x, token, name):
    def body(x_ref, t_ref, o_ref):
        del x_ref, t_ref, o_ref

    return pl.pallas_call(body, name=name, out_shape=_sds(x.shape, x.dtype), in_specs=[ANY, ANY], out_specs=ANY,
                          input_output_aliases={0: 0})(x, token)


TM_E = 272


def _norm_fwd_t(h, g, name):
    def body(h_ref, g_ref, o_ref, ot_ref):
        xv = h_ref[...]
        r = lax.rsqrt(jnp.mean(xv * xv, axis=-1, keepdims=True) + EPS)
        y = xv * r * g_ref[...]
        o_ref[...] = y.astype(BF16)
        ot_ref[...] = y.T.astype(BF16)

    return pl.pallas_call(
        body, name=name, grid=(T // 128,),
        in_specs=[pl.BlockSpec((128, D), lambda i: (i, 0)), pl.BlockSpec((1, D), lambda i: (0, 0))],
        out_specs=(pl.BlockSpec((128, D), lambda i: (i, 0)), pl.BlockSpec((D, 128), lambda i: (0, i))),
        out_shape=(_sds((T, D), BF16), _sds((D, T), BF16)), compiler_params=_cp(("parallel",)))(h, g)


def _norm_bwd_rows(xv, gv, dnv, dres):
    r = lax.rsqrt(jnp.mean(xv * xv, axis=-1, keepdims=True) + EPS)
    xh = xv * r
    dxh = dnv * gv
    dx = dres + r * (dxh - xh * jnp.mean(dxh * xh, axis=-1, keepdims=True))
    return dx, jnp.sum(dnv * xh, axis=0, keepdims=True)


TM_MM = 1088


def _inproj_fwd(a, w_g):
    nb = w_g.shape[2]

    def body(a_ref, w_ref, o_ref):
        o_ref[...] = _dot(a_ref[...], w_ref[0])

    return pl.pallas_call(
        body, name="inproj_fwd", grid=(T // TM_MM, NDEV),
        in_specs=[pl.BlockSpec((TM_MM, D), lambda i, j: (i, 0)), pl.BlockSpec((1, D, nb), lambda i, j: (j, 0, 0))],
        out_specs=pl.BlockSpec((TM_MM, nb), lambda i, j: (i, j)), out_shape=_sds((T, NDEV * nb), F32),
        compiler_params=_cp(("parallel", "parallel")))(a, w_g)


TM_B = 544


W_IN_B = IN_COLS // NDEV


NA_BLKS = 1536 // W_IN_B


def _dp_specs(rows, row_index):
    return [pl.BlockSpec((rows, W_IN_B), lambda *g: (row_index(*g), jnp.minimum(g[-1], NA_BLKS - 1))),
            pl.BlockSpec((rows, W_IN_B), lambda *g: (row_index(*g), jnp.maximum(g[-1] - NA_BLKS, 0)))]


def _inproj_bwd_dw(a_t, dp_na, dp_rest):
    def body(at_ref, na_ref, rest_ref, dw_ref):
        j = pl.program_id(0)

        @pl.when(j < NA_BLKS)
        def _():
            dw_ref[0] = _dot(at_ref[...], na_ref[...]).astype(BF16)

        @pl.when(j >= NA_BLKS)
        def _():
            dw_ref[0] = _dot(at_ref[...], rest_ref[...]).astype(BF16)

    return pl.pallas_call(
        body, name="inproj_bwd_dw", grid=(NDEV,),
        in_specs=[pl.BlockSpec((D, T), lambda j: (0, 0))] + _dp_specs(T, lambda j: 0),
        out_specs=pl.BlockSpec((1, D, W_IN_B), lambda j: (j, 0, 0)), out_shape=_sds((NDEV, D, W_IN_B), BF16),
        compiler_params=_cp(("parallel",)))(a_t, dp_na, dp_rest)


def _inproj_bwd_da(dp_na, dp_rest, w_g, h0, g_mix, dh1):
    nsub = TM_MM // TM_E

    def body(na_ref, rest_ref, w_ref, h0_ref, g_ref, dres_ref, dh0_ref, dg_ref, da):
        i, j = pl.program_id(0), pl.program_id(1)
        dpv = jnp.where(j < NA_BLKS, na_ref[...], rest_ref[...])
        dav = _dot(dpv, w_ref[0], NT)

        @pl.when(j == 0)
        def _():
            da[...] = dav

        @pl.when(j > 0)
        def _():
            da[...] += dav

        @pl.when(j == NDEV - 1)
        def _():
            gsum = jnp.zeros((1, D), F32)
            for s in range(nsub):
                sub = slice(s * TM_E, (s + 1) * TM_E)
                dx, gpart = _norm_bwd_rows(h0_ref[sub, :], g_ref[...], da[sub, :], dres_ref[sub, :])
                dh0_ref[sub, :] = dx
                gsum = gsum + gpart

            @pl.when(i == 0)
            def _():
                dg_ref[...] = gsum

            @pl.when(i > 0)
            def _():
                dg_ref[...] += gsum

    rblk = pl.BlockSpec((TM_MM, D), lambda i, j: (i, 0))
    vec = pl.BlockSpec((1, D), lambda i, j: (0, 0))
    return pl.pallas_call(
        body, name="inproj_bwd_da", grid=(T // TM_MM, NDEV),
        in_specs=_dp_specs(TM_MM, lambda i, j: i) + [pl.BlockSpec((1, D, W_IN_B), lambda i, j: (j, 0, 0)), rblk, vec, rblk],
        out_specs=(rblk, vec), out_shape=(_sds((T, D), F32), _sds((1, D), F32)),
        scratch_shapes=[pltpu.VMEM((TM_MM, D), F32)],
        compiler_params=_cp(("arbitrary", "arbitrary"), 56))(dp_na, dp_rest, w_g, h0, g_mix, dh1)


NA_QB = 256
NA_GROUPS = ROWS // 4
NA_UROWS = 11
NA_KW = NA_UROWS * GRID_W
NA_KU = 768


def _na_row_offset(var, i, j):
    valid = (j < 8, i <= j < i + 8, 3 <= j < NA_UROWS)[var]
    return (j - i + (7, 3, 0)[var]) if valid else None


def _na_bias_table(rp):
    def body(r_ref, o_ref):
        row3 = lax.broadcasted_iota(jnp.int32, (15, GRID_W, 128), 1)
        lane3 = lax.broadcasted_iota(jnp.int32, (15, GRID_W, 128), 2)
        w3 = lane3 & (GRID_W - 1)
        cs3 = jnp.clip(row3 - 8, 0, GRID_W - 16)
        lane = lax.broadcasted_iota(jnp.int32, (GRID_W, 128), 1)
        neg = jnp.full((GRID_W, 128), NEG, F32)
        z = jnp.stack([jnp.broadcast_to(r_ref[0, a:a + 1, :], (GRID_W, 128)) for a in range(15)])
        for bit in range(6):
            sh = 1 << bit
            z = jnp.where((row3 & sh) != 0, jnp.roll(z, sh, axis=2), z)
        z = jnp.roll(z, 128 - 15, axis=2)
        z = jnp.where(lane3 < GRID_W, z, 0.0)
        z = z + jnp.roll(z, GRID_W, axis=2)
        tabs = jnp.where((w3 >= cs3) & (w3 < cs3 + 16), z, NEG)
        tail = jnp.where(lane < GRID_W + NM, 0.0, NEG)
        for var in range(3):
            for i in range(4):
                for jp in range(NA_KU // 128):
                    halves = []
                    for j in (2 * jp, 2 * jp + 1):
                        a = _na_row_offset(var, i, j) if j < NA_UROWS else None
                        halves.append(tail if j >= NA_UROWS else (neg if a is None else tabs[a]))
                    o_ref[var, 0, i * 64:(i + 1) * 64, jp * 128:(jp + 1) * 128] = jnp.where(lane < GRID_W, halves[0], halves[1])

    return pl.pallas_call(
        body, name="na_bias_table", grid=(NA_HEADS,),
        in_specs=[pl.BlockSpec((1, 15, 128), lambda h: (h, 0, 0))],
        out_specs=pl.BlockSpec((3, 1, NA_QB, NA_KU), lambda h: (0, h, 0, 0)),
        out_shape=_sds((3, NA_HEADS, NA_QB, NA_KU), F32), compiler_params=_cp(("parallel",)))(rp)


def _na_var(g):
    return jnp.where(g == 0, 0, jnp.where(g == NA_GROUPS - 1, 2, 1))


def _na_load_window(src_ref, dst, g):
    us = jnp.clip(4 * g - 4, 0, ROWS - NA_UROWS)
    kstart = pl.multiple_of(NM + GRID_W * us, 16)
    dst[0:NA_KW, :] = src_ref[pl.ds(kstart, NA_KW), :].astype(BF16)
    dst[NA_KW:NA_KW + NM, :] = src_ref[0:NM, :].astype(BF16)
    dst[NA_KW + NM:, :] = jnp.zeros((NA_KU - NA_KW - NM, 128), BF16)
    return kstart


def _na_fwd(p_act, bias_tab):
    def body(q_ref, k_ref, v_ref, b_ref, o_ref, lse_ref, ku, vu):
        g = pl.program_id(1)
        _na_load_window(k_ref, ku, g)
        _na_load_window(v_ref, vu, g)
        qstart = pl.multiple_of(NM + NA_QB * g, 16)
        q = q_ref[pl.ds(qstart, NA_QB), :]
        lane = lax.broadcasted_iota(jnp.int32, (NA_QB, 128), 1)
        o_h, lse_h = [], []
        for h in range(2):
            hm = (lane < 64) if h == 0 else (lane >= 64)
            qm = (jnp.where(hm, q, 0.0) * NA_SCALE).astype(BF16)
            s = _dot(qm, ku[...], NT) + b_ref[0, h]
            m = jnp.max(s, axis=-1, keepdims=True)
            p = jnp.exp(s - m)
            l = jnp.sum(p, axis=-1, keepdims=True)
            o_h.append(_dot(p.astype(BF16), vu[...]) / l)
            lse_h.append(jnp.broadcast_to(m + jnp.log(l), (NA_QB, 128)))
        o_ref[pl.ds(qstart, NA_QB), :] = jnp.where(lane < 64, o_h[0], o_h[1]).astype(BF16)
        lse_ref[0, pl.ds(qstart, NA_QB), :] = jnp.where(lane < 64, lse_h[0], lse_h[1])

        @pl.when(g == 0)
        def _():
            qm_ = q_ref[0:NM, :]
            lane_m = lax.broadcasted_iota(jnp.int32, (NM, 128), 1)
            km, vm = ku[NA_KW:NA_KW + NM, :], vu[NA_KW:NA_KW + NM, :]
            om = []
            for h in range(2):
                hm = (lane_m < 64) if h == 0 else (lane_m >= 64)
                s = _dot(jnp.where(hm, qm_, 0.0).astype(BF16), km, NT) * NA_SCALE
                p = jnp.exp(s - jnp.max(s, axis=-1, keepdims=True))
                l = jnp.sum(p, axis=-1, keepdims=True)
                om.append(_dot(p.astype(BF16), vm) / l)
            o_ref[0:NM, :] = jnp.where(lane_m < 64, om[0], om[1]).astype(BF16)
            o_ref[L:T, :] = jnp.zeros((T - L, 128), BF16)
            lse_ref[0, 0:NM, :] = jnp.zeros((NM, 128), F32)
            lse_ref[0, L:T, :] = jnp.zeros((T - L, 128), F32)

    col = lambda off: pl.BlockSpec((T, 128), lambda hp, g: (0, off + hp))
    return pl.pallas_call(
        body, name="na_fwd", grid=(4, NA_GROUPS),
        in_specs=[col(0), col(4), col(8),
                  pl.BlockSpec((1, 2, NA_QB, NA_KU), lambda hp, g: (_na_var(g), hp, 0, 0))],
        out_specs=(pl.BlockSpec((T, 128), lambda hp, g: (0, hp)), pl.BlockSpec((1, T, 128), lambda hp, g: (hp, 0, 0))),
        out_shape=(_sds((T, 512), BF16), _sds((4, T, 128), F32)),
        scratch_shapes=[pltpu.VMEM((NA_KU, 128), BF16), pltpu.VMEM((NA_KU, 128), BF16)],
        compiler_params=_cp(("parallel", "arbitrary")))(p_act, p_act, p_act, bias_tab)


def _na_bwd(p_act, do, lse, bias_tab):
    def body(q_ref, k_ref, v_ref, do_ref, lse_ref, b_ref, dq_ref, dk_ref, dv_ref, db_ref, ku, vu):
        g = pl.program_id(1)

        @pl.when(g == 0)
        def _():
            dq_ref[...] = jnp.zeros((T, 128), F32)
            dk_ref[...] = jnp.zeros((T, 128), F32)
            dv_ref[...] = jnp.zeros((T, 128), F32)

        kstart = _na_load_window(k_ref, ku, g)
        _na_load_window(v_ref, vu, g)
        qstart = pl.multiple_of(NM + NA_QB * g, 16)
        q = q_ref[pl.ds(qstart, NA_QB), :]
        dov = do_ref[pl.ds(qstart, NA_QB), :]
        lsev = lse_ref[0, pl.ds(qstart, NA_QB), :]
        lane = lax.broadcasted_iota(jnp.int32, (NA_QB, 128), 1)
        first = (g == 0) | (g == 1) | (g == NA_GROUPS - 1)
        dq_h = []
        dku = jnp.zeros((NA_KU, 128), F32)
        dvu = jnp.zeros((NA_KU, 128), F32)
        for h in range(2):
            hm = (lane < 64) if h == 0 else (lane >= 64)
            qm = (jnp.where(hm, q, 0.0) * NA_SCALE).astype(BF16)
            dom = jnp.where(hm, dov, 0.0).astype(BF16)
            s = _dot(qm, ku[...], NT) + b_ref[0, h]
            p = jnp.exp(s - lsev[:, 64 * h:64 * h + 1])
            dp = _dot(dom, vu[...], NT)
            delta = jnp.sum(p * dp, axis=-1, keepdims=True)
            ds = p * (dp - delta)

            @pl.when(first)
            def _():
                db_ref[0, h] = ds

            @pl.when(jnp.logical_not(first))
            def _():
                db_ref[0, h] += ds

            dsb = ds.astype(BF16)
            dq_h.append(_dot(dsb, ku[...]) * NA_SCALE)
            dku = dku + _dot(dsb, qm, TN)
            dvu = dvu + _dot(p.astype(BF16), dom, TN)
        dq_ref[pl.ds(qstart, NA_QB), :] = jnp.where(lane < 64, dq_h[0], dq_h[1])
        dk_ref[pl.ds(kstart, NA_KW), :] += dku[0:NA_KW]
        dv_ref[pl.ds(kstart, NA_KW), :] += dvu[0:NA_KW]
        dk_ref[0:NM, :] += dku[NA_KW:NA_KW + NM]
        dv_ref[0:NM, :] += dvu[NA_KW:NA_KW + NM]

        @pl.when(g == 0)
        def _():
            qm_ = q_ref[0:NM, :]
            dom_ = do_ref[0:NM, :]
            lane_m = lax.broadcasted_iota(jnp.int32, (NM, 128), 1)
            km, vm = ku[NA_KW:NA_KW + NM, :], vu[NA_KW:NA_KW + NM, :]
            dqs = []
            dkm = jnp.zeros((NM, 128), F32)
            dvm = jnp.zeros((NM, 128), F32)
            for h in range(2):
                hm = (lane_m < 64) if h == 0 else (lane_m >= 64)
                qh = jnp.where(hm, qm_, 0.0).astype(BF16)
                doh = jnp.where(hm, dom_, 0.0).astype(BF16)
                s = _dot(qh, km, NT) * NA_SCALE
                e = jnp.exp(s - jnp.max(s, axis=-1, keepdims=True))
                p = e / jnp.sum(e, axis=-1, keepdims=True)
                dp = _dot(doh, vm, NT)
                ds = p * (dp - jnp.sum(p * dp, axis=-1, keepdims=True))
                dsb = (ds * NA_SCALE).astype(BF16)
                dqs.append(_dot(dsb, km))
                dkm = dkm + _dot(dsb, qh, TN)
                dvm = dvm + _dot(p.astype(BF16), doh, TN)
            dq_ref[0:NM, :] = jnp.where(lane_m < 64, dqs[0], dqs[1])
            dk_ref[0:NM, :] += dkm
            dv_ref[0:NM, :] += dvm

    col = lambda off: pl.BlockSpec((T, 128), lambda hp, g: (0, off + hp))
    ocol = pl.BlockSpec((T, 128), lambda hp, g: (0, hp))
    bspec = pl.BlockSpec((1, 2, NA_QB, NA_KU), lambda hp, g: (_na_var(g), hp, 0, 0))
    return pl.pallas_call(
        body, name="na_bwd", grid=(4, NA_GROUPS),
        in_specs=[col(0), col(4), col(8), ocol, pl.BlockSpec((1, T, 128), lambda hp, g: (hp, 0, 0)), bspec],
        out_specs=(ocol, ocol, ocol, bspec),
        out_shape=(_sds((T, 512), F32), _sds((T, 512), F32), _sds((T, 512), F32), _sds((3, NA_HEADS, NA_QB, NA_KU), F32)),
        scratch_shapes=[pltpu.VMEM((NA_KU, 128), BF16), pltpu.VMEM((NA_KU, 128), BF16)],
        compiler_params=_cp(("parallel", "arbitrary")))(p_act, p_act, p_act, do, lse, bias_tab)


def _na_rpb_reduce(dbias):
    def body(db_ref, o_ref):
        lane = lax.broadcasted_iota(jnp.int32, (GRID_W, 128), 1)
        row3 = lax.broadcasted_iota(jnp.int32, (15, GRID_W, 128), 1)
        lane3 = lax.broadcasted_iota(jnp.int32, (15, GRID_W, 128), 2)
        accs = []
        for a in range(15):
            acc = jnp.zeros((GRID_W, 128), F32)
            for var in range(3):
                for i in range(4):
                    for j in range(NA_UROWS):
                        if _na_row_offset(var, i, j) == a:
                            pair = db_ref[var, 0, i * 64:(i + 1) * 64, (j // 2) * 128:(j // 2 + 1) * 128]
                            acc = acc + jnp.where((lane < GRID_W) if j % 2 == 0 else (lane >= GRID_W), pair, 0.0)
            accs.append(acc)
        z = jnp.stack(accs)
        z = jnp.where(lane3 < GRID_W, z + jnp.roll(z, GRID_W, axis=2), 0.0)
        for bit in range(6):
            sh = 1 << bit
            z = jnp.where((row3 & sh) != 0, jnp.roll(z, 128 - sh, axis=2), z)
        z = jnp.roll(z, 15, axis=2)
        o_ref[0] = jnp.sum(z, axis=1)

    return pl.pallas_call(
        body, name="na_rpb_reduce", grid=(NA_HEADS,),
        in_specs=[pl.BlockSpec((3, 1, NA_QB, NA_KU), lambda h: (0, h, 0, 0))],
        out_specs=pl.BlockSpec((1, 15, 128), lambda h: (h, 0, 0)), out_shape=_sds((NA_HEADS, 15, 128), F32),
        compiler_params=_cp(("parallel",)))(dbias)


HG_RB = 128
HG_NB = T // HG_RB
HG_SLOTS = HG_NB * 8
HI = lax.Precision.HIGHEST
HG_UNROLL = 4
HG_UNROLL_WIDE = 8


def _chunk_tri(lower):
    r = lax.broadcasted_iota(jnp.int32, (HG_RB, HG_RB), 0)
    c = lax.broadcasted_iota(jnp.int32, (HG_RB, HG_RB), 1)
    same = (r // HG_C) == (c // HG_C)
    keep = (c <= r) if lower else (c >= r)
    return jnp.where(same & keep, 1.0, 0.0).astype(F32)


def _hg_gate_terms(z, lg):
    dl = lg[0:1, :] - lg[1:2, :]
    log_lb = jax.nn.log_sigmoid(dl)
    log_1mlb = jax.nn.log_sigmoid(-dl)
    yz = log_1mlb + jax.nn.log_sigmoid(z)
    log_f = jnp.logaddexp(log_lb, yz)
    snz = jax.nn.sigmoid(-z)
    k = jnp.exp(log_1mlb) * snz
    w2 = jnp.exp(yz - log_f)
    return log_f, k, snz, w2


def _hg_pre(p_act, logits):
    def body(q_ref, zf_ref, zb_ref, lg_ref, qh_ref, kf_ref, bf_ref, kb_ref, bb_ref):
        qh_ref[...] = jax.nn.silu(q_ref[...])
        lf, kf, _, _ = _hg_gate_terms(zf_ref[...], lg_ref[0])
        kf_ref[...] = kf
        bf_ref[...] = jnp.dot(_chunk_tri(True), lf, precision=HI, preferred_element_type=F32)
        lb_, kb, _, _ = _hg_gate_terms(zb_ref[...], lg_ref[1])
        kb_ref[...] = kb
        bb_ref[...] = jnp.dot(_chunk_tri(False), lb_, precision=HI, preferred_element_type=F32)

    blk = lambda c: pl.BlockSpec((HG_RB, 512), lambda i: (i, c))
    ob = pl.BlockSpec((HG_RB, 512), lambda i: (i, 0))
    return pl.pallas_call(
        body, name="hg_pre", grid=(HG_NB,),
        in_specs=[blk(3), blk(4), blk(5), pl.BlockSpec((2, 2, 512), lambda i: (0, 0, 0))],
        out_specs=(ob,) * 5, out_shape=(_sds((T, 512), F32),) * 5,
        compiler_params=_cp(("parallel",)))(p_act, p_act, p_act, logits)


def _bdot(a, b, ca, cb):
    return lax.dot_general(a.astype(BF16), b.astype(BF16), (((ca,), (cb,)), ((0,), (0,))), preferred_element_type=F32)


HG_S = 8
HG_NS = HG_RB // HG_S


def _lane_sums(xs):
    l_io = lax.broadcasted_iota(jnp.int32, (HG_NS, HG_S, HG_S), 2)
    a = jnp.zeros((HG_NS, HG_S, HG_S), F32)
    for j, x in enumerate(xs):
        a = a + jnp.where(l_io == j, jnp.sum(x, axis=-1, keepdims=True), 0.0)
    return a


def _halves(x):
    y = x.reshape(8, 2, HG_S, x.shape[-1])
    return y[:, 0], y[:, 1]


def _join(first, second):
    return jnp.stack([first, second], axis=1).reshape(HG_RB, first.shape[-1])


def _cross_split(rev, b4):
    b_1, b_2 = _halves(b4)
    if rev:
        r = b_2[:, 0:1, :]
        return jnp.exp(b_1 - r), jnp.exp(r - b_2)
    r = b_1[:, HG_S - 1:HG_S, :]
    return jnp.exp(b_2 - r), jnp.exp(r - b_1)


def _hg_scan_fwd(qh, k, b, p_act, rev):
    anchor = 0 if rev else HG_C - 1

    def body(q_ref, k_ref, b_ref, v_ref, o_ref, st_ref, dsc):
        def phase_a(blk, _):
            rows = pl.ds(pl.multiple_of(blk * HG_RB, HG_RB), HG_RB)
            b3 = b_ref[rows, :].reshape(8, HG_C, 128)
            k3 = k_ref[rows, :].reshape(8, HG_C, 128)
            v3 = v_ref[rows, :].reshape(8, HG_C, 128)
            bl = b3[:, anchor:anchor + 1, :]
            kt = k3 * jnp.exp(bl - b3)
            st_ref[0, pl.ds(pl.multiple_of(blk * 8, 8), 8)] = _bdot(v3, kt, 1, 1)
            dsc[pl.ds(pl.multiple_of(blk * 8, 8), 8), :] = jnp.exp(bl[:, 0, :])
            return 0

        lax.fori_loop(0, HG_NB, phase_a, 0, unroll=HG_UNROLL_WIDE)

        def phase_b(n, carry):
            c = (NCHUNK - 1 - n) if rev else n
            u = st_ref[0, c]
            st_ref[0, c] = carry
            return carry * dsc[pl.ds(c, 1), :] + u

        lax.fori_loop(0, NCHUNK // 3, lambda n3, s: phase_b(3 * n3 + 2, phase_b(3 * n3 + 1, phase_b(3 * n3, s))),
                      jnp.zeros((128, 128), F32))
        for c in range(NCHUNK, HG_SLOTS):
            st_ref[0, c] = jnp.zeros((128, 128), F32)

        t_io = lax.broadcasted_iota(jnp.int32, (HG_NS, HG_S, 128), 1)

        def phase_c(blk, _):
            rows = pl.ds(pl.multiple_of(blk * HG_RB, HG_RB), HG_RB)
            b4 = b_ref[rows, :].reshape(HG_NS, HG_S, 128)
            k4 = k_ref[rows, :].reshape(HG_NS, HG_S, 128)
            q4 = q_ref[rows, :].reshape(HG_NS, HG_S, 128)
            v4 = v_ref[rows, :].reshape(HG_NS, HG_S, 128)
            st = st_ref[0, pl.ds(pl.multiple_of(blk * 8, 8), 8)]
            o = _bdot((q4 * jnp.exp(b4)).reshape(8, HG_C, 128), st, 2, 2).reshape(HG_RB, 128)
            terms = []
            for s in range(HG_S):
                ok = (t_io <= s) if rev else (t_io >= s)
                f = jnp.exp(jnp.where(ok, b4 - b4[:, s:s + 1, :], NEG))
                terms.append(q4 * f * k4[:, s:s + 1, :])
            o_in = _bdot(_lane_sums(terms), v4, 2, 1)
            wq, wk = _cross_split(rev, b4)
            q_1, q_2 = _halves(q4)
            k_1, k_2 = _halves(k4)
            v_1, v_2 = _halves(v4)
            o_1, o_2 = _halves(o_in)
            if rev:
                o_1 = o_1 + _bdot(_bdot(q_1 * wq, k_2 * wk, 2, 2), v_2, 2, 1)
            else:
                o_2 = o_2 + _bdot(_bdot(q_2 * wq, k_1 * wk, 2, 2), v_1, 2, 1)
            o_ref[rows, :] = o + _join(o_1, o_2)
            return 0

        lax.fori_loop(0, HG_NB, phase_c, 0, unroll=HG_UNROLL_WIDE)

    col = pl.BlockSpec((T, 128), lambda h: (0, h))
    return pl.pallas_call(
        body, name="hg_scan_bwd_dir" if rev else "hg_scan_fwd_dir", grid=(HG_HEADS,),
        in_specs=[col, col, col, pl.BlockSpec((T, 128), lambda h: (0, 24 + h))],
        out_specs=(col, pl.BlockSpec((1, HG_SLOTS, 128, 128), lambda h: (h, 0, 0, 0))),
        out_shape=(_sds((T, 512), F32), _sds((HG_HEADS, HG_SLOTS, 128, 128), F32)),
        scratch_shapes=[pltpu.VMEM((HG_SLOTS, 128), F32)],
        compiler_params=_cp(("parallel",), 56))(qh, k, b, p_act)


def _hg_scan_bwd(qh, k, b, p_act, st, do, rev):
    anchor = 0 if rev else HG_C - 1

    def body(q_ref, k_ref, b_ref, v_ref, st_ref, do_ref, dq_ref, dk_ref, db_ref, dv_ref, gst, dsc, dbl):
        def phase_a(blk, _):
            rows = pl.ds(pl.multiple_of(blk * HG_RB, HG_RB), HG_RB)
            b3 = b_ref[rows, :].reshape(8, HG_C, 128)
            q3 = q_ref[rows, :].reshape(8, HG_C, 128)
            do3 = do_ref[rows, :].reshape(8, HG_C, 128)
            gst[pl.ds(pl.multiple_of(blk * 8, 8), 8)] = _bdot(do3, q3 * jnp.exp(b3), 1, 1)
            dsc[pl.ds(pl.multiple_of(blk * 8, 8), 8), :] = jnp.exp(b3[:, anchor, :])
            return 0

        lax.fori_loop(0, HG_NB, phase_a, 0, unroll=HG_UNROLL_WIDE)

        def phase_b(n, carry):
            c = n if rev else (NCHUNK - 1 - n)
            w = gst[c]
            gst[c] = carry
            dcv = dsc[pl.ds(c, 1), :]
            dbl[pl.ds(c, 1), :] = dcv * jnp.sum(st_ref[0, c] * carry, axis=0, keepdims=True)
            return carry * dcv + w

        lax.fori_loop(0, NCHUNK // 3, lambda n3, s: phase_b(3 * n3 + 2, phase_b(3 * n3 + 1, phase_b(3 * n3, s))),
                      jnp.zeros((128, 128), F32))
        for c in range(NCHUNK, HG_SLOTS):
            gst[c] = jnp.zeros((128, 128), F32)
            dbl[c:c + 1, :] = jnp.zeros((1, 128), F32)

        t_io = lax.broadcasted_iota(jnp.int32, (HG_NS, HG_S, 128), 1)
        t16 = lax.broadcasted_iota(jnp.int32, (8, HG_C, 128), 1)
        r_io = lax.broadcasted_iota(jnp.int32, (HG_NS, HG_S, HG_S), 1)
        l_io = lax.broadcasted_iota(jnp.int32, (HG_NS, HG_S, HG_S), 2)

        def phase_c(blk, _):
            rows = pl.ds(pl.multiple_of(blk * HG_RB, HG_RB), HG_RB)
            cs = pl.ds(pl.multiple_of(blk * 8, 8), 8)
            b4 = b_ref[rows, :].reshape(HG_NS, HG_S, 128)
            k4 = k_ref[rows, :].reshape(HG_NS, HG_S, 128)
            q4 = q_ref[rows, :].reshape(HG_NS, HG_S, 128)
            v4 = v_ref[rows, :].reshape(HG_NS, HG_S, 128)
            do4 = do_ref[rows, :].reshape(HG_NS, HG_S, 128)
            b3, k3, q3 = (z.reshape(8, HG_C, 128) for z in (b4, k4, q4))
            v3, do3 = v4.reshape(8, HG_C, 128), do4.reshape(8, HG_C, 128)
            s_t = st_ref[0, cs]
            g_t = gst[cs]
            bl = b3[:, anchor:anchor + 1, :]
            ekl = jnp.exp(bl - b3)
            kt = k3 * ekl
            dkt = _bdot(v3, g_t, 2, 1)
            dq = (_bdot(do3, s_t, 2, 1) * jnp.exp(b3)).reshape(HG_NS, HG_S, 128)
            dk = (dkt * ekl).reshape(HG_NS, HG_S, 128)
            dv = _bdot(kt, g_t, 2, 2).reshape(HG_NS, HG_S, 128)
            dbl3 = dbl[cs, :].reshape(8, 1, 128) + jnp.sum(dkt * kt, axis=1, keepdims=True)
            causal = (l_io >= r_io) if rev else (l_io <= r_io)
            da = jnp.where(causal, _bdot(do4, v4, 2, 2), 0.0)
            causal_t = (l_io <= r_io) if rev else (l_io >= r_io)
            dat = jnp.where(causal_t, _bdot(v4, do4, 2, 2), 0.0)
            for s in range(HG_S):
                ok = (t_io <= s) if rev else (t_io >= s)
                f = jnp.exp(jnp.where(ok, b4 - b4[:, s:s + 1, :], NEG))
                dq = dq + da[:, :, s:s + 1] * (f * k4[:, s:s + 1, :])
            terms = []
            for t in range(HG_S):
                ok = (t_io >= t) if rev else (t_io <= t)
                e = jnp.exp(jnp.where(ok, b4[:, t:t + 1, :] - b4, NEG))
                eq = e * q4[:, t:t + 1, :]
                dk = dk + dat[:, :, t:t + 1] * eq
                terms.append(eq * k4)
            dv = dv + _bdot(_lane_sums(terms), do4, 2, 1)
            wq, wk = _cross_split(rev, b4)
            pick = (lambda z: _halves(z)) if rev else (lambda z: _halves(z)[::-1])
            (q_q, _), (_, k_k), (_, v_k), (do_q, _) = pick(q4), pick(k4), pick(v4), pick(do4)
            qx, kx = q_q * wq, k_k * wk
            dq_q = _bdot(_bdot(do_q, v_k, 2, 2), kx, 2, 1) * wq
            dk_k = _bdot(_bdot(v_k, do_q, 2, 2), qx, 2, 1) * wk
            dv_k = _bdot(_bdot(kx, qx, 2, 2), do_q, 2, 1)
            zero = jnp.zeros((8, HG_S, 128), F32)
            place_q = (lambda z: _join(z, zero)) if rev else (lambda z: _join(zero, z))
            place_k = (lambda z: _join(zero, z)) if rev else (lambda z: _join(z, zero))
            dq2 = dq.reshape(HG_RB, 128) + place_q(dq_q)
            dk2 = dk.reshape(HG_RB, 128) + place_k(dk_k)
            dv2 = dv.reshape(HG_RB, 128) + place_k(dv_k)
            dq3, dk3 = dq2.reshape(8, HG_C, 128), dk2.reshape(8, HG_C, 128)
            db = q3 * dq3 - k3 * dk3 + jnp.where(t16 == anchor, dbl3, 0.0)
            dq_ref[rows, :] = dq2
            dk_ref[rows, :] = dk2
            db_ref[rows, :] = db.reshape(HG_RB, 128)
            dv_ref[rows, :] = dv2
            return 0

        lax.fori_loop(0, HG_NB, phase_c, 0, unroll=HG_UNROLL)

    col = pl.BlockSpec((T, 128), lambda h: (0, h))
    return pl.pallas_call(
        body, name="hg_scan_bwd_dir_bwd" if rev else "hg_scan_fwd_dir_bwd", grid=(HG_HEADS,),
        in_specs=[col, col, col, pl.BlockSpec((T, 128), lambda h: (0, 24 + h)),
                  pl.BlockSpec((1, HG_SLOTS, 128, 128), lambda h: (h, 0, 0, 0)), col],
        out_specs=(col,) * 4, out_shape=(_sds((T, 512), F32),) * 4,
        scratch_shapes=[pltpu.VMEM((HG_SLOTS, 128, 128), F32), pltpu.VMEM((HG_SLOTS, 128), F32),
                        pltpu.VMEM((HG_SLOTS, 128), F32)],
        compiler_params=_cp(("parallel",), 56))(qh, k, b, p_act, st, do)


def _row_valid(i, tm):
    r = lax.broadcasted_iota(jnp.int32, (tm, 1), 0) + i * tm
    return r < L


def _hg_post_rows(o, gv, gain_v, valid):
    parts = []
    for h in range(HG_HEADS):
        oh = o[:, 128 * h:128 * (h + 1)]
        parts.append(oh * lax.rsqrt(jnp.mean(oh * oh, axis=-1, keepdims=True) + EPS))
    return jnp.where(valid, jnp.concatenate(parts, axis=1) * gain_v * jax.nn.silu(gv), 0.0)


def _hg_post_bwd_rows(du, o, gv, gain_v, valid):
    duv = jnp.where(valid, du, 0.0)
    sig = jax.nn.sigmoid(gv)
    sg = gv * sig
    dn = duv * gain_v * sg
    do_parts, n_parts = [], []
    for h in range(HG_HEADS):
        sl = slice(128 * h, 128 * (h + 1))
        oh = o[:, sl]
        r = lax.rsqrt(jnp.mean(oh * oh, axis=-1, keepdims=True) + EPS)
        nh = oh * r
        dnh = dn[:, sl]
        do_parts.append(r * (dnh - nh * jnp.mean(dnh * nh, axis=-1, keepdims=True)))
        n_parts.append(nh)
    n = jnp.where(valid, jnp.concatenate(n_parts, axis=1), 0.0)
    do = jnp.where(valid, jnp.concatenate(do_parts, axis=1), 0.0)
    dg = duv * n * gain_v * (sig * (1.0 + gv * (1.0 - sig)))
    return do, dg, jnp.sum(duv * n * sg, axis=0, keepdims=True)


def _hg_pre_bwd(p_act, logits, dq_f, dq_b, dk_f, dk_b, db_f, db_b, dv_f, dv_b, dp_rest):
    def body(q_ref, zf_ref, zb_ref, lg_ref, dqf_ref, dqb_ref, dkf_ref, dkb_ref, dbf_ref, dbb_ref, dvf_ref, dvb_ref, _,
             dp_ref, dlg_ref):
        dq_ref, dzf_ref, dzb_ref, di_ref = (dp_ref.at[:, 512 * c:512 * (c + 1)] for c in range(4))
        i = pl.program_id(0)
        valid = _row_valid(i, HG_RB)
        qv = q_ref[...]
        sig = jax.nn.sigmoid(qv)
        dq_ref[...] = jnp.where(valid, (dqf_ref[...] + dqb_ref[...]) * (sig * (1.0 + qv * (1.0 - sig))), 0.0).astype(BF16)
        di_ref[...] = jnp.where(valid, dvf_ref[...] + dvb_ref[...], 0.0).astype(BF16)
        for d, (z_ref, dk_r, db_r, dz_ref) in enumerate(((zf_ref, dkf_ref, dbf_ref, dzf_ref), (zb_ref, dkb_ref, dbb_ref, dzb_ref))):
            lg = lg_ref[d]
            dl = lg[0:1, :] - lg[1:2, :]
            lb = jax.nn.sigmoid(dl)
            one_m_lb = jax.nn.sigmoid(-dl)
            log_f, _, snz, w2 = _hg_gate_terms(z_ref[...], lg)
            dbv = jnp.where(valid, db_r[...], 0.0)
            dkv = jnp.where(valid, dk_r[...], 0.0)
            dlf = jnp.dot(_chunk_tri(d == 1), dbv, precision=HI, preferred_element_type=F32)
            sz = 1.0 - snz
            dz_ref[...] = (dlf * w2 * snz - dkv * one_m_lb * sz * snz).astype(BF16)
            dlb = jnp.sum(dlf * snz * jnp.exp(-log_f) - dkv * snz, axis=0, keepdims=True)
            dl0 = dlb * lb * one_m_lb
            part = jnp.concatenate([dl0, -dl0], axis=0)

            @pl.when(i == 0)
            def _():
                dlg_ref[d] = part

            @pl.when(i > 0)
            def _():
                dlg_ref[d] += part

    blk = lambda c: pl.BlockSpec((HG_RB, 512), lambda i: (i, c))
    ob = pl.BlockSpec((HG_RB, 512), lambda i: (i, 0))
    lgs = pl.BlockSpec((2, 2, 512), lambda i: (0, 0, 0))
    return pl.pallas_call(
        body, name="hg_pre_bwd", grid=(HG_NB,),
        in_specs=[blk(3), blk(4), blk(5), lgs] + [ob] * 8 + [ANY],
        out_specs=(pl.BlockSpec((HG_RB, 2048), lambda i: (i, 0)), lgs),
        out_shape=(_sds(dp_rest.shape, BF16), _sds((2, 2, 512), F32)), input_output_aliases={12: 0},
        compiler_params=_cp(("arbitrary",)))(p_act, p_act, p_act, logits, dq_f, dq_b, dk_f, dk_b, db_f, db_b, dv_f, dv_b,
                                             dp_rest)


def _mix_fwd(o_na, o_f, o_b, gain, w_na, w_hg, p_act):
    def body(ona_ref, of_ref, ob_ref, g_ref, gain_ref, wna_ref, whg_ref, gna_ref, ghg_ref, o_ref, u_ref):
        u = _hg_post_rows(of_ref[...] + ob_ref[...], g_ref[...], gain_ref[...], _row_valid(pl.program_id(0), TM_B)).astype(BF16)
        u_ref[...] = u
        y_na = _dot(ona_ref[...], wna_ref[...])
        y_hg = _dot(u, whg_ref[...])
        o_ref[...] = (jax.nn.sigmoid(gna_ref[...]) * y_na + jax.nn.sigmoid(ghg_ref[...]) * y_hg).astype(BF16)

    act = pl.BlockSpec((TM_B, 512), lambda i: (i, 0))
    wsp = pl.BlockSpec((512, D), lambda i: (0, 0))
    return pl.pallas_call(
        body, name="mix_fwd", grid=(T // TM_B,),
        in_specs=[act, act, act, pl.BlockSpec((TM_B, 512), lambda i: (i, 7)), pl.BlockSpec((1, 512), lambda i: (0, 0)),
                  wsp, wsp, pl.BlockSpec((TM_B, D), lambda i: (i, 4)), pl.BlockSpec((TM_B, D), lambda i: (i, 5))],
        out_specs=(pl.BlockSpec((TM_B, D), lambda i: (i, 0)), act), out_shape=(_sds((T, D), BF16), _sds((T, 512), BF16)),
        compiler_params=_cp(("parallel",)))(o_na, o_f, o_b, p_act, gain, w_na, w_hg, p_act, p_act)


DP_REST = IN_COLS - 1536


def _mix_bwd(o_na, u_hg, o_f, o_b, gain, w_na, w_hg, p_act, dmix):
    ni = T // TM_B

    def body(ona_ref, uhg_ref, of_ref, ob_ref, g_ref, gain_ref, wna_ref, whg_ref, gna_ref, ghg_ref, dmix_ref,
             dp_ref, dwna_ref, dwhg_ref, dona_ref, do_ref, dgain_ref, acc_na, acc_hg):
        i = pl.program_id(0)
        dg_ref, dgna_ref, dghg_ref = dp_ref.at[:, 2048:2560], dp_ref.at[:, 2560:3584], dp_ref.at[:, 3584:4608]
        dm = dmix_ref[...].astype(F32)
        dxs = []
        for x_ref, w_ref, gt_ref, dgt_ref, dw_ref, acc in (
                (ona_ref, wna_ref, gna_ref, dgna_ref, dwna_ref, acc_na), (uhg_ref, whg_ref, ghg_ref, dghg_ref, dwhg_ref, acc_hg)):
            xv = x_ref[...]
            y = _dot(xv, w_ref[...])
            sg = jax.nn.sigmoid(gt_ref[...])
            dgt_ref[...] = (dm * y * sg * (1.0 - sg)).astype(BF16)
            dy = (dm * sg).astype(BF16)
            dxs.append(_dot(dy, w_ref[...], NT))
            part = _dot(xv, dy, TN)

            @pl.when(i == 0)
            def _():
                acc[...] = part

            @pl.when(i > 0)
            def _():
                acc[...] += part

            @pl.when(i == ni - 1)
            def _():
                dw_ref[...] = acc[...].astype(BF16)

        dona_ref[...] = dxs[0]
        do, dg, gpart = _hg_post_bwd_rows(dxs[1], of_ref[...] + ob_ref[...], g_ref[...], gain_ref[...], _row_valid(i, TM_B))
        do_ref[...] = do
        dg_ref[...] = dg.astype(BF16)

        @pl.when(i == 0)
        def _():
            dgain_ref[...] = gpart

        @pl.when(i > 0)
        def _():
            dgain_ref[...] += gpart

    act = pl.BlockSpec((TM_B, 512), lambda i: (i, 0))
    wsp = pl.BlockSpec((512, D), lambda i: (0, 0))
    rblk = pl.BlockSpec((TM_B, D), lambda i: (i, 0))
    vec = pl.BlockSpec((1, 512), lambda i: (0, 0))
    return pl.pallas_call(
        body, name="mix_bwd", grid=(ni,),
        in_specs=[act, act, act, act, pl.BlockSpec((TM_B, 512), lambda i: (i, 7)), vec, wsp, wsp,
                  pl.BlockSpec((TM_B, D), lambda i: (i, 4)), pl.BlockSpec((TM_B, D), lambda i: (i, 5)), rblk],
        out_specs=(pl.BlockSpec((TM_B, DP_REST), lambda i: (i, 0)), wsp, wsp, act, act, vec),
        out_shape=(_sds((T, DP_REST), BF16), _sds((512, D), BF16), _sds((512, D), BF16),
                   _sds((T, 512), F32), _sds((T, 512), F32), _sds((1, 512), F32)),
        scratch_shapes=[pltpu.VMEM((512, D), F32), pltpu.VMEM((512, D), F32)],
        compiler_params=_cp(("arbitrary",)))(o_na, u_hg, o_f, o_b, p_act, gain, w_na, w_hg, p_act, p_act, dmix)


def _wo_fwd(mix, w_o, h0, g_mlp):
    def body(mix_ref, w_ref, h0_ref, g_ref, h1_ref, m_ref):
        h1 = h0_ref[...] + _dot(mix_ref[...], w_ref[...])
        h1_ref[...] = h1
        r = lax.rsqrt(jnp.mean(h1 * h1, axis=-1, keepdims=True) + EPS)
        m_ref[...] = (h1 * r * g_ref[...]).astype(BF16)

    blk = pl.BlockSpec((TM_B, D), lambda i: (i, 0))
    return pl.pallas_call(
        body, name="wo_fwd", grid=(T // TM_B,),
        in_specs=[blk, pl.BlockSpec((D, D), lambda i: (0, 0)), blk, pl.BlockSpec((1, D), lambda i: (0, 0))],
        out_specs=(blk, blk), out_shape=(_sds((T, D), F32), _sds((T, D), BF16)),
        compiler_params=_cp(("parallel",)))(mix, w_o, h0, g_mlp)


def _wo_bwd(dh1_b, w_o, mix):
    ni = T // TM_B

    def body(dh_ref, w_ref, mix_ref, dmix_ref, dw_ref, acc):
        i = pl.program_id(0)
        dh = dh_ref[...]
        dmix_ref[...] = _dot(dh, w_ref[...], NT).astype(BF16)
        part = _dot(mix_ref[...], dh, TN)

        @pl.when(i == 0)
        def _():
            acc[...] = part

        @pl.when(i > 0)
        def _():
            acc[...] += part

        @pl.when(i == ni - 1)
        def _():
            dw_ref[...] = acc[...].astype(BF16)

    blk = pl.BlockSpec((TM_B, D), lambda i: (i, 0))
    wsp = pl.BlockSpec((D, D), lambda i: (0, 0))
    return pl.pallas_call(
        body, name="wo_bwd", grid=(ni,), in_specs=[blk, wsp, blk], out_specs=(blk, wsp),
        out_shape=(_sds((T, D), BF16), _sds((D, D), BF16)), scratch_shapes=[pltpu.VMEM((D, D), F32)],
        compiler_params=_cp(("arbitrary",)))(dh1_b, w_o, mix)


FF_B = D_FF // NDEV


def _loss_rows(xv, gv, tv, row0):
    r_io = lax.broadcasted_iota(jnp.int32, (xv.shape[0], 1), 0) + row0
    valid = (r_io >= NM) & (r_io < L)
    r = lax.rsqrt(jnp.mean(xv * xv, axis=-1, keepdims=True) + EPS)
    xh = xv * r
    err = jnp.where(valid, xh * gv - tv, 0.0)
    lpart = 0.5 * jnp.sum(jnp.sum(err * err, axis=-1, keepdims=True) * (1.0 / D), axis=0, keepdims=True)
    dy = err * (1.0 / D)
    dxh = dy * gv
    dh = r * (dxh - xh * jnp.mean(dxh * xh, axis=-1, keepdims=True))
    return lpart, dh, jnp.sum(dy * xh, axis=0, keepdims=True)


def _mlp_fwd_loss(m, wup_g, wdown_g, h1, g_final, tgt):
    nsub = TM_MM // TM_E

    def body(m_ref, wu_ref, wd_ref, h1_ref, g_ref, t_ref, loss_ref, dh_ref, dhb_ref, dg_ref, h2):
        i, j = pl.program_id(0), pl.program_id(1)
        up = jnp.maximum(_dot(m_ref[...], wu_ref[0]), 0.0)
        part = _dot((up * up).astype(BF16), wd_ref[0])

        @pl.when(j == 0)
        def _():
            h2[...] = h1_ref[...] + part

        @pl.when(j > 0)
        def _():
            h2[...] += part

        @pl.when(j == NDEV - 1)
        def _():
            lsum = jnp.zeros((1, 1), F32)
            gsum = jnp.zeros((1, D), F32)
            for s in range(nsub):
                rows = slice(s * TM_E, (s + 1) * TM_E)
                lpart, dh, gpart = _loss_rows(h2[rows, :], g_ref[...], t_ref[rows, :], i * TM_MM + s * TM_E)
                dh_ref[rows, :] = dh
                dhb_ref[rows, :] = dh.astype(BF16)
                lsum = lsum + lpart
                gsum = gsum + gpart
            lsum = jnp.broadcast_to(lsum, (1, 128))

            @pl.when(i == 0)
            def _():
                loss_ref[...] = lsum
                dg_ref[...] = gsum

            @pl.when(i > 0)
            def _():
                loss_ref[...] += lsum
                dg_ref[...] += gsum

    blk = pl.BlockSpec((TM_MM, D), lambda i, j: (i, 0))
    vec = pl.BlockSpec((1, D), lambda i, j: (0, 0))
    return pl.pallas_call(
        body, name="mlp_fwd_loss", grid=(T // TM_MM, NDEV),
        in_specs=[blk, pl.BlockSpec((1, D, FF_B), lambda i, j: (j, 0, 0)), pl.BlockSpec((1, FF_B, D), lambda i, j: (j, 0, 0)),
                  blk, vec, blk],
        out_specs=(pl.BlockSpec((1, 128), lambda i, j: (0, 0)), blk, blk, vec),
        out_shape=(_sds((1, 128), F32), _sds((T, D), F32), _sds((T, D), BF16), _sds((1, D), F32)),
        scratch_shapes=[pltpu.VMEM((TM_MM, D), F32)],
        compiler_params=_cp(("arbitrary", "arbitrary"), 56))(m, wup_g, wdown_g, h1, g_final, tgt)


def _mlp_bwd(m, dh2_b, wup_g, wdown_g, h1, g_mlp, dh2):
    ni = T // TM_B
    nsub = TM_B // TM_E

    def body(m_ref, dh_ref, wu_ref, wd_ref, h1_ref, g_ref, dres_ref, dwu_ref, dwd_ref, dh1_ref, dh1b_ref, dg_ref,
             dm_ref, acc_u, acc_d):
        j, i = pl.program_id(0), pl.program_id(1)
        rows = pl.ds(pl.multiple_of(i * TM_B, TM_B), TM_B)
        mv, dh = m_ref[...], dh_ref[...]
        r = jnp.maximum(_dot(mv, wu_ref[0]), 0.0)
        act = (r * r).astype(BF16)
        dact = _dot(dh, wd_ref[0], NT)
        dup = (dact * (2.0 * r)).astype(BF16)
        pd = _dot(act, dh, TN)
        pu = _dot(mv, dup, TN)
        dmv = _dot(dup, wu_ref[0], NT)

        @pl.when(i == 0)
        def _():
            acc_u[...] = pu
            acc_d[...] = pd

        @pl.when(i > 0)
        def _():
            acc_u[...] += pu
            acc_d[...] += pd

        @pl.when(i == ni - 1)
        def _():
            dwu_ref[0] = acc_u[...].astype(BF16)
            dwd_ref[0] = acc_d[...].astype(BF16)

        @pl.when(j == 0)
        def _():
            dm_ref[rows, :] = dmv

        @pl.when(j > 0)
        def _():
            dm_ref[rows, :] += dmv

        @pl.when(j == NDEV - 1)
        def _():
            gsum = jnp.zeros((1, D), F32)
            for s in range(nsub):
                sub = slice(s * TM_E, (s + 1) * TM_E)
                dm_rows = dm_ref[pl.ds(pl.multiple_of(i * TM_B + s * TM_E, TM_E), TM_E), :]
                dx, gpart = _norm_bwd_rows(h1_ref[sub, :], g_ref[...], dm_rows, dres_ref[sub, :])
                dh1_ref[sub, :] = dx
                dh1b_ref[sub, :] = dx.astype(BF16)
                gsum = gsum + gpart

            @pl.when(i == 0)
            def _():
                dg_ref[...] = gsum

            @pl.when(i > 0)
            def _():
                dg_ref[...] += gsum

    blk = pl.BlockSpec((TM_B, D), lambda j, i: (i, 0))
    late = pl.BlockSpec((TM_B, D), lambda j, i: (jnp.where(j == NDEV - 1, i, 0), 0))
    vec = pl.BlockSpec((1, D), lambda j, i: (0, 0))
    wus = pl.BlockSpec((1, D, FF_B), lambda j, i: (j, 0, 0))
    wds = pl.BlockSpec((1, FF_B, D), lambda j, i: (j, 0, 0))
    return pl.pallas_call(
        body, name="mlp_bwd", grid=(NDEV, ni), in_specs=[blk, blk, wus, wds, late, vec, late],
        out_specs=(wus, wds, late, late, vec),
        out_shape=(_sds((NDEV, D, FF_B), BF16), _sds((NDEV, FF_B, D), BF16), _sds((T, D), F32), _sds((T, D), BF16),
                   _sds((1, D), F32)),
        scratch_shapes=[pltpu.VMEM((T, D), F32), pltpu.VMEM((D, FF_B), F32), pltpu.VMEM((FF_B, D), F32)],
        compiler_params=_cp(("arbitrary", "arbitrary"), 56))(m, dh2_b, wup_g, wdown_g, h1, g_mlp, dh2)


def _adamw(parts, w, m, v, name):
    rr, cc = w.shape
    nslot = parts.shape[0]
    tr = rr
    for cand in (256, 128, 64):
        if rr % cand == 0 and rr > cand:
            tr = cand
            break
    c1 = 1.0 - ADAM_B1 ** ADAM_STEP
    c2 = 1.0 - ADAM_B2 ** ADAM_STEP

    def body(p_ref, w_ref, m_ref, v_ref, g_ref, d_ref, nm_ref, nv_ref):
        g = p_ref[0].astype(F32)
        for s in range(1, nslot):
            g = g + p_ref[s].astype(F32)
        mn = ADAM_B1 * m_ref[...] + (1.0 - ADAM_B1) * g
        vn = ADAM_B2 * v_ref[...] + (1.0 - ADAM_B2) * (g * g)
        g_ref[...] = g
        nm_ref[...] = mn
        nv_ref[...] = vn
        d_ref[...] = -ADAM_LR * ((mn / c1) / (jnp.sqrt(vn / c2) + ADAM_EPS) + ADAM_WD * w_ref[...])

    blk = pl.BlockSpec((tr, cc), lambda i: (i, 0))
    return pl.pallas_call(
        body, name=name, grid=(rr // tr,),
        in_specs=[pl.BlockSpec((nslot, tr, cc), lambda i: (0, i, 0)), blk, blk, blk],
        out_specs=(blk,) * 4, out_shape=(_sds((rr, cc), F32),) * 4,
        compiler_params=_cp(("parallel",)))(parts, w, m, v)


RPB_N = NA_HEADS * 15 * 31
RPB_PAD = 4096
OWN_ROWS = NM + 8


def _pad_rows(a, rows):
    return jnp.pad(a, ((0, rows - a.shape[0]),) + ((0, 0),) * (a.ndim - 1))


def _pack_owned(meta_blk, lb_blk):
    return jnp.concatenate([meta_blk, _pad_rows(lb_blk.reshape(2, 128), 8)], axis=0)


LOSS_ROW = 28


def _pack_replicated(n_mix, n_mlp, n_final, hg_gain, rpb, loss_row=None):
    flat = _pad_rows(rpb.reshape(RPB_N), RPB_PAD)
    gain8 = _pad_rows(hg_gain.reshape(4, 128), 8)
    if loss_row is not None:
        gain8 = gain8 + jnp.pad(loss_row, ((LOSS_ROW - 24, 31 - LOSS_ROW), (0, 0)))
    return jnp.concatenate([n_mix.reshape(8, 128), n_mlp.reshape(8, 128), n_final.reshape(8, 128), gain8,
                            flat.reshape(32, 128)], axis=0)


def _unpack_replicated(a):
    return (a[0:8].reshape(1, D), a[8:16].reshape(1, D), a[16:24].reshape(D), a[24:28].reshape(1, 512),
            a[32:64].reshape(RPB_PAD)[:RPB_N].reshape(1, NA_HEADS, 15, 31))


def kernel(x, meta_tokens, w_in, w_na_out, w_hg_out, w_o, w_up, w_down, norm_mix, norm_mlp, norm_final, hg_norm, na_rpb, hg_lb_logits, loss_target, m_meta_tokens, m_w_in, m_w_na_out, m_w_hg_out, m_w_o, m_w_up, m_w_down, m_norm_mix, m_norm_mlp, m_norm_final, m_hg_norm, m_na_rpb, m_hg_lb_logits, v_meta_tokens, v_w_in, v_w_na_out, v_w_hg_out, v_w_o, v_w_up, v_w_down, v_norm_mix, v_norm_mlp, v_norm_final, v_hg_norm, v_na_rpb, v_hg_lb_logits):
    owned = _pack_owned(meta_tokens, hg_lb_logits)
    first_masks = (ALL_PEERS, SAME_CORE_AND_SIBLING)
    first, tok = _exchange_start([owned, w_in[0].astype(BF16)], [False] * 2, "gather_first_start", first_masks)
    bias_tab = _na_bias_table(_tie(jnp.pad(na_rpb[0], ((0, 0), (0, 0), (0, 128 - 31))), tok, "tie_bias_table"))
    later = [w[0].astype(BF16) for w in (w_na_out, w_hg_out, w_o, w_up, w_down)]
    lead = jnp.zeros((NM, D), F32) + tok[0, 0]
    h0_rows = jnp.concatenate([lead, x[0], jnp.zeros((T - L, D), F32)], axis=0)
    tgt = jnp.concatenate([lead, loss_target[0], jnp.zeros((T - L, D), F32)], axis=0)
    (owned_g, _), first = _exchange_wait(first, [False] * 2, [h0_rows], "gather_small_wait", first_masks, which=(0,))
    meta_full = jnp.transpose(owned_g[:, 0:NM, :], (1, 0, 2)).reshape(NM, D)
    logits = jnp.transpose(owned_g[:, NM:NM + 2, :].reshape(NDEV, 2, 2, 64), (1, 2, 0, 3)).reshape(2, 2, 512)
    h0 = lax.dynamic_update_slice(h0_rows, meta_full, (0, 0))
    a, a_t = _norm_fwd_t(h0, norm_mix, "norm_mix_fwd")
    (_, win_l), _ = _exchange_wait(first, [False] * 2, [a, logits, tgt, bias_tab] + later, "gather_first_wait", first_masks,
                                   which=(1,))
    (win_g,) = _forward_to_sibling([win_l], "gather_first_forward")
    later[0] = _tie(later[0], win_g, "tie_gather_rest")
    gather_rest, tok = _exchange_start(later, [False] * 5, "gather_rest_start")
    win_g = _tie(win_g, tok, "tie_inproj")

    p_act = _inproj_fwd(a, win_g)
    o_na, lse = _na_fwd(p_act, bias_tab)
    qh, k_f, b_f, k_b, b_b = _hg_pre(p_act, logits)
    o_f, st_f = _hg_scan_fwd(qh, k_f, b_f, p_act, False)
    o_b, st_b = _hg_scan_fwd(qh, k_b, b_b, p_act, True)
    (wna_g, whg_g, wo_g, _, _), gather_rest = _exchange_wait(
        gather_rest, [False] * 5, [o_f, o_b, o_na], "gather_rest_wait_a", which=(0, 1, 2))
    w_na_full = jnp.transpose(wna_g, (1, 0, 2)).reshape(512, D)
    w_hg_full = jnp.transpose(whg_g, (1, 0, 2)).reshape(512, D)
    mix, u_hg = _mix_fwd(o_na, o_f, o_b, hg_norm, w_na_full, w_hg_full, p_act)
    h1, m_act = _wo_fwd(mix, wo_g.reshape(D, D), h0, norm_mlp)
    (_, _, wo_g, wup_g, wdown_g), _ = _exchange_wait(gather_rest, [False] * 5, [m_act], "gather_rest_wait_b", which=(3, 4))
    w_o_full = wo_g.reshape(D, D)
    loss_part, dh2, dh2_b, d_nfinal = _mlp_fwd_loss(m_act, wup_g, wdown_g, h1, norm_final.reshape(1, D), tgt)

    dwup_p, dwdown_p, dh1, dh1_b, d_nmlp = _mlp_bwd(m_act, dh2_b, wup_g, wdown_g, h1, norm_mlp, dh2)
    sc_mlp, tok = _exchange_start([dwup_p, dwdown_p], [True] * 2, "scatter_mlp_start")
    dmix, dwo = _wo_bwd(_tie(dh1_b, tok, "tie_wo_bwd"), w_o_full, mix)
    sc_wo, tok = _exchange_start([dwo.reshape(NDEV, D // NDEV, D)], [True], "scatter_wo_start")
    dp_rest, dwna, dwhg, do_na, do_hg, d_gain = _mix_bwd(
        o_na, u_hg, o_f, o_b, hg_norm, w_na_full, w_hg_full, p_act, _tie(dmix, tok, "tie_mix_bwd"))
    owner_cols = lambda w: jnp.transpose(w.reshape(512, NDEV, D // NDEV), (1, 0, 2))
    sc_br, tok = _exchange_start([owner_cols(dwna), owner_cols(dwhg)], [True] * 2, "scatter_branch_start")
    do_hg = _tie(do_hg, tok, "tie_hg_scan_bwd")
    dq_f, dk_f, db_f, dv_f = _hg_scan_bwd(qh, k_f, b_f, p_act, st_f, do_hg, False)
    dq_b, dk_b, db_b, dv_b = _hg_scan_bwd(qh, k_b, b_b, p_act, st_b, do_hg, True)
    dp_rest, d_logits = _hg_pre_bwd(p_act, logits, dq_f, dq_b, dk_f, dk_b, db_f, db_b, dv_f, dv_b, dp_rest)
    dq_na, dk_na, dv_na, dbias = _na_bwd(p_act, do_na, lse, bias_tab)
    dp_na = jnp.concatenate([dq_na.astype(BF16), dk_na.astype(BF16), dv_na.astype(BF16)], axis=1)
    dwin_p = _inproj_bwd_dw(a_t, dp_na, dp_rest)
    far = 4 * (1 - lax.axis_index("x")) + 2 * (1 - lax.axis_index("y"))
    core = lax.axis_index("c")
    theirs = lax.dynamic_index_in_dim(dwin_p, far + 1 - core, 0, keepdims=True)
    mine = lax.dynamic_index_in_dim(dwin_p, far + core, 0, keepdims=True)
    pair = _add_bf16(mine, _sibling_swap(theirs, "pair_swap_in"), "pair_add_in")
    dwin_p = lax.dynamic_update_index_in_dim(dwin_p, pair, far + core, 0)
    sc_in, tok = _exchange_start([dwin_p], [True], "scatter_in_start", ALL_BUT_FAR_OTHER_CORE, absent=far + 1 - core)
    dh0, d_nmix = _inproj_bwd_da(_tie(dp_na, tok, "tie_inproj_bwd_da"), dp_rest, win_g, h0, norm_mix, dh1)
    d_rpb = _na_rpb_reduce(_tie(dbias, tok, "tie_rpb_reduce"))[:, :, :31]

    res = {}

    def update(nm, parts, w, mm, vv):
        res[nm] = [r[None] for r in _adamw(parts, w[0], mm[0], vv[0], "adamw_" + nm)]
        return res[nm][1]

    wup_r, wdown_r = _exchange_wait(sc_mlp, [True] * 2, [dh0, d_rpb], "scatter_mlp_wait")
    update("w_up", wup_r, w_up, m_w_up, v_w_up)
    last = update("w_down", wdown_r, w_down, m_w_down, v_w_down)
    (wo_r,) = _exchange_wait(sc_wo, [True], [last], "scatter_wo_wait")
    last = update("w_o", wo_r, w_o, m_w_o, v_w_o)
    wna_r, whg_r = _exchange_wait(sc_br, [True] * 2, [last], "scatter_branch_wait")
    update("w_na_out", wna_r, w_na_out, m_w_na_out, v_w_na_out)
    last = update("w_hg_out", whg_r, w_hg_out, m_w_hg_out, v_w_hg_out)

    d_meta = jnp.transpose(dh0[0:NM].reshape(NM, NDEV, 128), (1, 0, 2))
    d_lg = jnp.transpose(d_logits.reshape(2, 2, NDEV, 64), (2, 0, 1, 3)).reshape(NDEV, 2, 128)
    owned_p = jnp.concatenate([d_meta, jnp.pad(d_lg, ((0, 0), (0, OWN_ROWS - NM - 2), (0, 0)))], axis=1)
    repl_p = _pack_replicated(d_nmix, d_nmlp, d_nfinal, d_gain, d_rpb, loss_part)
    grad_x = dh0[NM:L][None]
    done_first = [grad_x] + [res[nm][0] for nm in ("w_up", "w_down", "w_o", "w_na_out", "w_hg_out")]
    owned_r, repl_r = _exchange([owned_p, repl_p], [True, False], "scatter_small", done_first)
    own = _adamw(owned_r, owned, _pack_owned(m_meta_tokens, m_hg_lb_logits), _pack_owned(v_meta_tokens, v_hg_lb_logits),
                 "adamw_owned_small")
    res["meta_tokens"] = [r[0:NM] for r in own]
    res["hg_lb_logits"] = [r[NM:NM + 2].reshape(2, 2, 64) for r in own]
    rep = _adamw(repl_r, _pack_replicated(norm_mix, norm_mlp, norm_final, hg_norm, na_rpb),
                 _pack_replicated(m_norm_mix, m_norm_mlp, m_norm_final, m_hg_norm, m_na_rpb),
                 _pack_replicated(v_norm_mix, v_norm_mlp, v_norm_final, v_hg_norm, v_na_rpb), "adamw_replicated")
    for q in range(4):
        um = _unpack_replicated(rep[q])
        for nm, val in zip(("norm_mix", "norm_mlp", "norm_final", "hg_norm", "na_rpb"), um):
            res.setdefault(nm, [None] * 4)[q] = val
    (win_r,) = _exchange_wait(sc_in, [True], [rep[1], own[1]], "scatter_in_wait", ALL_BUT_FAR_OTHER_CORE)
    update("w_in", win_r, w_in, m_w_in, v_w_in)

    loss = jnp.sum(repl_r[:, LOSS_ROW, 0])
    order = ("meta_tokens", "w_in", "w_na_out", "w_hg_out", "w_o", "w_up", "w_down", "norm_mix", "norm_mlp", "norm_final",
             "hg_norm", "na_rpb", "hg_lb_logits")
    outs = [loss, grad_x]
    for q in range(4):
        outs += [res[nm][q] for nm in order]
    return tuple(outs)
```

```python
import functools

import numpy as np
import jax
import jax.numpy as jnp
from jax import lax
from jax.experimental import pallas as pl
from jax.experimental.pallas import tpu as pltpu

F32 = jnp.float32
BF16 = jnp.bfloat16

D = 1024
SEQ = 2048
NM = 16
L = SEQ + NM
T = 2176
NDEV = 8
EPS = 1e-6
GRID_W = 64
ROWS = SEQ // GRID_W
NA_HEADS = 8
NA_DH = 64
NA_SCALE = NA_DH ** -0.5
HG_HEADS = 4
HG_C = 16
NCHUNK = L // HG_C
D_FF = 4096
IN_COLS = 6144
NEG = -1e30

ADAM_LR = 0.001
ADAM_B1 = 0.9
ADAM_B2 = 0.999
ADAM_EPS = 1e-08
ADAM_WD = 0.01
ADAM_STEP = 10

MESH_ID = pl.DeviceIdType.MESH
ANY = pl.BlockSpec(memory_space=pl.ANY)

NN = (((1,), (0,)), ((), ()))
NT = (((1,), (1,)), ((), ()))
TN = (((0,), (0,)), ((), ()))


def _cp(sem=None, vmem_mb=48):
    return pltpu.CompilerParams(dimension_semantics=sem, vmem_limit_bytes=vmem_mb * 1024 * 1024)


def _dot(a, b, dims=NN):
    return lax.dot_general(a, b, dims, preferred_element_type=F32)


def _sds(shape, dtype):
    return jax.ShapeDtypeStruct(shape, dtype)


HBM = pl.BlockSpec(memory_space=pltpu.HBM)
SEM = pl.BlockSpec(memory_space=pltpu.SEMAPHORE)
EFFECT = pltpu.SideEffectType.DATAFLOW_SIDE_EFFECTING


def _exchange(arrs, scatter, name, after=()):
    n = len(arrs)
    after = list(after)
    out_shapes = []
    for a, sc in zip(arrs, scatter):
        out_shapes.append(_sds(a.shape if sc else (NDEV,) + a.shape, a.dtype))

    def body(*refs):
        ins, outs = refs[:n], refs[n + len(after):2 * n + len(after)]
        send_sems, recv_sems, loc_sems = refs[2 * n + len(after):]
        me = 4 * lax.axis_index("x") + 2 * lax.axis_index("y") + lax.axis_index("c")
        copies = []
        for k in range(n):
            src_me = ins[k].at[me] if scatter[k] else ins[k]
            loc = pltpu.make_async_copy(src_me, outs[k].at[me], loc_sems.at[k])
            loc.start()
            copies.append(loc)
        remote = sum(_peer_copies(ins, outs, scatter, send_sems, recv_sems), [])
        for cp in remote:
            cp.start()
        for cp in remote:
            cp.wait_recv()
        for cp in remote:
            cp.wait_send()
        for cp in copies:
            cp.wait()

    return pl.pallas_call(
        body, name=name, out_shape=tuple(out_shapes), in_specs=[ANY] * (n + len(after)), out_specs=tuple([ANY] * n),
        scratch_shapes=[pltpu.SemaphoreType.DMA((n * (NDEV - 1),)), pltpu.SemaphoreType.DMA((n * (NDEV - 1),)),
                        pltpu.SemaphoreType.DMA((n,))],
    )(*arrs, *after)


def _forward_to_sibling(bufs, name):
    n = len(bufs)

    def body(*refs):
        ins, outs = refs[:n], refs[n:2 * n]
        send_sems, recv_sems = refs[2 * n:]
        x, y, c = lax.axis_index("x"), lax.axis_index("y"), lax.axis_index("c")
        copies = []
        for k in range(n):
            for j, (cx, cy) in enumerate(((1 - x, y), (x, 1 - y), (1 - x, 1 - y))):
                slot = 4 * cx + 2 * cy + c
                copies.append(pltpu.make_async_remote_copy(
                    src_ref=ins[k].at[slot], dst_ref=outs[k].at[slot], send_sem=send_sems.at[3 * k + j],
                    recv_sem=recv_sems.at[3 * k + j], device_id=(x, y, 1 - c), device_id_type=MESH_ID))
        for cp in copies:
            cp.start()
        for cp in copies:
            cp.wait_recv()
        for cp in copies:
            cp.wait_send()

    return pl.pallas_call(
        body, name=name, out_shape=tuple(_sds(b.shape, b.dtype) for b in bufs), in_specs=[ANY] * n,
        out_specs=tuple([ANY] * n), input_output_aliases={k: k for k in range(n)},
        scratch_shapes=[pltpu.SemaphoreType.DMA((3 * n,)), pltpu.SemaphoreType.DMA((3 * n,))],
    )(*bufs)


ALL_PEERS = tuple(range(1, NDEV))
SAME_CORE_AND_SIBLING = (1, 2, 4, 6)
ALL_BUT_FAR_OTHER_CORE = (1, 2, 3, 4, 5, 6)


def _far_slots():
    far = 4 * (1 - lax.axis_index("x")) + 2 * (1 - lax.axis_index("y"))
    core = lax.axis_index("c")
    return far + core, far + 1 - core


def _sibling_swap_far(parts, name):
    def body(x_ref, o_ref, send_sem, recv_sem):
        sib = (lax.axis_index("x"), lax.axis_index("y"), 1 - lax.axis_index("c"))
        cp = pltpu.make_async_remote_copy(src_ref=x_ref.at[_far_slots()[1]], dst_ref=o_ref.at[0], send_sem=send_sem,
                                          recv_sem=recv_sem, device_id=sib, device_id_type=MESH_ID)
        cp.start()
        cp.wait()

    return pl.pallas_call(
        body, name=name, out_shape=_sds((1,) + parts.shape[1:], parts.dtype), in_specs=[ANY], out_specs=ANY,
        scratch_shapes=[pltpu.SemaphoreType.DMA(()), pltpu.SemaphoreType.DMA(())])(parts)


def _add_into_slot(parts, other, slot, name):
    _, rr, cc = parts.shape

    def body(slot_ref, p_ref, o_ref, out_ref):
        del slot_ref
        out_ref[...] = (p_ref[...].astype(F32) + o_ref[...].astype(F32)).astype(BF16)

    mine = pl.BlockSpec((1, rr // 2, cc), lambda j, s: (s[0], j, 0))
    grid_spec = pltpu.PrefetchScalarGridSpec(
        num_scalar_prefetch=1, grid=(2,),
        in_specs=[mine, pl.BlockSpec((1, rr // 2, cc), lambda j, s: (0, j, 0))], out_specs=mine)
    return pl.pallas_call(body, name=name, grid_spec=grid_spec, out_shape=_sds(parts.shape, BF16),
                          input_output_aliases={1: 0}, compiler_params=_cp(("arbitrary",)))(
                              jnp.reshape(slot, (1,)).astype(jnp.int32), parts, other)


def _peer_copies(srcs, lands, scatter, send_sems, recv_sems, masks=ALL_PEERS):
    x, y, c = lax.axis_index("x"), lax.axis_index("y"), lax.axis_index("c")
    me = 4 * x + 2 * y + c
    out = []
    for k in range(len(srcs)):
        out.append([])
        for m in (masks[k] if isinstance(masks[0], tuple) else masks):
            px, py, pc = x ^ (m >> 2), y ^ ((m >> 1) & 1), c ^ (m & 1)
            src = srcs[k].at[4 * px + 2 * py + pc] if scatter[k] else srcs[k]
            out[k].append(pltpu.make_async_remote_copy(
                src_ref=src, dst_ref=lands[k].at[me], send_sem=send_sems.at[k * (NDEV - 1) + m - 1],
                recv_sem=recv_sems.at[k * (NDEV - 1) + m - 1],
                device_id=(px, py, pc), device_id_type=MESH_ID))
    return out


def _exchange_start(arrs, scatter, name, masks=ALL_PEERS, absent=None):
    n = len(arrs)
    me = 4 * lax.axis_index("x") + 2 * lax.axis_index("y") + lax.axis_index("c")
    lands = []
    for a, sc in zip(arrs, scatter):
        own = lax.dynamic_index_in_dim(a, me, 0, keepdims=True) if sc else a[None]
        shape = a.shape if sc else (NDEV,) + a.shape
        land = lax.dynamic_update_index_in_dim(lax.empty(shape, a.dtype), own, me, 0)
        if absent is not None:
            land = lax.dynamic_update_index_in_dim(land, jnp.zeros_like(own), absent, 0)
        lands.append(land)

    def body(*refs):
        srcs, lnds = refs[:n], refs[n:2 * n]
        send_sems, recv_sems = refs[2 * n], refs[2 * n + 1]
        token = refs[-1]
        for cp in sum(_peer_copies(srcs, lnds, scatter, send_sems, recv_sems, masks), []):
            cp.start()
        token[...] = jnp.zeros_like(token)

    ops = [pltpu.with_memory_space_constraint(a, pltpu.HBM) for a in list(arrs) + lands]
    res = pl.pallas_call(
        body, name=name,
        out_shape=(pltpu.SemaphoreType.DMA((n * (NDEV - 1),)), pltpu.SemaphoreType.DMA((n * (NDEV - 1),)))
        + tuple(pltpu.HBM(o.shape, o.dtype) for o in ops) + (_sds((8, 128), F32),),
        in_specs=[HBM] * (2 * n), out_specs=(SEM, SEM) + (HBM,) * (2 * n) + (pl.BlockSpec(memory_space=pltpu.VMEM),),
        input_output_aliases={k: 2 + k for k in range(2 * n)},
        compiler_params=pltpu.CompilerParams(has_side_effects=EFFECT),
    )(*ops)
    return res[:-1], res[-1]


def _exchange_wait(handle, scatter, after, name, masks=ALL_PEERS, which=None):
    send_sems, recv_sems = handle[0], handle[1]
    bufs = handle[2:]
    n = len(bufs) // 2
    after = list(after)

    def body(*refs):
        srcs, lnds = refs[:n], refs[n:2 * n]
        copies = _peer_copies(srcs, lnds, scatter, refs[2 * n], refs[2 * n + 1], masks)
        for k in (range(n) if which is None else which):
            for cp in copies[k]:
                cp.wait_send()
                cp.wait_recv()

    res = pl.pallas_call(
        body, name=name, out_shape=tuple(pltpu.HBM(b.shape, b.dtype) for b in bufs),
        in_specs=[HBM] * (2 * n) + [SEM, SEM] + [ANY] * len(after), out_specs=(HBM,) * (2 * n),
        input_output_aliases={k: k for k in range(2 * n)},
        compiler_params=pltpu.CompilerParams(has_side_effects=EFFECT),
    )(*bufs, send_sems, recv_sems, *after)
    return res[n:] if which is None else (res[n:], (send_sems, recv_sems) + tuple(res))


def _tie(x, token, name):
    def body(x_ref, t_ref, o_ref):
        del x_ref, t_ref, o_ref

    return pl.pallas_call(body, name=name, out_shape=_sds(x.shape, x.dtype), in_specs=[ANY, ANY], out_specs=ANY,
                          input_output_aliases={0: 0})(x, token)


TM_E = 272


def _norm_fwd_t(h, g, name):
    def body(h_ref, g_ref, o_ref, ot_ref):
        xv = h_ref[...]
        r = lax.rsqrt(jnp.mean(xv * xv, axis=-1, keepdims=True) + EPS)
        y = xv * r * g_ref[...]
        o_ref[...] = y.astype(BF16)
        ot_ref[...] = y.T.astype(BF16)

    return pl.pallas_call(
        body, name=name, grid=(T // 128,),
        in_specs=[pl.BlockSpec((128, D), lambda i: (i, 0)), pl.BlockSpec((1, D), lambda i: (0, 0))],
        out_specs=(pl.BlockSpec((128, D), lambda i: (i, 0)), pl.BlockSpec((D, 128), lambda i: (0, i))),
        out_shape=(_sds((T, D), BF16), _sds((D, T), BF16)), compiler_params=_cp(("parallel",)))(h, g)


def _norm_bwd_rows(xv, gv, dnv, dres):
    r = lax.rsqrt(jnp.mean(xv * xv, axis=-1, keepdims=True) + EPS)
    xh = xv * r
    dxh = dnv * gv
    dx = dres + r * (dxh - xh * jnp.mean(dxh * xh, axis=-1, keepdims=True))
    return dx, jnp.sum(dnv * xh, axis=0, keepdims=True)


TM_MM = 1088


def _inproj_fwd(a, w_g):
    nb = w_g.shape[2]

    def body(a_ref, w_ref, o_ref):
        o_ref[...] = _dot(a_ref[...], w_ref[0])

    return pl.pallas_call(
        body, name="inproj_fwd", grid=(T // TM_MM, NDEV),
        in_specs=[pl.BlockSpec((TM_MM, D), lambda i, j: (i, 0)), pl.BlockSpec((1, D, nb), lambda i, j: (j, 0, 0))],
        out_specs=pl.BlockSpec((TM_MM, nb), lambda i, j: (i, j)), out_shape=_sds((T, NDEV * nb), F32),
        compiler_params=_cp(("parallel", "parallel")))(a, w_g)


TM_B = 544


W_IN_B = IN_COLS // NDEV


NA_BLKS = 1536 // W_IN_B


def _dp_specs(rows, row_index):
    return [pl.BlockSpec((rows, W_IN_B), lambda *g: (row_index(*g), jnp.minimum(g[-1], NA_BLKS - 1))),
            pl.BlockSpec((rows, W_IN_B), lambda *g: (row_index(*g), jnp.maximum(g[-1] - NA_BLKS, 0)))]


def _inproj_bwd_dw(a_t, dp_na, dp_rest):
    def body(at_ref, na_ref, rest_ref, dw_ref):
        j = pl.program_id(0)

        @pl.when(j < NA_BLKS)
        def _():
            dw_ref[0] = _dot(at_ref[...], na_ref[...]).astype(BF16)

        @pl.when(j >= NA_BLKS)
        def _():
            dw_ref[0] = _dot(at_ref[...], rest_ref[...]).astype(BF16)

    return pl.pallas_call(
        body, name="inproj_bwd_dw", grid=(NDEV,),
        in_specs=[pl.BlockSpec((D, T), lambda j: (0, 0))] + _dp_specs(T, lambda j: 0),
        out_specs=pl.BlockSpec((1, D, W_IN_B), lambda j: (j, 0, 0)), out_shape=_sds((NDEV, D, W_IN_B), BF16),
        compiler_params=_cp(("parallel",)))(a_t, dp_na, dp_rest)


def _inproj_bwd_da(dp_na, dp_rest, w_g, h0, g_mix, dh1):
    nsub = TM_MM // TM_E

    def body(na_ref, rest_ref, w_ref, h0_ref, g_ref, dres_ref, dh0_ref, dg_ref, da):
        i, j = pl.program_id(0), pl.program_id(1)
        dpv = jnp.where(j < NA_BLKS, na_ref[...], rest_ref[...])
        dav = _dot(dpv, w_ref[0], NT)

        @pl.when(j == 0)
        def _():
            da[...] = dav

        @pl.when(j > 0)
        def _():
            da[...] += dav

        @pl.when(j == NDEV - 1)
        def _():
            gsum = jnp.zeros((1, D), F32)
            for s in range(nsub):
                sub = slice(s * TM_E, (s + 1) * TM_E)
                dx, gpart = _norm_bwd_rows(h0_ref[sub, :], g_ref[...], da[sub, :], dres_ref[sub, :])
                dh0_ref[sub, :] = dx
                gsum = gsum + gpart

            @pl.when(i == 0)
            def _():
                dg_ref[...] = gsum

            @pl.when(i > 0)
            def _():
                dg_ref[...] += gsum

    rblk = pl.BlockSpec((TM_MM, D), lambda i, j: (i, 0))
    vec = pl.BlockSpec((1, D), lambda i, j: (0, 0))
    return pl.pallas_call(
        body, name="inproj_bwd_da", grid=(T // TM_MM, NDEV),
        in_specs=_dp_specs(TM_MM, lambda i, j: i) + [pl.BlockSpec((1, D, W_IN_B), lambda i, j: (j, 0, 0)), rblk, vec, rblk],
        out_specs=(rblk, vec), out_shape=(_sds((T, D), F32), _sds((1, D), F32)),
        scratch_shapes=[pltpu.VMEM((TM_MM, D), F32)],
        compiler_params=_cp(("arbitrary", "arbitrary"), 56))(dp_na, dp_rest, w_g, h0, g_mix, dh1)


NA_QB = 256
NA_GROUPS = ROWS // 4
NA_UROWS = 11
NA_KW = NA_UROWS * GRID_W
NA_KU = 768


def _na_row_offset(var, i, j):
    valid = (j < 8, i <= j < i + 8, 3 <= j < NA_UROWS)[var]
    return (j - i + (7, 3, 0)[var]) if valid else None


def _na_bias_table(rp):
    def body(r_ref, o_ref):
        row3 = lax.broadcasted_iota(jnp.int32, (15, GRID_W, 128), 1)
        lane3 = lax.broadcasted_iota(jnp.int32, (15, GRID_W, 128), 2)
        w3 = lane3 & (GRID_W - 1)
        cs3 = jnp.clip(row3 - 8, 0, GRID_W - 16)
        lane = lax.broadcasted_iota(jnp.int32, (GRID_W, 128), 1)
        neg = jnp.full((GRID_W, 128), NEG, F32)
        z = jnp.stack([jnp.broadcast_to(r_ref[0, a:a + 1, :], (GRID_W, 128)) for a in range(15)])
        for bit in range(6):
            sh = 1 << bit
            z = jnp.where((row3 & sh) != 0, jnp.roll(z, sh, axis=2), z)
        z = jnp.roll(z, 128 - 15, axis=2)
        z = jnp.where(lane3 < GRID_W, z, 0.0)
        z = z + jnp.roll(z, GRID_W, axis=2)
        tabs = jnp.where((w3 >= cs3) & (w3 < cs3 + 16), z, NEG)
        tail = jnp.where(lane < GRID_W + NM, 0.0, NEG)
        for var in range(3):
            for i in range(4):
                for jp in range(NA_KU // 128):
                    halves = []
                    for j in (2 * jp, 2 * jp + 1):
                        a = _na_row_offset(var, i, j) if j < NA_UROWS else None
                        halves.append(tail if j >= NA_UROWS else (neg if a is None else tabs[a]))
                    o_ref[var, 0, i * 64:(i + 1) * 64, jp * 128:(jp + 1) * 128] = jnp.where(lane < GRID_W, halves[0], halves[1])

    return pl.pallas_call(
        body, name="na_bias_table", grid=(NA_HEADS,),
        in_specs=[pl.BlockSpec((1, 15, 128), lambda h: (h, 0, 0))],
        out_specs=pl.BlockSpec((3, 1, NA_QB, NA_KU), lambda h: (0, h, 0, 0)),
        out_shape=_sds((3, NA_HEADS, NA_QB, NA_KU), F32), compiler_params=_cp(("parallel",)))(rp)


def _na_var(g):
    return jnp.where(g == 0, 0, jnp.where(g == NA_GROUPS - 1, 2, 1))


def _na_load_window(src_ref, dst, g):
    us = jnp.clip(4 * g - 4, 0, ROWS - NA_UROWS)
    kstart = pl.multiple_of(NM + GRID_W * us, 16)
    dst[0:NA_KW, :] = src_ref[pl.ds(kstart, NA_KW), :].astype(BF16)
    dst[NA_KW:NA_KW + NM, :] = src_ref[0:NM, :].astype(BF16)
    dst[NA_KW + NM:, :] = jnp.zeros((NA_KU - NA_KW - NM, 128), BF16)
    return kstart


def _na_fwd(p_act, bias_tab):
    def body(q_ref, k_ref, v_ref, b_ref, o_ref, lse_ref, ku, vu):
        g = pl.program_id(1)
        _na_load_window(k_ref, ku, g)
        _na_load_window(v_ref, vu, g)
        qstart = pl.multiple_of(NM + NA_QB * g, 16)
        q = q_ref[pl.ds(qstart, NA_QB), :]
        lane = lax.broadcasted_iota(jnp.int32, (NA_QB, 128), 1)
        o_h, lse_h = [], []
        for h in range(2):
            hm = (lane < 64) if h == 0 else (lane >= 64)
            qm = (jnp.where(hm, q, 0.0) * NA_SCALE).astype(BF16)
            s = _dot(qm, ku[...], NT) + b_ref[0, h]
            m = jnp.max(s, axis=-1, keepdims=True)
            p = jnp.exp(s - m)
            l = jnp.sum(p, axis=-1, keepdims=True)
            o_h.append(_dot(p.astype(BF16), vu[...]) / l)
            lse_h.append(jnp.broadcast_to(m + jnp.log(l), (NA_QB, 128)))
        o_ref[pl.ds(qstart, NA_QB), :] = jnp.where(lane < 64, o_h[0], o_h[1]).astype(BF16)
        lse_ref[0, pl.ds(qstart, NA_QB), :] = jnp.where(lane < 64, lse_h[0], lse_h[1])

        @pl.when(g == 0)
        def _():
            qm_ = q_ref[0:NM, :]
            lane_m = lax.broadcasted_iota(jnp.int32, (NM, 128), 1)
            km, vm = ku[NA_KW:NA_KW + NM, :], vu[NA_KW:NA_KW + NM, :]
            om = []
            for h in range(2):
                hm = (lane_m < 64) if h == 0 else (lane_m >= 64)
                s = _dot(jnp.where(hm, qm_, 0.0).astype(BF16), km, NT) * NA_SCALE
                p = jnp.exp(s - jnp.max(s, axis=-1, keepdims=True))
                l = jnp.sum(p, axis=-1, keepdims=True)
                om.append(_dot(p.astype(BF16), vm) / l)
            o_ref[0:NM, :] = jnp.where(lane_m < 64, om[0], om[1]).astype(BF16)
            o_ref[L:T, :] = jnp.zeros((T - L, 128), BF16)
            lse_ref[0, 0:NM, :] = jnp.zeros((NM, 128), F32)
            lse_ref[0, L:T, :] = jnp.zeros((T - L, 128), F32)

    col = lambda off: pl.BlockSpec((T, 128), lambda hp, g: (0, off + hp))
    return pl.pallas_call(
        body, name="na_fwd", grid=(4, NA_GROUPS),
        in_specs=[col(0), col(4), col(8),
                  pl.BlockSpec((1, 2, NA_QB, NA_KU), lambda hp, g: (_na_var(g), hp, 0, 0))],
        out_specs=(pl.BlockSpec((T, 128), lambda hp, g: (0, hp)), pl.BlockSpec((1, T, 128), lambda hp, g: (hp, 0, 0))),
        out_shape=(_sds((T, 512), BF16), _sds((4, T, 128), F32)),
        scratch_shapes=[pltpu.VMEM((NA_KU, 128), BF16), pltpu.VMEM((NA_KU, 128), BF16)],
        compiler_params=_cp(("parallel", "arbitrary")))(p_act, p_act, p_act, bias_tab)


def _na_bwd(p_act, do, lse, bias_tab):
    def body(q_ref, k_ref, v_ref, do_ref, lse_ref, b_ref, dq_ref, dk_ref, dv_ref, db_ref, ku, vu):
        g = pl.program_id(1)

        @pl.when(g == 0)
        def _():
            dq_ref[...] = jnp.zeros((T, 128), F32)
            dk_ref[...] = jnp.zeros((T, 128), F32)
            dv_ref[...] = jnp.zeros((T, 128), F32)

        kstart = _na_load_window(k_ref, ku, g)
        _na_load_window(v_ref, vu, g)
        qstart = pl.multiple_of(NM + NA_QB * g, 16)
        q = q_ref[pl.ds(qstart, NA_QB), :]
        dov = do_ref[pl.ds(qstart, NA_QB), :]
        lsev = lse_ref[0, pl.ds(qstart, NA_QB), :]
        lane = lax.broadcasted_iota(jnp.int32, (NA_QB, 128), 1)
        first = (g == 0) | (g == 1) | (g == NA_GROUPS - 1)
        dq_h = []
        dku = jnp.zeros((NA_KU, 128), F32)
        dvu = jnp.zeros((NA_KU, 128), F32)
        for h in range(2):
            hm = (lane < 64) if h == 0 else (lane >= 64)
            qm = (jnp.where(hm, q, 0.0) * NA_SCALE).astype(BF16)
            dom = jnp.where(hm, dov, 0.0).astype(BF16)
            s = _dot(qm, ku[...], NT) + b_ref[0, h]
            p = jnp.exp(s - lsev[:, 64 * h:64 * h + 1])
            dp = _dot(dom, vu[...], NT)
            delta = jnp.sum(p * dp, axis=-1, keepdims=True)
            ds = p * (dp - delta)

            @pl.when(first)
            def _():
                db_ref[0, h] = ds

            @pl.when(jnp.logical_not(first))
            def _():
                db_ref[0, h] += ds

            dsb = ds.astype(BF16)
            dq_h.append(_dot(dsb, ku[...]) * NA_SCALE)
            dku = dku + _dot(dsb, qm, TN)
            dvu = dvu + _dot(p.astype(BF16), dom, TN)
        dq_ref[pl.ds(qstart, NA_QB), :] = jnp.where(lane < 64, dq_h[0], dq_h[1])
        dk_ref[pl.ds(kstart, NA_KW), :] += dku[0:NA_KW]
        dv_ref[pl.ds(kstart, NA_KW), :] += dvu[0:NA_KW]
        dk_ref[0:NM, :] += dku[NA_KW:NA_KW + NM]
        dv_ref[0:NM, :] += dvu[NA_KW:NA_KW + NM]

        @pl.when(g == 0)
        def _():
            qm_ = q_ref[0:NM, :]
            dom_ = do_ref[0:NM, :]
            lane_m = lax.broadcasted_iota(jnp.int32, (NM, 128), 1)
            km, vm = ku[NA_KW:NA_KW + NM, :], vu[NA_KW:NA_KW + NM, :]
            dqs = []
            dkm = jnp.zeros((NM, 128), F32)
            dvm = jnp.zeros((NM, 128), F32)
            for h in range(2):
                hm = (lane_m < 64) if h == 0 else (lane_m >= 64)
                qh = jnp.where(hm, qm_, 0.0).astype(BF16)
                doh = jnp.where(hm, dom_, 0.0).astype(BF16)
                s = _dot(qh, km, NT) * NA_SCALE
                e = jnp.exp(s - jnp.max(s, axis=-1, keepdims=True))
                p = e / jnp.sum(e, axis=-1, keepdims=True)
                dp = _dot(doh, vm, NT)
                ds = p * (dp - jnp.sum(p * dp, axis=-1, keepdims=True))
                dsb = (ds * NA_SCALE).astype(BF16)
                dqs.append(_dot(dsb, km))
                dkm = dkm + _dot(dsb, qh, TN)
                dvm = dvm + _dot(p.astype(BF16), doh, TN)
            dq_ref[0:NM, :] = jnp.where(lane_m < 64, dqs[0], dqs[1])
            dk_ref[0:NM, :] += dkm
            dv_ref[0:NM, :] += dvm

    col = lambda off: pl.BlockSpec((T, 128), lambda hp, g: (0, off + hp))
    ocol = pl.BlockSpec((T, 128), lambda hp, g: (0, hp))
    bspec = pl.BlockSpec((1, 2, NA_QB, NA_KU), lambda hp, g: (_na_var(g), hp, 0, 0))
    return pl.pallas_call(
        body, name="na_bwd", grid=(4, NA_GROUPS),
        in_specs=[col(0), col(4), col(8), ocol, pl.BlockSpec((1, T, 128), lambda hp, g: (hp, 0, 0)), bspec],
        out_specs=(ocol, ocol, ocol, bspec),
        out_shape=(_sds((T, 512), F32), _sds((T, 512), F32), _sds((T, 512), F32), _sds((3, NA_HEADS, NA_QB, NA_KU), F32)),
        scratch_shapes=[pltpu.VMEM((NA_KU, 128), BF16), pltpu.VMEM((NA_KU, 128), BF16)],
        compiler_params=_cp(("parallel", "arbitrary")))(p_act, p_act, p_act, do, lse, bias_tab)


def _na_rpb_reduce(dbias):
    def body(db_ref, o_ref):
        lane = lax.broadcasted_iota(jnp.int32, (GRID_W, 128), 1)
        row3 = lax.broadcasted_iota(jnp.int32, (15, GRID_W, 128), 1)
        lane3 = lax.broadcasted_iota(jnp.int32, (15, GRID_W, 128), 2)
        accs = []
        for a in range(15):
            acc = jnp.zeros((GRID_W, 128), F32)
            for var in range(3):
                for i in range(4):
                    for j in range(NA_UROWS):
                        if _na_row_offset(var, i, j) == a:
                            pair = db_ref[var, 0, i * 64:(i + 1) * 64, (j // 2) * 128:(j // 2 + 1) * 128]
                            acc = acc + jnp.where((lane < GRID_W) if j % 2 == 0 else (lane >= GRID_W), pair, 0.0)
            accs.append(acc)
        z = jnp.stack(accs)
        z = jnp.where(lane3 < GRID_W, z + jnp.roll(z, GRID_W, axis=2), 0.0)
        for bit in range(6):
            sh = 1 << bit
            z = jnp.where((row3 & sh) != 0, jnp.roll(z, 128 - sh, axis=2), z)
        z = jnp.roll(z, 15, axis=2)
        o_ref[0] = jnp.sum(z, axis=1)

    return pl.pallas_call(
        body, name="na_rpb_reduce", grid=(NA_HEADS,),
        in_specs=[pl.BlockSpec((3, 1, NA_QB, NA_KU), lambda h: (0, h, 0, 0))],
        out_specs=pl.BlockSpec((1, 15, 128), lambda h: (h, 0, 0)), out_shape=_sds((NA_HEADS, 15, 128), F32),
        compiler_params=_cp(("parallel",)))(dbias)


HG_RB = 128
HG_NB = T // HG_RB
HG_SLOTS = HG_NB * 8
HI = lax.Precision.HIGHEST
HG_UNROLL = 4
HG_UNROLL_WIDE = 8


def _chunk_tri(lower):
    r = lax.broadcasted_iota(jnp.int32, (HG_RB, HG_RB), 0)
    c = lax.broadcasted_iota(jnp.int32, (HG_RB, HG_RB), 1)
    same = (r // HG_C) == (c // HG_C)
    keep = (c <= r) if lower else (c >= r)
    return jnp.where(same & keep, 1.0, 0.0).astype(F32)


def _hg_gate_terms(z, lg):
    dl = lg[0:1, :] - lg[1:2, :]
    log_lb = jax.nn.log_sigmoid(dl)
    log_1mlb = jax.nn.log_sigmoid(-dl)
    yz = log_1mlb + jax.nn.log_sigmoid(z)
    log_f = jnp.logaddexp(log_lb, yz)
    snz = jax.nn.sigmoid(-z)
    k = jnp.exp(log_1mlb) * snz
    w2 = jnp.exp(yz - log_f)
    return log_f, k, snz, w2


def _hg_pre(p_act, logits):
    def body(q_ref, zf_ref, zb_ref, lg_ref, qh_ref, kf_ref, bf_ref, kb_ref, bb_ref):
        qh_ref[...] = jax.nn.silu(q_ref[...])
        lf, kf, _, _ = _hg_gate_terms(zf_ref[...], lg_ref[0])
        kf_ref[...] = kf
        bf_ref[...] = jnp.dot(_chunk_tri(True), lf, precision=HI, preferred_element_type=F32)
        lb_, kb, _, _ = _hg_gate_terms(zb_ref[...], lg_ref[1])
        kb_ref[...] = kb
        bb_ref[...] = jnp.dot(_chunk_tri(False), lb_, precision=HI, preferred_element_type=F32)

    blk = lambda c: pl.BlockSpec((HG_RB, 512), lambda i: (i, c))
    ob = pl.BlockSpec((HG_RB, 512), lambda i: (i, 0))
    return pl.pallas_call(
        body, name="hg_pre", grid=(HG_NB,),
        in_specs=[blk(3), blk(4), blk(5), pl.BlockSpec((2, 2, 512), lambda i: (0, 0, 0))],
        out_specs=(ob,) * 5, out_shape=(_sds((T, 512), F32),) * 5,
        compiler_params=_cp(("parallel",)))(p_act, p_act, p_act, logits)


def _bdot(a, b, ca, cb):
    return lax.dot_general(a.astype(BF16), b.astype(BF16), (((ca,), (cb,)), ((0,), (0,))), preferred_element_type=F32)


HG_S = 8
HG_NS = HG_RB // HG_S


def _lane_sums(xs):
    l_io = lax.broadcasted_iota(jnp.int32, (HG_NS, HG_S, HG_S), 2)
    a = jnp.zeros((HG_NS, HG_S, HG_S), F32)
    for j, x in enumerate(xs):
        a = a + jnp.where(l_io == j, jnp.sum(x, axis=-1, keepdims=True), 0.0)
    return a


def _halves(x):
    y = x.reshape(8, 2, HG_S, x.shape[-1])
    return y[:, 0], y[:, 1]


def _join(first, second):
    return jnp.stack([first, second], axis=1).reshape(HG_RB, first.shape[-1])


def _cross_split(rev, b4):
    b_1, b_2 = _halves(b4)
    if rev:
        r = b_2[:, 0:1, :]
        return jnp.exp(b_1 - r), jnp.exp(r - b_2)
    r = b_1[:, HG_S - 1:HG_S, :]
    return jnp.exp(b_2 - r), jnp.exp(r - b_1)


def _hg_scan_fwd(qh, k, b, p_act, rev):
    anchor = 0 if rev else HG_C - 1

    def body(q_ref, k_ref, b_ref, v_ref, o_ref, st_ref, dsc):
        def phase_a(blk, _):
            rows = pl.ds(pl.multiple_of(blk * HG_RB, HG_RB), HG_RB)
            b3 = b_ref[rows, :].reshape(8, HG_C, 128)
            k3 = k_ref[rows, :].reshape(8, HG_C, 128)
            v3 = v_ref[rows, :].reshape(8, HG_C, 128)
            bl = b3[:, anchor:anchor + 1, :]
            kt = k3 * jnp.exp(bl - b3)
            st_ref[0, pl.ds(pl.multiple_of(blk * 8, 8), 8)] = _bdot(v3, kt, 1, 1)
            dsc[pl.ds(pl.multiple_of(blk * 8, 8), 8), :] = jnp.exp(bl[:, 0, :])
            return 0

        lax.fori_loop(0, HG_NB, phase_a, 0, unroll=HG_UNROLL_WIDE)

        def phase_b(n, carry):
            c = (NCHUNK - 1 - n) if rev else n
            u = st_ref[0, c]
            st_ref[0, c] = carry
            return carry * dsc[pl.ds(c, 1), :] + u

        lax.fori_loop(0, NCHUNK // 3, lambda n3, s: phase_b(3 * n3 + 2, phase_b(3 * n3 + 1, phase_b(3 * n3, s))),
                      jnp.zeros((128, 128), F32))
        for c in range(NCHUNK, HG_SLOTS):
            st_ref[0, c] = jnp.zeros((128, 128), F32)

        t_io = lax.broadcasted_iota(jnp.int32, (HG_NS, HG_S, 128), 1)

        def phase_c(blk, _):
            rows = pl.ds(pl.multiple_of(blk * HG_RB, HG_RB), HG_RB)
            b4 = b_ref[rows, :].reshape(HG_NS, HG_S, 128)
            k4 = k_ref[rows, :].reshape(HG_NS, HG_S, 128)
            q4 = q_ref[rows, :].reshape(HG_NS, HG_S, 128)
            v4 = v_ref[rows, :].reshape(HG_NS, HG_S, 128)
            st = st_ref[0, pl.ds(pl.multiple_of(blk * 8, 8), 8)]
            o = _bdot((q4 * jnp.exp(b4)).reshape(8, HG_C, 128), st, 2, 2).reshape(HG_RB, 128)
            terms = []
            for s in range(HG_S):
                ok = (t_io <= s) if rev else (t_io >= s)
                f = jnp.exp(jnp.where(ok, b4 - b4[:, s:s + 1, :], NEG))
                terms.append(q4 * f * k4[:, s:s + 1, :])
            o_in = _bdot(_lane_sums(terms), v4, 2, 1)
            wq, wk = _cross_split(rev, b4)
            q_1, q_2 = _halves(q4)
            k_1, k_2 = _halves(k4)
            v_1, v_2 = _halves(v4)
            o_1, o_2 = _halves(o_in)
            if rev:
                o_1 = o_1 + _bdot(_bdot(q_1 * wq, k_2 * wk, 2, 2), v_2, 2, 1)
            else:
                o_2 = o_2 + _bdot(_bdot(q_2 * wq, k_1 * wk, 2, 2), v_1, 2, 1)
            o_ref[rows, :] = o + _join(o_1, o_2)
            return 0

        lax.fori_loop(0, HG_NB, phase_c, 0, unroll=HG_UNROLL_WIDE)

    col = pl.BlockSpec((T, 128), lambda h: (0, h))
    return pl.pallas_call(
        body, name="hg_scan_bwd_dir" if rev else "hg_scan_fwd_dir", grid=(HG_HEADS,),
        in_specs=[col, col, col, pl.BlockSpec((T, 128), lambda h: (0, 24 + h))],
        out_specs=(col, pl.BlockSpec((1, HG_SLOTS, 128, 128), lambda h: (h, 0, 0, 0))),
        out_shape=(_sds((T, 512), F32), _sds((HG_HEADS, HG_SLOTS, 128, 128), F32)),
        scratch_shapes=[pltpu.VMEM((HG_SLOTS, 128), F32)],
        compiler_params=_cp(("parallel",), 56))(qh, k, b, p_act)


def _hg_scan_bwd(qh, k, b, p_act, st, do, rev):
    anchor = 0 if rev else HG_C - 1

    def body(q_ref, k_ref, b_ref, v_ref, st_ref, do_ref, dq_ref, dk_ref, db_ref, dv_ref, gst, dsc, dbl):
        def phase_a(blk, _):
            rows = pl.ds(pl.multiple_of(blk * HG_RB, HG_RB), HG_RB)
            b3 = b_ref[rows, :].reshape(8, HG_C, 128)
            q3 = q_ref[rows, :].reshape(8, HG_C, 128)
            do3 = do_ref[rows, :].reshape(8, HG_C, 128)
            gst[pl.ds(pl.multiple_of(blk * 8, 8), 8)] = _bdot(do3, q3 * jnp.exp(b3), 1, 1)
            dsc[pl.ds(pl.multiple_of(blk * 8, 8), 8), :] = jnp.exp(b3[:, anchor, :])
            return 0

        lax.fori_loop(0, HG_NB, phase_a, 0, unroll=HG_UNROLL_WIDE)

        def phase_b(n, carry):
            c = n if rev else (NCHUNK - 1 - n)
            w = gst[c]
            gst[c] = carry
            dcv = dsc[pl.ds(c, 1), :]
            dbl[pl.ds(c, 1), :] = dcv * jnp.sum(st_ref[0, c] * carry, axis=0, keepdims=True)
            return carry * dcv + w

        lax.fori_loop(0, NCHUNK // 3, lambda n3, s: phase_b(3 * n3 + 2, phase_b(3 * n3 + 1, phase_b(3 * n3, s))),
                      jnp.zeros((128, 128), F32))
        for c in range(NCHUNK, HG_SLOTS):
            gst[c] = jnp.zeros((128, 128), F32)
            dbl[c:c + 1, :] = jnp.zeros((1, 128), F32)

        t_io = lax.broadcasted_iota(jnp.int32, (HG_NS, HG_S, 128), 1)
        t16 = lax.broadcasted_iota(jnp.int32, (8, HG_C, 128), 1)
        r_io = lax.broadcasted_iota(jnp.int32, (HG_NS, HG_S, HG_S), 1)
        l_io = lax.broadcasted_iota(jnp.int32, (HG_NS, HG_S, HG_S), 2)

        def phase_c(blk, _):
            rows = pl.ds(pl.multiple_of(blk * HG_RB, HG_RB), HG_RB)
            cs = pl.ds(pl.multiple_of(blk * 8, 8), 8)
            b4 = b_ref[rows, :].reshape(HG_NS, HG_S, 128)
            k4 = k_ref[rows, :].reshape(HG_NS, HG_S, 128)
            q4 = q_ref[rows, :].reshape(HG_NS, HG_S, 128)
            v4 = v_ref[rows, :].reshape(HG_NS, HG_S, 128)
            do4 = do_ref[rows, :].reshape(HG_NS, HG_S, 128)
            b3, k3, q3 = (z.reshape(8, HG_C, 128) for z in (b4, k4, q4))
            v3, do3 = v4.reshape(8, HG_C, 128), do4.reshape(8, HG_C, 128)
            s_t = st_ref[0, cs]
            g_t = gst[cs]
            bl = b3[:, anchor:anchor + 1, :]
            ekl = jnp.exp(bl - b3)
            kt = k3 * ekl
            dkt = _bdot(v3, g_t, 2, 1)
            dq = (_bdot(do3, s_t, 2, 1) * jnp.exp(b3)).reshape(HG_NS, HG_S, 128)
            dk = (dkt * ekl).reshape(HG_NS, HG_S, 128)
            dv = _bdot(kt, g_t, 2, 2).reshape(HG_NS, HG_S, 128)
            dbl3 = dbl[cs, :].reshape(8, 1, 128) + jnp.sum(dkt * kt, axis=1, keepdims=True)
            causal = (l_io >= r_io) if rev else (l_io <= r_io)
            da = jnp.where(causal, _bdot(do4, v4, 2, 2), 0.0)
            causal_t = (l_io <= r_io) if rev else (l_io >= r_io)
            dat = jnp.where(causal_t, _bdot(v4, do4, 2, 2), 0.0)
            for s in range(HG_S):
                ok = (t_io <= s) if rev else (t_io >= s)
                f = jnp.exp(jnp.where(ok, b4 - b4[:, s:s + 1, :], NEG))
                dq = dq + da[:, :, s:s + 1] * (f * k4[:, s:s + 1, :])
            terms = []
            for t in range(HG_S):
                ok = (t_io >= t) if rev else (t_io <= t)
                e = jnp.exp(jnp.where(ok, b4[:, t:t + 1, :] - b4, NEG))
                eq = e * q4[:, t:t + 1, :]
                dk = dk + dat[:, :, t:t + 1] * eq
                terms.append(eq * k4)
            dv = dv + _bdot(_lane_sums(terms), do4, 2, 1)
            wq, wk = _cross_split(rev, b4)
            pick = (lambda z: _halves(z)) if rev else (lambda z: _halves(z)[::-1])
            (q_q, _), (_, k_k), (_, v_k), (do_q, _) = pick(q4), pick(k4), pick(v4), pick(do4)
            qx, kx = q_q * wq, k_k * wk
            dq_q = _bdot(_bdot(do_q, v_k, 2, 2), kx, 2, 1) * wq
            dk_k = _bdot(_bdot(v_k, do_q, 2, 2), qx, 2, 1) * wk
            dv_k = _bdot(_bdot(kx, qx, 2, 2), do_q, 2, 1)
            zero = jnp.zeros((8, HG_S, 128), F32)
            place_q = (lambda z: _join(z, zero)) if rev else (lambda z: _join(zero, z))
            place_k = (lambda z: _join(zero, z)) if rev else (lambda z: _join(z, zero))
            dq2 = dq.reshape(HG_RB, 128) + place_q(dq_q)
            dk2 = dk.reshape(HG_RB, 128) + place_k(dk_k)
            dv2 = dv.reshape(HG_RB, 128) + place_k(dv_k)
            dq3, dk3 = dq2.reshape(8, HG_C, 128), dk2.reshape(8, HG_C, 128)
            db = q3 * dq3 - k3 * dk3 + jnp.where(t16 == anchor, dbl3, 0.0)
            dq_ref[rows, :] = dq2
            dk_ref[rows, :] = dk2
            db_ref[rows, :] = db.reshape(HG_RB, 128)
            dv_ref[rows, :] = dv2
            return 0

        lax.fori_loop(0, HG_NB, phase_c, 0, unroll=HG_UNROLL)

    col = pl.BlockSpec((T, 128), lambda h: (0, h))
    return pl.pallas_call(
        body, name="hg_scan_bwd_dir_bwd" if rev else "hg_scan_fwd_dir_bwd", grid=(HG_HEADS,),
        in_specs=[col, col, col, pl.BlockSpec((T, 128), lambda h: (0, 24 + h)),
                  pl.BlockSpec((1, HG_SLOTS, 128, 128), lambda h: (h, 0, 0, 0)), col],
        out_specs=(col,) * 4, out_shape=(_sds((T, 512), F32),) * 4,
        scratch_shapes=[pltpu.VMEM((HG_SLOTS, 128, 128), F32), pltpu.VMEM((HG_SLOTS, 128), F32),
                        pltpu.VMEM((HG_SLOTS, 128), F32)],
        compiler_params=_cp(("parallel",), 56))(qh, k, b, p_act, st, do)


def _row_valid(i, tm):
    r = lax.broadcasted_iota(jnp.int32, (tm, 1), 0) + i * tm
    return r < L


def _hg_post_rows(o, gv, gain_v, valid):
    parts = []
    for h in range(HG_HEADS):
        oh = o[:, 128 * h:128 * (h + 1)]
        parts.append(oh * lax.rsqrt(jnp.mean(oh * oh, axis=-1, keepdims=True) + EPS))
    return jnp.where(valid, jnp.concatenate(parts, axis=1) * gain_v * jax.nn.silu(gv), 0.0)


def _hg_post_bwd_rows(du, o, gv, gain_v, valid):
    duv = jnp.where(valid, du, 0.0)
    sig = jax.nn.sigmoid(gv)
    sg = gv * sig
    dn = duv * gain_v * sg
    do_parts, n_parts = [], []
    for h in range(HG_HEADS):
        sl = slice(128 * h, 128 * (h + 1))
        oh = o[:, sl]
        r = lax.rsqrt(jnp.mean(oh * oh, axis=-1, keepdims=True) + EPS)
        nh = oh * r
        dnh = dn[:, sl]
        do_parts.append(r * (dnh - nh * jnp.mean(dnh * nh, axis=-1, keepdims=True)))
        n_parts.append(nh)
    n = jnp.where(valid, jnp.concatenate(n_parts, axis=1), 0.0)
    do = jnp.where(valid, jnp.concatenate(do_parts, axis=1), 0.0)
    dg = duv * n * gain_v * (sig * (1.0 + gv * (1.0 - sig)))
    return do, dg, jnp.sum(duv * n * sg, axis=0, keepdims=True)


def _hg_pre_bwd(p_act, logits, dq_f, dq_b, dk_f, dk_b, db_f, db_b, dv_f, dv_b, dp_rest):
    def body(q_ref, zf_ref, zb_ref, lg_ref, dqf_ref, dqb_ref, dkf_ref, dkb_ref, dbf_ref, dbb_ref, dvf_ref, dvb_ref, _,
             dp_ref, dlg_ref):
        dq_ref, dzf_ref, dzb_ref, di_ref = (dp_ref.at[:, 512 * c:512 * (c + 1)] for c in range(4))
        i = pl.program_id(0)
        valid = _row_valid(i, HG_RB)
        qv = q_ref[...]
        sig = jax.nn.sigmoid(qv)
        dq_ref[...] = jnp.where(valid, (dqf_ref[...] + dqb_ref[...]) * (sig * (1.0 + qv * (1.0 - sig))), 0.0).astype(BF16)
        di_ref[...] = jnp.where(valid, dvf_ref[...] + dvb_ref[...], 0.0).astype(BF16)
        for d, (z_ref, dk_r, db_r, dz_ref) in enumerate(((zf_ref, dkf_ref, dbf_ref, dzf_ref), (zb_ref, dkb_ref, dbb_ref, dzb_ref))):
            lg = lg_ref[d]
            dl = lg[0:1, :] - lg[1:2, :]
            lb = jax.nn.sigmoid(dl)
            one_m_lb = jax.nn.sigmoid(-dl)
            log_f, _, snz, w2 = _hg_gate_terms(z_ref[...], lg)
            dbv = jnp.where(valid, db_r[...], 0.0)
            dkv = jnp.where(valid, dk_r[...], 0.0)
            dlf = jnp.dot(_chunk_tri(d == 1), dbv, precision=HI, preferred_element_type=F32)
            sz = 1.0 - snz
            dz_ref[...] = (dlf * w2 * snz - dkv * one_m_lb * sz * snz).astype(BF16)
            dlb = jnp.sum(dlf * snz * jnp.exp(-log_f) - dkv * snz, axis=0, keepdims=True)
            dl0 = dlb * lb * one_m_lb
            part = jnp.concatenate([dl0, -dl0], axis=0)

            @pl.when(i == 0)
            def _():
                dlg_ref[d] = part

            @pl.when(i > 0)
            def _():
                dlg_ref[d] += part

    blk = lambda c: pl.BlockSpec((HG_RB, 512), lambda i: (i, c))
    ob = pl.BlockSpec((HG_RB, 512), lambda i: (i, 0))
    lgs = pl.BlockSpec((2, 2, 512), lambda i: (0, 0, 0))
    return pl.pallas_call(
        body, name="hg_pre_bwd", grid=(HG_NB,),
        in_specs=[blk(3), blk(4), blk(5), lgs] + [ob] * 8 + [ANY],
        out_specs=(pl.BlockSpec((HG_RB, 2048), lambda i: (i, 0)), lgs),
        out_shape=(_sds(dp_rest.shape, BF16), _sds((2, 2, 512), F32)), input_output_aliases={12: 0},
        compiler_params=_cp(("arbitrary",)))(p_act, p_act, p_act, logits, dq_f, dq_b, dk_f, dk_b, db_f, db_b, dv_f, dv_b,
                                             dp_rest)


def _mix_fwd(o_na, o_f, o_b, gain, w_na, w_hg, p_act):
    def body(ona_ref, of_ref, ob_ref, g_ref, gain_ref, wna_ref, whg_ref, gna_ref, ghg_ref, o_ref, u_ref):
        u = _hg_post_rows(of_ref[...] + ob_ref[...], g_ref[...], gain_ref[...], _row_valid(pl.program_id(0), TM_B)).astype(BF16)
        u_ref[...] = u
        y_na = _dot(ona_ref[...], wna_ref[...])
        y_hg = _dot(u, whg_ref[...])
        o_ref[...] = (jax.nn.sigmoid(gna_ref[...]) * y_na + jax.nn.sigmoid(ghg_ref[...]) * y_hg).astype(BF16)

    act = pl.BlockSpec((TM_B, 512), lambda i: (i, 0))
    wsp = pl.BlockSpec((512, D), lambda i: (0, 0))
    return pl.pallas_call(
        body, name="mix_fwd", grid=(T // TM_B,),
        in_specs=[act, act, act, pl.BlockSpec((TM_B, 512), lambda i: (i, 7)), pl.BlockSpec((1, 512), lambda i: (0, 0)),
                  wsp, wsp, pl.BlockSpec((TM_B, D), lambda i: (i, 4)), pl.BlockSpec((TM_B, D), lambda i: (i, 5))],
        out_specs=(pl.BlockSpec((TM_B, D), lambda i: (i, 0)), act), out_shape=(_sds((T, D), BF16), _sds((T, 512), BF16)),
        compiler_params=_cp(("parallel",)))(o_na, o_f, o_b, p_act, gain, w_na, w_hg, p_act, p_act)


DP_REST = IN_COLS - 1536


def _mix_bwd(o_na, u_hg, o_f, o_b, gain, w_na, w_hg, p_act, dmix):
    ni = T // TM_B

    def body(ona_ref, uhg_ref, of_ref, ob_ref, g_ref, gain_ref, wna_ref, whg_ref, gna_ref, ghg_ref, dmix_ref,
             dp_ref, dwna_ref, dwhg_ref, dona_ref, do_ref, dgain_ref, acc_na, acc_hg):
        i = pl.program_id(0)
        dg_ref, dgna_ref, dghg_ref = dp_ref.at[:, 2048:2560], dp_ref.at[:, 2560:3584], dp_ref.at[:, 3584:4608]
        dm = dmix_ref[...].astype(F32)
        dxs = []
        for x_ref, w_ref, gt_ref, dgt_ref, dw_ref, acc in (
                (ona_ref, wna_ref, gna_ref, dgna_ref, dwna_ref, acc_na), (uhg_ref, whg_ref, ghg_ref, dghg_ref, dwhg_ref, acc_hg)):
            xv = x_ref[...]
            y = _dot(xv, w_ref[...])
            sg = jax.nn.sigmoid(gt_ref[...])
            dgt_ref[...] = (dm * y * sg * (1.0 - sg)).astype(BF16)
            dy = (dm * sg).astype(BF16)
            dxs.append(_dot(dy, w_ref[...], NT))
            part = _dot(xv, dy, TN)

            @pl.when(i == 0)
            def _():
                acc[...] = part

            @pl.when(i > 0)
            def _():
                acc[...] += part

            @pl.when(i == ni - 1)
            def _():
                dw_ref[...] = acc[...].astype(BF16)

        dona_ref[...] = dxs[0]
        do, dg, gpart = _hg_post_bwd_rows(dxs[1], of_ref[...] + ob_ref[...], g_ref[...], gain_ref[...], _row_valid(i, TM_B))
        do_ref[...] = do
        dg_ref[...] = dg.astype(BF16)

        @pl.when(i == 0)
        def _():
            dgain_ref[...] = gpart

        @pl.when(i > 0)
        def _():
            dgain_ref[...] += gpart

    act = pl.BlockSpec((TM_B, 512), lambda i: (i, 0))
    wsp = pl.BlockSpec((512, D), lambda i: (0, 0))
    rblk = pl.BlockSpec((TM_B, D), lambda i: (i, 0))
    vec = pl.BlockSpec((1, 512), lambda i: (0, 0))
    return pl.pallas_call(
        body, name="mix_bwd", grid=(ni,),
        in_specs=[act, act, act, act, pl.BlockSpec((TM_B, 512), lambda i: (i, 7)), vec, wsp, wsp,
                  pl.BlockSpec((TM_B, D), lambda i: (i, 4)), pl.BlockSpec((TM_B, D), lambda i: (i, 5)), rblk],
        out_specs=(pl.BlockSpec((TM_B, DP_REST), lambda i: (i, 0)), wsp, wsp, act, act, vec),
        out_shape=(_sds((T, DP_REST), BF16), _sds((512, D), BF16), _sds((512, D), BF16),
                   _sds((T, 512), F32), _sds((T, 512), F32), _sds((1, 512), F32)),
        scratch_shapes=[pltpu.VMEM((512, D), F32), pltpu.VMEM((512, D), F32)],
        compiler_params=_cp(("arbitrary",)))(o_na, u_hg, o_f, o_b, p_act, gain, w_na, w_hg, p_act, p_act, dmix)


def _wo_fwd(mix, w_o, h0, g_mlp):
    def body(mix_ref, w_ref, h0_ref, g_ref, h1_ref, m_ref):
        h1 = h0_ref[...] + _dot(mix_ref[...], w_ref[...])
        h1_ref[...] = h1
        r = lax.rsqrt(jnp.mean(h1 * h1, axis=-1, keepdims=True) + EPS)
        m_ref[...] = (h1 * r * g_ref[...]).astype(BF16)

    blk = pl.BlockSpec((TM_B, D), lambda i: (i, 0))
    return pl.pallas_call(
        body, name="wo_fwd", grid=(T // TM_B,),
        in_specs=[blk, pl.BlockSpec((D, D), lambda i: (0, 0)), blk, pl.BlockSpec((1, D), lambda i: (0, 0))],
        out_specs=(blk, blk), out_shape=(_sds((T, D), F32), _sds((T, D), BF16)),
        compiler_params=_cp(("parallel",)))(mix, w_o, h0, g_mlp)


def _wo_bwd(dh1_b, w_o, mix):
    ni = T // TM_B

    def body(dh_ref, w_ref, mix_ref, dmix_ref, dw_ref, acc):
        i = pl.program_id(0)
        dh = dh_ref[...]
        dmix_ref[...] = _dot(dh, w_ref[...], NT).astype(BF16)
        part = _dot(mix_ref[...], dh, TN)

        @pl.when(i == 0)
        def _():
            acc[...] = part

        @pl.when(i > 0)
        def _():
            acc[...] += part

        @pl.when(i == ni - 1)
        def _():
            dw_ref[...] = acc[...].astype(BF16)

    blk = pl.BlockSpec((TM_B, D), lambda i: (i, 0))
    wsp = pl.BlockSpec((D, D), lambda i: (0, 0))
    return pl.pallas_call(
        body, name="wo_bwd", grid=(ni,), in_specs=[blk, wsp, blk], out_specs=(blk, wsp),
        out_shape=(_sds((T, D), BF16), _sds((D, D), BF16)), scratch_shapes=[pltpu.VMEM((D, D), F32)],
        compiler_params=_cp(("arbitrary",)))(dh1_b, w_o, mix)


FF_B = D_FF // NDEV


def _loss_rows(xv, gv, tv, row0):
    r_io = lax.broadcasted_iota(jnp.int32, (xv.shape[0], 1), 0) + row0
    valid = (r_io >= NM) & (r_io < L)
    r = lax.rsqrt(jnp.mean(xv * xv, axis=-1, keepdims=True) + EPS)
    xh = xv * r
    err = jnp.where(valid, xh * gv - tv, 0.0)
    lpart = 0.5 * jnp.sum(jnp.sum(err * err, axis=-1, keepdims=True) * (1.0 / D), axis=0, keepdims=True)
    dy = err * (1.0 / D)
    dxh = dy * gv
    dh = r * (dxh - xh * jnp.mean(dxh * xh, axis=-1, keepdims=True))
    return lpart, dh, jnp.sum(dy * xh, axis=0, keepdims=True)


def _mlp_fwd_loss(m, wup_g, wdown_g, h1, g_final, tgt):
    nsub = TM_MM // TM_E

    def body(m_ref, wu_ref, wd_ref, h1_ref, g_ref, t_ref, loss_ref, dh_ref, dhb_ref, dg_ref, h2):
        i, j = pl.program_id(0), pl.program_id(1)
        up = jnp.maximum(_dot(m_ref[...], wu_ref[0]), 0.0)
        part = _dot((up * up).astype(BF16), wd_ref[0])

        @pl.when(j == 0)
        def _():
            h2[...] = h1_ref[...] + part

        @pl.when(j > 0)
        def _():
            h2[...] += part

        @pl.when(j == NDEV - 1)
        def _():
            lsum = jnp.zeros((1, 1), F32)
            gsum = jnp.zeros((1, D), F32)
            for s in range(nsub):
                rows = slice(s * TM_E, (s + 1) * TM_E)
                lpart, dh, gpart = _loss_rows(h2[rows, :], g_ref[...], t_ref[rows, :], i * TM_MM + s * TM_E)
                dh_ref[rows, :] = dh
                dhb_ref[rows, :] = dh.astype(BF16)
                lsum = lsum + lpart
                gsum = gsum + gpart
            lsum = jnp.broadcast_to(lsum, (1, 128))

            @pl.when(i == 0)
            def _():
                loss_ref[...] = lsum
                dg_ref[...] = gsum

            @pl.when(i > 0)
            def _():
                loss_ref[...] += lsum
                dg_ref[...] += gsum

    blk = pl.BlockSpec((TM_MM, D), lambda i, j: (i, 0))
    vec = pl.BlockSpec((1, D), lambda i, j: (0, 0))
    return pl.pallas_call(
        body, name="mlp_fwd_loss", grid=(T // TM_MM, NDEV),
        in_specs=[blk, pl.BlockSpec((1, D, FF_B), lambda i, j: (j, 0, 0)), pl.BlockSpec((1, FF_B, D), lambda i, j: (j, 0, 0)),
                  blk, vec, blk],
        out_specs=(pl.BlockSpec((1, 128), lambda i, j: (0, 0)), blk, blk, vec),
        out_shape=(_sds((1, 128), F32), _sds((T, D), F32), _sds((T, D), BF16), _sds((1, D), F32)),
        scratch_shapes=[pltpu.VMEM((TM_MM, D), F32)],
        compiler_params=_cp(("arbitrary", "arbitrary"), 56))(m, wup_g, wdown_g, h1, g_final, tgt)


def _mlp_bwd(m, dh2_b, wup_g, wdown_g, h1, g_mlp, dh2):
    ni = T // TM_B
    nsub = TM_B // TM_E

    def body(m_ref, dh_ref, wu_ref, wd_ref, h1_ref, g_ref, dres_ref, dwu_ref, dwd_ref, dh1_ref, dh1b_ref, dg_ref,
             dm_ref, acc_u, acc_d):
        j, i = pl.program_id(0), pl.program_id(1)
        rows = pl.ds(pl.multiple_of(i * TM_B, TM_B), TM_B)
        mv, dh = m_ref[...], dh_ref[...]
        r = jnp.maximum(_dot(mv, wu_ref[0]), 0.0)
        act = (r * r).astype(BF16)
        dact = _dot(dh, wd_ref[0], NT)
        dup = (dact * (2.0 * r)).astype(BF16)
        pd = _dot(act, dh, TN)
        pu = _dot(mv, dup, TN)
        dmv = _dot(dup, wu_ref[0], NT)

        @pl.when(i == 0)
        def _():
            acc_u[...] = pu
            acc_d[...] = pd

        @pl.when(i > 0)
        def _():
            acc_u[...] += pu
            acc_d[...] += pd

        @pl.when(i == ni - 1)
        def _():
            dwu_ref[0] = acc_u[...].astype(BF16)
            dwd_ref[0] = acc_d[...].astype(BF16)

        @pl.when(j == 0)
        def _():
            dm_ref[rows, :] = dmv

        @pl.when(j > 0)
        def _():
            dm_ref[rows, :] += dmv

        @pl.when(j == NDEV - 1)
        def _():
            gsum = jnp.zeros((1, D), F32)
            for s in range(nsub):
                sub = slice(s * TM_E, (s + 1) * TM_E)
                dm_rows = dm_ref[pl.ds(pl.multiple_of(i * TM_B + s * TM_E, TM_E), TM_E), :]
                dx, gpart = _norm_bwd_rows(h1_ref[sub, :], g_ref[...], dm_rows, dres_ref[sub, :])
                dh1_ref[sub, :] = dx
                dh1b_ref[sub, :] = dx.astype(BF16)
                gsum = gsum + gpart

            @pl.when(i == 0)
            def _():
                dg_ref[...] = gsum

            @pl.when(i > 0)
            def _():
                dg_ref[...] += gsum

    blk = pl.BlockSpec((TM_B, D), lambda j, i: (i, 0))
    late = pl.BlockSpec((TM_B, D), lambda j, i: (jnp.where(j == NDEV - 1, i, 0), 0))
    vec = pl.BlockSpec((1, D), lambda j, i: (0, 0))
    wus = pl.BlockSpec((1, D, FF_B), lambda j, i: (j, 0, 0))
    wds = pl.BlockSpec((1, FF_B, D), lambda j, i: (j, 0, 0))
    return pl.pallas_call(
        body, name="mlp_bwd", grid=(NDEV, ni), in_specs=[blk, blk, wus, wds, late, vec, late],
        out_specs=(wus, wds, late, late, vec),
        out_shape=(_sds((NDEV, D, FF_B), BF16), _sds((NDEV, FF_B, D), BF16), _sds((T, D), F32), _sds((T, D), BF16),
                   _sds((1, D), F32)),
        scratch_shapes=[pltpu.VMEM((T, D), F32), pltpu.VMEM((D, FF_B), F32), pltpu.VMEM((FF_B, D), F32)],
        compiler_params=_cp(("arbitrary", "arbitrary"), 56))(m, dh2_b, wup_g, wdown_g, h1, g_mlp, dh2)


def _adamw(parts, w, m, v, name):
    rr, cc = w.shape
    nslot = parts.shape[0]
    tr = rr
    for cand in (256, 128, 64):
        if rr % cand == 0 and rr > cand:
            tr = cand
            break
    c1 = 1.0 - ADAM_B1 ** ADAM_STEP
    c2 = 1.0 - ADAM_B2 ** ADAM_STEP

    def body(p_ref, w_ref, m_ref, v_ref, g_ref, d_ref, nm_ref, nv_ref):
        g = p_ref[0].astype(F32)
        for s in range(1, nslot):
            g = g + p_ref[s].astype(F32)
        mn = ADAM_B1 * m_ref[...] + (1.0 - ADAM_B1) * g
        vn = ADAM_B2 * v_ref[...] + (1.0 - ADAM_B2) * (g * g)
        g_ref[...] = g
        nm_ref[...] = mn
        nv_ref[...] = vn
        d_ref[...] = -ADAM_LR * ((mn / c1) / (jnp.sqrt(vn / c2) + ADAM_EPS) + ADAM_WD * w_ref[...])

    blk = pl.BlockSpec((tr, cc), lambda i: (i, 0))
    return pl.pallas_call(
        body, name=name, grid=(rr // tr,),
        in_specs=[pl.BlockSpec((nslot, tr, cc), lambda i: (0, i, 0)), blk, blk, blk],
        out_specs=(blk,) * 4, out_shape=(_sds((rr, cc), F32),) * 4,
        compiler_params=_cp(("parallel",)))(parts, w, m, v)


RPB_N = NA_HEADS * 15 * 31
RPB_PAD = 4096
OWN_ROWS = NM + 8


def _pad_rows(a, rows):
    return jnp.pad(a, ((0, rows - a.shape[0]),) + ((0, 0),) * (a.ndim - 1))


def _pack_owned(meta_blk, lb_blk):
    return jnp.concatenate([meta_blk, _pad_rows(lb_blk.reshape(2, 128), 8)], axis=0)


LOSS_ROW = 28


def _pack_replicated(n_mix, n_mlp, n_final, hg_gain, rpb, loss_row=None):
    flat = _pad_rows(rpb.reshape(RPB_N), RPB_PAD)
    gain8 = _pad_rows(hg_gain.reshape(4, 128), 8)
    if loss_row is not None:
        gain8 = gain8 + jnp.pad(loss_row, ((LOSS_ROW - 24, 31 - LOSS_ROW), (0, 0)))
    return jnp.concatenate([n_mix.reshape(8, 128), n_mlp.reshape(8, 128), n_final.reshape(8, 128), gain8,
                            flat.reshape(32, 128)], axis=0)


def _unpack_replicated(a):
    return (a[0:8].reshape(1, D), a[8:16].reshape(1, D), a[16:24].reshape(D), a[24:28].reshape(1, 512),
            a[32:64].reshape(RPB_PAD)[:RPB_N].reshape(1, NA_HEADS, 15, 31))


def kernel(x, meta_tokens, w_in, w_na_out, w_hg_out, w_o, w_up, w_down, norm_mix, norm_mlp, norm_final, hg_norm, na_rpb, hg_lb_logits, loss_target, m_meta_tokens, m_w_in, m_w_na_out, m_w_hg_out, m_w_o, m_w_up, m_w_down, m_norm_mix, m_norm_mlp, m_norm_final, m_hg_norm, m_na_rpb, m_hg_lb_logits, v_meta_tokens, v_w_in, v_w_na_out, v_w_hg_out, v_w_o, v_w_up, v_w_down, v_norm_mix, v_norm_mlp, v_norm_final, v_hg_norm, v_na_rpb, v_hg_lb_logits):
    owned = _pack_owned(meta_tokens, hg_lb_logits)
    first_masks = (ALL_PEERS, SAME_CORE_AND_SIBLING)
    first, tok = _exchange_start([owned, w_in[0].astype(BF16)], [False] * 2, "gather_first_start", first_masks)
    bias_tab = _na_bias_table(_tie(jnp.pad(na_rpb[0], ((0, 0), (0, 0), (0, 128 - 31))), tok, "tie_bias_table"))
    later = [w[0].astype(BF16) for w in (w_na_out, w_hg_out, w_o, w_up, w_down)]
    lead = jnp.zeros((NM, D), F32) + tok[0, 0]
    h0_rows = jnp.concatenate([lead, x[0], jnp.zeros((T - L, D), F32)], axis=0)
    tgt = jnp.concatenate([lead, loss_target[0], jnp.zeros((T - L, D), F32)], axis=0)
    (owned_g, _), first = _exchange_wait(first, [False] * 2, [h0_rows], "gather_small_wait", first_masks, which=(0,))
    meta_full = jnp.transpose(owned_g[:, 0:NM, :], (1, 0, 2)).reshape(NM, D)
    logits = jnp.transpose(owned_g[:, NM:NM + 2, :].reshape(NDEV, 2, 2, 64), (1, 2, 0, 3)).reshape(2, 2, 512)
    h0 = lax.dynamic_update_slice(h0_rows, meta_full, (0, 0))
    a, a_t = _norm_fwd_t(h0, norm_mix, "norm_mix_fwd")
    (_, win_l), _ = _exchange_wait(first, [False] * 2, [a, logits, tgt, bias_tab] + later, "gather_first_wait", first_masks,
                                   which=(1,))
    (win_g,) = _forward_to_sibling([win_l], "gather_first_forward")
    later[0] = _tie(later[0], win_g, "tie_gather_rest")
    gather_rest, tok = _exchange_start(later, [False] * 5, "gather_rest_start")
    win_g = _tie(win_g, tok, "tie_inproj")

    p_act = _inproj_fwd(a, win_g)
    o_na, lse = _na_fwd(p_act, bias_tab)
    qh, k_f, b_f, k_b, b_b = _hg_pre(p_act, logits)
    o_f, st_f = _hg_scan_fwd(qh, k_f, b_f, p_act, False)
    o_b, st_b = _hg_scan_fwd(qh, k_b, b_b, p_act, True)
    (wna_g, whg_g, wo_g, _, _), gather_rest = _exchange_wait(
        gather_rest, [False] * 5, [o_f, o_b, o_na], "gather_rest_wait_a", which=(0, 1, 2))
    w_na_full = jnp.transpose(wna_g, (1, 0, 2)).reshape(512, D)
    w_hg_full = jnp.transpose(whg_g, (1, 0, 2)).reshape(512, D)
    mix, u_hg = _mix_fwd(o_na, o_f, o_b, hg_norm, w_na_full, w_hg_full, p_act)
    h1, m_act = _wo_fwd(mix, wo_g.reshape(D, D), h0, norm_mlp)
    (_, _, wo_g, wup_g, wdown_g), _ = _exchange_wait(gather_rest, [False] * 5, [m_act], "gather_rest_wait_b", which=(3, 4))
    w_o_full = wo_g.reshape(D, D)
    loss_part, dh2, dh2_b, d_nfinal = _mlp_fwd_loss(m_act, wup_g, wdown_g, h1, norm_final.reshape(1, D), tgt)

    dwup_p, dwdown_p, dh1, dh1_b, d_nmlp = _mlp_bwd(m_act, dh2_b, wup_g, wdown_g, h1, norm_mlp, dh2)
    sc_mlp, tok = _exchange_start([dwup_p, dwdown_p], [True] * 2, "scatter_mlp_start")
    dmix, dwo = _wo_bwd(_tie(dh1_b, tok, "tie_wo_bwd"), w_o_full, mix)
    sc_wo, tok = _exchange_start([dwo.reshape(NDEV, D // NDEV, D)], [True], "scatter_wo_start")
    dp_rest, dwna, dwhg, do_na, do_hg, d_gain = _mix_bwd(
        o_na, u_hg, o_f, o_b, hg_norm, w_na_full, w_hg_full, p_act, _tie(dmix, tok, "tie_mix_bwd"))
    owner_cols = lambda w: jnp.transpose(w.reshape(512, NDEV, D // NDEV), (1, 0, 2))
    sc_br, tok = _exchange_start([owner_cols(dwna), owner_cols(dwhg)], [True] * 2, "scatter_branch_start")
    do_hg = _tie(do_hg, tok, "tie_hg_scan_bwd")
    dq_f, dk_f, db_f, dv_f = _hg_scan_bwd(qh, k_f, b_f, p_act, st_f, do_hg, False)
    dq_b, dk_b, db_b, dv_b = _hg_scan_bwd(qh, k_b, b_b, p_act, st_b, do_hg, True)
    dp_rest, d_logits = _hg_pre_bwd(p_act, logits, dq_f, dq_b, dk_f, dk_b, db_f, db_b, dv_f, dv_b, dp_rest)
    dq_na, dk_na, dv_na, dbias = _na_bwd(p_act, do_na, lse, bias_tab)
    dp_na = jnp.concatenate([dq_na.astype(BF16), dk_na.astype(BF16), dv_na.astype(BF16)], axis=1)
    dwin_p = _inproj_bwd_dw(a_t, dp_na, dp_rest)
    far_mine, far_other = _far_slots()
    dwin_p = _add_into_slot(dwin_p, _sibling_swap_far(dwin_p, "pair_swap_in"), far_mine, "pair_add_in")
    sc_in, tok = _exchange_start([dwin_p], [True], "scatter_in_start", ALL_BUT_FAR_OTHER_CORE, absent=far_other)
    dh0, d_nmix = _inproj_bwd_da(_tie(dp_na, tok, "tie_inproj_bwd_da"), dp_rest, win_g, h0, norm_mix, dh1)
    d_rpb = _na_rpb_reduce(_tie(dbias, tok, "tie_rpb_reduce"))[:, :, :31]

    res = {}

    def update(nm, parts, w, mm, vv):
        res[nm] = [r[None] for r in _adamw(parts, w[0], mm[0], vv[0], "adamw_" + nm)]
        return res[nm][1]

    wup_r, wdown_r = _exchange_wait(sc_mlp, [True] * 2, [dh0, d_rpb], "scatter_mlp_wait")
    update("w_up", wup_r, w_up, m_w_up, v_w_up)
    last = update("w_down", wdown_r, w_down, m_w_down, v_w_down)
    (wo_r,) = _exchange_wait(sc_wo, [True], [last], "scatter_wo_wait")
    last = update("w_o", wo_r, w_o, m_w_o, v_w_o)
    wna_r, whg_r = _exchange_wait(sc_br, [True] * 2, [last], "scatter_branch_wait")
    update("w_na_out", wna_r, w_na_out, m_w_na_out, v_w_na_out)
    last = update("w_hg_out", whg_r, w_hg_out, m_w_hg_out, v_w_hg_out)

    d_meta = jnp.transpose(dh0[0:NM].reshape(NM, NDEV, 128), (1, 0, 2))
    d_lg = jnp.transpose(d_logits.reshape(2, 2, NDEV, 64), (2, 0, 1, 3)).reshape(NDEV, 2, 128)
    owned_p = jnp.concatenate([d_meta, jnp.pad(d_lg, ((0, 0), (0, OWN_ROWS - NM - 2), (0, 0)))], axis=1)
    repl_p = _pack_replicated(d_nmix, d_nmlp, d_nfinal, d_gain, d_rpb, loss_part)
    grad_x = dh0[NM:L][None]
    done_first = [grad_x] + [res[nm][0] for nm in ("w_up", "w_down", "w_o", "w_na_out", "w_hg_out")]
    owned_r, repl_r = _exchange([owned_p, repl_p], [True, False], "scatter_small", done_first)
    own = _adamw(owned_r, owned, _pack_owned(m_meta_tokens, m_hg_lb_logits), _pack_owned(v_meta_tokens, v_hg_lb_logits),
                 "adamw_owned_small")
    res["meta_tokens"] = [r[0:NM] for r in own]
    res["hg_lb_logits"] = [r[NM:NM + 2].reshape(2, 2, 64) for r in own]
    rep = _adamw(repl_r, _pack_replicated(norm_mix, norm_mlp, norm_final, hg_norm, na_rpb),
                 _pack_replicated(m_norm_mix, m_norm_mlp, m_norm_final, m_hg_norm, m_na_rpb),
                 _pack_replicated(v_norm_mix, v_norm_mlp, v_norm_final, v_hg_norm, v_na_rpb), "adamw_replicated")
    for q in range(4):
        um = _unpack_replicated(rep[q])
        for nm, val in zip(("norm_mix", "norm_mlp", "norm_final", "hg_norm", "na_rpb"), um):
            res.setdefault(nm, [None] * 4)[q] = val
    (win_r,) = _exchange_wait(sc_in, [True], [rep[1], own[1]], "scatter_in_wait", ALL_BUT_FAR_OTHER_CORE)
    update("w_in", win_r, w_in, m_w_in, v_w_in)

    loss = jnp.sum(repl_r[:, LOSS_ROW, 0])
    order = ("meta_tokens", "w_in", "w_na_out", "w_hg_out", "w_o", "w_up", "w_down", "norm_mix", "norm_mlp", "norm_final",
             "hg_norm", "na_rpb", "hg_lb_logits")
    outs = [loss, grad_x]
    for q in range(4):
        outs += [res[nm][q] for nm in order]
    return tuple(outs)
```

```python
import functools

import numpy as np
import jax
import jax.numpy as jnp
from jax import lax
from jax.experimental import pallas as pl
from jax.experimental.pallas import tpu as pltpu

F32 = jnp.float32
BF16 = jnp.bfloat16

D = 1024
SEQ = 2048
NM = 16
L = SEQ + NM
T = 2176
NDEV = 8
EPS = 1e-6
GRID_W = 64
ROWS = SEQ // GRID_W
NA_HEADS = 8
NA_DH = 64
NA_SCALE = NA_DH ** -0.5
HG_HEADS = 4
HG_C = 16
NCHUNK = L // HG_C
D_FF = 4096
IN_COLS = 6144
NEG = -1e30

ADAM_LR = 0.001
ADAM_B1 = 0.9
ADAM_B2 = 0.999
ADAM_EPS = 1e-08
ADAM_WD = 0.01
ADAM_STEP = 10

MESH_ID = pl.DeviceIdType.MESH
ANY = pl.BlockSpec(memory_space=pl.ANY)

NN = (((1,), (0,)), ((), ()))
NT = (((1,), (1,)), ((), ()))
TN = (((0,), (0,)), ((), ()))


def _cp(sem=None, vmem_mb=48):
    return pltpu.CompilerParams(dimension_semantics=sem, vmem_limit_bytes=vmem_mb * 1024 * 1024)


def _dot(a, b, dims=NN):
    return lax.dot_general(a, b, dims, preferred_element_type=F32)


def _sds(shape, dtype):
    return jax.ShapeDtypeStruct(shape, dtype)


def _in_hbm(*xs):
    return tuple(pltpu.with_memory_space_constraint(x, pltpu.HBM) for x in xs)


HBM = pl.BlockSpec(memory_space=pltpu.HBM)
SEM = pl.BlockSpec(memory_space=pltpu.SEMAPHORE)
EFFECT = pltpu.SideEffectType.DATAFLOW_SIDE_EFFECTING


def _exchange(arrs, scatter, name, after=()):
    n = len(arrs)
    after = list(after)
    out_shapes = []
    for a, sc in zip(arrs, scatter):
        out_shapes.append(_sds(a.shape if sc else (NDEV,) + a.shape, a.dtype))

    def body(*refs):
        ins, outs = refs[:n], refs[n + len(after):2 * n + len(after)]
        send_sems, recv_sems, loc_sems = refs[2 * n + len(after):]
        me = 4 * lax.axis_index("x") + 2 * lax.axis_index("y") + lax.axis_index("c")
        copies = []
        for k in range(n):
            src_me = ins[k].at[me] if scatter[k] else ins[k]
            loc = pltpu.make_async_copy(src_me, outs[k].at[me], loc_sems.at[k])
            loc.start()
            copies.append(loc)
        remote = sum(_peer_copies(ins, outs, scatter, send_sems, recv_sems), [])
        for cp in remote:
            cp.start()
        for cp in remote:
            cp.wait_recv()
        for cp in remote:
            cp.wait_send()
        for cp in copies:
            cp.wait()

    return pl.pallas_call(
        body, name=name, out_shape=tuple(out_shapes), in_specs=[ANY] * (n + len(after)), out_specs=tuple([ANY] * n),
        scratch_shapes=[pltpu.SemaphoreType.DMA((n * (NDEV - 1),)), pltpu.SemaphoreType.DMA((n * (NDEV - 1),)),
                        pltpu.SemaphoreType.DMA((n,))],
    )(*arrs, *after)


def _forward_to_sibling(bufs, name):
    n = len(bufs)

    def body(*refs):
        ins, outs = refs[:n], refs[n:2 * n]
        send_sems, recv_sems = refs[2 * n:]
        x, y, c = lax.axis_index("x"), lax.axis_index("y"), lax.axis_index("c")
        copies = []
        for k in range(n):
            for j, (cx, cy) in enumerate(((1 - x, y), (x, 1 - y), (1 - x, 1 - y))):
                slot = 4 * cx + 2 * cy + c
                copies.append(pltpu.make_async_remote_copy(
                    src_ref=ins[k].at[slot], dst_ref=outs[k].at[slot], send_sem=send_sems.at[3 * k + j],
                    recv_sem=recv_sems.at[3 * k + j], device_id=(x, y, 1 - c), device_id_type=MESH_ID))
        for cp in copies:
            cp.start()
        for cp in copies:
            cp.wait_recv()
        for cp in copies:
            cp.wait_send()

    return pl.pallas_call(
        body, name=name, out_shape=tuple(_sds(b.shape, b.dtype) for b in bufs), in_specs=[ANY] * n,
        out_specs=tuple([ANY] * n), input_output_aliases={k: k for k in range(n)},
        scratch_shapes=[pltpu.SemaphoreType.DMA((3 * n,)), pltpu.SemaphoreType.DMA((3 * n,))],
    )(*bufs)


ALL_PEERS = tuple(range(1, NDEV))
SAME_CORE_AND_SIBLING = (1, 2, 4, 6)
ALL_BUT_FAR_OTHER_CORE = (1, 2, 3, 4, 5, 6)


def _far_slots():
    far = 4 * (1 - lax.axis_index("x")) + 2 * (1 - lax.axis_index("y"))
    core = lax.axis_index("c")
    return far + core, far + 1 - core


def _sibling_swap_far(parts, name):
    def body(x_ref, o_ref, send_sem, recv_sem):
        sib = (lax.axis_index("x"), lax.axis_index("y"), 1 - lax.axis_index("c"))
        cp = pltpu.make_async_remote_copy(src_ref=x_ref.at[_far_slots()[1]], dst_ref=o_ref.at[0], send_sem=send_sem,
                                          recv_sem=recv_sem, device_id=sib, device_id_type=MESH_ID)
        cp.start()
        cp.wait()

    return pl.pallas_call(
        body, name=name, out_shape=_sds((1,) + parts.shape[1:], parts.dtype), in_specs=[ANY], out_specs=ANY,
        scratch_shapes=[pltpu.SemaphoreType.DMA(()), pltpu.SemaphoreType.DMA(())])(parts)


def _add_into_slot(parts, other, slot, name):
    _, rr, cc = parts.shape

    def body(slot_ref, p_ref, o_ref, out_ref):
        del slot_ref
        out_ref[...] = (p_ref[...].astype(F32) + o_ref[...].astype(F32)).astype(BF16)

    mine = pl.BlockSpec((1, rr // 2, cc), lambda j, s: (s[0], j, 0))
    grid_spec = pltpu.PrefetchScalarGridSpec(
        num_scalar_prefetch=1, grid=(2,),
        in_specs=[mine, pl.BlockSpec((1, rr // 2, cc), lambda j, s: (0, j, 0))], out_specs=mine)
    return pl.pallas_call(body, name=name, grid_spec=grid_spec, out_shape=_sds(parts.shape, BF16),
                          input_output_aliases={1: 0}, compiler_params=_cp(("arbitrary",)))(
                              jnp.reshape(slot, (1,)).astype(jnp.int32), *_in_hbm(parts, other))


def _peer_copies(srcs, lands, scatter, send_sems, recv_sems, masks=ALL_PEERS):
    x, y, c = lax.axis_index("x"), lax.axis_index("y"), lax.axis_index("c")
    me = 4 * x + 2 * y + c
    out = []
    for k in range(len(srcs)):
        out.append([])
        for m in (masks[k] if isinstance(masks[0], tuple) else masks):
            px, py, pc = x ^ (m >> 2), y ^ ((m >> 1) & 1), c ^ (m & 1)
            src = srcs[k].at[4 * px + 2 * py + pc] if scatter[k] else srcs[k]
            out[k].append(pltpu.make_async_remote_copy(
                src_ref=src, dst_ref=lands[k].at[me], send_sem=send_sems.at[k * (NDEV - 1) + m - 1],
                recv_sem=recv_sems.at[k * (NDEV - 1) + m - 1],
                device_id=(px, py, pc), device_id_type=MESH_ID))
    return out


def _exchange_start(arrs, scatter, name, masks=ALL_PEERS, absent=None):
    n = len(arrs)
    me = 4 * lax.axis_index("x") + 2 * lax.axis_index("y") + lax.axis_index("c")
    lands = []
    for a, sc in zip(arrs, scatter):
        own = lax.dynamic_index_in_dim(a, me, 0, keepdims=True) if sc else a[None]
        shape = a.shape if sc else (NDEV,) + a.shape
        land = lax.dynamic_update_index_in_dim(lax.empty(shape, a.dtype), own, me, 0)
        if absent is not None:
            land = lax.dynamic_update_index_in_dim(land, jnp.zeros_like(own), absent, 0)
        lands.append(land)

    def body(*refs):
        srcs, lnds = refs[:n], refs[n:2 * n]
        send_sems, recv_sems = refs[2 * n], refs[2 * n + 1]
        token = refs[-1]
        for cp in sum(_peer_copies(srcs, lnds, scatter, send_sems, recv_sems, masks), []):
            cp.start()
        token[...] = jnp.zeros_like(token)

    ops = [pltpu.with_memory_space_constraint(a, pltpu.HBM) for a in list(arrs) + lands]
    res = pl.pallas_call(
        body, name=name,
        out_shape=(pltpu.SemaphoreType.DMA((n * (NDEV - 1),)), pltpu.SemaphoreType.DMA((n * (NDEV - 1),)))
        + tuple(pltpu.HBM(o.shape, o.dtype) for o in ops) + (_sds((8, 128), F32),),
        in_specs=[HBM] * (2 * n), out_specs=(SEM, SEM) + (HBM,) * (2 * n) + (pl.BlockSpec(memory_space=pltpu.VMEM),),
        input_output_aliases={k: 2 + k for k in range(2 * n)},
        compiler_params=pltpu.CompilerParams(has_side_effects=EFFECT),
    )(*ops)
    return res[:-1], res[-1]


def _exchange_wait(handle, scatter, after, name, masks=ALL_PEERS, which=None):
    send_sems, recv_sems = handle[0], handle[1]
    bufs = handle[2:]
    n = len(bufs) // 2
    after = list(after)

    def body(*refs):
        srcs, lnds = refs[:n], refs[n:2 * n]
        copies = _peer_copies(srcs, lnds, scatter, refs[2 * n], refs[2 * n + 1], masks)
        for k in (range(n) if which is None else which):
            for cp in copies[k]:
                cp.wait_send()
                cp.wait_recv()

    res = pl.pallas_call(
        body, name=name, out_shape=tuple(pltpu.HBM(b.shape, b.dtype) for b in bufs),
        in_specs=[HBM] * (2 * n) + [SEM, SEM] + [ANY] * len(after), out_specs=(HBM,) * (2 * n),
        input_output_aliases={k: k for k in range(2 * n)},
        compiler_params=pltpu.CompilerParams(has_side_effects=EFFECT),
    )(*bufs, send_sems, recv_sems, *after)
    return res[n:] if which is None else (res[n:], (send_sems, recv_sems) + tuple(res))


def _tie(x, token, name):
    def body(x_ref, t_ref, o_ref):
        del x_ref, t_ref, o_ref

    return pl.pallas_call(body, name=name, out_shape=_sds(x.shape, x.dtype), in_specs=[ANY, ANY], out_specs=ANY,
                          input_output_aliases={0: 0})(x, token)


TM_E = 272


def _norm_fwd_t(h, g, name):
    def body(h_ref, g_ref, o_ref, ot_ref):
        xv = h_ref[...]
        r = lax.rsqrt(jnp.mean(xv * xv, axis=-1, keepdims=True) + EPS)
        y = xv * r * g_ref[...]
        o_ref[...] = y.astype(BF16)
        ot_ref[...] = y.T.astype(BF16)

    return pl.pallas_call(
        body, name=name, grid=(T // 128,),
        in_specs=[pl.BlockSpec((128, D), lambda i: (i, 0)), pl.BlockSpec((1, D), lambda i: (0, 0))],
        out_specs=(pl.BlockSpec((128, D), lambda i: (i, 0)), pl.BlockSpec((D, 128), lambda i: (0, i))),
        out_shape=(_sds((T, D), BF16), _sds((D, T), BF16)), compiler_params=_cp(("parallel",)))(h, g)


def _norm_bwd_rows(xv, gv, dnv, dres):
    r = lax.rsqrt(jnp.mean(xv * xv, axis=-1, keepdims=True) + EPS)
    xh = xv * r
    dxh = dnv * gv
    dx = dres + r * (dxh - xh * jnp.mean(dxh * xh, axis=-1, keepdims=True))
    return dx, jnp.sum(dnv * xh, axis=0, keepdims=True)


TM_MM = 1088


def _inproj_fwd(a, w_g):
    nb = w_g.shape[2]

    def body(a_ref, w_ref, o_ref):
        o_ref[...] = _dot(a_ref[...], w_ref[0])

    return pl.pallas_call(
        body, name="inproj_fwd", grid=(T // TM_MM, NDEV),
        in_specs=[pl.BlockSpec((TM_MM, D), lambda i, j: (i, 0)), pl.BlockSpec((1, D, nb), lambda i, j: (j, 0, 0))],
        out_specs=pl.BlockSpec((TM_MM, nb), lambda i, j: (i, j)), out_shape=_sds((T, NDEV * nb), F32),
        compiler_params=_cp(("parallel", "parallel")))(a, w_g)


TM_B = 544


W_IN_B = IN_COLS // NDEV


NA_BLKS = 1536 // W_IN_B


def _dp_specs(rows, row_index):
    return [pl.BlockSpec((rows, W_IN_B), lambda *g: (row_index(*g), jnp.minimum(g[-1], NA_BLKS - 1))),
            pl.BlockSpec((rows, W_IN_B), lambda *g: (row_index(*g), jnp.maximum(g[-1] - NA_BLKS, 0)))]


def _inproj_bwd_dw(a_t, dp_na, dp_rest):
    def body(at_ref, na_ref, rest_ref, dw_ref):
        j = pl.program_id(0)

        @pl.when(j < NA_BLKS)
        def _():
            dw_ref[0] = _dot(at_ref[...], na_ref[...]).astype(BF16)

        @pl.when(j >= NA_BLKS)
        def _():
            dw_ref[0] = _dot(at_ref[...], rest_ref[...]).astype(BF16)

    return pl.pallas_call(
        body, name="inproj_bwd_dw", grid=(NDEV,),
        in_specs=[pl.BlockSpec((D, T), lambda j: (0, 0))] + _dp_specs(T, lambda j: 0),
        out_specs=pl.BlockSpec((1, D, W_IN_B), lambda j: (j, 0, 0)), out_shape=_sds((NDEV, D, W_IN_B), BF16),
        compiler_params=_cp(("parallel",)))(a_t, dp_na, dp_rest)


def _inproj_bwd_da(dp_na, dp_rest, w_g, h0, g_mix, dh1):
    nsub = TM_MM // TM_E

    def body(na_ref, rest_ref, w_ref, h0_ref, g_ref, dres_ref, dh0_ref, dg_ref, da):
        i, j = pl.program_id(0), pl.program_id(1)
        dpv = jnp.where(j < NA_BLKS, na_ref[...], rest_ref[...])
        dav = _dot(dpv, w_ref[0], NT)

        @pl.when(j == 0)
        def _():
            da[...] = dav

        @pl.when(j > 0)
        def _():
            da[...] += dav

        @pl.when(j == NDEV - 1)
        def _():
            gsum = jnp.zeros((1, D), F32)
            for s in range(nsub):
                sub = slice(s * TM_E, (s + 1) * TM_E)
                dx, gpart = _norm_bwd_rows(h0_ref[sub, :], g_ref[...], da[sub, :], dres_ref[sub, :])
                dh0_ref[sub, :] = dx
                gsum = gsum + gpart

            @pl.when(i == 0)
            def _():
                dg_ref[...] = gsum

            @pl.when(i > 0)
            def _():
                dg_ref[...] += gsum

    rblk = pl.BlockSpec((TM_MM, D), lambda i, j: (i, 0))
    vec = pl.BlockSpec((1, D), lambda i, j: (0, 0))
    return pl.pallas_call(
        body, name="inproj_bwd_da", grid=(T // TM_MM, NDEV),
        in_specs=_dp_specs(TM_MM, lambda i, j: i) + [pl.BlockSpec((1, D, W_IN_B), lambda i, j: (j, 0, 0)), rblk, vec, rblk],
        out_specs=(rblk, vec), out_shape=(_sds((T, D), F32), _sds((1, D), F32)),
        scratch_shapes=[pltpu.VMEM((TM_MM, D), F32)],
        compiler_params=_cp(("arbitrary", "arbitrary"), 56))(dp_na, dp_rest, w_g, h0, g_mix, dh1)


NA_QB = 256
NA_GROUPS = ROWS // 4
NA_UROWS = 11
NA_KW = NA_UROWS * GRID_W
NA_KU = 768


def _na_row_offset(var, i, j):
    valid = (j < 8, i <= j < i + 8, 3 <= j < NA_UROWS)[var]
    return (j - i + (7, 3, 0)[var]) if valid else None


def _na_bias_table(rp):
    def body(r_ref, o_ref):
        row3 = lax.broadcasted_iota(jnp.int32, (15, GRID_W, 128), 1)
        lane3 = lax.broadcasted_iota(jnp.int32, (15, GRID_W, 128), 2)
        w3 = lane3 & (GRID_W - 1)
        cs3 = jnp.clip(row3 - 8, 0, GRID_W - 16)
        lane = lax.broadcasted_iota(jnp.int32, (GRID_W, 128), 1)
        neg = jnp.full((GRID_W, 128), NEG, F32)
        z = jnp.stack([jnp.broadcast_to(r_ref[0, a:a + 1, :], (GRID_W, 128)) for a in range(15)])
        for bit in range(6):
            sh = 1 << bit
            z = jnp.where((row3 & sh) != 0, jnp.roll(z, sh, axis=2), z)
        z = jnp.roll(z, 128 - 15, axis=2)
        z = jnp.where(lane3 < GRID_W, z, 0.0)
        z = z + jnp.roll(z, GRID_W, axis=2)
        tabs = jnp.where((w3 >= cs3) & (w3 < cs3 + 16), z, NEG)
        tail = jnp.where(lane < GRID_W + NM, 0.0, NEG)
        for var in range(3):
            for i in range(4):
                for jp in range(NA_KU // 128):
                    halves = []
                    for j in (2 * jp, 2 * jp + 1):
                        a = _na_row_offset(var, i, j) if j < NA_UROWS else None
                        halves.append(tail if j >= NA_UROWS else (neg if a is None else tabs[a]))
                    o_ref[var, 0, i * 64:(i + 1) * 64, jp * 128:(jp + 1) * 128] = jnp.where(lane < GRID_W, halves[0], halves[1])

    return pl.pallas_call(
        body, name="na_bias_table", grid=(NA_HEADS,),
        in_specs=[pl.BlockSpec((1, 15, 128), lambda h: (h, 0, 0))],
        out_specs=pl.BlockSpec((3, 1, NA_QB, NA_KU), lambda h: (0, h, 0, 0)),
        out_shape=_sds((3, NA_HEADS, NA_QB, NA_KU), F32), compiler_params=_cp(("parallel",)))(rp)


def _na_var(g):
    return jnp.where(g == 0, 0, jnp.where(g == NA_GROUPS - 1, 2, 1))


def _na_load_window(src_ref, dst, g):
    us = jnp.clip(4 * g - 4, 0, ROWS - NA_UROWS)
    kstart = pl.multiple_of(NM + GRID_W * us, 16)
    dst[0:NA_KW, :] = src_ref[pl.ds(kstart, NA_KW), :].astype(BF16)
    dst[NA_KW:NA_KW + NM, :] = src_ref[0:NM, :].astype(BF16)
    dst[NA_KW + NM:, :] = jnp.zeros((NA_KU - NA_KW - NM, 128), BF16)
    return kstart


def _na_fwd(p_act, bias_tab):
    def body(q_ref, k_ref, v_ref, b_ref, o_ref, lse_ref, ku, vu):
        g = pl.program_id(1)
        _na_load_window(k_ref, ku, g)
        _na_load_window(v_ref, vu, g)
        qstart = pl.multiple_of(NM + NA_QB * g, 16)
        q = q_ref[pl.ds(qstart, NA_QB), :]
        lane = lax.broadcasted_iota(jnp.int32, (NA_QB, 128), 1)
        o_h, lse_h = [], []
        for h in range(2):
            hm = (lane < 64) if h == 0 else (lane >= 64)
            qm = (jnp.where(hm, q, 0.0) * NA_SCALE).astype(BF16)
            s = _dot(qm, ku[...], NT) + b_ref[0, h]
            m = jnp.max(s, axis=-1, keepdims=True)
            p = jnp.exp(s - m)
            l = jnp.sum(p, axis=-1, keepdims=True)
            o_h.append(_dot(p.astype(BF16), vu[...]) / l)
            lse_h.append(jnp.broadcast_to(m + jnp.log(l), (NA_QB, 128)))
        o_ref[pl.ds(qstart, NA_QB), :] = jnp.where(lane < 64, o_h[0], o_h[1]).astype(BF16)
        lse_ref[0, pl.ds(qstart, NA_QB), :] = jnp.where(lane < 64, lse_h[0], lse_h[1])

        @pl.when(g == 0)
        def _():
            qm_ = q_ref[0:NM, :]
            lane_m = lax.broadcasted_iota(jnp.int32, (NM, 128), 1)
            km, vm = ku[NA_KW:NA_KW + NM, :], vu[NA_KW:NA_KW + NM, :]
            om = []
            for h in range(2):
                hm = (lane_m < 64) if h == 0 else (lane_m >= 64)
                s = _dot(jnp.where(hm, qm_, 0.0).astype(BF16), km, NT) * NA_SCALE
                p = jnp.exp(s - jnp.max(s, axis=-1, keepdims=True))
                l = jnp.sum(p, axis=-1, keepdims=True)
                om.append(_dot(p.astype(BF16), vm) / l)
            o_ref[0:NM, :] = jnp.where(lane_m < 64, om[0], om[1]).astype(BF16)
            o_ref[L:T, :] = jnp.zeros((T - L, 128), BF16)
            lse_ref[0, 0:NM, :] = jnp.zeros((NM, 128), F32)
            lse_ref[0, L:T, :] = jnp.zeros((T - L, 128), F32)

    col = lambda off: pl.BlockSpec((T, 128), lambda hp, g: (0, off + hp))
    return pl.pallas_call(
        body, name="na_fwd", grid=(4, NA_GROUPS),
        in_specs=[col(0), col(4), col(8),
                  pl.BlockSpec((1, 2, NA_QB, NA_KU), lambda hp, g: (_na_var(g), hp, 0, 0))],
        out_specs=(pl.BlockSpec((T, 128), lambda hp, g: (0, hp)), pl.BlockSpec((1, T, 128), lambda hp, g: (hp, 0, 0))),
        out_shape=(_sds((T, 512), BF16), _sds((4, T, 128), F32)),
        scratch_shapes=[pltpu.VMEM((NA_KU, 128), BF16), pltpu.VMEM((NA_KU, 128), BF16)],
        compiler_params=_cp(("parallel", "arbitrary")))(p_act, p_act, p_act, bias_tab)


def _na_bwd(p_act, do, lse, bias_tab):
    def body(q_ref, k_ref, v_ref, do_ref, lse_ref, b_ref, dq_ref, dk_ref, dv_ref, db_ref, ku, vu):
        g = pl.program_id(1)

        @pl.when(g == 0)
        def _():
            dq_ref[...] = jnp.zeros((T, 128), F32)
            dk_ref[...] = jnp.zeros((T, 128), F32)
            dv_ref[...] = jnp.zeros((T, 128), F32)

        kstart = _na_load_window(k_ref, ku, g)
        _na_load_window(v_ref, vu, g)
        qstart = pl.multiple_of(NM + NA_QB * g, 16)
        q = q_ref[pl.ds(qstart, NA_QB), :]
        dov = do_ref[pl.ds(qstart, NA_QB), :]
        lsev = lse_ref[0, pl.ds(qstart, NA_QB), :]
        lane = lax.broadcasted_iota(jnp.int32, (NA_QB, 128), 1)
        first = (g == 0) | (g == 1) | (g == NA_GROUPS - 1)
        dq_h = []
        dku = jnp.zeros((NA_KU, 128), F32)
        dvu = jnp.zeros((NA_KU, 128), F32)
        for h in range(2):
            hm = (lane < 64) if h == 0 else (lane >= 64)
            qm = (jnp.where(hm, q, 0.0) * NA_SCALE).astype(BF16)
            dom = jnp.where(hm, dov, 0.0).astype(BF16)
            s = _dot(qm, ku[...], NT) + b_ref[0, h]
            p = jnp.exp(s - lsev[:, 64 * h:64 * h + 1])
            dp = _dot(dom, vu[...], NT)
            delta = jnp.sum(p * dp, axis=-1, keepdims=True)
            ds = p * (dp - delta)

            @pl.when(first)
            def _():
                db_ref[0, h] = ds

            @pl.when(jnp.logical_not(first))
            def _():
                db_ref[0, h] += ds

            dsb = ds.astype(BF16)
            dq_h.append(_dot(dsb, ku[...]) * NA_SCALE)
            dku = dku + _dot(dsb, qm, TN)
            dvu = dvu + _dot(p.astype(BF16), dom, TN)
        dq_ref[pl.ds(qstart, NA_QB), :] = jnp.where(lane < 64, dq_h[0], dq_h[1])
        dk_ref[pl.ds(kstart, NA_KW), :] += dku[0:NA_KW]
        dv_ref[pl.ds(kstart, NA_KW), :] += dvu[0:NA_KW]
        dk_ref[0:NM, :] += dku[NA_KW:NA_KW + NM]
        dv_ref[0:NM, :] += dvu[NA_KW:NA_KW + NM]

        @pl.when(g == 0)
        def _():
            qm_ = q_ref[0:NM, :]
            dom_ = do_ref[0:NM, :]
            lane_m = lax.broadcasted_iota(jnp.int32, (NM, 128), 1)
            km, vm = ku[NA_KW:NA_KW + NM, :], vu[NA_KW:NA_KW + NM, :]
            dqs = []
            dkm = jnp.zeros((NM, 128), F32)
            dvm = jnp.zeros((NM, 128), F32)
            for h in range(2):
                hm = (lane_m < 64) if h == 0 else (lane_m >= 64)
                qh = jnp.where(hm, qm_, 0.0).astype(BF16)
                doh = jnp.where(hm, dom_, 0.0).astype(BF16)
                s = _dot(qh, km, NT) * NA_SCALE
                e = jnp.exp(s - jnp.max(s, axis=-1, keepdims=True))
                p = e / jnp.sum(e, axis=-1, keepdims=True)
                dp = _dot(doh, vm, NT)
                ds = p * (dp - jnp.sum(p * dp, axis=-1, keepdims=True))
                dsb = (ds * NA_SCALE).astype(BF16)
                dqs.append(_dot(dsb, km))
                dkm = dkm + _dot(dsb, qh, TN)
                dvm = dvm + _dot(p.astype(BF16), doh, TN)
            dq_ref[0:NM, :] = jnp.where(lane_m < 64, dqs[0], dqs[1])
            dk_ref[0:NM, :] += dkm
            dv_ref[0:NM, :] += dvm

    col = lambda off: pl.BlockSpec((T, 128), lambda hp, g: (0, off + hp))
    ocol = pl.BlockSpec((T, 128), lambda hp, g: (0, hp))
    bspec = pl.BlockSpec((1, 2, NA_QB, NA_KU), lambda hp, g: (_na_var(g), hp, 0, 0))
    return pl.pallas_call(
        body, name="na_bwd", grid=(4, NA_GROUPS),
        in_specs=[col(0), col(4), col(8), ocol, pl.BlockSpec((1, T, 128), lambda hp, g: (hp, 0, 0)), bspec],
        out_specs=(ocol, ocol, ocol, bspec),
        out_shape=(_sds((T, 512), F32), _sds((T, 512), F32), _sds((T, 512), F32), _sds((3, NA_HEADS, NA_QB, NA_KU), F32)),
        scratch_shapes=[pltpu.VMEM((NA_KU, 128), BF16), pltpu.VMEM((NA_KU, 128), BF16)],
        compiler_params=_cp(("parallel", "arbitrary")))(p_act, p_act, p_act, do, lse, bias_tab)


def _na_rpb_reduce(dbias):
    def body(db_ref, o_ref):
        lane = lax.broadcasted_iota(jnp.int32, (GRID_W, 128), 1)
        row3 = lax.broadcasted_iota(jnp.int32, (15, GRID_W, 128), 1)
        lane3 = lax.broadcasted_iota(jnp.int32, (15, GRID_W, 128), 2)
        accs = []
        for a in range(15):
            acc = jnp.zeros((GRID_W, 128), F32)
            for var in range(3):
                for i in range(4):
                    for j in range(NA_UROWS):
                        if _na_row_offset(var, i, j) == a:
                            pair = db_ref[var, 0, i * 64:(i + 1) * 64, (j // 2) * 128:(j // 2 + 1) * 128]
                            acc = acc + jnp.where((lane < GRID_W) if j % 2 == 0 else (lane >= GRID_W), pair, 0.0)
            accs.append(acc)
        z = jnp.stack(accs)
        z = jnp.where(lane3 < GRID_W, z + jnp.roll(z, GRID_W, axis=2), 0.0)
        for bit in range(6):
            sh = 1 << bit
            z = jnp.where((row3 & sh) != 0, jnp.roll(z, 128 - sh, axis=2), z)
        z = jnp.roll(z, 15, axis=2)
        o_ref[0] = jnp.sum(z, axis=1)

    return pl.pallas_call(
        body, name="na_rpb_reduce", grid=(NA_HEADS,),
        in_specs=[pl.BlockSpec((3, 1, NA_QB, NA_KU), lambda h: (0, h, 0, 0))],
        out_specs=pl.BlockSpec((1, 15, 128), lambda h: (h, 0, 0)), out_shape=_sds((NA_HEADS, 15, 128), F32),
        compiler_params=_cp(("parallel",)))(dbias)


HG_RB = 128
HG_NB = T // HG_RB
HG_SLOTS = HG_NB * 8
HI = lax.Precision.HIGHEST
HG_UNROLL = 4
HG_UNROLL_WIDE = 8


def _chunk_tri(lower):
    r = lax.broadcasted_iota(jnp.int32, (HG_RB, HG_RB), 0)
    c = lax.broadcasted_iota(jnp.int32, (HG_RB, HG_RB), 1)
    same = (r // HG_C) == (c // HG_C)
    keep = (c <= r) if lower else (c >= r)
    return jnp.where(same & keep, 1.0, 0.0).astype(F32)


def _hg_gate_terms(z, lg):
    dl = lg[0:1, :] - lg[1:2, :]
    log_lb = jax.nn.log_sigmoid(dl)
    log_1mlb = jax.nn.log_sigmoid(-dl)
    yz = log_1mlb + jax.nn.log_sigmoid(z)
    log_f = jnp.logaddexp(log_lb, yz)
    snz = jax.nn.sigmoid(-z)
    k = jnp.exp(log_1mlb) * snz
    w2 = jnp.exp(yz - log_f)
    return log_f, k, snz, w2


def _hg_pre(p_act, logits):
    def body(q_ref, zf_ref, zb_ref, lg_ref, qh_ref, kf_ref, bf_ref, kb_ref, bb_ref):
        qh_ref[...] = jax.nn.silu(q_ref[...])
        lf, kf, _, _ = _hg_gate_terms(zf_ref[...], lg_ref[0])
        kf_ref[...] = kf
        bf_ref[...] = jnp.dot(_chunk_tri(True), lf, precision=HI, preferred_element_type=F32)
        lb_, kb, _, _ = _hg_gate_terms(zb_ref[...], lg_ref[1])
        kb_ref[...] = kb
        bb_ref[...] = jnp.dot(_chunk_tri(False), lb_, precision=HI, preferred_element_type=F32)

    blk = lambda c: pl.BlockSpec((HG_RB, 512), lambda i: (i, c))
    ob = pl.BlockSpec((HG_RB, 512), lambda i: (i, 0))
    return pl.pallas_call(
        body, name="hg_pre", grid=(HG_NB,),
        in_specs=[blk(3), blk(4), blk(5), pl.BlockSpec((2, 2, 512), lambda i: (0, 0, 0))],
        out_specs=(ob,) * 5, out_shape=(_sds((T, 512), F32),) * 5,
        compiler_params=_cp(("parallel",)))(p_act, p_act, p_act, logits)


def _bdot(a, b, ca, cb):
    return lax.dot_general(a.astype(BF16), b.astype(BF16), (((ca,), (cb,)), ((0,), (0,))), preferred_element_type=F32)


HG_S = 8
HG_NS = HG_RB // HG_S


def _lane_sums(xs):
    l_io = lax.broadcasted_iota(jnp.int32, (HG_NS, HG_S, HG_S), 2)
    a = jnp.zeros((HG_NS, HG_S, HG_S), F32)
    for j, x in enumerate(xs):
        a = a + jnp.where(l_io == j, jnp.sum(x, axis=-1, keepdims=True), 0.0)
    return a


def _halves(x):
    y = x.reshape(8, 2, HG_S, x.shape[-1])
    return y[:, 0], y[:, 1]


def _join(first, second):
    return jnp.stack([first, second], axis=1).reshape(HG_RB, first.shape[-1])


def _cross_split(rev, b4):
    b_1, b_2 = _halves(b4)
    if rev:
        r = b_2[:, 0:1, :]
        return jnp.exp(b_1 - r), jnp.exp(r - b_2)
    r = b_1[:, HG_S - 1:HG_S, :]
    return jnp.exp(b_2 - r), jnp.exp(r - b_1)


def _hg_scan_fwd(qh, k, b, p_act, rev):
    anchor = 0 if rev else HG_C - 1

    def body(q_ref, k_ref, b_ref, v_ref, o_ref, st_ref, dsc):
        def phase_a(blk, _):
            rows = pl.ds(pl.multiple_of(blk * HG_RB, HG_RB), HG_RB)
            b3 = b_ref[rows, :].reshape(8, HG_C, 128)
            k3 = k_ref[rows, :].reshape(8, HG_C, 128)
            v3 = v_ref[rows, :].reshape(8, HG_C, 128)
            bl = b3[:, anchor:anchor + 1, :]
            kt = k3 * jnp.exp(bl - b3)
            st_ref[0, pl.ds(pl.multiple_of(blk * 8, 8), 8)] = _bdot(v3, kt, 1, 1)
            dsc[pl.ds(pl.multiple_of(blk * 8, 8), 8), :] = jnp.exp(bl[:, 0, :])
            return 0

        lax.fori_loop(0, HG_NB, phase_a, 0, unroll=HG_UNROLL_WIDE)

        def phase_b(n, carry):
            c = (NCHUNK - 1 - n) if rev else n
            u = st_ref[0, c]
            st_ref[0, c] = carry
            return carry * dsc[pl.ds(c, 1), :] + u

        lax.fori_loop(0, NCHUNK // 3, lambda n3, s: phase_b(3 * n3 + 2, phase_b(3 * n3 + 1, phase_b(3 * n3, s))),
                      jnp.zeros((128, 128), F32))
        for c in range(NCHUNK, HG_SLOTS):
            st_ref[0, c] = jnp.zeros((128, 128), F32)

        t_io = lax.broadcasted_iota(jnp.int32, (HG_NS, HG_S, 128), 1)

        def phase_c(blk, _):
            rows = pl.ds(pl.multiple_of(blk * HG_RB, HG_RB), HG_RB)
            b4 = b_ref[rows, :].reshape(HG_NS, HG_S, 128)
            k4 = k_ref[rows, :].reshape(HG_NS, HG_S, 128)
            q4 = q_ref[rows, :].reshape(HG_NS, HG_S, 128)
            v4 = v_ref[rows, :].reshape(HG_NS, HG_S, 128)
            st = st_ref[0, pl.ds(pl.multiple_of(blk * 8, 8), 8)]
            o = _bdot((q4 * jnp.exp(b4)).reshape(8, HG_C, 128), st, 2, 2).reshape(HG_RB, 128)
            terms = []
            for s in range(HG_S):
                ok = (t_io <= s) if rev else (t_io >= s)
                f = jnp.exp(jnp.where(ok, b4 - b4[:, s:s + 1, :], NEG))
                terms.append(q4 * f * k4[:, s:s + 1, :])
            o_in = _bdot(_lane_sums(terms), v4, 2, 1)
            wq, wk = _cross_split(rev, b4)
            q_1, q_2 = _halves(q4)
            k_1, k_2 = _halves(k4)
            v_1, v_2 = _halves(v4)
            o_1, o_2 = _halves(o_in)
            if rev:
                o_1 = o_1 + _bdot(_bdot(q_1 * wq, k_2 * wk, 2, 2), v_2, 2, 1)
            else:
                o_2 = o_2 + _bdot(_bdot(q_2 * wq, k_1 * wk, 2, 2), v_1, 2, 1)
            o_ref[rows, :] = o + _join(o_1, o_2)
            return 0

        lax.fori_loop(0, HG_NB, phase_c, 0, unroll=HG_UNROLL_WIDE)

    col = pl.BlockSpec((T, 128), lambda h: (0, h))
    return pl.pallas_call(
        body, name="hg_scan_bwd_dir" if rev else "hg_scan_fwd_dir", grid=(HG_HEADS,),
        in_specs=[col, col, col, pl.BlockSpec((T, 128), lambda h: (0, 24 + h))],
        out_specs=(col, pl.BlockSpec((1, HG_SLOTS, 128, 128), lambda h: (h, 0, 0, 0))),
        out_shape=(_sds((T, 512), F32), _sds((HG_HEADS, HG_SLOTS, 128, 128), F32)),
        scratch_shapes=[pltpu.VMEM((HG_SLOTS, 128), F32)],
        compiler_params=_cp(("parallel",), 56))(*_in_hbm(qh, k, b, p_act))


def _hg_scan_bwd(qh, k, b, p_act, st, do, rev):
    anchor = 0 if rev else HG_C - 1

    def body(q_ref, k_ref, b_ref, v_ref, st_ref, do_ref, dq_ref, dk_ref, db_ref, dv_ref, gst, dsc, dbl):
        def phase_a(blk, _):
            rows = pl.ds(pl.multiple_of(blk * HG_RB, HG_RB), HG_RB)
            b3 = b_ref[rows, :].reshape(8, HG_C, 128)
            q3 = q_ref[rows, :].reshape(8, HG_C, 128)
            do3 = do_ref[rows, :].reshape(8, HG_C, 128)
            gst[pl.ds(pl.multiple_of(blk * 8, 8), 8)] = _bdot(do3, q3 * jnp.exp(b3), 1, 1)
            dsc[pl.ds(pl.multiple_of(blk * 8, 8), 8), :] = jnp.exp(b3[:, anchor, :])
            return 0

        lax.fori_loop(0, HG_NB, phase_a, 0, unroll=HG_UNROLL_WIDE)

        def phase_b(n, carry):
            c = n if rev else (NCHUNK - 1 - n)
            w = gst[c]
            gst[c] = carry
            dcv = dsc[pl.ds(c, 1), :]
            dbl[pl.ds(c, 1), :] = dcv * jnp.sum(st_ref[0, c] * carry, axis=0, keepdims=True)
            return carry * dcv + w

        lax.fori_loop(0, NCHUNK // 3, lambda n3, s: phase_b(3 * n3 + 2, phase_b(3 * n3 + 1, phase_b(3 * n3, s))),
                      jnp.zeros((128, 128), F32))
        for c in range(NCHUNK, HG_SLOTS):
            gst[c] = jnp.zeros((128, 128), F32)
            dbl[c:c + 1, :] = jnp.zeros((1, 128), F32)

        t_io = lax.broadcasted_iota(jnp.int32, (HG_NS, HG_S, 128), 1)
        t16 = lax.broadcasted_iota(jnp.int32, (8, HG_C, 128), 1)
        r_io = lax.broadcasted_iota(jnp.int32, (HG_NS, HG_S, HG_S), 1)
        l_io = lax.broadcasted_iota(jnp.int32, (HG_NS, HG_S, HG_S), 2)

        def phase_c(blk, _):
            rows = pl.ds(pl.multiple_of(blk * HG_RB, HG_RB), HG_RB)
            cs = pl.ds(pl.multiple_of(blk * 8, 8), 8)
            b4 = b_ref[rows, :].reshape(HG_NS, HG_S, 128)
            k4 = k_ref[rows, :].reshape(HG_NS, HG_S, 128)
            q4 = q_ref[rows, :].reshape(HG_NS, HG_S, 128)
            v4 = v_ref[rows, :].reshape(HG_NS, HG_S, 128)
            do4 = do_ref[rows, :].reshape(HG_NS, HG_S, 128)
            b3, k3, q3 = (z.reshape(8, HG_C, 128) for z in (b4, k4, q4))
            v3, do3 = v4.reshape(8, HG_C, 128), do4.reshape(8, HG_C, 128)
            s_t = st_ref[0, cs]
            g_t = gst[cs]
            bl = b3[:, anchor:anchor + 1, :]
            ekl = jnp.exp(bl - b3)
            kt = k3 * ekl
            dkt = _bdot(v3, g_t, 2, 1)
            dq = (_bdot(do3, s_t, 2, 1) * jnp.exp(b3)).reshape(HG_NS, HG_S, 128)
            dk = (dkt * ekl).reshape(HG_NS, HG_S, 128)
            dv = _bdot(kt, g_t, 2, 2).reshape(HG_NS, HG_S, 128)
            dbl3 = dbl[cs, :].reshape(8, 1, 128) + jnp.sum(dkt * kt, axis=1, keepdims=True)
            causal = (l_io >= r_io) if rev else (l_io <= r_io)
            da = jnp.where(causal, _bdot(do4, v4, 2, 2), 0.0)
            causal_t = (l_io <= r_io) if rev else (l_io >= r_io)
            dat = jnp.where(causal_t, _bdot(v4, do4, 2, 2), 0.0)
            for s in range(HG_S):
                ok = (t_io <= s) if rev else (t_io >= s)
                f = jnp.exp(jnp.where(ok, b4 - b4[:, s:s + 1, :], NEG))
                dq = dq + da[:, :, s:s + 1] * (f * k4[:, s:s + 1, :])
            terms = []
            for t in range(HG_S):
                ok = (t_io >= t) if rev else (t_io <= t)
                e = jnp.exp(jnp.where(ok, b4[:, t:t + 1, :] - b4, NEG))
                eq = e * q4[:, t:t + 1, :]
                dk = dk + dat[:, :, t:t + 1] * eq
                terms.append(eq * k4)
            dv = dv + _bdot(_lane_sums(terms), do4, 2, 1)
            wq, wk = _cross_split(rev, b4)
            pick = (lambda z: _halves(z)) if rev else (lambda z: _halves(z)[::-1])
            (q_q, _), (_, k_k), (_, v_k), (do_q, _) = pick(q4), pick(k4), pick(v4), pick(do4)
            qx, kx = q_q * wq, k_k * wk
            dq_q = _bdot(_bdot(do_q, v_k, 2, 2), kx, 2, 1) * wq
            dk_k = _bdot(_bdot(v_k, do_q, 2, 2), qx, 2, 1) * wk
            dv_k = _bdot(_bdot(kx, qx, 2, 2), do_q, 2, 1)
            zero = jnp.zeros((8, HG_S, 128), F32)
            place_q = (lambda z: _join(z, zero)) if rev else (lambda z: _join(zero, z))
            place_k = (lambda z: _join(zero, z)) if rev else (lambda z: _join(z, zero))
            dq2 = dq.reshape(HG_RB, 128) + place_q(dq_q)
            dk2 = dk.reshape(HG_RB, 128) + place_k(dk_k)
            dv2 = dv.reshape(HG_RB, 128) + place_k(dv_k)
            dq3, dk3 = dq2.reshape(8, HG_C, 128), dk2.reshape(8, HG_C, 128)
            db = q3 * dq3 - k3 * dk3 + jnp.where(t16 == anchor, dbl3, 0.0)
            dq_ref[rows, :] = dq2
            dk_ref[rows, :] = dk2
            db_ref[rows, :] = db.reshape(HG_RB, 128)
            dv_ref[rows, :] = dv2
            return 0

        lax.fori_loop(0, HG_NB, phase_c, 0, unroll=HG_UNROLL)

    col = pl.BlockSpec((T, 128), lambda h: (0, h))
    return pl.pallas_call(
        body, name="hg_scan_bwd_dir_bwd" if rev else "hg_scan_fwd_dir_bwd", grid=(HG_HEADS,),
        in_specs=[col, col, col, pl.BlockSpec((T, 128), lambda h: (0, 24 + h)),
                  pl.BlockSpec((1, HG_SLOTS, 128, 128), lambda h: (h, 0, 0, 0)), col],
        out_specs=(col,) * 4, out_shape=(_sds((T, 512), F32),) * 4,
        scratch_shapes=[pltpu.VMEM((HG_SLOTS, 128, 128), F32), pltpu.VMEM((HG_SLOTS, 128), F32),
                        pltpu.VMEM((HG_SLOTS, 128), F32)],
        compiler_params=_cp(("parallel",), 56))(*_in_hbm(qh, k, b, p_act, st, do))


def _row_valid(i, tm):
    r = lax.broadcasted_iota(jnp.int32, (tm, 1), 0) + i * tm
    return r < L


def _hg_post_rows(o, gv, gain_v, valid):
    parts = []
    for h in range(HG_HEADS):
        oh = o[:, 128 * h:128 * (h + 1)]
        parts.append(oh * lax.rsqrt(jnp.mean(oh * oh, axis=-1, keepdims=True) + EPS))
    return jnp.where(valid, jnp.concatenate(parts, axis=1) * gain_v * jax.nn.silu(gv), 0.0)


def _hg_post_bwd_rows(du, o, gv, gain_v, valid):
    duv = jnp.where(valid, du, 0.0)
    sig = jax.nn.sigmoid(gv)
    sg = gv * sig
    dn = duv * gain_v * sg
    do_parts, n_parts = [], []
    for h in range(HG_HEADS):
        sl = slice(128 * h, 128 * (h + 1))
        oh = o[:, sl]
        r = lax.rsqrt(jnp.mean(oh * oh, axis=-1, keepdims=True) + EPS)
        nh = oh * r
        dnh = dn[:, sl]
        do_parts.append(r * (dnh - nh * jnp.mean(dnh * nh, axis=-1, keepdims=True)))
        n_parts.append(nh)
    n = jnp.where(valid, jnp.concatenate(n_parts, axis=1), 0.0)
    do = jnp.where(valid, jnp.concatenate(do_parts, axis=1), 0.0)
    dg = duv * n * gain_v * (sig * (1.0 + gv * (1.0 - sig)))
    return do, dg, jnp.sum(duv * n * sg, axis=0, keepdims=True)


def _hg_pre_bwd(p_act, logits, dq_f, dq_b, dk_f, dk_b, db_f, db_b, dv_f, dv_b, dp_rest):
    def body(q_ref, zf_ref, zb_ref, lg_ref, dqf_ref, dqb_ref, dkf_ref, dkb_ref, dbf_ref, dbb_ref, dvf_ref, dvb_ref, _,
             dp_ref, dlg_ref):
        dq_ref, dzf_ref, dzb_ref, di_ref = (dp_ref.at[:, 512 * c:512 * (c + 1)] for c in range(4))
        i = pl.program_id(0)
        valid = _row_valid(i, HG_RB)
        qv = q_ref[...]
        sig = jax.nn.sigmoid(qv)
        dq_ref[...] = jnp.where(valid, (dqf_ref[...] + dqb_ref[...]) * (sig * (1.0 + qv * (1.0 - sig))), 0.0).astype(BF16)
        di_ref[...] = jnp.where(valid, dvf_ref[...] + dvb_ref[...], 0.0).astype(BF16)
        for d, (z_ref, dk_r, db_r, dz_ref) in enumerate(((zf_ref, dkf_ref, dbf_ref, dzf_ref), (zb_ref, dkb_ref, dbb_ref, dzb_ref))):
            lg = lg_ref[d]
            dl = lg[0:1, :] - lg[1:2, :]
            lb = jax.nn.sigmoid(dl)
            one_m_lb = jax.nn.sigmoid(-dl)
            log_f, _, snz, w2 = _hg_gate_terms(z_ref[...], lg)
            dbv = jnp.where(valid, db_r[...], 0.0)
            dkv = jnp.where(valid, dk_r[...], 0.0)
            dlf = jnp.dot(_chunk_tri(d == 1), dbv, precision=HI, preferred_element_type=F32)
            sz = 1.0 - snz
            dz_ref[...] = (dlf * w2 * snz - dkv * one_m_lb * sz * snz).astype(BF16)
            dlb = jnp.sum(dlf * snz * jnp.exp(-log_f) - dkv * snz, axis=0, keepdims=True)
            dl0 = dlb * lb * one_m_lb
            part = jnp.concatenate([dl0, -dl0], axis=0)

            @pl.when(i == 0)
            def _():
                dlg_ref[d] = part

            @pl.when(i > 0)
            def _():
                dlg_ref[d] += part

    blk = lambda c: pl.BlockSpec((HG_RB, 512), lambda i: (i, c))
    ob = pl.BlockSpec((HG_RB, 512), lambda i: (i, 0))
    lgs = pl.BlockSpec((2, 2, 512), lambda i: (0, 0, 0))
    return pl.pallas_call(
        body, name="hg_pre_bwd", grid=(HG_NB,),
        in_specs=[blk(3), blk(4), blk(5), lgs] + [ob] * 8 + [ANY],
        out_specs=(pl.BlockSpec((HG_RB, 2048), lambda i: (i, 0)), lgs),
        out_shape=(_sds(dp_rest.shape, BF16), _sds((2, 2, 512), F32)), input_output_aliases={12: 0},
        compiler_params=_cp(("arbitrary",)))(p_act, p_act, p_act, logits, dq_f, dq_b, dk_f, dk_b, db_f, db_b, dv_f, dv_b,
                                             dp_rest)


def _mix_fwd(o_na, o_f, o_b, gain, w_na, w_hg, p_act):
    def body(ona_ref, of_ref, ob_ref, g_ref, gain_ref, wna_ref, whg_ref, gna_ref, ghg_ref, o_ref, u_ref):
        u = _hg_post_rows(of_ref[...] + ob_ref[...], g_ref[...], gain_ref[...], _row_valid(pl.program_id(0), TM_B)).astype(BF16)
        u_ref[...] = u
        y_na = _dot(ona_ref[...], wna_ref[...])
        y_hg = _dot(u, whg_ref[...])
        o_ref[...] = (jax.nn.sigmoid(gna_ref[...]) * y_na + jax.nn.sigmoid(ghg_ref[...]) * y_hg).astype(BF16)

    act = pl.BlockSpec((TM_B, 512), lambda i: (i, 0))
    wsp = pl.BlockSpec((512, D), lambda i: (0, 0))
    return pl.pallas_call(
        body, name="mix_fwd", grid=(T // TM_B,),
        in_specs=[act, act, act, pl.BlockSpec((TM_B, 512), lambda i: (i, 7)), pl.BlockSpec((1, 512), lambda i: (0, 0)),
                  wsp, wsp, pl.BlockSpec((TM_B, D), lambda i: (i, 4)), pl.BlockSpec((TM_B, D), lambda i: (i, 5))],
        out_specs=(pl.BlockSpec((TM_B, D), lambda i: (i, 0)), act), out_shape=(_sds((T, D), BF16), _sds((T, 512), BF16)),
        compiler_params=_cp(("parallel",)))(o_na, o_f, o_b, p_act, gain, w_na, w_hg, p_act, p_act)


DP_REST = IN_COLS - 1536


def _mix_bwd(o_na, u_hg, o_f, o_b, gain, w_na, w_hg, p_act, dmix):
    ni = T // TM_B

    def body(ona_ref, uhg_ref, of_ref, ob_ref, g_ref, gain_ref, wna_ref, whg_ref, gna_ref, ghg_ref, dmix_ref,
             dp_ref, dwna_ref, dwhg_ref, dona_ref, do_ref, dgain_ref, acc_na, acc_hg):
        i = pl.program_id(0)
        dg_ref, dgna_ref, dghg_ref = dp_ref.at[:, 2048:2560], dp_ref.at[:, 2560:3584], dp_ref.at[:, 3584:4608]
        dm = dmix_ref[...].astype(F32)
        dxs = []
        for x_ref, w_ref, gt_ref, dgt_ref, dw_ref, acc in (
                (ona_ref, wna_ref, gna_ref, dgna_ref, dwna_ref, acc_na), (uhg_ref, whg_ref, ghg_ref, dghg_ref, dwhg_ref, acc_hg)):
            xv = x_ref[...]
            y = _dot(xv, w_ref[...])
            sg = jax.nn.sigmoid(gt_ref[...])
            dgt_ref[...] = (dm * y * sg * (1.0 - sg)).astype(BF16)
            dy = (dm * sg).astype(BF16)
            dxs.append(_dot(dy, w_ref[...], NT))
            part = _dot(xv, dy, TN)

            @pl.when(i == 0)
            def _():
                acc[...] = part

            @pl.when(i > 0)
            def _():
                acc[...] += part

            @pl.when(i == ni - 1)
            def _():
                dw_ref[...] = acc[...].astype(BF16)

        dona_ref[...] = dxs[0]
        do, dg, gpart = _hg_post_bwd_rows(dxs[1], of_ref[...] + ob_ref[...], g_ref[...], gain_ref[...], _row_valid(i, TM_B))
        do_ref[...] = do
        dg_ref[...] = dg.astype(BF16)

        @pl.when(i == 0)
        def _():
            dgain_ref[...] = gpart

        @pl.when(i > 0)
        def _():
            dgain_ref[...] += gpart

    act = pl.BlockSpec((TM_B, 512), lambda i: (i, 0))
    wsp = pl.BlockSpec((512, D), lambda i: (0, 0))
    rblk = pl.BlockSpec((TM_B, D), lambda i: (i, 0))
    vec = pl.BlockSpec((1, 512), lambda i: (0, 0))
    return pl.pallas_call(
        body, name="mix_bwd", grid=(ni,),
        in_specs=[act, act, act, act, pl.BlockSpec((TM_B, 512), lambda i: (i, 7)), vec, wsp, wsp,
                  pl.BlockSpec((TM_B, D), lambda i: (i, 4)), pl.BlockSpec((TM_B, D), lambda i: (i, 5)), rblk],
        out_specs=(pl.BlockSpec((TM_B, DP_REST), lambda i: (i, 0)), wsp, wsp, act, act, vec),
        out_shape=(_sds((T, DP_REST), BF16), _sds((512, D), BF16), _sds((512, D), BF16),
                   _sds((T, 512), F32), _sds((T, 512), F32), _sds((1, 512), F32)),
        scratch_shapes=[pltpu.VMEM((512, D), F32), pltpu.VMEM((512, D), F32)],
        compiler_params=_cp(("arbitrary",)))(o_na, u_hg, o_f, o_b, p_act, gain, w_na, w_hg, p_act, p_act, dmix)


def _wo_fwd(mix, w_o, h0, g_mlp):
    def body(mix_ref, w_ref, h0_ref, g_ref, h1_ref, m_ref):
        h1 = h0_ref[...] + _dot(mix_ref[...], w_ref[...])
        h1_ref[...] = h1
        r = lax.rsqrt(jnp.mean(h1 * h1, axis=-1, keepdims=True) + EPS)
        m_ref[...] = (h1 * r * g_ref[...]).astype(BF16)

    blk = pl.BlockSpec((TM_B, D), lambda i: (i, 0))
    return pl.pallas_call(
        body, name="wo_fwd", grid=(T // TM_B,),
        in_specs=[blk, pl.BlockSpec((D, D), lambda i: (0, 0)), blk, pl.BlockSpec((1, D), lambda i: (0, 0))],
        out_specs=(blk, blk), out_shape=(_sds((T, D), F32), _sds((T, D), BF16)),
        compiler_params=_cp(("parallel",)))(*_in_hbm(mix, w_o, h0, g_mlp))


def _wo_bwd(dh1_b, w_o, mix):
    ni = T // TM_B

    def body(dh_ref, w_ref, mix_ref, dmix_ref, dw_ref, acc):
        i = pl.program_id(0)
        dh = dh_ref[...]
        dmix_ref[...] = _dot(dh, w_ref[...], NT).astype(BF16)
        part = _dot(mix_ref[...], dh, TN)

        @pl.when(i == 0)
        def _():
            acc[...] = part

        @pl.when(i > 0)
        def _():
            acc[...] += part

        @pl.when(i == ni - 1)
        def _():
            dw_ref[...] = acc[...].astype(BF16)

    blk = pl.BlockSpec((TM_B, D), lambda i: (i, 0))
    wsp = pl.BlockSpec((D, D), lambda i: (0, 0))
    return pl.pallas_call(
        body, name="wo_bwd", grid=(ni,), in_specs=[blk, wsp, blk], out_specs=(blk, wsp),
        out_shape=(_sds((T, D), BF16), _sds((D, D), BF16)), scratch_shapes=[pltpu.VMEM((D, D), F32)],
        compiler_params=_cp(("arbitrary",)))(*_in_hbm(dh1_b, w_o, mix))


FF_B = D_FF // NDEV


def _loss_rows(xv, gv, tv, row0):
    r_io = lax.broadcasted_iota(jnp.int32, (xv.shape[0], 1), 0) + row0
    valid = (r_io >= NM) & (r_io < L)
    r = lax.rsqrt(jnp.mean(xv * xv, axis=-1, keepdims=True) + EPS)
    xh = xv * r
    err = jnp.where(valid, xh * gv - tv, 0.0)
    lpart = 0.5 * jnp.sum(jnp.sum(err * err, axis=-1, keepdims=True) * (1.0 / D), axis=0, keepdims=True)
    dy = err * (1.0 / D)
    dxh = dy * gv
    dh = r * (dxh - xh * jnp.mean(dxh * xh, axis=-1, keepdims=True))
    return lpart, dh, jnp.sum(dy * xh, axis=0, keepdims=True)


def _mlp_fwd_loss(m, wup_g, wdown_g, h1, g_final, tgt):
    nsub = TM_MM // TM_E

    def body(m_ref, wu_ref, wd_ref, h1_ref, g_ref, t_ref, loss_ref, dh_ref, dhb_ref, dg_ref, h2):
        i, j = pl.program_id(0), pl.program_id(1)
        up = jnp.maximum(_dot(m_ref[...], wu_ref[0]), 0.0)
        part = _dot((up * up).astype(BF16), wd_ref[0])

        @pl.when(j == 0)
        def _():
            h2[...] = h1_ref[...] + part

        @pl.when(j > 0)
        def _():
            h2[...] += part

        @pl.when(j == NDEV - 1)
        def _():
            lsum = jnp.zeros((1, 1), F32)
            gsum = jnp.zeros((1, D), F32)
            for s in range(nsub):
                rows = slice(s * TM_E, (s + 1) * TM_E)
                lpart, dh, gpart = _loss_rows(h2[rows, :], g_ref[...], t_ref[rows, :], i * TM_MM + s * TM_E)
                dh_ref[rows, :] = dh
                dhb_ref[rows, :] = dh.astype(BF16)
                lsum = lsum + lpart
                gsum = gsum + gpart
            lsum = jnp.broadcast_to(lsum, (1, 128))

            @pl.when(i == 0)
            def _():
                loss_ref[...] = lsum
                dg_ref[...] = gsum

            @pl.when(i > 0)
            def _():
                loss_ref[...] += lsum
                dg_ref[...] += gsum

    blk = pl.BlockSpec((TM_MM, D), lambda i, j: (i, 0))
    vec = pl.BlockSpec((1, D), lambda i, j: (0, 0))
    return pl.pallas_call(
        body, name="mlp_fwd_loss", grid=(T // TM_MM, NDEV),
        in_specs=[blk, pl.BlockSpec((1, D, FF_B), lambda i, j: (j, 0, 0)), pl.BlockSpec((1, FF_B, D), lambda i, j: (j, 0, 0)),
                  blk, vec, blk],
        out_specs=(pl.BlockSpec((1, 128), lambda i, j: (0, 0)), blk, blk, vec),
        out_shape=(_sds((1, 128), F32), _sds((T, D), F32), _sds((T, D), BF16), _sds((1, D), F32)),
        scratch_shapes=[pltpu.VMEM((TM_MM, D), F32)],
        compiler_params=_cp(("arbitrary", "arbitrary"), 56))(m, wup_g, wdown_g, h1, g_final, tgt)


def _mlp_bwd(m, dh2_b, wup_g, wdown_g, h1, g_mlp, dh2):
    ni = T // TM_B
    nsub = TM_B // TM_E

    def body(m_ref, dh_ref, wu_ref, wd_ref, h1_ref, g_ref, dres_ref, dwu_ref, dwd_ref, dh1_ref, dh1b_ref, dg_ref,
             dm_ref, acc_u, acc_d):
        j, i = pl.program_id(0), pl.program_id(1)
        rows = pl.ds(pl.multiple_of(i * TM_B, TM_B), TM_B)
        mv, dh = m_ref[...], dh_ref[...]
        r = jnp.maximum(_dot(mv, wu_ref[0]), 0.0)
        act = (r * r).astype(BF16)
        dact = _dot(dh, wd_ref[0], NT)
        dup = (dact * (2.0 * r)).astype(BF16)
        pd = _dot(act, dh, TN)
        pu = _dot(mv, dup, TN)
        dmv = _dot(dup, wu_ref[0], NT)

        @pl.when(i == 0)
        def _():
            acc_u[...] = pu
            acc_d[...] = pd

        @pl.when(i > 0)
        def _():
            acc_u[...] += pu
            acc_d[...] += pd

        @pl.when(i == ni - 1)
        def _():
            dwu_ref[0] = acc_u[...].astype(BF16)
            dwd_ref[0] = acc_d[...].astype(BF16)

        @pl.when(j == 0)
        def _():
            dm_ref[rows, :] = dmv

        @pl.when(j > 0)
        def _():
            dm_ref[rows, :] += dmv

        @pl.when(j == NDEV - 1)
        def _():
            gsum = jnp.zeros((1, D), F32)
            for s in range(nsub):
                sub = slice(s * TM_E, (s + 1) * TM_E)
                dm_rows = dm_ref[pl.ds(pl.multiple_of(i * TM_B + s * TM_E, TM_E), TM_E), :]
                dx, gpart = _norm_bwd_rows(h1_ref[sub, :], g_ref[...], dm_rows, dres_ref[sub, :])
                dh1_ref[sub, :] = dx
                dh1b_ref[sub, :] = dx.astype(BF16)
                gsum = gsum + gpart

            @pl.when(i == 0)
            def _():
                dg_ref[...] = gsum

            @pl.when(i > 0)
            def _():
                dg_ref[...] += gsum

    blk = pl.BlockSpec((TM_B, D), lambda j, i: (i, 0))
    late = pl.BlockSpec((TM_B, D), lambda j, i: (jnp.where(j == NDEV - 1, i, 0), 0))
    vec = pl.BlockSpec((1, D), lambda j, i: (0, 0))
    wus = pl.BlockSpec((1, D, FF_B), lambda j, i: (j, 0, 0))
    wds = pl.BlockSpec((1, FF_B, D), lambda j, i: (j, 0, 0))
    return pl.pallas_call(
        body, name="mlp_bwd", grid=(NDEV, ni), in_specs=[blk, blk, wus, wds, late, vec, late],
        out_specs=(wus, wds, late, late, vec),
        out_shape=(_sds((NDEV, D, FF_B), BF16), _sds((NDEV, FF_B, D), BF16), _sds((T, D), F32), _sds((T, D), BF16),
                   _sds((1, D), F32)),
        scratch_shapes=[pltpu.VMEM((T, D), F32), pltpu.VMEM((D, FF_B), F32), pltpu.VMEM((FF_B, D), F32)],
        compiler_params=_cp(("arbitrary", "arbitrary"), 56))(m, dh2_b, wup_g, wdown_g, h1, g_mlp, dh2)


def _adamw(parts, w, m, v, name):
    rr, cc = w.shape
    nslot = parts.shape[0]
    tr = rr
    for cand in (256, 128, 64):
        if rr % cand == 0 and rr > cand:
            tr = cand
            break
    c1 = 1.0 - ADAM_B1 ** ADAM_STEP
    c2 = 1.0 - ADAM_B2 ** ADAM_STEP

    def body(p_ref, w_ref, m_ref, v_ref, g_ref, d_ref, nm_ref, nv_ref):
        g = p_ref[0].astype(F32)
        for s in range(1, nslot):
            g = g + p_ref[s].astype(F32)
        mn = ADAM_B1 * m_ref[...] + (1.0 - ADAM_B1) * g
        vn = ADAM_B2 * v_ref[...] + (1.0 - ADAM_B2) * (g * g)
        g_ref[...] = g
        nm_ref[...] = mn
        nv_ref[...] = vn
        d_ref[...] = -ADAM_LR * ((mn / c1) / (jnp.sqrt(vn / c2) + ADAM_EPS) + ADAM_WD * w_ref[...])

    blk = pl.BlockSpec((tr, cc), lambda i: (i, 0))
    return pl.pallas_call(
        body, name=name, grid=(rr // tr,),
        in_specs=[pl.BlockSpec((nslot, tr, cc), lambda i: (0, i, 0)), blk, blk, blk],
        out_specs=(blk,) * 4, out_shape=(_sds((rr, cc), F32),) * 4,
        compiler_params=_cp(("parallel",)))(*_in_hbm(parts, w, m, v))


RPB_N = NA_HEADS * 15 * 31
RPB_PAD = 4096
OWN_ROWS = NM + 8


def _pad_rows(a, rows):
    return jnp.pad(a, ((0, rows - a.shape[0]),) + ((0, 0),) * (a.ndim - 1))


def _pack_owned(meta_blk, lb_blk):
    return jnp.concatenate([meta_blk, _pad_rows(lb_blk.reshape(2, 128), 8)], axis=0)


LOSS_ROW = 28


def _pack_replicated(n_mix, n_mlp, n_final, hg_gain, rpb, loss_row=None):
    flat = _pad_rows(rpb.reshape(RPB_N), RPB_PAD)
    gain8 = _pad_rows(hg_gain.reshape(4, 128), 8)
    if loss_row is not None:
        gain8 = gain8 + jnp.pad(loss_row, ((LOSS_ROW - 24, 31 - LOSS_ROW), (0, 0)))
    return jnp.concatenate([n_mix.reshape(8, 128), n_mlp.reshape(8, 128), n_final.reshape(8, 128), gain8,
                            flat.reshape(32, 128)], axis=0)


def _unpack_replicated(a):
    return (a[0:8].reshape(1, D), a[8:16].reshape(1, D), a[16:24].reshape(D), a[24:28].reshape(1, 512),
            a[32:64].reshape(RPB_PAD)[:RPB_N].reshape(1, NA_HEADS, 15, 31))


def kernel(x, meta_tokens, w_in, w_na_out, w_hg_out, w_o, w_up, w_down, norm_mix, norm_mlp, norm_final, hg_norm, na_rpb, hg_lb_logits, loss_target, m_meta_tokens, m_w_in, m_w_na_out, m_w_hg_out, m_w_o, m_w_up, m_w_down, m_norm_mix, m_norm_mlp, m_norm_final, m_hg_norm, m_na_rpb, m_hg_lb_logits, v_meta_tokens, v_w_in, v_w_na_out, v_w_hg_out, v_w_o, v_w_up, v_w_down, v_norm_mix, v_norm_mlp, v_norm_final, v_hg_norm, v_na_rpb, v_hg_lb_logits):
    owned = _pack_owned(meta_tokens, hg_lb_logits)
    first_masks = (ALL_PEERS, SAME_CORE_AND_SIBLING)
    first, tok = _exchange_start([owned, w_in[0].astype(BF16)], [False] * 2, "gather_first_start", first_masks)
    bias_tab = _na_bias_table(_tie(jnp.pad(na_rpb[0], ((0, 0), (0, 0), (0, 128 - 31))), tok, "tie_bias_table"))
    later = [w[0].astype(BF16) for w in (w_na_out, w_hg_out, w_o, w_up, w_down)]
    lead = jnp.zeros((NM, D), F32) + tok[0, 0]
    h0_rows = jnp.concatenate([lead, x[0], jnp.zeros((T - L, D), F32)], axis=0)
    tgt = jnp.concatenate([lead, loss_target[0], jnp.zeros((T - L, D), F32)], axis=0)
    (owned_g, _), first = _exchange_wait(first, [False] * 2, [h0_rows], "gather_small_wait", first_masks, which=(0,))
    meta_full = jnp.transpose(owned_g[:, 0:NM, :], (1, 0, 2)).reshape(NM, D)
    logits = jnp.transpose(owned_g[:, NM:NM + 2, :].reshape(NDEV, 2, 2, 64), (1, 2, 0, 3)).reshape(2, 2, 512)
    h0 = lax.dynamic_update_slice(h0_rows, meta_full, (0, 0))
    a, a_t = _norm_fwd_t(h0, norm_mix, "norm_mix_fwd")
    (_, win_l), _ = _exchange_wait(first, [False] * 2, [a, logits, tgt, bias_tab] + later, "gather_first_wait", first_masks,
                                   which=(1,))
    (win_g,) = _forward_to_sibling([win_l], "gather_first_forward")
    later[0] = _tie(later[0], win_g, "tie_gather_rest")
    gather_rest, tok = _exchange_start(later, [False] * 5, "gather_rest_start")
    win_g = _tie(win_g, tok, "tie_inproj")

    p_act = _inproj_fwd(a, win_g)
    o_na, lse = _na_fwd(p_act, bias_tab)
    qh, k_f, b_f, k_b, b_b = _hg_pre(p_act, logits)
    o_f, st_f = _hg_scan_fwd(qh, k_f, b_f, p_act, False)
    o_b, st_b = _hg_scan_fwd(qh, k_b, b_b, p_act, True)
    (wna_g, whg_g, wo_g, _, _), gather_rest = _exchange_wait(
        gather_rest, [False] * 5, [o_f, o_b, o_na], "gather_rest_wait_a", which=(0, 1, 2))
    w_na_full = jnp.transpose(wna_g, (1, 0, 2)).reshape(512, D)
    w_hg_full = jnp.transpose(whg_g, (1, 0, 2)).reshape(512, D)
    mix, u_hg = _mix_fwd(o_na, o_f, o_b, hg_norm, w_na_full, w_hg_full, p_act)
    h1, m_act = _wo_fwd(mix, wo_g.reshape(D, D), h0, norm_mlp)
    (_, _, wo_g, wup_g, wdown_g), _ = _exchange_wait(gather_rest, [False] * 5, [m_act], "gather_rest_wait_b", which=(3, 4))
    w_o_full = wo_g.reshape(D, D)
    loss_part, dh2, dh2_b, d_nfinal = _mlp_fwd_loss(m_act, wup_g, wdown_g, h1, norm_final.reshape(1, D), tgt)

    dwup_p, dwdown_p, dh1, dh1_b, d_nmlp = _mlp_bwd(m_act, dh2_b, wup_g, wdown_g, h1, norm_mlp, dh2)
    sc_mlp, tok = _exchange_start([dwup_p, dwdown_p], [True] * 2, "scatter_mlp_start")
    dmix, dwo = _wo_bwd(_tie(dh1_b, tok, "tie_wo_bwd"), w_o_full, mix)
    sc_wo, tok = _exchange_start([dwo.reshape(NDEV, D // NDEV, D)], [True], "scatter_wo_start")
    dp_rest, dwna, dwhg, do_na, do_hg, d_gain = _mix_bwd(
        o_na, u_hg, o_f, o_b, hg_norm, w_na_full, w_hg_full, p_act, _tie(dmix, tok, "tie_mix_bwd"))
    owner_cols = lambda w: jnp.transpose(w.reshape(512, NDEV, D // NDEV), (1, 0, 2))
    sc_br, tok = _exchange_start([owner_cols(dwna), owner_cols(dwhg)], [True] * 2, "scatter_branch_start")
    do_hg = _tie(do_hg, tok, "tie_hg_scan_bwd")
    dq_f, dk_f, db_f, dv_f = _hg_scan_bwd(qh, k_f, b_f, p_act, st_f, do_hg, False)
    dq_b, dk_b, db_b, dv_b = _hg_scan_bwd(qh, k_b, b_b, p_act, st_b, do_hg, True)
    dp_rest, d_logits = _hg_pre_bwd(p_act, logits, dq_f, dq_b, dk_f, dk_b, db_f, db_b, dv_f, dv_b, dp_rest)
    dq_na, dk_na, dv_na, dbias = _na_bwd(p_act, do_na, lse, bias_tab)
    dp_na = jnp.concatenate([dq_na.astype(BF16), dk_na.astype(BF16), dv_na.astype(BF16)], axis=1)
    dwin_p = _inproj_bwd_dw(a_t, dp_na, dp_rest)
    far_mine, far_other = _far_slots()
    dwin_p = _add_into_slot(dwin_p, _sibling_swap_far(dwin_p, "pair_swap_in"), far_mine, "pair_add_in")
    sc_in, tok = _exchange_start([dwin_p], [True], "scatter_in_start", ALL_BUT_FAR_OTHER_CORE, absent=far_other)
    dh0, d_nmix = _inproj_bwd_da(_tie(dp_na, tok, "tie_inproj_bwd_da"), dp_rest, win_g, h0, norm_mix, dh1)
    d_rpb = _na_rpb_reduce(_tie(dbias, tok, "tie_rpb_reduce"))[:, :, :31]

    res = {}

    def update(nm, parts, w, mm, vv):
        res[nm] = [r[None] for r in _adamw(parts, w[0], mm[0], vv[0], "adamw_" + nm)]
        return res[nm][1]

    wup_r, wdown_r = _exchange_wait(sc_mlp, [True] * 2, [dh0, d_rpb], "scatter_mlp_wait")
    update("w_up", wup_r, w_up, m_w_up, v_w_up)
    last = update("w_down", wdown_r, w_down, m_w_down, v_w_down)
    (wo_r,) = _exchange_wait(sc_wo, [True], [last], "scatter_wo_wait")
    last = update("w_o", wo_r, w_o, m_w_o, v_w_o)
    wna_r, whg_r = _exchange_wait(sc_br, [True] * 2, [last], "scatter_branch_wait")
    update("w_na_out", wna_r, w_na_out, m_w_na_out, v_w_na_out)
    last = update("w_hg_out", whg_r, w_hg_out, m_w_hg_out, v_w_hg_out)

    d_meta = jnp.transpose(dh0[0:NM].reshape(NM, NDEV, 128), (1, 0, 2))
    d_lg = jnp.transpose(d_logits.reshape(2, 2, NDEV, 64), (2, 0, 1, 3)).reshape(NDEV, 2, 128)
    owned_p = jnp.concatenate([d_meta, jnp.pad(d_lg, ((0, 0), (0, OWN_ROWS - NM - 2), (0, 0)))], axis=1)
    repl_p = _pack_replicated(d_nmix, d_nmlp, d_nfinal, d_gain, d_rpb, loss_part)
    grad_x = dh0[NM:L][None]
    done_first = [grad_x] + [res[nm][0] for nm in ("w_up", "w_down", "w_o", "w_na_out", "w_hg_out")]
    owned_r, repl_r = _exchange([owned_p, repl_p], [True, False], "scatter_small", done_first)
    own = _adamw(owned_r, owned, _pack_owned(m_meta_tokens, m_hg_lb_logits), _pack_owned(v_meta_tokens, v_hg_lb_logits),
                 "adamw_owned_small")
    res["meta_tokens"] = [r[0:NM] for r in own]
    res["hg_lb_logits"] = [r[NM:NM + 2].reshape(2, 2, 64) for r in own]
    rep = _adamw(repl_r, _pack_replicated(norm_mix, norm_mlp, norm_final, hg_norm, na_rpb),
                 _pack_replicated(m_norm_mix, m_norm_mlp, m_norm_final, m_hg_norm, m_na_rpb),
                 _pack_replicated(v_norm_mix, v_norm_mlp, v_norm_final, v_hg_norm, v_na_rpb), "adamw_replicated")
    for q in range(4):
        um = _unpack_replicated(rep[q])
        for nm, val in zip(("norm_mix", "norm_mlp", "norm_final", "hg_norm", "na_rpb"), um):
            res.setdefault(nm, [None] * 4)[q] = val
    (win_r,) = _exchange_wait(sc_in, [True], [rep[1], own[1]], "scatter_in_wait", ALL_BUT_FAR_OTHER_CORE)
    update("w_in", win_r, w_in, m_w_in, v_w_in)

    loss = jnp.sum(repl_r[:, LOSS_ROW, 0])
    order = ("meta_tokens", "w_in", "w_na_out", "w_hg_out", "w_o", "w_up", "w_down", "norm_mix", "norm_mlp", "norm_final",
             "hg_norm", "na_rpb", "hg_lb_logits")
    outs = [loss, grad_x]
    for q in range(4):
        outs += [res[nm][q] for nm in order]
    return tuple(outs)
```

```python
import functools

import numpy as np
import jax
import jax.numpy as jnp
from jax import lax
from jax.experimental import pallas as pl
from jax.experimental.pallas import tpu as pltpu

F32 = jnp.float32
BF16 = jnp.bfloat16

D = 1024
SEQ = 2048
NM = 16
L = SEQ + NM
T = 2176
NDEV = 8
EPS = 1e-6
GRID_W = 64
ROWS = SEQ // GRID_W
NA_HEADS = 8
NA_DH = 64
NA_SCALE = NA_DH ** -0.5
HG_HEADS = 4
HG_C = 16
NCHUNK = L // HG_C
D_FF = 4096
IN_COLS = 6144
NEG = -1e30

ADAM_LR = 0.001
ADAM_B1 = 0.9
ADAM_B2 = 0.999
ADAM_EPS = 1e-08
ADAM_WD = 0.01
ADAM_STEP = 10

MESH_ID = pl.DeviceIdType.MESH
ANY = pl.BlockSpec(memory_space=pl.ANY)

NN = (((1,), (0,)), ((), ()))
NT = (((1,), (1,)), ((), ()))
TN = (((0,), (0,)), ((), ()))


def _cp(sem=None, vmem_mb=48):
    return pltpu.CompilerParams(dimension_semantics=sem, vmem_limit_bytes=vmem_mb * 1024 * 1024)


def _dot(a, b, dims=NN):
    return lax.dot_general(a, b, dims, preferred_element_type=F32)


def _sds(shape, dtype):
    return jax.ShapeDtypeStruct(shape, dtype)


HBM = pl.BlockSpec(memory_space=pltpu.HBM)
SEM = pl.BlockSpec(memory_space=pltpu.SEMAPHORE)
EFFECT = pltpu.SideEffectType.DATAFLOW_SIDE_EFFECTING


def _exchange(arrs, scatter, name, after=()):
    n = len(arrs)
    after = list(after)
    out_shapes = []
    for a, sc in zip(arrs, scatter):
        out_shapes.append(_sds(a.shape if sc else (NDEV,) + a.shape, a.dtype))

    def body(*refs):
        ins, outs = refs[:n], refs[n + len(after):2 * n + len(after)]
        send_sems, recv_sems, loc_sems = refs[2 * n + len(after):]
        me = 4 * lax.axis_index("x") + 2 * lax.axis_index("y") + lax.axis_index("c")
        copies = []
        for k in range(n):
            src_me = ins[k].at[me] if scatter[k] else ins[k]
            loc = pltpu.make_async_copy(src_me, outs[k].at[me], loc_sems.at[k])
            loc.start()
            copies.append(loc)
        remote = sum(_peer_copies(ins, outs, scatter, send_sems, recv_sems), [])
        for cp in remote:
            cp.start()
        for cp in remote:
            cp.wait_recv()
        for cp in remote:
            cp.wait_send()
        for cp in copies:
            cp.wait()

    return pl.pallas_call(
        body, name=name, out_shape=tuple(out_shapes), in_specs=[ANY] * (n + len(after)), out_specs=tuple([ANY] * n),
        scratch_shapes=[pltpu.SemaphoreType.DMA((n * (NDEV - 1),)), pltpu.SemaphoreType.DMA((n * (NDEV - 1),)),
                        pltpu.SemaphoreType.DMA((n,))],
    )(*arrs, *after)


def _forward_to_sibling(bufs, name):
    n = len(bufs)

    def body(*refs):
        ins, outs = refs[:n], refs[n:2 * n]
        send_sems, recv_sems = refs[2 * n:]
        x, y, c = lax.axis_index("x"), lax.axis_index("y"), lax.axis_index("c")
        copies = []
        for k in range(n):
            for j, (cx, cy) in enumerate(((1 - x, y), (x, 1 - y), (1 - x, 1 - y))):
                slot = 4 * cx + 2 * cy + c
                copies.append(pltpu.make_async_remote_copy(
                    src_ref=ins[k].at[slot], dst_ref=outs[k].at[slot], send_sem=send_sems.at[3 * k + j],
                    recv_sem=recv_sems.at[3 * k + j], device_id=(x, y, 1 - c), device_id_type=MESH_ID))
        for cp in copies:
            cp.start()
        for cp in copies:
            cp.wait_recv()
        for cp in copies:
            cp.wait_send()

    return pl.pallas_call(
        body, name=name, out_shape=tuple(_sds(b.shape, b.dtype) for b in bufs), in_specs=[ANY] * n,
        out_specs=tuple([ANY] * n), input_output_aliases={k: k for k in range(n)},
        scratch_shapes=[pltpu.SemaphoreType.DMA((3 * n,)), pltpu.SemaphoreType.DMA((3 * n,))],
    )(*bufs)


ALL_PEERS = tuple(range(1, NDEV))
SAME_CORE_AND_SIBLING = (1, 2, 4, 6)
ALL_BUT_FAR_OTHER_CORE = (1, 2, 3, 4, 5, 6)


def _far_slots():
    far = 4 * (1 - lax.axis_index("x")) + 2 * (1 - lax.axis_index("y"))
    core = lax.axis_index("c")
    return far + core, far + 1 - core


def _sibling_swap_far(parts, name):
    def body(x_ref, o_ref, send_sem, recv_sem):
        sib = (lax.axis_index("x"), lax.axis_index("y"), 1 - lax.axis_index("c"))
        cp = pltpu.make_async_remote_copy(src_ref=x_ref.at[_far_slots()[1]], dst_ref=o_ref.at[0], send_sem=send_sem,
                                          recv_sem=recv_sem, device_id=sib, device_id_type=MESH_ID)
        cp.start()
        cp.wait()

    return pl.pallas_call(
        body, name=name, out_shape=_sds((1,) + parts.shape[1:], parts.dtype), in_specs=[ANY], out_specs=ANY,
        scratch_shapes=[pltpu.SemaphoreType.DMA(()), pltpu.SemaphoreType.DMA(())])(parts)


def _add_into_slot(parts, other, slot, name):
    _, rr, cc = parts.shape

    def body(slot_ref, p_ref, o_ref, out_ref):
        del slot_ref
        out_ref[...] = (p_ref[...].astype(F32) + o_ref[...].astype(F32)).astype(BF16)

    mine = pl.BlockSpec((1, rr // 2, cc), lambda j, s: (s[0], j, 0))
    grid_spec = pltpu.PrefetchScalarGridSpec(
        num_scalar_prefetch=1, grid=(2,),
        in_specs=[mine, pl.BlockSpec((1, rr // 2, cc), lambda j, s: (0, j, 0))], out_specs=mine)
    return pl.pallas_call(body, name=name, grid_spec=grid_spec, out_shape=_sds(parts.shape, BF16),
                          input_output_aliases={1: 0}, compiler_params=_cp(("arbitrary",)))(
                              jnp.reshape(slot, (1,)).astype(jnp.int32), parts, other)


def _peer_copies(srcs, lands, scatter, send_sems, recv_sems, masks=ALL_PEERS):
    x, y, c = lax.axis_index("x"), lax.axis_index("y"), lax.axis_index("c")
    me = 4 * x + 2 * y + c
    out = []
    for k in range(len(srcs)):
        out.append([])
        for m in (masks[k] if isinstance(masks[0], tuple) else masks):
            px, py, pc = x ^ (m >> 2), y ^ ((m >> 1) & 1), c ^ (m & 1)
            src = srcs[k].at[4 * px + 2 * py + pc] if scatter[k] else srcs[k]
            out[k].append(pltpu.make_async_remote_copy(
                src_ref=src, dst_ref=lands[k].at[me], send_sem=send_sems.at[k * (NDEV - 1) + m - 1],
                recv_sem=recv_sems.at[k * (NDEV - 1) + m - 1],
                device_id=(px, py, pc), device_id_type=MESH_ID))
    return out


def _exchange_start(arrs, scatter, name, masks=ALL_PEERS, absent=None):
    n = len(arrs)
    me = 4 * lax.axis_index("x") + 2 * lax.axis_index("y") + lax.axis_index("c")
    lands = []
    for a, sc in zip(arrs, scatter):
        own = lax.dynamic_index_in_dim(a, me, 0, keepdims=True) if sc else a[None]
        shape = a.shape if sc else (NDEV,) + a.shape
        land = lax.dynamic_update_index_in_dim(lax.empty(shape, a.dtype), own, me, 0)
        if absent is not None:
            land = lax.dynamic_update_index_in_dim(land, jnp.zeros_like(own), absent, 0)
        lands.append(land)

    def body(*refs):
        srcs, lnds = refs[:n], refs[n:2 * n]
        send_sems, recv_sems = refs[2 * n], refs[2 * n + 1]
        token = refs[-1]
        for cp in sum(_peer_copies(srcs, lnds, scatter, send_sems, recv_sems, masks), []):
            cp.start()
        token[...] = jnp.zeros_like(token)

    ops = [pltpu.with_memory_space_constraint(a, pltpu.HBM) for a in list(arrs) + lands]
    res = pl.pallas_call(
        body, name=name,
        out_shape=(pltpu.SemaphoreType.DMA((n * (NDEV - 1),)), pltpu.SemaphoreType.DMA((n * (NDEV - 1),)))
        + tuple(pltpu.HBM(o.shape, o.dtype) for o in ops) + (_sds((8, 128), F32),),
        in_specs=[HBM] * (2 * n), out_specs=(SEM, SEM) + (HBM,) * (2 * n) + (pl.BlockSpec(memory_space=pltpu.VMEM),),
        input_output_aliases={k: 2 + k for k in range(2 * n)},
        compiler_params=pltpu.CompilerParams(has_side_effects=EFFECT),
    )(*ops)
    return res[:-1], res[-1]


def _exchange_wait(handle, scatter, after, name, masks=ALL_PEERS, which=None):
    send_sems, recv_sems = handle[0], handle[1]
    bufs = handle[2:]
    n = len(bufs) // 2
    after = list(after)

    def body(*refs):
        srcs, lnds = refs[:n], refs[n:2 * n]
        copies = _peer_copies(srcs, lnds, scatter, refs[2 * n], refs[2 * n + 1], masks)
        for k in (range(n) if which is None else which):
            for cp in copies[k]:
                cp.wait_send()
                cp.wait_recv()

    res = pl.pallas_call(
        body, name=name, out_shape=tuple(pltpu.HBM(b.shape, b.dtype) for b in bufs),
        in_specs=[HBM] * (2 * n) + [SEM, SEM] + [ANY] * len(after), out_specs=(HBM,) * (2 * n),
        input_output_aliases={k: k for k in range(2 * n)},
        compiler_params=pltpu.CompilerParams(has_side_effects=EFFECT),
    )(*bufs, send_sems, recv_sems, *after)
    return res[n:] if which is None else (res[n:], (send_sems, recv_sems) + tuple(res))


def _tie(x, token, name):
    def body(x_ref, t_ref, o_ref):
        del x_ref, t_ref, o_ref

    return pl.pallas_call(body, name=name, out_shape=_sds(x.shape, x.dtype), in_specs=[ANY, ANY], out_specs=ANY,
                          input_output_aliases={0: 0})(x, token)


TM_E = 272


def _norm_fwd_t(h, g, name):
    def body(h_ref, g_ref, o_ref, ot_ref):
        xv = h_ref[...]
        r = lax.rsqrt(jnp.mean(xv * xv, axis=-1, keepdims=True) + EPS)
        y = xv * r * g_ref[...]
        o_ref[...] = y.astype(BF16)
        ot_ref[...] = y.T.astype(BF16)

    return pl.pallas_call(
        body, name=name, grid=(T // 128,),
        in_specs=[pl.BlockSpec((128, D), lambda i: (i, 0)), pl.BlockSpec((1, D), lambda i: (0, 0))],
        out_specs=(pl.BlockSpec((128, D), lambda i: (i, 0)), pl.BlockSpec((D, 128), lambda i: (0, i))),
        out_shape=(_sds((T, D), BF16), _sds((D, T), BF16)), compiler_params=_cp(("parallel",)))(h, g)


def _norm_bwd_rows(xv, gv, dnv, dres):
    r = lax.rsqrt(jnp.mean(xv * xv, axis=-1, keepdims=True) + EPS)
    xh = xv * r
    dxh = dnv * gv
    dx = dres + r * (dxh - xh * jnp.mean(dxh * xh, axis=-1, keepdims=True))
    return dx, jnp.sum(dnv * xh, axis=0, keepdims=True)


TM_MM = 1088


def _inproj_fwd(a, w_g):
    nb = w_g.shape[2]

    def body(a_ref, w_ref, o_ref):
        o_ref[...] = _dot(a_ref[...], w_ref[0])

    return pl.pallas_call(
        body, name="inproj_fwd", grid=(T // TM_MM, NDEV),
        in_specs=[pl.BlockSpec((TM_MM, D), lambda i, j: (i, 0)), pl.BlockSpec((1, D, nb), lambda i, j: (j, 0, 0))],
        out_specs=pl.BlockSpec((TM_MM, nb), lambda i, j: (i, j)), out_shape=_sds((T, NDEV * nb), F32),
        compiler_params=_cp(("parallel", "parallel")))(a, w_g)


TM_B = 544


W_IN_B = IN_COLS // NDEV


NA_BLKS = 1536 // W_IN_B


def _dp_specs(rows, row_index):
    return [pl.BlockSpec((rows, W_IN_B), lambda *g: (row_index(*g), jnp.minimum(g[-1], NA_BLKS - 1))),
            pl.BlockSpec((rows, W_IN_B), lambda *g: (row_index(*g), jnp.maximum(g[-1] - NA_BLKS, 0)))]


def _inproj_bwd_dw(a_t, dp_na, dp_rest):
    def body(at_ref, na_ref, rest_ref, dw_ref):
        j = pl.program_id(0)

        @pl.when(j < NA_BLKS)
        def _():
            dw_ref[0] = _dot(at_ref[...], na_ref[...]).astype(BF16)

        @pl.when(j >= NA_BLKS)
        def _():
            dw_ref[0] = _dot(at_ref[...], rest_ref[...]).astype(BF16)

    return pl.pallas_call(
        body, name="inproj_bwd_dw", grid=(NDEV,),
        in_specs=[pl.BlockSpec((D, T), lambda j: (0, 0))] + _dp_specs(T, lambda j: 0),
        out_specs=pl.BlockSpec((1, D, W_IN_B), lambda j: (j, 0, 0)), out_shape=_sds((NDEV, D, W_IN_B), BF16),
        compiler_params=_cp(("parallel",)))(a_t, dp_na, dp_rest)


def _inproj_bwd_da(dp_na, dp_rest, w_g, h0, g_mix, dh1):
    nsub = TM_MM // TM_E
    nblk = T // TM_MM

    def seq_copies(b, dh0_ref, gx_ref, sems):
        out = []
        for s in range(nsub):
            lo, hi = max(NM, b * TM_MM + s * TM_E), min(L, b * TM_MM + (s + 1) * TM_E)
            if hi > lo:
                out.append((s, pltpu.make_async_copy(dh0_ref.at[pl.ds(lo - b * TM_MM, hi - lo)],
                                                     gx_ref.at[pl.ds(lo - NM, hi - lo)], sems.at[b * nsub + s])))
        return out

    def body(na_ref, rest_ref, w_ref, h0_ref, g_ref, dres_ref, dh0_ref, dg_ref, gx_ref, da, sems):
        i, j = pl.program_id(0), pl.program_id(1)
        dpv = jnp.where(j < NA_BLKS, na_ref[...], rest_ref[...])
        dav = _dot(dpv, w_ref[0], NT)

        @pl.when(j == 0)
        def _():
            da[...] = dav

        @pl.when(j > 0)
        def _():
            da[...] += dav

        @pl.when(j == NDEV - 1)
        def _():
            gsum = jnp.zeros((1, D), F32)
            for s in range(nsub):
                sub = slice(s * TM_E, (s + 1) * TM_E)
                dx, gpart = _norm_bwd_rows(h0_ref[sub, :], g_ref[...], da[sub, :], dres_ref[sub, :])
                dh0_ref[sub, :] = dx
                gsum = gsum + gpart
                for b in range(nblk):
                    for _, cp in (c for c in seq_copies(b, dh0_ref, gx_ref, sems) if c[0] == s):
                        pl.when(i == b)(cp.start)

            @pl.when(i == 0)
            def _():
                dg_ref[...] = gsum

            @pl.when(i > 0)
            def _():
                dg_ref[...] += gsum

            for b in range(nblk):
                @pl.when(i == b)
                def _(b=b):
                    for _, cp in seq_copies(b, dh0_ref, gx_ref, sems):
                        cp.wait()

    rblk = pl.BlockSpec((TM_MM, D), lambda i, j: (i, 0))
    vec = pl.BlockSpec((1, D), lambda i, j: (0, 0))
    return pl.pallas_call(
        body, name="inproj_bwd_da", grid=(T // TM_MM, NDEV),
        in_specs=_dp_specs(TM_MM, lambda i, j: i) + [pl.BlockSpec((1, D, W_IN_B), lambda i, j: (j, 0, 0)), rblk, vec, rblk],
        out_specs=(rblk, vec, ANY), out_shape=(_sds((T, D), F32), _sds((1, D), F32), _sds((L - NM, D), F32)),
        scratch_shapes=[pltpu.VMEM((TM_MM, D), F32), pltpu.SemaphoreType.DMA((nblk * nsub,))],
        compiler_params=_cp(("arbitrary", "arbitrary"), 56))(dp_na, dp_rest, w_g, h0, g_mix, dh1)


NA_QB = 256
NA_GROUPS = ROWS // 4
NA_UROWS = 11
NA_KW = NA_UROWS * GRID_W
NA_KU = 768


def _na_row_offset(var, i, j):
    valid = (j < 8, i <= j < i + 8, 3 <= j < NA_UROWS)[var]
    return (j - i + (7, 3, 0)[var]) if valid else None


def _na_bias_table(rp):
    def body(r_ref, o_ref):
        row3 = lax.broadcasted_iota(jnp.int32, (15, GRID_W, 128), 1)
        lane3 = lax.broadcasted_iota(jnp.int32, (15, GRID_W, 128), 2)
        w3 = lane3 & (GRID_W - 1)
        cs3 = jnp.clip(row3 - 8, 0, GRID_W - 16)
        lane = lax.broadcasted_iota(jnp.int32, (GRID_W, 128), 1)
        neg = jnp.full((GRID_W, 128), NEG, F32)
        z = jnp.stack([jnp.broadcast_to(r_ref[0, a:a + 1, :], (GRID_W, 128)) for a in range(15)])
        for bit in range(6):
            sh = 1 << bit
            z = jnp.where((row3 & sh) != 0, jnp.roll(z, sh, axis=2), z)
        z = jnp.roll(z, 128 - 15, axis=2)
        z = jnp.where(lane3 < GRID_W, z, 0.0)
        z = z + jnp.roll(z, GRID_W, axis=2)
        tabs = jnp.where((w3 >= cs3) & (w3 < cs3 + 16), z, NEG)
        tail = jnp.where(lane < GRID_W + NM, 0.0, NEG)
        for var in range(3):
            for i in range(4):
                for jp in range(NA_KU // 128):
                    halves = []
                    for j in (2 * jp, 2 * jp + 1):
                        a = _na_row_offset(var, i, j) if j < NA_UROWS else None
                        halves.append(tail if j >= NA_UROWS else (neg if a is None else tabs[a]))
                    o_ref[var, 0, i * 64:(i + 1) * 64, jp * 128:(jp + 1) * 128] = jnp.where(lane < GRID_W, halves[0], halves[1])

    return pl.pallas_call(
        body, name="na_bias_table", grid=(NA_HEADS,),
        in_specs=[pl.BlockSpec((1, 15, 128), lambda h: (h, 0, 0))],
        out_specs=pl.BlockSpec((3, 1, NA_QB, NA_KU), lambda h: (0, h, 0, 0)),
        out_shape=_sds((3, NA_HEADS, NA_QB, NA_KU), F32), compiler_params=_cp(("parallel",)))(rp)


def _na_var(g):
    return jnp.where(g == 0, 0, jnp.where(g == NA_GROUPS - 1, 2, 1))


def _na_load_window(src_ref, dst, g):
    us = jnp.clip(4 * g - 4, 0, ROWS - NA_UROWS)
    kstart = pl.multiple_of(NM + GRID_W * us, 16)
    dst[0:NA_KW, :] = src_ref[pl.ds(kstart, NA_KW), :].astype(BF16)
    dst[NA_KW:NA_KW + NM, :] = src_ref[0:NM, :].astype(BF16)
    dst[NA_KW + NM:, :] = jnp.zeros((NA_KU - NA_KW - NM, 128), BF16)
    return kstart


def _na_fwd(p_act, bias_tab):
    def body(q_ref, k_ref, v_ref, b_ref, o_ref, lse_ref, ku, vu):
        g = pl.program_id(1)
        _na_load_window(k_ref, ku, g)
        _na_load_window(v_ref, vu, g)
        qstart = pl.multiple_of(NM + NA_QB * g, 16)
        q = q_ref[pl.ds(qstart, NA_QB), :]
        lane = lax.broadcasted_iota(jnp.int32, (NA_QB, 128), 1)
        o_h, lse_h = [], []
        for h in range(2):
            hm = (lane < 64) if h == 0 else (lane >= 64)
            qm = (jnp.where(hm, q, 0.0) * NA_SCALE).astype(BF16)
            s = _dot(qm, ku[...], NT) + b_ref[0, h]
            m = jnp.max(s, axis=-1, keepdims=True)
            p = jnp.exp(s - m)
            l = jnp.sum(p, axis=-1, keepdims=True)
            o_h.append(_dot(p.astype(BF16), vu[...]) / l)
            lse_h.append(jnp.broadcast_to(m + jnp.log(l), (NA_QB, 128)))
        o_ref[pl.ds(qstart, NA_QB), :] = jnp.where(lane < 64, o_h[0], o_h[1]).astype(BF16)
        lse_ref[0, pl.ds(qstart, NA_QB), :] = jnp.where(lane < 64, lse_h[0], lse_h[1])

        @pl.when(g == 0)
        def _():
            qm_ = q_ref[0:NM, :]
            lane_m = lax.broadcasted_iota(jnp.int32, (NM, 128), 1)
            km, vm = ku[NA_KW:NA_KW + NM, :], vu[NA_KW:NA_KW + NM, :]
            om = []
            for h in range(2):
                hm = (lane_m < 64) if h == 0 else (lane_m >= 64)
                s = _dot(jnp.where(hm, qm_, 0.0).astype(BF16), km, NT) * NA_SCALE
                p = jnp.exp(s - jnp.max(s, axis=-1, keepdims=True))
                l = jnp.sum(p, axis=-1, keepdims=True)
                om.append(_dot(p.astype(BF16), vm) / l)
            o_ref[0:NM, :] = jnp.where(lane_m < 64, om[0], om[1]).astype(BF16)
            o_ref[L:T, :] = jnp.zeros((T - L, 128), BF16)
            lse_ref[0, 0:NM, :] = jnp.zeros((NM, 128), F32)
            lse_ref[0, L:T, :] = jnp.zeros((T - L, 128), F32)

    col = lambda off: pl.BlockSpec((T, 128), lambda hp, g: (0, off + hp))
    return pl.pallas_call(
        body, name="na_fwd", grid=(4, NA_GROUPS),
        in_specs=[col(0), col(4), col(8),
                  pl.BlockSpec((1, 2, NA_QB, NA_KU), lambda hp, g: (_na_var(g), hp, 0, 0))],
        out_specs=(pl.BlockSpec((T, 128), lambda hp, g: (0, hp)), pl.BlockSpec((1, T, 128), lambda hp, g: (hp, 0, 0))),
        out_shape=(_sds((T, 512), BF16), _sds((4, T, 128), F32)),
        scratch_shapes=[pltpu.VMEM((NA_KU, 128), BF16), pltpu.VMEM((NA_KU, 128), BF16)],
        compiler_params=_cp(("parallel", "arbitrary")))(p_act, p_act, p_act, bias_tab)


def _na_bwd(p_act, do, lse, bias_tab):
    def body(q_ref, k_ref, v_ref, do_ref, lse_ref, b_ref, dq_ref, dk_ref, dv_ref, db_ref, ku, vu):
        g = pl.program_id(1)

        @pl.when(g == 0)
        def _():
            dq_ref[...] = jnp.zeros((T, 128), F32)
            dk_ref[...] = jnp.zeros((T, 128), F32)
            dv_ref[...] = jnp.zeros((T, 128), F32)

        kstart = _na_load_window(k_ref, ku, g)
        _na_load_window(v_ref, vu, g)
        qstart = pl.multiple_of(NM + NA_QB * g, 16)
        q = q_ref[pl.ds(qstart, NA_QB), :]
        dov = do_ref[pl.ds(qstart, NA_QB), :]
        lsev = lse_ref[0, pl.ds(qstart, NA_QB), :]
        lane = lax.broadcasted_iota(jnp.int32, (NA_QB, 128), 1)
        first = (g == 0) | (g == 1) | (g == NA_GROUPS - 1)
        dq_h = []
        dku = jnp.zeros((NA_KU, 128), F32)
        dvu = jnp.zeros((NA_KU, 128), F32)
        for h in range(2):
            hm = (lane < 64) if h == 0 else (lane >= 64)
            qm = (jnp.where(hm, q, 0.0) * NA_SCALE).astype(BF16)
            dom = jnp.where(hm, dov, 0.0).astype(BF16)
            s = _dot(qm, ku[...], NT) + b_ref[0, h]
            p = jnp.exp(s - lsev[:, 64 * h:64 * h + 1])
            dp = _dot(dom, vu[...], NT)
            delta = jnp.sum(p * dp, axis=-1, keepdims=True)
            ds = p * (dp - delta)

            @pl.when(first)
            def _():
                db_ref[0, h] = ds

            @pl.when(jnp.logical_not(first))
            def _():
                db_ref[0, h] += ds

            dsb = ds.astype(BF16)
            dq_h.append(_dot(dsb, ku[...]) * NA_SCALE)
            dku = dku + _dot(dsb, qm, TN)
            dvu = dvu + _dot(p.astype(BF16), dom, TN)
        dq_ref[pl.ds(qstart, NA_QB), :] = jnp.where(lane < 64, dq_h[0], dq_h[1])
        dk_ref[pl.ds(kstart, NA_KW), :] += dku[0:NA_KW]
        dv_ref[pl.ds(kstart, NA_KW), :] += dvu[0:NA_KW]
        dk_ref[0:NM, :] += dku[NA_KW:NA_KW + NM]
        dv_ref[0:NM, :] += dvu[NA_KW:NA_KW + NM]

        @pl.when(g == 0)
        def _():
            qm_ = q_ref[0:NM, :]
            dom_ = do_ref[0:NM, :]
            lane_m = lax.broadcasted_iota(jnp.int32, (NM, 128), 1)
            km, vm = ku[NA_KW:NA_KW + NM, :], vu[NA_KW:NA_KW + NM, :]
            dqs = []
            dkm = jnp.zeros((NM, 128), F32)
            dvm = jnp.zeros((NM, 128), F32)
            for h in range(2):
                hm = (lane_m < 64) if h == 0 else (lane_m >= 64)
                qh = jnp.where(hm, qm_, 0.0).astype(BF16)
                doh = jnp.where(hm, dom_, 0.0).astype(BF16)
                s = _dot(qh, km, NT) * NA_SCALE
                e = jnp.exp(s - jnp.max(s, axis=-1, keepdims=True))
                p = e / jnp.sum(e, axis=-1, keepdims=True)
                dp = _dot(doh, vm, NT)
                ds = p * (dp - jnp.sum(p * dp, axis=-1, keepdims=True))
                dsb = (ds * NA_SCALE).astype(BF16)
                dqs.append(_dot(dsb, km))
                dkm = dkm + _dot(dsb, qh, TN)
                dvm = dvm + _dot(p.astype(BF16), doh, TN)
            dq_ref[0:NM, :] = jnp.where(lane_m < 64, dqs[0], dqs[1])
            dk_ref[0:NM, :] += dkm
            dv_ref[0:NM, :] += dvm

    col = lambda off: pl.BlockSpec((T, 128), lambda hp, g: (0, off + hp))
    ocol = pl.BlockSpec((T, 128), lambda hp, g: (0, hp))
    bspec = pl.BlockSpec((1, 2, NA_QB, NA_KU), lambda hp, g: (_na_var(g), hp, 0, 0))
    return pl.pallas_call(
        body, name="na_bwd", grid=(4, NA_GROUPS),
        in_specs=[col(0), col(4), col(8), ocol, pl.BlockSpec((1, T, 128), lambda hp, g: (hp, 0, 0)), bspec],
        out_specs=(ocol, ocol, ocol, bspec),
        out_shape=(_sds((T, 512), F32), _sds((T, 512), F32), _sds((T, 512), F32), _sds((3, NA_HEADS, NA_QB, NA_KU), F32)),
        scratch_shapes=[pltpu.VMEM((NA_KU, 128), BF16), pltpu.VMEM((NA_KU, 128), BF16)],
        compiler_params=_cp(("parallel", "arbitrary")))(p_act, p_act, p_act, do, lse, bias_tab)


def _na_rpb_reduce(dbias):
    def body(db_ref, o_ref):
        lane = lax.broadcasted_iota(jnp.int32, (GRID_W, 128), 1)
        row3 = lax.broadcasted_iota(jnp.int32, (15, GRID_W, 128), 1)
        lane3 = lax.broadcasted_iota(jnp.int32, (15, GRID_W, 128), 2)
        accs = []
        for a in range(15):
            acc = jnp.zeros((GRID_W, 128), F32)
            for var in range(3):
                for i in range(4):
                    for j in range(NA_UROWS):
                        if _na_row_offset(var, i, j) == a:
                            pair = db_ref[var, 0, i * 64:(i + 1) * 64, (j // 2) * 128:(j // 2 + 1) * 128]
                            acc = acc + jnp.where((lane < GRID_W) if j % 2 == 0 else (lane >= GRID_W), pair, 0.0)
            accs.append(acc)
        z = jnp.stack(accs)
        z = jnp.where(lane3 < GRID_W, z + jnp.roll(z, GRID_W, axis=2), 0.0)
        for bit in range(6):
            sh = 1 << bit
            z = jnp.where((row3 & sh) != 0, jnp.roll(z, 128 - sh, axis=2), z)
        z = jnp.roll(z, 15, axis=2)
        o_ref[0] = jnp.sum(z, axis=1)

    return pl.pallas_call(
        body, name="na_rpb_reduce", grid=(NA_HEADS,),
        in_specs=[pl.BlockSpec((3, 1, NA_QB, NA_KU), lambda h: (0, h, 0, 0))],
        out_specs=pl.BlockSpec((1, 15, 128), lambda h: (h, 0, 0)), out_shape=_sds((NA_HEADS, 15, 128), F32),
        compiler_params=_cp(("parallel",)))(dbias)


HG_RB = 128
HG_NB = T // HG_RB
HG_SLOTS = HG_NB * 8
HI = lax.Precision.HIGHEST
HG_UNROLL = 4
HG_UNROLL_WIDE = 8


def _chunk_tri(lower):
    r = lax.broadcasted_iota(jnp.int32, (HG_RB, HG_RB), 0)
    c = lax.broadcasted_iota(jnp.int32, (HG_RB, HG_RB), 1)
    same = (r // HG_C) == (c // HG_C)
    keep = (c <= r) if lower else (c >= r)
    return jnp.where(same & keep, 1.0, 0.0).astype(F32)


def _hg_gate_terms(z, lg):
    dl = lg[0:1, :] - lg[1:2, :]
    log_lb = jax.nn.log_sigmoid(dl)
    log_1mlb = jax.nn.log_sigmoid(-dl)
    yz = log_1mlb + jax.nn.log_sigmoid(z)
    log_f = jnp.logaddexp(log_lb, yz)
    snz = jax.nn.sigmoid(-z)
    k = jnp.exp(log_1mlb) * snz
    w2 = jnp.exp(yz - log_f)
    return log_f, k, snz, w2


def _hg_pre(p_act, logits):
    def body(q_ref, zf_ref, zb_ref, lg_ref, qh_ref, kf_ref, bf_ref, kb_ref, bb_ref):
        qh_ref[...] = jax.nn.silu(q_ref[...])
        lf, kf, _, _ = _hg_gate_terms(zf_ref[...], lg_ref[0])
        kf_ref[...] = kf
        bf_ref[...] = jnp.dot(_chunk_tri(True), lf, precision=HI, preferred_element_type=F32)
        lb_, kb, _, _ = _hg_gate_terms(zb_ref[...], lg_ref[1])
        kb_ref[...] = kb
        bb_ref[...] = jnp.dot(_chunk_tri(False), lb_, precision=HI, preferred_element_type=F32)

    blk = lambda c: pl.BlockSpec((HG_RB, 512), lambda i: (i, c))
    ob = pl.BlockSpec((HG_RB, 512), lambda i: (i, 0))
    return pl.pallas_call(
        body, name="hg_pre", grid=(HG_NB,),
        in_specs=[blk(3), blk(4), blk(5), pl.BlockSpec((2, 2, 512), lambda i: (0, 0, 0))],
        out_specs=(ob,) * 5, out_shape=(_sds((T, 512), F32),) * 5,
        compiler_params=_cp(("parallel",)))(p_act, p_act, p_act, logits)


def _bdot(a, b, ca, cb):
    return lax.dot_general(a.astype(BF16), b.astype(BF16), (((ca,), (cb,)), ((0,), (0,))), preferred_element_type=F32)


HG_S = 8
HG_NS = HG_RB // HG_S


def _lane_sums(xs):
    l_io = lax.broadcasted_iota(jnp.int32, (HG_NS, HG_S, HG_S), 2)
    a = jnp.zeros((HG_NS, HG_S, HG_S), F32)
    for j, x in enumerate(xs):
        a = a + jnp.where(l_io == j, jnp.sum(x, axis=-1, keepdims=True), 0.0)
    return a


def _halves(x):
    y = x.reshape(8, 2, HG_S, x.shape[-1])
    return y[:, 0], y[:, 1]


def _join(first, second):
    return jnp.stack([first, second], axis=1).reshape(HG_RB, first.shape[-1])


def _cross_split(rev, b4):
    b_1, b_2 = _halves(b4)
    if rev:
        r = b_2[:, 0:1, :]
        return jnp.exp(b_1 - r), jnp.exp(r - b_2)
    r = b_1[:, HG_S - 1:HG_S, :]
    return jnp.exp(b_2 - r), jnp.exp(r - b_1)


def _hg_scan_fwd(qh, k, b, p_act, rev):
    anchor = 0 if rev else HG_C - 1

    def body(q_ref, k_ref, b_ref, v_ref, o_ref, st_ref, dsc):
        def phase_a(blk, _):
            rows = pl.ds(pl.multiple_of(blk * HG_RB, HG_RB), HG_RB)
            b3 = b_ref[rows, :].reshape(8, HG_C, 128)
            k3 = k_ref[rows, :].reshape(8, HG_C, 128)
            v3 = v_ref[rows, :].reshape(8, HG_C, 128)
            bl = b3[:, anchor:anchor + 1, :]
            kt = k3 * jnp.exp(bl - b3)
            st_ref[0, pl.ds(pl.multiple_of(blk * 8, 8), 8)] = _bdot(v3, kt, 1, 1)
            dsc[pl.ds(pl.multiple_of(blk * 8, 8), 8), :] = jnp.exp(bl[:, 0, :])
            return 0

        lax.fori_loop(0, HG_NB, phase_a, 0, unroll=HG_UNROLL_WIDE)

        def phase_b(n, carry):
            c = (NCHUNK - 1 - n) if rev else n
            u = st_ref[0, c]
            st_ref[0, c] = carry
            return carry * dsc[pl.ds(c, 1), :] + u

        lax.fori_loop(0, NCHUNK // 3, lambda n3, s: phase_b(3 * n3 + 2, phase_b(3 * n3 + 1, phase_b(3 * n3, s))),
                      jnp.zeros((128, 128), F32))
        for c in range(NCHUNK, HG_SLOTS):
            st_ref[0, c] = jnp.zeros((128, 128), F32)

        t_io = lax.broadcasted_iota(jnp.int32, (HG_NS, HG_S, 128), 1)

        def phase_c(blk, _):
            rows = pl.ds(pl.multiple_of(blk * HG_RB, HG_RB), HG_RB)
            b4 = b_ref[rows, :].reshape(HG_NS, HG_S, 128)
            k4 = k_ref[rows, :].reshape(HG_NS, HG_S, 128)
            q4 = q_ref[rows, :].reshape(HG_NS, HG_S, 128)
            v4 = v_ref[rows, :].reshape(HG_NS, HG_S, 128)
            st = st_ref[0, pl.ds(pl.multiple_of(blk * 8, 8), 8)]
            o = _bdot((q4 * jnp.exp(b4)).reshape(8, HG_C, 128), st, 2, 2).reshape(HG_RB, 128)
            terms = []
            for s in range(HG_S):
                ok = (t_io <= s) if rev else (t_io >= s)
                f = jnp.exp(jnp.where(ok, b4 - b4[:, s:s + 1, :], NEG))
                terms.append(q4 * f * k4[:, s:s + 1, :])
            o_in = _bdot(_lane_sums(terms), v4, 2, 1)
            wq, wk = _cross_split(rev, b4)
            q_1, q_2 = _halves(q4)
            k_1, k_2 = _halves(k4)
            v_1, v_2 = _halves(v4)
            o_1, o_2 = _halves(o_in)
            if rev:
                o_1 = o_1 + _bdot(_bdot(q_1 * wq, k_2 * wk, 2, 2), v_2, 2, 1)
            else:
                o_2 = o_2 + _bdot(_bdot(q_2 * wq, k_1 * wk, 2, 2), v_1, 2, 1)
            o_ref[rows, :] = o + _join(o_1, o_2)
            return 0

        lax.fori_loop(0, HG_NB, phase_c, 0, unroll=HG_UNROLL_WIDE)

    col = pl.BlockSpec((T, 128), lambda h: (0, h))
    return pl.pallas_call(
        body, name="hg_scan_bwd_dir" if rev else "hg_scan_fwd_dir", grid=(HG_HEADS,),
        in_specs=[col, col, col, pl.BlockSpec((T, 128), lambda h: (0, 24 + h))],
        out_specs=(col, pl.BlockSpec((1, HG_SLOTS, 128, 128), lambda h: (h, 0, 0, 0))),
        out_shape=(_sds((T, 512), F32), _sds((HG_HEADS, HG_SLOTS, 128, 128), F32)),
        scratch_shapes=[pltpu.VMEM((HG_SLOTS, 128), F32)],
        compiler_params=_cp(("parallel",), 56))(qh, k, b, p_act)


def _hg_scan_bwd(qh, k, b, p_act, st, do, rev):
    anchor = 0 if rev else HG_C - 1

    def body(q_ref, k_ref, b_ref, v_ref, st_ref, do_ref, dq_ref, dk_ref, db_ref, dv_ref, gst, dsc, dbl):
        def phase_a(blk, _):
            rows = pl.ds(pl.multiple_of(blk * HG_RB, HG_RB), HG_RB)
            b3 = b_ref[rows, :].reshape(8, HG_C, 128)
            q3 = q_ref[rows, :].reshape(8, HG_C, 128)
            do3 = do_ref[rows, :].reshape(8, HG_C, 128)
            gst[pl.ds(pl.multiple_of(blk * 8, 8), 8)] = _bdot(do3, q3 * jnp.exp(b3), 1, 1)
            dsc[pl.ds(pl.multiple_of(blk * 8, 8), 8), :] = jnp.exp(b3[:, anchor, :])
            return 0

        lax.fori_loop(0, HG_NB, phase_a, 0, unroll=HG_UNROLL_WIDE)

        def phase_b(n, carry):
            c = n if rev else (NCHUNK - 1 - n)
            w = gst[c]
            gst[c] = carry
            dcv = dsc[pl.ds(c, 1), :]
            dbl[pl.ds(c, 1), :] = dcv * jnp.sum(st_ref[0, c] * carry, axis=0, keepdims=True)
            return carry * dcv + w

        lax.fori_loop(0, NCHUNK // 3, lambda n3, s: phase_b(3 * n3 + 2, phase_b(3 * n3 + 1, phase_b(3 * n3, s))),
                      jnp.zeros((128, 128), F32))
        for c in range(NCHUNK, HG_SLOTS):
            gst[c] = jnp.zeros((128, 128), F32)
            dbl[c:c + 1, :] = jnp.zeros((1, 128), F32)

        t_io = lax.broadcasted_iota(jnp.int32, (HG_NS, HG_S, 128), 1)
        t16 = lax.broadcasted_iota(jnp.int32, (8, HG_C, 128), 1)
        r_io = lax.broadcasted_iota(jnp.int32, (HG_NS, HG_S, HG_S), 1)
        l_io = lax.broadcasted_iota(jnp.int32, (HG_NS, HG_S, HG_S), 2)

        def phase_c(blk, _):
            rows = pl.ds(pl.multiple_of(blk * HG_RB, HG_RB), HG_RB)
            cs = pl.ds(pl.multiple_of(blk * 8, 8), 8)
            b4 = b_ref[rows, :].reshape(HG_NS, HG_S, 128)
            k4 = k_ref[rows, :].reshape(HG_NS, HG_S, 128)
            q4 = q_ref[rows, :].reshape(HG_NS, HG_S, 128)
            v4 = v_ref[rows, :].reshape(HG_NS, HG_S, 128)
            do4 = do_ref[rows, :].reshape(HG_NS, HG_S, 128)
            b3, k3, q3 = (z.reshape(8, HG_C, 128) for z in (b4, k4, q4))
            v3, do3 = v4.reshape(8, HG_C, 128), do4.reshape(8, HG_C, 128)
            s_t = st_ref[0, cs]
            g_t = gst[cs]
            bl = b3[:, anchor:anchor + 1, :]
            ekl = jnp.exp(bl - b3)
            kt = k3 * ekl
            dkt = _bdot(v3, g_t, 2, 1)
            dq = (_bdot(do3, s_t, 2, 1) * jnp.exp(b3)).reshape(HG_NS, HG_S, 128)
            dk = (dkt * ekl).reshape(HG_NS, HG_S, 128)
            dv = _bdot(kt, g_t, 2, 2).reshape(HG_NS, HG_S, 128)
            dbl3 = dbl[cs, :].reshape(8, 1, 128) + jnp.sum(dkt * kt, axis=1, keepdims=True)
            causal = (l_io >= r_io) if rev else (l_io <= r_io)
            da = jnp.where(causal, _bdot(do4, v4, 2, 2), 0.0)
            causal_t = (l_io <= r_io) if rev else (l_io >= r_io)
            dat = jnp.where(causal_t, _bdot(v4, do4, 2, 2), 0.0)
            for s in range(HG_S):
                ok = (t_io <= s) if rev else (t_io >= s)
                f = jnp.exp(jnp.where(ok, b4 - b4[:, s:s + 1, :], NEG))
                dq = dq + da[:, :, s:s + 1] * (f * k4[:, s:s + 1, :])
            terms = []
            for t in range(HG_S):
                ok = (t_io >= t) if rev else (t_io <= t)
                e = jnp.exp(jnp.where(ok, b4[:, t:t + 1, :] - b4, NEG))
                eq = e * q4[:, t:t + 1, :]
                dk = dk + dat[:, :, t:t + 1] * eq
                terms.append(eq * k4)
            dv = dv + _bdot(_lane_sums(terms), do4, 2, 1)
            wq, wk = _cross_split(rev, b4)
            pick = (lambda z: _halves(z)) if rev else (lambda z: _halves(z)[::-1])
            (q_q, _), (_, k_k), (_, v_k), (do_q, _) = pick(q4), pick(k4), pick(v4), pick(do4)
            qx, kx = q_q * wq, k_k * wk
            dq_q = _bdot(_bdot(do_q, v_k, 2, 2), kx, 2, 1) * wq
            dk_k = _bdot(_bdot(v_k, do_q, 2, 2), qx, 2, 1) * wk
            dv_k = _bdot(_bdot(kx, qx, 2, 2), do_q, 2, 1)
            zero = jnp.zeros((8, HG_S, 128), F32)
            place_q = (lambda z: _join(z, zero)) if rev else (lambda z: _join(zero, z))
            place_k = (lambda z: _join(zero, z)) if rev else (lambda z: _join(z, zero))
            dq2 = dq.reshape(HG_RB, 128) + place_q(dq_q)
            dk2 = dk.reshape(HG_RB, 128) + place_k(dk_k)
            dv2 = dv.reshape(HG_RB, 128) + place_k(dv_k)
            dq3, dk3 = dq2.reshape(8, HG_C, 128), dk2.reshape(8, HG_C, 128)
            db = q3 * dq3 - k3 * dk3 + jnp.where(t16 == anchor, dbl3, 0.0)
            dq_ref[rows, :] = dq2
            dk_ref[rows, :] = dk2
            db_ref[rows, :] = db.reshape(HG_RB, 128)
            dv_ref[rows, :] = dv2
            return 0

        lax.fori_loop(0, HG_NB, phase_c, 0, unroll=HG_UNROLL)

    col = pl.BlockSpec((T, 128), lambda h: (0, h))
    return pl.pallas_call(
        body, name="hg_scan_bwd_dir_bwd" if rev else "hg_scan_fwd_dir_bwd", grid=(HG_HEADS,),
        in_specs=[col, col, col, pl.BlockSpec((T, 128), lambda h: (0, 24 + h)),
                  pl.BlockSpec((1, HG_SLOTS, 128, 128), lambda h: (h, 0, 0, 0)), col],
        out_specs=(col,) * 4, out_shape=(_sds((T, 512), F32),) * 4,
        scratch_shapes=[pltpu.VMEM((HG_SLOTS, 128, 128), F32), pltpu.VMEM((HG_SLOTS, 128), F32),
                        pltpu.VMEM((HG_SLOTS, 128), F32)],
        compiler_params=_cp(("parallel",), 56))(qh, k, b, p_act, st, do)


def _row_valid(i, tm):
    r = lax.broadcasted_iota(jnp.int32, (tm, 1), 0) + i * tm
    return r < L


def _hg_post_rows(o, gv, gain_v, valid):
    parts = []
    for h in range(HG_HEADS):
        oh = o[:, 128 * h:128 * (h + 1)]
        parts.append(oh * lax.rsqrt(jnp.mean(oh * oh, axis=-1, keepdims=True) + EPS))
    return jnp.where(valid, jnp.concatenate(parts, axis=1) * gain_v * jax.nn.silu(gv), 0.0)


def _hg_post_bwd_rows(du, o, gv, gain_v, valid):
    duv = jnp.where(valid, du, 0.0)
    sig = jax.nn.sigmoid(gv)
    sg = gv * sig
    dn = duv * gain_v * sg
    do_parts, n_parts = [], []
    for h in range(HG_HEADS):
        sl = slice(128 * h, 128 * (h + 1))
        oh = o[:, sl]
        r = lax.rsqrt(jnp.mean(oh * oh, axis=-1, keepdims=True) + EPS)
        nh = oh * r
        dnh = dn[:, sl]
        do_parts.append(r * (dnh - nh * jnp.mean(dnh * nh, axis=-1, keepdims=True)))
        n_parts.append(nh)
    n = jnp.where(valid, jnp.concatenate(n_parts, axis=1), 0.0)
    do = jnp.where(valid, jnp.concatenate(do_parts, axis=1), 0.0)
    dg = duv * n * gain_v * (sig * (1.0 + gv * (1.0 - sig)))
    return do, dg, jnp.sum(duv * n * sg, axis=0, keepdims=True)


def _hg_pre_bwd(p_act, logits, dq_f, dq_b, dk_f, dk_b, db_f, db_b, dv_f, dv_b, dp_rest):
    def body(q_ref, zf_ref, zb_ref, lg_ref, dqf_ref, dqb_ref, dkf_ref, dkb_ref, dbf_ref, dbb_ref, dvf_ref, dvb_ref, _,
             dp_ref, dlg_ref):
        dq_ref, dzf_ref, dzb_ref, di_ref = (dp_ref.at[:, 512 * c:512 * (c + 1)] for c in range(4))
        i = pl.program_id(0)
        valid = _row_valid(i, HG_RB)
        qv = q_ref[...]
        sig = jax.nn.sigmoid(qv)
        dq_ref[...] = jnp.where(valid, (dqf_ref[...] + dqb_ref[...]) * (sig * (1.0 + qv * (1.0 - sig))), 0.0).astype(BF16)
        di_ref[...] = jnp.where(valid, dvf_ref[...] + dvb_ref[...], 0.0).astype(BF16)
        for d, (z_ref, dk_r, db_r, dz_ref) in enumerate(((zf_ref, dkf_ref, dbf_ref, dzf_ref), (zb_ref, dkb_ref, dbb_ref, dzb_ref))):
            lg = lg_ref[d]
            dl = lg[0:1, :] - lg[1:2, :]
            lb = jax.nn.sigmoid(dl)
            one_m_lb = jax.nn.sigmoid(-dl)
            log_f, _, snz, w2 = _hg_gate_terms(z_ref[...], lg)
            dbv = jnp.where(valid, db_r[...], 0.0)
            dkv = jnp.where(valid, dk_r[...], 0.0)
            dlf = jnp.dot(_chunk_tri(d == 1), dbv, precision=HI, preferred_element_type=F32)
            sz = 1.0 - snz
            dz_ref[...] = (dlf * w2 * snz - dkv * one_m_lb * sz * snz).astype(BF16)
            dlb = jnp.sum(dlf * snz * jnp.exp(-log_f) - dkv * snz, axis=0, keepdims=True)
            dl0 = dlb * lb * one_m_lb
            part = jnp.concatenate([dl0, -dl0], axis=0)

            @pl.when(i == 0)
            def _():
                dlg_ref[d] = part

            @pl.when(i > 0)
            def _():
                dlg_ref[d] += part

    blk = lambda c: pl.BlockSpec((HG_RB, 512), lambda i: (i, c))
    ob = pl.BlockSpec((HG_RB, 512), lambda i: (i, 0))
    lgs = pl.BlockSpec((2, 2, 512), lambda i: (0, 0, 0))
    return pl.pallas_call(
        body, name="hg_pre_bwd", grid=(HG_NB,),
        in_specs=[blk(3), blk(4), blk(5), lgs] + [ob] * 8 + [ANY],
        out_specs=(pl.BlockSpec((HG_RB, 2048), lambda i: (i, 0)), lgs),
        out_shape=(_sds(dp_rest.shape, BF16), _sds((2, 2, 512), F32)), input_output_aliases={12: 0},
        compiler_params=_cp(("arbitrary",)))(p_act, p_act, p_act, logits, dq_f, dq_b, dk_f, dk_b, db_f, db_b, dv_f, dv_b,
                                             dp_rest)


def _mix_fwd(o_na, o_f, o_b, gain, w_na, w_hg, p_act):
    def body(ona_ref, of_ref, ob_ref, g_ref, gain_ref, wna_ref, whg_ref, gna_ref, ghg_ref, o_ref, u_ref):
        u = _hg_post_rows(of_ref[...] + ob_ref[...], g_ref[...], gain_ref[...], _row_valid(pl.program_id(0), TM_B)).astype(BF16)
        u_ref[...] = u
        y_na = _dot(ona_ref[...], wna_ref[...])
        y_hg = _dot(u, whg_ref[...])
        o_ref[...] = (jax.nn.sigmoid(gna_ref[...]) * y_na + jax.nn.sigmoid(ghg_ref[...]) * y_hg).astype(BF16)

    act = pl.BlockSpec((TM_B, 512), lambda i: (i, 0))
    wsp = pl.BlockSpec((512, D), lambda i: (0, 0))
    return pl.pallas_call(
        body, name="mix_fwd", grid=(T // TM_B,),
        in_specs=[act, act, act, pl.BlockSpec((TM_B, 512), lambda i: (i, 7)), pl.BlockSpec((1, 512), lambda i: (0, 0)),
                  wsp, wsp, pl.BlockSpec((TM_B, D), lambda i: (i, 4)), pl.BlockSpec((TM_B, D), lambda i: (i, 5))],
        out_specs=(pl.BlockSpec((TM_B, D), lambda i: (i, 0)), act), out_shape=(_sds((T, D), BF16), _sds((T, 512), BF16)),
        compiler_params=_cp(("parallel",)))(o_na, o_f, o_b, p_act, gain, w_na, w_hg, p_act, p_act)


DP_REST = IN_COLS - 1536


def _mix_bwd(o_na, u_hg, o_f, o_b, gain, w_na, w_hg, p_act, dmix):
    ni = T // TM_B

    def body(ona_ref, uhg_ref, of_ref, ob_ref, g_ref, gain_ref, wna_ref, whg_ref, gna_ref, ghg_ref, dmix_ref,
             dp_ref, dwna_ref, dwhg_ref, dona_ref, do_ref, dgain_ref, acc_na, acc_hg):
        i = pl.program_id(0)
        dg_ref, dgna_ref, dghg_ref = dp_ref.at[:, 2048:2560], dp_ref.at[:, 2560:3584], dp_ref.at[:, 3584:4608]
        dm = dmix_ref[...].astype(F32)
        dxs = []
        for x_ref, w_ref, gt_ref, dgt_ref, dw_ref, acc in (
                (ona_ref, wna_ref, gna_ref, dgna_ref, dwna_ref, acc_na), (uhg_ref, whg_ref, ghg_ref, dghg_ref, dwhg_ref, acc_hg)):
            xv = x_ref[...]
            y = _dot(xv, w_ref[...])
            sg = jax.nn.sigmoid(gt_ref[...])
            dgt_ref[...] = (dm * y * sg * (1.0 - sg)).astype(BF16)
            dy = (dm * sg).astype(BF16)
            dxs.append(_dot(dy, w_ref[...], NT))
            part = _dot(xv, dy, TN)

            @pl.when(i == 0)
            def _():
                acc[...] = part

            @pl.when(i > 0)
            def _():
                acc[...] += part

            @pl.when(i == ni - 1)
            def _():
                dw_ref[...] = acc[...].astype(BF16)

        dona_ref[...] = dxs[0]
        do, dg, gpart = _hg_post_bwd_rows(dxs[1], of_ref[...] + ob_ref[...], g_ref[...], gain_ref[...], _row_valid(i, TM_B))
        do_ref[...] = do
        dg_ref[...] = dg.astype(BF16)

        @pl.when(i == 0)
        def _():
            dgain_ref[...] = gpart

        @pl.when(i > 0)
        def _():
            dgain_ref[...] += gpart

    act = pl.BlockSpec((TM_B, 512), lambda i: (i, 0))
    wsp = pl.BlockSpec((512, D), lambda i: (0, 0))
    rblk = pl.BlockSpec((TM_B, D), lambda i: (i, 0))
    vec = pl.BlockSpec((1, 512), lambda i: (0, 0))
    return pl.pallas_call(
        body, name="mix_bwd", grid=(ni,),
        in_specs=[act, act, act, act, pl.BlockSpec((TM_B, 512), lambda i: (i, 7)), vec, wsp, wsp,
                  pl.BlockSpec((TM_B, D), lambda i: (i, 4)), pl.BlockSpec((TM_B, D), lambda i: (i, 5)), rblk],
        out_specs=(pl.BlockSpec((TM_B, DP_REST), lambda i: (i, 0)), wsp, wsp, act, act, vec),
        out_shape=(_sds((T, DP_REST), BF16), _sds((512, D), BF16), _sds((512, D), BF16),
                   _sds((T, 512), F32), _sds((T, 512), F32), _sds((1, 512), F32)),
        scratch_shapes=[pltpu.VMEM((512, D), F32), pltpu.VMEM((512, D), F32)],
        compiler_params=_cp(("arbitrary",)))(o_na, u_hg, o_f, o_b, p_act, gain, w_na, w_hg, p_act, p_act, dmix)


def _wo_fwd(mix, w_o, h0, g_mlp):
    def body(mix_ref, w_ref, h0_ref, g_ref, h1_ref, m_ref):
        h1 = h0_ref[...] + _dot(mix_ref[...], w_ref[...])
        h1_ref[...] = h1
        r = lax.rsqrt(jnp.mean(h1 * h1, axis=-1, keepdims=True) + EPS)
        m_ref[...] = (h1 * r * g_ref[...]).astype(BF16)

    blk = pl.BlockSpec((TM_B, D), lambda i: (i, 0))
    return pl.pallas_call(
        body, name="wo_fwd", grid=(T // TM_B,),
        in_specs=[blk, pl.BlockSpec((D, D), lambda i: (0, 0)), blk, pl.BlockSpec((1, D), lambda i: (0, 0))],
        out_specs=(blk, blk), out_shape=(_sds((T, D), F32), _sds((T, D), BF16)),
        compiler_params=_cp(("parallel",)))(mix, w_o, h0, g_mlp)


def _wo_bwd(dh1_b, w_o, mix):
    ni = T // TM_B

    def body(dh_ref, w_ref, mix_ref, dmix_ref, dw_ref, acc):
        i = pl.program_id(0)
        dh = dh_ref[...]
        dmix_ref[...] = _dot(dh, w_ref[...], NT).astype(BF16)
        part = _dot(mix_ref[...], dh, TN)

        @pl.when(i == 0)
        def _():
            acc[...] = part

        @pl.when(i > 0)
        def _():
            acc[...] += part

        @pl.when(i == ni - 1)
        def _():
            dw_ref[...] = acc[...].astype(BF16)

    blk = pl.BlockSpec((TM_B, D), lambda i: (i, 0))
    wsp = pl.BlockSpec((D, D), lambda i: (0, 0))
    return pl.pallas_call(
        body, name="wo_bwd", grid=(ni,), in_specs=[blk, wsp, blk], out_specs=(blk, wsp),
        out_shape=(_sds((T, D), BF16), _sds((D, D), BF16)), scratch_shapes=[pltpu.VMEM((D, D), F32)],
        compiler_params=_cp(("arbitrary",)))(dh1_b, w_o, mix)


FF_B = D_FF // NDEV


def _loss_rows(xv, gv, tv, row0):
    r_io = lax.broadcasted_iota(jnp.int32, (xv.shape[0], 1), 0) + row0
    valid = (r_io >= NM) & (r_io < L)
    r = lax.rsqrt(jnp.mean(xv * xv, axis=-1, keepdims=True) + EPS)
    xh = xv * r
    err = jnp.where(valid, xh * gv - tv, 0.0)
    lpart = 0.5 * jnp.sum(jnp.sum(err * err, axis=-1, keepdims=True) * (1.0 / D), axis=0, keepdims=True)
    dy = err * (1.0 / D)
    dxh = dy * gv
    dh = r * (dxh - xh * jnp.mean(dxh * xh, axis=-1, keepdims=True))
    return lpart, dh, jnp.sum(dy * xh, axis=0, keepdims=True)


def _mlp_fwd_loss(m, wup_g, wdown_g, h1, g_final, tgt):
    nsub = TM_MM // TM_E

    def body(m_ref, wu_ref, wd_ref, h1_ref, g_ref, t_ref, loss_ref, dh_ref, dhb_ref, dg_ref, h2):
        i, j = pl.program_id(0), pl.program_id(1)
        up = jnp.maximum(_dot(m_ref[...], wu_ref[0]), 0.0)
        part = _dot((up * up).astype(BF16), wd_ref[0])

        @pl.when(j == 0)
        def _():
            h2[...] = h1_ref[...] + part

        @pl.when(j > 0)
        def _():
            h2[...] += part

        @pl.when(j == NDEV - 1)
        def _():
            lsum = jnp.zeros((1, 1), F32)
            gsum = jnp.zeros((1, D), F32)
            for s in range(nsub):
                rows = slice(s * TM_E, (s + 1) * TM_E)
                lpart, dh, gpart = _loss_rows(h2[rows, :], g_ref[...], t_ref[rows, :], i * TM_MM + s * TM_E)
                dh_ref[rows, :] = dh
                dhb_ref[rows, :] = dh.astype(BF16)
                lsum = lsum + lpart
                gsum = gsum + gpart
            lsum = jnp.broadcast_to(lsum, (1, 128))

            @pl.when(i == 0)
            def _():
                loss_ref[...] = lsum
                dg_ref[...] = gsum

            @pl.when(i > 0)
            def _():
                loss_ref[...] += lsum
                dg_ref[...] += gsum

    blk = pl.BlockSpec((TM_MM, D), lambda i, j: (i, 0))
    vec = pl.BlockSpec((1, D), lambda i, j: (0, 0))
    return pl.pallas_call(
        body, name="mlp_fwd_loss", grid=(T // TM_MM, NDEV),
        in_specs=[blk, pl.BlockSpec((1, D, FF_B), lambda i, j: (j, 0, 0)), pl.BlockSpec((1, FF_B, D), lambda i, j: (j, 0, 0)),
                  blk, vec, blk],
        out_specs=(pl.BlockSpec((1, 128), lambda i, j: (0, 0)), blk, blk, vec),
        out_shape=(_sds((1, 128), F32), _sds((T, D), F32), _sds((T, D), BF16), _sds((1, D), F32)),
        scratch_shapes=[pltpu.VMEM((TM_MM, D), F32)],
        compiler_params=_cp(("arbitrary", "arbitrary"), 56))(m, wup_g, wdown_g, h1, g_final, tgt)


def _mlp_bwd(m, dh2_b, wup_g, wdown_g, h1, g_mlp, dh2):
    ni = T // TM_B
    nsub = TM_B // TM_E

    def body(m_ref, dh_ref, wu_ref, wd_ref, h1_ref, g_ref, dres_ref, dwu_ref, dwd_ref, dh1_ref, dh1b_ref, dg_ref,
             dm_ref, acc_u, acc_d):
        j, i = pl.program_id(0), pl.program_id(1)
        rows = pl.ds(pl.multiple_of(i * TM_B, TM_B), TM_B)
        mv, dh = m_ref[...], dh_ref[...]
        r = jnp.maximum(_dot(mv, wu_ref[0]), 0.0)
        act = (r * r).astype(BF16)
        dact = _dot(dh, wd_ref[0], NT)
        dup = (dact * (2.0 * r)).astype(BF16)
        pd = _dot(act, dh, TN)
        pu = _dot(mv, dup, TN)
        dmv = _dot(dup, wu_ref[0], NT)

        @pl.when(i == 0)
        def _():
            acc_u[...] = pu
            acc_d[...] = pd

        @pl.when(i > 0)
        def _():
            acc_u[...] += pu
            acc_d[...] += pd

        @pl.when(i == ni - 1)
        def _():
            dwu_ref[0] = acc_u[...].astype(BF16)
            dwd_ref[0] = acc_d[...].astype(BF16)

        @pl.when(j == 0)
        def _():
            dm_ref[rows, :] = dmv

        @pl.when(j > 0)
        def _():
            dm_ref[rows, :] += dmv

        @pl.when(j == NDEV - 1)
        def _():
            gsum = jnp.zeros((1, D), F32)
            for s in range(nsub):
                sub = slice(s * TM_E, (s + 1) * TM_E)
                dm_rows = dm_ref[pl.ds(pl.multiple_of(i * TM_B + s * TM_E, TM_E), TM_E), :]
                dx, gpart = _norm_bwd_rows(h1_ref[sub, :], g_ref[...], dm_rows, dres_ref[sub, :])
                dh1_ref[sub, :] = dx
                dh1b_ref[sub, :] = dx.astype(BF16)
                gsum = gsum + gpart

            @pl.when(i == 0)
            def _():
                dg_ref[...] = gsum

            @pl.when(i > 0)
            def _():
                dg_ref[...] += gsum

    blk = pl.BlockSpec((TM_B, D), lambda j, i: (i, 0))
    late = pl.BlockSpec((TM_B, D), lambda j, i: (jnp.where(j == NDEV - 1, i, 0), 0))
    vec = pl.BlockSpec((1, D), lambda j, i: (0, 0))
    wus = pl.BlockSpec((1, D, FF_B), lambda j, i: (j, 0, 0))
    wds = pl.BlockSpec((1, FF_B, D), lambda j, i: (j, 0, 0))
    return pl.pallas_call(
        body, name="mlp_bwd", grid=(NDEV, ni), in_specs=[blk, blk, wus, wds, late, vec, late],
        out_specs=(wus, wds, late, late, vec),
        out_shape=(_sds((NDEV, D, FF_B), BF16), _sds((NDEV, FF_B, D), BF16), _sds((T, D), F32), _sds((T, D), BF16),
                   _sds((1, D), F32)),
        scratch_shapes=[pltpu.VMEM((T, D), F32), pltpu.VMEM((D, FF_B), F32), pltpu.VMEM((FF_B, D), F32)],
        compiler_params=_cp(("arbitrary", "arbitrary"), 56))(m, dh2_b, wup_g, wdown_g, h1, g_mlp, dh2)


def _adamw(parts, w, m, v, name):
    rr, cc = w.shape
    nslot = parts.shape[0]
    tr = rr
    for cand in (256, 128, 64):
        if rr % cand == 0 and rr > cand:
            tr = cand
            break
    c1 = 1.0 - ADAM_B1 ** ADAM_STEP
    c2 = 1.0 - ADAM_B2 ** ADAM_STEP

    def body(p_ref, w_ref, m_ref, v_ref, g_ref, d_ref, nm_ref, nv_ref):
        g = p_ref[0].astype(F32)
        for s in range(1, nslot):
            g = g + p_ref[s].astype(F32)
        mn = ADAM_B1 * m_ref[...] + (1.0 - ADAM_B1) * g
        vn = ADAM_B2 * v_ref[...] + (1.0 - ADAM_B2) * (g * g)
        g_ref[...] = g
        nm_ref[...] = mn
        nv_ref[...] = vn
        d_ref[...] = -ADAM_LR * ((mn / c1) / (jnp.sqrt(vn / c2) + ADAM_EPS) + ADAM_WD * w_ref[...])

    blk = pl.BlockSpec((tr, cc), lambda i: (i, 0))
    return pl.pallas_call(
        body, name=name, grid=(rr // tr,),
        in_specs=[pl.BlockSpec((nslot, tr, cc), lambda i: (0, i, 0)), blk, blk, blk],
        out_specs=(blk,) * 4, out_shape=(_sds((rr, cc), F32),) * 4,
        compiler_params=_cp(("parallel",)))(parts, w, m, v)


RPB_N = NA_HEADS * 15 * 31
RPB_PAD = 4096
OWN_ROWS = NM + 8


def _pad_rows(a, rows):
    return jnp.pad(a, ((0, rows - a.shape[0]),) + ((0, 0),) * (a.ndim - 1))


def _pack_owned(meta_blk, lb_blk):
    return jnp.concatenate([meta_blk, _pad_rows(lb_blk.reshape(2, 128), 8)], axis=0)


LOSS_ROW = 28


def _pack_replicated(n_mix, n_mlp, n_final, hg_gain, rpb, loss_row=None):
    flat = _pad_rows(rpb.reshape(RPB_N), RPB_PAD)
    gain8 = _pad_rows(hg_gain.reshape(4, 128), 8)
    if loss_row is not None:
        gain8 = gain8 + jnp.pad(loss_row, ((LOSS_ROW - 24, 31 - LOSS_ROW), (0, 0)))
    return jnp.concatenate([n_mix.reshape(8, 128), n_mlp.reshape(8, 128), n_final.reshape(8, 128), gain8,
                            flat.reshape(32, 128)], axis=0)


def _unpack_replicated(a):
    return (a[0:8].reshape(1, D), a[8:16].reshape(1, D), a[16:24].reshape(D), a[24:28].reshape(1, 512),
            a[32:64].reshape(RPB_PAD)[:RPB_N].reshape(1, NA_HEADS, 15, 31))


def kernel(x, meta_tokens, w_in, w_na_out, w_hg_out, w_o, w_up, w_down, norm_mix, norm_mlp, norm_final, hg_norm, na_rpb, hg_lb_logits, loss_target, m_meta_tokens, m_w_in, m_w_na_out, m_w_hg_out, m_w_o, m_w_up, m_w_down, m_norm_mix, m_norm_mlp, m_norm_final, m_hg_norm, m_na_rpb, m_hg_lb_logits, v_meta_tokens, v_w_in, v_w_na_out, v_w_hg_out, v_w_o, v_w_up, v_w_down, v_norm_mix, v_norm_mlp, v_norm_final, v_hg_norm, v_na_rpb, v_hg_lb_logits):
    owned = _pack_owned(meta_tokens, hg_lb_logits)
    first_masks = (ALL_PEERS, SAME_CORE_AND_SIBLING)
    first, tok = _exchange_start([owned, w_in[0].astype(BF16)], [False] * 2, "gather_first_start", first_masks)
    bias_tab = _na_bias_table(_tie(jnp.pad(na_rpb[0], ((0, 0), (0, 0), (0, 128 - 31))), tok, "tie_bias_table"))
    later = [w[0].astype(BF16) for w in (w_na_out, w_hg_out, w_o, w_up, w_down)]
    lead = jnp.zeros((NM, D), F32) + tok[0, 0]
    h0_rows = jnp.concatenate([lead, x[0], jnp.zeros((T - L, D), F32)], axis=0)
    tgt = jnp.concatenate([lead, loss_target[0], jnp.zeros((T - L, D), F32)], axis=0)
    (owned_g, _), first = _exchange_wait(first, [False] * 2, [h0_rows], "gather_small_wait", first_masks, which=(0,))
    meta_full = jnp.transpose(owned_g[:, 0:NM, :], (1, 0, 2)).reshape(NM, D)
    logits = jnp.transpose(owned_g[:, NM:NM + 2, :].reshape(NDEV, 2, 2, 64), (1, 2, 0, 3)).reshape(2, 2, 512)
    h0 = lax.dynamic_update_slice(h0_rows, meta_full, (0, 0))
    a, a_t = _norm_fwd_t(h0, norm_mix, "norm_mix_fwd")
    (_, win_l), _ = _exchange_wait(first, [False] * 2, [a, logits, tgt, bias_tab] + later, "gather_first_wait", first_masks,
                                   which=(1,))
    (win_g,) = _forward_to_sibling([win_l], "gather_first_forward")
    later[0] = _tie(later[0], win_g, "tie_gather_rest")
    gather_rest, tok = _exchange_start(later, [False] * 5, "gather_rest_start")
    win_g = _tie(win_g, tok, "tie_inproj")

    p_act = _inproj_fwd(a, win_g)
    o_na, lse = _na_fwd(p_act, bias_tab)
    qh, k_f, b_f, k_b, b_b = _hg_pre(p_act, logits)
    o_f, st_f = _hg_scan_fwd(qh, k_f, b_f, p_act, False)
    o_b, st_b = _hg_scan_fwd(qh, k_b, b_b, p_act, True)
    (wna_g, whg_g, wo_g, _, _), gather_rest = _exchange_wait(
        gather_rest, [False] * 5, [o_f, o_b, o_na], "gather_rest_wait_a", which=(0, 1, 2))
    w_na_full = jnp.transpose(wna_g, (1, 0, 2)).reshape(512, D)
    w_hg_full = jnp.transpose(whg_g, (1, 0, 2)).reshape(512, D)
    mix, u_hg = _mix_fwd(o_na, o_f, o_b, hg_norm, w_na_full, w_hg_full, p_act)
    h1, m_act = _wo_fwd(mix, wo_g.reshape(D, D), h0, norm_mlp)
    (_, _, wo_g, wup_g, wdown_g), _ = _exchange_wait(gather_rest, [False] * 5, [m_act], "gather_rest_wait_b", which=(3, 4))
    w_o_full = wo_g.reshape(D, D)
    loss_part, dh2, dh2_b, d_nfinal = _mlp_fwd_loss(m_act, wup_g, wdown_g, h1, norm_final.reshape(1, D), tgt)

    dwup_p, dwdown_p, dh1, dh1_b, d_nmlp = _mlp_bwd(m_act, dh2_b, wup_g, wdown_g, h1, norm_mlp, dh2)
    sc_mlp, tok = _exchange_start([dwup_p, dwdown_p], [True] * 2, "scatter_mlp_start")
    dmix, dwo = _wo_bwd(_tie(dh1_b, tok, "tie_wo_bwd"), w_o_full, mix)
    sc_wo, tok = _exchange_start([dwo.reshape(NDEV, D // NDEV, D)], [True], "scatter_wo_start")
    dp_rest, dwna, dwhg, do_na, do_hg, d_gain = _mix_bwd(
        o_na, u_hg, o_f, o_b, hg_norm, w_na_full, w_hg_full, p_act, _tie(dmix, tok, "tie_mix_bwd"))
    owner_cols = lambda w: jnp.transpose(w.reshape(512, NDEV, D // NDEV), (1, 0, 2))
    sc_br, tok = _exchange_start([owner_cols(dwna), owner_cols(dwhg)], [True] * 2, "scatter_branch_start")
    do_hg = _tie(do_hg, tok, "tie_hg_scan_bwd")
    dq_f, dk_f, db_f, dv_f = _hg_scan_bwd(qh, k_f, b_f, p_act, st_f, do_hg, False)
    dq_b, dk_b, db_b, dv_b = _hg_scan_bwd(qh, k_b, b_b, p_act, st_b, do_hg, True)
    dp_rest, d_logits = _hg_pre_bwd(p_act, logits, dq_f, dq_b, dk_f, dk_b, db_f, db_b, dv_f, dv_b, dp_rest)
    dq_na, dk_na, dv_na, dbias = _na_bwd(p_act, do_na, lse, bias_tab)
    dp_na = jnp.concatenate([dq_na.astype(BF16), dk_na.astype(BF16), dv_na.astype(BF16)], axis=1)
    dwin_p = _inproj_bwd_dw(a_t, dp_na, dp_rest)
    far_mine, far_other = _far_slots()
    dwin_p = _add_into_slot(dwin_p, _sibling_swap_far(dwin_p, "pair_swap_in"), far_mine, "pair_add_in")
    sc_in, tok = _exchange_start([dwin_p], [True], "scatter_in_start", ALL_BUT_FAR_OTHER_CORE, absent=far_other)
    dh0, d_nmix, grad_x = _inproj_bwd_da(_tie(dp_na, tok, "tie_inproj_bwd_da"), dp_rest, win_g, h0, norm_mix, dh1)
    d_rpb = _na_rpb_reduce(_tie(dbias, tok, "tie_rpb_reduce"))[:, :, :31]

    res = {}

    def update(nm, parts, w, mm, vv):
        res[nm] = [r[None] for r in _adamw(parts, w[0], mm[0], vv[0], "adamw_" + nm)]
        return res[nm][1]

    wup_r, wdown_r = _exchange_wait(sc_mlp, [True] * 2, [dh0, d_rpb], "scatter_mlp_wait")
    update("w_up", wup_r, w_up, m_w_up, v_w_up)
    last = update("w_down", wdown_r, w_down, m_w_down, v_w_down)
    (wo_r,) = _exchange_wait(sc_wo, [True], [last], "scatter_wo_wait")
    last = update("w_o", wo_r, w_o, m_w_o, v_w_o)
    wna_r, whg_r = _exchange_wait(sc_br, [True] * 2, [last], "scatter_branch_wait")
    update("w_na_out", wna_r, w_na_out, m_w_na_out, v_w_na_out)
    last = update("w_hg_out", whg_r, w_hg_out, m_w_hg_out, v_w_hg_out)

    d_meta = jnp.transpose(dh0[0:NM].reshape(NM, NDEV, 128), (1, 0, 2))
    d_lg = jnp.transpose(d_logits.reshape(2, 2, NDEV, 64), (2, 0, 1, 3)).reshape(NDEV, 2, 128)
    owned_p = jnp.concatenate([d_meta, jnp.pad(d_lg, ((0, 0), (0, OWN_ROWS - NM - 2), (0, 0)))], axis=1)
    repl_p = _pack_replicated(d_nmix, d_nmlp, d_nfinal, d_gain, d_rpb, loss_part)
    grad_x = grad_x[None]
    done_first = [grad_x] + [res[nm][0] for nm in ("w_up", "w_down", "w_o", "w_na_out", "w_hg_out")]
    owned_r, repl_r = _exchange([owned_p, repl_p], [True, False], "scatter_small", done_first)
    own = _adamw(owned_r, owned, _pack_owned(m_meta_tokens, m_hg_lb_logits), _pack_owned(v_meta_tokens, v_hg_lb_logits),
                 "adamw_owned_small")
    res["meta_tokens"] = [r[0:NM] for r in own]
    res["hg_lb_logits"] = [r[NM:NM + 2].reshape(2, 2, 64) for r in own]
    rep = _adamw(repl_r, _pack_replicated(norm_mix, norm_mlp, norm_final, hg_norm, na_rpb),
                 _pack_replicated(m_norm_mix, m_norm_mlp, m_norm_final, m_hg_norm, m_na_rpb),
                 _pack_replicated(v_norm_mix, v_norm_mlp, v_norm_final, v_hg_norm, v_na_rpb), "adamw_replicated")
    for q in range(4):
        um = _unpack_replicated(rep[q])
        for nm, val in zip(("norm_mix", "norm_mlp", "norm_final", "hg_norm", "na_rpb"), um):
            res.setdefault(nm, [None] * 4)[q] = val
    (win_r,) = _exchange_wait(sc_in, [True], [rep[1], own[1]], "scatter_in_wait", ALL_BUT_FAR_OTHER_CORE)
    update("w_in", win_r, w_in, m_w_in, v_w_in)

    loss = jnp.sum(repl_r[:, LOSS_ROW, 0])
    order = ("meta_tokens", "w_in", "w_na_out", "w_hg_out", "w_o", "w_up", "w_down", "norm_mix", "norm_mlp", "norm_final",
             "hg_norm", "na_rpb", "hg_lb_logits")
    outs = [loss, grad_x]
    for q in range(4):
        outs += [res[nm][q] for nm in order]
    return tuple(outs)
```

```python
import functools

import numpy as np
import jax
import jax.numpy as jnp
from jax import lax
from jax.experimental import pallas as pl
from jax.experimental.pallas import tpu as pltpu

F32 = jnp.float32
BF16 = jnp.bfloat16

D = 1024
SEQ = 2048
NM = 16
L = SEQ + NM
T = 2176
NDEV = 8
EPS = 1e-6
GRID_W = 64
ROWS = SEQ // GRID_W
NA_HEADS = 8
NA_DH = 64
NA_SCALE = NA_DH ** -0.5
HG_HEADS = 4
HG_C = 16
NCHUNK = L // HG_C
D_FF = 4096
IN_COLS = 6144
NEG = -1e30

ADAM_LR = 0.001
ADAM_B1 = 0.9
ADAM_B2 = 0.999
ADAM_EPS = 1e-08
ADAM_WD = 0.01
ADAM_STEP = 10

MESH_ID = pl.DeviceIdType.MESH
ANY = pl.BlockSpec(memory_space=pl.ANY)

NN = (((1,), (0,)), ((), ()))
NT = (((1,), (1,)), ((), ()))
TN = (((0,), (0,)), ((), ()))


def _cp(sem=None, vmem_mb=48):
    return pltpu.CompilerParams(dimension_semantics=sem, vmem_limit_bytes=vmem_mb * 1024 * 1024)


def _dot(a, b, dims=NN):
    return lax.dot_general(a, b, dims, preferred_element_type=F32)


def _sds(shape, dtype):
    return jax.ShapeDtypeStruct(shape, dtype)


HBM = pl.BlockSpec(memory_space=pltpu.HBM)
SEM = pl.BlockSpec(memory_space=pltpu.SEMAPHORE)
EFFECT = pltpu.SideEffectType.DATAFLOW_SIDE_EFFECTING


def _exchange(arrs, scatter, name, after=()):
    n = len(arrs)
    after = list(after)
    out_shapes = []
    for a, sc in zip(arrs, scatter):
        out_shapes.append(_sds(a.shape if sc else (NDEV,) + a.shape, a.dtype))

    def body(*refs):
        ins, outs = refs[:n], refs[n + len(after):2 * n + len(after)]
        send_sems, recv_sems, loc_sems = refs[2 * n + len(after):]
        me = 4 * lax.axis_index("x") + 2 * lax.axis_index("y") + lax.axis_index("c")
        copies = []
        for k in range(n):
            src_me = ins[k].at[me] if scatter[k] else ins[k]
            loc = pltpu.make_async_copy(src_me, outs[k].at[me], loc_sems.at[k])
            loc.start()
            copies.append(loc)
        remote = sum(_peer_copies(ins, outs, scatter, send_sems, recv_sems), [])
        for cp in remote:
            cp.start()
        for cp in remote:
            cp.wait_recv()
        for cp in remote:
            cp.wait_send()
        for cp in copies:
            cp.wait()

    return pl.pallas_call(
        body, name=name, out_shape=tuple(out_shapes), in_specs=[ANY] * (n + len(after)), out_specs=tuple([ANY] * n),
        scratch_shapes=[pltpu.SemaphoreType.DMA((n * (NDEV - 1),)), pltpu.SemaphoreType.DMA((n * (NDEV - 1),)),
                        pltpu.SemaphoreType.DMA((n,))],
    )(*arrs, *after)


def _forward_to_sibling(bufs, name):
    n = len(bufs)

    def body(*refs):
        ins, outs = refs[:n], refs[n:2 * n]
        send_sems, recv_sems = refs[2 * n:]
        x, y, c = lax.axis_index("x"), lax.axis_index("y"), lax.axis_index("c")
        copies = []
        for k in range(n):
            for j, (cx, cy) in enumerate(((1 - x, y), (x, 1 - y), (1 - x, 1 - y))):
                slot = 4 * cx + 2 * cy + c
                copies.append(pltpu.make_async_remote_copy(
                    src_ref=ins[k].at[slot], dst_ref=outs[k].at[slot], send_sem=send_sems.at[3 * k + j],
                    recv_sem=recv_sems.at[3 * k + j], device_id=(x, y, 1 - c), device_id_type=MESH_ID))
        for cp in copies:
            cp.start()
        for cp in copies:
            cp.wait_recv()
        for cp in copies:
            cp.wait_send()

    return pl.pallas_call(
        body, name=name, out_shape=tuple(_sds(b.shape, b.dtype) for b in bufs), in_specs=[ANY] * n,
        out_specs=tuple([ANY] * n), input_output_aliases={k: k for k in range(n)},
        scratch_shapes=[pltpu.SemaphoreType.DMA((3 * n,)), pltpu.SemaphoreType.DMA((3 * n,))],
    )(*bufs)


ALL_PEERS = tuple(range(1, NDEV))
SAME_CORE_AND_SIBLING = (1, 2, 4, 6)
ALL_BUT_FAR_OTHER_CORE = (1, 2, 3, 4, 5, 6)


def _far_slots():
    far = 4 * (1 - lax.axis_index("x")) + 2 * (1 - lax.axis_index("y"))
    core = lax.axis_index("c")
    return far + core, far + 1 - core


def _sibling_swap_far(parts, name):
    def body(x_ref, o_ref, send_sem, recv_sem):
        sib = (lax.axis_index("x"), lax.axis_index("y"), 1 - lax.axis_index("c"))
        cp = pltpu.make_async_remote_copy(src_ref=x_ref.at[_far_slots()[1]], dst_ref=o_ref.at[0], send_sem=send_sem,
                                          recv_sem=recv_sem, device_id=sib, device_id_type=MESH_ID)
        cp.start()
        cp.wait()

    return pl.pallas_call(
        body, name=name, out_shape=_sds((1,) + parts.shape[1:], parts.dtype), in_specs=[ANY], out_specs=ANY,
        scratch_shapes=[pltpu.SemaphoreType.DMA(()), pltpu.SemaphoreType.DMA(())])(parts)


def _add_into_slot(parts, other, slot, name):
    _, rr, cc = parts.shape

    def body(slot_ref, p_ref, o_ref, out_ref):
        del slot_ref
        out_ref[...] = (p_ref[...].astype(F32) + o_ref[...].astype(F32)).astype(BF16)

    mine = pl.BlockSpec((1, rr // 2, cc), lambda j, s: (s[0], j, 0))
    grid_spec = pltpu.PrefetchScalarGridSpec(
        num_scalar_prefetch=1, grid=(2,),
        in_specs=[mine, pl.BlockSpec((1, rr // 2, cc), lambda j, s: (0, j, 0))], out_specs=mine)
    return pl.pallas_call(body, name=name, grid_spec=grid_spec, out_shape=_sds(parts.shape, BF16),
                          input_output_aliases={1: 0}, compiler_params=_cp(("arbitrary",)))(
                              jnp.reshape(slot, (1,)).astype(jnp.int32), parts, other)


def _peer_copies(srcs, lands, scatter, send_sems, recv_sems, masks=ALL_PEERS):
    x, y, c = lax.axis_index("x"), lax.axis_index("y"), lax.axis_index("c")
    me = 4 * x + 2 * y + c
    out = []
    for k in range(len(srcs)):
        out.append([])
        for m in (masks[k] if isinstance(masks[0], tuple) else masks):
            px, py, pc = x ^ (m >> 2), y ^ ((m >> 1) & 1), c ^ (m & 1)
            src = srcs[k].at[4 * px + 2 * py + pc] if scatter[k] else srcs[k]
            out[k].append(pltpu.make_async_remote_copy(
                src_ref=src, dst_ref=lands[k].at[me], send_sem=send_sems.at[k * (NDEV - 1) + m - 1],
                recv_sem=recv_sems.at[k * (NDEV - 1) + m - 1],
                device_id=(px, py, pc), device_id_type=MESH_ID))
    return out


def _exchange_start(arrs, scatter, name, masks=ALL_PEERS, absent=None):
    n = len(arrs)
    me = 4 * lax.axis_index("x") + 2 * lax.axis_index("y") + lax.axis_index("c")
    lands = []
    for a, sc in zip(arrs, scatter):
        own = lax.dynamic_index_in_dim(a, me, 0, keepdims=True) if sc else a[None]
        shape = a.shape if sc else (NDEV,) + a.shape
        land = lax.dynamic_update_index_in_dim(lax.empty(shape, a.dtype), own, me, 0)
        if absent is not None:
            land = lax.dynamic_update_index_in_dim(land, jnp.zeros_like(own), absent, 0)
        lands.append(land)

    def body(*refs):
        srcs, lnds = refs[:n], refs[n:2 * n]
        send_sems, recv_sems = refs[2 * n], refs[2 * n + 1]
        token = refs[-1]
        for cp in sum(_peer_copies(srcs, lnds, scatter, send_sems, recv_sems, masks), []):
            cp.start()
        token[...] = jnp.zeros_like(token)

    ops = [pltpu.with_memory_space_constraint(a, pltpu.HBM) for a in list(arrs) + lands]
    res = pl.pallas_call(
        body, name=name,
        out_shape=(pltpu.SemaphoreType.DMA((n * (NDEV - 1),)), pltpu.SemaphoreType.DMA((n * (NDEV - 1),)))
        + tuple(pltpu.HBM(o.shape, o.dtype) for o in ops) + (_sds((8, 128), F32),),
        in_specs=[HBM] * (2 * n), out_specs=(SEM, SEM) + (HBM,) * (2 * n) + (pl.BlockSpec(memory_space=pltpu.VMEM),),
        input_output_aliases={k: 2 + k for k in range(2 * n)},
        compiler_params=pltpu.CompilerParams(has_side_effects=EFFECT),
    )(*ops)
    return res[:-1], res[-1]


def _exchange_wait(handle, scatter, after, name, masks=ALL_PEERS, which=None):
    send_sems, recv_sems = handle[0], handle[1]
    bufs = handle[2:]
    n = len(bufs) // 2
    after = list(after)

    def body(*refs):
        srcs, lnds = refs[:n], refs[n:2 * n]
        copies = _peer_copies(srcs, lnds, scatter, refs[2 * n], refs[2 * n + 1], masks)
        for k in (range(n) if which is None else which):
            for cp in copies[k]:
                cp.wait_send()
                cp.wait_recv()

    res = pl.pallas_call(
        body, name=name, out_shape=tuple(pltpu.HBM(b.shape, b.dtype) for b in bufs),
        in_specs=[HBM] * (2 * n) + [SEM, SEM] + [ANY] * len(after), out_specs=(HBM,) * (2 * n),
        input_output_aliases={k: k for k in range(2 * n)},
        compiler_params=pltpu.CompilerParams(has_side_effects=EFFECT),
    )(*bufs, send_sems, recv_sems, *after)
    return res[n:] if which is None else (res[n:], (send_sems, recv_sems) + tuple(res))


def _tie(x, token, name):
    def body(x_ref, t_ref, o_ref):
        del x_ref, t_ref, o_ref

    return pl.pallas_call(body, name=name, out_shape=_sds(x.shape, x.dtype), in_specs=[ANY, ANY], out_specs=ANY,
                          input_output_aliases={0: 0})(x, token)


TM_E = 272


def _norm_fwd_t(h, g, name):
    def body(h_ref, g_ref, o_ref, ot_ref):
        xv = h_ref[...]
        r = lax.rsqrt(jnp.mean(xv * xv, axis=-1, keepdims=True) + EPS)
        y = xv * r * g_ref[...]
        o_ref[...] = y.astype(BF16)
        ot_ref[...] = y.T.astype(BF16)

    return pl.pallas_call(
        body, name=name, grid=(T // 128,),
        in_specs=[pl.BlockSpec((128, D), lambda i: (i, 0)), pl.BlockSpec((1, D), lambda i: (0, 0))],
        out_specs=(pl.BlockSpec((128, D), lambda i: (i, 0)), pl.BlockSpec((D, 128), lambda i: (0, i))),
        out_shape=(_sds((T, D), BF16), _sds((D, T), BF16)), compiler_params=_cp(("parallel",)))(h, g)


def _norm_bwd_rows(xv, gv, dnv, dres):
    r = lax.rsqrt(jnp.mean(xv * xv, axis=-1, keepdims=True) + EPS)
    xh = xv * r
    dxh = dnv * gv
    dx = dres + r * (dxh - xh * jnp.mean(dxh * xh, axis=-1, keepdims=True))
    return dx, jnp.sum(dnv * xh, axis=0, keepdims=True)


TM_MM = 1088


def _inproj_fwd(a, w_g):
    nb = w_g.shape[2]

    def body(a_ref, w_ref, o_ref):
        o_ref[...] = _dot(a_ref[...], w_ref[0])

    return pl.pallas_call(
        body, name="inproj_fwd", grid=(T // TM_MM, NDEV),
        in_specs=[pl.BlockSpec((TM_MM, D), lambda i, j: (i, 0)), pl.BlockSpec((1, D, nb), lambda i, j: (j, 0, 0))],
        out_specs=pl.BlockSpec((TM_MM, nb), lambda i, j: (i, j)), out_shape=_sds((T, NDEV * nb), F32),
        compiler_params=_cp(("parallel", "parallel")))(a, w_g)


TM_B = 544


W_IN_B = IN_COLS // NDEV


NA_BLKS = 1536 // W_IN_B


def _dp_specs(rows, row_index):
    return [pl.BlockSpec((rows, W_IN_B), lambda *g: (row_index(*g), jnp.minimum(g[-1], NA_BLKS - 1))),
            pl.BlockSpec((rows, W_IN_B), lambda *g: (row_index(*g), jnp.maximum(g[-1] - NA_BLKS, 0)))]


def _inproj_bwd_dw(a_t, dp_na, dp_rest):
    def body(at_ref, na_ref, rest_ref, dw_ref):
        j = pl.program_id(0)

        @pl.when(j < NA_BLKS)
        def _():
            dw_ref[0] = _dot(at_ref[...], na_ref[...]).astype(BF16)

        @pl.when(j >= NA_BLKS)
        def _():
            dw_ref[0] = _dot(at_ref[...], rest_ref[...]).astype(BF16)

    return pl.pallas_call(
        body, name="inproj_bwd_dw", grid=(NDEV,),
        in_specs=[pl.BlockSpec((D, T), lambda j: (0, 0))] + _dp_specs(T, lambda j: 0),
        out_specs=pl.BlockSpec((1, D, W_IN_B), lambda j: (j, 0, 0)), out_shape=_sds((NDEV, D, W_IN_B), BF16),
        compiler_params=_cp(("parallel",)))(a_t, dp_na, dp_rest)


def _inproj_bwd_da(dp_na, dp_rest, w_g, h0, g_mix, dh1):
    nsub = TM_MM // TM_E
    nblk = T // TM_MM

    def seq_copies(b, dh0_ref, gx_ref, sems):
        out = []
        for s in range(nsub):
            lo, hi = max(NM, b * TM_MM + s * TM_E), min(L, b * TM_MM + (s + 1) * TM_E)
            if hi > lo:
                out.append((s, pltpu.make_async_copy(dh0_ref.at[pl.ds(lo - b * TM_MM, hi - lo)],
                                                     gx_ref.at[pl.ds(lo - NM, hi - lo)], sems.at[b * nsub + s])))
        return out

    def body(na_ref, rest_ref, w_ref, h0_ref, g_ref, dres_ref, dh0_ref, dg_ref, gx_ref, da, sems):
        i, j = pl.program_id(0), pl.program_id(1)
        dpv = jnp.where(j < NA_BLKS, na_ref[...], rest_ref[...])
        dav = _dot(dpv, w_ref[0], NT)

        @pl.when(j == 0)
        def _():
            da[...] = dav

        @pl.when(j > 0)
        def _():
            da[...] += dav

        @pl.when(j == NDEV - 1)
        def _():
            gsum = jnp.zeros((1, D), F32)
            for s in range(nsub):
                sub = slice(s * TM_E, (s + 1) * TM_E)
                dx, gpart = _norm_bwd_rows(h0_ref[sub, :], g_ref[...], da[sub, :], dres_ref[sub, :])
                dh0_ref[sub, :] = dx
                gsum = gsum + gpart
                for b in range(nblk):
                    for _, cp in (c for c in seq_copies(b, dh0_ref, gx_ref, sems) if c[0] == s):
                        pl.when(i == b)(cp.start)

            @pl.when(i == 0)
            def _():
                dg_ref[...] = gsum

            @pl.when(i > 0)
            def _():
                dg_ref[...] += gsum

            for b in range(nblk):
                @pl.when(i == b)
                def _(b=b):
                    for _, cp in seq_copies(b, dh0_ref, gx_ref, sems):
                        cp.wait()

    rblk = pl.BlockSpec((TM_MM, D), lambda i, j: (i, 0))
    vec = pl.BlockSpec((1, D), lambda i, j: (0, 0))
    return pl.pallas_call(
        body, name="inproj_bwd_da", grid=(T // TM_MM, NDEV),
        in_specs=_dp_specs(TM_MM, lambda i, j: i) + [pl.BlockSpec((1, D, W_IN_B), lambda i, j: (j, 0, 0)), rblk, vec, rblk],
        out_specs=(rblk, vec, ANY), out_shape=(_sds((T, D), F32), _sds((1, D), F32), _sds((L - NM, D), F32)),
        scratch_shapes=[pltpu.VMEM((TM_MM, D), F32), pltpu.SemaphoreType.DMA((nblk * nsub,))],
        compiler_params=_cp(("arbitrary", "arbitrary"), 56))(dp_na, dp_rest, w_g, h0, g_mix, dh1)


NA_QB = 256
NA_GROUPS = ROWS // 4
NA_UROWS = 11
NA_KW = NA_UROWS * GRID_W
NA_KU = 768


def _na_row_offset(var, i, j):
    valid = (j < 8, i <= j < i + 8, 3 <= j < NA_UROWS)[var]
    return (j - i + (7, 3, 0)[var]) if valid else None


def _na_bias_table(rp):
    def body(r_ref, o_ref):
        row3 = lax.broadcasted_iota(jnp.int32, (15, GRID_W, 128), 1)
        lane3 = lax.broadcasted_iota(jnp.int32, (15, GRID_W, 128), 2)
        w3 = lane3 & (GRID_W - 1)
        cs3 = jnp.clip(row3 - 8, 0, GRID_W - 16)
        lane = lax.broadcasted_iota(jnp.int32, (GRID_W, 128), 1)
        neg = jnp.full((GRID_W, 128), NEG, F32)
        z = jnp.stack([jnp.broadcast_to(r_ref[0, a:a + 1, :], (GRID_W, 128)) for a in range(15)])
        for bit in range(6):
            sh = 1 << bit
            z = jnp.where((row3 & sh) != 0, jnp.roll(z, sh, axis=2), z)
        z = jnp.roll(z, 128 - 15, axis=2)
        z = jnp.where(lane3 < GRID_W, z, 0.0)
        z = z + jnp.roll(z, GRID_W, axis=2)
        tabs = jnp.where((w3 >= cs3) & (w3 < cs3 + 16), z, NEG)
        tail = jnp.where(lane < GRID_W + NM, 0.0, NEG)
        for var in range(3):
            for i in range(4):
                for jp in range(NA_KU // 128):
                    halves = []
                    for j in (2 * jp, 2 * jp + 1):
                        a = _na_row_offset(var, i, j) if j < NA_UROWS else None
                        halves.append(tail if j >= NA_UROWS else (neg if a is None else tabs[a]))
                    o_ref[var, 0, i * 64:(i + 1) * 64, jp * 128:(jp + 1) * 128] = jnp.where(lane < GRID_W, halves[0], halves[1])

    return pl.pallas_call(
        body, name="na_bias_table", grid=(NA_HEADS,),
        in_specs=[pl.BlockSpec((1, 15, 128), lambda h: (h, 0, 0))],
        out_specs=pl.BlockSpec((3, 1, NA_QB, NA_KU), lambda h: (0, h, 0, 0)),
        out_shape=_sds((3, NA_HEADS, NA_QB, NA_KU), F32), compiler_params=_cp(("parallel",)))(rp)


def _na_var(g):
    return jnp.where(g == 0, 0, jnp.where(g == NA_GROUPS - 1, 2, 1))


def _na_load_window(src_ref, dst, g):
    us = jnp.clip(4 * g - 4, 0, ROWS - NA_UROWS)
    kstart = pl.multiple_of(NM + GRID_W * us, 16)
    dst[0:NA_KW, :] = src_ref[pl.ds(kstart, NA_KW), :].astype(BF16)
    dst[NA_KW:NA_KW + NM, :] = src_ref[0:NM, :].astype(BF16)
    dst[NA_KW + NM:, :] = jnp.zeros((NA_KU - NA_KW - NM, 128), BF16)
    return kstart


def _na_fwd(p_act, bias_tab):
    def body(q_ref, k_ref, v_ref, b_ref, o_ref, lse_ref, ku, vu):
        g = pl.program_id(1)
        _na_load_window(k_ref, ku, g)
        _na_load_window(v_ref, vu, g)
        qstart = pl.multiple_of(NM + NA_QB * g, 16)
        q = q_ref[pl.ds(qstart, NA_QB), :]
        lane = lax.broadcasted_iota(jnp.int32, (NA_QB, 128), 1)
        o_h, lse_h = [], []
        for h in range(2):
            hm = (lane < 64) if h == 0 else (lane >= 64)
            qm = (jnp.where(hm, q, 0.0) * NA_SCALE).astype(BF16)
            s = _dot(qm, ku[...], NT) + b_ref[0, h]
            m = jnp.max(s, axis=-1, keepdims=True)
            p = jnp.exp(s - m)
            l = jnp.sum(p, axis=-1, keepdims=True)
            o_h.append(_dot(p.astype(BF16), vu[...]) / l)
            lse_h.append(jnp.broadcast_to(m + jnp.log(l), (NA_QB, 128)))
        o_ref[pl.ds(qstart, NA_QB), :] = jnp.where(lane < 64, o_h[0], o_h[1]).astype(BF16)
        lse_ref[0, pl.ds(qstart, NA_QB), :] = jnp.where(lane < 64, lse_h[0], lse_h[1])

        @pl.when(g == 0)
        def _():
            qm_ = q_ref[0:NM, :]
            lane_m = lax.broadcasted_iota(jnp.int32, (NM, 128), 1)
            km, vm = ku[NA_KW:NA_KW + NM, :], vu[NA_KW:NA_KW + NM, :]
            om = []
            for h in range(2):
                hm = (lane_m < 64) if h == 0 else (lane_m >= 64)
                s = _dot(jnp.where(hm, qm_, 0.0).astype(BF16), km, NT) * NA_SCALE
                p = jnp.exp(s - jnp.max(s, axis=-1, keepdims=True))
                l = jnp.sum(p, axis=-1, keepdims=True)
                om.append(_dot(p.astype(BF16), vm) / l)
            o_ref[0:NM, :] = jnp.where(lane_m < 64, om[0], om[1]).astype(BF16)
            o_ref[L:T, :] = jnp.zeros((T - L, 128), BF16)
            lse_ref[0, 0:NM, :] = jnp.zeros((NM, 128), F32)
            lse_ref[0, L:T, :] = jnp.zeros((T - L, 128), F32)

    col = lambda off: pl.BlockSpec((T, 128), lambda hp, g: (0, off + hp))
    return pl.pallas_call(
        body, name="na_fwd", grid=(4, NA_GROUPS),
        in_specs=[col(0), col(4), col(8),
                  pl.BlockSpec((1, 2, NA_QB, NA_KU), lambda hp, g: (_na_var(g), hp, 0, 0))],
        out_specs=(pl.BlockSpec((T, 128), lambda hp, g: (0, hp)), pl.BlockSpec((1, T, 128), lambda hp, g: (hp, 0, 0))),
        out_shape=(_sds((T, 512), BF16), _sds((4, T, 128), F32)),
        scratch_shapes=[pltpu.VMEM((NA_KU, 128), BF16), pltpu.VMEM((NA_KU, 128), BF16)],
        compiler_params=_cp(("parallel", "arbitrary")))(p_act, p_act, p_act, bias_tab)


def _na_bwd(p_act, do, lse, bias_tab):
    def body(q_ref, k_ref, v_ref, do_ref, lse_ref, b_ref, dp_ref, db_ref, ku, vu, dq_ref, dk_ref, dv_ref, stage, sems):
        g = pl.program_id(1)

        @pl.when(g == 0)
        def _():
            dq_ref[...] = jnp.zeros((T, 128), F32)
            dk_ref[...] = jnp.zeros((T, 128), F32)
            dv_ref[...] = jnp.zeros((T, 128), F32)

        kstart = _na_load_window(k_ref, ku, g)
        _na_load_window(v_ref, vu, g)
        qstart = pl.multiple_of(NM + NA_QB * g, 16)
        q = q_ref[pl.ds(qstart, NA_QB), :]
        dov = do_ref[pl.ds(qstart, NA_QB), :]
        lsev = lse_ref[0, pl.ds(qstart, NA_QB), :]
        lane = lax.broadcasted_iota(jnp.int32, (NA_QB, 128), 1)
        first = (g == 0) | (g == 1) | (g == NA_GROUPS - 1)
        dq_h = []
        dku = jnp.zeros((NA_KU, 128), F32)
        dvu = jnp.zeros((NA_KU, 128), F32)
        for h in range(2):
            hm = (lane < 64) if h == 0 else (lane >= 64)
            qm = (jnp.where(hm, q, 0.0) * NA_SCALE).astype(BF16)
            dom = jnp.where(hm, dov, 0.0).astype(BF16)
            s = _dot(qm, ku[...], NT) + b_ref[0, h]
            p = jnp.exp(s - lsev[:, 64 * h:64 * h + 1])
            dp = _dot(dom, vu[...], NT)
            delta = jnp.sum(p * dp, axis=-1, keepdims=True)
            ds = p * (dp - delta)

            @pl.when(first)
            def _():
                db_ref[0, h] = ds

            @pl.when(jnp.logical_not(first))
            def _():
                db_ref[0, h] += ds

            dsb = ds.astype(BF16)
            dq_h.append(_dot(dsb, ku[...]) * NA_SCALE)
            dku = dku + _dot(dsb, qm, TN)
            dvu = dvu + _dot(p.astype(BF16), dom, TN)
        dq_ref[pl.ds(qstart, NA_QB), :] = jnp.where(lane < 64, dq_h[0], dq_h[1])
        dk_ref[pl.ds(kstart, NA_KW), :] += dku[0:NA_KW]
        dv_ref[pl.ds(kstart, NA_KW), :] += dvu[0:NA_KW]
        dk_ref[0:NM, :] += dku[NA_KW:NA_KW + NM]
        dv_ref[0:NM, :] += dvu[NA_KW:NA_KW + NM]

        @pl.when(g == 0)
        def _():
            qm_ = q_ref[0:NM, :]
            dom_ = do_ref[0:NM, :]
            lane_m = lax.broadcasted_iota(jnp.int32, (NM, 128), 1)
            km, vm = ku[NA_KW:NA_KW + NM, :], vu[NA_KW:NA_KW + NM, :]
            dqs = []
            dkm = jnp.zeros((NM, 128), F32)
            dvm = jnp.zeros((NM, 128), F32)
            for h in range(2):
                hm = (lane_m < 64) if h == 0 else (lane_m >= 64)
                qh = jnp.where(hm, qm_, 0.0).astype(BF16)
                doh = jnp.where(hm, dom_, 0.0).astype(BF16)
                s = _dot(qh, km, NT) * NA_SCALE
                e = jnp.exp(s - jnp.max(s, axis=-1, keepdims=True))
                p = e / jnp.sum(e, axis=-1, keepdims=True)
                dp = _dot(doh, vm, NT)
                ds = p * (dp - jnp.sum(p * dp, axis=-1, keepdims=True))
                dsb = (ds * NA_SCALE).astype(BF16)
                dqs.append(_dot(dsb, km))
                dkm = dkm + _dot(dsb, qh, TN)
                dvm = dvm + _dot(p.astype(BF16), doh, TN)
            dq_ref[0:NM, :] = jnp.where(lane_m < 64, dqs[0], dqs[1])
            dk_ref[0:NM, :] += dkm
            dv_ref[0:NM, :] += dvm

        @pl.when(g == NA_GROUPS - 1)
        def _():
            copies = []
            for n, acc in enumerate((dq_ref, dk_ref, dv_ref)):
                stage[n] = acc[...].astype(BF16)
                cols = pl.ds(pl.multiple_of(512 * n + 128 * pl.program_id(0), 128), 128)
                copies.append(pltpu.make_async_copy(stage.at[n], dp_ref.at[pl.ds(0, T), cols], sems.at[n]))
                copies[-1].start()
            for cp in copies:
                cp.wait()

    col = lambda off: pl.BlockSpec((T, 128), lambda hp, g: (0, off + hp))
    ocol = pl.BlockSpec((T, 128), lambda hp, g: (0, hp))
    bspec = pl.BlockSpec((1, 2, NA_QB, NA_KU), lambda hp, g: (_na_var(g), hp, 0, 0))
    return pl.pallas_call(
        body, name="na_bwd", grid=(4, NA_GROUPS),
        in_specs=[col(0), col(4), col(8), ocol, pl.BlockSpec((1, T, 128), lambda hp, g: (hp, 0, 0)), bspec],
        out_specs=(ANY, bspec),
        out_shape=(_sds((T, 1536), BF16), _sds((3, NA_HEADS, NA_QB, NA_KU), F32)),
        scratch_shapes=[pltpu.VMEM((NA_KU, 128), BF16), pltpu.VMEM((NA_KU, 128), BF16)] + [pltpu.VMEM((T, 128), F32)] * 3
        + [pltpu.VMEM((3, T, 128), BF16), pltpu.SemaphoreType.DMA((3,))],
        compiler_params=_cp(("parallel", "arbitrary")))(p_act, p_act, p_act, do, lse, bias_tab)


def _na_rpb_reduce(dbias):
    def body(db_ref, o_ref):
        lane = lax.broadcasted_iota(jnp.int32, (GRID_W, 128), 1)
        row3 = lax.broadcasted_iota(jnp.int32, (15, GRID_W, 128), 1)
        lane3 = lax.broadcasted_iota(jnp.int32, (15, GRID_W, 128), 2)
        accs = []
        for a in range(15):
            acc = jnp.zeros((GRID_W, 128), F32)
            for var in range(3):
                for i in range(4):
                    for j in range(NA_UROWS):
                        if _na_row_offset(var, i, j) == a:
                            pair = db_ref[var, 0, i * 64:(i + 1) * 64, (j // 2) * 128:(j // 2 + 1) * 128]
                            acc = acc + jnp.where((lane < GRID_W) if j % 2 == 0 else (lane >= GRID_W), pair, 0.0)
            accs.append(acc)
        z = jnp.stack(accs)
        z = jnp.where(lane3 < GRID_W, z + jnp.roll(z, GRID_W, axis=2), 0.0)
        for bit in range(6):
            sh = 1 << bit
            z = jnp.where((row3 & sh) != 0, jnp.roll(z, 128 - sh, axis=2), z)
        z = jnp.roll(z, 15, axis=2)
        o_ref[0] = jnp.sum(z, axis=1)

    return pl.pallas_call(
        body, name="na_rpb_reduce", grid=(NA_HEADS,),
        in_specs=[pl.BlockSpec((3, 1, NA_QB, NA_KU), lambda h: (0, h, 0, 0))],
        out_specs=pl.BlockSpec((1, 15, 128), lambda h: (h, 0, 0)), out_shape=_sds((NA_HEADS, 15, 128), F32),
        compiler_params=_cp(("parallel",)))(dbias)


HG_RB = 128
HG_NB = T // HG_RB
HG_SLOTS = HG_NB * 8
HI = lax.Precision.HIGHEST
HG_UNROLL = 4
HG_UNROLL_WIDE = 8


def _chunk_tri(lower):
    r = lax.broadcasted_iota(jnp.int32, (HG_RB, HG_RB), 0)
    c = lax.broadcasted_iota(jnp.int32, (HG_RB, HG_RB), 1)
    same = (r // HG_C) == (c // HG_C)
    keep = (c <= r) if lower else (c >= r)
    return jnp.where(same & keep, 1.0, 0.0).astype(F32)


def _hg_gate_terms(z, lg):
    dl = lg[0:1, :] - lg[1:2, :]
    log_lb = jax.nn.log_sigmoid(dl)
    log_1mlb = jax.nn.log_sigmoid(-dl)
    yz = log_1mlb + jax.nn.log_sigmoid(z)
    log_f = jnp.logaddexp(log_lb, yz)
    snz = jax.nn.sigmoid(-z)
    k = jnp.exp(log_1mlb) * snz
    w2 = jnp.exp(yz - log_f)
    return log_f, k, snz, w2


def _hg_pre(p_act, logits):
    def body(q_ref, zf_ref, zb_ref, lg_ref, qh_ref, kf_ref, bf_ref, kb_ref, bb_ref):
        qh_ref[...] = jax.nn.silu(q_ref[...])
        lf, kf, _, _ = _hg_gate_terms(zf_ref[...], lg_ref[0])
        kf_ref[...] = kf
        bf_ref[...] = jnp.dot(_chunk_tri(True), lf, precision=HI, preferred_element_type=F32)
        lb_, kb, _, _ = _hg_gate_terms(zb_ref[...], lg_ref[1])
        kb_ref[...] = kb
        bb_ref[...] = jnp.dot(_chunk_tri(False), lb_, precision=HI, preferred_element_type=F32)

    blk = lambda c: pl.BlockSpec((HG_RB, 512), lambda i: (i, c))
    ob = pl.BlockSpec((HG_RB, 512), lambda i: (i, 0))
    return pl.pallas_call(
        body, name="hg_pre", grid=(HG_NB,),
        in_specs=[blk(3), blk(4), blk(5), pl.BlockSpec((2, 2, 512), lambda i: (0, 0, 0))],
        out_specs=(ob,) * 5, out_shape=(_sds((T, 512), F32),) * 5,
        compiler_params=_cp(("parallel",)))(p_act, p_act, p_act, logits)


def _bdot(a, b, ca, cb):
    return lax.dot_general(a.astype(BF16), b.astype(BF16), (((ca,), (cb,)), ((0,), (0,))), preferred_element_type=F32)


HG_S = 8
HG_NS = HG_RB // HG_S


def _lane_sums(xs):
    l_io = lax.broadcasted_iota(jnp.int32, (HG_NS, HG_S, HG_S), 2)
    a = jnp.zeros((HG_NS, HG_S, HG_S), F32)
    for j, x in enumerate(xs):
        a = a + jnp.where(l_io == j, jnp.sum(x, axis=-1, keepdims=True), 0.0)
    return a


def _halves(x):
    y = x.reshape(8, 2, HG_S, x.shape[-1])
    return y[:, 0], y[:, 1]


def _join(first, second):
    return jnp.stack([first, second], axis=1).reshape(HG_RB, first.shape[-1])


def _cross_split(rev, b4):
    b_1, b_2 = _halves(b4)
    if rev:
        r = b_2[:, 0:1, :]
        return jnp.exp(b_1 - r), jnp.exp(r - b_2)
    r = b_1[:, HG_S - 1:HG_S, :]
    return jnp.exp(b_2 - r), jnp.exp(r - b_1)


def _hg_scan_fwd(qh, k, b, p_act, rev):
    anchor = 0 if rev else HG_C - 1

    def body(q_ref, k_ref, b_ref, v_ref, o_ref, st_ref, dsc):
        def phase_a(blk, _):
            rows = pl.ds(pl.multiple_of(blk * HG_RB, HG_RB), HG_RB)
            b3 = b_ref[rows, :].reshape(8, HG_C, 128)
            k3 = k_ref[rows, :].reshape(8, HG_C, 128)
            v3 = v_ref[rows, :].reshape(8, HG_C, 128)
            bl = b3[:, anchor:anchor + 1, :]
            kt = k3 * jnp.exp(bl - b3)
            st_ref[0, pl.ds(pl.multiple_of(blk * 8, 8), 8)] = _bdot(v3, kt, 1, 1)
            dsc[pl.ds(pl.multiple_of(blk * 8, 8), 8), :] = jnp.exp(bl[:, 0, :])
            return 0

        lax.fori_loop(0, HG_NB, phase_a, 0, unroll=HG_UNROLL_WIDE)

        def phase_b(n, carry):
            c = (NCHUNK - 1 - n) if rev else n
            u = st_ref[0, c]
            st_ref[0, c] = carry
            return carry * dsc[pl.ds(c, 1), :] + u

        lax.fori_loop(0, NCHUNK // 3, lambda n3, s: phase_b(3 * n3 + 2, phase_b(3 * n3 + 1, phase_b(3 * n3, s))),
                      jnp.zeros((128, 128), F32))
        for c in range(NCHUNK, HG_SLOTS):
            st_ref[0, c] = jnp.zeros((128, 128), F32)

        t_io = lax.broadcasted_iota(jnp.int32, (HG_NS, HG_S, 128), 1)

        def phase_c(blk, _):
            rows = pl.ds(pl.multiple_of(blk * HG_RB, HG_RB), HG_RB)
            b4 = b_ref[rows, :].reshape(HG_NS, HG_S, 128)
            k4 = k_ref[rows, :].reshape(HG_NS, HG_S, 128)
            q4 = q_ref[rows, :].reshape(HG_NS, HG_S, 128)
            v4 = v_ref[rows, :].reshape(HG_NS, HG_S, 128)
            st = st_ref[0, pl.ds(pl.multiple_of(blk * 8, 8), 8)]
            o = _bdot((q4 * jnp.exp(b4)).reshape(8, HG_C, 128), st, 2, 2).reshape(HG_RB, 128)
            terms = []
            for s in range(HG_S):
                ok = (t_io <= s) if rev else (t_io >= s)
                f = jnp.exp(jnp.where(ok, b4 - b4[:, s:s + 1, :], NEG))
                terms.append(q4 * f * k4[:, s:s + 1, :])
            o_in = _bdot(_lane_sums(terms), v4, 2, 1)
            wq, wk = _cross_split(rev, b4)
            q_1, q_2 = _halves(q4)
            k_1, k_2 = _halves(k4)
            v_1, v_2 = _halves(v4)
            o_1, o_2 = _halves(o_in)
            if rev:
                o_1 = o_1 + _bdot(_bdot(q_1 * wq, k_2 * wk, 2, 2), v_2, 2, 1)
            else:
                o_2 = o_2 + _bdot(_bdot(q_2 * wq, k_1 * wk, 2, 2), v_1, 2, 1)
            o_ref[rows, :] = o + _join(o_1, o_2)
            return 0

        lax.fori_loop(0, HG_NB, phase_c, 0, unroll=HG_UNROLL_WIDE)

    col = pl.BlockSpec((T, 128), lambda h: (0, h))
    return pl.pallas_call(
        body, name="hg_scan_bwd_dir" if rev else "hg_scan_fwd_dir", grid=(HG_HEADS,),
        in_specs=[col, col, col, pl.BlockSpec((T, 128), lambda h: (0, 24 + h))],
        out_specs=(col, pl.BlockSpec((1, HG_SLOTS, 128, 128), lambda h: (h, 0, 0, 0))),
        out_shape=(_sds((T, 512), F32), _sds((HG_HEADS, HG_SLOTS, 128, 128), F32)),
        scratch_shapes=[pltpu.VMEM((HG_SLOTS, 128), F32)],
        compiler_params=_cp(("parallel",), 56))(qh, k, b, p_act)


def _hg_scan_bwd(qh, k, b, p_act, st, do, rev):
    anchor = 0 if rev else HG_C - 1

    def body(q_ref, k_ref, b_ref, v_ref, st_ref, do_ref, dq_ref, dk_ref, db_ref, dv_ref, gst, dsc, dbl):
        def phase_a(blk, _):
            rows = pl.ds(pl.multiple_of(blk * HG_RB, HG_RB), HG_RB)
            b3 = b_ref[rows, :].reshape(8, HG_C, 128)
            q3 = q_ref[rows, :].reshape(8, HG_C, 128)
            do3 = do_ref[rows, :].reshape(8, HG_C, 128)
            gst[pl.ds(pl.multiple_of(blk * 8, 8), 8)] = _bdot(do3, q3 * jnp.exp(b3), 1, 1)
            dsc[pl.ds(pl.multiple_of(blk * 8, 8), 8), :] = jnp.exp(b3[:, anchor, :])
            return 0

        lax.fori_loop(0, HG_NB, phase_a, 0, unroll=HG_UNROLL_WIDE)

        def phase_b(n, carry):
            c = n if rev else (NCHUNK - 1 - n)
            w = gst[c]
            gst[c] = carry
            dcv = dsc[pl.ds(c, 1), :]
            dbl[pl.ds(c, 1), :] = dcv * jnp.sum(st_ref[0, c] * carry, axis=0, keepdims=True)
            return carry * dcv + w

        lax.fori_loop(0, NCHUNK // 3, lambda n3, s: phase_b(3 * n3 + 2, phase_b(3 * n3 + 1, phase_b(3 * n3, s))),
                      jnp.zeros((128, 128), F32))
        for c in range(NCHUNK, HG_SLOTS):
            gst[c] = jnp.zeros((128, 128), F32)
            dbl[c:c + 1, :] = jnp.zeros((1, 128), F32)

        t_io = lax.broadcasted_iota(jnp.int32, (HG_NS, HG_S, 128), 1)
        t16 = lax.broadcasted_iota(jnp.int32, (8, HG_C, 128), 1)
        r_io = lax.broadcasted_iota(jnp.int32, (HG_NS, HG_S, HG_S), 1)
        l_io = lax.broadcasted_iota(jnp.int32, (HG_NS, HG_S, HG_S), 2)

        def phase_c(blk, _):
            rows = pl.ds(pl.multiple_of(blk * HG_RB, HG_RB), HG_RB)
            cs = pl.ds(pl.multiple_of(blk * 8, 8), 8)
            b4 = b_ref[rows, :].reshape(HG_NS, HG_S, 128)
            k4 = k_ref[rows, :].reshape(HG_NS, HG_S, 128)
            q4 = q_ref[rows, :].reshape(HG_NS, HG_S, 128)
            v4 = v_ref[rows, :].reshape(HG_NS, HG_S, 128)
            do4 = do_ref[rows, :].reshape(HG_NS, HG_S, 128)
            b3, k3, q3 = (z.reshape(8, HG_C, 128) for z in (b4, k4, q4))
            v3, do3 = v4.reshape(8, HG_C, 128), do4.reshape(8, HG_C, 128)
            s_t = st_ref[0, cs]
            g_t = gst[cs]
            bl = b3[:, anchor:anchor + 1, :]
            ekl = jnp.exp(bl - b3)
            kt = k3 * ekl
            dkt = _bdot(v3, g_t, 2, 1)
            dq = (_bdot(do3, s_t, 2, 1) * jnp.exp(b3)).reshape(HG_NS, HG_S, 128)
            dk = (dkt * ekl).reshape(HG_NS, HG_S, 128)
            dv = _bdot(kt, g_t, 2, 2).reshape(HG_NS, HG_S, 128)
            dbl3 = dbl[cs, :].reshape(8, 1, 128) + jnp.sum(dkt * kt, axis=1, keepdims=True)
            causal = (l_io >= r_io) if rev else (l_io <= r_io)
            da = jnp.where(causal, _bdot(do4, v4, 2, 2), 0.0)
            causal_t = (l_io <= r_io) if rev else (l_io >= r_io)
            dat = jnp.where(causal_t, _bdot(v4, do4, 2, 2), 0.0)
            for s in range(HG_S):
                ok = (t_io <= s) if rev else (t_io >= s)
                f = jnp.exp(jnp.where(ok, b4 - b4[:, s:s + 1, :], NEG))
                dq = dq + da[:, :, s:s + 1] * (f * k4[:, s:s + 1, :])
            terms = []
            for t in range(HG_S):
                ok = (t_io >= t) if rev else (t_io <= t)
                e = jnp.exp(jnp.where(ok, b4[:, t:t + 1, :] - b4, NEG))
                eq = e * q4[:, t:t + 1, :]
                dk = dk + dat[:, :, t:t + 1] * eq
                terms.append(eq * k4)
            dv = dv + _bdot(_lane_sums(terms), do4, 2, 1)
            wq, wk = _cross_split(rev, b4)
            pick = (lambda z: _halves(z)) if rev else (lambda z: _halves(z)[::-1])
            (q_q, _), (_, k_k), (_, v_k), (do_q, _) = pick(q4), pick(k4), pick(v4), pick(do4)
            qx, kx = q_q * wq, k_k * wk
            dq_q = _bdot(_bdot(do_q, v_k, 2, 2), kx, 2, 1) * wq
            dk_k = _bdot(_bdot(v_k, do_q, 2, 2), qx, 2, 1) * wk
            dv_k = _bdot(_bdot(kx, qx, 2, 2), do_q, 2, 1)
            zero = jnp.zeros((8, HG_S, 128), F32)
            place_q = (lambda z: _join(z, zero)) if rev else (lambda z: _join(zero, z))
            place_k = (lambda z: _join(zero, z)) if rev else (lambda z: _join(z, zero))
            dq2 = dq.reshape(HG_RB, 128) + place_q(dq_q)
            dk2 = dk.reshape(HG_RB, 128) + place_k(dk_k)
            dv2 = dv.reshape(HG_RB, 128) + place_k(dv_k)
            dq3, dk3 = dq2.reshape(8, HG_C, 128), dk2.reshape(8, HG_C, 128)
            db = q3 * dq3 - k3 * dk3 + jnp.where(t16 == anchor, dbl3, 0.0)
            dq_ref[rows, :] = dq2
            dk_ref[rows, :] = dk2
            db_ref[rows, :] = db.reshape(HG_RB, 128)
            dv_ref[rows, :] = dv2
            return 0

        lax.fori_loop(0, HG_NB, phase_c, 0, unroll=HG_UNROLL)

    col = pl.BlockSpec((T, 128), lambda h: (0, h))
    return pl.pallas_call(
        body, name="hg_scan_bwd_dir_bwd" if rev else "hg_scan_fwd_dir_bwd", grid=(HG_HEADS,),
        in_specs=[col, col, col, pl.BlockSpec((T, 128), lambda h: (0, 24 + h)),
                  pl.BlockSpec((1, HG_SLOTS, 128, 128), lambda h: (h, 0, 0, 0)), col],
        out_specs=(col,) * 4, out_shape=(_sds((T, 512), F32),) * 4,
        scratch_shapes=[pltpu.VMEM((HG_SLOTS, 128, 128), F32), pltpu.VMEM((HG_SLOTS, 128), F32),
                        pltpu.VMEM((HG_SLOTS, 128), F32)],
        compiler_params=_cp(("parallel",), 56))(qh, k, b, p_act, st, do)


def _row_valid(i, tm):
    r = lax.broadcasted_iota(jnp.int32, (tm, 1), 0) + i * tm
    return r < L


def _hg_post_rows(o, gv, gain_v, valid):
    parts = []
    for h in range(HG_HEADS):
        oh = o[:, 128 * h:128 * (h + 1)]
        parts.append(oh * lax.rsqrt(jnp.mean(oh * oh, axis=-1, keepdims=True) + EPS))
    return jnp.where(valid, jnp.concatenate(parts, axis=1) * gain_v * jax.nn.silu(gv), 0.0)


def _hg_post_bwd_rows(du, o, gv, gain_v, valid):
    duv = jnp.where(valid, du, 0.0)
    sig = jax.nn.sigmoid(gv)
    sg = gv * sig
    dn = duv * gain_v * sg
    do_parts, n_parts = [], []
    for h in range(HG_HEADS):
        sl = slice(128 * h, 128 * (h + 1))
        oh = o[:, sl]
        r = lax.rsqrt(jnp.mean(oh * oh, axis=-1, keepdims=True) + EPS)
        nh = oh * r
        dnh = dn[:, sl]
        do_parts.append(r * (dnh - nh * jnp.mean(dnh * nh, axis=-1, keepdims=True)))
        n_parts.append(nh)
    n = jnp.where(valid, jnp.concatenate(n_parts, axis=1), 0.0)
    do = jnp.where(valid, jnp.concatenate(do_parts, axis=1), 0.0)
    dg = duv * n * gain_v * (sig * (1.0 + gv * (1.0 - sig)))
    return do, dg, jnp.sum(duv * n * sg, axis=0, keepdims=True)


def _hg_pre_bwd(p_act, logits, dq_f, dq_b, dk_f, dk_b, db_f, db_b, dv_f, dv_b, dp_rest):
    def body(q_ref, zf_ref, zb_ref, lg_ref, dqf_ref, dqb_ref, dkf_ref, dkb_ref, dbf_ref, dbb_ref, dvf_ref, dvb_ref, _,
             dp_ref, dlg_ref):
        dq_ref, dzf_ref, dzb_ref, di_ref = (dp_ref.at[:, 512 * c:512 * (c + 1)] for c in range(4))
        i = pl.program_id(0)
        valid = _row_valid(i, HG_RB)
        qv = q_ref[...]
        sig = jax.nn.sigmoid(qv)
        dq_ref[...] = jnp.where(valid, (dqf_ref[...] + dqb_ref[...]) * (sig * (1.0 + qv * (1.0 - sig))), 0.0).astype(BF16)
        di_ref[...] = jnp.where(valid, dvf_ref[...] + dvb_ref[...], 0.0).astype(BF16)
        for d, (z_ref, dk_r, db_r, dz_ref) in enumerate(((zf_ref, dkf_ref, dbf_ref, dzf_ref), (zb_ref, dkb_ref, dbb_ref, dzb_ref))):
            lg = lg_ref[d]
            dl = lg[0:1, :] - lg[1:2, :]
            lb = jax.nn.sigmoid(dl)
            one_m_lb = jax.nn.sigmoid(-dl)
            log_f, _, snz, w2 = _hg_gate_terms(z_ref[...], lg)
            dbv = jnp.where(valid, db_r[...], 0.0)
            dkv = jnp.where(valid, dk_r[...], 0.0)
            dlf = jnp.dot(_chunk_tri(d == 1), dbv, precision=HI, preferred_element_type=F32)
            sz = 1.0 - snz
            dz_ref[...] = (dlf * w2 * snz - dkv * one_m_lb * sz * snz).astype(BF16)
            dlb = jnp.sum(dlf * snz * jnp.exp(-log_f) - dkv * snz, axis=0, keepdims=True)
            dl0 = dlb * lb * one_m_lb
            part = jnp.concatenate([dl0, -dl0], axis=0)

            @pl.when(i == 0)
            def _():
                dlg_ref[d] = part

            @pl.when(i > 0)
            def _():
                dlg_ref[d] += part

    blk = lambda c: pl.BlockSpec((HG_RB, 512), lambda i: (i, c))
    ob = pl.BlockSpec((HG_RB, 512), lambda i: (i, 0))
    lgs = pl.BlockSpec((2, 2, 512), lambda i: (0, 0, 0))
    return pl.pallas_call(
        body, name="hg_pre_bwd", grid=(HG_NB,),
        in_specs=[blk(3), blk(4), blk(5), lgs] + [ob] * 8 + [ANY],
        out_specs=(pl.BlockSpec((HG_RB, 2048), lambda i: (i, 0)), lgs),
        out_shape=(_sds(dp_rest.shape, BF16), _sds((2, 2, 512), F32)), input_output_aliases={12: 0},
        compiler_params=_cp(("arbitrary",)))(p_act, p_act, p_act, logits, dq_f, dq_b, dk_f, dk_b, db_f, db_b, dv_f, dv_b,
                                             dp_rest)


def _mix_fwd(o_na, o_f, o_b, gain, w_na, w_hg, p_act):
    def body(ona_ref, of_ref, ob_ref, g_ref, gain_ref, wna_ref, whg_ref, gna_ref, ghg_ref, o_ref, u_ref):
        u = _hg_post_rows(of_ref[...] + ob_ref[...], g_ref[...], gain_ref[...], _row_valid(pl.program_id(0), TM_B)).astype(BF16)
        u_ref[...] = u
        y_na = _dot(ona_ref[...], wna_ref[...])
        y_hg = _dot(u, whg_ref[...])
        o_ref[...] = (jax.nn.sigmoid(gna_ref[...]) * y_na + jax.nn.sigmoid(ghg_ref[...]) * y_hg).astype(BF16)

    act = pl.BlockSpec((TM_B, 512), lambda i: (i, 0))
    wsp = pl.BlockSpec((512, D), lambda i: (0, 0))
    return pl.pallas_call(
        body, name="mix_fwd", grid=(T // TM_B,),
        in_specs=[act, act, act, pl.BlockSpec((TM_B, 512), lambda i: (i, 7)), pl.BlockSpec((1, 512), lambda i: (0, 0)),
                  wsp, wsp, pl.BlockSpec((TM_B, D), lambda i: (i, 4)), pl.BlockSpec((TM_B, D), lambda i: (i, 5))],
        out_specs=(pl.BlockSpec((TM_B, D), lambda i: (i, 0)), act), out_shape=(_sds((T, D), BF16), _sds((T, 512), BF16)),
        compiler_params=_cp(("parallel",)))(o_na, o_f, o_b, p_act, gain, w_na, w_hg, p_act, p_act)


DP_REST = IN_COLS - 1536


def _mix_bwd(o_na, u_hg, o_f, o_b, gain, w_na, w_hg, p_act, dmix):
    ni = T // TM_B

    def body(ona_ref, uhg_ref, of_ref, ob_ref, g_ref, gain_ref, wna_ref, whg_ref, gna_ref, ghg_ref, dmix_ref,
             dp_ref, dwna_ref, dwhg_ref, dona_ref, do_ref, dgain_ref, acc_na, acc_hg):
        i = pl.program_id(0)
        dg_ref, dgna_ref, dghg_ref = dp_ref.at[:, 2048:2560], dp_ref.at[:, 2560:3584], dp_ref.at[:, 3584:4608]
        dm = dmix_ref[...].astype(F32)
        dxs = []
        for x_ref, w_ref, gt_ref, dgt_ref, dw_ref, acc in (
                (ona_ref, wna_ref, gna_ref, dgna_ref, dwna_ref, acc_na), (uhg_ref, whg_ref, ghg_ref, dghg_ref, dwhg_ref, acc_hg)):
            xv = x_ref[...]
            y = _dot(xv, w_ref[...])
            sg = jax.nn.sigmoid(gt_ref[...])
            dgt_ref[...] = (dm * y * sg * (1.0 - sg)).astype(BF16)
            dy = (dm * sg).astype(BF16)
            dxs.append(_dot(dy, w_ref[...], NT))
            part = _dot(xv, dy, TN)

            @pl.when(i == 0)
            def _():
                acc[...] = part

            @pl.when(i > 0)
            def _():
                acc[...] += part

            @pl.when(i == ni - 1)
            def _():
                dw_ref[...] = acc[...].astype(BF16)

        dona_ref[...] = dxs[0]
        do, dg, gpart = _hg_post_bwd_rows(dxs[1], of_ref[...] + ob_ref[...], g_ref[...], gain_ref[...], _row_valid(i, TM_B))
        do_ref[...] = do
        dg_ref[...] = dg.astype(BF16)

        @pl.when(i == 0)
        def _():
            dgain_ref[...] = gpart

        @pl.when(i > 0)
        def _():
            dgain_ref[...] += gpart

    act = pl.BlockSpec((TM_B, 512), lambda i: (i, 0))
    wsp = pl.BlockSpec((512, D), lambda i: (0, 0))
    rblk = pl.BlockSpec((TM_B, D), lambda i: (i, 0))
    vec = pl.BlockSpec((1, 512), lambda i: (0, 0))
    return pl.pallas_call(
        body, name="mix_bwd", grid=(ni,),
        in_specs=[act, act, act, act, pl.BlockSpec((TM_B, 512), lambda i: (i, 7)), vec, wsp, wsp,
                  pl.BlockSpec((TM_B, D), lambda i: (i, 4)), pl.BlockSpec((TM_B, D), lambda i: (i, 5)), rblk],
        out_specs=(pl.BlockSpec((TM_B, DP_REST), lambda i: (i, 0)), wsp, wsp, act, act, vec),
        out_shape=(_sds((T, DP_REST), BF16), _sds((512, D), BF16), _sds((512, D), BF16),
                   _sds((T, 512), F32), _sds((T, 512), F32), _sds((1, 512), F32)),
        scratch_shapes=[pltpu.VMEM((512, D), F32), pltpu.VMEM((512, D), F32)],
        compiler_params=_cp(("arbitrary",)))(o_na, u_hg, o_f, o_b, p_act, gain, w_na, w_hg, p_act, p_act, dmix)


def _wo_fwd(mix, w_o, h0, g_mlp):
    def body(mix_ref, w_ref, h0_ref, g_ref, h1_ref, m_ref):
        h1 = h0_ref[...] + _dot(mix_ref[...], w_ref[...])
        h1_ref[...] = h1
        r = lax.rsqrt(jnp.mean(h1 * h1, axis=-1, keepdims=True) + EPS)
        m_ref[...] = (h1 * r * g_ref[...]).astype(BF16)

    blk = pl.BlockSpec((TM_B, D), lambda i: (i, 0))
    return pl.pallas_call(
        body, name="wo_fwd", grid=(T // TM_B,),
        in_specs=[blk, pl.BlockSpec((D, D), lambda i: (0, 0)), blk, pl.BlockSpec((1, D), lambda i: (0, 0))],
        out_specs=(blk, blk), out_shape=(_sds((T, D), F32), _sds((T, D), BF16)),
        compiler_params=_cp(("parallel",)))(mix, w_o, h0, g_mlp)


def _wo_bwd(dh1_b, w_o, mix):
    ni = T // TM_B

    def body(dh_ref, w_ref, mix_ref, dmix_ref, dw_ref, acc):
        i = pl.program_id(0)
        dh = dh_ref[...]
        dmix_ref[...] = _dot(dh, w_ref[...], NT).astype(BF16)
        part = _dot(mix_ref[...], dh, TN)

        @pl.when(i == 0)
        def _():
            acc[...] = part

        @pl.when(i > 0)
        def _():
            acc[...] += part

        @pl.when(i == ni - 1)
        def _():
            dw_ref[...] = acc[...].astype(BF16)

    blk = pl.BlockSpec((TM_B, D), lambda i: (i, 0))
    wsp = pl.BlockSpec((D, D), lambda i: (0, 0))
    return pl.pallas_call(
        body, name="wo_bwd", grid=(ni,), in_specs=[blk, wsp, blk], out_specs=(blk, wsp),
        out_shape=(_sds((T, D), BF16), _sds((D, D), BF16)), scratch_shapes=[pltpu.VMEM((D, D), F32)],
        compiler_params=_cp(("arbitrary",)))(dh1_b, w_o, mix)


FF_B = D_FF // NDEV


def _loss_rows(xv, gv, tv, row0):
    r_io = lax.broadcasted_iota(jnp.int32, (xv.shape[0], 1), 0) + row0
    valid = (r_io >= NM) & (r_io < L)
    r = lax.rsqrt(jnp.mean(xv * xv, axis=-1, keepdims=True) + EPS)
    xh = xv * r
    err = jnp.where(valid, xh * gv - tv, 0.0)
    lpart = 0.5 * jnp.sum(jnp.sum(err * err, axis=-1, keepdims=True) * (1.0 / D), axis=0, keepdims=True)
    dy = err * (1.0 / D)
    dxh = dy * gv
    dh = r * (dxh - xh * jnp.mean(dxh * xh, axis=-1, keepdims=True))
    return lpart, dh, jnp.sum(dy * xh, axis=0, keepdims=True)


def _mlp_fwd_loss(m, wup_g, wdown_g, h1, g_final, tgt):
    nsub = TM_MM // TM_E

    def body(m_ref, wu_ref, wd_ref, h1_ref, g_ref, t_ref, loss_ref, dh_ref, dhb_ref, dg_ref, h2):
        i, j = pl.program_id(0), pl.program_id(1)
        up = jnp.maximum(_dot(m_ref[...], wu_ref[0]), 0.0)
        part = _dot((up * up).astype(BF16), wd_ref[0])

        @pl.when(j == 0)
        def _():
            h2[...] = h1_ref[...] + part

        @pl.when(j > 0)
        def _():
            h2[...] += part

        @pl.when(j == NDEV - 1)
        def _():
            lsum = jnp.zeros((1, 1), F32)
            gsum = jnp.zeros((1, D), F32)
            for s in range(nsub):
                rows = slice(s * TM_E, (s + 1) * TM_E)
                lpart, dh, gpart = _loss_rows(h2[rows, :], g_ref[...], t_ref[rows, :], i * TM_MM + s * TM_E)
                dh_ref[rows, :] = dh
                dhb_ref[rows, :] = dh.astype(BF16)
                lsum = lsum + lpart
                gsum = gsum + gpart
            lsum = jnp.broadcast_to(lsum, (1, 128))

            @pl.when(i == 0)
            def _():
                loss_ref[...] = lsum
                dg_ref[...] = gsum

            @pl.when(i > 0)
            def _():
                loss_ref[...] += lsum
                dg_ref[...] += gsum

    blk = pl.BlockSpec((TM_MM, D), lambda i, j: (i, 0))
    vec = pl.BlockSpec((1, D), lambda i, j: (0, 0))
    return pl.pallas_call(
        body, name="mlp_fwd_loss", grid=(T // TM_MM, NDEV),
        in_specs=[blk, pl.BlockSpec((1, D, FF_B), lambda i, j: (j, 0, 0)), pl.BlockSpec((1, FF_B, D), lambda i, j: (j, 0, 0)),
                  blk, vec, blk],
        out_specs=(pl.BlockSpec((1, 128), lambda i, j: (0, 0)), blk, blk, vec),
        out_shape=(_sds((1, 128), F32), _sds((T, D), F32), _sds((T, D), BF16), _sds((1, D), F32)),
        scratch_shapes=[pltpu.VMEM((TM_MM, D), F32)],
        compiler_params=_cp(("arbitrary", "arbitrary"), 56))(m, wup_g, wdown_g, h1, g_final, tgt)


def _mlp_bwd(m, dh2_b, wup_g, wdown_g, h1, g_mlp, dh2):
    ni = T // TM_B
    nsub = TM_B // TM_E

    def body(m_ref, dh_ref, wu_ref, wd_ref, h1_ref, g_ref, dres_ref, dwu_ref, dwd_ref, dh1_ref, dh1b_ref, dg_ref,
             dm_ref, acc_u, acc_d):
        j, i = pl.program_id(0), pl.program_id(1)
        rows = pl.ds(pl.multiple_of(i * TM_B, TM_B), TM_B)
        mv, dh = m_ref[...], dh_ref[...]
        r = jnp.maximum(_dot(mv, wu_ref[0]), 0.0)
        act = (r * r).astype(BF16)
        dact = _dot(dh, wd_ref[0], NT)
        dup = (dact * (2.0 * r)).astype(BF16)
        pd = _dot(act, dh, TN)
        pu = _dot(mv, dup, TN)
        dmv = _dot(dup, wu_ref[0], NT)

        @pl.when(i == 0)
        def _():
            acc_u[...] = pu
            acc_d[...] = pd

        @pl.when(i > 0)
        def _():
            acc_u[...] += pu
            acc_d[...] += pd

        @pl.when(i == ni - 1)
        def _():
            dwu_ref[0] = acc_u[...].astype(BF16)
            dwd_ref[0] = acc_d[...].astype(BF16)

        @pl.when(j == 0)
        def _():
            dm_ref[rows, :] = dmv

        @pl.when(j > 0)
        def _():
            dm_ref[rows, :] += dmv

        @pl.when(j == NDEV - 1)
        def _():
            gsum = jnp.zeros((1, D), F32)
            for s in range(nsub):
                sub = slice(s * TM_E, (s + 1) * TM_E)
                dm_rows = dm_ref[pl.ds(pl.multiple_of(i * TM_B + s * TM_E, TM_E), TM_E), :]
                dx, gpart = _norm_bwd_rows(h1_ref[sub, :], g_ref[...], dm_rows, dres_ref[sub, :])
                dh1_ref[sub, :] = dx
                dh1b_ref[sub, :] = dx.astype(BF16)
                gsum = gsum + gpart

            @pl.when(i == 0)
            def _():
                dg_ref[...] = gsum

            @pl.when(i > 0)
            def _():
                dg_ref[...] += gsum

    blk = pl.BlockSpec((TM_B, D), lambda j, i: (i, 0))
    late = pl.BlockSpec((TM_B, D), lambda j, i: (jnp.where(j == NDEV - 1, i, 0), 0))
    vec = pl.BlockSpec((1, D), lambda j, i: (0, 0))
    wus = pl.BlockSpec((1, D, FF_B), lambda j, i: (j, 0, 0))
    wds = pl.BlockSpec((1, FF_B, D), lambda j, i: (j, 0, 0))
    return pl.pallas_call(
        body, name="mlp_bwd", grid=(NDEV, ni), in_specs=[blk, blk, wus, wds, late, vec, late],
        out_specs=(wus, wds, late, late, vec),
        out_shape=(_sds((NDEV, D, FF_B), BF16), _sds((NDEV, FF_B, D), BF16), _sds((T, D), F32), _sds((T, D), BF16),
                   _sds((1, D), F32)),
        scratch_shapes=[pltpu.VMEM((T, D), F32), pltpu.VMEM((D, FF_B), F32), pltpu.VMEM((FF_B, D), F32)],
        compiler_params=_cp(("arbitrary", "arbitrary"), 56))(m, dh2_b, wup_g, wdown_g, h1, g_mlp, dh2)


def _adamw(parts, w, m, v, name):
    rr, cc = w.shape
    nslot = parts.shape[0]
    tr = rr
    for cand in (256, 128, 64):
        if rr % cand == 0 and rr > cand:
            tr = cand
            break
    c1 = 1.0 - ADAM_B1 ** ADAM_STEP
    c2 = 1.0 - ADAM_B2 ** ADAM_STEP

    def body(p_ref, w_ref, m_ref, v_ref, g_ref, d_ref, nm_ref, nv_ref):
        g = p_ref[0].astype(F32)
        for s in range(1, nslot):
            g = g + p_ref[s].astype(F32)
        mn = ADAM_B1 * m_ref[...] + (1.0 - ADAM_B1) * g
        vn = ADAM_B2 * v_ref[...] + (1.0 - ADAM_B2) * (g * g)
        g_ref[...] = g
        nm_ref[...] = mn
        nv_ref[...] = vn
        d_ref[...] = -ADAM_LR * ((mn / c1) / (jnp.sqrt(vn / c2) + ADAM_EPS) + ADAM_WD * w_ref[...])

    blk = pl.BlockSpec((tr, cc), lambda i: (i, 0))
    return pl.pallas_call(
        body, name=name, grid=(rr // tr,),
        in_specs=[pl.BlockSpec((nslot, tr, cc), lambda i: (0, i, 0)), blk, blk, blk],
        out_specs=(blk,) * 4, out_shape=(_sds((rr, cc), F32),) * 4,
        compiler_params=_cp(("parallel",)))(parts, w, m, v)


RPB_N = NA_HEADS * 15 * 31
RPB_PAD = 4096
OWN_ROWS = NM + 8


def _pad_rows(a, rows):
    return jnp.pad(a, ((0, rows - a.shape[0]),) + ((0, 0),) * (a.ndim - 1))


def _pack_owned(meta_blk, lb_blk):
    return jnp.concatenate([meta_blk, _pad_rows(lb_blk.reshape(2, 128), 8)], axis=0)


LOSS_ROW = 28


def _pack_replicated(n_mix, n_mlp, n_final, hg_gain, rpb, loss_row=None):
    flat = _pad_rows(rpb.reshape(RPB_N), RPB_PAD)
    gain8 = _pad_rows(hg_gain.reshape(4, 128), 8)
    if loss_row is not None:
        gain8 = gain8 + jnp.pad(loss_row, ((LOSS_ROW - 24, 31 - LOSS_ROW), (0, 0)))
    return jnp.concatenate([n_mix.reshape(8, 128), n_mlp.reshape(8, 128), n_final.reshape(8, 128), gain8,
                            flat.reshape(32, 128)], axis=0)


def _unpack_replicated(a):
    return (a[0:8].reshape(1, D), a[8:16].reshape(1, D), a[16:24].reshape(D), a[24:28].reshape(1, 512),
            a[32:64].reshape(RPB_PAD)[:RPB_N].reshape(1, NA_HEADS, 15, 31))


def kernel(x, meta_tokens, w_in, w_na_out, w_hg_out, w_o, w_up, w_down, norm_mix, norm_mlp, norm_final, hg_norm, na_rpb, hg_lb_logits, loss_target, m_meta_tokens, m_w_in, m_w_na_out, m_w_hg_out, m_w_o, m_w_up, m_w_down, m_norm_mix, m_norm_mlp, m_norm_final, m_hg_norm, m_na_rpb, m_hg_lb_logits, v_meta_tokens, v_w_in, v_w_na_out, v_w_hg_out, v_w_o, v_w_up, v_w_down, v_norm_mix, v_norm_mlp, v_norm_final, v_hg_norm, v_na_rpb, v_hg_lb_logits):
    owned = _pack_owned(meta_tokens, hg_lb_logits)
    first_masks = (ALL_PEERS, SAME_CORE_AND_SIBLING)
    first, tok = _exchange_start([owned, w_in[0].astype(BF16)], [False] * 2, "gather_first_start", first_masks)
    bias_tab = _na_bias_table(_tie(jnp.pad(na_rpb[0], ((0, 0), (0, 0), (0, 128 - 31))), tok, "tie_bias_table"))
    later = [w[0].astype(BF16) for w in (w_na_out, w_hg_out, w_o, w_up, w_down)]
    lead = jnp.zeros((NM, D), F32) + tok[0, 0]
    h0_rows = jnp.concatenate([lead, x[0], jnp.zeros((T - L, D), F32)], axis=0)
    tgt = jnp.concatenate([lead, loss_target[0], jnp.zeros((T - L, D), F32)], axis=0)
    (owned_g, _), first = _exchange_wait(first, [False] * 2, [h0_rows], "gather_small_wait", first_masks, which=(0,))
    meta_full = jnp.transpose(owned_g[:, 0:NM, :], (1, 0, 2)).reshape(NM, D)
    logits = jnp.transpose(owned_g[:, NM:NM + 2, :].reshape(NDEV, 2, 2, 64), (1, 2, 0, 3)).reshape(2, 2, 512)
    h0 = lax.dynamic_update_slice(h0_rows, meta_full, (0, 0))
    a, a_t = _norm_fwd_t(h0, norm_mix, "norm_mix_fwd")
    (_, win_l), _ = _exchange_wait(first, [False] * 2, [a, logits, tgt, bias_tab] + later, "gather_first_wait", first_masks,
                                   which=(1,))
    (win_g,) = _forward_to_sibling([win_l], "gather_first_forward")
    later[0] = _tie(later[0], win_g, "tie_gather_rest")
    gather_rest, tok = _exchange_start(later, [False] * 5, "gather_rest_start")
    win_g = _tie(win_g, tok, "tie_inproj")

    p_act = _inproj_fwd(a, win_g)
    o_na, lse = _na_fwd(p_act, bias_tab)
    qh, k_f, b_f, k_b, b_b = _hg_pre(p_act, logits)
    o_f, st_f = _hg_scan_fwd(qh, k_f, b_f, p_act, False)
    o_b, st_b = _hg_scan_fwd(qh, k_b, b_b, p_act, True)
    (wna_g, whg_g, wo_g, _, _), gather_rest = _exchange_wait(
        gather_rest, [False] * 5, [o_f, o_b, o_na], "gather_rest_wait_a", which=(0, 1, 2))
    w_na_full = jnp.transpose(wna_g, (1, 0, 2)).reshape(512, D)
    w_hg_full = jnp.transpose(whg_g, (1, 0, 2)).reshape(512, D)
    mix, u_hg = _mix_fwd(o_na, o_f, o_b, hg_norm, w_na_full, w_hg_full, p_act)
    h1, m_act = _wo_fwd(mix, wo_g.reshape(D, D), h0, norm_mlp)
    (_, _, wo_g, wup_g, wdown_g), _ = _exchange_wait(gather_rest, [False] * 5, [m_act], "gather_rest_wait_b", which=(3, 4))
    w_o_full = wo_g.reshape(D, D)
    loss_part, dh2, dh2_b, d_nfinal = _mlp_fwd_loss(m_act, wup_g, wdown_g, h1, norm_final.reshape(1, D), tgt)

    dwup_p, dwdown_p, dh1, dh1_b, d_nmlp = _mlp_bwd(m_act, dh2_b, wup_g, wdown_g, h1, norm_mlp, dh2)
    sc_mlp, tok = _exchange_start([dwup_p, dwdown_p], [True] * 2, "scatter_mlp_start")
    dmix, dwo = _wo_bwd(_tie(dh1_b, tok, "tie_wo_bwd"), w_o_full, mix)
    sc_wo, tok = _exchange_start([dwo.reshape(NDEV, D // NDEV, D)], [True], "scatter_wo_start")
    dp_rest, dwna, dwhg, do_na, do_hg, d_gain = _mix_bwd(
        o_na, u_hg, o_f, o_b, hg_norm, w_na_full, w_hg_full, p_act, _tie(dmix, tok, "tie_mix_bwd"))
    owner_cols = lambda w: jnp.transpose(w.reshape(512, NDEV, D // NDEV), (1, 0, 2))
    sc_br, tok = _exchange_start([owner_cols(dwna), owner_cols(dwhg)], [True] * 2, "scatter_branch_start")
    do_hg = _tie(do_hg, tok, "tie_hg_scan_bwd")
    dq_f, dk_f, db_f, dv_f = _hg_scan_bwd(qh, k_f, b_f, p_act, st_f, do_hg, False)
    dq_b, dk_b, db_b, dv_b = _hg_scan_bwd(qh, k_b, b_b, p_act, st_b, do_hg, True)
    dp_rest, d_logits = _hg_pre_bwd(p_act, logits, dq_f, dq_b, dk_f, dk_b, db_f, db_b, dv_f, dv_b, dp_rest)
    dp_na, dbias = _na_bwd(p_act, do_na, lse, bias_tab)
    dwin_p = _inproj_bwd_dw(a_t, dp_na, dp_rest)
    far_mine, far_other = _far_slots()
    dwin_p = _add_into_slot(dwin_p, _sibling_swap_far(dwin_p, "pair_swap_in"), far_mine, "pair_add_in")
    sc_in, tok = _exchange_start([dwin_p], [True], "scatter_in_start", ALL_BUT_FAR_OTHER_CORE, absent=far_other)
    dh0, d_nmix, grad_x = _inproj_bwd_da(_tie(dp_na, tok, "tie_inproj_bwd_da"), dp_rest, win_g, h0, norm_mix, dh1)
    d_rpb = _na_rpb_reduce(_tie(dbias, tok, "tie_rpb_reduce"))[:, :, :31]

    res = {}

    def update(nm, parts, w, mm, vv):
        res[nm] = [r[None] for r in _adamw(parts, w[0], mm[0], vv[0], "adamw_" + nm)]
        return res[nm][1]

    wup_r, wdown_r = _exchange_wait(sc_mlp, [True] * 2, [dh0, d_rpb], "scatter_mlp_wait")
    update("w_up", wup_r, w_up, m_w_up, v_w_up)
    last = update("w_down", wdown_r, w_down, m_w_down, v_w_down)
    (wo_r,) = _exchange_wait(sc_wo, [True], [last], "scatter_wo_wait")
    last = update("w_o", wo_r, w_o, m_w_o, v_w_o)
    wna_r, whg_r = _exchange_wait(sc_br, [True] * 2, [last], "scatter_branch_wait")
    update("w_na_out", wna_r, w_na_out, m_w_na_out, v_w_na_out)
    last = update("w_hg_out", whg_r, w_hg_out, m_w_hg_out, v_w_hg_out)

    d_meta = jnp.transpose(dh0[0:NM].reshape(NM, NDEV, 128), (1, 0, 2))
    d_lg = jnp.transpose(d_logits.reshape(2, 2, NDEV, 64), (2, 0, 1, 3)).reshape(NDEV, 2, 128)
    owned_p = jnp.concatenate([d_meta, jnp.pad(d_lg, ((0, 0), (0, OWN_ROWS - NM - 2), (0, 0)))], axis=1)
    repl_p = _pack_replicated(d_nmix, d_nmlp, d_nfinal, d_gain, d_rpb, loss_part)
    grad_x = grad_x[None]
    done_first = [grad_x] + [res[nm][0] for nm in ("w_up", "w_down", "w_o", "w_na_out", "w_hg_out")]
    owned_r, repl_r = _exchange([owned_p, repl_p], [True, False], "scatter_small", done_first)
    own = _adamw(owned_r, owned, _pack_owned(m_meta_tokens, m_hg_lb_logits), _pack_owned(v_meta_tokens, v_hg_lb_logits),
                 "adamw_owned_small")
    res["meta_tokens"] = [r[0:NM] for r in own]
    res["hg_lb_logits"] = [r[NM:NM + 2].reshape(2, 2, 64) for r in own]
    rep = _adamw(repl_r, _pack_replicated(norm_mix, norm_mlp, norm_final, hg_norm, na_rpb),
                 _pack_replicated(m_norm_mix, m_norm_mlp, m_norm_final, m_hg_norm, m_na_rpb),
                 _pack_replicated(v_norm_mix, v_norm_mlp, v_norm_final, v_hg_norm, v_na_rpb), "adamw_replicated")
    for q in range(4):
        um = _unpack_replicated(rep[q])
        for nm, val in zip(("norm_mix", "norm_mlp", "norm_final", "hg_norm", "na_rpb"), um):
            res.setdefault(nm, [None] * 4)[q] = val
    (win_r,) = _exchange_wait(sc_in, [True], [rep[1], own[1]], "scatter_in_wait", ALL_BUT_FAR_OTHER_CORE)
    update("w_in", win_r, w_in, m_w_in, v_w_in)

    loss = jnp.sum(repl_r[:, LOSS_ROW, 0])
    order = ("meta_tokens", "w_in", "w_na_out", "w_hg_out", "w_o", "w_up", "w_down", "norm_mix", "norm_mlp", "norm_final",
             "hg_norm", "na_rpb", "hg_lb_logits")
    outs = [loss, grad_x]
    for q in range(4):
        outs += [res[nm][q] for nm in order]
    return tuple(outs)
```

```python
import functools

import numpy as np
import jax
import jax.numpy as jnp
from jax import lax
from jax.experimental import pallas as pl
from jax.experimental.pallas import tpu as pltpu

F32 = jnp.float32
BF16 = jnp.bfloat16

D = 1024
SEQ = 2048
NM = 16
L = SEQ + NM
T = 2176
NDEV = 8
EPS = 1e-6
GRID_W = 64
ROWS = SEQ // GRID_W
NA_HEADS = 8
NA_DH = 64
NA_SCALE = NA_DH ** -0.5
HG_HEADS = 4
HG_C = 16
NCHUNK = L // HG_C
D_FF = 4096
IN_COLS = 6144
NEG = -1e30

ADAM_LR = 0.001
ADAM_B1 = 0.9
ADAM_B2 = 0.999
ADAM_EPS = 1e-08
ADAM_WD = 0.01
ADAM_STEP = 10

MESH_ID = pl.DeviceIdType.MESH
ANY = pl.BlockSpec(memory_space=pl.ANY)

NN = (((1,), (0,)), ((), ()))
NT = (((1,), (1,)), ((), ()))
TN = (((0,), (0,)), ((), ()))


def _cp(sem=None, vmem_mb=48):
    return pltpu.CompilerParams(dimension_semantics=sem, vmem_limit_bytes=vmem_mb * 1024 * 1024)


def _dot(a, b, dims=NN):
    return lax.dot_general(a, b, dims, preferred_element_type=F32)


def _sds(shape, dtype):
    return jax.ShapeDtypeStruct(shape, dtype)


HBM = pl.BlockSpec(memory_space=pltpu.HBM)
SEM = pl.BlockSpec(memory_space=pltpu.SEMAPHORE)
EFFECT = pltpu.SideEffectType.DATAFLOW_SIDE_EFFECTING


def _exchange(arrs, scatter, name, after=()):
    n = len(arrs)
    after = list(after)
    out_shapes = []
    for a, sc in zip(arrs, scatter):
        out_shapes.append(_sds(a.shape if sc else (NDEV,) + a.shape, a.dtype))

    def body(*refs):
        ins, outs = refs[:n], refs[n + len(after):2 * n + len(after)]
        send_sems, recv_sems, loc_sems = refs[2 * n + len(after):]
        me = 4 * lax.axis_index("x") + 2 * lax.axis_index("y") + lax.axis_index("c")
        copies = []
        for k in range(n):
            src_me = ins[k].at[me] if scatter[k] else ins[k]
            loc = pltpu.make_async_copy(src_me, outs[k].at[me], loc_sems.at[k])
            loc.start()
            copies.append(loc)
        remote = sum(_peer_copies(ins, outs, scatter, send_sems, recv_sems), [])
        for cp in remote:
            cp.start()
        for cp in remote:
            cp.wait_recv()
        for cp in remote:
            cp.wait_send()
        for cp in copies:
            cp.wait()

    return pl.pallas_call(
        body, name=name, out_shape=tuple(out_shapes), in_specs=[ANY] * (n + len(after)), out_specs=tuple([ANY] * n),
        scratch_shapes=[pltpu.SemaphoreType.DMA((n * (NDEV - 1),)), pltpu.SemaphoreType.DMA((n * (NDEV - 1),)),
                        pltpu.SemaphoreType.DMA((n,))],
    )(*arrs, *after)


def _forward_to_sibling(bufs, name):
    n = len(bufs)

    def body(*refs):
        ins, outs = refs[:n], refs[n:2 * n]
        send_sems, recv_sems = refs[2 * n:]
        x, y, c = lax.axis_index("x"), lax.axis_index("y"), lax.axis_index("c")
        copies = []
        for k in range(n):
            for j, (cx, cy) in enumerate(((1 - x, y), (x, 1 - y), (1 - x, 1 - y))):
                slot = 4 * cx + 2 * cy + c
                copies.append(pltpu.make_async_remote_copy(
                    src_ref=ins[k].at[slot], dst_ref=outs[k].at[slot], send_sem=send_sems.at[3 * k + j],
                    recv_sem=recv_sems.at[3 * k + j], device_id=(x, y, 1 - c), device_id_type=MESH_ID))
        for cp in copies:
            cp.start()
        for cp in copies:
            cp.wait_recv()
        for cp in copies:
            cp.wait_send()

    return pl.pallas_call(
        body, name=name, out_shape=tuple(_sds(b.shape, b.dtype) for b in bufs), in_specs=[ANY] * n,
        out_specs=tuple([ANY] * n), input_output_aliases={k: k for k in range(n)},
        scratch_shapes=[pltpu.SemaphoreType.DMA((3 * n,)), pltpu.SemaphoreType.DMA((3 * n,))],
    )(*bufs)


ALL_PEERS = tuple(range(1, NDEV))
SAME_CORE_AND_SIBLING = (1, 2, 4, 6)
ALL_BUT_FAR_OTHER_CORE = (1, 2, 3, 4, 5, 6)


def _far_slots():
    far = 4 * (1 - lax.axis_index("x")) + 2 * (1 - lax.axis_index("y"))
    core = lax.axis_index("c")
    return far + core, far + 1 - core


def _sibling_swap_far(parts, name):
    def body(x_ref, o_ref, send_sem, recv_sem):
        sib = (lax.axis_index("x"), lax.axis_index("y"), 1 - lax.axis_index("c"))
        cp = pltpu.make_async_remote_copy(src_ref=x_ref.at[_far_slots()[1]], dst_ref=o_ref.at[0], send_sem=send_sem,
                                          recv_sem=recv_sem, device_id=sib, device_id_type=MESH_ID)
        cp.start()
        cp.wait()

    return pl.pallas_call(
        body, name=name, out_shape=_sds((1,) + parts.shape[1:], parts.dtype), in_specs=[ANY], out_specs=ANY,
        scratch_shapes=[pltpu.SemaphoreType.DMA(()), pltpu.SemaphoreType.DMA(())])(parts)


def _add_into_slot(parts, other, slot, name):
    _, rr, cc = parts.shape

    def body(slot_ref, p_ref, o_ref, out_ref):
        del slot_ref
        out_ref[...] = (p_ref[...].astype(F32) + o_ref[...].astype(F32)).astype(BF16)

    mine = pl.BlockSpec((1, rr // 2, cc), lambda j, s: (s[0], j, 0))
    grid_spec = pltpu.PrefetchScalarGridSpec(
        num_scalar_prefetch=1, grid=(2,),
        in_specs=[mine, pl.BlockSpec((1, rr // 2, cc), lambda j, s: (0, j, 0))], out_specs=mine)
    return pl.pallas_call(body, name=name, grid_spec=grid_spec, out_shape=_sds(parts.shape, BF16),
                          input_output_aliases={1: 0}, compiler_params=_cp(("arbitrary",)))(
                              jnp.reshape(slot, (1,)).astype(jnp.int32), parts, other)


def _peer_copies(srcs, lands, scatter, send_sems, recv_sems, masks=ALL_PEERS):
    x, y, c = lax.axis_index("x"), lax.axis_index("y"), lax.axis_index("c")
    me = 4 * x + 2 * y + c
    out = []
    for k in range(len(srcs)):
        out.append([])
        for m in (masks[k] if isinstance(masks[0], tuple) else masks):
            px, py, pc = x ^ (m >> 2), y ^ ((m >> 1) & 1), c ^ (m & 1)
            src = srcs[k].at[4 * px + 2 * py + pc] if scatter[k] else srcs[k]
            out[k].append(pltpu.make_async_remote_copy(
                src_ref=src, dst_ref=lands[k].at[me], send_sem=send_sems.at[k * (NDEV - 1) + m - 1],
                recv_sem=recv_sems.at[k * (NDEV - 1) + m - 1],
                device_id=(px, py, pc), device_id_type=MESH_ID))
    return out


def _own_copies(srcs, lands, scatter, own_sems):
    me = 4 * lax.axis_index("x") + 2 * lax.axis_index("y") + lax.axis_index("c")
    return [pltpu.make_async_copy(srcs[k].at[me] if scatter[k] else srcs[k], lands[k].at[me], own_sems.at[k])
            for k in range(len(srcs))]


def _exchange_start(arrs, scatter, name, masks=ALL_PEERS, absent=None):
    n = len(arrs)
    lands = []
    for a, sc in zip(arrs, scatter):
        land = lax.empty(a.shape if sc else (NDEV,) + a.shape, a.dtype)
        if absent is not None:
            land = lax.dynamic_update_index_in_dim(land, jnp.zeros((1,) + land.shape[1:], a.dtype), absent, 0)
        lands.append(land)

    def body(*refs):
        srcs, lnds = refs[:n], refs[n:2 * n]
        send_sems, recv_sems, own_sems = refs[2 * n:2 * n + 3]
        token = refs[-1]
        for cp in _own_copies(srcs, lnds, scatter, own_sems) + sum(_peer_copies(srcs, lnds, scatter, send_sems, recv_sems, masks), []):
            cp.start()
        token[...] = jnp.zeros_like(token)

    ops = [pltpu.with_memory_space_constraint(a, pltpu.HBM) for a in list(arrs) + lands]
    res = pl.pallas_call(
        body, name=name,
        out_shape=(pltpu.SemaphoreType.DMA((n * (NDEV - 1),)), pltpu.SemaphoreType.DMA((n * (NDEV - 1),)),
                   pltpu.SemaphoreType.DMA((n,)))
        + tuple(pltpu.HBM(o.shape, o.dtype) for o in ops) + (_sds((8, 128), F32),),
        in_specs=[HBM] * (2 * n), out_specs=(SEM, SEM, SEM) + (HBM,) * (2 * n) + (pl.BlockSpec(memory_space=pltpu.VMEM),),
        input_output_aliases={k: 3 + k for k in range(2 * n)},
        compiler_params=pltpu.CompilerParams(has_side_effects=EFFECT),
    )(*ops)
    return res[:-1], res[-1]


def _exchange_wait(handle, scatter, after, name, masks=ALL_PEERS, which=None):
    sems = handle[:3]
    bufs = handle[3:]
    n = len(bufs) // 2
    after = list(after)

    def body(*refs):
        srcs, lnds = refs[:n], refs[n:2 * n]
        copies = _peer_copies(srcs, lnds, scatter, refs[2 * n], refs[2 * n + 1], masks)
        own = _own_copies(srcs, lnds, scatter, refs[2 * n + 2])
        for k in (range(n) if which is None else which):
            own[k].wait()
            for cp in copies[k]:
                cp.wait_send()
                cp.wait_recv()

    res = pl.pallas_call(
        body, name=name, out_shape=tuple(pltpu.HBM(b.shape, b.dtype) for b in bufs),
        in_specs=[HBM] * (2 * n) + [SEM] * 3 + [ANY] * len(after), out_specs=(HBM,) * (2 * n),
        input_output_aliases={k: k for k in range(2 * n)},
        compiler_params=pltpu.CompilerParams(has_side_effects=EFFECT),
    )(*bufs, *sems, *after)
    return res[n:] if which is None else (res[n:], tuple(sems) + tuple(res))


def _tie(x, token, name):
    def body(x_ref, t_ref, o_ref):
        del x_ref, t_ref, o_ref

    return pl.pallas_call(body, name=name, out_shape=_sds(x.shape, x.dtype), in_specs=[ANY, ANY], out_specs=ANY,
                          input_output_aliases={0: 0})(x, token)


TM_E = 272


def _norm_fwd_t(h, g, name):
    def body(h_ref, g_ref, o_ref, ot_ref):
        xv = h_ref[...]
        r = lax.rsqrt(jnp.mean(xv * xv, axis=-1, keepdims=True) + EPS)
        y = xv * r * g_ref[...]
        o_ref[...] = y.astype(BF16)
        ot_ref[...] = y.T.astype(BF16)

    return pl.pallas_call(
        body, name=name, grid=(T // 128,),
        in_specs=[pl.BlockSpec((128, D), lambda i: (i, 0)), pl.BlockSpec((1, D), lambda i: (0, 0))],
        out_specs=(pl.BlockSpec((128, D), lambda i: (i, 0)), pl.BlockSpec((D, 128), lambda i: (0, i))),
        out_shape=(_sds((T, D), BF16), _sds((D, T), BF16)), compiler_params=_cp(("parallel",)))(h, g)


def _norm_bwd_rows(xv, gv, dnv, dres):
    r = lax.rsqrt(jnp.mean(xv * xv, axis=-1, keepdims=True) + EPS)
    xh = xv * r
    dxh = dnv * gv
    dx = dres + r * (dxh - xh * jnp.mean(dxh * xh, axis=-1, keepdims=True))
    return dx, jnp.sum(dnv * xh, axis=0, keepdims=True)


TM_MM = 1088


def _inproj_fwd(a, w_g):
    nb = w_g.shape[2]

    def body(a_ref, w_ref, o_ref):
        o_ref[...] = _dot(a_ref[...], w_ref[0])

    return pl.pallas_call(
        body, name="inproj_fwd", grid=(T // TM_MM, NDEV),
        in_specs=[pl.BlockSpec((TM_MM, D), lambda i, j: (i, 0)), pl.BlockSpec((1, D, nb), lambda i, j: (j, 0, 0))],
        out_specs=pl.BlockSpec((TM_MM, nb), lambda i, j: (i, j)), out_shape=_sds((T, NDEV * nb), F32),
        compiler_params=_cp(("parallel", "parallel")))(a, w_g)


TM_B = 544


W_IN_B = IN_COLS // NDEV


NA_BLKS = 1536 // W_IN_B


def _dp_specs(rows, row_index):
    return [pl.BlockSpec((rows, W_IN_B), lambda *g: (row_index(*g), jnp.minimum(g[-1], NA_BLKS - 1))),
            pl.BlockSpec((rows, W_IN_B), lambda *g: (row_index(*g), jnp.maximum(g[-1] - NA_BLKS, 0)))]


def _inproj_bwd_dw(a_t, dp_na, dp_rest):
    def body(at_ref, na_ref, rest_ref, dw_ref):
        j = pl.program_id(0)

        @pl.when(j < NA_BLKS)
        def _():
            dw_ref[0] = _dot(at_ref[...], na_ref[...]).astype(BF16)

        @pl.when(j >= NA_BLKS)
        def _():
            dw_ref[0] = _dot(at_ref[...], rest_ref[...]).astype(BF16)

    return pl.pallas_call(
        body, name="inproj_bwd_dw", grid=(NDEV,),
        in_specs=[pl.BlockSpec((D, T), lambda j: (0, 0))] + _dp_specs(T, lambda j: 0),
        out_specs=pl.BlockSpec((1, D, W_IN_B), lambda j: (j, 0, 0)), out_shape=_sds((NDEV, D, W_IN_B), BF16),
        compiler_params=_cp(("parallel",)))(a_t, dp_na, dp_rest)


def _inproj_bwd_da(dp_na, dp_rest, w_g, h0, g_mix, dh1):
    nsub = TM_MM // TM_E
    nblk = T // TM_MM

    def seq_copies(b, dh0_ref, gx_ref, sems):
        out = []
        for s in range(nsub):
            lo, hi = max(NM, b * TM_MM + s * TM_E), min(L, b * TM_MM + (s + 1) * TM_E)
            if hi > lo:
                out.append((s, pltpu.make_async_copy(dh0_ref.at[pl.ds(lo - b * TM_MM, hi - lo)],
                                                     gx_ref.at[pl.ds(lo - NM, hi - lo)], sems.at[b * nsub + s])))
        return out

    def body(na_ref, rest_ref, w_ref, h0_ref, g_ref, dres_ref, dh0_ref, dg_ref, gx_ref, da, sems):
        i, j = pl.program_id(0), pl.program_id(1)
        dpv = jnp.where(j < NA_BLKS, na_ref[...], rest_ref[...])
        dav = _dot(dpv, w_ref[0], NT)

        @pl.when(j == 0)
        def _():
            da[...] = dav

        @pl.when(j > 0)
        def _():
            da[...] += dav

        @pl.when(j == NDEV - 1)
        def _():
            gsum = jnp.zeros((1, D), F32)
            for s in range(nsub):
                sub = slice(s * TM_E, (s + 1) * TM_E)
                dx, gpart = _norm_bwd_rows(h0_ref[sub, :], g_ref[...], da[sub, :], dres_ref[sub, :])
                dh0_ref[sub, :] = dx
                gsum = gsum + gpart
                for b in range(nblk):
                    for _, cp in (c for c in seq_copies(b, dh0_ref, gx_ref, sems) if c[0] == s):
                        pl.when(i == b)(cp.start)

            @pl.when(i == 0)
            def _():
                dg_ref[...] = gsum

            @pl.when(i > 0)
            def _():
                dg_ref[...] += gsum

            for b in range(nblk):
                @pl.when(i == b)
                def _(b=b):
                    for _, cp in seq_copies(b, dh0_ref, gx_ref, sems):
                        cp.wait()

    rblk = pl.BlockSpec((TM_MM, D), lambda i, j: (i, 0))
    vec = pl.BlockSpec((1, D), lambda i, j: (0, 0))
    return pl.pallas_call(
        body, name="inproj_bwd_da", grid=(T // TM_MM, NDEV),
        in_specs=_dp_specs(TM_MM, lambda i, j: i) + [pl.BlockSpec((1, D, W_IN_B), lambda i, j: (j, 0, 0)), rblk, vec, rblk],
        out_specs=(rblk, vec, ANY), out_shape=(_sds((T, D), F32), _sds((1, D), F32), _sds((L - NM, D), F32)),
        scratch_shapes=[pltpu.VMEM((TM_MM, D), F32), pltpu.SemaphoreType.DMA((nblk * nsub,))],
        compiler_params=_cp(("arbitrary", "arbitrary"), 56))(dp_na, dp_rest, w_g, h0, g_mix, dh1)


NA_QB = 256
NA_GROUPS = ROWS // 4
NA_UROWS = 11
NA_KW = NA_UROWS * GRID_W
NA_KU = 768


def _na_row_offset(var, i, j):
    valid = (j < 8, i <= j < i + 8, 3 <= j < NA_UROWS)[var]
    return (j - i + (7, 3, 0)[var]) if valid else None


def _na_bias_table(rp):
    def body(r_ref, o_ref):
        row3 = lax.broadcasted_iota(jnp.int32, (15, GRID_W, 128), 1)
        lane3 = lax.broadcasted_iota(jnp.int32, (15, GRID_W, 128), 2)
        w3 = lane3 & (GRID_W - 1)
        cs3 = jnp.clip(row3 - 8, 0, GRID_W - 16)
        lane = lax.broadcasted_iota(jnp.int32, (GRID_W, 128), 1)
        neg = jnp.full((GRID_W, 128), NEG, F32)
        z = jnp.stack([jnp.broadcast_to(r_ref[0, a:a + 1, :], (GRID_W, 128)) for a in range(15)])
        for bit in range(6):
            sh = 1 << bit
            z = jnp.where((row3 & sh) != 0, jnp.roll(z, sh, axis=2), z)
        z = jnp.roll(z, 128 - 15, axis=2)
        z = jnp.where(lane3 < GRID_W, z, 0.0)
        z = z + jnp.roll(z, GRID_W, axis=2)
        tabs = jnp.where((w3 >= cs3) & (w3 < cs3 + 16), z, NEG)
        tail = jnp.where(lane < GRID_W + NM, 0.0, NEG)
        for var in range(3):
            for i in range(4):
                for jp in range(NA_KU // 128):
                    halves = []
                    for j in (2 * jp, 2 * jp + 1):
                        a = _na_row_offset(var, i, j) if j < NA_UROWS else None
                        halves.append(tail if j >= NA_UROWS else (neg if a is None else tabs[a]))
                    o_ref[var, 0, i * 64:(i + 1) * 64, jp * 128:(jp + 1) * 128] = jnp.where(lane < GRID_W, halves[0], halves[1])

    return pl.pallas_call(
        body, name="na_bias_table", grid=(NA_HEADS,),
        in_specs=[pl.BlockSpec((1, 15, 128), lambda h: (h, 0, 0))],
        out_specs=pl.BlockSpec((3, 1, NA_QB, NA_KU), lambda h: (0, h, 0, 0)),
        out_shape=_sds((3, NA_HEADS, NA_QB, NA_KU), F32), compiler_params=_cp(("parallel",)))(rp)


def _na_var(g):
    return jnp.where(g == 0, 0, jnp.where(g == NA_GROUPS - 1, 2, 1))


def _na_load_window(src_ref, dst, g):
    us = jnp.clip(4 * g - 4, 0, ROWS - NA_UROWS)
    kstart = pl.multiple_of(NM + GRID_W * us, 16)
    dst[0:NA_KW, :] = src_ref[pl.ds(kstart, NA_KW), :].astype(BF16)
    dst[NA_KW:NA_KW + NM, :] = src_ref[0:NM, :].astype(BF16)
    dst[NA_KW + NM:, :] = jnp.zeros((NA_KU - NA_KW - NM, 128), BF16)
    return kstart


def _na_fwd(p_act, bias_tab):
    def body(q_ref, k_ref, v_ref, b_ref, o_ref, lse_ref, ku, vu):
        g = pl.program_id(1)
        _na_load_window(k_ref, ku, g)
        _na_load_window(v_ref, vu, g)
        qstart = pl.multiple_of(NM + NA_QB * g, 16)
        q = q_ref[pl.ds(qstart, NA_QB), :]
        lane = lax.broadcasted_iota(jnp.int32, (NA_QB, 128), 1)
        o_h, lse_h = [], []
        for h in range(2):
            hm = (lane < 64) if h == 0 else (lane >= 64)
            qm = (jnp.where(hm, q, 0.0) * NA_SCALE).astype(BF16)
            s = _dot(qm, ku[...], NT) + b_ref[0, h]
            m = jnp.max(s, axis=-1, keepdims=True)
            p = jnp.exp(s - m)
            l = jnp.sum(p, axis=-1, keepdims=True)
            o_h.append(_dot(p.astype(BF16), vu[...]) / l)
            lse_h.append(jnp.broadcast_to(m + jnp.log(l), (NA_QB, 128)))
        o_ref[pl.ds(qstart, NA_QB), :] = jnp.where(lane < 64, o_h[0], o_h[1]).astype(BF16)
        lse_ref[0, pl.ds(qstart, NA_QB), :] = jnp.where(lane < 64, lse_h[0], lse_h[1])

        @pl.when(g == 0)
        def _():
            qm_ = q_ref[0:NM, :]
            lane_m = lax.broadcasted_iota(jnp.int32, (NM, 128), 1)
            km, vm = ku[NA_KW:NA_KW + NM, :], vu[NA_KW:NA_KW + NM, :]
            om = []
            for h in range(2):
                hm = (lane_m < 64) if h == 0 else (lane_m >= 64)
                s = _dot(jnp.where(hm, qm_, 0.0).astype(BF16), km, NT) * NA_SCALE
                p = jnp.exp(s - jnp.max(s, axis=-1, keepdims=True))
                l = jnp.sum(p, axis=-1, keepdims=True)
                om.append(_dot(p.astype(BF16), vm) / l)
            o_ref[0:NM, :] = jnp.where(lane_m < 64, om[0], om[1]).astype(BF16)
            o_ref[L:T, :] = jnp.zeros((T - L, 128), BF16)
            lse_ref[0, 0:NM, :] = jnp.zeros((NM, 128), F32)
            lse_ref[0, L:T, :] = jnp.zeros((T - L, 128), F32)

    col = lambda off: pl.BlockSpec((T, 128), lambda hp, g: (0, off + hp))
    return pl.pallas_call(
        body, name="na_fwd", grid=(4, NA_GROUPS),
        in_specs=[col(0), col(4), col(8),
                  pl.BlockSpec((1, 2, NA_QB, NA_KU), lambda hp, g: (_na_var(g), hp, 0, 0))],
        out_specs=(pl.BlockSpec((T, 128), lambda hp, g: (0, hp)), pl.BlockSpec((1, T, 128), lambda hp, g: (hp, 0, 0))),
        out_shape=(_sds((T, 512), BF16), _sds((4, T, 128), F32)),
        scratch_shapes=[pltpu.VMEM((NA_KU, 128), BF16), pltpu.VMEM((NA_KU, 128), BF16)],
        compiler_params=_cp(("parallel", "arbitrary")))(p_act, p_act, p_act, bias_tab)


def _na_bwd(p_act, do, lse, bias_tab):
    def body(q_ref, k_ref, v_ref, do_ref, lse_ref, b_ref, dp_ref, db_ref, ku, vu, dq_ref, dk_ref, dv_ref, stage, sems):
        g = pl.program_id(1)

        @pl.when(g == 0)
        def _():
            dq_ref[...] = jnp.zeros((T, 128), F32)
            dk_ref[...] = jnp.zeros((T, 128), F32)
            dv_ref[...] = jnp.zeros((T, 128), F32)

        kstart = _na_load_window(k_ref, ku, g)
        _na_load_window(v_ref, vu, g)
        qstart = pl.multiple_of(NM + NA_QB * g, 16)
        q = q_ref[pl.ds(qstart, NA_QB), :]
        dov = do_ref[pl.ds(qstart, NA_QB), :]
        lsev = lse_ref[0, pl.ds(qstart, NA_QB), :]
        lane = lax.broadcasted_iota(jnp.int32, (NA_QB, 128), 1)
        first = (g == 0) | (g == 1) | (g == NA_GROUPS - 1)
        dq_h = []
        dku = jnp.zeros((NA_KU, 128), F32)
        dvu = jnp.zeros((NA_KU, 128), F32)
        for h in range(2):
            hm = (lane < 64) if h == 0 else (lane >= 64)
            qm = (jnp.where(hm, q, 0.0) * NA_SCALE).astype(BF16)
            dom = jnp.where(hm, dov, 0.0).astype(BF16)
            s = _dot(qm, ku[...], NT) + b_ref[0, h]
            p = jnp.exp(s - lsev[:, 64 * h:64 * h + 1])
            dp = _dot(dom, vu[...], NT)
            delta = jnp.sum(p * dp, axis=-1, keepdims=True)
            ds = p * (dp - delta)

            @pl.when(first)
            def _():
                db_ref[0, h] = ds

            @pl.when(jnp.logical_not(first))
            def _():
                db_ref[0, h] += ds

            dsb = ds.astype(BF16)
            dq_h.append(_dot(dsb, ku[...]) * NA_SCALE)
            dku = dku + _dot(dsb, qm, TN)
            dvu = dvu + _dot(p.astype(BF16), dom, TN)
        dq_ref[pl.ds(qstart, NA_QB), :] = jnp.where(lane < 64, dq_h[0], dq_h[1])
        dk_ref[pl.ds(kstart, NA_KW), :] += dku[0:NA_KW]
        dv_ref[pl.ds(kstart, NA_KW), :] += dvu[0:NA_KW]
        dk_ref[0:NM, :] += dku[NA_KW:NA_KW + NM]
        dv_ref[0:NM, :] += dvu[NA_KW:NA_KW + NM]

        @pl.when(g == 0)
        def _():
            qm_ = q_ref[0:NM, :]
            dom_ = do_ref[0:NM, :]
            lane_m = lax.broadcasted_iota(jnp.int32, (NM, 128), 1)
            km, vm = ku[NA_KW:NA_KW + NM, :], vu[NA_KW:NA_KW + NM, :]
            dqs = []
            dkm = jnp.zeros((NM, 128), F32)
            dvm = jnp.zeros((NM, 128), F32)
            for h in range(2):
                hm = (lane_m < 64) if h == 0 else (lane_m >= 64)
                qh = jnp.where(hm, qm_, 0.0).astype(BF16)
                doh = jnp.where(hm, dom_, 0.0).astype(BF16)
                s = _dot(qh, km, NT) * NA_SCALE
                e = jnp.exp(s - jnp.max(s, axis=-1, keepdims=True))
                p = e / jnp.sum(e, axis=-1, keepdims=True)
                dp = _dot(doh, vm, NT)
                ds = p * (dp - jnp.sum(p * dp, axis=-1, keepdims=True))
                dsb = (ds * NA_SCALE).astype(BF16)
                dqs.append(_dot(dsb, km))
                dkm = dkm + _dot(dsb, qh, TN)
                dvm = dvm + _dot(p.astype(BF16), doh, TN)
            dq_ref[0:NM, :] = jnp.where(lane_m < 64, dqs[0], dqs[1])
            dk_ref[0:NM, :] += dkm
            dv_ref[0:NM, :] += dvm

        @pl.when(g == NA_GROUPS - 1)
        def _():
            copies = []
            for n, acc in enumerate((dq_ref, dk_ref, dv_ref)):
                stage[n] = acc[...].astype(BF16)
                cols = pl.ds(pl.multiple_of(512 * n + 128 * pl.program_id(0), 128), 128)
                copies.append(pltpu.make_async_copy(stage.at[n], dp_ref.at[pl.ds(0, T), cols], sems.at[n]))
                copies[-1].start()
            for cp in copies:
                cp.wait()

    col = lambda off: pl.BlockSpec((T, 128), lambda hp, g: (0, off + hp))
    ocol = pl.BlockSpec((T, 128), lambda hp, g: (0, hp))
    bspec = pl.BlockSpec((1, 2, NA_QB, NA_KU), lambda hp, g: (_na_var(g), hp, 0, 0))
    return pl.pallas_call(
        body, name="na_bwd", grid=(4, NA_GROUPS),
        in_specs=[col(0), col(4), col(8), ocol, pl.BlockSpec((1, T, 128), lambda hp, g: (hp, 0, 0)), bspec],
        out_specs=(ANY, bspec),
        out_shape=(_sds((T, 1536), BF16), _sds((3, NA_HEADS, NA_QB, NA_KU), F32)),
        scratch_shapes=[pltpu.VMEM((NA_KU, 128), BF16), pltpu.VMEM((NA_KU, 128), BF16)] + [pltpu.VMEM((T, 128), F32)] * 3
        + [pltpu.VMEM((3, T, 128), BF16), pltpu.SemaphoreType.DMA((3,))],
        compiler_params=_cp(("parallel", "arbitrary")))(p_act, p_act, p_act, do, lse, bias_tab)


def _na_rpb_reduce(dbias):
    def body(db_ref, o_ref):
        lane = lax.broadcasted_iota(jnp.int32, (GRID_W, 128), 1)
        row3 = lax.broadcasted_iota(jnp.int32, (15, GRID_W, 128), 1)
        lane3 = lax.broadcasted_iota(jnp.int32, (15, GRID_W, 128), 2)
        accs = []
        for a in range(15):
            acc = jnp.zeros((GRID_W, 128), F32)
            for var in range(3):
                for i in range(4):
                    for j in range(NA_UROWS):
                        if _na_row_offset(var, i, j) == a:
                            pair = db_ref[var, 0, i * 64:(i + 1) * 64, (j // 2) * 128:(j // 2 + 1) * 128]
                            acc = acc + jnp.where((lane < GRID_W) if j % 2 == 0 else (lane >= GRID_W), pair, 0.0)
            accs.append(acc)
        z = jnp.stack(accs)
        z = jnp.where(lane3 < GRID_W, z + jnp.roll(z, GRID_W, axis=2), 0.0)
        for bit in range(6):
            sh = 1 << bit
            z = jnp.where((row3 & sh) != 0, jnp.roll(z, 128 - sh, axis=2), z)
        z = jnp.roll(z, 15, axis=2)
        o_ref[0] = jnp.sum(z, axis=1)

    return pl.pallas_call(
        body, name="na_rpb_reduce", grid=(NA_HEADS,),
        in_specs=[pl.BlockSpec((3, 1, NA_QB, NA_KU), lambda h: (0, h, 0, 0))],
        out_specs=pl.BlockSpec((1, 15, 128), lambda h: (h, 0, 0)), out_shape=_sds((NA_HEADS, 15, 128), F32),
        compiler_params=_cp(("parallel",)))(dbias)


HG_RB = 128
HG_NB = T // HG_RB
HG_SLOTS = HG_NB * 8
HI = lax.Precision.HIGHEST
HG_UNROLL = 4
HG_UNROLL_WIDE = 8


def _chunk_tri(lower):
    r = lax.broadcasted_iota(jnp.int32, (HG_RB, HG_RB), 0)
    c = lax.broadcasted_iota(jnp.int32, (HG_RB, HG_RB), 1)
    same = (r // HG_C) == (c // HG_C)
    keep = (c <= r) if lower else (c >= r)
    return jnp.where(same & keep, 1.0, 0.0).astype(F32)


def _hg_gate_terms(z, lg):
    dl = lg[0:1, :] - lg[1:2, :]
    log_lb = jax.nn.log_sigmoid(dl)
    log_1mlb = jax.nn.log_sigmoid(-dl)
    yz = log_1mlb + jax.nn.log_sigmoid(z)
    log_f = jnp.logaddexp(log_lb, yz)
    snz = jax.nn.sigmoid(-z)
    k = jnp.exp(log_1mlb) * snz
    w2 = jnp.exp(yz - log_f)
    return log_f, k, snz, w2


def _hg_pre(p_act, logits):
    def body(q_ref, zf_ref, zb_ref, lg_ref, qh_ref, kf_ref, bf_ref, kb_ref, bb_ref):
        qh_ref[...] = jax.nn.silu(q_ref[...])
        lf, kf, _, _ = _hg_gate_terms(zf_ref[...], lg_ref[0])
        kf_ref[...] = kf
        bf_ref[...] = jnp.dot(_chunk_tri(True), lf, precision=HI, preferred_element_type=F32)
        lb_, kb, _, _ = _hg_gate_terms(zb_ref[...], lg_ref[1])
        kb_ref[...] = kb
        bb_ref[...] = jnp.dot(_chunk_tri(False), lb_, precision=HI, preferred_element_type=F32)

    blk = lambda c: pl.BlockSpec((HG_RB, 512), lambda i: (i, c))
    ob = pl.BlockSpec((HG_RB, 512), lambda i: (i, 0))
    return pl.pallas_call(
        body, name="hg_pre", grid=(HG_NB,),
        in_specs=[blk(3), blk(4), blk(5), pl.BlockSpec((2, 2, 512), lambda i: (0, 0, 0))],
        out_specs=(ob,) * 5, out_shape=(_sds((T, 512), F32),) * 5,
        compiler_params=_cp(("parallel",)))(p_act, p_act, p_act, logits)


def _bdot(a, b, ca, cb):
    return lax.dot_general(a.astype(BF16), b.astype(BF16), (((ca,), (cb,)), ((0,), (0,))), preferred_element_type=F32)


HG_S = 8
HG_NS = HG_RB // HG_S


def _lane_sums(xs):
    l_io = lax.broadcasted_iota(jnp.int32, (HG_NS, HG_S, HG_S), 2)
    a = jnp.zeros((HG_NS, HG_S, HG_S), F32)
    for j, x in enumerate(xs):
        a = a + jnp.where(l_io == j, jnp.sum(x, axis=-1, keepdims=True), 0.0)
    return a


def _halves(x):
    y = x.reshape(8, 2, HG_S, x.shape[-1])
    return y[:, 0], y[:, 1]


def _join(first, second):
    return jnp.stack([first, second], axis=1).reshape(HG_RB, first.shape[-1])


def _cross_split(rev, b4):
    b_1, b_2 = _halves(b4)
    if rev:
        r = b_2[:, 0:1, :]
        return jnp.exp(b_1 - r), jnp.exp(r - b_2)
    r = b_1[:, HG_S - 1:HG_S, :]
    return jnp.exp(b_2 - r), jnp.exp(r - b_1)


def _hg_scan_fwd(qh, k, b, p_act, rev):
    anchor = 0 if rev else HG_C - 1

    def body(q_ref, k_ref, b_ref, v_ref, o_ref, st_ref, dsc):
        def phase_a(blk, _):
            rows = pl.ds(pl.multiple_of(blk * HG_RB, HG_RB), HG_RB)
            b3 = b_ref[rows, :].reshape(8, HG_C, 128)
            k3 = k_ref[rows, :].reshape(8, HG_C, 128)
            v3 = v_ref[rows, :].reshape(8, HG_C, 128)
            bl = b3[:, anchor:anchor + 1, :]
            kt = k3 * jnp.exp(bl - b3)
            st_ref[0, pl.ds(pl.multiple_of(blk * 8, 8), 8)] = _bdot(v3, kt, 1, 1)
            dsc[pl.ds(pl.multiple_of(blk * 8, 8), 8), :] = jnp.exp(bl[:, 0, :])
            return 0

        lax.fori_loop(0, HG_NB, phase_a, 0, unroll=HG_UNROLL_WIDE)

        def phase_b(n, carry):
            c = (NCHUNK - 1 - n) if rev else n
            u = st_ref[0, c]
            st_ref[0, c] = carry
            return carry * dsc[pl.ds(c, 1), :] + u

        lax.fori_loop(0, NCHUNK // 3, lambda n3, s: phase_b(3 * n3 + 2, phase_b(3 * n3 + 1, phase_b(3 * n3, s))),
                      jnp.zeros((128, 128), F32))
        for c in range(NCHUNK, HG_SLOTS):
            st_ref[0, c] = jnp.zeros((128, 128), F32)

        t_io = lax.broadcasted_iota(jnp.int32, (HG_NS, HG_S, 128), 1)

        def phase_c(blk, _):
            rows = pl.ds(pl.multiple_of(blk * HG_RB, HG_RB), HG_RB)
            b4 = b_ref[rows, :].reshape(HG_NS, HG_S, 128)
            k4 = k_ref[rows, :].reshape(HG_NS, HG_S, 128)
            q4 = q_ref[rows, :].reshape(HG_NS, HG_S, 128)
            v4 = v_ref[rows, :].reshape(HG_NS, HG_S, 128)
            st = st_ref[0, pl.ds(pl.multiple_of(blk * 8, 8), 8)]
            o = _bdot((q4 * jnp.exp(b4)).reshape(8, HG_C, 128), st, 2, 2).reshape(HG_RB, 128)
            terms = []
            for s in range(HG_S):
                ok = (t_io <= s) if rev else (t_io >= s)
                f = jnp.exp(jnp.where(ok, b4 - b4[:, s:s + 1, :], NEG))
                terms.append(q4 * f * k4[:, s:s + 1, :])
            o_in = _bdot(_lane_sums(terms), v4, 2, 1)
            wq, wk = _cross_split(rev, b4)
            q_1, q_2 = _halves(q4)
            k_1, k_2 = _halves(k4)
            v_1, v_2 = _halves(v4)
            o_1, o_2 = _halves(o_in)
            if rev:
                o_1 = o_1 + _bdot(_bdot(q_1 * wq, k_2 * wk, 2, 2), v_2, 2, 1)
            else:
                o_2 = o_2 + _bdot(_bdot(q_2 * wq, k_1 * wk, 2, 2), v_1, 2, 1)
            o_ref[rows, :] = o + _join(o_1, o_2)
            return 0

        lax.fori_loop(0, HG_NB, phase_c, 0, unroll=HG_UNROLL_WIDE)

    col = pl.BlockSpec((T, 128), lambda h: (0, h))
    return pl.pallas_call(
        body, name="hg_scan_bwd_dir" if rev else "hg_scan_fwd_dir", grid=(HG_HEADS,),
        in_specs=[col, col, col, pl.BlockSpec((T, 128), lambda h: (0, 24 + h))],
        out_specs=(col, pl.BlockSpec((1, HG_SLOTS, 128, 128), lambda h: (h, 0, 0, 0))),
        out_shape=(_sds((T, 512), F32), _sds((HG_HEADS, HG_SLOTS, 128, 128), F32)),
        scratch_shapes=[pltpu.VMEM((HG_SLOTS, 128), F32)],
        compiler_params=_cp(("parallel",), 56))(qh, k, b, p_act)


def _hg_scan_bwd(qh, k, b, p_act, st, do, rev):
    anchor = 0 if rev else HG_C - 1

    def body(q_ref, k_ref, b_ref, v_ref, st_ref, do_ref, dq_ref, dk_ref, db_ref, dv_ref, gst, dsc, dbl):
        def phase_a(blk, _):
            rows = pl.ds(pl.multiple_of(blk * HG_RB, HG_RB), HG_RB)
            b3 = b_ref[rows, :].reshape(8, HG_C, 128)
            q3 = q_ref[rows, :].reshape(8, HG_C, 128)
            do3 = do_ref[rows, :].reshape(8, HG_C, 128)
            gst[pl.ds(pl.multiple_of(blk * 8, 8), 8)] = _bdot(do3, q3 * jnp.exp(b3), 1, 1)
            dsc[pl.ds(pl.multiple_of(blk * 8, 8), 8), :] = jnp.exp(b3[:, anchor, :])
            return 0

        lax.fori_loop(0, HG_NB, phase_a, 0, unroll=HG_UNROLL_WIDE)

        def phase_b(n, carry):
            c = n if rev else (NCHUNK - 1 - n)
            w = gst[c]
            gst[c] = carry
            dcv = dsc[pl.ds(c, 1), :]
            dbl[pl.ds(c, 1), :] = dcv * jnp.sum(st_ref[0, c] * carry, axis=0, keepdims=True)
            return carry * dcv + w

        lax.fori_loop(0, NCHUNK // 3, lambda n3, s: phase_b(3 * n3 + 2, phase_b(3 * n3 + 1, phase_b(3 * n3, s))),
                      jnp.zeros((128, 128), F32))
        for c in range(NCHUNK, HG_SLOTS):
            gst[c] = jnp.zeros((128, 128), F32)
            dbl[c:c + 1, :] = jnp.zeros((1, 128), F32)

        t_io = lax.broadcasted_iota(jnp.int32, (HG_NS, HG_S, 128), 1)
        t16 = lax.broadcasted_iota(jnp.int32, (8, HG_C, 128), 1)
        r_io = lax.broadcasted_iota(jnp.int32, (HG_NS, HG_S, HG_S), 1)
        l_io = lax.broadcasted_iota(jnp.int32, (HG_NS, HG_S, HG_S), 2)

        def phase_c(blk, _):
            rows = pl.ds(pl.multiple_of(blk * HG_RB, HG_RB), HG_RB)
            cs = pl.ds(pl.multiple_of(blk * 8, 8), 8)
            b4 = b_ref[rows, :].reshape(HG_NS, HG_S, 128)
            k4 = k_ref[rows, :].reshape(HG_NS, HG_S, 128)
            q4 = q_ref[rows, :].reshape(HG_NS, HG_S, 128)
            v4 = v_ref[rows, :].reshape(HG_NS, HG_S, 128)
            do4 = do_ref[rows, :].reshape(HG_NS, HG_S, 128)
            b3, k3, q3 = (z.reshape(8, HG_C, 128) for z in (b4, k4, q4))
            v3, do3 = v4.reshape(8, HG_C, 128), do4.reshape(8, HG_C, 128)
            s_t = st_ref[0, cs]
            g_t = gst[cs]
            bl = b3[:, anchor:anchor + 1, :]
            ekl = jnp.exp(bl - b3)
            kt = k3 * ekl
            dkt = _bdot(v3, g_t, 2, 1)
            dq = (_bdot(do3, s_t, 2, 1) * jnp.exp(b3)).reshape(HG_NS, HG_S, 128)
            dk = (dkt * ekl).reshape(HG_NS, HG_S, 128)
            dv = _bdot(kt, g_t, 2, 2).reshape(HG_NS, HG_S, 128)
            dbl3 = dbl[cs, :].reshape(8, 1, 128) + jnp.sum(dkt * kt, axis=1, keepdims=True)
            causal = (l_io >= r_io) if rev else (l_io <= r_io)
            da = jnp.where(causal, _bdot(do4, v4, 2, 2), 0.0)
            causal_t = (l_io <= r_io) if rev else (l_io >= r_io)
            dat = jnp.where(causal_t, _bdot(v4, do4, 2, 2), 0.0)
            for s in range(HG_S):
                ok = (t_io <= s) if rev else (t_io >= s)
                f = jnp.exp(jnp.where(ok, b4 - b4[:, s:s + 1, :], NEG))
                dq = dq + da[:, :, s:s + 1] * (f * k4[:, s:s + 1, :])
            terms = []
            for t in range(HG_S):
                ok = (t_io >= t) if rev else (t_io <= t)
                e = jnp.exp(jnp.where(ok, b4[:, t:t + 1, :] - b4, NEG))
                eq = e * q4[:, t:t + 1, :]
                dk = dk + dat[:, :, t:t + 1] * eq
                terms.append(eq * k4)
            dv = dv + _bdot(_lane_sums(terms), do4, 2, 1)
            wq, wk = _cross_split(rev, b4)
            pick = (lambda z: _halves(z)) if rev else (lambda z: _halves(z)[::-1])
            (q_q, _), (_, k_k), (_, v_k), (do_q, _) = pick(q4), pick(k4), pick(v4), pick(do4)
            qx, kx = q_q * wq, k_k * wk
            dq_q = _bdot(_bdot(do_q, v_k, 2, 2), kx, 2, 1) * wq
            dk_k = _bdot(_bdot(v_k, do_q, 2, 2), qx, 2, 1) * wk
            dv_k = _bdot(_bdot(kx, qx, 2, 2), do_q, 2, 1)
            zero = jnp.zeros((8, HG_S, 128), F32)
            place_q = (lambda z: _join(z, zero)) if rev else (lambda z: _join(zero, z))
            place_k = (lambda z: _join(zero, z)) if rev else (lambda z: _join(z, zero))
            dq2 = dq.reshape(HG_RB, 128) + place_q(dq_q)
            dk2 = dk.reshape(HG_RB, 128) + place_k(dk_k)
            dv2 = dv.reshape(HG_RB, 128) + place_k(dv_k)
            dq3, dk3 = dq2.reshape(8, HG_C, 128), dk2.reshape(8, HG_C, 128)
            db = q3 * dq3 - k3 * dk3 + jnp.where(t16 == anchor, dbl3, 0.0)
            dq_ref[rows, :] = dq2
            dk_ref[rows, :] = dk2
            db_ref[rows, :] = db.reshape(HG_RB, 128)
            dv_ref[rows, :] = dv2
            return 0

        lax.fori_loop(0, HG_NB, phase_c, 0, unroll=HG_UNROLL)

    col = pl.BlockSpec((T, 128), lambda h: (0, h))
    return pl.pallas_call(
        body, name="hg_scan_bwd_dir_bwd" if rev else "hg_scan_fwd_dir_bwd", grid=(HG_HEADS,),
        in_specs=[col, col, col, pl.BlockSpec((T, 128), lambda h: (0, 24 + h)),
                  pl.BlockSpec((1, HG_SLOTS, 128, 128), lambda h: (h, 0, 0, 0)), col],
        out_specs=(col,) * 4, out_shape=(_sds((T, 512), F32),) * 4,
        scratch_shapes=[pltpu.VMEM((HG_SLOTS, 128, 128), F32), pltpu.VMEM((HG_SLOTS, 128), F32),
                        pltpu.VMEM((HG_SLOTS, 128), F32)],
        compiler_params=_cp(("parallel",), 56))(qh, k, b, p_act, st, do)


def _row_valid(i, tm):
    r = lax.broadcasted_iota(jnp.int32, (tm, 1), 0) + i * tm
    return r < L


def _hg_post_rows(o, gv, gain_v, valid):
    parts = []
    for h in range(HG_HEADS):
        oh = o[:, 128 * h:128 * (h + 1)]
        parts.append(oh * lax.rsqrt(jnp.mean(oh * oh, axis=-1, keepdims=True) + EPS))
    return jnp.where(valid, jnp.concatenate(parts, axis=1) * gain_v * jax.nn.silu(gv), 0.0)


def _hg_post_bwd_rows(du, o, gv, gain_v, valid):
    duv = jnp.where(valid, du, 0.0)
    sig = jax.nn.sigmoid(gv)
    sg = gv * sig
    dn = duv * gain_v * sg
    do_parts, n_parts = [], []
    for h in range(HG_HEADS):
        sl = slice(128 * h, 128 * (h + 1))
        oh = o[:, sl]
        r = lax.rsqrt(jnp.mean(oh * oh, axis=-1, keepdims=True) + EPS)
        nh = oh * r
        dnh = dn[:, sl]
        do_parts.append(r * (dnh - nh * jnp.mean(dnh * nh, axis=-1, keepdims=True)))
        n_parts.append(nh)
    n = jnp.where(valid, jnp.concatenate(n_parts, axis=1), 0.0)
    do = jnp.where(valid, jnp.concatenate(do_parts, axis=1), 0.0)
    dg = duv * n * gain_v * (sig * (1.0 + gv * (1.0 - sig)))
    return do, dg, jnp.sum(duv * n * sg, axis=0, keepdims=True)


def _hg_pre_bwd(p_act, logits, dq_f, dq_b, dk_f, dk_b, db_f, db_b, dv_f, dv_b, dp_rest):
    def body(q_ref, zf_ref, zb_ref, lg_ref, dqf_ref, dqb_ref, dkf_ref, dkb_ref, dbf_ref, dbb_ref, dvf_ref, dvb_ref, _,
             dp_ref, dlg_ref):
        dq_ref, dzf_ref, dzb_ref, di_ref = (dp_ref.at[:, 512 * c:512 * (c + 1)] for c in range(4))
        i = pl.program_id(0)
        valid = _row_valid(i, HG_RB)
        qv = q_ref[...]
        sig = jax.nn.sigmoid(qv)
        dq_ref[...] = jnp.where(valid, (dqf_ref[...] + dqb_ref[...]) * (sig * (1.0 + qv * (1.0 - sig))), 0.0).astype(BF16)
        di_ref[...] = jnp.where(valid, dvf_ref[...] + dvb_ref[...], 0.0).astype(BF16)
        for d, (z_ref, dk_r, db_r, dz_ref) in enumerate(((zf_ref, dkf_ref, dbf_ref, dzf_ref), (zb_ref, dkb_ref, dbb_ref, dzb_ref))):
            lg = lg_ref[d]
            dl = lg[0:1, :] - lg[1:2, :]
            lb = jax.nn.sigmoid(dl)
            one_m_lb = jax.nn.sigmoid(-dl)
            log_f, _, snz, w2 = _hg_gate_terms(z_ref[...], lg)
            dbv = jnp.where(valid, db_r[...], 0.0)
            dkv = jnp.where(valid, dk_r[...], 0.0)
            dlf = jnp.dot(_chunk_tri(d == 1), dbv, precision=HI, preferred_element_type=F32)
            sz = 1.0 - snz
            dz_ref[...] = (dlf * w2 * snz - dkv * one_m_lb * sz * snz).astype(BF16)
            dlb = jnp.sum(dlf * snz * jnp.exp(-log_f) - dkv * snz, axis=0, keepdims=True)
            dl0 = dlb * lb * one_m_lb
            part = jnp.concatenate([dl0, -dl0], axis=0)

            @pl.when(i == 0)
            def _():
                dlg_ref[d] = part

            @pl.when(i > 0)
            def _():
                dlg_ref[d] += part

    blk = lambda c: pl.BlockSpec((HG_RB, 512), lambda i: (i, c))
    ob = pl.BlockSpec((HG_RB, 512), lambda i: (i, 0))
    lgs = pl.BlockSpec((2, 2, 512), lambda i: (0, 0, 0))
    return pl.pallas_call(
        body, name="hg_pre_bwd", grid=(HG_NB,),
        in_specs=[blk(3), blk(4), blk(5), lgs] + [ob] * 8 + [ANY],
        out_specs=(pl.BlockSpec((HG_RB, 2048), lambda i: (i, 0)), lgs),
        out_shape=(_sds(dp_rest.shape, BF16), _sds((2, 2, 512), F32)), input_output_aliases={12: 0},
        compiler_params=_cp(("arbitrary",)))(p_act, p_act, p_act, logits, dq_f, dq_b, dk_f, dk_b, db_f, db_b, dv_f, dv_b,
                                             dp_rest)


def _mix_fwd(o_na, o_f, o_b, gain, w_na, w_hg, p_act):
    def body(ona_ref, of_ref, ob_ref, g_ref, gain_ref, wna_ref, whg_ref, gna_ref, ghg_ref, o_ref, u_ref):
        u = _hg_post_rows(of_ref[...] + ob_ref[...], g_ref[...], gain_ref[...], _row_valid(pl.program_id(0), TM_B)).astype(BF16)
        u_ref[...] = u
        y_na = _dot(ona_ref[...], wna_ref[...])
        y_hg = _dot(u, whg_ref[...])
        o_ref[...] = (jax.nn.sigmoid(gna_ref[...]) * y_na + jax.nn.sigmoid(ghg_ref[...]) * y_hg).astype(BF16)

    act = pl.BlockSpec((TM_B, 512), lambda i: (i, 0))
    wsp = pl.BlockSpec((512, D), lambda i: (0, 0))
    return pl.pallas_call(
        body, name="mix_fwd", grid=(T // TM_B,),
        in_specs=[act, act, act, pl.BlockSpec((TM_B, 512), lambda i: (i, 7)), pl.BlockSpec((1, 512), lambda i: (0, 0)),
                  wsp, wsp, pl.BlockSpec((TM_B, D), lambda i: (i, 4)), pl.BlockSpec((TM_B, D), lambda i: (i, 5))],
        out_specs=(pl.BlockSpec((TM_B, D), lambda i: (i, 0)), act), out_shape=(_sds((T, D), BF16), _sds((T, 512), BF16)),
        compiler_params=_cp(("parallel",)))(o_na, o_f, o_b, p_act, gain, w_na, w_hg, p_act, p_act)


DP_REST = IN_COLS - 1536


def _mix_bwd(o_na, u_hg, o_f, o_b, gain, w_na, w_hg, p_act, dmix):
    ni = T // TM_B

    def body(ona_ref, uhg_ref, of_ref, ob_ref, g_ref, gain_ref, wna_ref, whg_ref, gna_ref, ghg_ref, dmix_ref,
             dp_ref, dwna_ref, dwhg_ref, dona_ref, do_ref, dgain_ref, acc_na, acc_hg):
        i = pl.program_id(0)
        dg_ref, dgna_ref, dghg_ref = dp_ref.at[:, 2048:2560], dp_ref.at[:, 2560:3584], dp_ref.at[:, 3584:4608]
        dm = dmix_ref[...].astype(F32)
        dxs = []
        for x_ref, w_ref, gt_ref, dgt_ref, dw_ref, acc in (
                (ona_ref, wna_ref, gna_ref, dgna_ref, dwna_ref, acc_na), (uhg_ref, whg_ref, ghg_ref, dghg_ref, dwhg_ref, acc_hg)):
            xv = x_ref[...]
            y = _dot(xv, w_ref[...])
            sg = jax.nn.sigmoid(gt_ref[...])
            dgt_ref[...] = (dm * y * sg * (1.0 - sg)).astype(BF16)
            dy = (dm * sg).astype(BF16)
            dxs.append(_dot(dy, w_ref[...], NT))
            part = _dot(xv, dy, TN)

            @pl.when(i == 0)
            def _():
                acc[...] = part

            @pl.when(i > 0)
            def _():
                acc[...] += part

            @pl.when(i == ni - 1)
            def _():
                dw_ref[...] = acc[...].astype(BF16)

        dona_ref[...] = dxs[0]
        do, dg, gpart = _hg_post_bwd_rows(dxs[1], of_ref[...] + ob_ref[...], g_ref[...], gain_ref[...], _row_valid(i, TM_B))
        do_ref[...] = do
        dg_ref[...] = dg.astype(BF16)

        @pl.when(i == 0)
        def _():
            dgain_ref[...] = gpart

        @pl.when(i > 0)
        def _():
            dgain_ref[...] += gpart

    act = pl.BlockSpec((TM_B, 512), lambda i: (i, 0))
    wsp = pl.BlockSpec((512, D), lambda i: (0, 0))
    rblk = pl.BlockSpec((TM_B, D), lambda i: (i, 0))
    vec = pl.BlockSpec((1, 512), lambda i: (0, 0))
    return pl.pallas_call(
        body, name="mix_bwd", grid=(ni,),
        in_specs=[act, act, act, act, pl.BlockSpec((TM_B, 512), lambda i: (i, 7)), vec, wsp, wsp,
                  pl.BlockSpec((TM_B, D), lambda i: (i, 4)), pl.BlockSpec((TM_B, D), lambda i: (i, 5)), rblk],
        out_specs=(pl.BlockSpec((TM_B, DP_REST), lambda i: (i, 0)), wsp, wsp, act, act, vec),
        out_shape=(_sds((T, DP_REST), BF16), _sds((512, D), BF16), _sds((512, D), BF16),
                   _sds((T, 512), F32), _sds((T, 512), F32), _sds((1, 512), F32)),
        scratch_shapes=[pltpu.VMEM((512, D), F32), pltpu.VMEM((512, D), F32)],
        compiler_params=_cp(("arbitrary",)))(o_na, u_hg, o_f, o_b, p_act, gain, w_na, w_hg, p_act, p_act, dmix)


def _wo_fwd(mix, w_o, h0, g_mlp):
    def body(mix_ref, w_ref, h0_ref, g_ref, h1_ref, m_ref):
        h1 = h0_ref[...] + _dot(mix_ref[...], w_ref[...])
        h1_ref[...] = h1
        r = lax.rsqrt(jnp.mean(h1 * h1, axis=-1, keepdims=True) + EPS)
        m_ref[...] = (h1 * r * g_ref[...]).astype(BF16)

    blk = pl.BlockSpec((TM_B, D), lambda i: (i, 0))
    return pl.pallas_call(
        body, name="wo_fwd", grid=(T // TM_B,),
        in_specs=[blk, pl.BlockSpec((D, D), lambda i: (0, 0)), blk, pl.BlockSpec((1, D), lambda i: (0, 0))],
        out_specs=(blk, blk), out_shape=(_sds((T, D), F32), _sds((T, D), BF16)),
        compiler_params=_cp(("parallel",)))(mix, w_o, h0, g_mlp)


def _wo_bwd(dh1_b, w_o, mix):
    ni = T // TM_B

    def body(dh_ref, w_ref, mix_ref, dmix_ref, dw_ref, acc):
        i = pl.program_id(0)
        dh = dh_ref[...]
        dmix_ref[...] = _dot(dh, w_ref[...], NT).astype(BF16)
        part = _dot(mix_ref[...], dh, TN)

        @pl.when(i == 0)
        def _():
            acc[...] = part

        @pl.when(i > 0)
        def _():
            acc[...] += part

        @pl.when(i == ni - 1)
        def _():
            dw_ref[...] = acc[...].astype(BF16)

    blk = pl.BlockSpec((TM_B, D), lambda i: (i, 0))
    wsp = pl.BlockSpec((D, D), lambda i: (0, 0))
    return pl.pallas_call(
        body, name="wo_bwd", grid=(ni,), in_specs=[blk, wsp, blk], out_specs=(blk, wsp),
        out_shape=(_sds((T, D), BF16), _sds((D, D), BF16)), scratch_shapes=[pltpu.VMEM((D, D), F32)],
        compiler_params=_cp(("arbitrary",)))(dh1_b, w_o, mix)


FF_B = D_FF // NDEV


def _loss_rows(xv, gv, tv, row0):
    r_io = lax.broadcasted_iota(jnp.int32, (xv.shape[0], 1), 0) + row0
    valid = (r_io >= NM) & (r_io < L)
    r = lax.rsqrt(jnp.mean(xv * xv, axis=-1, keepdims=True) + EPS)
    xh = xv * r
    err = jnp.where(valid, xh * gv - tv, 0.0)
    lpart = 0.5 * jnp.sum(jnp.sum(err * err, axis=-1, keepdims=True) * (1.0 / D), axis=0, keepdims=True)
    dy = err * (1.0 / D)
    dxh = dy * gv
    dh = r * (dxh - xh * jnp.mean(dxh * xh, axis=-1, keepdims=True))
    return lpart, dh, jnp.sum(dy * xh, axis=0, keepdims=True)


def _mlp_fwd_loss(m, wup_g, wdown_g, h1, g_final, tgt):
    nsub = TM_MM // TM_E

    def body(m_ref, wu_ref, wd_ref, h1_ref, g_ref, t_ref, loss_ref, dh_ref, dhb_ref, dg_ref, h2):
        i, j = pl.program_id(0), pl.program_id(1)
        up = jnp.maximum(_dot(m_ref[...], wu_ref[0]), 0.0)
        part = _dot((up * up).astype(BF16), wd_ref[0])

        @pl.when(j == 0)
        def _():
            h2[...] = h1_ref[...] + part

        @pl.when(j > 0)
        def _():
            h2[...] += part

        @pl.when(j == NDEV - 1)
        def _():
            lsum = jnp.zeros((1, 1), F32)
            gsum = jnp.zeros((1, D), F32)
            for s in range(nsub):
                rows = slice(s * TM_E, (s + 1) * TM_E)
                lpart, dh, gpart = _loss_rows(h2[rows, :], g_ref[...], t_ref[rows, :], i * TM_MM + s * TM_E)
                dh_ref[rows, :] = dh
                dhb_ref[rows, :] = dh.astype(BF16)
                lsum = lsum + lpart
                gsum = gsum + gpart
            lsum = jnp.broadcast_to(lsum, (1, 128))

            @pl.when(i == 0)
            def _():
                loss_ref[...] = lsum
                dg_ref[...] = gsum

            @pl.when(i > 0)
            def _():
                loss_ref[...] += lsum
                dg_ref[...] += gsum

    blk = pl.BlockSpec((TM_MM, D), lambda i, j: (i, 0))
    vec = pl.BlockSpec((1, D), lambda i, j: (0, 0))
    return pl.pallas_call(
        body, name="mlp_fwd_loss", grid=(T // TM_MM, NDEV),
        in_specs=[blk, pl.BlockSpec((1, D, FF_B), lambda i, j: (j, 0, 0)), pl.BlockSpec((1, FF_B, D), lambda i, j: (j, 0, 0)),
                  blk, vec, blk],
        out_specs=(pl.BlockSpec((1, 128), lambda i, j: (0, 0)), blk, blk, vec),
        out_shape=(_sds((1, 128), F32), _sds((T, D), F32), _sds((T, D), BF16), _sds((1, D), F32)),
        scratch_shapes=[pltpu.VMEM((TM_MM, D), F32)],
        compiler_params=_cp(("arbitrary", "arbitrary"), 56))(m, wup_g, wdown_g, h1, g_final, tgt)


def _mlp_bwd(m, dh2_b, wup_g, wdown_g, h1, g_mlp, dh2):
    ni = T // TM_B
    nsub = TM_B // TM_E

    def body(m_ref, dh_ref, wu_ref, wd_ref, h1_ref, g_ref, dres_ref, dwu_ref, dwd_ref, dh1_ref, dh1b_ref, dg_ref,
             dm_ref, acc_u, acc_d):
        j, i = pl.program_id(0), pl.program_id(1)
        rows = pl.ds(pl.multiple_of(i * TM_B, TM_B), TM_B)
        mv, dh = m_ref[...], dh_ref[...]
        r = jnp.maximum(_dot(mv, wu_ref[0]), 0.0)
        act = (r * r).astype(BF16)
        dact = _dot(dh, wd_ref[0], NT)
        dup = (dact * (2.0 * r)).astype(BF16)
        pd = _dot(act, dh, TN)
        pu = _dot(mv, dup, TN)
        dmv = _dot(dup, wu_ref[0], NT)

        @pl.when(i == 0)
        def _():
            acc_u[...] = pu
            acc_d[...] = pd

        @pl.when(i > 0)
        def _():
            acc_u[...] += pu
            acc_d[...] += pd

        @pl.when(i == ni - 1)
        def _():
            dwu_ref[0] = acc_u[...].astype(BF16)
            dwd_ref[0] = acc_d[...].astype(BF16)

        @pl.when(j == 0)
        def _():
            dm_ref[rows, :] = dmv

        @pl.when(j > 0)
        def _():
            dm_ref[rows, :] += dmv

        @pl.when(j == NDEV - 1)
        def _():
            gsum = jnp.zeros((1, D), F32)
            for s in range(nsub):
                sub = slice(s * TM_E, (s + 1) * TM_E)
                dm_rows = dm_ref[pl.ds(pl.multiple_of(i * TM_B + s * TM_E, TM_E), TM_E), :]
                dx, gpart = _norm_bwd_rows(h1_ref[sub, :], g_ref[...], dm_rows, dres_ref[sub, :])
                dh1_ref[sub, :] = dx
                dh1b_ref[sub, :] = dx.astype(BF16)
                gsum = gsum + gpart

            @pl.when(i == 0)
            def _():
                dg_ref[...] = gsum

            @pl.when(i > 0)
            def _():
                dg_ref[...] += gsum

    blk = pl.BlockSpec((TM_B, D), lambda j, i: (i, 0))
    late = pl.BlockSpec((TM_B, D), lambda j, i: (jnp.where(j == NDEV - 1, i, 0), 0))
    vec = pl.BlockSpec((1, D), lambda j, i: (0, 0))
    wus = pl.BlockSpec((1, D, FF_B), lambda j, i: (j, 0, 0))
    wds = pl.BlockSpec((1, FF_B, D), lambda j, i: (j, 0, 0))
    return pl.pallas_call(
        body, name="mlp_bwd", grid=(NDEV, ni), in_specs=[blk, blk, wus, wds, late, vec, late],
        out_specs=(wus, wds, late, late, vec),
        out_shape=(_sds((NDEV, D, FF_B), BF16), _sds((NDEV, FF_B, D), BF16), _sds((T, D), F32), _sds((T, D), BF16),
                   _sds((1, D), F32)),
        scratch_shapes=[pltpu.VMEM((T, D), F32), pltpu.VMEM((D, FF_B), F32), pltpu.VMEM((FF_B, D), F32)],
        compiler_params=_cp(("arbitrary", "arbitrary"), 56))(m, dh2_b, wup_g, wdown_g, h1, g_mlp, dh2)


def _adamw(parts, w, m, v, name):
    rr, cc = w.shape
    nslot = parts.shape[0]
    tr = rr
    for cand in (256, 128, 64):
        if rr % cand == 0 and rr > cand:
            tr = cand
            break
    c1 = 1.0 - ADAM_B1 ** ADAM_STEP
    c2 = 1.0 - ADAM_B2 ** ADAM_STEP

    def body(p_ref, w_ref, m_ref, v_ref, g_ref, d_ref, nm_ref, nv_ref):
        g = p_ref[0].astype(F32)
        for s in range(1, nslot):
            g = g + p_ref[s].astype(F32)
        mn = ADAM_B1 * m_ref[...] + (1.0 - ADAM_B1) * g
        vn = ADAM_B2 * v_ref[...] + (1.0 - ADAM_B2) * (g * g)
        g_ref[...] = g
        nm_ref[...] = mn
        nv_ref[...] = vn
        d_ref[...] = -ADAM_LR * ((mn / c1) / (jnp.sqrt(vn / c2) + ADAM_EPS) + ADAM_WD * w_ref[...])

    blk = pl.BlockSpec((tr, cc), lambda i: (i, 0))
    return pl.pallas_call(
        body, name=name, grid=(rr // tr,),
        in_specs=[pl.BlockSpec((nslot, tr, cc), lambda i: (0, i, 0)), blk, blk, blk],
        out_specs=(blk,) * 4, out_shape=(_sds((rr, cc), F32),) * 4,
        compiler_params=_cp(("parallel",)))(parts, w, m, v)


RPB_N = NA_HEADS * 15 * 31
RPB_PAD = 4096
OWN_ROWS = NM + 8


def _pad_rows(a, rows):
    return jnp.pad(a, ((0, rows - a.shape[0]),) + ((0, 0),) * (a.ndim - 1))


def _pack_owned(meta_blk, lb_blk):
    return jnp.concatenate([meta_blk, _pad_rows(lb_blk.reshape(2, 128), 8)], axis=0)


LOSS_ROW = 28


def _pack_replicated(n_mix, n_mlp, n_final, hg_gain, rpb, loss_row=None):
    flat = _pad_rows(rpb.reshape(RPB_N), RPB_PAD)
    gain8 = _pad_rows(hg_gain.reshape(4, 128), 8)
    if loss_row is not None:
        gain8 = gain8 + jnp.pad(loss_row, ((LOSS_ROW - 24, 31 - LOSS_ROW), (0, 0)))
    return jnp.concatenate([n_mix.reshape(8, 128), n_mlp.reshape(8, 128), n_final.reshape(8, 128), gain8,
                            flat.reshape(32, 128)], axis=0)


def _unpack_replicated(a):
    return (a[0:8].reshape(1, D), a[8:16].reshape(1, D), a[16:24].reshape(D), a[24:28].reshape(1, 512),
            a[32:64].reshape(RPB_PAD)[:RPB_N].reshape(1, NA_HEADS, 15, 31))


def kernel(x, meta_tokens, w_in, w_na_out, w_hg_out, w_o, w_up, w_down, norm_mix, norm_mlp, norm_final, hg_norm, na_rpb, hg_lb_logits, loss_target, m_meta_tokens, m_w_in, m_w_na_out, m_w_hg_out, m_w_o, m_w_up, m_w_down, m_norm_mix, m_norm_mlp, m_norm_final, m_hg_norm, m_na_rpb, m_hg_lb_logits, v_meta_tokens, v_w_in, v_w_na_out, v_w_hg_out, v_w_o, v_w_up, v_w_down, v_norm_mix, v_norm_mlp, v_norm_final, v_hg_norm, v_na_rpb, v_hg_lb_logits):
    owned = _pack_owned(meta_tokens, hg_lb_logits)
    first_masks = (ALL_PEERS, SAME_CORE_AND_SIBLING)
    first, tok = _exchange_start([owned, w_in[0].astype(BF16)], [False] * 2, "gather_first_start", first_masks)
    bias_tab = _na_bias_table(_tie(jnp.pad(na_rpb[0], ((0, 0), (0, 0), (0, 128 - 31))), tok, "tie_bias_table"))
    later = [w[0].astype(BF16) for w in (w_na_out, w_hg_out, w_o, w_up, w_down)]
    lead = jnp.zeros((NM, D), F32) + tok[0, 0]
    h0_rows = jnp.concatenate([lead, x[0], jnp.zeros((T - L, D), F32)], axis=0)
    tgt = jnp.concatenate([lead, loss_target[0], jnp.zeros((T - L, D), F32)], axis=0)
    (owned_g, _), first = _exchange_wait(first, [False] * 2, [h0_rows], "gather_small_wait", first_masks, which=(0,))
    meta_full = jnp.transpose(owned_g[:, 0:NM, :], (1, 0, 2)).reshape(NM, D)
    logits = jnp.transpose(owned_g[:, NM:NM + 2, :].reshape(NDEV, 2, 2, 64), (1, 2, 0, 3)).reshape(2, 2, 512)
    h0 = lax.dynamic_update_slice(h0_rows, meta_full, (0, 0))
    a, a_t = _norm_fwd_t(h0, norm_mix, "norm_mix_fwd")
    (_, win_l), _ = _exchange_wait(first, [False] * 2, [a, logits, tgt, bias_tab] + later, "gather_first_wait", first_masks,
                                   which=(1,))
    (win_g,) = _forward_to_sibling([win_l], "gather_first_forward")
    later[0] = _tie(later[0], win_g, "tie_gather_rest")
    gather_rest, tok = _exchange_start(later, [False] * 5, "gather_rest_start")
    win_g = _tie(win_g, tok, "tie_inproj")

    p_act = _inproj_fwd(a, win_g)
    o_na, lse = _na_fwd(p_act, bias_tab)
    qh, k_f, b_f, k_b, b_b = _hg_pre(p_act, logits)
    o_f, st_f = _hg_scan_fwd(qh, k_f, b_f, p_act, False)
    o_b, st_b = _hg_scan_fwd(qh, k_b, b_b, p_act, True)
    (wna_g, whg_g, wo_g, _, _), gather_rest = _exchange_wait(
        gather_rest, [False] * 5, [o_f, o_b, o_na], "gather_rest_wait_a", which=(0, 1, 2))
    w_na_full = jnp.transpose(wna_g, (1, 0, 2)).reshape(512, D)
    w_hg_full = jnp.transpose(whg_g, (1, 0, 2)).reshape(512, D)
    mix, u_hg = _mix_fwd(o_na, o_f, o_b, hg_norm, w_na_full, w_hg_full, p_act)
    h1, m_act = _wo_fwd(mix, wo_g.reshape(D, D), h0, norm_mlp)
    (_, _, wo_g, wup_g, wdown_g), _ = _exchange_wait(gather_rest, [False] * 5, [m_act], "gather_rest_wait_b", which=(3, 4))
    w_o_full = wo_g.reshape(D, D)
    loss_part, dh2, dh2_b, d_nfinal = _mlp_fwd_loss(m_act, wup_g, wdown_g, h1, norm_final.reshape(1, D), tgt)

    dwup_p, dwdown_p, dh1, dh1_b, d_nmlp = _mlp_bwd(m_act, dh2_b, wup_g, wdown_g, h1, norm_mlp, dh2)
    sc_mlp, tok = _exchange_start([dwup_p, dwdown_p], [True] * 2, "scatter_mlp_start")
    dmix, dwo = _wo_bwd(_tie(dh1_b, tok, "tie_wo_bwd"), w_o_full, mix)
    sc_wo, tok = _exchange_start([dwo.reshape(NDEV, D // NDEV, D)], [True], "scatter_wo_start")
    dp_rest, dwna, dwhg, do_na, do_hg, d_gain = _mix_bwd(
        o_na, u_hg, o_f, o_b, hg_norm, w_na_full, w_hg_full, p_act, _tie(dmix, tok, "tie_mix_bwd"))
    owner_cols = lambda w: jnp.transpose(w.reshape(512, NDEV, D // NDEV), (1, 0, 2))
    sc_br, tok = _exchange_start([owner_cols(dwna), owner_cols(dwhg)], [True] * 2, "scatter_branch_start")
    do_hg = _tie(do_hg, tok, "tie_hg_scan_bwd")
    dq_f, dk_f, db_f, dv_f = _hg_scan_bwd(qh, k_f, b_f, p_act, st_f, do_hg, False)
    dq_b, dk_b, db_b, dv_b = _hg_scan_bwd(qh, k_b, b_b, p_act, st_b, do_hg, True)
    dp_rest, d_logits = _hg_pre_bwd(p_act, logits, dq_f, dq_b, dk_f, dk_b, db_f, db_b, dv_f, dv_b, dp_rest)
    dp_na, dbias = _na_bwd(p_act, do_na, lse, bias_tab)
    dwin_p = _inproj_bwd_dw(a_t, dp_na, dp_rest)
    far_mine, far_other = _far_slots()
    dwin_p = _add_into_slot(dwin_p, _sibling_swap_far(dwin_p, "pair_swap_in"), far_mine, "pair_add_in")
    sc_in, tok = _exchange_start([dwin_p], [True], "scatter_in_start", ALL_BUT_FAR_OTHER_CORE, absent=far_other)
    dh0, d_nmix, grad_x = _inproj_bwd_da(_tie(dp_na, tok, "tie_inproj_bwd_da"), dp_rest, win_g, h0, norm_mix, dh1)
    d_rpb = _na_rpb_reduce(_tie(dbias, tok, "tie_rpb_reduce"))[:, :, :31]

    res = {}

    def update(nm, parts, w, mm, vv):
        res[nm] = [r[None] for r in _adamw(parts, w[0], mm[0], vv[0], "adamw_" + nm)]
        return res[nm][1]

    wup_r, wdown_r = _exchange_wait(sc_mlp, [True] * 2, [dh0, d_rpb], "scatter_mlp_wait")
    update("w_up", wup_r, w_up, m_w_up, v_w_up)
    last = update("w_down", wdown_r, w_down, m_w_down, v_w_down)
    (wo_r,) = _exchange_wait(sc_wo, [True], [last], "scatter_wo_wait")
    last = update("w_o", wo_r, w_o, m_w_o, v_w_o)
    wna_r, whg_r = _exchange_wait(sc_br, [True] * 2, [last], "scatter_branch_wait")
    update("w_na_out", wna_r, w_na_out, m_w_na_out, v_w_na_out)
    last = update("w_hg_out", whg_r, w_hg_out, m_w_hg_out, v_w_hg_out)

    d_meta = jnp.transpose(dh0[0:NM].reshape(NM, NDEV, 128), (1, 0, 2))
    d_lg = jnp.transpose(d_logits.reshape(2, 2, NDEV, 64), (2, 0, 1, 3)).reshape(NDEV, 2, 128)
    owned_p = jnp.concatenate([d_meta, jnp.pad(d_lg, ((0, 0), (0, OWN_ROWS - NM - 2), (0, 0)))], axis=1)
    repl_p = _pack_replicated(d_nmix, d_nmlp, d_nfinal, d_gain, d_rpb, loss_part)
    grad_x = grad_x[None]
    done_first = [grad_x] + [res[nm][0] for nm in ("w_up", "w_down", "w_o", "w_na_out", "w_hg_out")]
    owned_r, repl_r = _exchange([owned_p, repl_p], [True, False], "scatter_small", done_first)
    own = _adamw(owned_r, owned, _pack_owned(m_meta_tokens, m_hg_lb_logits), _pack_owned(v_meta_tokens, v_hg_lb_logits),
                 "adamw_owned_small")
    res["meta_tokens"] = [r[0:NM] for r in own]
    res["hg_lb_logits"] = [r[NM:NM + 2].reshape(2, 2, 64) for r in own]
    rep = _adamw(repl_r, _pack_replicated(norm_mix, norm_mlp, norm_final, hg_norm, na_rpb),
                 _pack_replicated(m_norm_mix, m_norm_mlp, m_norm_final, m_hg_norm, m_na_rpb),
                 _pack_replicated(v_norm_mix, v_norm_mlp, v_norm_final, v_hg_norm, v_na_rpb), "adamw_replicated")
    for q in range(4):
        um = _unpack_replicated(rep[q])
        for nm, val in zip(("norm_mix", "norm_mlp", "norm_final", "hg_norm", "na_rpb"), um):
            res.setdefault(nm, [None] * 4)[q] = val
    (win_r,) = _exchange_wait(sc_in, [True], [rep[1], own[1]], "scatter_in_wait", ALL_BUT_FAR_OTHER_CORE)
    update("w_in", win_r, w_in, m_w_in, v_w_in)

    loss = jnp.sum(repl_r[:, LOSS_ROW, 0])
    order = ("meta_tokens", "w_in", "w_na_out", "w_hg_out", "w_o", "w_up", "w_down", "norm_mix", "norm_mlp", "norm_final",
             "hg_norm", "na_rpb", "hg_lb_logits")
    outs = [loss, grad_x]
    for q in range(4):
        outs += [res[nm][q] for nm in order]
    return tuple(outs)
```

```python
import functools

import numpy as np
import jax
import jax.numpy as jnp
from jax import lax
from jax.experimental import pallas as pl
from jax.experimental.pallas import tpu as pltpu

F32 = jnp.float32
BF16 = jnp.bfloat16

D = 1024
SEQ = 2048
NM = 16
L = SEQ + NM
T = 2176
NDEV = 8
EPS = 1e-6
GRID_W = 64
ROWS = SEQ // GRID_W
NA_HEADS = 8
NA_DH = 64
NA_SCALE = NA_DH ** -0.5
HG_HEADS = 4
HG_C = 16
NCHUNK = L // HG_C
D_FF = 4096
IN_COLS = 6144
NEG = -1e30

ADAM_LR = 0.001
ADAM_B1 = 0.9
ADAM_B2 = 0.999
ADAM_EPS = 1e-08
ADAM_WD = 0.01
ADAM_STEP = 10

MESH_ID = pl.DeviceIdType.MESH
ANY = pl.BlockSpec(memory_space=pl.ANY)

NN = (((1,), (0,)), ((), ()))
NT = (((1,), (1,)), ((), ()))
TN = (((0,), (0,)), ((), ()))


def _cp(sem=None, vmem_mb=48):
    return pltpu.CompilerParams(dimension_semantics=sem, vmem_limit_bytes=vmem_mb * 1024 * 1024)


def _dot(a, b, dims=NN):
    return lax.dot_general(a, b, dims, preferred_element_type=F32)


def _sds(shape, dtype):
    return jax.ShapeDtypeStruct(shape, dtype)


HBM = pl.BlockSpec(memory_space=pltpu.HBM)
SEM = pl.BlockSpec(memory_space=pltpu.SEMAPHORE)
EFFECT = pltpu.SideEffectType.DATAFLOW_SIDE_EFFECTING


def _exchange(arrs, scatter, name, after=()):
    n = len(arrs)
    after = list(after)
    out_shapes = []
    for a, sc in zip(arrs, scatter):
        out_shapes.append(_sds(a.shape if sc else (NDEV,) + a.shape, a.dtype))

    def body(*refs):
        ins, outs = refs[:n], refs[n + len(after):2 * n + len(after)]
        send_sems, recv_sems, loc_sems = refs[2 * n + len(after):]
        me = 4 * lax.axis_index("x") + 2 * lax.axis_index("y") + lax.axis_index("c")
        copies = []
        for k in range(n):
            src_me = ins[k].at[me] if scatter[k] else ins[k]
            loc = pltpu.make_async_copy(src_me, outs[k].at[me], loc_sems.at[k])
            loc.start()
            copies.append(loc)
        remote = sum(_peer_copies(ins, outs, scatter, send_sems, recv_sems), [])
        for cp in remote:
            cp.start()
        for cp in remote:
            cp.wait_recv()
        for cp in remote:
            cp.wait_send()
        for cp in copies:
            cp.wait()

    return pl.pallas_call(
        body, name=name, out_shape=tuple(out_shapes), in_specs=[ANY] * (n + len(after)), out_specs=tuple([ANY] * n),
        scratch_shapes=[pltpu.SemaphoreType.DMA((n * (NDEV - 1),)), pltpu.SemaphoreType.DMA((n * (NDEV - 1),)),
                        pltpu.SemaphoreType.DMA((n,))],
    )(*arrs, *after)


def _forward_to_sibling(bufs, name):
    n = len(bufs)

    def body(*refs):
        ins, outs = refs[:n], refs[n:2 * n]
        send_sems, recv_sems = refs[2 * n:]
        x, y, c = lax.axis_index("x"), lax.axis_index("y"), lax.axis_index("c")
        copies = []
        for k in range(n):
            for j, (cx, cy) in enumerate(((1 - x, y), (x, 1 - y), (1 - x, 1 - y))):
                slot = 4 * cx + 2 * cy + c
                copies.append(pltpu.make_async_remote_copy(
                    src_ref=ins[k].at[slot], dst_ref=outs[k].at[slot], send_sem=send_sems.at[3 * k + j],
                    recv_sem=recv_sems.at[3 * k + j], device_id=(x, y, 1 - c), device_id_type=MESH_ID))
        for cp in copies:
            cp.start()
        for cp in copies:
            cp.wait_recv()
        for cp in copies:
            cp.wait_send()

    return pl.pallas_call(
        body, name=name, out_shape=tuple(_sds(b.shape, b.dtype) for b in bufs), in_specs=[ANY] * n,
        out_specs=tuple([ANY] * n), input_output_aliases={k: k for k in range(n)},
        scratch_shapes=[pltpu.SemaphoreType.DMA((3 * n,)), pltpu.SemaphoreType.DMA((3 * n,))],
    )(*bufs)


ALL_PEERS = tuple(range(1, NDEV))
SAME_CORE_AND_SIBLING = (1, 2, 4, 6)
ALL_BUT_FAR_OTHER_CORE = (1, 2, 3, 4, 5, 6)


def _far_slots():
    far = 4 * (1 - lax.axis_index("x")) + 2 * (1 - lax.axis_index("y"))
    core = lax.axis_index("c")
    return far + core, far + 1 - core


def _sibling_swap_far(parts, name):
    def body(x_ref, o_ref, send_sem, recv_sem):
        sib = (lax.axis_index("x"), lax.axis_index("y"), 1 - lax.axis_index("c"))
        cp = pltpu.make_async_remote_copy(src_ref=x_ref.at[_far_slots()[1]], dst_ref=o_ref.at[0], send_sem=send_sem,
                                          recv_sem=recv_sem, device_id=sib, device_id_type=MESH_ID)
        cp.start()
        cp.wait()

    return pl.pallas_call(
        body, name=name, out_shape=_sds((1,) + parts.shape[1:], parts.dtype), in_specs=[ANY], out_specs=ANY,
        scratch_shapes=[pltpu.SemaphoreType.DMA(()), pltpu.SemaphoreType.DMA(())])(parts)


def _add_into_slot(parts, other, slot, name):
    _, rr, cc = parts.shape

    def body(slot_ref, p_ref, o_ref, out_ref):
        del slot_ref
        out_ref[...] = (p_ref[...].astype(F32) + o_ref[...].astype(F32)).astype(BF16)

    mine = pl.BlockSpec((1, rr // 2, cc), lambda j, s: (s[0], j, 0))
    grid_spec = pltpu.PrefetchScalarGridSpec(
        num_scalar_prefetch=1, grid=(2,),
        in_specs=[mine, pl.BlockSpec((1, rr // 2, cc), lambda j, s: (0, j, 0))], out_specs=mine)
    return pl.pallas_call(body, name=name, grid_spec=grid_spec, out_shape=_sds(parts.shape, BF16),
                          input_output_aliases={1: 0}, compiler_params=_cp(("arbitrary",)))(
                              jnp.reshape(slot, (1,)).astype(jnp.int32), parts, other)


def _peer_copies(srcs, lands, scatter, send_sems, recv_sems, masks=ALL_PEERS):
    x, y, c = lax.axis_index("x"), lax.axis_index("y"), lax.axis_index("c")
    me = 4 * x + 2 * y + c
    out = []
    for k in range(len(srcs)):
        out.append([])
        for m in (masks[k] if isinstance(masks[0], tuple) else masks):
            px, py, pc = x ^ (m >> 2), y ^ ((m >> 1) & 1), c ^ (m & 1)
            src = srcs[k].at[4 * px + 2 * py + pc] if scatter[k] else srcs[k]
            out[k].append(pltpu.make_async_remote_copy(
                src_ref=src, dst_ref=lands[k].at[me], send_sem=send_sems.at[k * (NDEV - 1) + m - 1],
                recv_sem=recv_sems.at[k * (NDEV - 1) + m - 1],
                device_id=(px, py, pc), device_id_type=MESH_ID))
    return out


def _own_copies(srcs, lands, scatter, own_sems):
    me = 4 * lax.axis_index("x") + 2 * lax.axis_index("y") + lax.axis_index("c")
    return [pltpu.make_async_copy(srcs[k].at[me] if scatter[k] else srcs[k], lands[k].at[me], own_sems.at[k])
            for k in range(len(srcs))]


def _exchange_start(arrs, scatter, name, masks=ALL_PEERS, absent=None):
    n = len(arrs)
    lands = []
    for a, sc in zip(arrs, scatter):
        land = lax.empty(a.shape if sc else (NDEV,) + a.shape, a.dtype)
        if absent is not None:
            land = lax.dynamic_update_index_in_dim(land, jnp.zeros((1,) + land.shape[1:], a.dtype), absent, 0)
        lands.append(land)

    def body(*refs):
        srcs, lnds = refs[:n], refs[n:2 * n]
        send_sems, recv_sems, own_sems = refs[2 * n:2 * n + 3]
        token = refs[-1]
        for cp in _own_copies(srcs, lnds, scatter, own_sems) + sum(_peer_copies(srcs, lnds, scatter, send_sems, recv_sems, masks), []):
            cp.start()
        token[...] = jnp.zeros_like(token)

    ops = [pltpu.with_memory_space_constraint(a, pltpu.HBM) for a in list(arrs) + lands]
    res = pl.pallas_call(
        body, name=name,
        out_shape=(pltpu.SemaphoreType.DMA((n * (NDEV - 1),)), pltpu.SemaphoreType.DMA((n * (NDEV - 1),)),
                   pltpu.SemaphoreType.DMA((n,)))
        + tuple(pltpu.HBM(o.shape, o.dtype) for o in ops) + (_sds((8, 128), F32),),
        in_specs=[HBM] * (2 * n), out_specs=(SEM, SEM, SEM) + (HBM,) * (2 * n) + (pl.BlockSpec(memory_space=pltpu.VMEM),),
        input_output_aliases={k: 3 + k for k in range(2 * n)},
        compiler_params=pltpu.CompilerParams(has_side_effects=EFFECT),
    )(*ops)
    return res[:-1], res[-1]


def _exchange_wait(handle, scatter, after, name, masks=ALL_PEERS, which=None):
    sems = handle[:3]
    bufs = handle[3:]
    n = len(bufs) // 2
    after = list(after)

    def body(*refs):
        srcs, lnds = refs[:n], refs[n:2 * n]
        copies = _peer_copies(srcs, lnds, scatter, refs[2 * n], refs[2 * n + 1], masks)
        own = _own_copies(srcs, lnds, scatter, refs[2 * n + 2])
        for k in (range(n) if which is None else which):
            own[k].wait()
            for cp in copies[k]:
                cp.wait_send()
                cp.wait_recv()

    res = pl.pallas_call(
        body, name=name, out_shape=tuple(pltpu.HBM(b.shape, b.dtype) for b in bufs),
        in_specs=[HBM] * (2 * n) + [SEM] * 3 + [ANY] * len(after), out_specs=(HBM,) * (2 * n),
        input_output_aliases={k: k for k in range(2 * n)},
        compiler_params=pltpu.CompilerParams(has_side_effects=EFFECT),
    )(*bufs, *sems, *after)
    return res[n:] if which is None else (res[n:], tuple(sems) + tuple(res))


def _tie(x, token, name):
    def body(x_ref, t_ref, o_ref):
        del x_ref, t_ref, o_ref

    return pl.pallas_call(body, name=name, out_shape=_sds(x.shape, x.dtype), in_specs=[ANY, ANY], out_specs=ANY,
                          input_output_aliases={0: 0})(x, token)


TM_E = 272


def _norm_fwd_t(h, g, name):
    def body(h_ref, g_ref, o_ref, ot_ref):
        xv = h_ref[...]
        r = lax.rsqrt(jnp.mean(xv * xv, axis=-1, keepdims=True) + EPS)
        y = xv * r * g_ref[...]
        o_ref[...] = y.astype(BF16)
        ot_ref[...] = y.T.astype(BF16)

    return pl.pallas_call(
        body, name=name, grid=(T // 128,),
        in_specs=[pl.BlockSpec((128, D), lambda i: (i, 0)), pl.BlockSpec((1, D), lambda i: (0, 0))],
        out_specs=(pl.BlockSpec((128, D), lambda i: (i, 0)), pl.BlockSpec((D, 128), lambda i: (0, i))),
        out_shape=(_sds((T, D), BF16), _sds((D, T), BF16)), compiler_params=_cp(("parallel",)))(h, g)


def _norm_bwd_rows(xv, gv, dnv, dres):
    r = lax.rsqrt(jnp.mean(xv * xv, axis=-1, keepdims=True) + EPS)
    xh = xv * r
    dxh = dnv * gv
    dx = dres + r * (dxh - xh * jnp.mean(dxh * xh, axis=-1, keepdims=True))
    return dx, jnp.sum(dnv * xh, axis=0, keepdims=True)


TM_MM = 1088


def _inproj_fwd(a, w_g):
    nb = w_g.shape[2]

    def body(a_ref, w_ref, o_ref):
        o_ref[...] = _dot(a_ref[...], w_ref[0])

    return pl.pallas_call(
        body, name="inproj_fwd", grid=(T // TM_MM, NDEV),
        in_specs=[pl.BlockSpec((TM_MM, D), lambda i, j: (i, 0)), pl.BlockSpec((1, D, nb), lambda i, j: (j, 0, 0))],
        out_specs=pl.BlockSpec((TM_MM, nb), lambda i, j: (i, j)), out_shape=_sds((T, NDEV * nb), F32),
        compiler_params=_cp(("parallel", "parallel")))(a, w_g)


TM_B = 544


W_IN_B = IN_COLS // NDEV


NA_BLKS = 1536 // W_IN_B


def _dp_specs(rows, row_index):
    return [pl.BlockSpec((rows, W_IN_B), lambda *g: (row_index(*g), jnp.minimum(g[-1], NA_BLKS - 1))),
            pl.BlockSpec((rows, W_IN_B), lambda *g: (row_index(*g), jnp.maximum(g[-1] - NA_BLKS, 0)))]


def _inproj_bwd_dw(a_t, dp_na, dp_rest):
    def body(at_ref, na_ref, rest_ref, dw_ref):
        j = pl.program_id(0)

        @pl.when(j < NA_BLKS)
        def _():
            dw_ref[0] = _dot(at_ref[...], na_ref[...]).astype(BF16)

        @pl.when(j >= NA_BLKS)
        def _():
            dw_ref[0] = _dot(at_ref[...], rest_ref[...]).astype(BF16)

    return pl.pallas_call(
        body, name="inproj_bwd_dw", grid=(NDEV,),
        in_specs=[pl.BlockSpec((D, T), lambda j: (0, 0))] + _dp_specs(T, lambda j: 0),
        out_specs=pl.BlockSpec((1, D, W_IN_B), lambda j: (j, 0, 0)), out_shape=_sds((NDEV, D, W_IN_B), BF16),
        compiler_params=_cp(("parallel",)))(a_t, dp_na, dp_rest)


def _inproj_bwd_da(dp_na, dp_rest, w_g, h0, g_mix, dh1):
    nsub = TM_MM // TM_E
    nblk = T // TM_MM

    def seq_copies(b, dh0_ref, gx_ref, sems):
        out = []
        for s in range(nsub):
            lo, hi = max(NM, b * TM_MM + s * TM_E), min(L, b * TM_MM + (s + 1) * TM_E)
            if hi > lo:
                out.append((s, pltpu.make_async_copy(dh0_ref.at[pl.ds(lo - b * TM_MM, hi - lo)],
                                                     gx_ref.at[pl.ds(lo - NM, hi - lo)], sems.at[b * nsub + s])))
        return out

    def body(na_ref, rest_ref, w_ref, h0_ref, g_ref, dres_ref, dh0_ref, dg_ref, gx_ref, da, sems):
        i, j = pl.program_id(0), pl.program_id(1)
        dpv = jnp.where(j < NA_BLKS, na_ref[...], rest_ref[...])
        dav = _dot(dpv, w_ref[0], NT)

        @pl.when(j == 0)
        def _():
            da[...] = dav

        @pl.when(j > 0)
        def _():
            da[...] += dav

        @pl.when(j == NDEV - 1)
        def _():
            gsum = jnp.zeros((1, D), F32)
            for s in range(nsub):
                sub = slice(s * TM_E, (s + 1) * TM_E)
                dx, gpart = _norm_bwd_rows(h0_ref[sub, :], g_ref[...], da[sub, :], dres_ref[sub, :])
                dh0_ref[sub, :] = dx
                gsum = gsum + gpart
                for b in range(nblk):
                    for _, cp in (c for c in seq_copies(b, dh0_ref, gx_ref, sems) if c[0] == s):
                        pl.when(i == b)(cp.start)

            @pl.when(i == 0)
            def _():
                dg_ref[...] = gsum

            @pl.when(i > 0)
            def _():
                dg_ref[...] += gsum

            for b in range(nblk):
                @pl.when(i == b)
                def _(b=b):
                    for _, cp in seq_copies(b, dh0_ref, gx_ref, sems):
                        cp.wait()

    rblk = pl.BlockSpec((TM_MM, D), lambda i, j: (i, 0))
    vec = pl.BlockSpec((1, D), lambda i, j: (0, 0))
    return pl.pallas_call(
        body, name="inproj_bwd_da", grid=(T // TM_MM, NDEV),
        in_specs=_dp_specs(TM_MM, lambda i, j: i) + [pl.BlockSpec((1, D, W_IN_B), lambda i, j: (j, 0, 0)), rblk, vec, rblk],
        out_specs=(rblk, vec, ANY), out_shape=(_sds((T, D), F32), _sds((1, D), F32), _sds((L - NM, D), F32)),
        scratch_shapes=[pltpu.VMEM((TM_MM, D), F32), pltpu.SemaphoreType.DMA((nblk * nsub,))],
        compiler_params=_cp(("arbitrary", "arbitrary"), 56))(dp_na, dp_rest, w_g, h0, g_mix, dh1)


NA_QB = 256
NA_GROUPS = ROWS // 4
NA_UROWS = 11
NA_KW = NA_UROWS * GRID_W
NA_KU = 768


def _na_row_offset(var, i, j):
    valid = (j < 8, i <= j < i + 8, 3 <= j < NA_UROWS)[var]
    return (j - i + (7, 3, 0)[var]) if valid else None


def _na_bias_table(rp):
    def body(r_ref, o_ref):
        row3 = lax.broadcasted_iota(jnp.int32, (15, GRID_W, 128), 1)
        lane3 = lax.broadcasted_iota(jnp.int32, (15, GRID_W, 128), 2)
        w3 = lane3 & (GRID_W - 1)
        cs3 = jnp.clip(row3 - 8, 0, GRID_W - 16)
        lane = lax.broadcasted_iota(jnp.int32, (GRID_W, 128), 1)
        neg = jnp.full((GRID_W, 128), NEG, F32)
        z = jnp.stack([jnp.broadcast_to(r_ref[0, a:a + 1, :], (GRID_W, 128)) for a in range(15)])
        for bit in range(6):
            sh = 1 << bit
            z = jnp.where((row3 & sh) != 0, jnp.roll(z, sh, axis=2), z)
        z = jnp.roll(z, 128 - 15, axis=2)
        z = jnp.where(lane3 < GRID_W, z, 0.0)
        z = z + jnp.roll(z, GRID_W, axis=2)
        tabs = jnp.where((w3 >= cs3) & (w3 < cs3 + 16), z, NEG)
        tail = jnp.where(lane < GRID_W + NM, 0.0, NEG)
        for var in range(3):
            for i in range(4):
                for jp in range(NA_KU // 128):
                    halves = []
                    for j in (2 * jp, 2 * jp + 1):
                        a = _na_row_offset(var, i, j) if j < NA_UROWS else None
                        halves.append(tail if j >= NA_UROWS else (neg if a is None else tabs[a]))
                    o_ref[var, 0, i * 64:(i + 1) * 64, jp * 128:(jp + 1) * 128] = jnp.where(lane < GRID_W, halves[0], halves[1])

    return pl.pallas_call(
        body, name="na_bias_table", grid=(NA_HEADS,),
        in_specs=[pl.BlockSpec((1, 15, 128), lambda h: (h, 0, 0))],
        out_specs=pl.BlockSpec((3, 1, NA_QB, NA_KU), lambda h: (0, h, 0, 0)),
        out_shape=_sds((3, NA_HEADS, NA_QB, NA_KU), F32), compiler_params=_cp(("parallel",)))(rp)


def _na_var(g):
    return jnp.where(g == 0, 0, jnp.where(g == NA_GROUPS - 1, 2, 1))


def _na_load_window(src_ref, dst, g):
    us = jnp.clip(4 * g - 4, 0, ROWS - NA_UROWS)
    kstart = pl.multiple_of(NM + GRID_W * us, 16)
    dst[0:NA_KW, :] = src_ref[pl.ds(kstart, NA_KW), :].astype(BF16)
    dst[NA_KW:NA_KW + NM, :] = src_ref[0:NM, :].astype(BF16)
    dst[NA_KW + NM:, :] = jnp.zeros((NA_KU - NA_KW - NM, 128), BF16)
    return kstart


def _na_fwd(p_act, bias_tab):
    def body(q_ref, k_ref, v_ref, b_ref, o_ref, lse_ref, ku, vu):
        g = pl.program_id(1)
        _na_load_window(k_ref, ku, g)
        _na_load_window(v_ref, vu, g)
        qstart = pl.multiple_of(NM + NA_QB * g, 16)
        q = q_ref[pl.ds(qstart, NA_QB), :]
        lane = lax.broadcasted_iota(jnp.int32, (NA_QB, 128), 1)
        o_h, lse_h = [], []
        for h in range(2):
            hm = (lane < 64) if h == 0 else (lane >= 64)
            qm = (jnp.where(hm, q, 0.0) * NA_SCALE).astype(BF16)
            s = _dot(qm, ku[...], NT) + b_ref[0, h]
            m = jnp.max(s, axis=-1, keepdims=True)
            p = jnp.exp(s - m)
            l = jnp.sum(p, axis=-1, keepdims=True)
            o_h.append(_dot(p.astype(BF16), vu[...]) / l)
            lse_h.append(jnp.broadcast_to(m + jnp.log(l), (NA_QB, 128)))
        o_ref[pl.ds(qstart, NA_QB), :] = jnp.where(lane < 64, o_h[0], o_h[1]).astype(BF16)
        lse_ref[0, pl.ds(qstart, NA_QB), :] = jnp.where(lane < 64, lse_h[0], lse_h[1])

        @pl.when(g == 0)
        def _():
            qm_ = q_ref[0:NM, :]
            lane_m = lax.broadcasted_iota(jnp.int32, (NM, 128), 1)
            km, vm = ku[NA_KW:NA_KW + NM, :], vu[NA_KW:NA_KW + NM, :]
            om = []
            for h in range(2):
                hm = (lane_m < 64) if h == 0 else (lane_m >= 64)
                s = _dot(jnp.where(hm, qm_, 0.0).astype(BF16), km, NT) * NA_SCALE
                p = jnp.exp(s - jnp.max(s, axis=-1, keepdims=True))
                l = jnp.sum(p, axis=-1, keepdims=True)
                om.append(_dot(p.astype(BF16), vm) / l)
            o_ref[0:NM, :] = jnp.where(lane_m < 64, om[0], om[1]).astype(BF16)
            o_ref[L:T, :] = jnp.zeros((T - L, 128), BF16)
            lse_ref[0, 0:NM, :] = jnp.zeros((NM, 128), F32)
            lse_ref[0, L:T, :] = jnp.zeros((T - L, 128), F32)

    col = lambda off: pl.BlockSpec((T, 128), lambda hp, g: (0, off + hp))
    return pl.pallas_call(
        body, name="na_fwd", grid=(4, NA_GROUPS),
        in_specs=[col(0), col(4), col(8),
                  pl.BlockSpec((1, 2, NA_QB, NA_KU), lambda hp, g: (_na_var(g), hp, 0, 0))],
        out_specs=(pl.BlockSpec((T, 128), lambda hp, g: (0, hp)), pl.BlockSpec((1, T, 128), lambda hp, g: (hp, 0, 0))),
        out_shape=(_sds((T, 512), BF16), _sds((4, T, 128), F32)),
        scratch_shapes=[pltpu.VMEM((NA_KU, 128), BF16), pltpu.VMEM((NA_KU, 128), BF16)],
        compiler_params=_cp(("parallel", "arbitrary")))(p_act, p_act, p_act, bias_tab)


def _na_bwd(p_act, do, lse, bias_tab):
    def body(q_ref, k_ref, v_ref, do_ref, lse_ref, b_ref, dp_ref, db_ref, ku, vu, dq_ref, dk_ref, dv_ref, stage, sems):
        g = pl.program_id(1)

        @pl.when(g == 0)
        def _():
            dq_ref[...] = jnp.zeros((T, 128), F32)
            dk_ref[...] = jnp.zeros((T, 128), F32)
            dv_ref[...] = jnp.zeros((T, 128), F32)

        kstart = _na_load_window(k_ref, ku, g)
        _na_load_window(v_ref, vu, g)
        qstart = pl.multiple_of(NM + NA_QB * g, 16)
        q = q_ref[pl.ds(qstart, NA_QB), :]
        dov = do_ref[pl.ds(qstart, NA_QB), :]
        lsev = lse_ref[0, pl.ds(qstart, NA_QB), :]
        lane = lax.broadcasted_iota(jnp.int32, (NA_QB, 128), 1)
        first = (g == 0) | (g == 1) | (g == NA_GROUPS - 1)
        dq_h = []
        dku = jnp.zeros((NA_KU, 128), F32)
        dvu = jnp.zeros((NA_KU, 128), F32)
        for h in range(2):
            hm = (lane < 64) if h == 0 else (lane >= 64)
            qm = (jnp.where(hm, q, 0.0) * NA_SCALE).astype(BF16)
            dom = jnp.where(hm, dov, 0.0).astype(BF16)
            s = _dot(qm, ku[...], NT) + b_ref[0, h]
            p = jnp.exp(s - lsev[:, 64 * h:64 * h + 1])
            dp = _dot(dom, vu[...], NT)
            delta = jnp.sum(p * dp, axis=-1, keepdims=True)
            ds = p * (dp - delta)

            @pl.when(first)
            def _():
                db_ref[0, h] = ds

            @pl.when(jnp.logical_not(first))
            def _():
                db_ref[0, h] += ds

            dsb = ds.astype(BF16)
            dq_h.append(_dot(dsb, ku[...]) * NA_SCALE)
            dku = dku + _dot(dsb, qm, TN)
            dvu = dvu + _dot(p.astype(BF16), dom, TN)
        dq_ref[pl.ds(qstart, NA_QB), :] = jnp.where(lane < 64, dq_h[0], dq_h[1])
        dk_ref[pl.ds(kstart, NA_KW), :] += dku[0:NA_KW]
        dv_ref[pl.ds(kstart, NA_KW), :] += dvu[0:NA_KW]
        dk_ref[0:NM, :] += dku[NA_KW:NA_KW + NM]
        dv_ref[0:NM, :] += dvu[NA_KW:NA_KW + NM]

        @pl.when(g == 0)
        def _():
            qm_ = q_ref[0:NM, :]
            dom_ = do_ref[0:NM, :]
            lane_m = lax.broadcasted_iota(jnp.int32, (NM, 128), 1)
            km, vm = ku[NA_KW:NA_KW + NM, :], vu[NA_KW:NA_KW + NM, :]
            dqs = []
            dkm = jnp.zeros((NM, 128), F32)
            dvm = jnp.zeros((NM, 128), F32)
            for h in range(2):
                hm = (lane_m < 64) if h == 0 else (lane_m >= 64)
                qh = jnp.where(hm, qm_, 0.0).astype(BF16)
                doh = jnp.where(hm, dom_, 0.0).astype(BF16)
                s = _dot(qh, km, NT) * NA_SCALE
                e = jnp.exp(s - jnp.max(s, axis=-1, keepdims=True))
                p = e / jnp.sum(e, axis=-1, keepdims=True)
                dp = _dot(doh, vm, NT)
                ds = p * (dp - jnp.sum(p * dp, axis=-1, keepdims=True))
                dsb = (ds * NA_SCALE).astype(BF16)
                dqs.append(_dot(dsb, km))
                dkm = dkm + _dot(dsb, qh, TN)
                dvm = dvm + _dot(p.astype(BF16), doh, TN)
            dq_ref[0:NM, :] = jnp.where(lane_m < 64, dqs[0], dqs[1])
            dk_ref[0:NM, :] += dkm
            dv_ref[0:NM, :] += dvm

        @pl.when(g == NA_GROUPS - 1)
        def _():
            copies = []
            for n, acc in enumerate((dq_ref, dk_ref, dv_ref)):
                stage[n] = acc[...].astype(BF16)
                cols = pl.ds(pl.multiple_of(512 * n + 128 * pl.program_id(0), 128), 128)
                copies.append(pltpu.make_async_copy(stage.at[n], dp_ref.at[pl.ds(0, T), cols], sems.at[n]))
                copies[-1].start()
            for cp in copies:
                cp.wait()

    col = lambda off: pl.BlockSpec((T, 128), lambda hp, g: (0, off + hp))
    ocol = pl.BlockSpec((T, 128), lambda hp, g: (0, hp))
    bspec = pl.BlockSpec((1, 2, NA_QB, NA_KU), lambda hp, g: (_na_var(g), hp, 0, 0))
    return pl.pallas_call(
        body, name="na_bwd", grid=(4, NA_GROUPS),
        in_specs=[col(0), col(4), col(8), ocol, pl.BlockSpec((1, T, 128), lambda hp, g: (hp, 0, 0)), bspec],
        out_specs=(ANY, bspec),
        out_shape=(_sds((T, 1536), BF16), _sds((3, NA_HEADS, NA_QB, NA_KU), F32)),
        scratch_shapes=[pltpu.VMEM((NA_KU, 128), BF16), pltpu.VMEM((NA_KU, 128), BF16)] + [pltpu.VMEM((T, 128), F32)] * 3
        + [pltpu.VMEM((3, T, 128), BF16), pltpu.SemaphoreType.DMA((3,))],
        compiler_params=_cp(("parallel", "arbitrary")))(p_act, p_act, p_act, do, lse, bias_tab)


def _na_rpb_reduce(dbias):
    def body(db_ref, o_ref):
        lane = lax.broadcasted_iota(jnp.int32, (GRID_W, 128), 1)
        row3 = lax.broadcasted_iota(jnp.int32, (15, GRID_W, 128), 1)
        lane3 = lax.broadcasted_iota(jnp.int32, (15, GRID_W, 128), 2)
        accs = []
        for a in range(15):
            acc = jnp.zeros((GRID_W, 128), F32)
            for var in range(3):
                for i in range(4):
                    for j in range(NA_UROWS):
                        if _na_row_offset(var, i, j) == a:
                            pair = db_ref[var, 0, i * 64:(i + 1) * 64, (j // 2) * 128:(j // 2 + 1) * 128]
                            acc = acc + jnp.where((lane < GRID_W) if j % 2 == 0 else (lane >= GRID_W), pair, 0.0)
            accs.append(acc)
        z = jnp.stack(accs)
        z = jnp.where(lane3 < GRID_W, z + jnp.roll(z, GRID_W, axis=2), 0.0)
        for bit in range(6):
            sh = 1 << bit
            z = jnp.where((row3 & sh) != 0, jnp.roll(z, 128 - sh, axis=2), z)
        z = jnp.roll(z, 15, axis=2)
        o_ref[0] = jnp.sum(z, axis=1)

    return pl.pallas_call(
        body, name="na_rpb_reduce", grid=(NA_HEADS,),
        in_specs=[pl.BlockSpec((3, 1, NA_QB, NA_KU), lambda h: (0, h, 0, 0))],
        out_specs=pl.BlockSpec((1, 15, 128), lambda h: (h, 0, 0)), out_shape=_sds((NA_HEADS, 15, 128), F32),
        compiler_params=_cp(("parallel",)))(dbias)


HG_RB = 128
HG_NB = T // HG_RB
HG_SLOTS = HG_NB * 8
HI = lax.Precision.HIGHEST
HG_UNROLL = 4
HG_UNROLL_WIDE = 8


def _chunk_tri(lower):
    r = lax.broadcasted_iota(jnp.int32, (HG_RB, HG_RB), 0)
    c = lax.broadcasted_iota(jnp.int32, (HG_RB, HG_RB), 1)
    same = (r // HG_C) == (c // HG_C)
    keep = (c <= r) if lower else (c >= r)
    return jnp.where(same & keep, 1.0, 0.0).astype(F32)


def _hg_gate_terms(z, lg):
    dl = lg[0:1, :] - lg[1:2, :]
    log_lb = jax.nn.log_sigmoid(dl)
    log_1mlb = jax.nn.log_sigmoid(-dl)
    yz = log_1mlb + jax.nn.log_sigmoid(z)
    log_f = jnp.logaddexp(log_lb, yz)
    snz = jax.nn.sigmoid(-z)
    k = jnp.exp(log_1mlb) * snz
    w2 = jnp.exp(yz - log_f)
    return log_f, k, snz, w2


def _hg_pre(p_act, logits):
    def body(q_ref, zf_ref, zb_ref, lg_ref, qh_ref, kf_ref, bf_ref, kb_ref, bb_ref):
        qh_ref[...] = jax.nn.silu(q_ref[...])
        lf, kf, _, _ = _hg_gate_terms(zf_ref[...], lg_ref[0])
        kf_ref[...] = kf
        bf_ref[...] = jnp.dot(_chunk_tri(True), lf, precision=HI, preferred_element_type=F32)
        lb_, kb, _, _ = _hg_gate_terms(zb_ref[...], lg_ref[1])
        kb_ref[...] = kb
        bb_ref[...] = jnp.dot(_chunk_tri(False), lb_, precision=HI, preferred_element_type=F32)

    blk = lambda c: pl.BlockSpec((HG_RB, 512), lambda i: (i, c))
    ob = pl.BlockSpec((HG_RB, 512), lambda i: (i, 0))
    return pl.pallas_call(
        body, name="hg_pre", grid=(HG_NB,),
        in_specs=[blk(3), blk(4), blk(5), pl.BlockSpec((2, 2, 512), lambda i: (0, 0, 0))],
        out_specs=(ob,) * 5, out_shape=(_sds((T, 512), F32),) * 5,
        compiler_params=_cp(("parallel",)))(p_act, p_act, p_act, logits)


def _bdot(a, b, ca, cb):
    return lax.dot_general(a.astype(BF16), b.astype(BF16), (((ca,), (cb,)), ((0,), (0,))), preferred_element_type=F32)


HG_S = 8
HG_NS = HG_RB // HG_S


def _lane_sums(xs):
    l_io = lax.broadcasted_iota(jnp.int32, (HG_NS, HG_S, HG_S), 2)
    a = jnp.zeros((HG_NS, HG_S, HG_S), F32)
    for j, x in enumerate(xs):
        a = a + jnp.where(l_io == j, jnp.sum(x, axis=-1, keepdims=True), 0.0)
    return a


def _halves(x):
    y = x.reshape(8, 2, HG_S, x.shape[-1])
    return y[:, 0], y[:, 1]


def _join(first, second):
    return jnp.stack([first, second], axis=1).reshape(HG_RB, first.shape[-1])


def _cross_split(rev, b4):
    b_1, b_2 = _halves(b4)
    if rev:
        r = b_2[:, 0:1, :]
        return jnp.exp(b_1 - r), jnp.exp(r - b_2)
    r = b_1[:, HG_S - 1:HG_S, :]
    return jnp.exp(b_2 - r), jnp.exp(r - b_1)


def _hg_scan_fwd(qh, k, b, p_act, rev):
    anchor = 0 if rev else HG_C - 1

    def body(q_ref, k_ref, b_ref, v_ref, o_ref, st_ref, dsc):
        def phase_a(blk, _):
            rows = pl.ds(pl.multiple_of(blk * HG_RB, HG_RB), HG_RB)
            b3 = b_ref[rows, :].reshape(8, HG_C, 128)
            k3 = k_ref[rows, :].reshape(8, HG_C, 128)
            v3 = v_ref[rows, :].reshape(8, HG_C, 128)
            bl = b3[:, anchor:anchor + 1, :]
            kt = k3 * jnp.exp(bl - b3)
            st_ref[0, pl.ds(pl.multiple_of(blk * 8, 8), 8)] = _bdot(v3, kt, 1, 1)
            dsc[pl.ds(pl.multiple_of(blk * 8, 8), 8), :] = jnp.exp(bl[:, 0, :])
            return 0

        lax.fori_loop(0, HG_NB, phase_a, 0, unroll=HG_UNROLL_WIDE)

        def phase_b(n, carry):
            c = (NCHUNK - 1 - n) if rev else n
            u = st_ref[0, c]
            st_ref[0, c] = carry
            return carry * dsc[pl.ds(c, 1), :] + u

        lax.fori_loop(0, NCHUNK // 3, lambda n3, s: phase_b(3 * n3 + 2, phase_b(3 * n3 + 1, phase_b(3 * n3, s))),
                      jnp.zeros((128, 128), F32))
        for c in range(NCHUNK, HG_SLOTS):
            st_ref[0, c] = jnp.zeros((128, 128), F32)

        t_io = lax.broadcasted_iota(jnp.int32, (HG_NS, HG_S, 128), 1)

        def phase_c(blk, _):
            rows = pl.ds(pl.multiple_of(blk * HG_RB, HG_RB), HG_RB)
            b4 = b_ref[rows, :].reshape(HG_NS, HG_S, 128)
            k4 = k_ref[rows, :].reshape(HG_NS, HG_S, 128)
            q4 = q_ref[rows, :].reshape(HG_NS, HG_S, 128)
            v4 = v_ref[rows, :].reshape(HG_NS, HG_S, 128)
            st = st_ref[0, pl.ds(pl.multiple_of(blk * 8, 8), 8)]
            o = _bdot((q4 * jnp.exp(b4)).reshape(8, HG_C, 128), st, 2, 2).reshape(HG_RB, 128)
            terms = []
            for s in range(HG_S):
                ok = (t_io <= s) if rev else (t_io >= s)
                f = jnp.exp(jnp.where(ok, b4 - b4[:, s:s + 1, :], NEG))
                terms.append(q4 * f * k4[:, s:s + 1, :])
            o_in = _bdot(_lane_sums(terms), v4, 2, 1)
            wq, wk = _cross_split(rev, b4)
            q_1, q_2 = _halves(q4)
            k_1, k_2 = _halves(k4)
            v_1, v_2 = _halves(v4)
            o_1, o_2 = _halves(o_in)
            if rev:
                o_1 = o_1 + _bdot(_bdot(q_1 * wq, k_2 * wk, 2, 2), v_2, 2, 1)
            else:
                o_2 = o_2 + _bdot(_bdot(q_2 * wq, k_1 * wk, 2, 2), v_1, 2, 1)
            o_ref[rows, :] = o + _join(o_1, o_2)
            return 0

        lax.fori_loop(0, HG_NB, phase_c, 0, unroll=HG_UNROLL_WIDE)

    col = pl.BlockSpec((T, 128), lambda h: (0, h))
    return pl.pallas_call(
        body, name="hg_scan_bwd_dir" if rev else "hg_scan_fwd_dir", grid=(HG_HEADS,),
        in_specs=[col, col, col, pl.BlockSpec((T, 128), lambda h: (0, 24 + h))],
        out_specs=(col, pl.BlockSpec((1, HG_SLOTS, 128, 128), lambda h: (h, 0, 0, 0))),
        out_shape=(_sds((T, 512), F32), _sds((HG_HEADS, HG_SLOTS, 128, 128), F32)),
        scratch_shapes=[pltpu.VMEM((HG_SLOTS, 128), F32)],
        compiler_params=_cp(("parallel",), 56))(qh, k, b, p_act)


def _hg_scan_bwd(qh, k, b, p_act, st, do, rev):
    anchor = 0 if rev else HG_C - 1

    def body(q_ref, k_ref, b_ref, v_ref, st_ref, do_ref, dq_ref, dk_ref, db_ref, dv_ref, gst, dsc, dbl):
        def phase_a(blk, _):
            rows = pl.ds(pl.multiple_of(blk * HG_RB, HG_RB), HG_RB)
            b3 = b_ref[rows, :].reshape(8, HG_C, 128)
            q3 = q_ref[rows, :].reshape(8, HG_C, 128)
            do3 = do_ref[rows, :].reshape(8, HG_C, 128)
            gst[pl.ds(pl.multiple_of(blk * 8, 8), 8)] = _bdot(do3, q3 * jnp.exp(b3), 1, 1)
            dsc[pl.ds(pl.multiple_of(blk * 8, 8), 8), :] = jnp.exp(b3[:, anchor, :])
            return 0

        lax.fori_loop(0, HG_NB, phase_a, 0, unroll=HG_UNROLL_WIDE)

        def phase_b(n, carry):
            c = n if rev else (NCHUNK - 1 - n)
            w = gst[c]
            gst[c] = carry
            dcv = dsc[pl.ds(c, 1), :]
            dbl[pl.ds(c, 1), :] = dcv * jnp.sum(st_ref[0, c] * carry, axis=0, keepdims=True)
            return carry * dcv + w

        lax.fori_loop(0, NCHUNK // 3, lambda n3, s: phase_b(3 * n3 + 2, phase_b(3 * n3 + 1, phase_b(3 * n3, s))),
                      jnp.zeros((128, 128), F32))
        for c in range(NCHUNK, HG_SLOTS):
            gst[c] = jnp.zeros((128, 128), F32)
            dbl[c:c + 1, :] = jnp.zeros((1, 128), F32)

        t_io = lax.broadcasted_iota(jnp.int32, (HG_NS, HG_S, 128), 1)
        t16 = lax.broadcasted_iota(jnp.int32, (8, HG_C, 128), 1)
        r_io = lax.broadcasted_iota(jnp.int32, (HG_NS, HG_S, HG_S), 1)
        l_io = lax.broadcasted_iota(jnp.int32, (HG_NS, HG_S, HG_S), 2)

        def phase_c(blk, _):
            rows = pl.ds(pl.multiple_of(blk * HG_RB, HG_RB), HG_RB)
            cs = pl.ds(pl.multiple_of(blk * 8, 8), 8)
            b4 = b_ref[rows, :].reshape(HG_NS, HG_S, 128)
            k4 = k_ref[rows, :].reshape(HG_NS, HG_S, 128)
            q4 = q_ref[rows, :].reshape(HG_NS, HG_S, 128)
            v4 = v_ref[rows, :].reshape(HG_NS, HG_S, 128)
            do4 = do_ref[rows, :].reshape(HG_NS, HG_S, 128)
            b3, k3, q3 = (z.reshape(8, HG_C, 128) for z in (b4, k4, q4))
            v3, do3 = v4.reshape(8, HG_C, 128), do4.reshape(8, HG_C, 128)
            s_t = st_ref[0, cs]
            g_t = gst[cs]
            bl = b3[:, anchor:anchor + 1, :]
            ekl = jnp.exp(bl - b3)
            kt = k3 * ekl
            dkt = _bdot(v3, g_t, 2, 1)
            dq = (_bdot(do3, s_t, 2, 1) * jnp.exp(b3)).reshape(HG_NS, HG_S, 128)
            dk = (dkt * ekl).reshape(HG_NS, HG_S, 128)
            dv = _bdot(kt, g_t, 2, 2).reshape(HG_NS, HG_S, 128)
            dbl3 = dbl[cs, :].reshape(8, 1, 128) + jnp.sum(dkt * kt, axis=1, keepdims=True)
            causal = (l_io >= r_io) if rev else (l_io <= r_io)
            da = jnp.where(causal, _bdot(do4, v4, 2, 2), 0.0)
            causal_t = (l_io <= r_io) if rev else (l_io >= r_io)
            dat = jnp.where(causal_t, _bdot(v4, do4, 2, 2), 0.0)
            for s in range(HG_S):
                ok = (t_io <= s) if rev else (t_io >= s)
                f = jnp.exp(jnp.where(ok, b4 - b4[:, s:s + 1, :], NEG))
                dq = dq + da[:, :, s:s + 1] * (f * k4[:, s:s + 1, :])
            terms = []
            for t in range(HG_S):
                ok = (t_io >= t) if rev else (t_io <= t)
                e = jnp.exp(jnp.where(ok, b4[:, t:t + 1, :] - b4, NEG))
                eq = e * q4[:, t:t + 1, :]
                dk = dk + dat[:, :, t:t + 1] * eq
                terms.append(eq * k4)
            dv = dv + _bdot(_lane_sums(terms), do4, 2, 1)
            wq, wk = _cross_split(rev, b4)
            pick = (lambda z: _halves(z)) if rev else (lambda z: _halves(z)[::-1])
            (q_q, _), (_, k_k), (_, v_k), (do_q, _) = pick(q4), pick(k4), pick(v4), pick(do4)
            qx, kx = q_q * wq, k_k * wk
            dq_q = _bdot(_bdot(do_q, v_k, 2, 2), kx, 2, 1) * wq
            dk_k = _bdot(_bdot(v_k, do_q, 2, 2), qx, 2, 1) * wk
            dv_k = _bdot(_bdot(kx, qx, 2, 2), do_q, 2, 1)
            zero = jnp.zeros((8, HG_S, 128), F32)
            place_q = (lambda z: _join(z, zero)) if rev else (lambda z: _join(zero, z))
            place_k = (lambda z: _join(zero, z)) if rev else (lambda z: _join(z, zero))
            dq2 = dq.reshape(HG_RB, 128) + place_q(dq_q)
            dk2 = dk.reshape(HG_RB, 128) + place_k(dk_k)
            dv2 = dv.reshape(HG_RB, 128) + place_k(dv_k)
            dq3, dk3 = dq2.reshape(8, HG_C, 128), dk2.reshape(8, HG_C, 128)
            db = q3 * dq3 - k3 * dk3 + jnp.where(t16 == anchor, dbl3, 0.0)
            dq_ref[rows, :] = dq2
            dk_ref[rows, :] = dk2
            db_ref[rows, :] = db.reshape(HG_RB, 128)
            dv_ref[rows, :] = dv2
            return 0

        lax.fori_loop(0, HG_NB, phase_c, 0, unroll=HG_UNROLL)

    col = pl.BlockSpec((T, 128), lambda h: (0, h))
    return pl.pallas_call(
        body, name="hg_scan_bwd_dir_bwd" if rev else "hg_scan_fwd_dir_bwd", grid=(HG_HEADS,),
        in_specs=[col, col, col, pl.BlockSpec((T, 128), lambda h: (0, 24 + h)),
                  pl.BlockSpec((1, HG_SLOTS, 128, 128), lambda h: (h, 0, 0, 0)), col],
        out_specs=(col,) * 4, out_shape=(_sds((T, 512), F32),) * 4,
        scratch_shapes=[pltpu.VMEM((HG_SLOTS, 128, 128), F32), pltpu.VMEM((HG_SLOTS, 128), F32),
                        pltpu.VMEM((HG_SLOTS, 128), F32)],
        compiler_params=_cp(("parallel",), 56))(qh, k, b, p_act, st, do)


def _row_valid(i, tm):
    r = lax.broadcasted_iota(jnp.int32, (tm, 1), 0) + i * tm
    return r < L


def _hg_post_rows(o, gv, gain_v, valid):
    parts = []
    for h in range(HG_HEADS):
        oh = o[:, 128 * h:128 * (h + 1)]
        parts.append(oh * lax.rsqrt(jnp.mean(oh * oh, axis=-1, keepdims=True) + EPS))
    return jnp.where(valid, jnp.concatenate(parts, axis=1) * gain_v * jax.nn.silu(gv), 0.0)


def _hg_post_bwd_rows(du, o, gv, gain_v, valid):
    duv = jnp.where(valid, du, 0.0)
    sig = jax.nn.sigmoid(gv)
    sg = gv * sig
    dn = duv * gain_v * sg
    do_parts, n_parts = [], []
    for h in range(HG_HEADS):
        sl = slice(128 * h, 128 * (h + 1))
        oh = o[:, sl]
        r = lax.rsqrt(jnp.mean(oh * oh, axis=-1, keepdims=True) + EPS)
        nh = oh * r
        dnh = dn[:, sl]
        do_parts.append(r * (dnh - nh * jnp.mean(dnh * nh, axis=-1, keepdims=True)))
        n_parts.append(nh)
    n = jnp.where(valid, jnp.concatenate(n_parts, axis=1), 0.0)
    do = jnp.where(valid, jnp.concatenate(do_parts, axis=1), 0.0)
    dg = duv * n * gain_v * (sig * (1.0 + gv * (1.0 - sig)))
    return do, dg, jnp.sum(duv * n * sg, axis=0, keepdims=True)


def _hg_pre_bwd(p_act, logits, dq_f, dq_b, dk_f, dk_b, db_f, db_b, dv_f, dv_b, dp_rest):
    def body(q_ref, zf_ref, zb_ref, lg_ref, dqf_ref, dqb_ref, dkf_ref, dkb_ref, dbf_ref, dbb_ref, dvf_ref, dvb_ref, _,
             dp_ref, dlg_ref):
        dq_ref, dzf_ref, dzb_ref, di_ref = (dp_ref.at[:, 512 * c:512 * (c + 1)] for c in range(4))
        i = pl.program_id(0)
        valid = _row_valid(i, HG_RB)
        qv = q_ref[...]
        sig = jax.nn.sigmoid(qv)
        dq_ref[...] = jnp.where(valid, (dqf_ref[...] + dqb_ref[...]) * (sig * (1.0 + qv * (1.0 - sig))), 0.0).astype(BF16)
        di_ref[...] = jnp.where(valid, dvf_ref[...] + dvb_ref[...], 0.0).astype(BF16)
        for d, (z_ref, dk_r, db_r, dz_ref) in enumerate(((zf_ref, dkf_ref, dbf_ref, dzf_ref), (zb_ref, dkb_ref, dbb_ref, dzb_ref))):
            lg = lg_ref[d]
            dl = lg[0:1, :] - lg[1:2, :]
            lb = jax.nn.sigmoid(dl)
            one_m_lb = jax.nn.sigmoid(-dl)
            log_f, _, snz, w2 = _hg_gate_terms(z_ref[...], lg)
            dbv = jnp.where(valid, db_r[...], 0.0)
            dkv = jnp.where(valid, dk_r[...], 0.0)
            dlf = jnp.dot(_chunk_tri(d == 1), dbv, precision=HI, preferred_element_type=F32)
            sz = 1.0 - snz
            dz_ref[...] = (dlf * w2 * snz - dkv * one_m_lb * sz * snz).astype(BF16)
            dlb = jnp.sum(dlf * snz * jnp.exp(-log_f) - dkv * snz, axis=0, keepdims=True)
            dl0 = dlb * lb * one_m_lb
            part = jnp.concatenate([dl0, -dl0], axis=0)

            @pl.when(i == 0)
            def _():
                dlg_ref[d] = part

            @pl.when(i > 0)
            def _():
                dlg_ref[d] += part

    blk = lambda c: pl.BlockSpec((HG_RB, 512), lambda i: (i, c))
    ob = pl.BlockSpec((HG_RB, 512), lambda i: (i, 0))
    lgs = pl.BlockSpec((2, 2, 512), lambda i: (0, 0, 0))
    return pl.pallas_call(
        body, name="hg_pre_bwd", grid=(HG_NB,),
        in_specs=[blk(3), blk(4), blk(5), lgs] + [ob] * 8 + [ANY],
        out_specs=(pl.BlockSpec((HG_RB, 2048), lambda i: (i, 0)), lgs),
        out_shape=(_sds(dp_rest.shape, BF16), _sds((2, 2, 512), F32)), input_output_aliases={12: 0},
        compiler_params=_cp(("arbitrary",)))(p_act, p_act, p_act, logits, dq_f, dq_b, dk_f, dk_b, db_f, db_b, dv_f, dv_b,
                                             dp_rest)


def _mix_fwd(o_na, o_f, o_b, gain, w_na, w_hg, p_act):
    def body(ona_ref, of_ref, ob_ref, g_ref, gain_ref, wna_ref, whg_ref, gna_ref, ghg_ref, o_ref, u_ref):
        u = _hg_post_rows(of_ref[...] + ob_ref[...], g_ref[...], gain_ref[...], _row_valid(pl.program_id(0), TM_B)).astype(BF16)
        u_ref[...] = u
        y_na = _dot(ona_ref[...], wna_ref[...])
        y_hg = _dot(u, whg_ref[...])
        o_ref[...] = (jax.nn.sigmoid(gna_ref[...]) * y_na + jax.nn.sigmoid(ghg_ref[...]) * y_hg).astype(BF16)

    act = pl.BlockSpec((TM_B, 512), lambda i: (i, 0))
    wsp = pl.BlockSpec((512, D), lambda i: (0, 0))
    return pl.pallas_call(
        body, name="mix_fwd", grid=(T // TM_B,),
        in_specs=[act, act, act, pl.BlockSpec((TM_B, 512), lambda i: (i, 7)), pl.BlockSpec((1, 512), lambda i: (0, 0)),
                  wsp, wsp, pl.BlockSpec((TM_B, D), lambda i: (i, 4)), pl.BlockSpec((TM_B, D), lambda i: (i, 5))],
        out_specs=(pl.BlockSpec((TM_B, D), lambda i: (i, 0)), act), out_shape=(_sds((T, D), BF16), _sds((T, 512), BF16)),
        compiler_params=_cp(("parallel",)))(o_na, o_f, o_b, p_act, gain, w_na, w_hg, p_act, p_act)


DP_REST = IN_COLS - 1536


def _mix_bwd(o_na, u_hg, o_f, o_b, gain, w_na, w_hg, p_act, dmix):
    ni = T // TM_B

    def body(ona_ref, uhg_ref, of_ref, ob_ref, g_ref, gain_ref, wna_ref, whg_ref, gna_ref, ghg_ref, dmix_ref,
             dp_ref, dwna_ref, dwhg_ref, dona_ref, do_ref, dgain_ref, acc_na, acc_hg):
        i = pl.program_id(0)
        dg_ref, dgna_ref, dghg_ref = dp_ref.at[:, 2048:2560], dp_ref.at[:, 2560:3584], dp_ref.at[:, 3584:4608]
        dm = dmix_ref[...].astype(F32)
        dxs = []
        for x_ref, w_ref, gt_ref, dgt_ref, dw_ref, acc in (
                (ona_ref, wna_ref, gna_ref, dgna_ref, dwna_ref, acc_na), (uhg_ref, whg_ref, ghg_ref, dghg_ref, dwhg_ref, acc_hg)):
            xv = x_ref[...]
            y = _dot(xv, w_ref[...])
            sg = jax.nn.sigmoid(gt_ref[...])
            dgt_ref[...] = (dm * y * sg * (1.0 - sg)).astype(BF16)
            dy = (dm * sg).astype(BF16)
            dxs.append(_dot(dy, w_ref[...], NT))
            part = _dot(xv, dy, TN)

            @pl.when(i == 0)
            def _():
                acc[...] = part

            @pl.when(i > 0)
            def _():
                acc[...] += part

            @pl.when(i == ni - 1)
            def _():
                for o in range(NDEV):
                    dw_ref[o] = acc[:, (D // NDEV) * o:(D // NDEV) * (o + 1)].astype(BF16)

        dona_ref[...] = dxs[0]
        do, dg, gpart = _hg_post_bwd_rows(dxs[1], of_ref[...] + ob_ref[...], g_ref[...], gain_ref[...], _row_valid(i, TM_B))
        do_ref[...] = do
        dg_ref[...] = dg.astype(BF16)

        @pl.when(i == 0)
        def _():
            dgain_ref[...] = gpart

        @pl.when(i > 0)
        def _():
            dgain_ref[...] += gpart

    act = pl.BlockSpec((TM_B, 512), lambda i: (i, 0))
    wsp = pl.BlockSpec((512, D), lambda i: (0, 0))
    dwsp = pl.BlockSpec((NDEV, 512, D // NDEV), lambda i: (0, 0, 0))
    rblk = pl.BlockSpec((TM_B, D), lambda i: (i, 0))
    vec = pl.BlockSpec((1, 512), lambda i: (0, 0))
    return pl.pallas_call(
        body, name="mix_bwd", grid=(ni,),
        in_specs=[act, act, act, act, pl.BlockSpec((TM_B, 512), lambda i: (i, 7)), vec, wsp, wsp,
                  pl.BlockSpec((TM_B, D), lambda i: (i, 4)), pl.BlockSpec((TM_B, D), lambda i: (i, 5)), rblk],
        out_specs=(pl.BlockSpec((TM_B, DP_REST), lambda i: (i, 0)), dwsp, dwsp, act, act, vec),
        out_shape=(_sds((T, DP_REST), BF16), _sds((NDEV, 512, D // NDEV), BF16), _sds((NDEV, 512, D // NDEV), BF16),
                   _sds((T, 512), F32), _sds((T, 512), F32), _sds((1, 512), F32)),
        scratch_shapes=[pltpu.VMEM((512, D), F32), pltpu.VMEM((512, D), F32)],
        compiler_params=_cp(("arbitrary",)))(o_na, u_hg, o_f, o_b, p_act, gain, w_na, w_hg, p_act, p_act, dmix)


def _wo_fwd(mix, w_o, h0, g_mlp):
    def body(mix_ref, w_ref, h0_ref, g_ref, h1_ref, m_ref):
        h1 = h0_ref[...] + _dot(mix_ref[...], w_ref[...])
        h1_ref[...] = h1
        r = lax.rsqrt(jnp.mean(h1 * h1, axis=-1, keepdims=True) + EPS)
        m_ref[...] = (h1 * r * g_ref[...]).astype(BF16)

    blk = pl.BlockSpec((TM_B, D), lambda i: (i, 0))
    return pl.pallas_call(
        body, name="wo_fwd", grid=(T // TM_B,),
        in_specs=[blk, pl.BlockSpec((D, D), lambda i: (0, 0)), blk, pl.BlockSpec((1, D), lambda i: (0, 0))],
        out_specs=(blk, blk), out_shape=(_sds((T, D), F32), _sds((T, D), BF16)),
        compiler_params=_cp(("parallel",)))(mix, w_o, h0, g_mlp)


def _wo_bwd(dh1_b, w_o, mix):
    ni = T // TM_B

    def body(dh_ref, w_ref, mix_ref, dmix_ref, dw_ref, acc):
        i = pl.program_id(0)
        dh = dh_ref[...]
        dmix_ref[...] = _dot(dh, w_ref[...], NT).astype(BF16)
        part = _dot(mix_ref[...], dh, TN)

        @pl.when(i == 0)
        def _():
            acc[...] = part

        @pl.when(i > 0)
        def _():
            acc[...] += part

        @pl.when(i == ni - 1)
        def _():
            dw_ref[...] = acc[...].astype(BF16)

    blk = pl.BlockSpec((TM_B, D), lambda i: (i, 0))
    wsp = pl.BlockSpec((D, D), lambda i: (0, 0))
    return pl.pallas_call(
        body, name="wo_bwd", grid=(ni,), in_specs=[blk, wsp, blk], out_specs=(blk, wsp),
        out_shape=(_sds((T, D), BF16), _sds((D, D), BF16)), scratch_shapes=[pltpu.VMEM((D, D), F32)],
        compiler_params=_cp(("arbitrary",)))(dh1_b, w_o, mix)


FF_B = D_FF // NDEV


def _loss_rows(xv, gv, tv, row0):
    r_io = lax.broadcasted_iota(jnp.int32, (xv.shape[0], 1), 0) + row0
    valid = (r_io >= NM) & (r_io < L)
    r = lax.rsqrt(jnp.mean(xv * xv, axis=-1, keepdims=True) + EPS)
    xh = xv * r
    err = jnp.where(valid, xh * gv - tv, 0.0)
    lpart = 0.5 * jnp.sum(jnp.sum(err * err, axis=-1, keepdims=True) * (1.0 / D), axis=0, keepdims=True)
    dy = err * (1.0 / D)
    dxh = dy * gv
    dh = r * (dxh - xh * jnp.mean(dxh * xh, axis=-1, keepdims=True))
    return lpart, dh, jnp.sum(dy * xh, axis=0, keepdims=True)


def _mlp_fwd_loss(m, wup_g, wdown_g, h1, g_final, tgt):
    nsub = TM_MM // TM_E

    def body(m_ref, wu_ref, wd_ref, h1_ref, g_ref, t_ref, loss_ref, dh_ref, dhb_ref, dg_ref, h2):
        i, j = pl.program_id(0), pl.program_id(1)
        up = jnp.maximum(_dot(m_ref[...], wu_ref[0]), 0.0)
        part = _dot((up * up).astype(BF16), wd_ref[0])

        @pl.when(j == 0)
        def _():
            h2[...] = h1_ref[...] + part

        @pl.when(j > 0)
        def _():
            h2[...] += part

        @pl.when(j == NDEV - 1)
        def _():
            lsum = jnp.zeros((1, 1), F32)
            gsum = jnp.zeros((1, D), F32)
            for s in range(nsub):
                rows = slice(s * TM_E, (s + 1) * TM_E)
                lpart, dh, gpart = _loss_rows(h2[rows, :], g_ref[...], t_ref[rows, :], i * TM_MM + s * TM_E)
                dh_ref[rows, :] = dh
                dhb_ref[rows, :] = dh.astype(BF16)
                lsum = lsum + lpart
                gsum = gsum + gpart
            lsum = jnp.broadcast_to(lsum, (1, 128))

            @pl.when(i == 0)
            def _():
                loss_ref[...] = lsum
                dg_ref[...] = gsum

            @pl.when(i > 0)
            def _():
                loss_ref[...] += lsum
                dg_ref[...] += gsum

    blk = pl.BlockSpec((TM_MM, D), lambda i, j: (i, 0))
    vec = pl.BlockSpec((1, D), lambda i, j: (0, 0))
    return pl.pallas_call(
        body, name="mlp_fwd_loss", grid=(T // TM_MM, NDEV),
        in_specs=[blk, pl.BlockSpec((1, D, FF_B), lambda i, j: (j, 0, 0)), pl.BlockSpec((1, FF_B, D), lambda i, j: (j, 0, 0)),
                  blk, vec, blk],
        out_specs=(pl.BlockSpec((1, 128), lambda i, j: (0, 0)), blk, blk, vec),
        out_shape=(_sds((1, 128), F32), _sds((T, D), F32), _sds((T, D), BF16), _sds((1, D), F32)),
        scratch_shapes=[pltpu.VMEM((TM_MM, D), F32)],
        compiler_params=_cp(("arbitrary", "arbitrary"), 56))(m, wup_g, wdown_g, h1, g_final, tgt)


def _mlp_bwd(m, dh2_b, wup_g, wdown_g, h1, g_mlp, dh2):
    ni = T // TM_B
    nsub = TM_B // TM_E

    def body(m_ref, dh_ref, wu_ref, wd_ref, h1_ref, g_ref, dres_ref, dwu_ref, dwd_ref, dh1_ref, dh1b_ref, dg_ref,
             dm_ref, acc_u, acc_d):
        j, i = pl.program_id(0), pl.program_id(1)
        rows = pl.ds(pl.multiple_of(i * TM_B, TM_B), TM_B)
        mv, dh = m_ref[...], dh_ref[...]
        r = jnp.maximum(_dot(mv, wu_ref[0]), 0.0)
        act = (r * r).astype(BF16)
        dact = _dot(dh, wd_ref[0], NT)
        dup = (dact * (2.0 * r)).astype(BF16)
        pd = _dot(act, dh, TN)
        pu = _dot(mv, dup, TN)
        dmv = _dot(dup, wu_ref[0], NT)

        @pl.when(i == 0)
        def _():
            acc_u[...] = pu
            acc_d[...] = pd

        @pl.when(i > 0)
        def _():
            acc_u[...] += pu
            acc_d[...] += pd

        @pl.when(i == ni - 1)
        def _():
            dwu_ref[0] = acc_u[...].astype(BF16)
            dwd_ref[0] = acc_d[...].astype(BF16)

        @pl.when(j == 0)
        def _():
            dm_ref[rows, :] = dmv

        @pl.when(j > 0)
        def _():
            dm_ref[rows, :] += dmv

        @pl.when(j == NDEV - 1)
        def _():
            gsum = jnp.zeros((1, D), F32)
            for s in range(nsub):
                sub = slice(s * TM_E, (s + 1) * TM_E)
                dm_rows = dm_ref[pl.ds(pl.multiple_of(i * TM_B + s * TM_E, TM_E), TM_E), :]
                dx, gpart = _norm_bwd_rows(h1_ref[sub, :], g_ref[...], dm_rows, dres_ref[sub, :])
                dh1_ref[sub, :] = dx
                dh1b_ref[sub, :] = dx.astype(BF16)
                gsum = gsum + gpart

            @pl.when(i == 0)
            def _():
                dg_ref[...] = gsum

            @pl.when(i > 0)
            def _():
                dg_ref[...] += gsum

    blk = pl.BlockSpec((TM_B, D), lambda j, i: (i, 0))
    late = pl.BlockSpec((TM_B, D), lambda j, i: (jnp.where(j == NDEV - 1, i, 0), 0))
    vec = pl.BlockSpec((1, D), lambda j, i: (0, 0))
    wus = pl.BlockSpec((1, D, FF_B), lambda j, i: (j, 0, 0))
    wds = pl.BlockSpec((1, FF_B, D), lambda j, i: (j, 0, 0))
    return pl.pallas_call(
        body, name="mlp_bwd", grid=(NDEV, ni), in_specs=[blk, blk, wus, wds, late, vec, late],
        out_specs=(wus, wds, late, late, vec),
        out_shape=(_sds((NDEV, D, FF_B), BF16), _sds((NDEV, FF_B, D), BF16), _sds((T, D), F32), _sds((T, D), BF16),
                   _sds((1, D), F32)),
        scratch_shapes=[pltpu.VMEM((T, D), F32), pltpu.VMEM((D, FF_B), F32), pltpu.VMEM((FF_B, D), F32)],
        compiler_params=_cp(("arbitrary", "arbitrary"), 56))(m, dh2_b, wup_g, wdown_g, h1, g_mlp, dh2)


def _adamw(parts, w, m, v, name):
    rr, cc = w.shape
    nslot = parts.shape[0]
    tr = rr
    for cand in (256, 128, 64):
        if rr % cand == 0 and rr > cand:
            tr = cand
            break
    c1 = 1.0 - ADAM_B1 ** ADAM_STEP
    c2 = 1.0 - ADAM_B2 ** ADAM_STEP

    def body(p_ref, w_ref, m_ref, v_ref, g_ref, d_ref, nm_ref, nv_ref):
        g = p_ref[0].astype(F32)
        for s in range(1, nslot):
            g = g + p_ref[s].astype(F32)
        mn = ADAM_B1 * m_ref[...] + (1.0 - ADAM_B1) * g
        vn = ADAM_B2 * v_ref[...] + (1.0 - ADAM_B2) * (g * g)
        g_ref[...] = g
        nm_ref[...] = mn
        nv_ref[...] = vn
        d_ref[...] = -ADAM_LR * ((mn / c1) / (jnp.sqrt(vn / c2) + ADAM_EPS) + ADAM_WD * w_ref[...])

    blk = pl.BlockSpec((tr, cc), lambda i: (i, 0))
    return pl.pallas_call(
        body, name=name, grid=(rr // tr,),
        in_specs=[pl.BlockSpec((nslot, tr, cc), lambda i: (0, i, 0)), blk, blk, blk],
        out_specs=(blk,) * 4, out_shape=(_sds((rr, cc), F32),) * 4,
        compiler_params=_cp(("parallel",)))(parts, w, m, v)


RPB_N = NA_HEADS * 15 * 31
RPB_PAD = 4096
OWN_ROWS = NM + 8


def _pad_rows(a, rows):
    return jnp.pad(a, ((0, rows - a.shape[0]),) + ((0, 0),) * (a.ndim - 1))


def _pack_owned(meta_blk, lb_blk):
    return jnp.concatenate([meta_blk, _pad_rows(lb_blk.reshape(2, 128), 8)], axis=0)


LOSS_ROW = 28


def _pack_replicated(n_mix, n_mlp, n_final, hg_gain, rpb, loss_row=None):
    flat = _pad_rows(rpb.reshape(RPB_N), RPB_PAD)
    gain8 = _pad_rows(hg_gain.reshape(4, 128), 8)
    if loss_row is not None:
        gain8 = gain8 + jnp.pad(loss_row, ((LOSS_ROW - 24, 31 - LOSS_ROW), (0, 0)))
    return jnp.concatenate([n_mix.reshape(8, 128), n_mlp.reshape(8, 128), n_final.reshape(8, 128), gain8,
                            flat.reshape(32, 128)], axis=0)


def _unpack_replicated(a):
    return (a[0:8].reshape(1, D), a[8:16].reshape(1, D), a[16:24].reshape(D), a[24:28].reshape(1, 512),
            a[32:64].reshape(RPB_PAD)[:RPB_N].reshape(1, NA_HEADS, 15, 31))


def kernel(x, meta_tokens, w_in, w_na_out, w_hg_out, w_o, w_up, w_down, norm_mix, norm_mlp, norm_final, hg_norm, na_rpb, hg_lb_logits, loss_target, m_meta_tokens, m_w_in, m_w_na_out, m_w_hg_out, m_w_o, m_w_up, m_w_down, m_norm_mix, m_norm_mlp, m_norm_final, m_hg_norm, m_na_rpb, m_hg_lb_logits, v_meta_tokens, v_w_in, v_w_na_out, v_w_hg_out, v_w_o, v_w_up, v_w_down, v_norm_mix, v_norm_mlp, v_norm_final, v_hg_norm, v_na_rpb, v_hg_lb_logits):
    owned = _pack_owned(meta_tokens, hg_lb_logits)
    first_masks = (ALL_PEERS, SAME_CORE_AND_SIBLING)
    first, tok = _exchange_start([owned, w_in[0].astype(BF16)], [False] * 2, "gather_first_start", first_masks)
    bias_tab = _na_bias_table(_tie(jnp.pad(na_rpb[0], ((0, 0), (0, 0), (0, 128 - 31))), tok, "tie_bias_table"))
    later = [w[0].astype(BF16) for w in (w_na_out, w_hg_out, w_o, w_up, w_down)]
    lead = jnp.zeros((NM, D), F32) + tok[0, 0]
    h0_rows = jnp.concatenate([lead, x[0], jnp.zeros((T - L, D), F32)], axis=0)
    tgt = jnp.concatenate([lead, loss_target[0], jnp.zeros((T - L, D), F32)], axis=0)
    (owned_g, _), first = _exchange_wait(first, [False] * 2, [h0_rows], "gather_small_wait", first_masks, which=(0,))
    meta_full = jnp.transpose(owned_g[:, 0:NM, :], (1, 0, 2)).reshape(NM, D)
    logits = jnp.transpose(owned_g[:, NM:NM + 2, :].reshape(NDEV, 2, 2, 64), (1, 2, 0, 3)).reshape(2, 2, 512)
    h0 = lax.dynamic_update_slice(h0_rows, meta_full, (0, 0))
    a, a_t = _norm_fwd_t(h0, norm_mix, "norm_mix_fwd")
    (_, win_l), _ = _exchange_wait(first, [False] * 2, [a, logits, tgt, bias_tab] + later, "gather_first_wait", first_masks,
                                   which=(1,))
    (win_g,) = _forward_to_sibling([win_l], "gather_first_forward")
    later[0] = _tie(later[0], win_g, "tie_gather_rest")
    gather_rest, tok = _exchange_start(later, [False] * 5, "gather_rest_start")
    win_g = _tie(win_g, tok, "tie_inproj")

    p_act = _inproj_fwd(a, win_g)
    o_na, lse = _na_fwd(p_act, bias_tab)
    qh, k_f, b_f, k_b, b_b = _hg_pre(p_act, logits)
    o_f, st_f = _hg_scan_fwd(qh, k_f, b_f, p_act, False)
    o_b, st_b = _hg_scan_fwd(qh, k_b, b_b, p_act, True)
    (wna_g, whg_g, wo_g, _, _), gather_rest = _exchange_wait(
        gather_rest, [False] * 5, [o_f, o_b, o_na], "gather_rest_wait_a", which=(0, 1, 2))
    w_na_full = jnp.transpose(wna_g, (1, 0, 2)).reshape(512, D)
    w_hg_full = jnp.transpose(whg_g, (1, 0, 2)).reshape(512, D)
    mix, u_hg = _mix_fwd(o_na, o_f, o_b, hg_norm, w_na_full, w_hg_full, p_act)
    h1, m_act = _wo_fwd(mix, wo_g.reshape(D, D), h0, norm_mlp)
    (_, _, wo_g, wup_g, wdown_g), _ = _exchange_wait(gather_rest, [False] * 5, [m_act], "gather_rest_wait_b", which=(3, 4))
    w_o_full = wo_g.reshape(D, D)
    loss_part, dh2, dh2_b, d_nfinal = _mlp_fwd_loss(m_act, wup_g, wdown_g, h1, norm_final.reshape(1, D), tgt)

    dwup_p, dwdown_p, dh1, dh1_b, d_nmlp = _mlp_bwd(m_act, dh2_b, wup_g, wdown_g, h1, norm_mlp, dh2)
    sc_mlp, tok = _exchange_start([dwup_p, dwdown_p], [True] * 2, "scatter_mlp_start")
    dmix, dwo = _wo_bwd(_tie(dh1_b, tok, "tie_wo_bwd"), w_o_full, mix)
    sc_wo, tok = _exchange_start([dwo.reshape(NDEV, D // NDEV, D)], [True], "scatter_wo_start")
    dp_rest, dwna, dwhg, do_na, do_hg, d_gain = _mix_bwd(
        o_na, u_hg, o_f, o_b, hg_norm, w_na_full, w_hg_full, p_act, _tie(dmix, tok, "tie_mix_bwd"))
    sc_br, tok = _exchange_start([dwna, dwhg], [True] * 2, "scatter_branch_start")
    do_hg = _tie(do_hg, tok, "tie_hg_scan_bwd")
    dq_f, dk_f, db_f, dv_f = _hg_scan_bwd(qh, k_f, b_f, p_act, st_f, do_hg, False)
    dq_b, dk_b, db_b, dv_b = _hg_scan_bwd(qh, k_b, b_b, p_act, st_b, do_hg, True)
    dp_rest, d_logits = _hg_pre_bwd(p_act, logits, dq_f, dq_b, dk_f, dk_b, db_f, db_b, dv_f, dv_b, dp_rest)
    dp_na, dbias = _na_bwd(p_act, do_na, lse, bias_tab)
    dwin_p = _inproj_bwd_dw(a_t, dp_na, dp_rest)
    far_mine, far_other = _far_slots()
    dwin_p = _add_into_slot(dwin_p, _sibling_swap_far(dwin_p, "pair_swap_in"), far_mine, "pair_add_in")
    sc_in, tok = _exchange_start([dwin_p], [True], "scatter_in_start", ALL_BUT_FAR_OTHER_CORE, absent=far_other)
    dh0, d_nmix, grad_x = _inproj_bwd_da(_tie(dp_na, tok, "tie_inproj_bwd_da"), dp_rest, win_g, h0, norm_mix, dh1)
    d_rpb = _na_rpb_reduce(_tie(dbias, tok, "tie_rpb_reduce"))[:, :, :31]

    res = {}

    def update(nm, parts, w, mm, vv):
        res[nm] = [r[None] for r in _adamw(parts, w[0], mm[0], vv[0], "adamw_" + nm)]
        return res[nm][1]

    wup_r, wdown_r = _exchange_wait(sc_mlp, [True] * 2, [dh0, d_rpb], "scatter_mlp_wait")
    update("w_up", wup_r, w_up, m_w_up, v_w_up)
    last = update("w_down", wdown_r, w_down, m_w_down, v_w_down)
    (wo_r,) = _exchange_wait(sc_wo, [True], [last], "scatter_wo_wait")
    last = update("w_o", wo_r, w_o, m_w_o, v_w_o)
    wna_r, whg_r = _exchange_wait(sc_br, [True] * 2, [last], "scatter_branch_wait")
    update("w_na_out", wna_r, w_na_out, m_w_na_out, v_w_na_out)
    last = update("w_hg_out", whg_r, w_hg_out, m_w_hg_out, v_w_hg_out)

    d_meta = jnp.transpose(dh0[0:NM].reshape(NM, NDEV, 128), (1, 0, 2))
    d_lg = jnp.transpose(d_logits.reshape(2, 2, NDEV, 64), (2, 0, 1, 3)).reshape(NDEV, 2, 128)
    owned_p = jnp.concatenate([d_meta, jnp.pad(d_lg, ((0, 0), (0, OWN_ROWS - NM - 2), (0, 0)))], axis=1)
    repl_p = _pack_replicated(d_nmix, d_nmlp, d_nfinal, d_gain, d_rpb, loss_part)
    grad_x = grad_x[None]
    done_first = [grad_x] + [res[nm][0] for nm in ("w_up", "w_down", "w_o", "w_na_out", "w_hg_out")]
    owned_r, repl_r = _exchange([owned_p, repl_p], [True, False], "scatter_small", done_first)
    own = _adamw(owned_r, owned, _pack_owned(m_meta_tokens, m_hg_lb_logits), _pack_owned(v_meta_tokens, v_hg_lb_logits),
                 "adamw_owned_small")
    res["meta_tokens"] = [r[0:NM] for r in own]
    res["hg_lb_logits"] = [r[NM:NM + 2].reshape(2, 2, 64) for r in own]
    rep = _adamw(repl_r, _pack_replicated(norm_mix, norm_mlp, norm_final, hg_norm, na_rpb),
                 _pack_replicated(m_norm_mix, m_norm_mlp, m_norm_final, m_hg_norm, m_na_rpb),
                 _pack_replicated(v_norm_mix, v_norm_mlp, v_norm_final, v_hg_norm, v_na_rpb), "adamw_replicated")
    for q in range(4):
        um = _unpack_replicated(rep[q])
        for nm, val in zip(("norm_mix", "norm_mlp", "norm_final", "hg_norm", "na_rpb"), um):
            res.setdefault(nm, [None] * 4)[q] = val
    (win_r,) = _exchange_wait(sc_in, [True], [rep[1], own[1]], "scatter_in_wait", ALL_BUT_FAR_OTHER_CORE)
    update("w_in", win_r, w_in, m_w_in, v_w_in)

    loss = jnp.sum(repl_r[:, LOSS_ROW, 0])
    order = ("meta_tokens", "w_in", "w_na_out", "w_hg_out", "w_o", "w_up", "w_down", "norm_mix", "norm_mlp", "norm_final",
             "hg_norm", "na_rpb", "hg_lb_logits")
    outs = [loss, grad_x]
    for q in range(4):
        outs += [res[nm][q] for nm in order]
    return tuple(outs)
```
